```python
import jax, jax.numpy as jnp
from jax import lax
import numpy as np

D_MODEL = 1024
BATCH = 32
SEQ = 2048
DEPTH = 1

HEAD_DIM = 64
N_Q_HEADS = 8
N_KV_HEADS = 2
WINDOW = 128
ATT_BLOCK = 128
N_BUCKETS = 32
MAX_DISTANCE = 128
REC_HEADS = 4
REC_KEY_DIM = 128
REC_VAL_DIM = 128
REC_CHUNK = 64
D_FF = 2816
PLE_DIM = 256
EPS = 1e-6

ATT_Q_W = N_Q_HEADS * HEAD_DIM
ATT_KV_W = N_KV_HEADS * HEAD_DIM
REC_K_W = REC_HEADS * REC_KEY_DIM
REC_V_W = REC_HEADS * REC_VAL_DIM
IN_W = ATT_Q_W + 2 * ATT_KV_W + 2 * REC_K_W + 2 * REC_V_W + 2 * D_MODEL

kernel_name = "hybrid_swa_hgrn2_macaron_block"


def _split_points():
    widths = [ATT_Q_W, ATT_KV_W, ATT_KV_W, REC_K_W, REC_K_W, REC_V_W, REC_V_W, D_MODEL]
    return [int(v) for v in np.cumsum(widths)]


def rms_norm(x, g):
    xf = x.astype(jnp.float32)
    y = xf * lax.rsqrt(jnp.mean(xf * xf, axis=-1, keepdims=True) + EPS)
    return (y * g.astype(jnp.float32)).astype(x.dtype)


def swiglu(x, w_in, w_out):
    gate, up = jnp.split(x @ w_in, 2, axis=-1)
    return (jax.nn.silu(gate) * up) @ w_out


def t5_band_buckets():
    qi = np.arange(ATT_BLOCK)[:, None] + ATT_BLOCK
    kj = np.arange(2 * ATT_BLOCK)[None, :]
    dist = qi - kj
    n = np.maximum(dist, 0)
    max_exact = N_BUCKETS // 2
    large = max_exact + (np.log(np.maximum(n, 1) / max_exact)
                         / np.log(MAX_DISTANCE / max_exact)
                         * (N_BUCKETS - max_exact)).astype(np.int32)
    large = np.minimum(large, N_BUCKETS - 1)
    bucket = np.where(n < max_exact, n, large).astype(np.int32)
    return bucket, dist.astype(np.int32)


def sliding_window_attention(q, k, v, rel_table, sinks):
    B, S = q.shape[0], q.shape[1]
    nb = S // ATT_BLOCK
    G = N_Q_HEADS // N_KV_HEADS
    qb = q.reshape(B, nb, ATT_BLOCK, N_KV_HEADS, G, HEAD_DIM)

    def band(t):
        t = t.reshape(B, nb, ATT_BLOCK, N_KV_HEADS, HEAD_DIM)
        prev = jnp.pad(t, ((0, 0), (1, 0), (0, 0), (0, 0), (0, 0)))[:, :-1]
        return jnp.concatenate([prev, t], axis=2)

    kb, vb = band(k), band(v)
    s = jnp.einsum('bnqhgd,bnkhd->bnhgqk', qb, kb).astype(jnp.float32) * (HEAD_DIM ** -0.5)
    bucket, dist = t5_band_buckets()
    bias = rel_table.astype(jnp.float32)[bucket]
    bias = bias.transpose(2, 0, 1).reshape(N_KV_HEADS, G, ATT_BLOCK, 2 * ATT_BLOCK)
    key_pos = (jnp.arange(nb)[:, None, None] * ATT_BLOCK - ATT_BLOCK
               + jnp.arange(2 * ATT_BLOCK)[None, None, :])
    dist_j = jnp.asarray(dist)[None]
    valid = (dist_j >= 0) & (dist_j < WINDOW) & (key_pos >= 0)
    s = jnp.where(valid[None, :, None, None], s + bias, -jnp.inf)
    sink = sinks.astype(jnp.float32).reshape(N_KV_HEADS, G, 1, 1)
    m = jnp.maximum(jnp.max(s, axis=-1, keepdims=True), sink)
    e = jnp.exp(s - m)
    probs = e / (jnp.sum(e, axis=-1, keepdims=True) + jnp.exp(sink - m))
    o = jnp.einsum('bnhgqk,bnkhd->bnqhgd', probs.astype(v.dtype), vb)
    return o.reshape(B, S, ATT_Q_W)


def hgrn2_recurrence(q, f_logit, i, lb):
    B, S = q.shape[0], q.shape[1]
    nc = S // REC_CHUNK
    lbf = lb.astype(jnp.float32).reshape(REC_HEADS, REC_KEY_DIM)
    f = lbf + (1.0 - lbf) * jax.nn.sigmoid(f_logit.astype(jnp.float32))
    log_f = jnp.log(f)
    k = 1.0 - f

    def to_chunks(t):
        return t.reshape(B, nc, REC_CHUNK, REC_HEADS, t.shape[-1]).transpose(1, 0, 3, 2, 4)

    qc = to_chunks(q.astype(jnp.float32))
    kc = to_chunks(k)
    vc = to_chunks(i.astype(jnp.float32))
    gc = to_chunks(log_f)
    causal = jnp.tril(jnp.ones((REC_CHUNK, REC_CHUNK), dtype=bool))[:, :, None]

    def step(state, inp):
        qt, kt, vt, gt = inp
        b = jnp.cumsum(gt, axis=2)
        diff = b[:, :, :, None, :] - b[:, :, None, :, :]
        decay = jnp.exp(jnp.where(causal, diff, -jnp.inf))
        attn = jnp.einsum('bhtd,bhsd,bhtsd->bhts', qt, kt, decay)
        o = (jnp.einsum('bhts,bhsv->bhtv', attn, vt)
             + jnp.einsum('bhtd,bhdv->bhtv', qt * jnp.exp(b), state))
        b_last = b[:, :, -1:, :]
        new_state = (jnp.exp(b_last[:, :, 0, :])[..., None] * state
                     + jnp.einsum('bhsd,bhsv->bhdv', kt * jnp.exp(b_last - b), vt))
        return new_state, o

    s0 = jnp.zeros((B, REC_HEADS, REC_KEY_DIM, REC_VAL_DIM), jnp.float32)
    _, o = lax.scan(step, s0, (qc, kc, vc, gc))
    return o.transpose(1, 0, 3, 2, 4).reshape(B, S, REC_HEADS, REC_VAL_DIM).astype(q.dtype)


def _fwd_setup_inputs(seed: int = 0) -> dict:
    key = jax.random.key(seed)
    ks = jax.random.split(key, 24)
    f32 = jnp.float32

    def nrm(k, shape, scale):
        return jax.random.normal(k, shape, f32) * scale

    def gain(k, shape):
        return 1.0 + 0.05 * jax.random.normal(k, shape, f32)

    return {
        "x": nrm(ks[0], (BATCH, SEQ, D_MODEL), 1.0),
        "p": nrm(ks[1], (DEPTH, BATCH, SEQ, PLE_DIM), 1.0),
        "rel_bias": nrm(ks[2], (N_BUCKETS, N_Q_HEADS), 0.5),
        "lb_param": nrm(ks[3], (DEPTH + 1, REC_K_W), 1.0),
        "norm_ffn1": gain(ks[4], (DEPTH, D_MODEL)),
        "w_ffn1_in": nrm(ks[5], (DEPTH, D_MODEL, 2 * D_FF), D_MODEL ** -0.5),
        "w_ffn1_out": nrm(ks[6], (DEPTH, D_FF, D_MODEL), D_FF ** -0.5),
        "norm_mix": gain(ks[7], (DEPTH, D_MODEL)),
        "w_in": nrm(ks[8], (DEPTH, D_MODEL, IN_W), D_MODEL ** -0.5),
        "attn_sinks": nrm(ks[9], (DEPTH, N_Q_HEADS), 1.0),
        "rec_norm": gain(ks[10], (DEPTH, REC_VAL_DIM)),
        "w_att_proj": nrm(ks[11], (DEPTH, ATT_Q_W, D_MODEL), ATT_Q_W ** -0.5),
        "w_rec_proj": nrm(ks[12], (DEPTH, REC_V_W, D_MODEL), REC_V_W ** -0.5),
        "w_out": nrm(ks[13], (DEPTH, D_MODEL, D_MODEL), D_MODEL ** -0.5),
        "norm_ffn2": gain(ks[14], (DEPTH, D_MODEL)),
        "w_ffn2_in": nrm(ks[15], (DEPTH, D_MODEL, 2 * D_FF), D_MODEL ** -0.5),
        "w_ffn2_out": nrm(ks[16], (DEPTH, D_FF, D_MODEL), D_FF ** -0.5),
        "norm_ple": gain(ks[17], (DEPTH, D_MODEL)),
        "w_ple_gate": nrm(ks[18], (DEPTH, D_MODEL, D_MODEL), D_MODEL ** -0.5),
        "w_ple_proj": nrm(ks[19], (DEPTH, PLE_DIM, D_MODEL), PLE_DIM ** -0.5),
        "norm_final": gain(ks[20], (D_MODEL,)),
    }


def _fwd_reference(x, p, rel_bias, lb_param, norm_ffn1, w_ffn1_in, w_ffn1_out, norm_mix, w_in,
              attn_sinks, rec_norm, w_att_proj, w_rec_proj, w_out, norm_ffn2, w_ffn2_in,
              w_ffn2_out, norm_ple, w_ple_gate, w_ple_proj, norm_final):
    B, S = x.shape[0], x.shape[1]
    lower_bounds = jnp.cumsum(jax.nn.softmax(lb_param.astype(jnp.float32), axis=0), axis=0)
    splits = _split_points()
    h = x
    for layer in range(DEPTH):
        h = h + 0.5 * swiglu(rms_norm(h, norm_ffn1[layer]), w_ffn1_in[layer], w_ffn1_out[layer])

        u = rms_norm(h, norm_mix[layer])
        proj = u @ w_in[layer]
        aq, ak, av, rq, rf, ri, rg, ga, gb = jnp.split(proj, splits, axis=-1)

        att = sliding_window_attention(
            aq.reshape(B, S, N_Q_HEADS, HEAD_DIM),
            ak.reshape(B, S, N_KV_HEADS, HEAD_DIM),
            av.reshape(B, S, N_KV_HEADS, HEAD_DIM),
            rel_bias, attn_sinks[layer])

        rec = hgrn2_recurrence(
            rq.reshape(B, S, REC_HEADS, REC_KEY_DIM),
            rf.reshape(B, S, REC_HEADS, REC_KEY_DIM),
            ri.reshape(B, S, REC_HEADS, REC_VAL_DIM),
            lower_bounds[layer])
        rec = rms_norm(rec, rec_norm[layer]).reshape(B, S, REC_V_W) * jax.nn.sigmoid(rg)

        y_a = att @ w_att_proj[layer]
        y_b = rec @ w_rec_proj[layer]
        merged = jax.nn.sigmoid(ga) * y_a + jax.nn.sigmoid(gb) * y_b
        h = h + merged @ w_out[layer]

        h = h + 0.5 * swiglu(rms_norm(h, norm_ffn2[layer]), w_ffn2_in[layer], w_ffn2_out[layer])

        gate = jax.nn.sigmoid(rms_norm(h, norm_ple[layer]) @ w_ple_gate[layer])
        h = h + gate * (p[layer] @ w_ple_proj[layer])
    return rms_norm(h, norm_final)


import jax as _jax
import jax.numpy as _jnp

TWIN_FORMAT = 'train_step'
FWD_PARAMS = ['x', 'p', 'rel_bias', 'lb_param', 'norm_ffn1', 'w_ffn1_in', 'w_ffn1_out', 'norm_mix', 'w_in', 'attn_sinks', 'rec_norm', 'w_att_proj', 'w_rec_proj', 'w_out', 'norm_ffn2', 'w_ffn2_in', 'w_ffn2_out', 'norm_ple', 'w_ple_gate', 'w_ple_proj', 'norm_final']
TWIN_WEIGHTS = ['rel_bias', 'lb_param', 'norm_ffn1', 'w_ffn1_in', 'w_ffn1_out', 'norm_mix', 'w_in', 'attn_sinks', 'rec_norm', 'w_att_proj', 'w_rec_proj', 'w_out', 'norm_ffn2', 'w_ffn2_in', 'w_ffn2_out', 'norm_ple', 'w_ple_gate', 'w_ple_proj', 'norm_final']
TWIN_DIFF_INPUT = 'x'
TWIN_INPUTS = ['x', 'p', 'rel_bias', 'lb_param', 'norm_ffn1', 'w_ffn1_in', 'w_ffn1_out', 'norm_mix', 'w_in', 'attn_sinks', 'rec_norm', 'w_att_proj', 'w_rec_proj', 'w_out', 'norm_ffn2', 'w_ffn2_in', 'w_ffn2_out', 'norm_ple', 'w_ple_gate', 'w_ple_proj', 'norm_final', 'loss_target', 'm_rel_bias', 'm_lb_param', 'm_norm_ffn1', 'm_w_ffn1_in', 'm_w_ffn1_out', 'm_norm_mix', 'm_w_in', 'm_attn_sinks', 'm_rec_norm', 'm_w_att_proj', 'm_w_rec_proj', 'm_w_out', 'm_norm_ffn2', 'm_w_ffn2_in', 'm_w_ffn2_out', 'm_norm_ple', 'm_w_ple_gate', 'm_w_ple_proj', 'm_norm_final', 'v_rel_bias', 'v_lb_param', 'v_norm_ffn1', 'v_w_ffn1_in', 'v_w_ffn1_out', 'v_norm_mix', 'v_w_in', 'v_attn_sinks', 'v_rec_norm', 'v_w_att_proj', 'v_w_rec_proj', 'v_w_out', 'v_norm_ffn2', 'v_w_ffn2_in', 'v_w_ffn2_out', 'v_norm_ple', 'v_w_ple_gate', 'v_w_ple_proj', 'v_norm_final']
TWIN_OUTPUTS = ['loss', 'grad_x', 'grad_rel_bias', 'grad_lb_param', 'grad_norm_ffn1', 'grad_w_ffn1_in', 'grad_w_ffn1_out', 'grad_norm_mix', 'grad_w_in', 'grad_attn_sinks', 'grad_rec_norm', 'grad_w_att_proj', 'grad_w_rec_proj', 'grad_w_out', 'grad_norm_ffn2', 'grad_w_ffn2_in', 'grad_w_ffn2_out', 'grad_norm_ple', 'grad_w_ple_gate', 'grad_w_ple_proj', 'grad_norm_final', 'delta_rel_bias', 'delta_lb_param', 'delta_norm_ffn1', 'delta_w_ffn1_in', 'delta_w_ffn1_out', 'delta_norm_mix', 'delta_w_in', 'delta_attn_sinks', 'delta_rec_norm', 'delta_w_att_proj', 'delta_w_rec_proj', 'delta_w_out', 'delta_norm_ffn2', 'delta_w_ffn2_in', 'delta_w_ffn2_out', 'delta_norm_ple', 'delta_w_ple_gate', 'delta_w_ple_proj', 'delta_norm_final', 'new_m_rel_bias', 'new_m_lb_param', 'new_m_norm_ffn1', 'new_m_w_ffn1_in', 'new_m_w_ffn1_out', 'new_m_norm_mix', 'new_m_w_in', 'new_m_attn_sinks', 'new_m_rec_norm', 'new_m_w_att_proj', 'new_m_w_rec_proj', 'new_m_w_out', 'new_m_norm_ffn2', 'new_m_w_ffn2_in', 'new_m_w_ffn2_out', 'new_m_norm_ple', 'new_m_w_ple_gate', 'new_m_w_ple_proj', 'new_m_norm_final', 'new_v_rel_bias', 'new_v_lb_param', 'new_v_norm_ffn1', 'new_v_w_ffn1_in', 'new_v_w_ffn1_out', 'new_v_norm_mix', 'new_v_w_in', 'new_v_attn_sinks', 'new_v_rec_norm', 'new_v_w_att_proj', 'new_v_w_rec_proj', 'new_v_w_out', 'new_v_norm_ffn2', 'new_v_w_ffn2_in', 'new_v_w_ffn2_out', 'new_v_norm_ple', 'new_v_w_ple_gate', 'new_v_w_ple_proj', 'new_v_norm_final']
TWIN_LEAF_KINDS = {'loss': 'loss', 'grad_x': 'grad_x', 'grad_rel_bias': 'grad_w', 'grad_lb_param': 'grad_w', 'grad_norm_ffn1': 'grad_w', 'grad_w_ffn1_in': 'grad_w', 'grad_w_ffn1_out': 'grad_w', 'grad_norm_mix': 'grad_w', 'grad_w_in': 'grad_w', 'grad_attn_sinks': 'grad_w', 'grad_rec_norm': 'grad_w', 'grad_w_att_proj': 'grad_w', 'grad_w_rec_proj': 'grad_w', 'grad_w_out': 'grad_w', 'grad_norm_ffn2': 'grad_w', 'grad_w_ffn2_in': 'grad_w', 'grad_w_ffn2_out': 'grad_w', 'grad_norm_ple': 'grad_w', 'grad_w_ple_gate': 'grad_w', 'grad_w_ple_proj': 'grad_w', 'grad_norm_final': 'grad_w', 'delta_rel_bias': 'delta_w', 'delta_lb_param': 'delta_w', 'delta_norm_ffn1': 'delta_w', 'delta_w_ffn1_in': 'delta_w', 'delta_w_ffn1_out': 'delta_w', 'delta_norm_mix': 'delta_w', 'delta_w_in': 'delta_w', 'delta_attn_sinks': 'delta_w', 'delta_rec_norm': 'delta_w', 'delta_w_att_proj': 'delta_w', 'delta_w_rec_proj': 'delta_w', 'delta_w_out': 'delta_w', 'delta_norm_ffn2': 'delta_w', 'delta_w_ffn2_in': 'delta_w', 'delta_w_ffn2_out': 'delta_w', 'delta_norm_ple': 'delta_w', 'delta_w_ple_gate': 'delta_w', 'delta_w_ple_proj': 'delta_w', 'delta_norm_final': 'delta_w', 'new_m_rel_bias': 'new_m', 'new_m_lb_param': 'new_m', 'new_m_norm_ffn1': 'new_m', 'new_m_w_ffn1_in': 'new_m', 'new_m_w_ffn1_out': 'new_m', 'new_m_norm_mix': 'new_m', 'new_m_w_in': 'new_m', 'new_m_attn_sinks': 'new_m', 'new_m_rec_norm': 'new_m', 'new_m_w_att_proj': 'new_m', 'new_m_w_rec_proj': 'new_m', 'new_m_w_out': 'new_m', 'new_m_norm_ffn2': 'new_m', 'new_m_w_ffn2_in': 'new_m', 'new_m_w_ffn2_out': 'new_m', 'new_m_norm_ple': 'new_m', 'new_m_w_ple_gate': 'new_m', 'new_m_w_ple_proj': 'new_m', 'new_m_norm_final': 'new_m', 'new_v_rel_bias': 'new_v', 'new_v_lb_param': 'new_v', 'new_v_norm_ffn1': 'new_v', 'new_v_w_ffn1_in': 'new_v', 'new_v_w_ffn1_out': 'new_v', 'new_v_norm_mix': 'new_v', 'new_v_w_in': 'new_v', 'new_v_attn_sinks': 'new_v', 'new_v_rec_norm': 'new_v', 'new_v_w_att_proj': 'new_v', 'new_v_w_rec_proj': 'new_v', 'new_v_w_out': 'new_v', 'new_v_norm_ffn2': 'new_v', 'new_v_w_ffn2_in': 'new_v', 'new_v_w_ffn2_out': 'new_v', 'new_v_norm_ple': 'new_v', 'new_v_w_ple_gate': 'new_v', 'new_v_w_ple_proj': 'new_v', 'new_v_norm_final': 'new_v'}


def _forward(args):
    return _fwd_reference(*[args[k] for k in FWD_PARAMS])


def _output_shape():
    out = _jax.eval_shape(lambda: _forward(_fwd_setup_inputs(0)))
    return out.shape, out.dtype

N_MICROBATCH = 1
ADAM_LR = 0.001
ADAM_B1 = 0.9
ADAM_B2 = 0.999
ADAM_EPS = 1e-08
ADAM_WD = 0.01
ADAM_STEP = 10
PER_EXAMPLE_BATCH_AXIS = {'x': 0, 'p': 1, 'loss_target': 0}
SHARED_INPUTS = []
_WEIGHT_DTYPES = {'rel_bias': _jnp.float32, 'lb_param': _jnp.float32, 'norm_ffn1': _jnp.float32, 'w_ffn1_in': _jnp.float32, 'w_ffn1_out': _jnp.float32, 'norm_mix': _jnp.float32, 'w_in': _jnp.float32, 'attn_sinks': _jnp.float32, 'rec_norm': _jnp.float32, 'w_att_proj': _jnp.float32, 'w_rec_proj': _jnp.float32, 'w_out': _jnp.float32, 'norm_ffn2': _jnp.float32, 'w_ffn2_in': _jnp.float32, 'w_ffn2_out': _jnp.float32, 'norm_ple': _jnp.float32, 'w_ple_gate': _jnp.float32, 'w_ple_proj': _jnp.float32, 'norm_final': _jnp.float32}
MOMENT_SCALE = {'rel_bias': 4.248077e-02, 'lb_param': 4.818690e-02, 'norm_ffn1': 1.093324e-01, 'w_ffn1_in': 4.637425e-02, 'w_ffn1_out': 7.564950e-02, 'norm_mix': 1.273431e-01, 'w_in': 6.075906e-02, 'attn_sinks': 1.595972e-02, 'rec_norm': 1.751638e-01, 'w_att_proj': 2.606930e-02, 'w_rec_proj': 6.568173e-02, 'w_out': 6.920338e-02, 'norm_ffn2': 9.187150e-02, 'w_ffn2_in': 3.779915e-02, 'w_ffn2_out': 6.193814e-02, 'norm_ple': 4.636579e-02, 'w_ple_gate': 4.310839e-02, 'w_ple_proj': 1.115853e-01, 'norm_final': 6.416126e+01}


def _to_microbatches(a, axis):
    t = _jnp.moveaxis(a, axis, 0)
    t = t.reshape((N_MICROBATCH, t.shape[0] // N_MICROBATCH) + t.shape[1:])
    return _jnp.moveaxis(t, 1, axis + 1)


def setup_inputs(seed: int = 0) -> dict:
    inp = _fwd_setup_inputs(seed)
    key = _jax.random.fold_in(_jax.random.key(seed), 7919)
    shape, _ = _output_shape()
    out = dict(inp)
    out["loss_target"] = _jax.random.normal(_jax.random.fold_in(key, 0), shape, _jnp.float32)
    for i, name in enumerate(TWIN_WEIGHTS):
        w = inp[name].astype(_jnp.float32)
        if MOMENT_SCALE is None:
            s = _jnp.sqrt(_jnp.mean(_jnp.square(w)) + 1e-30)
        else:
            s = MOMENT_SCALE[name]
        km, kv = _jax.random.split(_jax.random.fold_in(key, i + 1))
        out[name] = w
        out["m_" + name] = s * _jax.random.normal(km, w.shape, _jnp.float32)
        out["v_" + name] = (s * s) * _jax.random.uniform(kv, w.shape, _jnp.float32, 0.5, 1.5)
    if N_MICROBATCH > 1:
        for name, axis in PER_EXAMPLE_BATCH_AXIS.items():
            out[name] = _to_microbatches(out[name], axis)
    return {'x': out['x'], 'p': out['p'], 'rel_bias': out['rel_bias'], 'lb_param': out['lb_param'], 'norm_ffn1': out['norm_ffn1'], 'w_ffn1_in': out['w_ffn1_in'], 'w_ffn1_out': out['w_ffn1_out'], 'norm_mix': out['norm_mix'], 'w_in': out['w_in'], 'attn_sinks': out['attn_sinks'], 'rec_norm': out['rec_norm'], 'w_att_proj': out['w_att_proj'], 'w_rec_proj': out['w_rec_proj'], 'w_out': out['w_out'], 'norm_ffn2': out['norm_ffn2'], 'w_ffn2_in': out['w_ffn2_in'], 'w_ffn2_out': out['w_ffn2_out'], 'norm_ple': out['norm_ple'], 'w_ple_gate': out['w_ple_gate'], 'w_ple_proj': out['w_ple_proj'], 'norm_final': out['norm_final'], 'loss_target': out['loss_target'], 'm_rel_bias': out['m_rel_bias'], 'm_lb_param': out['m_lb_param'], 'm_norm_ffn1': out['m_norm_ffn1'], 'm_w_ffn1_in': out['m_w_ffn1_in'], 'm_w_ffn1_out': out['m_w_ffn1_out'], 'm_norm_mix': out['m_norm_mix'], 'm_w_in': out['m_w_in'], 'm_attn_sinks': out['m_attn_sinks'], 'm_rec_norm': out['m_rec_norm'], 'm_w_att_proj': out['m_w_att_proj'], 'm_w_rec_proj': out['m_w_rec_proj'], 'm_w_out': out['m_w_out'], 'm_norm_ffn2': out['m_norm_ffn2'], 'm_w_ffn2_in': out['m_w_ffn2_in'], 'm_w_ffn2_out': out['m_w_ffn2_out'], 'm_norm_ple': out['m_norm_ple'], 'm_w_ple_gate': out['m_w_ple_gate'], 'm_w_ple_proj': out['m_w_ple_proj'], 'm_norm_final': out['m_norm_final'], 'v_rel_bias': out['v_rel_bias'], 'v_lb_param': out['v_lb_param'], 'v_norm_ffn1': out['v_norm_ffn1'], 'v_w_ffn1_in': out['v_w_ffn1_in'], 'v_w_ffn1_out': out['v_w_ffn1_out'], 'v_norm_mix': out['v_norm_mix'], 'v_w_in': out['v_w_in'], 'v_attn_sinks': out['v_attn_sinks'], 'v_rec_norm': out['v_rec_norm'], 'v_w_att_proj': out['v_w_att_proj'], 'v_w_rec_proj': out['v_w_rec_proj'], 'v_w_out': out['v_w_out'], 'v_norm_ffn2': out['v_norm_ffn2'], 'v_w_ffn2_in': out['v_w_ffn2_in'], 'v_w_ffn2_out': out['v_w_ffn2_out'], 'v_norm_ple': out['v_norm_ple'], 'v_w_ple_gate': out['v_w_ple_gate'], 'v_w_ple_proj': out['v_w_ple_proj'], 'v_norm_final': out['v_norm_final']}


def _loss(weights, diff, rest, loss_target):
    with _jax.named_scope("forward"):
        args = {**rest, TWIN_DIFF_INPUT: diff, **{k: w.astype(_WEIGHT_DTYPES[k]) for k, w in weights.items()}}
        y = _forward(args)
    with _jax.named_scope("loss_head"):
        err = _jnp.square(y.astype(_jnp.float32) - loss_target)
        return 0.5 * _jnp.sum(_jnp.mean(err, axis=-1)) if err.ndim else 0.5 * err


def _adamw(w, g, m, v):
    m = ADAM_B1 * m + (1.0 - ADAM_B1) * g
    v = ADAM_B2 * v + (1.0 - ADAM_B2) * _jnp.square(g)
    m_hat = m / (1.0 - ADAM_B1 ** ADAM_STEP)
    v_hat = v / (1.0 - ADAM_B2 ** ADAM_STEP)
    delta = -ADAM_LR * (m_hat / (_jnp.sqrt(v_hat) + ADAM_EPS) + ADAM_WD * w)
    return delta, m, v


def reference(x, p, rel_bias, lb_param, norm_ffn1, w_ffn1_in, w_ffn1_out, norm_mix, w_in, attn_sinks, rec_norm, w_att_proj, w_rec_proj, w_out, norm_ffn2, w_ffn2_in, w_ffn2_out, norm_ple, w_ple_gate, w_ple_proj, norm_final, loss_target, m_rel_bias, m_lb_param, m_norm_ffn1, m_w_ffn1_in, m_w_ffn1_out, m_norm_mix, m_w_in, m_attn_sinks, m_rec_norm, m_w_att_proj, m_w_rec_proj, m_w_out, m_norm_ffn2, m_w_ffn2_in, m_w_ffn2_out, m_norm_ple, m_w_ple_gate, m_w_ple_proj, m_norm_final, v_rel_bias, v_lb_param, v_norm_ffn1, v_w_ffn1_in, v_w_ffn1_out, v_norm_mix, v_w_in, v_attn_sinks, v_rec_norm, v_w_att_proj, v_w_rec_proj, v_w_out, v_norm_ffn2, v_w_ffn2_in, v_w_ffn2_out, v_norm_ple, v_w_ple_gate, v_w_ple_proj, v_norm_final):
    given = dict(x=x, p=p, rel_bias=rel_bias, lb_param=lb_param, norm_ffn1=norm_ffn1, w_ffn1_in=w_ffn1_in, w_ffn1_out=w_ffn1_out, norm_mix=norm_mix, w_in=w_in, attn_sinks=attn_sinks, rec_norm=rec_norm, w_att_proj=w_att_proj, w_rec_proj=w_rec_proj, w_out=w_out, norm_ffn2=norm_ffn2, w_ffn2_in=w_ffn2_in, w_ffn2_out=w_ffn2_out, norm_ple=norm_ple, w_ple_gate=w_ple_gate, w_ple_proj=w_ple_proj, norm_final=norm_final, loss_target=loss_target, m_rel_bias=m_rel_bias, m_lb_param=m_lb_param, m_norm_ffn1=m_norm_ffn1, m_w_ffn1_in=m_w_ffn1_in, m_w_ffn1_out=m_w_ffn1_out, m_norm_mix=m_norm_mix, m_w_in=m_w_in, m_attn_sinks=m_attn_sinks, m_rec_norm=m_rec_norm, m_w_att_proj=m_w_att_proj, m_w_rec_proj=m_w_rec_proj, m_w_out=m_w_out, m_norm_ffn2=m_norm_ffn2, m_w_ffn2_in=m_w_ffn2_in, m_w_ffn2_out=m_w_ffn2_out, m_norm_ple=m_norm_ple, m_w_ple_gate=m_w_ple_gate, m_w_ple_proj=m_w_ple_proj, m_norm_final=m_norm_final, v_rel_bias=v_rel_bias, v_lb_param=v_lb_param, v_norm_ffn1=v_norm_ffn1, v_w_ffn1_in=v_w_ffn1_in, v_w_ffn1_out=v_w_ffn1_out, v_norm_mix=v_norm_mix, v_w_in=v_w_in, v_attn_sinks=v_attn_sinks, v_rec_norm=v_rec_norm, v_w_att_proj=v_w_att_proj, v_w_rec_proj=v_w_rec_proj, v_w_out=v_w_out, v_norm_ffn2=v_norm_ffn2, v_w_ffn2_in=v_w_ffn2_in, v_w_ffn2_out=v_w_ffn2_out, v_norm_ple=v_norm_ple, v_w_ple_gate=v_w_ple_gate, v_w_ple_proj=v_w_ple_proj, v_norm_final=v_norm_final)
    weights = {n: given[n] for n in TWIN_WEIGHTS}
    shared = {n: given[n] for n in SHARED_INPUTS}
    per_example = {n: given[n] for n in ['x', 'p']}
    grad_fn = _jax.value_and_grad(_loss, argnums=(0, 1))

    def one_microbatch(ex, loss_target):
        ex = dict(ex)
        diff = ex.pop(TWIN_DIFF_INPUT)
        return grad_fn(weights, diff, {**shared, **ex}, loss_target)

    if N_MICROBATCH == 1:
        loss, (grad_w, grad_x) = one_microbatch(per_example, given["loss_target"])
    else:
        def body(carry, xs):
            loss_sum, grad_sum = carry
            l_k, (gw_k, gx_k) = one_microbatch(xs[0], xs[1])
            with _jax.named_scope("update"):
                return (loss_sum + l_k, _jax.tree.map(_jnp.add, grad_sum, gw_k)), gx_k

        init = (_jnp.zeros((), _jnp.float32), _jax.tree.map(_jnp.zeros_like, weights))
        (loss, grad_w), grad_x = _jax.lax.scan(body, init, (per_example, given["loss_target"]))
    with _jax.named_scope("update"):
        delta_w, new_m, new_v = {}, {}, {}
        for n in TWIN_WEIGHTS:
            delta_w[n], new_m[n], new_v[n] = _adamw(weights[n], grad_w[n], given["m_" + n], given["v_" + n])
    return (loss, grad_x, *[grad_w[n] for n in TWIN_WEIGHTS], *[delta_w[n] for n in TWIN_WEIGHTS],
            *[new_m[n] for n in TWIN_WEIGHTS], *[new_v[n] for n in TWIN_WEIGHTS])
```

```python
import functools

import numpy as np
import jax
import jax.numpy as jnp
from jax import lax
from jax.experimental import pallas as pl
from jax.experimental.pallas import tpu as pltpu

F32 = jnp.float32
BF16 = jnp.bfloat16
MESH = pl.DeviceIdType.MESH

D_MODEL = 1024
D_FF = 2816
HEAD_DIM = 64
N_Q_HEADS = 8
ATT_BLOCK = 128
N_BUCKETS = 32
MAX_DISTANCE = 128
REC_HEADS = 4
REC_DIM = 128
PLE_DIM = 256
EPS = 1e-6
IN_W = 4864
COL_AQ, COL_AK, COL_AV, COL_RQ, COL_RF, COL_RI, COL_RG, COL_GA, COL_GB = 0, 4, 5, 6, 10, 14, 18, 22, 30

CHUNK = 64
SUB = 16
N_SUB = CHUNK // SUB

ADAM_LR, ADAM_B1, ADAM_B2, ADAM_EPS, ADAM_WD, ADAM_STEP = 0.001, 0.9, 0.999, 1e-08, 0.01, 10

V7X_VMEM_LIMIT = 56 * 1024 * 1024
N_CHIPS = 4
N_DEV = 8

BIG = ("w_ffn1_in", "w_ffn1_out", "w_in", "w_att_proj", "w_rec_proj", "w_out",
       "w_ffn2_in", "w_ffn2_out", "w_ple_gate", "w_ple_proj")
COL_SHARDED = ("w_ffn1_in", "w_in", "w_att_proj", "w_rec_proj", "w_ffn2_in", "w_ple_proj")
WEIGHTS = ("rel_bias", "lb_param", "norm_ffn1", "w_ffn1_in", "w_ffn1_out", "norm_mix", "w_in", "attn_sinks",
           "rec_norm", "w_att_proj", "w_rec_proj", "w_out", "norm_ffn2", "w_ffn2_in", "w_ffn2_out", "norm_ple",
           "w_ple_gate", "w_ple_proj", "norm_final")
SMALL = tuple(n for n in WEIGHTS if n not in BIG)
PACK_W = 1024
SMALL_ROWS = 64


def _params(*sem):
    return pltpu.CompilerParams(dimension_semantics=sem, vmem_limit_bytes=V7X_VMEM_LIMIT)


def _pick(n, cap, mult=8):
    if n <= cap:
        return n
    for t in range(cap - cap % mult, 0, -mult):
        if n % t == 0:
            return t
    raise ValueError((n, cap, mult))


def _dot(a, b):
    return jnp.dot(a, b, preferred_element_type=F32)


def _dot_nt(a, b):
    return lax.dot_general(a, b, (((1,), (1,)), ((), ())), preferred_element_type=F32)


def _dot_tn(a, b):
    return lax.dot_general(a, b, (((0,), (0,)), ((), ())), preferred_element_type=F32)


def _split3(x):
    hi = x.astype(BF16)
    r = x - hi.astype(F32)
    mid = r.astype(BF16)
    lo = (r - mid.astype(F32)).astype(BF16)
    return hi, mid, lo


def _sel_left(sel_bf16, x):
    hi, mid, lo = _split3(x)
    return _dot(sel_bf16, hi) + _dot(sel_bf16, mid) + _dot(sel_bf16, lo)


def _sel_right(x, sel_bf16):
    hi, mid, lo = _split3(x)
    return _dot(hi, sel_bf16) + _dot(mid, sel_bf16) + _dot(lo, sel_bf16)


def _sigmoid(x):
    return 1.0 / (1.0 + jnp.exp(-x))


def _group8(x):
    r, w = x.shape
    return x.reshape(r // 8, 8, w).sum(axis=0)


def _tile_call(name, fn, M, N, tm, tn, *, pairs=(), tiles=(), rows=(), consts=(), outs=(), parts=0):
    gi, gj = M // tm, N // tn
    assert gi * tm == M and gj * tn == N, (name, M, N, tm, tn)
    arrays, in_specs = [], []
    for a, a_col, b, kind in pairs:
        K = b.shape[0] if kind == "nn" else b.shape[1]
        arrays += [a, b]
        in_specs.append(pl.BlockSpec((tm, K), lambda i, j, c=a_col: (i, c)))
        if kind == "nn":
            in_specs.append(pl.BlockSpec((K, tn), lambda i, j: (0, j)))
        else:
            in_specs.append(pl.BlockSpec((tn, K), lambda i, j: (j, 0)))
    for arr, off in tiles:
        arrays.append(arr)
        in_specs.append(pl.BlockSpec((tm, tn), lambda i, j, o=off: (i, j + o)))
    for arr, bw, col in rows:
        arrays.append(arr)
        in_specs.append(pl.BlockSpec((tm, bw), lambda i, j, c=col: (i, c)))
    for arr in consts:
        arrays.append(arr)
        in_specs.append(pl.BlockSpec(arr.shape, lambda i, j: (0, 0)))
    out_shape, out_specs = [], []
    for dt, kind in outs:
        if kind == "tile":
            out_shape.append(jax.ShapeDtypeStruct((M, N), dt))
            out_specs.append(pl.BlockSpec((tm, tn), lambda i, j: (i, j)))
        else:
            assert gj == 1
            out_shape.append(jax.ShapeDtypeStruct((M, kind[1]), dt))
            out_specs.append(pl.BlockSpec((tm, kind[1]), lambda i, j: (i, 0)))
    for _ in range(parts):
        out_shape.append(jax.ShapeDtypeStruct((gi * 8, N), F32))
        out_specs.append(pl.BlockSpec((8, tn), lambda i, j: (i, j)))
    n_pairs, n_tiles, n_rows, n_consts = len(pairs), len(tiles), len(rows), len(consts)
    kinds = [p[3] for p in pairs]

    def body(*refs):
        pos = 0
        accs = []
        for kind in kinds:
            a_ref, b_ref = refs[pos], refs[pos + 1]
            pos += 2
            accs.append(_dot(a_ref[...], b_ref[...]) if kind == "nn" else _dot_nt(a_ref[...], b_ref[...]))
        tv = [r[...] for r in refs[pos:pos + n_tiles]]
        pos += n_tiles
        rv = [r[...] for r in refs[pos:pos + n_rows]]
        pos += n_rows
        cv = [r[...] for r in refs[pos:pos + n_consts]]
        pos += n_consts
        vals = fn(accs, tv, rv, cv)
        out_refs = refs[pos:]
        assert len(vals) == len(out_refs), (name, len(vals), len(out_refs))
        for o_ref, val in zip(out_refs, vals):
            o_ref[...] = val.astype(o_ref.dtype)

    res = pl.pallas_call(
        body, name=name, grid=(gi, gj), in_specs=in_specs, out_specs=out_specs, out_shape=out_shape,
        compiler_params=_params("arbitrary", "arbitrary"))(*arrays)
    return res


def _mm_tn(name, a, b, *, a_col=0, a_w=None, b_col=0, b_w=None, tm=1024, tn=512, tk=2048):
    T = a.shape[0]
    a_w = a.shape[1] if a_w is None else a_w
    b_w = b.shape[1] if b_w is None else b_w
    tm, tn, tk = _pick(a_w, tm, 128), _pick(b_w, tn, 128), _pick(T, tk, 128)
    a_off, b_off = a_col // tm, b_col // tn
    assert a_off * tm == a_col and b_off * tn == b_col
    nk = T // tk

    def body(a_ref, b_ref, o_ref, acc_ref):
        k = pl.program_id(2)

        @pl.when(k == 0)
        def _():
            acc_ref[...] = jnp.zeros_like(acc_ref)

        acc_ref[...] += _dot_tn(a_ref[...], b_ref[...])

        @pl.when(k == nk - 1)
        def _():
            o_ref[...] = acc_ref[...]

    return pl.pallas_call(
        body, name=name, grid=(a_w // tm, b_w // tn, nk),
        in_specs=[pl.BlockSpec((tk, tm), lambda i, j, k: (k, i + a_off)),
                  pl.BlockSpec((tk, tn), lambda i, j, k: (k, j + b_off))],
        out_specs=pl.BlockSpec((tm, tn), lambda i, j, k: (i, j)),
        out_shape=jax.ShapeDtypeStruct((a_w, b_w), F32),
        scratch_shapes=[pltpu.VMEM((tm, tn), F32)],
        compiler_params=_params("arbitrary", "arbitrary", "arbitrary"))(a, b)


def _colsum(name, x):
    def body(x_ref, o_ref):
        o_ref[...] = jnp.sum(x_ref[...], axis=0, keepdims=True)
    return pl.pallas_call(body, name=name, out_shape=jax.ShapeDtypeStruct((1, x.shape[1]), F32))(x)


def _rms_hat(h):
    return h * lax.rsqrt(jnp.mean(h * h, axis=-1, keepdims=True) + EPS)


def _rms_bwd_vals(dn, h, g):
    r = lax.rsqrt(jnp.mean(h * h, axis=-1, keepdims=True) + EPS)
    nh = h * r
    gd = dn * g
    dh = r * (gd - nh * jnp.mean(gd * nh, axis=-1, keepdims=True))
    return dh, _group8(dn * nh)


def _rms_fwd(name, h, g, tm=512):
    T = h.shape[0]

    def fn(accs, tv, rv, cv):
        return [_rms_hat(tv[0]) * cv[0]]

    return _tile_call(name, fn, T, D_MODEL, _pick(T, tm), D_MODEL, tiles=[(h, 0)], consts=[g],
                      outs=[(BF16, "tile")])[0]


def _ffn_fwd(tag, h, g, w_in, w_out):
    T = h.shape[0]
    n = _rms_fwd(tag + "_norm", h, g)
    tm = _pick(T, 1024)

    def act(accs, tv, rv, cv):
        gate, up = accs
        a = gate * _sigmoid(gate) * up
        return [gate, up, a]

    gate, up, a = _tile_call(
        tag + "_in", act, T, D_FF, tm, 256,
        pairs=[(n, 0, w_in[:, :D_FF], "nn"), (n, 0, w_in[:, D_FF:], "nn")],
        outs=[(BF16, "tile")] * 3)

    def res(accs, tv, rv, cv):
        return [tv[0] + 0.5 * accs[0]]

    h_new = _tile_call(tag + "_out", res, T, D_MODEL, _pick(T, 512), 512,
                       pairs=[(a, 0, w_out, "nn")], tiles=[(h, 0)], outs=[(F32, "tile")])[0]
    return h_new, (n, gate, up, a)


def _ffn_bwd(tag, dh_out, h, g, w_in, w_out, saved):
    T = h.shape[0]
    n, gate, up, a = saved

    def half(accs, tv, rv, cv):
        return [0.5 * tv[0]]

    df = _tile_call(tag + "_df", half, T, D_MODEL, _pick(T, 512), D_MODEL, tiles=[(dh_out, 0)],
                    outs=[(BF16, "tile")])[0]

    def dact(accs, tv, rv, cv):
        da = accs[0]
        gt, u = tv[0].astype(F32), tv[1].astype(F32)
        sg = _sigmoid(gt)
        silu = gt * sg
        dgate = da * u * (sg + silu * (1.0 - sg))
        dup = da * silu
        return [dgate, dup]

    dzg, dzu = _tile_call(tag + "_dact", dact, T, D_FF, _pick(T, 1024), 256,
                          pairs=[(df, 0, w_out, "nt")], tiles=[(gate, 0), (up, 0)],
                          outs=[(BF16, "tile")] * 2)
    dw_out = _mm_tn(tag + "_dwout", a, df)
    dw_in = jnp.concatenate([_mm_tn(tag + "_dwin_g", n, dzg), _mm_tn(tag + "_dwin_u", n, dzu)], axis=1)

    def dnorm(accs, tv, rv, cv):
        dh, dg = _rms_bwd_vals(accs[0] + accs[1], tv[0], cv[0])
        return [tv[1] + dh, dg]

    dh, dg = _tile_call(tag + "_dnorm", dnorm, T, D_MODEL, _pick(T, 256), D_MODEL,
                        pairs=[(dzg, 0, w_in[:, :D_FF], "nt"), (dzu, 0, w_in[:, D_FF:], "nt")],
                        tiles=[(h, 0), (dh_out, 0)], consts=[g], outs=[(F32, "tile")], parts=1)
    return dh, dg, dw_in, dw_out


def _t5_onehot():
    qi = np.arange(ATT_BLOCK)[:, None] + ATT_BLOCK
    kj = np.arange(2 * ATT_BLOCK)[None, :]
    nn = np.maximum(qi - kj, 0)
    max_exact = N_BUCKETS // 2
    large = max_exact + (np.log(np.maximum(nn, 1) / max_exact) / np.log(MAX_DISTANCE / max_exact)
                         * (N_BUCKETS - max_exact)).astype(np.int32)
    large = np.minimum(large, N_BUCKETS - 1)
    bucket = np.where(nn < max_exact, nn, large).astype(np.int32).reshape(-1)
    return (bucket[None, :] == np.arange(N_BUCKETS)[:, None]).astype(np.float32)


def _small_mm(name, a, b, sel):
    def body(a_ref, b_ref, o_ref):
        if sel == "right":
            o_ref[...] = _sel_right(a_ref[...], b_ref[...])
        else:
            o_ref[...] = _sel_left(a_ref[...], b_ref[...])
    return pl.pallas_call(body, name=name, out_shape=jax.ShapeDtypeStruct((a.shape[0], b.shape[1]), F32),
                          compiler_params=pltpu.CompilerParams(vmem_limit_bytes=V7X_VMEM_LIMIT))(a, b)


def _kv_layouts(proj):
    k = proj[:, COL_AK * 128:(COL_AK + 1) * 128]
    v = proj[:, COL_AV * 128:(COL_AV + 1) * 128]
    sw = lambda t: jnp.concatenate([t[:, HEAD_DIM:], t[:, :HEAD_DIM]], axis=1)
    return (jnp.concatenate([k, sw(k)], axis=1).astype(BF16), jnp.concatenate([v, sw(v)], axis=1).astype(BF16))


def _swa_masks():
    row = lax.broadcasted_iota(jnp.int32, (ATT_BLOCK, 2 * ATT_BLOCK), 0)
    col = lax.broadcasted_iota(jnp.int32, (ATT_BLOCK, 2 * ATT_BLOCK), 1)
    dist = ATT_BLOCK + row - col
    return (dist >= 0) & (dist < ATT_BLOCK), col


def _swa_heads():
    out = []
    for h in range(N_Q_HEADS):
        lo = h % 2 == 0
        hk = h // 4
        swapped = (hk == 1) if lo else (hk == 0)
        out.append((h // 2, lo, swapped))
    return out


def _swa_probs(qm, kk, bias_h, sink, valid):
    s = _dot_nt(qm, kk) * (HEAD_DIM ** -0.5) + bias_h
    s = jnp.where(valid, s, -jnp.inf)
    m = jnp.maximum(jnp.max(s, axis=-1, keepdims=True), sink)
    e = jnp.exp(s - m)
    es = jnp.exp(sink - m)
    den = jnp.sum(e, axis=-1, keepdims=True) + es
    return e / den, es / den


def _swa_fwd(proj, kk2, vv2, bias, sinks, B, S):
    T = B * S
    nb = S // ATT_BLOCK

    def body(q_ref, k_ref, v_ref, bias_ref, sink_ref, o_ref, kpad, vpad):
        zeros = jnp.zeros((ATT_BLOCK, 256), BF16)
        kpad[pl.ds(0, ATT_BLOCK), :] = zeros
        vpad[pl.ds(0, ATT_BLOCK), :] = zeros
        kpad[pl.ds(ATT_BLOCK, S), :] = k_ref[...]
        vpad[pl.ds(ATT_BLOCK, S), :] = v_ref[...]
        valid0, col = _swa_masks()
        lane = lax.broadcasted_iota(jnp.int32, (1, 128), 1)
        lo_q = lane < HEAD_DIM
        heads = _swa_heads()

        def blk(n, carry):
            r0 = pl.multiple_of(n * ATT_BLOCK, ATT_BLOCK)
            valid = valid0 & ((n > 0) | (col >= ATT_BLOCK))
            kb = kpad[pl.ds(r0, 2 * ATT_BLOCK), :]
            vb = vpad[pl.ds(r0, 2 * ATT_BLOCK), :]
            for j in range(N_Q_HEADS // 2):
                qblk = q_ref[pl.ds(r0, ATT_BLOCK), pl.ds(128 * j, 128)].astype(BF16)
                acc = jnp.zeros((ATT_BLOCK, 128), F32)
                for h in (2 * j, 2 * j + 1):
                    _, lo, swapped = heads[h]
                    keep = lo_q if lo else ~lo_q
                    qm = jnp.where(keep, qblk, jnp.zeros_like(qblk))
                    kk = kb[:, 128:] if swapped else kb[:, :128]
                    vv = vb[:, 128:] if swapped else vb[:, :128]
                    vm = jnp.where(keep, vv, jnp.zeros_like(vv))
                    p, _ = _swa_probs(qm, kk, bias_ref[h], sink_ref[h], valid)
                    acc = acc + _dot(p.astype(BF16), vm)
                o_ref[pl.ds(r0, ATT_BLOCK), pl.ds(128 * j, 128)] = acc.astype(o_ref.dtype)
            return carry

        lax.fori_loop(0, nb, blk, 0)

    return pl.pallas_call(
        body, name="swa_fwd", grid=(B,),
        in_specs=[pl.BlockSpec((S, 512), lambda b: (b, 0)),
                  pl.BlockSpec((S, 256), lambda b: (b, 0)),
                  pl.BlockSpec((S, 256), lambda b: (b, 0)),
                  pl.BlockSpec((N_Q_HEADS, ATT_BLOCK, 2 * ATT_BLOCK), lambda b: (0, 0, 0)),
                  pl.BlockSpec(memory_space=pltpu.SMEM)],
        out_specs=pl.BlockSpec((S, 512), lambda b: (b, 0)),
        out_shape=jax.ShapeDtypeStruct((T, 512), BF16),
        scratch_shapes=[pltpu.VMEM((S + ATT_BLOCK, 256), BF16), pltpu.VMEM((S + ATT_BLOCK, 256), BF16)],
        compiler_params=_params("arbitrary"))(proj, kk2, vv2, bias, sinks)


def _swa_bwd(proj, kk2, vv2, bias, sinks, datt, B, S):
    T = B * S
    nb = S // ATT_BLOCK

    def body(q_ref, k_ref, v_ref, bias_ref, sink_ref, do_ref, dq_ref, dk_ref, dv_ref, dbias_ref, dsink_ref,
             kpad, vpad, dkpad, dvpad):
        b = pl.program_id(0)

        @pl.when(b == 0)
        def _():
            dbias_ref[...] = jnp.zeros_like(dbias_ref)
            dsink_ref[...] = jnp.zeros_like(dsink_ref)

        zeros = jnp.zeros((ATT_BLOCK, 256), BF16)
        kpad[pl.ds(0, ATT_BLOCK), :] = zeros
        vpad[pl.ds(0, ATT_BLOCK), :] = zeros
        kpad[pl.ds(ATT_BLOCK, S), :] = k_ref[...]
        vpad[pl.ds(ATT_BLOCK, S), :] = v_ref[...]
        dkpad[...] = jnp.zeros_like(dkpad)
        dvpad[...] = jnp.zeros_like(dvpad)
        valid0, col = _swa_masks()
        lane = lax.broadcasted_iota(jnp.int32, (1, 128), 1)
        lo_q = lane < HEAD_DIM
        heads = _swa_heads()
        scale = HEAD_DIM ** -0.5

        def blk(n, carry):
            r0 = pl.multiple_of(n * ATT_BLOCK, ATT_BLOCK)
            valid = valid0 & ((n > 0) | (col >= ATT_BLOCK))
            kb = kpad[pl.ds(r0, 2 * ATT_BLOCK), :]
            vb = vpad[pl.ds(r0, 2 * ATT_BLOCK), :]
            dk_acc = [jnp.zeros((2 * ATT_BLOCK, 128), F32), jnp.zeros((2 * ATT_BLOCK, 128), F32)]
            dv_acc = [jnp.zeros((2 * ATT_BLOCK, 128), F32), jnp.zeros((2 * ATT_BLOCK, 128), F32)]
            for j in range(N_Q_HEADS // 2):
                qblk = q_ref[pl.ds(r0, ATT_BLOCK), pl.ds(128 * j, 128)].astype(BF16)
                doblk = do_ref[pl.ds(r0, ATT_BLOCK), pl.ds(128 * j, 128)]
                dq = jnp.zeros((ATT_BLOCK, 128), F32)
                for h in (2 * j, 2 * j + 1):
                    _, lo, swapped = heads[h]
                    keep = lo_q if lo else ~lo_q
                    qm = jnp.where(keep, qblk, jnp.zeros_like(qblk))
                    dom = jnp.where(keep, doblk, jnp.zeros_like(doblk))
                    kk = kb[:, 128:] if swapped else kb[:, :128]
                    vv = vb[:, 128:] if swapped else vb[:, :128]
                    km = jnp.where(keep, kk, jnp.zeros_like(kk))
                    p, ps = _swa_probs(qm, kk, bias_ref[h], sink_ref[h], valid)
                    dp = _dot_nt(dom, vv)
                    delta = jnp.sum(p * dp, axis=-1, keepdims=True)
                    ds = p * (dp - delta)
                    dbias_ref[h] += ds
                    dsink_ref[pl.ds(h, 1), :] += -jnp.sum(jnp.broadcast_to(ps * delta, (ATT_BLOCK, 128)),
                                                          axis=0, keepdims=True)
                    dsb = (ds * scale).astype(BF16)
                    dq = dq + _dot(dsb, km)
                    idx = 1 if swapped else 0
                    dk_acc[idx] = dk_acc[idx] + _dot_tn(dsb, qm)
                    dv_acc[idx] = dv_acc[idx] + _dot_tn(p.astype(BF16), dom)
                dq_ref[pl.ds(r0, ATT_BLOCK), pl.ds(128 * j, 128)] = dq.astype(dq_ref.dtype)
            dkpad[pl.ds(r0, 2 * ATT_BLOCK), :] += jnp.concatenate(dk_acc, axis=1)
            dvpad[pl.ds(r0, 2 * ATT_BLOCK), :] += jnp.concatenate(dv_acc, axis=1)
            return carry

        lax.fori_loop(0, nb, blk, 0)
        dk_ref[...] = dkpad[pl.ds(ATT_BLOCK, S), :]
        dv_ref[...] = dvpad[pl.ds(ATT_BLOCK, S), :]

    return pl.pallas_call(
        body, name="swa_bwd", grid=(B,),
        in_specs=[pl.BlockSpec((S, 512), lambda b: (b, 0)),
                  pl.BlockSpec((S, 256), lambda b: (b, 0)),
                  pl.BlockSpec((S, 256), lambda b: (b, 0)),
                  pl.BlockSpec((N_Q_HEADS, ATT_BLOCK, 2 * ATT_BLOCK), lambda b: (0, 0, 0)),
                  pl.BlockSpec(memory_space=pltpu.SMEM),
                  pl.BlockSpec((S, 512), lambda b: (b, 0))],
        out_specs=[pl.BlockSpec((S, 512), lambda b: (b, 0)),
                   pl.BlockSpec((S, 256), lambda b: (b, 0)),
                   pl.BlockSpec((S, 256), lambda b: (b, 0)),
                   pl.BlockSpec((N_Q_HEADS, ATT_BLOCK, 2 * ATT_BLOCK), lambda b: (0, 0, 0)),
                   pl.BlockSpec((N_Q_HEADS, 128), lambda b: (0, 0))],
        out_shape=[jax.ShapeDtypeStruct((T, 512), BF16),
                   jax.ShapeDtypeStruct((T, 256), F32),
                   jax.ShapeDtypeStruct((T, 256), F32),
                   jax.ShapeDtypeStruct((N_Q_HEADS, ATT_BLOCK, 2 * ATT_BLOCK), F32),
                   jax.ShapeDtypeStruct((N_Q_HEADS, 128), F32)],
        scratch_shapes=[pltpu.VMEM((S + ATT_BLOCK, 256), BF16), pltpu.VMEM((S + ATT_BLOCK, 256), BF16),
                        pltpu.VMEM((S + ATT_BLOCK, 256), F32), pltpu.VMEM((S + ATT_BLOCK, 256), F32)],
        compiler_params=_params("arbitrary"))(proj, kk2, vv2, bias, sinks, datt)


def _hgrn_gates(z, lb):
    sg = _sigmoid(z)
    f = lb + (1.0 - lb) * sg
    return sg, f, jnp.log(f), 1.0 - f


def _hgrn_consts():
    r = lax.broadcasted_iota(jnp.int32, (CHUNK, CHUNK), 0)
    c = lax.broadcasted_iota(jnp.int32, (CHUNK, CHUNK), 1)
    tril = (r >= c).astype(BF16)
    triu = (r <= c).astype(BF16)
    causal = r >= c
    below = (r // SUB) > (c // SUB)
    return tril, triu, causal, below, r, c


def _hgrn_offdiag(q, k, b_ref):
    zero = jnp.zeros((SUB, REC_DIM), F32)
    q_rows, k_cols, eqs, eks = [jnp.zeros((SUB, (N_SUB - 1) * REC_DIM), F32)], [], [], []
    for i in range(1, N_SUB):
        p = b_ref[pl.ds(SUB * i - 1, 1), :]
        eq = jnp.exp(b_ref[pl.ds(SUB * i, SUB), :] - p)
        qi = q[SUB * i:SUB * (i + 1), :] * eq
        q_rows.append(jnp.concatenate([zero] * (i - 1) + [qi] + [zero] * (N_SUB - 1 - i), axis=1))
        ek = jnp.exp(p - b_ref[pl.ds(0, SUB * i), :])
        ki = k[:SUB * i, :] * ek
        pad = jnp.zeros((CHUNK - SUB * i, REC_DIM), F32)
        k_cols.append(jnp.concatenate([ki, pad], axis=0))
        eqs.append(eq)
        eks.append(jnp.concatenate([ek, pad], axis=0))
    return jnp.concatenate(q_rows, axis=0), jnp.concatenate(k_cols, axis=1), eqs, eks


def _hgrn_diag(q, k_ref, b_ref):
    lane = lax.broadcasted_iota(jnp.int32, (SUB, CHUNK), 1)
    rowm = lax.broadcasted_iota(jnp.int32, (SUB, CHUNK), 0)
    blocks = []
    for i in range(N_SUB):
        qi = q[SUB * i:SUB * (i + 1), :]
        bi = b_ref[pl.ds(SUB * i, SUB), :]
        d = jnp.zeros((SUB, CHUNK), F32)
        for s in range(SUB):
            ks = k_ref[pl.ds(SUB * i + s, 1), :]
            bs = b_ref[pl.ds(SUB * i + s, 1), :]
            w = jnp.exp(jnp.minimum(bi - bs, 0.0))
            colv = jnp.sum(qi * ks * w, axis=-1, keepdims=True)
            d = jnp.where((lane == SUB * i + s) & (rowm >= s), colv, d)
        blocks.append(d)
    return jnp.concatenate(blocks, axis=0)


def _hgrn_fwd(proj, lb_param, B, S):
    T = B * S
    nc = S // CHUNK

    def body(q_ref, z_ref, v_ref, lb_ref, o_ref, st_ref, k_s, b_s):
        lb = _sigmoid(lb_ref[0:1, :] - lb_ref[1:2, :])
        tril, _, _, below, _, _ = _hgrn_consts()

        def chunk(ci, ht):
            r0 = pl.multiple_of(ci * CHUNK, CHUNK)
            q = q_ref[pl.ds(r0, CHUNK), :]
            v = v_ref[pl.ds(r0, CHUNK), :]
            _, _, g, k = _hgrn_gates(z_ref[pl.ds(r0, CHUNK), :], lb)
            bcum = _sel_left(tril, g)
            k_s[...] = k
            b_s[...] = bcum
            st_ref[ci] = ht
            qst, kst, _, _ = _hgrn_offdiag(q, k, b_s)
            a = jnp.where(below, _dot_nt(qst.astype(BF16), kst.astype(BF16)), 0.0) + _hgrn_diag(q, k_s, b_s)
            vb = v.astype(BF16)
            qb = (q * jnp.exp(bcum)).astype(BF16)
            o = _dot(a.astype(BF16), vb) + _dot_nt(qb, ht.astype(BF16))
            o_ref[pl.ds(r0, CHUNK), :] = o
            b_last = b_s[pl.ds(CHUNK - 1, 1), :]
            kb = (k * jnp.exp(b_last - bcum)).astype(BF16)
            return ht * jnp.exp(b_last) + _dot_tn(vb, kb)

        lax.fori_loop(0, nc, chunk, jnp.zeros((REC_DIM, REC_DIM), F32))

    H = REC_HEADS
    return pl.pallas_call(
        body, name="hgrn_fwd", grid=(B, H),
        in_specs=[pl.BlockSpec((S, 128), lambda b, h: (b, COL_RQ + h)),
                  pl.BlockSpec((S, 128), lambda b, h: (b, COL_RF + h)),
                  pl.BlockSpec((S, 128), lambda b, h: (b, COL_RI + h)),
                  pl.BlockSpec((2, 128), lambda b, h: (0, h))],
        out_specs=[pl.BlockSpec((S, 128), lambda b, h: (b, h)),
                   pl.BlockSpec((nc, REC_DIM, REC_DIM), lambda b, h: (b * H + h, 0, 0))],
        out_shape=[jax.ShapeDtypeStruct((T, 512), F32),
                   jax.ShapeDtypeStruct((B * H * nc, REC_DIM, REC_DIM), F32)],
        scratch_shapes=[pltpu.VMEM((CHUNK, REC_DIM), F32), pltpu.VMEM((CHUNK, REC_DIM), F32)],
        compiler_params=_params("arbitrary", "arbitrary"))(proj, proj, proj, lb_param)


def _hgrn_bwd(proj, lb_param, states, do, B, S):
    T = B * S
    nc = S // CHUNK

    def body(q_ref, z_ref, v_ref, lb_ref, st_ref, do_ref, dq_ref, dz_ref, dv_ref, dlb_ref, k_s, b_s, dkd_s):
        lb = _sigmoid(lb_ref[0:1, :] - lb_ref[1:2, :])
        tril, triu, causal, below, r, _ = _hgrn_consts()
        lane = lax.broadcasted_iota(jnp.int32, (SUB, CHUNK), 1)
        rowm = lax.broadcasted_iota(jnp.int32, (SUB, CHUNK), 0)
        last_row = lax.broadcasted_iota(jnp.int32, (CHUNK, 1), 0) == CHUNK - 1

        def chunk(it, carry):
            dht, dlb = carry
            ci = nc - 1 - it
            r0 = pl.multiple_of(ci * CHUNK, CHUNK)
            q = q_ref[pl.ds(r0, CHUNK), :]
            v = v_ref[pl.ds(r0, CHUNK), :]
            dout = do_ref[pl.ds(r0, CHUNK), :]
            sg, f, g, k = _hgrn_gates(z_ref[pl.ds(r0, CHUNK), :], lb)
            bcum = _sel_left(tril, g)
            k_s[...] = k
            b_s[...] = bcum
            ht = st_ref[ci]
            qst, kst, eqs, eks = _hgrn_offdiag(q, k, b_s)
            qst_b, kst_b = qst.astype(BF16), kst.astype(BF16)
            a = jnp.where(below, _dot_nt(qst_b, kst_b), 0.0) + _hgrn_diag(q, k_s, b_s)
            vb, dob = v.astype(BF16), dout.astype(BF16)
            eb = jnp.exp(bcum)
            b_last = b_s[pl.ds(CHUNK - 1, 1), :]
            el = jnp.exp(b_last)
            ekb = jnp.exp(b_last - bcum)
            qb = (q * eb).astype(BF16)
            kb = k * ekb
            dhb = dht.astype(BF16)
            dv = _dot_tn(a.astype(BF16), dob) + _dot_nt(kb.astype(BF16), dhb)
            da = jnp.where(causal, _dot_nt(dob, vb), 0.0)
            dqb = _dot(dob, ht.astype(BF16))
            dkb = _dot(vb, dhb)
            dht_new = dht * el + _dot_tn(dob, qb)
            da_off = jnp.where(below, da, 0.0).astype(BF16)
            dqst = _dot(da_off, kst_b)
            dkst = _dot_tn(da_off, qst_b)
            dq_rows = [jnp.zeros((SUB, REC_DIM), F32)]
            dk = jnp.zeros((CHUNK, REC_DIM), F32)
            for i in range(1, N_SUB):
                dq_rows.append(dqst[SUB * i:SUB * (i + 1), REC_DIM * (i - 1):REC_DIM * i] * eqs[i - 1])
                dk = dk + dkst[:, REC_DIM * (i - 1):REC_DIM * i] * eks[i - 1]
            dq = jnp.concatenate(dq_rows, axis=0)
            dkd_s[...] = jnp.zeros_like(dkd_s)
            dq_diag = []
            for i in range(N_SUB):
                qi = q[SUB * i:SUB * (i + 1), :]
                bi = b_s[pl.ds(SUB * i, SUB), :]
                dai = da[SUB * i:SUB * (i + 1), :]
                dqi = jnp.zeros((SUB, REC_DIM), F32)
                for s in range(SUB):
                    ks = k_s[pl.ds(SUB * i + s, 1), :]
                    bs = b_s[pl.ds(SUB * i + s, 1), :]
                    w = jnp.exp(jnp.minimum(bi - bs, 0.0))
                    dacol = jnp.sum(jnp.where((lane == SUB * i + s) & (rowm >= s), dai, 0.0), axis=-1,
                                    keepdims=True)
                    dqi = dqi + dacol * ks * w
                    dkd_s[pl.ds(SUB * i + s, 1), :] += jnp.sum(dacol * qi * w, axis=0, keepdims=True)
                dq_diag.append(dqi)
            dq = dq + jnp.concatenate(dq_diag, axis=0) + eb * dqb
            dk = dk + dkd_s[...] + ekb * dkb
            edge = jnp.sum(kb * dkb, axis=0, keepdims=True) + el * jnp.sum(ht * dht, axis=0, keepdims=True)
            db = q * dq - k * dk + jnp.where(last_row, edge, 0.0)
            dg = _sel_left(triu, db)
            df = dg / f - dk
            dz = df * (1.0 - lb) * sg * (1.0 - sg)
            dlb = dlb + jnp.sum(df * (1.0 - sg), axis=0, keepdims=True)
            dq_ref[pl.ds(r0, CHUNK), :] = dq.astype(dq_ref.dtype)
            dz_ref[pl.ds(r0, CHUNK), :] = dz.astype(dz_ref.dtype)
            dv_ref[pl.ds(r0, CHUNK), :] = dv.astype(dv_ref.dtype)
            return dht_new, dlb

        _, dlb = lax.fori_loop(0, nc, chunk, (jnp.zeros((REC_DIM, REC_DIM), F32), jnp.zeros((1, REC_DIM), F32)))
        dlb_ref[...] = jnp.broadcast_to(dlb * lb * (1.0 - lb), (8, REC_DIM))

    H = REC_HEADS
    return pl.pallas_call(
        body, name="hgrn_bwd", grid=(B, H),
        in_specs=[pl.BlockSpec((S, 128), lambda b, h: (b, COL_RQ + h)),
                  pl.BlockSpec((S, 128), lambda b, h: (b, COL_RF + h)),
                  pl.BlockSpec((S, 128), lambda b, h: (b, COL_RI + h)),
                  pl.BlockSpec((2, 128), lambda b, h: (0, h)),
                  pl.BlockSpec((nc, REC_DIM, REC_DIM), lambda b, h: (b * H + h, 0, 0)),
                  pl.BlockSpec((S, 128), lambda b, h: (b, h))],
        out_specs=[pl.BlockSpec((S, 128), lambda b, h: (b, h))] * 3
        + [pl.BlockSpec((8, 128), lambda b, h: (b, h))],
        out_shape=[jax.ShapeDtypeStruct((T, 512), BF16)] * 3 + [jax.ShapeDtypeStruct((B * 8, 512), F32)],
        scratch_shapes=[pltpu.VMEM((CHUNK, REC_DIM), F32)] * 3,
        compiler_params=_params("arbitrary", "arbitrary"))(proj, proj, proj, lb_param, states, do)


def _rec_gate_fwd(rec, proj, rec_norm):
    T = rec.shape[0]

    def fn(accs, tv, rv, cv):
        return [_rms_hat(tv[0]) * cv[0] * _sigmoid(tv[1])]

    return _tile_call("rec_gate", fn, T, 512, _pick(T, 1024), REC_DIM, tiles=[(rec, 0), (proj, COL_RG)],
                      consts=[rec_norm], outs=[(BF16, "tile")])[0]


def _rec_gate_bwd(dyb, w_rec_proj, rec, proj, rec_norm):
    T = rec.shape[0]

    def fn(accs, tv, rv, cv):
        d, r, rg = accs[0], tv[0], tv[1]
        sg = _sigmoid(rg)
        rn = _rms_hat(r) * cv[0]
        dh, dg = _rms_bwd_vals(d * sg, r, cv[0])
        return [dh, d * rn * sg * (1.0 - sg), dg]

    return _tile_call("rec_gate_bwd", fn, T, 512, _pick(T, 1024), REC_DIM, pairs=[(dyb, 0, w_rec_proj, "nt")],
                      tiles=[(rec, 0), (proj, COL_RG)], consts=[rec_norm],
                      outs=[(F32, "tile"), (BF16, "tile")], parts=1)


def _mix_out_fwd(att, recn, proj, w_att_proj, w_rec_proj, w_out, h1):
    T = att.shape[0]
    tn = 256

    def merge(accs, tv, rv, cv):
        ya, yb = accs
        return [ya, yb, _sigmoid(tv[0]) * ya + _sigmoid(tv[1]) * yb]

    ya, yb, merged = _tile_call(
        "merge", merge, T, D_MODEL, _pick(T, 1024), tn,
        pairs=[(att, 0, w_att_proj, "nn"), (recn, 0, w_rec_proj, "nn")],
        tiles=[(proj, COL_GA * 128 // tn), (proj, COL_GB * 128 // tn)], outs=[(BF16, "tile")] * 3)

    def res(accs, tv, rv, cv):
        return [tv[0] + accs[0]]

    h2 = _tile_call("mix_out", res, T, D_MODEL, _pick(T, 512), 512, pairs=[(merged, 0, w_out, "nn")],
                    tiles=[(h1, 0)], outs=[(F32, "tile")])[0]
    return h2, (ya, yb, merged)


def _local_step(x, p, tgt, w, B, S):
    T = B * S
    g_ffn1, g_mix, g_ffn2, g_ple = w["norm_ffn1"], w["norm_mix"], w["norm_ffn2"], w["norm_ple"]
    g_fin = w["norm_final"].reshape(1, D_MODEL)
    grads = {}

    h1, sv1 = _ffn_fwd("ffn1", x, g_ffn1, w["w_ffn1_in"], w["w_ffn1_out"])
    u = _rms_fwd("mix_norm", h1, g_mix)

    def ident(accs, tv, rv, cv):
        return [accs[0]]

    proj = _tile_call("in_proj", ident, T, IN_W, _pick(T, 1024), 256, pairs=[(u, 0, w["w_in"], "nn")],
                      outs=[(F32, "tile")])[0]
    onehot = jnp.asarray(_t5_onehot())
    bias = _small_mm("t5_bias", w["rel_bias"].T, onehot.astype(BF16), "right")
    bias = bias.reshape(N_Q_HEADS, ATT_BLOCK, 2 * ATT_BLOCK)
    sinks = w["attn_sinks"].reshape(N_Q_HEADS)
    kk2, vv2 = _kv_layouts(proj)
    att = _swa_fwd(proj, kk2, vv2, bias, sinks, B, S)
    rec, states = _hgrn_fwd(proj, w["lb_param"], B, S)
    recn = _rec_gate_fwd(rec, proj, w["rec_norm"])
    h2, (ya, yb, merged) = _mix_out_fwd(att, recn, proj, w["w_att_proj"], w["w_rec_proj"], w["w_out"], h1)
    h3, sv2 = _ffn_fwd("ffn2", h2, g_ffn2, w["w_ffn2_in"], w["w_ffn2_out"])
    n3 = _rms_fwd("ple_norm", h3, g_ple)
    pb = p.astype(BF16)

    def ple(accs, tv, rv, cv):
        gate = _sigmoid(accs[0])
        return [gate, accs[1], tv[0] + gate * accs[1]]

    gate_p, pp, h4 = _tile_call(
        "ple", ple, T, D_MODEL, _pick(T, 512), 512,
        pairs=[(n3, 0, w["w_ple_gate"], "nn"), (pb, 0, w["w_ple_proj"], "nn")], tiles=[(h3, 0)],
        outs=[(BF16, "tile"), (BF16, "tile"), (F32, "tile")])

    def head(accs, tv, rv, cv):
        h, t = tv
        err = _rms_hat(h) * cv[0] - t
        dh, dg = _rms_bwd_vals(err * (1.0 / D_MODEL), h, cv[0])
        return [dh, _group8(err * err), dg]

    dh4, loss_p, dg_fin = _tile_call("loss_head", head, T, D_MODEL, _pick(T, 256), D_MODEL,
                                     tiles=[(h4, 0), (tgt, 0)], consts=[g_fin], outs=[(F32, "tile")], parts=2)
    grads["norm_final"] = dg_fin

    def dple(accs, tv, rv, cv):
        d, gt, ppv = tv[0], tv[1].astype(F32), tv[2].astype(F32)
        return [d * ppv * gt * (1.0 - gt), d * gt]

    dzg, dpp = _tile_call("ple_dact", dple, T, D_MODEL, _pick(T, 512), D_MODEL,
                          tiles=[(dh4, 0), (gate_p, 0), (pp, 0)], outs=[(BF16, "tile")] * 2)
    grads["w_ple_gate"] = _mm_tn("ple_dwg", n3, dzg)
    grads["w_ple_proj"] = _mm_tn("ple_dwp", pb, dpp)

    def dnorm(accs, tv, rv, cv):
        dh, dg = _rms_bwd_vals(accs[0], tv[0], cv[0])
        return [tv[1] + dh, dg]

    dh3, grads["norm_ple"] = _tile_call(
        "ple_dnorm", dnorm, T, D_MODEL, _pick(T, 256), D_MODEL, pairs=[(dzg, 0, w["w_ple_gate"], "nt")],
        tiles=[(h3, 0), (dh4, 0)], consts=[g_ple], outs=[(F32, "tile")], parts=1)

    dh2, grads["norm_ffn2"], grads["w_ffn2_in"], grads["w_ffn2_out"] = _ffn_bwd(
        "ffn2b", dh3, h2, g_ffn2, w["w_ffn2_in"], w["w_ffn2_out"], sv2)

    def to_bf(accs, tv, rv, cv):
        return [tv[0]]

    dh2b = _tile_call("mix_dcast", to_bf, T, D_MODEL, _pick(T, 512), D_MODEL, tiles=[(dh2, 0)],
                      outs=[(BF16, "tile")])[0]
    grads["w_out"] = _mm_tn("mix_dwout", merged, dh2b)
    tn = 256

    def dmerge(accs, tv, rv, cv):
        dm = accs[0]
        sa, sb = _sigmoid(tv[0]), _sigmoid(tv[1])
        yav, ybv = tv[2].astype(F32), tv[3].astype(F32)
        return [dm * sa, dm * sb, dm * yav * sa * (1.0 - sa), dm * ybv * sb * (1.0 - sb)]

    dya, dyb, dga, dgb = _tile_call(
        "mix_dmerge", dmerge, T, D_MODEL, _pick(T, 1024), tn, pairs=[(dh2b, 0, w["w_out"], "nt")],
        tiles=[(proj, COL_GA * 128 // tn), (proj, COL_GB * 128 // tn), (ya, 0), (yb, 0)],
        outs=[(BF16, "tile")] * 4)
    grads["w_att_proj"] = _mm_tn("mix_dwatt", att, dya)
    grads["w_rec_proj"] = _mm_tn("mix_dwrec", recn, dyb)

    datt = _tile_call("mix_datt", ident, T, 512, _pick(T, 1024), 512, pairs=[(dya, 0, w["w_att_proj"], "nt")],
                      outs=[(BF16, "tile")])[0]
    drec, drg, grads["rec_norm"] = _rec_gate_bwd(dyb, w["w_rec_proj"], rec, proj, w["rec_norm"])

    drq, drf, dri, dlb = _hgrn_bwd(proj, w["lb_param"], states, drec, B, S)
    grads["lb_param"] = dlb
    daq, dk2, dv2, dbias, dsink = _swa_bwd(proj, kk2, vv2, bias, sinks, datt, B, S)
    grads["attn_sinks"] = dsink
    grads["rel_bias"] = _small_mm("t5_dbias", dbias.reshape(N_Q_HEADS, -1), onehot.T.astype(BF16), "right")
    sw = lambda t: jnp.concatenate([t[:, HEAD_DIM:], t[:, :HEAD_DIM]], axis=1)
    dak = (dk2[:, :128] + sw(dk2[:, 128:])).astype(BF16)
    dav = (dv2[:, :128] + sw(dv2[:, 128:])).astype(BF16)
    dproj = jnp.concatenate([daq, dak, dav, drq, drf, dri, drg, dga, dgb], axis=1)
    grads["w_in"] = _mm_tn("mix_dwin", u, dproj)

    def dnorm_mix(accs, tv, rv, cv):
        dh, dg = _rms_bwd_vals(accs[0], tv[0], cv[0])
        return [tv[1] + dh, dg]

    dh1, grads["norm_mix"] = _tile_call(
        "mix_dnorm", dnorm_mix, T, D_MODEL, _pick(T, 256), D_MODEL, pairs=[(dproj, 0, w["w_in"], "nt")],
        tiles=[(h1, 0), (dh2, 0)], consts=[g_mix], outs=[(F32, "tile")], parts=1)

    dx, grads["norm_ffn1"], grads["w_ffn1_in"], grads["w_ffn1_out"] = _ffn_bwd(
        "ffn1b", dh1, x, g_ffn1, w["w_ffn1_in"], w["w_ffn1_out"], sv1)
    return loss_p, dx, grads


def _place():
    x, y, c = lax.axis_index("x"), lax.axis_index("y"), lax.axis_index("c")
    return x, y, c


def _other_chips(x, y):
    return [(1 - x, y, 2 * (1 - x) + y), (x, 1 - y, 2 * x + 1 - y), (1 - x, 1 - y, 2 * (1 - x) + 1 - y)]


ANY = pl.BlockSpec(memory_space=pl.ANY)


def _gather_weights(wp):
    R = wp.shape[0]
    half = R // 2

    def body(w_ref, out_ref, send_sems, recv_sems, local_sem):
        x, y, c = _place()
        me = 2 * x + y
        sibling = (x, y, 1 - c)
        chips = _other_chips(x, y)

        def rows(chip, h):
            return out_ref.at[chip, pl.ds(h * half, half), :]

        def copy(k, chip, h, to, src=None):
            return pltpu.make_async_remote_copy(
                src_ref=rows(chip, h) if src is None else src, dst_ref=rows(chip, h),
                send_sem=send_sems.at[k], recv_sem=recv_sems.at[k], device_id=to, device_id_type=MESH)

        mine = pltpu.make_async_copy(w_ref, out_ref.at[me], local_sem)
        mine.start()
        first = [copy(j, me, c, (cx, cy, c), src=w_ref.at[pl.ds(c * half, half), :])
                 for j, (cx, cy, _) in enumerate(chips)]
        for cp in first:
            cp.start()
        passed = [copy(3 + j, ci, c, sibling) for j, (_, _, ci) in enumerate(chips)]
        for j, (cx, cy, ci) in enumerate(chips):
            copy(j, ci, c, (cx, cy, c)).wait_recv()
            passed[j].start()
        for j, (_, _, ci) in enumerate(chips):
            copy(3 + j, ci, 1 - c, sibling).wait_recv()
        for cp in first + passed:
            cp.wait_send()
        mine.wait()

    return pl.pallas_call(
        body, name="gather_weights", in_specs=[ANY], out_specs=ANY,
        out_shape=jax.ShapeDtypeStruct((N_CHIPS, R, PACK_W), wp.dtype),
        scratch_shapes=[pltpu.SemaphoreType.DMA((6,)), pltpu.SemaphoreType.DMA((6,)), pltpu.SemaphoreType.DMA(())],
    )(wp)


def _swap_halves(gp):
    R = gp.shape[1]
    half = R // 2

    def body(g_ref, out_ref, send_sem, recv_sem):
        x, y, c = _place()
        cp = pltpu.make_async_remote_copy(
            src_ref=g_ref.at[:, pl.ds((1 - c) * half, half), :], dst_ref=out_ref,
            send_sem=send_sem, recv_sem=recv_sem, device_id=(x, y, 1 - c), device_id_type=MESH)
        cp.start()
        cp.wait()

    return pl.pallas_call(
        body, name="rs_sibling", in_specs=[ANY], out_specs=ANY,
        out_shape=jax.ShapeDtypeStruct((N_CHIPS, half, PACK_W), gp.dtype),
        scratch_shapes=[pltpu.SemaphoreType.DMA(()), pltpu.SemaphoreType.DMA(())],
    )(gp)


def _scatter_chips(pp):
    h = pp.shape[1]

    def body(p_ref, out_ref, send_sems, recv_sems):
        x, y, c = _place()
        cps = [pltpu.make_async_remote_copy(
            src_ref=p_ref.at[ci], dst_ref=out_ref.at[j], send_sem=send_sems.at[j], recv_sem=recv_sems.at[j],
            device_id=(cx, cy, c), device_id_type=MESH) for j, (cx, cy, ci) in enumerate(_other_chips(x, y))]
        for cp in cps:
            cp.start()
        for cp in cps:
            cp.wait()

    return pl.pallas_call(
        body, name="rs_chips", in_specs=[ANY], out_specs=ANY,
        out_shape=jax.ShapeDtypeStruct((3, h, PACK_W), pp.dtype),
        scratch_shapes=[pltpu.SemaphoreType.DMA((3,)), pltpu.SemaphoreType.DMA((3,))],
    )(pp)


def _join_halves(s):
    h = s.shape[0]

    def body(s_ref, out_ref, send_sem, recv_sem, local_sem):
        x, y, c = _place()
        mine = pltpu.make_async_copy(s_ref, out_ref.at[c], local_sem)
        mine.start()
        cp = pltpu.make_async_remote_copy(
            src_ref=s_ref, dst_ref=out_ref.at[c], send_sem=send_sem, recv_sem=recv_sem,
            device_id=(x, y, 1 - c), device_id_type=MESH)
        cp.start()
        cp.wait()
        mine.wait()

    return pl.pallas_call(
        body, name="rs_join", in_specs=[ANY], out_specs=ANY,
        out_shape=jax.ShapeDtypeStruct((2, h, PACK_W), s.dtype),
        scratch_shapes=[pltpu.SemaphoreType.DMA(()), pltpu.SemaphoreType.DMA(()), pltpu.SemaphoreType.DMA(())],
    )(s)


def _allreduce_small(sp):
    def body(s_ref, out_ref, slots, send_sems, recv_sems):
        x, y, c = _place()
        me = 4 * x + 2 * y + c
        slots[me] = s_ref[...]
        cps = []
        for r in range(1, N_DEV):
            px, py, pc = x ^ (r >> 2), y ^ ((r >> 1) & 1), c ^ (r & 1)
            cps.append(pltpu.make_async_remote_copy(
                src_ref=s_ref, dst_ref=slots.at[me], send_sem=send_sems.at[r - 1], recv_sem=recv_sems.at[r - 1],
                device_id=(px, py, pc), device_id_type=MESH))
        for cp in cps:
            cp.start()
        for r in range(1, N_DEV):
            px, py, pc = x ^ (r >> 2), y ^ ((r >> 1) & 1), c ^ (r & 1)
            pltpu.make_async_remote_copy(
                src_ref=s_ref, dst_ref=slots.at[4 * px + 2 * py + pc], send_sem=send_sems.at[r - 1],
                recv_sem=recv_sems.at[r - 1], device_id=(px, py, pc), device_id_type=MESH).wait_recv()
        for cp in cps:
            cp.wait_send()
        acc = slots[0]
        for d in range(1, N_DEV):
            acc = acc + slots[d]
        out_ref[...] = acc

    return pl.pallas_call(
        body, name="allreduce_small",
        in_specs=[pl.BlockSpec(memory_space=pltpu.VMEM)], out_specs=pl.BlockSpec(memory_space=pltpu.VMEM),
        out_shape=jax.ShapeDtypeStruct(sp.shape, F32),
        scratch_shapes=[pltpu.VMEM((N_DEV,) + sp.shape, F32), pltpu.SemaphoreType.DMA((N_DEV - 1,)),
                        pltpu.SemaphoreType.DMA((N_DEV - 1,))],
    )(sp)


def _add_call(name, terms, out_dtypes, tm=448):
    R = terms[0].shape[0]

    def fn(accs, tv, rv, cv):
        s = tv[0].astype(F32)
        for t in tv[1:]:
            s = s + t.astype(F32)
        return [s] * len(out_dtypes)

    return _tile_call(name, fn, R, PACK_W, _pick(R, tm, 16), PACK_W, tiles=[(t, 0) for t in terms],
                      outs=[(dt, "tile") for dt in out_dtypes])


def _shard_shapes(w):
    return {n: w[n].shape[1:] for n in BIG}


def _pack_shard(arrs):
    return jnp.concatenate([a.reshape(-1, PACK_W) for a in arrs], axis=0)


def _to_shards(name, g):
    if name in COL_SHARDED:
        r, cdim = g.shape
        return g.reshape(r, N_CHIPS, cdim // N_CHIPS).transpose(1, 0, 2)
    return g.reshape(N_CHIPS, g.shape[0] // N_CHIPS, g.shape[1])


def _from_shards(name, s):
    if name in COL_SHARDED:
        return s.transpose(1, 0, 2).reshape(s.shape[1], -1)
    return s.reshape(-1, s.shape[2])


def _adamw_vals(w, g, m, v):
    m = ADAM_B1 * m + (1.0 - ADAM_B1) * g
    v = ADAM_B2 * v + (1.0 - ADAM_B2) * (g * g)
    m_hat = m / (1.0 - ADAM_B1 ** ADAM_STEP)
    v_hat = v / (1.0 - ADAM_B2 ** ADAM_STEP)
    delta = -ADAM_LR * (m_hat / (jnp.sqrt(v_hat) + ADAM_EPS) + ADAM_WD * w)
    return delta, m, v


def _adamw(name, w, g, m, v):
    R, W = w.shape

    def fn(accs, tv, rv, cv):
        return list(_adamw_vals(*tv))

    return _tile_call(name, fn, R, W, _pick(R, 256), W, tiles=[(w, 0), (g, 0), (m, 0), (v, 0)],
                      outs=[(F32, "tile")] * 3)


SMALL_LAYOUT = (("rel_bias", 2, 256), ("lb_param", 8, 1024), ("norm_ffn1", 8, 1024), ("norm_mix", 8, 1024),
                ("attn_sinks", 1, 8), ("rec_norm", 1, 128), ("norm_ffn2", 8, 1024), ("norm_ple", 8, 1024),
                ("norm_final", 8, 1024), ("loss", 8, 1024))


def _pack_small(vals):
    rows = []
    for name, nrows, n in SMALL_LAYOUT:
        flat = vals[name].reshape(-1)
        flat = jnp.pad(flat, (0, nrows * 128 - n))
        rows.append(flat.reshape(nrows, 128))
    packed = jnp.concatenate(rows, axis=0)
    return jnp.pad(packed, ((0, SMALL_ROWS - packed.shape[0]), (0, 0)))


def _unpack_small(packed, shapes):
    out, r = {}, 0
    for name, nrows, n in SMALL_LAYOUT:
        out[name] = packed[r:r + nrows].reshape(-1)[:n].reshape(shapes[name])
        r += nrows
    return out


def kernel(x, p, rel_bias, lb_param, norm_ffn1, w_ffn1_in, w_ffn1_out, norm_mix, w_in, attn_sinks, rec_norm, w_att_proj, w_rec_proj, w_out, norm_ffn2, w_ffn2_in, w_ffn2_out, norm_ple, w_ple_gate, w_ple_proj, norm_final, loss_target, m_rel_bias, m_lb_param, m_norm_ffn1, m_w_ffn1_in, m_w_ffn1_out, m_norm_mix, m_w_in, m_attn_sinks, m_rec_norm, m_w_att_proj, m_w_rec_proj, m_w_out, m_norm_ffn2, m_w_ffn2_in, m_w_ffn2_out, m_norm_ple, m_w_ple_gate, m_w_ple_proj, m_norm_final, v_rel_bias, v_lb_param, v_norm_ffn1, v_w_ffn1_in, v_w_ffn1_out, v_norm_mix, v_w_in, v_attn_sinks, v_rec_norm, v_w_att_proj, v_w_rec_proj, v_w_out, v_norm_ffn2, v_w_ffn2_in, v_w_ffn2_out, v_norm_ple, v_w_ple_gate, v_w_ple_proj, v_norm_final):
    args = dict(locals())
    wsh = {n: args[n] for n in WEIGHTS}
    B, S = x.shape[0], x.shape[1]
    T = B * S
    cx, cy, cc = _place()
    me_chip = 2 * cx + cy

    shard_shapes = _shard_shapes(wsh)
    wp = _pack_shard([wsh[n][0].astype(BF16) for n in BIG])
    R = wp.shape[0]
    gathered = _gather_weights(wp)
    wfull, r0 = {}, 0
    for n in BIG:
        shp = shard_shapes[n]
        nr = shp[0] * shp[1] // PACK_W
        wfull[n] = _from_shards(n, gathered[:, r0:r0 + nr].reshape((N_CHIPS,) + shp))
        r0 += nr
    for n in SMALL:
        wfull[n] = wsh[n]

    loss_p, dx, grads = _local_step(x.reshape(T, D_MODEL), p.reshape(T, PLE_DIM),
                                    loss_target.reshape(T, D_MODEL), wfull, B, S)

    gp = jnp.stack([_pack_shard([_to_shards(n, grads[n])[k] for n in BIG]) for k in range(N_CHIPS)])
    half = R // 2
    from_sibling = _swap_halves(gp.astype(BF16))
    mine = lax.dynamic_slice_in_dim(gp, cc * half, half, axis=1)
    chip_f32, chip_bf = _add_call("rs_add_sibling", [mine.reshape(-1, PACK_W), from_sibling.reshape(-1, PACK_W)],
                                  [F32, BF16])
    chip_f32 = chip_f32.reshape(N_CHIPS, half, PACK_W)
    from_chips = _scatter_chips(chip_bf.reshape(N_CHIPS, half, PACK_W))
    own = lax.dynamic_index_in_dim(chip_f32, me_chip, axis=0, keepdims=False)
    s_half = _add_call("rs_add_chips", [own, from_chips[0], from_chips[1], from_chips[2]], [F32])[0]
    gsh = _join_halves(s_half).reshape(R, PACK_W)

    small_vals = {
        "rel_bias": grads["rel_bias"].T,
        "lb_param": jnp.concatenate([_colsum("dlb_sum", grads["lb_param"]),
                                     -_colsum("dlb_sum2", grads["lb_param"])], axis=0) / 8.0,
        "attn_sinks": grads["attn_sinks"][:, 0],
        "rec_norm": _colsum("drn_sum", grads["rec_norm"]).reshape(REC_HEADS, REC_DIM).sum(axis=0),
        "loss": _colsum("loss_sum", loss_p),
    }
    for n in ("norm_ffn1", "norm_mix", "norm_ffn2", "norm_ple", "norm_final"):
        small_vals[n] = _colsum(n + "_sum", grads[n])
    red = _allreduce_small(_pack_small(small_vals))
    small_shapes = {n: wsh[n].shape for n in SMALL}
    small_shapes["loss"] = (D_MODEL,)
    small = _unpack_small(red, small_shapes)
    loss = 0.5 * jnp.sum(small["loss"]) / D_MODEL

    out_g, out_d, out_m, out_v = {}, {}, {}, {}
    r0 = 0
    for n in BIG:
        shp = shard_shapes[n]
        nr = shp[0] * shp[1] // PACK_W
        g = gsh[r0:r0 + nr].reshape(shp)
        r0 += nr
        d, nm, nv = _adamw("adamw_" + n, wsh[n][0], g, args["m_" + n][0], args["v_" + n][0])
        out_g[n], out_d[n], out_m[n], out_v[n] = g[None], d[None], nm[None], nv[None]
    sw = _pack_small({**{n: wsh[n] for n in SMALL}, "loss": jnp.zeros((D_MODEL,), F32)})
    sm = _pack_small({**{n: args["m_" + n] for n in SMALL}, "loss": jnp.zeros((D_MODEL,), F32)})
    sv = _pack_small({**{n: args["v_" + n] for n in SMALL}, "loss": jnp.ones((D_MODEL,), F32)})
    sd, snm, snv = _adamw("adamw_small", sw, red, sm, sv)
    ud, um, uv = (_unpack_small(t, small_shapes) for t in (sd, snm, snv))
    for n in SMALL:
        out_g[n], out_d[n], out_m[n], out_v[n] = small[n], ud[n], um[n], uv[n]

    return (loss, dx.reshape(B, S, D_MODEL), *[out_g[n] for n in WEIGHTS], *[out_d[n] for n in WEIGHTS],
            *[out_m[n] for n in WEIGHTS], *[out_v[n] for n in WEIGHTS])
```

```python
import functools

import numpy as np
import jax
import jax.numpy as jnp
from jax import lax
from jax.experimental import pallas as pl
from jax.experimental.pallas import tpu as pltpu

F32 = jnp.float32
BF16 = jnp.bfloat16
MESH = pl.DeviceIdType.MESH

D_MODEL = 1024
D_FF = 2816
HEAD_DIM = 64
N_Q_HEADS = 8
ATT_BLOCK = 128
N_BUCKETS = 32
MAX_DISTANCE = 128
REC_HEADS = 4
REC_DIM = 128
PLE_DIM = 256
EPS = 1e-6
IN_W = 4864
COL_AQ, COL_AK, COL_AV, COL_RQ, COL_RF, COL_RI, COL_RG, COL_GA, COL_GB = 0, 4, 5, 6, 10, 14, 18, 22, 30

CHUNK = 64
SUB = 16
N_SUB = CHUNK // SUB

ADAM_LR, ADAM_B1, ADAM_B2, ADAM_EPS, ADAM_WD, ADAM_STEP = 0.001, 0.9, 0.999, 1e-08, 0.01, 10

V7X_VMEM_LIMIT = 56 * 1024 * 1024
N_CHIPS = 4
N_DEV = 8

BIG = ("w_ffn1_in", "w_ffn1_out", "w_in", "w_att_proj", "w_rec_proj", "w_out",
       "w_ffn2_in", "w_ffn2_out", "w_ple_gate", "w_ple_proj")
COL_SHARDED = ("w_ffn1_in", "w_in", "w_att_proj", "w_rec_proj", "w_ffn2_in", "w_ple_proj")
WEIGHTS = ("rel_bias", "lb_param", "norm_ffn1", "w_ffn1_in", "w_ffn1_out", "norm_mix", "w_in", "attn_sinks",
           "rec_norm", "w_att_proj", "w_rec_proj", "w_out", "norm_ffn2", "w_ffn2_in", "w_ffn2_out", "norm_ple",
           "w_ple_gate", "w_ple_proj", "norm_final")
SMALL = tuple(n for n in WEIGHTS if n not in BIG)
PACK_W = 1024
SMALL_ROWS = 64


def _params(*sem):
    return pltpu.CompilerParams(dimension_semantics=sem, vmem_limit_bytes=V7X_VMEM_LIMIT)


def _pick(n, cap, mult=8):
    if n <= cap:
        return n
    for t in range(cap - cap % mult, 0, -mult):
        if n % t == 0:
            return t
    raise ValueError((n, cap, mult))


def _dot(a, b):
    return jnp.dot(a, b, preferred_element_type=F32)


def _dot_nt(a, b):
    return lax.dot_general(a, b, (((1,), (1,)), ((), ())), preferred_element_type=F32)


def _dot_tn(a, b):
    return lax.dot_general(a, b, (((0,), (0,)), ((), ())), preferred_element_type=F32)


def _split3(x):
    hi = x.astype(BF16)
    r = x - hi.astype(F32)
    mid = r.astype(BF16)
    lo = (r - mid.astype(F32)).astype(BF16)
    return hi, mid, lo


def _sel_left(sel_bf16, x):
    hi, mid, lo = _split3(x)
    return _dot(sel_bf16, hi) + _dot(sel_bf16, mid) + _dot(sel_bf16, lo)


def _sel_right(x, sel_bf16):
    hi, mid, lo = _split3(x)
    return _dot(hi, sel_bf16) + _dot(mid, sel_bf16) + _dot(lo, sel_bf16)


def _sigmoid(x):
    return 1.0 / (1.0 + jnp.exp(-x))


def _group8(x):
    r, w = x.shape
    return x.reshape(r // 8, 8, w).sum(axis=0)


def _tile_call(name, fn, M, N, tm, tn, *, pairs=(), tiles=(), rows=(), consts=(), outs=(), parts=0):
    gi, gj = M // tm, N // tn
    assert gi * tm == M and gj * tn == N, (name, M, N, tm, tn)
    arrays, in_specs = [], []
    for a, a_col, b, kind in pairs:
        K = b.shape[0] if kind == "nn" else b.shape[1]
        arrays += [a, b]
        in_specs.append(pl.BlockSpec((tm, K), lambda i, j, c=a_col: (i, c)))
        if kind == "nn":
            in_specs.append(pl.BlockSpec((K, tn), lambda i, j: (0, j)))
        else:
            in_specs.append(pl.BlockSpec((tn, K), lambda i, j: (j, 0)))
    for arr, off in tiles:
        arrays.append(arr)
        in_specs.append(pl.BlockSpec((tm, tn), lambda i, j, o=off: (i, j + o)))
    for arr, bw, col in rows:
        arrays.append(arr)
        in_specs.append(pl.BlockSpec((tm, bw), lambda i, j, c=col: (i, c)))
    for arr in consts:
        arrays.append(arr)
        in_specs.append(pl.BlockSpec(arr.shape, lambda i, j: (0, 0)))
    out_shape, out_specs = [], []
    for dt, kind in outs:
        if kind == "tile":
            out_shape.append(jax.ShapeDtypeStruct((M, N), dt))
            out_specs.append(pl.BlockSpec((tm, tn), lambda i, j: (i, j)))
        else:
            assert gj == 1
            out_shape.append(jax.ShapeDtypeStruct((M, kind[1]), dt))
            out_specs.append(pl.BlockSpec((tm, kind[1]), lambda i, j: (i, 0)))
    for _ in range(parts):
        out_shape.append(jax.ShapeDtypeStruct((gi * 8, N), F32))
        out_specs.append(pl.BlockSpec((8, tn), lambda i, j: (i, j)))
    n_pairs, n_tiles, n_rows, n_consts = len(pairs), len(tiles), len(rows), len(consts)
    kinds = [p[3] for p in pairs]

    def body(*refs):
        pos = 0
        accs = []
        for kind in kinds:
            a_ref, b_ref = refs[pos], refs[pos + 1]
            pos += 2
            accs.append(_dot(a_ref[...], b_ref[...]) if kind == "nn" else _dot_nt(a_ref[...], b_ref[...]))
        tv = [r[...] for r in refs[pos:pos + n_tiles]]
        pos += n_tiles
        rv = [r[...] for r in refs[pos:pos + n_rows]]
        pos += n_rows
        cv = [r[...] for r in refs[pos:pos + n_consts]]
        pos += n_consts
        vals = fn(accs, tv, rv, cv)
        out_refs = refs[pos:]
        assert len(vals) == len(out_refs), (name, len(vals), len(out_refs))
        for o_ref, val in zip(out_refs, vals):
            o_ref[...] = val.astype(o_ref.dtype)

    res = pl.pallas_call(
        body, name=name, grid=(gi, gj), in_specs=in_specs, out_specs=out_specs, out_shape=out_shape,
        compiler_params=_params("arbitrary", "arbitrary"))(*arrays)
    return res


def _mm_tn(name, a, b, *, a_col=0, a_w=None, b_col=0, b_w=None, tm=1024, tn=512, tk=2048):
    T = a.shape[0]
    a_w = a.shape[1] if a_w is None else a_w
    b_w = b.shape[1] if b_w is None else b_w
    tm, tn, tk = _pick(a_w, tm, 128), _pick(b_w, tn, 128), _pick(T, tk, 128)
    a_off, b_off = a_col // tm, b_col // tn
    assert a_off * tm == a_col and b_off * tn == b_col
    nk = T // tk

    def body(a_ref, b_ref, o_ref, acc_ref):
        k = pl.program_id(2)

        @pl.when(k == 0)
        def _():
            acc_ref[...] = jnp.zeros_like(acc_ref)

        acc_ref[...] += _dot_tn(a_ref[...], b_ref[...])

        @pl.when(k == nk - 1)
        def _():
            o_ref[...] = acc_ref[...]

    return pl.pallas_call(
        body, name=name, grid=(a_w // tm, b_w // tn, nk),
        in_specs=[pl.BlockSpec((tk, tm), lambda i, j, k: (k, i + a_off)),
                  pl.BlockSpec((tk, tn), lambda i, j, k: (k, j + b_off))],
        out_specs=pl.BlockSpec((tm, tn), lambda i, j, k: (i, j)),
        out_shape=jax.ShapeDtypeStruct((a_w, b_w), F32),
        scratch_shapes=[pltpu.VMEM((tm, tn), F32)],
        compiler_params=_params("arbitrary", "arbitrary", "arbitrary"))(a, b)


def _colsum(name, x):
    def body(x_ref, o_ref):
        o_ref[...] = jnp.sum(x_ref[...], axis=0, keepdims=True)
    return pl.pallas_call(body, name=name, out_shape=jax.ShapeDtypeStruct((1, x.shape[1]), F32))(x)


def _rms_hat(h):
    return h * lax.rsqrt(jnp.mean(h * h, axis=-1, keepdims=True) + EPS)


def _rms_bwd_vals(dn, h, g):
    r = lax.rsqrt(jnp.mean(h * h, axis=-1, keepdims=True) + EPS)
    nh = h * r
    gd = dn * g
    dh = r * (gd - nh * jnp.mean(gd * nh, axis=-1, keepdims=True))
    return dh, _group8(dn * nh)


def _rms_fwd(name, h, g, tm=512):
    T = h.shape[0]

    def fn(accs, tv, rv, cv):
        return [_rms_hat(tv[0]) * cv[0]]

    return _tile_call(name, fn, T, D_MODEL, _pick(T, tm), D_MODEL, tiles=[(h, 0)], consts=[g],
                      outs=[(BF16, "tile")])[0]


def _ffn_fwd(tag, h, g, w_in, w_out):
    T = h.shape[0]
    n = _rms_fwd(tag + "_norm", h, g)
    tm = _pick(T, 1024)

    def act(accs, tv, rv, cv):
        gate, up = accs
        a = gate * _sigmoid(gate) * up
        return [gate, up, a]

    gate, up, a = _tile_call(
        tag + "_in", act, T, D_FF, tm, 256,
        pairs=[(n, 0, w_in[:, :D_FF], "nn"), (n, 0, w_in[:, D_FF:], "nn")],
        outs=[(BF16, "tile")] * 3)

    def res(accs, tv, rv, cv):
        return [tv[0] + 0.5 * accs[0]]

    h_new = _tile_call(tag + "_out", res, T, D_MODEL, _pick(T, 512), 512,
                       pairs=[(a, 0, w_out, "nn")], tiles=[(h, 0)], outs=[(F32, "tile")])[0]
    return h_new, (n, gate, up, a)


def _ffn_bwd(tag, dh_out, h, g, w_in, w_out, saved):
    T = h.shape[0]
    n, gate, up, a = saved

    def half(accs, tv, rv, cv):
        return [0.5 * tv[0]]

    df = _tile_call(tag + "_df", half, T, D_MODEL, _pick(T, 512), D_MODEL, tiles=[(dh_out, 0)],
                    outs=[(BF16, "tile")])[0]

    def dact(accs, tv, rv, cv):
        da = accs[0]
        gt, u = tv[0].astype(F32), tv[1].astype(F32)
        sg = _sigmoid(gt)
        silu = gt * sg
        dgate = da * u * (sg + silu * (1.0 - sg))
        dup = da * silu
        return [dgate, dup]

    dzg, dzu = _tile_call(tag + "_dact", dact, T, D_FF, _pick(T, 1024), 256,
                          pairs=[(df, 0, w_out, "nt")], tiles=[(gate, 0), (up, 0)],
                          outs=[(BF16, "tile")] * 2)
    dw_out = _mm_tn(tag + "_dwout", a, df)
    dw_in = jnp.concatenate([_mm_tn(tag + "_dwin_g", n, dzg), _mm_tn(tag + "_dwin_u", n, dzu)], axis=1)

    def dnorm(accs, tv, rv, cv):
        dh, dg = _rms_bwd_vals(accs[0] + accs[1], tv[0], cv[0])
        return [tv[1] + dh, dg]

    dh, dg = _tile_call(tag + "_dnorm", dnorm, T, D_MODEL, _pick(T, 256), D_MODEL,
                        pairs=[(dzg, 0, w_in[:, :D_FF], "nt"), (dzu, 0, w_in[:, D_FF:], "nt")],
                        tiles=[(h, 0), (dh_out, 0)], consts=[g], outs=[(F32, "tile")], parts=1)
    return dh, dg, dw_in, dw_out


def _t5_onehot():
    qi = np.arange(ATT_BLOCK)[:, None] + ATT_BLOCK
    kj = np.arange(2 * ATT_BLOCK)[None, :]
    nn = np.maximum(qi - kj, 0)
    max_exact = N_BUCKETS // 2
    large = max_exact + (np.log(np.maximum(nn, 1) / max_exact) / np.log(MAX_DISTANCE / max_exact)
                         * (N_BUCKETS - max_exact)).astype(np.int32)
    large = np.minimum(large, N_BUCKETS - 1)
    bucket = np.where(nn < max_exact, nn, large).astype(np.int32).reshape(-1)
    return (bucket[None, :] == np.arange(N_BUCKETS)[:, None]).astype(np.float32)


def _small_mm(name, a, b, sel):
    def body(a_ref, b_ref, o_ref):
        if sel == "right":
            o_ref[...] = _sel_right(a_ref[...], b_ref[...])
        else:
            o_ref[...] = _sel_left(a_ref[...], b_ref[...])
    return pl.pallas_call(body, name=name, out_shape=jax.ShapeDtypeStruct((a.shape[0], b.shape[1]), F32),
                          compiler_params=pltpu.CompilerParams(vmem_limit_bytes=V7X_VMEM_LIMIT))(a, b)


def _kv_layouts(proj):
    k = proj[:, COL_AK * 128:(COL_AK + 1) * 128]
    v = proj[:, COL_AV * 128:(COL_AV + 1) * 128]
    sw = lambda t: jnp.concatenate([t[:, HEAD_DIM:], t[:, :HEAD_DIM]], axis=1)
    return (jnp.concatenate([k, sw(k)], axis=1).astype(BF16), jnp.concatenate([v, sw(v)], axis=1).astype(BF16))


def _swa_masks():
    row = lax.broadcasted_iota(jnp.int32, (ATT_BLOCK, 2 * ATT_BLOCK), 0)
    col = lax.broadcasted_iota(jnp.int32, (ATT_BLOCK, 2 * ATT_BLOCK), 1)
    dist = ATT_BLOCK + row - col
    return (dist >= 0) & (dist < ATT_BLOCK), col


def _swa_heads():
    out = []
    for h in range(N_Q_HEADS):
        lo = h % 2 == 0
        hk = h // 4
        swapped = (hk == 1) if lo else (hk == 0)
        out.append((h // 2, lo, swapped))
    return out


def _swa_probs(qm, kk, bias_h, sink, valid):
    s = _dot_nt(qm, kk) * (HEAD_DIM ** -0.5) + bias_h
    s = jnp.where(valid, s, -jnp.inf)
    m = jnp.maximum(jnp.max(s, axis=-1, keepdims=True), sink)
    e = jnp.exp(s - m)
    es = jnp.exp(sink - m)
    den = jnp.sum(e, axis=-1, keepdims=True) + es
    return e / den, es / den


def _swa_fwd(proj, kk2, vv2, bias, sinks, B, S):
    T = B * S
    nb = S // ATT_BLOCK

    def body(q_ref, k_ref, v_ref, bias_ref, sink_ref, o_ref, kpad, vpad):
        zeros = jnp.zeros((ATT_BLOCK, 256), BF16)
        kpad[pl.ds(0, ATT_BLOCK), :] = zeros
        vpad[pl.ds(0, ATT_BLOCK), :] = zeros
        kpad[pl.ds(ATT_BLOCK, S), :] = k_ref[...]
        vpad[pl.ds(ATT_BLOCK, S), :] = v_ref[...]
        valid0, col = _swa_masks()
        lane = lax.broadcasted_iota(jnp.int32, (1, 128), 1)
        lo_q = lane < HEAD_DIM
        heads = _swa_heads()

        def blk(n, carry):
            r0 = pl.multiple_of(n * ATT_BLOCK, ATT_BLOCK)
            valid = valid0 & ((n > 0) | (col >= ATT_BLOCK))
            kb = kpad[pl.ds(r0, 2 * ATT_BLOCK), :]
            vb = vpad[pl.ds(r0, 2 * ATT_BLOCK), :]
            for j in range(N_Q_HEADS // 2):
                qblk = q_ref[pl.ds(r0, ATT_BLOCK), pl.ds(128 * j, 128)].astype(BF16)
                acc = jnp.zeros((ATT_BLOCK, 128), F32)
                for h in (2 * j, 2 * j + 1):
                    _, lo, swapped = heads[h]
                    keep = lo_q if lo else ~lo_q
                    qm = jnp.where(keep, qblk, jnp.zeros_like(qblk))
                    kk = kb[:, 128:] if swapped else kb[:, :128]
                    vv = vb[:, 128:] if swapped else vb[:, :128]
                    vm = jnp.where(keep, vv, jnp.zeros_like(vv))
                    p, _ = _swa_probs(qm, kk, bias_ref[h], sink_ref[h], valid)
                    acc = acc + _dot(p.astype(BF16), vm)
                o_ref[pl.ds(r0, ATT_BLOCK), pl.ds(128 * j, 128)] = acc.astype(o_ref.dtype)
            return carry

        lax.fori_loop(0, nb, blk, 0)

    return pl.pallas_call(
        body, name="swa_fwd", grid=(B,),
        in_specs=[pl.BlockSpec((S, 512), lambda b: (b, 0)),
                  pl.BlockSpec((S, 256), lambda b: (b, 0)),
                  pl.BlockSpec((S, 256), lambda b: (b, 0)),
                  pl.BlockSpec((N_Q_HEADS, ATT_BLOCK, 2 * ATT_BLOCK), lambda b: (0, 0, 0)),
                  pl.BlockSpec(memory_space=pltpu.SMEM)],
        out_specs=pl.BlockSpec((S, 512), lambda b: (b, 0)),
        out_shape=jax.ShapeDtypeStruct((T, 512), BF16),
        scratch_shapes=[pltpu.VMEM((S + ATT_BLOCK, 256), BF16), pltpu.VMEM((S + ATT_BLOCK, 256), BF16)],
        compiler_params=_params("arbitrary"))(proj, kk2, vv2, bias, sinks)


def _swa_bwd(proj, kk2, vv2, bias, sinks, datt, B, S):
    T = B * S
    nb = S // ATT_BLOCK

    def body(q_ref, k_ref, v_ref, bias_ref, sink_ref, do_ref, dq_ref, dk_ref, dv_ref, dbias_ref, dsink_ref,
             kpad, vpad, dkpad, dvpad):
        b = pl.program_id(0)

        @pl.when(b == 0)
        def _():
            dbias_ref[...] = jnp.zeros_like(dbias_ref)
            dsink_ref[...] = jnp.zeros_like(dsink_ref)

        zeros = jnp.zeros((ATT_BLOCK, 256), BF16)
        kpad[pl.ds(0, ATT_BLOCK), :] = zeros
        vpad[pl.ds(0, ATT_BLOCK), :] = zeros
        kpad[pl.ds(ATT_BLOCK, S), :] = k_ref[...]
        vpad[pl.ds(ATT_BLOCK, S), :] = v_ref[...]
        dkpad[...] = jnp.zeros_like(dkpad)
        dvpad[...] = jnp.zeros_like(dvpad)
        valid0, col = _swa_masks()
        lane = lax.broadcasted_iota(jnp.int32, (1, 128), 1)
        lo_q = lane < HEAD_DIM
        heads = _swa_heads()
        scale = HEAD_DIM ** -0.5

        def blk(n, carry):
            r0 = pl.multiple_of(n * ATT_BLOCK, ATT_BLOCK)
            valid = valid0 & ((n > 0) | (col >= ATT_BLOCK))
            kb = kpad[pl.ds(r0, 2 * ATT_BLOCK), :]
            vb = vpad[pl.ds(r0, 2 * ATT_BLOCK), :]
            dk_acc = [jnp.zeros((2 * ATT_BLOCK, 128), F32), jnp.zeros((2 * ATT_BLOCK, 128), F32)]
            dv_acc = [jnp.zeros((2 * ATT_BLOCK, 128), F32), jnp.zeros((2 * ATT_BLOCK, 128), F32)]
            for j in range(N_Q_HEADS // 2):
                qblk = q_ref[pl.ds(r0, ATT_BLOCK), pl.ds(128 * j, 128)].astype(BF16)
                doblk = do_ref[pl.ds(r0, ATT_BLOCK), pl.ds(128 * j, 128)]
                dq = jnp.zeros((ATT_BLOCK, 128), F32)
                for h in (2 * j, 2 * j + 1):
                    _, lo, swapped = heads[h]
                    keep = lo_q if lo else ~lo_q
                    qm = jnp.where(keep, qblk, jnp.zeros_like(qblk))
                    dom = jnp.where(keep, doblk, jnp.zeros_like(doblk))
                    kk = kb[:, 128:] if swapped else kb[:, :128]
                    vv = vb[:, 128:] if swapped else vb[:, :128]
                    km = jnp.where(keep, kk, jnp.zeros_like(kk))
                    p, ps = _swa_probs(qm, kk, bias_ref[h], sink_ref[h], valid)
                    dp = _dot_nt(dom, vv)
                    delta = jnp.sum(p * dp, axis=-1, keepdims=True)
                    ds = p * (dp - delta)
                    dbias_ref[h] += ds
                    dsink_ref[pl.ds(h, 1), :] += -jnp.sum(jnp.broadcast_to(ps * delta, (ATT_BLOCK, 128)),
                                                          axis=0, keepdims=True)
                    dsb = (ds * scale).astype(BF16)
                    dq = dq + _dot(dsb, km)
                    idx = 1 if swapped else 0
                    dk_acc[idx] = dk_acc[idx] + _dot_tn(dsb, qm)
                    dv_acc[idx] = dv_acc[idx] + _dot_tn(p.astype(BF16), dom)
                dq_ref[pl.ds(r0, ATT_BLOCK), pl.ds(128 * j, 128)] = dq.astype(dq_ref.dtype)
            dkpad[pl.ds(r0, 2 * ATT_BLOCK), :] += jnp.concatenate(dk_acc, axis=1)
            dvpad[pl.ds(r0, 2 * ATT_BLOCK), :] += jnp.concatenate(dv_acc, axis=1)
            return carry

        lax.fori_loop(0, nb, blk, 0)
        dk_ref[...] = dkpad[pl.ds(ATT_BLOCK, S), :]
        dv_ref[...] = dvpad[pl.ds(ATT_BLOCK, S), :]

    return pl.pallas_call(
        body, name="swa_bwd", grid=(B,),
        in_specs=[pl.BlockSpec((S, 512), lambda b: (b, 0)),
                  pl.BlockSpec((S, 256), lambda b: (b, 0)),
                  pl.BlockSpec((S, 256), lambda b: (b, 0)),
                  pl.BlockSpec((N_Q_HEADS, ATT_BLOCK, 2 * ATT_BLOCK), lambda b: (0, 0, 0)),
                  pl.BlockSpec(memory_space=pltpu.SMEM),
                  pl.BlockSpec((S, 512), lambda b: (b, 0))],
        out_specs=[pl.BlockSpec((S, 512), lambda b: (b, 0)),
                   pl.BlockSpec((S, 256), lambda b: (b, 0)),
                   pl.BlockSpec((S, 256), lambda b: (b, 0)),
                   pl.BlockSpec((N_Q_HEADS, ATT_BLOCK, 2 * ATT_BLOCK), lambda b: (0, 0, 0)),
                   pl.BlockSpec((N_Q_HEADS, 128), lambda b: (0, 0))],
        out_shape=[jax.ShapeDtypeStruct((T, 512), BF16),
                   jax.ShapeDtypeStruct((T, 256), F32),
                   jax.ShapeDtypeStruct((T, 256), F32),
                   jax.ShapeDtypeStruct((N_Q_HEADS, ATT_BLOCK, 2 * ATT_BLOCK), F32),
                   jax.ShapeDtypeStruct((N_Q_HEADS, 128), F32)],
        scratch_shapes=[pltpu.VMEM((S + ATT_BLOCK, 256), BF16), pltpu.VMEM((S + ATT_BLOCK, 256), BF16),
                        pltpu.VMEM((S + ATT_BLOCK, 256), F32), pltpu.VMEM((S + ATT_BLOCK, 256), F32)],
        compiler_params=_params("arbitrary"))(proj, kk2, vv2, bias, sinks, datt)


def _hgrn_gates(z, lb):
    sg = _sigmoid(z)
    f = lb + (1.0 - lb) * sg
    return sg, f, jnp.log(f), 1.0 - f


def _hgrn_consts():
    r = lax.broadcasted_iota(jnp.int32, (CHUNK, CHUNK), 0)
    c = lax.broadcasted_iota(jnp.int32, (CHUNK, CHUNK), 1)
    tril = (r >= c).astype(BF16)
    triu = (r <= c).astype(BF16)
    causal = r >= c
    below = (r // SUB) > (c // SUB)
    return tril, triu, causal, below, r, c


def _hgrn_offdiag(q, k, b_ref):
    zero = jnp.zeros((SUB, REC_DIM), F32)
    q_rows, k_cols, eqs, eks = [jnp.zeros((SUB, (N_SUB - 1) * REC_DIM), F32)], [], [], []
    for i in range(1, N_SUB):
        p = b_ref[pl.ds(SUB * i - 1, 1), :]
        eq = jnp.exp(b_ref[pl.ds(SUB * i, SUB), :] - p)
        qi = q[SUB * i:SUB * (i + 1), :] * eq
        q_rows.append(jnp.concatenate([zero] * (i - 1) + [qi] + [zero] * (N_SUB - 1 - i), axis=1))
        ek = jnp.exp(p - b_ref[pl.ds(0, SUB * i), :])
        ki = k[:SUB * i, :] * ek
        pad = jnp.zeros((CHUNK - SUB * i, REC_DIM), F32)
        k_cols.append(jnp.concatenate([ki, pad], axis=0))
        eqs.append(eq)
        eks.append(jnp.concatenate([ek, pad], axis=0))
    return jnp.concatenate(q_rows, axis=0), jnp.concatenate(k_cols, axis=1), eqs, eks


def _hgrn_diag(q, k_ref, b_ref):
    lane = lax.broadcasted_iota(jnp.int32, (SUB, CHUNK), 1)
    rowm = lax.broadcasted_iota(jnp.int32, (SUB, CHUNK), 0)
    blocks = []
    for i in range(N_SUB):
        qi = q[SUB * i:SUB * (i + 1), :]
        bi = b_ref[pl.ds(SUB * i, SUB), :]
        d = jnp.zeros((SUB, CHUNK), F32)
        for s in range(SUB):
            ks = k_ref[pl.ds(SUB * i + s, 1), :]
            bs = b_ref[pl.ds(SUB * i + s, 1), :]
            w = jnp.exp(jnp.minimum(bi - bs, 0.0))
            colv = jnp.sum(qi * ks * w, axis=-1, keepdims=True)
            d = jnp.where((lane == SUB * i + s) & (rowm >= s), colv, d)
        blocks.append(d)
    return jnp.concatenate(blocks, axis=0)


def _hgrn_fwd(proj, lb_param, B, S):
    T = B * S
    nc = S // CHUNK

    def body(q_ref, z_ref, v_ref, lb_ref, o_ref, st_ref, k_s, b_s):
        lb = _sigmoid(lb_ref[0:1, :] - lb_ref[1:2, :])
        tril, _, _, below, _, _ = _hgrn_consts()

        def chunk(ci, ht):
            r0 = pl.multiple_of(ci * CHUNK, CHUNK)
            q = q_ref[pl.ds(r0, CHUNK), :]
            v = v_ref[pl.ds(r0, CHUNK), :]
            _, _, g, k = _hgrn_gates(z_ref[pl.ds(r0, CHUNK), :], lb)
            bcum = _sel_left(tril, g)
            k_s[...] = k
            b_s[...] = bcum
            st_ref[ci] = ht
            qst, kst, _, _ = _hgrn_offdiag(q, k, b_s)
            a = jnp.where(below, _dot_nt(qst.astype(BF16), kst.astype(BF16)), 0.0) + _hgrn_diag(q, k_s, b_s)
            vb = v.astype(BF16)
            qb = (q * jnp.exp(bcum)).astype(BF16)
            o = _dot(a.astype(BF16), vb) + _dot_nt(qb, ht.astype(BF16))
            o_ref[pl.ds(r0, CHUNK), :] = o
            b_last = b_s[pl.ds(CHUNK - 1, 1), :]
            kb = (k * jnp.exp(b_last - bcum)).astype(BF16)
            return ht * jnp.exp(b_last) + _dot_tn(vb, kb)

        lax.fori_loop(0, nc, chunk, jnp.zeros((REC_DIM, REC_DIM), F32))

    H = REC_HEADS
    return pl.pallas_call(
        body, name="hgrn_fwd", grid=(B, H),
        in_specs=[pl.BlockSpec((S, 128), lambda b, h: (b, COL_RQ + h)),
                  pl.BlockSpec((S, 128), lambda b, h: (b, COL_RF + h)),
                  pl.BlockSpec((S, 128), lambda b, h: (b, COL_RI + h)),
                  pl.BlockSpec((2, 128), lambda b, h: (0, h))],
        out_specs=[pl.BlockSpec((S, 128), lambda b, h: (b, h)),
                   pl.BlockSpec((nc, REC_DIM, REC_DIM), lambda b, h: (b * H + h, 0, 0))],
        out_shape=[jax.ShapeDtypeStruct((T, 512), F32),
                   jax.ShapeDtypeStruct((B * H * nc, REC_DIM, REC_DIM), F32)],
        scratch_shapes=[pltpu.VMEM((CHUNK, REC_DIM), F32), pltpu.VMEM((CHUNK, REC_DIM), F32)],
        compiler_params=_params("arbitrary", "arbitrary"))(proj, proj, proj, lb_param)


def _hgrn_bwd(proj, lb_param, states, do, B, S):
    T = B * S
    nc = S // CHUNK

    def body(q_ref, z_ref, v_ref, lb_ref, st_ref, do_ref, dq_ref, dz_ref, dv_ref, dlb_ref, k_s, b_s, dkd_s):
        lb = _sigmoid(lb_ref[0:1, :] - lb_ref[1:2, :])
        tril, triu, causal, below, r, _ = _hgrn_consts()
        lane = lax.broadcasted_iota(jnp.int32, (SUB, CHUNK), 1)
        rowm = lax.broadcasted_iota(jnp.int32, (SUB, CHUNK), 0)
        last_row = lax.broadcasted_iota(jnp.int32, (CHUNK, 1), 0) == CHUNK - 1

        def chunk(it, carry):
            dht, dlb = carry
            ci = nc - 1 - it
            r0 = pl.multiple_of(ci * CHUNK, CHUNK)
            q = q_ref[pl.ds(r0, CHUNK), :]
            v = v_ref[pl.ds(r0, CHUNK), :]
            dout = do_ref[pl.ds(r0, CHUNK), :]
            sg, f, g, k = _hgrn_gates(z_ref[pl.ds(r0, CHUNK), :], lb)
            bcum = _sel_left(tril, g)
            k_s[...] = k
            b_s[...] = bcum
            ht = st_ref[ci]
            qst, kst, eqs, eks = _hgrn_offdiag(q, k, b_s)
            qst_b, kst_b = qst.astype(BF16), kst.astype(BF16)
            a = jnp.where(below, _dot_nt(qst_b, kst_b), 0.0) + _hgrn_diag(q, k_s, b_s)
            vb, dob = v.astype(BF16), dout.astype(BF16)
            eb = jnp.exp(bcum)
            b_last = b_s[pl.ds(CHUNK - 1, 1), :]
            el = jnp.exp(b_last)
            ekb = jnp.exp(b_last - bcum)
            qb = (q * eb).astype(BF16)
            kb = k * ekb
            dhb = dht.astype(BF16)
            dv = _dot_tn(a.astype(BF16), dob) + _dot_nt(kb.astype(BF16), dhb)
            da = jnp.where(causal, _dot_nt(dob, vb), 0.0)
            dqb = _dot(dob, ht.astype(BF16))
            dkb = _dot(vb, dhb)
            dht_new = dht * el + _dot_tn(dob, qb)
            da_off = jnp.where(below, da, 0.0).astype(BF16)
            dqst = _dot(da_off, kst_b)
            dkst = _dot_tn(da_off, qst_b)
            dq_rows = [jnp.zeros((SUB, REC_DIM), F32)]
            dk = jnp.zeros((CHUNK, REC_DIM), F32)
            for i in range(1, N_SUB):
                dq_rows.append(dqst[SUB * i:SUB * (i + 1), REC_DIM * (i - 1):REC_DIM * i] * eqs[i - 1])
                dk = dk + dkst[:, REC_DIM * (i - 1):REC_DIM * i] * eks[i - 1]
            dq = jnp.concatenate(dq_rows, axis=0)
            dkd_s[...] = jnp.zeros_like(dkd_s)
            dq_diag = []
            for i in range(N_SUB):
                qi = q[SUB * i:SUB * (i + 1), :]
                bi = b_s[pl.ds(SUB * i, SUB), :]
                dai = da[SUB * i:SUB * (i + 1), :]
                dqi = jnp.zeros((SUB, REC_DIM), F32)
                for s in range(SUB):
                    ks = k_s[pl.ds(SUB * i + s, 1), :]
                    bs = b_s[pl.ds(SUB * i + s, 1), :]
                    w = jnp.exp(jnp.minimum(bi - bs, 0.0))
                    dacol = jnp.sum(jnp.where((lane == SUB * i + s) & (rowm >= s), dai, 0.0), axis=-1,
                                    keepdims=True)
                    dqi = dqi + dacol * ks * w
                    dkd_s[pl.ds(SUB * i + s, 1), :] += jnp.sum(dacol * qi * w, axis=0, keepdims=True)
                dq_diag.append(dqi)
            dq = dq + jnp.concatenate(dq_diag, axis=0) + eb * dqb
            dk = dk + dkd_s[...] + ekb * dkb
            edge = jnp.sum(kb * dkb, axis=0, keepdims=True) + el * jnp.sum(ht * dht, axis=0, keepdims=True)
            db = q * dq - k * dk + jnp.where(last_row, edge, 0.0)
            dg = _sel_left(triu, db)
            df = dg / f - dk
            dz = df * (1.0 - lb) * sg * (1.0 - sg)
            dlb = dlb + jnp.sum(df * (1.0 - sg), axis=0, keepdims=True)
            dq_ref[pl.ds(r0, CHUNK), :] = dq.astype(dq_ref.dtype)
            dz_ref[pl.ds(r0, CHUNK), :] = dz.astype(dz_ref.dtype)
            dv_ref[pl.ds(r0, CHUNK), :] = dv.astype(dv_ref.dtype)
            return dht_new, dlb

        _, dlb = lax.fori_loop(0, nc, chunk, (jnp.zeros((REC_DIM, REC_DIM), F32), jnp.zeros((1, REC_DIM), F32)))
        dlb_ref[...] = jnp.broadcast_to(dlb * lb * (1.0 - lb), (8, REC_DIM))

    H = REC_HEADS
    return pl.pallas_call(
        body, name="hgrn_bwd", grid=(B, H),
        in_specs=[pl.BlockSpec((S, 128), lambda b, h: (b, COL_RQ + h)),
                  pl.BlockSpec((S, 128), lambda b, h: (b, COL_RF + h)),
                  pl.BlockSpec((S, 128), lambda b, h: (b, COL_RI + h)),
                  pl.BlockSpec((2, 128), lambda b, h: (0, h)),
                  pl.BlockSpec((nc, REC_DIM, REC_DIM), lambda b, h: (b * H + h, 0, 0)),
                  pl.BlockSpec((S, 128), lambda b, h: (b, h))],
        out_specs=[pl.BlockSpec((S, 128), lambda b, h: (b, h))] * 3
        + [pl.BlockSpec((8, 128), lambda b, h: (b, h))],
        out_shape=[jax.ShapeDtypeStruct((T, 512), BF16)] * 3 + [jax.ShapeDtypeStruct((B * 8, 512), F32)],
        scratch_shapes=[pltpu.VMEM((CHUNK, REC_DIM), F32)] * 3,
        compiler_params=_params("arbitrary", "arbitrary"))(proj, proj, proj, lb_param, states, do)


def _rec_gate_fwd(rec, proj, rec_norm):
    T = rec.shape[0]

    def fn(accs, tv, rv, cv):
        return [_rms_hat(tv[0]) * cv[0] * _sigmoid(tv[1])]

    return _tile_call("rec_gate", fn, T, 512, _pick(T, 1024), REC_DIM, tiles=[(rec, 0), (proj, COL_RG)],
                      consts=[rec_norm], outs=[(BF16, "tile")])[0]


def _rec_gate_bwd(dyb, w_rec_proj, rec, proj, rec_norm):
    T = rec.shape[0]

    def fn(accs, tv, rv, cv):
        d, r, rg = accs[0], tv[0], tv[1]
        sg = _sigmoid(rg)
        rn = _rms_hat(r) * cv[0]
        dh, dg = _rms_bwd_vals(d * sg, r, cv[0])
        return [dh, d * rn * sg * (1.0 - sg), dg]

    return _tile_call("rec_gate_bwd", fn, T, 512, _pick(T, 1024), REC_DIM, pairs=[(dyb, 0, w_rec_proj, "nt")],
                      tiles=[(rec, 0), (proj, COL_RG)], consts=[rec_norm],
                      outs=[(F32, "tile"), (BF16, "tile")], parts=1)


def _mix_out_fwd(att, recn, proj, w_att_proj, w_rec_proj, w_out, h1):
    T = att.shape[0]
    tn = 256

    def merge(accs, tv, rv, cv):
        ya, yb = accs
        return [ya, yb, _sigmoid(tv[0]) * ya + _sigmoid(tv[1]) * yb]

    ya, yb, merged = _tile_call(
        "merge", merge, T, D_MODEL, _pick(T, 1024), tn,
        pairs=[(att, 0, w_att_proj, "nn"), (recn, 0, w_rec_proj, "nn")],
        tiles=[(proj, COL_GA * 128 // tn), (proj, COL_GB * 128 // tn)], outs=[(BF16, "tile")] * 3)

    def res(accs, tv, rv, cv):
        return [tv[0] + accs[0]]

    h2 = _tile_call("mix_out", res, T, D_MODEL, _pick(T, 512), 512, pairs=[(merged, 0, w_out, "nn")],
                    tiles=[(h1, 0)], outs=[(F32, "tile")])[0]
    return h2, (ya, yb, merged)


def _local_step(x, p, tgt, w, B, S):
    T = B * S
    g_ffn1, g_mix, g_ffn2, g_ple = w["norm_ffn1"], w["norm_mix"], w["norm_ffn2"], w["norm_ple"]
    g_fin = w["norm_final"].reshape(1, D_MODEL)
    grads = {}

    h1, sv1 = _ffn_fwd("ffn1", x, g_ffn1, w["w_ffn1_in"], w["w_ffn1_out"])
    u = _rms_fwd("mix_norm", h1, g_mix)

    def ident(accs, tv, rv, cv):
        return [accs[0]]

    proj = _tile_call("in_proj", ident, T, IN_W, _pick(T, 1024), 256, pairs=[(u, 0, w["w_in"], "nn")],
                      outs=[(F32, "tile")])[0]
    onehot = jnp.asarray(_t5_onehot())
    bias = _small_mm("t5_bias", w["rel_bias"].T, onehot.astype(BF16), "right")
    bias = bias.reshape(N_Q_HEADS, ATT_BLOCK, 2 * ATT_BLOCK)
    sinks = w["attn_sinks"].reshape(N_Q_HEADS)
    kk2, vv2 = _kv_layouts(proj)
    att = _swa_fwd(proj, kk2, vv2, bias, sinks, B, S)
    rec, states = _hgrn_fwd(proj, w["lb_param"], B, S)
    recn = _rec_gate_fwd(rec, proj, w["rec_norm"])
    h2, (ya, yb, merged) = _mix_out_fwd(att, recn, proj, w["w_att_proj"], w["w_rec_proj"], w["w_out"], h1)
    h3, sv2 = _ffn_fwd("ffn2", h2, g_ffn2, w["w_ffn2_in"], w["w_ffn2_out"])
    n3 = _rms_fwd("ple_norm", h3, g_ple)
    pb = p.astype(BF16)

    def ple(accs, tv, rv, cv):
        gate = _sigmoid(accs[0])
        return [gate, accs[1], tv[0] + gate * accs[1]]

    gate_p, pp, h4 = _tile_call(
        "ple", ple, T, D_MODEL, _pick(T, 512), 512,
        pairs=[(n3, 0, w["w_ple_gate"], "nn"), (pb, 0, w["w_ple_proj"], "nn")], tiles=[(h3, 0)],
        outs=[(BF16, "tile"), (BF16, "tile"), (F32, "tile")])

    def head(accs, tv, rv, cv):
        h, t = tv
        err = _rms_hat(h) * cv[0] - t
        dh, dg = _rms_bwd_vals(err * (1.0 / D_MODEL), h, cv[0])
        return [dh, _group8(err * err), dg]

    dh4, loss_p, dg_fin = _tile_call("loss_head", head, T, D_MODEL, _pick(T, 256), D_MODEL,
                                     tiles=[(h4, 0), (tgt, 0)], consts=[g_fin], outs=[(F32, "tile")], parts=2)
    grads["norm_final"] = dg_fin

    def dple(accs, tv, rv, cv):
        d, gt, ppv = tv[0], tv[1].astype(F32), tv[2].astype(F32)
        return [d * ppv * gt * (1.0 - gt), d * gt]

    dzg, dpp = _tile_call("ple_dact", dple, T, D_MODEL, _pick(T, 512), D_MODEL,
                          tiles=[(dh4, 0), (gate_p, 0), (pp, 0)], outs=[(BF16, "tile")] * 2)
    grads["w_ple_gate"] = _mm_tn("ple_dwg", n3, dzg)
    grads["w_ple_proj"] = _mm_tn("ple_dwp", pb, dpp)

    def dnorm(accs, tv, rv, cv):
        dh, dg = _rms_bwd_vals(accs[0], tv[0], cv[0])
        return [tv[1] + dh, dg]

    dh3, grads["norm_ple"] = _tile_call(
        "ple_dnorm", dnorm, T, D_MODEL, _pick(T, 256), D_MODEL, pairs=[(dzg, 0, w["w_ple_gate"], "nt")],
        tiles=[(h3, 0), (dh4, 0)], consts=[g_ple], outs=[(F32, "tile")], parts=1)

    dh2, grads["norm_ffn2"], grads["w_ffn2_in"], grads["w_ffn2_out"] = _ffn_bwd(
        "ffn2b", dh3, h2, g_ffn2, w["w_ffn2_in"], w["w_ffn2_out"], sv2)

    def to_bf(accs, tv, rv, cv):
        return [tv[0]]

    dh2b = _tile_call("mix_dcast", to_bf, T, D_MODEL, _pick(T, 512), D_MODEL, tiles=[(dh2, 0)],
                      outs=[(BF16, "tile")])[0]
    grads["w_out"] = _mm_tn("mix_dwout", merged, dh2b)
    tn = 256

    def dmerge(accs, tv, rv, cv):
        dm = accs[0]
        sa, sb = _sigmoid(tv[0]), _sigmoid(tv[1])
        yav, ybv = tv[2].astype(F32), tv[3].astype(F32)
        return [dm * sa, dm * sb, dm * yav * sa * (1.0 - sa), dm * ybv * sb * (1.0 - sb)]

    dya, dyb, dga, dgb = _tile_call(
        "mix_dmerge", dmerge, T, D_MODEL, _pick(T, 1024), tn, pairs=[(dh2b, 0, w["w_out"], "nt")],
        tiles=[(proj, COL_GA * 128 // tn), (proj, COL_GB * 128 // tn), (ya, 0), (yb, 0)],
        outs=[(BF16, "tile")] * 4)
    grads["w_att_proj"] = _mm_tn("mix_dwatt", att, dya)
    grads["w_rec_proj"] = _mm_tn("mix_dwrec", recn, dyb)

    datt = _tile_call("mix_datt", ident, T, 512, _pick(T, 1024), 512, pairs=[(dya, 0, w["w_att_proj"], "nt")],
                      outs=[(BF16, "tile")])[0]
    drec, drg, grads["rec_norm"] = _rec_gate_bwd(dyb, w["w_rec_proj"], rec, proj, w["rec_norm"])

    drq, drf, dri, dlb = _hgrn_bwd(proj, w["lb_param"], states, drec, B, S)
    grads["lb_param"] = dlb
    daq, dk2, dv2, dbias, dsink = _swa_bwd(proj, kk2, vv2, bias, sinks, datt, B, S)
    grads["attn_sinks"] = dsink
    grads["rel_bias"] = _small_mm("t5_dbias", dbias.reshape(N_Q_HEADS, -1), onehot.T.astype(BF16), "right")
    sw = lambda t: jnp.concatenate([t[:, HEAD_DIM:], t[:, :HEAD_DIM]], axis=1)
    dak = (dk2[:, :128] + sw(dk2[:, 128:])).astype(BF16)
    dav = (dv2[:, :128] + sw(dv2[:, 128:])).astype(BF16)
    dproj = jnp.concatenate([daq, dak, dav, drq, drf, dri, drg, dga, dgb], axis=1)
    grads["w_in"] = _mm_tn("mix_dwin", u, dproj)

    def dnorm_mix(accs, tv, rv, cv):
        dh, dg = _rms_bwd_vals(accs[0], tv[0], cv[0])
        return [tv[1] + dh, dg]

    dh1, grads["norm_mix"] = _tile_call(
        "mix_dnorm", dnorm_mix, T, D_MODEL, _pick(T, 256), D_MODEL, pairs=[(dproj, 0, w["w_in"], "nt")],
        tiles=[(h1, 0), (dh2, 0)], consts=[g_mix], outs=[(F32, "tile")], parts=1)

    dx, grads["norm_ffn1"], grads["w_ffn1_in"], grads["w_ffn1_out"] = _ffn_bwd(
        "ffn1b", dh1, x, g_ffn1, w["w_ffn1_in"], w["w_ffn1_out"], sv1)
    return loss_p, dx, grads


def _place():
    x, y, c = lax.axis_index("x"), lax.axis_index("y"), lax.axis_index("c")
    return x, y, c


def _other_chips(x, y):
    return [(1 - x, y, 2 * (1 - x) + y), (x, 1 - y, 2 * x + 1 - y), (1 - x, 1 - y, 2 * (1 - x) + 1 - y)]


ANY = pl.BlockSpec(memory_space=pl.ANY)


def _gather_weights(wp):
    R = wp.shape[0]
    half = R // 2

    def body(w_ref, out_ref, send_sems, recv_sems):
        x, y, c = _place()
        me = 2 * x + y
        sibling = (x, y, 1 - c)
        chips = _other_chips(x, y)

        def rows(chip, h):
            return out_ref.at[chip, pl.ds(h * half, half), :]

        def copy(k, chip, h, to, src=None):
            return pltpu.make_async_remote_copy(
                src_ref=rows(chip, h) if src is None else src, dst_ref=rows(chip, h),
                send_sem=send_sems.at[k], recv_sem=recv_sems.at[k], device_id=to, device_id_type=MESH)

        first = [copy(j, me, c, (cx, cy, c), src=w_ref.at[pl.ds(c * half, half), :])
                 for j, (cx, cy, _) in enumerate(chips)]
        for cp in first:
            cp.start()
        passed = [copy(3 + j, ci, c, sibling) for j, (_, _, ci) in enumerate(chips)]
        for j, (cx, cy, ci) in enumerate(chips):
            copy(j, ci, c, (cx, cy, c)).wait_recv()
            passed[j].start()
        for j, (_, _, ci) in enumerate(chips):
            copy(3 + j, ci, 1 - c, sibling).wait_recv()
        for cp in first + passed:
            cp.wait_send()

    return pl.pallas_call(
        body, name="gather_weights", in_specs=[ANY], out_specs=ANY,
        out_shape=jax.ShapeDtypeStruct((N_CHIPS, R, PACK_W), wp.dtype),
        scratch_shapes=[pltpu.SemaphoreType.DMA((6,)), pltpu.SemaphoreType.DMA((6,))],
    )(wp)


def _swap_halves(gp):
    R = gp.shape[1]
    half = R // 2

    def body(g_ref, out_ref, send_sem, recv_sem):
        x, y, c = _place()
        cp = pltpu.make_async_remote_copy(
            src_ref=g_ref.at[:, pl.ds((1 - c) * half, half), :], dst_ref=out_ref,
            send_sem=send_sem, recv_sem=recv_sem, device_id=(x, y, 1 - c), device_id_type=MESH)
        cp.start()
        cp.wait()

    return pl.pallas_call(
        body, name="rs_sibling", in_specs=[ANY], out_specs=ANY,
        out_shape=jax.ShapeDtypeStruct((N_CHIPS, half, PACK_W), gp.dtype),
        scratch_shapes=[pltpu.SemaphoreType.DMA(()), pltpu.SemaphoreType.DMA(())],
    )(gp)


def _scatter_chips(pp):
    h = pp.shape[1]

    def body(p_ref, out_ref, send_sems, recv_sems):
        x, y, c = _place()
        cps = [pltpu.make_async_remote_copy(
            src_ref=p_ref.at[ci], dst_ref=out_ref.at[j], send_sem=send_sems.at[j], recv_sem=recv_sems.at[j],
            device_id=(cx, cy, c), device_id_type=MESH) for j, (cx, cy, ci) in enumerate(_other_chips(x, y))]
        for cp in cps:
            cp.start()
        for cp in cps:
            cp.wait()

    return pl.pallas_call(
        body, name="rs_chips", in_specs=[ANY], out_specs=ANY,
        out_shape=jax.ShapeDtypeStruct((3, h, PACK_W), pp.dtype),
        scratch_shapes=[pltpu.SemaphoreType.DMA((3,)), pltpu.SemaphoreType.DMA((3,))],
    )(pp)


def _join_halves(s):
    h = s.shape[0]

    def body(s_ref, out_ref, send_sem, recv_sem):
        x, y, c = _place()
        cp = pltpu.make_async_remote_copy(
            src_ref=s_ref, dst_ref=out_ref, send_sem=send_sem, recv_sem=recv_sem,
            device_id=(x, y, 1 - c), device_id_type=MESH)
        cp.start()
        cp.wait()

    return pl.pallas_call(
        body, name="rs_join", in_specs=[ANY], out_specs=ANY,
        out_shape=jax.ShapeDtypeStruct((h, PACK_W), s.dtype),
        scratch_shapes=[pltpu.SemaphoreType.DMA(()), pltpu.SemaphoreType.DMA(())],
    )(s)


def _allreduce_small(sp):
    def body(s_ref, out_ref, slots, send_sems, recv_sems):
        x, y, c = _place()
        me = 4 * x + 2 * y + c
        slots[me] = s_ref[...]
        cps = []
        for r in range(1, N_DEV):
            px, py, pc = x ^ (r >> 2), y ^ ((r >> 1) & 1), c ^ (r & 1)
            cps.append(pltpu.make_async_remote_copy(
                src_ref=s_ref, dst_ref=slots.at[me], send_sem=send_sems.at[r - 1], recv_sem=recv_sems.at[r - 1],
                device_id=(px, py, pc), device_id_type=MESH))
        for cp in cps:
            cp.start()
        for r in range(1, N_DEV):
            px, py, pc = x ^ (r >> 2), y ^ ((r >> 1) & 1), c ^ (r & 1)
            pltpu.make_async_remote_copy(
                src_ref=s_ref, dst_ref=slots.at[4 * px + 2 * py + pc], send_sem=send_sems.at[r - 1],
                recv_sem=recv_sems.at[r - 1], device_id=(px, py, pc), device_id_type=MESH).wait_recv()
        for cp in cps:
            cp.wait_send()
        acc = slots[0]
        for d in range(1, N_DEV):
            acc = acc + slots[d]
        out_ref[...] = acc

    return pl.pallas_call(
        body, name="allreduce_small",
        in_specs=[pl.BlockSpec(memory_space=pltpu.VMEM)], out_specs=pl.BlockSpec(memory_space=pltpu.VMEM),
        out_shape=jax.ShapeDtypeStruct(sp.shape, F32),
        scratch_shapes=[pltpu.VMEM((N_DEV,) + sp.shape, F32), pltpu.SemaphoreType.DMA((N_DEV - 1,)),
                        pltpu.SemaphoreType.DMA((N_DEV - 1,))],
    )(sp)


def _add_call(name, terms, out_dtypes, tm=448):
    R = terms[0].shape[0]

    def fn(accs, tv, rv, cv):
        s = tv[0].astype(F32)
        for t in tv[1:]:
            s = s + t.astype(F32)
        return [s] * len(out_dtypes)

    return _tile_call(name, fn, R, PACK_W, _pick(R, tm, 16), PACK_W, tiles=[(t, 0) for t in terms],
                      outs=[(dt, "tile") for dt in out_dtypes])


def _shard_shapes(w):
    return {n: w[n].shape[1:] for n in BIG}


def _pack_shard(arrs):
    return jnp.concatenate([a.reshape(-1, PACK_W) for a in arrs], axis=0)


def _to_shards(name, g):
    if name in COL_SHARDED:
        r, cdim = g.shape
        return g.reshape(r, N_CHIPS, cdim // N_CHIPS).transpose(1, 0, 2)
    return g.reshape(N_CHIPS, g.shape[0] // N_CHIPS, g.shape[1])


def _from_shards(name, s):
    if name in COL_SHARDED:
        return s.transpose(1, 0, 2).reshape(s.shape[1], -1)
    return s.reshape(-1, s.shape[2])


def _adamw_vals(w, g, m, v):
    m = ADAM_B1 * m + (1.0 - ADAM_B1) * g
    v = ADAM_B2 * v + (1.0 - ADAM_B2) * (g * g)
    m_hat = m / (1.0 - ADAM_B1 ** ADAM_STEP)
    v_hat = v / (1.0 - ADAM_B2 ** ADAM_STEP)
    delta = -ADAM_LR * (m_hat / (jnp.sqrt(v_hat) + ADAM_EPS) + ADAM_WD * w)
    return delta, m, v


def _adamw(name, w, g, m, v):
    R, W = w.shape

    def fn(accs, tv, rv, cv):
        return list(_adamw_vals(*tv))

    return _tile_call(name, fn, R, W, _pick(R, 256), W, tiles=[(w, 0), (g, 0), (m, 0), (v, 0)],
                      outs=[(F32, "tile")] * 3)


SMALL_LAYOUT = (("rel_bias", 2, 256), ("lb_param", 8, 1024), ("norm_ffn1", 8, 1024), ("norm_mix", 8, 1024),
                ("attn_sinks", 1, 8), ("rec_norm", 1, 128), ("norm_ffn2", 8, 1024), ("norm_ple", 8, 1024),
                ("norm_final", 8, 1024), ("loss", 8, 1024))


def _pack_small(vals):
    rows = []
    for name, nrows, n in SMALL_LAYOUT:
        flat = vals[name].reshape(-1)
        flat = jnp.pad(flat, (0, nrows * 128 - n))
        rows.append(flat.reshape(nrows, 128))
    packed = jnp.concatenate(rows, axis=0)
    return jnp.pad(packed, ((0, SMALL_ROWS - packed.shape[0]), (0, 0)))


def _unpack_small(packed, shapes):
    out, r = {}, 0
    for name, nrows, n in SMALL_LAYOUT:
        out[name] = packed[r:r + nrows].reshape(-1)[:n].reshape(shapes[name])
        r += nrows
    return out


def kernel(x, p, rel_bias, lb_param, norm_ffn1, w_ffn1_in, w_ffn1_out, norm_mix, w_in, attn_sinks, rec_norm, w_att_proj, w_rec_proj, w_out, norm_ffn2, w_ffn2_in, w_ffn2_out, norm_ple, w_ple_gate, w_ple_proj, norm_final, loss_target, m_rel_bias, m_lb_param, m_norm_ffn1, m_w_ffn1_in, m_w_ffn1_out, m_norm_mix, m_w_in, m_attn_sinks, m_rec_norm, m_w_att_proj, m_w_rec_proj, m_w_out, m_norm_ffn2, m_w_ffn2_in, m_w_ffn2_out, m_norm_ple, m_w_ple_gate, m_w_ple_proj, m_norm_final, v_rel_bias, v_lb_param, v_norm_ffn1, v_w_ffn1_in, v_w_ffn1_out, v_norm_mix, v_w_in, v_attn_sinks, v_rec_norm, v_w_att_proj, v_w_rec_proj, v_w_out, v_norm_ffn2, v_w_ffn2_in, v_w_ffn2_out, v_norm_ple, v_w_ple_gate, v_w_ple_proj, v_norm_final):
    args = dict(locals())
    wsh = {n: args[n] for n in WEIGHTS}
    B, S = x.shape[0], x.shape[1]
    T = B * S
    cx, cy, cc = _place()
    me_chip = 2 * cx + cy

    shard_shapes = _shard_shapes(wsh)
    wp = _pack_shard([wsh[n][0].astype(BF16) for n in BIG])
    R = wp.shape[0]
    gathered = lax.dynamic_update_index_in_dim(_gather_weights(wp), wp, me_chip, 0)
    wfull, r0 = {}, 0
    for n in BIG:
        shp = shard_shapes[n]
        nr = shp[0] * shp[1] // PACK_W
        wfull[n] = _from_shards(n, gathered[:, r0:r0 + nr].reshape((N_CHIPS,) + shp))
        r0 += nr
    for n in SMALL:
        wfull[n] = wsh[n]

    loss_p, dx, grads = _local_step(x.reshape(T, D_MODEL), p.reshape(T, PLE_DIM),
                                    loss_target.reshape(T, D_MODEL), wfull, B, S)

    gp = jnp.stack([_pack_shard([_to_shards(n, grads[n])[k] for n in BIG]) for k in range(N_CHIPS)])
    half = R // 2
    from_sibling = _swap_halves(gp.astype(BF16))
    mine = lax.dynamic_slice_in_dim(gp, cc * half, half, axis=1)
    chip_f32, chip_bf = _add_call("rs_add_sibling", [mine.reshape(-1, PACK_W), from_sibling.reshape(-1, PACK_W)],
                                  [F32, BF16])
    chip_f32 = chip_f32.reshape(N_CHIPS, half, PACK_W)
    from_chips = _scatter_chips(chip_bf.reshape(N_CHIPS, half, PACK_W))
    own = lax.dynamic_index_in_dim(chip_f32, me_chip, axis=0, keepdims=False)
    s_half = _add_call("rs_add_chips", [own, from_chips[0], from_chips[1], from_chips[2]], [F32])[0]
    s_sib = _join_halves(s_half)
    gsh = jnp.where(cc == 0, jnp.concatenate([s_half, s_sib], axis=0),
                    jnp.concatenate([s_sib, s_half], axis=0))

    small_vals = {
        "rel_bias": grads["rel_bias"].T,
        "lb_param": jnp.concatenate([_colsum("dlb_sum", grads["lb_param"]),
                                     -_colsum("dlb_sum2", grads["lb_param"])], axis=0) / 8.0,
        "attn_sinks": grads["attn_sinks"][:, 0],
        "rec_norm": _colsum("drn_sum", grads["rec_norm"]).reshape(REC_HEADS, REC_DIM).sum(axis=0),
        "loss": _colsum("loss_sum", loss_p),
    }
    for n in ("norm_ffn1", "norm_mix", "norm_ffn2", "norm_ple", "norm_final"):
        small_vals[n] = _colsum(n + "_sum", grads[n])
    red = _allreduce_small(_pack_small(small_vals))
    small_shapes = {n: wsh[n].shape for n in SMALL}
    small_shapes["loss"] = (D_MODEL,)
    small = _unpack_small(red, small_shapes)
    loss = 0.5 * jnp.sum(small["loss"]) / D_MODEL

    out_g, out_d, out_m, out_v = {}, {}, {}, {}
    r0 = 0
    for n in BIG:
        shp = shard_shapes[n]
        nr = shp[0] * shp[1] // PACK_W
        g = gsh[r0:r0 + nr].reshape(shp)
        r0 += nr
        d, nm, nv = _adamw("adamw_" + n, wsh[n][0], g, args["m_" + n][0], args["v_" + n][0])
        out_g[n], out_d[n], out_m[n], out_v[n] = g[None], d[None], nm[None], nv[None]
    sw = _pack_small({**{n: wsh[n] for n in SMALL}, "loss": jnp.zeros((D_MODEL,), F32)})
    sm = _pack_small({**{n: args["m_" + n] for n in SMALL}, "loss": jnp.zeros((D_MODEL,), F32)})
    sv = _pack_small({**{n: args["v_" + n] for n in SMALL}, "loss": jnp.ones((D_MODEL,), F32)})
    sd, snm, snv = _adamw("adamw_small", sw, red, sm, sv)
    ud, um, uv = (_unpack_small(t, small_shapes) for t in (sd, snm, snv))
    for n in SMALL:
        out_g[n], out_d[n], out_m[n], out_v[n] = small[n], ud[n], um[n], uv[n]

    return (loss, dx.reshape(B, S, D_MODEL), *[out_g[n] for n in WEIGHTS], *[out_d[n] for n in WEIGHTS],
            *[out_m[n] for n in WEIGHTS], *[out_v[n] for n in WEIGHTS])
```

```python
import numpy as np
import jax
import jax.numpy as jnp
from jax import lax
from jax.experimental import pallas as pl
from jax.experimental.pallas import tpu as pltpu

F32 = jnp.float32
BF16 = jnp.bfloat16
MESH = pl.DeviceIdType.MESH

D_MODEL = 1024
D_FF = 2816
FF_SHARD = 2 * D_FF // 4
HEAD_DIM = 64
N_Q_HEADS = 8
ATT_BLOCK = 128
N_BUCKETS = 32
MAX_DISTANCE = 128
REC_HEADS = 4
REC_DIM = 128
PLE_DIM = 256
EPS = 1e-6
IN_W = 4864
COL_AQ, COL_AK, COL_AV, COL_RQ, COL_RF, COL_RI, COL_RG, COL_GA, COL_GB = 0, 4, 5, 6, 10, 14, 18, 22, 30

CHUNK = 64
SUB = 16
N_SUB = CHUNK // SUB

ADAM_LR, ADAM_B1, ADAM_B2, ADAM_EPS, ADAM_WD, ADAM_STEP = 0.001, 0.9, 0.999, 1e-08, 0.01, 10

V7X_VMEM_LIMIT = 56 * 1024 * 1024
N_CHIPS = 4
N_DEV = 8

BIG = ("w_ffn1_in", "w_ffn1_out", "w_in", "w_att_proj", "w_rec_proj", "w_out",
       "w_ffn2_in", "w_ffn2_out", "w_ple_gate", "w_ple_proj")
COL_SHARDED = ("w_ffn1_in", "w_in", "w_att_proj", "w_rec_proj", "w_ffn2_in", "w_ple_proj")
WEIGHTS = ("rel_bias", "lb_param", "norm_ffn1", "w_ffn1_in", "w_ffn1_out", "norm_mix", "w_in", "attn_sinks",
           "rec_norm", "w_att_proj", "w_rec_proj", "w_out", "norm_ffn2", "w_ffn2_in", "w_ffn2_out", "norm_ple",
           "w_ple_gate", "w_ple_proj", "norm_final")
SMALL = tuple(n for n in WEIGHTS if n not in BIG)
SMALL_ROWS = 64


def _params(*sem):
    return pltpu.CompilerParams(dimension_semantics=sem, vmem_limit_bytes=V7X_VMEM_LIMIT)


def _pick(n, cap, mult=8):
    if n <= cap:
        return n
    for t in range(cap - cap % mult, 0, -mult):
        if n % t == 0:
            return t
    raise ValueError((n, cap, mult))


def _dot(a, b):
    return jnp.dot(a, b, preferred_element_type=F32)


def _dot_nt(a, b):
    return lax.dot_general(a, b, (((1,), (1,)), ((), ())), preferred_element_type=F32)


def _dot_tn(a, b):
    return lax.dot_general(a, b, (((0,), (0,)), ((), ())), preferred_element_type=F32)


def _split3(x):
    hi = x.astype(BF16)
    r = x - hi.astype(F32)
    mid = r.astype(BF16)
    lo = (r - mid.astype(F32)).astype(BF16)
    return hi, mid, lo


def _sel_left(sel_bf16, x):
    hi, mid, lo = _split3(x)
    return _dot(sel_bf16, hi) + _dot(sel_bf16, mid) + _dot(sel_bf16, lo)


def _sel_right(x, sel_bf16):
    hi, mid, lo = _split3(x)
    return _dot(hi, sel_bf16) + _dot(mid, sel_bf16) + _dot(lo, sel_bf16)


def _sigmoid(x):
    return 1.0 / (1.0 + jnp.exp(-x))


def _group8(x):
    r, w = x.shape
    return x.reshape(r // 8, 8, w).sum(axis=0)


def _call(name, fn, grid, ins, outs, pairs=()):
    in_pair = {i for p in pairs for i in p[:2]}
    n_in = len(ins)

    def body(*refs):
        accs = []
        for ia, ib, kind in pairs:
            a, b = refs[ia][...].astype(BF16), refs[ib][...].astype(BF16)
            accs.append(_dot(a, b) if kind == "nn" else _dot_nt(a, b))
        vals = [refs[i][...] for i in range(n_in) if i not in in_pair]
        res = fn(accs, vals)
        out_refs = refs[n_in:]
        assert len(res) == len(out_refs), (name, len(res), len(out_refs))
        for o_ref, val in zip(out_refs, res):
            o_ref[...] = val.astype(o_ref.dtype)

    return pl.pallas_call(
        body, name=name, grid=grid,
        in_specs=[pl.BlockSpec(blk, im) for _, blk, im in ins],
        out_specs=[pl.BlockSpec(blk, im) for _, _, blk, im in outs],
        out_shape=[jax.ShapeDtypeStruct(shp, dt) for shp, dt, _, _ in outs],
        compiler_params=_params(*(["arbitrary"] * len(grid))))(*[a for a, _, _ in ins])


def _tile_call(name, fn, M, N, tm, tn, *, pairs=(), tiles=(), consts=(), outs=(), parts=0):
    gi, gj = M // tm, N // tn
    assert gi * tm == M and gj * tn == N, (name, M, N, tm, tn)
    ins, prs = [], []
    for a, a_col, b, kind in pairs:
        K = b.shape[0] if kind == "nn" else b.shape[1]
        ins.append((a, (tm, K), lambda i, j, c=a_col: (i, c)))
        if kind == "nn":
            ins.append((b, (K, tn), lambda i, j: (0, j)))
        else:
            ins.append((b, (tn, K), lambda i, j: (j, 0)))
        prs.append((len(ins) - 2, len(ins) - 1, kind))
    for arr, off in tiles:
        ins.append((arr, (tm, tn), lambda i, j, o=off: (i, j + o)))
    for arr in consts:
        ins.append((arr, arr.shape, lambda i, j: (0, 0)))
    out_l = [((M, N), dt, (tm, tn), lambda i, j: (i, j)) for dt in outs]
    out_l += [((gi * 8, N), F32, (8, tn), lambda i, j: (i, j))] * parts
    nt = len(tiles)

    def wrapped(accs, vals):
        return fn(accs, vals[:nt], vals[nt:])

    return _call(name, wrapped, (gi, gj), ins, out_l, prs)


def _mm_tn(name, grid, a_in, b_in, outs):
    nk = grid[2]
    tm = [d for d in a_in[1] if d is not None][1]
    tn = [d for d in b_in[1] if d is not None][1]

    def body(a_ref, b_ref, *rest):
        out_refs, acc_ref = rest[:-1], rest[-1]
        k = pl.program_id(2)

        @pl.when(k == 0)
        def _():
            acc_ref[...] = jnp.zeros_like(acc_ref)

        acc_ref[...] += _dot_tn(a_ref[...].astype(BF16), b_ref[...].astype(BF16))

        @pl.when(k == nk - 1)
        def _():
            for o_ref in out_refs:
                o_ref[...] = acc_ref[...].astype(o_ref.dtype)

    return pl.pallas_call(
        body, name=name, grid=grid,
        in_specs=[pl.BlockSpec(a_in[1], a_in[2]), pl.BlockSpec(b_in[1], b_in[2])],
        out_specs=[pl.BlockSpec(blk, im) for _, _, blk, im in outs],
        out_shape=[jax.ShapeDtypeStruct(shp, dt) for shp, dt, _, _ in outs],
        scratch_shapes=[pltpu.VMEM((tm, tn), F32)],
        compiler_params=_params("arbitrary", "arbitrary", "arbitrary"))(a_in[0], b_in[0])


def _grad_pair(shape, block, imap):
    return [(shape, F32, block, imap), (shape, BF16, block, imap)]


def _mm_tn_rows(name, a, b, tk=1024):
    T, a_w = a.shape
    b_w = b.shape[1]
    tm = _pick(a_w, 1408, 128)
    tk = _pick(T, tk, 128)
    g32, g16 = _mm_tn(name, (a_w // tm, 1, T // tk),
                      (a, (tk, tm), lambda i, j, k: (k, i)), (b, (tk, b_w), lambda i, j, k: (k, 0)),
                      _grad_pair((a_w, b_w), (tm, b_w), lambda i, j, k: (i, 0)))
    shp = (N_CHIPS, a_w // N_CHIPS, b_w)
    return g32.reshape(shp), g16.reshape(shp)


def _mm_tn_cols(name, a, b, tk=1024):
    T, a_w = a.shape
    n = b.shape[1] // N_CHIPS
    tk = _pick(T, tk, 128)
    return _mm_tn(name, (1, N_CHIPS, T // tk),
                  (a, (tk, a_w), lambda i, j, k: (k, 0)), (b, (tk, n), lambda i, j, k: (k, j)),
                  _grad_pair((N_CHIPS, a_w, n), (None, a_w, n), lambda i, j, k: (j, 0, 0)))


def _colsum(name, x):
    def body(x_ref, o_ref):
        o_ref[...] = jnp.sum(x_ref[...], axis=0, keepdims=True)
    return pl.pallas_call(body, name=name, out_shape=jax.ShapeDtypeStruct((1, x.shape[1]), F32))(x)


def _rms_hat(h):
    return h * lax.rsqrt(jnp.mean(h * h, axis=-1, keepdims=True) + EPS)


def _rms_bwd_vals(dn, h, g):
    r = lax.rsqrt(jnp.mean(h * h, axis=-1, keepdims=True) + EPS)
    nh = h * r
    gd = dn * g
    dh = r * (gd - nh * jnp.mean(gd * nh, axis=-1, keepdims=True))
    return dh, _group8(dn * nh)


def _rms_fwd(name, h, g, tm=512):
    T = h.shape[0]

    def fn(accs, tv, cv):
        return [_rms_hat(tv[0]) * cv[0]]

    return _tile_call(name, fn, T, D_MODEL, _pick(T, tm), D_MODEL, tiles=[(h, 0)], consts=[g], outs=[BF16])[0]


def _ffn_fwd(tag, h, g, w_in, w_out):
    T = h.shape[0]
    n = _rms_fwd(tag + "_norm", h, g)
    tm = _pick(T, 512)
    wblk = (None, D_MODEL, FF_SHARD)

    def act(accs, vals):
        gate, up = accs
        return [gate, up, gate * _sigmoid(gate) * up]

    tile = lambda: ((T, D_FF), BF16, (tm, FF_SHARD), lambda i, j: (i, j))
    gate, up, a = _call(
        tag + "_in", act, (T // tm, 2),
        [(n, (tm, D_MODEL), lambda i, j: (i, 0)),
         (w_in, wblk, lambda i, j: (j, 0, 0)), (w_in, wblk, lambda i, j: (j + 2, 0, 0))],
        [tile(), tile(), tile()], pairs=[(0, 1, "nn"), (0, 2, "nn")])

    def res(accs, tv, cv):
        return [tv[0] + 0.5 * accs[0]]

    h_new = _tile_call(tag + "_out", res, T, D_MODEL, _pick(T, 512), 512,
                       pairs=[(a, 0, w_out, "nn")], tiles=[(h, 0)], outs=[F32])[0]
    return h_new, (n, gate, up, a)


def _ffn_bwd(tag, dh_out, h, g, w_in, w_out, saved):
    T = h.shape[0]
    n, gate, up, a = saved
    tm = _pick(T, 512)

    def half(accs, tv, cv):
        return [0.5 * tv[0]]

    df = _tile_call(tag + "_df", half, T, D_MODEL, _pick(T, 512), D_MODEL, tiles=[(dh_out, 0)], outs=[BF16])[0]

    def dact(accs, vals):
        da = accs[0]
        gt, u = vals[0].astype(F32), vals[1].astype(F32)
        sg = _sigmoid(gt)
        silu = gt * sg
        return [jnp.stack([da * u * (sg + silu * (1.0 - sg)), da * silu])]

    dz = _call(
        tag + "_dact", dact, (T // tm, 2),
        [(df, (tm, D_MODEL), lambda i, j: (i, 0)), (w_out, (FF_SHARD, D_MODEL), lambda i, j: (j, 0)),
         (gate, (tm, FF_SHARD), lambda i, j: (i, j)), (up, (tm, FF_SHARD), lambda i, j: (i, j))],
        [((2, T, D_FF), BF16, (2, tm, FF_SHARD), lambda i, j: (0, i, j))], pairs=[(0, 1, "nt")])[0]
    dw_out = _mm_tn_rows(tag + "_dwout", a, df)
    tk = _pick(T, 1024, 128)
    dw_in = _mm_tn(tag + "_dwin", (1, N_CHIPS, T // tk),
                   (n, (tk, D_MODEL), lambda i, j, k: (k, 0)),
                   (dz, (None, tk, FF_SHARD), lambda i, j, k: (j // 2, k, j % 2)),
                   _grad_pair((N_CHIPS, D_MODEL, FF_SHARD), (None, D_MODEL, FF_SHARD), lambda i, j, k: (j, 0, 0)))

    def dnorm(accs, vals):
        dn = accs[0] + accs[1] + accs[2] + accs[3]
        dh, dg = _rms_bwd_vals(dn, vals[0], vals[2])
        return [vals[1] + dh, dg]

    tm2 = _pick(T, 256)
    ins = [(dz, (None, tm2, FF_SHARD), lambda i, j, s=s: (s // 2, i, s % 2)) for s in range(N_CHIPS)]
    ins += [(w_in, (None, D_MODEL, FF_SHARD), lambda i, j, s=s: (s, 0, 0)) for s in range(N_CHIPS)]
    ins += [(h, (tm2, D_MODEL), lambda i, j: (i, 0)), (dh_out, (tm2, D_MODEL), lambda i, j: (i, 0)),
            (g, g.shape, lambda i, j: (0, 0))]
    dh, dg = _call(tag + "_dnorm", dnorm, (T // tm2, 1), ins,
                   [((T, D_MODEL), F32, (tm2, D_MODEL), lambda i, j: (i, 0)),
                    ((T // tm2 * 8, D_MODEL), F32, (8, D_MODEL), lambda i, j: (i, 0))],
                   pairs=[(s, N_CHIPS + s, "nt") for s in range(N_CHIPS)])
    return dh, dg, dw_in, dw_out


def _t5_onehot():
    qi = np.arange(ATT_BLOCK)[:, None] + ATT_BLOCK
    kj = np.arange(2 * ATT_BLOCK)[None, :]
    nn = np.maximum(qi - kj, 0)
    max_exact = N_BUCKETS // 2
    large = max_exact + (np.log(np.maximum(nn, 1) / max_exact) / np.log(MAX_DISTANCE / max_exact)
                         * (N_BUCKETS - max_exact)).astype(np.int32)
    large = np.minimum(large, N_BUCKETS - 1)
    bucket = np.where(nn < max_exact, nn, large).astype(np.int32).reshape(-1)
    return (bucket[None, :] == np.arange(N_BUCKETS)[:, None]).astype(np.float32)


def _small_mm(name, a, b, sel):
    def body(a_ref, b_ref, o_ref):
        if sel == "right":
            o_ref[...] = _sel_right(a_ref[...], b_ref[...])
        else:
            o_ref[...] = _sel_left(a_ref[...], b_ref[...])
    return pl.pallas_call(body, name=name, out_shape=jax.ShapeDtypeStruct((a.shape[0], b.shape[1]), F32),
                          compiler_params=pltpu.CompilerParams(vmem_limit_bytes=V7X_VMEM_LIMIT))(a, b)


def _swap_heads(t):
    return jnp.concatenate([t[:, HEAD_DIM:], t[:, :HEAD_DIM]], axis=1)


def _kv_layouts(proj):
    k = proj[:, COL_AK * 128:(COL_AK + 1) * 128]
    v = proj[:, COL_AV * 128:(COL_AV + 1) * 128]
    return (jnp.concatenate([k, _swap_heads(k)], axis=1).astype(BF16),
            jnp.concatenate([v, _swap_heads(v)], axis=1).astype(BF16))


def _swa_masks():
    row = lax.broadcasted_iota(jnp.int32, (ATT_BLOCK, 2 * ATT_BLOCK), 0)
    col = lax.broadcasted_iota(jnp.int32, (ATT_BLOCK, 2 * ATT_BLOCK), 1)
    dist = ATT_BLOCK + row - col
    return (dist >= 0) & (dist < ATT_BLOCK), col


def _swa_heads():
    out = []
    for h in range(N_Q_HEADS):
        lo = h % 2 == 0
        hk = h // 4
        swapped = (hk == 1) if lo else (hk == 0)
        out.append((h // 2, lo, swapped))
    return out


def _swa_probs(qm, kk, bias_h, sink, valid):
    s = _dot_nt(qm, kk) * (HEAD_DIM ** -0.5) + bias_h
    s = jnp.where(valid, s, -jnp.inf)
    m = jnp.maximum(jnp.max(s, axis=-1, keepdims=True), sink)
    e = jnp.exp(s - m)
    es = jnp.exp(sink - m)
    den = jnp.sum(e, axis=-1, keepdims=True) + es
    return e / den, es / den


def _swa_fwd(proj, kk2, vv2, bias, sinks, B, S):
    T = B * S
    nb = S // ATT_BLOCK

    def body(q_ref, k_ref, v_ref, bias_ref, sink_ref, o_ref, kpad, vpad):
        zeros = jnp.zeros((ATT_BLOCK, 256), BF16)
        kpad[pl.ds(0, ATT_BLOCK), :] = zeros
        vpad[pl.ds(0, ATT_BLOCK), :] = zeros
        kpad[pl.ds(ATT_BLOCK, S), :] = k_ref[...]
        vpad[pl.ds(ATT_BLOCK, S), :] = v_ref[...]
        valid0, col = _swa_masks()
        lane = lax.broadcasted_iota(jnp.int32, (1, 128), 1)
        lo_q = lane < HEAD_DIM
        heads = _swa_heads()

        def blk(n, carry):
            r0 = pl.multiple_of(n * ATT_BLOCK, ATT_BLOCK)
            valid = valid0 & ((n > 0) | (col >= ATT_BLOCK))
            kb = kpad[pl.ds(r0, 2 * ATT_BLOCK), :]
            vb = vpad[pl.ds(r0, 2 * ATT_BLOCK), :]
            for j in range(N_Q_HEADS // 2):
                qblk = q_ref[pl.ds(r0, ATT_BLOCK), pl.ds(128 * j, 128)].astype(BF16)
                acc = jnp.zeros((ATT_BLOCK, 128), F32)
                for h in (2 * j, 2 * j + 1):
                    _, lo, swapped = heads[h]
                    keep = lo_q if lo else ~lo_q
                    qm = jnp.where(keep, qblk, jnp.zeros_like(qblk))
                    kk = kb[:, 128:] if swapped else kb[:, :128]
                    vv = vb[:, 128:] if swapped else vb[:, :128]
                    vm = jnp.where(keep, vv, jnp.zeros_like(vv))
                    p, _ = _swa_probs(qm, kk, bias_ref[h], sink_ref[h], valid)
                    acc = acc + _dot(p.astype(BF16), vm)
                o_ref[pl.ds(r0, ATT_BLOCK), pl.ds(128 * j, 128)] = acc.astype(o_ref.dtype)
            return carry

        lax.fori_loop(0, nb, blk, 0)

    return pl.pallas_call(
        body, name="swa_fwd", grid=(B,),
        in_specs=[pl.BlockSpec((S, 512), lambda b: (b, 0)),
                  pl.BlockSpec((S, 256), lambda b: (b, 0)),
                  pl.BlockSpec((S, 256), lambda b: (b, 0)),
                  pl.BlockSpec((N_Q_HEADS, ATT_BLOCK, 2 * ATT_BLOCK), lambda b: (0, 0, 0)),
                  pl.BlockSpec(memory_space=pltpu.SMEM)],
        out_specs=pl.BlockSpec((S, 512), lambda b: (b, 0)),
        out_shape=jax.ShapeDtypeStruct((T, 512), BF16),
        scratch_shapes=[pltpu.VMEM((S + ATT_BLOCK, 256), BF16), pltpu.VMEM((S + ATT_BLOCK, 256), BF16)],
        compiler_params=_params("arbitrary"))(proj, kk2, vv2, bias, sinks)


def _swa_bwd(proj, kk2, vv2, bias, sinks, datt, B, S):
    T = B * S
    nb = S // ATT_BLOCK

    def body(q_ref, k_ref, v_ref, bias_ref, sink_ref, do_ref, dq_ref, dk_ref, dv_ref, dbias_ref, dsink_ref,
             kpad, vpad, dkpad, dvpad):
        b = pl.program_id(0)

        @pl.when(b == 0)
        def _():
            dbias_ref[...] = jnp.zeros_like(dbias_ref)
            dsink_ref[...] = jnp.zeros_like(dsink_ref)

        zeros = jnp.zeros((ATT_BLOCK, 256), BF16)
        kpad[pl.ds(0, ATT_BLOCK), :] = zeros
        vpad[pl.ds(0, ATT_BLOCK), :] = zeros
        kpad[pl.ds(ATT_BLOCK, S), :] = k_ref[...]
        vpad[pl.ds(ATT_BLOCK, S), :] = v_ref[...]
        dkpad[...] = jnp.zeros_like(dkpad)
        dvpad[...] = jnp.zeros_like(dvpad)
        valid0, col = _swa_masks()
        lane = lax.broadcasted_iota(jnp.int32, (1, 128), 1)
        lo_q = lane < HEAD_DIM
        heads = _swa_heads()
        scale = HEAD_DIM ** -0.5

        def blk(n, carry):
            r0 = pl.multiple_of(n * ATT_BLOCK, ATT_BLOCK)
            valid = valid0 & ((n > 0) | (col >= ATT_BLOCK))
            kb = kpad[pl.ds(r0, 2 * ATT_BLOCK), :]
            vb = vpad[pl.ds(r0, 2 * ATT_BLOCK), :]
            dk_acc = [jnp.zeros((2 * ATT_BLOCK, 128), F32), jnp.zeros((2 * ATT_BLOCK, 128), F32)]
            dv_acc = [jnp.zeros((2 * ATT_BLOCK, 128), F32), jnp.zeros((2 * ATT_BLOCK, 128), F32)]
            for j in range(N_Q_HEADS // 2):
                qblk = q_ref[pl.ds(r0, ATT_BLOCK), pl.ds(128 * j, 128)].astype(BF16)
                doblk = do_ref[pl.ds(r0, ATT_BLOCK), pl.ds(128 * j, 128)]
                dq = jnp.zeros((ATT_BLOCK, 128), F32)
                for h in (2 * j, 2 * j + 1):
                    _, lo, swapped = heads[h]
                    keep = lo_q if lo else ~lo_q
                    qm = jnp.where(keep, qblk, jnp.zeros_like(qblk))
                    dom = jnp.where(keep, doblk, jnp.zeros_like(doblk))
                    kk = kb[:, 128:] if swapped else kb[:, :128]
                    vv = vb[:, 128:] if swapped else vb[:, :128]
                    km = jnp.where(keep, kk, jnp.zeros_like(kk))
                    p, ps = _swa_probs(qm, kk, bias_ref[h], sink_ref[h], valid)
                    dp = _dot_nt(dom, vv)
                    delta = jnp.sum(p * dp, axis=-1, keepdims=True)
                    ds = p * (dp - delta)
                    dbias_ref[h] += ds
                    dsink_ref[pl.ds(h, 1), :] += -jnp.sum(jnp.broadcast_to(ps * delta, (ATT_BLOCK, 128)),
                                                          axis=0, keepdims=True)
                    dsb = (ds * scale).astype(BF16)
                    dq = dq + _dot(dsb, km)
                    idx = 1 if swapped else 0
                    dk_acc[idx] = dk_acc[idx] + _dot_tn(dsb, qm)
                    dv_acc[idx] = dv_acc[idx] + _dot_tn(p.astype(BF16), dom)
                dq_ref[pl.ds(r0, ATT_BLOCK), pl.ds(128 * j, 128)] = dq.astype(dq_ref.dtype)
            dkpad[pl.ds(r0, 2 * ATT_BLOCK), :] += jnp.concatenate(dk_acc, axis=1)
            dvpad[pl.ds(r0, 2 * ATT_BLOCK), :] += jnp.concatenate(dv_acc, axis=1)
            return carry

        lax.fori_loop(0, nb, blk, 0)
        dk_ref[...] = dkpad[pl.ds(ATT_BLOCK, S), :]
        dv_ref[...] = dvpad[pl.ds(ATT_BLOCK, S), :]

    return pl.pallas_call(
        body, name="swa_bwd", grid=(B,),
        in_specs=[pl.BlockSpec((S, 512), lambda b: (b, 0)),
                  pl.BlockSpec((S, 256), lambda b: (b, 0)),
                  pl.BlockSpec((S, 256), lambda b: (b, 0)),
                  pl.BlockSpec((N_Q_HEADS, ATT_BLOCK, 2 * ATT_BLOCK), lambda b: (0, 0, 0)),
                  pl.BlockSpec(memory_space=pltpu.SMEM),
                  pl.BlockSpec((S, 512), lambda b: (b, 0))],
        out_specs=[pl.BlockSpec((S, 512), lambda b: (b, 0)),
                   pl.BlockSpec((S, 256), lambda b: (b, 0)),
                   pl.BlockSpec((S, 256), lambda b: (b, 0)),
                   pl.BlockSpec((N_Q_HEADS, ATT_BLOCK, 2 * ATT_BLOCK), lambda b: (0, 0, 0)),
                   pl.BlockSpec((N_Q_HEADS, 128), lambda b: (0, 0))],
        out_shape=[jax.ShapeDtypeStruct((T, 512), BF16),
                   jax.ShapeDtypeStruct((T, 256), F32),
                   jax.ShapeDtypeStruct((T, 256), F32),
                   jax.ShapeDtypeStruct((N_Q_HEADS, ATT_BLOCK, 2 * ATT_BLOCK), F32),
                   jax.ShapeDtypeStruct((N_Q_HEADS, 128), F32)],
        scratch_shapes=[pltpu.VMEM((S + ATT_BLOCK, 256), BF16), pltpu.VMEM((S + ATT_BLOCK, 256), BF16),
                        pltpu.VMEM((S + ATT_BLOCK, 256), F32), pltpu.VMEM((S + ATT_BLOCK, 256), F32)],
        compiler_params=_params("arbitrary"))(proj, kk2, vv2, bias, sinks, datt)


def _hgrn_gates(z, lb):
    sg = _sigmoid(z)
    f = lb + (1.0 - lb) * sg
    return sg, f, jnp.log(f), 1.0 - f


def _hgrn_consts():
    r = lax.broadcasted_iota(jnp.int32, (CHUNK, CHUNK), 0)
    c = lax.broadcasted_iota(jnp.int32, (CHUNK, CHUNK), 1)
    tril = (r >= c).astype(BF16)
    triu = (r <= c).astype(BF16)
    causal = r >= c
    below = (r // SUB) > (c // SUB)
    return tril, triu, causal, below, r, c


def _hgrn_offdiag(q, k, b_ref):
    zero = jnp.zeros((SUB, REC_DIM), F32)
    q_rows, k_cols, eqs, eks = [jnp.zeros((SUB, (N_SUB - 1) * REC_DIM), F32)], [], [], []
    for i in range(1, N_SUB):
        p = b_ref[pl.ds(SUB * i - 1, 1), :]
        eq = jnp.exp(b_ref[pl.ds(SUB * i, SUB), :] - p)
        qi = q[SUB * i:SUB * (i + 1), :] * eq
        q_rows.append(jnp.concatenate([zero] * (i - 1) + [qi] + [zero] * (N_SUB - 1 - i), axis=1))
        ek = jnp.exp(p - b_ref[pl.ds(0, SUB * i), :])
        ki = k[:SUB * i, :] * ek
        pad = jnp.zeros((CHUNK - SUB * i, REC_DIM), F32)
        k_cols.append(jnp.concatenate([ki, pad], axis=0))
        eqs.append(eq)
        eks.append(jnp.concatenate([ek, pad], axis=0))
    return jnp.concatenate(q_rows, axis=0), jnp.concatenate(k_cols, axis=1), eqs, eks


def _hgrn_diag(q, k_ref, b_ref):
    lane = lax.broadcasted_iota(jnp.int32, (SUB, CHUNK), 1)
    rowm = lax.broadcasted_iota(jnp.int32, (SUB, CHUNK), 0)
    blocks = []
    for i in range(N_SUB):
        qi = q[SUB * i:SUB * (i + 1), :]
        bi = b_ref[pl.ds(SUB * i, SUB), :]
        d = jnp.zeros((SUB, CHUNK), F32)
        for s in range(SUB):
            ks = k_ref[pl.ds(SUB * i + s, 1), :]
            bs = b_ref[pl.ds(SUB * i + s, 1), :]
            w = jnp.exp(jnp.minimum(bi - bs, 0.0))
            colv = jnp.sum(qi * ks * w, axis=-1, keepdims=True)
            d = jnp.where((lane == SUB * i + s) & (rowm >= s), colv, d)
        blocks.append(d)
    return jnp.concatenate(blocks, axis=0)


def _hgrn_fwd(proj, lb_param, B, S):
    T = B * S
    nc = S // CHUNK

    def body(q_ref, z_ref, v_ref, lb_ref, o_ref, st_ref, k_s, b_s):
        lb = _sigmoid(lb_ref[0:1, :] - lb_ref[1:2, :])
        tril, _, _, below, _, _ = _hgrn_consts()

        def chunk(ci, ht):
            r0 = pl.multiple_of(ci * CHUNK, CHUNK)
            q = q_ref[pl.ds(r0, CHUNK), :]
            v = v_ref[pl.ds(r0, CHUNK), :]
            _, _, g, k = _hgrn_gates(z_ref[pl.ds(r0, CHUNK), :], lb)
            bcum = _sel_left(tril, g)
            k_s[...] = k
            b_s[...] = bcum
            st_ref[ci] = ht
            qst, kst, _, _ = _hgrn_offdiag(q, k, b_s)
            a = jnp.where(below, _dot_nt(qst.astype(BF16), kst.astype(BF16)), 0.0) + _hgrn_diag(q, k_s, b_s)
            vb = v.astype(BF16)
            qb = (q * jnp.exp(bcum)).astype(BF16)
            o = _dot(a.astype(BF16), vb) + _dot_nt(qb, ht.astype(BF16))
            o_ref[pl.ds(r0, CHUNK), :] = o
            b_last = b_s[pl.ds(CHUNK - 1, 1), :]
            kb = (k * jnp.exp(b_last - bcum)).astype(BF16)
            return ht * jnp.exp(b_last) + _dot_tn(vb, kb)

        lax.fori_loop(0, nc, chunk, jnp.zeros((REC_DIM, REC_DIM), F32))

    H = REC_HEADS
    return pl.pallas_call(
        body, name="hgrn_fwd", grid=(B, H),
        in_specs=[pl.BlockSpec((S, 128), lambda b, h: (b, COL_RQ + h)),
                  pl.BlockSpec((S, 128), lambda b, h: (b, COL_RF + h)),
                  pl.BlockSpec((S, 128), lambda b, h: (b, COL_RI + h)),
                  pl.BlockSpec((2, 128), lambda b, h: (0, h))],
        out_specs=[pl.BlockSpec((S, 128), lambda b, h: (b, h)),
                   pl.BlockSpec((nc, REC_DIM, REC_DIM), lambda b, h: (b * H + h, 0, 0))],
        out_shape=[jax.ShapeDtypeStruct((T, 512), F32),
                   jax.ShapeDtypeStruct((B * H * nc, REC_DIM, REC_DIM), F32)],
        scratch_shapes=[pltpu.VMEM((CHUNK, REC_DIM), F32), pltpu.VMEM((CHUNK, REC_DIM), F32)],
        compiler_params=_params("arbitrary", "arbitrary"))(proj, proj, proj, lb_param)


def _hgrn_bwd(proj, lb_param, states, do, B, S):
    T = B * S
    nc = S // CHUNK

    def body(q_ref, z_ref, v_ref, lb_ref, st_ref, do_ref, dq_ref, dz_ref, dv_ref, dlb_ref, k_s, b_s, dkd_s):
        lb = _sigmoid(lb_ref[0:1, :] - lb_ref[1:2, :])
        tril, triu, causal, below, r, _ = _hgrn_consts()
        lane = lax.broadcasted_iota(jnp.int32, (SUB, CHUNK), 1)
        rowm = lax.broadcasted_iota(jnp.int32, (SUB, CHUNK), 0)
        last_row = lax.broadcasted_iota(jnp.int32, (CHUNK, 1), 0) == CHUNK - 1

        def chunk(it, carry):
            dht, dlb = carry
            ci = nc - 1 - it
            r0 = pl.multiple_of(ci * CHUNK, CHUNK)
            q = q_ref[pl.ds(r0, CHUNK), :]
            v = v_ref[pl.ds(r0, CHUNK), :]
            dout = do_ref[pl.ds(r0, CHUNK), :]
            sg, f, g, k = _hgrn_gates(z_ref[pl.ds(r0, CHUNK), :], lb)
            bcum = _sel_left(tril, g)
            k_s[...] = k
            b_s[...] = bcum
            ht = st_ref[ci]
            qst, kst, eqs, eks = _hgrn_offdiag(q, k, b_s)
            qst_b, kst_b = qst.astype(BF16), kst.astype(BF16)
            a = jnp.where(below, _dot_nt(qst_b, kst_b), 0.0) + _hgrn_diag(q, k_s, b_s)
            vb, dob = v.astype(BF16), dout.astype(BF16)
            eb = jnp.exp(bcum)
            b_last = b_s[pl.ds(CHUNK - 1, 1), :]
            el = jnp.exp(b_last)
            ekb = jnp.exp(b_last - bcum)
            qb = (q * eb).astype(BF16)
            kb = k * ekb
            dhb = dht.astype(BF16)
            dv = _dot_tn(a.astype(BF16), dob) + _dot_nt(kb.astype(BF16), dhb)
            da = jnp.where(causal, _dot_nt(dob, vb), 0.0)
            dqb = _dot(dob, ht.astype(BF16))
            dkb = _dot(vb, dhb)
            dht_new = dht * el + _dot_tn(dob, qb)
            da_off = jnp.where(below, da, 0.0).astype(BF16)
            dqst = _dot(da_off, kst_b)
            dkst = _dot_tn(da_off, qst_b)
            dq_rows = [jnp.zeros((SUB, REC_DIM), F32)]
            dk = jnp.zeros((CHUNK, REC_DIM), F32)
            for i in range(1, N_SUB):
                dq_rows.append(dqst[SUB * i:SUB * (i + 1), REC_DIM * (i - 1):REC_DIM * i] * eqs[i - 1])
                dk = dk + dkst[:, REC_DIM * (i - 1):REC_DIM * i] * eks[i - 1]
            dq = jnp.concatenate(dq_rows, axis=0)
            dkd_s[...] = jnp.zeros_like(dkd_s)
            dq_diag = []
            for i in range(N_SUB):
                qi = q[SUB * i:SUB * (i + 1), :]
                bi = b_s[pl.ds(SUB * i, SUB), :]
                dai = da[SUB * i:SUB * (i + 1), :]
                dqi = jnp.zeros((SUB, REC_DIM), F32)
                for s in range(SUB):
                    ks = k_s[pl.ds(SUB * i + s, 1), :]
                    bs = b_s[pl.ds(SUB * i + s, 1), :]
                    w = jnp.exp(jnp.minimum(bi - bs, 0.0))
                    dacol = jnp.sum(jnp.where((lane == SUB * i + s) & (rowm >= s), dai, 0.0), axis=-1,
                                    keepdims=True)
                    dqi = dqi + dacol * ks * w
                    dkd_s[pl.ds(SUB * i + s, 1), :] += jnp.sum(dacol * qi * w, axis=0, keepdims=True)
                dq_diag.append(dqi)
            dq = dq + jnp.concatenate(dq_diag, axis=0) + eb * dqb
            dk = dk + dkd_s[...] + ekb * dkb
            edge = jnp.sum(kb * dkb, axis=0, keepdims=True) + el * jnp.sum(ht * dht, axis=0, keepdims=True)
            db = q * dq - k * dk + jnp.where(last_row, edge, 0.0)
            dg = _sel_left(triu, db)
            df = dg / f - dk
            dz = df * (1.0 - lb) * sg * (1.0 - sg)
            dlb = dlb + jnp.sum(df * (1.0 - sg), axis=0, keepdims=True)
            dq_ref[pl.ds(r0, CHUNK), :] = dq.astype(dq_ref.dtype)
            dz_ref[pl.ds(r0, CHUNK), :] = dz.astype(dz_ref.dtype)
            dv_ref[pl.ds(r0, CHUNK), :] = dv.astype(dv_ref.dtype)
            return dht_new, dlb

        _, dlb = lax.fori_loop(0, nc, chunk, (jnp.zeros((REC_DIM, REC_DIM), F32), jnp.zeros((1, REC_DIM), F32)))
        dlb_ref[...] = jnp.broadcast_to(dlb * lb * (1.0 - lb), (8, REC_DIM))

    H = REC_HEADS
    return pl.pallas_call(
        body, name="hgrn_bwd", grid=(B, H),
        in_specs=[pl.BlockSpec((S, 128), lambda b, h: (b, COL_RQ + h)),
                  pl.BlockSpec((S, 128), lambda b, h: (b, COL_RF + h)),
                  pl.BlockSpec((S, 128), lambda b, h: (b, COL_RI + h)),
                  pl.BlockSpec((2, 128), lambda b, h: (0, h)),
                  pl.BlockSpec((nc, REC_DIM, REC_DIM), lambda b, h: (b * H + h, 0, 0)),
                  pl.BlockSpec((S, 128), lambda b, h: (b, h))],
        out_specs=[pl.BlockSpec((S, 128), lambda b, h: (b, h))] * 3
        + [pl.BlockSpec((8, 128), lambda b, h: (b, h))],
        out_shape=[jax.ShapeDtypeStruct((T, 512), BF16)] * 3 + [jax.ShapeDtypeStruct((B * 8, 512), F32)],
        scratch_shapes=[pltpu.VMEM((CHUNK, REC_DIM), F32)] * 3,
        compiler_params=_params("arbitrary", "arbitrary"))(proj, proj, proj, lb_param, states, do)


def _rec_gate_fwd(rec, proj, rec_norm):
    T = rec.shape[0]

    def fn(accs, tv, cv):
        return [_rms_hat(tv[0]) * cv[0] * _sigmoid(tv[1])]

    return _tile_call("rec_gate", fn, T, 512, _pick(T, 1024), REC_DIM, tiles=[(rec, 0), (proj, COL_RG)],
                      consts=[rec_norm], outs=[BF16])[0]


def _rec_gate_bwd(dyb, w_rec_proj, rec, proj, rec_norm):
    T = rec.shape[0]

    def fn(accs, tv, cv):
        d, r, rg = accs[0], tv[0], tv[1]
        sg = _sigmoid(rg)
        rn = _rms_hat(r) * cv[0]
        dh, dg = _rms_bwd_vals(d * sg, r, cv[0])
        return [dh, d * rn * sg * (1.0 - sg), dg]

    return _tile_call("rec_gate_bwd", fn, T, 512, _pick(T, 1024), REC_DIM, pairs=[(dyb, 0, w_rec_proj, "nt")],
                      tiles=[(rec, 0), (proj, COL_RG)], consts=[rec_norm], outs=[F32, BF16], parts=1)


def _mix_out_fwd(att, recn, proj, w_att_proj, w_rec_proj, w_out, h1):
    T = att.shape[0]
    tn = 256

    def merge(accs, tv, cv):
        ya, yb = accs
        return [ya, yb, _sigmoid(tv[0]) * ya + _sigmoid(tv[1]) * yb]

    ya, yb, merged = _tile_call(
        "merge", merge, T, D_MODEL, _pick(T, 1024), tn,
        pairs=[(att, 0, w_att_proj, "nn"), (recn, 0, w_rec_proj, "nn")],
        tiles=[(proj, COL_GA * 128 // tn), (proj, COL_GB * 128 // tn)], outs=[BF16] * 3)

    def res(accs, tv, cv):
        return [tv[0] + accs[0]]

    h2 = _tile_call("mix_out", res, T, D_MODEL, _pick(T, 512), 512, pairs=[(merged, 0, w_out, "nn")],
                    tiles=[(h1, 0)], outs=[F32])[0]
    return h2, (ya, yb, merged)


def _local_step(x, p, tgt, w, B, S):
    T = B * S
    g_ffn1, g_mix, g_ffn2, g_ple = w["norm_ffn1"], w["norm_mix"], w["norm_ffn2"], w["norm_ple"]
    g_fin = w["norm_final"].reshape(1, D_MODEL)
    grads = {}

    h1, sv1 = _ffn_fwd("ffn1", x, g_ffn1, w["w_ffn1_in"], w["w_ffn1_out"])
    u = _rms_fwd("mix_norm", h1, g_mix)

    def ident(accs, tv, cv):
        return [accs[0]]

    proj = _tile_call("in_proj", ident, T, IN_W, _pick(T, 1024), 256, pairs=[(u, 0, w["w_in"], "nn")],
                      outs=[F32])[0]
    onehot = jnp.asarray(_t5_onehot())
    bias = _small_mm("t5_bias", w["rel_bias"].T, onehot.astype(BF16), "right")
    bias = bias.reshape(N_Q_HEADS, ATT_BLOCK, 2 * ATT_BLOCK)
    sinks = w["attn_sinks"].reshape(N_Q_HEADS)
    kk2, vv2 = _kv_layouts(proj)
    att = _swa_fwd(proj, kk2, vv2, bias, sinks, B, S)
    rec, states = _hgrn_fwd(proj, w["lb_param"], B, S)
    recn = _rec_gate_fwd(rec, proj, w["rec_norm"])
    h2, (ya, yb, merged) = _mix_out_fwd(att, recn, proj, w["w_att_proj"], w["w_rec_proj"], w["w_out"], h1)
    h3, sv2 = _ffn_fwd("ffn2", h2, g_ffn2, w["w_ffn2_in"], w["w_ffn2_out"])
    n3 = _rms_fwd("ple_norm", h3, g_ple)

    def ple(accs, tv, cv):
        gate = _sigmoid(accs[0])
        return [gate, accs[1], tv[0] + gate * accs[1]]

    gate_p, pp, h4 = _tile_call(
        "ple", ple, T, D_MODEL, _pick(T, 512), 512,
        pairs=[(n3, 0, w["w_ple_gate"], "nn"), (p, 0, w["w_ple_proj"], "nn")], tiles=[(h3, 0)],
        outs=[BF16, BF16, F32])

    def head(accs, tv, cv):
        h, t = tv
        err = _rms_hat(h) * cv[0] - t
        dh, dg = _rms_bwd_vals(err * (1.0 / D_MODEL), h, cv[0])
        return [dh, _group8(err * err), dg]

    dh4, loss_p, dg_fin = _tile_call("loss_head", head, T, D_MODEL, _pick(T, 256), D_MODEL,
                                     tiles=[(h4, 0), (tgt, 0)], consts=[g_fin], outs=[F32], parts=2)
    grads["norm_final"] = dg_fin

    def dple(accs, tv, cv):
        d, gt, ppv = tv[0], tv[1].astype(F32), tv[2].astype(F32)
        return [d * ppv * gt * (1.0 - gt), d * gt]

    dzg, dpp = _tile_call("ple_dact", dple, T, D_MODEL, _pick(T, 512), D_MODEL,
                          tiles=[(dh4, 0), (gate_p, 0), (pp, 0)], outs=[BF16] * 2)
    grads["w_ple_gate"] = _mm_tn_rows("ple_dwg", n3, dzg)
    grads["w_ple_proj"] = _mm_tn_cols("ple_dwp", p, dpp)

    def dnorm(accs, tv, cv):
        dh, dg = _rms_bwd_vals(accs[0], tv[0], cv[0])
        return [tv[1] + dh, dg]

    dh3, grads["norm_ple"] = _tile_call(
        "ple_dnorm", dnorm, T, D_MODEL, _pick(T, 256), D_MODEL, pairs=[(dzg, 0, w["w_ple_gate"], "nt")],
        tiles=[(h3, 0), (dh4, 0)], consts=[g_ple], outs=[F32], parts=1)

    dh2, grads["norm_ffn2"], grads["w_ffn2_in"], grads["w_ffn2_out"] = _ffn_bwd(
        "ffn2b", dh3, h2, g_ffn2, w["w_ffn2_in"], w["w_ffn2_out"], sv2)

    def to_bf(accs, tv, cv):
        return [tv[0]]

    dh2b = _tile_call("mix_dcast", to_bf, T, D_MODEL, _pick(T, 512), D_MODEL, tiles=[(dh2, 0)], outs=[BF16])[0]
    grads["w_out"] = _mm_tn_rows("mix_dwout", merged, dh2b)
    tn = 256

    def dmerge(accs, tv, cv):
        dm = accs[0]
        sa, sb = _sigmoid(tv[0]), _sigmoid(tv[1])
        yav, ybv = tv[2].astype(F32), tv[3].astype(F32)
        return [dm * sa, dm * sb, dm * yav * sa * (1.0 - sa), dm * ybv * sb * (1.0 - sb)]

    dya, dyb, dga, dgb = _tile_call(
        "mix_dmerge", dmerge, T, D_MODEL, _pick(T, 1024), tn, pairs=[(dh2b, 0, w["w_out"], "nt")],
        tiles=[(proj, COL_GA * 128 // tn), (proj, COL_GB * 128 // tn), (ya, 0), (yb, 0)], outs=[BF16] * 4)
    grads["w_att_proj"] = _mm_tn_cols("mix_dwatt", att, dya)
    grads["w_rec_proj"] = _mm_tn_cols("mix_dwrec", recn, dyb)

    datt = _tile_call("mix_datt", ident, T, 512, _pick(T, 1024), 512, pairs=[(dya, 0, w["w_att_proj"], "nt")],
                      outs=[BF16])[0]
    drec, drg, grads["rec_norm"] = _rec_gate_bwd(dyb, w["w_rec_proj"], rec, proj, w["rec_norm"])

    drq, drf, dri, dlb = _hgrn_bwd(proj, w["lb_param"], states, drec, B, S)
    grads["lb_param"] = dlb
    daq, dk2, dv2, dbias, dsink = _swa_bwd(proj, kk2, vv2, bias, sinks, datt, B, S)
    grads["attn_sinks"] = dsink
    grads["rel_bias"] = _small_mm("t5_dbias", dbias.reshape(N_Q_HEADS, -1), onehot.T.astype(BF16), "right")
    dak = (dk2[:, :128] + _swap_heads(dk2[:, 128:])).astype(BF16)
    dav = (dv2[:, :128] + _swap_heads(dv2[:, 128:])).astype(BF16)
    dproj = jnp.concatenate([daq, dak, dav, drq, drf, dri, drg, dga, dgb], axis=1)
    tk = _pick(T, 1024, 128)
    w_in_shard = IN_W // N_CHIPS
    gw32, gw16 = _mm_tn("mix_dwin", (1, IN_W // 256, T // tk),
                        (u, (tk, D_MODEL), lambda i, j, k: (k, 0)), (dproj, (tk, 256), lambda i, j, k: (k, j)),
                        _grad_pair((D_MODEL, IN_W), (D_MODEL, 256), lambda i, j, k: (0, j)))
    to_sh = lambda t: t.reshape(D_MODEL, N_CHIPS, w_in_shard).transpose(1, 0, 2)
    grads["w_in"] = (to_sh(gw32), to_sh(gw16))

    def dnorm_mix(accs, tv, cv):
        dh, dg = _rms_bwd_vals(accs[0], tv[0], cv[0])
        return [tv[1] + dh, dg]

    dh1, grads["norm_mix"] = _tile_call(
        "mix_dnorm", dnorm_mix, T, D_MODEL, _pick(T, 256), D_MODEL, pairs=[(dproj, 0, w["w_in"], "nt")],
        tiles=[(h1, 0), (dh2, 0)], consts=[g_mix], outs=[F32], parts=1)

    dx, grads["norm_ffn1"], grads["w_ffn1_in"], grads["w_ffn1_out"] = _ffn_bwd(
        "ffn1b", dh1, x, g_ffn1, w["w_ffn1_in"], w["w_ffn1_out"], sv1)
    return loss_p, dx, grads


def _place():
    x, y, c = lax.axis_index("x"), lax.axis_index("y"), lax.axis_index("c")
    return x, y, c


def _other_chips(x, y):
    return [(1 - x, y, 2 * (1 - x) + y), (x, 1 - y, 2 * x + 1 - y), (1 - x, 1 - y, 2 * (1 - x) + 1 - y)]


ANY = pl.BlockSpec(memory_space=pl.ANY)


def _half_rows(ref_3d, chip, h, rows):
    return ref_3d.at[chip, pl.ds(h * rows, rows), :]


def _gather_weights(ws):
    nw = len(ws)

    def body(*refs):
        w_refs, out_refs = refs[:nw], refs[nw:2 * nw]
        send_sems, recv_sems = refs[2 * nw], refs[2 * nw + 1]
        x, y, c = _place()
        me = 2 * x + y
        sibling = (x, y, 1 - c)
        chips = _other_chips(x, y)

        def copy(i, k, chip, h, to, src=None):
            half = ws[i].shape[0] // 2
            dst = _half_rows(out_refs[i], chip, h, half)
            return pltpu.make_async_remote_copy(
                src_ref=dst if src is None else src, dst_ref=dst,
                send_sem=send_sems.at[6 * i + k], recv_sem=recv_sems.at[6 * i + k], device_id=to, device_id_type=MESH)

        first = []
        for i in range(nw):
            half = ws[i].shape[0] // 2
            first += [copy(i, j, me, c, (cx, cy, c), src=w_refs[i].at[pl.ds(c * half, half), :])
                      for j, (cx, cy, _) in enumerate(chips)]
        for cp in first:
            cp.start()
        passed = []
        for i in range(nw):
            for j, (cx, cy, ci) in enumerate(chips):
                copy(i, j, ci, c, (cx, cy, c)).wait_recv()
                fw = copy(i, 3 + j, ci, c, sibling)
                fw.start()
                passed.append(fw)
        for i in range(nw):
            for j, (_, _, ci) in enumerate(chips):
                copy(i, 3 + j, ci, 1 - c, sibling).wait_recv()
        for cp in first + passed:
            cp.wait_send()

    return pl.pallas_call(
        body, name="gather_weights", in_specs=[ANY] * nw, out_specs=[ANY] * nw,
        out_shape=[jax.ShapeDtypeStruct((N_CHIPS,) + w.shape, w.dtype) for w in ws],
        scratch_shapes=[pltpu.SemaphoreType.DMA((6 * nw,)), pltpu.SemaphoreType.DMA((6 * nw,))],
    )(*ws)


def _swap_halves(gs):
    nw = len(gs)

    def body(*refs):
        g_refs, out_refs, send_sems, recv_sems = refs[:nw], refs[nw:2 * nw], refs[2 * nw], refs[2 * nw + 1]
        x, y, c = _place()
        cps = []
        for i in range(nw):
            half = gs[i].shape[1] // 2
            cps.append(pltpu.make_async_remote_copy(
                src_ref=g_refs[i].at[:, pl.ds((1 - c) * half, half), :], dst_ref=out_refs[i],
                send_sem=send_sems.at[i], recv_sem=recv_sems.at[i], device_id=(x, y, 1 - c), device_id_type=MESH))
        for cp in cps:
            cp.start()
        for cp in cps:
            cp.wait()

    return pl.pallas_call(
        body, name="rs_sibling", in_specs=[ANY] * nw, out_specs=[ANY] * nw,
        out_shape=[jax.ShapeDtypeStruct((N_CHIPS, g.shape[1] // 2, g.shape[2]), g.dtype) for g in gs],
        scratch_shapes=[pltpu.SemaphoreType.DMA((nw,)), pltpu.SemaphoreType.DMA((nw,))],
    )(*gs)


def _scatter_chips(ps):
    nw = len(ps)

    def body(*refs):
        p_refs, out_refs, send_sems, recv_sems = refs[:nw], refs[nw:2 * nw], refs[2 * nw], refs[2 * nw + 1]
        x, y, c = _place()
        cps = []
        for i in range(nw):
            for j, (cx, cy, ci) in enumerate(_other_chips(x, y)):
                cps.append(pltpu.make_async_remote_copy(
                    src_ref=p_refs[i].at[ci], dst_ref=out_refs[i].at[j], send_sem=send_sems.at[3 * i + j],
                    recv_sem=recv_sems.at[3 * i + j], device_id=(cx, cy, c), device_id_type=MESH))
        for cp in cps:
            cp.start()
        for cp in cps:
            cp.wait()

    return pl.pallas_call(
        body, name="rs_chips", in_specs=[ANY] * nw, out_specs=[ANY] * nw,
        out_shape=[jax.ShapeDtypeStruct((3,) + p.shape[1:], p.dtype) for p in ps],
        scratch_shapes=[pltpu.SemaphoreType.DMA((3 * nw,)), pltpu.SemaphoreType.DMA((3 * nw,))],
    )(*ps)


def _join_halves(ss):
    nw = len(ss)

    def body(*refs):
        s_refs, out_refs, send_sems, recv_sems = refs[:nw], refs[nw:2 * nw], refs[2 * nw], refs[2 * nw + 1]
        x, y, c = _place()
        cps = [pltpu.make_async_remote_copy(
            src_ref=s_refs[i], dst_ref=out_refs[i], send_sem=send_sems.at[i], recv_sem=recv_sems.at[i],
            device_id=(x, y, 1 - c), device_id_type=MESH) for i in range(nw)]
        for cp in cps:
            cp.start()
        for cp in cps:
            cp.wait()

    return pl.pallas_call(
        body, name="rs_join", in_specs=[ANY] * nw, out_specs=[ANY] * nw,
        out_shape=[jax.ShapeDtypeStruct(s.shape, s.dtype) for s in ss],
        scratch_shapes=[pltpu.SemaphoreType.DMA((nw,)), pltpu.SemaphoreType.DMA((nw,))],
    )(*ss)


def _allreduce_small(sp):
    def body(s_ref, out_ref, slots, send_sems, recv_sems):
        x, y, c = _place()
        me = 4 * x + 2 * y + c
        slots[me] = s_ref[...]
        cps = []
        for r in range(1, N_DEV):
            px, py, pc = x ^ (r >> 2), y ^ ((r >> 1) & 1), c ^ (r & 1)
            cps.append(pltpu.make_async_remote_copy(
                src_ref=s_ref, dst_ref=slots.at[me], send_sem=send_sems.at[r - 1], recv_sem=recv_sems.at[r - 1],
                device_id=(px, py, pc), device_id_type=MESH))
        for cp in cps:
            cp.start()
        for r in range(1, N_DEV):
            px, py, pc = x ^ (r >> 2), y ^ ((r >> 1) & 1), c ^ (r & 1)
            pltpu.make_async_remote_copy(
                src_ref=s_ref, dst_ref=slots.at[4 * px + 2 * py + pc], send_sem=send_sems.at[r - 1],
                recv_sem=recv_sems.at[r - 1], device_id=(px, py, pc), device_id_type=MESH).wait_recv()
        for cp in cps:
            cp.wait_send()
        acc = slots[0]
        for d in range(1, N_DEV):
            acc = acc + slots[d]
        out_ref[...] = acc

    return pl.pallas_call(
        body, name="allreduce_small",
        in_specs=[pl.BlockSpec(memory_space=pltpu.VMEM)], out_specs=pl.BlockSpec(memory_space=pltpu.VMEM),
        out_shape=jax.ShapeDtypeStruct(sp.shape, F32),
        scratch_shapes=[pltpu.VMEM((N_DEV,) + sp.shape, F32), pltpu.SemaphoreType.DMA((N_DEV - 1,)),
                        pltpu.SemaphoreType.DMA((N_DEV - 1,))],
    )(sp)


def _scalar(v):
    return jnp.reshape(v, (1,)).astype(jnp.int32)


def _row_tile(h, dtype_mult=16):
    return _pick(h, 256, dtype_mult)


def _add_sibling(name, g32, from_sib, c):
    _, r, n = g32.shape
    h = r // 2
    th = _row_tile(h)
    nt = h // th

    def body(c_ref, g_ref, s_ref, o32_ref, o16_ref):
        s = g_ref[...] + s_ref[...].astype(F32)
        o32_ref[...] = s
        o16_ref[...] = s.astype(BF16)

    blk = (None, th, n)
    return pl.pallas_call(
        body, name=name,
        grid_spec=pltpu.PrefetchScalarGridSpec(
            num_scalar_prefetch=1, grid=(N_CHIPS, nt),
            in_specs=[pl.BlockSpec(blk, lambda k, t, c_ref: (k, c_ref[0] * nt + t, 0)),
                      pl.BlockSpec(blk, lambda k, t, c_ref: (k, t, 0))],
            out_specs=[pl.BlockSpec(blk, lambda k, t, c_ref: (k, t, 0))] * 2),
        out_shape=[jax.ShapeDtypeStruct((N_CHIPS, h, n), F32), jax.ShapeDtypeStruct((N_CHIPS, h, n), BF16)],
        compiler_params=_params("arbitrary", "arbitrary"))(_scalar(c), g32, from_sib)


def _add_chips(name, p32, from_chips, me_chip):
    _, h, n = p32.shape
    th = _row_tile(h)

    def body(m_ref, p_ref, a_ref, b_ref, c_ref, o_ref):
        o_ref[...] = p_ref[...] + a_ref[...].astype(F32) + b_ref[...].astype(F32) + c_ref[...].astype(F32)

    blk = (None, th, n)
    return pl.pallas_call(
        body, name=name,
        grid_spec=pltpu.PrefetchScalarGridSpec(
            num_scalar_prefetch=1, grid=(h // th,),
            in_specs=[pl.BlockSpec(blk, lambda t, m_ref: (m_ref[0], t, 0))]
            + [pl.BlockSpec(blk, lambda t, m_ref, j=j: (j, t, 0)) for j in range(3)],
            out_specs=pl.BlockSpec((th, n), lambda t, m_ref: (t, 0))),
        out_shape=jax.ShapeDtypeStruct((h, n), F32),
        compiler_params=_params("arbitrary"))(_scalar(me_chip), p32, from_chips, from_chips, from_chips)


def _adamw_vals(w, g, m, v):
    m = ADAM_B1 * m + (1.0 - ADAM_B1) * g
    v = ADAM_B2 * v + (1.0 - ADAM_B2) * (g * g)
    m_hat = m / (1.0 - ADAM_B1 ** ADAM_STEP)
    v_hat = v / (1.0 - ADAM_B2 ** ADAM_STEP)
    delta = -ADAM_LR * (m_hat / (jnp.sqrt(v_hat) + ADAM_EPS) + ADAM_WD * w)
    return delta, m, v


def _adamw_halves(name, w, m, v, g_mine, g_sib, c):
    r, n = w.shape
    h = r // 2
    th = _row_tile(h, 8)
    nt = h // th

    def body(c_ref, w_ref, m_ref, v_ref, a_ref, b_ref, g_ref, d_ref, nm_ref, nv_ref):
        mine = (pl.program_id(0) // nt) == c_ref[0]
        g = jnp.where(mine, a_ref[...], b_ref[...])
        d, nm, nv = _adamw_vals(w_ref[...], g, m_ref[...], v_ref[...])
        g_ref[...] = g
        d_ref[...] = d
        nm_ref[...] = nm
        nv_ref[...] = nv

    full = pl.BlockSpec((th, n), lambda t, c_ref: (t, 0))
    part = pl.BlockSpec((th, n), lambda t, c_ref: (t % nt, 0))
    return pl.pallas_call(
        body, name=name,
        grid_spec=pltpu.PrefetchScalarGridSpec(
            num_scalar_prefetch=1, grid=(2 * nt,), in_specs=[full, full, full, part, part], out_specs=[full] * 4),
        out_shape=[jax.ShapeDtypeStruct((r, n), F32)] * 4,
        compiler_params=_params("arbitrary"))(_scalar(c), w, m, v, g_mine, g_sib)


def _adamw(name, w, g, m, v):
    R, W = w.shape

    def fn(accs, tv, cv):
        return list(_adamw_vals(*tv))

    return _tile_call(name, fn, R, W, _pick(R, 256), W, tiles=[(w, 0), (g, 0), (m, 0), (v, 0)], outs=[F32] * 3)


SMALL_LAYOUT = (("rel_bias", 2, 256), ("lb_param", 8, 1024), ("norm_ffn1", 8, 1024), ("norm_mix", 8, 1024),
                ("attn_sinks", 1, 8), ("rec_norm", 1, 128), ("norm_ffn2", 8, 1024), ("norm_ple", 8, 1024),
                ("norm_final", 8, 1024), ("loss", 8, 1024))


def _pack_small(vals):
    rows = []
    for name, nrows, n in SMALL_LAYOUT:
        flat = vals[name].reshape(-1)
        flat = jnp.pad(flat, (0, nrows * 128 - n))
        rows.append(flat.reshape(nrows, 128))
    packed = jnp.concatenate(rows, axis=0)
    return jnp.pad(packed, ((0, SMALL_ROWS - packed.shape[0]), (0, 0)))


def _unpack_small(packed, shapes):
    out, r = {}, 0
    for name, nrows, n in SMALL_LAYOUT:
        out[name] = packed[r:r + nrows].reshape(-1)[:n].reshape(shapes[name])
        r += nrows
    return out


def _natural(name, s):
    if name in COL_SHARDED:
        return s.transpose(1, 0, 2).reshape(s.shape[1], -1)
    return s.reshape(-1, s.shape[2])


def kernel(x, p, rel_bias, lb_param, norm_ffn1, w_ffn1_in, w_ffn1_out, norm_mix, w_in, attn_sinks, rec_norm, w_att_proj, w_rec_proj, w_out, norm_ffn2, w_ffn2_in, w_ffn2_out, norm_ple, w_ple_gate, w_ple_proj, norm_final, loss_target, m_rel_bias, m_lb_param, m_norm_ffn1, m_w_ffn1_in, m_w_ffn1_out, m_norm_mix, m_w_in, m_attn_sinks, m_rec_norm, m_w_att_proj, m_w_rec_proj, m_w_out, m_norm_ffn2, m_w_ffn2_in, m_w_ffn2_out, m_norm_ple, m_w_ple_gate, m_w_ple_proj, m_norm_final, v_rel_bias, v_lb_param, v_norm_ffn1, v_w_ffn1_in, v_w_ffn1_out, v_norm_mix, v_w_in, v_attn_sinks, v_rec_norm, v_w_att_proj, v_w_rec_proj, v_w_out, v_norm_ffn2, v_w_ffn2_in, v_w_ffn2_out, v_norm_ple, v_w_ple_gate, v_w_ple_proj, v_norm_final):
    args = dict(locals())
    wsh = {n: args[n] for n in WEIGHTS}
    B, S = x.shape[0], x.shape[1]
    T = B * S
    cx, cy, cc = _place()
    me_chip = 2 * cx + cy

    mine16 = [wsh[n][0].astype(BF16) for n in BIG]
    gathered = _gather_weights(mine16)
    wfull = {}
    for n, own, got in zip(BIG, mine16, gathered):
        full = lax.dynamic_update_index_in_dim(got, own, me_chip, 0)
        wfull[n] = full if n in ("w_ffn1_in", "w_ffn2_in") else _natural(n, full)
    for n in SMALL:
        wfull[n] = wsh[n]

    loss_p, dx, grads = _local_step(x.reshape(T, D_MODEL), p.reshape(T, PLE_DIM),
                                    loss_target.reshape(T, D_MODEL), wfull, B, S)

    from_sib = _swap_halves([grads[n][1] for n in BIG])
    part = [_add_sibling("rs_add_sib_" + n, grads[n][0], fs, cc) for n, fs in zip(BIG, from_sib)]
    from_chips = _scatter_chips([pt[1] for pt in part])
    s_mine = [_add_chips("rs_add_chips_" + n, pt[0], fc, me_chip) for n, pt, fc in zip(BIG, part, from_chips)]
    s_sib = _join_halves(s_mine)

    small_vals = {
        "rel_bias": grads["rel_bias"].T,
        "lb_param": jnp.concatenate([_colsum("dlb_sum", grads["lb_param"]),
                                     -_colsum("dlb_sum2", grads["lb_param"])], axis=0) / 8.0,
        "attn_sinks": grads["attn_sinks"][:, 0],
        "rec_norm": _colsum("drn_sum", grads["rec_norm"]).reshape(REC_HEADS, REC_DIM).sum(axis=0),
        "loss": _colsum("loss_sum", loss_p),
    }
    for n in ("norm_ffn1", "norm_mix", "norm_ffn2", "norm_ple", "norm_final"):
        small_vals[n] = _colsum(n + "_sum", grads[n])
    red = _allreduce_small(_pack_small(small_vals))
    small_shapes = {n: wsh[n].shape for n in SMALL}
    small_shapes["loss"] = (D_MODEL,)
    small = _unpack_small(red, small_shapes)
    loss = 0.5 * jnp.sum(small["loss"]) / D_MODEL

    out_g, out_d, out_m, out_v = {}, {}, {}, {}
    for n, gm, gs in zip(BIG, s_mine, s_sib):
        res = _adamw_halves("adamw_" + n, wsh[n][0], args["m_" + n][0], args["v_" + n][0], gm, gs, cc)
        out_g[n], out_d[n], out_m[n], out_v[n] = (t[None] for t in res)
    sw = _pack_small({**{n: wsh[n] for n in SMALL}, "loss": jnp.zeros((D_MODEL,), F32)})
    sm = _pack_small({**{n: args["m_" + n] for n in SMALL}, "loss": jnp.zeros((D_MODEL,), F32)})
    sv = _pack_small({**{n: args["v_" + n] for n in SMALL}, "loss": jnp.ones((D_MODEL,), F32)})
    sd, snm, snv = _adamw("adamw_small", sw, red, sm, sv)
    ud, um, uv = (_unpack_small(t, small_shapes) for t in (sd, snm, snv))
    for n in SMALL:
        out_g[n], out_d[n], out_m[n], out_v[n] = small[n], ud[n], um[n], uv[n]

    return (loss, dx.reshape(B, S, D_MODEL), *[out_g[n] for n in WEIGHTS], *[out_d[n] for n in WEIGHTS],
            *[out_m[n] for n in WEIGHTS], *[out_v[n] for n in WEIGHTS])
```

```python
import numpy as np
import jax
import jax.numpy as jnp
from jax import lax
from jax.experimental import pallas as pl
from jax.experimental.pallas import tpu as pltpu

F32 = jnp.float32
BF16 = jnp.bfloat16
MESH = pl.DeviceIdType.MESH

D_MODEL = 1024
D_FF = 2816
FF_SHARD = 2 * D_FF // 4
HEAD_DIM = 64
N_Q_HEADS = 8
ATT_BLOCK = 128
N_BUCKETS = 32
MAX_DISTANCE = 128
REC_HEADS = 4
REC_DIM = 128
PLE_DIM = 256
EPS = 1e-6
IN_W = 4864
COL_AQ, COL_AK, COL_AV, COL_RQ, COL_RF, COL_RI, COL_RG, COL_GA, COL_GB = 0, 4, 5, 6, 10, 14, 18, 22, 30

CHUNK = 64
SUB = 16
N_SUB = CHUNK // SUB

ADAM_LR, ADAM_B1, ADAM_B2, ADAM_EPS, ADAM_WD, ADAM_STEP = 0.001, 0.9, 0.999, 1e-08, 0.01, 10

V7X_VMEM_LIMIT = 56 * 1024 * 1024
N_CHIPS = 4
N_DEV = 8

BIG = ("w_ffn1_in", "w_ffn1_out", "w_in", "w_att_proj", "w_rec_proj", "w_out",
       "w_ffn2_in", "w_ffn2_out", "w_ple_gate", "w_ple_proj")
COL_SHARDED = ("w_ffn1_in", "w_in", "w_att_proj", "w_rec_proj", "w_ffn2_in", "w_ple_proj")
WEIGHTS = ("rel_bias", "lb_param", "norm_ffn1", "w_ffn1_in", "w_ffn1_out", "norm_mix", "w_in", "attn_sinks",
           "rec_norm", "w_att_proj", "w_rec_proj", "w_out", "norm_ffn2", "w_ffn2_in", "w_ffn2_out", "norm_ple",
           "w_ple_gate", "w_ple_proj", "norm_final")
SMALL = tuple(n for n in WEIGHTS if n not in BIG)
SMALL_ROWS = 64


def _params(*sem):
    return pltpu.CompilerParams(dimension_semantics=sem, vmem_limit_bytes=V7X_VMEM_LIMIT)


def _pick(n, cap, mult=8):
    if n <= cap:
        return n
    for t in range(cap - cap % mult, 0, -mult):
        if n % t == 0:
            return t
    raise ValueError((n, cap, mult))


def _dot(a, b):
    return jnp.dot(a, b, preferred_element_type=F32)


def _dot_nt(a, b):
    return lax.dot_general(a, b, (((1,), (1,)), ((), ())), preferred_element_type=F32)


def _dot_tn(a, b):
    return lax.dot_general(a, b, (((0,), (0,)), ((), ())), preferred_element_type=F32)


def _split3(x):
    hi = x.astype(BF16)
    r = x - hi.astype(F32)
    mid = r.astype(BF16)
    lo = (r - mid.astype(F32)).astype(BF16)
    return hi, mid, lo


def _sel_left(sel_bf16, x):
    hi, mid, lo = _split3(x)
    return _dot(sel_bf16, hi) + _dot(sel_bf16, mid) + _dot(sel_bf16, lo)


def _sel_right(x, sel_bf16):
    hi, mid, lo = _split3(x)
    return _dot(hi, sel_bf16) + _dot(mid, sel_bf16) + _dot(lo, sel_bf16)


def _sigmoid(x):
    return 1.0 / (1.0 + jnp.exp(-x))


def _group8(x):
    r, w = x.shape
    return x.reshape(r // 8, 8, w).sum(axis=0)


class _Comm:
    def __init__(self, ins, out_shapes, n_sems, start, finish):
        self.ins, self.out_shapes, self.n_sems, self.start, self.finish = ins, out_shapes, n_sems, start, finish


ANY = pl.BlockSpec(memory_space=pl.ANY)


def _comm_parts(comm):
    if comm is None:
        return [], [], [], []
    sems = [pltpu.SemaphoreType.DMA((comm.n_sems,)), pltpu.SemaphoreType.DMA((comm.n_sems,))]
    return list(comm.ins), [ANY] * len(comm.ins), list(comm.out_shapes), sems


def _comm_run(comm, grid, refs, n_in, n_out):
    if comm is None:
        return (lambda: None), (lambda: None)
    nci, nco = len(comm.ins), len(comm.out_shapes)
    cin = refs[n_in:n_in + nci]
    cout = refs[n_in + nci + n_out:n_in + nci + n_out + nco]
    send_sems, recv_sems = refs[-2], refs[-1]
    ids = [pl.program_id(d) for d in range(len(grid))]
    is_first = ids[0] == 0
    is_last = ids[0] == grid[0] - 1
    for d in range(1, len(grid)):
        is_first = is_first & (ids[d] == 0)
        is_last = is_last & (ids[d] == grid[d] - 1)

    def first():
        @pl.when(is_first)
        def _():
            comm.start(cin, cout, send_sems, recv_sems)

    def last():
        @pl.when(is_last)
        def _():
            comm.finish(cin, cout, send_sems, recv_sems)

    return first, last


def _call(name, fn, grid, ins, outs, pairs=(), comm=None):
    in_pair = {i for p in pairs for i in p[:2]}
    n_in, n_out = len(ins), len(outs)
    c_arrays, c_in_specs, c_out_shapes, c_sems = _comm_parts(comm)

    def body(*refs):
        first, last = _comm_run(comm, grid, refs, n_in, n_out)
        first()
        accs = []
        for ia, ib, kind in pairs:
            a, b = refs[ia][...].astype(BF16), refs[ib][...].astype(BF16)
            accs.append(_dot(a, b) if kind == "nn" else _dot_nt(a, b))
        vals = [refs[i][...] for i in range(n_in) if i not in in_pair]
        res = fn(accs, vals)
        out_refs = refs[n_in + len(c_arrays):n_in + len(c_arrays) + n_out]
        assert len(res) == len(out_refs), (name, len(res), len(out_refs))
        for o_ref, val in zip(out_refs, res):
            o_ref[...] = val.astype(o_ref.dtype)
        last()

    return pl.pallas_call(
        body, name=name, grid=grid,
        in_specs=[pl.BlockSpec(blk, im) for _, blk, im in ins] + c_in_specs,
        out_specs=[pl.BlockSpec(blk, im) for _, _, blk, im in outs] + [ANY] * len(c_out_shapes),
        out_shape=[jax.ShapeDtypeStruct(shp, dt) for shp, dt, _, _ in outs] + c_out_shapes,
        scratch_shapes=c_sems,
        compiler_params=_params(*(["arbitrary"] * len(grid))))(*[a for a, _, _ in ins], *c_arrays)


def _tile_call(name, fn, M, N, tm, tn, *, pairs=(), tiles=(), consts=(), outs=(), parts=0, comm=None):
    gi, gj = M // tm, N // tn
    assert gi * tm == M and gj * tn == N, (name, M, N, tm, tn)
    ins, prs = [], []
    for a, a_col, b, kind in pairs:
        K = b.shape[0] if kind == "nn" else b.shape[1]
        ins.append((a, (tm, K), lambda i, j, c=a_col: (i, c)))
        if kind == "nn":
            ins.append((b, (K, tn), lambda i, j: (0, j)))
        else:
            ins.append((b, (tn, K), lambda i, j: (j, 0)))
        prs.append((len(ins) - 2, len(ins) - 1, kind))
    for arr, off in tiles:
        ins.append((arr, (tm, tn), lambda i, j, o=off: (i, j + o)))
    for arr in consts:
        ins.append((arr, arr.shape, lambda i, j: (0, 0)))
    out_l = [((M, N), dt, (tm, tn), lambda i, j: (i, j)) for dt in outs]
    out_l += [((gi * 8, N), F32, (8, tn), lambda i, j: (i, j))] * parts
    nt = len(tiles)

    def wrapped(accs, vals):
        return fn(accs, vals[:nt], vals[nt:])

    return _call(name, wrapped, (gi, gj), ins, out_l, prs, comm=comm)


def _mm_tn(name, grid, a_in, b_in, outs):
    nk = grid[2]
    tm = [d for d in a_in[1] if d is not None][1]
    tn = [d for d in b_in[1] if d is not None][1]

    def body(a_ref, b_ref, *rest):
        out_refs, acc_ref = rest[:-1], rest[-1]
        k = pl.program_id(2)

        @pl.when(k == 0)
        def _():
            acc_ref[...] = jnp.zeros_like(acc_ref)

        acc_ref[...] += _dot_tn(a_ref[...].astype(BF16), b_ref[...].astype(BF16))

        @pl.when(k == nk - 1)
        def _():
            for o_ref in out_refs:
                o_ref[...] = acc_ref[...].astype(o_ref.dtype)

    return pl.pallas_call(
        body, name=name, grid=grid,
        in_specs=[pl.BlockSpec(a_in[1], a_in[2]), pl.BlockSpec(b_in[1], b_in[2])],
        out_specs=[pl.BlockSpec(blk, im) for _, _, blk, im in outs],
        out_shape=[jax.ShapeDtypeStruct(shp, dt) for shp, dt, _, _ in outs],
        scratch_shapes=[pltpu.VMEM((tm, tn), F32)],
        compiler_params=_params("arbitrary", "arbitrary", "arbitrary"))(a_in[0], b_in[0])


def _grad_pair(shape, block, imap):
    return [(shape, F32, block, imap), (shape, BF16, block, imap)]


def _mm_tn_rows(name, a, b, tk=1024):
    T, a_w = a.shape
    b_w = b.shape[1]
    tm = _pick(a_w, 1408, 128)
    tk = _pick(T, tk, 128)
    g32, g16 = _mm_tn(name, (a_w // tm, 1, T // tk),
                      (a, (tk, tm), lambda i, j, k: (k, i)), (b, (tk, b_w), lambda i, j, k: (k, 0)),
                      _grad_pair((a_w, b_w), (tm, b_w), lambda i, j, k: (i, 0)))
    shp = (N_CHIPS, a_w // N_CHIPS, b_w)
    return g32.reshape(shp), g16.reshape(shp)


def _mm_tn_cols(name, a, b, tk=1024):
    T, a_w = a.shape
    n = b.shape[1] // N_CHIPS
    tk = _pick(T, tk, 128)
    return _mm_tn(name, (1, N_CHIPS, T // tk),
                  (a, (tk, a_w), lambda i, j, k: (k, 0)), (b, (tk, n), lambda i, j, k: (k, j)),
                  _grad_pair((N_CHIPS, a_w, n), (None, a_w, n), lambda i, j, k: (j, 0, 0)))


def _colsum(name, x):
    def body(x_ref, o_ref):
        o_ref[...] = jnp.sum(x_ref[...], axis=0, keepdims=True)
    return pl.pallas_call(body, name=name, out_shape=jax.ShapeDtypeStruct((1, x.shape[1]), F32))(x)


def _rms_hat(h):
    return h * lax.rsqrt(jnp.mean(h * h, axis=-1, keepdims=True) + EPS)


def _rms_bwd_vals(dn, h, g):
    r = lax.rsqrt(jnp.mean(h * h, axis=-1, keepdims=True) + EPS)
    nh = h * r
    gd = dn * g
    dh = r * (gd - nh * jnp.mean(gd * nh, axis=-1, keepdims=True))
    return dh, _group8(dn * nh)


def _rms_fwd(name, h, g, tm=512):
    T = h.shape[0]

    def fn(accs, tv, cv):
        return [_rms_hat(tv[0]) * cv[0]]

    return _tile_call(name, fn, T, D_MODEL, _pick(T, tm), D_MODEL, tiles=[(h, 0)], consts=[g], outs=[BF16])[0]


def _ffn_fwd(tag, h, g, w_in, w_out, comm_in=None, comm_out=None):
    T = h.shape[0]
    n = _rms_fwd(tag + "_norm", h, g)
    tm = _pick(T, 512)
    wblk = (None, D_MODEL, FF_SHARD)

    def act(accs, vals):
        gate, up = accs
        return [gate, up, gate * _sigmoid(gate) * up]

    tile = lambda: ((T, D_FF), BF16, (tm, FF_SHARD), lambda i, j: (i, j))
    gate, up, a, *got_in = _call(
        tag + "_in", act, (T // tm, 2),
        [(n, (tm, D_MODEL), lambda i, j: (i, 0)),
         (w_in, wblk, lambda i, j: (j, 0, 0)), (w_in, wblk, lambda i, j: (j + 2, 0, 0))],
        [tile(), tile(), tile()], pairs=[(0, 1, "nn"), (0, 2, "nn")], comm=comm_in)

    def res(accs, tv, cv):
        return [tv[0] + 0.5 * accs[0]]

    h_new, *got_out = _tile_call(tag + "_out", res, T, D_MODEL, _pick(T, 512), 512,
                                 pairs=[(a, 0, w_out, "nn")], tiles=[(h, 0)], outs=[F32], comm=comm_out)
    return h_new, (n, gate, up, a), got_in, got_out


def _ffn_bwd(tag, dh_out, h, g, w_in, w_out, saved, comm=None):
    T = h.shape[0]
    n, gate, up, a = saved
    tm = _pick(T, 512)

    def half(accs, tv, cv):
        return [0.5 * tv[0]]

    df = _tile_call(tag + "_df", half, T, D_MODEL, _pick(T, 512), D_MODEL, tiles=[(dh_out, 0)], outs=[BF16])[0]

    def dact(accs, vals):
        da = accs[0]
        gt, u = vals[0].astype(F32), vals[1].astype(F32)
        sg = _sigmoid(gt)
        silu = gt * sg
        return [jnp.stack([da * u * (sg + silu * (1.0 - sg)), da * silu])]

    dz, *got = _call(
        tag + "_dact", dact, (T // tm, 2),
        [(df, (tm, D_MODEL), lambda i, j: (i, 0)), (w_out, (FF_SHARD, D_MODEL), lambda i, j: (j, 0)),
         (gate, (tm, FF_SHARD), lambda i, j: (i, j)), (up, (tm, FF_SHARD), lambda i, j: (i, j))],
        [((2, T, D_FF), BF16, (2, tm, FF_SHARD), lambda i, j: (0, i, j))], pairs=[(0, 1, "nt")], comm=comm)
    dw_out = _mm_tn_rows(tag + "_dwout", a, df)
    tk = _pick(T, 1024, 128)
    dw_in = _mm_tn(tag + "_dwin", (1, N_CHIPS, T // tk),
                   (n, (tk, D_MODEL), lambda i, j, k: (k, 0)),
                   (dz, (None, tk, FF_SHARD), lambda i, j, k: (j // 2, k, j % 2)),
                   _grad_pair((N_CHIPS, D_MODEL, FF_SHARD), (None, D_MODEL, FF_SHARD), lambda i, j, k: (j, 0, 0)))

    def dnorm(accs, vals):
        dn = accs[0] + accs[1] + accs[2] + accs[3]
        dh, dg = _rms_bwd_vals(dn, vals[0], vals[2])
        return [vals[1] + dh, dg]

    tm2 = _pick(T, 256)
    ins = [(dz, (None, tm2, FF_SHARD), lambda i, j, s=s: (s // 2, i, s % 2)) for s in range(N_CHIPS)]
    ins += [(w_in, (None, D_MODEL, FF_SHARD), lambda i, j, s=s: (s, 0, 0)) for s in range(N_CHIPS)]
    ins += [(h, (tm2, D_MODEL), lambda i, j: (i, 0)), (dh_out, (tm2, D_MODEL), lambda i, j: (i, 0)),
            (g, g.shape, lambda i, j: (0, 0))]
    dh, dg = _call(tag + "_dnorm", dnorm, (T // tm2, 1), ins,
                   [((T, D_MODEL), F32, (tm2, D_MODEL), lambda i, j: (i, 0)),
                    ((T // tm2 * 8, D_MODEL), F32, (8, D_MODEL), lambda i, j: (i, 0))],
                   pairs=[(s, N_CHIPS + s, "nt") for s in range(N_CHIPS)])
    return dh, dg, dw_in, dw_out, got


def _t5_onehot():
    qi = np.arange(ATT_BLOCK)[:, None] + ATT_BLOCK
    kj = np.arange(2 * ATT_BLOCK)[None, :]
    nn = np.maximum(qi - kj, 0)
    max_exact = N_BUCKETS // 2
    large = max_exact + (np.log(np.maximum(nn, 1) / max_exact) / np.log(MAX_DISTANCE / max_exact)
                         * (N_BUCKETS - max_exact)).astype(np.int32)
    large = np.minimum(large, N_BUCKETS - 1)
    bucket = np.where(nn < max_exact, nn, large).astype(np.int32).reshape(-1)
    return (bucket[None, :] == np.arange(N_BUCKETS)[:, None]).astype(np.float32)


def _small_mm(name, a, b, sel):
    def body(a_ref, b_ref, o_ref):
        if sel == "right":
            o_ref[...] = _sel_right(a_ref[...], b_ref[...])
        else:
            o_ref[...] = _sel_left(a_ref[...], b_ref[...])
    return pl.pallas_call(body, name=name, out_shape=jax.ShapeDtypeStruct((a.shape[0], b.shape[1]), F32),
                          compiler_params=pltpu.CompilerParams(vmem_limit_bytes=V7X_VMEM_LIMIT))(a, b)


def _swap_heads(t):
    return jnp.concatenate([t[:, HEAD_DIM:], t[:, :HEAD_DIM]], axis=1)


def _kv_layouts(proj):
    T = proj.shape[0]

    def fn(accs, tv, cv):
        return [tv[0], tv[1]]

    k, v = _tile_call("kv_cast", fn, T, 128, _pick(T, 1024), 128, tiles=[(proj, COL_AK), (proj, COL_AV)],
                      outs=[BF16, BF16])
    return jnp.concatenate([k, _swap_heads(k)], axis=1), jnp.concatenate([v, _swap_heads(v)], axis=1)


def _swa_masks():
    row = lax.broadcasted_iota(jnp.int32, (ATT_BLOCK, 2 * ATT_BLOCK), 0)
    col = lax.broadcasted_iota(jnp.int32, (ATT_BLOCK, 2 * ATT_BLOCK), 1)
    dist = ATT_BLOCK + row - col
    return (dist >= 0) & (dist < ATT_BLOCK), col


def _swa_heads():
    out = []
    for h in range(N_Q_HEADS):
        lo = h % 2 == 0
        hk = h // 4
        swapped = (hk == 1) if lo else (hk == 0)
        out.append((h // 2, lo, swapped))
    return out


def _swa_probs(qm, kk, bias_h, sink, valid):
    s = _dot_nt(qm, kk) * (HEAD_DIM ** -0.5) + bias_h
    s = jnp.where(valid, s, -jnp.inf)
    m = jnp.maximum(jnp.max(s, axis=-1, keepdims=True), sink)
    e = jnp.exp(s - m)
    es = jnp.exp(sink - m)
    den = jnp.sum(e, axis=-1, keepdims=True) + es
    return e / den, es / den


def _swa_fwd(proj, kk2, vv2, bias, sinks, B, S):
    T = B * S
    nb = S // ATT_BLOCK

    def body(q_ref, k_ref, v_ref, bias_ref, sink_ref, o_ref, kpad, vpad):
        zeros = jnp.zeros((ATT_BLOCK, 256), BF16)
        kpad[pl.ds(0, ATT_BLOCK), :] = zeros
        vpad[pl.ds(0, ATT_BLOCK), :] = zeros
        kpad[pl.ds(ATT_BLOCK, S), :] = k_ref[...]
        vpad[pl.ds(ATT_BLOCK, S), :] = v_ref[...]
        valid0, col = _swa_masks()
        lane = lax.broadcasted_iota(jnp.int32, (1, 128), 1)
        lo_q = lane < HEAD_DIM
        heads = _swa_heads()

        def blk(n, carry):
            r0 = pl.multiple_of(n * ATT_BLOCK, ATT_BLOCK)
            valid = valid0 & ((n > 0) | (col >= ATT_BLOCK))
            kb = kpad[pl.ds(r0, 2 * ATT_BLOCK), :]
            vb = vpad[pl.ds(r0, 2 * ATT_BLOCK), :]
            for j in range(N_Q_HEADS // 2):
                qblk = q_ref[pl.ds(r0, ATT_BLOCK), pl.ds(128 * j, 128)].astype(BF16)
                acc = jnp.zeros((ATT_BLOCK, 128), F32)
                for h in (2 * j, 2 * j + 1):
                    _, lo, swapped = heads[h]
                    keep = lo_q if lo else ~lo_q
                    qm = jnp.where(keep, qblk, jnp.zeros_like(qblk))
                    kk = kb[:, 128:] if swapped else kb[:, :128]
                    vv = vb[:, 128:] if swapped else vb[:, :128]
                    vm = jnp.where(keep, vv, jnp.zeros_like(vv))
                    p, _ = _swa_probs(qm, kk, bias_ref[h], sink_ref[h], valid)
                    acc = acc + _dot(p.astype(BF16), vm)
                o_ref[pl.ds(r0, ATT_BLOCK), pl.ds(128 * j, 128)] = acc.astype(o_ref.dtype)
            return carry

        lax.fori_loop(0, nb, blk, 0)

    return pl.pallas_call(
        body, name="swa_fwd", grid=(B,),
        in_specs=[pl.BlockSpec((S, 512), lambda b: (b, 0)),
                  pl.BlockSpec((S, 256), lambda b: (b, 0)),
                  pl.BlockSpec((S, 256), lambda b: (b, 0)),
                  pl.BlockSpec((N_Q_HEADS, ATT_BLOCK, 2 * ATT_BLOCK), lambda b: (0, 0, 0)),
                  pl.BlockSpec(memory_space=pltpu.SMEM)],
        out_specs=pl.BlockSpec((S, 512), lambda b: (b, 0)),
        out_shape=jax.ShapeDtypeStruct((T, 512), BF16),
        scratch_shapes=[pltpu.VMEM((S + ATT_BLOCK, 256), BF16), pltpu.VMEM((S + ATT_BLOCK, 256), BF16)],
        compiler_params=_params("arbitrary"))(proj, kk2, vv2, bias, sinks)


def _swa_bwd(proj, kk2, vv2, bias, sinks, datt, B, S):
    T = B * S
    nb = S // ATT_BLOCK

    def body(q_ref, k_ref, v_ref, bias_ref, sink_ref, do_ref, dq_ref, dk_ref, dv_ref, dbias_ref, dsink_ref,
             kpad, vpad, dkpad, dvpad):
        b = pl.program_id(0)

        @pl.when(b == 0)
        def _():
            dbias_ref[...] = jnp.zeros_like(dbias_ref)
            dsink_ref[...] = jnp.zeros_like(dsink_ref)

        zeros = jnp.zeros((ATT_BLOCK, 256), BF16)
        kpad[pl.ds(0, ATT_BLOCK), :] = zeros
        vpad[pl.ds(0, ATT_BLOCK), :] = zeros
        kpad[pl.ds(ATT_BLOCK, S), :] = k_ref[...]
        vpad[pl.ds(ATT_BLOCK, S), :] = v_ref[...]
        dkpad[...] = jnp.zeros_like(dkpad)
        dvpad[...] = jnp.zeros_like(dvpad)
        valid0, col = _swa_masks()
        lane = lax.broadcasted_iota(jnp.int32, (1, 128), 1)
        lo_q = lane < HEAD_DIM
        heads = _swa_heads()
        scale = HEAD_DIM ** -0.5

        def blk(n, carry):
            r0 = pl.multiple_of(n * ATT_BLOCK, ATT_BLOCK)
            valid = valid0 & ((n > 0) | (col >= ATT_BLOCK))
            kb = kpad[pl.ds(r0, 2 * ATT_BLOCK), :]
            vb = vpad[pl.ds(r0, 2 * ATT_BLOCK), :]
            dk_acc = [jnp.zeros((2 * ATT_BLOCK, 128), F32), jnp.zeros((2 * ATT_BLOCK, 128), F32)]
            dv_acc = [jnp.zeros((2 * ATT_BLOCK, 128), F32), jnp.zeros((2 * ATT_BLOCK, 128), F32)]
            for j in range(N_Q_HEADS // 2):
                qblk = q_ref[pl.ds(r0, ATT_BLOCK), pl.ds(128 * j, 128)].astype(BF16)
                doblk = do_ref[pl.ds(r0, ATT_BLOCK), pl.ds(128 * j, 128)]
                dq = jnp.zeros((ATT_BLOCK, 128), F32)
                for h in (2 * j, 2 * j + 1):
                    _, lo, swapped = heads[h]
                    keep = lo_q if lo else ~lo_q
                    qm = jnp.where(keep, qblk, jnp.zeros_like(qblk))
                    dom = jnp.where(keep, doblk, jnp.zeros_like(doblk))
                    kk = kb[:, 128:] if swapped else kb[:, :128]
                    vv = vb[:, 128:] if swapped else vb[:, :128]
                    km = jnp.where(keep, kk, jnp.zeros_like(kk))
                    p, ps = _swa_probs(qm, kk, bias_ref[h], sink_ref[h], valid)
                    dp = _dot_nt(dom, vv)
                    delta = jnp.sum(p * dp, axis=-1, keepdims=True)
                    ds = p * (dp - delta)
                    dbias_ref[h] += ds
                    dsink_ref[pl.ds(h, 1), :] += -jnp.sum(jnp.broadcast_to(ps * delta, (ATT_BLOCK, 128)),
                                                          axis=0, keepdims=True)
                    dsb = (ds * scale).astype(BF16)
                    dq = dq + _dot(dsb, km)
                    idx = 1 if swapped else 0
                    dk_acc[idx] = dk_acc[idx] + _dot_tn(dsb, qm)
                    dv_acc[idx] = dv_acc[idx] + _dot_tn(p.astype(BF16), dom)
                dq_ref[pl.ds(r0, ATT_BLOCK), pl.ds(128 * j, 128)] = dq.astype(dq_ref.dtype)
            dkpad[pl.ds(r0, 2 * ATT_BLOCK), :] += jnp.concatenate(dk_acc, axis=1)
            dvpad[pl.ds(r0, 2 * ATT_BLOCK), :] += jnp.concatenate(dv_acc, axis=1)
            return carry

        lax.fori_loop(0, nb, blk, 0)
        dk_ref[...] = dkpad[pl.ds(ATT_BLOCK, S), :]
        dv_ref[...] = dvpad[pl.ds(ATT_BLOCK, S), :]

    return pl.pallas_call(
        body, name="swa_bwd", grid=(B,),
        in_specs=[pl.BlockSpec((S, 512), lambda b: (b, 0)),
                  pl.BlockSpec((S, 256), lambda b: (b, 0)),
                  pl.BlockSpec((S, 256), lambda b: (b, 0)),
                  pl.BlockSpec((N_Q_HEADS, ATT_BLOCK, 2 * ATT_BLOCK), lambda b: (0, 0, 0)),
                  pl.BlockSpec(memory_space=pltpu.SMEM),
                  pl.BlockSpec((S, 512), lambda b: (b, 0))],
        out_specs=[pl.BlockSpec((S, 512), lambda b: (b, 0)),
                   pl.BlockSpec((S, 256), lambda b: (b, 0)),
                   pl.BlockSpec((S, 256), lambda b: (b, 0)),
                   pl.BlockSpec((N_Q_HEADS, ATT_BLOCK, 2 * ATT_BLOCK), lambda b: (0, 0, 0)),
                   pl.BlockSpec((N_Q_HEADS, 128), lambda b: (0, 0))],
        out_shape=[jax.ShapeDtypeStruct((T, 512), BF16),
                   jax.ShapeDtypeStruct((T, 256), F32),
                   jax.ShapeDtypeStruct((T, 256), F32),
                   jax.ShapeDtypeStruct((N_Q_HEADS, ATT_BLOCK, 2 * ATT_BLOCK), F32),
                   jax.ShapeDtypeStruct((N_Q_HEADS, 128), F32)],
        scratch_shapes=[pltpu.VMEM((S + ATT_BLOCK, 256), BF16), pltpu.VMEM((S + ATT_BLOCK, 256), BF16),
                        pltpu.VMEM((S + ATT_BLOCK, 256), F32), pltpu.VMEM((S + ATT_BLOCK, 256), F32)],
        compiler_params=_params("arbitrary"))(proj, kk2, vv2, bias, sinks, datt)


def _hgrn_gates(z, lb):
    sg = _sigmoid(z)
    f = lb + (1.0 - lb) * sg
    return sg, f, jnp.log(f), 1.0 - f


def _hgrn_consts():
    r = lax.broadcasted_iota(jnp.int32, (CHUNK, CHUNK), 0)
    c = lax.broadcasted_iota(jnp.int32, (CHUNK, CHUNK), 1)
    tril = (r >= c).astype(BF16)
    triu = (r <= c).astype(BF16)
    causal = r >= c
    below = (r // SUB) > (c // SUB)
    return tril, triu, causal, below, r, c


def _hgrn_offdiag(q, k, b_ref):
    zero = jnp.zeros((SUB, REC_DIM), F32)
    q_rows, k_cols, eqs, eks = [jnp.zeros((SUB, (N_SUB - 1) * REC_DIM), F32)], [], [], []
    for i in range(1, N_SUB):
        p = b_ref[pl.ds(SUB * i - 1, 1), :]
        eq = jnp.exp(b_ref[pl.ds(SUB * i, SUB), :] - p)
        qi = q[SUB * i:SUB * (i + 1), :] * eq
        q_rows.append(jnp.concatenate([zero] * (i - 1) + [qi] + [zero] * (N_SUB - 1 - i), axis=1))
        ek = jnp.exp(p - b_ref[pl.ds(0, SUB * i), :])
        ki = k[:SUB * i, :] * ek
        pad = jnp.zeros((CHUNK - SUB * i, REC_DIM), F32)
        k_cols.append(jnp.concatenate([ki, pad], axis=0))
        eqs.append(eq)
        eks.append(jnp.concatenate([ek, pad], axis=0))
    return jnp.concatenate(q_rows, axis=0), jnp.concatenate(k_cols, axis=1), eqs, eks


def _hgrn_diag(q, k_ref, b_ref):
    lane = lax.broadcasted_iota(jnp.int32, (SUB, CHUNK), 1)
    rowm = lax.broadcasted_iota(jnp.int32, (SUB, CHUNK), 0)
    blocks = []
    for i in range(N_SUB):
        qi = q[SUB * i:SUB * (i + 1), :]
        bi = b_ref[pl.ds(SUB * i, SUB), :]
        d = jnp.zeros((SUB, CHUNK), F32)
        for s in range(SUB):
            ks = k_ref[pl.ds(SUB * i + s, 1), :]
            bs = b_ref[pl.ds(SUB * i + s, 1), :]
            w = jnp.exp(jnp.minimum(bi - bs, 0.0))
            colv = jnp.sum(qi * ks * w, axis=-1, keepdims=True)
            d = jnp.where((lane == SUB * i + s) & (rowm >= s), colv, d)
        blocks.append(d)
    return jnp.concatenate(blocks, axis=0)


def _hgrn_fwd(proj, lb_param, B, S):
    T = B * S
    nc = S // CHUNK

    def body(q_ref, z_ref, v_ref, lb_ref, o_ref, st_ref, k_s, b_s):
        lb = _sigmoid(lb_ref[0:1, :] - lb_ref[1:2, :])
        tril, _, _, below, _, _ = _hgrn_consts()

        def chunk(ci, ht):
            r0 = pl.multiple_of(ci * CHUNK, CHUNK)
            q = q_ref[pl.ds(r0, CHUNK), :]
            v = v_ref[pl.ds(r0, CHUNK), :]
            _, _, g, k = _hgrn_gates(z_ref[pl.ds(r0, CHUNK), :], lb)
            bcum = _sel_left(tril, g)
            k_s[...] = k
            b_s[...] = bcum
            st_ref[ci] = ht
            qst, kst, _, _ = _hgrn_offdiag(q, k, b_s)
            a = jnp.where(below, _dot_nt(qst.astype(BF16), kst.astype(BF16)), 0.0) + _hgrn_diag(q, k_s, b_s)
            vb = v.astype(BF16)
            qb = (q * jnp.exp(bcum)).astype(BF16)
            o = _dot(a.astype(BF16), vb) + _dot_nt(qb, ht.astype(BF16))
            o_ref[pl.ds(r0, CHUNK), :] = o
            b_last = b_s[pl.ds(CHUNK - 1, 1), :]
            kb = (k * jnp.exp(b_last - bcum)).astype(BF16)
            return ht * jnp.exp(b_last) + _dot_tn(vb, kb)

        lax.fori_loop(0, nc, chunk, jnp.zeros((REC_DIM, REC_DIM), F32))

    H = REC_HEADS
    return pl.pallas_call(
        body, name="hgrn_fwd", grid=(B, H),
        in_specs=[pl.BlockSpec((S, 128), lambda b, h: (b, COL_RQ + h)),
                  pl.BlockSpec((S, 128), lambda b, h: (b, COL_RF + h)),
                  pl.BlockSpec((S, 128), lambda b, h: (b, COL_RI + h)),
                  pl.BlockSpec((2, 128), lambda b, h: (0, h))],
        out_specs=[pl.BlockSpec((S, 128), lambda b, h: (b, h)),
                   pl.BlockSpec((nc, REC_DIM, REC_DIM), lambda b, h: (b * H + h, 0, 0))],
        out_shape=[jax.ShapeDtypeStruct((T, 512), F32),
                   jax.ShapeDtypeStruct((B * H * nc, REC_DIM, REC_DIM), F32)],
        scratch_shapes=[pltpu.VMEM((CHUNK, REC_DIM), F32), pltpu.VMEM((CHUNK, REC_DIM), F32)],
        compiler_params=_params("arbitrary", "arbitrary"))(proj, proj, proj, lb_param)


def _hgrn_bwd(proj, lb_param, states, do, B, S, comm=None):
    T = B * S
    nc = S // CHUNK

    c_arrays, c_in_specs, c_out_shapes, c_sems = _comm_parts(comm)
    nci, nco = len(c_arrays), len(c_out_shapes)

    def body(*refs):
        q_ref, z_ref, v_ref, lb_ref, st_ref, do_ref = refs[:6]
        dq_ref, dz_ref, dv_ref, dlb_ref = refs[6 + nci:10 + nci]
        k_s, b_s, dkd_s = refs[10 + nci + nco:13 + nci + nco]
        comm_first, comm_last = _comm_run(comm, (B, REC_HEADS), refs, 6, 4)
        comm_first()
        lb = _sigmoid(lb_ref[0:1, :] - lb_ref[1:2, :])
        tril, triu, causal, below, r, _ = _hgrn_consts()
        lane = lax.broadcasted_iota(jnp.int32, (SUB, CHUNK), 1)
        rowm = lax.broadcasted_iota(jnp.int32, (SUB, CHUNK), 0)
        last_row = lax.broadcasted_iota(jnp.int32, (CHUNK, 1), 0) == CHUNK - 1

        def chunk(it, carry):
            dht, dlb = carry
            ci = nc - 1 - it
            r0 = pl.multiple_of(ci * CHUNK, CHUNK)
            q = q_ref[pl.ds(r0, CHUNK), :]
            v = v_ref[pl.ds(r0, CHUNK), :]
            dout = do_ref[pl.ds(r0, CHUNK), :]
            sg, f, g, k = _hgrn_gates(z_ref[pl.ds(r0, CHUNK), :], lb)
            bcum = _sel_left(tril, g)
            k_s[...] = k
            b_s[...] = bcum
            ht = st_ref[ci]
            qst, kst, eqs, eks = _hgrn_offdiag(q, k, b_s)
            qst_b, kst_b = qst.astype(BF16), kst.astype(BF16)
            a = jnp.where(below, _dot_nt(qst_b, kst_b), 0.0) + _hgrn_diag(q, k_s, b_s)
            vb, dob = v.astype(BF16), dout.astype(BF16)
            eb = jnp.exp(bcum)
            b_last = b_s[pl.ds(CHUNK - 1, 1), :]
            el = jnp.exp(b_last)
            ekb = jnp.exp(b_last - bcum)
            qb = (q * eb).astype(BF16)
            kb = k * ekb
            dhb = dht.astype(BF16)
            dv = _dot_tn(a.astype(BF16), dob) + _dot_nt(kb.astype(BF16), dhb)
            da = jnp.where(causal, _dot_nt(dob, vb), 0.0)
            dqb = _dot(dob, ht.astype(BF16))
            dkb = _dot(vb, dhb)
            dht_new = dht * el + _dot_tn(dob, qb)
            da_off = jnp.where(below, da, 0.0).astype(BF16)
            dqst = _dot(da_off, kst_b)
            dkst = _dot_tn(da_off, qst_b)
            dq_rows = [jnp.zeros((SUB, REC_DIM), F32)]
            dk = jnp.zeros((CHUNK, REC_DIM), F32)
            for i in range(1, N_SUB):
                dq_rows.append(dqst[SUB * i:SUB * (i + 1), REC_DIM * (i - 1):REC_DIM * i] * eqs[i - 1])
                dk = dk + dkst[:, REC_DIM * (i - 1):REC_DIM * i] * eks[i - 1]
            dq = jnp.concatenate(dq_rows, axis=0)
            dkd_s[...] = jnp.zeros_like(dkd_s)
            dq_diag = []
            for i in range(N_SUB):
                qi = q[SUB * i:SUB * (i + 1), :]
                bi = b_s[pl.ds(SUB * i, SUB), :]
                dai = da[SUB * i:SUB * (i + 1), :]
                dqi = jnp.zeros((SUB, REC_DIM), F32)
                for s in range(SUB):
                    ks = k_s[pl.ds(SUB * i + s, 1), :]
                    bs = b_s[pl.ds(SUB * i + s, 1), :]
                    w = jnp.exp(jnp.minimum(bi - bs, 0.0))
                    dacol = jnp.sum(jnp.where((lane == SUB * i + s) & (rowm >= s), dai, 0.0), axis=-1,
                                    keepdims=True)
                    dqi = dqi + dacol * ks * w
                    dkd_s[pl.ds(SUB * i + s, 1), :] += jnp.sum(dacol * qi * w, axis=0, keepdims=True)
                dq_diag.append(dqi)
            dq = dq + jnp.concatenate(dq_diag, axis=0) + eb * dqb
            dk = dk + dkd_s[...] + ekb * dkb
            edge = jnp.sum(kb * dkb, axis=0, keepdims=True) + el * jnp.sum(ht * dht, axis=0, keepdims=True)
            db = q * dq - k * dk + jnp.where(last_row, edge, 0.0)
            dg = _sel_left(triu, db)
            df = dg / f - dk
            dz = df * (1.0 - lb) * sg * (1.0 - sg)
            dlb = dlb + jnp.sum(df * (1.0 - sg), axis=0, keepdims=True)
            dq_ref[pl.ds(r0, CHUNK), :] = dq.astype(dq_ref.dtype)
            dz_ref[pl.ds(r0, CHUNK), :] = dz.astype(dz_ref.dtype)
            dv_ref[pl.ds(r0, CHUNK), :] = dv.astype(dv_ref.dtype)
            return dht_new, dlb

        _, dlb = lax.fori_loop(0, nc, chunk, (jnp.zeros((REC_DIM, REC_DIM), F32), jnp.zeros((1, REC_DIM), F32)))
        dlb_ref[...] = jnp.broadcast_to(dlb * lb * (1.0 - lb), (8, REC_DIM))
        comm_last()

    H = REC_HEADS
    return pl.pallas_call(
        body, name="hgrn_bwd", grid=(B, H),
        in_specs=[pl.BlockSpec((S, 128), lambda b, h: (b, COL_RQ + h)),
                  pl.BlockSpec((S, 128), lambda b, h: (b, COL_RF + h)),
                  pl.BlockSpec((S, 128), lambda b, h: (b, COL_RI + h)),
                  pl.BlockSpec((2, 128), lambda b, h: (0, h)),
                  pl.BlockSpec((nc, REC_DIM, REC_DIM), lambda b, h: (b * H + h, 0, 0)),
                  pl.BlockSpec((S, 128), lambda b, h: (b, h))] + c_in_specs,
        out_specs=[pl.BlockSpec((S, 128), lambda b, h: (b, h))] * 3
        + [pl.BlockSpec((8, 128), lambda b, h: (b, h))] + [ANY] * nco,
        out_shape=[jax.ShapeDtypeStruct((T, 512), BF16)] * 3 + [jax.ShapeDtypeStruct((B * 8, 512), F32)]
        + c_out_shapes,
        scratch_shapes=[pltpu.VMEM((CHUNK, REC_DIM), F32)] * 3 + c_sems,
        compiler_params=_params("arbitrary", "arbitrary"))(proj, proj, proj, lb_param, states, do, *c_arrays)


def _rec_gate_fwd(rec, proj, rec_norm):
    T = rec.shape[0]

    def fn(accs, tv, cv):
        return [_rms_hat(tv[0]) * cv[0] * _sigmoid(tv[1])]

    return _tile_call("rec_gate", fn, T, 512, _pick(T, 1024), REC_DIM, tiles=[(rec, 0), (proj, COL_RG)],
                      consts=[rec_norm], outs=[BF16])[0]


def _rec_gate_bwd(dyb, w_rec_proj, rec, proj, rec_norm):
    T = rec.shape[0]

    def fn(accs, tv, cv):
        d, r, rg = accs[0], tv[0], tv[1]
        sg = _sigmoid(rg)
        rn = _rms_hat(r) * cv[0]
        dh, dg = _rms_bwd_vals(d * sg, r, cv[0])
        return [dh, d * rn * sg * (1.0 - sg), dg]

    return _tile_call("rec_gate_bwd", fn, T, 512, _pick(T, 1024), REC_DIM, pairs=[(dyb, 0, w_rec_proj, "nt")],
                      tiles=[(rec, 0), (proj, COL_RG)], consts=[rec_norm], outs=[F32, BF16], parts=1)


def _mix_out_fwd(att, recn, proj, w_att_proj, w_rec_proj, w_out, h1):
    T = att.shape[0]
    tn = 256

    def merge(accs, tv, cv):
        ya, yb = accs
        return [ya, yb, _sigmoid(tv[0]) * ya + _sigmoid(tv[1]) * yb]

    ya, yb, merged = _tile_call(
        "merge", merge, T, D_MODEL, _pick(T, 1024), tn,
        pairs=[(att, 0, w_att_proj, "nn"), (recn, 0, w_rec_proj, "nn")],
        tiles=[(proj, COL_GA * 128 // tn), (proj, COL_GB * 128 // tn)], outs=[BF16] * 3)

    def res(accs, tv, cv):
        return [tv[0] + accs[0]]

    h2 = _tile_call("mix_out", res, T, D_MODEL, _pick(T, 512), 512, pairs=[(merged, 0, w_out, "nn")],
                    tiles=[(h1, 0)], outs=[F32])[0]
    return h2, (ya, yb, merged)


GATHER_FIRST = ("w_ffn1_in", "w_ffn1_out")
GATHER_MIX = ("w_in", "w_att_proj", "w_rec_proj", "w_out")
GATHER_LAST = ("w_ffn2_in", "w_ffn2_out", "w_ple_gate", "w_ple_proj")
SCATTER_LATE = ("w_ple_gate", "w_ple_proj", "w_ffn2_in", "w_ffn2_out")
SCATTER_MIX = ("w_out", "w_att_proj", "w_rec_proj", "w_in")
SCATTER_LAST = ("w_ffn1_in", "w_ffn1_out")


def _local_step(x, p, tgt, w, mine16, cc, me_chip, B, S):
    T = B * S
    w = dict(w)
    g_ffn1, g_mix, g_ffn2, g_ple = w["norm_ffn1"], w["norm_mix"], w["norm_ffn2"], w["norm_ple"]
    g_fin = w["norm_final"].reshape(1, D_MODEL)
    grads, part, from_chips = {}, {}, {}

    def gather(names):
        return _gather_comm([mine16[n] for n in names])

    def place(names, got):
        for n, g in zip(names, got):
            full = lax.dynamic_update_index_in_dim(g, mine16[n], me_chip, 0)
            w[n] = full if n in ("w_ffn1_in", "w_ffn2_in") else _natural(n, full)

    def scatter(tag, names):
        from_sib = _swap_halves("rs_sibling_" + tag, [grads[n][1] for n in names])
        for n, fs in zip(names, from_sib):
            part[n] = _add_sibling("rs_add_sib_" + n, grads[n][0], fs, cc)
        return _scatter_comm([part[n][1] for n in names])

    def scattered(names, got):
        for n, g in zip(names, got):
            from_chips[n] = g

    place(GATHER_FIRST, _run_comm("gather_first", gather(GATHER_FIRST)))
    h1, sv1, got_mix, got_last = _ffn_fwd("ffn1", x, g_ffn1, w["w_ffn1_in"], w["w_ffn1_out"],
                                          comm_in=gather(GATHER_MIX), comm_out=gather(GATHER_LAST))
    place(GATHER_MIX, got_mix)
    place(GATHER_LAST, got_last)
    u = _rms_fwd("mix_norm", h1, g_mix)

    def ident(accs, tv, cv):
        return [accs[0]]

    proj = _tile_call("in_proj", ident, T, IN_W, _pick(T, 1024), 256, pairs=[(u, 0, w["w_in"], "nn")],
                      outs=[F32])[0]
    onehot = jnp.asarray(_t5_onehot())
    bias = _small_mm("t5_bias", w["rel_bias"].T, onehot.astype(BF16), "right")
    bias = bias.reshape(N_Q_HEADS, ATT_BLOCK, 2 * ATT_BLOCK)
    sinks = w["attn_sinks"].reshape(N_Q_HEADS)
    kk2, vv2 = _kv_layouts(proj)
    att = _swa_fwd(proj, kk2, vv2, bias, sinks, B, S)
    rec, states = _hgrn_fwd(proj, w["lb_param"], B, S)
    recn = _rec_gate_fwd(rec, proj, w["rec_norm"])
    h2, (ya, yb, merged) = _mix_out_fwd(att, recn, proj, w["w_att_proj"], w["w_rec_proj"], w["w_out"], h1)
    h3, sv2, _, _ = _ffn_fwd("ffn2", h2, g_ffn2, w["w_ffn2_in"], w["w_ffn2_out"])
    n3 = _rms_fwd("ple_norm", h3, g_ple)

    def ple(accs, tv, cv):
        gate = _sigmoid(accs[0])
        return [gate, accs[1], tv[0] + gate * accs[1]]

    gate_p, pp, h4 = _tile_call(
        "ple", ple, T, D_MODEL, _pick(T, 512), 512,
        pairs=[(n3, 0, w["w_ple_gate"], "nn"), (p, 0, w["w_ple_proj"], "nn")], tiles=[(h3, 0)],
        outs=[BF16, BF16, F32])

    def head(accs, tv, cv):
        h, t = tv
        err = _rms_hat(h) * cv[0] - t
        dh, dg = _rms_bwd_vals(err * (1.0 / D_MODEL), h, cv[0])
        return [dh, _group8(err * err), dg]

    dh4, loss_p, dg_fin = _tile_call("loss_head", head, T, D_MODEL, _pick(T, 256), D_MODEL,
                                     tiles=[(h4, 0), (tgt, 0)], consts=[g_fin], outs=[F32], parts=2)
    grads["norm_final"] = dg_fin

    def dple(accs, tv, cv):
        d, gt, ppv = tv[0], tv[1].astype(F32), tv[2].astype(F32)
        return [d * ppv * gt * (1.0 - gt), d * gt]

    dzg, dpp = _tile_call("ple_dact", dple, T, D_MODEL, _pick(T, 512), D_MODEL,
                          tiles=[(dh4, 0), (gate_p, 0), (pp, 0)], outs=[BF16] * 2)
    grads["w_ple_gate"] = _mm_tn_rows("ple_dwg", n3, dzg)
    grads["w_ple_proj"] = _mm_tn_cols("ple_dwp", p, dpp)

    def dnorm(accs, tv, cv):
        dh, dg = _rms_bwd_vals(accs[0], tv[0], cv[0])
        return [tv[1] + dh, dg]

    dh3, grads["norm_ple"] = _tile_call(
        "ple_dnorm", dnorm, T, D_MODEL, _pick(T, 256), D_MODEL, pairs=[(dzg, 0, w["w_ple_gate"], "nt")],
        tiles=[(h3, 0), (dh4, 0)], consts=[g_ple], outs=[F32], parts=1)

    dh2, grads["norm_ffn2"], grads["w_ffn2_in"], grads["w_ffn2_out"], _ = _ffn_bwd(
        "ffn2b", dh3, h2, g_ffn2, w["w_ffn2_in"], w["w_ffn2_out"], sv2)
    scatter_late = scatter("late", SCATTER_LATE)

    def to_bf(accs, tv, cv):
        return [tv[0]]

    dh2b = _tile_call("mix_dcast", to_bf, T, D_MODEL, _pick(T, 512), D_MODEL, tiles=[(dh2, 0)], outs=[BF16])[0]
    grads["w_out"] = _mm_tn_rows("mix_dwout", merged, dh2b)
    tn = 256

    def dmerge(accs, tv, cv):
        dm = accs[0]
        sa, sb = _sigmoid(tv[0]), _sigmoid(tv[1])
        yav, ybv = tv[2].astype(F32), tv[3].astype(F32)
        return [dm * sa, dm * sb, dm * yav * sa * (1.0 - sa), dm * ybv * sb * (1.0 - sb)]

    dya, dyb, dga, dgb = _tile_call(
        "mix_dmerge", dmerge, T, D_MODEL, _pick(T, 1024), tn, pairs=[(dh2b, 0, w["w_out"], "nt")],
        tiles=[(proj, COL_GA * 128 // tn), (proj, COL_GB * 128 // tn), (ya, 0), (yb, 0)], outs=[BF16] * 4)
    grads["w_att_proj"] = _mm_tn_cols("mix_dwatt", att, dya)
    grads["w_rec_proj"] = _mm_tn_cols("mix_dwrec", recn, dyb)

    datt = _tile_call("mix_datt", ident, T, 512, _pick(T, 1024), 512, pairs=[(dya, 0, w["w_att_proj"], "nt")],
                      outs=[BF16])[0]
    drec, drg, grads["rec_norm"] = _rec_gate_bwd(dyb, w["w_rec_proj"], rec, proj, w["rec_norm"])

    drq, drf, dri, dlb, *got = _hgrn_bwd(proj, w["lb_param"], states, drec, B, S, comm=scatter_late)
    scattered(SCATTER_LATE, got)
    grads["lb_param"] = dlb
    daq, dk2, dv2, dbias, dsink = _swa_bwd(proj, kk2, vv2, bias, sinks, datt, B, S)
    grads["attn_sinks"] = dsink
    grads["rel_bias"] = _small_mm("t5_dbias", dbias.reshape(N_Q_HEADS, -1), onehot.T.astype(BF16), "right")
    dak = (dk2[:, :128] + _swap_heads(dk2[:, 128:])).astype(BF16)
    dav = (dv2[:, :128] + _swap_heads(dv2[:, 128:])).astype(BF16)
    dproj = jnp.concatenate([daq, dak, dav, drq, drf, dri, drg, dga, dgb], axis=1)
    tk = _pick(T, 512, 128)
    w_in_shard = IN_W // N_CHIPS
    gw32, gw16 = _mm_tn("mix_dwin", (1, 2, T // tk),
                        (u, (tk, D_MODEL), lambda i, j, k: (k, 0)), (dproj, (tk, IN_W // 2), lambda i, j, k: (k, j)),
                        _grad_pair((D_MODEL, IN_W), (D_MODEL, IN_W // 2), lambda i, j, k: (0, j)))
    to_sh = lambda t: t.reshape(D_MODEL, N_CHIPS, w_in_shard).transpose(1, 0, 2)
    grads["w_in"] = (to_sh(gw32), to_sh(gw16))
    scatter_mix = scatter("mix", SCATTER_MIX)

    def dnorm_mix(accs, tv, cv):
        dh, dg = _rms_bwd_vals(accs[0], tv[0], cv[0])
        return [tv[1] + dh, dg]

    dh1, grads["norm_mix"] = _tile_call(
        "mix_dnorm", dnorm_mix, T, D_MODEL, _pick(T, 256), D_MODEL, pairs=[(dproj, 0, w["w_in"], "nt")],
        tiles=[(h1, 0), (dh2, 0)], consts=[g_mix], outs=[F32], parts=1)

    dx, grads["norm_ffn1"], grads["w_ffn1_in"], grads["w_ffn1_out"], got = _ffn_bwd(
        "ffn1b", dh1, x, g_ffn1, w["w_ffn1_in"], w["w_ffn1_out"], sv1, comm=scatter_mix)
    scattered(SCATTER_MIX, got)
    scattered(SCATTER_LAST, _run_comm("rs_chips_last", scatter("last", SCATTER_LAST)))
    return loss_p, dx, grads, part, from_chips


def _place():
    x, y, c = lax.axis_index("x"), lax.axis_index("y"), lax.axis_index("c")
    return x, y, c


def _other_chips(x, y):
    return [(1 - x, y, 2 * (1 - x) + y), (x, 1 - y, 2 * x + 1 - y), (1 - x, 1 - y, 2 * (1 - x) + 1 - y)]


def _half_rows(ref_3d, chip, h, rows):
    return ref_3d.at[chip, pl.ds(h * rows, rows), :]


def _run_comm(name, comm):
    nci, nco = len(comm.ins), len(comm.out_shapes)

    def body(*refs):
        cin, cout, send_sems, recv_sems = refs[:nci], refs[nci:nci + nco], refs[-2], refs[-1]
        comm.start(cin, cout, send_sems, recv_sems)
        comm.finish(cin, cout, send_sems, recv_sems)

    return pl.pallas_call(
        body, name=name, in_specs=[ANY] * nci, out_specs=[ANY] * nco, out_shape=list(comm.out_shapes),
        scratch_shapes=[pltpu.SemaphoreType.DMA((comm.n_sems,)), pltpu.SemaphoreType.DMA((comm.n_sems,))],
    )(*comm.ins)


def _gather_comm(ws):
    nw = len(ws)

    def parts(w_refs, out_refs, send_sems, recv_sems):
        x, y, c = _place()
        me = 2 * x + y
        chips = _other_chips(x, y)

        def copy(i, k, chip, h, to, src=None):
            half = ws[i].shape[0] // 2
            dst = _half_rows(out_refs[i], chip, h, half)
            return pltpu.make_async_remote_copy(
                src_ref=dst if src is None else src, dst_ref=dst,
                send_sem=send_sems.at[6 * i + k], recv_sem=recv_sems.at[6 * i + k], device_id=to, device_id_type=MESH)

        def first():
            out = []
            for i in range(nw):
                half = ws[i].shape[0] // 2
                out += [copy(i, j, me, c, (cx, cy, c), src=w_refs[i].at[pl.ds(c * half, half), :])
                        for j, (cx, cy, _) in enumerate(chips)]
            return out

        return copy, first, chips, c, (x, y, 1 - c)

    def start(*refs):
        _, first, _, _, _ = parts(*refs)
        for cp in first():
            cp.start()

    def finish(*refs):
        copy, first, chips, c, sibling = parts(*refs)
        passed = []
        for i in range(nw):
            for j, (cx, cy, ci) in enumerate(chips):
                copy(i, j, ci, c, (cx, cy, c)).wait_recv()
                fw = copy(i, 3 + j, ci, c, sibling)
                fw.start()
                passed.append(fw)
        for i in range(nw):
            for j, (_, _, ci) in enumerate(chips):
                copy(i, 3 + j, ci, 1 - c, sibling).wait_recv()
        for cp in first() + passed:
            cp.wait_send()

    return _Comm(list(ws), [jax.ShapeDtypeStruct((N_CHIPS,) + w.shape, w.dtype) for w in ws], 6 * nw, start, finish)


def _scatter_comm(ps):
    nw = len(ps)

    def copies(p_refs, out_refs, send_sems, recv_sems):
        x, y, c = _place()
        cps = []
        for i in range(nw):
            for j, (cx, cy, ci) in enumerate(_other_chips(x, y)):
                cps.append(pltpu.make_async_remote_copy(
                    src_ref=p_refs[i].at[ci], dst_ref=out_refs[i].at[j], send_sem=send_sems.at[3 * i + j],
                    recv_sem=recv_sems.at[3 * i + j], device_id=(cx, cy, c), device_id_type=MESH))
        return cps

    def start(*refs):
        for cp in copies(*refs):
            cp.start()

    def finish(*refs):
        for cp in copies(*refs):
            cp.wait()

    return _Comm(list(ps), [jax.ShapeDtypeStruct((3,) + p.shape[1:], p.dtype) for p in ps], 3 * nw, start, finish)


def _swap_halves(name, gs):
    nw = len(gs)

    def body(*refs):
        g_refs, out_refs, send_sems, recv_sems = refs[:nw], refs[nw:2 * nw], refs[2 * nw], refs[2 * nw + 1]
        x, y, c = _place()
        cps = []
        for i in range(nw):
            half = gs[i].shape[1] // 2
            cps.append(pltpu.make_async_remote_copy(
                src_ref=g_refs[i].at[:, pl.ds((1 - c) * half, half), :], dst_ref=out_refs[i],
                send_sem=send_sems.at[i], recv_sem=recv_sems.at[i], device_id=(x, y, 1 - c), device_id_type=MESH))
        for cp in cps:
            cp.start()
        for cp in cps:
            cp.wait()

    return pl.pallas_call(
        body, name=name, in_specs=[ANY] * nw, out_specs=[ANY] * nw,
        out_shape=[jax.ShapeDtypeStruct((N_CHIPS, g.shape[1] // 2, g.shape[2]), g.dtype) for g in gs],
        scratch_shapes=[pltpu.SemaphoreType.DMA((nw,)), pltpu.SemaphoreType.DMA((nw,))],
    )(*gs)


def _join_halves(name, ss):
    nw = len(ss)

    def body(*refs):
        s_refs, out_refs, send_sems, recv_sems = refs[:nw], refs[nw:2 * nw], refs[2 * nw], refs[2 * nw + 1]
        x, y, c = _place()
        cps = [pltpu.make_async_remote_copy(
            src_ref=s_refs[i], dst_ref=out_refs[i], send_sem=send_sems.at[i], recv_sem=recv_sems.at[i],
            device_id=(x, y, 1 - c), device_id_type=MESH) for i in range(nw)]
        for cp in cps:
            cp.start()
        for cp in cps:
            cp.wait()

    return pl.pallas_call(
        body, name=name, in_specs=[ANY] * nw, out_specs=[ANY] * nw,
        out_shape=[jax.ShapeDtypeStruct(s.shape, s.dtype) for s in ss],
        scratch_shapes=[pltpu.SemaphoreType.DMA((nw,)), pltpu.SemaphoreType.DMA((nw,))],
    )(*ss)


def _allreduce_small(sp):
    def body(s_ref, out_ref, slots, send_sems, recv_sems):
        x, y, c = _place()
        me = 4 * x + 2 * y + c
        slots[me] = s_ref[...]
        cps = []
        for r in range(1, N_DEV):
            px, py, pc = x ^ (r >> 2), y ^ ((r >> 1) & 1), c ^ (r & 1)
            cps.append(pltpu.make_async_remote_copy(
                src_ref=s_ref, dst_ref=slots.at[me], send_sem=send_sems.at[r - 1], recv_sem=recv_sems.at[r - 1],
                device_id=(px, py, pc), device_id_type=MESH))
        for cp in cps:
            cp.start()
        for r in range(1, N_DEV):
            px, py, pc = x ^ (r >> 2), y ^ ((r >> 1) & 1), c ^ (r & 1)
            pltpu.make_async_remote_copy(
                src_ref=s_ref, dst_ref=slots.at[4 * px + 2 * py + pc], send_sem=send_sems.at[r - 1],
                recv_sem=recv_sems.at[r - 1], device_id=(px, py, pc), device_id_type=MESH).wait_recv()
        for cp in cps:
            cp.wait_send()
        acc = slots[0]
        for d in range(1, N_DEV):
            acc = acc + slots[d]
        out_ref[...] = acc

    return pl.pallas_call(
        body, name="allreduce_small",
        in_specs=[pl.BlockSpec(memory_space=pltpu.VMEM)], out_specs=pl.BlockSpec(memory_space=pltpu.VMEM),
        out_shape=jax.ShapeDtypeStruct(sp.shape, F32),
        scratch_shapes=[pltpu.VMEM((N_DEV,) + sp.shape, F32), pltpu.SemaphoreType.DMA((N_DEV - 1,)),
                        pltpu.SemaphoreType.DMA((N_DEV - 1,))],
    )(sp)


def _scalar(v):
    return jnp.reshape(v, (1,)).astype(jnp.int32)


def _row_tile(h, dtype_mult=16):
    return _pick(h, 256, dtype_mult)


def _add_sibling(name, g32, from_sib, c):
    _, r, n = g32.shape
    h = r // 2
    th = _row_tile(h)
    nt = h // th

    def body(c_ref, g_ref, s_ref, o32_ref, o16_ref):
        s = g_ref[...] + s_ref[...].astype(F32)
        o32_ref[...] = s
        o16_ref[...] = s.astype(BF16)

    blk = (None, th, n)
    return pl.pallas_call(
        body, name=name,
        grid_spec=pltpu.PrefetchScalarGridSpec(
            num_scalar_prefetch=1, grid=(N_CHIPS, nt),
            in_specs=[pl.BlockSpec(blk, lambda k, t, c_ref: (k, c_ref[0] * nt + t, 0)),
                      pl.BlockSpec(blk, lambda k, t, c_ref: (k, t, 0))],
            out_specs=[pl.BlockSpec(blk, lambda k, t, c_ref: (k, t, 0))] * 2),
        out_shape=[jax.ShapeDtypeStruct((N_CHIPS, h, n), F32), jax.ShapeDtypeStruct((N_CHIPS, h, n), BF16)],
        compiler_params=_params("arbitrary", "arbitrary"))(_scalar(c), g32, from_sib)


def _add_chips(name, p32, from_chips, me_chip):
    _, h, n = p32.shape
    th = _row_tile(h)

    def body(m_ref, p_ref, a_ref, b_ref, c_ref, o_ref):
        o_ref[...] = p_ref[...] + a_ref[...].astype(F32) + b_ref[...].astype(F32) + c_ref[...].astype(F32)

    blk = (None, th, n)
    return pl.pallas_call(
        body, name=name,
        grid_spec=pltpu.PrefetchScalarGridSpec(
            num_scalar_prefetch=1, grid=(h // th,),
            in_specs=[pl.BlockSpec(blk, lambda t, m_ref: (m_ref[0], t, 0))]
            + [pl.BlockSpec(blk, lambda t, m_ref, j=j: (j, t, 0)) for j in range(3)],
            out_specs=pl.BlockSpec((th, n), lambda t, m_ref: (t, 0))),
        out_shape=jax.ShapeDtypeStruct((h, n), F32),
        compiler_params=_params("arbitrary"))(_scalar(me_chip), p32, from_chips, from_chips, from_chips)


def _adamw_vals(w, g, m, v):
    m = ADAM_B1 * m + (1.0 - ADAM_B1) * g
    v = ADAM_B2 * v + (1.0 - ADAM_B2) * (g * g)
    m_hat = m / (1.0 - ADAM_B1 ** ADAM_STEP)
    v_hat = v / (1.0 - ADAM_B2 ** ADAM_STEP)
    delta = -ADAM_LR * (m_hat / (jnp.sqrt(v_hat) + ADAM_EPS) + ADAM_WD * w)
    return delta, m, v


def _adamw_halves(name, w, m, v, g_mine, g_sib, c):
    r, n = w.shape
    h = r // 2
    th = _row_tile(h, 8)
    nt = h // th

    def body(c_ref, w_ref, m_ref, v_ref, a_ref, b_ref, g_ref, d_ref, nm_ref, nv_ref):
        mine = (pl.program_id(0) // nt) == c_ref[0]
        g = jnp.where(mine, a_ref[...], b_ref[...])
        d, nm, nv = _adamw_vals(w_ref[...], g, m_ref[...], v_ref[...])
        g_ref[...] = g
        d_ref[...] = d
        nm_ref[...] = nm
        nv_ref[...] = nv

    full = pl.BlockSpec((th, n), lambda t, c_ref: (t, 0))
    part = pl.BlockSpec((th, n), lambda t, c_ref: (t % nt, 0))
    return pl.pallas_call(
        body, name=name,
        grid_spec=pltpu.PrefetchScalarGridSpec(
            num_scalar_prefetch=1, grid=(2 * nt,), in_specs=[full, full, full, part, part], out_specs=[full] * 4),
        out_shape=[jax.ShapeDtypeStruct((r, n), F32)] * 4,
        compiler_params=_params("arbitrary"))(_scalar(c), w, m, v, g_mine, g_sib)


def _adamw(name, w, g, m, v):
    R, W = w.shape

    def fn(accs, tv, cv):
        return list(_adamw_vals(*tv))

    return _tile_call(name, fn, R, W, _pick(R, 256), W, tiles=[(w, 0), (g, 0), (m, 0), (v, 0)], outs=[F32] * 3)


SMALL_LAYOUT = (("rel_bias", 2, 256), ("lb_param", 8, 1024), ("norm_ffn1", 8, 1024), ("norm_mix", 8, 1024),
                ("attn_sinks", 1, 8), ("rec_norm", 1, 128), ("norm_ffn2", 8, 1024), ("norm_ple", 8, 1024),
                ("norm_final", 8, 1024), ("loss", 8, 1024))


def _pack_small(vals):
    rows = []
    for name, nrows, n in SMALL_LAYOUT:
        flat = vals[name].reshape(-1)
        flat = jnp.pad(flat, (0, nrows * 128 - n))
        rows.append(flat.reshape(nrows, 128))
    packed = jnp.concatenate(rows, axis=0)
    return jnp.pad(packed, ((0, SMALL_ROWS - packed.shape[0]), (0, 0)))


def _unpack_small(packed, shapes):
    out, r = {}, 0
    for name, nrows, n in SMALL_LAYOUT:
        out[name] = packed[r:r + nrows].reshape(-1)[:n].reshape(shapes[name])
        r += nrows
    return out


def _natural(name, s):
    if name in COL_SHARDED:
        return s.transpose(1, 0, 2).reshape(s.shape[1], -1)
    return s.reshape(-1, s.shape[2])


def kernel(x, p, rel_bias, lb_param, norm_ffn1, w_ffn1_in, w_ffn1_out, norm_mix, w_in, attn_sinks, rec_norm, w_att_proj, w_rec_proj, w_out, norm_ffn2, w_ffn2_in, w_ffn2_out, norm_ple, w_ple_gate, w_ple_proj, norm_final, loss_target, m_rel_bias, m_lb_param, m_norm_ffn1, m_w_ffn1_in, m_w_ffn1_out, m_norm_mix, m_w_in, m_attn_sinks, m_rec_norm, m_w_att_proj, m_w_rec_proj, m_w_out, m_norm_ffn2, m_w_ffn2_in, m_w_ffn2_out, m_norm_ple, m_w_ple_gate, m_w_ple_proj, m_norm_final, v_rel_bias, v_lb_param, v_norm_ffn1, v_w_ffn1_in, v_w_ffn1_out, v_norm_mix, v_w_in, v_attn_sinks, v_rec_norm, v_w_att_proj, v_w_rec_proj, v_w_out, v_norm_ffn2, v_w_ffn2_in, v_w_ffn2_out, v_norm_ple, v_w_ple_gate, v_w_ple_proj, v_norm_final):
    args = dict(locals())
    wsh = {n: args[n] for n in WEIGHTS}
    B, S = x.shape[0], x.shape[1]
    T = B * S
    cx, cy, cc = _place()
    me_chip = 2 * cx + cy

    mine16 = {n: wsh[n][0].astype(BF16) for n in BIG}
    loss_p, dx, grads, part, from_chips = _local_step(
        x.reshape(T, D_MODEL), p.reshape(T, PLE_DIM), loss_target.reshape(T, D_MODEL),
        {n: wsh[n] for n in SMALL}, mine16, cc, me_chip, B, S)

    s_mine = [_add_chips("rs_add_chips_" + n, part[n][0], from_chips[n], me_chip) for n in BIG]
    s_sib = _join_halves("rs_join", s_mine)

    small_vals = {
        "rel_bias": grads["rel_bias"].T,
        "lb_param": jnp.concatenate([_colsum("dlb_sum", grads["lb_param"]),
                                     -_colsum("dlb_sum2", grads["lb_param"])], axis=0) / 8.0,
        "attn_sinks": grads["attn_sinks"][:, 0],
        "rec_norm": _colsum("drn_sum", grads["rec_norm"]).reshape(REC_HEADS, REC_DIM).sum(axis=0),
        "loss": _colsum("loss_sum", loss_p),
    }
    for n in ("norm_ffn1", "norm_mix", "norm_ffn2", "norm_ple", "norm_final"):
        small_vals[n] = _colsum(n + "_sum", grads[n])
    red = _allreduce_small(_pack_small(small_vals))
    small_shapes = {n: wsh[n].shape for n in SMALL}
    small_shapes["loss"] = (D_MODEL,)
    small = _unpack_small(red, small_shapes)
    loss = 0.5 * jnp.sum(small["loss"]) / D_MODEL

    out_g, out_d, out_m, out_v = {}, {}, {}, {}
    for n, gm, gs in zip(BIG, s_mine, s_sib):
        res = _adamw_halves("adamw_" + n, wsh[n][0], args["m_" + n][0], args["v_" + n][0], gm, gs, cc)
        out_g[n], out_d[n], out_m[n], out_v[n] = (t[None] for t in res)
    sw = _pack_small({**{n: wsh[n] for n in SMALL}, "loss": jnp.zeros((D_MODEL,), F32)})
    sm = _pack_small({**{n: args["m_" + n] for n in SMALL}, "loss": jnp.zeros((D_MODEL,), F32)})
    sv = _pack_small({**{n: args["v_" + n] for n in SMALL}, "loss": jnp.ones((D_MODEL,), F32)})
    sd, snm, snv = _adamw("adamw_small", sw, red, sm, sv)
    ud, um, uv = (_unpack_small(t, small_shapes) for t in (sd, snm, snv))
    for n in SMALL:
        out_g[n], out_d[n], out_m[n], out_v[n] = small[n], ud[n], um[n], uv[n]

    return (loss, dx.reshape(B, S, D_MODEL), *[out_g[n] for n in WEIGHTS], *[out_d[n] for n in WEIGHTS],
            *[out_m[n] for n in WEIGHTS], *[out_v[n] for n in WEIGHTS])
```

```python
import numpy as np
import jax
import jax.numpy as jnp
from jax import lax
from jax.experimental import pallas as pl
from jax.experimental.pallas import tpu as pltpu

F32 = jnp.float32
BF16 = jnp.bfloat16
MESH = pl.DeviceIdType.MESH

D_MODEL = 1024
D_FF = 2816
FF_SHARD = 2 * D_FF // 4
HEAD_DIM = 64
N_Q_HEADS = 8
ATT_BLOCK = 128
N_BUCKETS = 32
MAX_DISTANCE = 128
REC_HEADS = 4
REC_DIM = 128
PLE_DIM = 256
EPS = 1e-6
IN_W = 4864
COL_AQ, COL_AK, COL_AV, COL_RQ, COL_RF, COL_RI, COL_RG, COL_GA, COL_GB = 0, 4, 5, 6, 10, 14, 18, 22, 30

CHUNK = 64
SUB = 8
N_SUB = CHUNK // SUB

ADAM_LR, ADAM_B1, ADAM_B2, ADAM_EPS, ADAM_WD, ADAM_STEP = 0.001, 0.9, 0.999, 1e-08, 0.01, 10

V7X_VMEM_LIMIT = 56 * 1024 * 1024
N_CHIPS = 4
N_DEV = 8

BIG = ("w_ffn1_in", "w_ffn1_out", "w_in", "w_att_proj", "w_rec_proj", "w_out",
       "w_ffn2_in", "w_ffn2_out", "w_ple_gate", "w_ple_proj")
COL_SHARDED = ("w_ffn1_in", "w_in", "w_att_proj", "w_rec_proj", "w_ffn2_in", "w_ple_proj")
WEIGHTS = ("rel_bias", "lb_param", "norm_ffn1", "w_ffn1_in", "w_ffn1_out", "norm_mix", "w_in", "attn_sinks",
           "rec_norm", "w_att_proj", "w_rec_proj", "w_out", "norm_ffn2", "w_ffn2_in", "w_ffn2_out", "norm_ple",
           "w_ple_gate", "w_ple_proj", "norm_final")
SMALL = tuple(n for n in WEIGHTS if n not in BIG)
SMALL_ROWS = 64


def _params(*sem):
    return pltpu.CompilerParams(dimension_semantics=sem, vmem_limit_bytes=V7X_VMEM_LIMIT)


def _pick(n, cap, mult=8):
    if n <= cap:
        return n
    for t in range(cap - cap % mult, 0, -mult):
        if n % t == 0:
            return t
    raise ValueError((n, cap, mult))


def _dot(a, b):
    return jnp.dot(a, b, preferred_element_type=F32)


def _dot_nt(a, b):
    return lax.dot_general(a, b, (((1,), (1,)), ((), ())), preferred_element_type=F32)


def _dot_tn(a, b):
    return lax.dot_general(a, b, (((0,), (0,)), ((), ())), preferred_element_type=F32)


def _split3(x):
    hi = x.astype(BF16)
    r = x - hi.astype(F32)
    mid = r.astype(BF16)
    lo = (r - mid.astype(F32)).astype(BF16)
    return hi, mid, lo


def _split2(x):
    hi = x.astype(BF16)
    return hi, (x - hi.astype(F32)).astype(BF16)


def _sel_left(sel_bf16, x):
    hi, mid, lo = _split3(x)
    return _dot(sel_bf16, hi) + _dot(sel_bf16, mid) + _dot(sel_bf16, lo)


def _sel_right(x, sel_bf16):
    hi, mid, lo = _split3(x)
    return _dot(hi, sel_bf16) + _dot(mid, sel_bf16) + _dot(lo, sel_bf16)


def _sigmoid(x):
    return 1.0 / (1.0 + jnp.exp(-x))


def _group8(x):
    r, w = x.shape
    return x.reshape(r // 8, 8, w).sum(axis=0)


class _Comm:
    def __init__(self, ins, out_shapes, n_sems, start, finish):
        self.ins, self.out_shapes, self.n_sems, self.start, self.finish = ins, out_shapes, n_sems, start, finish


ANY = pl.BlockSpec(memory_space=pl.ANY)


def _comm_parts(comm):
    if comm is None:
        return [], [], [], []
    sems = [pltpu.SemaphoreType.DMA((comm.n_sems,)), pltpu.SemaphoreType.DMA((comm.n_sems,))]
    return list(comm.ins), [ANY] * len(comm.ins), list(comm.out_shapes), sems


def _comm_run(comm, grid, refs, n_in, n_out):
    if comm is None:
        return (lambda: None), (lambda: None)
    nci, nco = len(comm.ins), len(comm.out_shapes)
    cin = refs[n_in:n_in + nci]
    cout = refs[n_in + nci + n_out:n_in + nci + n_out + nco]
    send_sems, recv_sems = refs[-2], refs[-1]
    ids = [pl.program_id(d) for d in range(len(grid))]
    is_first = ids[0] == 0
    is_last = ids[0] == grid[0] - 1
    for d in range(1, len(grid)):
        is_first = is_first & (ids[d] == 0)
        is_last = is_last & (ids[d] == grid[d] - 1)

    def first():
        @pl.when(is_first)
        def _():
            comm.start(cin, cout, send_sems, recv_sems)

    def last():
        @pl.when(is_last)
        def _():
            comm.finish(cin, cout, send_sems, recv_sems)

    return first, last


def _call(name, fn, grid, ins, outs, pairs=(), comm=None):
    in_pair = {i for p in pairs for i in p[:2]}
    n_in, n_out = len(ins), len(outs)
    c_arrays, c_in_specs, c_out_shapes, c_sems = _comm_parts(comm)

    def body(*refs):
        first, last = _comm_run(comm, grid, refs, n_in, n_out)
        first()
        accs = []
        for ia, ib, kind in pairs:
            a, b = refs[ia][...].astype(BF16), refs[ib][...].astype(BF16)
            accs.append(_dot(a, b) if kind == "nn" else _dot_nt(a, b))
        vals = [refs[i][...] for i in range(n_in) if i not in in_pair]
        res = fn(accs, vals)
        out_refs = refs[n_in + len(c_arrays):n_in + len(c_arrays) + n_out]
        assert len(res) == len(out_refs), (name, len(res), len(out_refs))
        for o_ref, val in zip(out_refs, res):
            o_ref[...] = val.astype(o_ref.dtype)
        last()

    return pl.pallas_call(
        body, name=name, grid=grid,
        in_specs=[pl.BlockSpec(blk, im) for _, blk, im in ins] + c_in_specs,
        out_specs=[pl.BlockSpec(blk, im) for _, _, blk, im in outs] + [ANY] * len(c_out_shapes),
        out_shape=[jax.ShapeDtypeStruct(shp, dt) for shp, dt, _, _ in outs] + c_out_shapes,
        scratch_shapes=c_sems,
        compiler_params=_params(*(["arbitrary"] * len(grid))))(*[a for a, _, _ in ins], *c_arrays)


def _tile_call(name, fn, M, N, tm, tn, *, pairs=(), tiles=(), consts=(), outs=(), parts=0, comm=None):
    gi, gj = M // tm, N // tn
    assert gi * tm == M and gj * tn == N, (name, M, N, tm, tn)
    ins, prs = [], []
    for a, a_col, b, kind in pairs:
        K = b.shape[0] if kind == "nn" else b.shape[1]
        ins.append((a, (tm, K), lambda i, j, c=a_col: (i, c)))
        if kind == "nn":
            ins.append((b, (K, tn), lambda i, j: (0, j)))
        else:
            ins.append((b, (tn, K), lambda i, j: (j, 0)))
        prs.append((len(ins) - 2, len(ins) - 1, kind))
    for arr, off in tiles:
        ins.append((arr, (tm, tn), lambda i, j, o=off: (i, j + o)))
    for arr in consts:
        ins.append((arr, arr.shape, lambda i, j: (0, 0)))
    out_l = [((M, N), dt, (tm, tn), lambda i, j: (i, j)) for dt in outs]
    out_l += [((gi * 8, N), F32, (8, tn), lambda i, j: (i, j))] * parts
    nt = len(tiles)

    def wrapped(accs, vals):
        return fn(accs, vals[:nt], vals[nt:])

    return _call(name, wrapped, (gi, gj), ins, out_l, prs, comm=comm)


def _mm_tn(name, grid, a_in, b_in, outs):
    nk = grid[2]
    tm = [d for d in a_in[1] if d is not None][1]
    tn = [d for d in b_in[1] if d is not None][1]

    def body(a_ref, b_ref, *rest):
        out_refs, acc_ref = rest[:-1], rest[-1]
        k = pl.program_id(2)

        @pl.when(k == 0)
        def _():
            acc_ref[...] = jnp.zeros_like(acc_ref)

        acc_ref[...] += _dot_tn(a_ref[...].astype(BF16), b_ref[...].astype(BF16))

        @pl.when(k == nk - 1)
        def _():
            for o_ref in out_refs:
                o_ref[...] = acc_ref[...].astype(o_ref.dtype)

    return pl.pallas_call(
        body, name=name, grid=grid,
        in_specs=[pl.BlockSpec(a_in[1], a_in[2]), pl.BlockSpec(b_in[1], b_in[2])],
        out_specs=[pl.BlockSpec(blk, im) for _, _, blk, im in outs],
        out_shape=[jax.ShapeDtypeStruct(shp, dt) for shp, dt, _, _ in outs],
        scratch_shapes=[pltpu.VMEM((tm, tn), F32)],
        compiler_params=_params("arbitrary", "arbitrary", "arbitrary"))(a_in[0], b_in[0])


def _grad_pair(shape, block, imap):
    return [(shape, F32, block, imap), (shape, BF16, block, imap)]


def _mm_tn_rows(name, a, b, tk=1024):
    T, a_w = a.shape
    b_w = b.shape[1]
    tm = _pick(a_w, 1408, 128)
    tk = _pick(T, tk, 128)
    g32, g16 = _mm_tn(name, (a_w // tm, 1, T // tk),
                      (a, (tk, tm), lambda i, j, k: (k, i)), (b, (tk, b_w), lambda i, j, k: (k, 0)),
                      _grad_pair((a_w, b_w), (tm, b_w), lambda i, j, k: (i, 0)))
    shp = (N_CHIPS, a_w // N_CHIPS, b_w)
    return g32.reshape(shp), g16.reshape(shp)


def _mm_tn_cols(name, a, b, tk=1024):
    T, a_w = a.shape
    n = b.shape[1] // N_CHIPS
    tk = _pick(T, tk, 128)
    return _mm_tn(name, (1, N_CHIPS, T // tk),
                  (a, (tk, a_w), lambda i, j, k: (k, 0)), (b, (tk, n), lambda i, j, k: (k, j)),
                  _grad_pair((N_CHIPS, a_w, n), (None, a_w, n), lambda i, j, k: (j, 0, 0)))


def _colsum(name, x):
    def body(x_ref, o_ref):
        o_ref[...] = jnp.sum(x_ref[...], axis=0, keepdims=True)
    return pl.pallas_call(body, name=name, out_shape=jax.ShapeDtypeStruct((1, x.shape[1]), F32))(x)


def _rms_hat(h):
    return h * lax.rsqrt(jnp.mean(h * h, axis=-1, keepdims=True) + EPS)


def _rms_bwd_vals(dn, h, g):
    r = lax.rsqrt(jnp.mean(h * h, axis=-1, keepdims=True) + EPS)
    nh = h * r
    gd = dn * g
    dh = r * (gd - nh * jnp.mean(gd * nh, axis=-1, keepdims=True))
    return dh, _group8(dn * nh)


def _rms_fwd(name, h, g, tm=512):
    T = h.shape[0]

    def fn(accs, tv, cv):
        return [_rms_hat(tv[0]) * cv[0]]

    return _tile_call(name, fn, T, D_MODEL, _pick(T, tm), D_MODEL, tiles=[(h, 0)], consts=[g], outs=[BF16])[0]


def _ffn_fwd(tag, h, g, w_in, w_out, comm_in=None, comm_out=None, w_out_of=None):
    T = h.shape[0]
    n = _rms_fwd(tag + "_norm", h, g)
    tm = _pick(T, 512)
    wblk = (None, D_MODEL, FF_SHARD)

    def act(accs, vals):
        gate, up = accs
        return [gate, up, gate * _sigmoid(gate) * up]

    tile = lambda: ((T, D_FF), BF16, (tm, FF_SHARD), lambda i, j: (i, j))
    gate, up, a, *got_in = _call(
        tag + "_in", act, (T // tm, 2),
        [(n, (tm, D_MODEL), lambda i, j: (i, 0)),
         (w_in, wblk, lambda i, j: (j, 0, 0)), (w_in, wblk, lambda i, j: (j + 2, 0, 0))],
        [tile(), tile(), tile()], pairs=[(0, 1, "nn"), (0, 2, "nn")], comm=comm_in)

    def res(accs, tv, cv):
        return [tv[0] + 0.5 * accs[0]]

    if w_out_of is not None:
        w_out = w_out_of(got_in)
    h_new, *got_out = _tile_call(tag + "_out", res, T, D_MODEL, _pick(T, 512), 512,
                                 pairs=[(a, 0, w_out, "nn")], tiles=[(h, 0)], outs=[F32], comm=comm_out)
    return h_new, (n, gate, up, a), got_out


def _ffn_bwd(tag, dh_out, h, g, w_in, w_out, saved, comm=None, comm_last=None):
    T = h.shape[0]
    n, gate, up, a = saved
    tm = _pick(T, 512)

    def half(accs, tv, cv):
        return [0.5 * tv[0]]

    df = _tile_call(tag + "_df", half, T, D_MODEL, _pick(T, 512), D_MODEL, tiles=[(dh_out, 0)], outs=[BF16])[0]

    def dact(accs, vals):
        da = accs[0]
        gt, u = vals[0].astype(F32), vals[1].astype(F32)
        sg = _sigmoid(gt)
        silu = gt * sg
        return [jnp.stack([da * u * (sg + silu * (1.0 - sg)), da * silu])]

    dz, *got = _call(
        tag + "_dact", dact, (T // tm, 2),
        [(df, (tm, D_MODEL), lambda i, j: (i, 0)), (w_out, (FF_SHARD, D_MODEL), lambda i, j: (j, 0)),
         (gate, (tm, FF_SHARD), lambda i, j: (i, j)), (up, (tm, FF_SHARD), lambda i, j: (i, j))],
        [((2, T, D_FF), BF16, (2, tm, FF_SHARD), lambda i, j: (0, i, j))], pairs=[(0, 1, "nt")], comm=comm)
    dw_out = _mm_tn_rows(tag + "_dwout", a, df)
    tk = _pick(T, 1024, 128)
    dw_in = _mm_tn(tag + "_dwin", (1, N_CHIPS, T // tk),
                   (n, (tk, D_MODEL), lambda i, j, k: (k, 0)),
                   (dz, (None, tk, FF_SHARD), lambda i, j, k: (j // 2, k, j % 2)),
                   _grad_pair((N_CHIPS, D_MODEL, FF_SHARD), (None, D_MODEL, FF_SHARD), lambda i, j, k: (j, 0, 0)))

    def dnorm(accs, vals):
        dn = accs[0] + accs[1] + accs[2] + accs[3]
        dh, dg = _rms_bwd_vals(dn, vals[0], vals[2])
        return [vals[1] + dh, dg]

    tm2 = _pick(T, 256)
    ins = [(dz, (None, tm2, FF_SHARD), lambda i, j, s=s: (s // 2, i, s % 2)) for s in range(N_CHIPS)]
    ins += [(w_in, (None, D_MODEL, FF_SHARD), lambda i, j, s=s: (s, 0, 0)) for s in range(N_CHIPS)]
    ins += [(h, (tm2, D_MODEL), lambda i, j: (i, 0)), (dh_out, (tm2, D_MODEL), lambda i, j: (i, 0)),
            (g, g.shape, lambda i, j: (0, 0))]
    dh, dg, *got_last = _call(
        tag + "_dnorm", dnorm, (T // tm2, 1), ins,
        [((T, D_MODEL), F32, (tm2, D_MODEL), lambda i, j: (i, 0)),
         ((T // tm2 * 8, D_MODEL), F32, (8, D_MODEL), lambda i, j: (i, 0))],
        pairs=[(s, N_CHIPS + s, "nt") for s in range(N_CHIPS)],
        comm=None if comm_last is None else comm_last(dw_in, dw_out))
    return dh, dg, dw_in, dw_out, got, got_last


def _t5_onehot():
    qi = np.arange(ATT_BLOCK)[:, None] + ATT_BLOCK
    kj = np.arange(2 * ATT_BLOCK)[None, :]
    nn = np.maximum(qi - kj, 0)
    max_exact = N_BUCKETS // 2
    large = max_exact + (np.log(np.maximum(nn, 1) / max_exact) / np.log(MAX_DISTANCE / max_exact)
                         * (N_BUCKETS - max_exact)).astype(np.int32)
    large = np.minimum(large, N_BUCKETS - 1)
    bucket = np.where(nn < max_exact, nn, large).astype(np.int32).reshape(-1)
    return (bucket[None, :] == np.arange(N_BUCKETS)[:, None]).astype(np.float32)


def _small_mm(name, a, b, sel):
    def body(a_ref, b_ref, o_ref):
        if sel == "right":
            o_ref[...] = _sel_right(a_ref[...], b_ref[...])
        else:
            o_ref[...] = _sel_left(a_ref[...], b_ref[...])
    return pl.pallas_call(body, name=name, out_shape=jax.ShapeDtypeStruct((a.shape[0], b.shape[1]), F32),
                          compiler_params=pltpu.CompilerParams(vmem_limit_bytes=V7X_VMEM_LIMIT))(a, b)


def _swap_heads(t):
    return jnp.concatenate([t[:, HEAD_DIM:], t[:, :HEAD_DIM]], axis=1)


def _kv_layouts(proj):
    T = proj.shape[0]

    def fn(accs, tv, cv):
        return [tv[0], tv[1]]

    k, v = _tile_call("kv_cast", fn, T, 128, _pick(T, 1024), 128, tiles=[(proj, COL_AK), (proj, COL_AV)],
                      outs=[BF16, BF16])
    return jnp.concatenate([k, _swap_heads(k)], axis=1), jnp.concatenate([v, _swap_heads(v)], axis=1)


def _swa_masks():
    row = lax.broadcasted_iota(jnp.int32, (ATT_BLOCK, 2 * ATT_BLOCK), 0)
    col = lax.broadcasted_iota(jnp.int32, (ATT_BLOCK, 2 * ATT_BLOCK), 1)
    dist = ATT_BLOCK + row - col
    return (dist >= 0) & (dist < ATT_BLOCK), col


def _swa_heads():
    out = []
    for h in range(N_Q_HEADS):
        lo = h % 2 == 0
        hk = h // 4
        swapped = (hk == 1) if lo else (hk == 0)
        out.append((h // 2, lo, swapped))
    return out


def _swa_probs(qm, kk, bias_h, sink, valid):
    s = _dot_nt(qm, kk) * (HEAD_DIM ** -0.5) + bias_h
    s = jnp.where(valid, s, -jnp.inf)
    m = jnp.maximum(jnp.max(s, axis=-1, keepdims=True), sink)
    e = jnp.exp(s - m)
    es = jnp.exp(sink - m)
    den = jnp.sum(e, axis=-1, keepdims=True) + es
    return e / den, es / den


def _swa_fwd(proj, kk2, vv2, bias, sinks, B, S):
    T = B * S
    nb = S // ATT_BLOCK

    def body(q_ref, k_ref, v_ref, bias_ref, sink_ref, o_ref, kpad, vpad):
        zeros = jnp.zeros((ATT_BLOCK, 256), BF16)
        kpad[pl.ds(0, ATT_BLOCK), :] = zeros
        vpad[pl.ds(0, ATT_BLOCK), :] = zeros
        kpad[pl.ds(ATT_BLOCK, S), :] = k_ref[...]
        vpad[pl.ds(ATT_BLOCK, S), :] = v_ref[...]
        valid0, col = _swa_masks()
        lane = lax.broadcasted_iota(jnp.int32, (1, 128), 1)
        lo_q = lane < HEAD_DIM
        heads = _swa_heads()

        def blk(n, carry):
            r0 = pl.multiple_of(n * ATT_BLOCK, ATT_BLOCK)
            valid = valid0 & ((n > 0) | (col >= ATT_BLOCK))
            kb = kpad[pl.ds(r0, 2 * ATT_BLOCK), :]
            vb = vpad[pl.ds(r0, 2 * ATT_BLOCK), :]
            for j in range(N_Q_HEADS // 2):
                qblk = q_ref[pl.ds(r0, ATT_BLOCK), pl.ds(128 * j, 128)].astype(BF16)
                acc = jnp.zeros((ATT_BLOCK, 128), F32)
                for h in (2 * j, 2 * j + 1):
                    _, lo, swapped = heads[h]
                    keep = lo_q if lo else ~lo_q
                    qm = jnp.where(keep, qblk, jnp.zeros_like(qblk))
                    kk = kb[:, 128:] if swapped else kb[:, :128]
                    vv = vb[:, 128:] if swapped else vb[:, :128]
                    vm = jnp.where(keep, vv, jnp.zeros_like(vv))
                    p, _ = _swa_probs(qm, kk, bias_ref[h], sink_ref[h], valid)
                    acc = acc + _dot(p.astype(BF16), vm)
                o_ref[pl.ds(r0, ATT_BLOCK), pl.ds(128 * j, 128)] = acc.astype(o_ref.dtype)
            return carry

        lax.fori_loop(0, nb, blk, 0)

    return pl.pallas_call(
        body, name="swa_fwd", grid=(B,),
        in_specs=[pl.BlockSpec((S, 512), lambda b: (b, 0)),
                  pl.BlockSpec((S, 256), lambda b: (b, 0)),
                  pl.BlockSpec((S, 256), lambda b: (b, 0)),
                  pl.BlockSpec((N_Q_HEADS, ATT_BLOCK, 2 * ATT_BLOCK), lambda b: (0, 0, 0)),
                  pl.BlockSpec(memory_space=pltpu.SMEM)],
        out_specs=pl.BlockSpec((S, 512), lambda b: (b, 0)),
        out_shape=jax.ShapeDtypeStruct((T, 512), BF16),
        scratch_shapes=[pltpu.VMEM((S + ATT_BLOCK, 256), BF16), pltpu.VMEM((S + ATT_BLOCK, 256), BF16)],
        compiler_params=_params("arbitrary"))(proj, kk2, vv2, bias, sinks)


def _swa_bwd(proj, kk2, vv2, bias, sinks, datt, B, S):
    T = B * S
    nb = S // ATT_BLOCK

    def body(q_ref, k_ref, v_ref, bias_ref, sink_ref, do_ref, dq_ref, dk_ref, dv_ref, dbias_ref, dsink_ref,
             kpad, vpad, dkpad, dvpad):
        b = pl.program_id(0)

        @pl.when(b == 0)
        def _():
            dbias_ref[...] = jnp.zeros_like(dbias_ref)
            dsink_ref[...] = jnp.zeros_like(dsink_ref)

        zeros = jnp.zeros((ATT_BLOCK, 256), BF16)
        kpad[pl.ds(0, ATT_BLOCK), :] = zeros
        vpad[pl.ds(0, ATT_BLOCK), :] = zeros
        kpad[pl.ds(ATT_BLOCK, S), :] = k_ref[...]
        vpad[pl.ds(ATT_BLOCK, S), :] = v_ref[...]
        dkpad[...] = jnp.zeros_like(dkpad)
        dvpad[...] = jnp.zeros_like(dvpad)
        valid0, col = _swa_masks()
        lane = lax.broadcasted_iota(jnp.int32, (1, 128), 1)
        lo_q = lane < HEAD_DIM
        heads = _swa_heads()
        scale = HEAD_DIM ** -0.5

        def blk(n, carry):
            r0 = pl.multiple_of(n * ATT_BLOCK, ATT_BLOCK)
            valid = valid0 & ((n > 0) | (col >= ATT_BLOCK))
            kb = kpad[pl.ds(r0, 2 * ATT_BLOCK), :]
            vb = vpad[pl.ds(r0, 2 * ATT_BLOCK), :]
            dk_acc = [jnp.zeros((2 * ATT_BLOCK, 128), F32), jnp.zeros((2 * ATT_BLOCK, 128), F32)]
            dv_acc = [jnp.zeros((2 * ATT_BLOCK, 128), F32), jnp.zeros((2 * ATT_BLOCK, 128), F32)]
            for j in range(N_Q_HEADS // 2):
                qblk = q_ref[pl.ds(r0, ATT_BLOCK), pl.ds(128 * j, 128)].astype(BF16)
                doblk = do_ref[pl.ds(r0, ATT_BLOCK), pl.ds(128 * j, 128)]
                dq = jnp.zeros((ATT_BLOCK, 128), F32)
                for h in (2 * j, 2 * j + 1):
                    _, lo, swapped = heads[h]
                    keep = lo_q if lo else ~lo_q
                    qm = jnp.where(keep, qblk, jnp.zeros_like(qblk))
                    dom = jnp.where(keep, doblk, jnp.zeros_like(doblk))
                    kk = kb[:, 128:] if swapped else kb[:, :128]
                    vv = vb[:, 128:] if swapped else vb[:, :128]
                    km = jnp.where(keep, kk, jnp.zeros_like(kk))
                    p, ps = _swa_probs(qm, kk, bias_ref[h], sink_ref[h], valid)
                    dp = _dot_nt(dom, vv)
                    delta = jnp.sum(p * dp, axis=-1, keepdims=True)
                    ds = p * (dp - delta)
                    dbias_ref[h] += ds
                    dsink_ref[pl.ds(h, 1), :] += -jnp.sum(jnp.broadcast_to(ps * delta, (ATT_BLOCK, 128)),
                                                          axis=0, keepdims=True)
                    dsb = (ds * scale).astype(BF16)
                    dq = dq + _dot(dsb, km)
                    idx = 1 if swapped else 0
                    dk_acc[idx] = dk_acc[idx] + _dot_tn(dsb, qm)
                    dv_acc[idx] = dv_acc[idx] + _dot_tn(p.astype(BF16), dom)
                dq_ref[pl.ds(r0, ATT_BLOCK), pl.ds(128 * j, 128)] = dq.astype(dq_ref.dtype)
            dkpad[pl.ds(r0, 2 * ATT_BLOCK), :] += jnp.concatenate(dk_acc, axis=1)
            dvpad[pl.ds(r0, 2 * ATT_BLOCK), :] += jnp.concatenate(dv_acc, axis=1)
            return carry

        lax.fori_loop(0, nb, blk, 0)
        dk_ref[...] = dkpad[pl.ds(ATT_BLOCK, S), :]
        dv_ref[...] = dvpad[pl.ds(ATT_BLOCK, S), :]

    return pl.pallas_call(
        body, name="swa_bwd", grid=(B,),
        in_specs=[pl.BlockSpec((S, 512), lambda b: (b, 0)),
                  pl.BlockSpec((S, 256), lambda b: (b, 0)),
                  pl.BlockSpec((S, 256), lambda b: (b, 0)),
                  pl.BlockSpec((N_Q_HEADS, ATT_BLOCK, 2 * ATT_BLOCK), lambda b: (0, 0, 0)),
                  pl.BlockSpec(memory_space=pltpu.SMEM),
                  pl.BlockSpec((S, 512), lambda b: (b, 0))],
        out_specs=[pl.BlockSpec((S, 512), lambda b: (b, 0)),
                   pl.BlockSpec((S, 256), lambda b: (b, 0)),
                   pl.BlockSpec((S, 256), lambda b: (b, 0)),
                   pl.BlockSpec((N_Q_HEADS, ATT_BLOCK, 2 * ATT_BLOCK), lambda b: (0, 0, 0)),
                   pl.BlockSpec((N_Q_HEADS, 128), lambda b: (0, 0))],
        out_shape=[jax.ShapeDtypeStruct((T, 512), BF16),
                   jax.ShapeDtypeStruct((T, 256), F32),
                   jax.ShapeDtypeStruct((T, 256), F32),
                   jax.ShapeDtypeStruct((N_Q_HEADS, ATT_BLOCK, 2 * ATT_BLOCK), F32),
                   jax.ShapeDtypeStruct((N_Q_HEADS, 128), F32)],
        scratch_shapes=[pltpu.VMEM((S + ATT_BLOCK, 256), BF16), pltpu.VMEM((S + ATT_BLOCK, 256), BF16),
                        pltpu.VMEM((S + ATT_BLOCK, 256), F32), pltpu.VMEM((S + ATT_BLOCK, 256), F32)],
        compiler_params=_params("arbitrary"))(proj, kk2, vv2, bias, sinks, datt)


def _hgrn_gates(z, lb):
    sg = _sigmoid(z)
    f = lb + (1.0 - lb) * sg
    return sg, f, jnp.log(f), 1.0 - f


def _hgrn_consts():
    r = lax.broadcasted_iota(jnp.int32, (CHUNK, CHUNK), 0)
    c = lax.broadcasted_iota(jnp.int32, (CHUNK, CHUNK), 1)
    tril = (r >= c).astype(BF16)
    triu = (r <= c).astype(BF16)
    causal = r >= c
    below = (r // SUB) > (c // SUB)
    inside = ((r // SUB) == (c // SUB)) & causal
    return tril, triu, causal, below, inside, c


def _block_rows(ref, s):
    rows = []
    for i in range(N_SUB):
        if SUB * i + s < 0:
            rows.append(jnp.zeros((SUB, REC_DIM), F32))
        else:
            rows.append(jnp.broadcast_to(ref[pl.ds(SUB * i + s, 1), :], (SUB, REC_DIM)))
    return jnp.concatenate(rows, axis=0)


def _hgrn_offdiag(q, k, bcum, b_ref):
    eq = jnp.exp(jnp.minimum(bcum - _block_rows(b_ref, -1), 0.0))
    qe = q * eq
    zero = jnp.zeros((SUB, REC_DIM), F32)
    q_rows, k_cols, eks = [jnp.zeros((SUB, (N_SUB - 1) * REC_DIM), F32)], [], []
    for i in range(1, N_SUB):
        q_rows.append(jnp.concatenate([zero] * (i - 1) + [qe[SUB * i:SUB * (i + 1), :]] + [zero] * (N_SUB - 1 - i),
                                      axis=1))
        p = b_ref[pl.ds(SUB * i - 1, 1), :]
        pad = jnp.zeros((CHUNK - SUB * i, REC_DIM), F32)
        ek = jnp.concatenate([jnp.exp(p - b_ref[pl.ds(0, SUB * i), :]), pad], axis=0)
        k_cols.append(k * ek)
        eks.append(ek)
    return jnp.concatenate(q_rows, axis=0), jnp.concatenate(k_cols, axis=1), eq, eks


def _hgrn_fwd(proj, lb_param, B, S):
    T = B * S
    nc = S // CHUNK

    def body(q_ref, z_ref, v_ref, lb_ref, o_ref, st_ref, k_s, b_s):
        lb = _sigmoid(lb_ref[0:1, :] - lb_ref[1:2, :])
        tril, _, _, below, inside, col = _hgrn_consts()
        col_s = col & (SUB - 1)

        def chunk(ci, ht):
            r0 = pl.multiple_of(ci * CHUNK, CHUNK)
            q = q_ref[pl.ds(r0, CHUNK), :]
            v = v_ref[pl.ds(r0, CHUNK), :]
            _, _, g, k = _hgrn_gates(z_ref[pl.ds(r0, CHUNK), :], lb)
            bcum = _sel_left(tril, g)
            k_s[...] = k
            b_s[...] = bcum
            st_ref[ci] = ht
            qst, kst, _, _ = _hgrn_offdiag(q, k, bcum, b_s)
            d = jnp.zeros((CHUNK, CHUNK), F32)
            for s in range(SUB):
                w = jnp.exp(jnp.minimum(bcum - _block_rows(b_s, s), 0.0))
                colv = jnp.sum(q * _block_rows(k_s, s) * w, axis=-1, keepdims=True)
                d = jnp.where(col_s == s, colv, d)
            a = jnp.where(below, _dot_nt(qst.astype(BF16), kst.astype(BF16)), 0.0) + jnp.where(inside, d, 0.0)
            vb = v.astype(BF16)
            qb = (q * jnp.exp(bcum)).astype(BF16)
            o_ref[pl.ds(r0, CHUNK), :] = _dot(a.astype(BF16), vb) + _dot_nt(qb, ht.astype(BF16))
            b_last = b_s[pl.ds(CHUNK - 1, 1), :]
            kb = (k * jnp.exp(b_last - bcum)).astype(BF16)
            return ht * jnp.exp(b_last) + _dot_tn(vb, kb)

        lax.fori_loop(0, nc, chunk, jnp.zeros((REC_DIM, REC_DIM), F32))

    H = REC_HEADS
    return pl.pallas_call(
        body, name="hgrn_fwd", grid=(B, H),
        in_specs=[pl.BlockSpec((S, 128), lambda b, h: (b, COL_RQ + h)),
                  pl.BlockSpec((S, 128), lambda b, h: (b, COL_RF + h)),
                  pl.BlockSpec((S, 128), lambda b, h: (b, COL_RI + h)),
                  pl.BlockSpec((2, 128), lambda b, h: (0, h))],
        out_specs=[pl.BlockSpec((S, 128), lambda b, h: (b, h)),
                   pl.BlockSpec((nc, REC_DIM, REC_DIM), lambda b, h: (b * H + h, 0, 0))],
        out_shape=[jax.ShapeDtypeStruct((T, 512), F32),
                   jax.ShapeDtypeStruct((B * H * nc, REC_DIM, REC_DIM), F32)],
        scratch_shapes=[pltpu.VMEM((CHUNK, REC_DIM), F32), pltpu.VMEM((CHUNK, REC_DIM), F32)],
        compiler_params=_params("arbitrary", "arbitrary"))(proj, proj, proj, lb_param)


def _hgrn_bwd(proj, lb_param, states, do, B, S, comm=None):
    T = B * S
    nc = S // CHUNK

    c_arrays, c_in_specs, c_out_shapes, c_sems = _comm_parts(comm)
    nci, nco = len(c_arrays), len(c_out_shapes)

    def body(*refs):
        q_ref, z_ref, v_ref, lb_ref, st_ref, do_ref = refs[:6]
        dq_ref, dz_ref, dv_ref, dlb_ref = refs[6 + nci:10 + nci]
        k_s, b_s = refs[10 + nci + nco:12 + nci + nco]
        comm_first, comm_last = _comm_run(comm, (B, REC_HEADS), refs, 6, 4)
        comm_first()
        lb = _sigmoid(lb_ref[0:1, :] - lb_ref[1:2, :])
        tril, triu, causal, below, inside, col = _hgrn_consts()
        col_s = col & (SUB - 1)
        last_row = lax.broadcasted_iota(jnp.int32, (CHUNK, 1), 0) == CHUNK - 1
        rc = lax.broadcasted_iota(jnp.int32, (CHUNK, SUB * REC_DIM), 0)
        lc = lax.broadcasted_iota(jnp.int32, (CHUNK, SUB * REC_DIM), 1)
        spread = ((rc & (SUB - 1)) == (lc // REC_DIM)).astype(BF16)
        rr = lax.broadcasted_iota(jnp.int32, (CHUNK, SUB * CHUNK), 0)
        cc = lax.broadcasted_iota(jnp.int32, (CHUNK, SUB * CHUNK), 1)
        gather = (((rr // SUB) == ((cc & (CHUNK - 1)) // SUB)) & ((rr & (SUB - 1)) == (cc // CHUNK))).astype(BF16)

        def chunk(it, carry):
            dht, dlb = carry
            ci = nc - 1 - it
            r0 = pl.multiple_of(ci * CHUNK, CHUNK)
            q = q_ref[pl.ds(r0, CHUNK), :]
            v = v_ref[pl.ds(r0, CHUNK), :]
            dout = do_ref[pl.ds(r0, CHUNK), :]
            sg, f, g, k = _hgrn_gates(z_ref[pl.ds(r0, CHUNK), :], lb)
            bcum = _sel_left(tril, g)
            k_s[...] = k
            b_s[...] = bcum
            ht = st_ref[ci]
            vb, dob = v.astype(BF16), dout.astype(BF16)
            da = jnp.where(causal, _dot_nt(dob, vb), 0.0)
            da_hi, da_lo = _split2(jnp.where(inside, da, 0.0))
            da_in = _dot(da_hi, spread) + _dot(da_lo, spread)
            d = jnp.zeros((CHUNK, CHUNK), F32)
            dq = jnp.zeros((CHUNK, REC_DIM), F32)
            pieces, pieces_lo = [], []
            for s in range(SUB):
                w = jnp.exp(jnp.minimum(bcum - _block_rows(b_s, s), 0.0))
                ks = _block_rows(k_s, s)
                qw = q * w
                d = jnp.where(col_s == s, jnp.sum(qw * ks, axis=-1, keepdims=True), d)
                das = da_in[:, REC_DIM * s:REC_DIM * (s + 1)]
                dq = dq + das * ks * w
                hi, lo = _split2(das * qw)
                pieces.append(hi)
                pieces_lo.append(lo)
            dk = _dot(gather, jnp.concatenate(pieces, axis=0)) + _dot(gather, jnp.concatenate(pieces_lo, axis=0))
            qst, kst, eq, eks = _hgrn_offdiag(q, k, bcum, b_s)
            qst_b, kst_b = qst.astype(BF16), kst.astype(BF16)
            a = jnp.where(below, _dot_nt(qst_b, kst_b), 0.0) + jnp.where(inside, d, 0.0)
            da_off = jnp.where(below, da, 0.0).astype(BF16)
            dqst = _dot(da_off, kst_b)
            dkst = _dot_tn(da_off, qst_b)
            dq_rows = [jnp.zeros((SUB, REC_DIM), F32)]
            for i in range(1, N_SUB):
                dq_rows.append(dqst[SUB * i:SUB * (i + 1), REC_DIM * (i - 1):REC_DIM * i])
                dk = dk + dkst[:, REC_DIM * (i - 1):REC_DIM * i] * eks[i - 1]
            dq = dq + jnp.concatenate(dq_rows, axis=0) * eq
            eb = jnp.exp(bcum)
            b_last = b_s[pl.ds(CHUNK - 1, 1), :]
            el = jnp.exp(b_last)
            ekb = jnp.exp(b_last - bcum)
            qb = (q * eb).astype(BF16)
            kb = k * ekb
            dhb = dht.astype(BF16)
            dv = _dot_tn(a.astype(BF16), dob) + _dot_nt(kb.astype(BF16), dhb)
            dqb = _dot(dob, ht.astype(BF16))
            dkb = _dot(vb, dhb)
            dht_new = dht * el + _dot_tn(dob, qb)
            dq = dq + eb * dqb
            dk = dk + ekb * dkb
            edge = jnp.sum(kb * dkb, axis=0, keepdims=True) + el * jnp.sum(ht * dht, axis=0, keepdims=True)
            db = q * dq - k * dk + jnp.where(last_row, edge, 0.0)
            dg = _sel_left(triu, db)
            df = dg / f - dk
            dz = df * (1.0 - lb) * sg * (1.0 - sg)
            dlb = dlb + jnp.sum(df * (1.0 - sg), axis=0, keepdims=True)
            dq_ref[pl.ds(r0, CHUNK), :] = dq.astype(dq_ref.dtype)
            dz_ref[pl.ds(r0, CHUNK), :] = dz.astype(dz_ref.dtype)
            dv_ref[pl.ds(r0, CHUNK), :] = dv.astype(dv_ref.dtype)
            return dht_new, dlb

        _, dlb = lax.fori_loop(0, nc, chunk, (jnp.zeros((REC_DIM, REC_DIM), F32), jnp.zeros((1, REC_DIM), F32)))
        dlb_ref[...] = jnp.broadcast_to(dlb * lb * (1.0 - lb), (8, REC_DIM))
        comm_last()

    H = REC_HEADS
    return pl.pallas_call(
        body, name="hgrn_bwd", grid=(B, H),
        in_specs=[pl.BlockSpec((S, 128), lambda b, h: (b, COL_RQ + h)),
                  pl.BlockSpec((S, 128), lambda b, h: (b, COL_RF + h)),
                  pl.BlockSpec((S, 128), lambda b, h: (b, COL_RI + h)),
                  pl.BlockSpec((2, 128), lambda b, h: (0, h)),
                  pl.BlockSpec((nc, REC_DIM, REC_DIM), lambda b, h: (b * H + h, 0, 0)),
                  pl.BlockSpec((S, 128), lambda b, h: (b, h))] + c_in_specs,
        out_specs=[pl.BlockSpec((S, 128), lambda b, h: (b, h))] * 3
        + [pl.BlockSpec((8, 128), lambda b, h: (b, h))] + [ANY] * nco,
        out_shape=[jax.ShapeDtypeStruct((T, 512), BF16)] * 3 + [jax.ShapeDtypeStruct((B * 8, 512), F32)]
        + c_out_shapes,
        scratch_shapes=[pltpu.VMEM((CHUNK, REC_DIM), F32)] * 2 + c_sems,
        compiler_params=_params("arbitrary", "arbitrary"))(proj, proj, proj, lb_param, states, do, *c_arrays)


def _rec_gate_fwd(rec, proj, rec_norm):
    T = rec.shape[0]

    def fn(accs, tv, cv):
        return [_rms_hat(tv[0]) * cv[0] * _sigmoid(tv[1])]

    return _tile_call("rec_gate", fn, T, 512, _pick(T, 1024), REC_DIM, tiles=[(rec, 0), (proj, COL_RG)],
                      consts=[rec_norm], outs=[BF16])[0]


def _rec_gate_bwd(dyb, w_rec_proj, rec, proj, rec_norm):
    T = rec.shape[0]

    def fn(accs, tv, cv):
        d, r, rg = accs[0], tv[0], tv[1]
        sg = _sigmoid(rg)
        rn = _rms_hat(r) * cv[0]
        dh, dg = _rms_bwd_vals(d * sg, r, cv[0])
        return [dh, d * rn * sg * (1.0 - sg), dg]

    return _tile_call("rec_gate_bwd", fn, T, 512, _pick(T, 1024), REC_DIM, pairs=[(dyb, 0, w_rec_proj, "nt")],
                      tiles=[(rec, 0), (proj, COL_RG)], consts=[rec_norm], outs=[F32, BF16], parts=1)


def _mix_out_fwd(att, recn, proj, w_att_proj, w_rec_proj, w_out, h1):
    T = att.shape[0]
    tn = 256

    def merge(accs, tv, cv):
        ya, yb = accs
        return [ya, yb, _sigmoid(tv[0]) * ya + _sigmoid(tv[1]) * yb]

    ya, yb, merged = _tile_call(
        "merge", merge, T, D_MODEL, _pick(T, 1024), tn,
        pairs=[(att, 0, w_att_proj, "nn"), (recn, 0, w_rec_proj, "nn")],
        tiles=[(proj, COL_GA * 128 // tn), (proj, COL_GB * 128 // tn)], outs=[BF16] * 3)

    def res(accs, tv, cv):
        return [tv[0] + accs[0]]

    h2 = _tile_call("mix_out", res, T, D_MODEL, _pick(T, 512), 512, pairs=[(merged, 0, w_out, "nn")],
                    tiles=[(h1, 0)], outs=[F32])[0]
    return h2, (ya, yb, merged)


GATHER_FIRST = ("w_ffn1_in",)
GATHER_MIX = ("w_ffn1_out", "w_in", "w_att_proj", "w_rec_proj", "w_out")
GATHER_LAST = ("w_ffn2_in", "w_ffn2_out", "w_ple_gate", "w_ple_proj")
SCATTER_LATE = ("w_ple_gate", "w_ple_proj", "w_ffn2_in", "w_ffn2_out")
SCATTER_MIX = ("w_out", "w_att_proj", "w_rec_proj", "w_in")
SCATTER_LAST = ("w_ffn1_in", "w_ffn1_out")


def _local_step(x, p, tgt, w, mine16, cc, me_chip, B, S):
    T = B * S
    w = dict(w)
    g_ffn1, g_mix, g_ffn2, g_ple = w["norm_ffn1"], w["norm_mix"], w["norm_ffn2"], w["norm_ple"]
    g_fin = w["norm_final"].reshape(1, D_MODEL)
    grads, part, from_chips = {}, {}, {}

    def gather(names):
        return _gather_comm([mine16[n] for n in names])

    def place(names, got):
        for n, g in zip(names, got):
            full = lax.dynamic_update_index_in_dim(g, mine16[n], me_chip, 0)
            w[n] = full if n in ("w_ffn1_in", "w_ffn2_in") else _natural(n, full)

    def scatter(tag, names):
        from_sib = _swap_halves("rs_sibling_" + tag, [grads[n][1] for n in names])
        for n, fs in zip(names, from_sib):
            part[n] = _add_sibling("rs_add_sib_" + n, grads[n][0], fs, cc)
        return _scatter_comm([part[n][1] for n in names])

    def scattered(names, got):
        for n, g in zip(names, got):
            from_chips[n] = g

    place(GATHER_FIRST, _run_comm("gather_first", gather(GATHER_FIRST)))
    def ffn1_out_weight(got):
        place(GATHER_MIX, got)
        return w["w_ffn1_out"]

    h1, sv1, got_last = _ffn_fwd("ffn1", x, g_ffn1, w["w_ffn1_in"], None, comm_in=gather(GATHER_MIX),
                                 comm_out=gather(GATHER_LAST), w_out_of=ffn1_out_weight)
    place(GATHER_LAST, got_last)
    u = _rms_fwd("mix_norm", h1, g_mix)

    def ident(accs, tv, cv):
        return [accs[0]]

    proj = _tile_call("in_proj", ident, T, IN_W, _pick(T, 1024), 256, pairs=[(u, 0, w["w_in"], "nn")],
                      outs=[F32])[0]
    onehot = jnp.asarray(_t5_onehot())
    bias = _small_mm("t5_bias", w["rel_bias"].T, onehot.astype(BF16), "right")
    bias = bias.reshape(N_Q_HEADS, ATT_BLOCK, 2 * ATT_BLOCK)
    sinks = w["attn_sinks"].reshape(N_Q_HEADS)
    kk2, vv2 = _kv_layouts(proj)
    att = _swa_fwd(proj, kk2, vv2, bias, sinks, B, S)
    rec, states = _hgrn_fwd(proj, w["lb_param"], B, S)
    recn = _rec_gate_fwd(rec, proj, w["rec_norm"])
    h2, (ya, yb, merged) = _mix_out_fwd(att, recn, proj, w["w_att_proj"], w["w_rec_proj"], w["w_out"], h1)
    h3, sv2, _ = _ffn_fwd("ffn2", h2, g_ffn2, w["w_ffn2_in"], w["w_ffn2_out"])
    n3 = _rms_fwd("ple_norm", h3, g_ple)

    def ple(accs, tv, cv):
        gate = _sigmoid(accs[0])
        return [gate, accs[1], tv[0] + gate * accs[1]]

    gate_p, pp, h4 = _tile_call(
        "ple", ple, T, D_MODEL, _pick(T, 512), 512,
        pairs=[(n3, 0, w["w_ple_gate"], "nn"), (p, 0, w["w_ple_proj"], "nn")], tiles=[(h3, 0)],
        outs=[BF16, BF16, F32])

    def head(accs, tv, cv):
        h, t = tv
        err = _rms_hat(h) * cv[0] - t
        dh, dg = _rms_bwd_vals(err * (1.0 / D_MODEL), h, cv[0])
        return [dh, _group8(err * err), dg]

    dh4, loss_p, dg_fin = _tile_call("loss_head", head, T, D_MODEL, _pick(T, 256), D_MODEL,
                                     tiles=[(h4, 0), (tgt, 0)], consts=[g_fin], outs=[F32], parts=2)
    grads["norm_final"] = dg_fin

    def dple(accs, tv, cv):
        d, gt, ppv = tv[0], tv[1].astype(F32), tv[2].astype(F32)
        return [d * ppv * gt * (1.0 - gt), d * gt]

    dzg, dpp = _tile_call("ple_dact", dple, T, D_MODEL, _pick(T, 512), D_MODEL,
                          tiles=[(dh4, 0), (gate_p, 0), (pp, 0)], outs=[BF16] * 2)
    grads["w_ple_gate"] = _mm_tn_rows("ple_dwg", n3, dzg)
    grads["w_ple_proj"] = _mm_tn_cols("ple_dwp", p, dpp)

    def dnorm(accs, tv, cv):
        dh, dg = _rms_bwd_vals(accs[0], tv[0], cv[0])
        return [tv[1] + dh, dg]

    dh3, grads["norm_ple"] = _tile_call(
        "ple_dnorm", dnorm, T, D_MODEL, _pick(T, 256), D_MODEL, pairs=[(dzg, 0, w["w_ple_gate"], "nt")],
        tiles=[(h3, 0), (dh4, 0)], consts=[g_ple], outs=[F32], parts=1)

    dh2, grads["norm_ffn2"], grads["w_ffn2_in"], grads["w_ffn2_out"], _, _ = _ffn_bwd(
        "ffn2b", dh3, h2, g_ffn2, w["w_ffn2_in"], w["w_ffn2_out"], sv2)
    scatter_late = scatter("late", SCATTER_LATE)

    def to_bf(accs, tv, cv):
        return [tv[0]]

    dh2b = _tile_call("mix_dcast", to_bf, T, D_MODEL, _pick(T, 512), D_MODEL, tiles=[(dh2, 0)], outs=[BF16])[0]
    grads["w_out"] = _mm_tn_rows("mix_dwout", merged, dh2b)
    tn = 256

    def dmerge(accs, tv, cv):
        dm = accs[0]
        sa, sb = _sigmoid(tv[0]), _sigmoid(tv[1])
        yav, ybv = tv[2].astype(F32), tv[3].astype(F32)
        return [dm * sa, dm * sb, dm * yav * sa * (1.0 - sa), dm * ybv * sb * (1.0 - sb)]

    dya, dyb, dga, dgb = _tile_call(
        "mix_dmerge", dmerge, T, D_MODEL, _pick(T, 1024), tn, pairs=[(dh2b, 0, w["w_out"], "nt")],
        tiles=[(proj, COL_GA * 128 // tn), (proj, COL_GB * 128 // tn), (ya, 0), (yb, 0)], outs=[BF16] * 4)
    grads["w_att_proj"] = _mm_tn_cols("mix_dwatt", att, dya)
    grads["w_rec_proj"] = _mm_tn_cols("mix_dwrec", recn, dyb)

    datt = _tile_call("mix_datt", ident, T, 512, _pick(T, 1024), 512, pairs=[(dya, 0, w["w_att_proj"], "nt")],
                      outs=[BF16])[0]
    drec, drg, grads["rec_norm"] = _rec_gate_bwd(dyb, w["w_rec_proj"], rec, proj, w["rec_norm"])

    drq, drf, dri, dlb, *got = _hgrn_bwd(proj, w["lb_param"], states, drec, B, S, comm=scatter_late)
    scattered(SCATTER_LATE, got)
    grads["lb_param"] = dlb
    daq, dk2, dv2, dbias, dsink = _swa_bwd(proj, kk2, vv2, bias, sinks, datt, B, S)
    grads["attn_sinks"] = dsink
    grads["rel_bias"] = _small_mm("t5_dbias", dbias.reshape(N_Q_HEADS, -1), onehot.T.astype(BF16), "right")
    dak = (dk2[:, :128] + _swap_heads(dk2[:, 128:])).astype(BF16)
    dav = (dv2[:, :128] + _swap_heads(dv2[:, 128:])).astype(BF16)
    dproj = jnp.concatenate([daq, dak, dav, drq, drf, dri, drg, dga, dgb], axis=1)
    tk = _pick(T, 512, 128)
    w_in_shard = IN_W // N_CHIPS
    gw32, gw16 = _mm_tn("mix_dwin", (1, 2, T // tk),
                        (u, (tk, D_MODEL), lambda i, j, k: (k, 0)), (dproj, (tk, IN_W // 2), lambda i, j, k: (k, j)),
                        _grad_pair((D_MODEL, IN_W), (D_MODEL, IN_W // 2), lambda i, j, k: (0, j)))
    to_sh = lambda t: t.reshape(D_MODEL, N_CHIPS, w_in_shard).transpose(1, 0, 2)
    grads["w_in"] = (to_sh(gw32), to_sh(gw16))
    scatter_mix = scatter("mix", SCATTER_MIX)

    def dnorm_mix(accs, tv, cv):
        dh, dg = _rms_bwd_vals(accs[0], tv[0], cv[0])
        return [tv[1] + dh, dg]

    dh1, grads["norm_mix"] = _tile_call(
        "mix_dnorm", dnorm_mix, T, D_MODEL, _pick(T, 256), D_MODEL, pairs=[(dproj, 0, w["w_in"], "nt")],
        tiles=[(h1, 0), (dh2, 0)], consts=[g_mix], outs=[F32], parts=1)

    def scatter_last(dw_in, dw_out):
        grads["w_ffn1_in"], grads["w_ffn1_out"] = dw_in, dw_out
        return scatter("last", SCATTER_LAST)

    dx, grads["norm_ffn1"], _, _, got, got_last = _ffn_bwd(
        "ffn1b", dh1, x, g_ffn1, w["w_ffn1_in"], w["w_ffn1_out"], sv1, comm=scatter_mix, comm_last=scatter_last)
    scattered(SCATTER_MIX, got)
    scattered(SCATTER_LAST, got_last)
    return loss_p, dx, grads, part, from_chips


def _place():
    x, y, c = lax.axis_index("x"), lax.axis_index("y"), lax.axis_index("c")
    return x, y, c


def _other_chips(x, y):
    return [(1 - x, y, 2 * (1 - x) + y), (x, 1 - y, 2 * x + 1 - y), (1 - x, 1 - y, 2 * (1 - x) + 1 - y)]


def _half_rows(ref_3d, chip, h, rows):
    return ref_3d.at[chip, pl.ds(h * rows, rows), :]


def _run_comm(name, comm):
    nci, nco = len(comm.ins), len(comm.out_shapes)

    def body(*refs):
        cin, cout, send_sems, recv_sems = refs[:nci], refs[nci:nci + nco], refs[-2], refs[-1]
        comm.start(cin, cout, send_sems, recv_sems)
        comm.finish(cin, cout, send_sems, recv_sems)

    return pl.pallas_call(
        body, name=name, in_specs=[ANY] * nci, out_specs=[ANY] * nco, out_shape=list(comm.out_shapes),
        scratch_shapes=[pltpu.SemaphoreType.DMA((comm.n_sems,)), pltpu.SemaphoreType.DMA((comm.n_sems,))],
    )(*comm.ins)


def _gather_comm(ws):
    nw = len(ws)

    def parts(w_refs, out_refs, send_sems, recv_sems):
        x, y, c = _place()
        me = 2 * x + y
        chips = _other_chips(x, y)

        def copy(i, k, chip, h, to, src=None):
            half = ws[i].shape[0] // 2
            dst = _half_rows(out_refs[i], chip, h, half)
            return pltpu.make_async_remote_copy(
                src_ref=dst if src is None else src, dst_ref=dst,
                send_sem=send_sems.at[6 * i + k], recv_sem=recv_sems.at[6 * i + k], device_id=to, device_id_type=MESH)

        def first():
            out = []
            for i in range(nw):
                half = ws[i].shape[0] // 2
                out += [copy(i, j, me, c, (cx, cy, c), src=w_refs[i].at[pl.ds(c * half, half), :])
                        for j, (cx, cy, _) in enumerate(chips)]
            return out

        return copy, first, chips, c, (x, y, 1 - c)

    def start(*refs):
        _, first, _, _, _ = parts(*refs)
        for cp in first():
            cp.start()

    def finish(*refs):
        copy, first, chips, c, sibling = parts(*refs)
        passed = []
        for i in range(nw):
            for j, (cx, cy, ci) in enumerate(chips):
                copy(i, j, ci, c, (cx, cy, c)).wait_recv()
                fw = copy(i, 3 + j, ci, c, sibling)
                fw.start()
                passed.append(fw)
        for i in range(nw):
            for j, (_, _, ci) in enumerate(chips):
                copy(i, 3 + j, ci, 1 - c, sibling).wait_recv()
        for cp in first() + passed:
            cp.wait_send()

    return _Comm(list(ws), [jax.ShapeDtypeStruct((N_CHIPS,) + w.shape, w.dtype) for w in ws], 6 * nw, start, finish)


def _scatter_comm(ps):
    nw = len(ps)

    def copies(p_refs, out_refs, send_sems, recv_sems):
        x, y, c = _place()
        cps = []
        for i in range(nw):
            for j, (cx, cy, ci) in enumerate(_other_chips(x, y)):
                cps.append(pltpu.make_async_remote_copy(
                    src_ref=p_refs[i].at[ci], dst_ref=out_refs[i].at[j], send_sem=send_sems.at[3 * i + j],
                    recv_sem=recv_sems.at[3 * i + j], device_id=(cx, cy, c), device_id_type=MESH))
        return cps

    def start(*refs):
        for cp in copies(*refs):
            cp.start()

    def finish(*refs):
        for cp in copies(*refs):
            cp.wait()

    return _Comm(list(ps), [jax.ShapeDtypeStruct((3,) + p.shape[1:], p.dtype) for p in ps], 3 * nw, start, finish)


def _swap_halves(name, gs):
    nw = len(gs)

    def body(*refs):
        g_refs, out_refs, send_sems, recv_sems = refs[:nw], refs[nw:2 * nw], refs[2 * nw], refs[2 * nw + 1]
        x, y, c = _place()
        cps = []
        for i in range(nw):
            half = gs[i].shape[1] // 2
            cps.append(pltpu.make_async_remote_copy(
                src_ref=g_refs[i].at[:, pl.ds((1 - c) * half, half), :], dst_ref=out_refs[i],
                send_sem=send_sems.at[i], recv_sem=recv_sems.at[i], device_id=(x, y, 1 - c), device_id_type=MESH))
        for cp in cps:
            cp.start()
        for cp in cps:
            cp.wait()

    return pl.pallas_call(
        body, name=name, in_specs=[ANY] * nw, out_specs=[ANY] * nw,
        out_shape=[jax.ShapeDtypeStruct((N_CHIPS, g.shape[1] // 2, g.shape[2]), g.dtype) for g in gs],
        scratch_shapes=[pltpu.SemaphoreType.DMA((nw,)), pltpu.SemaphoreType.DMA((nw,))],
    )(*gs)


def _join_halves(name, ss):
    nw = len(ss)

    def body(*refs):
        s_refs, out_refs, send_sems, recv_sems = refs[:nw], refs[nw:2 * nw], refs[2 * nw], refs[2 * nw + 1]
        x, y, c = _place()
        cps = [pltpu.make_async_remote_copy(
            src_ref=s_refs[i], dst_ref=out_refs[i], send_sem=send_sems.at[i], recv_sem=recv_sems.at[i],
            device_id=(x, y, 1 - c), device_id_type=MESH) for i in range(nw)]
        for cp in cps:
            cp.start()
        for cp in cps:
            cp.wait()

    return pl.pallas_call(
        body, name=name, in_specs=[ANY] * nw, out_specs=[ANY] * nw,
        out_shape=[jax.ShapeDtypeStruct(s.shape, s.dtype) for s in ss],
        scratch_shapes=[pltpu.SemaphoreType.DMA((nw,)), pltpu.SemaphoreType.DMA((nw,))],
    )(*ss)


def _allreduce_small(sp):
    def body(s_ref, out_ref, slots, send_sems, recv_sems):
        x, y, c = _place()
        me = 4 * x + 2 * y + c
        slots[me] = s_ref[...]
        cps = []
        for r in range(1, N_DEV):
            px, py, pc = x ^ (r >> 2), y ^ ((r >> 1) & 1), c ^ (r & 1)
            cps.append(pltpu.make_async_remote_copy(
                src_ref=s_ref, dst_ref=slots.at[me], send_sem=send_sems.at[r - 1], recv_sem=recv_sems.at[r - 1],
                device_id=(px, py, pc), device_id_type=MESH))
        for cp in cps:
            cp.start()
        for r in range(1, N_DEV):
            px, py, pc = x ^ (r >> 2), y ^ ((r >> 1) & 1), c ^ (r & 1)
            pltpu.make_async_remote_copy(
                src_ref=s_ref, dst_ref=slots.at[4 * px + 2 * py + pc], send_sem=send_sems.at[r - 1],
                recv_sem=recv_sems.at[r - 1], device_id=(px, py, pc), device_id_type=MESH).wait_recv()
        for cp in cps:
            cp.wait_send()
        acc = slots[0]
        for d in range(1, N_DEV):
            acc = acc + slots[d]
        out_ref[...] = acc

    return pl.pallas_call(
        body, name="allreduce_small",
        in_specs=[pl.BlockSpec(memory_space=pltpu.VMEM)], out_specs=pl.BlockSpec(memory_space=pltpu.VMEM),
        out_shape=jax.ShapeDtypeStruct(sp.shape, F32),
        scratch_shapes=[pltpu.VMEM((N_DEV,) + sp.shape, F32), pltpu.SemaphoreType.DMA((N_DEV - 1,)),
                        pltpu.SemaphoreType.DMA((N_DEV - 1,))],
    )(sp)


def _scalar(v):
    return jnp.reshape(v, (1,)).astype(jnp.int32)


def _row_tile(h, dtype_mult=16):
    return _pick(h, 256, dtype_mult)


def _add_sibling(name, g32, from_sib, c):
    _, r, n = g32.shape
    h = r // 2
    th = _row_tile(h)
    nt = h // th

    def body(c_ref, g_ref, s_ref, o32_ref, o16_ref):
        s = g_ref[...] + s_ref[...].astype(F32)
        o32_ref[...] = s
        o16_ref[...] = s.astype(BF16)

    blk = (None, th, n)
    return pl.pallas_call(
        body, name=name,
        grid_spec=pltpu.PrefetchScalarGridSpec(
            num_scalar_prefetch=1, grid=(N_CHIPS, nt),
            in_specs=[pl.BlockSpec(blk, lambda k, t, c_ref: (k, c_ref[0] * nt + t, 0)),
                      pl.BlockSpec(blk, lambda k, t, c_ref: (k, t, 0))],
            out_specs=[pl.BlockSpec(blk, lambda k, t, c_ref: (k, t, 0))] * 2),
        out_shape=[jax.ShapeDtypeStruct((N_CHIPS, h, n), F32), jax.ShapeDtypeStruct((N_CHIPS, h, n), BF16)],
        compiler_params=_params("arbitrary", "arbitrary"))(_scalar(c), g32, from_sib)


def _add_chips(name, p32, from_chips, me_chip):
    _, h, n = p32.shape
    th = _row_tile(h)

    def body(m_ref, p_ref, a_ref, b_ref, c_ref, o_ref):
        o_ref[...] = p_ref[...] + a_ref[...].astype(F32) + b_ref[...].astype(F32) + c_ref[...].astype(F32)

    blk = (None, th, n)
    return pl.pallas_call(
        body, name=name,
        grid_spec=pltpu.PrefetchScalarGridSpec(
            num_scalar_prefetch=1, grid=(h // th,),
            in_specs=[pl.BlockSpec(blk, lambda t, m_ref: (m_ref[0], t, 0))]
            + [pl.BlockSpec(blk, lambda t, m_ref, j=j: (j, t, 0)) for j in range(3)],
            out_specs=pl.BlockSpec((th, n), lambda t, m_ref: (t, 0))),
        out_shape=jax.ShapeDtypeStruct((h, n), F32),
        compiler_params=_params("arbitrary"))(_scalar(me_chip), p32, from_chips, from_chips, from_chips)


def _adamw_vals(w, g, m, v):
    m = ADAM_B1 * m + (1.0 - ADAM_B1) * g
    v = ADAM_B2 * v + (1.0 - ADAM_B2) * (g * g)
    m_hat = m / (1.0 - ADAM_B1 ** ADAM_STEP)
    v_hat = v / (1.0 - ADAM_B2 ** ADAM_STEP)
    delta = -ADAM_LR * (m_hat / (jnp.sqrt(v_hat) + ADAM_EPS) + ADAM_WD * w)
    return delta, m, v


def _adamw_halves(name, w, m, v, g_mine, g_sib, c):
    r, n = w.shape
    h = r // 2
    th = _row_tile(h, 8)
    nt = h // th

    def body(c_ref, w_ref, m_ref, v_ref, a_ref, b_ref, g_ref, d_ref, nm_ref, nv_ref):
        mine = (pl.program_id(0) // nt) == c_ref[0]
        g = jnp.where(mine, a_ref[...], b_ref[...])
        d, nm, nv = _adamw_vals(w_ref[...], g, m_ref[...], v_ref[...])
        g_ref[...] = g
        d_ref[...] = d
        nm_ref[...] = nm
        nv_ref[...] = nv

    full = pl.BlockSpec((th, n), lambda t, c_ref: (t, 0))
    part = pl.BlockSpec((th, n), lambda t, c_ref: (t % nt, 0))
    return pl.pallas_call(
        body, name=name,
        grid_spec=pltpu.PrefetchScalarGridSpec(
            num_scalar_prefetch=1, grid=(2 * nt,), in_specs=[full, full, full, part, part], out_specs=[full] * 4),
        out_shape=[jax.ShapeDtypeStruct((r, n), F32)] * 4,
        compiler_params=_params("arbitrary"))(_scalar(c), w, m, v, g_mine, g_sib)


def _adamw(name, w, g, m, v):
    R, W = w.shape

    def fn(accs, tv, cv):
        return list(_adamw_vals(*tv))

    return _tile_call(name, fn, R, W, _pick(R, 256), W, tiles=[(w, 0), (g, 0), (m, 0), (v, 0)], outs=[F32] * 3)


SMALL_LAYOUT = (("rel_bias", 2, 256), ("lb_param", 8, 1024), ("norm_ffn1", 8, 1024), ("norm_mix", 8, 1024),
                ("attn_sinks", 1, 8), ("rec_norm", 1, 128), ("norm_ffn2", 8, 1024), ("norm_ple", 8, 1024),
                ("norm_final", 8, 1024), ("loss", 8, 1024))


def _pack_small(vals):
    rows = []
    for name, nrows, n in SMALL_LAYOUT:
        flat = vals[name].reshape(-1)
        flat = jnp.pad(flat, (0, nrows * 128 - n))
        rows.append(flat.reshape(nrows, 128))
    packed = jnp.concatenate(rows, axis=0)
    return jnp.pad(packed, ((0, SMALL_ROWS - packed.shape[0]), (0, 0)))


def _unpack_small(packed, shapes):
    out, r = {}, 0
    for name, nrows, n in SMALL_LAYOUT:
        out[name] = packed[r:r + nrows].reshape(-1)[:n].reshape(shapes[name])
        r += nrows
    return out


def _natural(name, s):
    if name in COL_SHARDED:
        return s.transpose(1, 0, 2).reshape(s.shape[1], -1)
    return s.reshape(-1, s.shape[2])


def kernel(x, p, rel_bias, lb_param, norm_ffn1, w_ffn1_in, w_ffn1_out, norm_mix, w_in, attn_sinks, rec_norm, w_att_proj, w_rec_proj, w_out, norm_ffn2, w_ffn2_in, w_ffn2_out, norm_ple, w_ple_gate, w_ple_proj, norm_final, loss_target, m_rel_bias, m_lb_param, m_norm_ffn1, m_w_ffn1_in, m_w_ffn1_out, m_norm_mix, m_w_in, m_attn_sinks, m_rec_norm, m_w_att_proj, m_w_rec_proj, m_w_out, m_norm_ffn2, m_w_ffn2_in, m_w_ffn2_out, m_norm_ple, m_w_ple_gate, m_w_ple_proj, m_norm_final, v_rel_bias, v_lb_param, v_norm_ffn1, v_w_ffn1_in, v_w_ffn1_out, v_norm_mix, v_w_in, v_attn_sinks, v_rec_norm, v_w_att_proj, v_w_rec_proj, v_w_out, v_norm_ffn2, v_w_ffn2_in, v_w_ffn2_out, v_norm_ple, v_w_ple_gate, v_w_ple_proj, v_norm_final):
    args = dict(locals())
    wsh = {n: args[n] for n in WEIGHTS}
    B, S = x.shape[0], x.shape[1]
    T = B * S
    cx, cy, cc = _place()
    me_chip = 2 * cx + cy

    mine16 = {n: wsh[n][0].astype(BF16) for n in BIG}
    loss_p, dx, grads, part, from_chips = _local_step(
        x.reshape(T, D_MODEL), p.reshape(T, PLE_DIM), loss_target.reshape(T, D_MODEL),
        {n: wsh[n] for n in SMALL}, mine16, cc, me_chip, B, S)

    s_mine = [_add_chips("rs_add_chips_" + n, part[n][0], from_chips[n], me_chip) for n in BIG]
    s_sib = _join_halves("rs_join", s_mine)

    small_vals = {
        "rel_bias": grads["rel_bias"].T,
        "lb_param": jnp.concatenate([_colsum("dlb_sum", grads["lb_param"]),
                                     -_colsum("dlb_sum2", grads["lb_param"])], axis=0) / 8.0,
        "attn_sinks": grads["attn_sinks"][:, 0],
        "rec_norm": _colsum("drn_sum", grads["rec_norm"]).reshape(REC_HEADS, REC_DIM).sum(axis=0),
        "loss": _colsum("loss_sum", loss_p),
    }
    for n in ("norm_ffn1", "norm_mix", "norm_ffn2", "norm_ple", "norm_final"):
        small_vals[n] = _colsum(n + "_sum", grads[n])
    red = _allreduce_small(_pack_small(small_vals))
    small_shapes = {n: wsh[n].shape for n in SMALL}
    small_shapes["loss"] = (D_MODEL,)
    small = _unpack_small(red, small_shapes)
    loss = 0.5 * jnp.sum(small["loss"]) / D_MODEL

    out_g, out_d, out_m, out_v = {}, {}, {}, {}
    for n, gm, gs in zip(BIG, s_mine, s_sib):
        res = _adamw_halves("adamw_" + n, wsh[n][0], args["m_" + n][0], args["v_" + n][0], gm, gs, cc)
        out_g[n], out_d[n], out_m[n], out_v[n] = (t[None] for t in res)
    sw = _pack_small({**{n: wsh[n] for n in SMALL}, "loss": jnp.zeros((D_MODEL,), F32)})
    sm = _pack_small({**{n: args["m_" + n] for n in SMALL}, "loss": jnp.zeros((D_MODEL,), F32)})
    sv = _pack_small({**{n: args["v_" + n] for n in SMALL}, "loss": jnp.ones((D_MODEL,), F32)})
    sd, snm, snv = _adamw("adamw_small", sw, red, sm, sv)
    ud, um, uv = (_unpack_small(t, small_shapes) for t in (sd, snm, snv))
    for n in SMALL:
        out_g[n], out_d[n], out_m[n], out_v[n] = small[n], ud[n], um[n], uv[n]

    return (loss, dx.reshape(B, S, D_MODEL), *[out_g[n] for n in WEIGHTS], *[out_d[n] for n in WEIGHTS],
            *[out_m[n] for n in WEIGHTS], *[out_v[n] for n in WEIGHTS])
```

```python
import numpy as np
import jax
import jax.numpy as jnp
from jax import lax
from jax.experimental import pallas as pl
from jax.experimental.pallas import tpu as pltpu

F32 = jnp.float32
BF16 = jnp.bfloat16
MESH = pl.DeviceIdType.MESH

D_MODEL = 1024
D_FF = 2816
FF_SHARD = 2 * D_FF // 4
HEAD_DIM = 64
N_Q_HEADS = 8
ATT_BLOCK = 128
N_BUCKETS = 32
MAX_DISTANCE = 128
REC_HEADS = 4
REC_DIM = 128
PLE_DIM = 256
EPS = 1e-6
IN_W = 4864
COL_AQ, COL_AK, COL_AV, COL_RQ, COL_RF, COL_RI, COL_RG, COL_GA, COL_GB = 0, 4, 5, 6, 10, 14, 18, 22, 30

CHUNK = 64
SUB = 8
N_SUB = CHUNK // SUB
HGRN_PAIR = 2

ADAM_LR, ADAM_B1, ADAM_B2, ADAM_EPS, ADAM_WD, ADAM_STEP = 0.001, 0.9, 0.999, 1e-08, 0.01, 10

V7X_VMEM_LIMIT = 56 * 1024 * 1024
N_CHIPS = 4
N_DEV = 8

BIG = ("w_ffn1_in", "w_ffn1_out", "w_in", "w_att_proj", "w_rec_proj", "w_out",
       "w_ffn2_in", "w_ffn2_out", "w_ple_gate", "w_ple_proj")
COL_SHARDED = ("w_ffn1_in", "w_in", "w_att_proj", "w_rec_proj", "w_ffn2_in", "w_ple_proj")
WEIGHTS = ("rel_bias", "lb_param", "norm_ffn1", "w_ffn1_in", "w_ffn1_out", "norm_mix", "w_in", "attn_sinks",
           "rec_norm", "w_att_proj", "w_rec_proj", "w_out", "norm_ffn2", "w_ffn2_in", "w_ffn2_out", "norm_ple",
           "w_ple_gate", "w_ple_proj", "norm_final")
SMALL = tuple(n for n in WEIGHTS if n not in BIG)
SMALL_ROWS = 64


def _params(*sem):
    return pltpu.CompilerParams(dimension_semantics=sem, vmem_limit_bytes=V7X_VMEM_LIMIT)


def _pick(n, cap, mult=8):
    if n <= cap:
        return n
    for t in range(cap - cap % mult, 0, -mult):
        if n % t == 0:
            return t
    raise ValueError((n, cap, mult))


def _dot(a, b):
    return jnp.dot(a, b, preferred_element_type=F32)


def _dot_nt(a, b):
    return lax.dot_general(a, b, (((1,), (1,)), ((), ())), preferred_element_type=F32)


def _dot_tn(a, b):
    return lax.dot_general(a, b, (((0,), (0,)), ((), ())), preferred_element_type=F32)


def _split3(x):
    hi = x.astype(BF16)
    r = x - hi.astype(F32)
    mid = r.astype(BF16)
    lo = (r - mid.astype(F32)).astype(BF16)
    return hi, mid, lo


def _split2(x):
    hi = x.astype(BF16)
    return hi, (x - hi.astype(F32)).astype(BF16)


def _sel_left(sel_bf16, x):
    hi, mid, lo = _split3(x)
    return _dot(sel_bf16, hi) + _dot(sel_bf16, mid) + _dot(sel_bf16, lo)


def _sel_right(x, sel_bf16):
    hi, mid, lo = _split3(x)
    return _dot(hi, sel_bf16) + _dot(mid, sel_bf16) + _dot(lo, sel_bf16)


def _sigmoid(x):
    return 1.0 / (1.0 + jnp.exp(-x))


def _group8(x):
    r, w = x.shape
    return x.reshape(r // 8, 8, w).sum(axis=0)


class _Comm:
    def __init__(self, ins, out_shapes, n_sems, start, finish):
        self.ins, self.out_shapes, self.n_sems, self.start, self.finish = ins, out_shapes, n_sems, start, finish


ANY = pl.BlockSpec(memory_space=pl.ANY)


def _comm_parts(comm):
    if comm is None:
        return [], [], [], []
    sems = [pltpu.SemaphoreType.DMA((comm.n_sems,)), pltpu.SemaphoreType.DMA((comm.n_sems,))]
    return list(comm.ins), [ANY] * len(comm.ins), list(comm.out_shapes), sems


def _comm_run(comm, grid, refs, n_in, n_out):
    if comm is None:
        return (lambda: None), (lambda: None)
    nci, nco = len(comm.ins), len(comm.out_shapes)
    cin = refs[n_in:n_in + nci]
    cout = refs[n_in + nci + n_out:n_in + nci + n_out + nco]
    send_sems, recv_sems = refs[-2], refs[-1]
    ids = [pl.program_id(d) for d in range(len(grid))]
    is_first = ids[0] == 0
    is_last = ids[0] == grid[0] - 1
    for d in range(1, len(grid)):
        is_first = is_first & (ids[d] == 0)
        is_last = is_last & (ids[d] == grid[d] - 1)

    def first():
        @pl.when(is_first)
        def _():
            comm.start(cin, cout, send_sems, recv_sems)

    def last():
        @pl.when(is_last)
        def _():
            comm.finish(cin, cout, send_sems, recv_sems)

    return first, last


def _call(name, fn, grid, ins, outs, pairs=(), comm=None):
    in_pair = {i for p in pairs for i in p[:2]}
    n_in, n_out = len(ins), len(outs)
    c_arrays, c_in_specs, c_out_shapes, c_sems = _comm_parts(comm)

    def body(*refs):
        first, last = _comm_run(comm, grid, refs, n_in, n_out)
        first()
        accs = []
        for ia, ib, kind in pairs:
            a, b = refs[ia][...].astype(BF16), refs[ib][...].astype(BF16)
            accs.append(_dot(a, b) if kind == "nn" else _dot_nt(a, b))
        vals = [refs[i][...] for i in range(n_in) if i not in in_pair]
        res = fn(accs, vals)
        out_refs = refs[n_in + len(c_arrays):n_in + len(c_arrays) + n_out]
        assert len(res) == len(out_refs), (name, len(res), len(out_refs))
        for o_ref, val in zip(out_refs, res):
            o_ref[...] = val.astype(o_ref.dtype)
        last()

    return pl.pallas_call(
        body, name=name, grid=grid,
        in_specs=[pl.BlockSpec(blk, im) for _, blk, im in ins] + c_in_specs,
        out_specs=[pl.BlockSpec(blk, im) for _, _, blk, im in outs] + [ANY] * len(c_out_shapes),
        out_shape=[jax.ShapeDtypeStruct(shp, dt) for shp, dt, _, _ in outs] + c_out_shapes,
        scratch_shapes=c_sems,
        compiler_params=_params(*(["arbitrary"] * len(grid))))(*[a for a, _, _ in ins], *c_arrays)


def _tile_call(name, fn, M, N, tm, tn, *, pairs=(), tiles=(), consts=(), outs=(), parts=0, comm=None):
    gi, gj = M // tm, N // tn
    assert gi * tm == M and gj * tn == N, (name, M, N, tm, tn)
    ins, prs = [], []
    for a, a_col, b, kind in pairs:
        K = b.shape[0] if kind == "nn" else b.shape[1]
        ins.append((a, (tm, K), lambda i, j, c=a_col: (i, c)))
        if kind == "nn":
            ins.append((b, (K, tn), lambda i, j: (0, j)))
        else:
            ins.append((b, (tn, K), lambda i, j: (j, 0)))
        prs.append((len(ins) - 2, len(ins) - 1, kind))
    for arr, off in tiles:
        ins.append((arr, (tm, tn), lambda i, j, o=off: (i, j + o)))
    for arr in consts:
        ins.append((arr, arr.shape, lambda i, j: (0, 0)))
    out_l = [((M, N), dt, (tm, tn), lambda i, j: (i, j)) for dt in outs]
    out_l += [((gi * 8, N), F32, (8, tn), lambda i, j: (i, j))] * parts
    nt = len(tiles)

    def wrapped(accs, vals):
        return fn(accs, vals[:nt], vals[nt:])

    return _call(name, wrapped, (gi, gj), ins, out_l, prs, comm=comm)


def _mm_tn(name, grid, a_in, b_in, outs):
    nk = grid[2]
    tm = [d for d in a_in[1] if d is not None][1]
    tn = [d for d in b_in[1] if d is not None][1]

    def body(a_ref, b_ref, *rest):
        out_refs, acc_ref = rest[:-1], rest[-1]
        k = pl.program_id(2)

        @pl.when(k == 0)
        def _():
            acc_ref[...] = jnp.zeros_like(acc_ref)

        acc_ref[...] += _dot_tn(a_ref[...].astype(BF16), b_ref[...].astype(BF16))

        @pl.when(k == nk - 1)
        def _():
            for o_ref in out_refs:
                o_ref[...] = acc_ref[...].astype(o_ref.dtype)

    return pl.pallas_call(
        body, name=name, grid=grid,
        in_specs=[pl.BlockSpec(a_in[1], a_in[2]), pl.BlockSpec(b_in[1], b_in[2])],
        out_specs=[pl.BlockSpec(blk, im) for _, _, blk, im in outs],
        out_shape=[jax.ShapeDtypeStruct(shp, dt) for shp, dt, _, _ in outs],
        scratch_shapes=[pltpu.VMEM((tm, tn), F32)],
        compiler_params=_params("arbitrary", "arbitrary", "arbitrary"))(a_in[0], b_in[0])


def _grad_pair(shape, block, imap):
    return [(shape, F32, block, imap), (shape, BF16, block, imap)]


def _mm_tn_rows(name, a, b, tk=1024):
    T, a_w = a.shape
    b_w = b.shape[1]
    tm = _pick(a_w, 1408, 128)
    tk = _pick(T, tk, 128)
    g32, g16 = _mm_tn(name, (a_w // tm, 1, T // tk),
                      (a, (tk, tm), lambda i, j, k: (k, i)), (b, (tk, b_w), lambda i, j, k: (k, 0)),
                      _grad_pair((a_w, b_w), (tm, b_w), lambda i, j, k: (i, 0)))
    shp = (N_CHIPS, a_w // N_CHIPS, b_w)
    return g32.reshape(shp), g16.reshape(shp)


def _mm_tn_cols(name, a, b, tk=1024):
    T, a_w = a.shape
    n = b.shape[1] // N_CHIPS
    tk = _pick(T, tk, 128)
    return _mm_tn(name, (1, N_CHIPS, T // tk),
                  (a, (tk, a_w), lambda i, j, k: (k, 0)), (b, (tk, n), lambda i, j, k: (k, j)),
                  _grad_pair((N_CHIPS, a_w, n), (None, a_w, n), lambda i, j, k: (j, 0, 0)))


def _colsum(name, x):
    def body(x_ref, o_ref):
        o_ref[...] = jnp.sum(x_ref[...], axis=0, keepdims=True)
    return pl.pallas_call(body, name=name, out_shape=jax.ShapeDtypeStruct((1, x.shape[1]), F32))(x)


def _rms_hat(h):
    return h * lax.rsqrt(jnp.mean(h * h, axis=-1, keepdims=True) + EPS)


def _rms_bwd_vals(dn, h, g):
    r = lax.rsqrt(jnp.mean(h * h, axis=-1, keepdims=True) + EPS)
    nh = h * r
    gd = dn * g
    dh = r * (gd - nh * jnp.mean(gd * nh, axis=-1, keepdims=True))
    return dh, _group8(dn * nh)


def _rms_fwd(name, h, g, tm=512):
    T = h.shape[0]

    def fn(accs, tv, cv):
        return [_rms_hat(tv[0]) * cv[0]]

    return _tile_call(name, fn, T, D_MODEL, _pick(T, tm), D_MODEL, tiles=[(h, 0)], consts=[g], outs=[BF16])[0]


def _ffn_fwd(tag, h, g, w_in, w_out, comm_in=None, comm_out=None, w_out_of=None):
    T = h.shape[0]
    n = _rms_fwd(tag + "_norm", h, g)
    tm = _pick(T, 512)
    wblk = (None, D_MODEL, FF_SHARD)

    def act(accs, vals):
        gate, up = accs
        return [gate, up, gate * _sigmoid(gate) * up]

    tile = lambda: ((T, D_FF), BF16, (tm, FF_SHARD), lambda i, j: (i, j))
    gate, up, a, *got_in = _call(
        tag + "_in", act, (T // tm, 2),
        [(n, (tm, D_MODEL), lambda i, j: (i, 0)),
         (w_in, wblk, lambda i, j: (j, 0, 0)), (w_in, wblk, lambda i, j: (j + 2, 0, 0))],
        [tile(), tile(), tile()], pairs=[(0, 1, "nn"), (0, 2, "nn")], comm=comm_in)

    def res(accs, tv, cv):
        return [tv[0] + 0.5 * accs[0]]

    if w_out_of is not None:
        w_out = w_out_of(got_in)
    h_new, *got_out = _tile_call(tag + "_out", res, T, D_MODEL, _pick(T, 512), 512,
                                 pairs=[(a, 0, w_out, "nn")], tiles=[(h, 0)], outs=[F32], comm=comm_out)
    return h_new, (n, gate, up, a), got_out


def _ffn_bwd(tag, dh_out, h, g, w_in, w_out, saved, comm=None, comm_last=None):
    T = h.shape[0]
    n, gate, up, a = saved
    tm = _pick(T, 512)

    def half(accs, tv, cv):
        return [0.5 * tv[0]]

    df = _tile_call(tag + "_df", half, T, D_MODEL, _pick(T, 512), D_MODEL, tiles=[(dh_out, 0)], outs=[BF16])[0]

    def dact(accs, vals):
        da = accs[0]
        gt, u = vals[0].astype(F32), vals[1].astype(F32)
        sg = _sigmoid(gt)
        silu = gt * sg
        return [jnp.stack([da * u * (sg + silu * (1.0 - sg)), da * silu])]

    dz, *got = _call(
        tag + "_dact", dact, (T // tm, 2),
        [(df, (tm, D_MODEL), lambda i, j: (i, 0)), (w_out, (FF_SHARD, D_MODEL), lambda i, j: (j, 0)),
         (gate, (tm, FF_SHARD), lambda i, j: (i, j)), (up, (tm, FF_SHARD), lambda i, j: (i, j))],
        [((2, T, D_FF), BF16, (2, tm, FF_SHARD), lambda i, j: (0, i, j))], pairs=[(0, 1, "nt")], comm=comm)
    dw_out = _mm_tn_rows(tag + "_dwout", a, df)
    tk = _pick(T, 1024, 128)
    dw_in = _mm_tn(tag + "_dwin", (1, N_CHIPS, T // tk),
                   (n, (tk, D_MODEL), lambda i, j, k: (k, 0)),
                   (dz, (None, tk, FF_SHARD), lambda i, j, k: (j // 2, k, j % 2)),
                   _grad_pair((N_CHIPS, D_MODEL, FF_SHARD), (None, D_MODEL, FF_SHARD), lambda i, j, k: (j, 0, 0)))

    def dnorm(accs, vals):
        dn = accs[0] + accs[1] + accs[2] + accs[3]
        dh, dg = _rms_bwd_vals(dn, vals[0], vals[2])
        return [vals[1] + dh, dg]

    tm2 = _pick(T, 256)
    ins = [(dz, (None, tm2, FF_SHARD), lambda i, j, s=s: (s // 2, i, s % 2)) for s in range(N_CHIPS)]
    ins += [(w_in, (None, D_MODEL, FF_SHARD), lambda i, j, s=s: (s, 0, 0)) for s in range(N_CHIPS)]
    ins += [(h, (tm2, D_MODEL), lambda i, j: (i, 0)), (dh_out, (tm2, D_MODEL), lambda i, j: (i, 0)),
            (g, g.shape, lambda i, j: (0, 0))]
    dh, dg, *got_last = _call(
        tag + "_dnorm", dnorm, (T // tm2, 1), ins,
        [((T, D_MODEL), F32, (tm2, D_MODEL), lambda i, j: (i, 0)),
         ((T // tm2 * 8, D_MODEL), F32, (8, D_MODEL), lambda i, j: (i, 0))],
        pairs=[(s, N_CHIPS + s, "nt") for s in range(N_CHIPS)],
        comm=None if comm_last is None else comm_last(dw_in, dw_out))
    return dh, dg, dw_in, dw_out, got, got_last


def _t5_onehot():
    qi = np.arange(ATT_BLOCK)[:, None] + ATT_BLOCK
    kj = np.arange(2 * ATT_BLOCK)[None, :]
    nn = np.maximum(qi - kj, 0)
    max_exact = N_BUCKETS // 2
    large = max_exact + (np.log(np.maximum(nn, 1) / max_exact) / np.log(MAX_DISTANCE / max_exact)
                         * (N_BUCKETS - max_exact)).astype(np.int32)
    large = np.minimum(large, N_BUCKETS - 1)
    bucket = np.where(nn < max_exact, nn, large).astype(np.int32).reshape(-1)
    return (bucket[None, :] == np.arange(N_BUCKETS)[:, None]).astype(np.float32)


def _small_mm(name, a, b, sel):
    def body(a_ref, b_ref, o_ref):
        if sel == "right":
            o_ref[...] = _sel_right(a_ref[...], b_ref[...])
        else:
            o_ref[...] = _sel_left(a_ref[...], b_ref[...])
    return pl.pallas_call(body, name=name, out_shape=jax.ShapeDtypeStruct((a.shape[0], b.shape[1]), F32),
                          compiler_params=pltpu.CompilerParams(vmem_limit_bytes=V7X_VMEM_LIMIT))(a, b)


def _swap_heads(t):
    return jnp.concatenate([t[:, HEAD_DIM:], t[:, :HEAD_DIM]], axis=1)


def _kv_layouts(proj):
    T = proj.shape[0]

    def fn(accs, tv, cv):
        return [tv[0], tv[1]]

    k, v = _tile_call("kv_cast", fn, T, 128, _pick(T, 1024), 128, tiles=[(proj, COL_AK), (proj, COL_AV)],
                      outs=[BF16, BF16])
    return jnp.concatenate([k, _swap_heads(k)], axis=1), jnp.concatenate([v, _swap_heads(v)], axis=1)


def _swa_masks():
    row = lax.broadcasted_iota(jnp.int32, (ATT_BLOCK, 2 * ATT_BLOCK), 0)
    col = lax.broadcasted_iota(jnp.int32, (ATT_BLOCK, 2 * ATT_BLOCK), 1)
    dist = ATT_BLOCK + row - col
    return (dist >= 0) & (dist < ATT_BLOCK), col


def _swa_heads():
    out = []
    for h in range(N_Q_HEADS):
        lo = h % 2 == 0
        hk = h // 4
        swapped = (hk == 1) if lo else (hk == 0)
        out.append((h // 2, lo, swapped))
    return out


def _swa_probs(qm, kk, bias_h, sink, valid):
    s = _dot_nt(qm, kk) * (HEAD_DIM ** -0.5) + bias_h
    s = jnp.where(valid, s, -jnp.inf)
    m = jnp.maximum(jnp.max(s, axis=-1, keepdims=True), sink)
    e = jnp.exp(s - m)
    es = jnp.exp(sink - m)
    den = jnp.sum(e, axis=-1, keepdims=True) + es
    return e / den, es / den


def _swa_fwd(proj, kk2, vv2, bias, sinks, B, S):
    T = B * S
    nb = S // ATT_BLOCK

    def body(q_ref, k_ref, v_ref, bias_ref, sink_ref, o_ref, kpad, vpad):
        zeros = jnp.zeros((ATT_BLOCK, 256), BF16)
        kpad[pl.ds(0, ATT_BLOCK), :] = zeros
        vpad[pl.ds(0, ATT_BLOCK), :] = zeros
        kpad[pl.ds(ATT_BLOCK, S), :] = k_ref[...]
        vpad[pl.ds(ATT_BLOCK, S), :] = v_ref[...]
        valid0, col = _swa_masks()
        lane = lax.broadcasted_iota(jnp.int32, (1, 128), 1)
        lo_q = lane < HEAD_DIM
        heads = _swa_heads()

        def blk(n, carry):
            r0 = pl.multiple_of(n * ATT_BLOCK, ATT_BLOCK)
            valid = valid0 & ((n > 0) | (col >= ATT_BLOCK))
            kb = kpad[pl.ds(r0, 2 * ATT_BLOCK), :]
            vb = vpad[pl.ds(r0, 2 * ATT_BLOCK), :]
            for j in range(N_Q_HEADS // 2):
                qblk = q_ref[pl.ds(r0, ATT_BLOCK), pl.ds(128 * j, 128)].astype(BF16)
                acc = jnp.zeros((ATT_BLOCK, 128), F32)
                for h in (2 * j, 2 * j + 1):
                    _, lo, swapped = heads[h]
                    keep = lo_q if lo else ~lo_q
                    qm = jnp.where(keep, qblk, jnp.zeros_like(qblk))
                    kk = kb[:, 128:] if swapped else kb[:, :128]
                    vv = vb[:, 128:] if swapped else vb[:, :128]
                    vm = jnp.where(keep, vv, jnp.zeros_like(vv))
                    p, _ = _swa_probs(qm, kk, bias_ref[h], sink_ref[h], valid)
                    acc = acc + _dot(p.astype(BF16), vm)
                o_ref[pl.ds(r0, ATT_BLOCK), pl.ds(128 * j, 128)] = acc.astype(o_ref.dtype)
            return carry

        lax.fori_loop(0, nb, blk, 0)

    return pl.pallas_call(
        body, name="swa_fwd", grid=(B,),
        in_specs=[pl.BlockSpec((S, 512), lambda b: (b, 0)),
                  pl.BlockSpec((S, 256), lambda b: (b, 0)),
                  pl.BlockSpec((S, 256), lambda b: (b, 0)),
                  pl.BlockSpec((N_Q_HEADS, ATT_BLOCK, 2 * ATT_BLOCK), lambda b: (0, 0, 0)),
                  pl.BlockSpec(memory_space=pltpu.SMEM)],
        out_specs=pl.BlockSpec((S, 512), lambda b: (b, 0)),
        out_shape=jax.ShapeDtypeStruct((T, 512), BF16),
        scratch_shapes=[pltpu.VMEM((S + ATT_BLOCK, 256), BF16), pltpu.VMEM((S + ATT_BLOCK, 256), BF16)],
        compiler_params=_params("arbitrary"))(proj, kk2, vv2, bias, sinks)


def _swa_bwd(proj, kk2, vv2, bias, sinks, datt, B, S):
    T = B * S
    nb = S // ATT_BLOCK

    def body(q_ref, k_ref, v_ref, bias_ref, sink_ref, do_ref, dq_ref, dk_ref, dv_ref, dbias_ref, dsink_ref,
             kpad, vpad, dkpad, dvpad):
        b = pl.program_id(0)

        @pl.when(b == 0)
        def _():
            dbias_ref[...] = jnp.zeros_like(dbias_ref)
            dsink_ref[...] = jnp.zeros_like(dsink_ref)

        zeros = jnp.zeros((ATT_BLOCK, 256), BF16)
        kpad[pl.ds(0, ATT_BLOCK), :] = zeros
        vpad[pl.ds(0, ATT_BLOCK), :] = zeros
        kpad[pl.ds(ATT_BLOCK, S), :] = k_ref[...]
        vpad[pl.ds(ATT_BLOCK, S), :] = v_ref[...]
        dkpad[...] = jnp.zeros_like(dkpad)
        dvpad[...] = jnp.zeros_like(dvpad)
        valid0, col = _swa_masks()
        lane = lax.broadcasted_iota(jnp.int32, (1, 128), 1)
        lo_q = lane < HEAD_DIM
        heads = _swa_heads()
        scale = HEAD_DIM ** -0.5

        def blk(n, carry):
            r0 = pl.multiple_of(n * ATT_BLOCK, ATT_BLOCK)
            valid = valid0 & ((n > 0) | (col >= ATT_BLOCK))
            kb = kpad[pl.ds(r0, 2 * ATT_BLOCK), :]
            vb = vpad[pl.ds(r0, 2 * ATT_BLOCK), :]
            dk_acc = [jnp.zeros((2 * ATT_BLOCK, 128), F32), jnp.zeros((2 * ATT_BLOCK, 128), F32)]
            dv_acc = [jnp.zeros((2 * ATT_BLOCK, 128), F32), jnp.zeros((2 * ATT_BLOCK, 128), F32)]
            for j in range(N_Q_HEADS // 2):
                qblk = q_ref[pl.ds(r0, ATT_BLOCK), pl.ds(128 * j, 128)].astype(BF16)
                doblk = do_ref[pl.ds(r0, ATT_BLOCK), pl.ds(128 * j, 128)]
                dq = jnp.zeros((ATT_BLOCK, 128), F32)
                for h in (2 * j, 2 * j + 1):
                    _, lo, swapped = heads[h]
                    keep = lo_q if lo else ~lo_q
                    qm = jnp.where(keep, qblk, jnp.zeros_like(qblk))
                    dom = jnp.where(keep, doblk, jnp.zeros_like(doblk))
                    kk = kb[:, 128:] if swapped else kb[:, :128]
                    vv = vb[:, 128:] if swapped else vb[:, :128]
                    km = jnp.where(keep, kk, jnp.zeros_like(kk))
                    p, ps = _swa_probs(qm, kk, bias_ref[h], sink_ref[h], valid)
                    dp = _dot_nt(dom, vv)
                    delta = jnp.sum(p * dp, axis=-1, keepdims=True)
                    ds = p * (dp - delta)
                    dbias_ref[h] += ds
                    dsink_ref[pl.ds(h, 1), :] += -jnp.sum(jnp.broadcast_to(ps * delta, (ATT_BLOCK, 128)),
                                                          axis=0, keepdims=True)
                    dsb = (ds * scale).astype(BF16)
                    dq = dq + _dot(dsb, km)
                    idx = 1 if swapped else 0
                    dk_acc[idx] = dk_acc[idx] + _dot_tn(dsb, qm)
                    dv_acc[idx] = dv_acc[idx] + _dot_tn(p.astype(BF16), dom)
                dq_ref[pl.ds(r0, ATT_BLOCK), pl.ds(128 * j, 128)] = dq.astype(dq_ref.dtype)
            dkpad[pl.ds(r0, 2 * ATT_BLOCK), :] += jnp.concatenate(dk_acc, axis=1)
            dvpad[pl.ds(r0, 2 * ATT_BLOCK), :] += jnp.concatenate(dv_acc, axis=1)
            return carry

        lax.fori_loop(0, nb, blk, 0)
        dk_ref[...] = dkpad[pl.ds(ATT_BLOCK, S), :]
        dv_ref[...] = dvpad[pl.ds(ATT_BLOCK, S), :]

    return pl.pallas_call(
        body, name="swa_bwd", grid=(B,),
        in_specs=[pl.BlockSpec((S, 512), lambda b: (b, 0)),
                  pl.BlockSpec((S, 256), lambda b: (b, 0)),
                  pl.BlockSpec((S, 256), lambda b: (b, 0)),
                  pl.BlockSpec((N_Q_HEADS, ATT_BLOCK, 2 * ATT_BLOCK), lambda b: (0, 0, 0)),
                  pl.BlockSpec(memory_space=pltpu.SMEM),
                  pl.BlockSpec((S, 512), lambda b: (b, 0))],
        out_specs=[pl.BlockSpec((S, 512), lambda b: (b, 0)),
                   pl.BlockSpec((S, 256), lambda b: (b, 0)),
                   pl.BlockSpec((S, 256), lambda b: (b, 0)),
                   pl.BlockSpec((N_Q_HEADS, ATT_BLOCK, 2 * ATT_BLOCK), lambda b: (0, 0, 0)),
                   pl.BlockSpec((N_Q_HEADS, 128), lambda b: (0, 0))],
        out_shape=[jax.ShapeDtypeStruct((T, 512), BF16),
                   jax.ShapeDtypeStruct((T, 256), F32),
                   jax.ShapeDtypeStruct((T, 256), F32),
                   jax.ShapeDtypeStruct((N_Q_HEADS, ATT_BLOCK, 2 * ATT_BLOCK), F32),
                   jax.ShapeDtypeStruct((N_Q_HEADS, 128), F32)],
        scratch_shapes=[pltpu.VMEM((S + ATT_BLOCK, 256), BF16), pltpu.VMEM((S + ATT_BLOCK, 256), BF16),
                        pltpu.VMEM((S + ATT_BLOCK, 256), F32), pltpu.VMEM((S + ATT_BLOCK, 256), F32)],
        compiler_params=_params("arbitrary"))(proj, kk2, vv2, bias, sinks, datt)


def _hgrn_gates(z, lb):
    sg = _sigmoid(z)
    f = lb + (1.0 - lb) * sg
    return sg, f, jnp.log(f), 1.0 - f


def _hgrn_consts():
    r = lax.broadcasted_iota(jnp.int32, (CHUNK, CHUNK), 0)
    c = lax.broadcasted_iota(jnp.int32, (CHUNK, CHUNK), 1)
    tril = (r >= c).astype(BF16)
    triu = (r <= c).astype(BF16)
    causal = r >= c
    below = (r // SUB) > (c // SUB)
    inside = ((r // SUB) == (c // SUB)) & causal
    return tril, triu, causal, below, inside, c


def _block_rows(ref, lanes, s):
    rows = []
    for i in range(N_SUB):
        if SUB * i + s < 0:
            rows.append(jnp.zeros((SUB, REC_DIM), F32))
        else:
            rows.append(jnp.broadcast_to(ref[pl.ds(SUB * i + s, 1), lanes], (SUB, REC_DIM)))
    return jnp.concatenate(rows, axis=0)


def _hgrn_offdiag(q, k, bcum, b_ref, lanes):
    eq = jnp.exp(jnp.minimum(bcum - _block_rows(b_ref, lanes, -1), 0.0))
    qe = q * eq
    zero = jnp.zeros((SUB, REC_DIM), F32)
    q_rows, k_cols, eks = [jnp.zeros((SUB, (N_SUB - 1) * REC_DIM), F32)], [], []
    for i in range(1, N_SUB):
        q_rows.append(jnp.concatenate([zero] * (i - 1) + [qe[SUB * i:SUB * (i + 1), :]] + [zero] * (N_SUB - 1 - i),
                                      axis=1))
        p = b_ref[pl.ds(SUB * i - 1, 1), lanes]
        pad = jnp.zeros((CHUNK - SUB * i, REC_DIM), F32)
        ek = jnp.concatenate([jnp.exp(p - b_ref[pl.ds(0, SUB * i), lanes]), pad], axis=0)
        k_cols.append(k * ek)
        eks.append(ek)
    return jnp.concatenate(q_rows, axis=0), jnp.concatenate(k_cols, axis=1), eq, eks


def _hgrn_fwd(proj, lb_param, B, S):
    T = B * S
    nc = S // CHUNK

    def body(q_ref, z_ref, v_ref, lb_ref, o_ref, st_ref, k_s, b_s):
        tril, _, _, below, inside, col = _hgrn_consts()
        col_s = col & (SUB - 1)

        def chunk(ci, hts):
            r0 = pl.multiple_of(ci * CHUNK, CHUNK)
            lb = _sigmoid(lb_ref[0:1, :] - lb_ref[1:2, :])
            _, _, g_all, k_all = _hgrn_gates(z_ref[pl.ds(r0, CHUNK), :], lb)
            b_all = _sel_left(tril, g_all)
            k_s[...] = k_all
            b_s[...] = b_all
            new = []
            for e, ht in enumerate(hts):
                lanes = pl.ds(REC_DIM * e, REC_DIM)
                cols = slice(REC_DIM * e, REC_DIM * (e + 1))
                q = q_ref[pl.ds(r0, CHUNK), lanes]
                v = v_ref[pl.ds(r0, CHUNK), lanes]
                k, bcum = k_all[:, cols], b_all[:, cols]
                st_ref[e * nc + ci] = ht
                qst, kst, _, _ = _hgrn_offdiag(q, k, bcum, b_s, lanes)
                d = jnp.zeros((CHUNK, CHUNK), F32)
                for s in range(SUB):
                    w = jnp.exp(jnp.minimum(bcum - _block_rows(b_s, lanes, s), 0.0))
                    colv = jnp.sum(q * _block_rows(k_s, lanes, s) * w, axis=-1, keepdims=True)
                    d = jnp.where(col_s == s, colv, d)
                a = jnp.where(below, _dot_nt(qst.astype(BF16), kst.astype(BF16)), 0.0) + jnp.where(inside, d, 0.0)
                vb = v.astype(BF16)
                qb = (q * jnp.exp(bcum)).astype(BF16)
                o_ref[pl.ds(r0, CHUNK), lanes] = _dot(a.astype(BF16), vb) + _dot_nt(qb, ht.astype(BF16))
                b_last = b_s[pl.ds(CHUNK - 1, 1), lanes]
                kb = (k * jnp.exp(b_last - bcum)).astype(BF16)
                new.append(ht * jnp.exp(b_last) + _dot_tn(vb, kb))
            return tuple(new)

        lax.fori_loop(0, nc, chunk, tuple(jnp.zeros((REC_DIM, REC_DIM), F32) for _ in range(HGRN_PAIR)))

    hp, wd = REC_HEADS // HGRN_PAIR, HGRN_PAIR * REC_DIM
    cq, cf, ci_ = (c * REC_DIM // wd for c in (COL_RQ, COL_RF, COL_RI))
    return pl.pallas_call(
        body, name="hgrn_fwd", grid=(B, hp),
        in_specs=[pl.BlockSpec((S, wd), lambda b, h: (b, cq + h)),
                  pl.BlockSpec((S, wd), lambda b, h: (b, cf + h)),
                  pl.BlockSpec((S, wd), lambda b, h: (b, ci_ + h)),
                  pl.BlockSpec((2, wd), lambda b, h: (0, h))],
        out_specs=[pl.BlockSpec((S, wd), lambda b, h: (b, h)),
                   pl.BlockSpec((HGRN_PAIR * nc, REC_DIM, REC_DIM), lambda b, h: (b * hp + h, 0, 0))],
        out_shape=[jax.ShapeDtypeStruct((T, 512), F32),
                   jax.ShapeDtypeStruct((B * REC_HEADS * nc, REC_DIM, REC_DIM), F32)],
        scratch_shapes=[pltpu.VMEM((CHUNK, wd), F32), pltpu.VMEM((CHUNK, wd), F32)],
        compiler_params=_params("arbitrary", "arbitrary"))(proj, proj, proj, lb_param)


def _hgrn_bwd(proj, lb_param, states, do, B, S, comm=None):
    T = B * S
    nc = S // CHUNK

    c_arrays, c_in_specs, c_out_shapes, c_sems = _comm_parts(comm)
    nci, nco = len(c_arrays), len(c_out_shapes)

    def body(*refs):
        q_ref, z_ref, v_ref, lb_ref, st_ref, do_ref = refs[:6]
        dq_ref, dz_ref, dv_ref, dlb_ref = refs[6 + nci:10 + nci]
        k_s, b_s = refs[10 + nci + nco:12 + nci + nco]
        comm_first, comm_last = _comm_run(comm, (B, REC_HEADS // HGRN_PAIR), refs, 6, 4)
        comm_first()
        tril, triu, causal, below, inside, col = _hgrn_consts()
        col_s = col & (SUB - 1)
        last_row = lax.broadcasted_iota(jnp.int32, (CHUNK, 1), 0) == CHUNK - 1
        rc = lax.broadcasted_iota(jnp.int32, (CHUNK, SUB * REC_DIM), 0)
        lc = lax.broadcasted_iota(jnp.int32, (CHUNK, SUB * REC_DIM), 1)
        spread = ((rc & (SUB - 1)) == (lc // REC_DIM)).astype(BF16)
        rr = lax.broadcasted_iota(jnp.int32, (CHUNK, SUB * CHUNK), 0)
        cc = lax.broadcasted_iota(jnp.int32, (CHUNK, SUB * CHUNK), 1)
        gather = (((rr // SUB) == ((cc & (CHUNK - 1)) // SUB)) & ((rr & (SUB - 1)) == (cc // CHUNK))).astype(BF16)

        heads = range(HGRN_PAIR)
        cols = [slice(REC_DIM * e, REC_DIM * (e + 1)) for e in heads]
        lanes = [pl.ds(REC_DIM * e, REC_DIM) for e in heads]
        lane_cat = lambda vals: jnp.concatenate(vals, axis=1)
        row_cat = lambda vals: jnp.concatenate(vals, axis=0)

        def chunk(it, carry):
            dhts, dlb = carry
            ci = nc - 1 - it
            r0 = pl.multiple_of(ci * CHUNK, CHUNK)
            rows = pl.ds(r0, CHUNK)
            lb = _sigmoid(lb_ref[0:1, :] - lb_ref[1:2, :])
            sg, f, g_all, k_all = _hgrn_gates(z_ref[rows, :], lb)
            b_all = _sel_left(tril, g_all)
            k_s[...] = k_all
            b_s[...] = b_all
            q_all = q_ref[rows, :]
            das, hd = [], []
            for e in heads:
                vb, dob = v_ref[rows, lanes[e]].astype(BF16), do_ref[rows, lanes[e]].astype(BF16)
                da = jnp.where(causal, _dot_nt(dob, vb), 0.0)
                das.append(jnp.where(inside, da, 0.0))
                hd.append((vb, dob, da))
            da_hi, da_lo = _split2(row_cat(das))
            da_in = _dot(da_hi, spread) + _dot(da_lo, spread)
            ds, dqs, pieces, pieces_lo = [], [], [[] for _ in range(SUB)], [[] for _ in range(SUB)]
            for e in heads:
                q, bcum = q_all[:, cols[e]], b_all[:, cols[e]]
                d = jnp.zeros((CHUNK, CHUNK), F32)
                dq = jnp.zeros((CHUNK, REC_DIM), F32)
                for s in range(SUB):
                    w = jnp.exp(jnp.minimum(bcum - _block_rows(b_s, lanes[e], s), 0.0))
                    ks = _block_rows(k_s, lanes[e], s)
                    qw = q * w
                    d = jnp.where(col_s == s, jnp.sum(qw * ks, axis=-1, keepdims=True), d)
                    da_s = da_in[CHUNK * e:CHUNK * (e + 1), REC_DIM * s:REC_DIM * (s + 1)]
                    dq = dq + da_s * ks * w
                    hi, lo = _split2(da_s * qw)
                    pieces[s].append(hi)
                    pieces_lo[s].append(lo)
                ds.append(d)
                dqs.append(dq)
            dk_in = (_dot(gather, row_cat([lane_cat(p) for p in pieces]))
                     + _dot(gather, row_cat([lane_cat(p) for p in pieces_lo])))
            dq_out, dk_out, dv_out, db_out, new_dhts = [], [], [], [], []
            for e in heads:
                q, k, bcum = q_all[:, cols[e]], k_all[:, cols[e]], b_all[:, cols[e]]
                vb, dob, da = hd[e]
                dht, ht = dhts[e], st_ref[e * nc + ci]
                qst, kst, eq, eks = _hgrn_offdiag(q, k, bcum, b_s, lanes[e])
                qst_b, kst_b = qst.astype(BF16), kst.astype(BF16)
                a = jnp.where(below, _dot_nt(qst_b, kst_b), 0.0) + jnp.where(inside, ds[e], 0.0)
                da_off = jnp.where(below, da, 0.0).astype(BF16)
                dqst = _dot(da_off, kst_b)
                dkst = _dot_tn(da_off, qst_b)
                dk = dk_in[:, cols[e]]
                dq_rows = [jnp.zeros((SUB, REC_DIM), F32)]
                for i in range(1, N_SUB):
                    dq_rows.append(dqst[SUB * i:SUB * (i + 1), REC_DIM * (i - 1):REC_DIM * i])
                    dk = dk + dkst[:, REC_DIM * (i - 1):REC_DIM * i] * eks[i - 1]
                dq = dqs[e] + row_cat(dq_rows) * eq
                eb = jnp.exp(bcum)
                b_last = b_s[pl.ds(CHUNK - 1, 1), lanes[e]]
                el = jnp.exp(b_last)
                ekb = jnp.exp(b_last - bcum)
                qb = (q * eb).astype(BF16)
                kb = k * ekb
                dhb = dht.astype(BF16)
                dv_out.append(_dot_tn(a.astype(BF16), dob) + _dot_nt(kb.astype(BF16), dhb))
                dqb = _dot(dob, ht.astype(BF16))
                dkb = _dot(vb, dhb)
                new_dhts.append(dht * el + _dot_tn(dob, qb))
                dq = dq + eb * dqb
                dk = dk + ekb * dkb
                edge = jnp.sum(kb * dkb, axis=0, keepdims=True) + el * jnp.sum(ht * dht, axis=0, keepdims=True)
                db_out.append(q * dq - k * dk + jnp.where(last_row, edge, 0.0))
                dq_out.append(dq)
                dk_out.append(dk)
            dk_all = lane_cat(dk_out)
            db_hi, db_lo = _split2(lane_cat(db_out))
            dg = _dot(triu, db_hi) + _dot(triu, db_lo)
            df = dg / f - dk_all
            dz_ref[rows, :] = (df * (1.0 - lb) * sg * (1.0 - sg)).astype(dz_ref.dtype)
            dq_ref[rows, :] = lane_cat(dq_out).astype(dq_ref.dtype)
            dv_ref[rows, :] = lane_cat(dv_out).astype(dv_ref.dtype)
            return tuple(new_dhts), dlb + jnp.sum(df * (1.0 - sg), axis=0, keepdims=True)

        zero = (tuple(jnp.zeros((REC_DIM, REC_DIM), F32) for _ in heads), jnp.zeros((1, HGRN_PAIR * REC_DIM), F32))
        _, dlb = lax.fori_loop(0, nc, chunk, zero)
        lb = _sigmoid(lb_ref[0:1, :] - lb_ref[1:2, :])
        dlb_ref[...] = jnp.broadcast_to(dlb * lb * (1.0 - lb), (8, HGRN_PAIR * REC_DIM))
        comm_last()

    hp, wd = REC_HEADS // HGRN_PAIR, HGRN_PAIR * REC_DIM
    cq, cf, ci_ = (c * REC_DIM // wd for c in (COL_RQ, COL_RF, COL_RI))
    return pl.pallas_call(
        body, name="hgrn_bwd", grid=(B, hp),
        in_specs=[pl.BlockSpec((S, wd), lambda b, h: (b, cq + h)),
                  pl.BlockSpec((S, wd), lambda b, h: (b, cf + h)),
                  pl.BlockSpec((S, wd), lambda b, h: (b, ci_ + h)),
                  pl.BlockSpec((2, wd), lambda b, h: (0, h)),
                  pl.BlockSpec((HGRN_PAIR * nc, REC_DIM, REC_DIM), lambda b, h: (b * hp + h, 0, 0)),
                  pl.BlockSpec((S, wd), lambda b, h: (b, h))] + c_in_specs,
        out_specs=[pl.BlockSpec((S, wd), lambda b, h: (b, h))] * 3
        + [pl.BlockSpec((8, wd), lambda b, h: (b, h))] + [ANY] * nco,
        out_shape=[jax.ShapeDtypeStruct((T, 512), BF16)] * 3 + [jax.ShapeDtypeStruct((B * 8, 512), F32)]
        + c_out_shapes,
        scratch_shapes=[pltpu.VMEM((CHUNK, wd), F32)] * 2 + c_sems,
        compiler_params=_params("arbitrary", "arbitrary"))(proj, proj, proj, lb_param, states, do, *c_arrays)


def _rec_gate_fwd(rec, proj, rec_norm):
    T = rec.shape[0]

    def fn(accs, tv, cv):
        return [_rms_hat(tv[0]) * cv[0] * _sigmoid(tv[1])]

    return _tile_call("rec_gate", fn, T, 512, _pick(T, 1024), REC_DIM, tiles=[(rec, 0), (proj, COL_RG)],
                      consts=[rec_norm], outs=[BF16])[0]


def _rec_gate_bwd(dyb, w_rec_proj, rec, proj, rec_norm):
    T = rec.shape[0]

    def fn(accs, tv, cv):
        d, r, rg = accs[0], tv[0], tv[1]
        sg = _sigmoid(rg)
        rn = _rms_hat(r) * cv[0]
        dh, dg = _rms_bwd_vals(d * sg, r, cv[0])
        return [dh, d * rn * sg * (1.0 - sg), dg]

    return _tile_call("rec_gate_bwd", fn, T, 512, _pick(T, 1024), REC_DIM, pairs=[(dyb, 0, w_rec_proj, "nt")],
                      tiles=[(rec, 0), (proj, COL_RG)], consts=[rec_norm], outs=[F32, BF16], parts=1)


def _mix_out_fwd(att, recn, proj, w_att_proj, w_rec_proj, w_out, h1):
    T = att.shape[0]
    tn = 256

    def merge(accs, tv, cv):
        ya, yb = accs
        return [ya, yb, _sigmoid(tv[0]) * ya + _sigmoid(tv[1]) * yb]

    ya, yb, merged = _tile_call(
        "merge", merge, T, D_MODEL, _pick(T, 1024), tn,
        pairs=[(att, 0, w_att_proj, "nn"), (recn, 0, w_rec_proj, "nn")],
        tiles=[(proj, COL_GA * 128 // tn), (proj, COL_GB * 128 // tn)], outs=[BF16] * 3)

    def res(accs, tv, cv):
        return [tv[0] + accs[0]]

    h2 = _tile_call("mix_out", res, T, D_MODEL, _pick(T, 512), 512, pairs=[(merged, 0, w_out, "nn")],
                    tiles=[(h1, 0)], outs=[F32])[0]
    return h2, (ya, yb, merged)


GATHER_FIRST = ("w_ffn1_in",)
GATHER_MIX = ("w_ffn1_out", "w_in", "w_att_proj", "w_rec_proj", "w_out")
GATHER_LAST = ("w_ffn2_in", "w_ffn2_out", "w_ple_gate", "w_ple_proj")
SCATTER_LATE = ("w_ple_gate", "w_ple_proj", "w_ffn2_in", "w_ffn2_out")
SCATTER_MIX = ("w_out", "w_att_proj", "w_rec_proj", "w_in")
SCATTER_LAST = ("w_ffn1_in", "w_ffn1_out")


def _local_step(x, p, tgt, w, mine16, cc, me_chip, B, S):
    T = B * S
    w = dict(w)
    g_ffn1, g_mix, g_ffn2, g_ple = w["norm_ffn1"], w["norm_mix"], w["norm_ffn2"], w["norm_ple"]
    g_fin = w["norm_final"].reshape(1, D_MODEL)
    grads, part, from_chips = {}, {}, {}

    def gather(names):
        return _gather_comm([mine16[n] for n in names])

    def place(names, got):
        for n, g in zip(names, got):
            full = lax.dynamic_update_index_in_dim(g, mine16[n], me_chip, 0)
            w[n] = full if n in ("w_ffn1_in", "w_ffn2_in") else _natural(n, full)

    def scatter(tag, names):
        from_sib = _swap_halves("rs_sibling_" + tag, [grads[n][1] for n in names])
        for n, fs in zip(names, from_sib):
            part[n] = _add_sibling("rs_add_sib_" + n, grads[n][0], fs, cc)
        return _scatter_comm([part[n][1] for n in names])

    def scattered(names, got):
        for n, g in zip(names, got):
            from_chips[n] = g

    place(GATHER_FIRST, _run_comm("gather_first", gather(GATHER_FIRST)))
    def ffn1_out_weight(got):
        place(GATHER_MIX, got)
        return w["w_ffn1_out"]

    h1, sv1, got_last = _ffn_fwd("ffn1", x, g_ffn1, w["w_ffn1_in"], None, comm_in=gather(GATHER_MIX),
                                 comm_out=gather(GATHER_LAST), w_out_of=ffn1_out_weight)
    place(GATHER_LAST, got_last)
    u = _rms_fwd("mix_norm", h1, g_mix)

    def ident(accs, tv, cv):
        return [accs[0]]

    proj = _tile_call("in_proj", ident, T, IN_W, _pick(T, 1024), 256, pairs=[(u, 0, w["w_in"], "nn")],
                      outs=[F32])[0]
    onehot = jnp.asarray(_t5_onehot())
    bias = _small_mm("t5_bias", w["rel_bias"].T, onehot.astype(BF16), "right")
    bias = bias.reshape(N_Q_HEADS, ATT_BLOCK, 2 * ATT_BLOCK)
    sinks = w["attn_sinks"].reshape(N_Q_HEADS)
    kk2, vv2 = _kv_layouts(proj)
    att = _swa_fwd(proj, kk2, vv2, bias, sinks, B, S)
    rec, states = _hgrn_fwd(proj, w["lb_param"], B, S)
    recn = _rec_gate_fwd(rec, proj, w["rec_norm"])
    h2, (ya, yb, merged) = _mix_out_fwd(att, recn, proj, w["w_att_proj"], w["w_rec_proj"], w["w_out"], h1)
    h3, sv2, _ = _ffn_fwd("ffn2", h2, g_ffn2, w["w_ffn2_in"], w["w_ffn2_out"])
    n3 = _rms_fwd("ple_norm", h3, g_ple)

    def ple(accs, tv, cv):
        gate = _sigmoid(accs[0])
        return [gate, accs[1], tv[0] + gate * accs[1]]

    gate_p, pp, h4 = _tile_call(
        "ple", ple, T, D_MODEL, _pick(T, 512), 512,
        pairs=[(n3, 0, w["w_ple_gate"], "nn"), (p, 0, w["w_ple_proj"], "nn")], tiles=[(h3, 0)],
        outs=[BF16, BF16, F32])

    def head(accs, tv, cv):
        h, t = tv
        err = _rms_hat(h) * cv[0] - t
        dh, dg = _rms_bwd_vals(err * (1.0 / D_MODEL), h, cv[0])
        return [dh, _group8(err * err), dg]

    dh4, loss_p, dg_fin = _tile_call("loss_head", head, T, D_MODEL, _pick(T, 256), D_MODEL,
                                     tiles=[(h4, 0), (tgt, 0)], consts=[g_fin], outs=[F32], parts=2)
    grads["norm_final"] = dg_fin

    def dple(accs, tv, cv):
        d, gt, ppv = tv[0], tv[1].astype(F32), tv[2].astype(F32)
        return [d * ppv * gt * (1.0 - gt), d * gt]

    dzg, dpp = _tile_call("ple_dact", dple, T, D_MODEL, _pick(T, 512), D_MODEL,
                          tiles=[(dh4, 0), (gate_p, 0), (pp, 0)], outs=[BF16] * 2)
    grads["w_ple_gate"] = _mm_tn_rows("ple_dwg", n3, dzg)
    grads["w_ple_proj"] = _mm_tn_cols("ple_dwp", p, dpp)

    def dnorm(accs, tv, cv):
        dh, dg = _rms_bwd_vals(accs[0], tv[0], cv[0])
        return [tv[1] + dh, dg]

    dh3, grads["norm_ple"] = _tile_call(
        "ple_dnorm", dnorm, T, D_MODEL, _pick(T, 256), D_MODEL, pairs=[(dzg, 0, w["w_ple_gate"], "nt")],
        tiles=[(h3, 0), (dh4, 0)], consts=[g_ple], outs=[F32], parts=1)

    dh2, grads["norm_ffn2"], grads["w_ffn2_in"], grads["w_ffn2_out"], _, _ = _ffn_bwd(
        "ffn2b", dh3, h2, g_ffn2, w["w_ffn2_in"], w["w_ffn2_out"], sv2)
    scatter_late = scatter("late", SCATTER_LATE)

    def to_bf(accs, tv, cv):
        return [tv[0]]

    dh2b = _tile_call("mix_dcast", to_bf, T, D_MODEL, _pick(T, 512), D_MODEL, tiles=[(dh2, 0)], outs=[BF16])[0]
    grads["w_out"] = _mm_tn_rows("mix_dwout", merged, dh2b)
    tn = 256

    def dmerge(accs, tv, cv):
        dm = accs[0]
        sa, sb = _sigmoid(tv[0]), _sigmoid(tv[1])
        yav, ybv = tv[2].astype(F32), tv[3].astype(F32)
        return [dm * sa, dm * sb, dm * yav * sa * (1.0 - sa), dm * ybv * sb * (1.0 - sb)]

    dya, dyb, dga, dgb = _tile_call(
        "mix_dmerge", dmerge, T, D_MODEL, _pick(T, 1024), tn, pairs=[(dh2b, 0, w["w_out"], "nt")],
        tiles=[(proj, COL_GA * 128 // tn), (proj, COL_GB * 128 // tn), (ya, 0), (yb, 0)], outs=[BF16] * 4)
    grads["w_att_proj"] = _mm_tn_cols("mix_dwatt", att, dya)
    grads["w_rec_proj"] = _mm_tn_cols("mix_dwrec", recn, dyb)

    datt = _tile_call("mix_datt", ident, T, 512, _pick(T, 1024), 512, pairs=[(dya, 0, w["w_att_proj"], "nt")],
                      outs=[BF16])[0]
    drec, drg, grads["rec_norm"] = _rec_gate_bwd(dyb, w["w_rec_proj"], rec, proj, w["rec_norm"])

    drq, drf, dri, dlb, *got = _hgrn_bwd(proj, w["lb_param"], states, drec, B, S, comm=scatter_late)
    scattered(SCATTER_LATE, got)
    grads["lb_param"] = dlb
    daq, dk2, dv2, dbias, dsink = _swa_bwd(proj, kk2, vv2, bias, sinks, datt, B, S)
    grads["attn_sinks"] = dsink
    grads["rel_bias"] = _small_mm("t5_dbias", dbias.reshape(N_Q_HEADS, -1), onehot.T.astype(BF16), "right")
    dak = (dk2[:, :128] + _swap_heads(dk2[:, 128:])).astype(BF16)
    dav = (dv2[:, :128] + _swap_heads(dv2[:, 128:])).astype(BF16)
    dproj = jnp.concatenate([daq, dak, dav, drq, drf, dri, drg, dga, dgb], axis=1)
    tk = _pick(T, 512, 128)
    w_in_shard = IN_W // N_CHIPS
    gw32, gw16 = _mm_tn("mix_dwin", (1, 2, T // tk),
                        (u, (tk, D_MODEL), lambda i, j, k: (k, 0)), (dproj, (tk, IN_W // 2), lambda i, j, k: (k, j)),
                        _grad_pair((D_MODEL, IN_W), (D_MODEL, IN_W // 2), lambda i, j, k: (0, j)))
    to_sh = lambda t: t.reshape(D_MODEL, N_CHIPS, w_in_shard).transpose(1, 0, 2)
    grads["w_in"] = (to_sh(gw32), to_sh(gw16))
    scatter_mix = scatter("mix", SCATTER_MIX)

    def dnorm_mix(accs, tv, cv):
        dh, dg = _rms_bwd_vals(accs[0], tv[0], cv[0])
        return [tv[1] + dh, dg]

    dh1, grads["norm_mix"] = _tile_call(
        "mix_dnorm", dnorm_mix, T, D_MODEL, _pick(T, 256), D_MODEL, pairs=[(dproj, 0, w["w_in"], "nt")],
        tiles=[(h1, 0), (dh2, 0)], consts=[g_mix], outs=[F32], parts=1)

    def scatter_last(dw_in, dw_out):
        grads["w_ffn1_in"], grads["w_ffn1_out"] = dw_in, dw_out
        return scatter("last", SCATTER_LAST)

    dx, grads["norm_ffn1"], _, _, got, got_last = _ffn_bwd(
        "ffn1b", dh1, x, g_ffn1, w["w_ffn1_in"], w["w_ffn1_out"], sv1, comm=scatter_mix, comm_last=scatter_last)
    scattered(SCATTER_MIX, got)
    scattered(SCATTER_LAST, got_last)
    return loss_p, dx, grads, part, from_chips


def _place():
    x, y, c = lax.axis_index("x"), lax.axis_index("y"), lax.axis_index("c")
    return x, y, c


def _other_chips(x, y):
    return [(1 - x, y, 2 * (1 - x) + y), (x, 1 - y, 2 * x + 1 - y), (1 - x, 1 - y, 2 * (1 - x) + 1 - y)]


def _half_rows(ref_3d, chip, h, rows):
    return ref_3d.at[chip, pl.ds(h * rows, rows), :]


def _run_comm(name, comm):
    nci, nco = len(comm.ins), len(comm.out_shapes)

    def body(*refs):
        cin, cout, send_sems, recv_sems = refs[:nci], refs[nci:nci + nco], refs[-2], refs[-1]
        comm.start(cin, cout, send_sems, recv_sems)
        comm.finish(cin, cout, send_sems, recv_sems)

    return pl.pallas_call(
        body, name=name, in_specs=[ANY] * nci, out_specs=[ANY] * nco, out_shape=list(comm.out_shapes),
        scratch_shapes=[pltpu.SemaphoreType.DMA((comm.n_sems,)), pltpu.SemaphoreType.DMA((comm.n_sems,))],
    )(*comm.ins)


def _gather_comm(ws):
    nw = len(ws)

    def parts(w_refs, out_refs, send_sems, recv_sems):
        x, y, c = _place()
        me = 2 * x + y
        chips = _other_chips(x, y)

        def copy(i, k, chip, h, to, src=None):
            half = ws[i].shape[0] // 2
            dst = _half_rows(out_refs[i], chip, h, half)
            return pltpu.make_async_remote_copy(
                src_ref=dst if src is None else src, dst_ref=dst,
                send_sem=send_sems.at[6 * i + k], recv_sem=recv_sems.at[6 * i + k], device_id=to, device_id_type=MESH)

        def first():
            out = []
            for i in range(nw):
                half = ws[i].shape[0] // 2
                out += [copy(i, j, me, c, (cx, cy, c), src=w_refs[i].at[pl.ds(c * half, half), :])
                        for j, (cx, cy, _) in enumerate(chips)]
            return out

        return copy, first, chips, c, (x, y, 1 - c)

    def start(*refs):
        _, first, _, _, _ = parts(*refs)
        for cp in first():
            cp.start()

    def finish(*refs):
        copy, first, chips, c, sibling = parts(*refs)
        passed = []
        for i in range(nw):
            for j, (cx, cy, ci) in enumerate(chips):
                copy(i, j, ci, c, (cx, cy, c)).wait_recv()
                fw = copy(i, 3 + j, ci, c, sibling)
                fw.start()
                passed.append(fw)
        for i in range(nw):
            for j, (_, _, ci) in enumerate(chips):
                copy(i, 3 + j, ci, 1 - c, sibling).wait_recv()
        for cp in first() + passed:
            cp.wait_send()

    return _Comm(list(ws), [jax.ShapeDtypeStruct((N_CHIPS,) + w.shape, w.dtype) for w in ws], 6 * nw, start, finish)


def _scatter_comm(ps):
    nw = len(ps)

    def copies(p_refs, out_refs, send_sems, recv_sems):
        x, y, c = _place()
        cps = []
        for i in range(nw):
            for j, (cx, cy, ci) in enumerate(_other_chips(x, y)):
                cps.append(pltpu.make_async_remote_copy(
                    src_ref=p_refs[i].at[ci], dst_ref=out_refs[i].at[j], send_sem=send_sems.at[3 * i + j],
                    recv_sem=recv_sems.at[3 * i + j], device_id=(cx, cy, c), device_id_type=MESH))
        return cps

    def start(*refs):
        for cp in copies(*refs):
            cp.start()

    def finish(*refs):
        for cp in copies(*refs):
            cp.wait()

    return _Comm(list(ps), [jax.ShapeDtypeStruct((3,) + p.shape[1:], p.dtype) for p in ps], 3 * nw, start, finish)


def _swap_halves(name, gs):
    nw = len(gs)

    def body(*refs):
        g_refs, out_refs, send_sems, recv_sems = refs[:nw], refs[nw:2 * nw], refs[2 * nw], refs[2 * nw + 1]
        x, y, c = _place()
        cps = []
        for i in range(nw):
            half = gs[i].shape[1] // 2
            cps.append(pltpu.make_async_remote_copy(
                src_ref=g_refs[i].at[:, pl.ds((1 - c) * half, half), :], dst_ref=out_refs[i],
                send_sem=send_sems.at[i], recv_sem=recv_sems.at[i], device_id=(x, y, 1 - c), device_id_type=MESH))
        for cp in cps:
            cp.start()
        for cp in cps:
            cp.wait()

    return pl.pallas_call(
        body, name=name, in_specs=[ANY] * nw, out_specs=[ANY] * nw,
        out_shape=[jax.ShapeDtypeStruct((N_CHIPS, g.shape[1] // 2, g.shape[2]), g.dtype) for g in gs],
        scratch_shapes=[pltpu.SemaphoreType.DMA((nw,)), pltpu.SemaphoreType.DMA((nw,))],
    )(*gs)


def _join_halves(name, ss):
    nw = len(ss)

    def body(*refs):
        s_refs, out_refs, send_sems, recv_sems = refs[:nw], refs[nw:2 * nw], refs[2 * nw], refs[2 * nw + 1]
        x, y, c = _place()
        cps = [pltpu.make_async_remote_copy(
            src_ref=s_refs[i], dst_ref=out_refs[i], send_sem=send_sems.at[i], recv_sem=recv_sems.at[i],
            device_id=(x, y, 1 - c), device_id_type=MESH) for i in range(nw)]
        for cp in cps:
            cp.start()
        for cp in cps:
            cp.wait()

    return pl.pallas_call(
        body, name=name, in_specs=[ANY] * nw, out_specs=[ANY] * nw,
        out_shape=[jax.ShapeDtypeStruct(s.shape, s.dtype) for s in ss],
        scratch_shapes=[pltpu.SemaphoreType.DMA((nw,)), pltpu.SemaphoreType.DMA((nw,))],
    )(*ss)


def _allreduce_small(sp):
    def body(s_ref, out_ref, slots, send_sems, recv_sems):
        x, y, c = _place()
        me = 4 * x + 2 * y + c
        slots[me] = s_ref[...]
        cps = []
        for r in range(1, N_DEV):
            px, py, pc = x ^ (r >> 2), y ^ ((r >> 1) & 1), c ^ (r & 1)
            cps.append(pltpu.make_async_remote_copy(
                src_ref=s_ref, dst_ref=slots.at[me], send_sem=send_sems.at[r - 1], recv_sem=recv_sems.at[r - 1],
                device_id=(px, py, pc), device_id_type=MESH))
        for cp in cps:
            cp.start()
        for r in range(1, N_DEV):
            px, py, pc = x ^ (r >> 2), y ^ ((r >> 1) & 1), c ^ (r & 1)
            pltpu.make_async_remote_copy(
                src_ref=s_ref, dst_ref=slots.at[4 * px + 2 * py + pc], send_sem=send_sems.at[r - 1],
                recv_sem=recv_sems.at[r - 1], device_id=(px, py, pc), device_id_type=MESH).wait_recv()
        for cp in cps:
            cp.wait_send()
        acc = slots[0]
        for d in range(1, N_DEV):
            acc = acc + slots[d]
        out_ref[...] = acc

    return pl.pallas_call(
        body, name="allreduce_small",
        in_specs=[pl.BlockSpec(memory_space=pltpu.VMEM)], out_specs=pl.BlockSpec(memory_space=pltpu.VMEM),
        out_shape=jax.ShapeDtypeStruct(sp.shape, F32),
        scratch_shapes=[pltpu.VMEM((N_DEV,) + sp.shape, F32), pltpu.SemaphoreType.DMA((N_DEV - 1,)),
                        pltpu.SemaphoreType.DMA((N_DEV - 1,))],
    )(sp)


def _scalar(v):
    return jnp.reshape(v, (1,)).astype(jnp.int32)


def _row_tile(h, dtype_mult=16):
    return _pick(h, 256, dtype_mult)


def _add_sibling(name, g32, from_sib, c):
    _, r, n = g32.shape
    h = r // 2
    th = _row_tile(h)
    nt = h // th

    def body(c_ref, g_ref, s_ref, o32_ref, o16_ref):
        s = g_ref[...] + s_ref[...].astype(F32)
        o32_ref[...] = s
        o16_ref[...] = s.astype(BF16)

    blk = (None, th, n)
    return pl.pallas_call(
        body, name=name,
        grid_spec=pltpu.PrefetchScalarGridSpec(
            num_scalar_prefetch=1, grid=(N_CHIPS, nt),
            in_specs=[pl.BlockSpec(blk, lambda k, t, c_ref: (k, c_ref[0] * nt + t, 0)),
                      pl.BlockSpec(blk, lambda k, t, c_ref: (k, t, 0))],
            out_specs=[pl.BlockSpec(blk, lambda k, t, c_ref: (k, t, 0))] * 2),
        out_shape=[jax.ShapeDtypeStruct((N_CHIPS, h, n), F32), jax.ShapeDtypeStruct((N_CHIPS, h, n), BF16)],
        compiler_params=_params("arbitrary", "arbitrary"))(_scalar(c), g32, from_sib)


def _add_chips(name, p32, from_chips, me_chip):
    _, h, n = p32.shape
    th = _row_tile(h)

    def body(m_ref, p_ref, a_ref, b_ref, c_ref, o_ref):
        o_ref[...] = p_ref[...] + a_ref[...].astype(F32) + b_ref[...].astype(F32) + c_ref[...].astype(F32)

    blk = (None, th, n)
    return pl.pallas_call(
        body, name=name,
        grid_spec=pltpu.PrefetchScalarGridSpec(
            num_scalar_prefetch=1, grid=(h // th,),
            in_specs=[pl.BlockSpec(blk, lambda t, m_ref: (m_ref[0], t, 0))]
            + [pl.BlockSpec(blk, lambda t, m_ref, j=j: (j, t, 0)) for j in range(3)],
            out_specs=pl.BlockSpec((th, n), lambda t, m_ref: (t, 0))),
        out_shape=jax.ShapeDtypeStruct((h, n), F32),
        compiler_params=_params("arbitrary"))(_scalar(me_chip), p32, from_chips, from_chips, from_chips)


def _adamw_vals(w, g, m, v):
    m = ADAM_B1 * m + (1.0 - ADAM_B1) * g
    v = ADAM_B2 * v + (1.0 - ADAM_B2) * (g * g)
    m_hat = m / (1.0 - ADAM_B1 ** ADAM_STEP)
    v_hat = v / (1.0 - ADAM_B2 ** ADAM_STEP)
    delta = -ADAM_LR * (m_hat / (jnp.sqrt(v_hat) + ADAM_EPS) + ADAM_WD * w)
    return delta, m, v


def _adamw_halves(name, w, m, v, g_mine, g_sib, c):
    r, n = w.shape
    h = r // 2
    th = _row_tile(h, 8)
    nt = h // th

    def body(c_ref, w_ref, m_ref, v_ref, a_ref, b_ref, g_ref, d_ref, nm_ref, nv_ref):
        mine = (pl.program_id(0) // nt) == c_ref[0]
        g = jnp.where(mine, a_ref[...], b_ref[...])
        d, nm, nv = _adamw_vals(w_ref[...], g, m_ref[...], v_ref[...])
        g_ref[...] = g
        d_ref[...] = d
        nm_ref[...] = nm
        nv_ref[...] = nv

    full = pl.BlockSpec((th, n), lambda t, c_ref: (t, 0))
    part = pl.BlockSpec((th, n), lambda t, c_ref: (t % nt, 0))
    return pl.pallas_call(
        body, name=name,
        grid_spec=pltpu.PrefetchScalarGridSpec(
            num_scalar_prefetch=1, grid=(2 * nt,), in_specs=[full, full, full, part, part], out_specs=[full] * 4),
        out_shape=[jax.ShapeDtypeStruct((r, n), F32)] * 4,
        compiler_params=_params("arbitrary"))(_scalar(c), w, m, v, g_mine, g_sib)


def _adamw(name, w, g, m, v):
    R, W = w.shape

    def fn(accs, tv, cv):
        return list(_adamw_vals(*tv))

    return _tile_call(name, fn, R, W, _pick(R, 256), W, tiles=[(w, 0), (g, 0), (m, 0), (v, 0)], outs=[F32] * 3)


SMALL_LAYOUT = (("rel_bias", 2, 256), ("lb_param", 8, 1024), ("norm_ffn1", 8, 1024), ("norm_mix", 8, 1024),
                ("attn_sinks", 1, 8), ("rec_norm", 1, 128), ("norm_ffn2", 8, 1024), ("norm_ple", 8, 1024),
                ("norm_final", 8, 1024), ("loss", 8, 1024))


def _pack_small(vals):
    rows = []
    for name, nrows, n in SMALL_LAYOUT:
        flat = vals[name].reshape(-1)
        flat = jnp.pad(flat, (0, nrows * 128 - n))
        rows.append(flat.reshape(nrows, 128))
    packed = jnp.concatenate(rows, axis=0)
    return jnp.pad(packed, ((0, SMALL_ROWS - packed.shape[0]), (0, 0)))


def _unpack_small(packed, shapes):
    out, r = {}, 0
    for name, nrows, n in SMALL_LAYOUT:
        out[name] = packed[r:r + nrows].reshape(-1)[:n].reshape(shapes[name])
        r += nrows
    return out


def _natural(name, s):
    if name in COL_SHARDED:
        return s.transpose(1, 0, 2).reshape(s.shape[1], -1)
    return s.reshape(-1, s.shape[2])


def kernel(x, p, rel_bias, lb_param, norm_ffn1, w_ffn1_in, w_ffn1_out, norm_mix, w_in, attn_sinks, rec_norm, w_att_proj, w_rec_proj, w_out, norm_ffn2, w_ffn2_in, w_ffn2_out, norm_ple, w_ple_gate, w_ple_proj, norm_final, loss_target, m_rel_bias, m_lb_param, m_norm_ffn1, m_w_ffn1_in, m_w_ffn1_out, m_norm_mix, m_w_in, m_attn_sinks, m_rec_norm, m_w_att_proj, m_w_rec_proj, m_w_out, m_norm_ffn2, m_w_ffn2_in, m_w_ffn2_out, m_norm_ple, m_w_ple_gate, m_w_ple_proj, m_norm_final, v_rel_bias, v_lb_param, v_norm_ffn1, v_w_ffn1_in, v_w_ffn1_out, v_norm_mix, v_w_in, v_attn_sinks, v_rec_norm, v_w_att_proj, v_w_rec_proj, v_w_out, v_norm_ffn2, v_w_ffn2_in, v_w_ffn2_out, v_norm_ple, v_w_ple_gate, v_w_ple_proj, v_norm_final):
    args = dict(locals())
    wsh = {n: args[n] for n in WEIGHTS}
    B, S = x.shape[0], x.shape[1]
    T = B * S
    cx, cy, cc = _place()
    me_chip = 2 * cx + cy

    mine16 = {n: wsh[n][0].astype(BF16) for n in BIG}
    loss_p, dx, grads, part, from_chips = _local_step(
        x.reshape(T, D_MODEL), p.reshape(T, PLE_DIM), loss_target.reshape(T, D_MODEL),
        {n: wsh[n] for n in SMALL}, mine16, cc, me_chip, B, S)

    s_mine = [_add_chips("rs_add_chips_" + n, part[n][0], from_chips[n], me_chip) for n in BIG]
    s_sib = _join_halves("rs_join", s_mine)

    small_vals = {
        "rel_bias": grads["rel_bias"].T,
        "lb_param": jnp.concatenate([_colsum("dlb_sum", grads["lb_param"]),
                                     -_colsum("dlb_sum2", grads["lb_param"])], axis=0) / 8.0,
        "attn_sinks": grads["attn_sinks"][:, 0],
        "rec_norm": _colsum("drn_sum", grads["rec_norm"]).reshape(REC_HEADS, REC_DIM).sum(axis=0),
        "loss": _colsum("loss_sum", loss_p),
    }
    for n in ("norm_ffn1", "norm_mix", "norm_ffn2", "norm_ple", "norm_final"):
        small_vals[n] = _colsum(n + "_sum", grads[n])
    red = _allreduce_small(_pack_small(small_vals))
    small_shapes = {n: wsh[n].shape for n in SMALL}
    small_shapes["loss"] = (D_MODEL,)
    small = _unpack_small(red, small_shapes)
    loss = 0.5 * jnp.sum(small["loss"]) / D_MODEL

    out_g, out_d, out_m, out_v = {}, {}, {}, {}
    for n, gm, gs in zip(BIG, s_mine, s_sib):
        res = _adamw_halves("adamw_" + n, wsh[n][0], args["m_" + n][0], args["v_" + n][0], gm, gs, cc)
        out_g[n], out_d[n], out_m[n], out_v[n] = (t[None] for t in res)
    sw = _pack_small({**{n: wsh[n] for n in SMALL}, "loss": jnp.zeros((D_MODEL,), F32)})
    sm = _pack_small({**{n: args["m_" + n] for n in SMALL}, "loss": jnp.zeros((D_MODEL,), F32)})
    sv = _pack_small({**{n: args["v_" + n] for n in SMALL}, "loss": jnp.ones((D_MODEL,), F32)})
    sd, snm, snv = _adamw("adamw_small", sw, red, sm, sv)
    ud, um, uv = (_unpack_small(t, small_shapes) for t in (sd, snm, snv))
    for n in SMALL:
        out_g[n], out_d[n], out_m[n], out_v[n] = small[n], ud[n], um[n], uv[n]

    return (loss, dx.reshape(B, S, D_MODEL), *[out_g[n] for n in WEIGHTS], *[out_d[n] for n in WEIGHTS],
            *[out_m[n] for n in WEIGHTS], *[out_v[n] for n in WEIGHTS])
```

```python
import numpy as np
import jax
import jax.numpy as jnp
from jax import lax
from jax.experimental import pallas as pl
from jax.experimental.pallas import tpu as pltpu

F32 = jnp.float32
BF16 = jnp.bfloat16
MESH = pl.DeviceIdType.MESH

D_MODEL = 1024
D_FF = 2816
FF_SHARD = 2 * D_FF // 4
HEAD_DIM = 64
N_Q_HEADS = 8
ATT_BLOCK = 128
N_BUCKETS = 32
MAX_DISTANCE = 128
REC_HEADS = 4
REC_DIM = 128
PLE_DIM = 256
EPS = 1e-6
IN_W = 4864
COL_AQ, COL_AK, COL_AV, COL_RQ, COL_RF, COL_RI, COL_RG, COL_GA, COL_GB = 0, 4, 5, 6, 10, 14, 18, 22, 30

CHUNK = 64
SUB = 8
N_SUB = CHUNK // SUB
HGRN_PAIR = 2

ADAM_LR, ADAM_B1, ADAM_B2, ADAM_EPS, ADAM_WD, ADAM_STEP = 0.001, 0.9, 0.999, 1e-08, 0.01, 10

V7X_VMEM_LIMIT = 56 * 1024 * 1024
N_CHIPS = 4
N_DEV = 8

BIG = ("w_ffn1_in", "w_ffn1_out", "w_in", "w_att_proj", "w_rec_proj", "w_out",
       "w_ffn2_in", "w_ffn2_out", "w_ple_gate", "w_ple_proj")
COL_SHARDED = ("w_ffn1_in", "w_in", "w_att_proj", "w_rec_proj", "w_ffn2_in", "w_ple_proj")
WEIGHTS = ("rel_bias", "lb_param", "norm_ffn1", "w_ffn1_in", "w_ffn1_out", "norm_mix", "w_in", "attn_sinks",
           "rec_norm", "w_att_proj", "w_rec_proj", "w_out", "norm_ffn2", "w_ffn2_in", "w_ffn2_out", "norm_ple",
           "w_ple_gate", "w_ple_proj", "norm_final")
SMALL = tuple(n for n in WEIGHTS if n not in BIG)
SMALL_ROWS = 64


def _params(*sem):
    return pltpu.CompilerParams(dimension_semantics=sem, vmem_limit_bytes=V7X_VMEM_LIMIT)


def _pick(n, cap, mult=8):
    if n <= cap:
        return n
    for t in range(cap - cap % mult, 0, -mult):
        if n % t == 0:
            return t
    raise ValueError((n, cap, mult))


def _dot(a, b):
    return jnp.dot(a, b, preferred_element_type=F32)


def _dot_nt(a, b):
    return lax.dot_general(a, b, (((1,), (1,)), ((), ())), preferred_element_type=F32)


def _dot_tn(a, b):
    return lax.dot_general(a, b, (((0,), (0,)), ((), ())), preferred_element_type=F32)


def _split3(x):
    hi = x.astype(BF16)
    r = x - hi.astype(F32)
    mid = r.astype(BF16)
    lo = (r - mid.astype(F32)).astype(BF16)
    return hi, mid, lo


def _split2(x):
    hi = x.astype(BF16)
    return hi, (x - hi.astype(F32)).astype(BF16)


def _sel_left(sel_bf16, x):
    hi, mid, lo = _split3(x)
    return _dot(sel_bf16, hi) + _dot(sel_bf16, mid) + _dot(sel_bf16, lo)


def _sel_right(x, sel_bf16):
    hi, mid, lo = _split3(x)
    return _dot(hi, sel_bf16) + _dot(mid, sel_bf16) + _dot(lo, sel_bf16)


def _sigmoid(x):
    return 1.0 / (1.0 + jnp.exp(-x))


def _group8(x):
    r, w = x.shape
    return x.reshape(r // 8, 8, w).sum(axis=0)


class _Comm:
    def __init__(self, ins, out_shapes, n_sems, start, finish):
        self.ins, self.out_shapes, self.n_sems, self.start, self.finish = ins, out_shapes, n_sems, start, finish


ANY = pl.BlockSpec(memory_space=pl.ANY)


def _comm_parts(comm):
    if comm is None:
        return [], [], [], []
    sems = [pltpu.SemaphoreType.DMA((comm.n_sems,)), pltpu.SemaphoreType.DMA((comm.n_sems,))]
    return list(comm.ins), [ANY] * len(comm.ins), list(comm.out_shapes), sems


def _comm_run(comm, grid, refs, n_in, n_out):
    if comm is None:
        return (lambda: None), (lambda: None)
    nci, nco = len(comm.ins), len(comm.out_shapes)
    cin = refs[n_in:n_in + nci]
    cout = refs[n_in + nci + n_out:n_in + nci + n_out + nco]
    send_sems, recv_sems = refs[-2], refs[-1]
    ids = [pl.program_id(d) for d in range(len(grid))]
    is_first = ids[0] == 0
    is_last = ids[0] == grid[0] - 1
    for d in range(1, len(grid)):
        is_first = is_first & (ids[d] == 0)
        is_last = is_last & (ids[d] == grid[d] - 1)

    def first():
        @pl.when(is_first)
        def _():
            comm.start(cin, cout, send_sems, recv_sems)

    def last():
        @pl.when(is_last)
        def _():
            comm.finish(cin, cout, send_sems, recv_sems)

    return first, last


def _call(name, fn, grid, ins, outs, pairs=(), comm=None):
    in_pair = {i for p in pairs for i in p[:2]}
    n_in, n_out = len(ins), len(outs)
    c_arrays, c_in_specs, c_out_shapes, c_sems = _comm_parts(comm)

    def body(*refs):
        first, last = _comm_run(comm, grid, refs, n_in, n_out)
        first()
        accs = []
        for ia, ib, kind in pairs:
            a, b = refs[ia][...].astype(BF16), refs[ib][...].astype(BF16)
            accs.append(_dot(a, b) if kind == "nn" else _dot_nt(a, b))
        vals = [refs[i][...] for i in range(n_in) if i not in in_pair]
        res = fn(accs, vals)
        out_refs = refs[n_in + len(c_arrays):n_in + len(c_arrays) + n_out]
        assert len(res) == len(out_refs), (name, len(res), len(out_refs))
        for o_ref, val in zip(out_refs, res):
            o_ref[...] = val.astype(o_ref.dtype)
        last()

    return pl.pallas_call(
        body, name=name, grid=grid,
        in_specs=[pl.BlockSpec(blk, im) for _, blk, im in ins] + c_in_specs,
        out_specs=[pl.BlockSpec(blk, im) for _, _, blk, im in outs] + [ANY] * len(c_out_shapes),
        out_shape=[jax.ShapeDtypeStruct(shp, dt) for shp, dt, _, _ in outs] + c_out_shapes,
        scratch_shapes=c_sems,
        compiler_params=_params(*(["arbitrary"] * len(grid))))(*[a for a, _, _ in ins], *c_arrays)


def _tile_call(name, fn, M, N, tm, tn, *, pairs=(), tiles=(), consts=(), outs=(), parts=0, comm=None):
    gi, gj = M // tm, N // tn
    assert gi * tm == M and gj * tn == N, (name, M, N, tm, tn)
    ins, prs = [], []
    for a, a_col, b, kind in pairs:
        K = b.shape[0] if kind == "nn" else b.shape[1]
        ins.append((a, (tm, K), lambda i, j, c=a_col: (i, c)))
        if kind == "nn":
            ins.append((b, (K, tn), lambda i, j: (0, j)))
        else:
            ins.append((b, (tn, K), lambda i, j: (j, 0)))
        prs.append((len(ins) - 2, len(ins) - 1, kind))
    for arr, off in tiles:
        ins.append((arr, (tm, tn), lambda i, j, o=off: (i, j + o)))
    for arr in consts:
        ins.append((arr, arr.shape, lambda i, j: (0, 0)))
    out_l = [((M, N), dt, (tm, tn), lambda i, j: (i, j)) for dt in outs]
    out_l += [((gi * 8, N), F32, (8, tn), lambda i, j: (i, j))] * parts
    nt = len(tiles)

    def wrapped(accs, vals):
        return fn(accs, vals[:nt], vals[nt:])

    return _call(name, wrapped, (gi, gj), ins, out_l, prs, comm=comm)


def _mm_tn(name, grid, a_in, b_in, outs):
    nk = grid[2]
    tm = [d for d in a_in[1] if d is not None][1]
    tn = [d for d in b_in[1] if d is not None][1]

    def body(a_ref, b_ref, *rest):
        out_refs, acc_ref = rest[:-1], rest[-1]
        k = pl.program_id(2)

        @pl.when(k == 0)
        def _():
            acc_ref[...] = jnp.zeros_like(acc_ref)

        acc_ref[...] += _dot_tn(a_ref[...].astype(BF16), b_ref[...].astype(BF16))

        @pl.when(k == nk - 1)
        def _():
            for o_ref in out_refs:
                o_ref[...] = acc_ref[...].astype(o_ref.dtype)

    return pl.pallas_call(
        body, name=name, grid=grid,
        in_specs=[pl.BlockSpec(a_in[1], a_in[2]), pl.BlockSpec(b_in[1], b_in[2])],
        out_specs=[pl.BlockSpec(blk, im) for _, _, blk, im in outs],
        out_shape=[jax.ShapeDtypeStruct(shp, dt) for shp, dt, _, _ in outs],
        scratch_shapes=[pltpu.VMEM((tm, tn), F32)],
        compiler_params=_params("arbitrary", "arbitrary", "arbitrary"))(a_in[0], b_in[0])


def _grad_pair(shape, block, imap):
    return [(shape, F32, block, imap), (shape, BF16, block, imap)]


def _mm_tn_rows(name, a, b, tk=1024):
    T, a_w = a.shape
    b_w = b.shape[1]
    tm = _pick(a_w, 1408, 128)
    tk = _pick(T, tk, 128)
    g32, g16 = _mm_tn(name, (a_w // tm, 1, T // tk),
                      (a, (tk, tm), lambda i, j, k: (k, i)), (b, (tk, b_w), lambda i, j, k: (k, 0)),
                      _grad_pair((a_w, b_w), (tm, b_w), lambda i, j, k: (i, 0)))
    shp = (N_CHIPS, a_w // N_CHIPS, b_w)
    return g32.reshape(shp), g16.reshape(shp)


def _mm_tn_cols(name, a, b, tk=1024):
    T, a_w = a.shape
    n = b.shape[1] // N_CHIPS
    tk = _pick(T, tk, 128)
    return _mm_tn(name, (1, N_CHIPS, T // tk),
                  (a, (tk, a_w), lambda i, j, k: (k, 0)), (b, (tk, n), lambda i, j, k: (k, j)),
                  _grad_pair((N_CHIPS, a_w, n), (None, a_w, n), lambda i, j, k: (j, 0, 0)))


def _colsum(name, x):
    def body(x_ref, o_ref):
        o_ref[...] = jnp.sum(x_ref[...], axis=0, keepdims=True)
    return pl.pallas_call(body, name=name, out_shape=jax.ShapeDtypeStruct((1, x.shape[1]), F32))(x)


def _rms_hat(h):
    return h * lax.rsqrt(jnp.mean(h * h, axis=-1, keepdims=True) + EPS)


def _rms_bwd_vals(dn, h, g):
    r = lax.rsqrt(jnp.mean(h * h, axis=-1, keepdims=True) + EPS)
    nh = h * r
    gd = dn * g
    dh = r * (gd - nh * jnp.mean(gd * nh, axis=-1, keepdims=True))
    return dh, _group8(dn * nh)


def _rms_fwd(name, h, g, tm=512):
    T = h.shape[0]

    def fn(accs, tv, cv):
        return [_rms_hat(tv[0]) * cv[0]]

    return _tile_call(name, fn, T, D_MODEL, _pick(T, tm), D_MODEL, tiles=[(h, 0)], consts=[g], outs=[BF16])[0]


def _ffn_fwd(tag, h, g, w_in, w_out, comm_in=None, comm_out=None, w_out_of=None):
    T = h.shape[0]
    n = _rms_fwd(tag + "_norm", h, g)
    tm = _pick(T, 512)
    wblk = (None, D_MODEL, FF_SHARD)

    def act(accs, vals):
        gate, up = accs
        return [gate, up, gate * _sigmoid(gate) * up]

    tile = lambda: ((T, D_FF), BF16, (tm, FF_SHARD), lambda i, j: (i, j))
    gate, up, a, *got_in = _call(
        tag + "_in", act, (T // tm, 2),
        [(n, (tm, D_MODEL), lambda i, j: (i, 0)),
         (w_in, wblk, lambda i, j: (j, 0, 0)), (w_in, wblk, lambda i, j: (j + 2, 0, 0))],
        [tile(), tile(), tile()], pairs=[(0, 1, "nn"), (0, 2, "nn")], comm=comm_in)

    def res(accs, tv, cv):
        return [tv[0] + 0.5 * accs[0]]

    if w_out_of is not None:
        w_out = w_out_of(got_in)
    h_new, *got_out = _tile_call(tag + "_out", res, T, D_MODEL, _pick(T, 512), 512,
                                 pairs=[(a, 0, w_out, "nn")], tiles=[(h, 0)], outs=[F32], comm=comm_out)
    return h_new, (n, gate, up, a), got_out


def _ffn_bwd(tag, dh_out, h, g, w_in, w_out, saved, comm=None, comm_last=None):
    T = h.shape[0]
    n, gate, up, a = saved
    tm = _pick(T, 512)

    def half(accs, tv, cv):
        return [0.5 * tv[0]]

    df = _tile_call(tag + "_df", half, T, D_MODEL, _pick(T, 512), D_MODEL, tiles=[(dh_out, 0)], outs=[BF16])[0]

    def dact(accs, vals):
        da = accs[0]
        gt, u = vals[0].astype(F32), vals[1].astype(F32)
        sg = _sigmoid(gt)
        silu = gt * sg
        return [jnp.stack([da * u * (sg + silu * (1.0 - sg)), da * silu])]

    dz, *got = _call(
        tag + "_dact", dact, (T // tm, 2),
        [(df, (tm, D_MODEL), lambda i, j: (i, 0)), (w_out, (FF_SHARD, D_MODEL), lambda i, j: (j, 0)),
         (gate, (tm, FF_SHARD), lambda i, j: (i, j)), (up, (tm, FF_SHARD), lambda i, j: (i, j))],
        [((2, T, D_FF), BF16, (2, tm, FF_SHARD), lambda i, j: (0, i, j))], pairs=[(0, 1, "nt")], comm=comm)
    dw_out = _mm_tn_rows(tag + "_dwout", a, df)
    tk = _pick(T, 1024, 128)
    dw_in = _mm_tn(tag + "_dwin", (1, N_CHIPS, T // tk),
                   (n, (tk, D_MODEL), lambda i, j, k: (k, 0)),
                   (dz, (None, tk, FF_SHARD), lambda i, j, k: (j // 2, k, j % 2)),
                   _grad_pair((N_CHIPS, D_MODEL, FF_SHARD), (None, D_MODEL, FF_SHARD), lambda i, j, k: (j, 0, 0)))

    def dnorm(accs, vals):
        dn = accs[0] + accs[1] + accs[2] + accs[3]
        dh, dg = _rms_bwd_vals(dn, vals[0], vals[2])
        return [vals[1] + dh, dg]

    tm2 = _pick(T, 256)
    ins = [(dz, (None, tm2, FF_SHARD), lambda i, j, s=s: (s // 2, i, s % 2)) for s in range(N_CHIPS)]
    ins += [(w_in, (None, D_MODEL, FF_SHARD), lambda i, j, s=s: (s, 0, 0)) for s in range(N_CHIPS)]
    ins += [(h, (tm2, D_MODEL), lambda i, j: (i, 0)), (dh_out, (tm2, D_MODEL), lambda i, j: (i, 0)),
            (g, g.shape, lambda i, j: (0, 0))]
    dh, dg, *got_last = _call(
        tag + "_dnorm", dnorm, (T // tm2, 1), ins,
        [((T, D_MODEL), F32, (tm2, D_MODEL), lambda i, j: (i, 0)),
         ((T // tm2 * 8, D_MODEL), F32, (8, D_MODEL), lambda i, j: (i, 0))],
        pairs=[(s, N_CHIPS + s, "nt") for s in range(N_CHIPS)],
        comm=None if comm_last is None else comm_last(dw_in, dw_out))
    return dh, dg, dw_in, dw_out, got, got_last


def _t5_onehot():
    qi = np.arange(ATT_BLOCK)[:, None] + ATT_BLOCK
    kj = np.arange(2 * ATT_BLOCK)[None, :]
    nn = np.maximum(qi - kj, 0)
    max_exact = N_BUCKETS // 2
    large = max_exact + (np.log(np.maximum(nn, 1) / max_exact) / np.log(MAX_DISTANCE / max_exact)
                         * (N_BUCKETS - max_exact)).astype(np.int32)
    large = np.minimum(large, N_BUCKETS - 1)
    bucket = np.where(nn < max_exact, nn, large).astype(np.int32).reshape(-1)
    return (bucket[None, :] == np.arange(N_BUCKETS)[:, None]).astype(np.float32)


def _small_mm(name, a, b, sel):
    def body(a_ref, b_ref, o_ref):
        if sel == "right":
            o_ref[...] = _sel_right(a_ref[...], b_ref[...])
        else:
            o_ref[...] = _sel_left(a_ref[...], b_ref[...])
    return pl.pallas_call(body, name=name, out_shape=jax.ShapeDtypeStruct((a.shape[0], b.shape[1]), F32),
                          compiler_params=pltpu.CompilerParams(vmem_limit_bytes=V7X_VMEM_LIMIT))(a, b)


def _dup_heads(t):
    a, b = t[:, :HEAD_DIM], t[:, HEAD_DIM:]
    return jnp.concatenate([a, a, b, b], axis=1)


def _fold_heads(t):
    return jnp.concatenate([t[:, 0:64] + t[:, 64:128], t[:, 128:192] + t[:, 192:256]], axis=1)


def _kv_layouts(proj):
    T = proj.shape[0]

    def fn(accs, tv, cv):
        return [tv[0], tv[1]]

    k, v = _tile_call("kv_cast", fn, T, 128, _pick(T, 1024), 128, tiles=[(proj, COL_AK), (proj, COL_AV)],
                      outs=[BF16, BF16])
    return _dup_heads(k), _dup_heads(v)


def _swa_masks():
    row = lax.broadcasted_iota(jnp.int32, (ATT_BLOCK, 2 * ATT_BLOCK), 0)
    col = lax.broadcasted_iota(jnp.int32, (ATT_BLOCK, 2 * ATT_BLOCK), 1)
    dist = ATT_BLOCK + row - col
    return (dist >= 0) & (dist < ATT_BLOCK), col


GROUP = 4


def _stack_group(blk, lo_q):
    zero = jnp.zeros_like(blk[:, :128])
    rows = []
    for pair in range(GROUP // 2):
        pb = blk[:, 128 * pair:128 * (pair + 1)]
        rows += [jnp.where(lo_q, pb, zero), jnp.where(lo_q, zero, pb)]
    return jnp.concatenate(rows, axis=0)


def _unstack_group(st, lo_q):
    pairs = [jnp.where(lo_q, st[256 * pair:256 * pair + 128], st[256 * pair + 128:256 * (pair + 1)])
             for pair in range(GROUP // 2)]
    return jnp.concatenate(pairs, axis=1)


def _swa_probs(s, bias_h, sink, valid):
    s = jnp.where(valid, s * (HEAD_DIM ** -0.5) + bias_h, -jnp.inf)
    m = jnp.maximum(jnp.max(s, axis=-1, keepdims=True), sink)
    e = jnp.exp(s - m)
    es = jnp.exp(sink - m)
    den = jnp.sum(e, axis=-1, keepdims=True) + es
    return e / den, es / den


def _swa_fwd(proj, kk2, vv2, bias, sinks, B, S):
    T = B * S
    nb = S // ATT_BLOCK

    def body(q_ref, k_ref, v_ref, bias_ref, sink_ref, o_ref, kpad, vpad):
        zeros = jnp.zeros((ATT_BLOCK, 256), BF16)
        kpad[pl.ds(0, ATT_BLOCK), :] = zeros
        vpad[pl.ds(0, ATT_BLOCK), :] = zeros
        kpad[pl.ds(ATT_BLOCK, S), :] = k_ref[...]
        vpad[pl.ds(ATT_BLOCK, S), :] = v_ref[...]
        valid0, col = _swa_masks()
        lo_q = lax.broadcasted_iota(jnp.int32, (1, 128), 1) < HEAD_DIM

        def blk(n, carry):
            r0 = pl.multiple_of(n * ATT_BLOCK, ATT_BLOCK)
            rows = pl.ds(r0, ATT_BLOCK)
            valid = valid0 & ((n > 0) | (col >= ATT_BLOCK))
            for g in range(N_Q_HEADS // GROUP):
                lanes = pl.ds(128 * g, 128)
                kg = kpad[pl.ds(r0, 2 * ATT_BLOCK), lanes]
                vg = vpad[pl.ds(r0, 2 * ATT_BLOCK), lanes]
                qm = _stack_group(q_ref[rows, pl.ds(256 * g, 256)].astype(BF16), lo_q)
                s = _dot_nt(qm, kg)
                ps = []
                for i in range(GROUP):
                    h = GROUP * g + i
                    p, _ = _swa_probs(s[ATT_BLOCK * i:ATT_BLOCK * (i + 1)], bias_ref[h], sink_ref[h], valid)
                    ps.append(p.astype(BF16))
                o = _dot(jnp.concatenate(ps, axis=0), vg)
                o_ref[rows, pl.ds(256 * g, 256)] = _unstack_group(o, lo_q).astype(o_ref.dtype)
            return carry

        lax.fori_loop(0, nb, blk, 0)

    return pl.pallas_call(
        body, name="swa_fwd", grid=(B,),
        in_specs=[pl.BlockSpec((S, 512), lambda b: (b, 0)),
                  pl.BlockSpec((S, 256), lambda b: (b, 0)),
                  pl.BlockSpec((S, 256), lambda b: (b, 0)),
                  pl.BlockSpec((N_Q_HEADS, ATT_BLOCK, 2 * ATT_BLOCK), lambda b: (0, 0, 0)),
                  pl.BlockSpec(memory_space=pltpu.SMEM)],
        out_specs=pl.BlockSpec((S, 512), lambda b: (b, 0)),
        out_shape=jax.ShapeDtypeStruct((T, 512), BF16),
        scratch_shapes=[pltpu.VMEM((S + ATT_BLOCK, 256), BF16), pltpu.VMEM((S + ATT_BLOCK, 256), BF16)],
        compiler_params=_params("arbitrary"))(proj, kk2, vv2, bias, sinks)


def _swa_bwd(proj, kk2, vv2, bias, sinks, datt, B, S):
    T = B * S
    nb = S // ATT_BLOCK

    def body(q_ref, k_ref, v_ref, bias_ref, sink_ref, do_ref, dq_ref, dk_ref, dv_ref, dbias_ref, dsink_ref,
             kpad, vpad, dkpad, dvpad):
        b = pl.program_id(0)

        @pl.when(b == 0)
        def _():
            dbias_ref[...] = jnp.zeros_like(dbias_ref)
            dsink_ref[...] = jnp.zeros_like(dsink_ref)

        zeros = jnp.zeros((ATT_BLOCK, 256), BF16)
        kpad[pl.ds(0, ATT_BLOCK), :] = zeros
        vpad[pl.ds(0, ATT_BLOCK), :] = zeros
        kpad[pl.ds(ATT_BLOCK, S), :] = k_ref[...]
        vpad[pl.ds(ATT_BLOCK, S), :] = v_ref[...]
        dkpad[...] = jnp.zeros_like(dkpad)
        dvpad[...] = jnp.zeros_like(dvpad)
        valid0, col = _swa_masks()
        lo_q = lax.broadcasted_iota(jnp.int32, (1, 128), 1) < HEAD_DIM
        scale = HEAD_DIM ** -0.5

        def blk(n, carry):
            r0 = pl.multiple_of(n * ATT_BLOCK, ATT_BLOCK)
            rows = pl.ds(r0, ATT_BLOCK)
            band = pl.ds(r0, 2 * ATT_BLOCK)
            valid = valid0 & ((n > 0) | (col >= ATT_BLOCK))
            for g in range(N_Q_HEADS // GROUP):
                lanes = pl.ds(128 * g, 128)
                kg = kpad[band, lanes]
                vg = vpad[band, lanes]
                qm = _stack_group(q_ref[rows, pl.ds(256 * g, 256)].astype(BF16), lo_q)
                dom = _stack_group(do_ref[rows, pl.ds(256 * g, 256)], lo_q)
                s = _dot_nt(qm, kg)
                dp = _dot_nt(dom, vg)
                pst, dst = [], []
                for i in range(GROUP):
                    h = GROUP * g + i
                    sl = slice(ATT_BLOCK * i, ATT_BLOCK * (i + 1))
                    p, ps = _swa_probs(s[sl], bias_ref[h], sink_ref[h], valid)
                    delta = jnp.sum(p * dp[sl], axis=-1, keepdims=True)
                    ds = p * (dp[sl] - delta)
                    dbias_ref[h] += ds
                    dsink_ref[pl.ds(h, 1), :] += -jnp.sum(jnp.broadcast_to(ps * delta, (ATT_BLOCK, 128)),
                                                          axis=0, keepdims=True)
                    pst.append(p.astype(BF16))
                    dst.append((ds * scale).astype(BF16))
                pst, dst = jnp.concatenate(pst, axis=0), jnp.concatenate(dst, axis=0)
                dq_ref[rows, pl.ds(256 * g, 256)] = _unstack_group(_dot(dst, kg), lo_q).astype(dq_ref.dtype)
                dkpad[band, lanes] += _dot_tn(dst, qm)
                dvpad[band, lanes] += _dot_tn(pst, dom)
            return carry

        lax.fori_loop(0, nb, blk, 0)
        dk_ref[...] = dkpad[pl.ds(ATT_BLOCK, S), :]
        dv_ref[...] = dvpad[pl.ds(ATT_BLOCK, S), :]

    return pl.pallas_call(
        body, name="swa_bwd", grid=(B,),
        in_specs=[pl.BlockSpec((S, 512), lambda b: (b, 0)),
                  pl.BlockSpec((S, 256), lambda b: (b, 0)),
                  pl.BlockSpec((S, 256), lambda b: (b, 0)),
                  pl.BlockSpec((N_Q_HEADS, ATT_BLOCK, 2 * ATT_BLOCK), lambda b: (0, 0, 0)),
                  pl.BlockSpec(memory_space=pltpu.SMEM),
                  pl.BlockSpec((S, 512), lambda b: (b, 0))],
        out_specs=[pl.BlockSpec((S, 512), lambda b: (b, 0)),
                   pl.BlockSpec((S, 256), lambda b: (b, 0)),
                   pl.BlockSpec((S, 256), lambda b: (b, 0)),
                   pl.BlockSpec((N_Q_HEADS, ATT_BLOCK, 2 * ATT_BLOCK), lambda b: (0, 0, 0)),
                   pl.BlockSpec((N_Q_HEADS, 128), lambda b: (0, 0))],
        out_shape=[jax.ShapeDtypeStruct((T, 512), BF16),
                   jax.ShapeDtypeStruct((T, 256), F32),
                   jax.ShapeDtypeStruct((T, 256), F32),
                   jax.ShapeDtypeStruct((N_Q_HEADS, ATT_BLOCK, 2 * ATT_BLOCK), F32),
                   jax.ShapeDtypeStruct((N_Q_HEADS, 128), F32)],
        scratch_shapes=[pltpu.VMEM((S + ATT_BLOCK, 256), BF16), pltpu.VMEM((S + ATT_BLOCK, 256), BF16),
                        pltpu.VMEM((S + ATT_BLOCK, 256), F32), pltpu.VMEM((S + ATT_BLOCK, 256), F32)],
        compiler_params=_params("arbitrary"))(proj, kk2, vv2, bias, sinks, datt)


def _hgrn_gates(z, lb):
    sg = _sigmoid(z)
    f = lb + (1.0 - lb) * sg
    return sg, f, jnp.log(f), 1.0 - f


def _hgrn_consts():
    r = lax.broadcasted_iota(jnp.int32, (CHUNK, CHUNK), 0)
    c = lax.broadcasted_iota(jnp.int32, (CHUNK, CHUNK), 1)
    tril = (r >= c).astype(BF16)
    triu = (r <= c).astype(BF16)
    causal = r >= c
    below = (r // SUB) > (c // SUB)
    inside = ((r // SUB) == (c // SUB)) & causal
    return tril, triu, causal, below, inside, c


def _block_rows(ref, lanes, s):
    rows = []
    for i in range(N_SUB):
        if SUB * i + s < 0:
            rows.append(jnp.zeros((SUB, REC_DIM), F32))
        else:
            rows.append(jnp.broadcast_to(ref[pl.ds(SUB * i + s, 1), lanes], (SUB, REC_DIM)))
    return jnp.concatenate(rows, axis=0)


def _hgrn_offdiag(q, k, bcum, b_ref, lanes):
    eq = jnp.exp(jnp.minimum(bcum - _block_rows(b_ref, lanes, -1), 0.0))
    qe = q * eq
    zero = jnp.zeros((SUB, REC_DIM), F32)
    q_rows, k_cols, eks = [jnp.zeros((SUB, (N_SUB - 1) * REC_DIM), F32)], [], []
    for i in range(1, N_SUB):
        q_rows.append(jnp.concatenate([zero] * (i - 1) + [qe[SUB * i:SUB * (i + 1), :]] + [zero] * (N_SUB - 1 - i),
                                      axis=1))
        p = b_ref[pl.ds(SUB * i - 1, 1), lanes]
        pad = jnp.zeros((CHUNK - SUB * i, REC_DIM), F32)
        ek = jnp.concatenate([jnp.exp(p - b_ref[pl.ds(0, SUB * i), lanes]), pad], axis=0)
        k_cols.append(k * ek)
        eks.append(ek)
    return jnp.concatenate(q_rows, axis=0), jnp.concatenate(k_cols, axis=1), eq, eks


def _hgrn_fwd(proj, lb_param, B, S):
    T = B * S
    nc = S // CHUNK

    def body(q_ref, z_ref, v_ref, lb_ref, o_ref, st_ref, k_s, b_s):
        tril, _, _, below, inside, col = _hgrn_consts()
        col_s = col & (SUB - 1)

        def chunk(ci, hts):
            r0 = pl.multiple_of(ci * CHUNK, CHUNK)
            lb = _sigmoid(lb_ref[0:1, :] - lb_ref[1:2, :])
            _, _, g_all, k_all = _hgrn_gates(z_ref[pl.ds(r0, CHUNK), :], lb)
            b_all = _sel_left(tril, g_all)
            k_s[...] = k_all
            b_s[...] = b_all
            new = []
            for e, ht in enumerate(hts):
                lanes = pl.ds(REC_DIM * e, REC_DIM)
                cols = slice(REC_DIM * e, REC_DIM * (e + 1))
                q = q_ref[pl.ds(r0, CHUNK), lanes]
                v = v_ref[pl.ds(r0, CHUNK), lanes]
                k, bcum = k_all[:, cols], b_all[:, cols]
                st_ref[e * nc + ci] = ht
                qst, kst, _, _ = _hgrn_offdiag(q, k, bcum, b_s, lanes)
                d = jnp.zeros((CHUNK, CHUNK), F32)
                for s in range(SUB):
                    w = jnp.exp(jnp.minimum(bcum - _block_rows(b_s, lanes, s), 0.0))
                    colv = jnp.sum(q * _block_rows(k_s, lanes, s) * w, axis=-1, keepdims=True)
                    d = jnp.where(col_s == s, colv, d)
                a = jnp.where(below, _dot_nt(qst.astype(BF16), kst.astype(BF16)), 0.0) + jnp.where(inside, d, 0.0)
                vb = v.astype(BF16)
                qb = (q * jnp.exp(bcum)).astype(BF16)
                o_ref[pl.ds(r0, CHUNK), lanes] = _dot(a.astype(BF16), vb) + _dot_nt(qb, ht.astype(BF16))
                b_last = b_s[pl.ds(CHUNK - 1, 1), lanes]
                kb = (k * jnp.exp(b_last - bcum)).astype(BF16)
                new.append(ht * jnp.exp(b_last) + _dot_tn(vb, kb))
            return tuple(new)

        lax.fori_loop(0, nc, chunk, tuple(jnp.zeros((REC_DIM, REC_DIM), F32) for _ in range(HGRN_PAIR)))

    hp, wd = REC_HEADS // HGRN_PAIR, HGRN_PAIR * REC_DIM
    cq, cf, ci_ = (c * REC_DIM // wd for c in (COL_RQ, COL_RF, COL_RI))
    return pl.pallas_call(
        body, name="hgrn_fwd", grid=(B, hp),
        in_specs=[pl.BlockSpec((S, wd), lambda b, h: (b, cq + h)),
                  pl.BlockSpec((S, wd), lambda b, h: (b, cf + h)),
                  pl.BlockSpec((S, wd), lambda b, h: (b, ci_ + h)),
                  pl.BlockSpec((2, wd), lambda b, h: (0, h))],
        out_specs=[pl.BlockSpec((S, wd), lambda b, h: (b, h)),
                   pl.BlockSpec((HGRN_PAIR * nc, REC_DIM, REC_DIM), lambda b, h: (b * hp + h, 0, 0))],
        out_shape=[jax.ShapeDtypeStruct((T, 512), F32),
                   jax.ShapeDtypeStruct((B * REC_HEADS * nc, REC_DIM, REC_DIM), F32)],
        scratch_shapes=[pltpu.VMEM((CHUNK, wd), F32), pltpu.VMEM((CHUNK, wd), F32)],
        compiler_params=_params("arbitrary", "arbitrary"))(proj, proj, proj, lb_param)


def _hgrn_bwd(proj, lb_param, states, do, B, S, comm=None):
    T = B * S
    nc = S // CHUNK

    c_arrays, c_in_specs, c_out_shapes, c_sems = _comm_parts(comm)
    nci, nco = len(c_arrays), len(c_out_shapes)

    def body(*refs):
        q_ref, z_ref, v_ref, lb_ref, st_ref, do_ref = refs[:6]
        dq_ref, dz_ref, dv_ref, dlb_ref = refs[6 + nci:10 + nci]
        k_s, b_s = refs[10 + nci + nco:12 + nci + nco]
        comm_first, comm_last = _comm_run(comm, (B, REC_HEADS // HGRN_PAIR), refs, 6, 4)
        comm_first()
        tril, triu, causal, below, inside, col = _hgrn_consts()
        col_s = col & (SUB - 1)
        last_row = lax.broadcasted_iota(jnp.int32, (CHUNK, 1), 0) == CHUNK - 1
        rc = lax.broadcasted_iota(jnp.int32, (CHUNK, SUB * REC_DIM), 0)
        lc = lax.broadcasted_iota(jnp.int32, (CHUNK, SUB * REC_DIM), 1)
        spread = ((rc & (SUB - 1)) == (lc // REC_DIM)).astype(BF16)
        rr = lax.broadcasted_iota(jnp.int32, (CHUNK, SUB * CHUNK), 0)
        cc = lax.broadcasted_iota(jnp.int32, (CHUNK, SUB * CHUNK), 1)
        gather = (((rr // SUB) == ((cc & (CHUNK - 1)) // SUB)) & ((rr & (SUB - 1)) == (cc // CHUNK))).astype(BF16)

        heads = range(HGRN_PAIR)
        cols = [slice(REC_DIM * e, REC_DIM * (e + 1)) for e in heads]
        lanes = [pl.ds(REC_DIM * e, REC_DIM) for e in heads]
        lane_cat = lambda vals: jnp.concatenate(vals, axis=1)
        row_cat = lambda vals: jnp.concatenate(vals, axis=0)

        def chunk(it, carry):
            dhts, dlb = carry
            ci = nc - 1 - it
            r0 = pl.multiple_of(ci * CHUNK, CHUNK)
            rows = pl.ds(r0, CHUNK)
            lb = _sigmoid(lb_ref[0:1, :] - lb_ref[1:2, :])
            sg, f, g_all, k_all = _hgrn_gates(z_ref[rows, :], lb)
            b_all = _sel_left(tril, g_all)
            k_s[...] = k_all
            b_s[...] = b_all
            q_all = q_ref[rows, :]
            das, hd = [], []
            for e in heads:
                vb, dob = v_ref[rows, lanes[e]].astype(BF16), do_ref[rows, lanes[e]].astype(BF16)
                da = jnp.where(causal, _dot_nt(dob, vb), 0.0)
                das.append(jnp.where(inside, da, 0.0))
                hd.append((vb, dob, da))
            da_hi, da_lo = _split2(row_cat(das))
            da_in = _dot(da_hi, spread) + _dot(da_lo, spread)
            ds, dqs, pieces, pieces_lo = [], [], [[] for _ in range(SUB)], [[] for _ in range(SUB)]
            for e in heads:
                q, bcum = q_all[:, cols[e]], b_all[:, cols[e]]
                d = jnp.zeros((CHUNK, CHUNK), F32)
                dq = jnp.zeros((CHUNK, REC_DIM), F32)
                for s in range(SUB):
                    w = jnp.exp(jnp.minimum(bcum - _block_rows(b_s, lanes[e], s), 0.0))
                    ks = _block_rows(k_s, lanes[e], s)
                    qw = q * w
                    d = jnp.where(col_s == s, jnp.sum(qw * ks, axis=-1, keepdims=True), d)
                    da_s = da_in[CHUNK * e:CHUNK * (e + 1), REC_DIM * s:REC_DIM * (s + 1)]
                    dq = dq + da_s * ks * w
                    hi, lo = _split2(da_s * qw)
                    pieces[s].append(hi)
                    pieces_lo[s].append(lo)
                ds.append(d)
                dqs.append(dq)
            dk_in = (_dot(gather, row_cat([lane_cat(p) for p in pieces]))
                     + _dot(gather, row_cat([lane_cat(p) for p in pieces_lo])))
            dq_out, dk_out, dv_out, db_out, new_dhts = [], [], [], [], []
            for e in heads:
                q, k, bcum = q_all[:, cols[e]], k_all[:, cols[e]], b_all[:, cols[e]]
                vb, dob, da = hd[e]
                dht, ht = dhts[e], st_ref[e * nc + ci]
                qst, kst, eq, eks = _hgrn_offdiag(q, k, bcum, b_s, lanes[e])
                qst_b, kst_b = qst.astype(BF16), kst.astype(BF16)
                a = jnp.where(below, _dot_nt(qst_b, kst_b), 0.0) + jnp.where(inside, ds[e], 0.0)
                da_off = jnp.where(below, da, 0.0).astype(BF16)
                dqst = _dot(da_off, kst_b)
                dkst = _dot_tn(da_off, qst_b)
                dk = dk_in[:, cols[e]]
                dq_rows = [jnp.zeros((SUB, REC_DIM), F32)]
                for i in range(1, N_SUB):
                    dq_rows.append(dqst[SUB * i:SUB * (i + 1), REC_DIM * (i - 1):REC_DIM * i])
                    dk = dk + dkst[:, REC_DIM * (i - 1):REC_DIM * i] * eks[i - 1]
                dq = dqs[e] + row_cat(dq_rows) * eq
                eb = jnp.exp(bcum)
                b_last = b_s[pl.ds(CHUNK - 1, 1), lanes[e]]
                el = jnp.exp(b_last)
                ekb = jnp.exp(b_last - bcum)
                qb = (q * eb).astype(BF16)
                kb = k * ekb
                dhb = dht.astype(BF16)
                dv_out.append(_dot_tn(a.astype(BF16), dob) + _dot_nt(kb.astype(BF16), dhb))
                dqb = _dot(dob, ht.astype(BF16))
                dkb = _dot(vb, dhb)
                new_dhts.append(dht * el + _dot_tn(dob, qb))
                dq = dq + eb * dqb
                dk = dk + ekb * dkb
                edge = jnp.sum(kb * dkb, axis=0, keepdims=True) + el * jnp.sum(ht * dht, axis=0, keepdims=True)
                db_out.append(q * dq - k * dk + jnp.where(last_row, edge, 0.0))
                dq_out.append(dq)
                dk_out.append(dk)
            dk_all = lane_cat(dk_out)
            db_hi, db_lo = _split2(lane_cat(db_out))
            dg = _dot(triu, db_hi) + _dot(triu, db_lo)
            df = dg / f - dk_all
            dz_ref[rows, :] = (df * (1.0 - lb) * sg * (1.0 - sg)).astype(dz_ref.dtype)
            dq_ref[rows, :] = lane_cat(dq_out).astype(dq_ref.dtype)
            dv_ref[rows, :] = lane_cat(dv_out).astype(dv_ref.dtype)
            return tuple(new_dhts), dlb + jnp.sum(df * (1.0 - sg), axis=0, keepdims=True)

        zero = (tuple(jnp.zeros((REC_DIM, REC_DIM), F32) for _ in heads), jnp.zeros((1, HGRN_PAIR * REC_DIM), F32))
        _, dlb = lax.fori_loop(0, nc, chunk, zero)
        lb = _sigmoid(lb_ref[0:1, :] - lb_ref[1:2, :])
        dlb_ref[...] = jnp.broadcast_to(dlb * lb * (1.0 - lb), (8, HGRN_PAIR * REC_DIM))
        comm_last()

    hp, wd = REC_HEADS // HGRN_PAIR, HGRN_PAIR * REC_DIM
    cq, cf, ci_ = (c * REC_DIM // wd for c in (COL_RQ, COL_RF, COL_RI))
    return pl.pallas_call(
        body, name="hgrn_bwd", grid=(B, hp),
        in_specs=[pl.BlockSpec((S, wd), lambda b, h: (b, cq + h)),
                  pl.BlockSpec((S, wd), lambda b, h: (b, cf + h)),
                  pl.BlockSpec((S, wd), lambda b, h: (b, ci_ + h)),
                  pl.BlockSpec((2, wd), lambda b, h: (0, h)),
                  pl.BlockSpec((HGRN_PAIR * nc, REC_DIM, REC_DIM), lambda b, h: (b * hp + h, 0, 0)),
                  pl.BlockSpec((S, wd), lambda b, h: (b, h))] + c_in_specs,
        out_specs=[pl.BlockSpec((S, wd), lambda b, h: (b, h))] * 3
        + [pl.BlockSpec((8, wd), lambda b, h: (b, h))] + [ANY] * nco,
        out_shape=[jax.ShapeDtypeStruct((T, 512), BF16)] * 3 + [jax.ShapeDtypeStruct((B * 8, 512), F32)]
        + c_out_shapes,
        scratch_shapes=[pltpu.VMEM((CHUNK, wd), F32)] * 2 + c_sems,
        compiler_params=_params("arbitrary", "arbitrary"))(proj, proj, proj, lb_param, states, do, *c_arrays)


def _rec_gate_fwd(rec, proj, rec_norm):
    T = rec.shape[0]

    def fn(accs, tv, cv):
        return [_rms_hat(tv[0]) * cv[0] * _sigmoid(tv[1])]

    return _tile_call("rec_gate", fn, T, 512, _pick(T, 1024), REC_DIM, tiles=[(rec, 0), (proj, COL_RG)],
                      consts=[rec_norm], outs=[BF16])[0]


def _rec_gate_bwd(dyb, w_rec_proj, rec, proj, rec_norm):
    T = rec.shape[0]

    def fn(accs, tv, cv):
        d, r, rg = accs[0], tv[0], tv[1]
        sg = _sigmoid(rg)
        rn = _rms_hat(r) * cv[0]
        dh, dg = _rms_bwd_vals(d * sg, r, cv[0])
        return [dh, d * rn * sg * (1.0 - sg), dg]

    return _tile_call("rec_gate_bwd", fn, T, 512, _pick(T, 1024), REC_DIM, pairs=[(dyb, 0, w_rec_proj, "nt")],
                      tiles=[(rec, 0), (proj, COL_RG)], consts=[rec_norm], outs=[F32, BF16], parts=1)


def _mix_out_fwd(att, recn, proj, w_att_proj, w_rec_proj, w_out, h1):
    T = att.shape[0]
    tn = 256

    def merge(accs, tv, cv):
        ya, yb = accs
        return [ya, yb, _sigmoid(tv[0]) * ya + _sigmoid(tv[1]) * yb]

    ya, yb, merged = _tile_call(
        "merge", merge, T, D_MODEL, _pick(T, 1024), tn,
        pairs=[(att, 0, w_att_proj, "nn"), (recn, 0, w_rec_proj, "nn")],
        tiles=[(proj, COL_GA * 128 // tn), (proj, COL_GB * 128 // tn)], outs=[BF16] * 3)

    def res(accs, tv, cv):
        return [tv[0] + accs[0]]

    h2 = _tile_call("mix_out", res, T, D_MODEL, _pick(T, 512), 512, pairs=[(merged, 0, w_out, "nn")],
                    tiles=[(h1, 0)], outs=[F32])[0]
    return h2, (ya, yb, merged)


GATHER_FIRST = ("w_ffn1_in",)
GATHER_MIX = ("w_ffn1_out", "w_in", "w_att_proj", "w_rec_proj", "w_out")
GATHER_LAST = ("w_ffn2_in", "w_ffn2_out", "w_ple_gate", "w_ple_proj")
SCATTER_LATE = ("w_ple_gate", "w_ple_proj", "w_ffn2_in", "w_ffn2_out")
SCATTER_MIX = ("w_out", "w_att_proj", "w_rec_proj", "w_in")
SCATTER_LAST = ("w_ffn1_in", "w_ffn1_out")


def _local_step(x, p, tgt, w, mine16, cc, me_chip, B, S):
    T = B * S
    w = dict(w)
    g_ffn1, g_mix, g_ffn2, g_ple = w["norm_ffn1"], w["norm_mix"], w["norm_ffn2"], w["norm_ple"]
    g_fin = w["norm_final"].reshape(1, D_MODEL)
    grads, part, from_chips = {}, {}, {}

    def gather(names):
        return _gather_comm([mine16[n] for n in names])

    def place(names, got):
        for n, g in zip(names, got):
            full = lax.dynamic_update_index_in_dim(g, mine16[n], me_chip, 0)
            w[n] = full if n in ("w_ffn1_in", "w_ffn2_in") else _natural(n, full)

    def scatter(tag, names):
        from_sib = _swap_halves("rs_sibling_" + tag, [grads[n][1] for n in names])
        for n, fs in zip(names, from_sib):
            part[n] = _add_sibling("rs_add_sib_" + n, grads[n][0], fs, cc)
        return _scatter_comm([part[n][1] for n in names])

    def scattered(names, got):
        for n, g in zip(names, got):
            from_chips[n] = g

    place(GATHER_FIRST, _run_comm("gather_first", gather(GATHER_FIRST)))
    def ffn1_out_weight(got):
        place(GATHER_MIX, got)
        return w["w_ffn1_out"]

    h1, sv1, got_last = _ffn_fwd("ffn1", x, g_ffn1, w["w_ffn1_in"], None, comm_in=gather(GATHER_MIX),
                                 comm_out=gather(GATHER_LAST), w_out_of=ffn1_out_weight)
    place(GATHER_LAST, got_last)
    u = _rms_fwd("mix_norm", h1, g_mix)

    def ident(accs, tv, cv):
        return [accs[0]]

    proj = _tile_call("in_proj", ident, T, IN_W, _pick(T, 1024), 256, pairs=[(u, 0, w["w_in"], "nn")],
                      outs=[F32])[0]
    onehot = jnp.asarray(_t5_onehot())
    bias = _small_mm("t5_bias", w["rel_bias"].T, onehot.astype(BF16), "right")
    bias = bias.reshape(N_Q_HEADS, ATT_BLOCK, 2 * ATT_BLOCK)
    sinks = w["attn_sinks"].reshape(N_Q_HEADS)
    kk2, vv2 = _kv_layouts(proj)
    att = _swa_fwd(proj, kk2, vv2, bias, sinks, B, S)
    rec, states = _hgrn_fwd(proj, w["lb_param"], B, S)
    recn = _rec_gate_fwd(rec, proj, w["rec_norm"])
    h2, (ya, yb, merged) = _mix_out_fwd(att, recn, proj, w["w_att_proj"], w["w_rec_proj"], w["w_out"], h1)
    h3, sv2, _ = _ffn_fwd("ffn2", h2, g_ffn2, w["w_ffn2_in"], w["w_ffn2_out"])
    n3 = _rms_fwd("ple_norm", h3, g_ple)

    def ple(accs, tv, cv):
        gate = _sigmoid(accs[0])
        return [gate, accs[1], tv[0] + gate * accs[1]]

    gate_p, pp, h4 = _tile_call(
        "ple", ple, T, D_MODEL, _pick(T, 512), 512,
        pairs=[(n3, 0, w["w_ple_gate"], "nn"), (p, 0, w["w_ple_proj"], "nn")], tiles=[(h3, 0)],
        outs=[BF16, BF16, F32])

    def head(accs, tv, cv):
        h, t = tv
        err = _rms_hat(h) * cv[0] - t
        dh, dg = _rms_bwd_vals(err * (1.0 / D_MODEL), h, cv[0])
        return [dh, _group8(err * err), dg]

    dh4, loss_p, dg_fin = _tile_call("loss_head", head, T, D_MODEL, _pick(T, 256), D_MODEL,
                                     tiles=[(h4, 0), (tgt, 0)], consts=[g_fin], outs=[F32], parts=2)
    grads["norm_final"] = dg_fin

    def dple(accs, tv, cv):
        d, gt, ppv = tv[0], tv[1].astype(F32), tv[2].astype(F32)
        return [d * ppv * gt * (1.0 - gt), d * gt]

    dzg, dpp = _tile_call("ple_dact", dple, T, D_MODEL, _pick(T, 512), D_MODEL,
                          tiles=[(dh4, 0), (gate_p, 0), (pp, 0)], outs=[BF16] * 2)
    grads["w_ple_gate"] = _mm_tn_rows("ple_dwg", n3, dzg)
    grads["w_ple_proj"] = _mm_tn_cols("ple_dwp", p, dpp)

    def dnorm(accs, tv, cv):
        dh, dg = _rms_bwd_vals(accs[0], tv[0], cv[0])
        return [tv[1] + dh, dg]

    dh3, grads["norm_ple"] = _tile_call(
        "ple_dnorm", dnorm, T, D_MODEL, _pick(T, 256), D_MODEL, pairs=[(dzg, 0, w["w_ple_gate"], "nt")],
        tiles=[(h3, 0), (dh4, 0)], consts=[g_ple], outs=[F32], parts=1)

    dh2, grads["norm_ffn2"], grads["w_ffn2_in"], grads["w_ffn2_out"], _, _ = _ffn_bwd(
        "ffn2b", dh3, h2, g_ffn2, w["w_ffn2_in"], w["w_ffn2_out"], sv2)
    scatter_late = scatter("late", SCATTER_LATE)

    def to_bf(accs, tv, cv):
        return [tv[0]]

    dh2b = _tile_call("mix_dcast", to_bf, T, D_MODEL, _pick(T, 512), D_MODEL, tiles=[(dh2, 0)], outs=[BF16])[0]
    grads["w_out"] = _mm_tn_rows("mix_dwout", merged, dh2b)
    tn = 256

    def dmerge(accs, tv, cv):
        dm = accs[0]
        sa, sb = _sigmoid(tv[0]), _sigmoid(tv[1])
        yav, ybv = tv[2].astype(F32), tv[3].astype(F32)
        return [dm * sa, dm * sb, dm * yav * sa * (1.0 - sa), dm * ybv * sb * (1.0 - sb)]

    dya, dyb, dga, dgb = _tile_call(
        "mix_dmerge", dmerge, T, D_MODEL, _pick(T, 1024), tn, pairs=[(dh2b, 0, w["w_out"], "nt")],
        tiles=[(proj, COL_GA * 128 // tn), (proj, COL_GB * 128 // tn), (ya, 0), (yb, 0)], outs=[BF16] * 4)
    grads["w_att_proj"] = _mm_tn_cols("mix_dwatt", att, dya)
    grads["w_rec_proj"] = _mm_tn_cols("mix_dwrec", recn, dyb)

    datt = _tile_call("mix_datt", ident, T, 512, _pick(T, 1024), 512, pairs=[(dya, 0, w["w_att_proj"], "nt")],
                      outs=[BF16])[0]
    drec, drg, grads["rec_norm"] = _rec_gate_bwd(dyb, w["w_rec_proj"], rec, proj, w["rec_norm"])

    drq, drf, dri, dlb, *got = _hgrn_bwd(proj, w["lb_param"], states, drec, B, S, comm=scatter_late)
    scattered(SCATTER_LATE, got)
    grads["lb_param"] = dlb
    daq, dk2, dv2, dbias, dsink = _swa_bwd(proj, kk2, vv2, bias, sinks, datt, B, S)
    grads["attn_sinks"] = dsink
    grads["rel_bias"] = _small_mm("t5_dbias", dbias.reshape(N_Q_HEADS, -1), onehot.T.astype(BF16), "right")
    dak = _fold_heads(dk2).astype(BF16)
    dav = _fold_heads(dv2).astype(BF16)
    dproj = jnp.concatenate([daq, dak, dav, drq, drf, dri, drg, dga, dgb], axis=1)
    tk = _pick(T, 512, 128)
    w_in_shard = IN_W // N_CHIPS
    gw32, gw16 = _mm_tn("mix_dwin", (1, 2, T // tk),
                        (u, (tk, D_MODEL), lambda i, j, k: (k, 0)), (dproj, (tk, IN_W // 2), lambda i, j, k: (k, j)),
                        _grad_pair((D_MODEL, IN_W), (D_MODEL, IN_W // 2), lambda i, j, k: (0, j)))
    to_sh = lambda t: t.reshape(D_MODEL, N_CHIPS, w_in_shard).transpose(1, 0, 2)
    grads["w_in"] = (to_sh(gw32), to_sh(gw16))
    scatter_mix = scatter("mix", SCATTER_MIX)

    def dnorm_mix(accs, tv, cv):
        dh, dg = _rms_bwd_vals(accs[0], tv[0], cv[0])
        return [tv[1] + dh, dg]

    dh1, grads["norm_mix"] = _tile_call(
        "mix_dnorm", dnorm_mix, T, D_MODEL, _pick(T, 256), D_MODEL, pairs=[(dproj, 0, w["w_in"], "nt")],
        tiles=[(h1, 0), (dh2, 0)], consts=[g_mix], outs=[F32], parts=1)

    def scatter_last(dw_in, dw_out):
        grads["w_ffn1_in"], grads["w_ffn1_out"] = dw_in, dw_out
        return scatter("last", SCATTER_LAST)

    dx, grads["norm_ffn1"], _, _, got, got_last = _ffn_bwd(
        "ffn1b", dh1, x, g_ffn1, w["w_ffn1_in"], w["w_ffn1_out"], sv1, comm=scatter_mix, comm_last=scatter_last)
    scattered(SCATTER_MIX, got)
    scattered(SCATTER_LAST, got_last)
    return loss_p, dx, grads, part, from_chips


def _place():
    x, y, c = lax.axis_index("x"), lax.axis_index("y"), lax.axis_index("c")
    return x, y, c


def _other_chips(x, y):
    return [(1 - x, y, 2 * (1 - x) + y), (x, 1 - y, 2 * x + 1 - y), (1 - x, 1 - y, 2 * (1 - x) + 1 - y)]


def _half_rows(ref_3d, chip, h, rows):
    return ref_3d.at[chip, pl.ds(h * rows, rows), :]


def _run_comm(name, comm):
    nci, nco = len(comm.ins), len(comm.out_shapes)

    def body(*refs):
        cin, cout, send_sems, recv_sems = refs[:nci], refs[nci:nci + nco], refs[-2], refs[-1]
        comm.start(cin, cout, send_sems, recv_sems)
        comm.finish(cin, cout, send_sems, recv_sems)

    return pl.pallas_call(
        body, name=name, in_specs=[ANY] * nci, out_specs=[ANY] * nco, out_shape=list(comm.out_shapes),
        scratch_shapes=[pltpu.SemaphoreType.DMA((comm.n_sems,)), pltpu.SemaphoreType.DMA((comm.n_sems,))],
    )(*comm.ins)


def _gather_comm(ws):
    nw = len(ws)

    def parts(w_refs, out_refs, send_sems, recv_sems):
        x, y, c = _place()
        me = 2 * x + y
        chips = _other_chips(x, y)

        def copy(i, k, chip, h, to, src=None):
            half = ws[i].shape[0] // 2
            dst = _half_rows(out_refs[i], chip, h, half)
            return pltpu.make_async_remote_copy(
                src_ref=dst if src is None else src, dst_ref=dst,
                send_sem=send_sems.at[6 * i + k], recv_sem=recv_sems.at[6 * i + k], device_id=to, device_id_type=MESH)

        def first():
            out = []
            for i in range(nw):
                half = ws[i].shape[0] // 2
                out += [copy(i, j, me, c, (cx, cy, c), src=w_refs[i].at[pl.ds(c * half, half), :])
                        for j, (cx, cy, _) in enumerate(chips)]
            return out

        return copy, first, chips, c, (x, y, 1 - c)

    def start(*refs):
        _, first, _, _, _ = parts(*refs)
        for cp in first():
            cp.start()

    def finish(*refs):
        copy, first, chips, c, sibling = parts(*refs)
        passed = []
        for i in range(nw):
            for j, (cx, cy, ci) in enumerate(chips):
                copy(i, j, ci, c, (cx, cy, c)).wait_recv()
                fw = copy(i, 3 + j, ci, c, sibling)
                fw.start()
                passed.append(fw)
        for i in range(nw):
            for j, (_, _, ci) in enumerate(chips):
                copy(i, 3 + j, ci, 1 - c, sibling).wait_recv()
        for cp in first() + passed:
            cp.wait_send()

    return _Comm(list(ws), [jax.ShapeDtypeStruct((N_CHIPS,) + w.shape, w.dtype) for w in ws], 6 * nw, start, finish)


def _scatter_comm(ps):
    nw = len(ps)

    def copies(p_refs, out_refs, send_sems, recv_sems):
        x, y, c = _place()
        cps = []
        for i in range(nw):
            for j, (cx, cy, ci) in enumerate(_other_chips(x, y)):
                cps.append(pltpu.make_async_remote_copy(
                    src_ref=p_refs[i].at[ci], dst_ref=out_refs[i].at[j], send_sem=send_sems.at[3 * i + j],
                    recv_sem=recv_sems.at[3 * i + j], device_id=(cx, cy, c), device_id_type=MESH))
        return cps

    def start(*refs):
        for cp in copies(*refs):
            cp.start()

    def finish(*refs):
        for cp in copies(*refs):
            cp.wait()

    return _Comm(list(ps), [jax.ShapeDtypeStruct((3,) + p.shape[1:], p.dtype) for p in ps], 3 * nw, start, finish)


def _swap_halves(name, gs):
    nw = len(gs)

    def body(*refs):
        g_refs, out_refs, send_sems, recv_sems = refs[:nw], refs[nw:2 * nw], refs[2 * nw], refs[2 * nw + 1]
        x, y, c = _place()
        cps = []
        for i in range(nw):
            half = gs[i].shape[1] // 2
            cps.append(pltpu.make_async_remote_copy(
                src_ref=g_refs[i].at[:, pl.ds((1 - c) * half, half), :], dst_ref=out_refs[i],
                send_sem=send_sems.at[i], recv_sem=recv_sems.at[i], device_id=(x, y, 1 - c), device_id_type=MESH))
        for cp in cps:
            cp.start()
        for cp in cps:
            cp.wait()

    return pl.pallas_call(
        body, name=name, in_specs=[ANY] * nw, out_specs=[ANY] * nw,
        out_shape=[jax.ShapeDtypeStruct((N_CHIPS, g.shape[1] // 2, g.shape[2]), g.dtype) for g in gs],
        scratch_shapes=[pltpu.SemaphoreType.DMA((nw,)), pltpu.SemaphoreType.DMA((nw,))],
    )(*gs)


def _join_halves(name, ss):
    nw = len(ss)

    def body(*refs):
        s_refs, out_refs, send_sems, recv_sems = refs[:nw], refs[nw:2 * nw], refs[2 * nw], refs[2 * nw + 1]
        x, y, c = _place()
        cps = [pltpu.make_async_remote_copy(
            src_ref=s_refs[i], dst_ref=out_refs[i], send_sem=send_sems.at[i], recv_sem=recv_sems.at[i],
            device_id=(x, y, 1 - c), device_id_type=MESH) for i in range(nw)]
        for cp in cps:
            cp.start()
        for cp in cps:
            cp.wait()

    return pl.pallas_call(
        body, name=name, in_specs=[ANY] * nw, out_specs=[ANY] * nw,
        out_shape=[jax.ShapeDtypeStruct(s.shape, s.dtype) for s in ss],
        scratch_shapes=[pltpu.SemaphoreType.DMA((nw,)), pltpu.SemaphoreType.DMA((nw,))],
    )(*ss)


def _allreduce_small(sp):
    def body(s_ref, out_ref, slots, send_sems, recv_sems):
        x, y, c = _place()
        me = 4 * x + 2 * y + c
        slots[me] = s_ref[...]
        cps = []
        for r in range(1, N_DEV):
            px, py, pc = x ^ (r >> 2), y ^ ((r >> 1) & 1), c ^ (r & 1)
            cps.append(pltpu.make_async_remote_copy(
                src_ref=s_ref, dst_ref=slots.at[me], send_sem=send_sems.at[r - 1], recv_sem=recv_sems.at[r - 1],
                device_id=(px, py, pc), device_id_type=MESH))
        for cp in cps:
            cp.start()
        for r in range(1, N_DEV):
            px, py, pc = x ^ (r >> 2), y ^ ((r >> 1) & 1), c ^ (r & 1)
            pltpu.make_async_remote_copy(
                src_ref=s_ref, dst_ref=slots.at[4 * px + 2 * py + pc], send_sem=send_sems.at[r - 1],
                recv_sem=recv_sems.at[r - 1], device_id=(px, py, pc), device_id_type=MESH).wait_recv()
        for cp in cps:
            cp.wait_send()
        acc = slots[0]
        for d in range(1, N_DEV):
            acc = acc + slots[d]
        out_ref[...] = acc

    return pl.pallas_call(
        body, name="allreduce_small",
        in_specs=[pl.BlockSpec(memory_space=pltpu.VMEM)], out_specs=pl.BlockSpec(memory_space=pltpu.VMEM),
        out_shape=jax.ShapeDtypeStruct(sp.shape, F32),
        scratch_shapes=[pltpu.VMEM((N_DEV,) + sp.shape, F32), pltpu.SemaphoreType.DMA((N_DEV - 1,)),
                        pltpu.SemaphoreType.DMA((N_DEV - 1,))],
    )(sp)


def _scalar(v):
    return jnp.reshape(v, (1,)).astype(jnp.int32)


def _row_tile(h, dtype_mult=16):
    return _pick(h, 256, dtype_mult)


def _add_sibling(name, g32, from_sib, c):
    _, r, n = g32.shape
    h = r // 2
    th = _row_tile(h)
    nt = h // th

    def body(c_ref, g_ref, s_ref, o32_ref, o16_ref):
        s = g_ref[...] + s_ref[...].astype(F32)
        o32_ref[...] = s
        o16_ref[...] = s.astype(BF16)

    blk = (None, th, n)
    return pl.pallas_call(
        body, name=name,
        grid_spec=pltpu.PrefetchScalarGridSpec(
            num_scalar_prefetch=1, grid=(N_CHIPS, nt),
            in_specs=[pl.BlockSpec(blk, lambda k, t, c_ref: (k, c_ref[0] * nt + t, 0)),
                      pl.BlockSpec(blk, lambda k, t, c_ref: (k, t, 0))],
            out_specs=[pl.BlockSpec(blk, lambda k, t, c_ref: (k, t, 0))] * 2),
        out_shape=[jax.ShapeDtypeStruct((N_CHIPS, h, n), F32), jax.ShapeDtypeStruct((N_CHIPS, h, n), BF16)],
        compiler_params=_params("arbitrary", "arbitrary"))(_scalar(c), g32, from_sib)


def _add_chips(name, p32, from_chips, me_chip):
    _, h, n = p32.shape
    th = _row_tile(h)

    def body(m_ref, p_ref, a_ref, b_ref, c_ref, o_ref):
        o_ref[...] = p_ref[...] + a_ref[...].astype(F32) + b_ref[...].astype(F32) + c_ref[...].astype(F32)

    blk = (None, th, n)
    return pl.pallas_call(
        body, name=name,
        grid_spec=pltpu.PrefetchScalarGridSpec(
            num_scalar_prefetch=1, grid=(h // th,),
            in_specs=[pl.BlockSpec(blk, lambda t, m_ref: (m_ref[0], t, 0))]
            + [pl.BlockSpec(blk, lambda t, m_ref, j=j: (j, t, 0)) for j in range(3)],
            out_specs=pl.BlockSpec((th, n), lambda t, m_ref: (t, 0))),
        out_shape=jax.ShapeDtypeStruct((h, n), F32),
        compiler_params=_params("arbitrary"))(_scalar(me_chip), p32, from_chips, from_chips, from_chips)


def _adamw_vals(w, g, m, v):
    m = ADAM_B1 * m + (1.0 - ADAM_B1) * g
    v = ADAM_B2 * v + (1.0 - ADAM_B2) * (g * g)
    m_hat = m / (1.0 - ADAM_B1 ** ADAM_STEP)
    v_hat = v / (1.0 - ADAM_B2 ** ADAM_STEP)
    delta = -ADAM_LR * (m_hat / (jnp.sqrt(v_hat) + ADAM_EPS) + ADAM_WD * w)
    return delta, m, v


def _adamw_halves(name, w, m, v, g_mine, g_sib, c):
    r, n = w.shape
    h = r // 2
    th = _row_tile(h, 8)
    nt = h // th

    def body(c_ref, w_ref, m_ref, v_ref, a_ref, b_ref, g_ref, d_ref, nm_ref, nv_ref):
        mine = (pl.program_id(0) // nt) == c_ref[0]
        g = jnp.where(mine, a_ref[...], b_ref[...])
        d, nm, nv = _adamw_vals(w_ref[...], g, m_ref[...], v_ref[...])
        g_ref[...] = g
        d_ref[...] = d
        nm_ref[...] = nm
        nv_ref[...] = nv

    full = pl.BlockSpec((th, n), lambda t, c_ref: (t, 0))
    part = pl.BlockSpec((th, n), lambda t, c_ref: (t % nt, 0))
    return pl.pallas_call(
        body, name=name,
        grid_spec=pltpu.PrefetchScalarGridSpec(
            num_scalar_prefetch=1, grid=(2 * nt,), in_specs=[full, full, full, part, part], out_specs=[full] * 4),
        out_shape=[jax.ShapeDtypeStruct((r, n), F32)] * 4,
        compiler_params=_params("arbitrary"))(_scalar(c), w, m, v, g_mine, g_sib)


def _adamw(name, w, g, m, v):
    R, W = w.shape

    def fn(accs, tv, cv):
        return list(_adamw_vals(*tv))

    return _tile_call(name, fn, R, W, _pick(R, 256), W, tiles=[(w, 0), (g, 0), (m, 0), (v, 0)], outs=[F32] * 3)


SMALL_LAYOUT = (("rel_bias", 2, 256), ("lb_param", 8, 1024), ("norm_ffn1", 8, 1024), ("norm_mix", 8, 1024),
                ("attn_sinks", 1, 8), ("rec_norm", 1, 128), ("norm_ffn2", 8, 1024), ("norm_ple", 8, 1024),
                ("norm_final", 8, 1024), ("loss", 8, 1024))


def _pack_small(vals):
    rows = []
    for name, nrows, n in SMALL_LAYOUT:
        flat = vals[name].reshape(-1)
        flat = jnp.pad(flat, (0, nrows * 128 - n))
        rows.append(flat.reshape(nrows, 128))
    packed = jnp.concatenate(rows, axis=0)
    return jnp.pad(packed, ((0, SMALL_ROWS - packed.shape[0]), (0, 0)))


def _unpack_small(packed, shapes):
    out, r = {}, 0
    for name, nrows, n in SMALL_LAYOUT:
        out[name] = packed[r:r + nrows].reshape(-1)[:n].reshape(shapes[name])
        r += nrows
    return out


def _natural(name, s):
    if name in COL_SHARDED:
        return s.transpose(1, 0, 2).reshape(s.shape[1], -1)
    return s.reshape(-1, s.shape[2])


def kernel(x, p, rel_bias, lb_param, norm_ffn1, w_ffn1_in, w_ffn1_out, norm_mix, w_in, attn_sinks, rec_norm, w_att_proj, w_rec_proj, w_out, norm_ffn2, w_ffn2_in, w_ffn2_out, norm_ple, w_ple_gate, w_ple_proj, norm_final, loss_target, m_rel_bias, m_lb_param, m_norm_ffn1, m_w_ffn1_in, m_w_ffn1_out, m_norm_mix, m_w_in, m_attn_sinks, m_rec_norm, m_w_att_proj, m_w_rec_proj, m_w_out, m_norm_ffn2, m_w_ffn2_in, m_w_ffn2_out, m_norm_ple, m_w_ple_gate, m_w_ple_proj, m_norm_final, v_rel_bias, v_lb_param, v_norm_ffn1, v_w_ffn1_in, v_w_ffn1_out, v_norm_mix, v_w_in, v_attn_sinks, v_rec_norm, v_w_att_proj, v_w_rec_proj, v_w_out, v_norm_ffn2, v_w_ffn2_in, v_w_ffn2_out, v_norm_ple, v_w_ple_gate, v_w_ple_proj, v_norm_final):
    args = dict(locals())
    wsh = {n: args[n] for n in WEIGHTS}
    B, S = x.shape[0], x.shape[1]
    T = B * S
    cx, cy, cc = _place()
    me_chip = 2 * cx + cy

    mine16 = {n: wsh[n][0].astype(BF16) for n in BIG}
    loss_p, dx, grads, part, from_chips = _local_step(
        x.reshape(T, D_MODEL), p.reshape(T, PLE_DIM), loss_target.reshape(T, D_MODEL),
        {n: wsh[n] for n in SMALL}, mine16, cc, me_chip, B, S)

    s_mine = [_add_chips("rs_add_chips_" + n, part[n][0], from_chips[n], me_chip) for n in BIG]
    s_sib = _join_halves("rs_join", s_mine)

    small_vals = {
        "rel_bias": grads["rel_bias"].T,
        "lb_param": jnp.concatenate([_colsum("dlb_sum", grads["lb_param"]),
                                     -_colsum("dlb_sum2", grads["lb_param"])], axis=0) / 8.0,
        "attn_sinks": grads["attn_sinks"][:, 0],
        "rec_norm": _colsum("drn_sum", grads["rec_norm"]).reshape(REC_HEADS, REC_DIM).sum(axis=0),
        "loss": _colsum("loss_sum", loss_p),
    }
    for n in ("norm_ffn1", "norm_mix", "norm_ffn2", "norm_ple", "norm_final"):
        small_vals[n] = _colsum(n + "_sum", grads[n])
    red = _allreduce_small(_pack_small(small_vals))
    small_shapes = {n: wsh[n].shape for n in SMALL}
    small_shapes["loss"] = (D_MODEL,)
    small = _unpack_small(red, small_shapes)
    loss = 0.5 * jnp.sum(small["loss"]) / D_MODEL

    out_g, out_d, out_m, out_v = {}, {}, {}, {}
    for n, gm, gs in zip(BIG, s_mine, s_sib):
        res = _adamw_halves("adamw_" + n, wsh[n][0], args["m_" + n][0], args["v_" + n][0], gm, gs, cc)
        out_g[n], out_d[n], out_m[n], out_v[n] = (t[None] for t in res)
    sw = _pack_small({**{n: wsh[n] for n in SMALL}, "loss": jnp.zeros((D_MODEL,), F32)})
    sm = _pack_small({**{n: args["m_" + n] for n in SMALL}, "loss": jnp.zeros((D_MODEL,), F32)})
    sv = _pack_small({**{n: args["v_" + n] for n in SMALL}, "loss": jnp.ones((D_MODEL,), F32)})
    sd, snm, snv = _adamw("adamw_small", sw, red, sm, sv)
    ud, um, uv = (_unpack_small(t, small_shapes) for t in (sd, snm, snv))
    for n in SMALL:
        out_g[n], out_d[n], out_m[n], out_v[n] = small[n], ud[n], um[n], uv[n]

    return (loss, dx.reshape(B, S, D_MODEL), *[out_g[n] for n in WEIGHTS], *[out_d[n] for n in WEIGHTS],
            *[out_m[n] for n in WEIGHTS], *[out_v[n] for n in WEIGHTS])
```

```python
import numpy as np
import jax
import jax.numpy as jnp
from jax import lax
from jax.experimental import pallas as pl
from jax.experimental.pallas import tpu as pltpu

F32 = jnp.float32
BF16 = jnp.bfloat16
MESH = pl.DeviceIdType.MESH

D_MODEL = 1024
D_FF = 2816
FF_SHARD = 2 * D_FF // 4
HEAD_DIM = 64
N_Q_HEADS = 8
ATT_BLOCK = 128
N_BUCKETS = 32
MAX_DISTANCE = 128
REC_HEADS = 4
REC_DIM = 128
PLE_DIM = 256
EPS = 1e-6
IN_W = 4864
COL_AQ, COL_AK, COL_AV, COL_RQ, COL_RF, COL_RI, COL_RG, COL_GA, COL_GB = 0, 4, 5, 6, 10, 14, 18, 22, 30

CHUNK = 64
SUB = 8
N_SUB = CHUNK // SUB
HGRN_PAIR = 2

ADAM_LR, ADAM_B1, ADAM_B2, ADAM_EPS, ADAM_WD, ADAM_STEP = 0.001, 0.9, 0.999, 1e-08, 0.01, 10

V7X_VMEM_LIMIT = 56 * 1024 * 1024
N_CHIPS = 4
N_DEV = 8

BIG = ("w_ffn1_in", "w_ffn1_out", "w_in", "w_att_proj", "w_rec_proj", "w_out",
       "w_ffn2_in", "w_ffn2_out", "w_ple_gate", "w_ple_proj")
COL_SHARDED = ("w_ffn1_in", "w_in", "w_att_proj", "w_rec_proj", "w_ffn2_in", "w_ple_proj")
WEIGHTS = ("rel_bias", "lb_param", "norm_ffn1", "w_ffn1_in", "w_ffn1_out", "norm_mix", "w_in", "attn_sinks",
           "rec_norm", "w_att_proj", "w_rec_proj", "w_out", "norm_ffn2", "w_ffn2_in", "w_ffn2_out", "norm_ple",
           "w_ple_gate", "w_ple_proj", "norm_final")
SMALL = tuple(n for n in WEIGHTS if n not in BIG)
SMALL_ROWS = 64


def _params(*sem):
    return pltpu.CompilerParams(dimension_semantics=sem, vmem_limit_bytes=V7X_VMEM_LIMIT)


def _pick(n, cap, mult=8):
    if n <= cap:
        return n
    for t in range(cap - cap % mult, 0, -mult):
        if n % t == 0:
            return t
    raise ValueError((n, cap, mult))


def _dot(a, b):
    return jnp.dot(a, b, preferred_element_type=F32)


def _dot_nt(a, b):
    return lax.dot_general(a, b, (((1,), (1,)), ((), ())), preferred_element_type=F32)


def _dot_tn(a, b):
    return lax.dot_general(a, b, (((0,), (0,)), ((), ())), preferred_element_type=F32)


def _split3(x):
    hi = x.astype(BF16)
    r = x - hi.astype(F32)
    mid = r.astype(BF16)
    lo = (r - mid.astype(F32)).astype(BF16)
    return hi, mid, lo


def _split2(x):
    hi = x.astype(BF16)
    return hi, (x - hi.astype(F32)).astype(BF16)


def _sel_left(sel_bf16, x):
    hi, mid, lo = _split3(x)
    return _dot(sel_bf16, hi) + _dot(sel_bf16, mid) + _dot(sel_bf16, lo)


def _sel_right(x, sel_bf16):
    hi, mid, lo = _split3(x)
    return _dot(hi, sel_bf16) + _dot(mid, sel_bf16) + _dot(lo, sel_bf16)


def _sigmoid(x):
    return 1.0 / (1.0 + jnp.exp(-x))


def _group8(x):
    r, w = x.shape
    return x.reshape(r // 8, 8, w).sum(axis=0)


class _Comm:
    def __init__(self, ins, out_shapes, n_sems, start, finish):
        self.ins, self.out_shapes, self.n_sems, self.start, self.finish = ins, out_shapes, n_sems, start, finish


ANY = pl.BlockSpec(memory_space=pl.ANY)


def _comm_parts(comm):
    if comm is None:
        return [], [], [], []
    sems = [pltpu.SemaphoreType.DMA((comm.n_sems,)), pltpu.SemaphoreType.DMA((comm.n_sems,))]
    return list(comm.ins), [ANY] * len(comm.ins), list(comm.out_shapes), sems


def _comm_run(comm, grid, refs, n_in, n_out):
    if comm is None:
        return (lambda: None), (lambda: None)
    nci, nco = len(comm.ins), len(comm.out_shapes)
    cin = refs[n_in:n_in + nci]
    cout = refs[n_in + nci + n_out:n_in + nci + n_out + nco]
    send_sems, recv_sems = refs[-2], refs[-1]
    ids = [pl.program_id(d) for d in range(len(grid))]
    is_first = ids[0] == 0
    is_last = ids[0] == grid[0] - 1
    for d in range(1, len(grid)):
        is_first = is_first & (ids[d] == 0)
        is_last = is_last & (ids[d] == grid[d] - 1)

    def first():
        @pl.when(is_first)
        def _():
            comm.start(cin, cout, send_sems, recv_sems)

    def last():
        @pl.when(is_last)
        def _():
            comm.finish(cin, cout, send_sems, recv_sems)

    return first, last


def _call(name, fn, grid, ins, outs, pairs=(), comm=None):
    in_pair = {i for p in pairs for i in p[:2]}
    n_in, n_out = len(ins), len(outs)
    c_arrays, c_in_specs, c_out_shapes, c_sems = _comm_parts(comm)

    def body(*refs):
        first, last = _comm_run(comm, grid, refs, n_in, n_out)
        first()
        accs = []
        for ia, ib, kind in pairs:
            a, b = refs[ia][...].astype(BF16), refs[ib][...].astype(BF16)
            accs.append(_dot(a, b) if kind == "nn" else _dot_nt(a, b))
        vals = [refs[i][...] for i in range(n_in) if i not in in_pair]
        res = fn(accs, vals)
        out_refs = refs[n_in + len(c_arrays):n_in + len(c_arrays) + n_out]
        assert len(res) == len(out_refs), (name, len(res), len(out_refs))
        for o_ref, val in zip(out_refs, res):
            o_ref[...] = val.astype(o_ref.dtype)
        last()

    return pl.pallas_call(
        body, name=name, grid=grid,
        in_specs=[pl.BlockSpec(blk, im) for _, blk, im in ins] + c_in_specs,
        out_specs=[pl.BlockSpec(blk, im) for _, _, blk, im in outs] + [ANY] * len(c_out_shapes),
        out_shape=[jax.ShapeDtypeStruct(shp, dt) for shp, dt, _, _ in outs] + c_out_shapes,
        scratch_shapes=c_sems,
        compiler_params=_params(*(["arbitrary"] * len(grid))))(*[a for a, _, _ in ins], *c_arrays)


def _tile_call(name, fn, M, N, tm, tn, *, pairs=(), tiles=(), consts=(), outs=(), parts=0, comm=None):
    gi, gj = M // tm, N // tn
    assert gi * tm == M and gj * tn == N, (name, M, N, tm, tn)
    ins, prs = [], []
    for a, a_col, b, kind in pairs:
        K = b.shape[0] if kind == "nn" else b.shape[1]
        ins.append((a, (tm, K), lambda i, j, c=a_col: (i, c)))
        if kind == "nn":
            ins.append((b, (K, tn), lambda i, j: (0, j)))
        else:
            ins.append((b, (tn, K), lambda i, j: (j, 0)))
        prs.append((len(ins) - 2, len(ins) - 1, kind))
    for arr, off in tiles:
        ins.append((arr, (tm, tn), lambda i, j, o=off: (i, j + o)))
    for arr in consts:
        ins.append((arr, arr.shape, lambda i, j: (0, 0)))
    out_l = [((M, N), dt, (tm, tn), lambda i, j: (i, j)) for dt in outs]
    out_l += [((gi * 8, N), F32, (8, tn), lambda i, j: (i, j))] * parts
    nt = len(tiles)

    def wrapped(accs, vals):
        return fn(accs, vals[:nt], vals[nt:])

    return _call(name, wrapped, (gi, gj), ins, out_l, prs, comm=comm)


def _mm_tn(name, grid, a_in, b_in, outs):
    nk = grid[2]
    tm = [d for d in a_in[1] if d is not None][1]
    tn = [d for d in b_in[1] if d is not None][1]

    def body(a_ref, b_ref, *rest):
        out_refs, acc_ref = rest[:-1], rest[-1]
        k = pl.program_id(2)

        @pl.when(k == 0)
        def _():
            acc_ref[...] = jnp.zeros_like(acc_ref)

        acc_ref[...] += _dot_tn(a_ref[...].astype(BF16), b_ref[...].astype(BF16))

        @pl.when(k == nk - 1)
        def _():
            for o_ref in out_refs:
                o_ref[...] = acc_ref[...].astype(o_ref.dtype)

    return pl.pallas_call(
        body, name=name, grid=grid,
        in_specs=[pl.BlockSpec(a_in[1], a_in[2]), pl.BlockSpec(b_in[1], b_in[2])],
        out_specs=[pl.BlockSpec(blk, im) for _, _, blk, im in outs],
        out_shape=[jax.ShapeDtypeStruct(shp, dt) for shp, dt, _, _ in outs],
        scratch_shapes=[pltpu.VMEM((tm, tn), F32)],
        compiler_params=_params("arbitrary", "arbitrary", "arbitrary"))(a_in[0], b_in[0])


def _grad_pair(shape, block, imap):
    return [(shape, F32, block, imap), (shape, BF16, block, imap)]


def _mm_tn_rows(name, a, b, tk=1024):
    T, a_w = a.shape
    b_w = b.shape[1]
    tm = _pick(a_w, 1408, 128)
    tk = _pick(T, tk, 128)
    g32, g16 = _mm_tn(name, (a_w // tm, 1, T // tk),
                      (a, (tk, tm), lambda i, j, k: (k, i)), (b, (tk, b_w), lambda i, j, k: (k, 0)),
                      _grad_pair((a_w, b_w), (tm, b_w), lambda i, j, k: (i, 0)))
    shp = (N_CHIPS, a_w // N_CHIPS, b_w)
    return g32.reshape(shp), g16.reshape(shp)


def _mm_tn_cols(name, a, b, tk=1024):
    T, a_w = a.shape
    n = b.shape[1] // N_CHIPS
    tk = _pick(T, tk, 128)
    return _mm_tn(name, (1, N_CHIPS, T // tk),
                  (a, (tk, a_w), lambda i, j, k: (k, 0)), (b, (tk, n), lambda i, j, k: (k, j)),
                  _grad_pair((N_CHIPS, a_w, n), (None, a_w, n), lambda i, j, k: (j, 0, 0)))


def _colsum(name, x):
    def body(x_ref, o_ref):
        o_ref[...] = jnp.sum(x_ref[...], axis=0, keepdims=True)
    return pl.pallas_call(body, name=name, out_shape=jax.ShapeDtypeStruct((1, x.shape[1]), F32))(x)


def _rms_hat(h):
    return h * lax.rsqrt(jnp.mean(h * h, axis=-1, keepdims=True) + EPS)


def _rms_bwd_vals(dn, h, g):
    r = lax.rsqrt(jnp.mean(h * h, axis=-1, keepdims=True) + EPS)
    nh = h * r
    gd = dn * g
    dh = r * (gd - nh * jnp.mean(gd * nh, axis=-1, keepdims=True))
    return dh, _group8(dn * nh)


def _rms_fwd(name, h, g, tm=512):
    T = h.shape[0]

    def fn(accs, tv, cv):
        return [_rms_hat(tv[0]) * cv[0]]

    return _tile_call(name, fn, T, D_MODEL, _pick(T, tm), D_MODEL, tiles=[(h, 0)], consts=[g], outs=[BF16])[0]


def _ffn_fwd(tag, h, g, w_in, w_out, g_next, n=None, comm_in=None, comm_out=None, w_out_of=None):
    T = h.shape[0]
    if n is None:
        n = _rms_fwd(tag + "_norm", h, g)
    tm = _pick(T, 512)
    wblk = (None, D_MODEL, FF_SHARD)

    def act(accs, vals):
        gate, up = accs
        return [gate, up, gate * _sigmoid(gate) * up]

    tile = lambda: ((T, D_FF), BF16, (tm, FF_SHARD), lambda i, j: (i, j))
    gate, up, a, *got_in = _call(
        tag + "_in", act, (T // tm, 2),
        [(n, (tm, D_MODEL), lambda i, j: (i, 0)),
         (w_in, wblk, lambda i, j: (j, 0, 0)), (w_in, wblk, lambda i, j: (j + 2, 0, 0))],
        [tile(), tile(), tile()], pairs=[(0, 1, "nn"), (0, 2, "nn")], comm=comm_in)

    def res(accs, tv, cv):
        h_new = tv[0] + 0.5 * accs[0]
        return [h_new, _rms_hat(h_new) * cv[0]]

    if w_out_of is not None:
        w_out = w_out_of(got_in)
    h_new, n_next, *got_out = _tile_call(
        tag + "_out", res, T, D_MODEL, _pick(T, 512), D_MODEL, pairs=[(a, 0, w_out, "nn")], tiles=[(h, 0)],
        consts=[g_next], outs=[F32, BF16], comm=comm_out)
    return h_new, n_next, (n, gate, up, a), got_out


def _ffn_bwd(tag, dh_out, df, h, g, w_in, w_out, saved, comm=None, comm_last=None):
    T = h.shape[0]
    n, gate, up, a = saved
    tm = _pick(T, 512)

    def dact(accs, vals):
        da = accs[0]
        gt, u = vals[0].astype(F32), vals[1].astype(F32)
        sg = _sigmoid(gt)
        silu = gt * sg
        return [jnp.stack([da * u * (sg + silu * (1.0 - sg)), da * silu])]

    dz, *got = _call(
        tag + "_dact", dact, (T // tm, 2),
        [(df, (tm, D_MODEL), lambda i, j: (i, 0)), (w_out, (FF_SHARD, D_MODEL), lambda i, j: (j, 0)),
         (gate, (tm, FF_SHARD), lambda i, j: (i, j)), (up, (tm, FF_SHARD), lambda i, j: (i, j))],
        [((2, T, D_FF), BF16, (2, tm, FF_SHARD), lambda i, j: (0, i, j))], pairs=[(0, 1, "nt")], comm=comm)
    dw_out = _mm_tn_rows(tag + "_dwout", a, df)
    tk = _pick(T, 1024, 128)
    dw_in = _mm_tn(tag + "_dwin", (1, N_CHIPS, T // tk),
                   (n, (tk, D_MODEL), lambda i, j, k: (k, 0)),
                   (dz, (None, tk, FF_SHARD), lambda i, j, k: (j // 2, k, j % 2)),
                   _grad_pair((N_CHIPS, D_MODEL, FF_SHARD), (None, D_MODEL, FF_SHARD), lambda i, j, k: (j, 0, 0)))

    def dnorm(accs, vals):
        dn = accs[0] + accs[1] + accs[2] + accs[3]
        dh, dg = _rms_bwd_vals(dn, vals[0], vals[2])
        dh = vals[1] + dh
        return [dh, dh, dg]

    tm2 = _pick(T, 256)
    ins = [(dz, (None, tm2, FF_SHARD), lambda i, j, s=s: (s // 2, i, s % 2)) for s in range(N_CHIPS)]
    ins += [(w_in, (None, D_MODEL, FF_SHARD), lambda i, j, s=s: (s, 0, 0)) for s in range(N_CHIPS)]
    ins += [(h, (tm2, D_MODEL), lambda i, j: (i, 0)), (dh_out, (tm2, D_MODEL), lambda i, j: (i, 0)),
            (g, g.shape, lambda i, j: (0, 0))]
    dh, dh16, dg, *got_last = _call(
        tag + "_dnorm", dnorm, (T // tm2, 1), ins,
        [((T, D_MODEL), F32, (tm2, D_MODEL), lambda i, j: (i, 0)),
         ((T, D_MODEL), BF16, (tm2, D_MODEL), lambda i, j: (i, 0)),
         ((T // tm2 * 8, D_MODEL), F32, (8, D_MODEL), lambda i, j: (i, 0))],
        pairs=[(s, N_CHIPS + s, "nt") for s in range(N_CHIPS)],
        comm=None if comm_last is None else comm_last(dw_in, dw_out))
    return dh, dh16, dg, dw_in, dw_out, got, got_last


def _t5_onehot():
    qi = np.arange(ATT_BLOCK)[:, None] + ATT_BLOCK
    kj = np.arange(2 * ATT_BLOCK)[None, :]
    nn = np.maximum(qi - kj, 0)
    max_exact = N_BUCKETS // 2
    large = max_exact + (np.log(np.maximum(nn, 1) / max_exact) / np.log(MAX_DISTANCE / max_exact)
                         * (N_BUCKETS - max_exact)).astype(np.int32)
    large = np.minimum(large, N_BUCKETS - 1)
    bucket = np.where(nn < max_exact, nn, large).astype(np.int32).reshape(-1)
    return (bucket[None, :] == np.arange(N_BUCKETS)[:, None]).astype(np.float32)


def _small_mm(name, a, b, sel):
    def body(a_ref, b_ref, o_ref):
        if sel == "right":
            o_ref[...] = _sel_right(a_ref[...], b_ref[...])
        else:
            o_ref[...] = _sel_left(a_ref[...], b_ref[...])
    return pl.pallas_call(body, name=name, out_shape=jax.ShapeDtypeStruct((a.shape[0], b.shape[1]), F32),
                          compiler_params=pltpu.CompilerParams(vmem_limit_bytes=V7X_VMEM_LIMIT))(a, b)


def _dup_heads(t):
    a, b = t[:, :HEAD_DIM], t[:, HEAD_DIM:]
    return jnp.concatenate([a, a, b, b], axis=1)


def _kv_layouts(proj):
    T = proj.shape[0]

    def fn(accs, tv, cv):
        return [tv[0], tv[1]]

    k, v = _tile_call("kv_cast", fn, T, 128, _pick(T, 1024), 128, tiles=[(proj, COL_AK), (proj, COL_AV)],
                      outs=[BF16, BF16])
    return _dup_heads(k), _dup_heads(v)


def _swa_masks():
    row = lax.broadcasted_iota(jnp.int32, (ATT_BLOCK, 2 * ATT_BLOCK), 0)
    col = lax.broadcasted_iota(jnp.int32, (ATT_BLOCK, 2 * ATT_BLOCK), 1)
    dist = ATT_BLOCK + row - col
    return (dist >= 0) & (dist < ATT_BLOCK), col


GROUP = 4


def _stack_group(blk, lo_q):
    zero = jnp.zeros_like(blk[:, :128])
    rows = []
    for pair in range(GROUP // 2):
        pb = blk[:, 128 * pair:128 * (pair + 1)]
        rows += [jnp.where(lo_q, pb, zero), jnp.where(lo_q, zero, pb)]
    return jnp.concatenate(rows, axis=0)


def _unstack_group(st, lo_q):
    pairs = [jnp.where(lo_q, st[256 * pair:256 * pair + 128], st[256 * pair + 128:256 * (pair + 1)])
             for pair in range(GROUP // 2)]
    return jnp.concatenate(pairs, axis=1)


def _swa_probs(s, bias_h, sink, valid):
    s = jnp.where(valid, s * (HEAD_DIM ** -0.5) + bias_h, -jnp.inf)
    m = jnp.maximum(jnp.max(s, axis=-1, keepdims=True), sink)
    e = jnp.exp(s - m)
    es = jnp.exp(sink - m)
    den = jnp.sum(e, axis=-1, keepdims=True) + es
    return e / den, es / den


def _swa_fwd(proj, kk2, vv2, bias, sinks, B, S):
    T = B * S
    nb = S // ATT_BLOCK

    def body(q_ref, k_ref, v_ref, bias_ref, sink_ref, o_ref, kpad, vpad):
        zeros = jnp.zeros((ATT_BLOCK, 256), BF16)
        kpad[pl.ds(0, ATT_BLOCK), :] = zeros
        vpad[pl.ds(0, ATT_BLOCK), :] = zeros
        kpad[pl.ds(ATT_BLOCK, S), :] = k_ref[...]
        vpad[pl.ds(ATT_BLOCK, S), :] = v_ref[...]
        valid0, col = _swa_masks()
        lo_q = lax.broadcasted_iota(jnp.int32, (1, 128), 1) < HEAD_DIM

        def blk(n, carry):
            r0 = pl.multiple_of(n * ATT_BLOCK, ATT_BLOCK)
            rows = pl.ds(r0, ATT_BLOCK)
            valid = valid0 & ((n > 0) | (col >= ATT_BLOCK))
            for g in range(N_Q_HEADS // GROUP):
                lanes = pl.ds(128 * g, 128)
                kg = kpad[pl.ds(r0, 2 * ATT_BLOCK), lanes]
                vg = vpad[pl.ds(r0, 2 * ATT_BLOCK), lanes]
                qm = _stack_group(q_ref[rows, pl.ds(256 * g, 256)].astype(BF16), lo_q)
                s = _dot_nt(qm, kg)
                ps = []
                for i in range(GROUP):
                    h = GROUP * g + i
                    p, _ = _swa_probs(s[ATT_BLOCK * i:ATT_BLOCK * (i + 1)], bias_ref[h], sink_ref[h], valid)
                    ps.append(p.astype(BF16))
                o = _dot(jnp.concatenate(ps, axis=0), vg)
                o_ref[rows, pl.ds(256 * g, 256)] = _unstack_group(o, lo_q).astype(o_ref.dtype)
            return carry

        lax.fori_loop(0, nb, blk, 0)

    return pl.pallas_call(
        body, name="swa_fwd", grid=(B,),
        in_specs=[pl.BlockSpec((S, 512), lambda b: (b, 0)),
                  pl.BlockSpec((S, 256), lambda b: (b, 0)),
                  pl.BlockSpec((S, 256), lambda b: (b, 0)),
                  pl.BlockSpec((N_Q_HEADS, ATT_BLOCK, 2 * ATT_BLOCK), lambda b: (0, 0, 0)),
                  pl.BlockSpec(memory_space=pltpu.SMEM)],
        out_specs=pl.BlockSpec((S, 512), lambda b: (b, 0)),
        out_shape=jax.ShapeDtypeStruct((T, 512), BF16),
        scratch_shapes=[pltpu.VMEM((S + ATT_BLOCK, 256), BF16), pltpu.VMEM((S + ATT_BLOCK, 256), BF16)],
        compiler_params=_params("arbitrary"))(proj, kk2, vv2, bias, sinks)


def _swa_bwd(proj, kk2, vv2, bias, sinks, datt, B, S):
    T = B * S
    nb = S // ATT_BLOCK

    def body(q_ref, k_ref, v_ref, bias_ref, sink_ref, do_ref, dq_ref, dk_ref, dv_ref, dbias_ref, dsink_ref,
             kpad, vpad, dkpad, dvpad):
        b = pl.program_id(0)

        @pl.when(b == 0)
        def _():
            dbias_ref[...] = jnp.zeros_like(dbias_ref)
            dsink_ref[...] = jnp.zeros_like(dsink_ref)

        zeros = jnp.zeros((ATT_BLOCK, 256), BF16)
        kpad[pl.ds(0, ATT_BLOCK), :] = zeros
        vpad[pl.ds(0, ATT_BLOCK), :] = zeros
        kpad[pl.ds(ATT_BLOCK, S), :] = k_ref[...]
        vpad[pl.ds(ATT_BLOCK, S), :] = v_ref[...]
        dkpad[...] = jnp.zeros_like(dkpad)
        dvpad[...] = jnp.zeros_like(dvpad)
        valid0, col = _swa_masks()
        lo_q = lax.broadcasted_iota(jnp.int32, (1, 128), 1) < HEAD_DIM
        scale = HEAD_DIM ** -0.5

        def blk(n, carry):
            r0 = pl.multiple_of(n * ATT_BLOCK, ATT_BLOCK)
            rows = pl.ds(r0, ATT_BLOCK)
            band = pl.ds(r0, 2 * ATT_BLOCK)
            valid = valid0 & ((n > 0) | (col >= ATT_BLOCK))
            for g in range(N_Q_HEADS // GROUP):
                lanes = pl.ds(128 * g, 128)
                kg = kpad[band, lanes]
                vg = vpad[band, lanes]
                qm = _stack_group(q_ref[rows, pl.ds(256 * g, 256)].astype(BF16), lo_q)
                dom = _stack_group(do_ref[rows, pl.ds(256 * g, 256)], lo_q)
                s = _dot_nt(qm, kg)
                dp = _dot_nt(dom, vg)
                pst, dst = [], []
                for i in range(GROUP):
                    h = GROUP * g + i
                    sl = slice(ATT_BLOCK * i, ATT_BLOCK * (i + 1))
                    p, ps = _swa_probs(s[sl], bias_ref[h], sink_ref[h], valid)
                    delta = jnp.sum(p * dp[sl], axis=-1, keepdims=True)
                    ds = p * (dp[sl] - delta)
                    dbias_ref[h] += ds
                    dsink_ref[pl.ds(h, 1), :] += -jnp.sum(jnp.broadcast_to(ps * delta, (ATT_BLOCK, 128)),
                                                          axis=0, keepdims=True)
                    pst.append(p.astype(BF16))
                    dst.append((ds * scale).astype(BF16))
                pst, dst = jnp.concatenate(pst, axis=0), jnp.concatenate(dst, axis=0)
                dq_ref[rows, pl.ds(256 * g, 256)] = _unstack_group(_dot(dst, kg), lo_q).astype(dq_ref.dtype)
                dkpad[band, lanes] += _dot_tn(dst, qm)
                dvpad[band, lanes] += _dot_tn(pst, dom)
            return carry

        lax.fori_loop(0, nb, blk, 0)
        lo_out = lax.broadcasted_iota(jnp.int32, (1, 128), 1) < HEAD_DIM

        def fold(pad_ref):
            halves = []
            for g in range(N_Q_HEADS // GROUP):
                t = pad_ref[pl.ds(ATT_BLOCK, S), pl.ds(128 * g, 128)]
                halves.append(t + pltpu.roll(t, HEAD_DIM, 1))
            return jnp.where(lo_out, halves[0], halves[1])

        dk_ref[...] = fold(dkpad).astype(dk_ref.dtype)
        dv_ref[...] = fold(dvpad).astype(dv_ref.dtype)

    return pl.pallas_call(
        body, name="swa_bwd", grid=(B,),
        in_specs=[pl.BlockSpec((S, 512), lambda b: (b, 0)),
                  pl.BlockSpec((S, 256), lambda b: (b, 0)),
                  pl.BlockSpec((S, 256), lambda b: (b, 0)),
                  pl.BlockSpec((N_Q_HEADS, ATT_BLOCK, 2 * ATT_BLOCK), lambda b: (0, 0, 0)),
                  pl.BlockSpec(memory_space=pltpu.SMEM),
                  pl.BlockSpec((S, 512), lambda b: (b, 0))],
        out_specs=[pl.BlockSpec((S, 512), lambda b: (b, 0)),
                   pl.BlockSpec((S, 128), lambda b: (b, 0)),
                   pl.BlockSpec((S, 128), lambda b: (b, 0)),
                   pl.BlockSpec((N_Q_HEADS, ATT_BLOCK, 2 * ATT_BLOCK), lambda b: (0, 0, 0)),
                   pl.BlockSpec((N_Q_HEADS, 128), lambda b: (0, 0))],
        out_shape=[jax.ShapeDtypeStruct((T, 512), BF16),
                   jax.ShapeDtypeStruct((T, 128), BF16),
                   jax.ShapeDtypeStruct((T, 128), BF16),
                   jax.ShapeDtypeStruct((N_Q_HEADS, ATT_BLOCK, 2 * ATT_BLOCK), F32),
                   jax.ShapeDtypeStruct((N_Q_HEADS, 128), F32)],
        scratch_shapes=[pltpu.VMEM((S + ATT_BLOCK, 256), BF16), pltpu.VMEM((S + ATT_BLOCK, 256), BF16),
                        pltpu.VMEM((S + ATT_BLOCK, 256), F32), pltpu.VMEM((S + ATT_BLOCK, 256), F32)],
        compiler_params=_params("arbitrary"))(proj, kk2, vv2, bias, sinks, datt)


def _hgrn_gates(z, lb):
    sg = _sigmoid(z)
    f = lb + (1.0 - lb) * sg
    return sg, f, jnp.log(f), 1.0 - f


def _hgrn_consts():
    r = lax.broadcasted_iota(jnp.int32, (CHUNK, CHUNK), 0)
    c = lax.broadcasted_iota(jnp.int32, (CHUNK, CHUNK), 1)
    tril = (r >= c).astype(BF16)
    triu = (r <= c).astype(BF16)
    causal = r >= c
    below = (r // SUB) > (c // SUB)
    inside = ((r // SUB) == (c // SUB)) & causal
    return tril, triu, causal, below, inside, c


def _block_rows(ref, lanes, s):
    rows = []
    for i in range(N_SUB):
        if SUB * i + s < 0:
            rows.append(jnp.zeros((SUB, REC_DIM), F32))
        else:
            rows.append(jnp.broadcast_to(ref[pl.ds(SUB * i + s, 1), lanes], (SUB, REC_DIM)))
    return jnp.concatenate(rows, axis=0)


def _hgrn_offdiag(q, k, bcum, b_ref, lanes):
    eq = jnp.exp(jnp.minimum(bcum - _block_rows(b_ref, lanes, -1), 0.0))
    qe = q * eq
    zero = jnp.zeros((SUB, REC_DIM), F32)
    q_rows, k_cols, eks = [jnp.zeros((SUB, (N_SUB - 1) * REC_DIM), F32)], [], []
    for i in range(1, N_SUB):
        q_rows.append(jnp.concatenate([zero] * (i - 1) + [qe[SUB * i:SUB * (i + 1), :]] + [zero] * (N_SUB - 1 - i),
                                      axis=1))
        p = b_ref[pl.ds(SUB * i - 1, 1), lanes]
        pad = jnp.zeros((CHUNK - SUB * i, REC_DIM), F32)
        ek = jnp.concatenate([jnp.exp(p - b_ref[pl.ds(0, SUB * i), lanes]), pad], axis=0)
        k_cols.append(k * ek)
        eks.append(ek)
    return jnp.concatenate(q_rows, axis=0), jnp.concatenate(k_cols, axis=1), eq, eks


def _hgrn_fwd(proj, lb_param, B, S):
    T = B * S
    nc = S // CHUNK

    def body(q_ref, z_ref, v_ref, lb_ref, o_ref, st_ref, k_s, b_s):
        tril, _, _, below, inside, col = _hgrn_consts()
        col_s = col & (SUB - 1)

        def chunk(ci, hts):
            r0 = pl.multiple_of(ci * CHUNK, CHUNK)
            lb = _sigmoid(lb_ref[0:1, :] - lb_ref[1:2, :])
            _, _, g_all, k_all = _hgrn_gates(z_ref[pl.ds(r0, CHUNK), :], lb)
            b_all = _sel_left(tril, g_all)
            k_s[...] = k_all
            b_s[...] = b_all
            new = []
            for e, ht in enumerate(hts):
                lanes = pl.ds(REC_DIM * e, REC_DIM)
                cols = slice(REC_DIM * e, REC_DIM * (e + 1))
                q = q_ref[pl.ds(r0, CHUNK), lanes]
                v = v_ref[pl.ds(r0, CHUNK), lanes]
                k, bcum = k_all[:, cols], b_all[:, cols]
                st_ref[e * nc + ci] = ht
                qst, kst, _, _ = _hgrn_offdiag(q, k, bcum, b_s, lanes)
                d = jnp.zeros((CHUNK, CHUNK), F32)
                for s in range(SUB):
                    w = jnp.exp(jnp.minimum(bcum - _block_rows(b_s, lanes, s), 0.0))
                    colv = jnp.sum(q * _block_rows(k_s, lanes, s) * w, axis=-1, keepdims=True)
                    d = jnp.where(col_s == s, colv, d)
                a = jnp.where(below, _dot_nt(qst.astype(BF16), kst.astype(BF16)), 0.0) + jnp.where(inside, d, 0.0)
                vb = v.astype(BF16)
                qb = (q * jnp.exp(bcum)).astype(BF16)
                o_ref[pl.ds(r0, CHUNK), lanes] = _dot(a.astype(BF16), vb) + _dot_nt(qb, ht.astype(BF16))
                b_last = b_s[pl.ds(CHUNK - 1, 1), lanes]
                kb = (k * jnp.exp(b_last - bcum)).astype(BF16)
                new.append(ht * jnp.exp(b_last) + _dot_tn(vb, kb))
            return tuple(new)

        lax.fori_loop(0, nc, chunk, tuple(jnp.zeros((REC_DIM, REC_DIM), F32) for _ in range(HGRN_PAIR)))

    hp, wd = REC_HEADS // HGRN_PAIR, HGRN_PAIR * REC_DIM
    cq, cf, ci_ = (c * REC_DIM // wd for c in (COL_RQ, COL_RF, COL_RI))
    return pl.pallas_call(
        body, name="hgrn_fwd", grid=(B, hp),
        in_specs=[pl.BlockSpec((S, wd), lambda b, h: (b, cq + h)),
                  pl.BlockSpec((S, wd), lambda b, h: (b, cf + h)),
                  pl.BlockSpec((S, wd), lambda b, h: (b, ci_ + h)),
                  pl.BlockSpec((2, wd), lambda b, h: (0, h))],
        out_specs=[pl.BlockSpec((S, wd), lambda b, h: (b, h)),
                   pl.BlockSpec((HGRN_PAIR * nc, REC_DIM, REC_DIM), lambda b, h: (b * hp + h, 0, 0))],
        out_shape=[jax.ShapeDtypeStruct((T, 512), F32),
                   jax.ShapeDtypeStruct((B * REC_HEADS * nc, REC_DIM, REC_DIM), F32)],
        scratch_shapes=[pltpu.VMEM((CHUNK, wd), F32), pltpu.VMEM((CHUNK, wd), F32)],
        compiler_params=_params("arbitrary", "arbitrary"))(proj, proj, proj, lb_param)


def _hgrn_bwd(proj, lb_param, states, do, B, S, comm=None):
    T = B * S
    nc = S // CHUNK

    c_arrays, c_in_specs, c_out_shapes, c_sems = _comm_parts(comm)
    nci, nco = len(c_arrays), len(c_out_shapes)

    def body(*refs):
        q_ref, z_ref, v_ref, lb_ref, st_ref, do_ref = refs[:6]
        dq_ref, dz_ref, dv_ref, dlb_ref = refs[6 + nci:10 + nci]
        k_s, b_s = refs[10 + nci + nco:12 + nci + nco]
        comm_first, comm_last = _comm_run(comm, (B, REC_HEADS // HGRN_PAIR), refs, 6, 4)
        comm_first()
        tril, triu, causal, below, inside, col = _hgrn_consts()
        col_s = col & (SUB - 1)
        last_row = lax.broadcasted_iota(jnp.int32, (CHUNK, 1), 0) == CHUNK - 1
        rc = lax.broadcasted_iota(jnp.int32, (CHUNK, SUB * REC_DIM), 0)
        lc = lax.broadcasted_iota(jnp.int32, (CHUNK, SUB * REC_DIM), 1)
        spread = ((rc & (SUB - 1)) == (lc // REC_DIM)).astype(BF16)
        rr = lax.broadcasted_iota(jnp.int32, (CHUNK, SUB * CHUNK), 0)
        cc = lax.broadcasted_iota(jnp.int32, (CHUNK, SUB * CHUNK), 1)
        gather = (((rr // SUB) == ((cc & (CHUNK - 1)) // SUB)) & ((rr & (SUB - 1)) == (cc // CHUNK))).astype(BF16)

        heads = range(HGRN_PAIR)
        cols = [slice(REC_DIM * e, REC_DIM * (e + 1)) for e in heads]
        lanes = [pl.ds(REC_DIM * e, REC_DIM) for e in heads]
        lane_cat = lambda vals: jnp.concatenate(vals, axis=1)
        row_cat = lambda vals: jnp.concatenate(vals, axis=0)

        def chunk(it, carry):
            dhts, dlb = carry
            ci = nc - 1 - it
            r0 = pl.multiple_of(ci * CHUNK, CHUNK)
            rows = pl.ds(r0, CHUNK)
            lb = _sigmoid(lb_ref[0:1, :] - lb_ref[1:2, :])
            sg, f, g_all, k_all = _hgrn_gates(z_ref[rows, :], lb)
            b_all = _sel_left(tril, g_all)
            k_s[...] = k_all
            b_s[...] = b_all
            q_all = q_ref[rows, :]
            das, hd = [], []
            for e in heads:
                vb, dob = v_ref[rows, lanes[e]].astype(BF16), do_ref[rows, lanes[e]].astype(BF16)
                da = jnp.where(causal, _dot_nt(dob, vb), 0.0)
                das.append(jnp.where(inside, da, 0.0))
                hd.append((vb, dob, da))
            da_hi, da_lo = _split2(row_cat(das))
            da_in = _dot(da_hi, spread) + _dot(da_lo, spread)
            ds, dqs, pieces, pieces_lo = [], [], [[] for _ in range(SUB)], [[] for _ in range(SUB)]
            for e in heads:
                q, bcum = q_all[:, cols[e]], b_all[:, cols[e]]
                d = jnp.zeros((CHUNK, CHUNK), F32)
                dq = jnp.zeros((CHUNK, REC_DIM), F32)
                for s in range(SUB):
                    w = jnp.exp(jnp.minimum(bcum - _block_rows(b_s, lanes[e], s), 0.0))
                    ks = _block_rows(k_s, lanes[e], s)
                    qw = q * w
                    d = jnp.where(col_s == s, jnp.sum(qw * ks, axis=-1, keepdims=True), d)
                    da_s = da_in[CHUNK * e:CHUNK * (e + 1), REC_DIM * s:REC_DIM * (s + 1)]
                    dq = dq + da_s * ks * w
                    hi, lo = _split2(da_s * qw)
                    pieces[s].append(hi)
                    pieces_lo[s].append(lo)
                ds.append(d)
                dqs.append(dq)
            dk_in = (_dot(gather, row_cat([lane_cat(p) for p in pieces]))
                     + _dot(gather, row_cat([lane_cat(p) for p in pieces_lo])))
            dq_out, dk_out, dv_out, db_out, new_dhts = [], [], [], [], []
            for e in heads:
                q, k, bcum = q_all[:, cols[e]], k_all[:, cols[e]], b_all[:, cols[e]]
                vb, dob, da = hd[e]
                dht, ht = dhts[e], st_ref[e * nc + ci]
                qst, kst, eq, eks = _hgrn_offdiag(q, k, bcum, b_s, lanes[e])
                qst_b, kst_b = qst.astype(BF16), kst.astype(BF16)
                a = jnp.where(below, _dot_nt(qst_b, kst_b), 0.0) + jnp.where(inside, ds[e], 0.0)
                da_off = jnp.where(below, da, 0.0).astype(BF16)
                dqst = _dot(da_off, kst_b)
                dkst = _dot_tn(da_off, qst_b)
                dk = dk_in[:, cols[e]]
                dq_rows = [jnp.zeros((SUB, REC_DIM), F32)]
                for i in range(1, N_SUB):
                    dq_rows.append(dqst[SUB * i:SUB * (i + 1), REC_DIM * (i - 1):REC_DIM * i])
                    dk = dk + dkst[:, REC_DIM * (i - 1):REC_DIM * i] * eks[i - 1]
                dq = dqs[e] + row_cat(dq_rows) * eq
                eb = jnp.exp(bcum)
                b_last = b_s[pl.ds(CHUNK - 1, 1), lanes[e]]
                el = jnp.exp(b_last)
                ekb = jnp.exp(b_last - bcum)
                qb = (q * eb).astype(BF16)
                kb = k * ekb
                dhb = dht.astype(BF16)
                dv_out.append(_dot_tn(a.astype(BF16), dob) + _dot_nt(kb.astype(BF16), dhb))
                dqb = _dot(dob, ht.astype(BF16))
                dkb = _dot(vb, dhb)
                new_dhts.append(dht * el + _dot_tn(dob, qb))
                dq = dq + eb * dqb
                dk = dk + ekb * dkb
                edge = jnp.sum(kb * dkb, axis=0, keepdims=True) + el * jnp.sum(ht * dht, axis=0, keepdims=True)
                db_out.append(q * dq - k * dk + jnp.where(last_row, edge, 0.0))
                dq_out.append(dq)
                dk_out.append(dk)
            dk_all = lane_cat(dk_out)
            db_hi, db_lo = _split2(lane_cat(db_out))
            dg = _dot(triu, db_hi) + _dot(triu, db_lo)
            df = dg / f - dk_all
            dz_ref[rows, :] = (df * (1.0 - lb) * sg * (1.0 - sg)).astype(dz_ref.dtype)
            dq_ref[rows, :] = lane_cat(dq_out).astype(dq_ref.dtype)
            dv_ref[rows, :] = lane_cat(dv_out).astype(dv_ref.dtype)
            return tuple(new_dhts), dlb + jnp.sum(df * (1.0 - sg), axis=0, keepdims=True)

        zero = (tuple(jnp.zeros((REC_DIM, REC_DIM), F32) for _ in heads), jnp.zeros((1, HGRN_PAIR * REC_DIM), F32))
        _, dlb = lax.fori_loop(0, nc, chunk, zero)
        lb = _sigmoid(lb_ref[0:1, :] - lb_ref[1:2, :])
        dlb_ref[...] = jnp.broadcast_to(dlb * lb * (1.0 - lb), (8, HGRN_PAIR * REC_DIM))
        comm_last()

    hp, wd = REC_HEADS // HGRN_PAIR, HGRN_PAIR * REC_DIM
    cq, cf, ci_ = (c * REC_DIM // wd for c in (COL_RQ, COL_RF, COL_RI))
    return pl.pallas_call(
        body, name="hgrn_bwd", grid=(B, hp),
        in_specs=[pl.BlockSpec((S, wd), lambda b, h: (b, cq + h)),
                  pl.BlockSpec((S, wd), lambda b, h: (b, cf + h)),
                  pl.BlockSpec((S, wd), lambda b, h: (b, ci_ + h)),
                  pl.BlockSpec((2, wd), lambda b, h: (0, h)),
                  pl.BlockSpec((HGRN_PAIR * nc, REC_DIM, REC_DIM), lambda b, h: (b * hp + h, 0, 0)),
                  pl.BlockSpec((S, wd), lambda b, h: (b, h))] + c_in_specs,
        out_specs=[pl.BlockSpec((S, wd), lambda b, h: (b, h))] * 3
        + [pl.BlockSpec((8, wd), lambda b, h: (b, h))] + [ANY] * nco,
        out_shape=[jax.ShapeDtypeStruct((T, 512), BF16)] * 3 + [jax.ShapeDtypeStruct((B * 8, 512), F32)]
        + c_out_shapes,
        scratch_shapes=[pltpu.VMEM((CHUNK, wd), F32)] * 2 + c_sems,
        compiler_params=_params("arbitrary", "arbitrary"))(proj, proj, proj, lb_param, states, do, *c_arrays)


def _rec_gate_fwd(rec, proj, rec_norm):
    T = rec.shape[0]

    def fn(accs, tv, cv):
        return [_rms_hat(tv[0]) * cv[0] * _sigmoid(tv[1])]

    return _tile_call("rec_gate", fn, T, 512, _pick(T, 1024), REC_DIM, tiles=[(rec, 0), (proj, COL_RG)],
                      consts=[rec_norm], outs=[BF16])[0]


def _rec_gate_bwd(dyb, w_rec_proj, rec, proj, rec_norm):
    T = rec.shape[0]

    def fn(accs, tv, cv):
        d, r, rg = accs[0], tv[0], tv[1]
        sg = _sigmoid(rg)
        rn = _rms_hat(r) * cv[0]
        dh, dg = _rms_bwd_vals(d * sg, r, cv[0])
        return [dh, d * rn * sg * (1.0 - sg), dg]

    return _tile_call("rec_gate_bwd", fn, T, 512, _pick(T, 1024), REC_DIM, pairs=[(dyb, 0, w_rec_proj, "nt")],
                      tiles=[(rec, 0), (proj, COL_RG)], consts=[rec_norm], outs=[F32, BF16], parts=1)


def _mix_out_fwd(att, recn, proj, w_att_proj, w_rec_proj, w_out, h1, g_next):
    T = att.shape[0]
    tn = 256

    def merge(accs, tv, cv):
        ya, yb = accs
        return [ya, yb, _sigmoid(tv[0]) * ya + _sigmoid(tv[1]) * yb]

    ya, yb, merged = _tile_call(
        "merge", merge, T, D_MODEL, _pick(T, 1024), tn,
        pairs=[(att, 0, w_att_proj, "nn"), (recn, 0, w_rec_proj, "nn")],
        tiles=[(proj, COL_GA * 128 // tn), (proj, COL_GB * 128 // tn)], outs=[BF16] * 3)

    def res(accs, tv, cv):
        h2 = tv[0] + accs[0]
        return [h2, _rms_hat(h2) * cv[0]]

    h2, n2 = _tile_call("mix_out", res, T, D_MODEL, _pick(T, 512), D_MODEL, pairs=[(merged, 0, w_out, "nn")],
                        tiles=[(h1, 0)], consts=[g_next], outs=[F32, BF16])
    return h2, n2, (ya, yb, merged)


GATHER_FIRST = ("w_ffn1_in",)
GATHER_MIX = ("w_ffn1_out", "w_in", "w_att_proj", "w_rec_proj", "w_out")
GATHER_LAST = ("w_ffn2_in", "w_ffn2_out", "w_ple_gate", "w_ple_proj")
SCATTER_LATE = ("w_ple_gate", "w_ple_proj", "w_ffn2_in", "w_ffn2_out")
SCATTER_MIX = ("w_out", "w_att_proj", "w_rec_proj", "w_in")
SCATTER_LAST = ("w_ffn1_in", "w_ffn1_out")


def _local_step(x, p, tgt, w, mine16, cc, me_chip, B, S):
    T = B * S
    w = dict(w)
    g_ffn1, g_mix, g_ffn2, g_ple = w["norm_ffn1"], w["norm_mix"], w["norm_ffn2"], w["norm_ple"]
    g_fin = w["norm_final"].reshape(1, D_MODEL)
    grads, part, from_chips = {}, {}, {}

    def gather(names):
        return _gather_comm([mine16[n] for n in names])

    def place(names, got):
        for n, g in zip(names, got):
            full = lax.dynamic_update_index_in_dim(g, mine16[n], me_chip, 0)
            w[n] = full if n in ("w_ffn1_in", "w_ffn2_in") else _natural(n, full)

    def scatter(tag, names):
        from_sib = _swap_halves("rs_sibling_" + tag, [grads[n][1] for n in names])
        for n, fs in zip(names, from_sib):
            part[n] = _add_sibling("rs_add_sib_" + n, grads[n][0], fs, cc)
        return _scatter_comm([part[n][1] for n in names])

    def scattered(names, got):
        for n, g in zip(names, got):
            from_chips[n] = g

    place(GATHER_FIRST, _run_comm("gather_first", gather(GATHER_FIRST)))
    def ffn1_out_weight(got):
        place(GATHER_MIX, got)
        return w["w_ffn1_out"]

    h1, u, sv1, got_last = _ffn_fwd("ffn1", x, g_ffn1, w["w_ffn1_in"], None, g_mix, comm_in=gather(GATHER_MIX),
                                    comm_out=gather(GATHER_LAST), w_out_of=ffn1_out_weight)
    place(GATHER_LAST, got_last)

    def ident(accs, tv, cv):
        return [accs[0]]

    proj = _tile_call("in_proj", ident, T, IN_W, _pick(T, 1024), 256, pairs=[(u, 0, w["w_in"], "nn")],
                      outs=[F32])[0]
    onehot = jnp.asarray(_t5_onehot())
    bias = _small_mm("t5_bias", w["rel_bias"].T, onehot.astype(BF16), "right")
    bias = bias.reshape(N_Q_HEADS, ATT_BLOCK, 2 * ATT_BLOCK)
    sinks = w["attn_sinks"].reshape(N_Q_HEADS)
    kk2, vv2 = _kv_layouts(proj)
    att = _swa_fwd(proj, kk2, vv2, bias, sinks, B, S)
    rec, states = _hgrn_fwd(proj, w["lb_param"], B, S)
    recn = _rec_gate_fwd(rec, proj, w["rec_norm"])
    h2, n2, (ya, yb, merged) = _mix_out_fwd(att, recn, proj, w["w_att_proj"], w["w_rec_proj"], w["w_out"], h1,
                                            g_ffn2)
    h3, n3, sv2, _ = _ffn_fwd("ffn2", h2, g_ffn2, w["w_ffn2_in"], w["w_ffn2_out"], g_ple, n=n2)

    def ple(accs, tv, cv):
        gate = _sigmoid(accs[0])
        return [gate, accs[1], tv[0] + gate * accs[1]]

    gate_p, pp, h4 = _tile_call(
        "ple", ple, T, D_MODEL, _pick(T, 512), 512,
        pairs=[(n3, 0, w["w_ple_gate"], "nn"), (p, 0, w["w_ple_proj"], "nn")], tiles=[(h3, 0)],
        outs=[BF16, BF16, F32])

    def head(accs, tv, cv):
        h, t, gt, ppv = tv[0], tv[1], tv[2].astype(F32), tv[3].astype(F32)
        err = _rms_hat(h) * cv[0] - t
        dh, dg = _rms_bwd_vals(err * (1.0 / D_MODEL), h, cv[0])
        return [dh, dh * ppv * gt * (1.0 - gt), dh * gt, _group8(err * err), dg]

    dh4, dzg, dpp, loss_p, dg_fin = _tile_call(
        "loss_head", head, T, D_MODEL, _pick(T, 256), D_MODEL,
        tiles=[(h4, 0), (tgt, 0), (gate_p, 0), (pp, 0)], consts=[g_fin], outs=[F32, BF16, BF16], parts=2)
    grads["norm_final"] = dg_fin

    grads["w_ple_gate"] = _mm_tn_rows("ple_dwg", n3, dzg)
    grads["w_ple_proj"] = _mm_tn_cols("ple_dwp", p, dpp)

    def dnorm(accs, tv, cv):
        dh, dg = _rms_bwd_vals(accs[0], tv[0], cv[0])
        dh = tv[1] + dh
        return [dh, 0.5 * dh, dg]

    dh3, df3, grads["norm_ple"] = _tile_call(
        "ple_dnorm", dnorm, T, D_MODEL, _pick(T, 256), D_MODEL, pairs=[(dzg, 0, w["w_ple_gate"], "nt")],
        tiles=[(h3, 0), (dh4, 0)], consts=[g_ple], outs=[F32, BF16], parts=1)

    dh2, dh2b, grads["norm_ffn2"], grads["w_ffn2_in"], grads["w_ffn2_out"], _, _ = _ffn_bwd(
        "ffn2b", dh3, df3, h2, g_ffn2, w["w_ffn2_in"], w["w_ffn2_out"], sv2)
    scatter_late = scatter("late", SCATTER_LATE)

    grads["w_out"] = _mm_tn_rows("mix_dwout", merged, dh2b)
    tn = 256

    def dmerge(accs, tv, cv):
        dm = accs[0]
        sa, sb = _sigmoid(tv[0]), _sigmoid(tv[1])
        yav, ybv = tv[2].astype(F32), tv[3].astype(F32)
        return [dm * sa, dm * sb, dm * yav * sa * (1.0 - sa), dm * ybv * sb * (1.0 - sb)]

    dya, dyb, dga, dgb = _tile_call(
        "mix_dmerge", dmerge, T, D_MODEL, _pick(T, 1024), tn, pairs=[(dh2b, 0, w["w_out"], "nt")],
        tiles=[(proj, COL_GA * 128 // tn), (proj, COL_GB * 128 // tn), (ya, 0), (yb, 0)], outs=[BF16] * 4)
    grads["w_att_proj"] = _mm_tn_cols("mix_dwatt", att, dya)
    grads["w_rec_proj"] = _mm_tn_cols("mix_dwrec", recn, dyb)

    datt = _tile_call("mix_datt", ident, T, 512, _pick(T, 1024), 512, pairs=[(dya, 0, w["w_att_proj"], "nt")],
                      outs=[BF16])[0]
    drec, drg, grads["rec_norm"] = _rec_gate_bwd(dyb, w["w_rec_proj"], rec, proj, w["rec_norm"])

    drq, drf, dri, dlb, *got = _hgrn_bwd(proj, w["lb_param"], states, drec, B, S, comm=scatter_late)
    scattered(SCATTER_LATE, got)
    grads["lb_param"] = dlb
    daq, dak, dav, dbias, dsink = _swa_bwd(proj, kk2, vv2, bias, sinks, datt, B, S)
    grads["attn_sinks"] = dsink
    grads["rel_bias"] = _small_mm("t5_dbias", dbias.reshape(N_Q_HEADS, -1), onehot.T.astype(BF16), "right")
    dproj = jnp.concatenate([daq, dak, dav, drq, drf, dri, drg, dga, dgb], axis=1)
    tk = _pick(T, 512, 128)
    w_in_shard = IN_W // N_CHIPS
    gw32, gw16 = _mm_tn("mix_dwin", (1, 2, T // tk),
                        (u, (tk, D_MODEL), lambda i, j, k: (k, 0)), (dproj, (tk, IN_W // 2), lambda i, j, k: (k, j)),
                        _grad_pair((D_MODEL, IN_W), (D_MODEL, IN_W // 2), lambda i, j, k: (0, j)))
    to_sh = lambda t: t.reshape(D_MODEL, N_CHIPS, w_in_shard).transpose(1, 0, 2)
    grads["w_in"] = (to_sh(gw32), to_sh(gw16))
    scatter_mix = scatter("mix", SCATTER_MIX)

    def dnorm_mix(accs, tv, cv):
        dh, dg = _rms_bwd_vals(accs[0], tv[0], cv[0])
        dh = tv[1] + dh
        return [dh, 0.5 * dh, dg]

    dh1, df1, grads["norm_mix"] = _tile_call(
        "mix_dnorm", dnorm_mix, T, D_MODEL, _pick(T, 256), D_MODEL, pairs=[(dproj, 0, w["w_in"], "nt")],
        tiles=[(h1, 0), (dh2, 0)], consts=[g_mix], outs=[F32, BF16], parts=1)

    def scatter_last(dw_in, dw_out):
        grads["w_ffn1_in"], grads["w_ffn1_out"] = dw_in, dw_out
        return scatter("last", SCATTER_LAST)

    dx, _, grads["norm_ffn1"], _, _, got, got_last = _ffn_bwd(
        "ffn1b", dh1, df1, x, g_ffn1, w["w_ffn1_in"], w["w_ffn1_out"], sv1, comm=scatter_mix, comm_last=scatter_last)
    scattered(SCATTER_MIX, got)
    scattered(SCATTER_LAST, got_last)
    return loss_p, dx, grads, part, from_chips


def _place():
    x, y, c = lax.axis_index("x"), lax.axis_index("y"), lax.axis_index("c")
    return x, y, c


def _other_chips(x, y):
    return [(1 - x, y, 2 * (1 - x) + y), (x, 1 - y, 2 * x + 1 - y), (1 - x, 1 - y, 2 * (1 - x) + 1 - y)]


def _half_rows(ref_3d, chip, h, rows):
    return ref_3d.at[chip, pl.ds(h * rows, rows), :]


def _run_comm(name, comm):
    nci, nco = len(comm.ins), len(comm.out_shapes)

    def body(*refs):
        cin, cout, send_sems, recv_sems = refs[:nci], refs[nci:nci + nco], refs[-2], refs[-1]
        comm.start(cin, cout, send_sems, recv_sems)
        comm.finish(cin, cout, send_sems, recv_sems)

    return pl.pallas_call(
        body, name=name, in_specs=[ANY] * nci, out_specs=[ANY] * nco, out_shape=list(comm.out_shapes),
        scratch_shapes=[pltpu.SemaphoreType.DMA((comm.n_sems,)), pltpu.SemaphoreType.DMA((comm.n_sems,))],
    )(*comm.ins)


def _gather_comm(ws):
    nw = len(ws)

    def parts(w_refs, out_refs, send_sems, recv_sems):
        x, y, c = _place()
        me = 2 * x + y
        chips = _other_chips(x, y)

        def copy(i, k, chip, h, to, src=None):
            half = ws[i].shape[0] // 2
            dst = _half_rows(out_refs[i], chip, h, half)
            return pltpu.make_async_remote_copy(
                src_ref=dst if src is None else src, dst_ref=dst,
                send_sem=send_sems.at[6 * i + k], recv_sem=recv_sems.at[6 * i + k], device_id=to, device_id_type=MESH)

        def first():
            out = []
            for i in range(nw):
                half = ws[i].shape[0] // 2
                out += [copy(i, j, me, c, (cx, cy, c), src=w_refs[i].at[pl.ds(c * half, half), :])
                        for j, (cx, cy, _) in enumerate(chips)]
            return out

        return copy, first, chips, c, (x, y, 1 - c)

    def start(*refs):
        _, first, _, _, _ = parts(*refs)
        for cp in first():
            cp.start()

    def finish(*refs):
        copy, first, chips, c, sibling = parts(*refs)
        passed = []
        for i in range(nw):
            for j, (cx, cy, ci) in enumerate(chips):
                copy(i, j, ci, c, (cx, cy, c)).wait_recv()
                fw = copy(i, 3 + j, ci, c, sibling)
                fw.start()
                passed.append(fw)
        for i in range(nw):
            for j, (_, _, ci) in enumerate(chips):
                copy(i, 3 + j, ci, 1 - c, sibling).wait_recv()
        for cp in first() + passed:
            cp.wait_send()

    return _Comm(list(ws), [jax.ShapeDtypeStruct((N_CHIPS,) + w.shape, w.dtype) for w in ws], 6 * nw, start, finish)


def _scatter_comm(ps):
    nw = len(ps)

    def copies(p_refs, out_refs, send_sems, recv_sems):
        x, y, c = _place()
        cps = []
        for i in range(nw):
            for j, (cx, cy, ci) in enumerate(_other_chips(x, y)):
                cps.append(pltpu.make_async_remote_copy(
                    src_ref=p_refs[i].at[ci], dst_ref=out_refs[i].at[j], send_sem=send_sems.at[3 * i + j],
                    recv_sem=recv_sems.at[3 * i + j], device_id=(cx, cy, c), device_id_type=MESH))
        return cps

    def start(*refs):
        for cp in copies(*refs):
            cp.start()

    def finish(*refs):
        for cp in copies(*refs):
            cp.wait()

    return _Comm(list(ps), [jax.ShapeDtypeStruct((3,) + p.shape[1:], p.dtype) for p in ps], 3 * nw, start, finish)


def _swap_halves(name, gs):
    nw = len(gs)

    def body(*refs):
        g_refs, out_refs, send_sems, recv_sems = refs[:nw], refs[nw:2 * nw], refs[2 * nw], refs[2 * nw + 1]
        x, y, c = _place()
        cps = []
        for i in range(nw):
            half = gs[i].shape[1] // 2
            cps.append(pltpu.make_async_remote_copy(
                src_ref=g_refs[i].at[:, pl.ds((1 - c) * half, half), :], dst_ref=out_refs[i],
                send_sem=send_sems.at[i], recv_sem=recv_sems.at[i], device_id=(x, y, 1 - c), device_id_type=MESH))
        for cp in cps:
            cp.start()
        for cp in cps:
            cp.wait()

    return pl.pallas_call(
        body, name=name, in_specs=[ANY] * nw, out_specs=[ANY] * nw,
        out_shape=[jax.ShapeDtypeStruct((N_CHIPS, g.shape[1] // 2, g.shape[2]), g.dtype) for g in gs],
        scratch_shapes=[pltpu.SemaphoreType.DMA((nw,)), pltpu.SemaphoreType.DMA((nw,))],
    )(*gs)


def _join_halves(name, ss):
    nw = len(ss)

    def body(*refs):
        s_refs, out_refs, send_sems, recv_sems = refs[:nw], refs[nw:2 * nw], refs[2 * nw], refs[2 * nw + 1]
        x, y, c = _place()
        cps = [pltpu.make_async_remote_copy(
            src_ref=s_refs[i], dst_ref=out_refs[i], send_sem=send_sems.at[i], recv_sem=recv_sems.at[i],
            device_id=(x, y, 1 - c), device_id_type=MESH) for i in range(nw)]
        for cp in cps:
            cp.start()
        for cp in cps:
            cp.wait()

    return pl.pallas_call(
        body, name=name, in_specs=[ANY] * nw, out_specs=[ANY] * nw,
        out_shape=[jax.ShapeDtypeStruct(s.shape, s.dtype) for s in ss],
        scratch_shapes=[pltpu.SemaphoreType.DMA((nw,)), pltpu.SemaphoreType.DMA((nw,))],
    )(*ss)


def _allreduce_small(sp):
    def body(s_ref, out_ref, slots, send_sems, recv_sems):
        x, y, c = _place()
        me = 4 * x + 2 * y + c
        slots[me] = s_ref[...]
        cps = []
        for r in range(1, N_DEV):
            px, py, pc = x ^ (r >> 2), y ^ ((r >> 1) & 1), c ^ (r & 1)
            cps.append(pltpu.make_async_remote_copy(
                src_ref=s_ref, dst_ref=slots.at[me], send_sem=send_sems.at[r - 1], recv_sem=recv_sems.at[r - 1],
                device_id=(px, py, pc), device_id_type=MESH))
        for cp in cps:
            cp.start()
        for r in range(1, N_DEV):
            px, py, pc = x ^ (r >> 2), y ^ ((r >> 1) & 1), c ^ (r & 1)
            pltpu.make_async_remote_copy(
                src_ref=s_ref, dst_ref=slots.at[4 * px + 2 * py + pc], send_sem=send_sems.at[r - 1],
                recv_sem=recv_sems.at[r - 1], device_id=(px, py, pc), device_id_type=MESH).wait_recv()
        for cp in cps:
            cp.wait_send()
        acc = slots[0]
        for d in range(1, N_DEV):
            acc = acc + slots[d]
        out_ref[...] = acc

    return pl.pallas_call(
        body, name="allreduce_small",
        in_specs=[pl.BlockSpec(memory_space=pltpu.VMEM)], out_specs=pl.BlockSpec(memory_space=pltpu.VMEM),
        out_shape=jax.ShapeDtypeStruct(sp.shape, F32),
        scratch_shapes=[pltpu.VMEM((N_DEV,) + sp.shape, F32), pltpu.SemaphoreType.DMA((N_DEV - 1,)),
                        pltpu.SemaphoreType.DMA((N_DEV - 1,))],
    )(sp)


def _scalar(v):
    return jnp.reshape(v, (1,)).astype(jnp.int32)


def _row_tile(h, dtype_mult=16):
    return _pick(h, 256, dtype_mult)


def _add_sibling(name, g32, from_sib, c):
    _, r, n = g32.shape
    h = r // 2
    th = _row_tile(h)
    nt = h // th

    def body(c_ref, g_ref, s_ref, o32_ref, o16_ref):
        s = g_ref[...] + s_ref[...].astype(F32)
        o32_ref[...] = s
        o16_ref[...] = s.astype(BF16)

    blk = (None, th, n)
    return pl.pallas_call(
        body, name=name,
        grid_spec=pltpu.PrefetchScalarGridSpec(
            num_scalar_prefetch=1, grid=(N_CHIPS, nt),
            in_specs=[pl.BlockSpec(blk, lambda k, t, c_ref: (k, c_ref[0] * nt + t, 0)),
                      pl.BlockSpec(blk, lambda k, t, c_ref: (k, t, 0))],
            out_specs=[pl.BlockSpec(blk, lambda k, t, c_ref: (k, t, 0))] * 2),
        out_shape=[jax.ShapeDtypeStruct((N_CHIPS, h, n), F32), jax.ShapeDtypeStruct((N_CHIPS, h, n), BF16)],
        compiler_params=_params("arbitrary", "arbitrary"))(_scalar(c), g32, from_sib)


def _add_chips(name, p32, from_chips, me_chip):
    _, h, n = p32.shape
    th = _row_tile(h)

    def body(m_ref, p_ref, a_ref, b_ref, c_ref, o_ref):
        o_ref[...] = p_ref[...] + a_ref[...].astype(F32) + b_ref[...].astype(F32) + c_ref[...].astype(F32)

    blk = (None, th, n)
    return pl.pallas_call(
        body, name=name,
        grid_spec=pltpu.PrefetchScalarGridSpec(
            num_scalar_prefetch=1, grid=(h // th,),
            in_specs=[pl.BlockSpec(blk, lambda t, m_ref: (m_ref[0], t, 0))]
            + [pl.BlockSpec(blk, lambda t, m_ref, j=j: (j, t, 0)) for j in range(3)],
            out_specs=pl.BlockSpec((th, n), lambda t, m_ref: (t, 0))),
        out_shape=jax.ShapeDtypeStruct((h, n), F32),
        compiler_params=_params("arbitrary"))(_scalar(me_chip), p32, from_chips, from_chips, from_chips)


def _adamw_vals(w, g, m, v):
    m = ADAM_B1 * m + (1.0 - ADAM_B1) * g
    v = ADAM_B2 * v + (1.0 - ADAM_B2) * (g * g)
    m_hat = m / (1.0 - ADAM_B1 ** ADAM_STEP)
    v_hat = v / (1.0 - ADAM_B2 ** ADAM_STEP)
    delta = -ADAM_LR * (m_hat / (jnp.sqrt(v_hat) + ADAM_EPS) + ADAM_WD * w)
    return delta, m, v


def _adamw_halves(name, w, m, v, g_mine, g_sib, c):
    r, n = w.shape
    h = r // 2
    th = _row_tile(h, 8)
    nt = h // th

    def body(c_ref, w_ref, m_ref, v_ref, a_ref, b_ref, g_ref, d_ref, nm_ref, nv_ref):
        mine = (pl.program_id(0) // nt) == c_ref[0]
        g = jnp.where(mine, a_ref[...], b_ref[...])
        d, nm, nv = _adamw_vals(w_ref[...], g, m_ref[...], v_ref[...])
        g_ref[...] = g
        d_ref[...] = d
        nm_ref[...] = nm
        nv_ref[...] = nv

    full = pl.BlockSpec((th, n), lambda t, c_ref: (t, 0))
    part = pl.BlockSpec((th, n), lambda t, c_ref: (t % nt, 0))
    return pl.pallas_call(
        body, name=name,
        grid_spec=pltpu.PrefetchScalarGridSpec(
            num_scalar_prefetch=1, grid=(2 * nt,), in_specs=[full, full, full, part, part], out_specs=[full] * 4),
        out_shape=[jax.ShapeDtypeStruct((r, n), F32)] * 4,
        compiler_params=_params("arbitrary"))(_scalar(c), w, m, v, g_mine, g_sib)


def _adamw(name, w, g, m, v):
    R, W = w.shape

    def fn(accs, tv, cv):
        return list(_adamw_vals(*tv))

    return _tile_call(name, fn, R, W, _pick(R, 256), W, tiles=[(w, 0), (g, 0), (m, 0), (v, 0)], outs=[F32] * 3)


SMALL_LAYOUT = (("rel_bias", 2, 256), ("lb_param", 8, 1024), ("norm_ffn1", 8, 1024), ("norm_mix", 8, 1024),
                ("attn_sinks", 1, 8), ("rec_norm", 1, 128), ("norm_ffn2", 8, 1024), ("norm_ple", 8, 1024),
                ("norm_final", 8, 1024), ("loss", 8, 1024))


def _pack_small(vals):
    rows = []
    for name, nrows, n in SMALL_LAYOUT:
        flat = vals[name].reshape(-1)
        flat = jnp.pad(flat, (0, nrows * 128 - n))
        rows.append(flat.reshape(nrows, 128))
    packed = jnp.concatenate(rows, axis=0)
    return jnp.pad(packed, ((0, SMALL_ROWS - packed.shape[0]), (0, 0)))


def _unpack_small(packed, shapes):
    out, r = {}, 0
    for name, nrows, n in SMALL_LAYOUT:
        out[name] = packed[r:r + nrows].reshape(-1)[:n].reshape(shapes[name])
        r += nrows
    return out


def _natural(name, s):
    if name in COL_SHARDED:
        return s.transpose(1, 0, 2).reshape(s.shape[1], -1)
    return s.reshape(-1, s.shape[2])


def kernel(x, p, rel_bias, lb_param, norm_ffn1, w_ffn1_in, w_ffn1_out, norm_mix, w_in, attn_sinks, rec_norm, w_att_proj, w_rec_proj, w_out, norm_ffn2, w_ffn2_in, w_ffn2_out, norm_ple, w_ple_gate, w_ple_proj, norm_final, loss_target, m_rel_bias, m_lb_param, m_norm_ffn1, m_w_ffn1_in, m_w_ffn1_out, m_norm_mix, m_w_in, m_attn_sinks, m_rec_norm, m_w_att_proj, m_w_rec_proj, m_w_out, m_norm_ffn2, m_w_ffn2_in, m_w_ffn2_out, m_norm_ple, m_w_ple_gate, m_w_ple_proj, m_norm_final, v_rel_bias, v_lb_param, v_norm_ffn1, v_w_ffn1_in, v_w_ffn1_out, v_norm_mix, v_w_in, v_attn_sinks, v_rec_norm, v_w_att_proj, v_w_rec_proj, v_w_out, v_norm_ffn2, v_w_ffn2_in, v_w_ffn2_out, v_norm_ple, v_w_ple_gate, v_w_ple_proj, v_norm_final):
    args = dict(locals())
    wsh = {n: args[n] for n in WEIGHTS}
    B, S = x.shape[0], x.shape[1]
    T = B * S
    cx, cy, cc = _place()
    me_chip = 2 * cx + cy

    mine16 = {n: wsh[n][0].astype(BF16) for n in BIG}
    loss_p, dx, grads, part, from_chips = _local_step(
        x.reshape(T, D_MODEL), p.reshape(T, PLE_DIM), loss_target.reshape(T, D_MODEL),
        {n: wsh[n] for n in SMALL}, mine16, cc, me_chip, B, S)

    s_mine = [_add_chips("rs_add_chips_" + n, part[n][0], from_chips[n], me_chip) for n in BIG]
    s_sib = _join_halves("rs_join", s_mine)

    small_vals = {
        "rel_bias": grads["rel_bias"].T,
        "lb_param": jnp.concatenate([_colsum("dlb_sum", grads["lb_param"]),
                                     -_colsum("dlb_sum2", grads["lb_param"])], axis=0) / 8.0,
        "attn_sinks": grads["attn_sinks"][:, 0],
        "rec_norm": _colsum("drn_sum", grads["rec_norm"]).reshape(REC_HEADS, REC_DIM).sum(axis=0),
        "loss": _colsum("loss_sum", loss_p),
    }
    for n in ("norm_ffn1", "norm_mix", "norm_ffn2", "norm_ple", "norm_final"):
        small_vals[n] = _colsum(n + "_sum", grads[n])
    red = _allreduce_small(_pack_small(small_vals))
    small_shapes = {n: wsh[n].shape for n in SMALL}
    small_shapes["loss"] = (D_MODEL,)
    small = _unpack_small(red, small_shapes)
    loss = 0.5 * jnp.sum(small["loss"]) / D_MODEL

    out_g, out_d, out_m, out_v = {}, {}, {}, {}
    for n, gm, gs in zip(BIG, s_mine, s_sib):
        res = _adamw_halves("adamw_" + n, wsh[n][0], args["m_" + n][0], args["v_" + n][0], gm, gs, cc)
        out_g[n], out_d[n], out_m[n], out_v[n] = (t[None] for t in res)
    sw = _pack_small({**{n: wsh[n] for n in SMALL}, "loss": jnp.zeros((D_MODEL,), F32)})
    sm = _pack_small({**{n: args["m_" + n] for n in SMALL}, "loss": jnp.zeros((D_MODEL,), F32)})
    sv = _pack_small({**{n: args["v_" + n] for n in SMALL}, "loss": jnp.ones((D_MODEL,), F32)})
    sd, snm, snv = _adamw("adamw_small", sw, red, sm, sv)
    ud, um, uv = (_unpack_small(t, small_shapes) for t in (sd, snm, snv))
    for n in SMALL:
        out_g[n], out_d[n], out_m[n], out_v[n] = small[n], ud[n], um[n], uv[n]

    return (loss, dx.reshape(B, S, D_MODEL), *[out_g[n] for n in WEIGHTS], *[out_d[n] for n in WEIGHTS],
            *[out_m[n] for n in WEIGHTS], *[out_v[n] for n in WEIGHTS])
```

```python
import numpy as np
import jax
import jax.numpy as jnp
from jax import lax
from jax.experimental import pallas as pl
from jax.experimental.pallas import tpu as pltpu

F32 = jnp.float32
BF16 = jnp.bfloat16
MESH = pl.DeviceIdType.MESH

D_MODEL = 1024
D_FF = 2816
FF_SHARD = 2 * D_FF // 4
HEAD_DIM = 64
N_Q_HEADS = 8
ATT_BLOCK = 128
N_BUCKETS = 32
MAX_DISTANCE = 128
REC_HEADS = 4
REC_DIM = 128
PLE_DIM = 256
EPS = 1e-6
IN_W = 4864
COL_AQ, COL_AK, COL_AV, COL_RQ, COL_RF, COL_RI, COL_RG, COL_GA, COL_GB = 0, 4, 5, 6, 10, 14, 18, 22, 30

CHUNK = 64
SUB = 8
N_SUB = CHUNK // SUB
HGRN_PAIR = 2

ADAM_LR, ADAM_B1, ADAM_B2, ADAM_EPS, ADAM_WD, ADAM_STEP = 0.001, 0.9, 0.999, 1e-08, 0.01, 10

V7X_VMEM_LIMIT = 56 * 1024 * 1024
N_CHIPS = 4
N_DEV = 8

BIG = ("w_ffn1_in", "w_ffn1_out", "w_in", "w_att_proj", "w_rec_proj", "w_out",
       "w_ffn2_in", "w_ffn2_out", "w_ple_gate", "w_ple_proj")
COL_SHARDED = ("w_ffn1_in", "w_in", "w_att_proj", "w_rec_proj", "w_ffn2_in", "w_ple_proj")
WEIGHTS = ("rel_bias", "lb_param", "norm_ffn1", "w_ffn1_in", "w_ffn1_out", "norm_mix", "w_in", "attn_sinks",
           "rec_norm", "w_att_proj", "w_rec_proj", "w_out", "norm_ffn2", "w_ffn2_in", "w_ffn2_out", "norm_ple",
           "w_ple_gate", "w_ple_proj", "norm_final")
SMALL = tuple(n for n in WEIGHTS if n not in BIG)
SMALL_ROWS = 64


def _params(*sem):
    return pltpu.CompilerParams(dimension_semantics=sem, vmem_limit_bytes=V7X_VMEM_LIMIT)


def _pick(n, cap, mult=8):
    if n <= cap:
        return n
    for t in range(cap - cap % mult, 0, -mult):
        if n % t == 0:
            return t
    raise ValueError((n, cap, mult))


def _dot(a, b):
    return jnp.dot(a, b, preferred_element_type=F32)


def _dot_nt(a, b):
    return lax.dot_general(a, b, (((1,), (1,)), ((), ())), preferred_element_type=F32)


def _dot_tn(a, b):
    return lax.dot_general(a, b, (((0,), (0,)), ((), ())), preferred_element_type=F32)


def _split3(x):
    hi = x.astype(BF16)
    r = x - hi.astype(F32)
    mid = r.astype(BF16)
    lo = (r - mid.astype(F32)).astype(BF16)
    return hi, mid, lo


def _split2(x):
    hi = x.astype(BF16)
    return hi, (x - hi.astype(F32)).astype(BF16)


def _sel_left(sel_bf16, x):
    hi, mid, lo = _split3(x)
    return _dot(sel_bf16, hi) + _dot(sel_bf16, mid) + _dot(sel_bf16, lo)


def _sel_right(x, sel_bf16):
    hi, mid, lo = _split3(x)
    return _dot(hi, sel_bf16) + _dot(mid, sel_bf16) + _dot(lo, sel_bf16)


def _sigmoid(x):
    return 1.0 / (1.0 + jnp.exp(-x))


def _group8(x):
    r, w = x.shape
    return x.reshape(r // 8, 8, w).sum(axis=0)


class _Comm:
    def __init__(self, ins, out_shapes, n_sems, start, finish):
        self.ins, self.out_shapes, self.n_sems, self.start, self.finish = ins, out_shapes, n_sems, start, finish


ANY = pl.BlockSpec(memory_space=pl.ANY)


def _comm_parts(comm):
    if comm is None:
        return [], [], [], []
    sems = [pltpu.SemaphoreType.DMA((comm.n_sems,)), pltpu.SemaphoreType.DMA((comm.n_sems,))]
    return list(comm.ins), [ANY] * len(comm.ins), list(comm.out_shapes), sems


def _comm_run(comm, grid, refs, n_in, n_out):
    if comm is None:
        return (lambda: None), (lambda: None)
    nci, nco = len(comm.ins), len(comm.out_shapes)
    cin = refs[n_in:n_in + nci]
    cout = refs[n_in + nci + n_out:n_in + nci + n_out + nco]
    send_sems, recv_sems = refs[-2], refs[-1]
    ids = [pl.program_id(d) for d in range(len(grid))]
    is_first = ids[0] == 0
    is_last = ids[0] == grid[0] - 1
    for d in range(1, len(grid)):
        is_first = is_first & (ids[d] == 0)
        is_last = is_last & (ids[d] == grid[d] - 1)

    def first():
        @pl.when(is_first)
        def _():
            comm.start(cin, cout, send_sems, recv_sems)

    def last():
        @pl.when(is_last)
        def _():
            comm.finish(cin, cout, send_sems, recv_sems)

    return first, last


def _call(name, fn, grid, ins, outs, pairs=(), comm=None, j_outer=False):
    in_pair = {i for p in pairs for i in p[:2]}
    n_in, n_out = len(ins), len(outs)
    c_arrays, c_in_specs, c_out_shapes, c_sems = _comm_parts(comm)

    def body(*refs):
        first, last = _comm_run(comm, grid, refs, n_in, n_out)
        first()
        accs = []
        for ia, ib, kind in pairs:
            a, b = refs[ia][...].astype(BF16), refs[ib][...].astype(BF16)
            accs.append(_dot(a, b) if kind == "nn" else _dot_nt(a, b))
        vals = [refs[i][...] for i in range(n_in) if i not in in_pair]
        res = fn(accs, vals)
        out_refs = refs[n_in + len(c_arrays):n_in + len(c_arrays) + n_out]
        assert len(res) == len(out_refs), (name, len(res), len(out_refs))
        for o_ref, val in zip(out_refs, res):
            o_ref[...] = val.astype(o_ref.dtype)
        last()

    if j_outer:
        grid = (grid[1], grid[0])
        swap = lambda im: (lambda j, i: im(i, j))
        ins = [(a, blk, swap(im)) for a, blk, im in ins]
        outs = [(shp, dt, blk, swap(im)) for shp, dt, blk, im in outs]

    return pl.pallas_call(
        body, name=name, grid=grid,
        in_specs=[pl.BlockSpec(blk, im) for _, blk, im in ins] + c_in_specs,
        out_specs=[pl.BlockSpec(blk, im) for _, _, blk, im in outs] + [ANY] * len(c_out_shapes),
        out_shape=[jax.ShapeDtypeStruct(shp, dt) for shp, dt, _, _ in outs] + c_out_shapes,
        scratch_shapes=c_sems,
        compiler_params=_params(*(["arbitrary"] * len(grid))))(*[a for a, _, _ in ins], *c_arrays)


def _tile_call(name, fn, M, N, tm, tn, *, pairs=(), tiles=(), consts=(), outs=(), parts=0, comm=None,
               j_outer=False):
    gi, gj = M // tm, N // tn
    assert gi * tm == M and gj * tn == N, (name, M, N, tm, tn)
    ins, prs = [], []
    for a, a_col, b, kind in pairs:
        K = b.shape[0] if kind == "nn" else b.shape[1]
        ins.append((a, (tm, K), lambda i, j, c=a_col: (i, c)))
        if kind == "nn":
            ins.append((b, (K, tn), lambda i, j: (0, j)))
        else:
            ins.append((b, (tn, K), lambda i, j: (j, 0)))
        prs.append((len(ins) - 2, len(ins) - 1, kind))
    for arr, off in tiles:
        ins.append((arr, (tm, tn), lambda i, j, o=off: (i, j + o)))
    for arr in consts:
        ins.append((arr, arr.shape, lambda i, j: (0, 0)))
    out_l = [((M, N), dt, (tm, tn), lambda i, j: (i, j)) for dt in outs]
    out_l += [((gi * 8, N), F32, (8, tn), lambda i, j: (i, j))] * parts
    nt = len(tiles)

    def wrapped(accs, vals):
        return fn(accs, vals[:nt], vals[nt:])

    return _call(name, wrapped, (gi, gj), ins, out_l, prs, comm=comm, j_outer=j_outer)


def _mm_tn(name, grid, a_in, b_in, outs):
    nk = grid[2]
    tm = [d for d in a_in[1] if d is not None][1]
    tn = [d for d in b_in[1] if d is not None][1]

    def body(a_ref, b_ref, *rest):
        out_refs, acc_ref = rest[:-1], rest[-1]
        k = pl.program_id(2)

        @pl.when(k == 0)
        def _():
            acc_ref[...] = jnp.zeros_like(acc_ref)

        acc_ref[...] += _dot_tn(a_ref[...].astype(BF16), b_ref[...].astype(BF16))

        @pl.when(k == nk - 1)
        def _():
            for o_ref in out_refs:
                o_ref[...] = acc_ref[...].astype(o_ref.dtype)

    return pl.pallas_call(
        body, name=name, grid=grid,
        in_specs=[pl.BlockSpec(a_in[1], a_in[2]), pl.BlockSpec(b_in[1], b_in[2])],
        out_specs=[pl.BlockSpec(blk, im) for _, _, blk, im in outs],
        out_shape=[jax.ShapeDtypeStruct(shp, dt) for shp, dt, _, _ in outs],
        scratch_shapes=[pltpu.VMEM((tm, tn), F32)],
        compiler_params=_params("arbitrary", "arbitrary", "arbitrary"))(a_in[0], b_in[0])


def _grad_pair(shape, block, imap):
    return [(shape, F32, block, imap), (shape, BF16, block, imap)]


def _mm_tn_rows(name, a, b, tk=1024):
    T, a_w = a.shape
    b_w = b.shape[1]
    tm = _pick(a_w, 1408, 128)
    tk = _pick(T, tk, 128)
    g32, g16 = _mm_tn(name, (a_w // tm, 1, T // tk),
                      (a, (tk, tm), lambda i, j, k: (k, i)), (b, (tk, b_w), lambda i, j, k: (k, 0)),
                      _grad_pair((a_w, b_w), (tm, b_w), lambda i, j, k: (i, 0)))
    shp = (N_CHIPS, a_w // N_CHIPS, b_w)
    return g32.reshape(shp), g16.reshape(shp)


def _mm_tn_cols(name, a, b, tk=1024):
    T, a_w = a.shape
    n = b.shape[1] // N_CHIPS
    tk = _pick(T, tk, 128)
    return _mm_tn(name, (1, N_CHIPS, T // tk),
                  (a, (tk, a_w), lambda i, j, k: (k, 0)), (b, (tk, n), lambda i, j, k: (k, j)),
                  _grad_pair((N_CHIPS, a_w, n), (None, a_w, n), lambda i, j, k: (j, 0, 0)))


def _colsum(name, x):
    def body(x_ref, o_ref):
        o_ref[...] = jnp.sum(x_ref[...], axis=0, keepdims=True)
    return pl.pallas_call(body, name=name, out_shape=jax.ShapeDtypeStruct((1, x.shape[1]), F32))(x)


def _rms_hat(h):
    return h * lax.rsqrt(jnp.mean(h * h, axis=-1, keepdims=True) + EPS)


def _rms_bwd_vals(dn, h, g):
    r = lax.rsqrt(jnp.mean(h * h, axis=-1, keepdims=True) + EPS)
    nh = h * r
    gd = dn * g
    dh = r * (gd - nh * jnp.mean(gd * nh, axis=-1, keepdims=True))
    return dh, _group8(dn * nh)


def _rms_fwd(name, h, g, tm=512):
    T = h.shape[0]

    def fn(accs, tv, cv):
        return [_rms_hat(tv[0]) * cv[0]]

    return _tile_call(name, fn, T, D_MODEL, _pick(T, tm), D_MODEL, tiles=[(h, 0)], consts=[g], outs=[BF16])[0]


def _ffn_fwd(tag, h, g, w_in, w_out, g_next, n=None, comm_in=None, comm_out=None, w_out_of=None):
    T = h.shape[0]
    if n is None:
        n = _rms_fwd(tag + "_norm", h, g)
    tm = _pick(T, 512)
    wblk = (None, D_MODEL, FF_SHARD)

    def act(accs, vals):
        gate, up = accs
        return [gate, up, gate * _sigmoid(gate) * up]

    tile = lambda: ((T, D_FF), BF16, (tm, FF_SHARD), lambda i, j: (i, j))
    gate, up, a, *got_in = _call(
        tag + "_in", act, (T // tm, 2),
        [(n, (tm, D_MODEL), lambda i, j: (i, 0)),
         (w_in, wblk, lambda i, j: (j, 0, 0)), (w_in, wblk, lambda i, j: (j + 2, 0, 0))],
        [tile(), tile(), tile()], pairs=[(0, 1, "nn"), (0, 2, "nn")], comm=comm_in, j_outer=True)

    def res(accs, tv, cv):
        h_new = tv[0] + 0.5 * accs[0]
        return [h_new, _rms_hat(h_new) * cv[0]]

    if w_out_of is not None:
        w_out = w_out_of(got_in)
    h_new, n_next, *got_out = _tile_call(
        tag + "_out", res, T, D_MODEL, _pick(T, 512), D_MODEL, pairs=[(a, 0, w_out, "nn")], tiles=[(h, 0)],
        consts=[g_next], outs=[F32, BF16], comm=comm_out)
    return h_new, n_next, (n, gate, up, a), got_out


def _ffn_bwd(tag, dh_out, df, h, g, w_in, w_out, saved, comm=None, comm_last=None):
    T = h.shape[0]
    n, gate, up, a = saved
    tm = _pick(T, 512)

    def dact(accs, vals):
        da = accs[0]
        gt, u = vals[0].astype(F32), vals[1].astype(F32)
        sg = _sigmoid(gt)
        silu = gt * sg
        return [jnp.stack([da * u * (sg + silu * (1.0 - sg)), da * silu])]

    dz, *got = _call(
        tag + "_dact", dact, (T // tm, 2),
        [(df, (tm, D_MODEL), lambda i, j: (i, 0)), (w_out, (FF_SHARD, D_MODEL), lambda i, j: (j, 0)),
         (gate, (tm, FF_SHARD), lambda i, j: (i, j)), (up, (tm, FF_SHARD), lambda i, j: (i, j))],
        [((2, T, D_FF), BF16, (2, tm, FF_SHARD), lambda i, j: (0, i, j))], pairs=[(0, 1, "nt")], comm=comm,
        j_outer=True)
    dw_out = _mm_tn_rows(tag + "_dwout", a, df)
    tk = _pick(T, 1024, 128)
    dw_in = _mm_tn(tag + "_dwin", (1, N_CHIPS, T // tk),
                   (n, (tk, D_MODEL), lambda i, j, k: (k, 0)),
                   (dz, (None, tk, FF_SHARD), lambda i, j, k: (j // 2, k, j % 2)),
                   _grad_pair((N_CHIPS, D_MODEL, FF_SHARD), (None, D_MODEL, FF_SHARD), lambda i, j, k: (j, 0, 0)))

    def dnorm(accs, vals):
        dn = accs[0] + accs[1] + accs[2] + accs[3]
        dh, dg = _rms_bwd_vals(dn, vals[0], vals[2])
        dh = vals[1] + dh
        return [dh, dh, dg]

    tm2 = _pick(T, 256)
    ins = [(dz, (None, tm2, FF_SHARD), lambda i, j, s=s: (s // 2, i, s % 2)) for s in range(N_CHIPS)]
    ins += [(w_in, (None, D_MODEL, FF_SHARD), lambda i, j, s=s: (s, 0, 0)) for s in range(N_CHIPS)]
    ins += [(h, (tm2, D_MODEL), lambda i, j: (i, 0)), (dh_out, (tm2, D_MODEL), lambda i, j: (i, 0)),
            (g, g.shape, lambda i, j: (0, 0))]
    dh, dh16, dg, *got_last = _call(
        tag + "_dnorm", dnorm, (T // tm2, 1), ins,
        [((T, D_MODEL), F32, (tm2, D_MODEL), lambda i, j: (i, 0)),
         ((T, D_MODEL), BF16, (tm2, D_MODEL), lambda i, j: (i, 0)),
         ((T // tm2 * 8, D_MODEL), F32, (8, D_MODEL), lambda i, j: (i, 0))],
        pairs=[(s, N_CHIPS + s, "nt") for s in range(N_CHIPS)],
        comm=None if comm_last is None else comm_last(dw_in, dw_out))
    return dh, dh16, dg, dw_in, dw_out, got, got_last


def _t5_onehot():
    qi = np.arange(ATT_BLOCK)[:, None] + ATT_BLOCK
    kj = np.arange(2 * ATT_BLOCK)[None, :]
    nn = np.maximum(qi - kj, 0)
    max_exact = N_BUCKETS // 2
    large = max_exact + (np.log(np.maximum(nn, 1) / max_exact) / np.log(MAX_DISTANCE / max_exact)
                         * (N_BUCKETS - max_exact)).astype(np.int32)
    large = np.minimum(large, N_BUCKETS - 1)
    bucket = np.where(nn < max_exact, nn, large).astype(np.int32).reshape(-1)
    return (bucket[None, :] == np.arange(N_BUCKETS)[:, None]).astype(np.float32)


def _small_mm(name, a, b, sel):
    def body(a_ref, b_ref, o_ref):
        if sel == "right":
            o_ref[...] = _sel_right(a_ref[...], b_ref[...])
        else:
            o_ref[...] = _sel_left(a_ref[...], b_ref[...])
    return pl.pallas_call(body, name=name, out_shape=jax.ShapeDtypeStruct((a.shape[0], b.shape[1]), F32),
                          compiler_params=pltpu.CompilerParams(vmem_limit_bytes=V7X_VMEM_LIMIT))(a, b)


def _dup_heads(t):
    a, b = t[:, :HEAD_DIM], t[:, HEAD_DIM:]
    return jnp.concatenate([a, a, b, b], axis=1)


def _kv_layouts(proj):
    T = proj.shape[0]

    def fn(accs, tv, cv):
        return [tv[0], tv[1]]

    k, v = _tile_call("kv_cast", fn, T, 128, _pick(T, 1024), 128, tiles=[(proj, COL_AK), (proj, COL_AV)],
                      outs=[BF16, BF16])
    return _dup_heads(k), _dup_heads(v)


def _swa_masks():
    row = lax.broadcasted_iota(jnp.int32, (ATT_BLOCK, 2 * ATT_BLOCK), 0)
    col = lax.broadcasted_iota(jnp.int32, (ATT_BLOCK, 2 * ATT_BLOCK), 1)
    dist = ATT_BLOCK + row - col
    return (dist >= 0) & (dist < ATT_BLOCK), col


GROUP = 4


def _stack_group(blk, lo_q):
    zero = jnp.zeros_like(blk[:, :128])
    rows = []
    for pair in range(GROUP // 2):
        pb = blk[:, 128 * pair:128 * (pair + 1)]
        rows += [jnp.where(lo_q, pb, zero), jnp.where(lo_q, zero, pb)]
    return jnp.concatenate(rows, axis=0)


def _unstack_group(st, lo_q):
    pairs = [jnp.where(lo_q, st[256 * pair:256 * pair + 128], st[256 * pair + 128:256 * (pair + 1)])
             for pair in range(GROUP // 2)]
    return jnp.concatenate(pairs, axis=1)


def _swa_probs(s, bias_h, sink, valid):
    s = jnp.where(valid, s * (HEAD_DIM ** -0.5) + bias_h, -jnp.inf)
    m = jnp.maximum(jnp.max(s, axis=-1, keepdims=True), sink)
    e = jnp.exp(s - m)
    es = jnp.exp(sink - m)
    den = jnp.sum(e, axis=-1, keepdims=True) + es
    return e / den, es / den


def _swa_fwd(proj, kk2, vv2, bias, sinks, B, S):
    T = B * S
    nb = S // ATT_BLOCK

    def body(q_ref, k_ref, v_ref, bias_ref, sink_ref, o_ref, kpad, vpad):
        zeros = jnp.zeros((ATT_BLOCK, 256), BF16)
        kpad[pl.ds(0, ATT_BLOCK), :] = zeros
        vpad[pl.ds(0, ATT_BLOCK), :] = zeros
        kpad[pl.ds(ATT_BLOCK, S), :] = k_ref[...]
        vpad[pl.ds(ATT_BLOCK, S), :] = v_ref[...]
        valid0, col = _swa_masks()
        lo_q = lax.broadcasted_iota(jnp.int32, (1, 128), 1) < HEAD_DIM

        def blk(n, carry):
            r0 = pl.multiple_of(n * ATT_BLOCK, ATT_BLOCK)
            rows = pl.ds(r0, ATT_BLOCK)
            valid = valid0 & ((n > 0) | (col >= ATT_BLOCK))
            for g in range(N_Q_HEADS // GROUP):
                lanes = pl.ds(128 * g, 128)
                kg = kpad[pl.ds(r0, 2 * ATT_BLOCK), lanes]
                vg = vpad[pl.ds(r0, 2 * ATT_BLOCK), lanes]
                qm = _stack_group(q_ref[rows, pl.ds(256 * g, 256)].astype(BF16), lo_q)
                s = _dot_nt(qm, kg)
                ps = []
                for i in range(GROUP):
                    h = GROUP * g + i
                    p, _ = _swa_probs(s[ATT_BLOCK * i:ATT_BLOCK * (i + 1)], bias_ref[h], sink_ref[h], valid)
                    ps.append(p.astype(BF16))
                o = _dot(jnp.concatenate(ps, axis=0), vg)
                o_ref[rows, pl.ds(256 * g, 256)] = _unstack_group(o, lo_q).astype(o_ref.dtype)
            return carry

        lax.fori_loop(0, nb, blk, 0)

    return pl.pallas_call(
        body, name="swa_fwd", grid=(B,),
        in_specs=[pl.BlockSpec((S, 512), lambda b: (b, 0)),
                  pl.BlockSpec((S, 256), lambda b: (b, 0)),
                  pl.BlockSpec((S, 256), lambda b: (b, 0)),
                  pl.BlockSpec((N_Q_HEADS, ATT_BLOCK, 2 * ATT_BLOCK), lambda b: (0, 0, 0)),
                  pl.BlockSpec(memory_space=pltpu.SMEM)],
        out_specs=pl.BlockSpec((S, 512), lambda b: (b, 0)),
        out_shape=jax.ShapeDtypeStruct((T, 512), BF16),
        scratch_shapes=[pltpu.VMEM((S + ATT_BLOCK, 256), BF16), pltpu.VMEM((S + ATT_BLOCK, 256), BF16)],
        compiler_params=_params("arbitrary"))(proj, kk2, vv2, bias, sinks)


def _swa_bwd(proj, kk2, vv2, bias, sinks, datt, B, S):
    T = B * S
    nb = S // ATT_BLOCK

    def body(q_ref, k_ref, v_ref, bias_ref, sink_ref, do_ref, dq_ref, dk_ref, dv_ref, dbias_ref, dsink_ref,
             kpad, vpad, dkpad, dvpad):
        b = pl.program_id(0)

        @pl.when(b == 0)
        def _():
            dbias_ref[...] = jnp.zeros_like(dbias_ref)
            dsink_ref[...] = jnp.zeros_like(dsink_ref)

        zeros = jnp.zeros((ATT_BLOCK, 256), BF16)
        kpad[pl.ds(0, ATT_BLOCK), :] = zeros
        vpad[pl.ds(0, ATT_BLOCK), :] = zeros
        kpad[pl.ds(ATT_BLOCK, S), :] = k_ref[...]
        vpad[pl.ds(ATT_BLOCK, S), :] = v_ref[...]
        dkpad[...] = jnp.zeros_like(dkpad)
        dvpad[...] = jnp.zeros_like(dvpad)
        valid0, col = _swa_masks()
        lo_q = lax.broadcasted_iota(jnp.int32, (1, 128), 1) < HEAD_DIM
        scale = HEAD_DIM ** -0.5

        def blk(n, carry):
            r0 = pl.multiple_of(n * ATT_BLOCK, ATT_BLOCK)
            rows = pl.ds(r0, ATT_BLOCK)
            band = pl.ds(r0, 2 * ATT_BLOCK)
            valid = valid0 & ((n > 0) | (col >= ATT_BLOCK))
            for g in range(N_Q_HEADS // GROUP):
                lanes = pl.ds(128 * g, 128)
                kg = kpad[band, lanes]
                vg = vpad[band, lanes]
                qm = _stack_group(q_ref[rows, pl.ds(256 * g, 256)].astype(BF16), lo_q)
                dom = _stack_group(do_ref[rows, pl.ds(256 * g, 256)], lo_q)
                s = _dot_nt(qm, kg)
                dp = _dot_nt(dom, vg)
                pst, dst = [], []
                for i in range(GROUP):
                    h = GROUP * g + i
                    sl = slice(ATT_BLOCK * i, ATT_BLOCK * (i + 1))
                    p, ps = _swa_probs(s[sl], bias_ref[h], sink_ref[h], valid)
                    delta = jnp.sum(p * dp[sl], axis=-1, keepdims=True)
                    ds = p * (dp[sl] - delta)
                    dbias_ref[h] += ds
                    dsink_ref[pl.ds(h, 1), :] += -jnp.sum(jnp.broadcast_to(ps * delta, (ATT_BLOCK, 128)),
                                                          axis=0, keepdims=True)
                    pst.append(p.astype(BF16))
                    dst.append((ds * scale).astype(BF16))
                pst, dst = jnp.concatenate(pst, axis=0), jnp.concatenate(dst, axis=0)
                dq_ref[rows, pl.ds(256 * g, 256)] = _unstack_group(_dot(dst, kg), lo_q).astype(dq_ref.dtype)
                dkpad[band, lanes] += _dot_tn(dst, qm)
                dvpad[band, lanes] += _dot_tn(pst, dom)
            return carry

        lax.fori_loop(0, nb, blk, 0)
        lo_out = lax.broadcasted_iota(jnp.int32, (1, 128), 1) < HEAD_DIM

        def fold(pad_ref):
            halves = []
            for g in range(N_Q_HEADS // GROUP):
                t = pad_ref[pl.ds(ATT_BLOCK, S), pl.ds(128 * g, 128)]
                halves.append(t + pltpu.roll(t, HEAD_DIM, 1))
            return jnp.where(lo_out, halves[0], halves[1])

        dk_ref[...] = fold(dkpad).astype(dk_ref.dtype)
        dv_ref[...] = fold(dvpad).astype(dv_ref.dtype)

    return pl.pallas_call(
        body, name="swa_bwd", grid=(B,),
        in_specs=[pl.BlockSpec((S, 512), lambda b: (b, 0)),
                  pl.BlockSpec((S, 256), lambda b: (b, 0)),
                  pl.BlockSpec((S, 256), lambda b: (b, 0)),
                  pl.BlockSpec((N_Q_HEADS, ATT_BLOCK, 2 * ATT_BLOCK), lambda b: (0, 0, 0)),
                  pl.BlockSpec(memory_space=pltpu.SMEM),
                  pl.BlockSpec((S, 512), lambda b: (b, 0))],
        out_specs=[pl.BlockSpec((S, 512), lambda b: (b, 0)),
                   pl.BlockSpec((S, 128), lambda b: (b, 0)),
                   pl.BlockSpec((S, 128), lambda b: (b, 0)),
                   pl.BlockSpec((N_Q_HEADS, ATT_BLOCK, 2 * ATT_BLOCK), lambda b: (0, 0, 0)),
                   pl.BlockSpec((N_Q_HEADS, 128), lambda b: (0, 0))],
        out_shape=[jax.ShapeDtypeStruct((T, 512), BF16),
                   jax.ShapeDtypeStruct((T, 128), BF16),
                   jax.ShapeDtypeStruct((T, 128), BF16),
                   jax.ShapeDtypeStruct((N_Q_HEADS, ATT_BLOCK, 2 * ATT_BLOCK), F32),
                   jax.ShapeDtypeStruct((N_Q_HEADS, 128), F32)],
        scratch_shapes=[pltpu.VMEM((S + ATT_BLOCK, 256), BF16), pltpu.VMEM((S + ATT_BLOCK, 256), BF16),
                        pltpu.VMEM((S + ATT_BLOCK, 256), F32), pltpu.VMEM((S + ATT_BLOCK, 256), F32)],
        compiler_params=_params("arbitrary"))(proj, kk2, vv2, bias, sinks, datt)


def _hgrn_gates(z, lb):
    sg = _sigmoid(z)
    f = lb + (1.0 - lb) * sg
    return sg, f, jnp.log(f), 1.0 - f


def _hgrn_consts():
    r = lax.broadcasted_iota(jnp.int32, (CHUNK, CHUNK), 0)
    c = lax.broadcasted_iota(jnp.int32, (CHUNK, CHUNK), 1)
    tril = (r >= c).astype(BF16)
    triu = (r <= c).astype(BF16)
    causal = r >= c
    below = (r // SUB) > (c // SUB)
    inside = ((r // SUB) == (c // SUB)) & causal
    return tril, triu, causal, below, inside, c


def _block_rows(ref, lanes, s):
    rows = []
    for i in range(N_SUB):
        if SUB * i + s < 0:
            rows.append(jnp.zeros((SUB, REC_DIM), F32))
        else:
            rows.append(jnp.broadcast_to(ref[pl.ds(SUB * i + s, 1), lanes], (SUB, REC_DIM)))
    return jnp.concatenate(rows, axis=0)


def _hgrn_offdiag(q, k, bcum, b_ref, lanes):
    eq = jnp.exp(jnp.minimum(bcum - _block_rows(b_ref, lanes, -1), 0.0))
    qe = q * eq
    zero = jnp.zeros((SUB, REC_DIM), F32)
    q_rows, k_cols, eks = [jnp.zeros((SUB, (N_SUB - 1) * REC_DIM), F32)], [], []
    for i in range(1, N_SUB):
        q_rows.append(jnp.concatenate([zero] * (i - 1) + [qe[SUB * i:SUB * (i + 1), :]] + [zero] * (N_SUB - 1 - i),
                                      axis=1))
        p = b_ref[pl.ds(SUB * i - 1, 1), lanes]
        pad = jnp.zeros((CHUNK - SUB * i, REC_DIM), F32)
        ek = jnp.concatenate([jnp.exp(p - b_ref[pl.ds(0, SUB * i), lanes]), pad], axis=0)
        k_cols.append(k * ek)
        eks.append(ek)
    return jnp.concatenate(q_rows, axis=0), jnp.concatenate(k_cols, axis=1), eq, eks


def _hgrn_fwd(proj, lb_param, B, S):
    T = B * S
    nc = S // CHUNK

    def body(q_ref, z_ref, v_ref, lb_ref, o_ref, st_ref, k_s, b_s):
        tril, _, _, below, inside, col = _hgrn_consts()
        col_s = col & (SUB - 1)

        def chunk(ci, hts):
            r0 = pl.multiple_of(ci * CHUNK, CHUNK)
            lb = _sigmoid(lb_ref[0:1, :] - lb_ref[1:2, :])
            _, _, g_all, k_all = _hgrn_gates(z_ref[pl.ds(r0, CHUNK), :], lb)
            b_all = _sel_left(tril, g_all)
            k_s[...] = k_all
            b_s[...] = b_all
            new = []
            for e, ht in enumerate(hts):
                lanes = pl.ds(REC_DIM * e, REC_DIM)
                cols = slice(REC_DIM * e, REC_DIM * (e + 1))
                q = q_ref[pl.ds(r0, CHUNK), lanes]
                v = v_ref[pl.ds(r0, CHUNK), lanes]
                k, bcum = k_all[:, cols], b_all[:, cols]
                st_ref[e * nc + ci] = ht
                qst, kst, _, _ = _hgrn_offdiag(q, k, bcum, b_s, lanes)
                d = jnp.zeros((CHUNK, CHUNK), F32)
                for s in range(SUB):
                    w = jnp.exp(jnp.minimum(bcum - _block_rows(b_s, lanes, s), 0.0))
                    colv = jnp.sum(q * _block_rows(k_s, lanes, s) * w, axis=-1, keepdims=True)
                    d = jnp.where(col_s == s, colv, d)
                a = jnp.where(below, _dot_nt(qst.astype(BF16), kst.astype(BF16)), 0.0) + jnp.where(inside, d, 0.0)
                vb = v.astype(BF16)
                qb = (q * jnp.exp(bcum)).astype(BF16)
                o_ref[pl.ds(r0, CHUNK), lanes] = _dot(a.astype(BF16), vb) + _dot_nt(qb, ht.astype(BF16))
                b_last = b_s[pl.ds(CHUNK - 1, 1), lanes]
                kb = (k * jnp.exp(b_last - bcum)).astype(BF16)
                new.append(ht * jnp.exp(b_last) + _dot_tn(vb, kb))
            return tuple(new)

        lax.fori_loop(0, nc, chunk, tuple(jnp.zeros((REC_DIM, REC_DIM), F32) for _ in range(HGRN_PAIR)))

    hp, wd = REC_HEADS // HGRN_PAIR, HGRN_PAIR * REC_DIM
    cq, cf, ci_ = (c * REC_DIM // wd for c in (COL_RQ, COL_RF, COL_RI))
    return pl.pallas_call(
        body, name="hgrn_fwd", grid=(B, hp),
        in_specs=[pl.BlockSpec((S, wd), lambda b, h: (b, cq + h)),
                  pl.BlockSpec((S, wd), lambda b, h: (b, cf + h)),
                  pl.BlockSpec((S, wd), lambda b, h: (b, ci_ + h)),
                  pl.BlockSpec((2, wd), lambda b, h: (0, h))],
        out_specs=[pl.BlockSpec((S, wd), lambda b, h: (b, h)),
                   pl.BlockSpec((HGRN_PAIR * nc, REC_DIM, REC_DIM), lambda b, h: (b * hp + h, 0, 0))],
        out_shape=[jax.ShapeDtypeStruct((T, 512), F32),
                   jax.ShapeDtypeStruct((B * REC_HEADS * nc, REC_DIM, REC_DIM), F32)],
        scratch_shapes=[pltpu.VMEM((CHUNK, wd), F32), pltpu.VMEM((CHUNK, wd), F32)],
        compiler_params=_params("arbitrary", "arbitrary"))(proj, proj, proj, lb_param)


def _hgrn_bwd(proj, lb_param, states, do, B, S, comm=None):
    T = B * S
    nc = S // CHUNK

    c_arrays, c_in_specs, c_out_shapes, c_sems = _comm_parts(comm)
    nci, nco = len(c_arrays), len(c_out_shapes)

    def body(*refs):
        q_ref, z_ref, v_ref, lb_ref, st_ref, do_ref = refs[:6]
        dq_ref, dz_ref, dv_ref, dlb_ref = refs[6 + nci:10 + nci]
        k_s, b_s = refs[10 + nci + nco:12 + nci + nco]
        comm_first, comm_last = _comm_run(comm, (B, REC_HEADS // HGRN_PAIR), refs, 6, 4)
        comm_first()
        tril, triu, causal, below, inside, col = _hgrn_consts()
        col_s = col & (SUB - 1)
        last_row = lax.broadcasted_iota(jnp.int32, (CHUNK, 1), 0) == CHUNK - 1
        rc = lax.broadcasted_iota(jnp.int32, (CHUNK, SUB * REC_DIM), 0)
        lc = lax.broadcasted_iota(jnp.int32, (CHUNK, SUB * REC_DIM), 1)
        spread = ((rc & (SUB - 1)) == (lc // REC_DIM)).astype(BF16)
        rr = lax.broadcasted_iota(jnp.int32, (CHUNK, SUB * CHUNK), 0)
        cc = lax.broadcasted_iota(jnp.int32, (CHUNK, SUB * CHUNK), 1)
        gather = (((rr // SUB) == ((cc & (CHUNK - 1)) // SUB)) & ((rr & (SUB - 1)) == (cc // CHUNK))).astype(BF16)

        heads = range(HGRN_PAIR)
        cols = [slice(REC_DIM * e, REC_DIM * (e + 1)) for e in heads]
        lanes = [pl.ds(REC_DIM * e, REC_DIM) for e in heads]
        lane_cat = lambda vals: jnp.concatenate(vals, axis=1)
        row_cat = lambda vals: jnp.concatenate(vals, axis=0)

        def chunk(it, carry):
            dhts, dlb = carry
            ci = nc - 1 - it
            r0 = pl.multiple_of(ci * CHUNK, CHUNK)
            rows = pl.ds(r0, CHUNK)
            lb = _sigmoid(lb_ref[0:1, :] - lb_ref[1:2, :])
            sg, f, g_all, k_all = _hgrn_gates(z_ref[rows, :], lb)
            b_all = _sel_left(tril, g_all)
            k_s[...] = k_all
            b_s[...] = b_all
            q_all = q_ref[rows, :]
            das, hd = [], []
            for e in heads:
                vb, dob = v_ref[rows, lanes[e]].astype(BF16), do_ref[rows, lanes[e]].astype(BF16)
                da = jnp.where(causal, _dot_nt(dob, vb), 0.0)
                das.append(jnp.where(inside, da, 0.0))
                hd.append((vb, dob, da))
            da_hi, da_lo = _split2(row_cat(das))
            da_in = _dot(da_hi, spread) + _dot(da_lo, spread)
            ds, dqs, pieces, pieces_lo = [], [], [[] for _ in range(SUB)], [[] for _ in range(SUB)]
            for e in heads:
                q, bcum = q_all[:, cols[e]], b_all[:, cols[e]]
                d = jnp.zeros((CHUNK, CHUNK), F32)
                dq = jnp.zeros((CHUNK, REC_DIM), F32)
                for s in range(SUB):
                    w = jnp.exp(jnp.minimum(bcum - _block_rows(b_s, lanes[e], s), 0.0))
                    ks = _block_rows(k_s, lanes[e], s)
                    qw = q * w
                    d = jnp.where(col_s == s, jnp.sum(qw * ks, axis=-1, keepdims=True), d)
                    da_s = da_in[CHUNK * e:CHUNK * (e + 1), REC_DIM * s:REC_DIM * (s + 1)]
                    dq = dq + da_s * ks * w
                    hi, lo = _split2(da_s * qw)
                    pieces[s].append(hi)
                    pieces_lo[s].append(lo)
                ds.append(d)
                dqs.append(dq)
            dk_in = (_dot(gather, row_cat([lane_cat(p) for p in pieces]))
                     + _dot(gather, row_cat([lane_cat(p) for p in pieces_lo])))
            dq_out, dk_out, dv_out, db_out, new_dhts = [], [], [], [], []
            for e in heads:
                q, k, bcum = q_all[:, cols[e]], k_all[:, cols[e]], b_all[:, cols[e]]
                vb, dob, da = hd[e]
                dht, ht = dhts[e], st_ref[e * nc + ci]
                qst, kst, eq, eks = _hgrn_offdiag(q, k, bcum, b_s, lanes[e])
                qst_b, kst_b = qst.astype(BF16), kst.astype(BF16)
                a = jnp.where(below, _dot_nt(qst_b, kst_b), 0.0) + jnp.where(inside, ds[e], 0.0)
                da_off = jnp.where(below, da, 0.0).astype(BF16)
                dqst = _dot(da_off, kst_b)
                dkst = _dot_tn(da_off, qst_b)
                dk = dk_in[:, cols[e]]
                dq_rows = [jnp.zeros((SUB, REC_DIM), F32)]
                for i in range(1, N_SUB):
                    dq_rows.append(dqst[SUB * i:SUB * (i + 1), REC_DIM * (i - 1):REC_DIM * i])
                    dk = dk + dkst[:, REC_DIM * (i - 1):REC_DIM * i] * eks[i - 1]
                dq = dqs[e] + row_cat(dq_rows) * eq
                eb = jnp.exp(bcum)
                b_last = b_s[pl.ds(CHUNK - 1, 1), lanes[e]]
                el = jnp.exp(b_last)
                ekb = jnp.exp(b_last - bcum)
                qb = (q * eb).astype(BF16)
                kb = k * ekb
                dhb = dht.astype(BF16)
                dv_out.append(_dot_tn(a.astype(BF16), dob) + _dot_nt(kb.astype(BF16), dhb))
                dqb = _dot(dob, ht.astype(BF16))
                dkb = _dot(vb, dhb)
                new_dhts.append(dht * el + _dot_tn(dob, qb))
                dq = dq + eb * dqb
                dk = dk + ekb * dkb
                edge = jnp.sum(kb * dkb, axis=0, keepdims=True) + el * jnp.sum(ht * dht, axis=0, keepdims=True)
                db_out.append(q * dq - k * dk + jnp.where(last_row, edge, 0.0))
                dq_out.append(dq)
                dk_out.append(dk)
            dk_all = lane_cat(dk_out)
            db_hi, db_lo = _split2(lane_cat(db_out))
            dg = _dot(triu, db_hi) + _dot(triu, db_lo)
            df = dg / f - dk_all
            dz_ref[rows, :] = (df * (1.0 - lb) * sg * (1.0 - sg)).astype(dz_ref.dtype)
            dq_ref[rows, :] = lane_cat(dq_out).astype(dq_ref.dtype)
            dv_ref[rows, :] = lane_cat(dv_out).astype(dv_ref.dtype)
            return tuple(new_dhts), dlb + jnp.sum(df * (1.0 - sg), axis=0, keepdims=True)

        zero = (tuple(jnp.zeros((REC_DIM, REC_DIM), F32) for _ in heads), jnp.zeros((1, HGRN_PAIR * REC_DIM), F32))
        _, dlb = lax.fori_loop(0, nc, chunk, zero)
        lb = _sigmoid(lb_ref[0:1, :] - lb_ref[1:2, :])
        dlb_ref[...] = jnp.broadcast_to(dlb * lb * (1.0 - lb), (8, HGRN_PAIR * REC_DIM))
        comm_last()

    hp, wd = REC_HEADS // HGRN_PAIR, HGRN_PAIR * REC_DIM
    cq, cf, ci_ = (c * REC_DIM // wd for c in (COL_RQ, COL_RF, COL_RI))
    return pl.pallas_call(
        body, name="hgrn_bwd", grid=(B, hp),
        in_specs=[pl.BlockSpec((S, wd), lambda b, h: (b, cq + h)),
                  pl.BlockSpec((S, wd), lambda b, h: (b, cf + h)),
                  pl.BlockSpec((S, wd), lambda b, h: (b, ci_ + h)),
                  pl.BlockSpec((2, wd), lambda b, h: (0, h)),
                  pl.BlockSpec((HGRN_PAIR * nc, REC_DIM, REC_DIM), lambda b, h: (b * hp + h, 0, 0)),
                  pl.BlockSpec((S, wd), lambda b, h: (b, h))] + c_in_specs,
        out_specs=[pl.BlockSpec((S, wd), lambda b, h: (b, h))] * 3
        + [pl.BlockSpec((8, wd), lambda b, h: (b, h))] + [ANY] * nco,
        out_shape=[jax.ShapeDtypeStruct((T, 512), BF16)] * 3 + [jax.ShapeDtypeStruct((B * 8, 512), F32)]
        + c_out_shapes,
        scratch_shapes=[pltpu.VMEM((CHUNK, wd), F32)] * 2 + c_sems,
        compiler_params=_params("arbitrary", "arbitrary"))(proj, proj, proj, lb_param, states, do, *c_arrays)


def _rec_gate_fwd(rec, proj, rec_norm):
    T = rec.shape[0]

    def fn(accs, tv, cv):
        return [_rms_hat(tv[0]) * cv[0] * _sigmoid(tv[1])]

    return _tile_call("rec_gate", fn, T, 512, _pick(T, 1024), REC_DIM, tiles=[(rec, 0), (proj, COL_RG)],
                      consts=[rec_norm], outs=[BF16])[0]


def _rec_gate_bwd(dyb, w_rec_proj, rec, proj, rec_norm):
    T = rec.shape[0]

    def fn(accs, tv, cv):
        d, r, rg = accs[0], tv[0], tv[1]
        sg = _sigmoid(rg)
        rn = _rms_hat(r) * cv[0]
        dh, dg = _rms_bwd_vals(d * sg, r, cv[0])
        return [dh, d * rn * sg * (1.0 - sg), dg]

    return _tile_call("rec_gate_bwd", fn, T, 512, _pick(T, 1024), REC_DIM, pairs=[(dyb, 0, w_rec_proj, "nt")],
                      tiles=[(rec, 0), (proj, COL_RG)], consts=[rec_norm], outs=[F32, BF16], parts=1)


def _mix_out_fwd(att, recn, proj, w_att_proj, w_rec_proj, w_out, h1, g_next):
    T = att.shape[0]
    tn = 256

    def merge(accs, tv, cv):
        ya, yb = accs
        return [ya, yb, _sigmoid(tv[0]) * ya + _sigmoid(tv[1]) * yb]

    ya, yb, merged = _tile_call(
        "merge", merge, T, D_MODEL, _pick(T, 1024), tn,
        pairs=[(att, 0, w_att_proj, "nn"), (recn, 0, w_rec_proj, "nn")],
        tiles=[(proj, COL_GA * 128 // tn), (proj, COL_GB * 128 // tn)], outs=[BF16] * 3)

    def res(accs, tv, cv):
        h2 = tv[0] + accs[0]
        return [h2, _rms_hat(h2) * cv[0]]

    h2, n2 = _tile_call("mix_out", res, T, D_MODEL, _pick(T, 512), D_MODEL, pairs=[(merged, 0, w_out, "nn")],
                        tiles=[(h1, 0)], consts=[g_next], outs=[F32, BF16])
    return h2, n2, (ya, yb, merged)


GATHER_FIRST = ("w_ffn1_in",)
GATHER_MIX = ("w_ffn1_out", "w_in", "w_att_proj", "w_rec_proj", "w_out")
GATHER_LAST = ("w_ffn2_in", "w_ffn2_out", "w_ple_gate", "w_ple_proj")
SCATTER_LATE = ("w_ple_gate", "w_ple_proj", "w_ffn2_in", "w_ffn2_out")
SCATTER_MIX = ("w_out", "w_att_proj", "w_rec_proj", "w_in")
SCATTER_LAST = ("w_ffn1_in", "w_ffn1_out")


def _local_step(x, p, tgt, w, mine16, cc, me_chip, B, S):
    T = B * S
    w = dict(w)
    g_ffn1, g_mix, g_ffn2, g_ple = w["norm_ffn1"], w["norm_mix"], w["norm_ffn2"], w["norm_ple"]
    g_fin = w["norm_final"].reshape(1, D_MODEL)
    grads, part, from_chips = {}, {}, {}

    def gather(names):
        return _gather_comm([mine16[n] for n in names])

    def place(names, got):
        for n, g in zip(names, got):
            full = lax.dynamic_update_index_in_dim(g, mine16[n], me_chip, 0)
            w[n] = full if n in ("w_ffn1_in", "w_ffn2_in") else _natural(n, full)

    def scatter(tag, names):
        from_sib = _swap_halves("rs_sibling_" + tag, [grads[n][1] for n in names])
        for n, fs in zip(names, from_sib):
            part[n] = _add_sibling("rs_add_sib_" + n, grads[n][0], fs, cc)
        return _scatter_comm([part[n][1] for n in names])

    def scattered(names, got):
        for n, g in zip(names, got):
            from_chips[n] = g

    place(GATHER_FIRST, _run_comm("gather_first", gather(GATHER_FIRST)))
    def ffn1_out_weight(got):
        place(GATHER_MIX, got)
        return w["w_ffn1_out"]

    h1, u, sv1, got_last = _ffn_fwd("ffn1", x, g_ffn1, w["w_ffn1_in"], None, g_mix, comm_in=gather(GATHER_MIX),
                                    comm_out=gather(GATHER_LAST), w_out_of=ffn1_out_weight)
    place(GATHER_LAST, got_last)

    def ident(accs, tv, cv):
        return [accs[0]]

    proj = _tile_call("in_proj", ident, T, IN_W, _pick(T, 512), IN_W // 2, pairs=[(u, 0, w["w_in"], "nn")],
                      outs=[F32], j_outer=True)[0]
    onehot = jnp.asarray(_t5_onehot())
    bias = _small_mm("t5_bias", w["rel_bias"].T, onehot.astype(BF16), "right")
    bias = bias.reshape(N_Q_HEADS, ATT_BLOCK, 2 * ATT_BLOCK)
    sinks = w["attn_sinks"].reshape(N_Q_HEADS)
    kk2, vv2 = _kv_layouts(proj)
    att = _swa_fwd(proj, kk2, vv2, bias, sinks, B, S)
    rec, states = _hgrn_fwd(proj, w["lb_param"], B, S)
    recn = _rec_gate_fwd(rec, proj, w["rec_norm"])
    h2, n2, (ya, yb, merged) = _mix_out_fwd(att, recn, proj, w["w_att_proj"], w["w_rec_proj"], w["w_out"], h1,
                                            g_ffn2)
    h3, n3, sv2, _ = _ffn_fwd("ffn2", h2, g_ffn2, w["w_ffn2_in"], w["w_ffn2_out"], g_ple, n=n2)

    def ple(accs, tv, cv):
        gate = _sigmoid(accs[0])
        return [gate, accs[1], tv[0] + gate * accs[1]]

    gate_p, pp, h4 = _tile_call(
        "ple", ple, T, D_MODEL, _pick(T, 512), 512,
        pairs=[(n3, 0, w["w_ple_gate"], "nn"), (p, 0, w["w_ple_proj"], "nn")], tiles=[(h3, 0)],
        outs=[BF16, BF16, F32])

    def head(accs, tv, cv):
        h, t, gt, ppv = tv[0], tv[1], tv[2].astype(F32), tv[3].astype(F32)
        err = _rms_hat(h) * cv[0] - t
        dh, dg = _rms_bwd_vals(err * (1.0 / D_MODEL), h, cv[0])
        return [dh, dh * ppv * gt * (1.0 - gt), dh * gt, _group8(err * err), dg]

    dh4, dzg, dpp, loss_p, dg_fin = _tile_call(
        "loss_head", head, T, D_MODEL, _pick(T, 256), D_MODEL,
        tiles=[(h4, 0), (tgt, 0), (gate_p, 0), (pp, 0)], consts=[g_fin], outs=[F32, BF16, BF16], parts=2)
    grads["norm_final"] = dg_fin

    grads["w_ple_gate"] = _mm_tn_rows("ple_dwg", n3, dzg)
    grads["w_ple_proj"] = _mm_tn_cols("ple_dwp", p, dpp)

    def dnorm(accs, tv, cv):
        dh, dg = _rms_bwd_vals(accs[0], tv[0], cv[0])
        dh = tv[1] + dh
        return [dh, 0.5 * dh, dg]

    dh3, df3, grads["norm_ple"] = _tile_call(
        "ple_dnorm", dnorm, T, D_MODEL, _pick(T, 256), D_MODEL, pairs=[(dzg, 0, w["w_ple_gate"], "nt")],
        tiles=[(h3, 0), (dh4, 0)], consts=[g_ple], outs=[F32, BF16], parts=1)

    dh2, dh2b, grads["norm_ffn2"], grads["w_ffn2_in"], grads["w_ffn2_out"], _, _ = _ffn_bwd(
        "ffn2b", dh3, df3, h2, g_ffn2, w["w_ffn2_in"], w["w_ffn2_out"], sv2)
    scatter_late = scatter("late", SCATTER_LATE)

    grads["w_out"] = _mm_tn_rows("mix_dwout", merged, dh2b)
    tn = 256

    def dmerge(accs, tv, cv):
        dm = accs[0]
        sa, sb = _sigmoid(tv[0]), _sigmoid(tv[1])
        yav, ybv = tv[2].astype(F32), tv[3].astype(F32)
        return [dm * sa, dm * sb, dm * yav * sa * (1.0 - sa), dm * ybv * sb * (1.0 - sb)]

    dya, dyb, dga, dgb = _tile_call(
        "mix_dmerge", dmerge, T, D_MODEL, _pick(T, 1024), tn, pairs=[(dh2b, 0, w["w_out"], "nt")],
        tiles=[(proj, COL_GA * 128 // tn), (proj, COL_GB * 128 // tn), (ya, 0), (yb, 0)], outs=[BF16] * 4)
    grads["w_att_proj"] = _mm_tn_cols("mix_dwatt", att, dya)
    grads["w_rec_proj"] = _mm_tn_cols("mix_dwrec", recn, dyb)

    datt = _tile_call("mix_datt", ident, T, 512, _pick(T, 1024), 512, pairs=[(dya, 0, w["w_att_proj"], "nt")],
                      outs=[BF16])[0]
    drec, drg, grads["rec_norm"] = _rec_gate_bwd(dyb, w["w_rec_proj"], rec, proj, w["rec_norm"])

    drq, drf, dri, dlb, *got = _hgrn_bwd(proj, w["lb_param"], states, drec, B, S, comm=scatter_late)
    scattered(SCATTER_LATE, got)
    grads["lb_param"] = dlb
    daq, dak, dav, dbias, dsink = _swa_bwd(proj, kk2, vv2, bias, sinks, datt, B, S)
    grads["attn_sinks"] = dsink
    grads["rel_bias"] = _small_mm("t5_dbias", dbias.reshape(N_Q_HEADS, -1), onehot.T.astype(BF16), "right")
    dproj = jnp.concatenate([daq, dak, dav, drq, drf, dri, drg, dga, dgb], axis=1)
    tk = _pick(T, 512, 128)
    w_in_shard = IN_W // N_CHIPS
    gw32, gw16 = _mm_tn("mix_dwin", (1, 2, T // tk),
                        (u, (tk, D_MODEL), lambda i, j, k: (k, 0)), (dproj, (tk, IN_W // 2), lambda i, j, k: (k, j)),
                        _grad_pair((D_MODEL, IN_W), (D_MODEL, IN_W // 2), lambda i, j, k: (0, j)))
    to_sh = lambda t: t.reshape(D_MODEL, N_CHIPS, w_in_shard).transpose(1, 0, 2)
    grads["w_in"] = (to_sh(gw32), to_sh(gw16))
    scatter_mix = scatter("mix", SCATTER_MIX)

    def dnorm_mix(accs, tv, cv):
        dh, dg = _rms_bwd_vals(accs[0], tv[0], cv[0])
        dh = tv[1] + dh
        return [dh, 0.5 * dh, dg]

    dh1, df1, grads["norm_mix"] = _tile_call(
        "mix_dnorm", dnorm_mix, T, D_MODEL, _pick(T, 256), D_MODEL, pairs=[(dproj, 0, w["w_in"], "nt")],
        tiles=[(h1, 0), (dh2, 0)], consts=[g_mix], outs=[F32, BF16], parts=1)

    def scatter_last(dw_in, dw_out):
        grads["w_ffn1_in"], grads["w_ffn1_out"] = dw_in, dw_out
        return scatter("last", SCATTER_LAST)

    dx, _, grads["norm_ffn1"], _, _, got, got_last = _ffn_bwd(
        "ffn1b", dh1, df1, x, g_ffn1, w["w_ffn1_in"], w["w_ffn1_out"], sv1, comm=scatter_mix, comm_last=scatter_last)
    scattered(SCATTER_MIX, got)
    scattered(SCATTER_LAST, got_last)
    return loss_p, dx, grads, part, from_chips


def _place():
    x, y, c = lax.axis_index("x"), lax.axis_index("y"), lax.axis_index("c")
    return x, y, c


def _other_chips(x, y):
    return [(1 - x, y, 2 * (1 - x) + y), (x, 1 - y, 2 * x + 1 - y), (1 - x, 1 - y, 2 * (1 - x) + 1 - y)]


def _half_rows(ref_3d, chip, h, rows):
    return ref_3d.at[chip, pl.ds(h * rows, rows), :]


def _run_comm(name, comm):
    nci, nco = len(comm.ins), len(comm.out_shapes)

    def body(*refs):
        cin, cout, send_sems, recv_sems = refs[:nci], refs[nci:nci + nco], refs[-2], refs[-1]
        comm.start(cin, cout, send_sems, recv_sems)
        comm.finish(cin, cout, send_sems, recv_sems)

    return pl.pallas_call(
        body, name=name, in_specs=[ANY] * nci, out_specs=[ANY] * nco, out_shape=list(comm.out_shapes),
        scratch_shapes=[pltpu.SemaphoreType.DMA((comm.n_sems,)), pltpu.SemaphoreType.DMA((comm.n_sems,))],
    )(*comm.ins)


def _gather_comm(ws):
    nw = len(ws)

    def parts(w_refs, out_refs, send_sems, recv_sems):
        x, y, c = _place()
        me = 2 * x + y
        chips = _other_chips(x, y)

        def copy(i, k, chip, h, to, src=None):
            half = ws[i].shape[0] // 2
            dst = _half_rows(out_refs[i], chip, h, half)
            return pltpu.make_async_remote_copy(
                src_ref=dst if src is None else src, dst_ref=dst,
                send_sem=send_sems.at[6 * i + k], recv_sem=recv_sems.at[6 * i + k], device_id=to, device_id_type=MESH)

        def first():
            out = []
            for i in range(nw):
                half = ws[i].shape[0] // 2
                out += [copy(i, j, me, c, (cx, cy, c), src=w_refs[i].at[pl.ds(c * half, half), :])
                        for j, (cx, cy, _) in enumerate(chips)]
            return out

        return copy, first, chips, c, (x, y, 1 - c)

    def start(*refs):
        _, first, _, _, _ = parts(*refs)
        for cp in first():
            cp.start()

    def finish(*refs):
        copy, first, chips, c, sibling = parts(*refs)
        passed = []
        for i in range(nw):
            for j, (cx, cy, ci) in enumerate(chips):
                copy(i, j, ci, c, (cx, cy, c)).wait_recv()
                fw = copy(i, 3 + j, ci, c, sibling)
                fw.start()
                passed.append(fw)
        for i in range(nw):
            for j, (_, _, ci) in enumerate(chips):
                copy(i, 3 + j, ci, 1 - c, sibling).wait_recv()
        for cp in first() + passed:
            cp.wait_send()

    return _Comm(list(ws), [jax.ShapeDtypeStruct((N_CHIPS,) + w.shape, w.dtype) for w in ws], 6 * nw, start, finish)


def _scatter_comm(ps):
    nw = len(ps)

    def copies(p_refs, out_refs, send_sems, recv_sems):
        x, y, c = _place()
        cps = []
        for i in range(nw):
            for j, (cx, cy, ci) in enumerate(_other_chips(x, y)):
                cps.append(pltpu.make_async_remote_copy(
                    src_ref=p_refs[i].at[ci], dst_ref=out_refs[i].at[j], send_sem=send_sems.at[3 * i + j],
                    recv_sem=recv_sems.at[3 * i + j], device_id=(cx, cy, c), device_id_type=MESH))
        return cps

    def start(*refs):
        for cp in copies(*refs):
            cp.start()

    def finish(*refs):
        for cp in copies(*refs):
            cp.wait()

    return _Comm(list(ps), [jax.ShapeDtypeStruct((3,) + p.shape[1:], p.dtype) for p in ps], 3 * nw, start, finish)


def _swap_halves(name, gs):
    nw = len(gs)

    def body(*refs):
        g_refs, out_refs, send_sems, recv_sems = refs[:nw], refs[nw:2 * nw], refs[2 * nw], refs[2 * nw + 1]
        x, y, c = _place()
        cps = []
        for i in range(nw):
            half = gs[i].shape[1] // 2
            cps.append(pltpu.make_async_remote_copy(
                src_ref=g_refs[i].at[:, pl.ds((1 - c) * half, half), :], dst_ref=out_refs[i],
                send_sem=send_sems.at[i], recv_sem=recv_sems.at[i], device_id=(x, y, 1 - c), device_id_type=MESH))
        for cp in cps:
            cp.start()
        for cp in cps:
            cp.wait()

    return pl.pallas_call(
        body, name=name, in_specs=[ANY] * nw, out_specs=[ANY] * nw,
        out_shape=[jax.ShapeDtypeStruct((N_CHIPS, g.shape[1] // 2, g.shape[2]), g.dtype) for g in gs],
        scratch_shapes=[pltpu.SemaphoreType.DMA((nw,)), pltpu.SemaphoreType.DMA((nw,))],
    )(*gs)


def _join_halves(name, ss):
    nw = len(ss)

    def body(*refs):
        s_refs, out_refs, send_sems, recv_sems = refs[:nw], refs[nw:2 * nw], refs[2 * nw], refs[2 * nw + 1]
        x, y, c = _place()
        cps = [pltpu.make_async_remote_copy(
            src_ref=s_refs[i], dst_ref=out_refs[i], send_sem=send_sems.at[i], recv_sem=recv_sems.at[i],
            device_id=(x, y, 1 - c), device_id_type=MESH) for i in range(nw)]
        for cp in cps:
            cp.start()
        for cp in cps:
            cp.wait()

    return pl.pallas_call(
        body, name=name, in_specs=[ANY] * nw, out_specs=[ANY] * nw,
        out_shape=[jax.ShapeDtypeStruct(s.shape, s.dtype) for s in ss],
        scratch_shapes=[pltpu.SemaphoreType.DMA((nw,)), pltpu.SemaphoreType.DMA((nw,))],
    )(*ss)


def _allreduce_small(sp):
    def body(s_ref, out_ref, slots, send_sems, recv_sems):
        x, y, c = _place()
        me = 4 * x + 2 * y + c
        slots[me] = s_ref[...]
        cps = []
        for r in range(1, N_DEV):
            px, py, pc = x ^ (r >> 2), y ^ ((r >> 1) & 1), c ^ (r & 1)
            cps.append(pltpu.make_async_remote_copy(
                src_ref=s_ref, dst_ref=slots.at[me], send_sem=send_sems.at[r - 1], recv_sem=recv_sems.at[r - 1],
                device_id=(px, py, pc), device_id_type=MESH))
        for cp in cps:
            cp.start()
        for r in range(1, N_DEV):
            px, py, pc = x ^ (r >> 2), y ^ ((r >> 1) & 1), c ^ (r & 1)
            pltpu.make_async_remote_copy(
                src_ref=s_ref, dst_ref=slots.at[4 * px + 2 * py + pc], send_sem=send_sems.at[r - 1],
                recv_sem=recv_sems.at[r - 1], device_id=(px, py, pc), device_id_type=MESH).wait_recv()
        for cp in cps:
            cp.wait_send()
        acc = slots[0]
        for d in range(1, N_DEV):
            acc = acc + slots[d]
        out_ref[...] = acc

    return pl.pallas_call(
        body, name="allreduce_small",
        in_specs=[pl.BlockSpec(memory_space=pltpu.VMEM)], out_specs=pl.BlockSpec(memory_space=pltpu.VMEM),
        out_shape=jax.ShapeDtypeStruct(sp.shape, F32),
        scratch_shapes=[pltpu.VMEM((N_DEV,) + sp.shape, F32), pltpu.SemaphoreType.DMA((N_DEV - 1,)),
                        pltpu.SemaphoreType.DMA((N_DEV - 1,))],
    )(sp)


def _scalar(v):
    return jnp.reshape(v, (1,)).astype(jnp.int32)


def _row_tile(h, dtype_mult=16):
    return _pick(h, 256, dtype_mult)


def _add_sibling(name, g32, from_sib, c):
    _, r, n = g32.shape
    h = r // 2
    th = _row_tile(h)
    nt = h // th

    def body(c_ref, g_ref, s_ref, o32_ref, o16_ref):
        s = g_ref[...] + s_ref[...].astype(F32)
        o32_ref[...] = s
        o16_ref[...] = s.astype(BF16)

    blk = (None, th, n)
    return pl.pallas_call(
        body, name=name,
        grid_spec=pltpu.PrefetchScalarGridSpec(
            num_scalar_prefetch=1, grid=(N_CHIPS, nt),
            in_specs=[pl.BlockSpec(blk, lambda k, t, c_ref: (k, c_ref[0] * nt + t, 0)),
                      pl.BlockSpec(blk, lambda k, t, c_ref: (k, t, 0))],
            out_specs=[pl.BlockSpec(blk, lambda k, t, c_ref: (k, t, 0))] * 2),
        out_shape=[jax.ShapeDtypeStruct((N_CHIPS, h, n), F32), jax.ShapeDtypeStruct((N_CHIPS, h, n), BF16)],
        compiler_params=_params("arbitrary", "arbitrary"))(_scalar(c), g32, from_sib)


def _add_chips(name, p32, from_chips, me_chip):
    _, h, n = p32.shape
    th = _row_tile(h)

    def body(m_ref, p_ref, a_ref, b_ref, c_ref, o_ref):
        o_ref[...] = p_ref[...] + a_ref[...].astype(F32) + b_ref[...].astype(F32) + c_ref[...].astype(F32)

    blk = (None, th, n)
    return pl.pallas_call(
        body, name=name,
        grid_spec=pltpu.PrefetchScalarGridSpec(
            num_scalar_prefetch=1, grid=(h // th,),
            in_specs=[pl.BlockSpec(blk, lambda t, m_ref: (m_ref[0], t, 0))]
            + [pl.BlockSpec(blk, lambda t, m_ref, j=j: (j, t, 0)) for j in range(3)],
            out_specs=pl.BlockSpec((th, n), lambda t, m_ref: (t, 0))),
        out_shape=jax.ShapeDtypeStruct((h, n), F32),
        compiler_params=_params("arbitrary"))(_scalar(me_chip), p32, from_chips, from_chips, from_chips)


def _adamw_vals(w, g, m, v):
    m = ADAM_B1 * m + (1.0 - ADAM_B1) * g
    v = ADAM_B2 * v + (1.0 - ADAM_B2) * (g * g)
    m_hat = m / (1.0 - ADAM_B1 ** ADAM_STEP)
    v_hat = v / (1.0 - ADAM_B2 ** ADAM_STEP)
    delta = -ADAM_LR * (m_hat / (jnp.sqrt(v_hat) + ADAM_EPS) + ADAM_WD * w)
    return delta, m, v


def _adamw_halves(name, w, m, v, g_mine, g_sib, c):
    r, n = w.shape
    h = r // 2
    th = _row_tile(h, 8)
    nt = h // th

    def body(c_ref, w_ref, m_ref, v_ref, a_ref, b_ref, g_ref, d_ref, nm_ref, nv_ref):
        mine = (pl.program_id(0) // nt) == c_ref[0]
        g = jnp.where(mine, a_ref[...], b_ref[...])
        d, nm, nv = _adamw_vals(w_ref[...], g, m_ref[...], v_ref[...])
        g_ref[...] = g
        d_ref[...] = d
        nm_ref[...] = nm
        nv_ref[...] = nv

    full = pl.BlockSpec((th, n), lambda t, c_ref: (t, 0))
    part = pl.BlockSpec((th, n), lambda t, c_ref: (t % nt, 0))
    return pl.pallas_call(
        body, name=name,
        grid_spec=pltpu.PrefetchScalarGridSpec(
            num_scalar_prefetch=1, grid=(2 * nt,), in_specs=[full, full, full, part, part], out_specs=[full] * 4),
        out_shape=[jax.ShapeDtypeStruct((r, n), F32)] * 4,
        compiler_params=_params("arbitrary"))(_scalar(c), w, m, v, g_mine, g_sib)


def _adamw(name, w, g, m, v):
    R, W = w.shape

    def fn(accs, tv, cv):
        return list(_adamw_vals(*tv))

    return _tile_call(name, fn, R, W, _pick(R, 256), W, tiles=[(w, 0), (g, 0), (m, 0), (v, 0)], outs=[F32] * 3)


SMALL_LAYOUT = (("rel_bias", 2, 256), ("lb_param", 8, 1024), ("norm_ffn1", 8, 1024), ("norm_mix", 8, 1024),
                ("attn_sinks", 1, 8), ("rec_norm", 1, 128), ("norm_ffn2", 8, 1024), ("norm_ple", 8, 1024),
                ("norm_final", 8, 1024), ("loss", 8, 1024))


def _pack_small(vals):
    rows = []
    for name, nrows, n in SMALL_LAYOUT:
        flat = vals[name].reshape(-1)
        flat = jnp.pad(flat, (0, nrows * 128 - n))
        rows.append(flat.reshape(nrows, 128))
    packed = jnp.concatenate(rows, axis=0)
    return jnp.pad(packed, ((0, SMALL_ROWS - packed.shape[0]), (0, 0)))


def _unpack_small(packed, shapes):
    out, r = {}, 0
    for name, nrows, n in SMALL_LAYOUT:
        out[name] = packed[r:r + nrows].reshape(-1)[:n].reshape(shapes[name])
        r += nrows
    return out


def _natural(name, s):
    if name in COL_SHARDED:
        return s.transpose(1, 0, 2).reshape(s.shape[1], -1)
    return s.reshape(-1, s.shape[2])


def kernel(x, p, rel_bias, lb_param, norm_ffn1, w_ffn1_in, w_ffn1_out, norm_mix, w_in, attn_sinks, rec_norm, w_att_proj, w_rec_proj, w_out, norm_ffn2, w_ffn2_in, w_ffn2_out, norm_ple, w_ple_gate, w_ple_proj, norm_final, loss_target, m_rel_bias, m_lb_param, m_norm_ffn1, m_w_ffn1_in, m_w_ffn1_out, m_norm_mix, m_w_in, m_attn_sinks, m_rec_norm, m_w_att_proj, m_w_rec_proj, m_w_out, m_norm_ffn2, m_w_ffn2_in, m_w_ffn2_out, m_norm_ple, m_w_ple_gate, m_w_ple_proj, m_norm_final, v_rel_bias, v_lb_param, v_norm_ffn1, v_w_ffn1_in, v_w_ffn1_out, v_norm_mix, v_w_in, v_attn_sinks, v_rec_norm, v_w_att_proj, v_w_rec_proj, v_w_out, v_norm_ffn2, v_w_ffn2_in, v_w_ffn2_out, v_norm_ple, v_w_ple_gate, v_w_ple_proj, v_norm_final):
    args = dict(locals())
    wsh = {n: args[n] for n in WEIGHTS}
    B, S = x.shape[0], x.shape[1]
    T = B * S
    cx, cy, cc = _place()
    me_chip = 2 * cx + cy

    mine16 = {n: wsh[n][0].astype(BF16) for n in BIG}
    loss_p, dx, grads, part, from_chips = _local_step(
        x.reshape(T, D_MODEL), p.reshape(T, PLE_DIM), loss_target.reshape(T, D_MODEL),
        {n: wsh[n] for n in SMALL}, mine16, cc, me_chip, B, S)

    s_mine = [_add_chips("rs_add_chips_" + n, part[n][0], from_chips[n], me_chip) for n in BIG]
    s_sib = _join_halves("rs_join", s_mine)

    small_vals = {
        "rel_bias": grads["rel_bias"].T,
        "lb_param": jnp.concatenate([_colsum("dlb_sum", grads["lb_param"]),
                                     -_colsum("dlb_sum2", grads["lb_param"])], axis=0) / 8.0,
        "attn_sinks": grads["attn_sinks"][:, 0],
        "rec_norm": _colsum("drn_sum", grads["rec_norm"]).reshape(REC_HEADS, REC_DIM).sum(axis=0),
        "loss": _colsum("loss_sum", loss_p),
    }
    for n in ("norm_ffn1", "norm_mix", "norm_ffn2", "norm_ple", "norm_final"):
        small_vals[n] = _colsum(n + "_sum", grads[n])
    red = _allreduce_small(_pack_small(small_vals))
    small_shapes = {n: wsh[n].shape for n in SMALL}
    small_shapes["loss"] = (D_MODEL,)
    small = _unpack_small(red, small_shapes)
    loss = 0.5 * jnp.sum(small["loss"]) / D_MODEL

    out_g, out_d, out_m, out_v = {}, {}, {}, {}
    for n, gm, gs in zip(BIG, s_mine, s_sib):
        res = _adamw_halves("adamw_" + n, wsh[n][0], args["m_" + n][0], args["v_" + n][0], gm, gs, cc)
        out_g[n], out_d[n], out_m[n], out_v[n] = (t[None] for t in res)
    sw = _pack_small({**{n: wsh[n] for n in SMALL}, "loss": jnp.zeros((D_MODEL,), F32)})
    sm = _pack_small({**{n: args["m_" + n] for n in SMALL}, "loss": jnp.zeros((D_MODEL,), F32)})
    sv = _pack_small({**{n: args["v_" + n] for n in SMALL}, "loss": jnp.ones((D_MODEL,), F32)})
    sd, snm, snv = _adamw("adamw_small", sw, red, sm, sv)
    ud, um, uv = (_unpack_small(t, small_shapes) for t in (sd, snm, snv))
    for n in SMALL:
        out_g[n], out_d[n], out_m[n], out_v[n] = small[n], ud[n], um[n], uv[n]

    return (loss, dx.reshape(B, S, D_MODEL), *[out_g[n] for n in WEIGHTS], *[out_d[n] for n in WEIGHTS],
            *[out_m[n] for n in WEIGHTS], *[out_v[n] for n in WEIGHTS])
```

```python
import numpy as np
import jax
import jax.numpy as jnp
from jax import lax
from jax.experimental import pallas as pl
from jax.experimental.pallas import tpu as pltpu

F32 = jnp.float32
BF16 = jnp.bfloat16
MESH = pl.DeviceIdType.MESH

D_MODEL = 1024
D_FF = 2816
FF_SHARD = 2 * D_FF // 4
HEAD_DIM = 64
N_Q_HEADS = 8
ATT_BLOCK = 128
N_BUCKETS = 32
MAX_DISTANCE = 128
REC_HEADS = 4
REC_DIM = 128
PLE_DIM = 256
EPS = 1e-6
IN_W = 4864
COL_AQ, COL_AK, COL_AV, COL_RQ, COL_RF, COL_RI, COL_RG, COL_GA, COL_GB = 0, 4, 5, 6, 10, 14, 18, 22, 30

CHUNK = 64
SUB = 8
N_SUB = CHUNK // SUB
HGRN_PAIR = 2

ADAM_LR, ADAM_B1, ADAM_B2, ADAM_EPS, ADAM_WD, ADAM_STEP = 0.001, 0.9, 0.999, 1e-08, 0.01, 10

V7X_VMEM_LIMIT = 56 * 1024 * 1024
N_CHIPS = 4
N_DEV = 8

BIG = ("w_ffn1_in", "w_ffn1_out", "w_in", "w_att_proj", "w_rec_proj", "w_out",
       "w_ffn2_in", "w_ffn2_out", "w_ple_gate", "w_ple_proj")
COL_SHARDED = ("w_ffn1_in", "w_in", "w_att_proj", "w_rec_proj", "w_ffn2_in", "w_ple_proj")
WEIGHTS = ("rel_bias", "lb_param", "norm_ffn1", "w_ffn1_in", "w_ffn1_out", "norm_mix", "w_in", "attn_sinks",
           "rec_norm", "w_att_proj", "w_rec_proj", "w_out", "norm_ffn2", "w_ffn2_in", "w_ffn2_out", "norm_ple",
           "w_ple_gate", "w_ple_proj", "norm_final")
SMALL = tuple(n for n in WEIGHTS if n not in BIG)
SMALL_ROWS = 64


def _params(*sem):
    return pltpu.CompilerParams(dimension_semantics=sem, vmem_limit_bytes=V7X_VMEM_LIMIT)


def _pick(n, cap, mult=8):
    if n <= cap:
        return n
    for t in range(cap - cap % mult, 0, -mult):
        if n % t == 0:
            return t
    raise ValueError((n, cap, mult))


def _dot(a, b):
    return jnp.dot(a, b, preferred_element_type=F32)


def _dot_nt(a, b):
    return lax.dot_general(a, b, (((1,), (1,)), ((), ())), preferred_element_type=F32)


def _dot_tn(a, b):
    return lax.dot_general(a, b, (((0,), (0,)), ((), ())), preferred_element_type=F32)


def _split3(x):
    hi = x.astype(BF16)
    r = x - hi.astype(F32)
    mid = r.astype(BF16)
    lo = (r - mid.astype(F32)).astype(BF16)
    return hi, mid, lo


def _split2(x):
    hi = x.astype(BF16)
    return hi, (x - hi.astype(F32)).astype(BF16)


def _sel_left(sel_bf16, x):
    hi, mid, lo = _split3(x)
    return _dot(sel_bf16, hi) + _dot(sel_bf16, mid) + _dot(sel_bf16, lo)


def _sel_right(x, sel_bf16):
    hi, mid, lo = _split3(x)
    return _dot(hi, sel_bf16) + _dot(mid, sel_bf16) + _dot(lo, sel_bf16)


def _sigmoid(x):
    return 0.5 * jnp.tanh(0.5 * x) + 0.5


def _group8(x):
    r, w = x.shape
    return x.reshape(r // 8, 8, w).sum(axis=0)


class _Comm:
    def __init__(self, ins, out_shapes, n_sems, start, finish):
        self.ins, self.out_shapes, self.n_sems, self.start, self.finish = ins, out_shapes, n_sems, start, finish


ANY = pl.BlockSpec(memory_space=pl.ANY)


def _comm_parts(comm):
    if comm is None:
        return [], [], [], []
    sems = [pltpu.SemaphoreType.DMA((comm.n_sems,)), pltpu.SemaphoreType.DMA((comm.n_sems,))]
    return list(comm.ins), [ANY] * len(comm.ins), list(comm.out_shapes), sems


def _comm_run(comm, grid, refs, n_in, n_out):
    if comm is None:
        return (lambda: None), (lambda: None)
    nci, nco = len(comm.ins), len(comm.out_shapes)
    cin = refs[n_in:n_in + nci]
    cout = refs[n_in + nci + n_out:n_in + nci + n_out + nco]
    send_sems, recv_sems = refs[-2], refs[-1]
    ids = [pl.program_id(d) for d in range(len(grid))]
    is_first = ids[0] == 0
    is_last = ids[0] == grid[0] - 1
    for d in range(1, len(grid)):
        is_first = is_first & (ids[d] == 0)
        is_last = is_last & (ids[d] == grid[d] - 1)

    def first():
        @pl.when(is_first)
        def _():
            comm.start(cin, cout, send_sems, recv_sems)

    def last():
        @pl.when(is_last)
        def _():
            comm.finish(cin, cout, send_sems, recv_sems)

    return first, last


def _call(name, fn, grid, ins, outs, pairs=(), comm=None, j_outer=False):
    in_pair = {i for p in pairs for i in p[:2]}
    n_in, n_out = len(ins), len(outs)
    c_arrays, c_in_specs, c_out_shapes, c_sems = _comm_parts(comm)

    def body(*refs):
        first, last = _comm_run(comm, grid, refs, n_in, n_out)
        first()
        accs = []
        for ia, ib, kind in pairs:
            a, b = refs[ia][...].astype(BF16), refs[ib][...].astype(BF16)
            accs.append(_dot(a, b) if kind == "nn" else _dot_nt(a, b))
        vals = [refs[i][...] for i in range(n_in) if i not in in_pair]
        res = fn(accs, vals)
        out_refs = refs[n_in + len(c_arrays):n_in + len(c_arrays) + n_out]
        assert len(res) == len(out_refs), (name, len(res), len(out_refs))
        for o_ref, val in zip(out_refs, res):
            o_ref[...] = val.astype(o_ref.dtype)
        last()

    if j_outer:
        grid = (grid[1], grid[0])
        swap = lambda im: (lambda j, i: im(i, j))
        ins = [(a, blk, swap(im)) for a, blk, im in ins]
        outs = [(shp, dt, blk, swap(im)) for shp, dt, blk, im in outs]

    return pl.pallas_call(
        body, name=name, grid=grid,
        in_specs=[pl.BlockSpec(blk, im) for _, blk, im in ins] + c_in_specs,
        out_specs=[pl.BlockSpec(blk, im) for _, _, blk, im in outs] + [ANY] * len(c_out_shapes),
        out_shape=[jax.ShapeDtypeStruct(shp, dt) for shp, dt, _, _ in outs] + c_out_shapes,
        scratch_shapes=c_sems,
        compiler_params=_params(*(["arbitrary"] * len(grid))))(*[a for a, _, _ in ins], *c_arrays)


def _tile_call(name, fn, M, N, tm, tn, *, pairs=(), tiles=(), consts=(), outs=(), parts=0, comm=None,
               j_outer=False):
    gi, gj = M // tm, N // tn
    assert gi * tm == M and gj * tn == N, (name, M, N, tm, tn)
    ins, prs = [], []
    for a, a_col, b, kind in pairs:
        K = b.shape[0] if kind == "nn" else b.shape[1]
        ins.append((a, (tm, K), lambda i, j, c=a_col: (i, c)))
        if kind == "nn":
            ins.append((b, (K, tn), lambda i, j: (0, j)))
        else:
            ins.append((b, (tn, K), lambda i, j: (j, 0)))
        prs.append((len(ins) - 2, len(ins) - 1, kind))
    for arr, off in tiles:
        ins.append((arr, (tm, tn), lambda i, j, o=off: (i, j + o)))
    for arr in consts:
        ins.append((arr, arr.shape, lambda i, j: (0, 0)))
    out_l = [((M, N), dt, (tm, tn), lambda i, j: (i, j)) for dt in outs]
    out_l += [((gi * 8, N), F32, (8, tn), lambda i, j: (i, j))] * parts
    nt = len(tiles)

    def wrapped(accs, vals):
        return fn(accs, vals[:nt], vals[nt:])

    return _call(name, wrapped, (gi, gj), ins, out_l, prs, comm=comm, j_outer=j_outer)


def _mm_tn(name, grid, a_in, b_in, outs):
    nk = grid[2]
    tm = [d for d in a_in[1] if d is not None][1]
    tn = [d for d in b_in[1] if d is not None][1]

    def body(a_ref, b_ref, *rest):
        out_refs, acc_ref = rest[:-1], rest[-1]
        k = pl.program_id(2)

        @pl.when(k == 0)
        def _():
            acc_ref[...] = jnp.zeros_like(acc_ref)

        acc_ref[...] += _dot_tn(a_ref[...].astype(BF16), b_ref[...].astype(BF16))

        @pl.when(k == nk - 1)
        def _():
            for o_ref in out_refs:
                o_ref[...] = acc_ref[...].astype(o_ref.dtype)

    return pl.pallas_call(
        body, name=name, grid=grid,
        in_specs=[pl.BlockSpec(a_in[1], a_in[2]), pl.BlockSpec(b_in[1], b_in[2])],
        out_specs=[pl.BlockSpec(blk, im) for _, _, blk, im in outs],
        out_shape=[jax.ShapeDtypeStruct(shp, dt) for shp, dt, _, _ in outs],
        scratch_shapes=[pltpu.VMEM((tm, tn), F32)],
        compiler_params=_params("arbitrary", "arbitrary", "arbitrary"))(a_in[0], b_in[0])


def _grad_pair(shape, block, imap):
    return [(shape, F32, block, imap), (shape, BF16, block, imap)]


def _mm_tn_rows(name, a, b, tk=1024):
    T, a_w = a.shape
    b_w = b.shape[1]
    tm = _pick(a_w, 1408, 128)
    tk = _pick(T, tk, 128)
    g32, g16 = _mm_tn(name, (a_w // tm, 1, T // tk),
                      (a, (tk, tm), lambda i, j, k: (k, i)), (b, (tk, b_w), lambda i, j, k: (k, 0)),
                      _grad_pair((a_w, b_w), (tm, b_w), lambda i, j, k: (i, 0)))
    shp = (N_CHIPS, a_w // N_CHIPS, b_w)
    return g32.reshape(shp), g16.reshape(shp)


def _mm_tn_cols(name, a, b, tk=1024):
    T, a_w = a.shape
    n = b.shape[1] // N_CHIPS
    tk = _pick(T, tk, 128)
    return _mm_tn(name, (1, N_CHIPS, T // tk),
                  (a, (tk, a_w), lambda i, j, k: (k, 0)), (b, (tk, n), lambda i, j, k: (k, j)),
                  _grad_pair((N_CHIPS, a_w, n), (None, a_w, n), lambda i, j, k: (j, 0, 0)))


def _colsum(name, x):
    def body(x_ref, o_ref):
        o_ref[...] = jnp.sum(x_ref[...], axis=0, keepdims=True)
    return pl.pallas_call(body, name=name, out_shape=jax.ShapeDtypeStruct((1, x.shape[1]), F32))(x)


def _rms_hat(h):
    return h * lax.rsqrt(jnp.mean(h * h, axis=-1, keepdims=True) + EPS)


def _rms_bwd_vals(dn, h, g):
    r = lax.rsqrt(jnp.mean(h * h, axis=-1, keepdims=True) + EPS)
    nh = h * r
    gd = dn * g
    dh = r * (gd - nh * jnp.mean(gd * nh, axis=-1, keepdims=True))
    return dh, _group8(dn * nh)


def _rms_fwd(name, h, g, tm=512):
    T = h.shape[0]

    def fn(accs, tv, cv):
        return [_rms_hat(tv[0]) * cv[0]]

    return _tile_call(name, fn, T, D_MODEL, _pick(T, tm), D_MODEL, tiles=[(h, 0)], consts=[g], outs=[BF16])[0]


def _ffn_fwd(tag, h, g, w_in, w_out, g_next, n=None, comm_in=None, comm_out=None, w_out_of=None):
    T = h.shape[0]
    if n is None:
        n = _rms_fwd(tag + "_norm", h, g)
    tm = _pick(T, 512)
    wblk = (None, D_MODEL, FF_SHARD)

    def act(accs, vals):
        gate, up = accs
        return [gate, up, gate * _sigmoid(gate) * up]

    tile = lambda: ((T, D_FF), BF16, (tm, FF_SHARD), lambda i, j: (i, j))
    gate, up, a, *got_in = _call(
        tag + "_in", act, (T // tm, 2),
        [(n, (tm, D_MODEL), lambda i, j: (i, 0)),
         (w_in, wblk, lambda i, j: (j, 0, 0)), (w_in, wblk, lambda i, j: (j + 2, 0, 0))],
        [tile(), tile(), tile()], pairs=[(0, 1, "nn"), (0, 2, "nn")], comm=comm_in, j_outer=True)

    def res(accs, tv, cv):
        h_new = tv[0] + 0.5 * accs[0]
        return [h_new, _rms_hat(h_new) * cv[0]]

    if w_out_of is not None:
        w_out = w_out_of(got_in)
    h_new, n_next, *got_out = _tile_call(
        tag + "_out", res, T, D_MODEL, _pick(T, 512), D_MODEL, pairs=[(a, 0, w_out, "nn")], tiles=[(h, 0)],
        consts=[g_next], outs=[F32, BF16], comm=comm_out)
    return h_new, n_next, (n, gate, up, a), got_out


def _ffn_bwd(tag, dh_out, df, h, g, w_in, w_out, saved, comm=None, comm_last=None):
    T = h.shape[0]
    n, gate, up, a = saved
    tm = _pick(T, 512)

    def dact(accs, vals):
        da = accs[0]
        gt, u = vals[0].astype(F32), vals[1].astype(F32)
        sg = _sigmoid(gt)
        silu = gt * sg
        return [jnp.stack([da * u * (sg + silu * (1.0 - sg)), da * silu])]

    dz, *got = _call(
        tag + "_dact", dact, (T // tm, 2),
        [(df, (tm, D_MODEL), lambda i, j: (i, 0)), (w_out, (FF_SHARD, D_MODEL), lambda i, j: (j, 0)),
         (gate, (tm, FF_SHARD), lambda i, j: (i, j)), (up, (tm, FF_SHARD), lambda i, j: (i, j))],
        [((2, T, D_FF), BF16, (2, tm, FF_SHARD), lambda i, j: (0, i, j))], pairs=[(0, 1, "nt")], comm=comm,
        j_outer=True)
    dw_out = _mm_tn_rows(tag + "_dwout", a, df)
    tk = _pick(T, 1024, 128)
    dw_in = _mm_tn(tag + "_dwin", (1, N_CHIPS, T // tk),
                   (n, (tk, D_MODEL), lambda i, j, k: (k, 0)),
                   (dz, (None, tk, FF_SHARD), lambda i, j, k: (j // 2, k, j % 2)),
                   _grad_pair((N_CHIPS, D_MODEL, FF_SHARD), (None, D_MODEL, FF_SHARD), lambda i, j, k: (j, 0, 0)))

    def dnorm(accs, vals):
        dn = accs[0] + accs[1] + accs[2] + accs[3]
        dh, dg = _rms_bwd_vals(dn, vals[0], vals[2])
        dh = vals[1] + dh
        return [dh, dh, dg]

    tm2 = _pick(T, 256)
    ins = [(dz, (None, tm2, FF_SHARD), lambda i, j, s=s: (s // 2, i, s % 2)) for s in range(N_CHIPS)]
    ins += [(w_in, (None, D_MODEL, FF_SHARD), lambda i, j, s=s: (s, 0, 0)) for s in range(N_CHIPS)]
    ins += [(h, (tm2, D_MODEL), lambda i, j: (i, 0)), (dh_out, (tm2, D_MODEL), lambda i, j: (i, 0)),
            (g, g.shape, lambda i, j: (0, 0))]
    dh, dh16, dg, *got_last = _call(
        tag + "_dnorm", dnorm, (T // tm2, 1), ins,
        [((T, D_MODEL), F32, (tm2, D_MODEL), lambda i, j: (i, 0)),
         ((T, D_MODEL), BF16, (tm2, D_MODEL), lambda i, j: (i, 0)),
         ((T // tm2 * 8, D_MODEL), F32, (8, D_MODEL), lambda i, j: (i, 0))],
        pairs=[(s, N_CHIPS + s, "nt") for s in range(N_CHIPS)],
        comm=None if comm_last is None else comm_last(dw_in, dw_out))
    return dh, dh16, dg, dw_in, dw_out, got, got_last


def _t5_onehot():
    qi = np.arange(ATT_BLOCK)[:, None] + ATT_BLOCK
    kj = np.arange(2 * ATT_BLOCK)[None, :]
    nn = np.maximum(qi - kj, 0)
    max_exact = N_BUCKETS // 2
    large = max_exact + (np.log(np.maximum(nn, 1) / max_exact) / np.log(MAX_DISTANCE / max_exact)
                         * (N_BUCKETS - max_exact)).astype(np.int32)
    large = np.minimum(large, N_BUCKETS - 1)
    bucket = np.where(nn < max_exact, nn, large).astype(np.int32).reshape(-1)
    return (bucket[None, :] == np.arange(N_BUCKETS)[:, None]).astype(np.float32)


def _small_mm(name, a, b, sel):
    def body(a_ref, b_ref, o_ref):
        if sel == "right":
            o_ref[...] = _sel_right(a_ref[...], b_ref[...])
        else:
            o_ref[...] = _sel_left(a_ref[...], b_ref[...])
    return pl.pallas_call(body, name=name, out_shape=jax.ShapeDtypeStruct((a.shape[0], b.shape[1]), F32),
                          compiler_params=pltpu.CompilerParams(vmem_limit_bytes=V7X_VMEM_LIMIT))(a, b)


def _dup_heads(t):
    a, b = t[:, :HEAD_DIM], t[:, HEAD_DIM:]
    return jnp.concatenate([a, a, b, b], axis=1)


def _kv_layouts(proj):
    T = proj.shape[0]

    def fn(accs, tv, cv):
        return [tv[0], tv[1]]

    k, v = _tile_call("kv_cast", fn, T, 128, _pick(T, 1024), 128, tiles=[(proj, COL_AK), (proj, COL_AV)],
                      outs=[BF16, BF16])
    return _dup_heads(k), _dup_heads(v)


def _swa_masks():
    row = lax.broadcasted_iota(jnp.int32, (ATT_BLOCK, 2 * ATT_BLOCK), 0)
    col = lax.broadcasted_iota(jnp.int32, (ATT_BLOCK, 2 * ATT_BLOCK), 1)
    dist = ATT_BLOCK + row - col
    return (dist >= 0) & (dist < ATT_BLOCK), col


GROUP = 4


def _stack_group(blk, lo_q):
    zero = jnp.zeros_like(blk[:, :128])
    rows = []
    for pair in range(GROUP // 2):
        pb = blk[:, 128 * pair:128 * (pair + 1)]
        rows += [jnp.where(lo_q, pb, zero), jnp.where(lo_q, zero, pb)]
    return jnp.concatenate(rows, axis=0)


def _unstack_group(st, lo_q):
    pairs = [jnp.where(lo_q, st[256 * pair:256 * pair + 128], st[256 * pair + 128:256 * (pair + 1)])
             for pair in range(GROUP // 2)]
    return jnp.concatenate(pairs, axis=1)


def _swa_probs(s, bias_h, sink, valid):
    s = jnp.where(valid, s * (HEAD_DIM ** -0.5) + bias_h, -jnp.inf)
    m = jnp.maximum(jnp.max(s, axis=-1, keepdims=True), sink)
    e = jnp.exp(s - m)
    es = jnp.exp(sink - m)
    den = jnp.sum(e, axis=-1, keepdims=True) + es
    return e / den, es / den


def _swa_fwd(proj, kk2, vv2, bias, sinks, B, S):
    T = B * S
    nb = S // ATT_BLOCK

    def body(q_ref, k_ref, v_ref, bias_ref, sink_ref, o_ref, kpad, vpad):
        zeros = jnp.zeros((ATT_BLOCK, 256), BF16)
        kpad[pl.ds(0, ATT_BLOCK), :] = zeros
        vpad[pl.ds(0, ATT_BLOCK), :] = zeros
        kpad[pl.ds(ATT_BLOCK, S), :] = k_ref[...]
        vpad[pl.ds(ATT_BLOCK, S), :] = v_ref[...]
        valid0, col = _swa_masks()
        lo_q = lax.broadcasted_iota(jnp.int32, (1, 128), 1) < HEAD_DIM

        def blk(n, carry):
            r0 = pl.multiple_of(n * ATT_BLOCK, ATT_BLOCK)
            rows = pl.ds(r0, ATT_BLOCK)
            valid = valid0 & ((n > 0) | (col >= ATT_BLOCK))
            for g in range(N_Q_HEADS // GROUP):
                lanes = pl.ds(128 * g, 128)
                kg = kpad[pl.ds(r0, 2 * ATT_BLOCK), lanes]
                vg = vpad[pl.ds(r0, 2 * ATT_BLOCK), lanes]
                qm = _stack_group(q_ref[rows, pl.ds(256 * g, 256)].astype(BF16), lo_q)
                s = _dot_nt(qm, kg)
                ps = []
                for i in range(GROUP):
                    h = GROUP * g + i
                    p, _ = _swa_probs(s[ATT_BLOCK * i:ATT_BLOCK * (i + 1)], bias_ref[h], sink_ref[h], valid)
                    ps.append(p.astype(BF16))
                o = _dot(jnp.concatenate(ps, axis=0), vg)
                o_ref[rows, pl.ds(256 * g, 256)] = _unstack_group(o, lo_q).astype(o_ref.dtype)
            return carry

        lax.fori_loop(0, nb, blk, 0)

    return pl.pallas_call(
        body, name="swa_fwd", grid=(B,),
        in_specs=[pl.BlockSpec((S, 512), lambda b: (b, 0)),
                  pl.BlockSpec((S, 256), lambda b: (b, 0)),
                  pl.BlockSpec((S, 256), lambda b: (b, 0)),
                  pl.BlockSpec((N_Q_HEADS, ATT_BLOCK, 2 * ATT_BLOCK), lambda b: (0, 0, 0)),
                  pl.BlockSpec(memory_space=pltpu.SMEM)],
        out_specs=pl.BlockSpec((S, 512), lambda b: (b, 0)),
        out_shape=jax.ShapeDtypeStruct((T, 512), BF16),
        scratch_shapes=[pltpu.VMEM((S + ATT_BLOCK, 256), BF16), pltpu.VMEM((S + ATT_BLOCK, 256), BF16)],
        compiler_params=_params("arbitrary"))(proj, kk2, vv2, bias, sinks)


def _swa_bwd(proj, kk2, vv2, bias, sinks, datt, B, S):
    T = B * S
    nb = S // ATT_BLOCK

    def body(q_ref, k_ref, v_ref, bias_ref, sink_ref, do_ref, dq_ref, dk_ref, dv_ref, dbias_ref, dsink_ref,
             kpad, vpad, dkpad, dvpad):
        b = pl.program_id(0)

        @pl.when(b == 0)
        def _():
            dbias_ref[...] = jnp.zeros_like(dbias_ref)
            dsink_ref[...] = jnp.zeros_like(dsink_ref)

        zeros = jnp.zeros((ATT_BLOCK, 256), BF16)
        kpad[pl.ds(0, ATT_BLOCK), :] = zeros
        vpad[pl.ds(0, ATT_BLOCK), :] = zeros
        kpad[pl.ds(ATT_BLOCK, S), :] = k_ref[...]
        vpad[pl.ds(ATT_BLOCK, S), :] = v_ref[...]
        dkpad[...] = jnp.zeros_like(dkpad)
        dvpad[...] = jnp.zeros_like(dvpad)
        valid0, col = _swa_masks()
        lo_q = lax.broadcasted_iota(jnp.int32, (1, 128), 1) < HEAD_DIM
        scale = HEAD_DIM ** -0.5

        def blk(n, carry):
            r0 = pl.multiple_of(n * ATT_BLOCK, ATT_BLOCK)
            rows = pl.ds(r0, ATT_BLOCK)
            band = pl.ds(r0, 2 * ATT_BLOCK)
            valid = valid0 & ((n > 0) | (col >= ATT_BLOCK))
            for g in range(N_Q_HEADS // GROUP):
                lanes = pl.ds(128 * g, 128)
                kg = kpad[band, lanes]
                vg = vpad[band, lanes]
                qm = _stack_group(q_ref[rows, pl.ds(256 * g, 256)].astype(BF16), lo_q)
                dom = _stack_group(do_ref[rows, pl.ds(256 * g, 256)], lo_q)
                s = _dot_nt(qm, kg)
                dp = _dot_nt(dom, vg)
                pst, dst = [], []
                for i in range(GROUP):
                    h = GROUP * g + i
                    sl = slice(ATT_BLOCK * i, ATT_BLOCK * (i + 1))
                    p, ps = _swa_probs(s[sl], bias_ref[h], sink_ref[h], valid)
                    delta = jnp.sum(p * dp[sl], axis=-1, keepdims=True)
                    ds = p * (dp[sl] - delta)
                    dbias_ref[h] += ds
                    dsink_ref[pl.ds(h, 1), :] += -jnp.sum(jnp.broadcast_to(ps * delta, (ATT_BLOCK, 128)),
                                                          axis=0, keepdims=True)
                    pst.append(p.astype(BF16))
                    dst.append((ds * scale).astype(BF16))
                pst, dst = jnp.concatenate(pst, axis=0), jnp.concatenate(dst, axis=0)
                dq_ref[rows, pl.ds(256 * g, 256)] = _unstack_group(_dot(dst, kg), lo_q).astype(dq_ref.dtype)
                dkpad[band, lanes] += _dot_tn(dst, qm)
                dvpad[band, lanes] += _dot_tn(pst, dom)
            return carry

        lax.fori_loop(0, nb, blk, 0)
        lo_out = lax.broadcasted_iota(jnp.int32, (1, 128), 1) < HEAD_DIM

        def fold(pad_ref):
            halves = []
            for g in range(N_Q_HEADS // GROUP):
                t = pad_ref[pl.ds(ATT_BLOCK, S), pl.ds(128 * g, 128)]
                halves.append(t + pltpu.roll(t, HEAD_DIM, 1))
            return jnp.where(lo_out, halves[0], halves[1])

        dk_ref[...] = fold(dkpad).astype(dk_ref.dtype)
        dv_ref[...] = fold(dvpad).astype(dv_ref.dtype)

    return pl.pallas_call(
        body, name="swa_bwd", grid=(B,),
        in_specs=[pl.BlockSpec((S, 512), lambda b: (b, 0)),
                  pl.BlockSpec((S, 256), lambda b: (b, 0)),
                  pl.BlockSpec((S, 256), lambda b: (b, 0)),
                  pl.BlockSpec((N_Q_HEADS, ATT_BLOCK, 2 * ATT_BLOCK), lambda b: (0, 0, 0)),
                  pl.BlockSpec(memory_space=pltpu.SMEM),
                  pl.BlockSpec((S, 512), lambda b: (b, 0))],
        out_specs=[pl.BlockSpec((S, 512), lambda b: (b, 0)),
                   pl.BlockSpec((S, 128), lambda b: (b, 0)),
                   pl.BlockSpec((S, 128), lambda b: (b, 0)),
                   pl.BlockSpec((N_Q_HEADS, ATT_BLOCK, 2 * ATT_BLOCK), lambda b: (0, 0, 0)),
                   pl.BlockSpec((N_Q_HEADS, 128), lambda b: (0, 0))],
        out_shape=[jax.ShapeDtypeStruct((T, 512), BF16),
                   jax.ShapeDtypeStruct((T, 128), BF16),
                   jax.ShapeDtypeStruct((T, 128), BF16),
                   jax.ShapeDtypeStruct((N_Q_HEADS, ATT_BLOCK, 2 * ATT_BLOCK), F32),
                   jax.ShapeDtypeStruct((N_Q_HEADS, 128), F32)],
        scratch_shapes=[pltpu.VMEM((S + ATT_BLOCK, 256), BF16), pltpu.VMEM((S + ATT_BLOCK, 256), BF16),
                        pltpu.VMEM((S + ATT_BLOCK, 256), F32), pltpu.VMEM((S + ATT_BLOCK, 256), F32)],
        compiler_params=_params("arbitrary"))(proj, kk2, vv2, bias, sinks, datt)


def _hgrn_gates(z, lb):
    sg = _sigmoid(z)
    f = lb + (1.0 - lb) * sg
    return sg, f, jnp.log(f), 1.0 - f


def _hgrn_consts():
    r = lax.broadcasted_iota(jnp.int32, (CHUNK, CHUNK), 0)
    c = lax.broadcasted_iota(jnp.int32, (CHUNK, CHUNK), 1)
    tril = (r >= c).astype(BF16)
    triu = (r <= c).astype(BF16)
    causal = r >= c
    below = (r // SUB) > (c // SUB)
    inside = ((r // SUB) == (c // SUB)) & causal
    return tril, triu, causal, below, inside, c


def _block_rows(ref, lanes, s):
    rows = []
    for i in range(N_SUB):
        if SUB * i + s < 0:
            rows.append(jnp.zeros((SUB, REC_DIM), F32))
        else:
            rows.append(jnp.broadcast_to(ref[pl.ds(SUB * i + s, 1), lanes], (SUB, REC_DIM)))
    return jnp.concatenate(rows, axis=0)


def _hgrn_offdiag(q, k, bcum, b_ref, lanes):
    eq = jnp.exp(jnp.minimum(bcum - _block_rows(b_ref, lanes, -1), 0.0))
    qe = q * eq
    zero = jnp.zeros((SUB, REC_DIM), F32)
    q_rows, k_cols, eks = [jnp.zeros((SUB, (N_SUB - 1) * REC_DIM), F32)], [], []
    for i in range(1, N_SUB):
        q_rows.append(jnp.concatenate([zero] * (i - 1) + [qe[SUB * i:SUB * (i + 1), :]] + [zero] * (N_SUB - 1 - i),
                                      axis=1))
        p = b_ref[pl.ds(SUB * i - 1, 1), lanes]
        pad = jnp.zeros((CHUNK - SUB * i, REC_DIM), F32)
        ek = jnp.concatenate([jnp.exp(p - b_ref[pl.ds(0, SUB * i), lanes]), pad], axis=0)
        k_cols.append(k * ek)
        eks.append(ek)
    return jnp.concatenate(q_rows, axis=0), jnp.concatenate(k_cols, axis=1), eq, eks


def _hgrn_fwd(proj, lb_param, B, S):
    T = B * S
    nc = S // CHUNK

    def body(q_ref, z_ref, v_ref, lb_ref, o_ref, st_ref, k_s, b_s):
        tril, _, _, below, inside, col = _hgrn_consts()
        col_s = col & (SUB - 1)

        def chunk(ci, hts):
            r0 = pl.multiple_of(ci * CHUNK, CHUNK)
            lb = _sigmoid(lb_ref[0:1, :] - lb_ref[1:2, :])
            _, _, g_all, k_all = _hgrn_gates(z_ref[pl.ds(r0, CHUNK), :], lb)
            b_all = _sel_left(tril, g_all)
            k_s[...] = k_all
            b_s[...] = b_all
            new = []
            for e, ht in enumerate(hts):
                lanes = pl.ds(REC_DIM * e, REC_DIM)
                cols = slice(REC_DIM * e, REC_DIM * (e + 1))
                q = q_ref[pl.ds(r0, CHUNK), lanes]
                v = v_ref[pl.ds(r0, CHUNK), lanes]
                k, bcum = k_all[:, cols], b_all[:, cols]
                st_ref[e * nc + ci] = ht
                qst, kst, _, _ = _hgrn_offdiag(q, k, bcum, b_s, lanes)
                d = jnp.zeros((CHUNK, CHUNK), F32)
                for s in range(SUB):
                    w = jnp.exp(jnp.minimum(bcum - _block_rows(b_s, lanes, s), 0.0))
                    colv = jnp.sum(q * _block_rows(k_s, lanes, s) * w, axis=-1, keepdims=True)
                    d = jnp.where(col_s == s, colv, d)
                a = jnp.where(below, _dot_nt(qst.astype(BF16), kst.astype(BF16)), 0.0) + jnp.where(inside, d, 0.0)
                vb = v.astype(BF16)
                qb = (q * jnp.exp(bcum)).astype(BF16)
                o_ref[pl.ds(r0, CHUNK), lanes] = _dot(a.astype(BF16), vb) + _dot_nt(qb, ht.astype(BF16))
                b_last = b_s[pl.ds(CHUNK - 1, 1), lanes]
                kb = (k * jnp.exp(b_last - bcum)).astype(BF16)
                new.append(ht * jnp.exp(b_last) + _dot_tn(vb, kb))
            return tuple(new)

        lax.fori_loop(0, nc, chunk, tuple(jnp.zeros((REC_DIM, REC_DIM), F32) for _ in range(HGRN_PAIR)))

    hp, wd = REC_HEADS // HGRN_PAIR, HGRN_PAIR * REC_DIM
    cq, cf, ci_ = (c * REC_DIM // wd for c in (COL_RQ, COL_RF, COL_RI))
    return pl.pallas_call(
        body, name="hgrn_fwd", grid=(B, hp),
        in_specs=[pl.BlockSpec((S, wd), lambda b, h: (b, cq + h)),
                  pl.BlockSpec((S, wd), lambda b, h: (b, cf + h)),
                  pl.BlockSpec((S, wd), lambda b, h: (b, ci_ + h)),
                  pl.BlockSpec((2, wd), lambda b, h: (0, h))],
        out_specs=[pl.BlockSpec((S, wd), lambda b, h: (b, h)),
                   pl.BlockSpec((HGRN_PAIR * nc, REC_DIM, REC_DIM), lambda b, h: (b * hp + h, 0, 0))],
        out_shape=[jax.ShapeDtypeStruct((T, 512), F32),
                   jax.ShapeDtypeStruct((B * REC_HEADS * nc, REC_DIM, REC_DIM), F32)],
        scratch_shapes=[pltpu.VMEM((CHUNK, wd), F32), pltpu.VMEM((CHUNK, wd), F32)],
        compiler_params=_params("arbitrary", "arbitrary"))(proj, proj, proj, lb_param)


def _hgrn_bwd(proj, lb_param, states, do, B, S, comm=None):
    T = B * S
    nc = S // CHUNK

    c_arrays, c_in_specs, c_out_shapes, c_sems = _comm_parts(comm)
    nci, nco = len(c_arrays), len(c_out_shapes)

    def body(*refs):
        q_ref, z_ref, v_ref, lb_ref, st_ref, do_ref = refs[:6]
        dq_ref, dz_ref, dv_ref, dlb_ref = refs[6 + nci:10 + nci]
        k_s, b_s, pc_hi, pc_lo = refs[10 + nci + nco:14 + nci + nco]
        comm_first, comm_last = _comm_run(comm, (B, REC_HEADS // HGRN_PAIR), refs, 6, 4)
        comm_first()
        tril, triu, causal, below, inside, col = _hgrn_consts()
        col_s = col & (SUB - 1)
        last_row = lax.broadcasted_iota(jnp.int32, (CHUNK, 1), 0) == CHUNK - 1
        rc = lax.broadcasted_iota(jnp.int32, (CHUNK, SUB * REC_DIM), 0)
        lc = lax.broadcasted_iota(jnp.int32, (CHUNK, SUB * REC_DIM), 1)
        spread = ((rc & (SUB - 1)) == (lc // REC_DIM)).astype(BF16)
        rr = lax.broadcasted_iota(jnp.int32, (CHUNK, SUB * CHUNK), 0)
        cc = lax.broadcasted_iota(jnp.int32, (CHUNK, SUB * CHUNK), 1)
        gather = (((rr // SUB) == ((cc & (CHUNK - 1)) // SUB)) & ((rr & (SUB - 1)) == (cc // CHUNK))).astype(BF16)

        heads = range(HGRN_PAIR)
        cols = [slice(REC_DIM * e, REC_DIM * (e + 1)) for e in heads]
        lanes = [pl.ds(REC_DIM * e, REC_DIM) for e in heads]
        lane_cat = lambda vals: jnp.concatenate(vals, axis=1)
        row_cat = lambda vals: jnp.concatenate(vals, axis=0)

        def chunk(it, carry):
            dhts, dlb = carry
            ci = nc - 1 - it
            r0 = pl.multiple_of(ci * CHUNK, CHUNK)
            rows = pl.ds(r0, CHUNK)
            lb = _sigmoid(lb_ref[0:1, :] - lb_ref[1:2, :])
            sg, f, g_all, k_all = _hgrn_gates(z_ref[rows, :], lb)
            b_all = _sel_left(tril, g_all)
            k_s[...] = k_all
            b_s[...] = b_all
            q_all = q_ref[rows, :]
            das, hd = [], []
            for e in heads:
                vb, dob = v_ref[rows, lanes[e]].astype(BF16), do_ref[rows, lanes[e]].astype(BF16)
                da = jnp.where(causal, _dot_nt(dob, vb), 0.0)
                das.append(jnp.where(inside, da, 0.0))
                hd.append((vb, dob, da))
            da_hi, da_lo = _split2(row_cat(das))
            da_in = _dot(da_hi, spread) + _dot(da_lo, spread)
            ds, dqs = [], []
            for e in heads:
                q, bcum = q_all[:, cols[e]], b_all[:, cols[e]]
                d = jnp.zeros((CHUNK, CHUNK), F32)
                dq = jnp.zeros((CHUNK, REC_DIM), F32)
                for s in range(SUB):
                    w = jnp.exp(jnp.minimum(bcum - _block_rows(b_s, lanes[e], s), 0.0))
                    ks = _block_rows(k_s, lanes[e], s)
                    qw = q * w
                    d = jnp.where(col_s == s, jnp.sum(qw * ks, axis=-1, keepdims=True), d)
                    da_s = da_in[CHUNK * e:CHUNK * (e + 1), REC_DIM * s:REC_DIM * (s + 1)]
                    dq = dq + da_s * ks * w
                    hi, lo = _split2(da_s * qw)
                    pc_hi[pl.ds(CHUNK * s, CHUNK), lanes[e]] = hi
                    pc_lo[pl.ds(CHUNK * s, CHUNK), lanes[e]] = lo
                ds.append(d)
                dqs.append(dq)
            dk_in = _dot(gather, pc_hi[...]) + _dot(gather, pc_lo[...])
            dq_out, dk_out, dv_out, db_out, new_dhts = [], [], [], [], []
            for e in heads:
                q, k, bcum = q_all[:, cols[e]], k_all[:, cols[e]], b_all[:, cols[e]]
                vb, dob, da = hd[e]
                dht, ht = dhts[e], st_ref[e * nc + ci]
                qst, kst, eq, eks = _hgrn_offdiag(q, k, bcum, b_s, lanes[e])
                qst_b, kst_b = qst.astype(BF16), kst.astype(BF16)
                a = jnp.where(below, _dot_nt(qst_b, kst_b), 0.0) + jnp.where(inside, ds[e], 0.0)
                da_off = jnp.where(below, da, 0.0).astype(BF16)
                dqst = _dot(da_off, kst_b)
                dkst = _dot_tn(da_off, qst_b)
                dk = dk_in[:, cols[e]]
                dq_rows = [jnp.zeros((SUB, REC_DIM), F32)]
                for i in range(1, N_SUB):
                    dq_rows.append(dqst[SUB * i:SUB * (i + 1), REC_DIM * (i - 1):REC_DIM * i])
                    dk = dk + dkst[:, REC_DIM * (i - 1):REC_DIM * i] * eks[i - 1]
                dq = dqs[e] + row_cat(dq_rows) * eq
                eb = jnp.exp(bcum)
                b_last = b_s[pl.ds(CHUNK - 1, 1), lanes[e]]
                el = jnp.exp(b_last)
                ekb = jnp.exp(b_last - bcum)
                qb = (q * eb).astype(BF16)
                kb = k * ekb
                dhb = dht.astype(BF16)
                dv_out.append(_dot_tn(a.astype(BF16), dob) + _dot_nt(kb.astype(BF16), dhb))
                dqb = _dot(dob, ht.astype(BF16))
                dkb = _dot(vb, dhb)
                new_dhts.append(dht * el + _dot_tn(dob, qb))
                dq = dq + eb * dqb
                dk = dk + ekb * dkb
                edge = jnp.sum(kb * dkb, axis=0, keepdims=True) + el * jnp.sum(ht * dht, axis=0, keepdims=True)
                db_out.append(q * dq - k * dk + jnp.where(last_row, edge, 0.0))
                dq_out.append(dq)
                dk_out.append(dk)
            dk_all = lane_cat(dk_out)
            db_hi, db_lo = _split2(lane_cat(db_out))
            dg = _dot(triu, db_hi) + _dot(triu, db_lo)
            df = dg / f - dk_all
            dz_ref[rows, :] = (df * (1.0 - lb) * sg * (1.0 - sg)).astype(dz_ref.dtype)
            dq_ref[rows, :] = lane_cat(dq_out).astype(dq_ref.dtype)
            dv_ref[rows, :] = lane_cat(dv_out).astype(dv_ref.dtype)
            return tuple(new_dhts), dlb + jnp.sum(df * (1.0 - sg), axis=0, keepdims=True)

        zero = (tuple(jnp.zeros((REC_DIM, REC_DIM), F32) for _ in heads), jnp.zeros((1, HGRN_PAIR * REC_DIM), F32))
        _, dlb = lax.fori_loop(0, nc, chunk, zero)
        lb = _sigmoid(lb_ref[0:1, :] - lb_ref[1:2, :])
        dlb_ref[...] = jnp.broadcast_to(dlb * lb * (1.0 - lb), (8, HGRN_PAIR * REC_DIM))
        comm_last()

    hp, wd = REC_HEADS // HGRN_PAIR, HGRN_PAIR * REC_DIM
    cq, cf, ci_ = (c * REC_DIM // wd for c in (COL_RQ, COL_RF, COL_RI))
    return pl.pallas_call(
        body, name="hgrn_bwd", grid=(B, hp),
        in_specs=[pl.BlockSpec((S, wd), lambda b, h: (b, cq + h)),
                  pl.BlockSpec((S, wd), lambda b, h: (b, cf + h)),
                  pl.BlockSpec((S, wd), lambda b, h: (b, ci_ + h)),
                  pl.BlockSpec((2, wd), lambda b, h: (0, h)),
                  pl.BlockSpec((HGRN_PAIR * nc, REC_DIM, REC_DIM), lambda b, h: (b * hp + h, 0, 0)),
                  pl.BlockSpec((S, wd), lambda b, h: (b, h))] + c_in_specs,
        out_specs=[pl.BlockSpec((S, wd), lambda b, h: (b, h))] * 3
        + [pl.BlockSpec((8, wd), lambda b, h: (b, h))] + [ANY] * nco,
        out_shape=[jax.ShapeDtypeStruct((T, 512), BF16)] * 3 + [jax.ShapeDtypeStruct((B * 8, 512), F32)]
        + c_out_shapes,
        scratch_shapes=[pltpu.VMEM((CHUNK, wd), F32)] * 2 + [pltpu.VMEM((SUB * CHUNK, wd), BF16)] * 2 + c_sems,
        compiler_params=_params("arbitrary", "arbitrary"))(proj, proj, proj, lb_param, states, do, *c_arrays)


def _rec_gate_fwd(rec, proj, rec_norm):
    T = rec.shape[0]

    def fn(accs, tv, cv):
        return [_rms_hat(tv[0]) * cv[0] * _sigmoid(tv[1])]

    return _tile_call("rec_gate", fn, T, 512, _pick(T, 1024), REC_DIM, tiles=[(rec, 0), (proj, COL_RG)],
                      consts=[rec_norm], outs=[BF16])[0]


def _rec_gate_bwd(dyb, w_rec_proj, rec, proj, rec_norm):
    T = rec.shape[0]

    def fn(accs, tv, cv):
        d, r, rg = accs[0], tv[0], tv[1]
        sg = _sigmoid(rg)
        rn = _rms_hat(r) * cv[0]
        dh, dg = _rms_bwd_vals(d * sg, r, cv[0])
        return [dh, d * rn * sg * (1.0 - sg), dg]

    return _tile_call("rec_gate_bwd", fn, T, 512, _pick(T, 1024), REC_DIM, pairs=[(dyb, 0, w_rec_proj, "nt")],
                      tiles=[(rec, 0), (proj, COL_RG)], consts=[rec_norm], outs=[F32, BF16], parts=1)


def _mix_out_fwd(att, recn, proj, w_att_proj, w_rec_proj, w_out, h1, g_next):
    T = att.shape[0]
    tn = 256

    def merge(accs, tv, cv):
        ya, yb = accs
        return [ya, yb, _sigmoid(tv[0]) * ya + _sigmoid(tv[1]) * yb]

    ya, yb, merged = _tile_call(
        "merge", merge, T, D_MODEL, _pick(T, 1024), tn,
        pairs=[(att, 0, w_att_proj, "nn"), (recn, 0, w_rec_proj, "nn")],
        tiles=[(proj, COL_GA * 128 // tn), (proj, COL_GB * 128 // tn)], outs=[BF16] * 3)

    def res(accs, tv, cv):
        h2 = tv[0] + accs[0]
        return [h2, _rms_hat(h2) * cv[0]]

    h2, n2 = _tile_call("mix_out", res, T, D_MODEL, _pick(T, 512), D_MODEL, pairs=[(merged, 0, w_out, "nn")],
                        tiles=[(h1, 0)], consts=[g_next], outs=[F32, BF16])
    return h2, n2, (ya, yb, merged)


GATHER_FIRST = ("w_ffn1_in",)
GATHER_MIX = ("w_ffn1_out", "w_in", "w_att_proj", "w_rec_proj", "w_out")
GATHER_LAST = ("w_ffn2_in", "w_ffn2_out", "w_ple_gate", "w_ple_proj")
SCATTER_LATE = ("w_ple_gate", "w_ple_proj", "w_ffn2_in", "w_ffn2_out")
SCATTER_MIX = ("w_out", "w_att_proj", "w_rec_proj", "w_in")
SCATTER_LAST = ("w_ffn1_in", "w_ffn1_out")


def _local_step(x, p, tgt, w, mine16, cc, me_chip, B, S):
    T = B * S
    w = dict(w)
    g_ffn1, g_mix, g_ffn2, g_ple = w["norm_ffn1"], w["norm_mix"], w["norm_ffn2"], w["norm_ple"]
    g_fin = w["norm_final"].reshape(1, D_MODEL)
    grads, part, from_chips = {}, {}, {}

    def gather(names):
        return _gather_comm([mine16[n] for n in names])

    def place(names, got):
        for n, g in zip(names, got):
            full = lax.dynamic_update_index_in_dim(g, mine16[n], me_chip, 0)
            w[n] = full if n in ("w_ffn1_in", "w_ffn2_in") else _natural(n, full)

    def scatter(tag, names):
        from_sib = _swap_halves("rs_sibling_" + tag, [grads[n][1] for n in names])
        for n, fs in zip(names, from_sib):
            part[n] = _add_sibling("rs_add_sib_" + n, grads[n][0], fs, cc)
        return _scatter_comm([part[n][1] for n in names])

    def scattered(names, got):
        for n, g in zip(names, got):
            from_chips[n] = g

    place(GATHER_FIRST, _run_comm("gather_first", gather(GATHER_FIRST)))
    def ffn1_out_weight(got):
        place(GATHER_MIX, got)
        return w["w_ffn1_out"]

    h1, u, sv1, got_last = _ffn_fwd("ffn1", x, g_ffn1, w["w_ffn1_in"], None, g_mix, comm_in=gather(GATHER_MIX),
                                    comm_out=gather(GATHER_LAST), w_out_of=ffn1_out_weight)
    place(GATHER_LAST, got_last)

    def ident(accs, tv, cv):
        return [accs[0]]

    proj = _tile_call("in_proj", ident, T, IN_W, _pick(T, 512), IN_W // 2, pairs=[(u, 0, w["w_in"], "nn")],
                      outs=[F32], j_outer=True)[0]
    onehot = jnp.asarray(_t5_onehot())
    bias = _small_mm("t5_bias", w["rel_bias"].T, onehot.astype(BF16), "right")
    bias = bias.reshape(N_Q_HEADS, ATT_BLOCK, 2 * ATT_BLOCK)
    sinks = w["attn_sinks"].reshape(N_Q_HEADS)
    kk2, vv2 = _kv_layouts(proj)
    att = _swa_fwd(proj, kk2, vv2, bias, sinks, B, S)
    rec, states = _hgrn_fwd(proj, w["lb_param"], B, S)
    recn = _rec_gate_fwd(rec, proj, w["rec_norm"])
    h2, n2, (ya, yb, merged) = _mix_out_fwd(att, recn, proj, w["w_att_proj"], w["w_rec_proj"], w["w_out"], h1,
                                            g_ffn2)
    h3, n3, sv2, _ = _ffn_fwd("ffn2", h2, g_ffn2, w["w_ffn2_in"], w["w_ffn2_out"], g_ple, n=n2)

    def ple(accs, tv, cv):
        gate = _sigmoid(accs[0])
        return [gate, accs[1], tv[0] + gate * accs[1]]

    gate_p, pp, h4 = _tile_call(
        "ple", ple, T, D_MODEL, _pick(T, 512), D_MODEL,
        pairs=[(n3, 0, w["w_ple_gate"], "nn"), (p, 0, w["w_ple_proj"], "nn")], tiles=[(h3, 0)],
        outs=[BF16, BF16, F32])

    def head(accs, tv, cv):
        h, t, gt, ppv = tv[0], tv[1], tv[2].astype(F32), tv[3].astype(F32)
        err = _rms_hat(h) * cv[0] - t
        dh, dg = _rms_bwd_vals(err * (1.0 / D_MODEL), h, cv[0])
        return [dh, dh * ppv * gt * (1.0 - gt), dh * gt, _group8(err * err), dg]

    dh4, dzg, dpp, loss_p, dg_fin = _tile_call(
        "loss_head", head, T, D_MODEL, _pick(T, 256), D_MODEL,
        tiles=[(h4, 0), (tgt, 0), (gate_p, 0), (pp, 0)], consts=[g_fin], outs=[F32, BF16, BF16], parts=2)
    grads["norm_final"] = dg_fin

    grads["w_ple_gate"] = _mm_tn_rows("ple_dwg", n3, dzg)
    grads["w_ple_proj"] = _mm_tn_cols("ple_dwp", p, dpp)

    def dnorm(accs, tv, cv):
        dh, dg = _rms_bwd_vals(accs[0], tv[0], cv[0])
        dh = tv[1] + dh
        return [dh, 0.5 * dh, dg]

    dh3, df3, grads["norm_ple"] = _tile_call(
        "ple_dnorm", dnorm, T, D_MODEL, _pick(T, 256), D_MODEL, pairs=[(dzg, 0, w["w_ple_gate"], "nt")],
        tiles=[(h3, 0), (dh4, 0)], consts=[g_ple], outs=[F32, BF16], parts=1)

    dh2, dh2b, grads["norm_ffn2"], grads["w_ffn2_in"], grads["w_ffn2_out"], _, _ = _ffn_bwd(
        "ffn2b", dh3, df3, h2, g_ffn2, w["w_ffn2_in"], w["w_ffn2_out"], sv2)
    scatter_late = scatter("late", SCATTER_LATE)

    grads["w_out"] = _mm_tn_rows("mix_dwout", merged, dh2b)
    tn = 256

    def dmerge(accs, tv, cv):
        dm = accs[0]
        sa, sb = _sigmoid(tv[0]), _sigmoid(tv[1])
        yav, ybv = tv[2].astype(F32), tv[3].astype(F32)
        return [dm * sa, dm * sb, dm * yav * sa * (1.0 - sa), dm * ybv * sb * (1.0 - sb)]

    dya, dyb, dga, dgb = _tile_call(
        "mix_dmerge", dmerge, T, D_MODEL, _pick(T, 1024), tn, pairs=[(dh2b, 0, w["w_out"], "nt")],
        tiles=[(proj, COL_GA * 128 // tn), (proj, COL_GB * 128 // tn), (ya, 0), (yb, 0)], outs=[BF16] * 4)
    grads["w_att_proj"] = _mm_tn_cols("mix_dwatt", att, dya)
    grads["w_rec_proj"] = _mm_tn_cols("mix_dwrec", recn, dyb)

    datt = _tile_call("mix_datt", ident, T, 512, _pick(T, 1024), 512, pairs=[(dya, 0, w["w_att_proj"], "nt")],
                      outs=[BF16])[0]
    drec, drg, grads["rec_norm"] = _rec_gate_bwd(dyb, w["w_rec_proj"], rec, proj, w["rec_norm"])

    drq, drf, dri, dlb, *got = _hgrn_bwd(proj, w["lb_param"], states, drec, B, S, comm=scatter_late)
    scattered(SCATTER_LATE, got)
    grads["lb_param"] = dlb
    daq, dak, dav, dbias, dsink = _swa_bwd(proj, kk2, vv2, bias, sinks, datt, B, S)
    grads["attn_sinks"] = dsink
    grads["rel_bias"] = _small_mm("t5_dbias", dbias.reshape(N_Q_HEADS, -1), onehot.T.astype(BF16), "right")
    dproj = jnp.concatenate([daq, dak, dav, drq, drf, dri, drg, dga, dgb], axis=1)
    tk = _pick(T, 512, 128)
    w_in_shard = IN_W // N_CHIPS
    gw32, gw16 = _mm_tn("mix_dwin", (1, 2, T // tk),
                        (u, (tk, D_MODEL), lambda i, j, k: (k, 0)), (dproj, (tk, IN_W // 2), lambda i, j, k: (k, j)),
                        _grad_pair((D_MODEL, IN_W), (D_MODEL, IN_W // 2), lambda i, j, k: (0, j)))
    to_sh = lambda t: t.reshape(D_MODEL, N_CHIPS, w_in_shard).transpose(1, 0, 2)
    grads["w_in"] = (to_sh(gw32), to_sh(gw16))
    scatter_mix = scatter("mix", SCATTER_MIX)

    def dnorm_mix(accs, tv, cv):
        dh, dg = _rms_bwd_vals(accs[0], tv[0], cv[0])
        dh = tv[1] + dh
        return [dh, 0.5 * dh, dg]

    dh1, df1, grads["norm_mix"] = _tile_call(
        "mix_dnorm", dnorm_mix, T, D_MODEL, _pick(T, 256), D_MODEL, pairs=[(dproj, 0, w["w_in"], "nt")],
        tiles=[(h1, 0), (dh2, 0)], consts=[g_mix], outs=[F32, BF16], parts=1)

    def scatter_last(dw_in, dw_out):
        grads["w_ffn1_in"], grads["w_ffn1_out"] = dw_in, dw_out
        return scatter("last", SCATTER_LAST)

    dx, _, grads["norm_ffn1"], _, _, got, got_last = _ffn_bwd(
        "ffn1b", dh1, df1, x, g_ffn1, w["w_ffn1_in"], w["w_ffn1_out"], sv1, comm=scatter_mix, comm_last=scatter_last)
    scattered(SCATTER_MIX, got)
    scattered(SCATTER_LAST, got_last)
    return loss_p, dx, grads, part, from_chips


def _place():
    x, y, c = lax.axis_index("x"), lax.axis_index("y"), lax.axis_index("c")
    return x, y, c


def _other_chips(x, y):
    return [(1 - x, y, 2 * (1 - x) + y), (x, 1 - y, 2 * x + 1 - y), (1 - x, 1 - y, 2 * (1 - x) + 1 - y)]


def _half_rows(ref_3d, chip, h, rows):
    return ref_3d.at[chip, pl.ds(h * rows, rows), :]


def _run_comm(name, comm):
    nci, nco = len(comm.ins), len(comm.out_shapes)

    def body(*refs):
        cin, cout, send_sems, recv_sems = refs[:nci], refs[nci:nci + nco], refs[-2], refs[-1]
        comm.start(cin, cout, send_sems, recv_sems)
        comm.finish(cin, cout, send_sems, recv_sems)

    return pl.pallas_call(
        body, name=name, in_specs=[ANY] * nci, out_specs=[ANY] * nco, out_shape=list(comm.out_shapes),
        scratch_shapes=[pltpu.SemaphoreType.DMA((comm.n_sems,)), pltpu.SemaphoreType.DMA((comm.n_sems,))],
    )(*comm.ins)


def _gather_comm(ws):
    nw = len(ws)

    def parts(w_refs, out_refs, send_sems, recv_sems):
        x, y, c = _place()
        me = 2 * x + y
        chips = _other_chips(x, y)

        def copy(i, k, chip, h, to, src=None):
            half = ws[i].shape[0] // 2
            dst = _half_rows(out_refs[i], chip, h, half)
            return pltpu.make_async_remote_copy(
                src_ref=dst if src is None else src, dst_ref=dst,
                send_sem=send_sems.at[6 * i + k], recv_sem=recv_sems.at[6 * i + k], device_id=to, device_id_type=MESH)

        def first():
            out = []
            for i in range(nw):
                half = ws[i].shape[0] // 2
                out += [copy(i, j, me, c, (cx, cy, c), src=w_refs[i].at[pl.ds(c * half, half), :])
                        for j, (cx, cy, _) in enumerate(chips)]
            return out

        return copy, first, chips, c, (x, y, 1 - c)

    def start(*refs):
        _, first, _, _, _ = parts(*refs)
        for cp in first():
            cp.start()

    def finish(*refs):
        copy, first, chips, c, sibling = parts(*refs)
        passed = []
        for i in range(nw):
            for j, (cx, cy, ci) in enumerate(chips):
                copy(i, j, ci, c, (cx, cy, c)).wait_recv()
                fw = copy(i, 3 + j, ci, c, sibling)
                fw.start()
                passed.append(fw)
        for i in range(nw):
            for j, (_, _, ci) in enumerate(chips):
                copy(i, 3 + j, ci, 1 - c, sibling).wait_recv()
        for cp in first() + passed:
            cp.wait_send()

    return _Comm(list(ws), [jax.ShapeDtypeStruct((N_CHIPS,) + w.shape, w.dtype) for w in ws], 6 * nw, start, finish)


def _scatter_comm(ps):
    nw = len(ps)

    def copies(p_refs, out_refs, send_sems, recv_sems):
        x, y, c = _place()
        cps = []
        for i in range(nw):
            for j, (cx, cy, ci) in enumerate(_other_chips(x, y)):
                cps.append(pltpu.make_async_remote_copy(
                    src_ref=p_refs[i].at[ci], dst_ref=out_refs[i].at[j], send_sem=send_sems.at[3 * i + j],
                    recv_sem=recv_sems.at[3 * i + j], device_id=(cx, cy, c), device_id_type=MESH))
        return cps

    def start(*refs):
        for cp in copies(*refs):
            cp.start()

    def finish(*refs):
        for cp in copies(*refs):
            cp.wait()

    return _Comm(list(ps), [jax.ShapeDtypeStruct((3,) + p.shape[1:], p.dtype) for p in ps], 3 * nw, start, finish)


def _swap_halves(name, gs):
    nw = len(gs)

    def body(*refs):
        g_refs, out_refs, send_sems, recv_sems = refs[:nw], refs[nw:2 * nw], refs[2 * nw], refs[2 * nw + 1]
        x, y, c = _place()
        cps = []
        for i in range(nw):
            half = gs[i].shape[1] // 2
            cps.append(pltpu.make_async_remote_copy(
                src_ref=g_refs[i].at[:, pl.ds((1 - c) * half, half), :], dst_ref=out_refs[i],
                send_sem=send_sems.at[i], recv_sem=recv_sems.at[i], device_id=(x, y, 1 - c), device_id_type=MESH))
        for cp in cps:
            cp.start()
        for cp in cps:
            cp.wait()

    return pl.pallas_call(
        body, name=name, in_specs=[ANY] * nw, out_specs=[ANY] * nw,
        out_shape=[jax.ShapeDtypeStruct((N_CHIPS, g.shape[1] // 2, g.shape[2]), g.dtype) for g in gs],
        scratch_shapes=[pltpu.SemaphoreType.DMA((nw,)), pltpu.SemaphoreType.DMA((nw,))],
    )(*gs)


def _join_halves(name, ss):
    nw = len(ss)

    def body(*refs):
        s_refs, out_refs, send_sems, recv_sems = refs[:nw], refs[nw:2 * nw], refs[2 * nw], refs[2 * nw + 1]
        x, y, c = _place()
        cps = [pltpu.make_async_remote_copy(
            src_ref=s_refs[i], dst_ref=out_refs[i], send_sem=send_sems.at[i], recv_sem=recv_sems.at[i],
            device_id=(x, y, 1 - c), device_id_type=MESH) for i in range(nw)]
        for cp in cps:
            cp.start()
        for cp in cps:
            cp.wait()

    return pl.pallas_call(
        body, name=name, in_specs=[ANY] * nw, out_specs=[ANY] * nw,
        out_shape=[jax.ShapeDtypeStruct(s.shape, s.dtype) for s in ss],
        scratch_shapes=[pltpu.SemaphoreType.DMA((nw,)), pltpu.SemaphoreType.DMA((nw,))],
    )(*ss)


def _allreduce_small(sp):
    def body(s_ref, out_ref, slots, send_sems, recv_sems):
        x, y, c = _place()
        me = 4 * x + 2 * y + c
        slots[me] = s_ref[...]
        cps = []
        for r in range(1, N_DEV):
            px, py, pc = x ^ (r >> 2), y ^ ((r >> 1) & 1), c ^ (r & 1)
            cps.append(pltpu.make_async_remote_copy(
                src_ref=s_ref, dst_ref=slots.at[me], send_sem=send_sems.at[r - 1], recv_sem=recv_sems.at[r - 1],
                device_id=(px, py, pc), device_id_type=MESH))
        for cp in cps:
            cp.start()
        for r in range(1, N_DEV):
            px, py, pc = x ^ (r >> 2), y ^ ((r >> 1) & 1), c ^ (r & 1)
            pltpu.make_async_remote_copy(
                src_ref=s_ref, dst_ref=slots.at[4 * px + 2 * py + pc], send_sem=send_sems.at[r - 1],
                recv_sem=recv_sems.at[r - 1], device_id=(px, py, pc), device_id_type=MESH).wait_recv()
        for cp in cps:
            cp.wait_send()
        acc = slots[0]
        for d in range(1, N_DEV):
            acc = acc + slots[d]
        out_ref[...] = acc

    return pl.pallas_call(
        body, name="allreduce_small",
        in_specs=[pl.BlockSpec(memory_space=pltpu.VMEM)], out_specs=pl.BlockSpec(memory_space=pltpu.VMEM),
        out_shape=jax.ShapeDtypeStruct(sp.shape, F32),
        scratch_shapes=[pltpu.VMEM((N_DEV,) + sp.shape, F32), pltpu.SemaphoreType.DMA((N_DEV - 1,)),
                        pltpu.SemaphoreType.DMA((N_DEV - 1,))],
    )(sp)


def _scalar(v):
    return jnp.reshape(v, (1,)).astype(jnp.int32)


def _row_tile(h, dtype_mult=16):
    return _pick(h, 256, dtype_mult)


def _add_sibling(name, g32, from_sib, c):
    _, r, n = g32.shape
    h = r // 2
    th = _row_tile(h)
    nt = h // th

    def body(c_ref, g_ref, s_ref, o32_ref, o16_ref):
        s = g_ref[...] + s_ref[...].astype(F32)
        o32_ref[...] = s
        o16_ref[...] = s.astype(BF16)

    blk = (None, th, n)
    return pl.pallas_call(
        body, name=name,
        grid_spec=pltpu.PrefetchScalarGridSpec(
            num_scalar_prefetch=1, grid=(N_CHIPS, nt),
            in_specs=[pl.BlockSpec(blk, lambda k, t, c_ref: (k, c_ref[0] * nt + t, 0)),
                      pl.BlockSpec(blk, lambda k, t, c_ref: (k, t, 0))],
            out_specs=[pl.BlockSpec(blk, lambda k, t, c_ref: (k, t, 0))] * 2),
        out_shape=[jax.ShapeDtypeStruct((N_CHIPS, h, n), F32), jax.ShapeDtypeStruct((N_CHIPS, h, n), BF16)],
        compiler_params=_params("arbitrary", "arbitrary"))(_scalar(c), g32, from_sib)


def _add_chips(name, p32, from_chips, me_chip):
    _, h, n = p32.shape
    th = _row_tile(h)

    def body(m_ref, p_ref, a_ref, b_ref, c_ref, o_ref):
        o_ref[...] = p_ref[...] + a_ref[...].astype(F32) + b_ref[...].astype(F32) + c_ref[...].astype(F32)

    blk = (None, th, n)
    return pl.pallas_call(
        body, name=name,
        grid_spec=pltpu.PrefetchScalarGridSpec(
            num_scalar_prefetch=1, grid=(h // th,),
            in_specs=[pl.BlockSpec(blk, lambda t, m_ref: (m_ref[0], t, 0))]
            + [pl.BlockSpec(blk, lambda t, m_ref, j=j: (j, t, 0)) for j in range(3)],
            out_specs=pl.BlockSpec((th, n), lambda t, m_ref: (t, 0))),
        out_shape=jax.ShapeDtypeStruct((h, n), F32),
        compiler_params=_params("arbitrary"))(_scalar(me_chip), p32, from_chips, from_chips, from_chips)


def _adamw_vals(w, g, m, v):
    m = ADAM_B1 * m + (1.0 - ADAM_B1) * g
    v = ADAM_B2 * v + (1.0 - ADAM_B2) * (g * g)
    m_hat = m / (1.0 - ADAM_B1 ** ADAM_STEP)
    v_hat = v / (1.0 - ADAM_B2 ** ADAM_STEP)
    delta = -ADAM_LR * (m_hat / (jnp.sqrt(v_hat) + ADAM_EPS) + ADAM_WD * w)
    return delta, m, v


def _adamw_halves(name, w, m, v, g_mine, g_sib, c):
    r, n = w.shape
    h = r // 2
    th = _row_tile(h, 8)
    nt = h // th

    def body(c_ref, w_ref, m_ref, v_ref, a_ref, b_ref, g_ref, d_ref, nm_ref, nv_ref):
        mine = (pl.program_id(0) // nt) == c_ref[0]
        g = jnp.where(mine, a_ref[...], b_ref[...])
        d, nm, nv = _adamw_vals(w_ref[...], g, m_ref[...], v_ref[...])
        g_ref[...] = g
        d_ref[...] = d
        nm_ref[...] = nm
        nv_ref[...] = nv

    full = pl.BlockSpec((th, n), lambda t, c_ref: (t, 0))
    part = pl.BlockSpec((th, n), lambda t, c_ref: (t % nt, 0))
    return pl.pallas_call(
        body, name=name,
        grid_spec=pltpu.PrefetchScalarGridSpec(
            num_scalar_prefetch=1, grid=(2 * nt,), in_specs=[full, full, full, part, part], out_specs=[full] * 4),
        out_shape=[jax.ShapeDtypeStruct((r, n), F32)] * 4,
        compiler_params=_params("arbitrary"))(_scalar(c), w, m, v, g_mine, g_sib)


def _adamw(name, w, g, m, v):
    R, W = w.shape

    def fn(accs, tv, cv):
        return list(_adamw_vals(*tv))

    return _tile_call(name, fn, R, W, _pick(R, 256), W, tiles=[(w, 0), (g, 0), (m, 0), (v, 0)], outs=[F32] * 3)


SMALL_LAYOUT = (("rel_bias", 2, 256), ("lb_param", 8, 1024), ("norm_ffn1", 8, 1024), ("norm_mix", 8, 1024),
                ("attn_sinks", 1, 8), ("rec_norm", 1, 128), ("norm_ffn2", 8, 1024), ("norm_ple", 8, 1024),
                ("norm_final", 8, 1024), ("loss", 8, 1024))


def _pack_small(vals):
    rows = []
    for name, nrows, n in SMALL_LAYOUT:
        flat = vals[name].reshape(-1)
        flat = jnp.pad(flat, (0, nrows * 128 - n))
        rows.append(flat.reshape(nrows, 128))
    packed = jnp.concatenate(rows, axis=0)
    return jnp.pad(packed, ((0, SMALL_ROWS - packed.shape[0]), (0, 0)))


def _unpack_small(packed, shapes):
    out, r = {}, 0
    for name, nrows, n in SMALL_LAYOUT:
        out[name] = packed[r:r + nrows].reshape(-1)[:n].reshape(shapes[name])
        r += nrows
    return out


def _natural(name, s):
    if name in COL_SHARDED:
        return s.transpose(1, 0, 2).reshape(s.shape[1], -1)
    return s.reshape(-1, s.shape[2])


def kernel(x, p, rel_bias, lb_param, norm_ffn1, w_ffn1_in, w_ffn1_out, norm_mix, w_in, attn_sinks, rec_norm, w_att_proj, w_rec_proj, w_out, norm_ffn2, w_ffn2_in, w_ffn2_out, norm_ple, w_ple_gate, w_ple_proj, norm_final, loss_target, m_rel_bias, m_lb_param, m_norm_ffn1, m_w_ffn1_in, m_w_ffn1_out, m_norm_mix, m_w_in, m_attn_sinks, m_rec_norm, m_w_att_proj, m_w_rec_proj, m_w_out, m_norm_ffn2, m_w_ffn2_in, m_w_ffn2_out, m_norm_ple, m_w_ple_gate, m_w_ple_proj, m_norm_final, v_rel_bias, v_lb_param, v_norm_ffn1, v_w_ffn1_in, v_w_ffn1_out, v_norm_mix, v_w_in, v_attn_sinks, v_rec_norm, v_w_att_proj, v_w_rec_proj, v_w_out, v_norm_ffn2, v_w_ffn2_in, v_w_ffn2_out, v_norm_ple, v_w_ple_gate, v_w_ple_proj, v_norm_final):
    args = dict(locals())
    wsh = {n: args[n] for n in WEIGHTS}
    B, S = x.shape[0], x.shape[1]
    T = B * S
    cx, cy, cc = _place()
    me_chip = 2 * cx + cy

    mine16 = {n: wsh[n][0].astype(BF16) for n in BIG}
    loss_p, dx, grads, part, from_chips = _local_step(
        x.reshape(T, D_MODEL), p.reshape(T, PLE_DIM), loss_target.reshape(T, D_MODEL),
        {n: wsh[n] for n in SMALL}, mine16, cc, me_chip, B, S)

    s_mine = [_add_chips("rs_add_chips_" + n, part[n][0], from_chips[n], me_chip) for n in BIG]
    s_sib = _join_halves("rs_join", s_mine)

    small_vals = {
        "rel_bias": grads["rel_bias"].T,
        "lb_param": jnp.concatenate([_colsum("dlb_sum", grads["lb_param"]),
                                     -_colsum("dlb_sum2", grads["lb_param"])], axis=0) / 8.0,
        "attn_sinks": grads["attn_sinks"][:, 0],
        "rec_norm": _colsum("drn_sum", grads["rec_norm"]).reshape(REC_HEADS, REC_DIM).sum(axis=0),
        "loss": _colsum("loss_sum", loss_p),
    }
    for n in ("norm_ffn1", "norm_mix", "norm_ffn2", "norm_ple", "norm_final"):
        small_vals[n] = _colsum(n + "_sum", grads[n])
    red = _allreduce_small(_pack_small(small_vals))
    small_shapes = {n: wsh[n].shape for n in SMALL}
    small_shapes["loss"] = (D_MODEL,)
    small = _unpack_small(red, small_shapes)
    loss = 0.5 * jnp.sum(small["loss"]) / D_MODEL

    out_g, out_d, out_m, out_v = {}, {}, {}, {}
    for n, gm, gs in zip(BIG, s_mine, s_sib):
        res = _adamw_halves("adamw_" + n, wsh[n][0], args["m_" + n][0], args["v_" + n][0], gm, gs, cc)
        out_g[n], out_d[n], out_m[n], out_v[n] = (t[None] for t in res)
    sw = _pack_small({**{n: wsh[n] for n in SMALL}, "loss": jnp.zeros((D_MODEL,), F32)})
    sm = _pack_small({**{n: args["m_" + n] for n in SMALL}, "loss": jnp.zeros((D_MODEL,), F32)})
    sv = _pack_small({**{n: args["v_" + n] for n in SMALL}, "loss": jnp.ones((D_MODEL,), F32)})
    sd, snm, snv = _adamw("adamw_small", sw, red, sm, sv)
    ud, um, uv = (_unpack_small(t, small_shapes) for t in (sd, snm, snv))
    for n in SMALL:
        out_g[n], out_d[n], out_m[n], out_v[n] = small[n], ud[n], um[n], uv[n]

    return (loss, dx.reshape(B, S, D_MODEL), *[out_g[n] for n in WEIGHTS], *[out_d[n] for n in WEIGHTS],
            *[out_m[n] for n in WEIGHTS], *[out_v[n] for n in WEIGHTS])
```

```python
import numpy as np
import jax
import jax.numpy as jnp
from jax import lax
from jax.experimental import pallas as pl
from jax.experimental.pallas import tpu as pltpu

F32 = jnp.float32
BF16 = jnp.bfloat16
MESH = pl.DeviceIdType.MESH

D_MODEL = 1024
D_FF = 2816
FF_SHARD = 2 * D_FF // 4
HEAD_DIM = 64
N_Q_HEADS = 8
ATT_BLOCK = 128
N_BUCKETS = 32
MAX_DISTANCE = 128
REC_HEADS = 4
REC_DIM = 128
PLE_DIM = 256
EPS = 1e-6
IN_W = 4864
COL_AQ, COL_AK, COL_AV, COL_RQ, COL_RF, COL_RI, COL_RG, COL_GA, COL_GB = 0, 4, 5, 6, 10, 14, 18, 22, 30

CHUNK = 64
SUB = 8
N_SUB = CHUNK // SUB
HGRN_PAIR = 2

ADAM_LR, ADAM_B1, ADAM_B2, ADAM_EPS, ADAM_WD, ADAM_STEP = 0.001, 0.9, 0.999, 1e-08, 0.01, 10

V7X_VMEM_LIMIT = 56 * 1024 * 1024
N_CHIPS = 4
N_DEV = 8

BIG = ("w_ffn1_in", "w_ffn1_out", "w_in", "w_att_proj", "w_rec_proj", "w_out",
       "w_ffn2_in", "w_ffn2_out", "w_ple_gate", "w_ple_proj")
COL_SHARDED = ("w_ffn1_in", "w_in", "w_att_proj", "w_rec_proj", "w_ffn2_in", "w_ple_proj")
WEIGHTS = ("rel_bias", "lb_param", "norm_ffn1", "w_ffn1_in", "w_ffn1_out", "norm_mix", "w_in", "attn_sinks",
           "rec_norm", "w_att_proj", "w_rec_proj", "w_out", "norm_ffn2", "w_ffn2_in", "w_ffn2_out", "norm_ple",
           "w_ple_gate", "w_ple_proj", "norm_final")
SMALL = tuple(n for n in WEIGHTS if n not in BIG)
SMALL_ROWS = 64


def _params(*sem):
    return pltpu.CompilerParams(dimension_semantics=sem, vmem_limit_bytes=V7X_VMEM_LIMIT)


def _pick(n, cap, mult=8):
    if n <= cap:
        return n
    for t in range(cap - cap % mult, 0, -mult):
        if n % t == 0:
            return t
    raise ValueError((n, cap, mult))


def _dot(a, b):
    return jnp.dot(a, b, preferred_element_type=F32)


def _dot_nt(a, b):
    return lax.dot_general(a, b, (((1,), (1,)), ((), ())), preferred_element_type=F32)


def _dot_tn(a, b):
    return lax.dot_general(a, b, (((0,), (0,)), ((), ())), preferred_element_type=F32)


def _split3(x):
    hi = x.astype(BF16)
    r = x - hi.astype(F32)
    mid = r.astype(BF16)
    lo = (r - mid.astype(F32)).astype(BF16)
    return hi, mid, lo


def _split2(x):
    hi = x.astype(BF16)
    return hi, (x - hi.astype(F32)).astype(BF16)


def _sel_left(sel_bf16, x):
    hi, mid, lo = _split3(x)
    return _dot(sel_bf16, hi) + _dot(sel_bf16, mid) + _dot(sel_bf16, lo)


def _sel_right(x, sel_bf16):
    hi, mid, lo = _split3(x)
    return _dot(hi, sel_bf16) + _dot(mid, sel_bf16) + _dot(lo, sel_bf16)


def _sigmoid(x):
    return 0.5 * jnp.tanh(0.5 * x) + 0.5


def _group8(x):
    r, w = x.shape
    return x.reshape(r // 8, 8, w).sum(axis=0)


class _Comm:
    def __init__(self, ins, out_shapes, n_sems, start, finish):
        self.ins, self.out_shapes, self.n_sems, self.start, self.finish = ins, out_shapes, n_sems, start, finish


ANY = pl.BlockSpec(memory_space=pl.ANY)


def _comm_parts(comm):
    if comm is None:
        return [], [], [], []
    sems = [pltpu.SemaphoreType.DMA((comm.n_sems,)), pltpu.SemaphoreType.DMA((comm.n_sems,))]
    return list(comm.ins), [ANY] * len(comm.ins), list(comm.out_shapes), sems


def _comm_run(comm, grid, refs, n_in, n_out):
    if comm is None:
        return (lambda: None), (lambda: None)
    nci, nco = len(comm.ins), len(comm.out_shapes)
    cin = refs[n_in:n_in + nci]
    cout = refs[n_in + nci + n_out:n_in + nci + n_out + nco]
    send_sems, recv_sems = refs[-2], refs[-1]
    ids = [pl.program_id(d) for d in range(len(grid))]
    is_first = ids[0] == 0
    is_last = ids[0] == grid[0] - 1
    for d in range(1, len(grid)):
        is_first = is_first & (ids[d] == 0)
        is_last = is_last & (ids[d] == grid[d] - 1)

    def first():
        @pl.when(is_first)
        def _():
            comm.start(cin, cout, send_sems, recv_sems)

    def last():
        @pl.when(is_last)
        def _():
            comm.finish(cin, cout, send_sems, recv_sems)

    return first, last


def _call(name, fn, grid, ins, outs, pairs=(), comm=None, j_outer=False):
    in_pair = {i for p in pairs for i in p[:2]}
    n_in, n_out = len(ins), len(outs)
    c_arrays, c_in_specs, c_out_shapes, c_sems = _comm_parts(comm)

    def body(*refs):
        first, last = _comm_run(comm, grid, refs, n_in, n_out)
        first()
        accs = []
        for ia, ib, kind in pairs:
            a, b = refs[ia][...].astype(BF16), refs[ib][...].astype(BF16)
            accs.append(_dot(a, b) if kind == "nn" else _dot_nt(a, b))
        vals = [refs[i][...] for i in range(n_in) if i not in in_pair]
        res = fn(accs, vals)
        out_refs = refs[n_in + len(c_arrays):n_in + len(c_arrays) + n_out]
        assert len(res) == len(out_refs), (name, len(res), len(out_refs))
        for o_ref, val in zip(out_refs, res):
            o_ref[...] = val.astype(o_ref.dtype)
        last()

    if j_outer:
        grid = (grid[1], grid[0])
        swap = lambda im: (lambda j, i: im(i, j))
        ins = [(a, blk, swap(im)) for a, blk, im in ins]
        outs = [(shp, dt, blk, swap(im)) for shp, dt, blk, im in outs]

    return pl.pallas_call(
        body, name=name, grid=grid,
        in_specs=[pl.BlockSpec(blk, im) for _, blk, im in ins] + c_in_specs,
        out_specs=[pl.BlockSpec(blk, im) for _, _, blk, im in outs] + [ANY] * len(c_out_shapes),
        out_shape=[jax.ShapeDtypeStruct(shp, dt) for shp, dt, _, _ in outs] + c_out_shapes,
        scratch_shapes=c_sems,
        compiler_params=_params(*(["arbitrary"] * len(grid))))(*[a for a, _, _ in ins], *c_arrays)


def _tile_call(name, fn, M, N, tm, tn, *, pairs=(), tiles=(), consts=(), outs=(), parts=0, comm=None,
               j_outer=False):
    gi, gj = M // tm, N // tn
    assert gi * tm == M and gj * tn == N, (name, M, N, tm, tn)
    ins, prs = [], []
    for a, a_col, b, kind in pairs:
        K = b.shape[0] if kind == "nn" else b.shape[1]
        ins.append((a, (tm, K), lambda i, j, c=a_col: (i, c)))
        if kind == "nn":
            ins.append((b, (K, tn), lambda i, j: (0, j)))
        else:
            ins.append((b, (tn, K), lambda i, j: (j, 0)))
        prs.append((len(ins) - 2, len(ins) - 1, kind))
    for arr, off in tiles:
        ins.append((arr, (tm, tn), lambda i, j, o=off: (i, j + o)))
    for arr in consts:
        ins.append((arr, arr.shape, lambda i, j: (0, 0)))
    out_l = [((M, N), dt, (tm, tn), lambda i, j: (i, j)) for dt in outs]
    out_l += [((gi * 8, N), F32, (8, tn), lambda i, j: (i, j))] * parts
    nt = len(tiles)

    def wrapped(accs, vals):
        return fn(accs, vals[:nt], vals[nt:])

    return _call(name, wrapped, (gi, gj), ins, out_l, prs, comm=comm, j_outer=j_outer)


def _mm_tn(name, grid, a_in, b_in, outs):
    nk = grid[2]
    tm = [d for d in a_in[1] if d is not None][1]
    tn = [d for d in b_in[1] if d is not None][1]

    def body(a_ref, b_ref, *rest):
        out_refs, acc_ref = rest[:-1], rest[-1]
        k = pl.program_id(2)

        @pl.when(k == 0)
        def _():
            acc_ref[...] = jnp.zeros_like(acc_ref)

        acc_ref[...] += _dot_tn(a_ref[...].astype(BF16), b_ref[...].astype(BF16))

        @pl.when(k == nk - 1)
        def _():
            for o_ref in out_refs:
                o_ref[...] = acc_ref[...].astype(o_ref.dtype)

    return pl.pallas_call(
        body, name=name, grid=grid,
        in_specs=[pl.BlockSpec(a_in[1], a_in[2]), pl.BlockSpec(b_in[1], b_in[2])],
        out_specs=[pl.BlockSpec(blk, im) for _, _, blk, im in outs],
        out_shape=[jax.ShapeDtypeStruct(shp, dt) for shp, dt, _, _ in outs],
        scratch_shapes=[pltpu.VMEM((tm, tn), F32)],
        compiler_params=_params("arbitrary", "arbitrary", "arbitrary"))(a_in[0], b_in[0])


def _grad_pair(shape, block, imap):
    return [(shape, F32, block, imap), (shape, BF16, block, imap)]


def _mm_tn_rows(name, a, b, tk=1024):
    T, a_w = a.shape
    b_w = b.shape[1]
    tm = _pick(a_w, 1408, 128)
    tk = _pick(T, tk, 128)
    g32, g16 = _mm_tn(name, (a_w // tm, 1, T // tk),
                      (a, (tk, tm), lambda i, j, k: (k, i)), (b, (tk, b_w), lambda i, j, k: (k, 0)),
                      _grad_pair((a_w, b_w), (tm, b_w), lambda i, j, k: (i, 0)))
    shp = (N_CHIPS, a_w // N_CHIPS, b_w)
    return g32.reshape(shp), g16.reshape(shp)


def _mm_tn_cols(name, a, b, tk=1024):
    T, a_w = a.shape
    n = b.shape[1] // N_CHIPS
    tk = _pick(T, tk, 128)
    return _mm_tn(name, (1, N_CHIPS, T // tk),
                  (a, (tk, a_w), lambda i, j, k: (k, 0)), (b, (tk, n), lambda i, j, k: (k, j)),
                  _grad_pair((N_CHIPS, a_w, n), (None, a_w, n), lambda i, j, k: (j, 0, 0)))


def _colsum(name, x):
    def body(x_ref, o_ref):
        o_ref[...] = jnp.sum(x_ref[...], axis=0, keepdims=True)
    return pl.pallas_call(body, name=name, out_shape=jax.ShapeDtypeStruct((1, x.shape[1]), F32))(x)


def _rms_hat(h):
    return h * lax.rsqrt(jnp.mean(h * h, axis=-1, keepdims=True) + EPS)


def _rms_bwd_vals(dn, h, g):
    r = lax.rsqrt(jnp.mean(h * h, axis=-1, keepdims=True) + EPS)
    nh = h * r
    gd = dn * g
    dh = r * (gd - nh * jnp.mean(gd * nh, axis=-1, keepdims=True))
    return dh, _group8(dn * nh)


def _rms_fwd(name, h, g, tm=512):
    T = h.shape[0]

    def fn(accs, tv, cv):
        return [_rms_hat(tv[0]) * cv[0]]

    return _tile_call(name, fn, T, D_MODEL, _pick(T, tm), D_MODEL, tiles=[(h, 0)], consts=[g], outs=[BF16])[0]


def _ffn_fwd(tag, h, g, w_in, w_out, g_next, n=None, comm_in=None, comm_out=None, w_out_of=None):
    T = h.shape[0]
    if n is None:
        n = _rms_fwd(tag + "_norm", h, g)
    tm = _pick(T, 512)
    wblk = (None, D_MODEL, FF_SHARD)

    def act(accs, vals):
        gate, up = accs
        return [gate, up, gate * _sigmoid(gate) * up]

    tile = lambda: ((T, D_FF), BF16, (tm, FF_SHARD), lambda i, j: (i, j))
    gate, up, a, *got_in = _call(
        tag + "_in", act, (T // tm, 2),
        [(n, (tm, D_MODEL), lambda i, j: (i, 0)),
         (w_in, wblk, lambda i, j: (j, 0, 0)), (w_in, wblk, lambda i, j: (j + 2, 0, 0))],
        [tile(), tile(), tile()], pairs=[(0, 1, "nn"), (0, 2, "nn")], comm=comm_in, j_outer=True)

    def res(accs, tv, cv):
        h_new = tv[0] + 0.5 * accs[0]
        return [h_new, _rms_hat(h_new) * cv[0]]

    if w_out_of is not None:
        w_out = w_out_of(got_in)
    h_new, n_next, *got_out = _tile_call(
        tag + "_out", res, T, D_MODEL, _pick(T, 512), D_MODEL, pairs=[(a, 0, w_out, "nn")], tiles=[(h, 0)],
        consts=[g_next], outs=[F32, BF16], comm=comm_out)
    return h_new, n_next, (n, gate, up, a), got_out


def _ffn_bwd(tag, dh_out, df, h, g, w_in, w_out, saved, comm=None, comm_last=None):
    T = h.shape[0]
    n, gate, up, a = saved
    tm = _pick(T, 512)

    def dact(accs, vals):
        da = accs[0]
        gt, u = vals[0].astype(F32), vals[1].astype(F32)
        sg = _sigmoid(gt)
        silu = gt * sg
        return [jnp.stack([da * u * (sg + silu * (1.0 - sg)), da * silu])]

    dz, *got = _call(
        tag + "_dact", dact, (T // tm, 2),
        [(df, (tm, D_MODEL), lambda i, j: (i, 0)), (w_out, (FF_SHARD, D_MODEL), lambda i, j: (j, 0)),
         (gate, (tm, FF_SHARD), lambda i, j: (i, j)), (up, (tm, FF_SHARD), lambda i, j: (i, j))],
        [((2, T, D_FF), BF16, (2, tm, FF_SHARD), lambda i, j: (0, i, j))], pairs=[(0, 1, "nt")], comm=comm,
        j_outer=True)
    dw_out = _mm_tn_rows(tag + "_dwout", a, df)
    tk = _pick(T, 1024, 128)
    dw_in = _mm_tn(tag + "_dwin", (1, N_CHIPS, T // tk),
                   (n, (tk, D_MODEL), lambda i, j, k: (k, 0)),
                   (dz, (None, tk, FF_SHARD), lambda i, j, k: (j // 2, k, j % 2)),
                   _grad_pair((N_CHIPS, D_MODEL, FF_SHARD), (None, D_MODEL, FF_SHARD), lambda i, j, k: (j, 0, 0)))

    def dnorm(accs, vals):
        dn = accs[0] + accs[1] + accs[2] + accs[3]
        dh, dg = _rms_bwd_vals(dn, vals[0], vals[2])
        dh = vals[1] + dh
        return [dh, dh, dg]

    tm2 = _pick(T, 256)
    ins = [(dz, (None, tm2, FF_SHARD), lambda i, j, s=s: (s // 2, i, s % 2)) for s in range(N_CHIPS)]
    ins += [(w_in, (None, D_MODEL, FF_SHARD), lambda i, j, s=s: (s, 0, 0)) for s in range(N_CHIPS)]
    ins += [(h, (tm2, D_MODEL), lambda i, j: (i, 0)), (dh_out, (tm2, D_MODEL), lambda i, j: (i, 0)),
            (g, g.shape, lambda i, j: (0, 0))]
    dh, dh16, dg, *got_last = _call(
        tag + "_dnorm", dnorm, (T // tm2, 1), ins,
        [((T, D_MODEL), F32, (tm2, D_MODEL), lambda i, j: (i, 0)),
         ((T, D_MODEL), BF16, (tm2, D_MODEL), lambda i, j: (i, 0)),
         ((T // tm2 * 8, D_MODEL), F32, (8, D_MODEL), lambda i, j: (i, 0))],
        pairs=[(s, N_CHIPS + s, "nt") for s in range(N_CHIPS)],
        comm=None if comm_last is None else comm_last(dw_in, dw_out))
    return dh, dh16, dg, dw_in, dw_out, got, got_last


def _t5_onehot():
    qi = np.arange(ATT_BLOCK)[:, None] + ATT_BLOCK
    kj = np.arange(2 * ATT_BLOCK)[None, :]
    nn = np.maximum(qi - kj, 0)
    max_exact = N_BUCKETS // 2
    large = max_exact + (np.log(np.maximum(nn, 1) / max_exact) / np.log(MAX_DISTANCE / max_exact)
                         * (N_BUCKETS - max_exact)).astype(np.int32)
    large = np.minimum(large, N_BUCKETS - 1)
    bucket = np.where(nn < max_exact, nn, large).astype(np.int32).reshape(-1)
    return (bucket[None, :] == np.arange(N_BUCKETS)[:, None]).astype(np.float32)


def _small_mm(name, a, b, sel):
    def body(a_ref, b_ref, o_ref):
        if sel == "right":
            o_ref[...] = _sel_right(a_ref[...], b_ref[...])
        else:
            o_ref[...] = _sel_left(a_ref[...], b_ref[...])
    return pl.pallas_call(body, name=name, out_shape=jax.ShapeDtypeStruct((a.shape[0], b.shape[1]), F32),
                          compiler_params=pltpu.CompilerParams(vmem_limit_bytes=V7X_VMEM_LIMIT))(a, b)


def _dup_heads(t):
    a, b = t[:, :HEAD_DIM], t[:, HEAD_DIM:]
    return jnp.concatenate([a, a, b, b], axis=1)


def _kv_layouts(proj):
    T = proj.shape[0]

    def fn(accs, tv, cv):
        return [tv[0], tv[1]]

    k, v = _tile_call("kv_cast", fn, T, 128, _pick(T, 1024), 128, tiles=[(proj, COL_AK), (proj, COL_AV)],
                      outs=[BF16, BF16])
    return _dup_heads(k), _dup_heads(v)


def _swa_masks():
    row = lax.broadcasted_iota(jnp.int32, (ATT_BLOCK, 2 * ATT_BLOCK), 0)
    col = lax.broadcasted_iota(jnp.int32, (ATT_BLOCK, 2 * ATT_BLOCK), 1)
    dist = ATT_BLOCK + row - col
    return (dist >= 0) & (dist < ATT_BLOCK), col


GROUP = 4


def _stack_group(blk, lo_q):
    zero = jnp.zeros_like(blk[:, :128])
    rows = []
    for pair in range(GROUP // 2):
        pb = blk[:, 128 * pair:128 * (pair + 1)]
        rows += [jnp.where(lo_q, pb, zero), jnp.where(lo_q, zero, pb)]
    return jnp.concatenate(rows, axis=0)


def _unstack_group(st, lo_q):
    pairs = [jnp.where(lo_q, st[256 * pair:256 * pair + 128], st[256 * pair + 128:256 * (pair + 1)])
             for pair in range(GROUP // 2)]
    return jnp.concatenate(pairs, axis=1)


def _swa_probs(s, bias_h, sink, valid):
    s = jnp.where(valid, s * (HEAD_DIM ** -0.5) + bias_h, -jnp.inf)
    m = jnp.maximum(jnp.max(s, axis=-1, keepdims=True), sink)
    e = jnp.exp(s - m)
    es = jnp.exp(sink - m)
    den = jnp.sum(e, axis=-1, keepdims=True) + es
    return e / den, es / den


def _swa_fwd(proj, kk2, vv2, bias, sinks, B, S):
    T = B * S
    nb = S // ATT_BLOCK

    def body(q_ref, k_ref, v_ref, bias_ref, sink_ref, o_ref, kpad, vpad):
        zeros = jnp.zeros((ATT_BLOCK, 256), BF16)
        kpad[pl.ds(0, ATT_BLOCK), :] = zeros
        vpad[pl.ds(0, ATT_BLOCK), :] = zeros
        kpad[pl.ds(ATT_BLOCK, S), :] = k_ref[...]
        vpad[pl.ds(ATT_BLOCK, S), :] = v_ref[...]
        valid0, col = _swa_masks()
        lo_q = lax.broadcasted_iota(jnp.int32, (1, 128), 1) < HEAD_DIM

        def blk(n, carry):
            r0 = pl.multiple_of(n * ATT_BLOCK, ATT_BLOCK)
            rows = pl.ds(r0, ATT_BLOCK)
            valid = valid0 & ((n > 0) | (col >= ATT_BLOCK))
            for g in range(N_Q_HEADS // GROUP):
                lanes = pl.ds(128 * g, 128)
                kg = kpad[pl.ds(r0, 2 * ATT_BLOCK), lanes]
                vg = vpad[pl.ds(r0, 2 * ATT_BLOCK), lanes]
                qm = _stack_group(q_ref[rows, pl.ds(256 * g, 256)].astype(BF16), lo_q)
                s = _dot_nt(qm, kg)
                ps = []
                for i in range(GROUP):
                    h = GROUP * g + i
                    p, _ = _swa_probs(s[ATT_BLOCK * i:ATT_BLOCK * (i + 1)], bias_ref[h], sink_ref[h], valid)
                    ps.append(p.astype(BF16))
                o = _dot(jnp.concatenate(ps, axis=0), vg)
                o_ref[rows, pl.ds(256 * g, 256)] = _unstack_group(o, lo_q).astype(o_ref.dtype)
            return carry

        lax.fori_loop(0, nb, blk, 0)

    return pl.pallas_call(
        body, name="swa_fwd", grid=(B,),
        in_specs=[pl.BlockSpec((S, 512), lambda b: (b, 0)),
                  pl.BlockSpec((S, 256), lambda b: (b, 0)),
                  pl.BlockSpec((S, 256), lambda b: (b, 0)),
                  pl.BlockSpec((N_Q_HEADS, ATT_BLOCK, 2 * ATT_BLOCK), lambda b: (0, 0, 0)),
                  pl.BlockSpec(memory_space=pltpu.SMEM)],
        out_specs=pl.BlockSpec((S, 512), lambda b: (b, 0)),
        out_shape=jax.ShapeDtypeStruct((T, 512), BF16),
        scratch_shapes=[pltpu.VMEM((S + ATT_BLOCK, 256), BF16), pltpu.VMEM((S + ATT_BLOCK, 256), BF16)],
        compiler_params=_params("arbitrary"))(proj, kk2, vv2, bias, sinks)


def _swa_bwd(proj, kk2, vv2, bias, sinks, datt, B, S):
    T = B * S
    nb = S // ATT_BLOCK

    def body(q_ref, k_ref, v_ref, bias_ref, sink_ref, do_ref, dq_ref, dk_ref, dv_ref, dbias_ref, dsink_ref,
             kpad, vpad, dkpad, dvpad):
        b = pl.program_id(0)

        @pl.when(b == 0)
        def _():
            dbias_ref[...] = jnp.zeros_like(dbias_ref)
            dsink_ref[...] = jnp.zeros_like(dsink_ref)

        zeros = jnp.zeros((ATT_BLOCK, 256), BF16)
        kpad[pl.ds(0, ATT_BLOCK), :] = zeros
        vpad[pl.ds(0, ATT_BLOCK), :] = zeros
        kpad[pl.ds(ATT_BLOCK, S), :] = k_ref[...]
        vpad[pl.ds(ATT_BLOCK, S), :] = v_ref[...]
        dkpad[...] = jnp.zeros_like(dkpad)
        dvpad[...] = jnp.zeros_like(dvpad)
        valid0, col = _swa_masks()
        lo_q = lax.broadcasted_iota(jnp.int32, (1, 128), 1) < HEAD_DIM
        scale = HEAD_DIM ** -0.5

        def blk(n, carry):
            r0 = pl.multiple_of(n * ATT_BLOCK, ATT_BLOCK)
            rows = pl.ds(r0, ATT_BLOCK)
            band = pl.ds(r0, 2 * ATT_BLOCK)
            valid = valid0 & ((n > 0) | (col >= ATT_BLOCK))
            for g in range(N_Q_HEADS // GROUP):
                lanes = pl.ds(128 * g, 128)
                kg = kpad[band, lanes]
                vg = vpad[band, lanes]
                qm = _stack_group(q_ref[rows, pl.ds(256 * g, 256)].astype(BF16), lo_q)
                dom = _stack_group(do_ref[rows, pl.ds(256 * g, 256)], lo_q)
                s = _dot_nt(qm, kg)
                dp = _dot_nt(dom, vg)
                pst, dst = [], []
                for i in range(GROUP):
                    h = GROUP * g + i
                    sl = slice(ATT_BLOCK * i, ATT_BLOCK * (i + 1))
                    p, ps = _swa_probs(s[sl], bias_ref[h], sink_ref[h], valid)
                    delta = jnp.sum(p * dp[sl], axis=-1, keepdims=True)
                    ds = p * (dp[sl] - delta)
                    dbias_ref[h] += ds
                    dsink_ref[pl.ds(h, 1), :] += -jnp.sum(jnp.broadcast_to(ps * delta, (ATT_BLOCK, 128)),
                                                          axis=0, keepdims=True)
                    pst.append(p.astype(BF16))
                    dst.append((ds * scale).astype(BF16))
                pst, dst = jnp.concatenate(pst, axis=0), jnp.concatenate(dst, axis=0)
                dq_ref[rows, pl.ds(256 * g, 256)] = _unstack_group(_dot(dst, kg), lo_q).astype(dq_ref.dtype)
                dkpad[band, lanes] += _dot_tn(dst, qm)
                dvpad[band, lanes] += _dot_tn(pst, dom)
            return carry

        lax.fori_loop(0, nb, blk, 0)
        lo_out = lax.broadcasted_iota(jnp.int32, (1, 128), 1) < HEAD_DIM

        def fold(pad_ref):
            halves = []
            for g in range(N_Q_HEADS // GROUP):
                t = pad_ref[pl.ds(ATT_BLOCK, S), pl.ds(128 * g, 128)]
                halves.append(t + pltpu.roll(t, HEAD_DIM, 1))
            return jnp.where(lo_out, halves[0], halves[1])

        dk_ref[...] = fold(dkpad).astype(dk_ref.dtype)
        dv_ref[...] = fold(dvpad).astype(dv_ref.dtype)

    return pl.pallas_call(
        body, name="swa_bwd", grid=(B,),
        in_specs=[pl.BlockSpec((S, 512), lambda b: (b, 0)),
                  pl.BlockSpec((S, 256), lambda b: (b, 0)),
                  pl.BlockSpec((S, 256), lambda b: (b, 0)),
                  pl.BlockSpec((N_Q_HEADS, ATT_BLOCK, 2 * ATT_BLOCK), lambda b: (0, 0, 0)),
                  pl.BlockSpec(memory_space=pltpu.SMEM),
                  pl.BlockSpec((S, 512), lambda b: (b, 0))],
        out_specs=[pl.BlockSpec((S, 512), lambda b: (b, 0)),
                   pl.BlockSpec((S, 128), lambda b: (b, 0)),
                   pl.BlockSpec((S, 128), lambda b: (b, 0)),
                   pl.BlockSpec((N_Q_HEADS, ATT_BLOCK, 2 * ATT_BLOCK), lambda b: (0, 0, 0)),
                   pl.BlockSpec((N_Q_HEADS, 128), lambda b: (0, 0))],
        out_shape=[jax.ShapeDtypeStruct((T, 512), BF16),
                   jax.ShapeDtypeStruct((T, 128), BF16),
                   jax.ShapeDtypeStruct((T, 128), BF16),
                   jax.ShapeDtypeStruct((N_Q_HEADS, ATT_BLOCK, 2 * ATT_BLOCK), F32),
                   jax.ShapeDtypeStruct((N_Q_HEADS, 128), F32)],
        scratch_shapes=[pltpu.VMEM((S + ATT_BLOCK, 256), BF16), pltpu.VMEM((S + ATT_BLOCK, 256), BF16),
                        pltpu.VMEM((S + ATT_BLOCK, 256), F32), pltpu.VMEM((S + ATT_BLOCK, 256), F32)],
        compiler_params=_params("arbitrary"))(proj, kk2, vv2, bias, sinks, datt)


def _hgrn_gates(z, lb):
    sg = _sigmoid(z)
    f = lb + (1.0 - lb) * sg
    return sg, f, jnp.log(f), 1.0 - f


def _hgrn_consts():
    r = lax.broadcasted_iota(jnp.int32, (CHUNK, CHUNK), 0)
    c = lax.broadcasted_iota(jnp.int32, (CHUNK, CHUNK), 1)
    tril = (r >= c).astype(BF16)
    triu = (r <= c).astype(BF16)
    causal = r >= c
    below = (r // SUB) > (c // SUB)
    inside = ((r // SUB) == (c // SUB)) & causal
    return tril, triu, causal, below, inside, c


def _block_rows(ref, lanes, s):
    rows = []
    for i in range(N_SUB):
        if SUB * i + s < 0:
            rows.append(jnp.zeros((SUB, REC_DIM), F32))
        else:
            rows.append(jnp.broadcast_to(ref[pl.ds(SUB * i + s, 1), lanes], (SUB, REC_DIM)))
    return jnp.concatenate(rows, axis=0)


def _hgrn_offdiag(q, k, bcum, b_ref, lanes):
    eq = jnp.exp(jnp.minimum(bcum - _block_rows(b_ref, lanes, -1), 0.0))
    qe = q * eq
    zero = jnp.zeros((SUB, REC_DIM), F32)
    q_rows, k_cols, eks = [jnp.zeros((SUB, (N_SUB - 1) * REC_DIM), F32)], [], []
    for i in range(1, N_SUB):
        q_rows.append(jnp.concatenate([zero] * (i - 1) + [qe[SUB * i:SUB * (i + 1), :]] + [zero] * (N_SUB - 1 - i),
                                      axis=1))
        p = b_ref[pl.ds(SUB * i - 1, 1), lanes]
        pad = jnp.zeros((CHUNK - SUB * i, REC_DIM), F32)
        ek = jnp.concatenate([jnp.exp(p - b_ref[pl.ds(0, SUB * i), lanes]), pad], axis=0)
        k_cols.append(k * ek)
        eks.append(ek)
    return jnp.concatenate(q_rows, axis=0), jnp.concatenate(k_cols, axis=1), eq, eks


def _hgrn_fwd(proj, lb_param, B, S):
    T = B * S
    nc = S // CHUNK

    def body(q_ref, z_ref, v_ref, lb_ref, o_ref, st_ref, k_slots, b_slots):
        tril, _, _, below, inside, col = _hgrn_consts()
        col_s = col & (SUB - 1)

        def chunk(ci, hts, slot):
            k_s, b_s = k_slots.at[slot], b_slots.at[slot]
            r0 = pl.multiple_of(ci * CHUNK, CHUNK)
            lb = _sigmoid(lb_ref[0:1, :] - lb_ref[1:2, :])
            _, _, g_all, k_all = _hgrn_gates(z_ref[pl.ds(r0, CHUNK), :], lb)
            b_all = _sel_left(tril, g_all)
            k_s[...] = k_all
            b_s[...] = b_all
            new = []
            for e, ht in enumerate(hts):
                lanes = pl.ds(REC_DIM * e, REC_DIM)
                cols = slice(REC_DIM * e, REC_DIM * (e + 1))
                q = q_ref[pl.ds(r0, CHUNK), lanes]
                v = v_ref[pl.ds(r0, CHUNK), lanes]
                k, bcum = k_all[:, cols], b_all[:, cols]
                st_ref[e * nc + ci] = ht
                qst, kst, _, _ = _hgrn_offdiag(q, k, bcum, b_s, lanes)
                d = jnp.zeros((CHUNK, CHUNK), F32)
                for s in range(SUB):
                    w = jnp.exp(jnp.minimum(bcum - _block_rows(b_s, lanes, s), 0.0))
                    colv = jnp.sum(q * _block_rows(k_s, lanes, s) * w, axis=-1, keepdims=True)
                    d = jnp.where(col_s == s, colv, d)
                a = jnp.where(below, _dot_nt(qst.astype(BF16), kst.astype(BF16)), 0.0) + jnp.where(inside, d, 0.0)
                vb = v.astype(BF16)
                qb = (q * jnp.exp(bcum)).astype(BF16)
                o_ref[pl.ds(r0, CHUNK), lanes] = _dot(a.astype(BF16), vb) + _dot_nt(qb, ht.astype(BF16))
                b_last = b_s[pl.ds(CHUNK - 1, 1), lanes]
                kb = (k * jnp.exp(b_last - bcum)).astype(BF16)
                new.append(ht * jnp.exp(b_last) + _dot_tn(vb, kb))
            return tuple(new)

        lax.fori_loop(0, nc // 2, lambda i, hts: chunk(2 * i + 1, chunk(2 * i, hts, 0), 1),
                      tuple(jnp.zeros((REC_DIM, REC_DIM), F32) for _ in range(HGRN_PAIR)))

    hp, wd = REC_HEADS // HGRN_PAIR, HGRN_PAIR * REC_DIM
    cq, cf, ci_ = (c * REC_DIM // wd for c in (COL_RQ, COL_RF, COL_RI))
    return pl.pallas_call(
        body, name="hgrn_fwd", grid=(B, hp),
        in_specs=[pl.BlockSpec((S, wd), lambda b, h: (b, cq + h)),
                  pl.BlockSpec((S, wd), lambda b, h: (b, cf + h)),
                  pl.BlockSpec((S, wd), lambda b, h: (b, ci_ + h)),
                  pl.BlockSpec((2, wd), lambda b, h: (0, h))],
        out_specs=[pl.BlockSpec((S, wd), lambda b, h: (b, h)),
                   pl.BlockSpec((HGRN_PAIR * nc, REC_DIM, REC_DIM), lambda b, h: (b * hp + h, 0, 0))],
        out_shape=[jax.ShapeDtypeStruct((T, 512), F32),
                   jax.ShapeDtypeStruct((B * REC_HEADS * nc, REC_DIM, REC_DIM), F32)],
        scratch_shapes=[pltpu.VMEM((2, CHUNK, wd), F32), pltpu.VMEM((2, CHUNK, wd), F32)],
        compiler_params=_params("arbitrary", "arbitrary"))(proj, proj, proj, lb_param)


def _hgrn_bwd(proj, lb_param, states, do, B, S, comm=None):
    T = B * S
    nc = S // CHUNK

    c_arrays, c_in_specs, c_out_shapes, c_sems = _comm_parts(comm)
    nci, nco = len(c_arrays), len(c_out_shapes)

    def body(*refs):
        q_ref, z_ref, v_ref, lb_ref, st_ref, do_ref = refs[:6]
        dq_ref, dz_ref, dv_ref, dlb_ref = refs[6 + nci:10 + nci]
        slots = refs[10 + nci + nco:14 + nci + nco]
        comm_first, comm_last = _comm_run(comm, (B, REC_HEADS // HGRN_PAIR), refs, 6, 4)
        comm_first()
        tril, triu, causal, below, inside, col = _hgrn_consts()
        col_s = col & (SUB - 1)
        last_row = lax.broadcasted_iota(jnp.int32, (CHUNK, 1), 0) == CHUNK - 1
        rc = lax.broadcasted_iota(jnp.int32, (CHUNK, SUB * REC_DIM), 0)
        lc = lax.broadcasted_iota(jnp.int32, (CHUNK, SUB * REC_DIM), 1)
        spread = ((rc & (SUB - 1)) == (lc // REC_DIM)).astype(BF16)
        rr = lax.broadcasted_iota(jnp.int32, (CHUNK, SUB * CHUNK), 0)
        cc = lax.broadcasted_iota(jnp.int32, (CHUNK, SUB * CHUNK), 1)
        gather = (((rr // SUB) == ((cc & (CHUNK - 1)) // SUB)) & ((rr & (SUB - 1)) == (cc // CHUNK))).astype(BF16)

        heads = range(HGRN_PAIR)
        cols = [slice(REC_DIM * e, REC_DIM * (e + 1)) for e in heads]
        lanes = [pl.ds(REC_DIM * e, REC_DIM) for e in heads]
        lane_cat = lambda vals: jnp.concatenate(vals, axis=1)
        row_cat = lambda vals: jnp.concatenate(vals, axis=0)

        def chunk(it, carry, slot):
            k_s, b_s, pc_hi, pc_lo = (r.at[slot] for r in slots)
            dhts, dlb = carry
            ci = nc - 1 - it
            r0 = pl.multiple_of(ci * CHUNK, CHUNK)
            rows = pl.ds(r0, CHUNK)
            lb = _sigmoid(lb_ref[0:1, :] - lb_ref[1:2, :])
            sg, f, g_all, k_all = _hgrn_gates(z_ref[rows, :], lb)
            b_all = _sel_left(tril, g_all)
            k_s[...] = k_all
            b_s[...] = b_all
            q_all = q_ref[rows, :]
            das, hd = [], []
            for e in heads:
                vb, dob = v_ref[rows, lanes[e]].astype(BF16), do_ref[rows, lanes[e]].astype(BF16)
                da = jnp.where(causal, _dot_nt(dob, vb), 0.0)
                das.append(jnp.where(inside, da, 0.0))
                hd.append((vb, dob, da))
            da_hi, da_lo = _split2(row_cat(das))
            da_in = _dot(da_hi, spread) + _dot(da_lo, spread)
            ds, dqs = [], []
            for e in heads:
                q, bcum = q_all[:, cols[e]], b_all[:, cols[e]]
                d = jnp.zeros((CHUNK, CHUNK), F32)
                dq = jnp.zeros((CHUNK, REC_DIM), F32)
                for s in range(SUB):
                    w = jnp.exp(jnp.minimum(bcum - _block_rows(b_s, lanes[e], s), 0.0))
                    ks = _block_rows(k_s, lanes[e], s)
                    qw = q * w
                    d = jnp.where(col_s == s, jnp.sum(qw * ks, axis=-1, keepdims=True), d)
                    da_s = da_in[CHUNK * e:CHUNK * (e + 1), REC_DIM * s:REC_DIM * (s + 1)]
                    dq = dq + da_s * ks * w
                    hi, lo = _split2(da_s * qw)
                    pc_hi[pl.ds(CHUNK * s, CHUNK), lanes[e]] = hi
                    pc_lo[pl.ds(CHUNK * s, CHUNK), lanes[e]] = lo
                ds.append(d)
                dqs.append(dq)
            dk_in = _dot(gather, pc_hi[...]) + _dot(gather, pc_lo[...])
            dq_out, dk_out, dv_out, db_out, new_dhts = [], [], [], [], []
            for e in heads:
                q, k, bcum = q_all[:, cols[e]], k_all[:, cols[e]], b_all[:, cols[e]]
                vb, dob, da = hd[e]
                dht, ht = dhts[e], st_ref[e * nc + ci]
                qst, kst, eq, eks = _hgrn_offdiag(q, k, bcum, b_s, lanes[e])
                qst_b, kst_b = qst.astype(BF16), kst.astype(BF16)
                a = jnp.where(below, _dot_nt(qst_b, kst_b), 0.0) + jnp.where(inside, ds[e], 0.0)
                da_off = jnp.where(below, da, 0.0).astype(BF16)
                dqst = _dot(da_off, kst_b)
                dkst = _dot_tn(da_off, qst_b)
                dk = dk_in[:, cols[e]]
                dq_rows = [jnp.zeros((SUB, REC_DIM), F32)]
                for i in range(1, N_SUB):
                    dq_rows.append(dqst[SUB * i:SUB * (i + 1), REC_DIM * (i - 1):REC_DIM * i])
                    dk = dk + dkst[:, REC_DIM * (i - 1):REC_DIM * i] * eks[i - 1]
                dq = dqs[e] + row_cat(dq_rows) * eq
                eb = jnp.exp(bcum)
                b_last = b_s[pl.ds(CHUNK - 1, 1), lanes[e]]
                el = jnp.exp(b_last)
                ekb = jnp.exp(b_last - bcum)
                qb = (q * eb).astype(BF16)
                kb = k * ekb
                dhb = dht.astype(BF16)
                dv_out.append(_dot_tn(a.astype(BF16), dob) + _dot_nt(kb.astype(BF16), dhb))
                dqb = _dot(dob, ht.astype(BF16))
                dkb = _dot(vb, dhb)
                new_dhts.append(dht * el + _dot_tn(dob, qb))
                dq = dq + eb * dqb
                dk = dk + ekb * dkb
                edge = jnp.sum(kb * dkb, axis=0, keepdims=True) + el * jnp.sum(ht * dht, axis=0, keepdims=True)
                db_out.append(q * dq - k * dk + jnp.where(last_row, edge, 0.0))
                dq_out.append(dq)
                dk_out.append(dk)
            dk_all = lane_cat(dk_out)
            db_hi, db_lo = _split2(lane_cat(db_out))
            dg = _dot(triu, db_hi) + _dot(triu, db_lo)
            df = dg / f - dk_all
            dz_ref[rows, :] = (df * (1.0 - lb) * sg * (1.0 - sg)).astype(dz_ref.dtype)
            dq_ref[rows, :] = lane_cat(dq_out).astype(dq_ref.dtype)
            dv_ref[rows, :] = lane_cat(dv_out).astype(dv_ref.dtype)
            return tuple(new_dhts), dlb + jnp.sum(df * (1.0 - sg), axis=0, keepdims=True)

        zero = (tuple(jnp.zeros((REC_DIM, REC_DIM), F32) for _ in heads), jnp.zeros((1, HGRN_PAIR * REC_DIM), F32))
        _, dlb = lax.fori_loop(0, nc // 2, lambda i, c: chunk(2 * i + 1, chunk(2 * i, c, 0), 1), zero)
        lb = _sigmoid(lb_ref[0:1, :] - lb_ref[1:2, :])
        dlb_ref[...] = jnp.broadcast_to(dlb * lb * (1.0 - lb), (8, HGRN_PAIR * REC_DIM))
        comm_last()

    hp, wd = REC_HEADS // HGRN_PAIR, HGRN_PAIR * REC_DIM
    cq, cf, ci_ = (c * REC_DIM // wd for c in (COL_RQ, COL_RF, COL_RI))
    return pl.pallas_call(
        body, name="hgrn_bwd", grid=(B, hp),
        in_specs=[pl.BlockSpec((S, wd), lambda b, h: (b, cq + h)),
                  pl.BlockSpec((S, wd), lambda b, h: (b, cf + h)),
                  pl.BlockSpec((S, wd), lambda b, h: (b, ci_ + h)),
                  pl.BlockSpec((2, wd), lambda b, h: (0, h)),
                  pl.BlockSpec((HGRN_PAIR * nc, REC_DIM, REC_DIM), lambda b, h: (b * hp + h, 0, 0)),
                  pl.BlockSpec((S, wd), lambda b, h: (b, h))] + c_in_specs,
        out_specs=[pl.BlockSpec((S, wd), lambda b, h: (b, h))] * 3
        + [pl.BlockSpec((8, wd), lambda b, h: (b, h))] + [ANY] * nco,
        out_shape=[jax.ShapeDtypeStruct((T, 512), BF16)] * 3 + [jax.ShapeDtypeStruct((B * 8, 512), F32)]
        + c_out_shapes,
        scratch_shapes=[pltpu.VMEM((2, CHUNK, wd), F32)] * 2 + [pltpu.VMEM((2, SUB * CHUNK, wd), BF16)] * 2 + c_sems,
        compiler_params=_params("arbitrary", "arbitrary"))(proj, proj, proj, lb_param, states, do, *c_arrays)


def _rec_gate_fwd(rec, proj, rec_norm):
    T = rec.shape[0]

    def fn(accs, tv, cv):
        return [_rms_hat(tv[0]) * cv[0] * _sigmoid(tv[1])]

    return _tile_call("rec_gate", fn, T, 512, _pick(T, 1024), REC_DIM, tiles=[(rec, 0), (proj, COL_RG)],
                      consts=[rec_norm], outs=[BF16])[0]


def _rec_gate_bwd(dyb, w_rec_proj, rec, proj, rec_norm):
    T = rec.shape[0]

    def fn(accs, tv, cv):
        d, r, rg = accs[0], tv[0], tv[1]
        sg = _sigmoid(rg)
        rn = _rms_hat(r) * cv[0]
        dh, dg = _rms_bwd_vals(d * sg, r, cv[0])
        return [dh, d * rn * sg * (1.0 - sg), dg]

    return _tile_call("rec_gate_bwd", fn, T, 512, _pick(T, 1024), REC_DIM, pairs=[(dyb, 0, w_rec_proj, "nt")],
                      tiles=[(rec, 0), (proj, COL_RG)], consts=[rec_norm], outs=[F32, BF16], parts=1)


def _mix_out_fwd(att, recn, proj, w_att_proj, w_rec_proj, w_out, h1, g_next):
    T = att.shape[0]
    tn = 256

    def merge(accs, tv, cv):
        ya, yb = accs
        return [ya, yb, _sigmoid(tv[0]) * ya + _sigmoid(tv[1]) * yb]

    ya, yb, merged = _tile_call(
        "merge", merge, T, D_MODEL, _pick(T, 1024), tn,
        pairs=[(att, 0, w_att_proj, "nn"), (recn, 0, w_rec_proj, "nn")],
        tiles=[(proj, COL_GA * 128 // tn), (proj, COL_GB * 128 // tn)], outs=[BF16] * 3)

    def res(accs, tv, cv):
        h2 = tv[0] + accs[0]
        return [h2, _rms_hat(h2) * cv[0]]

    h2, n2 = _tile_call("mix_out", res, T, D_MODEL, _pick(T, 512), D_MODEL, pairs=[(merged, 0, w_out, "nn")],
                        tiles=[(h1, 0)], consts=[g_next], outs=[F32, BF16])
    return h2, n2, (ya, yb, merged)


GATHER_FIRST = ("w_ffn1_in",)
GATHER_MIX = ("w_ffn1_out", "w_in", "w_att_proj", "w_rec_proj", "w_out")
GATHER_LAST = ("w_ffn2_in", "w_ffn2_out", "w_ple_gate", "w_ple_proj")
SCATTER_LATE = ("w_ple_gate", "w_ple_proj", "w_ffn2_in", "w_ffn2_out")
SCATTER_MIX = ("w_out", "w_att_proj", "w_rec_proj", "w_in")
SCATTER_LAST = ("w_ffn1_in", "w_ffn1_out")


def _local_step(x, p, tgt, w, mine16, cc, me_chip, B, S):
    T = B * S
    w = dict(w)
    g_ffn1, g_mix, g_ffn2, g_ple = w["norm_ffn1"], w["norm_mix"], w["norm_ffn2"], w["norm_ple"]
    g_fin = w["norm_final"].reshape(1, D_MODEL)
    grads, part, from_chips = {}, {}, {}

    def gather(names):
        return _gather_comm([mine16[n] for n in names])

    def place(names, got):
        for n, g in zip(names, got):
            full = lax.dynamic_update_index_in_dim(g, mine16[n], me_chip, 0)
            w[n] = full if n in ("w_ffn1_in", "w_ffn2_in") else _natural(n, full)

    def scatter(tag, names):
        from_sib = _swap_halves("rs_sibling_" + tag, [grads[n][1] for n in names])
        for n, fs in zip(names, from_sib):
            part[n] = _add_sibling("rs_add_sib_" + n, grads[n][0], fs, cc)
        return _scatter_comm([part[n][1] for n in names])

    def scattered(names, got):
        for n, g in zip(names, got):
            from_chips[n] = g

    place(GATHER_FIRST, _run_comm("gather_first", gather(GATHER_FIRST)))
    def ffn1_out_weight(got):
        place(GATHER_MIX, got)
        return w["w_ffn1_out"]

    h1, u, sv1, got_last = _ffn_fwd("ffn1", x, g_ffn1, w["w_ffn1_in"], None, g_mix, comm_in=gather(GATHER_MIX),
                                    comm_out=gather(GATHER_LAST), w_out_of=ffn1_out_weight)
    place(GATHER_LAST, got_last)

    def ident(accs, tv, cv):
        return [accs[0]]

    proj = _tile_call("in_proj", ident, T, IN_W, _pick(T, 512), IN_W // 2, pairs=[(u, 0, w["w_in"], "nn")],
                      outs=[F32], j_outer=True)[0]
    onehot = jnp.asarray(_t5_onehot())
    bias = _small_mm("t5_bias", w["rel_bias"].T, onehot.astype(BF16), "right")
    bias = bias.reshape(N_Q_HEADS, ATT_BLOCK, 2 * ATT_BLOCK)
    sinks = w["attn_sinks"].reshape(N_Q_HEADS)
    kk2, vv2 = _kv_layouts(proj)
    att = _swa_fwd(proj, kk2, vv2, bias, sinks, B, S)
    rec, states = _hgrn_fwd(proj, w["lb_param"], B, S)
    recn = _rec_gate_fwd(rec, proj, w["rec_norm"])
    h2, n2, (ya, yb, merged) = _mix_out_fwd(att, recn, proj, w["w_att_proj"], w["w_rec_proj"], w["w_out"], h1,
                                            g_ffn2)
    h3, n3, sv2, _ = _ffn_fwd("ffn2", h2, g_ffn2, w["w_ffn2_in"], w["w_ffn2_out"], g_ple, n=n2)

    def ple(accs, tv, cv):
        gate = _sigmoid(accs[0])
        return [gate, accs[1], tv[0] + gate * accs[1]]

    gate_p, pp, h4 = _tile_call(
        "ple", ple, T, D_MODEL, _pick(T, 512), D_MODEL,
        pairs=[(n3, 0, w["w_ple_gate"], "nn"), (p, 0, w["w_ple_proj"], "nn")], tiles=[(h3, 0)],
        outs=[BF16, BF16, F32])

    def head(accs, tv, cv):
        h, t, gt, ppv = tv[0], tv[1], tv[2].astype(F32), tv[3].astype(F32)
        err = _rms_hat(h) * cv[0] - t
        dh, dg = _rms_bwd_vals(err * (1.0 / D_MODEL), h, cv[0])
        return [dh, dh * ppv * gt * (1.0 - gt), dh * gt, _group8(err * err), dg]

    dh4, dzg, dpp, loss_p, dg_fin = _tile_call(
        "loss_head", head, T, D_MODEL, _pick(T, 256), D_MODEL,
        tiles=[(h4, 0), (tgt, 0), (gate_p, 0), (pp, 0)], consts=[g_fin], outs=[F32, BF16, BF16], parts=2)
    grads["norm_final"] = dg_fin

    grads["w_ple_gate"] = _mm_tn_rows("ple_dwg", n3, dzg)
    grads["w_ple_proj"] = _mm_tn_cols("ple_dwp", p, dpp)

    def dnorm(accs, tv, cv):
        dh, dg = _rms_bwd_vals(accs[0], tv[0], cv[0])
        dh = tv[1] + dh
        return [dh, 0.5 * dh, dg]

    dh3, df3, grads["norm_ple"] = _tile_call(
        "ple_dnorm", dnorm, T, D_MODEL, _pick(T, 256), D_MODEL, pairs=[(dzg, 0, w["w_ple_gate"], "nt")],
        tiles=[(h3, 0), (dh4, 0)], consts=[g_ple], outs=[F32, BF16], parts=1)

    dh2, dh2b, grads["norm_ffn2"], grads["w_ffn2_in"], grads["w_ffn2_out"], _, _ = _ffn_bwd(
        "ffn2b", dh3, df3, h2, g_ffn2, w["w_ffn2_in"], w["w_ffn2_out"], sv2)
    scatter_late = scatter("late", SCATTER_LATE)

    grads["w_out"] = _mm_tn_rows("mix_dwout", merged, dh2b)
    tn = 256

    def dmerge(accs, tv, cv):
        dm = accs[0]
        sa, sb = _sigmoid(tv[0]), _sigmoid(tv[1])
        yav, ybv = tv[2].astype(F32), tv[3].astype(F32)
        return [dm * sa, dm * sb, dm * yav * sa * (1.0 - sa), dm * ybv * sb * (1.0 - sb)]

    dya, dyb, dga, dgb = _tile_call(
        "mix_dmerge", dmerge, T, D_MODEL, _pick(T, 1024), tn, pairs=[(dh2b, 0, w["w_out"], "nt")],
        tiles=[(proj, COL_GA * 128 // tn), (proj, COL_GB * 128 // tn), (ya, 0), (yb, 0)], outs=[BF16] * 4)
    grads["w_att_proj"] = _mm_tn_cols("mix_dwatt", att, dya)
    grads["w_rec_proj"] = _mm_tn_cols("mix_dwrec", recn, dyb)

    datt = _tile_call("mix_datt", ident, T, 512, _pick(T, 1024), 512, pairs=[(dya, 0, w["w_att_proj"], "nt")],
                      outs=[BF16])[0]
    drec, drg, grads["rec_norm"] = _rec_gate_bwd(dyb, w["w_rec_proj"], rec, proj, w["rec_norm"])

    drq, drf, dri, dlb, *got = _hgrn_bwd(proj, w["lb_param"], states, drec, B, S, comm=scatter_late)
    scattered(SCATTER_LATE, got)
    grads["lb_param"] = dlb
    daq, dak, dav, dbias, dsink = _swa_bwd(proj, kk2, vv2, bias, sinks, datt, B, S)
    grads["attn_sinks"] = dsink
    grads["rel_bias"] = _small_mm("t5_dbias", dbias.reshape(N_Q_HEADS, -1), onehot.T.astype(BF16), "right")
    dproj = jnp.concatenate([daq, dak, dav, drq, drf, dri, drg, dga, dgb], axis=1)
    tk = _pick(T, 512, 128)
    w_in_shard = IN_W // N_CHIPS
    gw32, gw16 = _mm_tn("mix_dwin", (1, 2, T // tk),
                        (u, (tk, D_MODEL), lambda i, j, k: (k, 0)), (dproj, (tk, IN_W // 2), lambda i, j, k: (k, j)),
                        _grad_pair((D_MODEL, IN_W), (D_MODEL, IN_W // 2), lambda i, j, k: (0, j)))
    to_sh = lambda t: t.reshape(D_MODEL, N_CHIPS, w_in_shard).transpose(1, 0, 2)
    grads["w_in"] = (to_sh(gw32), to_sh(gw16))
    scatter_mix = scatter("mix", SCATTER_MIX)

    def dnorm_mix(accs, tv, cv):
        dh, dg = _rms_bwd_vals(accs[0], tv[0], cv[0])
        dh = tv[1] + dh
        return [dh, 0.5 * dh, dg]

    dh1, df1, grads["norm_mix"] = _tile_call(
        "mix_dnorm", dnorm_mix, T, D_MODEL, _pick(T, 256), D_MODEL, pairs=[(dproj, 0, w["w_in"], "nt")],
        tiles=[(h1, 0), (dh2, 0)], consts=[g_mix], outs=[F32, BF16], parts=1)

    def scatter_last(dw_in, dw_out):
        grads["w_ffn1_in"], grads["w_ffn1_out"] = dw_in, dw_out
        return scatter("last", SCATTER_LAST)

    dx, _, grads["norm_ffn1"], _, _, got, got_last = _ffn_bwd(
        "ffn1b", dh1, df1, x, g_ffn1, w["w_ffn1_in"], w["w_ffn1_out"], sv1, comm=scatter_mix, comm_last=scatter_last)
    scattered(SCATTER_MIX, got)
    scattered(SCATTER_LAST, got_last)
    return loss_p, dx, grads, part, from_chips


def _place():
    x, y, c = lax.axis_index("x"), lax.axis_index("y"), lax.axis_index("c")
    return x, y, c


def _other_chips(x, y):
    return [(1 - x, y, 2 * (1 - x) + y), (x, 1 - y, 2 * x + 1 - y), (1 - x, 1 - y, 2 * (1 - x) + 1 - y)]


def _half_rows(ref_3d, chip, h, rows):
    return ref_3d.at[chip, pl.ds(h * rows, rows), :]


def _run_comm(name, comm):
    nci, nco = len(comm.ins), len(comm.out_shapes)

    def body(*refs):
        cin, cout, send_sems, recv_sems = refs[:nci], refs[nci:nci + nco], refs[-2], refs[-1]
        comm.start(cin, cout, send_sems, recv_sems)
        comm.finish(cin, cout, send_sems, recv_sems)

    return pl.pallas_call(
        body, name=name, in_specs=[ANY] * nci, out_specs=[ANY] * nco, out_shape=list(comm.out_shapes),
        scratch_shapes=[pltpu.SemaphoreType.DMA((comm.n_sems,)), pltpu.SemaphoreType.DMA((comm.n_sems,))],
    )(*comm.ins)


def _gather_comm(ws):
    nw = len(ws)

    def parts(w_refs, out_refs, send_sems, recv_sems):
        x, y, c = _place()
        me = 2 * x + y
        chips = _other_chips(x, y)

        def copy(i, k, chip, h, to, src=None):
            half = ws[i].shape[0] // 2
            dst = _half_rows(out_refs[i], chip, h, half)
            return pltpu.make_async_remote_copy(
                src_ref=dst if src is None else src, dst_ref=dst,
                send_sem=send_sems.at[6 * i + k], recv_sem=recv_sems.at[6 * i + k], device_id=to, device_id_type=MESH)

        def first():
            out = []
            for i in range(nw):
                half = ws[i].shape[0] // 2
                out += [copy(i, j, me, c, (cx, cy, c), src=w_refs[i].at[pl.ds(c * half, half), :])
                        for j, (cx, cy, _) in enumerate(chips)]
            return out

        return copy, first, chips, c, (x, y, 1 - c)

    def start(*refs):
        _, first, _, _, _ = parts(*refs)
        for cp in first():
            cp.start()

    def finish(*refs):
        copy, first, chips, c, sibling = parts(*refs)
        passed = []
        for i in range(nw):
            for j, (cx, cy, ci) in enumerate(chips):
                copy(i, j, ci, c, (cx, cy, c)).wait_recv()
                fw = copy(i, 3 + j, ci, c, sibling)
                fw.start()
                passed.append(fw)
        for i in range(nw):
            for j, (_, _, ci) in enumerate(chips):
                copy(i, 3 + j, ci, 1 - c, sibling).wait_recv()
        for cp in first() + passed:
            cp.wait_send()

    return _Comm(list(ws), [jax.ShapeDtypeStruct((N_CHIPS,) + w.shape, w.dtype) for w in ws], 6 * nw, start, finish)


def _scatter_comm(ps):
    nw = len(ps)

    def copies(p_refs, out_refs, send_sems, recv_sems):
        x, y, c = _place()
        cps = []
        for i in range(nw):
            for j, (cx, cy, ci) in enumerate(_other_chips(x, y)):
                cps.append(pltpu.make_async_remote_copy(
                    src_ref=p_refs[i].at[ci], dst_ref=out_refs[i].at[j], send_sem=send_sems.at[3 * i + j],
                    recv_sem=recv_sems.at[3 * i + j], device_id=(cx, cy, c), device_id_type=MESH))
        return cps

    def start(*refs):
        for cp in copies(*refs):
            cp.start()

    def finish(*refs):
        for cp in copies(*refs):
            cp.wait()

    return _Comm(list(ps), [jax.ShapeDtypeStruct((3,) + p.shape[1:], p.dtype) for p in ps], 3 * nw, start, finish)


def _swap_halves(name, gs):
    nw = len(gs)

    def body(*refs):
        g_refs, out_refs, send_sems, recv_sems = refs[:nw], refs[nw:2 * nw], refs[2 * nw], refs[2 * nw + 1]
        x, y, c = _place()
        cps = []
        for i in range(nw):
            half = gs[i].shape[1] // 2
            cps.append(pltpu.make_async_remote_copy(
                src_ref=g_refs[i].at[:, pl.ds((1 - c) * half, half), :], dst_ref=out_refs[i],
                send_sem=send_sems.at[i], recv_sem=recv_sems.at[i], device_id=(x, y, 1 - c), device_id_type=MESH))
        for cp in cps:
            cp.start()
        for cp in cps:
            cp.wait()

    return pl.pallas_call(
        body, name=name, in_specs=[ANY] * nw, out_specs=[ANY] * nw,
        out_shape=[jax.ShapeDtypeStruct((N_CHIPS, g.shape[1] // 2, g.shape[2]), g.dtype) for g in gs],
        scratch_shapes=[pltpu.SemaphoreType.DMA((nw,)), pltpu.SemaphoreType.DMA((nw,))],
    )(*gs)


def _join_halves(name, ss):
    nw = len(ss)

    def body(*refs):
        s_refs, out_refs, send_sems, recv_sems = refs[:nw], refs[nw:2 * nw], refs[2 * nw], refs[2 * nw + 1]
        x, y, c = _place()
        cps = [pltpu.make_async_remote_copy(
            src_ref=s_refs[i], dst_ref=out_refs[i], send_sem=send_sems.at[i], recv_sem=recv_sems.at[i],
            device_id=(x, y, 1 - c), device_id_type=MESH) for i in range(nw)]
        for cp in cps:
            cp.start()
        for cp in cps:
            cp.wait()

    return pl.pallas_call(
        body, name=name, in_specs=[ANY] * nw, out_specs=[ANY] * nw,
        out_shape=[jax.ShapeDtypeStruct(s.shape, s.dtype) for s in ss],
        scratch_shapes=[pltpu.SemaphoreType.DMA((nw,)), pltpu.SemaphoreType.DMA((nw,))],
    )(*ss)


def _allreduce_small(sp):
    def body(s_ref, out_ref, slots, send_sems, recv_sems):
        x, y, c = _place()
        me = 4 * x + 2 * y + c
        slots[me] = s_ref[...]
        cps = []
        for r in range(1, N_DEV):
            px, py, pc = x ^ (r >> 2), y ^ ((r >> 1) & 1), c ^ (r & 1)
            cps.append(pltpu.make_async_remote_copy(
                src_ref=s_ref, dst_ref=slots.at[me], send_sem=send_sems.at[r - 1], recv_sem=recv_sems.at[r - 1],
                device_id=(px, py, pc), device_id_type=MESH))
        for cp in cps:
            cp.start()
        for r in range(1, N_DEV):
            px, py, pc = x ^ (r >> 2), y ^ ((r >> 1) & 1), c ^ (r & 1)
            pltpu.make_async_remote_copy(
                src_ref=s_ref, dst_ref=slots.at[4 * px + 2 * py + pc], send_sem=send_sems.at[r - 1],
                recv_sem=recv_sems.at[r - 1], device_id=(px, py, pc), device_id_type=MESH).wait_recv()
        for cp in cps:
            cp.wait_send()
        acc = slots[0]
        for d in range(1, N_DEV):
            acc = acc + slots[d]
        out_ref[...] = acc

    return pl.pallas_call(
        body, name="allreduce_small",
        in_specs=[pl.BlockSpec(memory_space=pltpu.VMEM)], out_specs=pl.BlockSpec(memory_space=pltpu.VMEM),
        out_shape=jax.ShapeDtypeStruct(sp.shape, F32),
        scratch_shapes=[pltpu.VMEM((N_DEV,) + sp.shape, F32), pltpu.SemaphoreType.DMA((N_DEV - 1,)),
                        pltpu.SemaphoreType.DMA((N_DEV - 1,))],
    )(sp)


def _scalar(v):
    return jnp.reshape(v, (1,)).astype(jnp.int32)


def _row_tile(h, dtype_mult=16):
    return _pick(h, 256, dtype_mult)


def _add_sibling(name, g32, from_sib, c):
    _, r, n = g32.shape
    h = r // 2
    th = _row_tile(h)
    nt = h // th

    def body(c_ref, g_ref, s_ref, o32_ref, o16_ref):
        s = g_ref[...] + s_ref[...].astype(F32)
        o32_ref[...] = s
        o16_ref[...] = s.astype(BF16)

    blk = (None, th, n)
    return pl.pallas_call(
        body, name=name,
        grid_spec=pltpu.PrefetchScalarGridSpec(
            num_scalar_prefetch=1, grid=(N_CHIPS, nt),
            in_specs=[pl.BlockSpec(blk, lambda k, t, c_ref: (k, c_ref[0] * nt + t, 0)),
                      pl.BlockSpec(blk, lambda k, t, c_ref: (k, t, 0))],
            out_specs=[pl.BlockSpec(blk, lambda k, t, c_ref: (k, t, 0))] * 2),
        out_shape=[jax.ShapeDtypeStruct((N_CHIPS, h, n), F32), jax.ShapeDtypeStruct((N_CHIPS, h, n), BF16)],
        compiler_params=_params("arbitrary", "arbitrary"))(_scalar(c), g32, from_sib)


def _add_chips(name, p32, from_chips, me_chip):
    _, h, n = p32.shape
    th = _row_tile(h)

    def body(m_ref, p_ref, a_ref, b_ref, c_ref, o_ref):
        o_ref[...] = p_ref[...] + a_ref[...].astype(F32) + b_ref[...].astype(F32) + c_ref[...].astype(F32)

    blk = (None, th, n)
    return pl.pallas_call(
        body, name=name,
        grid_spec=pltpu.PrefetchScalarGridSpec(
            num_scalar_prefetch=1, grid=(h // th,),
            in_specs=[pl.BlockSpec(blk, lambda t, m_ref: (m_ref[0], t, 0))]
            + [pl.BlockSpec(blk, lambda t, m_ref, j=j: (j, t, 0)) for j in range(3)],
            out_specs=pl.BlockSpec((th, n), lambda t, m_ref: (t, 0))),
        out_shape=jax.ShapeDtypeStruct((h, n), F32),
        compiler_params=_params("arbitrary"))(_scalar(me_chip), p32, from_chips, from_chips, from_chips)


def _adamw_vals(w, g, m, v):
    m = ADAM_B1 * m + (1.0 - ADAM_B1) * g
    v = ADAM_B2 * v + (1.0 - ADAM_B2) * (g * g)
    m_hat = m / (1.0 - ADAM_B1 ** ADAM_STEP)
    v_hat = v / (1.0 - ADAM_B2 ** ADAM_STEP)
    delta = -ADAM_LR * (m_hat / (jnp.sqrt(v_hat) + ADAM_EPS) + ADAM_WD * w)
    return delta, m, v


def _adamw_halves(name, w, m, v, g_mine, g_sib, c):
    r, n = w.shape
    h = r // 2
    th = _row_tile(h, 8)
    nt = h // th

    def body(c_ref, w_ref, m_ref, v_ref, a_ref, b_ref, g_ref, d_ref, nm_ref, nv_ref):
        mine = (pl.program_id(0) // nt) == c_ref[0]
        g = jnp.where(mine, a_ref[...], b_ref[...])
        d, nm, nv = _adamw_vals(w_ref[...], g, m_ref[...], v_ref[...])
        g_ref[...] = g
        d_ref[...] = d
        nm_ref[...] = nm
        nv_ref[...] = nv

    full = pl.BlockSpec((th, n), lambda t, c_ref: (t, 0))
    part = pl.BlockSpec((th, n), lambda t, c_ref: (t % nt, 0))
    return pl.pallas_call(
        body, name=name,
        grid_spec=pltpu.PrefetchScalarGridSpec(
            num_scalar_prefetch=1, grid=(2 * nt,), in_specs=[full, full, full, part, part], out_specs=[full] * 4),
        out_shape=[jax.ShapeDtypeStruct((r, n), F32)] * 4,
        compiler_params=_params("arbitrary"))(_scalar(c), w, m, v, g_mine, g_sib)


def _adamw(name, w, g, m, v):
    R, W = w.shape

    def fn(accs, tv, cv):
        return list(_adamw_vals(*tv))

    return _tile_call(name, fn, R, W, _pick(R, 256), W, tiles=[(w, 0), (g, 0), (m, 0), (v, 0)], outs=[F32] * 3)


SMALL_LAYOUT = (("rel_bias", 2, 256), ("lb_param", 8, 1024), ("norm_ffn1", 8, 1024), ("norm_mix", 8, 1024),
                ("attn_sinks", 1, 8), ("rec_norm", 1, 128), ("norm_ffn2", 8, 1024), ("norm_ple", 8, 1024),
                ("norm_final", 8, 1024), ("loss", 8, 1024))


def _pack_small(vals):
    rows = []
    for name, nrows, n in SMALL_LAYOUT:
        flat = vals[name].reshape(-1)
        flat = jnp.pad(flat, (0, nrows * 128 - n))
        rows.append(flat.reshape(nrows, 128))
    packed = jnp.concatenate(rows, axis=0)
    return jnp.pad(packed, ((0, SMALL_ROWS - packed.shape[0]), (0, 0)))


def _unpack_small(packed, shapes):
    out, r = {}, 0
    for name, nrows, n in SMALL_LAYOUT:
        out[name] = packed[r:r + nrows].reshape(-1)[:n].reshape(shapes[name])
        r += nrows
    return out


def _natural(name, s):
    if name in COL_SHARDED:
        return s.transpose(1, 0, 2).reshape(s.shape[1], -1)
    return s.reshape(-1, s.shape[2])


def kernel(x, p, rel_bias, lb_param, norm_ffn1, w_ffn1_in, w_ffn1_out, norm_mix, w_in, attn_sinks, rec_norm, w_att_proj, w_rec_proj, w_out, norm_ffn2, w_ffn2_in, w_ffn2_out, norm_ple, w_ple_gate, w_ple_proj, norm_final, loss_target, m_rel_bias, m_lb_param, m_norm_ffn1, m_w_ffn1_in, m_w_ffn1_out, m_norm_mix, m_w_in, m_attn_sinks, m_rec_norm, m_w_att_proj, m_w_rec_proj, m_w_out, m_norm_ffn2, m_w_ffn2_in, m_w_ffn2_out, m_norm_ple, m_w_ple_gate, m_w_ple_proj, m_norm_final, v_rel_bias, v_lb_param, v_norm_ffn1, v_w_ffn1_in, v_w_ffn1_out, v_norm_mix, v_w_in, v_attn_sinks, v_rec_norm, v_w_att_proj, v_w_rec_proj, v_w_out, v_norm_ffn2, v_w_ffn2_in, v_w_ffn2_out, v_norm_ple, v_w_ple_gate, v_w_ple_proj, v_norm_final):
    args = dict(locals())
    wsh = {n: args[n] for n in WEIGHTS}
    B, S = x.shape[0], x.shape[1]
    T = B * S
    cx, cy, cc = _place()
    me_chip = 2 * cx + cy

    mine16 = {n: wsh[n][0].astype(BF16) for n in BIG}
    loss_p, dx, grads, part, from_chips = _local_step(
        x.reshape(T, D_MODEL), p.reshape(T, PLE_DIM), loss_target.reshape(T, D_MODEL),
        {n: wsh[n] for n in SMALL}, mine16, cc, me_chip, B, S)

    s_mine = [_add_chips("rs_add_chips_" + n, part[n][0], from_chips[n], me_chip) for n in BIG]
    s_sib = _join_halves("rs_join", s_mine)

    small_vals = {
        "rel_bias": grads["rel_bias"].T,
        "lb_param": jnp.concatenate([_colsum("dlb_sum", grads["lb_param"]),
                                     -_colsum("dlb_sum2", grads["lb_param"])], axis=0) / 8.0,
        "attn_sinks": grads["attn_sinks"][:, 0],
        "rec_norm": _colsum("drn_sum", grads["rec_norm"]).reshape(REC_HEADS, REC_DIM).sum(axis=0),
        "loss": _colsum("loss_sum", loss_p),
    }
    for n in ("norm_ffn1", "norm_mix", "norm_ffn2", "norm_ple", "norm_final"):
        small_vals[n] = _colsum(n + "_sum", grads[n])
    red = _allreduce_small(_pack_small(small_vals))
    small_shapes = {n: wsh[n].shape for n in SMALL}
    small_shapes["loss"] = (D_MODEL,)
    small = _unpack_small(red, small_shapes)
    loss = 0.5 * jnp.sum(small["loss"]) / D_MODEL

    out_g, out_d, out_m, out_v = {}, {}, {}, {}
    for n, gm, gs in zip(BIG, s_mine, s_sib):
        res = _adamw_halves("adamw_" + n, wsh[n][0], args["m_" + n][0], args["v_" + n][0], gm, gs, cc)
        out_g[n], out_d[n], out_m[n], out_v[n] = (t[None] for t in res)
    sw = _pack_small({**{n: wsh[n] for n in SMALL}, "loss": jnp.zeros((D_MODEL,), F32)})
    sm = _pack_small({**{n: args["m_" + n] for n in SMALL}, "loss": jnp.zeros((D_MODEL,), F32)})
    sv = _pack_small({**{n: args["v_" + n] for n in SMALL}, "loss": jnp.ones((D_MODEL,), F32)})
    sd, snm, snv = _adamw("adamw_small", sw, red, sm, sv)
    ud, um, uv = (_unpack_small(t, small_shapes) for t in (sd, snm, snv))
    for n in SMALL:
        out_g[n], out_d[n], out_m[n], out_v[n] = small[n], ud[n], um[n], uv[n]

    return (loss, dx.reshape(B, S, D_MODEL), *[out_g[n] for n in WEIGHTS], *[out_d[n] for n in WEIGHTS],
            *[out_m[n] for n in WEIGHTS], *[out_v[n] for n in WEIGHTS])
```

```python
import numpy as np
import jax
import jax.numpy as jnp
from jax import lax
from jax.experimental import pallas as pl
from jax.experimental.pallas import tpu as pltpu

F32 = jnp.float32
BF16 = jnp.bfloat16
MESH = pl.DeviceIdType.MESH

D_MODEL = 1024
D_FF = 2816
FF_SHARD = 2 * D_FF // 4
HEAD_DIM = 64
N_Q_HEADS = 8
ATT_BLOCK = 128
N_BUCKETS = 32
MAX_DISTANCE = 128
REC_HEADS = 4
REC_DIM = 128
PLE_DIM = 256
EPS = 1e-6
IN_W = 4864
COL_AQ, COL_AK, COL_AV, COL_RQ, COL_RF, COL_RI, COL_RG, COL_GA, COL_GB = 0, 4, 5, 6, 10, 14, 18, 22, 30

CHUNK = 64
SUB = 8
N_SUB = CHUNK // SUB
HGRN_PAIR = 2

ADAM_LR, ADAM_B1, ADAM_B2, ADAM_EPS, ADAM_WD, ADAM_STEP = 0.001, 0.9, 0.999, 1e-08, 0.01, 10

V7X_VMEM_LIMIT = 56 * 1024 * 1024
N_CHIPS = 4
N_DEV = 8

BIG = ("w_ffn1_in", "w_ffn1_out", "w_in", "w_att_proj", "w_rec_proj", "w_out",
       "w_ffn2_in", "w_ffn2_out", "w_ple_gate", "w_ple_proj")
COL_SHARDED = ("w_ffn1_in", "w_in", "w_att_proj", "w_rec_proj", "w_ffn2_in", "w_ple_proj")
WEIGHTS = ("rel_bias", "lb_param", "norm_ffn1", "w_ffn1_in", "w_ffn1_out", "norm_mix", "w_in", "attn_sinks",
           "rec_norm", "w_att_proj", "w_rec_proj", "w_out", "norm_ffn2", "w_ffn2_in", "w_ffn2_out", "norm_ple",
           "w_ple_gate", "w_ple_proj", "norm_final")
SMALL = tuple(n for n in WEIGHTS if n not in BIG)
SMALL_ROWS = 64


def _params(*sem):
    return pltpu.CompilerParams(dimension_semantics=sem, vmem_limit_bytes=V7X_VMEM_LIMIT)


def _pick(n, cap, mult=8):
    if n <= cap:
        return n
    for t in range(cap - cap % mult, 0, -mult):
        if n % t == 0:
            return t
    raise ValueError((n, cap, mult))


def _dot(a, b):
    return jnp.dot(a, b, preferred_element_type=F32)


def _dot_nt(a, b):
    return lax.dot_general(a, b, (((1,), (1,)), ((), ())), preferred_element_type=F32)


def _dot_tn(a, b):
    return lax.dot_general(a, b, (((0,), (0,)), ((), ())), preferred_element_type=F32)


def _split3(x):
    hi = x.astype(BF16)
    r = x - hi.astype(F32)
    mid = r.astype(BF16)
    lo = (r - mid.astype(F32)).astype(BF16)
    return hi, mid, lo


def _split2(x):
    hi = x.astype(BF16)
    return hi, (x - hi.astype(F32)).astype(BF16)


def _sel_left(sel_bf16, x):
    hi, mid, lo = _split3(x)
    return _dot(sel_bf16, hi) + _dot(sel_bf16, mid) + _dot(sel_bf16, lo)


def _sel_right(x, sel_bf16):
    hi, mid, lo = _split3(x)
    return _dot(hi, sel_bf16) + _dot(mid, sel_bf16) + _dot(lo, sel_bf16)


def _sigmoid(x):
    return 0.5 * jnp.tanh(0.5 * x) + 0.5


def _group8(x):
    r, w = x.shape
    return x.reshape(r // 8, 8, w).sum(axis=0)


class _Comm:
    def __init__(self, ins, out_shapes, n_sems, start, finish):
        self.ins, self.out_shapes, self.n_sems, self.start, self.finish = ins, out_shapes, n_sems, start, finish


ANY = pl.BlockSpec(memory_space=pl.ANY)


def _comm_parts(comm):
    if comm is None:
        return [], [], [], []
    sems = [pltpu.SemaphoreType.DMA((comm.n_sems,)), pltpu.SemaphoreType.DMA((comm.n_sems,))]
    return list(comm.ins), [ANY] * len(comm.ins), list(comm.out_shapes), sems


def _comm_run(comm, grid, refs, n_in, n_out):
    if comm is None:
        return (lambda: None), (lambda: None)
    nci, nco = len(comm.ins), len(comm.out_shapes)
    cin = refs[n_in:n_in + nci]
    cout = refs[n_in + nci + n_out:n_in + nci + n_out + nco]
    send_sems, recv_sems = refs[-2], refs[-1]
    ids = [pl.program_id(d) for d in range(len(grid))]
    is_first = ids[0] == 0
    is_last = ids[0] == grid[0] - 1
    for d in range(1, len(grid)):
        is_first = is_first & (ids[d] == 0)
        is_last = is_last & (ids[d] == grid[d] - 1)

    def first():
        @pl.when(is_first)
        def _():
            comm.start(cin, cout, send_sems, recv_sems)

    def last():
        @pl.when(is_last)
        def _():
            comm.finish(cin, cout, send_sems, recv_sems)

    return first, last


def _call(name, fn, grid, ins, outs, pairs=(), comm=None, j_outer=False):
    in_pair = {i for p in pairs for i in p[:2]}
    n_in, n_out = len(ins), len(outs)
    c_arrays, c_in_specs, c_out_shapes, c_sems = _comm_parts(comm)

    def body(*refs):
        first, last = _comm_run(comm, grid, refs, n_in, n_out)
        first()
        accs = []
        for ia, ib, kind in pairs:
            a, b = refs[ia][...].astype(BF16), refs[ib][...].astype(BF16)
            accs.append(_dot(a, b) if kind == "nn" else _dot_nt(a, b))
        vals = [refs[i][...] for i in range(n_in) if i not in in_pair]
        res = fn(accs, vals)
        out_refs = refs[n_in + len(c_arrays):n_in + len(c_arrays) + n_out]
        assert len(res) == len(out_refs), (name, len(res), len(out_refs))
        for o_ref, val in zip(out_refs, res):
            o_ref[...] = val.astype(o_ref.dtype)
        last()

    if j_outer:
        grid = (grid[1], grid[0])
        swap = lambda im: (lambda j, i: im(i, j))
        ins = [(a, blk, swap(im)) for a, blk, im in ins]
        outs = [(shp, dt, blk, swap(im)) for shp, dt, blk, im in outs]

    return pl.pallas_call(
        body, name=name, grid=grid,
        in_specs=[pl.BlockSpec(blk, im) for _, blk, im in ins] + c_in_specs,
        out_specs=[pl.BlockSpec(blk, im) for _, _, blk, im in outs] + [ANY] * len(c_out_shapes),
        out_shape=[jax.ShapeDtypeStruct(shp, dt) for shp, dt, _, _ in outs] + c_out_shapes,
        scratch_shapes=c_sems,
        compiler_params=_params(*(["arbitrary"] * len(grid))))(*[a for a, _, _ in ins], *c_arrays)


def _tile_call(name, fn, M, N, tm, tn, *, pairs=(), tiles=(), consts=(), outs=(), parts=0, comm=None,
               j_outer=False):
    gi, gj = M // tm, N // tn
    assert gi * tm == M and gj * tn == N, (name, M, N, tm, tn)
    ins, prs = [], []
    for a, a_col, b, kind in pairs:
        K = b.shape[0] if kind == "nn" else b.shape[1]
        ins.append((a, (tm, K), lambda i, j, c=a_col: (i, c)))
        if kind == "nn":
            ins.append((b, (K, tn), lambda i, j: (0, j)))
        else:
            ins.append((b, (tn, K), lambda i, j: (j, 0)))
        prs.append((len(ins) - 2, len(ins) - 1, kind))
    for arr, off in tiles:
        ins.append((arr, (tm, tn), lambda i, j, o=off: (i, j + o)))
    for arr in consts:
        ins.append((arr, arr.shape, lambda i, j: (0, 0)))
    out_l = [((M, N), dt, (tm, tn), lambda i, j: (i, j)) for dt in outs]
    out_l += [((gi * 8, N), F32, (8, tn), lambda i, j: (i, j))] * parts
    nt = len(tiles)

    def wrapped(accs, vals):
        return fn(accs, vals[:nt], vals[nt:])

    return _call(name, wrapped, (gi, gj), ins, out_l, prs, comm=comm, j_outer=j_outer)


def _mm_tn(name, grid, a_in, b_in, outs):
    nk = grid[2]
    tm = [d for d in a_in[1] if d is not None][1]
    tn = [d for d in b_in[1] if d is not None][1]

    def body(a_ref, b_ref, *rest):
        out_refs, acc_ref = rest[:-1], rest[-1]
        k = pl.program_id(2)

        @pl.when(k == 0)
        def _():
            acc_ref[...] = jnp.zeros_like(acc_ref)

        acc_ref[...] += _dot_tn(a_ref[...].astype(BF16), b_ref[...].astype(BF16))

        @pl.when(k == nk - 1)
        def _():
            for o_ref in out_refs:
                o_ref[...] = acc_ref[...].astype(o_ref.dtype)

    return pl.pallas_call(
        body, name=name, grid=grid,
        in_specs=[pl.BlockSpec(a_in[1], a_in[2]), pl.BlockSpec(b_in[1], b_in[2])],
        out_specs=[pl.BlockSpec(blk, im) for _, _, blk, im in outs],
        out_shape=[jax.ShapeDtypeStruct(shp, dt) for shp, dt, _, _ in outs],
        scratch_shapes=[pltpu.VMEM((tm, tn), F32)],
        compiler_params=_params("arbitrary", "arbitrary", "arbitrary"))(a_in[0], b_in[0])


def _grad_pair(shape, block, imap):
    return [(shape, F32, block, imap), (shape, BF16, block, imap)]


def _mm_tn_rows(name, a, b, tk=2048):
    T, a_w = a.shape
    b_w = b.shape[1]
    tm = _pick(a_w, 1408, 128)
    tk = _pick(T, tk, 128)
    g32, g16 = _mm_tn(name, (a_w // tm, 1, T // tk),
                      (a, (tk, tm), lambda i, j, k: (k, i)), (b, (tk, b_w), lambda i, j, k: (k, 0)),
                      _grad_pair((a_w, b_w), (tm, b_w), lambda i, j, k: (i, 0)))
    shp = (N_CHIPS, a_w // N_CHIPS, b_w)
    return g32.reshape(shp), g16.reshape(shp)


def _mm_tn_cols(name, a, b, tk=2048):
    T, a_w = a.shape
    n = b.shape[1] // N_CHIPS
    tk = _pick(T, tk, 128)
    return _mm_tn(name, (1, N_CHIPS, T // tk),
                  (a, (tk, a_w), lambda i, j, k: (k, 0)), (b, (tk, n), lambda i, j, k: (k, j)),
                  _grad_pair((N_CHIPS, a_w, n), (None, a_w, n), lambda i, j, k: (j, 0, 0)))


def _colsum(name, x):
    def body(x_ref, o_ref):
        o_ref[...] = jnp.sum(x_ref[...], axis=0, keepdims=True)
    return pl.pallas_call(body, name=name, out_shape=jax.ShapeDtypeStruct((1, x.shape[1]), F32))(x)


def _rms_hat(h):
    return h * lax.rsqrt(jnp.mean(h * h, axis=-1, keepdims=True) + EPS)


def _rms_bwd_vals(dn, h, g):
    r = lax.rsqrt(jnp.mean(h * h, axis=-1, keepdims=True) + EPS)
    nh = h * r
    gd = dn * g
    dh = r * (gd - nh * jnp.mean(gd * nh, axis=-1, keepdims=True))
    return dh, _group8(dn * nh)


def _rms_fwd(name, h, g, tm=512, comm=None):
    T = h.shape[0]

    def fn(accs, tv, cv):
        return [_rms_hat(tv[0]) * cv[0]]

    return _tile_call(name, fn, T, D_MODEL, _pick(T, tm), D_MODEL, tiles=[(h, 0)], consts=[g], outs=[BF16],
                      comm=comm)


def _ffn_fwd(tag, h, g, w_in, w_out, g_next, n=None, comm_norm=None, w_in_of=None, comm_in=None, comm_out=None,
             w_out_of=None):
    T = h.shape[0]
    if n is None:
        n, *got_norm = _rms_fwd(tag + "_norm", h, g, comm=comm_norm)
        if w_in_of is not None:
            w_in = w_in_of(got_norm)
    tm = _pick(T, 512)
    wblk = (None, D_MODEL, FF_SHARD)

    def act(accs, vals):
        gate, up = accs
        return [gate, up, gate * _sigmoid(gate) * up]

    tile = lambda: ((T, D_FF), BF16, (tm, FF_SHARD), lambda i, j: (i, j))
    gate, up, a, *got_in = _call(
        tag + "_in", act, (T // tm, 2),
        [(n, (tm, D_MODEL), lambda i, j: (i, 0)),
         (w_in, wblk, lambda i, j: (j, 0, 0)), (w_in, wblk, lambda i, j: (j + 2, 0, 0))],
        [tile(), tile(), tile()], pairs=[(0, 1, "nn"), (0, 2, "nn")], comm=comm_in, j_outer=True)

    def res(accs, tv, cv):
        h_new = tv[0] + 0.5 * accs[0]
        return [h_new, _rms_hat(h_new) * cv[0]]

    if w_out_of is not None:
        w_out = w_out_of(got_in)
    h_new, n_next, *got_out = _tile_call(
        tag + "_out", res, T, D_MODEL, _pick(T, 512), D_MODEL, pairs=[(a, 0, w_out, "nn")], tiles=[(h, 0)],
        consts=[g_next], outs=[F32, BF16], comm=comm_out)
    return h_new, n_next, (n, gate, up, a), got_out


def _ffn_bwd(tag, dh_out, df, h, g, w_in, w_out, saved, comm=None, comm_last=None):
    T = h.shape[0]
    n, gate, up, a = saved
    tm = _pick(T, 512)

    def dact(accs, vals):
        da = accs[0]
        gt, u = vals[0].astype(F32), vals[1].astype(F32)
        sg = _sigmoid(gt)
        silu = gt * sg
        return [jnp.stack([(da * u * (sg + silu * (1.0 - sg))).astype(BF16), (da * silu).astype(BF16)])]

    dz, *got = _call(
        tag + "_dact", dact, (T // tm, 2),
        [(df, (tm, D_MODEL), lambda i, j: (i, 0)), (w_out, (FF_SHARD, D_MODEL), lambda i, j: (j, 0)),
         (gate, (tm, FF_SHARD), lambda i, j: (i, j)), (up, (tm, FF_SHARD), lambda i, j: (i, j))],
        [((2, T, D_FF), BF16, (2, tm, FF_SHARD), lambda i, j: (0, i, j))], pairs=[(0, 1, "nt")], comm=comm,
        j_outer=True)
    dw_out = _mm_tn_rows(tag + "_dwout", a, df)
    tk = _pick(T, 2048, 128)
    dw_in = _mm_tn(tag + "_dwin", (1, N_CHIPS, T // tk),
                   (n, (tk, D_MODEL), lambda i, j, k: (k, 0)),
                   (dz, (None, tk, FF_SHARD), lambda i, j, k: (j // 2, k, j % 2)),
                   _grad_pair((N_CHIPS, D_MODEL, FF_SHARD), (None, D_MODEL, FF_SHARD), lambda i, j, k: (j, 0, 0)))

    def dnorm(accs, vals):
        dn = accs[0] + accs[1] + accs[2] + accs[3]
        dh, dg = _rms_bwd_vals(dn, vals[0], vals[2])
        dh = vals[1] + dh
        return [dh, dh, dg]

    tm2 = _pick(T, 256)
    ins = [(dz, (None, tm2, FF_SHARD), lambda i, j, s=s: (s // 2, i, s % 2)) for s in range(N_CHIPS)]
    ins += [(w_in, (None, D_MODEL, FF_SHARD), lambda i, j, s=s: (s, 0, 0)) for s in range(N_CHIPS)]
    ins += [(h, (tm2, D_MODEL), lambda i, j: (i, 0)), (dh_out, (tm2, D_MODEL), lambda i, j: (i, 0)),
            (g, g.shape, lambda i, j: (0, 0))]
    dh, dh16, dg, *got_last = _call(
        tag + "_dnorm", dnorm, (T // tm2, 1), ins,
        [((T, D_MODEL), F32, (tm2, D_MODEL), lambda i, j: (i, 0)),
         ((T, D_MODEL), BF16, (tm2, D_MODEL), lambda i, j: (i, 0)),
         ((T // tm2 * 8, D_MODEL), F32, (8, D_MODEL), lambda i, j: (i, 0))],
        pairs=[(s, N_CHIPS + s, "nt") for s in range(N_CHIPS)],
        comm=None if comm_last is None else comm_last(dw_in, dw_out))
    return dh, dh16, dg, dw_in, dw_out, got, got_last


def _t5_onehot():
    qi = np.arange(ATT_BLOCK)[:, None] + ATT_BLOCK
    kj = np.arange(2 * ATT_BLOCK)[None, :]
    nn = np.maximum(qi - kj, 0)
    max_exact = N_BUCKETS // 2
    large = max_exact + (np.log(np.maximum(nn, 1) / max_exact) / np.log(MAX_DISTANCE / max_exact)
                         * (N_BUCKETS - max_exact)).astype(np.int32)
    large = np.minimum(large, N_BUCKETS - 1)
    bucket = np.where(nn < max_exact, nn, large).astype(np.int32).reshape(-1)
    return (bucket[None, :] == np.arange(N_BUCKETS)[:, None]).astype(np.float32)


def _small_mm(name, a, b, sel):
    def body(a_ref, b_ref, o_ref):
        if sel == "right":
            o_ref[...] = _sel_right(a_ref[...], b_ref[...])
        else:
            o_ref[...] = _sel_left(a_ref[...], b_ref[...])
    return pl.pallas_call(body, name=name, out_shape=jax.ShapeDtypeStruct((a.shape[0], b.shape[1]), F32),
                          compiler_params=pltpu.CompilerParams(vmem_limit_bytes=V7X_VMEM_LIMIT))(a, b)


def _dup_heads(t):
    a, b = t[:, :HEAD_DIM], t[:, HEAD_DIM:]
    return jnp.concatenate([a, a, b, b], axis=1)


def _kv_layouts(proj):
    T = proj.shape[0]

    def fn(accs, tv, cv):
        return [tv[0], tv[1]]

    k, v = _tile_call("kv_cast", fn, T, 128, _pick(T, 1024), 128, tiles=[(proj, COL_AK), (proj, COL_AV)],
                      outs=[BF16, BF16])
    return _dup_heads(k), _dup_heads(v)


def _swa_masks():
    row = lax.broadcasted_iota(jnp.int32, (ATT_BLOCK, 2 * ATT_BLOCK), 0)
    col = lax.broadcasted_iota(jnp.int32, (ATT_BLOCK, 2 * ATT_BLOCK), 1)
    dist = ATT_BLOCK + row - col
    return (dist >= 0) & (dist < ATT_BLOCK), col


GROUP = 4


def _stack_group(blk, lo_q):
    zero = jnp.zeros_like(blk[:, :128])
    rows = []
    for pair in range(GROUP // 2):
        pb = blk[:, 128 * pair:128 * (pair + 1)]
        rows += [jnp.where(lo_q, pb, zero), jnp.where(lo_q, zero, pb)]
    return jnp.concatenate(rows, axis=0)


def _unstack_group(st, lo_q):
    pairs = [jnp.where(lo_q, st[256 * pair:256 * pair + 128], st[256 * pair + 128:256 * (pair + 1)])
             for pair in range(GROUP // 2)]
    return jnp.concatenate(pairs, axis=1)


def _swa_probs(s, bias_h, sink, valid):
    s = jnp.where(valid, s * (HEAD_DIM ** -0.5) + bias_h, -jnp.inf)
    m = jnp.maximum(jnp.max(s, axis=-1, keepdims=True), sink)
    e = jnp.exp(s - m)
    es = jnp.exp(sink - m)
    den = jnp.sum(e, axis=-1, keepdims=True) + es
    return e / den, es / den


def _swa_fwd(proj, kk2, vv2, bias, sinks, B, S):
    T = B * S
    nb = S // ATT_BLOCK

    def body(q_ref, k_ref, v_ref, bias_ref, sink_ref, o_ref, kpad, vpad):
        zeros = jnp.zeros((ATT_BLOCK, 256), BF16)
        kpad[pl.ds(0, ATT_BLOCK), :] = zeros
        vpad[pl.ds(0, ATT_BLOCK), :] = zeros
        kpad[pl.ds(ATT_BLOCK, S), :] = k_ref[...]
        vpad[pl.ds(ATT_BLOCK, S), :] = v_ref[...]
        valid0, col = _swa_masks()
        lo_q = lax.broadcasted_iota(jnp.int32, (1, 128), 1) < HEAD_DIM

        def blk(n, carry):
            r0 = pl.multiple_of(n * ATT_BLOCK, ATT_BLOCK)
            rows = pl.ds(r0, ATT_BLOCK)
            valid = valid0 & ((n > 0) | (col >= ATT_BLOCK))
            for g in range(N_Q_HEADS // GROUP):
                lanes = pl.ds(128 * g, 128)
                kg = kpad[pl.ds(r0, 2 * ATT_BLOCK), lanes]
                vg = vpad[pl.ds(r0, 2 * ATT_BLOCK), lanes]
                qm = _stack_group(q_ref[rows, pl.ds(256 * g, 256)].astype(BF16), lo_q)
                s = _dot_nt(qm, kg)
                ps = []
                for i in range(GROUP):
                    h = GROUP * g + i
                    p, _ = _swa_probs(s[ATT_BLOCK * i:ATT_BLOCK * (i + 1)], bias_ref[h], sink_ref[h], valid)
                    ps.append(p.astype(BF16))
                o = _dot(jnp.concatenate(ps, axis=0), vg)
                o_ref[rows, pl.ds(256 * g, 256)] = _unstack_group(o, lo_q).astype(o_ref.dtype)
            return carry

        lax.fori_loop(0, nb, blk, 0)

    return pl.pallas_call(
        body, name="swa_fwd", grid=(B,),
        in_specs=[pl.BlockSpec((S, 512), lambda b: (b, 0)),
                  pl.BlockSpec((S, 256), lambda b: (b, 0)),
                  pl.BlockSpec((S, 256), lambda b: (b, 0)),
                  pl.BlockSpec((N_Q_HEADS, ATT_BLOCK, 2 * ATT_BLOCK), lambda b: (0, 0, 0)),
                  pl.BlockSpec(memory_space=pltpu.SMEM)],
        out_specs=pl.BlockSpec((S, 512), lambda b: (b, 0)),
        out_shape=jax.ShapeDtypeStruct((T, 512), BF16),
        scratch_shapes=[pltpu.VMEM((S + ATT_BLOCK, 256), BF16), pltpu.VMEM((S + ATT_BLOCK, 256), BF16)],
        compiler_params=_params("arbitrary"))(proj, kk2, vv2, bias, sinks)


def _swa_bwd(proj, kk2, vv2, bias, sinks, datt, B, S):
    T = B * S
    nb = S // ATT_BLOCK

    def body(q_ref, k_ref, v_ref, bias_ref, sink_ref, do_ref, dq_ref, dk_ref, dv_ref, dbias_ref, dsink_ref,
             kpad, vpad, dkpad, dvpad):
        b = pl.program_id(0)

        @pl.when(b == 0)
        def _():
            dbias_ref[...] = jnp.zeros_like(dbias_ref)
            dsink_ref[...] = jnp.zeros_like(dsink_ref)

        zeros = jnp.zeros((ATT_BLOCK, 256), BF16)
        kpad[pl.ds(0, ATT_BLOCK), :] = zeros
        vpad[pl.ds(0, ATT_BLOCK), :] = zeros
        kpad[pl.ds(ATT_BLOCK, S), :] = k_ref[...]
        vpad[pl.ds(ATT_BLOCK, S), :] = v_ref[...]
        dkpad[...] = jnp.zeros_like(dkpad)
        dvpad[...] = jnp.zeros_like(dvpad)
        valid0, col = _swa_masks()
        lo_q = lax.broadcasted_iota(jnp.int32, (1, 128), 1) < HEAD_DIM
        scale = HEAD_DIM ** -0.5

        def blk(n, carry):
            r0 = pl.multiple_of(n * ATT_BLOCK, ATT_BLOCK)
            rows = pl.ds(r0, ATT_BLOCK)
            band = pl.ds(r0, 2 * ATT_BLOCK)
            valid = valid0 & ((n > 0) | (col >= ATT_BLOCK))
            for g in range(N_Q_HEADS // GROUP):
                lanes = pl.ds(128 * g, 128)
                kg = kpad[band, lanes]
                vg = vpad[band, lanes]
                qm = _stack_group(q_ref[rows, pl.ds(256 * g, 256)].astype(BF16), lo_q)
                dom = _stack_group(do_ref[rows, pl.ds(256 * g, 256)], lo_q)
                s = _dot_nt(qm, kg)
                dp = _dot_nt(dom, vg)
                pst, dst = [], []
                for i in range(GROUP):
                    h = GROUP * g + i
                    sl = slice(ATT_BLOCK * i, ATT_BLOCK * (i + 1))
                    p, ps = _swa_probs(s[sl], bias_ref[h], sink_ref[h], valid)
                    delta = jnp.sum(p * dp[sl], axis=-1, keepdims=True)
                    ds = p * (dp[sl] - delta)
                    dbias_ref[h] += ds
                    dsink_ref[pl.ds(h, 1), :] += -jnp.sum(jnp.broadcast_to(ps * delta, (ATT_BLOCK, 128)),
                                                          axis=0, keepdims=True)
                    pst.append(p.astype(BF16))
                    dst.append((ds * scale).astype(BF16))
                pst, dst = jnp.concatenate(pst, axis=0), jnp.concatenate(dst, axis=0)
                dq_ref[rows, pl.ds(256 * g, 256)] = _unstack_group(_dot(dst, kg), lo_q).astype(dq_ref.dtype)
                dkpad[band, lanes] += _dot_tn(dst, qm)
                dvpad[band, lanes] += _dot_tn(pst, dom)
            return carry

        lax.fori_loop(0, nb, blk, 0)
        lo_out = lax.broadcasted_iota(jnp.int32, (1, 128), 1) < HEAD_DIM

        def fold(pad_ref):
            halves = []
            for g in range(N_Q_HEADS // GROUP):
                t = pad_ref[pl.ds(ATT_BLOCK, S), pl.ds(128 * g, 128)]
                halves.append(t + pltpu.roll(t, HEAD_DIM, 1))
            return jnp.where(lo_out, halves[0], halves[1])

        dk_ref[...] = fold(dkpad).astype(dk_ref.dtype)
        dv_ref[...] = fold(dvpad).astype(dv_ref.dtype)

    return pl.pallas_call(
        body, name="swa_bwd", grid=(B,),
        in_specs=[pl.BlockSpec((S, 512), lambda b: (b, 0)),
                  pl.BlockSpec((S, 256), lambda b: (b, 0)),
                  pl.BlockSpec((S, 256), lambda b: (b, 0)),
                  pl.BlockSpec((N_Q_HEADS, ATT_BLOCK, 2 * ATT_BLOCK), lambda b: (0, 0, 0)),
                  pl.BlockSpec(memory_space=pltpu.SMEM),
                  pl.BlockSpec((S, 512), lambda b: (b, 0))],
        out_specs=[pl.BlockSpec((S, 512), lambda b: (b, 0)),
                   pl.BlockSpec((S, 128), lambda b: (b, 0)),
                   pl.BlockSpec((S, 128), lambda b: (b, 0)),
                   pl.BlockSpec((N_Q_HEADS, ATT_BLOCK, 2 * ATT_BLOCK), lambda b: (0, 0, 0)),
                   pl.BlockSpec((N_Q_HEADS, 128), lambda b: (0, 0))],
        out_shape=[jax.ShapeDtypeStruct((T, 512), BF16),
                   jax.ShapeDtypeStruct((T, 128), BF16),
                   jax.ShapeDtypeStruct((T, 128), BF16),
                   jax.ShapeDtypeStruct((N_Q_HEADS, ATT_BLOCK, 2 * ATT_BLOCK), F32),
                   jax.ShapeDtypeStruct((N_Q_HEADS, 128), F32)],
        scratch_shapes=[pltpu.VMEM((S + ATT_BLOCK, 256), BF16), pltpu.VMEM((S + ATT_BLOCK, 256), BF16),
                        pltpu.VMEM((S + ATT_BLOCK, 256), F32), pltpu.VMEM((S + ATT_BLOCK, 256), F32)],
        compiler_params=_params("arbitrary"))(proj, kk2, vv2, bias, sinks, datt)


def _hgrn_gates(z, lb):
    sg = _sigmoid(z)
    f = lb + (1.0 - lb) * sg
    return sg, f, jnp.log(f), 1.0 - f


def _hgrn_consts():
    r = lax.broadcasted_iota(jnp.int32, (CHUNK, CHUNK), 0)
    c = lax.broadcasted_iota(jnp.int32, (CHUNK, CHUNK), 1)
    tril = (r >= c).astype(BF16)
    triu = (r <= c).astype(BF16)
    causal = r >= c
    below = (r // SUB) > (c // SUB)
    inside = ((r // SUB) == (c // SUB)) & causal
    return tril, triu, causal, below, inside, c


def _block_rows(ref, lanes, s):
    rows = []
    for i in range(N_SUB):
        if SUB * i + s < 0:
            rows.append(jnp.zeros((SUB, REC_DIM), F32))
        else:
            rows.append(jnp.broadcast_to(ref[pl.ds(SUB * i + s, 1), lanes], (SUB, REC_DIM)))
    return jnp.concatenate(rows, axis=0)


def _hgrn_offdiag(q, k, bcum, b_ref, lanes):
    eq = jnp.exp(jnp.minimum(bcum - _block_rows(b_ref, lanes, -1), 0.0))
    qe = q * eq
    zero = jnp.zeros((SUB, REC_DIM), F32)
    q_rows, k_cols, eks = [jnp.zeros((SUB, (N_SUB - 1) * REC_DIM), F32)], [], []
    for i in range(1, N_SUB):
        q_rows.append(jnp.concatenate([zero] * (i - 1) + [qe[SUB * i:SUB * (i + 1), :]] + [zero] * (N_SUB - 1 - i),
                                      axis=1))
        p = b_ref[pl.ds(SUB * i - 1, 1), lanes]
        pad = jnp.zeros((CHUNK - SUB * i, REC_DIM), F32)
        ek = jnp.concatenate([jnp.exp(p - b_ref[pl.ds(0, SUB * i), lanes]), pad], axis=0)
        k_cols.append(k * ek)
        eks.append(ek)
    return jnp.concatenate(q_rows, axis=0), jnp.concatenate(k_cols, axis=1), eq, eks


def _hgrn_fwd(proj, lb_param, B, S):
    T = B * S
    nc = S // CHUNK

    def body(q_ref, z_ref, v_ref, lb_ref, o_ref, st_ref, k_slots, b_slots):
        tril, _, _, below, inside, col = _hgrn_consts()
        col_s = col & (SUB - 1)

        def chunk(ci, hts, slot):
            k_s, b_s = k_slots.at[slot], b_slots.at[slot]
            r0 = pl.multiple_of(ci * CHUNK, CHUNK)
            lb = _sigmoid(lb_ref[0:1, :] - lb_ref[1:2, :])
            _, _, g_all, k_all = _hgrn_gates(z_ref[pl.ds(r0, CHUNK), :], lb)
            b_all = _sel_left(tril, g_all)
            k_s[...] = k_all
            b_s[...] = b_all
            new = []
            for e, ht in enumerate(hts):
                lanes = pl.ds(REC_DIM * e, REC_DIM)
                cols = slice(REC_DIM * e, REC_DIM * (e + 1))
                q = q_ref[pl.ds(r0, CHUNK), lanes]
                v = v_ref[pl.ds(r0, CHUNK), lanes]
                k, bcum = k_all[:, cols], b_all[:, cols]
                st_ref[e * nc + ci] = ht
                qst, kst, _, _ = _hgrn_offdiag(q, k, bcum, b_s, lanes)
                d = jnp.zeros((CHUNK, CHUNK), F32)
                for s in range(SUB):
                    w = jnp.exp(jnp.minimum(bcum - _block_rows(b_s, lanes, s), 0.0))
                    colv = jnp.sum(q * _block_rows(k_s, lanes, s) * w, axis=-1, keepdims=True)
                    d = jnp.where(col_s == s, colv, d)
                a = jnp.where(below, _dot_nt(qst.astype(BF16), kst.astype(BF16)), 0.0) + jnp.where(inside, d, 0.0)
                vb = v.astype(BF16)
                qb = (q * jnp.exp(bcum)).astype(BF16)
                o_ref[pl.ds(r0, CHUNK), lanes] = _dot(a.astype(BF16), vb) + _dot_nt(qb, ht.astype(BF16))
                b_last = b_s[pl.ds(CHUNK - 1, 1), lanes]
                kb = (k * jnp.exp(b_last - bcum)).astype(BF16)
                new.append(ht * jnp.exp(b_last) + _dot_tn(vb, kb))
            return tuple(new)

        lax.fori_loop(0, nc // 2, lambda i, hts: chunk(2 * i + 1, chunk(2 * i, hts, 0), 1),
                      tuple(jnp.zeros((REC_DIM, REC_DIM), F32) for _ in range(HGRN_PAIR)))

    hp, wd = REC_HEADS // HGRN_PAIR, HGRN_PAIR * REC_DIM
    cq, cf, ci_ = (c * REC_DIM // wd for c in (COL_RQ, COL_RF, COL_RI))
    return pl.pallas_call(
        body, name="hgrn_fwd", grid=(B, hp),
        in_specs=[pl.BlockSpec((S, wd), lambda b, h: (b, cq + h)),
                  pl.BlockSpec((S, wd), lambda b, h: (b, cf + h)),
                  pl.BlockSpec((S, wd), lambda b, h: (b, ci_ + h)),
                  pl.BlockSpec((2, wd), lambda b, h: (0, h))],
        out_specs=[pl.BlockSpec((S, wd), lambda b, h: (b, h)),
                   pl.BlockSpec((HGRN_PAIR * nc, REC_DIM, REC_DIM), lambda b, h: (b * hp + h, 0, 0))],
        out_shape=[jax.ShapeDtypeStruct((T, 512), F32),
                   jax.ShapeDtypeStruct((B * REC_HEADS * nc, REC_DIM, REC_DIM), F32)],
        scratch_shapes=[pltpu.VMEM((2, CHUNK, wd), F32), pltpu.VMEM((2, CHUNK, wd), F32)],
        compiler_params=_params("arbitrary", "arbitrary"))(proj, proj, proj, lb_param)


def _hgrn_bwd(proj, lb_param, states, do, B, S, comm=None):
    T = B * S
    nc = S // CHUNK

    c_arrays, c_in_specs, c_out_shapes, c_sems = _comm_parts(comm)
    nci, nco = len(c_arrays), len(c_out_shapes)

    def body(*refs):
        q_ref, z_ref, v_ref, lb_ref, st_ref, do_ref = refs[:6]
        dq_ref, dz_ref, dv_ref, dlb_ref = refs[6 + nci:10 + nci]
        slots = refs[10 + nci + nco:14 + nci + nco]
        comm_first, comm_last = _comm_run(comm, (B, REC_HEADS // HGRN_PAIR), refs, 6, 4)
        comm_first()
        tril, triu, causal, below, inside, col = _hgrn_consts()
        col_s = col & (SUB - 1)
        last_row = lax.broadcasted_iota(jnp.int32, (CHUNK, 1), 0) == CHUNK - 1
        rc = lax.broadcasted_iota(jnp.int32, (CHUNK, SUB * REC_DIM), 0)
        lc = lax.broadcasted_iota(jnp.int32, (CHUNK, SUB * REC_DIM), 1)
        spread = ((rc & (SUB - 1)) == (lc // REC_DIM)).astype(BF16)
        rr = lax.broadcasted_iota(jnp.int32, (CHUNK, SUB * CHUNK), 0)
        cc = lax.broadcasted_iota(jnp.int32, (CHUNK, SUB * CHUNK), 1)
        gather = (((rr // SUB) == ((cc & (CHUNK - 1)) // SUB)) & ((rr & (SUB - 1)) == (cc // CHUNK))).astype(BF16)

        heads = range(HGRN_PAIR)
        cols = [slice(REC_DIM * e, REC_DIM * (e + 1)) for e in heads]
        lanes = [pl.ds(REC_DIM * e, REC_DIM) for e in heads]
        lane_cat = lambda vals: jnp.concatenate(vals, axis=1)
        row_cat = lambda vals: jnp.concatenate(vals, axis=0)

        def chunk(it, carry, slot):
            k_s, b_s, pc_hi, pc_lo = (r.at[slot] for r in slots)
            dhts, dlb = carry
            ci = nc - 1 - it
            r0 = pl.multiple_of(ci * CHUNK, CHUNK)
            rows = pl.ds(r0, CHUNK)
            lb = _sigmoid(lb_ref[0:1, :] - lb_ref[1:2, :])
            sg, f, g_all, k_all = _hgrn_gates(z_ref[rows, :], lb)
            b_all = _sel_left(tril, g_all)
            k_s[...] = k_all
            b_s[...] = b_all
            q_all = q_ref[rows, :]
            das, hd = [], []
            for e in heads:
                vb, dob = v_ref[rows, lanes[e]].astype(BF16), do_ref[rows, lanes[e]].astype(BF16)
                da = jnp.where(causal, _dot_nt(dob, vb), 0.0)
                das.append(jnp.where(inside, da, 0.0))
                hd.append((vb, dob, da))
            da_hi, da_lo = _split2(row_cat(das))
            da_in = _dot(da_hi, spread) + _dot(da_lo, spread)
            ds, dqs = [], []
            for e in heads:
                q, bcum = q_all[:, cols[e]], b_all[:, cols[e]]
                d = jnp.zeros((CHUNK, CHUNK), F32)
                dq = jnp.zeros((CHUNK, REC_DIM), F32)
                for s in range(SUB):
                    w = jnp.exp(jnp.minimum(bcum - _block_rows(b_s, lanes[e], s), 0.0))
                    ks = _block_rows(k_s, lanes[e], s)
                    qw = q * w
                    d = jnp.where(col_s == s, jnp.sum(qw * ks, axis=-1, keepdims=True), d)
                    da_s = da_in[CHUNK * e:CHUNK * (e + 1), REC_DIM * s:REC_DIM * (s + 1)]
                    dq = dq + da_s * ks * w
                    hi, lo = _split2(da_s * qw)
                    pc_hi[pl.ds(CHUNK * s, CHUNK), lanes[e]] = hi
                    pc_lo[pl.ds(CHUNK * s, CHUNK), lanes[e]] = lo
                ds.append(d)
                dqs.append(dq)
            dk_in = _dot(gather, pc_hi[...]) + _dot(gather, pc_lo[...])
            dq_out, dk_out, dv_out, db_out, new_dhts = [], [], [], [], []
            for e in heads:
                q, k, bcum = q_all[:, cols[e]], k_all[:, cols[e]], b_all[:, cols[e]]
                vb, dob, da = hd[e]
                dht, ht = dhts[e], st_ref[e * nc + ci]
                qst, kst, eq, eks = _hgrn_offdiag(q, k, bcum, b_s, lanes[e])
                qst_b, kst_b = qst.astype(BF16), kst.astype(BF16)
                a = jnp.where(below, _dot_nt(qst_b, kst_b), 0.0) + jnp.where(inside, ds[e], 0.0)
                da_off = jnp.where(below, da, 0.0).astype(BF16)
                dqst = _dot(da_off, kst_b)
                dkst = _dot_tn(da_off, qst_b)
                dk = dk_in[:, cols[e]]
                dq_rows = [jnp.zeros((SUB, REC_DIM), F32)]
                for i in range(1, N_SUB):
                    dq_rows.append(dqst[SUB * i:SUB * (i + 1), REC_DIM * (i - 1):REC_DIM * i])
                    dk = dk + dkst[:, REC_DIM * (i - 1):REC_DIM * i] * eks[i - 1]
                dq = dqs[e] + row_cat(dq_rows) * eq
                eb = jnp.exp(bcum)
                b_last = b_s[pl.ds(CHUNK - 1, 1), lanes[e]]
                el = jnp.exp(b_last)
                ekb = jnp.exp(b_last - bcum)
                qb = (q * eb).astype(BF16)
                kb = k * ekb
                dhb = dht.astype(BF16)
                dv_out.append(_dot_tn(a.astype(BF16), dob) + _dot_nt(kb.astype(BF16), dhb))
                dqb = _dot(dob, ht.astype(BF16))
                dkb = _dot(vb, dhb)
                new_dhts.append(dht * el + _dot_tn(dob, qb))
                dq = dq + eb * dqb
                dk = dk + ekb * dkb
                edge = jnp.sum(kb * dkb, axis=0, keepdims=True) + el * jnp.sum(ht * dht, axis=0, keepdims=True)
                db_out.append(q * dq - k * dk + jnp.where(last_row, edge, 0.0))
                dq_out.append(dq)
                dk_out.append(dk)
            dk_all = lane_cat(dk_out)
            db_hi, db_lo = _split2(lane_cat(db_out))
            dg = _dot(triu, db_hi) + _dot(triu, db_lo)
            df = dg / f - dk_all
            dz_ref[rows, :] = (df * (1.0 - lb) * sg * (1.0 - sg)).astype(dz_ref.dtype)
            dq_ref[rows, :] = lane_cat(dq_out).astype(dq_ref.dtype)
            dv_ref[rows, :] = lane_cat(dv_out).astype(dv_ref.dtype)
            return tuple(new_dhts), dlb + jnp.sum(df * (1.0 - sg), axis=0, keepdims=True)

        zero = (tuple(jnp.zeros((REC_DIM, REC_DIM), F32) for _ in heads), jnp.zeros((1, HGRN_PAIR * REC_DIM), F32))
        _, dlb = lax.fori_loop(0, nc // 2, lambda i, c: chunk(2 * i + 1, chunk(2 * i, c, 0), 1), zero)
        lb = _sigmoid(lb_ref[0:1, :] - lb_ref[1:2, :])
        dlb_ref[...] = jnp.broadcast_to(dlb * lb * (1.0 - lb), (8, HGRN_PAIR * REC_DIM))
        comm_last()

    hp, wd = REC_HEADS // HGRN_PAIR, HGRN_PAIR * REC_DIM
    cq, cf, ci_ = (c * REC_DIM // wd for c in (COL_RQ, COL_RF, COL_RI))
    return pl.pallas_call(
        body, name="hgrn_bwd", grid=(B, hp),
        in_specs=[pl.BlockSpec((S, wd), lambda b, h: (b, cq + h)),
                  pl.BlockSpec((S, wd), lambda b, h: (b, cf + h)),
                  pl.BlockSpec((S, wd), lambda b, h: (b, ci_ + h)),
                  pl.BlockSpec((2, wd), lambda b, h: (0, h)),
                  pl.BlockSpec((HGRN_PAIR * nc, REC_DIM, REC_DIM), lambda b, h: (b * hp + h, 0, 0)),
                  pl.BlockSpec((S, wd), lambda b, h: (b, h))] + c_in_specs,
        out_specs=[pl.BlockSpec((S, wd), lambda b, h: (b, h))] * 3
        + [pl.BlockSpec((8, wd), lambda b, h: (b, h))] + [ANY] * nco,
        out_shape=[jax.ShapeDtypeStruct((T, 512), BF16)] * 3 + [jax.ShapeDtypeStruct((B * 8, 512), F32)]
        + c_out_shapes,
        scratch_shapes=[pltpu.VMEM((2, CHUNK, wd), F32)] * 2 + [pltpu.VMEM((2, SUB * CHUNK, wd), BF16)] * 2 + c_sems,
        compiler_params=_params("arbitrary", "arbitrary"))(proj, proj, proj, lb_param, states, do, *c_arrays)


def _rec_gate_fwd(rec, proj, rec_norm):
    T = rec.shape[0]

    def fn(accs, tv, cv):
        return [_rms_hat(tv[0]) * cv[0] * _sigmoid(tv[1])]

    return _tile_call("rec_gate", fn, T, 512, _pick(T, 1024), REC_DIM, tiles=[(rec, 0), (proj, COL_RG)],
                      consts=[rec_norm], outs=[BF16])[0]


def _rec_gate_bwd(dyb, w_rec_proj, rec, proj, rec_norm):
    T = rec.shape[0]

    def fn(accs, tv, cv):
        d, r, rg = accs[0], tv[0], tv[1]
        sg = _sigmoid(rg)
        rn = _rms_hat(r) * cv[0]
        dh, dg = _rms_bwd_vals(d * sg, r, cv[0])
        return [dh, d * rn * sg * (1.0 - sg), dg]

    return _tile_call("rec_gate_bwd", fn, T, 512, _pick(T, 1024), REC_DIM, pairs=[(dyb, 0, w_rec_proj, "nt")],
                      tiles=[(rec, 0), (proj, COL_RG)], consts=[rec_norm], outs=[F32, BF16], parts=1)


def _mix_out_fwd(att, recn, proj, w_att_proj, w_rec_proj, w_out, h1, g_next):
    T = att.shape[0]
    tn = 256

    def merge(accs, tv, cv):
        ya, yb = accs
        return [ya, yb, _sigmoid(tv[0]) * ya + _sigmoid(tv[1]) * yb]

    ya, yb, merged = _tile_call(
        "merge", merge, T, D_MODEL, _pick(T, 1024), tn,
        pairs=[(att, 0, w_att_proj, "nn"), (recn, 0, w_rec_proj, "nn")],
        tiles=[(proj, COL_GA * 128 // tn), (proj, COL_GB * 128 // tn)], outs=[BF16] * 3)

    def res(accs, tv, cv):
        h2 = tv[0] + accs[0]
        return [h2, _rms_hat(h2) * cv[0]]

    h2, n2 = _tile_call("mix_out", res, T, D_MODEL, _pick(T, 512), D_MODEL, pairs=[(merged, 0, w_out, "nn")],
                        tiles=[(h1, 0)], consts=[g_next], outs=[F32, BF16])
    return h2, n2, (ya, yb, merged)


GATHER_FIRST = ("w_ffn1_in",)
GATHER_MIX = ("w_ffn1_out", "w_in", "w_att_proj", "w_rec_proj", "w_out")
GATHER_LAST = ("w_ffn2_in", "w_ffn2_out", "w_ple_gate", "w_ple_proj")
SCATTER_LATE = ("w_ple_gate", "w_ple_proj", "w_ffn2_in", "w_ffn2_out")
SCATTER_MIX = ("w_out", "w_att_proj", "w_rec_proj", "w_in")
SCATTER_LAST = ("w_ffn1_in", "w_ffn1_out")


def _local_step(x, p, tgt, w, mine16, cc, me_chip, B, S):
    T = B * S
    w = dict(w)
    g_ffn1, g_mix, g_ffn2, g_ple = w["norm_ffn1"], w["norm_mix"], w["norm_ffn2"], w["norm_ple"]
    g_fin = w["norm_final"].reshape(1, D_MODEL)
    grads, part, from_chips = {}, {}, {}

    def gather(names):
        return _gather_comm([mine16[n] for n in names])

    def place(names, got):
        for n, g in zip(names, got):
            full = lax.dynamic_update_index_in_dim(g, mine16[n], me_chip, 0)
            w[n] = full if n in ("w_ffn1_in", "w_ffn2_in") else _natural(n, full)

    def scatter(tag, names):
        from_sib = _swap_halves("rs_sibling_" + tag, [grads[n][1] for n in names])
        for n, fs in zip(names, from_sib):
            part[n] = _add_sibling("rs_add_sib_" + n, grads[n][0], fs, cc)
        return _scatter_comm([part[n][1] for n in names])

    def scattered(names, got):
        for n, g in zip(names, got):
            from_chips[n] = g

    def ffn1_in_weight(got):
        place(GATHER_FIRST, got)
        return w["w_ffn1_in"]

    def ffn1_out_weight(got):
        place(GATHER_MIX, got)
        return w["w_ffn1_out"]

    h1, u, sv1, got_last = _ffn_fwd("ffn1", x, g_ffn1, None, None, g_mix, comm_norm=gather(GATHER_FIRST),
                                    w_in_of=ffn1_in_weight, comm_in=gather(GATHER_MIX),
                                    comm_out=gather(GATHER_LAST), w_out_of=ffn1_out_weight)
    place(GATHER_LAST, got_last)

    def ident(accs, tv, cv):
        return [accs[0]]

    proj = _tile_call("in_proj", ident, T, IN_W, _pick(T, 512), IN_W // 2, pairs=[(u, 0, w["w_in"], "nn")],
                      outs=[F32], j_outer=True)[0]
    onehot = jnp.asarray(_t5_onehot())
    bias = _small_mm("t5_bias", w["rel_bias"].T, onehot.astype(BF16), "right")
    bias = bias.reshape(N_Q_HEADS, ATT_BLOCK, 2 * ATT_BLOCK)
    sinks = w["attn_sinks"].reshape(N_Q_HEADS)
    kk2, vv2 = _kv_layouts(proj)
    att = _swa_fwd(proj, kk2, vv2, bias, sinks, B, S)
    rec, states = _hgrn_fwd(proj, w["lb_param"], B, S)
    recn = _rec_gate_fwd(rec, proj, w["rec_norm"])
    h2, n2, (ya, yb, merged) = _mix_out_fwd(att, recn, proj, w["w_att_proj"], w["w_rec_proj"], w["w_out"], h1,
                                            g_ffn2)
    h3, n3, sv2, _ = _ffn_fwd("ffn2", h2, g_ffn2, w["w_ffn2_in"], w["w_ffn2_out"], g_ple, n=n2)

    def ple(accs, tv, cv):
        gate = _sigmoid(accs[0])
        return [gate, accs[1], tv[0] + gate * accs[1]]

    gate_p, pp, h4 = _tile_call(
        "ple", ple, T, D_MODEL, _pick(T, 512), D_MODEL,
        pairs=[(n3, 0, w["w_ple_gate"], "nn"), (p, 0, w["w_ple_proj"], "nn")], tiles=[(h3, 0)],
        outs=[BF16, BF16, F32])

    def head(accs, tv, cv):
        h, t, gt, ppv = tv[0], tv[1], tv[2].astype(F32), tv[3].astype(F32)
        err = _rms_hat(h) * cv[0] - t
        dh, dg = _rms_bwd_vals(err * (1.0 / D_MODEL), h, cv[0])
        return [dh, dh * ppv * gt * (1.0 - gt), dh * gt, _group8(err * err), dg]

    dh4, dzg, dpp, loss_p, dg_fin = _tile_call(
        "loss_head", head, T, D_MODEL, _pick(T, 256), D_MODEL,
        tiles=[(h4, 0), (tgt, 0), (gate_p, 0), (pp, 0)], consts=[g_fin], outs=[F32, BF16, BF16], parts=2)
    grads["norm_final"] = dg_fin

    grads["w_ple_gate"] = _mm_tn_rows("ple_dwg", n3, dzg)
    grads["w_ple_proj"] = _mm_tn_cols("ple_dwp", p, dpp)

    def dnorm(accs, tv, cv):
        dh, dg = _rms_bwd_vals(accs[0], tv[0], cv[0])
        dh = tv[1] + dh
        return [dh, 0.5 * dh, dg]

    dh3, df3, grads["norm_ple"] = _tile_call(
        "ple_dnorm", dnorm, T, D_MODEL, _pick(T, 256), D_MODEL, pairs=[(dzg, 0, w["w_ple_gate"], "nt")],
        tiles=[(h3, 0), (dh4, 0)], consts=[g_ple], outs=[F32, BF16], parts=1)

    dh2, dh2b, grads["norm_ffn2"], grads["w_ffn2_in"], grads["w_ffn2_out"], _, _ = _ffn_bwd(
        "ffn2b", dh3, df3, h2, g_ffn2, w["w_ffn2_in"], w["w_ffn2_out"], sv2)
    scatter_late = scatter("late", SCATTER_LATE)

    grads["w_out"] = _mm_tn_rows("mix_dwout", merged, dh2b)
    tn = 256

    def dmerge(accs, tv, cv):
        dm = accs[0]
        sa, sb = _sigmoid(tv[0]), _sigmoid(tv[1])
        yav, ybv = tv[2].astype(F32), tv[3].astype(F32)
        return [dm * sa, dm * sb, dm * yav * sa * (1.0 - sa), dm * ybv * sb * (1.0 - sb)]

    dya, dyb, dga, dgb = _tile_call(
        "mix_dmerge", dmerge, T, D_MODEL, _pick(T, 1024), tn, pairs=[(dh2b, 0, w["w_out"], "nt")],
        tiles=[(proj, COL_GA * 128 // tn), (proj, COL_GB * 128 // tn), (ya, 0), (yb, 0)], outs=[BF16] * 4)
    grads["w_att_proj"] = _mm_tn_cols("mix_dwatt", att, dya)
    grads["w_rec_proj"] = _mm_tn_cols("mix_dwrec", recn, dyb)

    datt = _tile_call("mix_datt", ident, T, 512, _pick(T, 1024), 512, pairs=[(dya, 0, w["w_att_proj"], "nt")],
                      outs=[BF16])[0]
    drec, drg, grads["rec_norm"] = _rec_gate_bwd(dyb, w["w_rec_proj"], rec, proj, w["rec_norm"])

    drq, drf, dri, dlb, *got = _hgrn_bwd(proj, w["lb_param"], states, drec, B, S, comm=scatter_late)
    scattered(SCATTER_LATE, got)
    grads["lb_param"] = dlb
    daq, dak, dav, dbias, dsink = _swa_bwd(proj, kk2, vv2, bias, sinks, datt, B, S)
    grads["attn_sinks"] = dsink
    grads["rel_bias"] = _small_mm("t5_dbias", dbias.reshape(N_Q_HEADS, -1), onehot.T.astype(BF16), "right")
    dproj = jnp.concatenate([daq, dak, dav, drq, drf, dri, drg, dga, dgb], axis=1)
    tk = _pick(T, 2048, 128)
    w_in_shard = IN_W // N_CHIPS
    half_d = D_MODEL // 2
    gw32, gw16 = _mm_tn("mix_dwin", (2, 2, T // tk),
                        (u, (tk, half_d), lambda i, j, k: (k, i)), (dproj, (tk, IN_W // 2), lambda i, j, k: (k, j)),
                        _grad_pair((D_MODEL, IN_W), (half_d, IN_W // 2), lambda i, j, k: (i, j)))
    to_sh = lambda t: t.reshape(D_MODEL, N_CHIPS, w_in_shard).transpose(1, 0, 2)
    grads["w_in"] = (to_sh(gw32), to_sh(gw16))
    scatter_mix = scatter("mix", SCATTER_MIX)

    def dnorm_mix(accs, tv, cv):
        dh, dg = _rms_bwd_vals(accs[0], tv[0], cv[0])
        dh = tv[1] + dh
        return [dh, 0.5 * dh, dg]

    dh1, df1, grads["norm_mix"] = _tile_call(
        "mix_dnorm", dnorm_mix, T, D_MODEL, _pick(T, 256), D_MODEL, pairs=[(dproj, 0, w["w_in"], "nt")],
        tiles=[(h1, 0), (dh2, 0)], consts=[g_mix], outs=[F32, BF16], parts=1)

    def scatter_last(dw_in, dw_out):
        grads["w_ffn1_in"], grads["w_ffn1_out"] = dw_in, dw_out
        return scatter("last", SCATTER_LAST)

    dx, _, grads["norm_ffn1"], _, _, got, got_last = _ffn_bwd(
        "ffn1b", dh1, df1, x, g_ffn1, w["w_ffn1_in"], w["w_ffn1_out"], sv1, comm=scatter_mix, comm_last=scatter_last)
    scattered(SCATTER_MIX, got)
    scattered(SCATTER_LAST, got_last)
    return loss_p, dx, grads, part, from_chips


def _place():
    x, y, c = lax.axis_index("x"), lax.axis_index("y"), lax.axis_index("c")
    return x, y, c


def _other_chips(x, y):
    return [(1 - x, y, 2 * (1 - x) + y), (x, 1 - y, 2 * x + 1 - y), (1 - x, 1 - y, 2 * (1 - x) + 1 - y)]


def _half_rows(ref_3d, chip, h, rows):
    return ref_3d.at[chip, pl.ds(h * rows, rows), :]


def _gather_comm(ws):
    nw = len(ws)

    def parts(w_refs, out_refs, send_sems, recv_sems):
        x, y, c = _place()
        me = 2 * x + y
        chips = _other_chips(x, y)

        def copy(i, k, chip, h, to, src=None):
            half = ws[i].shape[0] // 2
            dst = _half_rows(out_refs[i], chip, h, half)
            return pltpu.make_async_remote_copy(
                src_ref=dst if src is None else src, dst_ref=dst,
                send_sem=send_sems.at[6 * i + k], recv_sem=recv_sems.at[6 * i + k], device_id=to, device_id_type=MESH)

        def first():
            out = []
            for i in range(nw):
                half = ws[i].shape[0] // 2
                out += [copy(i, j, me, c, (cx, cy, c), src=w_refs[i].at[pl.ds(c * half, half), :])
                        for j, (cx, cy, _) in enumerate(chips)]
            return out

        return copy, first, chips, c, (x, y, 1 - c)

    def start(*refs):
        _, first, _, _, _ = parts(*refs)
        for cp in first():
            cp.start()

    def finish(*refs):
        copy, first, chips, c, sibling = parts(*refs)
        passed = []
        for i in range(nw):
            for j, (cx, cy, ci) in enumerate(chips):
                copy(i, j, ci, c, (cx, cy, c)).wait_recv()
                fw = copy(i, 3 + j, ci, c, sibling)
                fw.start()
                passed.append(fw)
        for i in range(nw):
            for j, (_, _, ci) in enumerate(chips):
                copy(i, 3 + j, ci, 1 - c, sibling).wait_recv()
        for cp in first() + passed:
            cp.wait_send()

    return _Comm(list(ws), [jax.ShapeDtypeStruct((N_CHIPS,) + w.shape, w.dtype) for w in ws], 6 * nw, start, finish)


def _scatter_comm(ps):
    nw = len(ps)

    def copies(p_refs, out_refs, send_sems, recv_sems):
        x, y, c = _place()
        cps = []
        for i in range(nw):
            for j, (cx, cy, ci) in enumerate(_other_chips(x, y)):
                cps.append(pltpu.make_async_remote_copy(
                    src_ref=p_refs[i].at[ci], dst_ref=out_refs[i].at[j], send_sem=send_sems.at[3 * i + j],
                    recv_sem=recv_sems.at[3 * i + j], device_id=(cx, cy, c), device_id_type=MESH))
        return cps

    def start(*refs):
        for cp in copies(*refs):
            cp.start()

    def finish(*refs):
        for cp in copies(*refs):
            cp.wait()

    return _Comm(list(ps), [jax.ShapeDtypeStruct((3,) + p.shape[1:], p.dtype) for p in ps], 3 * nw, start, finish)


def _swap_halves(name, gs):
    nw = len(gs)

    def body(*refs):
        g_refs, out_refs, send_sems, recv_sems = refs[:nw], refs[nw:2 * nw], refs[2 * nw], refs[2 * nw + 1]
        x, y, c = _place()
        cps = []
        for i in range(nw):
            half = gs[i].shape[1] // 2
            cps.append(pltpu.make_async_remote_copy(
                src_ref=g_refs[i].at[:, pl.ds((1 - c) * half, half), :], dst_ref=out_refs[i],
                send_sem=send_sems.at[i], recv_sem=recv_sems.at[i], device_id=(x, y, 1 - c), device_id_type=MESH))
        for cp in cps:
            cp.start()
        for cp in cps:
            cp.wait()

    return pl.pallas_call(
        body, name=name, in_specs=[ANY] * nw, out_specs=[ANY] * nw,
        out_shape=[jax.ShapeDtypeStruct((N_CHIPS, g.shape[1] // 2, g.shape[2]), g.dtype) for g in gs],
        scratch_shapes=[pltpu.SemaphoreType.DMA((nw,)), pltpu.SemaphoreType.DMA((nw,))],
    )(*gs)


def _join_halves(name, ss):
    nw = len(ss)

    def body(*refs):
        s_refs, out_refs, send_sems, recv_sems = refs[:nw], refs[nw:2 * nw], refs[2 * nw], refs[2 * nw + 1]
        x, y, c = _place()
        cps = [pltpu.make_async_remote_copy(
            src_ref=s_refs[i], dst_ref=out_refs[i], send_sem=send_sems.at[i], recv_sem=recv_sems.at[i],
            device_id=(x, y, 1 - c), device_id_type=MESH) for i in range(nw)]
        for cp in cps:
            cp.start()
        for cp in cps:
            cp.wait()

    return pl.pallas_call(
        body, name=name, in_specs=[ANY] * nw, out_specs=[ANY] * nw,
        out_shape=[jax.ShapeDtypeStruct(s.shape, s.dtype) for s in ss],
        scratch_shapes=[pltpu.SemaphoreType.DMA((nw,)), pltpu.SemaphoreType.DMA((nw,))],
    )(*ss)


def _allreduce_small(sp):
    def body(s_ref, out_ref, slots, send_sems, recv_sems):
        x, y, c = _place()
        me = 4 * x + 2 * y + c
        slots[me] = s_ref[...]
        cps = []
        for r in range(1, N_DEV):
            px, py, pc = x ^ (r >> 2), y ^ ((r >> 1) & 1), c ^ (r & 1)
            cps.append(pltpu.make_async_remote_copy(
                src_ref=s_ref, dst_ref=slots.at[me], send_sem=send_sems.at[r - 1], recv_sem=recv_sems.at[r - 1],
                device_id=(px, py, pc), device_id_type=MESH))
        for cp in cps:
            cp.start()
        for r in range(1, N_DEV):
            px, py, pc = x ^ (r >> 2), y ^ ((r >> 1) & 1), c ^ (r & 1)
            pltpu.make_async_remote_copy(
                src_ref=s_ref, dst_ref=slots.at[4 * px + 2 * py + pc], send_sem=send_sems.at[r - 1],
                recv_sem=recv_sems.at[r - 1], device_id=(px, py, pc), device_id_type=MESH).wait_recv()
        for cp in cps:
            cp.wait_send()
        acc = slots[0]
        for d in range(1, N_DEV):
            acc = acc + slots[d]
        out_ref[...] = acc

    return pl.pallas_call(
        body, name="allreduce_small",
        in_specs=[pl.BlockSpec(memory_space=pltpu.VMEM)], out_specs=pl.BlockSpec(memory_space=pltpu.VMEM),
        out_shape=jax.ShapeDtypeStruct(sp.shape, F32),
        scratch_shapes=[pltpu.VMEM((N_DEV,) + sp.shape, F32), pltpu.SemaphoreType.DMA((N_DEV - 1,)),
                        pltpu.SemaphoreType.DMA((N_DEV - 1,))],
    )(sp)


def _scalar(v):
    return jnp.reshape(v, (1,)).astype(jnp.int32)


def _row_tile(h, dtype_mult=16):
    return _pick(h, 256, dtype_mult)


def _add_sibling(name, g32, from_sib, c):
    _, r, n = g32.shape
    h = r // 2
    th = _row_tile(h)
    nt = h // th

    def body(c_ref, g_ref, s_ref, o32_ref, o16_ref):
        s = g_ref[...] + s_ref[...].astype(F32)
        o32_ref[...] = s
        o16_ref[...] = s.astype(BF16)

    blk = (None, th, n)
    return pl.pallas_call(
        body, name=name,
        grid_spec=pltpu.PrefetchScalarGridSpec(
            num_scalar_prefetch=1, grid=(N_CHIPS, nt),
            in_specs=[pl.BlockSpec(blk, lambda k, t, c_ref: (k, c_ref[0] * nt + t, 0)),
                      pl.BlockSpec(blk, lambda k, t, c_ref: (k, t, 0))],
            out_specs=[pl.BlockSpec(blk, lambda k, t, c_ref: (k, t, 0))] * 2),
        out_shape=[jax.ShapeDtypeStruct((N_CHIPS, h, n), F32), jax.ShapeDtypeStruct((N_CHIPS, h, n), BF16)],
        compiler_params=_params("arbitrary", "arbitrary"))(_scalar(c), g32, from_sib)


def _add_chips(name, p32, from_chips, me_chip):
    _, h, n = p32.shape
    th = _row_tile(h)

    def body(m_ref, p_ref, a_ref, b_ref, c_ref, o_ref):
        o_ref[...] = p_ref[...] + a_ref[...].astype(F32) + b_ref[...].astype(F32) + c_ref[...].astype(F32)

    blk = (None, th, n)
    return pl.pallas_call(
        body, name=name,
        grid_spec=pltpu.PrefetchScalarGridSpec(
            num_scalar_prefetch=1, grid=(h // th,),
            in_specs=[pl.BlockSpec(blk, lambda t, m_ref: (m_ref[0], t, 0))]
            + [pl.BlockSpec(blk, lambda t, m_ref, j=j: (j, t, 0)) for j in range(3)],
            out_specs=pl.BlockSpec((th, n), lambda t, m_ref: (t, 0))),
        out_shape=jax.ShapeDtypeStruct((h, n), F32),
        compiler_params=_params("arbitrary"))(_scalar(me_chip), p32, from_chips, from_chips, from_chips)


def _adamw_vals(w, g, m, v):
    m = ADAM_B1 * m + (1.0 - ADAM_B1) * g
    v = ADAM_B2 * v + (1.0 - ADAM_B2) * (g * g)
    m_hat = m / (1.0 - ADAM_B1 ** ADAM_STEP)
    v_hat = v / (1.0 - ADAM_B2 ** ADAM_STEP)
    delta = -ADAM_LR * (m_hat / (jnp.sqrt(v_hat) + ADAM_EPS) + ADAM_WD * w)
    return delta, m, v


def _adamw_halves(name, w, m, v, g_mine, g_sib, c):
    r, n = w.shape
    h = r // 2
    th = _row_tile(h, 8)
    nt = h // th

    def body(c_ref, w_ref, m_ref, v_ref, a_ref, b_ref, g_ref, d_ref, nm_ref, nv_ref):
        mine = (pl.program_id(0) // nt) == c_ref[0]
        g = jnp.where(mine, a_ref[...], b_ref[...])
        d, nm, nv = _adamw_vals(w_ref[...], g, m_ref[...], v_ref[...])
        g_ref[...] = g
        d_ref[...] = d
        nm_ref[...] = nm
        nv_ref[...] = nv

    full = pl.BlockSpec((th, n), lambda t, c_ref: (t, 0))
    part = pl.BlockSpec((th, n), lambda t, c_ref: (t % nt, 0))
    return pl.pallas_call(
        body, name=name,
        grid_spec=pltpu.PrefetchScalarGridSpec(
            num_scalar_prefetch=1, grid=(2 * nt,), in_specs=[full, full, full, part, part], out_specs=[full] * 4),
        out_shape=[jax.ShapeDtypeStruct((r, n), F32)] * 4,
        compiler_params=_params("arbitrary"))(_scalar(c), w, m, v, g_mine, g_sib)


def _adamw(name, w, g, m, v):
    R, W = w.shape

    def fn(accs, tv, cv):
        return list(_adamw_vals(*tv))

    return _tile_call(name, fn, R, W, _pick(R, 256), W, tiles=[(w, 0), (g, 0), (m, 0), (v, 0)], outs=[F32] * 3)


SMALL_LAYOUT = (("rel_bias", 2, 256), ("lb_param", 8, 1024), ("norm_ffn1", 8, 1024), ("norm_mix", 8, 1024),
                ("attn_sinks", 1, 8), ("rec_norm", 1, 128), ("norm_ffn2", 8, 1024), ("norm_ple", 8, 1024),
                ("norm_final", 8, 1024), ("loss", 8, 1024))


def _pack_small(vals):
    rows = []
    for name, nrows, n in SMALL_LAYOUT:
        flat = vals[name].reshape(-1)
        flat = jnp.pad(flat, (0, nrows * 128 - n))
        rows.append(flat.reshape(nrows, 128))
    packed = jnp.concatenate(rows, axis=0)
    return jnp.pad(packed, ((0, SMALL_ROWS - packed.shape[0]), (0, 0)))


def _unpack_small(packed, shapes):
    out, r = {}, 0
    for name, nrows, n in SMALL_LAYOUT:
        out[name] = packed[r:r + nrows].reshape(-1)[:n].reshape(shapes[name])
        r += nrows
    return out


def _natural(name, s):
    if name in COL_SHARDED:
        return s.transpose(1, 0, 2).reshape(s.shape[1], -1)
    return s.reshape(-1, s.shape[2])


def kernel(x, p, rel_bias, lb_param, norm_ffn1, w_ffn1_in, w_ffn1_out, norm_mix, w_in, attn_sinks, rec_norm, w_att_proj, w_rec_proj, w_out, norm_ffn2, w_ffn2_in, w_ffn2_out, norm_ple, w_ple_gate, w_ple_proj, norm_final, loss_target, m_rel_bias, m_lb_param, m_norm_ffn1, m_w_ffn1_in, m_w_ffn1_out, m_norm_mix, m_w_in, m_attn_sinks, m_rec_norm, m_w_att_proj, m_w_rec_proj, m_w_out, m_norm_ffn2, m_w_ffn2_in, m_w_ffn2_out, m_norm_ple, m_w_ple_gate, m_w_ple_proj, m_norm_final, v_rel_bias, v_lb_param, v_norm_ffn1, v_w_ffn1_in, v_w_ffn1_out, v_norm_mix, v_w_in, v_attn_sinks, v_rec_norm, v_w_att_proj, v_w_rec_proj, v_w_out, v_norm_ffn2, v_w_ffn2_in, v_w_ffn2_out, v_norm_ple, v_w_ple_gate, v_w_ple_proj, v_norm_final):
    args = dict(locals())
    wsh = {n: args[n] for n in WEIGHTS}
    B, S = x.shape[0], x.shape[1]
    T = B * S
    cx, cy, cc = _place()
    me_chip = 2 * cx + cy

    mine16 = {n: wsh[n][0].astype(BF16) for n in BIG}
    loss_p, dx, grads, part, from_chips = _local_step(
        x.reshape(T, D_MODEL), p.reshape(T, PLE_DIM), loss_target.reshape(T, D_MODEL),
        {n: wsh[n] for n in SMALL}, mine16, cc, me_chip, B, S)

    s_mine = [_add_chips("rs_add_chips_" + n, part[n][0], from_chips[n], me_chip) for n in BIG]
    s_sib = _join_halves("rs_join", s_mine)

    small_vals = {
        "rel_bias": grads["rel_bias"].T,
        "lb_param": jnp.concatenate([_colsum("dlb_sum", grads["lb_param"]),
                                     -_colsum("dlb_sum2", grads["lb_param"])], axis=0) / 8.0,
        "attn_sinks": grads["attn_sinks"][:, 0],
        "rec_norm": _colsum("drn_sum", grads["rec_norm"]).reshape(REC_HEADS, REC_DIM).sum(axis=0),
        "loss": _colsum("loss_sum", loss_p),
    }
    for n in ("norm_ffn1", "norm_mix", "norm_ffn2", "norm_ple", "norm_final"):
        small_vals[n] = _colsum(n + "_sum", grads[n])
    red = _allreduce_small(_pack_small(small_vals))
    small_shapes = {n: wsh[n].shape for n in SMALL}
    small_shapes["loss"] = (D_MODEL,)
    small = _unpack_small(red, small_shapes)
    loss = 0.5 * jnp.sum(small["loss"]) / D_MODEL

    out_g, out_d, out_m, out_v = {}, {}, {}, {}
    for n, gm, gs in zip(BIG, s_mine, s_sib):
        res = _adamw_halves("adamw_" + n, wsh[n][0], args["m_" + n][0], args["v_" + n][0], gm, gs, cc)
        out_g[n], out_d[n], out_m[n], out_v[n] = (t[None] for t in res)
    sw = _pack_small({**{n: wsh[n] for n in SMALL}, "loss": jnp.zeros((D_MODEL,), F32)})
    sm = _pack_small({**{n: args["m_" + n] for n in SMALL}, "loss": jnp.zeros((D_MODEL,), F32)})
    sv = _pack_small({**{n: args["v_" + n] for n in SMALL}, "loss": jnp.ones((D_MODEL,), F32)})
    sd, snm, snv = _adamw("adamw_small", sw, red, sm, sv)
    ud, um, uv = (_unpack_small(t, small_shapes) for t in (sd, snm, snv))
    for n in SMALL:
        out_g[n], out_d[n], out_m[n], out_v[n] = small[n], ud[n], um[n], uv[n]

    return (loss, dx.reshape(B, S, D_MODEL), *[out_g[n] for n in WEIGHTS], *[out_d[n] for n in WEIGHTS],
            *[out_m[n] for n in WEIGHTS], *[out_v[n] for n in WEIGHTS])
```

```python
import numpy as np
import jax
import jax.numpy as jnp
from jax import lax
from jax.experimental import pallas as pl
from jax.experimental.pallas import tpu as pltpu

F32 = jnp.float32
BF16 = jnp.bfloat16
MESH = pl.DeviceIdType.MESH

D_MODEL = 1024
D_FF = 2816
FF_SHARD = 2 * D_FF // 4
HEAD_DIM = 64
N_Q_HEADS = 8
ATT_BLOCK = 128
N_BUCKETS = 32
MAX_DISTANCE = 128
REC_HEADS = 4
REC_DIM = 128
PLE_DIM = 256
EPS = 1e-6
IN_W = 4864
COL_AQ, COL_AK, COL_AV, COL_RQ, COL_RF, COL_RI, COL_RG, COL_GA, COL_GB = 0, 4, 5, 6, 10, 14, 18, 22, 30

CHUNK = 64
SUB = 8
N_SUB = CHUNK // SUB
HGRN_PAIR = 2

ADAM_LR, ADAM_B1, ADAM_B2, ADAM_EPS, ADAM_WD, ADAM_STEP = 0.001, 0.9, 0.999, 1e-08, 0.01, 10

V7X_VMEM_LIMIT = 56 * 1024 * 1024
N_CHIPS = 4
N_DEV = 8

BIG = ("w_ffn1_in", "w_ffn1_out", "w_in", "w_att_proj", "w_rec_proj", "w_out",
       "w_ffn2_in", "w_ffn2_out", "w_ple_gate", "w_ple_proj")
COL_SHARDED = ("w_ffn1_in", "w_in", "w_att_proj", "w_rec_proj", "w_ffn2_in", "w_ple_proj")
WEIGHTS = ("rel_bias", "lb_param", "norm_ffn1", "w_ffn1_in", "w_ffn1_out", "norm_mix", "w_in", "attn_sinks",
           "rec_norm", "w_att_proj", "w_rec_proj", "w_out", "norm_ffn2", "w_ffn2_in", "w_ffn2_out", "norm_ple",
           "w_ple_gate", "w_ple_proj", "norm_final")
SMALL = tuple(n for n in WEIGHTS if n not in BIG)
SMALL_ROWS = 64


def _params(*sem):
    return pltpu.CompilerParams(dimension_semantics=sem, vmem_limit_bytes=V7X_VMEM_LIMIT)


def _pick(n, cap, mult=8):
    if n <= cap:
        return n
    for t in range(cap - cap % mult, 0, -mult):
        if n % t == 0:
            return t
    raise ValueError((n, cap, mult))


def _dot(a, b):
    return jnp.dot(a, b, preferred_element_type=F32)


def _dot_nt(a, b):
    return lax.dot_general(a, b, (((1,), (1,)), ((), ())), preferred_element_type=F32)


def _dot_tn(a, b):
    return lax.dot_general(a, b, (((0,), (0,)), ((), ())), preferred_element_type=F32)


def _split3(x):
    hi = x.astype(BF16)
    r = x - hi.astype(F32)
    mid = r.astype(BF16)
    lo = (r - mid.astype(F32)).astype(BF16)
    return hi, mid, lo


def _split2(x):
    hi = x.astype(BF16)
    return hi, (x - hi.astype(F32)).astype(BF16)


def _sel_left(sel_bf16, x):
    hi, mid, lo = _split3(x)
    return _dot(sel_bf16, hi) + _dot(sel_bf16, mid) + _dot(sel_bf16, lo)


def _sel_right(x, sel_bf16):
    hi, mid, lo = _split3(x)
    return _dot(hi, sel_bf16) + _dot(mid, sel_bf16) + _dot(lo, sel_bf16)


def _sigmoid(x):
    return 0.5 * jnp.tanh(0.5 * x) + 0.5


def _group8(x):
    r, w = x.shape
    return x.reshape(r // 8, 8, w).sum(axis=0)


class _Comm:
    def __init__(self, ins, out_shapes, n_sems, start, finish):
        self.ins, self.out_shapes, self.n_sems, self.start, self.finish = ins, out_shapes, n_sems, start, finish


ANY = pl.BlockSpec(memory_space=pl.ANY)


def _comm_parts(comm):
    if comm is None:
        return [], [], [], []
    sems = [pltpu.SemaphoreType.DMA((comm.n_sems,)), pltpu.SemaphoreType.DMA((comm.n_sems,))]
    return list(comm.ins), [ANY] * len(comm.ins), list(comm.out_shapes), sems


def _comm_run(comm, grid, refs, n_in, n_out):
    if comm is None:
        return (lambda: None), (lambda: None)
    nci, nco = len(comm.ins), len(comm.out_shapes)
    cin = refs[n_in:n_in + nci]
    cout = refs[n_in + nci + n_out:n_in + nci + n_out + nco]
    send_sems, recv_sems = refs[-2], refs[-1]
    ids = [pl.program_id(d) for d in range(len(grid))]
    is_first = ids[0] == 0
    is_last = ids[0] == grid[0] - 1
    for d in range(1, len(grid)):
        is_first = is_first & (ids[d] == 0)
        is_last = is_last & (ids[d] == grid[d] - 1)

    def first():
        @pl.when(is_first)
        def _():
            comm.start(cin, cout, send_sems, recv_sems)

    def last():
        @pl.when(is_last)
        def _():
            comm.finish(cin, cout, send_sems, recv_sems)

    return first, last


def _call(name, fn, grid, ins, outs, pairs=(), comm=None, j_outer=False):
    in_pair = {i for p in pairs for i in p[:2]}
    n_in, n_out = len(ins), len(outs)
    c_arrays, c_in_specs, c_out_shapes, c_sems = _comm_parts(comm)

    def body(*refs):
        first, last = _comm_run(comm, grid, refs, n_in, n_out)
        first()
        accs = []
        for ia, ib, kind in pairs:
            a, b = refs[ia][...].astype(BF16), refs[ib][...].astype(BF16)
            accs.append(_dot(a, b) if kind == "nn" else _dot_nt(a, b))
        vals = [refs[i][...] for i in range(n_in) if i not in in_pair]
        res = fn(accs, vals)
        out_refs = refs[n_in + len(c_arrays):n_in + len(c_arrays) + n_out]
        assert len(res) == len(out_refs), (name, len(res), len(out_refs))
        for o_ref, val in zip(out_refs, res):
            o_ref[...] = val.astype(o_ref.dtype)
        last()

    if j_outer:
        grid = (grid[1], grid[0])
        swap = lambda im: (lambda j, i: im(i, j))
        ins = [(a, blk, swap(im)) for a, blk, im in ins]
        outs = [(shp, dt, blk, swap(im)) for shp, dt, blk, im in outs]

    return pl.pallas_call(
        body, name=name, grid=grid,
        in_specs=[pl.BlockSpec(blk, im) for _, blk, im in ins] + c_in_specs,
        out_specs=[pl.BlockSpec(blk, im) for _, _, blk, im in outs] + [ANY] * len(c_out_shapes),
        out_shape=[jax.ShapeDtypeStruct(shp, dt) for shp, dt, _, _ in outs] + c_out_shapes,
        scratch_shapes=c_sems,
        compiler_params=_params(*(["arbitrary"] * len(grid))))(*[a for a, _, _ in ins], *c_arrays)


def _tile_call(name, fn, M, N, tm, tn, *, pairs=(), tiles=(), consts=(), outs=(), parts=0, comm=None,
               j_outer=False):
    gi, gj = M // tm, N // tn
    assert gi * tm == M and gj * tn == N, (name, M, N, tm, tn)
    ins, prs = [], []
    for a, a_col, b, kind in pairs:
        K = b.shape[0] if kind == "nn" else b.shape[1]
        ins.append((a, (tm, K), lambda i, j, c=a_col: (i, c)))
        if kind == "nn":
            ins.append((b, (K, tn), lambda i, j: (0, j)))
        else:
            ins.append((b, (tn, K), lambda i, j: (j, 0)))
        prs.append((len(ins) - 2, len(ins) - 1, kind))
    for arr, off in tiles:
        ins.append((arr, (tm, tn), lambda i, j, o=off: (i, j + o)))
    for arr in consts:
        ins.append((arr, arr.shape, lambda i, j: (0, 0)))
    out_l = [((M, N), dt, (tm, tn), lambda i, j: (i, j)) for dt in outs]
    out_l += [((gi * 8, N), F32, (8, tn), lambda i, j: (i, j))] * parts
    nt = len(tiles)

    def wrapped(accs, vals):
        return fn(accs, vals[:nt], vals[nt:])

    return _call(name, wrapped, (gi, gj), ins, out_l, prs, comm=comm, j_outer=j_outer)


def _mm_tn(name, grid, a_in, b_in, outs):
    nk = grid[2]
    tm = [d for d in a_in[1] if d is not None][1]
    tn = [d for d in b_in[1] if d is not None][1]

    def body(a_ref, b_ref, *rest):
        out_refs, acc_ref = rest[:-1], rest[-1]
        k = pl.program_id(2)

        @pl.when(k == 0)
        def _():
            acc_ref[...] = jnp.zeros_like(acc_ref)

        acc_ref[...] += _dot_tn(a_ref[...].astype(BF16), b_ref[...].astype(BF16))

        @pl.when(k == nk - 1)
        def _():
            for o_ref in out_refs:
                o_ref[...] = acc_ref[...].astype(o_ref.dtype)

    return pl.pallas_call(
        body, name=name, grid=grid,
        in_specs=[pl.BlockSpec(a_in[1], a_in[2]), pl.BlockSpec(b_in[1], b_in[2])],
        out_specs=[pl.BlockSpec(blk, im) for _, _, blk, im in outs],
        out_shape=[jax.ShapeDtypeStruct(shp, dt) for shp, dt, _, _ in outs],
        scratch_shapes=[pltpu.VMEM((tm, tn), F32)],
        compiler_params=_params("arbitrary", "arbitrary", "arbitrary"))(a_in[0], b_in[0])


def _grad_pair(shape, block, imap):
    return [(shape, F32, block, imap), (shape, BF16, block, imap)]


def _mm_tn_rows(name, a, b, tk=2048):
    T, a_w = a.shape
    b_w = b.shape[1]
    tm = _pick(a_w, 1408, 128)
    tk = _pick(T, tk, 128)
    g32, g16 = _mm_tn(name, (a_w // tm, 1, T // tk),
                      (a, (tk, tm), lambda i, j, k: (k, i)), (b, (tk, b_w), lambda i, j, k: (k, 0)),
                      _grad_pair((a_w, b_w), (tm, b_w), lambda i, j, k: (i, 0)))
    shp = (N_CHIPS, a_w // N_CHIPS, b_w)
    return g32.reshape(shp), g16.reshape(shp)


def _mm_tn_cols(name, a, b, tk=2048):
    T, a_w = a.shape
    n = b.shape[1] // N_CHIPS
    tk = _pick(T, tk, 128)
    return _mm_tn(name, (1, N_CHIPS, T // tk),
                  (a, (tk, a_w), lambda i, j, k: (k, 0)), (b, (tk, n), lambda i, j, k: (k, j)),
                  _grad_pair((N_CHIPS, a_w, n), (None, a_w, n), lambda i, j, k: (j, 0, 0)))


def _colsum(name, x):
    def body(x_ref, o_ref):
        o_ref[...] = jnp.sum(x_ref[...], axis=0, keepdims=True)
    return pl.pallas_call(body, name=name, out_shape=jax.ShapeDtypeStruct((1, x.shape[1]), F32))(x)


def _rms_hat(h):
    return h * lax.rsqrt(jnp.mean(h * h, axis=-1, keepdims=True) + EPS)


def _rms_bwd_vals(dn, h, g):
    r = lax.rsqrt(jnp.mean(h * h, axis=-1, keepdims=True) + EPS)
    nh = h * r
    gd = dn * g
    dh = r * (gd - nh * jnp.mean(gd * nh, axis=-1, keepdims=True))
    return dh, _group8(dn * nh)


def _rms_fwd(name, h, g, tm=512, comm=None):
    T = h.shape[0]

    def fn(accs, tv, cv):
        return [_rms_hat(tv[0]) * cv[0]]

    return _tile_call(name, fn, T, D_MODEL, _pick(T, tm), D_MODEL, tiles=[(h, 0)], consts=[g], outs=[BF16],
                      comm=comm)


def _ffn_fwd(tag, h, g, w_in, w_out, g_next, n=None, comm_norm=None, w_in_of=None, comm_in=None, comm_out=None,
             w_out_of=None):
    T = h.shape[0]
    if n is None:
        n, *got_norm = _rms_fwd(tag + "_norm", h, g, comm=comm_norm)
        if w_in_of is not None:
            w_in = w_in_of(got_norm)
    tm = _pick(T, 1024)
    wblk = (None, D_MODEL, FF_SHARD)

    def act(accs, vals):
        gate, up = accs
        return [gate, up, gate * _sigmoid(gate) * up]

    tile = lambda: ((T, D_FF), BF16, (tm, FF_SHARD), lambda i, j: (i, j))
    gate, up, a, *got_in = _call(
        tag + "_in", act, (T // tm, 2),
        [(n, (tm, D_MODEL), lambda i, j: (i, 0)),
         (w_in, wblk, lambda i, j: (j, 0, 0)), (w_in, wblk, lambda i, j: (j + 2, 0, 0))],
        [tile(), tile(), tile()], pairs=[(0, 1, "nn"), (0, 2, "nn")], comm=comm_in, j_outer=True)

    def res(accs, tv, cv):
        h_new = tv[0] + 0.5 * accs[0]
        return [h_new, _rms_hat(h_new) * cv[0]]

    if w_out_of is not None:
        w_out = w_out_of(got_in)
    h_new, n_next, *got_out = _tile_call(
        tag + "_out", res, T, D_MODEL, _pick(T, 512), D_MODEL, pairs=[(a, 0, w_out, "nn")], tiles=[(h, 0)],
        consts=[g_next], outs=[F32, BF16], comm=comm_out)
    return h_new, n_next, (n, gate, up, a), got_out


def _ffn_bwd(tag, dh_out, df, h, g, w_in, w_out, saved, comm=None, comm_last=None):
    T = h.shape[0]
    n, gate, up, a = saved
    tm = _pick(T, 512)

    def dact(accs, vals):
        da = accs[0]
        gt, u = vals[0].astype(F32), vals[1].astype(F32)
        sg = _sigmoid(gt)
        silu = gt * sg
        return [jnp.stack([(da * u * (sg + silu * (1.0 - sg))).astype(BF16), (da * silu).astype(BF16)])]

    dz, *got = _call(
        tag + "_dact", dact, (T // tm, 2),
        [(df, (tm, D_MODEL), lambda i, j: (i, 0)), (w_out, (FF_SHARD, D_MODEL), lambda i, j: (j, 0)),
         (gate, (tm, FF_SHARD), lambda i, j: (i, j)), (up, (tm, FF_SHARD), lambda i, j: (i, j))],
        [((2, T, D_FF), BF16, (2, tm, FF_SHARD), lambda i, j: (0, i, j))], pairs=[(0, 1, "nt")], comm=comm,
        j_outer=True)
    dw_out = _mm_tn_rows(tag + "_dwout", a, df)
    tk = _pick(T, 2048, 128)
    dw_in = _mm_tn(tag + "_dwin", (1, N_CHIPS, T // tk),
                   (n, (tk, D_MODEL), lambda i, j, k: (k, 0)),
                   (dz, (None, tk, FF_SHARD), lambda i, j, k: (j // 2, k, j % 2)),
                   _grad_pair((N_CHIPS, D_MODEL, FF_SHARD), (None, D_MODEL, FF_SHARD), lambda i, j, k: (j, 0, 0)))

    def dnorm(accs, vals):
        dn = accs[0] + accs[1] + accs[2] + accs[3]
        dh, dg = _rms_bwd_vals(dn, vals[0], vals[2])
        dh = vals[1] + dh
        return [dh, dh, dg]

    tm2 = _pick(T, 512)
    ins = [(dz, (None, tm2, FF_SHARD), lambda i, j, s=s: (s // 2, i, s % 2)) for s in range(N_CHIPS)]
    ins += [(w_in, (None, D_MODEL, FF_SHARD), lambda i, j, s=s: (s, 0, 0)) for s in range(N_CHIPS)]
    ins += [(h, (tm2, D_MODEL), lambda i, j: (i, 0)), (dh_out, (tm2, D_MODEL), lambda i, j: (i, 0)),
            (g, g.shape, lambda i, j: (0, 0))]
    dh, dh16, dg, *got_last = _call(
        tag + "_dnorm", dnorm, (T // tm2, 1), ins,
        [((T, D_MODEL), F32, (tm2, D_MODEL), lambda i, j: (i, 0)),
         ((T, D_MODEL), BF16, (tm2, D_MODEL), lambda i, j: (i, 0)),
         ((T // tm2 * 8, D_MODEL), F32, (8, D_MODEL), lambda i, j: (i, 0))],
        pairs=[(s, N_CHIPS + s, "nt") for s in range(N_CHIPS)],
        comm=None if comm_last is None else comm_last(dw_in, dw_out))
    return dh, dh16, dg, dw_in, dw_out, got, got_last


def _t5_onehot():
    qi = np.arange(ATT_BLOCK)[:, None] + ATT_BLOCK
    kj = np.arange(2 * ATT_BLOCK)[None, :]
    nn = np.maximum(qi - kj, 0)
    max_exact = N_BUCKETS // 2
    large = max_exact + (np.log(np.maximum(nn, 1) / max_exact) / np.log(MAX_DISTANCE / max_exact)
                         * (N_BUCKETS - max_exact)).astype(np.int32)
    large = np.minimum(large, N_BUCKETS - 1)
    bucket = np.where(nn < max_exact, nn, large).astype(np.int32).reshape(-1)
    return (bucket[None, :] == np.arange(N_BUCKETS)[:, None]).astype(np.float32)


def _small_mm(name, a, b, sel):
    def body(a_ref, b_ref, o_ref):
        if sel == "right":
            o_ref[...] = _sel_right(a_ref[...], b_ref[...])
        else:
            o_ref[...] = _sel_left(a_ref[...], b_ref[...])
    return pl.pallas_call(body, name=name, out_shape=jax.ShapeDtypeStruct((a.shape[0], b.shape[1]), F32),
                          compiler_params=pltpu.CompilerParams(vmem_limit_bytes=V7X_VMEM_LIMIT))(a, b)


def _dup_heads(t):
    a, b = t[:, :HEAD_DIM], t[:, HEAD_DIM:]
    return jnp.concatenate([a, a, b, b], axis=1)


def _kv_layouts(proj):
    T = proj.shape[0]

    def fn(accs, tv, cv):
        return [tv[0], tv[1]]

    k, v = _tile_call("kv_cast", fn, T, 128, _pick(T, 1024), 128, tiles=[(proj, COL_AK), (proj, COL_AV)],
                      outs=[BF16, BF16])
    return _dup_heads(k), _dup_heads(v)


def _swa_masks():
    row = lax.broadcasted_iota(jnp.int32, (ATT_BLOCK, 2 * ATT_BLOCK), 0)
    col = lax.broadcasted_iota(jnp.int32, (ATT_BLOCK, 2 * ATT_BLOCK), 1)
    dist = ATT_BLOCK + row - col
    return (dist >= 0) & (dist < ATT_BLOCK), col


GROUP = 4


def _stack_group(blk, lo_q):
    zero = jnp.zeros_like(blk[:, :128])
    rows = []
    for pair in range(GROUP // 2):
        pb = blk[:, 128 * pair:128 * (pair + 1)]
        rows += [jnp.where(lo_q, pb, zero), jnp.where(lo_q, zero, pb)]
    return jnp.concatenate(rows, axis=0)


def _unstack_group(st, lo_q):
    pairs = [jnp.where(lo_q, st[256 * pair:256 * pair + 128], st[256 * pair + 128:256 * (pair + 1)])
             for pair in range(GROUP // 2)]
    return jnp.concatenate(pairs, axis=1)


def _swa_probs(s, bias_h, sink, valid):
    s = jnp.where(valid, s * (HEAD_DIM ** -0.5) + bias_h, -jnp.inf)
    m = jnp.maximum(jnp.max(s, axis=-1, keepdims=True), sink)
    e = jnp.exp(s - m)
    es = jnp.exp(sink - m)
    den = jnp.sum(e, axis=-1, keepdims=True) + es
    return e / den, es / den


def _swa_fwd(proj, kk2, vv2, bias, sinks, B, S):
    T = B * S
    nb = S // ATT_BLOCK

    def body(q_ref, k_ref, v_ref, bias_ref, sink_ref, o_ref, kpad, vpad):
        zeros = jnp.zeros((ATT_BLOCK, 256), BF16)
        kpad[pl.ds(0, ATT_BLOCK), :] = zeros
        vpad[pl.ds(0, ATT_BLOCK), :] = zeros
        kpad[pl.ds(ATT_BLOCK, S), :] = k_ref[...]
        vpad[pl.ds(ATT_BLOCK, S), :] = v_ref[...]
        valid0, col = _swa_masks()
        lo_q = lax.broadcasted_iota(jnp.int32, (1, 128), 1) < HEAD_DIM

        def blk(n, carry):
            r0 = pl.multiple_of(n * ATT_BLOCK, ATT_BLOCK)
            rows = pl.ds(r0, ATT_BLOCK)
            valid = valid0 & ((n > 0) | (col >= ATT_BLOCK))
            for g in range(N_Q_HEADS // GROUP):
                lanes = pl.ds(128 * g, 128)
                kg = kpad[pl.ds(r0, 2 * ATT_BLOCK), lanes]
                vg = vpad[pl.ds(r0, 2 * ATT_BLOCK), lanes]
                qm = _stack_group(q_ref[rows, pl.ds(256 * g, 256)].astype(BF16), lo_q)
                s = _dot_nt(qm, kg)
                ps = []
                for i in range(GROUP):
                    h = GROUP * g + i
                    p, _ = _swa_probs(s[ATT_BLOCK * i:ATT_BLOCK * (i + 1)], bias_ref[h], sink_ref[h], valid)
                    ps.append(p.astype(BF16))
                o = _dot(jnp.concatenate(ps, axis=0), vg)
                o_ref[rows, pl.ds(256 * g, 256)] = _unstack_group(o, lo_q).astype(o_ref.dtype)
            return carry

        lax.fori_loop(0, nb, blk, 0)

    return pl.pallas_call(
        body, name="swa_fwd", grid=(B,),
        in_specs=[pl.BlockSpec((S, 512), lambda b: (b, 0)),
                  pl.BlockSpec((S, 256), lambda b: (b, 0)),
                  pl.BlockSpec((S, 256), lambda b: (b, 0)),
                  pl.BlockSpec((N_Q_HEADS, ATT_BLOCK, 2 * ATT_BLOCK), lambda b: (0, 0, 0)),
                  pl.BlockSpec(memory_space=pltpu.SMEM)],
        out_specs=pl.BlockSpec((S, 512), lambda b: (b, 0)),
        out_shape=jax.ShapeDtypeStruct((T, 512), BF16),
        scratch_shapes=[pltpu.VMEM((S + ATT_BLOCK, 256), BF16), pltpu.VMEM((S + ATT_BLOCK, 256), BF16)],
        compiler_params=_params("arbitrary"))(proj, kk2, vv2, bias, sinks)


def _swa_bwd(proj, kk2, vv2, bias, sinks, datt, B, S):
    T = B * S
    nb = S // ATT_BLOCK

    def body(q_ref, k_ref, v_ref, bias_ref, sink_ref, do_ref, dq_ref, dk_ref, dv_ref, dbias_ref, dsink_ref,
             kpad, vpad, dkpad, dvpad):
        b = pl.program_id(0)

        @pl.when(b == 0)
        def _():
            dbias_ref[...] = jnp.zeros_like(dbias_ref)
            dsink_ref[...] = jnp.zeros_like(dsink_ref)

        zeros = jnp.zeros((ATT_BLOCK, 256), BF16)
        kpad[pl.ds(0, ATT_BLOCK), :] = zeros
        vpad[pl.ds(0, ATT_BLOCK), :] = zeros
        kpad[pl.ds(ATT_BLOCK, S), :] = k_ref[...]
        vpad[pl.ds(ATT_BLOCK, S), :] = v_ref[...]
        dkpad[...] = jnp.zeros_like(dkpad)
        dvpad[...] = jnp.zeros_like(dvpad)
        valid0, col = _swa_masks()
        lo_q = lax.broadcasted_iota(jnp.int32, (1, 128), 1) < HEAD_DIM
        scale = HEAD_DIM ** -0.5

        def blk(n, carry):
            r0 = pl.multiple_of(n * ATT_BLOCK, ATT_BLOCK)
            rows = pl.ds(r0, ATT_BLOCK)
            band = pl.ds(r0, 2 * ATT_BLOCK)
            valid = valid0 & ((n > 0) | (col >= ATT_BLOCK))
            for g in range(N_Q_HEADS // GROUP):
                lanes = pl.ds(128 * g, 128)
                kg = kpad[band, lanes]
                vg = vpad[band, lanes]
                qm = _stack_group(q_ref[rows, pl.ds(256 * g, 256)].astype(BF16), lo_q)
                dom = _stack_group(do_ref[rows, pl.ds(256 * g, 256)], lo_q)
                s = _dot_nt(qm, kg)
                dp = _dot_nt(dom, vg)
                pst, dst = [], []
                for i in range(GROUP):
                    h = GROUP * g + i
                    sl = slice(ATT_BLOCK * i, ATT_BLOCK * (i + 1))
                    p, ps = _swa_probs(s[sl], bias_ref[h], sink_ref[h], valid)
                    delta = jnp.sum(p * dp[sl], axis=-1, keepdims=True)
                    ds = p * (dp[sl] - delta)
                    dbias_ref[h] += ds
                    dsink_ref[pl.ds(h, 1), :] += -jnp.sum(jnp.broadcast_to(ps * delta, (ATT_BLOCK, 128)),
                                                          axis=0, keepdims=True)
                    pst.append(p.astype(BF16))
                    dst.append((ds * scale).astype(BF16))
                pst, dst = jnp.concatenate(pst, axis=0), jnp.concatenate(dst, axis=0)
                dq_ref[rows, pl.ds(256 * g, 256)] = _unstack_group(_dot(dst, kg), lo_q).astype(dq_ref.dtype)
                dkpad[band, lanes] += _dot_tn(dst, qm)
                dvpad[band, lanes] += _dot_tn(pst, dom)
            return carry

        lax.fori_loop(0, nb, blk, 0)
        lo_out = lax.broadcasted_iota(jnp.int32, (1, 128), 1) < HEAD_DIM

        def fold(pad_ref):
            halves = []
            for g in range(N_Q_HEADS // GROUP):
                t = pad_ref[pl.ds(ATT_BLOCK, S), pl.ds(128 * g, 128)]
                halves.append(t + pltpu.roll(t, HEAD_DIM, 1))
            return jnp.where(lo_out, halves[0], halves[1])

        dk_ref[...] = fold(dkpad).astype(dk_ref.dtype)
        dv_ref[...] = fold(dvpad).astype(dv_ref.dtype)

    return pl.pallas_call(
        body, name="swa_bwd", grid=(B,),
        in_specs=[pl.BlockSpec((S, 512), lambda b: (b, 0)),
                  pl.BlockSpec((S, 256), lambda b: (b, 0)),
                  pl.BlockSpec((S, 256), lambda b: (b, 0)),
                  pl.BlockSpec((N_Q_HEADS, ATT_BLOCK, 2 * ATT_BLOCK), lambda b: (0, 0, 0)),
                  pl.BlockSpec(memory_space=pltpu.SMEM),
                  pl.BlockSpec((S, 512), lambda b: (b, 0))],
        out_specs=[pl.BlockSpec((S, 512), lambda b: (b, 0)),
                   pl.BlockSpec((S, 128), lambda b: (b, 0)),
                   pl.BlockSpec((S, 128), lambda b: (b, 0)),
                   pl.BlockSpec((N_Q_HEADS, ATT_BLOCK, 2 * ATT_BLOCK), lambda b: (0, 0, 0)),
                   pl.BlockSpec((N_Q_HEADS, 128), lambda b: (0, 0))],
        out_shape=[jax.ShapeDtypeStruct((T, 512), BF16),
                   jax.ShapeDtypeStruct((T, 128), BF16),
                   jax.ShapeDtypeStruct((T, 128), BF16),
                   jax.ShapeDtypeStruct((N_Q_HEADS, ATT_BLOCK, 2 * ATT_BLOCK), F32),
                   jax.ShapeDtypeStruct((N_Q_HEADS, 128), F32)],
        scratch_shapes=[pltpu.VMEM((S + ATT_BLOCK, 256), BF16), pltpu.VMEM((S + ATT_BLOCK, 256), BF16),
                        pltpu.VMEM((S + ATT_BLOCK, 256), F32), pltpu.VMEM((S + ATT_BLOCK, 256), F32)],
        compiler_params=_params("arbitrary"))(proj, kk2, vv2, bias, sinks, datt)


def _hgrn_gates(z, lb):
    sg = _sigmoid(z)
    f = lb + (1.0 - lb) * sg
    return sg, f, jnp.log(f), 1.0 - f


def _hgrn_consts():
    r = lax.broadcasted_iota(jnp.int32, (CHUNK, CHUNK), 0)
    c = lax.broadcasted_iota(jnp.int32, (CHUNK, CHUNK), 1)
    tril = (r >= c).astype(BF16)
    triu = (r <= c).astype(BF16)
    causal = r >= c
    below = (r // SUB) > (c // SUB)
    inside = ((r // SUB) == (c // SUB)) & causal
    return tril, triu, causal, below, inside, c


def _block_rows(ref, lanes, s):
    rows = []
    for i in range(N_SUB):
        if SUB * i + s < 0:
            rows.append(jnp.zeros((SUB, REC_DIM), F32))
        else:
            rows.append(jnp.broadcast_to(ref[pl.ds(SUB * i + s, 1), lanes], (SUB, REC_DIM)))
    return jnp.concatenate(rows, axis=0)


def _hgrn_offdiag(q, k, bcum, b_ref, lanes):
    eq = jnp.exp(jnp.minimum(bcum - _block_rows(b_ref, lanes, -1), 0.0))
    qe = q * eq
    zero = jnp.zeros((SUB, REC_DIM), F32)
    q_rows, k_cols, eks = [jnp.zeros((SUB, (N_SUB - 1) * REC_DIM), F32)], [], []
    for i in range(1, N_SUB):
        q_rows.append(jnp.concatenate([zero] * (i - 1) + [qe[SUB * i:SUB * (i + 1), :]] + [zero] * (N_SUB - 1 - i),
                                      axis=1))
        p = b_ref[pl.ds(SUB * i - 1, 1), lanes]
        pad = jnp.zeros((CHUNK - SUB * i, REC_DIM), F32)
        ek = jnp.concatenate([jnp.exp(p - b_ref[pl.ds(0, SUB * i), lanes]), pad], axis=0)
        k_cols.append(k * ek)
        eks.append(ek)
    return jnp.concatenate(q_rows, axis=0), jnp.concatenate(k_cols, axis=1), eq, eks


def _hgrn_fwd(proj, lb_param, B, S):
    T = B * S
    nc = S // CHUNK

    def body(q_ref, z_ref, v_ref, lb_ref, o_ref, st_ref, k_slots, b_slots):
        tril, _, _, below, inside, col = _hgrn_consts()
        col_s = col & (SUB - 1)

        def chunk(ci, hts, slot):
            k_s, b_s = k_slots.at[slot], b_slots.at[slot]
            r0 = pl.multiple_of(ci * CHUNK, CHUNK)
            lb = _sigmoid(lb_ref[0:1, :] - lb_ref[1:2, :])
            _, _, g_all, k_all = _hgrn_gates(z_ref[pl.ds(r0, CHUNK), :], lb)
            b_all = _sel_left(tril, g_all)
            k_s[...] = k_all
            b_s[...] = b_all
            new = []
            for e, ht in enumerate(hts):
                lanes = pl.ds(REC_DIM * e, REC_DIM)
                cols = slice(REC_DIM * e, REC_DIM * (e + 1))
                q = q_ref[pl.ds(r0, CHUNK), lanes]
                v = v_ref[pl.ds(r0, CHUNK), lanes]
                k, bcum = k_all[:, cols], b_all[:, cols]
                st_ref[e * nc + ci] = ht
                qst, kst, _, _ = _hgrn_offdiag(q, k, bcum, b_s, lanes)
                d = jnp.zeros((CHUNK, CHUNK), F32)
                for s in range(SUB):
                    w = jnp.exp(jnp.minimum(bcum - _block_rows(b_s, lanes, s), 0.0))
                    colv = jnp.sum(q * _block_rows(k_s, lanes, s) * w, axis=-1, keepdims=True)
                    d = jnp.where(col_s == s, colv, d)
                a = jnp.where(below, _dot_nt(qst.astype(BF16), kst.astype(BF16)), 0.0) + jnp.where(inside, d, 0.0)
                vb = v.astype(BF16)
                qb = (q * jnp.exp(bcum)).astype(BF16)
                o_ref[pl.ds(r0, CHUNK), lanes] = _dot(a.astype(BF16), vb) + _dot_nt(qb, ht.astype(BF16))
                b_last = b_s[pl.ds(CHUNK - 1, 1), lanes]
                kb = (k * jnp.exp(b_last - bcum)).astype(BF16)
                new.append(ht * jnp.exp(b_last) + _dot_tn(vb, kb))
            return tuple(new)

        lax.fori_loop(0, nc // 2, lambda i, hts: chunk(2 * i + 1, chunk(2 * i, hts, 0), 1),
                      tuple(jnp.zeros((REC_DIM, REC_DIM), F32) for _ in range(HGRN_PAIR)))

    hp, wd = REC_HEADS // HGRN_PAIR, HGRN_PAIR * REC_DIM
    cq, cf, ci_ = (c * REC_DIM // wd for c in (COL_RQ, COL_RF, COL_RI))
    return pl.pallas_call(
        body, name="hgrn_fwd", grid=(B, hp),
        in_specs=[pl.BlockSpec((S, wd), lambda b, h: (b, cq + h)),
                  pl.BlockSpec((S, wd), lambda b, h: (b, cf + h)),
                  pl.BlockSpec((S, wd), lambda b, h: (b, ci_ + h)),
                  pl.BlockSpec((2, wd), lambda b, h: (0, h))],
        out_specs=[pl.BlockSpec((S, wd), lambda b, h: (b, h)),
                   pl.BlockSpec((HGRN_PAIR * nc, REC_DIM, REC_DIM), lambda b, h: (b * hp + h, 0, 0))],
        out_shape=[jax.ShapeDtypeStruct((T, 512), F32),
                   jax.ShapeDtypeStruct((B * REC_HEADS * nc, REC_DIM, REC_DIM), F32)],
        scratch_shapes=[pltpu.VMEM((2, CHUNK, wd), F32), pltpu.VMEM((2, CHUNK, wd), F32)],
        compiler_params=_params("arbitrary", "arbitrary"))(proj, proj, proj, lb_param)


def _hgrn_bwd(proj, lb_param, states, do, B, S, comm=None):
    T = B * S
    nc = S // CHUNK

    c_arrays, c_in_specs, c_out_shapes, c_sems = _comm_parts(comm)
    nci, nco = len(c_arrays), len(c_out_shapes)

    def body(*refs):
        q_ref, z_ref, v_ref, lb_ref, st_ref, do_ref = refs[:6]
        dq_ref, dz_ref, dv_ref, dlb_ref = refs[6 + nci:10 + nci]
        slots = refs[10 + nci + nco:14 + nci + nco]
        comm_first, comm_last = _comm_run(comm, (B, REC_HEADS // HGRN_PAIR), refs, 6, 4)
        comm_first()
        tril, triu, causal, below, inside, col = _hgrn_consts()
        col_s = col & (SUB - 1)
        last_row = lax.broadcasted_iota(jnp.int32, (CHUNK, 1), 0) == CHUNK - 1
        rc = lax.broadcasted_iota(jnp.int32, (CHUNK, SUB * REC_DIM), 0)
        lc = lax.broadcasted_iota(jnp.int32, (CHUNK, SUB * REC_DIM), 1)
        spread = ((rc & (SUB - 1)) == (lc // REC_DIM)).astype(BF16)
        rr = lax.broadcasted_iota(jnp.int32, (CHUNK, SUB * CHUNK), 0)
        cc = lax.broadcasted_iota(jnp.int32, (CHUNK, SUB * CHUNK), 1)
        gather = (((rr // SUB) == ((cc & (CHUNK - 1)) // SUB)) & ((rr & (SUB - 1)) == (cc // CHUNK))).astype(BF16)

        heads = range(HGRN_PAIR)
        cols = [slice(REC_DIM * e, REC_DIM * (e + 1)) for e in heads]
        lanes = [pl.ds(REC_DIM * e, REC_DIM) for e in heads]
        lane_cat = lambda vals: jnp.concatenate(vals, axis=1)
        row_cat = lambda vals: jnp.concatenate(vals, axis=0)

        def chunk(it, carry, slot):
            k_s, b_s, pc_hi, pc_lo = (r.at[slot] for r in slots)
            dhts, dlb = carry
            ci = nc - 1 - it
            r0 = pl.multiple_of(ci * CHUNK, CHUNK)
            rows = pl.ds(r0, CHUNK)
            lb = _sigmoid(lb_ref[0:1, :] - lb_ref[1:2, :])
            sg, f, g_all, k_all = _hgrn_gates(z_ref[rows, :], lb)
            b_all = _sel_left(tril, g_all)
            k_s[...] = k_all
            b_s[...] = b_all
            q_all = q_ref[rows, :]
            das, hd = [], []
            for e in heads:
                vb, dob = v_ref[rows, lanes[e]].astype(BF16), do_ref[rows, lanes[e]].astype(BF16)
                da = jnp.where(causal, _dot_nt(dob, vb), 0.0)
                das.append(jnp.where(inside, da, 0.0))
                hd.append((vb, dob, da))
            da_hi, da_lo = _split2(row_cat(das))
            da_in = _dot(da_hi, spread) + _dot(da_lo, spread)
            ds, dqs = [], []
            for e in heads:
                q, bcum = q_all[:, cols[e]], b_all[:, cols[e]]
                d = jnp.zeros((CHUNK, CHUNK), F32)
                dq = jnp.zeros((CHUNK, REC_DIM), F32)
                for s in range(SUB):
                    w = jnp.exp(jnp.minimum(bcum - _block_rows(b_s, lanes[e], s), 0.0))
                    ks = _block_rows(k_s, lanes[e], s)
                    qw = q * w
                    d = jnp.where(col_s == s, jnp.sum(qw * ks, axis=-1, keepdims=True), d)
                    da_s = da_in[CHUNK * e:CHUNK * (e + 1), REC_DIM * s:REC_DIM * (s + 1)]
                    dq = dq + da_s * ks * w
                    hi, lo = _split2(da_s * qw)
                    pc_hi[pl.ds(CHUNK * s, CHUNK), lanes[e]] = hi
                    pc_lo[pl.ds(CHUNK * s, CHUNK), lanes[e]] = lo
                ds.append(d)
                dqs.append(dq)
            dk_in = _dot(gather, pc_hi[...]) + _dot(gather, pc_lo[...])
            dq_out, dk_out, dv_out, db_out, new_dhts = [], [], [], [], []
            for e in heads:
                q, k, bcum = q_all[:, cols[e]], k_all[:, cols[e]], b_all[:, cols[e]]
                vb, dob, da = hd[e]
                dht, ht = dhts[e], st_ref[e * nc + ci]
                qst, kst, eq, eks = _hgrn_offdiag(q, k, bcum, b_s, lanes[e])
                qst_b, kst_b = qst.astype(BF16), kst.astype(BF16)
                a = jnp.where(below, _dot_nt(qst_b, kst_b), 0.0) + jnp.where(inside, ds[e], 0.0)
                da_off = jnp.where(below, da, 0.0).astype(BF16)
                dqst = _dot(da_off, kst_b)
                dkst = _dot_tn(da_off, qst_b)
                dk = dk_in[:, cols[e]]
                dq_rows = [jnp.zeros((SUB, REC_DIM), F32)]
                for i in range(1, N_SUB):
                    dq_rows.append(dqst[SUB * i:SUB * (i + 1), REC_DIM * (i - 1):REC_DIM * i])
                    dk = dk + dkst[:, REC_DIM * (i - 1):REC_DIM * i] * eks[i - 1]
                dq = dqs[e] + row_cat(dq_rows) * eq
                eb = jnp.exp(bcum)
                b_last = b_s[pl.ds(CHUNK - 1, 1), lanes[e]]
                el = jnp.exp(b_last)
                ekb = jnp.exp(b_last - bcum)
                qb = (q * eb).astype(BF16)
                kb = k * ekb
                dhb = dht.astype(BF16)
                dv_out.append(_dot_tn(a.astype(BF16), dob) + _dot_nt(kb.astype(BF16), dhb))
                dqb = _dot(dob, ht.astype(BF16))
                dkb = _dot(vb, dhb)
                new_dhts.append(dht * el + _dot_tn(dob, qb))
                dq = dq + eb * dqb
                dk = dk + ekb * dkb
                edge = jnp.sum(kb * dkb, axis=0, keepdims=True) + el * jnp.sum(ht * dht, axis=0, keepdims=True)
                db_out.append(q * dq - k * dk + jnp.where(last_row, edge, 0.0))
                dq_out.append(dq)
                dk_out.append(dk)
            dk_all = lane_cat(dk_out)
            db_hi, db_lo = _split2(lane_cat(db_out))
            dg = _dot(triu, db_hi) + _dot(triu, db_lo)
            df = dg / f - dk_all
            dz_ref[rows, :] = (df * (1.0 - lb) * sg * (1.0 - sg)).astype(dz_ref.dtype)
            dq_ref[rows, :] = lane_cat(dq_out).astype(dq_ref.dtype)
            dv_ref[rows, :] = lane_cat(dv_out).astype(dv_ref.dtype)
            return tuple(new_dhts), dlb + jnp.sum(df * (1.0 - sg), axis=0, keepdims=True)

        zero = (tuple(jnp.zeros((REC_DIM, REC_DIM), F32) for _ in heads), jnp.zeros((1, HGRN_PAIR * REC_DIM), F32))
        _, dlb = lax.fori_loop(0, nc // 2, lambda i, c: chunk(2 * i + 1, chunk(2 * i, c, 0), 1), zero)
        lb = _sigmoid(lb_ref[0:1, :] - lb_ref[1:2, :])
        dlb_ref[...] = jnp.broadcast_to(dlb * lb * (1.0 - lb), (8, HGRN_PAIR * REC_DIM))
        comm_last()

    hp, wd = REC_HEADS // HGRN_PAIR, HGRN_PAIR * REC_DIM
    cq, cf, ci_ = (c * REC_DIM // wd for c in (COL_RQ, COL_RF, COL_RI))
    return pl.pallas_call(
        body, name="hgrn_bwd", grid=(B, hp),
        in_specs=[pl.BlockSpec((S, wd), lambda b, h: (b, cq + h)),
                  pl.BlockSpec((S, wd), lambda b, h: (b, cf + h)),
                  pl.BlockSpec((S, wd), lambda b, h: (b, ci_ + h)),
                  pl.BlockSpec((2, wd), lambda b, h: (0, h)),
                  pl.BlockSpec((HGRN_PAIR * nc, REC_DIM, REC_DIM), lambda b, h: (b * hp + h, 0, 0)),
                  pl.BlockSpec((S, wd), lambda b, h: (b, h))] + c_in_specs,
        out_specs=[pl.BlockSpec((S, wd), lambda b, h: (b, h))] * 3
        + [pl.BlockSpec((8, wd), lambda b, h: (b, h))] + [ANY] * nco,
        out_shape=[jax.ShapeDtypeStruct((T, 512), BF16)] * 3 + [jax.ShapeDtypeStruct((B * 8, 512), F32)]
        + c_out_shapes,
        scratch_shapes=[pltpu.VMEM((2, CHUNK, wd), F32)] * 2 + [pltpu.VMEM((2, SUB * CHUNK, wd), BF16)] * 2 + c_sems,
        compiler_params=_params("arbitrary", "arbitrary"))(proj, proj, proj, lb_param, states, do, *c_arrays)


def _rec_gate_fwd(rec, proj, rec_norm):
    T = rec.shape[0]

    def fn(accs, tv, cv):
        return [_rms_hat(tv[0]) * cv[0] * _sigmoid(tv[1])]

    return _tile_call("rec_gate", fn, T, 512, _pick(T, 1024), REC_DIM, tiles=[(rec, 0), (proj, COL_RG)],
                      consts=[rec_norm], outs=[BF16])[0]


def _rec_gate_bwd(dyb, w_rec_proj, rec, proj, rec_norm):
    T = rec.shape[0]

    def fn(accs, tv, cv):
        d, r, rg = accs[0], tv[0], tv[1]
        sg = _sigmoid(rg)
        rn = _rms_hat(r) * cv[0]
        dh, dg = _rms_bwd_vals(d * sg, r, cv[0])
        return [dh, d * rn * sg * (1.0 - sg), dg]

    return _tile_call("rec_gate_bwd", fn, T, 512, _pick(T, 1024), REC_DIM, pairs=[(dyb, 0, w_rec_proj, "nt")],
                      tiles=[(rec, 0), (proj, COL_RG)], consts=[rec_norm], outs=[F32, BF16], parts=1)


def _mix_out_fwd(att, recn, proj, w_att_proj, w_rec_proj, w_out, h1, g_next):
    T = att.shape[0]
    tn = 256

    def merge(accs, tv, cv):
        ya, yb = accs
        return [ya, yb, _sigmoid(tv[0]) * ya + _sigmoid(tv[1]) * yb]

    ya, yb, merged = _tile_call(
        "merge", merge, T, D_MODEL, _pick(T, 1024), tn,
        pairs=[(att, 0, w_att_proj, "nn"), (recn, 0, w_rec_proj, "nn")],
        tiles=[(proj, COL_GA * 128 // tn), (proj, COL_GB * 128 // tn)], outs=[BF16] * 3)

    def res(accs, tv, cv):
        h2 = tv[0] + accs[0]
        return [h2, _rms_hat(h2) * cv[0]]

    h2, n2 = _tile_call("mix_out", res, T, D_MODEL, _pick(T, 512), D_MODEL, pairs=[(merged, 0, w_out, "nn")],
                        tiles=[(h1, 0)], consts=[g_next], outs=[F32, BF16])
    return h2, n2, (ya, yb, merged)


GATHER_FIRST = ("w_ffn1_in",)
GATHER_MIX = ("w_ffn1_out", "w_in", "w_att_proj", "w_rec_proj", "w_out")
GATHER_LAST = ("w_ffn2_in", "w_ffn2_out", "w_ple_gate", "w_ple_proj")
SCATTER_LATE = ("w_ple_gate", "w_ple_proj", "w_ffn2_in", "w_ffn2_out")
SCATTER_MIX = ("w_out", "w_att_proj", "w_rec_proj", "w_in")
SCATTER_LAST = ("w_ffn1_in", "w_ffn1_out")


def _local_step(x, p, tgt, w, mine16, cc, me_chip, B, S):
    T = B * S
    w = dict(w)
    g_ffn1, g_mix, g_ffn2, g_ple = w["norm_ffn1"], w["norm_mix"], w["norm_ffn2"], w["norm_ple"]
    g_fin = w["norm_final"].reshape(1, D_MODEL)
    grads, part, from_chips = {}, {}, {}

    def gather(names):
        return _gather_comm([mine16[n] for n in names])

    def place(names, got):
        for n, g in zip(names, got):
            full = lax.dynamic_update_index_in_dim(g, mine16[n], me_chip, 0)
            w[n] = full if n in ("w_ffn1_in", "w_ffn2_in") else _natural(n, full)

    def scatter(tag, names):
        from_sib = _swap_halves("rs_sibling_" + tag, [grads[n][1] for n in names])
        for n, fs in zip(names, from_sib):
            part[n] = _add_sibling("rs_add_sib_" + n, grads[n][0], fs, cc)
        return _scatter_comm([part[n][1] for n in names])

    def scattered(names, got):
        for n, g in zip(names, got):
            from_chips[n] = g

    def ffn1_in_weight(got):
        place(GATHER_FIRST, got)
        return w["w_ffn1_in"]

    def ffn1_out_weight(got):
        place(GATHER_MIX, got)
        return w["w_ffn1_out"]

    h1, u, sv1, got_last = _ffn_fwd("ffn1", x, g_ffn1, None, None, g_mix, comm_norm=gather(GATHER_FIRST),
                                    w_in_of=ffn1_in_weight, comm_in=gather(GATHER_MIX),
                                    comm_out=gather(GATHER_LAST), w_out_of=ffn1_out_weight)
    place(GATHER_LAST, got_last)

    def ident(accs, tv, cv):
        return [accs[0]]

    proj = _tile_call("in_proj", ident, T, IN_W, _pick(T, 512), IN_W // 2, pairs=[(u, 0, w["w_in"], "nn")],
                      outs=[F32], j_outer=True)[0]
    onehot = jnp.asarray(_t5_onehot())
    bias = _small_mm("t5_bias", w["rel_bias"].T, onehot.astype(BF16), "right")
    bias = bias.reshape(N_Q_HEADS, ATT_BLOCK, 2 * ATT_BLOCK)
    sinks = w["attn_sinks"].reshape(N_Q_HEADS)
    kk2, vv2 = _kv_layouts(proj)
    att = _swa_fwd(proj, kk2, vv2, bias, sinks, B, S)
    rec, states = _hgrn_fwd(proj, w["lb_param"], B, S)
    recn = _rec_gate_fwd(rec, proj, w["rec_norm"])
    h2, n2, (ya, yb, merged) = _mix_out_fwd(att, recn, proj, w["w_att_proj"], w["w_rec_proj"], w["w_out"], h1,
                                            g_ffn2)
    h3, n3, sv2, _ = _ffn_fwd("ffn2", h2, g_ffn2, w["w_ffn2_in"], w["w_ffn2_out"], g_ple, n=n2)

    def ple(accs, tv, cv):
        gate = _sigmoid(accs[0])
        return [gate, accs[1], tv[0] + gate * accs[1]]

    gate_p, pp, h4 = _tile_call(
        "ple", ple, T, D_MODEL, _pick(T, 512), D_MODEL,
        pairs=[(n3, 0, w["w_ple_gate"], "nn"), (p, 0, w["w_ple_proj"], "nn")], tiles=[(h3, 0)],
        outs=[BF16, BF16, F32])

    def head(accs, tv, cv):
        h, t, gt, ppv = tv[0], tv[1], tv[2].astype(F32), tv[3].astype(F32)
        err = _rms_hat(h) * cv[0] - t
        dh, dg = _rms_bwd_vals(err * (1.0 / D_MODEL), h, cv[0])
        return [dh, dh * ppv * gt * (1.0 - gt), dh * gt, _group8(err * err), dg]

    dh4, dzg, dpp, loss_p, dg_fin = _tile_call(
        "loss_head", head, T, D_MODEL, _pick(T, 256), D_MODEL,
        tiles=[(h4, 0), (tgt, 0), (gate_p, 0), (pp, 0)], consts=[g_fin], outs=[F32, BF16, BF16], parts=2)
    grads["norm_final"] = dg_fin

    grads["w_ple_gate"] = _mm_tn_rows("ple_dwg", n3, dzg)
    grads["w_ple_proj"] = _mm_tn_cols("ple_dwp", p, dpp)

    def dnorm(accs, tv, cv):
        dh, dg = _rms_bwd_vals(accs[0], tv[0], cv[0])
        dh = tv[1] + dh
        return [dh, 0.5 * dh, dg]

    dh3, df3, grads["norm_ple"] = _tile_call(
        "ple_dnorm", dnorm, T, D_MODEL, _pick(T, 512), D_MODEL, pairs=[(dzg, 0, w["w_ple_gate"], "nt")],
        tiles=[(h3, 0), (dh4, 0)], consts=[g_ple], outs=[F32, BF16], parts=1)

    dh2, dh2b, grads["norm_ffn2"], grads["w_ffn2_in"], grads["w_ffn2_out"], _, _ = _ffn_bwd(
        "ffn2b", dh3, df3, h2, g_ffn2, w["w_ffn2_in"], w["w_ffn2_out"], sv2)
    scatter_late = scatter("late", SCATTER_LATE)

    grads["w_out"] = _mm_tn_rows("mix_dwout", merged, dh2b)
    tn = 256

    def dmerge(accs, tv, cv):
        dm = accs[0]
        sa, sb = _sigmoid(tv[0]), _sigmoid(tv[1])
        yav, ybv = tv[2].astype(F32), tv[3].astype(F32)
        return [dm * sa, dm * sb, dm * yav * sa * (1.0 - sa), dm * ybv * sb * (1.0 - sb)]

    dya, dyb, dga, dgb = _tile_call(
        "mix_dmerge", dmerge, T, D_MODEL, _pick(T, 1024), tn, pairs=[(dh2b, 0, w["w_out"], "nt")],
        tiles=[(proj, COL_GA * 128 // tn), (proj, COL_GB * 128 // tn), (ya, 0), (yb, 0)], outs=[BF16] * 4)
    grads["w_att_proj"] = _mm_tn_cols("mix_dwatt", att, dya)
    grads["w_rec_proj"] = _mm_tn_cols("mix_dwrec", recn, dyb)

    datt = _tile_call("mix_datt", ident, T, 512, _pick(T, 1024), 512, pairs=[(dya, 0, w["w_att_proj"], "nt")],
                      outs=[BF16])[0]
    drec, drg, grads["rec_norm"] = _rec_gate_bwd(dyb, w["w_rec_proj"], rec, proj, w["rec_norm"])

    drq, drf, dri, dlb, *got = _hgrn_bwd(proj, w["lb_param"], states, drec, B, S, comm=scatter_late)
    scattered(SCATTER_LATE, got)
    grads["lb_param"] = dlb
    daq, dak, dav, dbias, dsink = _swa_bwd(proj, kk2, vv2, bias, sinks, datt, B, S)
    grads["attn_sinks"] = dsink
    grads["rel_bias"] = _small_mm("t5_dbias", dbias.reshape(N_Q_HEADS, -1), onehot.T.astype(BF16), "right")
    dproj = jnp.concatenate([daq, dak, dav, drq, drf, dri, drg, dga, dgb], axis=1)
    tk = _pick(T, 2048, 128)
    w_in_shard = IN_W // N_CHIPS
    half_d = D_MODEL // 2
    gw32, gw16 = _mm_tn("mix_dwin", (2, 2, T // tk),
                        (u, (tk, half_d), lambda i, j, k: (k, i)), (dproj, (tk, IN_W // 2), lambda i, j, k: (k, j)),
                        _grad_pair((D_MODEL, IN_W), (half_d, IN_W // 2), lambda i, j, k: (i, j)))
    to_sh = lambda t: t.reshape(D_MODEL, N_CHIPS, w_in_shard).transpose(1, 0, 2)
    grads["w_in"] = (to_sh(gw32), to_sh(gw16))
    scatter_mix = scatter("mix", SCATTER_MIX)

    def dnorm_mix(accs, tv, cv):
        dh, dg = _rms_bwd_vals(accs[0], tv[0], cv[0])
        dh = tv[1] + dh
        return [dh, 0.5 * dh, dg]

    dh1, df1, grads["norm_mix"] = _tile_call(
        "mix_dnorm", dnorm_mix, T, D_MODEL, _pick(T, 512), D_MODEL, pairs=[(dproj, 0, w["w_in"], "nt")],
        tiles=[(h1, 0), (dh2, 0)], consts=[g_mix], outs=[F32, BF16], parts=1)

    def scatter_last(dw_in, dw_out):
        grads["w_ffn1_in"], grads["w_ffn1_out"] = dw_in, dw_out
        return scatter("last", SCATTER_LAST)

    dx, _, grads["norm_ffn1"], _, _, got, got_last = _ffn_bwd(
        "ffn1b", dh1, df1, x, g_ffn1, w["w_ffn1_in"], w["w_ffn1_out"], sv1, comm=scatter_mix, comm_last=scatter_last)
    scattered(SCATTER_MIX, got)
    scattered(SCATTER_LAST, got_last)
    return loss_p, dx, grads, part, from_chips


def _place():
    x, y, c = lax.axis_index("x"), lax.axis_index("y"), lax.axis_index("c")
    return x, y, c


def _other_chips(x, y):
    return [(1 - x, y, 2 * (1 - x) + y), (x, 1 - y, 2 * x + 1 - y), (1 - x, 1 - y, 2 * (1 - x) + 1 - y)]


def _half_rows(ref_3d, chip, h, rows):
    return ref_3d.at[chip, pl.ds(h * rows, rows), :]


def _gather_comm(ws):
    nw = len(ws)

    def parts(w_refs, out_refs, send_sems, recv_sems):
        x, y, c = _place()
        me = 2 * x + y
        chips = _other_chips(x, y)

        def copy(i, k, chip, h, to, src=None):
            half = ws[i].shape[0] // 2
            dst = _half_rows(out_refs[i], chip, h, half)
            return pltpu.make_async_remote_copy(
                src_ref=dst if src is None else src, dst_ref=dst,
                send_sem=send_sems.at[6 * i + k], recv_sem=recv_sems.at[6 * i + k], device_id=to, device_id_type=MESH)

        def first():
            out = []
            for i in range(nw):
                half = ws[i].shape[0] // 2
                out += [copy(i, j, me, c, (cx, cy, c), src=w_refs[i].at[pl.ds(c * half, half), :])
                        for j, (cx, cy, _) in enumerate(chips)]
            return out

        return copy, first, chips, c, (x, y, 1 - c)

    def start(*refs):
        _, first, _, _, _ = parts(*refs)
        for cp in first():
            cp.start()

    def finish(*refs):
        copy, first, chips, c, sibling = parts(*refs)
        passed = []
        for i in range(nw):
            for j, (cx, cy, ci) in enumerate(chips):
                copy(i, j, ci, c, (cx, cy, c)).wait_recv()
                fw = copy(i, 3 + j, ci, c, sibling)
                fw.start()
                passed.append(fw)
        for i in range(nw):
            for j, (_, _, ci) in enumerate(chips):
                copy(i, 3 + j, ci, 1 - c, sibling).wait_recv()
        for cp in first() + passed:
            cp.wait_send()

    return _Comm(list(ws), [jax.ShapeDtypeStruct((N_CHIPS,) + w.shape, w.dtype) for w in ws], 6 * nw, start, finish)


def _scatter_comm(ps):
    nw = len(ps)

    def copies(p_refs, out_refs, send_sems, recv_sems):
        x, y, c = _place()
        cps = []
        for i in range(nw):
            for j, (cx, cy, ci) in enumerate(_other_chips(x, y)):
                cps.append(pltpu.make_async_remote_copy(
                    src_ref=p_refs[i].at[ci], dst_ref=out_refs[i].at[j], send_sem=send_sems.at[3 * i + j],
                    recv_sem=recv_sems.at[3 * i + j], device_id=(cx, cy, c), device_id_type=MESH))
        return cps

    def start(*refs):
        for cp in copies(*refs):
            cp.start()

    def finish(*refs):
        for cp in copies(*refs):
            cp.wait()

    return _Comm(list(ps), [jax.ShapeDtypeStruct((3,) + p.shape[1:], p.dtype) for p in ps], 3 * nw, start, finish)


def _swap_halves(name, gs):
    nw = len(gs)

    def body(*refs):
        g_refs, out_refs, send_sems, recv_sems = refs[:nw], refs[nw:2 * nw], refs[2 * nw], refs[2 * nw + 1]
        x, y, c = _place()
        cps = []
        for i in range(nw):
            half = gs[i].shape[1] // 2
            cps.append(pltpu.make_async_remote_copy(
                src_ref=g_refs[i].at[:, pl.ds((1 - c) * half, half), :], dst_ref=out_refs[i],
                send_sem=send_sems.at[i], recv_sem=recv_sems.at[i], device_id=(x, y, 1 - c), device_id_type=MESH))
        for cp in cps:
            cp.start()
        for cp in cps:
            cp.wait()

    return pl.pallas_call(
        body, name=name, in_specs=[ANY] * nw, out_specs=[ANY] * nw,
        out_shape=[jax.ShapeDtypeStruct((N_CHIPS, g.shape[1] // 2, g.shape[2]), g.dtype) for g in gs],
        scratch_shapes=[pltpu.SemaphoreType.DMA((nw,)), pltpu.SemaphoreType.DMA((nw,))],
    )(*gs)


def _join_halves(name, ss):
    nw = len(ss)

    def body(*refs):
        s_refs, out_refs, send_sems, recv_sems = refs[:nw], refs[nw:2 * nw], refs[2 * nw], refs[2 * nw + 1]
        x, y, c = _place()
        cps = [pltpu.make_async_remote_copy(
            src_ref=s_refs[i], dst_ref=out_refs[i], send_sem=send_sems.at[i], recv_sem=recv_sems.at[i],
            device_id=(x, y, 1 - c), device_id_type=MESH) for i in range(nw)]
        for cp in cps:
            cp.start()
        for cp in cps:
            cp.wait()

    return pl.pallas_call(
        body, name=name, in_specs=[ANY] * nw, out_specs=[ANY] * nw,
        out_shape=[jax.ShapeDtypeStruct(s.shape, s.dtype) for s in ss],
        scratch_shapes=[pltpu.SemaphoreType.DMA((nw,)), pltpu.SemaphoreType.DMA((nw,))],
    )(*ss)


def _allreduce_small(sp):
    def body(s_ref, out_ref, slots, send_sems, recv_sems):
        x, y, c = _place()
        me = 4 * x + 2 * y + c
        slots[me] = s_ref[...]
        cps = []
        for r in range(1, N_DEV):
            px, py, pc = x ^ (r >> 2), y ^ ((r >> 1) & 1), c ^ (r & 1)
            cps.append(pltpu.make_async_remote_copy(
                src_ref=s_ref, dst_ref=slots.at[me], send_sem=send_sems.at[r - 1], recv_sem=recv_sems.at[r - 1],
                device_id=(px, py, pc), device_id_type=MESH))
        for cp in cps:
            cp.start()
        for r in range(1, N_DEV):
            px, py, pc = x ^ (r >> 2), y ^ ((r >> 1) & 1), c ^ (r & 1)
            pltpu.make_async_remote_copy(
                src_ref=s_ref, dst_ref=slots.at[4 * px + 2 * py + pc], send_sem=send_sems.at[r - 1],
                recv_sem=recv_sems.at[r - 1], device_id=(px, py, pc), device_id_type=MESH).wait_recv()
        for cp in cps:
            cp.wait_send()
        acc = slots[0]
        for d in range(1, N_DEV):
            acc = acc + slots[d]
        out_ref[...] = acc

    return pl.pallas_call(
        body, name="allreduce_small",
        in_specs=[pl.BlockSpec(memory_space=pltpu.VMEM)], out_specs=pl.BlockSpec(memory_space=pltpu.VMEM),
        out_shape=jax.ShapeDtypeStruct(sp.shape, F32),
        scratch_shapes=[pltpu.VMEM((N_DEV,) + sp.shape, F32), pltpu.SemaphoreType.DMA((N_DEV - 1,)),
                        pltpu.SemaphoreType.DMA((N_DEV - 1,))],
    )(sp)


def _scalar(v):
    return jnp.reshape(v, (1,)).astype(jnp.int32)


def _row_tile(h, dtype_mult=16):
    return _pick(h, 256, dtype_mult)


def _add_sibling(name, g32, from_sib, c):
    _, r, n = g32.shape
    h = r // 2
    th = _row_tile(h)
    nt = h // th

    def body(c_ref, g_ref, s_ref, o32_ref, o16_ref):
        s = g_ref[...] + s_ref[...].astype(F32)
        o32_ref[...] = s
        o16_ref[...] = s.astype(BF16)

    blk = (None, th, n)
    return pl.pallas_call(
        body, name=name,
        grid_spec=pltpu.PrefetchScalarGridSpec(
            num_scalar_prefetch=1, grid=(N_CHIPS, nt),
            in_specs=[pl.BlockSpec(blk, lambda k, t, c_ref: (k, c_ref[0] * nt + t, 0)),
                      pl.BlockSpec(blk, lambda k, t, c_ref: (k, t, 0))],
            out_specs=[pl.BlockSpec(blk, lambda k, t, c_ref: (k, t, 0))] * 2),
        out_shape=[jax.ShapeDtypeStruct((N_CHIPS, h, n), F32), jax.ShapeDtypeStruct((N_CHIPS, h, n), BF16)],
        compiler_params=_params("arbitrary", "arbitrary"))(_scalar(c), g32, from_sib)


def _add_chips(name, p32, from_chips, me_chip):
    _, h, n = p32.shape
    th = _row_tile(h)

    def body(m_ref, p_ref, a_ref, b_ref, c_ref, o_ref):
        o_ref[...] = p_ref[...] + a_ref[...].astype(F32) + b_ref[...].astype(F32) + c_ref[...].astype(F32)

    blk = (None, th, n)
    return pl.pallas_call(
        body, name=name,
        grid_spec=pltpu.PrefetchScalarGridSpec(
            num_scalar_prefetch=1, grid=(h // th,),
            in_specs=[pl.BlockSpec(blk, lambda t, m_ref: (m_ref[0], t, 0))]
            + [pl.BlockSpec(blk, lambda t, m_ref, j=j: (j, t, 0)) for j in range(3)],
            out_specs=pl.BlockSpec((th, n), lambda t, m_ref: (t, 0))),
        out_shape=jax.ShapeDtypeStruct((h, n), F32),
        compiler_params=_params("arbitrary"))(_scalar(me_chip), p32, from_chips, from_chips, from_chips)


def _adamw_vals(w, g, m, v):
    m = ADAM_B1 * m + (1.0 - ADAM_B1) * g
    v = ADAM_B2 * v + (1.0 - ADAM_B2) * (g * g)
    m_hat = m / (1.0 - ADAM_B1 ** ADAM_STEP)
    v_hat = v / (1.0 - ADAM_B2 ** ADAM_STEP)
    delta = -ADAM_LR * (m_hat / (jnp.sqrt(v_hat) + ADAM_EPS) + ADAM_WD * w)
    return delta, m, v


def _adamw_halves(name, w, m, v, g_mine, g_sib, c):
    r, n = w.shape
    h = r // 2
    th = _row_tile(h, 8)
    nt = h // th

    def body(c_ref, w_ref, m_ref, v_ref, a_ref, b_ref, g_ref, d_ref, nm_ref, nv_ref):
        mine = (pl.program_id(0) // nt) == c_ref[0]
        g = jnp.where(mine, a_ref[...], b_ref[...])
        d, nm, nv = _adamw_vals(w_ref[...], g, m_ref[...], v_ref[...])
        g_ref[...] = g
        d_ref[...] = d
        nm_ref[...] = nm
        nv_ref[...] = nv

    full = pl.BlockSpec((th, n), lambda t, c_ref: (t, 0))
    part = pl.BlockSpec((th, n), lambda t, c_ref: (t % nt, 0))
    return pl.pallas_call(
        body, name=name,
        grid_spec=pltpu.PrefetchScalarGridSpec(
            num_scalar_prefetch=1, grid=(2 * nt,), in_specs=[full, full, full, part, part], out_specs=[full] * 4),
        out_shape=[jax.ShapeDtypeStruct((r, n), F32)] * 4,
        compiler_params=_params("arbitrary"))(_scalar(c), w, m, v, g_mine, g_sib)


def _adamw(name, w, g, m, v):
    R, W = w.shape

    def fn(accs, tv, cv):
        return list(_adamw_vals(*tv))

    return _tile_call(name, fn, R, W, _pick(R, 256), W, tiles=[(w, 0), (g, 0), (m, 0), (v, 0)], outs=[F32] * 3)


SMALL_LAYOUT = (("rel_bias", 2, 256), ("lb_param", 8, 1024), ("norm_ffn1", 8, 1024), ("norm_mix", 8, 1024),
                ("attn_sinks", 1, 8), ("rec_norm", 1, 128), ("norm_ffn2", 8, 1024), ("norm_ple", 8, 1024),
                ("norm_final", 8, 1024), ("loss", 8, 1024))


def _pack_small(vals):
    rows = []
    for name, nrows, n in SMALL_LAYOUT:
        flat = vals[name].reshape(-1)
        flat = jnp.pad(flat, (0, nrows * 128 - n))
        rows.append(flat.reshape(nrows, 128))
    packed = jnp.concatenate(rows, axis=0)
    return jnp.pad(packed, ((0, SMALL_ROWS - packed.shape[0]), (0, 0)))


def _unpack_small(packed, shapes):
    out, r = {}, 0
    for name, nrows, n in SMALL_LAYOUT:
        out[name] = packed[r:r + nrows].reshape(-1)[:n].reshape(shapes[name])
        r += nrows
    return out


def _natural(name, s):
    if name in COL_SHARDED:
        return s.transpose(1, 0, 2).reshape(s.shape[1], -1)
    return s.reshape(-1, s.shape[2])


def kernel(x, p, rel_bias, lb_param, norm_ffn1, w_ffn1_in, w_ffn1_out, norm_mix, w_in, attn_sinks, rec_norm, w_att_proj, w_rec_proj, w_out, norm_ffn2, w_ffn2_in, w_ffn2_out, norm_ple, w_ple_gate, w_ple_proj, norm_final, loss_target, m_rel_bias, m_lb_param, m_norm_ffn1, m_w_ffn1_in, m_w_ffn1_out, m_norm_mix, m_w_in, m_attn_sinks, m_rec_norm, m_w_att_proj, m_w_rec_proj, m_w_out, m_norm_ffn2, m_w_ffn2_in, m_w_ffn2_out, m_norm_ple, m_w_ple_gate, m_w_ple_proj, m_norm_final, v_rel_bias, v_lb_param, v_norm_ffn1, v_w_ffn1_in, v_w_ffn1_out, v_norm_mix, v_w_in, v_attn_sinks, v_rec_norm, v_w_att_proj, v_w_rec_proj, v_w_out, v_norm_ffn2, v_w_ffn2_in, v_w_ffn2_out, v_norm_ple, v_w_ple_gate, v_w_ple_proj, v_norm_final):
    args = dict(locals())
    wsh = {n: args[n] for n in WEIGHTS}
    B, S = x.shape[0], x.shape[1]
    T = B * S
    cx, cy, cc = _place()
    me_chip = 2 * cx + cy

    mine16 = {n: wsh[n][0].astype(BF16) for n in BIG}
    loss_p, dx, grads, part, from_chips = _local_step(
        x.reshape(T, D_MODEL), p.reshape(T, PLE_DIM), loss_target.reshape(T, D_MODEL),
        {n: wsh[n] for n in SMALL}, mine16, cc, me_chip, B, S)

    s_mine = [_add_chips("rs_add_chips_" + n, part[n][0], from_chips[n], me_chip) for n in BIG]
    s_sib = _join_halves("rs_join", s_mine)

    small_vals = {
        "rel_bias": grads["rel_bias"].T,
        "lb_param": jnp.concatenate([_colsum("dlb_sum", grads["lb_param"]),
                                     -_colsum("dlb_sum2", grads["lb_param"])], axis=0) / 8.0,
        "attn_sinks": grads["attn_sinks"][:, 0],
        "rec_norm": _colsum("drn_sum", grads["rec_norm"]).reshape(REC_HEADS, REC_DIM).sum(axis=0),
        "loss": _colsum("loss_sum", loss_p),
    }
    for n in ("norm_ffn1", "norm_mix", "norm_ffn2", "norm_ple", "norm_final"):
        small_vals[n] = _colsum(n + "_sum", grads[n])
    red = _allreduce_small(_pack_small(small_vals))
    small_shapes = {n: wsh[n].shape for n in SMALL}
    small_shapes["loss"] = (D_MODEL,)
    small = _unpack_small(red, small_shapes)
    loss = 0.5 * jnp.sum(small["loss"]) / D_MODEL

    out_g, out_d, out_m, out_v = {}, {}, {}, {}
    for n, gm, gs in zip(BIG, s_mine, s_sib):
        res = _adamw_halves("adamw_" + n, wsh[n][0], args["m_" + n][0], args["v_" + n][0], gm, gs, cc)
        out_g[n], out_d[n], out_m[n], out_v[n] = (t[None] for t in res)
    sw = _pack_small({**{n: wsh[n] for n in SMALL}, "loss": jnp.zeros((D_MODEL,), F32)})
    sm = _pack_small({**{n: args["m_" + n] for n in SMALL}, "loss": jnp.zeros((D_MODEL,), F32)})
    sv = _pack_small({**{n: args["v_" + n] for n in SMALL}, "loss": jnp.ones((D_MODEL,), F32)})
    sd, snm, snv = _adamw("adamw_small", sw, red, sm, sv)
    ud, um, uv = (_unpack_small(t, small_shapes) for t in (sd, snm, snv))
    for n in SMALL:
        out_g[n], out_d[n], out_m[n], out_v[n] = small[n], ud[n], um[n], uv[n]

    return (loss, dx.reshape(B, S, D_MODEL), *[out_g[n] for n in WEIGHTS], *[out_d[n] for n in WEIGHTS],
            *[out_m[n] for n in WEIGHTS], *[out_v[n] for n in WEIGHTS])
```

```python
import numpy as np
import jax
import jax.numpy as jnp
from jax import lax
from jax.experimental import pallas as pl
from jax.experimental.pallas import tpu as pltpu

F32 = jnp.float32
BF16 = jnp.bfloat16
MESH = pl.DeviceIdType.MESH

D_MODEL = 1024
D_FF = 2816
FF_SHARD = 2 * D_FF // 4
HEAD_DIM = 64
N_Q_HEADS = 8
ATT_BLOCK = 128
N_BUCKETS = 32
MAX_DISTANCE = 128
REC_HEADS = 4
REC_DIM = 128
PLE_DIM = 256
EPS = 1e-6
IN_W = 4864
COL_AQ, COL_AK, COL_AV, COL_RQ, COL_RF, COL_RI, COL_RG, COL_GA, COL_GB = 0, 4, 5, 6, 10, 14, 18, 22, 30

CHUNK = 64
SUB = 8
N_SUB = CHUNK // SUB
HGRN_PAIR = 2

ADAM_LR, ADAM_B1, ADAM_B2, ADAM_EPS, ADAM_WD, ADAM_STEP = 0.001, 0.9, 0.999, 1e-08, 0.01, 10

V7X_VMEM_LIMIT = 56 * 1024 * 1024
N_CHIPS = 4
N_DEV = 8

BIG = ("w_ffn1_in", "w_ffn1_out", "w_in", "w_att_proj", "w_rec_proj", "w_out",
       "w_ffn2_in", "w_ffn2_out", "w_ple_gate", "w_ple_proj")
COL_SHARDED = ("w_ffn1_in", "w_in", "w_att_proj", "w_rec_proj", "w_ffn2_in", "w_ple_proj")
WEIGHTS = ("rel_bias", "lb_param", "norm_ffn1", "w_ffn1_in", "w_ffn1_out", "norm_mix", "w_in", "attn_sinks",
           "rec_norm", "w_att_proj", "w_rec_proj", "w_out", "norm_ffn2", "w_ffn2_in", "w_ffn2_out", "norm_ple",
           "w_ple_gate", "w_ple_proj", "norm_final")
SMALL = tuple(n for n in WEIGHTS if n not in BIG)
SMALL_ROWS = 64


def _params(*sem):
    return pltpu.CompilerParams(dimension_semantics=sem, vmem_limit_bytes=V7X_VMEM_LIMIT)


def _pick(n, cap, mult=8):
    if n <= cap:
        return n
    for t in range(cap - cap % mult, 0, -mult):
        if n % t == 0:
            return t
    raise ValueError((n, cap, mult))


def _dot(a, b):
    return jnp.dot(a, b, preferred_element_type=F32)


def _dot_nt(a, b):
    return lax.dot_general(a, b, (((1,), (1,)), ((), ())), preferred_element_type=F32)


def _dot_tn(a, b):
    return lax.dot_general(a, b, (((0,), (0,)), ((), ())), preferred_element_type=F32)


def _split3(x):
    hi = x.astype(BF16)
    r = x - hi.astype(F32)
    mid = r.astype(BF16)
    lo = (r - mid.astype(F32)).astype(BF16)
    return hi, mid, lo


def _split2(x):
    hi = x.astype(BF16)
    return hi, (x - hi.astype(F32)).astype(BF16)


def _sel_left(sel_bf16, x):
    hi, mid, lo = _split3(x)
    return _dot(sel_bf16, hi) + _dot(sel_bf16, mid) + _dot(sel_bf16, lo)


def _sel_right(x, sel_bf16):
    hi, mid, lo = _split3(x)
    return _dot(hi, sel_bf16) + _dot(mid, sel_bf16) + _dot(lo, sel_bf16)


def _sigmoid(x):
    return 0.5 * jnp.tanh(0.5 * x) + 0.5


def _group8(x):
    r, w = x.shape
    return x.reshape(r // 8, 8, w).sum(axis=0)


class _Comm:
    def __init__(self, ins, out_shapes, n_sems, start, finish):
        self.ins, self.out_shapes, self.n_sems, self.start, self.finish = ins, out_shapes, n_sems, start, finish


ANY = pl.BlockSpec(memory_space=pl.ANY)


def _comm_parts(comm):
    if comm is None:
        return [], [], [], []
    sems = [pltpu.SemaphoreType.DMA((comm.n_sems,)), pltpu.SemaphoreType.DMA((comm.n_sems,))]
    return list(comm.ins), [ANY] * len(comm.ins), list(comm.out_shapes), sems


def _comm_run(comm, grid, refs, n_in, n_out):
    if comm is None:
        return (lambda: None), (lambda: None)
    nci, nco = len(comm.ins), len(comm.out_shapes)
    cin = refs[n_in:n_in + nci]
    cout = refs[n_in + nci + n_out:n_in + nci + n_out + nco]
    send_sems, recv_sems = refs[-2], refs[-1]
    ids = [pl.program_id(d) for d in range(len(grid))]
    is_first = ids[0] == 0
    is_last = ids[0] == grid[0] - 1
    for d in range(1, len(grid)):
        is_first = is_first & (ids[d] == 0)
        is_last = is_last & (ids[d] == grid[d] - 1)

    def first():
        @pl.when(is_first)
        def _():
            comm.start(cin, cout, send_sems, recv_sems)

    def last():
        @pl.when(is_last)
        def _():
            comm.finish(cin, cout, send_sems, recv_sems)

    return first, last


def _call(name, fn, grid, ins, outs, pairs=(), comm=None, j_outer=False):
    in_pair = {i for p in pairs for i in p[:2]}
    n_in, n_out = len(ins), len(outs)
    c_arrays, c_in_specs, c_out_shapes, c_sems = _comm_parts(comm)

    def body(*refs):
        first, last = _comm_run(comm, grid, refs, n_in, n_out)
        first()
        accs = []
        for ia, ib, kind in pairs:
            a, b = refs[ia][...].astype(BF16), refs[ib][...].astype(BF16)
            accs.append(_dot(a, b) if kind == "nn" else _dot_nt(a, b))
        vals = [refs[i][...] for i in range(n_in) if i not in in_pair]
        res = fn(accs, vals)
        out_refs = refs[n_in + len(c_arrays):n_in + len(c_arrays) + n_out]
        assert len(res) == len(out_refs), (name, len(res), len(out_refs))
        for o_ref, val in zip(out_refs, res):
            o_ref[...] = val.astype(o_ref.dtype)
        last()

    if j_outer:
        grid = (grid[1], grid[0])
        swap = lambda im: (lambda j, i: im(i, j))
        ins = [(a, blk, swap(im)) for a, blk, im in ins]
        outs = [(shp, dt, blk, swap(im)) for shp, dt, blk, im in outs]

    return pl.pallas_call(
        body, name=name, grid=grid,
        in_specs=[pl.BlockSpec(blk, im) for _, blk, im in ins] + c_in_specs,
        out_specs=[pl.BlockSpec(blk, im) for _, _, blk, im in outs] + [ANY] * len(c_out_shapes),
        out_shape=[jax.ShapeDtypeStruct(shp, dt) for shp, dt, _, _ in outs] + c_out_shapes,
        scratch_shapes=c_sems,
        compiler_params=_params(*(["arbitrary"] * len(grid))))(*[a for a, _, _ in ins], *c_arrays)


def _tile_call(name, fn, M, N, tm, tn, *, pairs=(), tiles=(), consts=(), outs=(), parts=0, comm=None,
               j_outer=False):
    gi, gj = M // tm, N // tn
    assert gi * tm == M and gj * tn == N, (name, M, N, tm, tn)
    ins, prs = [], []
    for a, a_col, b, kind in pairs:
        K = b.shape[0] if kind == "nn" else b.shape[1]
        ins.append((a, (tm, K), lambda i, j, c=a_col: (i, c)))
        if kind == "nn":
            ins.append((b, (K, tn), lambda i, j: (0, j)))
        else:
            ins.append((b, (tn, K), lambda i, j: (j, 0)))
        prs.append((len(ins) - 2, len(ins) - 1, kind))
    for arr, off in tiles:
        ins.append((arr, (tm, tn), lambda i, j, o=off: (i, j + o)))
    for arr in consts:
        ins.append((arr, arr.shape, lambda i, j: (0, 0)))
    out_l = [((M, N), dt, (tm, tn), lambda i, j: (i, j)) for dt in outs]
    out_l += [((gi * 8, N), F32, (8, tn), lambda i, j: (i, j))] * parts
    nt = len(tiles)

    def wrapped(accs, vals):
        return fn(accs, vals[:nt], vals[nt:])

    return _call(name, wrapped, (gi, gj), ins, out_l, prs, comm=comm, j_outer=j_outer)


def _mm_tn(name, grid, a_in, b_in, outs):
    nk = grid[2]
    tm = [d for d in a_in[1] if d is not None][1]
    tn = [d for d in b_in[1] if d is not None][1]

    def body(a_ref, b_ref, *rest):
        out_refs, acc_ref = rest[:-1], rest[-1]
        k = pl.program_id(2)

        @pl.when(k == 0)
        def _():
            acc_ref[...] = jnp.zeros_like(acc_ref)

        acc_ref[...] += _dot_tn(a_ref[...].astype(BF16), b_ref[...].astype(BF16))

        @pl.when(k == nk - 1)
        def _():
            for o_ref in out_refs:
                o_ref[...] = acc_ref[...].astype(o_ref.dtype)

    return pl.pallas_call(
        body, name=name, grid=grid,
        in_specs=[pl.BlockSpec(a_in[1], a_in[2]), pl.BlockSpec(b_in[1], b_in[2])],
        out_specs=[pl.BlockSpec(blk, im) for _, _, blk, im in outs],
        out_shape=[jax.ShapeDtypeStruct(shp, dt) for shp, dt, _, _ in outs],
        scratch_shapes=[pltpu.VMEM((tm, tn), F32)],
        compiler_params=_params("arbitrary", "arbitrary", "arbitrary"))(a_in[0], b_in[0])


def _grad_pair(shape, block, imap):
    return [(shape, F32, block, imap), (shape, BF16, block, imap)]


def _mm_tn_rows(name, a, b, tk=2048):
    T, a_w = a.shape
    b_w = b.shape[1]
    tm = _pick(a_w, 1408, 128)
    tk = _pick(T, tk, 128)
    g32, g16 = _mm_tn(name, (a_w // tm, 1, T // tk),
                      (a, (tk, tm), lambda i, j, k: (k, i)), (b, (tk, b_w), lambda i, j, k: (k, 0)),
                      _grad_pair((a_w, b_w), (tm, b_w), lambda i, j, k: (i, 0)))
    shp = (N_CHIPS, a_w // N_CHIPS, b_w)
    return g32.reshape(shp), g16.reshape(shp)


def _mm_tn_cols(name, a, b, tk=2048):
    T, a_w = a.shape
    n = b.shape[1] // N_CHIPS
    tk = _pick(T, tk, 128)
    return _mm_tn(name, (1, N_CHIPS, T // tk),
                  (a, (tk, a_w), lambda i, j, k: (k, 0)), (b, (tk, n), lambda i, j, k: (k, j)),
                  _grad_pair((N_CHIPS, a_w, n), (None, a_w, n), lambda i, j, k: (j, 0, 0)))


def _colsum(name, x):
    def body(x_ref, o_ref):
        o_ref[...] = jnp.sum(x_ref[...], axis=0, keepdims=True)
    return pl.pallas_call(body, name=name, out_shape=jax.ShapeDtypeStruct((1, x.shape[1]), F32))(x)


def _rms_hat(h):
    return h * lax.rsqrt(jnp.mean(h * h, axis=-1, keepdims=True) + EPS)


def _rms_bwd_vals(dn, h, g):
    r = lax.rsqrt(jnp.mean(h * h, axis=-1, keepdims=True) + EPS)
    nh = h * r
    gd = dn * g
    dh = r * (gd - nh * jnp.mean(gd * nh, axis=-1, keepdims=True))
    return dh, _group8(dn * nh)


def _rms_fwd(name, h, g, tm=512, comm=None):
    T = h.shape[0]

    def fn(accs, tv, cv):
        return [_rms_hat(tv[0]) * cv[0]]

    return _tile_call(name, fn, T, D_MODEL, _pick(T, tm), D_MODEL, tiles=[(h, 0)], consts=[g], outs=[BF16],
                      comm=comm)


def _ffn_fwd(tag, h, g, w_in, w_out, g_next, n=None, comm_norm=None, w_in_of=None, comm_in=None, comm_out=None,
             w_out_of=None):
    T = h.shape[0]
    if n is None:
        n, *got_norm = _rms_fwd(tag + "_norm", h, g, comm=comm_norm)
        if w_in_of is not None:
            w_in = w_in_of(got_norm)
    tm = _pick(T, 1024)
    wblk = (None, D_MODEL, FF_SHARD)

    def act(accs, vals):
        gate, up = accs
        return [gate, up, gate * _sigmoid(gate) * up]

    tile = lambda: ((T, D_FF), BF16, (tm, FF_SHARD), lambda i, j: (i, j))
    gate, up, a, *got_in = _call(
        tag + "_in", act, (T // tm, 2),
        [(n, (tm, D_MODEL), lambda i, j: (i, 0)),
         (w_in, wblk, lambda i, j: (j, 0, 0)), (w_in, wblk, lambda i, j: (j + 2, 0, 0))],
        [tile(), tile(), tile()], pairs=[(0, 1, "nn"), (0, 2, "nn")], comm=comm_in, j_outer=True)

    def res(accs, tv, cv):
        h_new = tv[0] + 0.5 * accs[0]
        return [h_new, _rms_hat(h_new) * cv[0]]

    if w_out_of is not None:
        w_out = w_out_of(got_in)
    h_new, n_next, *got_out = _tile_call(
        tag + "_out", res, T, D_MODEL, _pick(T, 512), D_MODEL, pairs=[(a, 0, w_out, "nn")], tiles=[(h, 0)],
        consts=[g_next], outs=[F32, BF16], comm=comm_out)
    return h_new, n_next, (n, gate, up, a), got_out


def _ffn_bwd(tag, dh_out, df, h, g, w_in, w_out, saved, comm=None, comm_last=None):
    T = h.shape[0]
    n, gate, up, a = saved
    tm = _pick(T, 512)

    def dact(accs, vals):
        da = accs[0]
        gt, u = vals[0].astype(F32), vals[1].astype(F32)
        sg = _sigmoid(gt)
        silu = gt * sg
        return [jnp.stack([(da * u * (sg + silu * (1.0 - sg))).astype(BF16), (da * silu).astype(BF16)])]

    dz, *got = _call(
        tag + "_dact", dact, (T // tm, 2),
        [(df, (tm, D_MODEL), lambda i, j: (i, 0)), (w_out, (FF_SHARD, D_MODEL), lambda i, j: (j, 0)),
         (gate, (tm, FF_SHARD), lambda i, j: (i, j)), (up, (tm, FF_SHARD), lambda i, j: (i, j))],
        [((2, T, D_FF), BF16, (2, tm, FF_SHARD), lambda i, j: (0, i, j))], pairs=[(0, 1, "nt")], comm=comm,
        j_outer=True)
    dw_out = _mm_tn_rows(tag + "_dwout", a, df)
    tk = _pick(T, 2048, 128)
    dw_in = _mm_tn(tag + "_dwin", (1, N_CHIPS, T // tk),
                   (n, (tk, D_MODEL), lambda i, j, k: (k, 0)),
                   (dz, (None, tk, FF_SHARD), lambda i, j, k: (j // 2, k, j % 2)),
                   _grad_pair((N_CHIPS, D_MODEL, FF_SHARD), (None, D_MODEL, FF_SHARD), lambda i, j, k: (j, 0, 0)))

    def dnorm(accs, vals):
        dn = accs[0] + accs[1] + accs[2] + accs[3]
        dh, dg = _rms_bwd_vals(dn, vals[0], vals[2])
        dh = vals[1] + dh
        return [dh, dh, dg]

    tm2 = _pick(T, 512)
    ins = [(dz, (None, tm2, FF_SHARD), lambda i, j, s=s: (s // 2, i, s % 2)) for s in range(N_CHIPS)]
    ins += [(w_in, (None, D_MODEL, FF_SHARD), lambda i, j, s=s: (s, 0, 0)) for s in range(N_CHIPS)]
    ins += [(h, (tm2, D_MODEL), lambda i, j: (i, 0)), (dh_out, (tm2, D_MODEL), lambda i, j: (i, 0)),
            (g, g.shape, lambda i, j: (0, 0))]
    dh, dh16, dg, *got_last = _call(
        tag + "_dnorm", dnorm, (T // tm2, 1), ins,
        [((T, D_MODEL), F32, (tm2, D_MODEL), lambda i, j: (i, 0)),
         ((T, D_MODEL), BF16, (tm2, D_MODEL), lambda i, j: (i, 0)),
         ((T // tm2 * 8, D_MODEL), F32, (8, D_MODEL), lambda i, j: (i, 0))],
        pairs=[(s, N_CHIPS + s, "nt") for s in range(N_CHIPS)],
        comm=None if comm_last is None else comm_last(dw_in, dw_out))
    return dh, dh16, dg, dw_in, dw_out, got, got_last


def _t5_onehot():
    qi = np.arange(ATT_BLOCK)[:, None] + ATT_BLOCK
    kj = np.arange(2 * ATT_BLOCK)[None, :]
    nn = np.maximum(qi - kj, 0)
    max_exact = N_BUCKETS // 2
    large = max_exact + (np.log(np.maximum(nn, 1) / max_exact) / np.log(MAX_DISTANCE / max_exact)
                         * (N_BUCKETS - max_exact)).astype(np.int32)
    large = np.minimum(large, N_BUCKETS - 1)
    bucket = np.where(nn < max_exact, nn, large).astype(np.int32).reshape(-1)
    return (bucket[None, :] == np.arange(N_BUCKETS)[:, None]).astype(np.float32)


def _small_mm(name, a, b, sel):
    def body(a_ref, b_ref, o_ref):
        if sel == "right":
            o_ref[...] = _sel_right(a_ref[...], b_ref[...])
        else:
            o_ref[...] = _sel_left(a_ref[...], b_ref[...])
    return pl.pallas_call(body, name=name, out_shape=jax.ShapeDtypeStruct((a.shape[0], b.shape[1]), F32),
                          compiler_params=pltpu.CompilerParams(vmem_limit_bytes=V7X_VMEM_LIMIT))(a, b)


def _dup_heads(t):
    a, b = t[:, :HEAD_DIM], t[:, HEAD_DIM:]
    return jnp.concatenate([a, a, b, b], axis=1)


def _kv_layouts(proj):
    T = proj.shape[0]

    def fn(accs, tv, cv):
        return [tv[0], tv[1]]

    k, v = _tile_call("kv_cast", fn, T, 128, _pick(T, 1024), 128, tiles=[(proj, COL_AK), (proj, COL_AV)],
                      outs=[BF16, BF16])
    return _dup_heads(k), _dup_heads(v)


def _swa_masks():
    row = lax.broadcasted_iota(jnp.int32, (ATT_BLOCK, 2 * ATT_BLOCK), 0)
    col = lax.broadcasted_iota(jnp.int32, (ATT_BLOCK, 2 * ATT_BLOCK), 1)
    dist = ATT_BLOCK + row - col
    return (dist >= 0) & (dist < ATT_BLOCK), col


GROUP = 4


def _stack_group(blk, lo_q):
    zero = jnp.zeros_like(blk[:, :128])
    rows = []
    for pair in range(GROUP // 2):
        pb = blk[:, 128 * pair:128 * (pair + 1)]
        rows += [jnp.where(lo_q, pb, zero), jnp.where(lo_q, zero, pb)]
    return jnp.concatenate(rows, axis=0)


def _unstack_group(st, lo_q):
    pairs = [jnp.where(lo_q, st[256 * pair:256 * pair + 128], st[256 * pair + 128:256 * (pair + 1)])
             for pair in range(GROUP // 2)]
    return jnp.concatenate(pairs, axis=1)


def _swa_probs(s, bias_h, sink, valid):
    s = jnp.where(valid, s * (HEAD_DIM ** -0.5) + bias_h, -jnp.inf)
    m = jnp.maximum(jnp.max(s, axis=-1, keepdims=True), sink)
    e = jnp.exp(s - m)
    es = jnp.exp(sink - m)
    den = jnp.sum(e, axis=-1, keepdims=True) + es
    return e / den, es / den


def _swa_fwd(proj, kk2, vv2, bias, sinks, B, S):
    T = B * S
    nb = S // ATT_BLOCK

    def body(q_ref, k_ref, v_ref, bias_ref, sink_ref, o_ref, kpad, vpad):
        zeros = jnp.zeros((ATT_BLOCK, 256), BF16)
        kpad[pl.ds(0, ATT_BLOCK), :] = zeros
        vpad[pl.ds(0, ATT_BLOCK), :] = zeros
        kpad[pl.ds(ATT_BLOCK, S), :] = k_ref[...]
        vpad[pl.ds(ATT_BLOCK, S), :] = v_ref[...]
        valid0, col = _swa_masks()
        lo_q = lax.broadcasted_iota(jnp.int32, (1, 128), 1) < HEAD_DIM

        def blk(n, carry):
            r0 = pl.multiple_of(n * ATT_BLOCK, ATT_BLOCK)
            rows = pl.ds(r0, ATT_BLOCK)
            valid = valid0 & ((n > 0) | (col >= ATT_BLOCK))
            for g in range(N_Q_HEADS // GROUP):
                lanes = pl.ds(128 * g, 128)
                kg = kpad[pl.ds(r0, 2 * ATT_BLOCK), lanes]
                vg = vpad[pl.ds(r0, 2 * ATT_BLOCK), lanes]
                qm = _stack_group(q_ref[rows, pl.ds(256 * g, 256)].astype(BF16), lo_q)
                s = _dot_nt(qm, kg)
                ps = []
                for i in range(GROUP):
                    h = GROUP * g + i
                    p, _ = _swa_probs(s[ATT_BLOCK * i:ATT_BLOCK * (i + 1)], bias_ref[h], sink_ref[h], valid)
                    ps.append(p.astype(BF16))
                o = _dot(jnp.concatenate(ps, axis=0), vg)
                o_ref[rows, pl.ds(256 * g, 256)] = _unstack_group(o, lo_q).astype(o_ref.dtype)
            return carry

        lax.fori_loop(0, nb, blk, 0)

    return pl.pallas_call(
        body, name="swa_fwd", grid=(B,),
        in_specs=[pl.BlockSpec((S, 512), lambda b: (b, 0)),
                  pl.BlockSpec((S, 256), lambda b: (b, 0)),
                  pl.BlockSpec((S, 256), lambda b: (b, 0)),
                  pl.BlockSpec((N_Q_HEADS, ATT_BLOCK, 2 * ATT_BLOCK), lambda b: (0, 0, 0)),
                  pl.BlockSpec(memory_space=pltpu.SMEM)],
        out_specs=pl.BlockSpec((S, 512), lambda b: (b, 0)),
        out_shape=jax.ShapeDtypeStruct((T, 512), BF16),
        scratch_shapes=[pltpu.VMEM((S + ATT_BLOCK, 256), BF16), pltpu.VMEM((S + ATT_BLOCK, 256), BF16)],
        compiler_params=_params("arbitrary"))(proj, kk2, vv2, bias, sinks)


def _swa_bwd(proj, kk2, vv2, bias, sinks, datt, B, S):
    T = B * S
    nb = S // ATT_BLOCK

    def body(q_ref, k_ref, v_ref, bias_ref, sink_ref, do_ref, dq_ref, dk_ref, dv_ref, dbias_ref, dsink_ref,
             kpad, vpad, dkpad, dvpad):
        b = pl.program_id(0)

        @pl.when(b == 0)
        def _():
            dbias_ref[...] = jnp.zeros_like(dbias_ref)
            dsink_ref[...] = jnp.zeros_like(dsink_ref)

        zeros = jnp.zeros((ATT_BLOCK, 256), BF16)
        kpad[pl.ds(0, ATT_BLOCK), :] = zeros
        vpad[pl.ds(0, ATT_BLOCK), :] = zeros
        kpad[pl.ds(ATT_BLOCK, S), :] = k_ref[...]
        vpad[pl.ds(ATT_BLOCK, S), :] = v_ref[...]
        dkpad[...] = jnp.zeros_like(dkpad)
        dvpad[...] = jnp.zeros_like(dvpad)
        valid0, col = _swa_masks()
        lo_q = lax.broadcasted_iota(jnp.int32, (1, 128), 1) < HEAD_DIM
        scale = HEAD_DIM ** -0.5

        def blk(n, carry):
            r0 = pl.multiple_of(n * ATT_BLOCK, ATT_BLOCK)
            rows = pl.ds(r0, ATT_BLOCK)
            band = pl.ds(r0, 2 * ATT_BLOCK)
            valid = valid0 & ((n > 0) | (col >= ATT_BLOCK))
            for g in range(N_Q_HEADS // GROUP):
                lanes = pl.ds(128 * g, 128)
                kg = kpad[band, lanes]
                vg = vpad[band, lanes]
                qm = _stack_group(q_ref[rows, pl.ds(256 * g, 256)].astype(BF16), lo_q)
                dom = _stack_group(do_ref[rows, pl.ds(256 * g, 256)], lo_q)
                s = _dot_nt(qm, kg)
                dp = _dot_nt(dom, vg)
                pst, dst = [], []
                for i in range(GROUP):
                    h = GROUP * g + i
                    sl = slice(ATT_BLOCK * i, ATT_BLOCK * (i + 1))
                    p, ps = _swa_probs(s[sl], bias_ref[h], sink_ref[h], valid)
                    delta = jnp.sum(p * dp[sl], axis=-1, keepdims=True)
                    ds = p * (dp[sl] - delta)
                    dbias_ref[h] += ds
                    dsink_ref[pl.ds(h, 1), :] += -jnp.sum(jnp.broadcast_to(ps * delta, (ATT_BLOCK, 128)),
                                                          axis=0, keepdims=True)
                    pst.append(p.astype(BF16))
                    dst.append((ds * scale).astype(BF16))
                pst, dst = jnp.concatenate(pst, axis=0), jnp.concatenate(dst, axis=0)
                dq_ref[rows, pl.ds(256 * g, 256)] = _unstack_group(_dot(dst, kg), lo_q).astype(dq_ref.dtype)
                dkpad[band, lanes] += _dot_tn(dst, qm)
                dvpad[band, lanes] += _dot_tn(pst, dom)
            return carry

        lax.fori_loop(0, nb, blk, 0)
        lo_out = lax.broadcasted_iota(jnp.int32, (1, 128), 1) < HEAD_DIM

        def fold(pad_ref):
            halves = []
            for g in range(N_Q_HEADS // GROUP):
                t = pad_ref[pl.ds(ATT_BLOCK, S), pl.ds(128 * g, 128)]
                halves.append(t + pltpu.roll(t, HEAD_DIM, 1))
            return jnp.where(lo_out, halves[0], halves[1])

        dk_ref[...] = fold(dkpad).astype(dk_ref.dtype)
        dv_ref[...] = fold(dvpad).astype(dv_ref.dtype)

    return pl.pallas_call(
        body, name="swa_bwd", grid=(B,),
        in_specs=[pl.BlockSpec((S, 512), lambda b: (b, 0)),
                  pl.BlockSpec((S, 256), lambda b: (b, 0)),
                  pl.BlockSpec((S, 256), lambda b: (b, 0)),
                  pl.BlockSpec((N_Q_HEADS, ATT_BLOCK, 2 * ATT_BLOCK), lambda b: (0, 0, 0)),
                  pl.BlockSpec(memory_space=pltpu.SMEM),
                  pl.BlockSpec((S, 512), lambda b: (b, 0))],
        out_specs=[pl.BlockSpec((S, 512), lambda b: (b, 0)),
                   pl.BlockSpec((S, 128), lambda b: (b, 0)),
                   pl.BlockSpec((S, 128), lambda b: (b, 0)),
                   pl.BlockSpec((N_Q_HEADS, ATT_BLOCK, 2 * ATT_BLOCK), lambda b: (0, 0, 0)),
                   pl.BlockSpec((N_Q_HEADS, 128), lambda b: (0, 0))],
        out_shape=[jax.ShapeDtypeStruct((T, 512), BF16),
                   jax.ShapeDtypeStruct((T, 128), BF16),
                   jax.ShapeDtypeStruct((T, 128), BF16),
                   jax.ShapeDtypeStruct((N_Q_HEADS, ATT_BLOCK, 2 * ATT_BLOCK), F32),
                   jax.ShapeDtypeStruct((N_Q_HEADS, 128), F32)],
        scratch_shapes=[pltpu.VMEM((S + ATT_BLOCK, 256), BF16), pltpu.VMEM((S + ATT_BLOCK, 256), BF16),
                        pltpu.VMEM((S + ATT_BLOCK, 256), F32), pltpu.VMEM((S + ATT_BLOCK, 256), F32)],
        compiler_params=_params("arbitrary"))(proj, kk2, vv2, bias, sinks, datt)


def _hgrn_gates(z, lb):
    sg = _sigmoid(z)
    f = lb + (1.0 - lb) * sg
    return sg, f, jnp.log(f), 1.0 - f


def _hgrn_consts():
    r = lax.broadcasted_iota(jnp.int32, (CHUNK, CHUNK), 0)
    c = lax.broadcasted_iota(jnp.int32, (CHUNK, CHUNK), 1)
    tril = (r >= c).astype(BF16)
    triu = (r <= c).astype(BF16)
    causal = r >= c
    below = (r // SUB) > (c // SUB)
    inside = ((r // SUB) == (c // SUB)) & causal
    return tril, triu, causal, below, inside, c


def _block_rows(ref, lanes, s):
    rows = []
    for i in range(N_SUB):
        if SUB * i + s < 0:
            rows.append(jnp.zeros((SUB, REC_DIM), F32))
        else:
            rows.append(jnp.broadcast_to(ref[pl.ds(SUB * i + s, 1), lanes], (SUB, REC_DIM)))
    return jnp.concatenate(rows, axis=0)


def _hgrn_offdiag(q, k, bcum, b_ref, lanes):
    eq = jnp.exp(jnp.minimum(bcum - _block_rows(b_ref, lanes, -1), 0.0))
    qe = q * eq
    zero = jnp.zeros((SUB, REC_DIM), F32)
    q_rows, k_cols, eks = [jnp.zeros((SUB, (N_SUB - 1) * REC_DIM), F32)], [], []
    for i in range(1, N_SUB):
        q_rows.append(jnp.concatenate([zero] * (i - 1) + [qe[SUB * i:SUB * (i + 1), :]] + [zero] * (N_SUB - 1 - i),
                                      axis=1))
        p = b_ref[pl.ds(SUB * i - 1, 1), lanes]
        pad = jnp.zeros((CHUNK - SUB * i, REC_DIM), F32)
        ek = jnp.concatenate([jnp.exp(p - b_ref[pl.ds(0, SUB * i), lanes]), pad], axis=0)
        k_cols.append(k * ek)
        eks.append(ek)
    return jnp.concatenate(q_rows, axis=0), jnp.concatenate(k_cols, axis=1), eq, eks


def _hgrn_fwd(proj, lb_param, B, S, comm=None):
    T = B * S
    nc = S // CHUNK
    c_arrays, c_in_specs, c_out_shapes, c_sems = _comm_parts(comm)
    nci, nco = len(c_arrays), len(c_out_shapes)

    def body(*refs):
        q_ref, z_ref, v_ref, lb_ref = refs[:4]
        o_ref, st_ref = refs[4 + nci:6 + nci]
        k_slots, b_slots = refs[6 + nci + nco:8 + nci + nco]
        comm_first, comm_last = _comm_run(comm, (B, REC_HEADS // HGRN_PAIR), refs, 4, 2)
        comm_first()
        tril, _, _, below, inside, col = _hgrn_consts()
        col_s = col & (SUB - 1)

        def chunk(ci, hts, slot):
            k_s, b_s = k_slots.at[slot], b_slots.at[slot]
            r0 = pl.multiple_of(ci * CHUNK, CHUNK)
            lb = _sigmoid(lb_ref[0:1, :] - lb_ref[1:2, :])
            _, _, g_all, k_all = _hgrn_gates(z_ref[pl.ds(r0, CHUNK), :], lb)
            b_all = _sel_left(tril, g_all)
            k_s[...] = k_all
            b_s[...] = b_all
            new = []
            for e, ht in enumerate(hts):
                lanes = pl.ds(REC_DIM * e, REC_DIM)
                cols = slice(REC_DIM * e, REC_DIM * (e + 1))
                q = q_ref[pl.ds(r0, CHUNK), lanes]
                v = v_ref[pl.ds(r0, CHUNK), lanes]
                k, bcum = k_all[:, cols], b_all[:, cols]
                st_ref[e * nc + ci] = ht
                qst, kst, _, _ = _hgrn_offdiag(q, k, bcum, b_s, lanes)
                d = jnp.zeros((CHUNK, CHUNK), F32)
                for s in range(SUB):
                    w = jnp.exp(jnp.minimum(bcum - _block_rows(b_s, lanes, s), 0.0))
                    colv = jnp.sum(q * _block_rows(k_s, lanes, s) * w, axis=-1, keepdims=True)
                    d = jnp.where(col_s == s, colv, d)
                a = jnp.where(below, _dot_nt(qst.astype(BF16), kst.astype(BF16)), 0.0) + jnp.where(inside, d, 0.0)
                vb = v.astype(BF16)
                qb = (q * jnp.exp(bcum)).astype(BF16)
                o_ref[pl.ds(r0, CHUNK), lanes] = _dot(a.astype(BF16), vb) + _dot_nt(qb, ht.astype(BF16))
                b_last = b_s[pl.ds(CHUNK - 1, 1), lanes]
                kb = (k * jnp.exp(b_last - bcum)).astype(BF16)
                new.append(ht * jnp.exp(b_last) + _dot_tn(vb, kb))
            return tuple(new)

        lax.fori_loop(0, nc // 2, lambda i, hts: chunk(2 * i + 1, chunk(2 * i, hts, 0), 1),
                      tuple(jnp.zeros((REC_DIM, REC_DIM), F32) for _ in range(HGRN_PAIR)))
        comm_last()

    hp, wd = REC_HEADS // HGRN_PAIR, HGRN_PAIR * REC_DIM
    cq, cf, ci_ = (c * REC_DIM // wd for c in (COL_RQ, COL_RF, COL_RI))
    return pl.pallas_call(
        body, name="hgrn_fwd", grid=(B, hp),
        in_specs=[pl.BlockSpec((S, wd), lambda b, h: (b, cq + h)),
                  pl.BlockSpec((S, wd), lambda b, h: (b, cf + h)),
                  pl.BlockSpec((S, wd), lambda b, h: (b, ci_ + h)),
                  pl.BlockSpec((2, wd), lambda b, h: (0, h))] + c_in_specs,
        out_specs=[pl.BlockSpec((S, wd), lambda b, h: (b, h)),
                   pl.BlockSpec((HGRN_PAIR * nc, REC_DIM, REC_DIM), lambda b, h: (b * hp + h, 0, 0))] + [ANY] * nco,
        out_shape=[jax.ShapeDtypeStruct((T, 512), F32),
                   jax.ShapeDtypeStruct((B * REC_HEADS * nc, REC_DIM, REC_DIM), F32)] + c_out_shapes,
        scratch_shapes=[pltpu.VMEM((2, CHUNK, wd), F32), pltpu.VMEM((2, CHUNK, wd), F32)] + c_sems,
        compiler_params=_params("arbitrary", "arbitrary"))(proj, proj, proj, lb_param, *c_arrays)


def _hgrn_bwd(proj, lb_param, states, do, B, S, comm=None):
    T = B * S
    nc = S // CHUNK

    c_arrays, c_in_specs, c_out_shapes, c_sems = _comm_parts(comm)
    nci, nco = len(c_arrays), len(c_out_shapes)

    def body(*refs):
        q_ref, z_ref, v_ref, lb_ref, st_ref, do_ref = refs[:6]
        dq_ref, dz_ref, dv_ref, dlb_ref = refs[6 + nci:10 + nci]
        slots = refs[10 + nci + nco:14 + nci + nco]
        comm_first, comm_last = _comm_run(comm, (B, REC_HEADS // HGRN_PAIR), refs, 6, 4)
        comm_first()
        tril, triu, causal, below, inside, col = _hgrn_consts()
        col_s = col & (SUB - 1)
        last_row = lax.broadcasted_iota(jnp.int32, (CHUNK, 1), 0) == CHUNK - 1
        rc = lax.broadcasted_iota(jnp.int32, (CHUNK, SUB * REC_DIM), 0)
        lc = lax.broadcasted_iota(jnp.int32, (CHUNK, SUB * REC_DIM), 1)
        spread = ((rc & (SUB - 1)) == (lc // REC_DIM)).astype(BF16)
        rr = lax.broadcasted_iota(jnp.int32, (CHUNK, SUB * CHUNK), 0)
        cc = lax.broadcasted_iota(jnp.int32, (CHUNK, SUB * CHUNK), 1)
        gather = (((rr // SUB) == ((cc & (CHUNK - 1)) // SUB)) & ((rr & (SUB - 1)) == (cc // CHUNK))).astype(BF16)

        heads = range(HGRN_PAIR)
        cols = [slice(REC_DIM * e, REC_DIM * (e + 1)) for e in heads]
        lanes = [pl.ds(REC_DIM * e, REC_DIM) for e in heads]
        lane_cat = lambda vals: jnp.concatenate(vals, axis=1)
        row_cat = lambda vals: jnp.concatenate(vals, axis=0)

        def chunk(it, carry, slot):
            k_s, b_s, pc_hi, pc_lo = (r.at[slot] for r in slots)
            dhts, dlb = carry
            ci = nc - 1 - it
            r0 = pl.multiple_of(ci * CHUNK, CHUNK)
            rows = pl.ds(r0, CHUNK)
            lb = _sigmoid(lb_ref[0:1, :] - lb_ref[1:2, :])
            sg, f, g_all, k_all = _hgrn_gates(z_ref[rows, :], lb)
            b_all = _sel_left(tril, g_all)
            k_s[...] = k_all
            b_s[...] = b_all
            q_all = q_ref[rows, :]
            das, hd = [], []
            for e in heads:
                vb, dob = v_ref[rows, lanes[e]].astype(BF16), do_ref[rows, lanes[e]].astype(BF16)
                da = jnp.where(causal, _dot_nt(dob, vb), 0.0)
                das.append(jnp.where(inside, da, 0.0))
                hd.append((vb, dob, da))
            da_hi, da_lo = _split2(row_cat(das))
            da_in = _dot(da_hi, spread) + _dot(da_lo, spread)
            ds, dqs = [], []
            for e in heads:
                q, bcum = q_all[:, cols[e]], b_all[:, cols[e]]
                d = jnp.zeros((CHUNK, CHUNK), F32)
                dq = jnp.zeros((CHUNK, REC_DIM), F32)
                for s in range(SUB):
                    w = jnp.exp(jnp.minimum(bcum - _block_rows(b_s, lanes[e], s), 0.0))
                    ks = _block_rows(k_s, lanes[e], s)
                    qw = q * w
                    d = jnp.where(col_s == s, jnp.sum(qw * ks, axis=-1, keepdims=True), d)
                    da_s = da_in[CHUNK * e:CHUNK * (e + 1), REC_DIM * s:REC_DIM * (s + 1)]
                    dq = dq + da_s * ks * w
                    hi, lo = _split2(da_s * qw)
                    pc_hi[pl.ds(CHUNK * s, CHUNK), lanes[e]] = hi
                    pc_lo[pl.ds(CHUNK * s, CHUNK), lanes[e]] = lo
                ds.append(d)
                dqs.append(dq)
            dk_in = _dot(gather, pc_hi[...]) + _dot(gather, pc_lo[...])
            dq_out, dk_out, dv_out, db_out, new_dhts = [], [], [], [], []
            for e in heads:
                q, k, bcum = q_all[:, cols[e]], k_all[:, cols[e]], b_all[:, cols[e]]
                vb, dob, da = hd[e]
                dht, ht = dhts[e], st_ref[e * nc + ci]
                qst, kst, eq, eks = _hgrn_offdiag(q, k, bcum, b_s, lanes[e])
                qst_b, kst_b = qst.astype(BF16), kst.astype(BF16)
                a = jnp.where(below, _dot_nt(qst_b, kst_b), 0.0) + jnp.where(inside, ds[e], 0.0)
                da_off = jnp.where(below, da, 0.0).astype(BF16)
                dqst = _dot(da_off, kst_b)
                dkst = _dot_tn(da_off, qst_b)
                dk = dk_in[:, cols[e]]
                dq_rows = [jnp.zeros((SUB, REC_DIM), F32)]
                for i in range(1, N_SUB):
                    dq_rows.append(dqst[SUB * i:SUB * (i + 1), REC_DIM * (i - 1):REC_DIM * i])
                    dk = dk + dkst[:, REC_DIM * (i - 1):REC_DIM * i] * eks[i - 1]
                dq = dqs[e] + row_cat(dq_rows) * eq
                eb = jnp.exp(bcum)
                b_last = b_s[pl.ds(CHUNK - 1, 1), lanes[e]]
                el = jnp.exp(b_last)
                ekb = jnp.exp(b_last - bcum)
                qb = (q * eb).astype(BF16)
                kb = k * ekb
                dhb = dht.astype(BF16)
                dv_out.append(_dot_tn(a.astype(BF16), dob) + _dot_nt(kb.astype(BF16), dhb))
                dqb = _dot(dob, ht.astype(BF16))
                dkb = _dot(vb, dhb)
                new_dhts.append(dht * el + _dot_tn(dob, qb))
                dq = dq + eb * dqb
                dk = dk + ekb * dkb
                edge = jnp.sum(kb * dkb, axis=0, keepdims=True) + el * jnp.sum(ht * dht, axis=0, keepdims=True)
                db_out.append(q * dq - k * dk + jnp.where(last_row, edge, 0.0))
                dq_out.append(dq)
                dk_out.append(dk)
            dk_all = lane_cat(dk_out)
            db_hi, db_lo = _split2(lane_cat(db_out))
            dg = _dot(triu, db_hi) + _dot(triu, db_lo)
            df = dg / f - dk_all
            dz_ref[rows, :] = (df * (1.0 - lb) * sg * (1.0 - sg)).astype(dz_ref.dtype)
            dq_ref[rows, :] = lane_cat(dq_out).astype(dq_ref.dtype)
            dv_ref[rows, :] = lane_cat(dv_out).astype(dv_ref.dtype)
            return tuple(new_dhts), dlb + jnp.sum(df * (1.0 - sg), axis=0, keepdims=True)

        zero = (tuple(jnp.zeros((REC_DIM, REC_DIM), F32) for _ in heads), jnp.zeros((1, HGRN_PAIR * REC_DIM), F32))
        _, dlb = lax.fori_loop(0, nc // 2, lambda i, c: chunk(2 * i + 1, chunk(2 * i, c, 0), 1), zero)
        lb = _sigmoid(lb_ref[0:1, :] - lb_ref[1:2, :])
        dlb_ref[...] = jnp.broadcast_to(dlb * lb * (1.0 - lb), (8, HGRN_PAIR * REC_DIM))
        comm_last()

    hp, wd = REC_HEADS // HGRN_PAIR, HGRN_PAIR * REC_DIM
    cq, cf, ci_ = (c * REC_DIM // wd for c in (COL_RQ, COL_RF, COL_RI))
    return pl.pallas_call(
        body, name="hgrn_bwd", grid=(B, hp),
        in_specs=[pl.BlockSpec((S, wd), lambda b, h: (b, cq + h)),
                  pl.BlockSpec((S, wd), lambda b, h: (b, cf + h)),
                  pl.BlockSpec((S, wd), lambda b, h: (b, ci_ + h)),
                  pl.BlockSpec((2, wd), lambda b, h: (0, h)),
                  pl.BlockSpec((HGRN_PAIR * nc, REC_DIM, REC_DIM), lambda b, h: (b * hp + h, 0, 0)),
                  pl.BlockSpec((S, wd), lambda b, h: (b, h))] + c_in_specs,
        out_specs=[pl.BlockSpec((S, wd), lambda b, h: (b, h))] * 3
        + [pl.BlockSpec((8, wd), lambda b, h: (b, h))] + [ANY] * nco,
        out_shape=[jax.ShapeDtypeStruct((T, 512), BF16)] * 3 + [jax.ShapeDtypeStruct((B * 8, 512), F32)]
        + c_out_shapes,
        scratch_shapes=[pltpu.VMEM((2, CHUNK, wd), F32)] * 2 + [pltpu.VMEM((2, SUB * CHUNK, wd), BF16)] * 2 + c_sems,
        compiler_params=_params("arbitrary", "arbitrary"))(proj, proj, proj, lb_param, states, do, *c_arrays)


def _rec_gate_fwd(rec, proj, rec_norm):
    T = rec.shape[0]

    def fn(accs, tv, cv):
        return [_rms_hat(tv[0]) * cv[0] * _sigmoid(tv[1])]

    return _tile_call("rec_gate", fn, T, 512, _pick(T, 1024), REC_DIM, tiles=[(rec, 0), (proj, COL_RG)],
                      consts=[rec_norm], outs=[BF16])[0]


def _rec_gate_bwd(dyb, w_rec_proj, rec, proj, rec_norm):
    T = rec.shape[0]

    def fn(accs, tv, cv):
        d, r, rg = accs[0], tv[0], tv[1]
        sg = _sigmoid(rg)
        rn = _rms_hat(r) * cv[0]
        dh, dg = _rms_bwd_vals(d * sg, r, cv[0])
        return [dh, d * rn * sg * (1.0 - sg), dg]

    return _tile_call("rec_gate_bwd", fn, T, 512, _pick(T, 1024), REC_DIM, pairs=[(dyb, 0, w_rec_proj, "nt")],
                      tiles=[(rec, 0), (proj, COL_RG)], consts=[rec_norm], outs=[F32, BF16], parts=1)


def _mix_out_fwd(att, recn, proj, w_att_proj, w_rec_proj, w_out, h1, g_next):
    T = att.shape[0]
    tn = 256

    def merge(accs, tv, cv):
        ya, yb = accs
        return [ya, yb, _sigmoid(tv[0]) * ya + _sigmoid(tv[1]) * yb]

    ya, yb, merged = _tile_call(
        "merge", merge, T, D_MODEL, _pick(T, 1024), tn,
        pairs=[(att, 0, w_att_proj, "nn"), (recn, 0, w_rec_proj, "nn")],
        tiles=[(proj, COL_GA * 128 // tn), (proj, COL_GB * 128 // tn)], outs=[BF16] * 3)

    def res(accs, tv, cv):
        h2 = tv[0] + accs[0]
        return [h2, _rms_hat(h2) * cv[0]]

    h2, n2 = _tile_call("mix_out", res, T, D_MODEL, _pick(T, 512), D_MODEL, pairs=[(merged, 0, w_out, "nn")],
                        tiles=[(h1, 0)], consts=[g_next], outs=[F32, BF16])
    return h2, n2, (ya, yb, merged)


GATHER_FIRST = ("w_ffn1_in",)
GATHER_MIX = ("w_ffn1_out", "w_in")
GATHER_PROJ = ("w_att_proj", "w_rec_proj", "w_out")
GATHER_LAST = ("w_ffn2_in", "w_ffn2_out", "w_ple_gate", "w_ple_proj")
SCATTER_LATE = ("w_ple_gate", "w_ple_proj", "w_ffn2_in", "w_ffn2_out")
SCATTER_MIX = ("w_out", "w_att_proj", "w_rec_proj", "w_in")
SCATTER_LAST = ("w_ffn1_in", "w_ffn1_out")


def _local_step(x, p, tgt, w, mine16, cc, me_chip, B, S):
    T = B * S
    w = dict(w)
    g_ffn1, g_mix, g_ffn2, g_ple = w["norm_ffn1"], w["norm_mix"], w["norm_ffn2"], w["norm_ple"]
    g_fin = w["norm_final"].reshape(1, D_MODEL)
    grads, part, from_chips = {}, {}, {}

    def gather(names):
        return _gather_comm([mine16[n] for n in names])

    def place(names, got):
        for n, g in zip(names, got):
            full = lax.dynamic_update_index_in_dim(g, mine16[n], me_chip, 0)
            w[n] = full if n in ("w_ffn1_in", "w_ffn2_in") else _natural(n, full)

    def swap(names):
        return _swap_comm([grads[n][1] for n in names])

    def after_swap(names, from_sib):
        for n, fs in zip(names, from_sib):
            part[n] = _add_sibling("rs_add_sib_" + n, grads[n][0], fs, cc)
        return _scatter_comm([part[n][1] for n in names])

    def scattered(names, got):
        for n, g in zip(names, got):
            from_chips[n] = g

    def ffn1_in_weight(got):
        place(GATHER_FIRST, got)
        return w["w_ffn1_in"]

    def ffn1_out_weight(got):
        place(GATHER_MIX, got)
        return w["w_ffn1_out"]

    h1, u, sv1, got_proj = _ffn_fwd("ffn1", x, g_ffn1, None, None, g_mix, comm_norm=gather(GATHER_FIRST),
                                    w_in_of=ffn1_in_weight, comm_in=gather(GATHER_MIX),
                                    comm_out=gather(GATHER_PROJ), w_out_of=ffn1_out_weight)
    place(GATHER_PROJ, got_proj)

    def ident(accs, tv, cv):
        return [accs[0]]

    proj = _tile_call("in_proj", ident, T, IN_W, _pick(T, 512), IN_W // 2, pairs=[(u, 0, w["w_in"], "nn")],
                      outs=[F32], j_outer=True)[0]
    onehot = jnp.asarray(_t5_onehot())
    bias = _small_mm("t5_bias", w["rel_bias"].T, onehot.astype(BF16), "right")
    bias = bias.reshape(N_Q_HEADS, ATT_BLOCK, 2 * ATT_BLOCK)
    sinks = w["attn_sinks"].reshape(N_Q_HEADS)
    kk2, vv2 = _kv_layouts(proj)
    att = _swa_fwd(proj, kk2, vv2, bias, sinks, B, S)
    rec, states, *got_last = _hgrn_fwd(proj, w["lb_param"], B, S, comm=gather(GATHER_LAST))
    place(GATHER_LAST, got_last)
    recn = _rec_gate_fwd(rec, proj, w["rec_norm"])
    h2, n2, (ya, yb, merged) = _mix_out_fwd(att, recn, proj, w["w_att_proj"], w["w_rec_proj"], w["w_out"], h1,
                                            g_ffn2)
    h3, n3, sv2, _ = _ffn_fwd("ffn2", h2, g_ffn2, w["w_ffn2_in"], w["w_ffn2_out"], g_ple, n=n2)

    def ple(accs, tv, cv):
        gate = _sigmoid(accs[0])
        return [gate, accs[1], tv[0] + gate * accs[1]]

    gate_p, pp, h4 = _tile_call(
        "ple", ple, T, D_MODEL, _pick(T, 512), D_MODEL,
        pairs=[(n3, 0, w["w_ple_gate"], "nn"), (p, 0, w["w_ple_proj"], "nn")], tiles=[(h3, 0)],
        outs=[BF16, BF16, F32])

    def head(accs, tv, cv):
        h, t, gt, ppv = tv[0], tv[1], tv[2].astype(F32), tv[3].astype(F32)
        err = _rms_hat(h) * cv[0] - t
        dh, dg = _rms_bwd_vals(err * (1.0 / D_MODEL), h, cv[0])
        return [dh, dh * ppv * gt * (1.0 - gt), dh * gt, _group8(err * err), dg]

    dh4, dzg, dpp, loss_p, dg_fin = _tile_call(
        "loss_head", head, T, D_MODEL, _pick(T, 256), D_MODEL,
        tiles=[(h4, 0), (tgt, 0), (gate_p, 0), (pp, 0)], consts=[g_fin], outs=[F32, BF16, BF16], parts=2)
    grads["norm_final"] = dg_fin

    grads["w_ple_gate"] = _mm_tn_rows("ple_dwg", n3, dzg)
    grads["w_ple_proj"] = _mm_tn_cols("ple_dwp", p, dpp)

    def dnorm(accs, tv, cv):
        dh, dg = _rms_bwd_vals(accs[0], tv[0], cv[0])
        dh = tv[1] + dh
        return [dh, 0.5 * dh, dg]

    dh3, df3, grads["norm_ple"] = _tile_call(
        "ple_dnorm", dnorm, T, D_MODEL, _pick(T, 512), D_MODEL, pairs=[(dzg, 0, w["w_ple_gate"], "nt")],
        tiles=[(h3, 0), (dh4, 0)], consts=[g_ple], outs=[F32, BF16], parts=1)

    def swap_late(dw_in, dw_out):
        grads["w_ffn2_in"], grads["w_ffn2_out"] = dw_in, dw_out
        return swap(SCATTER_LATE)

    dh2, dh2b, grads["norm_ffn2"], _, _, _, from_sib = _ffn_bwd(
        "ffn2b", dh3, df3, h2, g_ffn2, w["w_ffn2_in"], w["w_ffn2_out"], sv2, comm_last=swap_late)
    scatter_late = after_swap(SCATTER_LATE, from_sib)

    grads["w_out"] = _mm_tn_rows("mix_dwout", merged, dh2b)
    tn = 256

    def dmerge(accs, tv, cv):
        dm = accs[0]
        sa, sb = _sigmoid(tv[0]), _sigmoid(tv[1])
        yav, ybv = tv[2].astype(F32), tv[3].astype(F32)
        return [dm * sa, dm * sb, dm * yav * sa * (1.0 - sa), dm * ybv * sb * (1.0 - sb)]

    dya, dyb, dga, dgb = _tile_call(
        "mix_dmerge", dmerge, T, D_MODEL, _pick(T, 1024), tn, pairs=[(dh2b, 0, w["w_out"], "nt")],
        tiles=[(proj, COL_GA * 128 // tn), (proj, COL_GB * 128 // tn), (ya, 0), (yb, 0)], outs=[BF16] * 4)
    grads["w_att_proj"] = _mm_tn_cols("mix_dwatt", att, dya)
    grads["w_rec_proj"] = _mm_tn_cols("mix_dwrec", recn, dyb)

    datt = _tile_call("mix_datt", ident, T, 512, _pick(T, 1024), 512, pairs=[(dya, 0, w["w_att_proj"], "nt")],
                      outs=[BF16])[0]
    drec, drg, grads["rec_norm"] = _rec_gate_bwd(dyb, w["w_rec_proj"], rec, proj, w["rec_norm"])

    drq, drf, dri, dlb, *got = _hgrn_bwd(proj, w["lb_param"], states, drec, B, S, comm=scatter_late)
    scattered(SCATTER_LATE, got)
    grads["lb_param"] = dlb
    daq, dak, dav, dbias, dsink = _swa_bwd(proj, kk2, vv2, bias, sinks, datt, B, S)
    grads["attn_sinks"] = dsink
    grads["rel_bias"] = _small_mm("t5_dbias", dbias.reshape(N_Q_HEADS, -1), onehot.T.astype(BF16), "right")
    dproj = jnp.concatenate([daq, dak, dav, drq, drf, dri, drg, dga, dgb], axis=1)
    tk = _pick(T, 2048, 128)
    w_in_shard = IN_W // N_CHIPS
    half_d = D_MODEL // 2
    gw32, gw16 = _mm_tn("mix_dwin", (2, 2, T // tk),
                        (u, (tk, half_d), lambda i, j, k: (k, i)), (dproj, (tk, IN_W // 2), lambda i, j, k: (k, j)),
                        _grad_pair((D_MODEL, IN_W), (half_d, IN_W // 2), lambda i, j, k: (i, j)))
    to_sh = lambda t: t.reshape(D_MODEL, N_CHIPS, w_in_shard).transpose(1, 0, 2)
    grads["w_in"] = (to_sh(gw32), to_sh(gw16))

    def dnorm_mix(accs, tv, cv):
        dh, dg = _rms_bwd_vals(accs[0], tv[0], cv[0])
        dh = tv[1] + dh
        return [dh, 0.5 * dh, dg]

    dh1, df1, grads["norm_mix"], *from_sib = _tile_call(
        "mix_dnorm", dnorm_mix, T, D_MODEL, _pick(T, 512), D_MODEL, pairs=[(dproj, 0, w["w_in"], "nt")],
        tiles=[(h1, 0), (dh2, 0)], consts=[g_mix], outs=[F32, BF16], parts=1, comm=swap(SCATTER_MIX))
    scatter_mix = after_swap(SCATTER_MIX, from_sib)

    def scatter_last(dw_in, dw_out):
        grads["w_ffn1_in"], grads["w_ffn1_out"] = dw_in, dw_out
        return after_swap(SCATTER_LAST, _run_comm("rs_sibling_last", swap(SCATTER_LAST)))

    dx, _, grads["norm_ffn1"], _, _, got, got_last = _ffn_bwd(
        "ffn1b", dh1, df1, x, g_ffn1, w["w_ffn1_in"], w["w_ffn1_out"], sv1, comm=scatter_mix, comm_last=scatter_last)
    scattered(SCATTER_MIX, got)
    scattered(SCATTER_LAST, got_last)
    return loss_p, dx, grads, part, from_chips


def _place():
    x, y, c = lax.axis_index("x"), lax.axis_index("y"), lax.axis_index("c")
    return x, y, c


def _other_chips(x, y):
    return [(1 - x, y, 2 * (1 - x) + y), (x, 1 - y, 2 * x + 1 - y), (1 - x, 1 - y, 2 * (1 - x) + 1 - y)]


def _half_rows(ref_3d, chip, h, rows):
    return ref_3d.at[chip, pl.ds(h * rows, rows), :]


def _gather_comm(ws):
    nw = len(ws)

    def parts(w_refs, out_refs, send_sems, recv_sems):
        x, y, c = _place()
        me = 2 * x + y
        chips = _other_chips(x, y)

        def copy(i, k, chip, h, to, src=None):
            half = ws[i].shape[0] // 2
            dst = _half_rows(out_refs[i], chip, h, half)
            return pltpu.make_async_remote_copy(
                src_ref=dst if src is None else src, dst_ref=dst,
                send_sem=send_sems.at[6 * i + k], recv_sem=recv_sems.at[6 * i + k], device_id=to, device_id_type=MESH)

        def first():
            out = []
            for i in range(nw):
                half = ws[i].shape[0] // 2
                out += [copy(i, j, me, c, (cx, cy, c), src=w_refs[i].at[pl.ds(c * half, half), :])
                        for j, (cx, cy, _) in enumerate(chips)]
            return out

        return copy, first, chips, c, (x, y, 1 - c)

    def start(*refs):
        _, first, _, _, _ = parts(*refs)
        for cp in first():
            cp.start()

    def finish(*refs):
        copy, first, chips, c, sibling = parts(*refs)
        passed = []
        for i in range(nw):
            for j, (cx, cy, ci) in enumerate(chips):
                copy(i, j, ci, c, (cx, cy, c)).wait_recv()
                fw = copy(i, 3 + j, ci, c, sibling)
                fw.start()
                passed.append(fw)
        for i in range(nw):
            for j, (_, _, ci) in enumerate(chips):
                copy(i, 3 + j, ci, 1 - c, sibling).wait_recv()
        for cp in first() + passed:
            cp.wait_send()

    return _Comm(list(ws), [jax.ShapeDtypeStruct((N_CHIPS,) + w.shape, w.dtype) for w in ws], 6 * nw, start, finish)


def _scatter_comm(ps):
    nw = len(ps)

    def copies(p_refs, out_refs, send_sems, recv_sems):
        x, y, c = _place()
        cps = []
        for i in range(nw):
            for j, (cx, cy, ci) in enumerate(_other_chips(x, y)):
                cps.append(pltpu.make_async_remote_copy(
                    src_ref=p_refs[i].at[ci], dst_ref=out_refs[i].at[j], send_sem=send_sems.at[3 * i + j],
                    recv_sem=recv_sems.at[3 * i + j], device_id=(cx, cy, c), device_id_type=MESH))
        return cps

    def start(*refs):
        for cp in copies(*refs):
            cp.start()

    def finish(*refs):
        for cp in copies(*refs):
            cp.wait()

    return _Comm(list(ps), [jax.ShapeDtypeStruct((3,) + p.shape[1:], p.dtype) for p in ps], 3 * nw, start, finish)


def _swap_comm(gs):
    nw = len(gs)

    def copies(g_refs, out_refs, send_sems, recv_sems):
        x, y, c = _place()
        cps = []
        for i in range(nw):
            half = gs[i].shape[1] // 2
            cps.append(pltpu.make_async_remote_copy(
                src_ref=g_refs[i].at[:, pl.ds((1 - c) * half, half), :], dst_ref=out_refs[i],
                send_sem=send_sems.at[i], recv_sem=recv_sems.at[i], device_id=(x, y, 1 - c), device_id_type=MESH))
        return cps

    def start(*refs):
        for cp in copies(*refs):
            cp.start()

    def finish(*refs):
        for cp in copies(*refs):
            cp.wait()

    return _Comm(list(gs), [jax.ShapeDtypeStruct((N_CHIPS, g.shape[1] // 2, g.shape[2]), g.dtype) for g in gs],
                 nw, start, finish)


def _run_comm(name, comm):
    nci, nco = len(comm.ins), len(comm.out_shapes)

    def body(*refs):
        cin, cout, send_sems, recv_sems = refs[:nci], refs[nci:nci + nco], refs[-2], refs[-1]
        comm.start(cin, cout, send_sems, recv_sems)
        comm.finish(cin, cout, send_sems, recv_sems)

    return pl.pallas_call(
        body, name=name, in_specs=[ANY] * nci, out_specs=[ANY] * nco, out_shape=list(comm.out_shapes),
        scratch_shapes=[pltpu.SemaphoreType.DMA((comm.n_sems,)), pltpu.SemaphoreType.DMA((comm.n_sems,))],
    )(*comm.ins)


def _join_halves(name, ss):
    nw = len(ss)

    def body(*refs):
        s_refs, out_refs, send_sems, recv_sems = refs[:nw], refs[nw:2 * nw], refs[2 * nw], refs[2 * nw + 1]
        x, y, c = _place()
        cps = [pltpu.make_async_remote_copy(
            src_ref=s_refs[i], dst_ref=out_refs[i], send_sem=send_sems.at[i], recv_sem=recv_sems.at[i],
            device_id=(x, y, 1 - c), device_id_type=MESH) for i in range(nw)]
        for cp in cps:
            cp.start()
        for cp in cps:
            cp.wait()

    return pl.pallas_call(
        body, name=name, in_specs=[ANY] * nw, out_specs=[ANY] * nw,
        out_shape=[jax.ShapeDtypeStruct(s.shape, s.dtype) for s in ss],
        scratch_shapes=[pltpu.SemaphoreType.DMA((nw,)), pltpu.SemaphoreType.DMA((nw,))],
    )(*ss)


def _allreduce_small(sp):
    def body(s_ref, out_ref, slots, send_sems, recv_sems):
        x, y, c = _place()
        me = 4 * x + 2 * y + c
        slots[me] = s_ref[...]
        cps = []
        for r in range(1, N_DEV):
            px, py, pc = x ^ (r >> 2), y ^ ((r >> 1) & 1), c ^ (r & 1)
            cps.append(pltpu.make_async_remote_copy(
                src_ref=s_ref, dst_ref=slots.at[me], send_sem=send_sems.at[r - 1], recv_sem=recv_sems.at[r - 1],
                device_id=(px, py, pc), device_id_type=MESH))
        for cp in cps:
            cp.start()
        for r in range(1, N_DEV):
            px, py, pc = x ^ (r >> 2), y ^ ((r >> 1) & 1), c ^ (r & 1)
            pltpu.make_async_remote_copy(
                src_ref=s_ref, dst_ref=slots.at[4 * px + 2 * py + pc], send_sem=send_sems.at[r - 1],
                recv_sem=recv_sems.at[r - 1], device_id=(px, py, pc), device_id_type=MESH).wait_recv()
        for cp in cps:
            cp.wait_send()
        acc = slots[0]
        for d in range(1, N_DEV):
            acc = acc + slots[d]
        out_ref[...] = acc

    return pl.pallas_call(
        body, name="allreduce_small",
        in_specs=[pl.BlockSpec(memory_space=pltpu.VMEM)], out_specs=pl.BlockSpec(memory_space=pltpu.VMEM),
        out_shape=jax.ShapeDtypeStruct(sp.shape, F32),
        scratch_shapes=[pltpu.VMEM((N_DEV,) + sp.shape, F32), pltpu.SemaphoreType.DMA((N_DEV - 1,)),
                        pltpu.SemaphoreType.DMA((N_DEV - 1,))],
    )(sp)


def _scalar(v):
    return jnp.reshape(v, (1,)).astype(jnp.int32)


def _row_tile(h, dtype_mult=16):
    return _pick(h, 256, dtype_mult)


def _add_sibling(name, g32, from_sib, c):
    _, r, n = g32.shape
    h = r // 2
    th = _row_tile(h)
    nt = h // th

    def body(c_ref, g_ref, s_ref, o32_ref, o16_ref):
        s = g_ref[...] + s_ref[...].astype(F32)
        o32_ref[...] = s
        o16_ref[...] = s.astype(BF16)

    blk = (None, th, n)
    return pl.pallas_call(
        body, name=name,
        grid_spec=pltpu.PrefetchScalarGridSpec(
            num_scalar_prefetch=1, grid=(N_CHIPS, nt),
            in_specs=[pl.BlockSpec(blk, lambda k, t, c_ref: (k, c_ref[0] * nt + t, 0)),
                      pl.BlockSpec(blk, lambda k, t, c_ref: (k, t, 0))],
            out_specs=[pl.BlockSpec(blk, lambda k, t, c_ref: (k, t, 0))] * 2),
        out_shape=[jax.ShapeDtypeStruct((N_CHIPS, h, n), F32), jax.ShapeDtypeStruct((N_CHIPS, h, n), BF16)],
        compiler_params=_params("arbitrary", "arbitrary"))(_scalar(c), g32, from_sib)


def _add_chips(name, p32, from_chips, me_chip):
    _, h, n = p32.shape
    th = _row_tile(h)

    def body(m_ref, p_ref, a_ref, b_ref, c_ref, o_ref):
        o_ref[...] = p_ref[...] + a_ref[...].astype(F32) + b_ref[...].astype(F32) + c_ref[...].astype(F32)

    blk = (None, th, n)
    return pl.pallas_call(
        body, name=name,
        grid_spec=pltpu.PrefetchScalarGridSpec(
            num_scalar_prefetch=1, grid=(h // th,),
            in_specs=[pl.BlockSpec(blk, lambda t, m_ref: (m_ref[0], t, 0))]
            + [pl.BlockSpec(blk, lambda t, m_ref, j=j: (j, t, 0)) for j in range(3)],
            out_specs=pl.BlockSpec((th, n), lambda t, m_ref: (t, 0))),
        out_shape=jax.ShapeDtypeStruct((h, n), F32),
        compiler_params=_params("arbitrary"))(_scalar(me_chip), p32, from_chips, from_chips, from_chips)


def _adamw_vals(w, g, m, v):
    m = ADAM_B1 * m + (1.0 - ADAM_B1) * g
    v = ADAM_B2 * v + (1.0 - ADAM_B2) * (g * g)
    m_hat = m / (1.0 - ADAM_B1 ** ADAM_STEP)
    v_hat = v / (1.0 - ADAM_B2 ** ADAM_STEP)
    delta = -ADAM_LR * (m_hat / (jnp.sqrt(v_hat) + ADAM_EPS) + ADAM_WD * w)
    return delta, m, v


def _adamw_halves(name, w, m, v, g_mine, g_sib, c):
    r, n = w.shape
    h = r // 2
    th = _row_tile(h, 8)
    nt = h // th

    def body(c_ref, w_ref, m_ref, v_ref, a_ref, b_ref, g_ref, d_ref, nm_ref, nv_ref):
        mine = (pl.program_id(0) // nt) == c_ref[0]
        g = jnp.where(mine, a_ref[...], b_ref[...])
        d, nm, nv = _adamw_vals(w_ref[...], g, m_ref[...], v_ref[...])
        g_ref[...] = g
        d_ref[...] = d
        nm_ref[...] = nm
        nv_ref[...] = nv

    full = pl.BlockSpec((th, n), lambda t, c_ref: (t, 0))
    part = pl.BlockSpec((th, n), lambda t, c_ref: (t % nt, 0))
    return pl.pallas_call(
        body, name=name,
        grid_spec=pltpu.PrefetchScalarGridSpec(
            num_scalar_prefetch=1, grid=(2 * nt,), in_specs=[full, full, full, part, part], out_specs=[full] * 4),
        out_shape=[jax.ShapeDtypeStruct((r, n), F32)] * 4,
        compiler_params=_params("arbitrary"))(_scalar(c), w, m, v, g_mine, g_sib)


def _adamw(name, w, g, m, v):
    R, W = w.shape

    def fn(accs, tv, cv):
        return list(_adamw_vals(*tv))

    return _tile_call(name, fn, R, W, _pick(R, 256), W, tiles=[(w, 0), (g, 0), (m, 0), (v, 0)], outs=[F32] * 3)


SMALL_LAYOUT = (("rel_bias", 2, 256), ("lb_param", 8, 1024), ("norm_ffn1", 8, 1024), ("norm_mix", 8, 1024),
                ("attn_sinks", 1, 8), ("rec_norm", 1, 128), ("norm_ffn2", 8, 1024), ("norm_ple", 8, 1024),
                ("norm_final", 8, 1024), ("loss", 8, 1024))


def _pack_small(vals):
    rows = []
    for name, nrows, n in SMALL_LAYOUT:
        flat = vals[name].reshape(-1)
        flat = jnp.pad(flat, (0, nrows * 128 - n))
        rows.append(flat.reshape(nrows, 128))
    packed = jnp.concatenate(rows, axis=0)
    return jnp.pad(packed, ((0, SMALL_ROWS - packed.shape[0]), (0, 0)))


def _unpack_small(packed, shapes):
    out, r = {}, 0
    for name, nrows, n in SMALL_LAYOUT:
        out[name] = packed[r:r + nrows].reshape(-1)[:n].reshape(shapes[name])
        r += nrows
    return out


def _natural(name, s):
    if name in COL_SHARDED:
        return s.transpose(1, 0, 2).reshape(s.shape[1], -1)
    return s.reshape(-1, s.shape[2])


def kernel(x, p, rel_bias, lb_param, norm_ffn1, w_ffn1_in, w_ffn1_out, norm_mix, w_in, attn_sinks, rec_norm, w_att_proj, w_rec_proj, w_out, norm_ffn2, w_ffn2_in, w_ffn2_out, norm_ple, w_ple_gate, w_ple_proj, norm_final, loss_target, m_rel_bias, m_lb_param, m_norm_ffn1, m_w_ffn1_in, m_w_ffn1_out, m_norm_mix, m_w_in, m_attn_sinks, m_rec_norm, m_w_att_proj, m_w_rec_proj, m_w_out, m_norm_ffn2, m_w_ffn2_in, m_w_ffn2_out, m_norm_ple, m_w_ple_gate, m_w_ple_proj, m_norm_final, v_rel_bias, v_lb_param, v_norm_ffn1, v_w_ffn1_in, v_w_ffn1_out, v_norm_mix, v_w_in, v_attn_sinks, v_rec_norm, v_w_att_proj, v_w_rec_proj, v_w_out, v_norm_ffn2, v_w_ffn2_in, v_w_ffn2_out, v_norm_ple, v_w_ple_gate, v_w_ple_proj, v_norm_final):
    args = dict(locals())
    wsh = {n: args[n] for n in WEIGHTS}
    B, S = x.shape[0], x.shape[1]
    T = B * S
    cx, cy, cc = _place()
    me_chip = 2 * cx + cy

    mine16 = {n: wsh[n][0].astype(BF16) for n in BIG}
    loss_p, dx, grads, part, from_chips = _local_step(
        x.reshape(T, D_MODEL), p.reshape(T, PLE_DIM), loss_target.reshape(T, D_MODEL),
        {n: wsh[n] for n in SMALL}, mine16, cc, me_chip, B, S)

    s_mine = [_add_chips("rs_add_chips_" + n, part[n][0], from_chips[n], me_chip) for n in BIG]
    s_sib = _join_halves("rs_join", s_mine)

    small_vals = {
        "rel_bias": grads["rel_bias"].T,
        "lb_param": jnp.concatenate([_colsum("dlb_sum", grads["lb_param"]),
                                     -_colsum("dlb_sum2", grads["lb_param"])], axis=0) / 8.0,
        "attn_sinks": grads["attn_sinks"][:, 0],
        "rec_norm": _colsum("drn_sum", grads["rec_norm"]).reshape(REC_HEADS, REC_DIM).sum(axis=0),
        "loss": _colsum("loss_sum", loss_p),
    }
    for n in ("norm_ffn1", "norm_mix", "norm_ffn2", "norm_ple", "norm_final"):
        small_vals[n] = _colsum(n + "_sum", grads[n])
    red = _allreduce_small(_pack_small(small_vals))
    small_shapes = {n: wsh[n].shape for n in SMALL}
    small_shapes["loss"] = (D_MODEL,)
    small = _unpack_small(red, small_shapes)
    loss = 0.5 * jnp.sum(small["loss"]) / D_MODEL

    out_g, out_d, out_m, out_v = {}, {}, {}, {}
    for n, gm, gs in zip(BIG, s_mine, s_sib):
        res = _adamw_halves("adamw_" + n, wsh[n][0], args["m_" + n][0], args["v_" + n][0], gm, gs, cc)
        out_g[n], out_d[n], out_m[n], out_v[n] = (t[None] for t in res)
    sw = _pack_small({**{n: wsh[n] for n in SMALL}, "loss": jnp.zeros((D_MODEL,), F32)})
    sm = _pack_small({**{n: args["m_" + n] for n in SMALL}, "loss": jnp.zeros((D_MODEL,), F32)})
    sv = _pack_small({**{n: args["v_" + n] for n in SMALL}, "loss": jnp.ones((D_MODEL,), F32)})
    sd, snm, snv = _adamw("adamw_small", sw, red, sm, sv)
    ud, um, uv = (_unpack_small(t, small_shapes) for t in (sd, snm, snv))
    for n in SMALL:
        out_g[n], out_d[n], out_m[n], out_v[n] = small[n], ud[n], um[n], uv[n]

    return (loss, dx.reshape(B, S, D_MODEL), *[out_g[n] for n in WEIGHTS], *[out_d[n] for n in WEIGHTS],
            *[out_m[n] for n in WEIGHTS], *[out_v[n] for n in WEIGHTS])
```

```python
import numpy as np
import jax
import jax.numpy as jnp
from jax import lax
from jax.experimental import pallas as pl
from jax.experimental.pallas import tpu as pltpu

F32 = jnp.float32
BF16 = jnp.bfloat16
MESH = pl.DeviceIdType.MESH

D_MODEL = 1024
D_FF = 2816
FF_SHARD = 2 * D_FF // 4
HEAD_DIM = 64
N_Q_HEADS = 8
ATT_BLOCK = 128
N_BUCKETS = 32
MAX_DISTANCE = 128
REC_HEADS = 4
REC_DIM = 128
PLE_DIM = 256
EPS = 1e-6
IN_W = 4864
COL_AQ, COL_AK, COL_AV, COL_RQ, COL_RF, COL_RI, COL_RG, COL_GA, COL_GB = 0, 4, 5, 6, 10, 14, 18, 22, 30

CHUNK = 64
SUB = 8
N_SUB = CHUNK // SUB
HGRN_PAIR = 2

ADAM_LR, ADAM_B1, ADAM_B2, ADAM_EPS, ADAM_WD, ADAM_STEP = 0.001, 0.9, 0.999, 1e-08, 0.01, 10

V7X_VMEM_LIMIT = 56 * 1024 * 1024
N_CHIPS = 4
N_DEV = 8

BIG = ("w_ffn1_in", "w_ffn1_out", "w_in", "w_att_proj", "w_rec_proj", "w_out",
       "w_ffn2_in", "w_ffn2_out", "w_ple_gate", "w_ple_proj")
COL_SHARDED = ("w_ffn1_in", "w_in", "w_att_proj", "w_rec_proj", "w_ffn2_in", "w_ple_proj")
WEIGHTS = ("rel_bias", "lb_param", "norm_ffn1", "w_ffn1_in", "w_ffn1_out", "norm_mix", "w_in", "attn_sinks",
           "rec_norm", "w_att_proj", "w_rec_proj", "w_out", "norm_ffn2", "w_ffn2_in", "w_ffn2_out", "norm_ple",
           "w_ple_gate", "w_ple_proj", "norm_final")
SMALL = tuple(n for n in WEIGHTS if n not in BIG)
SMALL_ROWS = 64


def _params(*sem):
    return pltpu.CompilerParams(dimension_semantics=sem, vmem_limit_bytes=V7X_VMEM_LIMIT)


def _pick(n, cap, mult=8):
    if n <= cap:
        return n
    for t in range(cap - cap % mult, 0, -mult):
        if n % t == 0:
            return t
    raise ValueError((n, cap, mult))


def _dot(a, b):
    return jnp.dot(a, b, preferred_element_type=F32)


def _dot_nt(a, b):
    return lax.dot_general(a, b, (((1,), (1,)), ((), ())), preferred_element_type=F32)


def _dot_tn(a, b):
    return lax.dot_general(a, b, (((0,), (0,)), ((), ())), preferred_element_type=F32)


def _split3(x):
    hi = x.astype(BF16)
    r = x - hi.astype(F32)
    mid = r.astype(BF16)
    lo = (r - mid.astype(F32)).astype(BF16)
    return hi, mid, lo


def _split2(x):
    hi = x.astype(BF16)
    return hi, (x - hi.astype(F32)).astype(BF16)


def _sel_left(sel_bf16, x):
    hi, mid, lo = _split3(x)
    return _dot(sel_bf16, hi) + _dot(sel_bf16, mid) + _dot(sel_bf16, lo)


def _sel_right(x, sel_bf16):
    hi, mid, lo = _split3(x)
    return _dot(hi, sel_bf16) + _dot(mid, sel_bf16) + _dot(lo, sel_bf16)


def _sigmoid(x):
    return 0.5 * jnp.tanh(0.5 * x) + 0.5


def _group8(x):
    r, w = x.shape
    return x.reshape(r // 8, 8, w).sum(axis=0)


class _Comm:
    def __init__(self, ins, out_shapes, n_sems, start, finish):
        self.ins, self.out_shapes, self.n_sems, self.start, self.finish = ins, out_shapes, n_sems, start, finish


ANY = pl.BlockSpec(memory_space=pl.ANY)


def _comm_parts(comm):
    if comm is None:
        return [], [], [], []
    sems = [pltpu.SemaphoreType.DMA((comm.n_sems,)), pltpu.SemaphoreType.DMA((comm.n_sems,))]
    return list(comm.ins), [ANY] * len(comm.ins), list(comm.out_shapes), sems


def _comm_run(comm, grid, refs, n_in, n_out):
    if comm is None:
        return (lambda: None), (lambda: None)
    nci, nco = len(comm.ins), len(comm.out_shapes)
    cin = refs[n_in:n_in + nci]
    cout = refs[n_in + nci + n_out:n_in + nci + n_out + nco]
    send_sems, recv_sems = refs[-2], refs[-1]
    ids = [pl.program_id(d) for d in range(len(grid))]
    is_first = ids[0] == 0
    is_last = ids[0] == grid[0] - 1
    for d in range(1, len(grid)):
        is_first = is_first & (ids[d] == 0)
        is_last = is_last & (ids[d] == grid[d] - 1)

    def first():
        @pl.when(is_first)
        def _():
            comm.start(cin, cout, send_sems, recv_sems)

    def last():
        @pl.when(is_last)
        def _():
            comm.finish(cin, cout, send_sems, recv_sems)

    return first, last


def _call(name, fn, grid, ins, outs, pairs=(), comm=None, j_outer=False):
    in_pair = {i for p in pairs for i in p[:2]}
    n_in, n_out = len(ins), len(outs)
    c_arrays, c_in_specs, c_out_shapes, c_sems = _comm_parts(comm)

    def body(*refs):
        first, last = _comm_run(comm, grid, refs, n_in, n_out)
        first()
        accs = []
        for ia, ib, kind in pairs:
            a, b = refs[ia][...].astype(BF16), refs[ib][...].astype(BF16)
            accs.append(_dot(a, b) if kind == "nn" else _dot_nt(a, b))
        vals = [refs[i][...] for i in range(n_in) if i not in in_pair]
        res = fn(accs, vals)
        out_refs = refs[n_in + len(c_arrays):n_in + len(c_arrays) + n_out]
        assert len(res) == len(out_refs), (name, len(res), len(out_refs))
        for o_ref, val in zip(out_refs, res):
            o_ref[...] = val.astype(o_ref.dtype)
        last()

    if j_outer:
        grid = (grid[1], grid[0])
        swap = lambda im: (lambda j, i: im(i, j))
        ins = [(a, blk, swap(im)) for a, blk, im in ins]
        outs = [(shp, dt, blk, swap(im)) for shp, dt, blk, im in outs]

    return pl.pallas_call(
        body, name=name, grid=grid,
        in_specs=[pl.BlockSpec(blk, im) for _, blk, im in ins] + c_in_specs,
        out_specs=[pl.BlockSpec(blk, im) for _, _, blk, im in outs] + [ANY] * len(c_out_shapes),
        out_shape=[jax.ShapeDtypeStruct(shp, dt) for shp, dt, _, _ in outs] + c_out_shapes,
        scratch_shapes=c_sems,
        compiler_params=_params(*(["arbitrary"] * len(grid))))(*[a for a, _, _ in ins], *c_arrays)


def _tile_call(name, fn, M, N, tm, tn, *, pairs=(), tiles=(), consts=(), outs=(), parts=0, comm=None,
               j_outer=False):
    gi, gj = M // tm, N // tn
    assert gi * tm == M and gj * tn == N, (name, M, N, tm, tn)
    ins, prs = [], []
    for a, a_col, b, kind in pairs:
        K = b.shape[0] if kind == "nn" else b.shape[1]
        ins.append((a, (tm, K), lambda i, j, c=a_col: (i, c)))
        if kind == "nn":
            ins.append((b, (K, tn), lambda i, j: (0, j)))
        else:
            ins.append((b, (tn, K), lambda i, j: (j, 0)))
        prs.append((len(ins) - 2, len(ins) - 1, kind))
    for arr, off in tiles:
        ins.append((arr, (tm, tn), lambda i, j, o=off: (i, j + o)))
    for arr in consts:
        ins.append((arr, arr.shape, lambda i, j: (0, 0)))
    out_l = [((M, N), dt, (tm, tn), lambda i, j: (i, j)) for dt in outs]
    out_l += [((gi * 8, N), F32, (8, tn), lambda i, j: (i, j))] * parts
    nt = len(tiles)

    def wrapped(accs, vals):
        return fn(accs, vals[:nt], vals[nt:])

    return _call(name, wrapped, (gi, gj), ins, out_l, prs, comm=comm, j_outer=j_outer)


def _mm_tn(name, grid, a_in, b_in, outs):
    nk = grid[2]
    tm = [d for d in a_in[1] if d is not None][1]
    tn = [d for d in b_in[1] if d is not None][1]

    def body(a_ref, b_ref, *rest):
        out_refs, acc_ref = rest[:-1], rest[-1]
        k = pl.program_id(2)

        @pl.when(k == 0)
        def _():
            acc_ref[...] = jnp.zeros_like(acc_ref)

        acc_ref[...] += _dot_tn(a_ref[...].astype(BF16), b_ref[...].astype(BF16))

        @pl.when(k == nk - 1)
        def _():
            for o_ref in out_refs:
                o_ref[...] = acc_ref[...].astype(o_ref.dtype)

    return pl.pallas_call(
        body, name=name, grid=grid,
        in_specs=[pl.BlockSpec(a_in[1], a_in[2]), pl.BlockSpec(b_in[1], b_in[2])],
        out_specs=[pl.BlockSpec(blk, im) for _, _, blk, im in outs],
        out_shape=[jax.ShapeDtypeStruct(shp, dt) for shp, dt, _, _ in outs],
        scratch_shapes=[pltpu.VMEM((tm, tn), F32)],
        compiler_params=_params("arbitrary", "arbitrary", "arbitrary"))(a_in[0], b_in[0])


def _grad_pair(shape, block, imap):
    return [(shape, F32, block, imap), (shape, BF16, block, imap)]


def _mm_tn_rows(name, a, b, tk=2048):
    T, a_w = a.shape
    b_w = b.shape[1]
    tm = _pick(a_w, 1408, 128)
    tk = _pick(T, tk, 128)
    g32, g16 = _mm_tn(name, (a_w // tm, 1, T // tk),
                      (a, (tk, tm), lambda i, j, k: (k, i)), (b, (tk, b_w), lambda i, j, k: (k, 0)),
                      _grad_pair((a_w, b_w), (tm, b_w), lambda i, j, k: (i, 0)))
    shp = (N_CHIPS, a_w // N_CHIPS, b_w)
    return g32.reshape(shp), g16.reshape(shp)


def _mm_tn_cols(name, a, b, tk=2048):
    T, a_w = a.shape
    n = b.shape[1] // N_CHIPS
    tk = _pick(T, tk, 128)
    return _mm_tn(name, (1, N_CHIPS, T // tk),
                  (a, (tk, a_w), lambda i, j, k: (k, 0)), (b, (tk, n), lambda i, j, k: (k, j)),
                  _grad_pair((N_CHIPS, a_w, n), (None, a_w, n), lambda i, j, k: (j, 0, 0)))


def _colsum(name, x):
    def body(x_ref, o_ref):
        o_ref[...] = jnp.sum(x_ref[...], axis=0, keepdims=True)
    return pl.pallas_call(body, name=name, out_shape=jax.ShapeDtypeStruct((1, x.shape[1]), F32))(x)


def _rms_hat(h):
    return h * lax.rsqrt(jnp.mean(h * h, axis=-1, keepdims=True) + EPS)


def _rms_bwd_vals(dn, h, g):
    r = lax.rsqrt(jnp.mean(h * h, axis=-1, keepdims=True) + EPS)
    nh = h * r
    gd = dn * g
    dh = r * (gd - nh * jnp.mean(gd * nh, axis=-1, keepdims=True))
    return dh, _group8(dn * nh)


def _rms_fwd(name, h, g, tm=512, comm=None):
    T = h.shape[0]

    def fn(accs, tv, cv):
        return [_rms_hat(tv[0]) * cv[0]]

    return _tile_call(name, fn, T, D_MODEL, _pick(T, tm), D_MODEL, tiles=[(h, 0)], consts=[g], outs=[BF16],
                      comm=comm)


def _ffn_fwd(tag, h, g, w_in, w_out, g_next, n=None, comm_norm=None, w_in_of=None, comm_in=None, comm_out=None,
             w_out_of=None):
    T = h.shape[0]
    if n is None:
        n, *got_norm = _rms_fwd(tag + "_norm", h, g, comm=comm_norm)
        if w_in_of is not None:
            w_in = w_in_of(got_norm)
    tm = _pick(T, 1024)
    wblk = (None, D_MODEL, FF_SHARD)

    def act(accs, vals):
        gate, up = accs
        return [gate, up, gate * _sigmoid(gate) * up]

    tile = lambda: ((T, D_FF), BF16, (tm, FF_SHARD), lambda i, j: (i, j))
    gate, up, a, *got_in = _call(
        tag + "_in", act, (T // tm, 2),
        [(n, (tm, D_MODEL), lambda i, j: (i, 0)),
         (w_in, wblk, lambda i, j: (j, 0, 0)), (w_in, wblk, lambda i, j: (j + 2, 0, 0))],
        [tile(), tile(), tile()], pairs=[(0, 1, "nn"), (0, 2, "nn")], comm=comm_in, j_outer=True)

    def res(accs, tv, cv):
        h_new = tv[0] + 0.5 * accs[0]
        return [h_new, _rms_hat(h_new) * cv[0]]

    if w_out_of is not None:
        w_out = w_out_of(got_in)
    h_new, n_next, *got_out = _tile_call(
        tag + "_out", res, T, D_MODEL, _pick(T, 512), D_MODEL, pairs=[(a, 0, w_out, "nn")], tiles=[(h, 0)],
        consts=[g_next], outs=[F32, BF16], comm=comm_out)
    return h_new, n_next, (n, gate, up, a), got_out


def _ffn_bwd(tag, dh_out, df, h, g, w_in, w_out, saved, comm=None, comm_last=None):
    T = h.shape[0]
    n, gate, up, a = saved
    tm = _pick(T, 512)

    def dact(accs, vals):
        da = accs[0]
        gt, u = vals[0].astype(F32), vals[1].astype(F32)
        sg = _sigmoid(gt)
        silu = gt * sg
        return [jnp.stack([(da * u * (sg + silu * (1.0 - sg))).astype(BF16), (da * silu).astype(BF16)])]

    dz, *got = _call(
        tag + "_dact", dact, (T // tm, 2),
        [(df, (tm, D_MODEL), lambda i, j: (i, 0)), (w_out, (FF_SHARD, D_MODEL), lambda i, j: (j, 0)),
         (gate, (tm, FF_SHARD), lambda i, j: (i, j)), (up, (tm, FF_SHARD), lambda i, j: (i, j))],
        [((2, T, D_FF), BF16, (2, tm, FF_SHARD), lambda i, j: (0, i, j))], pairs=[(0, 1, "nt")], comm=comm,
        j_outer=True)
    dw_out = _mm_tn_rows(tag + "_dwout", a, df)
    tk = _pick(T, 2048, 128)
    dw_in = _mm_tn(tag + "_dwin", (1, N_CHIPS, T // tk),
                   (n, (tk, D_MODEL), lambda i, j, k: (k, 0)),
                   (dz, (None, tk, FF_SHARD), lambda i, j, k: (j // 2, k, j % 2)),
                   _grad_pair((N_CHIPS, D_MODEL, FF_SHARD), (None, D_MODEL, FF_SHARD), lambda i, j, k: (j, 0, 0)))

    def dnorm(accs, vals):
        dn = accs[0] + accs[1] + accs[2] + accs[3]
        dh, dg = _rms_bwd_vals(dn, vals[0], vals[2])
        dh = vals[1] + dh
        return [dh, dh, dg]

    tm2 = _pick(T, 512)
    ins = [(dz, (None, tm2, FF_SHARD), lambda i, j, s=s: (s // 2, i, s % 2)) for s in range(N_CHIPS)]
    ins += [(w_in, (None, D_MODEL, FF_SHARD), lambda i, j, s=s: (s, 0, 0)) for s in range(N_CHIPS)]
    ins += [(h, (tm2, D_MODEL), lambda i, j: (i, 0)), (dh_out, (tm2, D_MODEL), lambda i, j: (i, 0)),
            (g, g.shape, lambda i, j: (0, 0))]
    dh, dh16, dg, *got_last = _call(
        tag + "_dnorm", dnorm, (T // tm2, 1), ins,
        [((T, D_MODEL), F32, (tm2, D_MODEL), lambda i, j: (i, 0)),
         ((T, D_MODEL), BF16, (tm2, D_MODEL), lambda i, j: (i, 0)),
         ((T // tm2 * 8, D_MODEL), F32, (8, D_MODEL), lambda i, j: (i, 0))],
        pairs=[(s, N_CHIPS + s, "nt") for s in range(N_CHIPS)],
        comm=None if comm_last is None else comm_last(dw_in, dw_out))
    return dh, dh16, dg, dw_in, dw_out, got, got_last


def _t5_onehot():
    qi = np.arange(ATT_BLOCK)[:, None] + ATT_BLOCK
    kj = np.arange(2 * ATT_BLOCK)[None, :]
    nn = np.maximum(qi - kj, 0)
    max_exact = N_BUCKETS // 2
    large = max_exact + (np.log(np.maximum(nn, 1) / max_exact) / np.log(MAX_DISTANCE / max_exact)
                         * (N_BUCKETS - max_exact)).astype(np.int32)
    large = np.minimum(large, N_BUCKETS - 1)
    bucket = np.where(nn < max_exact, nn, large).astype(np.int32).reshape(-1)
    return (bucket[None, :] == np.arange(N_BUCKETS)[:, None]).astype(np.float32)


def _small_mm(name, a, b, sel):
    def body(a_ref, b_ref, o_ref):
        if sel == "right":
            o_ref[...] = _sel_right(a_ref[...], b_ref[...])
        else:
            o_ref[...] = _sel_left(a_ref[...], b_ref[...])
    return pl.pallas_call(body, name=name, out_shape=jax.ShapeDtypeStruct((a.shape[0], b.shape[1]), F32),
                          compiler_params=pltpu.CompilerParams(vmem_limit_bytes=V7X_VMEM_LIMIT))(a, b)


def _dup_heads(t):
    a, b = t[:, :HEAD_DIM], t[:, HEAD_DIM:]
    return jnp.concatenate([a, a, b, b], axis=1)


def _kv_layouts(proj):
    T = proj.shape[0]

    def fn(accs, tv, cv):
        return [tv[0], tv[1]]

    k, v = _tile_call("kv_cast", fn, T, 128, _pick(T, 1024), 128, tiles=[(proj, COL_AK), (proj, COL_AV)],
                      outs=[BF16, BF16])
    return _dup_heads(k), _dup_heads(v)


def _swa_masks():
    row = lax.broadcasted_iota(jnp.int32, (ATT_BLOCK, 2 * ATT_BLOCK), 0)
    col = lax.broadcasted_iota(jnp.int32, (ATT_BLOCK, 2 * ATT_BLOCK), 1)
    dist = ATT_BLOCK + row - col
    return (dist >= 0) & (dist < ATT_BLOCK), col


GROUP = 4


def _stack_group(blk, lo_q):
    zero = jnp.zeros_like(blk[:, :128])
    rows = []
    for pair in range(GROUP // 2):
        pb = blk[:, 128 * pair:128 * (pair + 1)]
        rows += [jnp.where(lo_q, pb, zero), jnp.where(lo_q, zero, pb)]
    return jnp.concatenate(rows, axis=0)


def _unstack_group(st, lo_q):
    pairs = [jnp.where(lo_q, st[256 * pair:256 * pair + 128], st[256 * pair + 128:256 * (pair + 1)])
             for pair in range(GROUP // 2)]
    return jnp.concatenate(pairs, axis=1)


def _swa_probs(s, bias_h, sink, valid):
    s = jnp.where(valid, s * (HEAD_DIM ** -0.5) + bias_h, -jnp.inf)
    m = jnp.maximum(jnp.max(s, axis=-1, keepdims=True), sink)
    e = jnp.exp(s - m)
    es = jnp.exp(sink - m)
    den = jnp.sum(e, axis=-1, keepdims=True) + es
    return e / den, es / den


def _swa_fwd(proj, kk2, vv2, bias, sinks, B, S):
    T = B * S
    nb = S // ATT_BLOCK

    def body(q_ref, k_ref, v_ref, bias_ref, sink_ref, o_ref, kpad, vpad):
        zeros = jnp.zeros((ATT_BLOCK, 256), BF16)
        kpad[pl.ds(0, ATT_BLOCK), :] = zeros
        vpad[pl.ds(0, ATT_BLOCK), :] = zeros
        kpad[pl.ds(ATT_BLOCK, S), :] = k_ref[...]
        vpad[pl.ds(ATT_BLOCK, S), :] = v_ref[...]
        valid0, col = _swa_masks()
        lo_q = lax.broadcasted_iota(jnp.int32, (1, 128), 1) < HEAD_DIM

        def blk(n, carry):
            r0 = pl.multiple_of(n * ATT_BLOCK, ATT_BLOCK)
            rows = pl.ds(r0, ATT_BLOCK)
            valid = valid0 & ((n > 0) | (col >= ATT_BLOCK))
            for g in range(N_Q_HEADS // GROUP):
                lanes = pl.ds(128 * g, 128)
                kg = kpad[pl.ds(r0, 2 * ATT_BLOCK), lanes]
                vg = vpad[pl.ds(r0, 2 * ATT_BLOCK), lanes]
                qm = _stack_group(q_ref[rows, pl.ds(256 * g, 256)].astype(BF16), lo_q)
                s = _dot_nt(qm, kg)
                ps = []
                for i in range(GROUP):
                    h = GROUP * g + i
                    p, _ = _swa_probs(s[ATT_BLOCK * i:ATT_BLOCK * (i + 1)], bias_ref[h], sink_ref[h], valid)
                    ps.append(p.astype(BF16))
                o = _dot(jnp.concatenate(ps, axis=0), vg)
                o_ref[rows, pl.ds(256 * g, 256)] = _unstack_group(o, lo_q).astype(o_ref.dtype)
            return carry

        if nb % 2 == 0:
            lax.fori_loop(0, nb // 2, lambda i, c: blk(2 * i + 1, blk(2 * i, c)), 0)
        else:
            lax.fori_loop(0, nb, blk, 0)

    return pl.pallas_call(
        body, name="swa_fwd", grid=(B,),
        in_specs=[pl.BlockSpec((S, 512), lambda b: (b, 0)),
                  pl.BlockSpec((S, 256), lambda b: (b, 0)),
                  pl.BlockSpec((S, 256), lambda b: (b, 0)),
                  pl.BlockSpec((N_Q_HEADS, ATT_BLOCK, 2 * ATT_BLOCK), lambda b: (0, 0, 0)),
                  pl.BlockSpec(memory_space=pltpu.SMEM)],
        out_specs=pl.BlockSpec((S, 512), lambda b: (b, 0)),
        out_shape=jax.ShapeDtypeStruct((T, 512), BF16),
        scratch_shapes=[pltpu.VMEM((S + ATT_BLOCK, 256), BF16), pltpu.VMEM((S + ATT_BLOCK, 256), BF16)],
        compiler_params=_params("arbitrary"))(proj, kk2, vv2, bias, sinks)


def _swa_bwd(proj, kk2, vv2, bias, sinks, datt, B, S):
    T = B * S
    nb = S // ATT_BLOCK

    def body(q_ref, k_ref, v_ref, bias_ref, sink_ref, do_ref, dq_ref, dk_ref, dv_ref, dbias_ref, dsink_ref,
             kpad, vpad, dkpad, dvpad):
        b = pl.program_id(0)

        @pl.when(b == 0)
        def _():
            dbias_ref[...] = jnp.zeros_like(dbias_ref)
            dsink_ref[...] = jnp.zeros_like(dsink_ref)

        zeros = jnp.zeros((ATT_BLOCK, 256), BF16)
        kpad[pl.ds(0, ATT_BLOCK), :] = zeros
        vpad[pl.ds(0, ATT_BLOCK), :] = zeros
        kpad[pl.ds(ATT_BLOCK, S), :] = k_ref[...]
        vpad[pl.ds(ATT_BLOCK, S), :] = v_ref[...]
        dkpad[...] = jnp.zeros_like(dkpad)
        dvpad[...] = jnp.zeros_like(dvpad)
        valid0, col = _swa_masks()
        lo_q = lax.broadcasted_iota(jnp.int32, (1, 128), 1) < HEAD_DIM
        scale = HEAD_DIM ** -0.5

        def blk(n, carry):
            r0 = pl.multiple_of(n * ATT_BLOCK, ATT_BLOCK)
            rows = pl.ds(r0, ATT_BLOCK)
            band = pl.ds(r0, 2 * ATT_BLOCK)
            valid = valid0 & ((n > 0) | (col >= ATT_BLOCK))
            for g in range(N_Q_HEADS // GROUP):
                lanes = pl.ds(128 * g, 128)
                kg = kpad[band, lanes]
                vg = vpad[band, lanes]
                qm = _stack_group(q_ref[rows, pl.ds(256 * g, 256)].astype(BF16), lo_q)
                dom = _stack_group(do_ref[rows, pl.ds(256 * g, 256)], lo_q)
                s = _dot_nt(qm, kg)
                dp = _dot_nt(dom, vg)
                pst, dst = [], []
                for i in range(GROUP):
                    h = GROUP * g + i
                    sl = slice(ATT_BLOCK * i, ATT_BLOCK * (i + 1))
                    p, ps = _swa_probs(s[sl], bias_ref[h], sink_ref[h], valid)
                    delta = jnp.sum(p * dp[sl], axis=-1, keepdims=True)
                    ds = p * (dp[sl] - delta)
                    dbias_ref[h] += ds
                    dsink_ref[pl.ds(h, 1), :] += -jnp.sum(jnp.broadcast_to(ps * delta, (ATT_BLOCK, 128)),
                                                          axis=0, keepdims=True)
                    pst.append(p.astype(BF16))
                    dst.append((ds * scale).astype(BF16))
                pst, dst = jnp.concatenate(pst, axis=0), jnp.concatenate(dst, axis=0)
                dq_ref[rows, pl.ds(256 * g, 256)] = _unstack_group(_dot(dst, kg), lo_q).astype(dq_ref.dtype)
                dkpad[band, lanes] += _dot_tn(dst, qm)
                dvpad[band, lanes] += _dot_tn(pst, dom)
            return carry

        if nb % 2 == 0:
            lax.fori_loop(0, nb // 2, lambda i, c: blk(2 * i + 1, blk(2 * i, c)), 0)
        else:
            lax.fori_loop(0, nb, blk, 0)
        lo_out = lax.broadcasted_iota(jnp.int32, (1, 128), 1) < HEAD_DIM

        def fold(pad_ref):
            halves = []
            for g in range(N_Q_HEADS // GROUP):
                t = pad_ref[pl.ds(ATT_BLOCK, S), pl.ds(128 * g, 128)]
                halves.append(t + pltpu.roll(t, HEAD_DIM, 1))
            return jnp.where(lo_out, halves[0], halves[1])

        dk_ref[...] = fold(dkpad).astype(dk_ref.dtype)
        dv_ref[...] = fold(dvpad).astype(dv_ref.dtype)

    return pl.pallas_call(
        body, name="swa_bwd", grid=(B,),
        in_specs=[pl.BlockSpec((S, 512), lambda b: (b, 0)),
                  pl.BlockSpec((S, 256), lambda b: (b, 0)),
                  pl.BlockSpec((S, 256), lambda b: (b, 0)),
                  pl.BlockSpec((N_Q_HEADS, ATT_BLOCK, 2 * ATT_BLOCK), lambda b: (0, 0, 0)),
                  pl.BlockSpec(memory_space=pltpu.SMEM),
                  pl.BlockSpec((S, 512), lambda b: (b, 0))],
        out_specs=[pl.BlockSpec((S, 512), lambda b: (b, 0)),
                   pl.BlockSpec((S, 128), lambda b: (b, 0)),
                   pl.BlockSpec((S, 128), lambda b: (b, 0)),
                   pl.BlockSpec((N_Q_HEADS, ATT_BLOCK, 2 * ATT_BLOCK), lambda b: (0, 0, 0)),
                   pl.BlockSpec((N_Q_HEADS, 128), lambda b: (0, 0))],
        out_shape=[jax.ShapeDtypeStruct((T, 512), BF16),
                   jax.ShapeDtypeStruct((T, 128), BF16),
                   jax.ShapeDtypeStruct((T, 128), BF16),
                   jax.ShapeDtypeStruct((N_Q_HEADS, ATT_BLOCK, 2 * ATT_BLOCK), F32),
                   jax.ShapeDtypeStruct((N_Q_HEADS, 128), F32)],
        scratch_shapes=[pltpu.VMEM((S + ATT_BLOCK, 256), BF16), pltpu.VMEM((S + ATT_BLOCK, 256), BF16),
                        pltpu.VMEM((S + ATT_BLOCK, 256), F32), pltpu.VMEM((S + ATT_BLOCK, 256), F32)],
        compiler_params=_params("arbitrary"))(proj, kk2, vv2, bias, sinks, datt)


def _hgrn_gates(z, lb):
    sg = _sigmoid(z)
    f = lb + (1.0 - lb) * sg
    return sg, f, jnp.log(f), 1.0 - f


def _hgrn_consts():
    r = lax.broadcasted_iota(jnp.int32, (CHUNK, CHUNK), 0)
    c = lax.broadcasted_iota(jnp.int32, (CHUNK, CHUNK), 1)
    tril = (r >= c).astype(BF16)
    triu = (r <= c).astype(BF16)
    causal = r >= c
    below = (r // SUB) > (c // SUB)
    inside = ((r // SUB) == (c // SUB)) & causal
    return tril, triu, causal, below, inside, c


def _block_rows(ref, lanes, s):
    rows = []
    for i in range(N_SUB):
        if SUB * i + s < 0:
            rows.append(jnp.zeros((SUB, REC_DIM), F32))
        else:
            rows.append(jnp.broadcast_to(ref[pl.ds(SUB * i + s, 1), lanes], (SUB, REC_DIM)))
    return jnp.concatenate(rows, axis=0)


def _hgrn_offdiag(q, k, bcum, b_ref, lanes):
    eq = jnp.exp(jnp.minimum(bcum - _block_rows(b_ref, lanes, -1), 0.0))
    qe = q * eq
    zero = jnp.zeros((SUB, REC_DIM), F32)
    q_rows, k_cols, eks = [jnp.zeros((SUB, (N_SUB - 1) * REC_DIM), F32)], [], []
    for i in range(1, N_SUB):
        q_rows.append(jnp.concatenate([zero] * (i - 1) + [qe[SUB * i:SUB * (i + 1), :]] + [zero] * (N_SUB - 1 - i),
                                      axis=1))
        p = b_ref[pl.ds(SUB * i - 1, 1), lanes]
        pad = jnp.zeros((CHUNK - SUB * i, REC_DIM), F32)
        ek = jnp.concatenate([jnp.exp(p - b_ref[pl.ds(0, SUB * i), lanes]), pad], axis=0)
        k_cols.append(k * ek)
        eks.append(ek)
    return jnp.concatenate(q_rows, axis=0), jnp.concatenate(k_cols, axis=1), eq, eks


def _hgrn_fwd(proj, lb_param, B, S, comm=None):
    T = B * S
    nc = S // CHUNK
    fwd_unroll = 4 if nc % 4 == 0 else 2
    c_arrays, c_in_specs, c_out_shapes, c_sems = _comm_parts(comm)
    nci, nco = len(c_arrays), len(c_out_shapes)

    def body(*refs):
        q_ref, z_ref, v_ref, lb_ref = refs[:4]
        o_ref, st_ref = refs[4 + nci:6 + nci]
        k_slots, b_slots = refs[6 + nci + nco:8 + nci + nco]
        comm_first, comm_last = _comm_run(comm, (B, REC_HEADS // HGRN_PAIR), refs, 4, 2)
        comm_first()
        tril, _, _, below, inside, col = _hgrn_consts()
        col_s = col & (SUB - 1)

        def chunk(ci, hts, slot):
            k_s, b_s = k_slots.at[slot], b_slots.at[slot]
            r0 = pl.multiple_of(ci * CHUNK, CHUNK)
            lb = _sigmoid(lb_ref[0:1, :] - lb_ref[1:2, :])
            _, _, g_all, k_all = _hgrn_gates(z_ref[pl.ds(r0, CHUNK), :], lb)
            b_all = _sel_left(tril, g_all)
            k_s[...] = k_all
            b_s[...] = b_all
            new = []
            for e, ht in enumerate(hts):
                lanes = pl.ds(REC_DIM * e, REC_DIM)
                cols = slice(REC_DIM * e, REC_DIM * (e + 1))
                q = q_ref[pl.ds(r0, CHUNK), lanes]
                v = v_ref[pl.ds(r0, CHUNK), lanes]
                k, bcum = k_all[:, cols], b_all[:, cols]
                st_ref[e * nc + ci] = ht
                qst, kst, _, _ = _hgrn_offdiag(q, k, bcum, b_s, lanes)
                d = jnp.zeros((CHUNK, CHUNK), F32)
                for s in range(SUB):
                    w = jnp.exp(jnp.minimum(bcum - _block_rows(b_s, lanes, s), 0.0))
                    colv = jnp.sum(q * _block_rows(k_s, lanes, s) * w, axis=-1, keepdims=True)
                    d = jnp.where(col_s == s, colv, d)
                a = jnp.where(below, _dot_nt(qst.astype(BF16), kst.astype(BF16)), 0.0) + jnp.where(inside, d, 0.0)
                vb = v.astype(BF16)
                qb = (q * jnp.exp(bcum)).astype(BF16)
                o_ref[pl.ds(r0, CHUNK), lanes] = _dot(a.astype(BF16), vb) + _dot_nt(qb, ht.astype(BF16))
                b_last = b_s[pl.ds(CHUNK - 1, 1), lanes]
                kb = (k * jnp.exp(b_last - bcum)).astype(BF16)
                new.append(ht * jnp.exp(b_last) + _dot_tn(vb, kb))
            return tuple(new)

        def trip(i, hts):
            for u in range(fwd_unroll):
                hts = chunk(fwd_unroll * i + u, hts, u)
            return hts

        lax.fori_loop(0, nc // fwd_unroll, trip, tuple(jnp.zeros((REC_DIM, REC_DIM), F32) for _ in range(HGRN_PAIR)))
        comm_last()

    hp, wd = REC_HEADS // HGRN_PAIR, HGRN_PAIR * REC_DIM
    cq, cf, ci_ = (c * REC_DIM // wd for c in (COL_RQ, COL_RF, COL_RI))
    return pl.pallas_call(
        body, name="hgrn_fwd", grid=(B, hp),
        in_specs=[pl.BlockSpec((S, wd), lambda b, h: (b, cq + h)),
                  pl.BlockSpec((S, wd), lambda b, h: (b, cf + h)),
                  pl.BlockSpec((S, wd), lambda b, h: (b, ci_ + h)),
                  pl.BlockSpec((2, wd), lambda b, h: (0, h))] + c_in_specs,
        out_specs=[pl.BlockSpec((S, wd), lambda b, h: (b, h)),
                   pl.BlockSpec((HGRN_PAIR * nc, REC_DIM, REC_DIM), lambda b, h: (b * hp + h, 0, 0))] + [ANY] * nco,
        out_shape=[jax.ShapeDtypeStruct((T, 512), F32),
                   jax.ShapeDtypeStruct((B * REC_HEADS * nc, REC_DIM, REC_DIM), F32)] + c_out_shapes,
        scratch_shapes=[pltpu.VMEM((fwd_unroll, CHUNK, wd), F32), pltpu.VMEM((fwd_unroll, CHUNK, wd), F32)] + c_sems,
        compiler_params=_params("arbitrary", "arbitrary"))(proj, proj, proj, lb_param, *c_arrays)


def _hgrn_bwd(proj, lb_param, states, do, B, S, comm=None):
    T = B * S
    nc = S // CHUNK
    bwd_unroll = 4 if nc % 4 == 0 else 2
    c_arrays, c_in_specs, c_out_shapes, c_sems = _comm_parts(comm)
    nci, nco = len(c_arrays), len(c_out_shapes)

    def body(*refs):
        q_ref, z_ref, v_ref, lb_ref, st_ref, do_ref = refs[:6]
        dq_ref, dz_ref, dv_ref, dlb_ref = refs[6 + nci:10 + nci]
        slots = refs[10 + nci + nco:14 + nci + nco]
        comm_first, comm_last = _comm_run(comm, (B, REC_HEADS // HGRN_PAIR), refs, 6, 4)
        comm_first()
        tril, triu, causal, below, inside, col = _hgrn_consts()
        col_s = col & (SUB - 1)
        last_row = lax.broadcasted_iota(jnp.int32, (CHUNK, 1), 0) == CHUNK - 1
        rc = lax.broadcasted_iota(jnp.int32, (CHUNK, SUB * REC_DIM), 0)
        lc = lax.broadcasted_iota(jnp.int32, (CHUNK, SUB * REC_DIM), 1)
        spread = ((rc & (SUB - 1)) == (lc // REC_DIM)).astype(BF16)
        rr = lax.broadcasted_iota(jnp.int32, (CHUNK, SUB * CHUNK), 0)
        cc = lax.broadcasted_iota(jnp.int32, (CHUNK, SUB * CHUNK), 1)
        gather = (((rr // SUB) == ((cc & (CHUNK - 1)) // SUB)) & ((rr & (SUB - 1)) == (cc // CHUNK))).astype(BF16)

        heads = range(HGRN_PAIR)
        cols = [slice(REC_DIM * e, REC_DIM * (e + 1)) for e in heads]
        lanes = [pl.ds(REC_DIM * e, REC_DIM) for e in heads]
        lane_cat = lambda vals: jnp.concatenate(vals, axis=1)
        row_cat = lambda vals: jnp.concatenate(vals, axis=0)

        def chunk(it, carry, slot):
            k_s, b_s, pc_hi, pc_lo = (r.at[slot] for r in slots)
            dhts, dlb = carry
            ci = nc - 1 - it
            r0 = pl.multiple_of(ci * CHUNK, CHUNK)
            rows = pl.ds(r0, CHUNK)
            lb = _sigmoid(lb_ref[0:1, :] - lb_ref[1:2, :])
            sg, f, g_all, k_all = _hgrn_gates(z_ref[rows, :], lb)
            b_all = _sel_left(tril, g_all)
            k_s[...] = k_all
            b_s[...] = b_all
            q_all = q_ref[rows, :]
            das, hd = [], []
            for e in heads:
                vb, dob = v_ref[rows, lanes[e]].astype(BF16), do_ref[rows, lanes[e]].astype(BF16)
                da = jnp.where(causal, _dot_nt(dob, vb), 0.0)
                das.append(jnp.where(inside, da, 0.0))
                hd.append((vb, dob, da))
            da_hi, da_lo = _split2(row_cat(das))
            da_in = _dot(da_hi, spread) + _dot(da_lo, spread)
            ds, dqs = [], []
            for e in heads:
                q, bcum = q_all[:, cols[e]], b_all[:, cols[e]]
                d = jnp.zeros((CHUNK, CHUNK), F32)
                dq = jnp.zeros((CHUNK, REC_DIM), F32)
                for s in range(SUB):
                    w = jnp.exp(jnp.minimum(bcum - _block_rows(b_s, lanes[e], s), 0.0))
                    ks = _block_rows(k_s, lanes[e], s)
                    qw = q * w
                    d = jnp.where(col_s == s, jnp.sum(qw * ks, axis=-1, keepdims=True), d)
                    da_s = da_in[CHUNK * e:CHUNK * (e + 1), REC_DIM * s:REC_DIM * (s + 1)]
                    dq = dq + da_s * ks * w
                    hi, lo = _split2(da_s * qw)
                    pc_hi[pl.ds(CHUNK * s, CHUNK), lanes[e]] = hi
                    pc_lo[pl.ds(CHUNK * s, CHUNK), lanes[e]] = lo
                ds.append(d)
                dqs.append(dq)
            dk_in = _dot(gather, pc_hi[...]) + _dot(gather, pc_lo[...])
            dq_out, dk_out, dv_out, db_out, new_dhts = [], [], [], [], []
            for e in heads:
                q, k, bcum = q_all[:, cols[e]], k_all[:, cols[e]], b_all[:, cols[e]]
                vb, dob, da = hd[e]
                dht, ht = dhts[e], st_ref[e * nc + ci]
                qst, kst, eq, eks = _hgrn_offdiag(q, k, bcum, b_s, lanes[e])
                qst_b, kst_b = qst.astype(BF16), kst.astype(BF16)
                a = jnp.where(below, _dot_nt(qst_b, kst_b), 0.0) + jnp.where(inside, ds[e], 0.0)
                da_off = jnp.where(below, da, 0.0).astype(BF16)
                dqst = _dot(da_off, kst_b)
                dkst = _dot_tn(da_off, qst_b)
                dk = dk_in[:, cols[e]]
                dq_rows = [jnp.zeros((SUB, REC_DIM), F32)]
                for i in range(1, N_SUB):
                    dq_rows.append(dqst[SUB * i:SUB * (i + 1), REC_DIM * (i - 1):REC_DIM * i])
                    dk = dk + dkst[:, REC_DIM * (i - 1):REC_DIM * i] * eks[i - 1]
                dq = dqs[e] + row_cat(dq_rows) * eq
                eb = jnp.exp(bcum)
                b_last = b_s[pl.ds(CHUNK - 1, 1), lanes[e]]
                el = jnp.exp(b_last)
                ekb = jnp.exp(b_last - bcum)
                qb = (q * eb).astype(BF16)
                kb = k * ekb
                dhb = dht.astype(BF16)
                dv_out.append(_dot_tn(a.astype(BF16), dob) + _dot_nt(kb.astype(BF16), dhb))
                dqb = _dot(dob, ht.astype(BF16))
                dkb = _dot(vb, dhb)
                new_dhts.append(dht * el + _dot_tn(dob, qb))
                dq = dq + eb * dqb
                dk = dk + ekb * dkb
                edge = jnp.sum(kb * dkb, axis=0, keepdims=True) + el * jnp.sum(ht * dht, axis=0, keepdims=True)
                db_out.append(q * dq - k * dk + jnp.where(last_row, edge, 0.0))
                dq_out.append(dq)
                dk_out.append(dk)
            dk_all = lane_cat(dk_out)
            db_hi, db_lo = _split2(lane_cat(db_out))
            dg = _dot(triu, db_hi) + _dot(triu, db_lo)
            df = dg / f - dk_all
            dz_ref[rows, :] = (df * (1.0 - lb) * sg * (1.0 - sg)).astype(dz_ref.dtype)
            dq_ref[rows, :] = lane_cat(dq_out).astype(dq_ref.dtype)
            dv_ref[rows, :] = lane_cat(dv_out).astype(dv_ref.dtype)
            return tuple(new_dhts), dlb + jnp.sum(df * (1.0 - sg), axis=0, keepdims=True)

        zero = (tuple(jnp.zeros((REC_DIM, REC_DIM), F32) for _ in heads), jnp.zeros((1, HGRN_PAIR * REC_DIM), F32))
        def trip(i, carry):
            for u in range(bwd_unroll):
                carry = chunk(bwd_unroll * i + u, carry, u)
            return carry

        _, dlb = lax.fori_loop(0, nc // bwd_unroll, trip, zero)
        lb = _sigmoid(lb_ref[0:1, :] - lb_ref[1:2, :])
        dlb_ref[...] = jnp.broadcast_to(dlb * lb * (1.0 - lb), (8, HGRN_PAIR * REC_DIM))
        comm_last()

    hp, wd = REC_HEADS // HGRN_PAIR, HGRN_PAIR * REC_DIM
    cq, cf, ci_ = (c * REC_DIM // wd for c in (COL_RQ, COL_RF, COL_RI))
    return pl.pallas_call(
        body, name="hgrn_bwd", grid=(B, hp),
        in_specs=[pl.BlockSpec((S, wd), lambda b, h: (b, cq + h)),
                  pl.BlockSpec((S, wd), lambda b, h: (b, cf + h)),
                  pl.BlockSpec((S, wd), lambda b, h: (b, ci_ + h)),
                  pl.BlockSpec((2, wd), lambda b, h: (0, h)),
                  pl.BlockSpec((HGRN_PAIR * nc, REC_DIM, REC_DIM), lambda b, h: (b * hp + h, 0, 0)),
                  pl.BlockSpec((S, wd), lambda b, h: (b, h))] + c_in_specs,
        out_specs=[pl.BlockSpec((S, wd), lambda b, h: (b, h))] * 3
        + [pl.BlockSpec((8, wd), lambda b, h: (b, h))] + [ANY] * nco,
        out_shape=[jax.ShapeDtypeStruct((T, 512), BF16)] * 3 + [jax.ShapeDtypeStruct((B * 8, 512), F32)]
        + c_out_shapes,
        scratch_shapes=[pltpu.VMEM((bwd_unroll, CHUNK, wd), F32)] * 2
        + [pltpu.VMEM((bwd_unroll, SUB * CHUNK, wd), BF16)] * 2 + c_sems,
        compiler_params=_params("arbitrary", "arbitrary"))(proj, proj, proj, lb_param, states, do, *c_arrays)


def _rec_gate_fwd(rec, proj, rec_norm):
    T = rec.shape[0]

    def fn(accs, tv, cv):
        return [_rms_hat(tv[0]) * cv[0] * _sigmoid(tv[1])]

    return _tile_call("rec_gate", fn, T, 512, _pick(T, 1024), REC_DIM, tiles=[(rec, 0), (proj, COL_RG)],
                      consts=[rec_norm], outs=[BF16])[0]


def _rec_gate_bwd(dyb, w_rec_proj, rec, proj, rec_norm):
    T = rec.shape[0]

    def fn(accs, tv, cv):
        d, r, rg = accs[0], tv[0], tv[1]
        sg = _sigmoid(rg)
        rn = _rms_hat(r) * cv[0]
        dh, dg = _rms_bwd_vals(d * sg, r, cv[0])
        return [dh, d * rn * sg * (1.0 - sg), dg]

    return _tile_call("rec_gate_bwd", fn, T, 512, _pick(T, 1024), REC_DIM, pairs=[(dyb, 0, w_rec_proj, "nt")],
                      tiles=[(rec, 0), (proj, COL_RG)], consts=[rec_norm], outs=[F32, BF16], parts=1)


def _mix_out_fwd(att, recn, proj, w_att_proj, w_rec_proj, w_out, h1, g_next):
    T = att.shape[0]
    tn = 256

    def merge(accs, tv, cv):
        ya, yb = accs
        return [ya, yb, _sigmoid(tv[0]) * ya + _sigmoid(tv[1]) * yb]

    ya, yb, merged = _tile_call(
        "merge", merge, T, D_MODEL, _pick(T, 1024), tn,
        pairs=[(att, 0, w_att_proj, "nn"), (recn, 0, w_rec_proj, "nn")],
        tiles=[(proj, COL_GA * 128 // tn), (proj, COL_GB * 128 // tn)], outs=[BF16] * 3)

    def res(accs, tv, cv):
        h2 = tv[0] + accs[0]
        return [h2, _rms_hat(h2) * cv[0]]

    h2, n2 = _tile_call("mix_out", res, T, D_MODEL, _pick(T, 512), D_MODEL, pairs=[(merged, 0, w_out, "nn")],
                        tiles=[(h1, 0)], consts=[g_next], outs=[F32, BF16])
    return h2, n2, (ya, yb, merged)


GATHER_FIRST = ("w_ffn1_in",)
GATHER_MIX = ("w_ffn1_out", "w_in")
GATHER_PROJ = ("w_att_proj", "w_rec_proj", "w_out")
GATHER_LAST = ("w_ffn2_in", "w_ffn2_out", "w_ple_gate", "w_ple_proj")
SCATTER_LATE = ("w_ple_gate", "w_ple_proj", "w_ffn2_in", "w_ffn2_out")
SCATTER_MIX = ("w_out", "w_att_proj", "w_rec_proj", "w_in")
SCATTER_LAST = ("w_ffn1_in", "w_ffn1_out")


def _local_step(x, p, tgt, w, mine16, cc, me_chip, B, S):
    T = B * S
    w = dict(w)
    g_ffn1, g_mix, g_ffn2, g_ple = w["norm_ffn1"], w["norm_mix"], w["norm_ffn2"], w["norm_ple"]
    g_fin = w["norm_final"].reshape(1, D_MODEL)
    grads, part, from_chips = {}, {}, {}

    def gather(names):
        return _gather_comm([mine16[n] for n in names])

    def place(names, got):
        for n, g in zip(names, got):
            full = lax.dynamic_update_index_in_dim(g, mine16[n], me_chip, 0)
            w[n] = full if n in ("w_ffn1_in", "w_ffn2_in") else _natural(n, full)

    def swap(names):
        return _swap_comm([grads[n][1] for n in names])

    def after_swap(names, from_sib):
        for n, fs in zip(names, from_sib):
            part[n] = _add_sibling("rs_add_sib_" + n, grads[n][0], fs, cc)
        return _scatter_comm([part[n][1] for n in names])

    def scattered(names, got):
        for n, g in zip(names, got):
            from_chips[n] = g

    def ffn1_in_weight(got):
        place(GATHER_FIRST, got)
        return w["w_ffn1_in"]

    def ffn1_out_weight(got):
        place(GATHER_MIX, got)
        return w["w_ffn1_out"]

    h1, u, sv1, got_proj = _ffn_fwd("ffn1", x, g_ffn1, None, None, g_mix, comm_norm=gather(GATHER_FIRST),
                                    w_in_of=ffn1_in_weight, comm_in=gather(GATHER_MIX),
                                    comm_out=gather(GATHER_PROJ), w_out_of=ffn1_out_weight)
    place(GATHER_PROJ, got_proj)

    def ident(accs, tv, cv):
        return [accs[0]]

    proj = _tile_call("in_proj", ident, T, IN_W, _pick(T, 512), IN_W // 2, pairs=[(u, 0, w["w_in"], "nn")],
                      outs=[F32], j_outer=True)[0]
    onehot = jnp.asarray(_t5_onehot())
    bias = _small_mm("t5_bias", w["rel_bias"].T, onehot.astype(BF16), "right")
    bias = bias.reshape(N_Q_HEADS, ATT_BLOCK, 2 * ATT_BLOCK)
    sinks = w["attn_sinks"].reshape(N_Q_HEADS)
    kk2, vv2 = _kv_layouts(proj)
    att = _swa_fwd(proj, kk2, vv2, bias, sinks, B, S)
    rec, states, *got_last = _hgrn_fwd(proj, w["lb_param"], B, S, comm=gather(GATHER_LAST))
    place(GATHER_LAST, got_last)
    recn = _rec_gate_fwd(rec, proj, w["rec_norm"])
    h2, n2, (ya, yb, merged) = _mix_out_fwd(att, recn, proj, w["w_att_proj"], w["w_rec_proj"], w["w_out"], h1,
                                            g_ffn2)
    h3, n3, sv2, _ = _ffn_fwd("ffn2", h2, g_ffn2, w["w_ffn2_in"], w["w_ffn2_out"], g_ple, n=n2)

    def ple(accs, tv, cv):
        gate = _sigmoid(accs[0])
        return [gate, accs[1], tv[0] + gate * accs[1]]

    gate_p, pp, h4 = _tile_call(
        "ple", ple, T, D_MODEL, _pick(T, 512), D_MODEL,
        pairs=[(n3, 0, w["w_ple_gate"], "nn"), (p, 0, w["w_ple_proj"], "nn")], tiles=[(h3, 0)],
        outs=[BF16, BF16, F32])

    def head(accs, tv, cv):
        h, t, gt, ppv = tv[0], tv[1], tv[2].astype(F32), tv[3].astype(F32)
        err = _rms_hat(h) * cv[0] - t
        dh, dg = _rms_bwd_vals(err * (1.0 / D_MODEL), h, cv[0])
        return [dh, dh * ppv * gt * (1.0 - gt), dh * gt, _group8(err * err), dg]

    dh4, dzg, dpp, loss_p, dg_fin = _tile_call(
        "loss_head", head, T, D_MODEL, _pick(T, 256), D_MODEL,
        tiles=[(h4, 0), (tgt, 0), (gate_p, 0), (pp, 0)], consts=[g_fin], outs=[F32, BF16, BF16], parts=2)
    grads["norm_final"] = dg_fin

    grads["w_ple_gate"] = _mm_tn_rows("ple_dwg", n3, dzg)
    grads["w_ple_proj"] = _mm_tn_cols("ple_dwp", p, dpp)

    def dnorm(accs, tv, cv):
        dh, dg = _rms_bwd_vals(accs[0], tv[0], cv[0])
        dh = tv[1] + dh
        return [dh, 0.5 * dh, dg]

    dh3, df3, grads["norm_ple"] = _tile_call(
        "ple_dnorm", dnorm, T, D_MODEL, _pick(T, 512), D_MODEL, pairs=[(dzg, 0, w["w_ple_gate"], "nt")],
        tiles=[(h3, 0), (dh4, 0)], consts=[g_ple], outs=[F32, BF16], parts=1)

    def swap_late(dw_in, dw_out):
        grads["w_ffn2_in"], grads["w_ffn2_out"] = dw_in, dw_out
        return swap(SCATTER_LATE)

    dh2, dh2b, grads["norm_ffn2"], _, _, _, from_sib = _ffn_bwd(
        "ffn2b", dh3, df3, h2, g_ffn2, w["w_ffn2_in"], w["w_ffn2_out"], sv2, comm_last=swap_late)
    scatter_late = after_swap(SCATTER_LATE, from_sib)

    grads["w_out"] = _mm_tn_rows("mix_dwout", merged, dh2b)
    tn = 256

    def dmerge(accs, tv, cv):
        dm = accs[0]
        sa, sb = _sigmoid(tv[0]), _sigmoid(tv[1])
        yav, ybv = tv[2].astype(F32), tv[3].astype(F32)
        return [dm * sa, dm * sb, dm * yav * sa * (1.0 - sa), dm * ybv * sb * (1.0 - sb)]

    dya, dyb, dga, dgb = _tile_call(
        "mix_dmerge", dmerge, T, D_MODEL, _pick(T, 1024), tn, pairs=[(dh2b, 0, w["w_out"], "nt")],
        tiles=[(proj, COL_GA * 128 // tn), (proj, COL_GB * 128 // tn), (ya, 0), (yb, 0)], outs=[BF16] * 4)
    grads["w_att_proj"] = _mm_tn_cols("mix_dwatt", att, dya)
    grads["w_rec_proj"] = _mm_tn_cols("mix_dwrec", recn, dyb)

    datt = _tile_call("mix_datt", ident, T, 512, _pick(T, 1024), 512, pairs=[(dya, 0, w["w_att_proj"], "nt")],
                      outs=[BF16])[0]
    drec, drg, grads["rec_norm"] = _rec_gate_bwd(dyb, w["w_rec_proj"], rec, proj, w["rec_norm"])

    drq, drf, dri, dlb, *got = _hgrn_bwd(proj, w["lb_param"], states, drec, B, S, comm=scatter_late)
    scattered(SCATTER_LATE, got)
    grads["lb_param"] = dlb
    daq, dak, dav, dbias, dsink = _swa_bwd(proj, kk2, vv2, bias, sinks, datt, B, S)
    grads["attn_sinks"] = dsink
    grads["rel_bias"] = _small_mm("t5_dbias", dbias.reshape(N_Q_HEADS, -1), onehot.T.astype(BF16), "right")
    dproj = jnp.concatenate([daq, dak, dav, drq, drf, dri, drg, dga, dgb], axis=1)
    tk = _pick(T, 2048, 128)
    w_in_shard = IN_W // N_CHIPS
    half_d = D_MODEL // 2
    gw32, gw16 = _mm_tn("mix_dwin", (2, 2, T // tk),
                        (u, (tk, half_d), lambda i, j, k: (k, i)), (dproj, (tk, IN_W // 2), lambda i, j, k: (k, j)),
                        _grad_pair((D_MODEL, IN_W), (half_d, IN_W // 2), lambda i, j, k: (i, j)))
    to_sh = lambda t: t.reshape(D_MODEL, N_CHIPS, w_in_shard).transpose(1, 0, 2)
    grads["w_in"] = (to_sh(gw32), to_sh(gw16))

    def dnorm_mix(accs, tv, cv):
        dh, dg = _rms_bwd_vals(accs[0], tv[0], cv[0])
        dh = tv[1] + dh
        return [dh, 0.5 * dh, dg]

    dh1, df1, grads["norm_mix"], *from_sib = _tile_call(
        "mix_dnorm", dnorm_mix, T, D_MODEL, _pick(T, 512), D_MODEL, pairs=[(dproj, 0, w["w_in"], "nt")],
        tiles=[(h1, 0), (dh2, 0)], consts=[g_mix], outs=[F32, BF16], parts=1, comm=swap(SCATTER_MIX))
    scatter_mix = after_swap(SCATTER_MIX, from_sib)

    def scatter_last(dw_in, dw_out):
        grads["w_ffn1_in"], grads["w_ffn1_out"] = dw_in, dw_out
        return after_swap(SCATTER_LAST, _run_comm("rs_sibling_last", swap(SCATTER_LAST)))

    dx, _, grads["norm_ffn1"], _, _, got, got_last = _ffn_bwd(
        "ffn1b", dh1, df1, x, g_ffn1, w["w_ffn1_in"], w["w_ffn1_out"], sv1, comm=scatter_mix, comm_last=scatter_last)
    scattered(SCATTER_MIX, got)
    scattered(SCATTER_LAST, got_last)
    return loss_p, dx, grads, part, from_chips


def _place():
    x, y, c = lax.axis_index("x"), lax.axis_index("y"), lax.axis_index("c")
    return x, y, c


def _other_chips(x, y):
    return [(1 - x, y, 2 * (1 - x) + y), (x, 1 - y, 2 * x + 1 - y), (1 - x, 1 - y, 2 * (1 - x) + 1 - y)]


def _half_rows(ref_3d, chip, h, rows):
    return ref_3d.at[chip, pl.ds(h * rows, rows), :]


def _gather_comm(ws):
    nw = len(ws)

    def parts(w_refs, out_refs, send_sems, recv_sems):
        x, y, c = _place()
        me = 2 * x + y
        chips = _other_chips(x, y)

        def copy(i, k, chip, h, to, src=None):
            half = ws[i].shape[0] // 2
            dst = _half_rows(out_refs[i], chip, h, half)
            return pltpu.make_async_remote_copy(
                src_ref=dst if src is None else src, dst_ref=dst,
                send_sem=send_sems.at[6 * i + k], recv_sem=recv_sems.at[6 * i + k], device_id=to, device_id_type=MESH)

        def first():
            out = []
            for i in range(nw):
                half = ws[i].shape[0] // 2
                out += [copy(i, j, me, c, (cx, cy, c), src=w_refs[i].at[pl.ds(c * half, half), :])
                        for j, (cx, cy, _) in enumerate(chips)]
            return out

        return copy, first, chips, c, (x, y, 1 - c)

    def start(*refs):
        _, first, _, _, _ = parts(*refs)
        for cp in first():
            cp.start()

    def finish(*refs):
        copy, first, chips, c, sibling = parts(*refs)
        passed = []
        for i in range(nw):
            for j, (cx, cy, ci) in enumerate(chips):
                copy(i, j, ci, c, (cx, cy, c)).wait_recv()
                fw = copy(i, 3 + j, ci, c, sibling)
                fw.start()
                passed.append(fw)
        for i in range(nw):
            for j, (_, _, ci) in enumerate(chips):
                copy(i, 3 + j, ci, 1 - c, sibling).wait_recv()
        for cp in first() + passed:
            cp.wait_send()

    return _Comm(list(ws), [jax.ShapeDtypeStruct((N_CHIPS,) + w.shape, w.dtype) for w in ws], 6 * nw, start, finish)


def _scatter_comm(ps):
    nw = len(ps)

    def copies(p_refs, out_refs, send_sems, recv_sems):
        x, y, c = _place()
        cps = []
        for i in range(nw):
            for j, (cx, cy, ci) in enumerate(_other_chips(x, y)):
                cps.append(pltpu.make_async_remote_copy(
                    src_ref=p_refs[i].at[ci], dst_ref=out_refs[i].at[j], send_sem=send_sems.at[3 * i + j],
                    recv_sem=recv_sems.at[3 * i + j], device_id=(cx, cy, c), device_id_type=MESH))
        return cps

    def start(*refs):
        for cp in copies(*refs):
            cp.start()

    def finish(*refs):
        for cp in copies(*refs):
            cp.wait()

    return _Comm(list(ps), [jax.ShapeDtypeStruct((3,) + p.shape[1:], p.dtype) for p in ps], 3 * nw, start, finish)


def _swap_comm(gs):
    nw = len(gs)

    def copies(g_refs, out_refs, send_sems, recv_sems):
        x, y, c = _place()
        cps = []
        for i in range(nw):
            half = gs[i].shape[1] // 2
            cps.append(pltpu.make_async_remote_copy(
                src_ref=g_refs[i].at[:, pl.ds((1 - c) * half, half), :], dst_ref=out_refs[i],
                send_sem=send_sems.at[i], recv_sem=recv_sems.at[i], device_id=(x, y, 1 - c), device_id_type=MESH))
        return cps

    def start(*refs):
        for cp in copies(*refs):
            cp.start()

    def finish(*refs):
        for cp in copies(*refs):
            cp.wait()

    return _Comm(list(gs), [jax.ShapeDtypeStruct((N_CHIPS, g.shape[1] // 2, g.shape[2]), g.dtype) for g in gs],
                 nw, start, finish)


def _run_comm(name, comm):
    nci, nco = len(comm.ins), len(comm.out_shapes)

    def body(*refs):
        cin, cout, send_sems, recv_sems = refs[:nci], refs[nci:nci + nco], refs[-2], refs[-1]
        comm.start(cin, cout, send_sems, recv_sems)
        comm.finish(cin, cout, send_sems, recv_sems)

    return pl.pallas_call(
        body, name=name, in_specs=[ANY] * nci, out_specs=[ANY] * nco, out_shape=list(comm.out_shapes),
        scratch_shapes=[pltpu.SemaphoreType.DMA((comm.n_sems,)), pltpu.SemaphoreType.DMA((comm.n_sems,))],
    )(*comm.ins)


def _join_halves(name, ss):
    nw = len(ss)

    def body(*refs):
        s_refs, out_refs, send_sems, recv_sems = refs[:nw], refs[nw:2 * nw], refs[2 * nw], refs[2 * nw + 1]
        x, y, c = _place()
        cps = [pltpu.make_async_remote_copy(
            src_ref=s_refs[i], dst_ref=out_refs[i], send_sem=send_sems.at[i], recv_sem=recv_sems.at[i],
            device_id=(x, y, 1 - c), device_id_type=MESH) for i in range(nw)]
        for cp in cps:
            cp.start()
        for cp in cps:
            cp.wait()

    return pl.pallas_call(
        body, name=name, in_specs=[ANY] * nw, out_specs=[ANY] * nw,
        out_shape=[jax.ShapeDtypeStruct(s.shape, s.dtype) for s in ss],
        scratch_shapes=[pltpu.SemaphoreType.DMA((nw,)), pltpu.SemaphoreType.DMA((nw,))],
    )(*ss)


def _allreduce_small(sp):
    def body(s_ref, out_ref, slots, send_sems, recv_sems):
        x, y, c = _place()
        me = 4 * x + 2 * y + c
        slots[me] = s_ref[...]
        cps = []
        for r in range(1, N_DEV):
            px, py, pc = x ^ (r >> 2), y ^ ((r >> 1) & 1), c ^ (r & 1)
            cps.append(pltpu.make_async_remote_copy(
                src_ref=s_ref, dst_ref=slots.at[me], send_sem=send_sems.at[r - 1], recv_sem=recv_sems.at[r - 1],
                device_id=(px, py, pc), device_id_type=MESH))
        for cp in cps:
            cp.start()
        for r in range(1, N_DEV):
            px, py, pc = x ^ (r >> 2), y ^ ((r >> 1) & 1), c ^ (r & 1)
            pltpu.make_async_remote_copy(
                src_ref=s_ref, dst_ref=slots.at[4 * px + 2 * py + pc], send_sem=send_sems.at[r - 1],
                recv_sem=recv_sems.at[r - 1], device_id=(px, py, pc), device_id_type=MESH).wait_recv()
        for cp in cps:
            cp.wait_send()
        acc = slots[0]
        for d in range(1, N_DEV):
            acc = acc + slots[d]
        out_ref[...] = acc

    return pl.pallas_call(
        body, name="allreduce_small",
        in_specs=[pl.BlockSpec(memory_space=pltpu.VMEM)], out_specs=pl.BlockSpec(memory_space=pltpu.VMEM),
        out_shape=jax.ShapeDtypeStruct(sp.shape, F32),
        scratch_shapes=[pltpu.VMEM((N_DEV,) + sp.shape, F32), pltpu.SemaphoreType.DMA((N_DEV - 1,)),
                        pltpu.SemaphoreType.DMA((N_DEV - 1,))],
    )(sp)


def _scalar(v):
    return jnp.reshape(v, (1,)).astype(jnp.int32)


def _row_tile(h, dtype_mult=16):
    return _pick(h, 256, dtype_mult)


def _add_sibling(name, g32, from_sib, c):
    _, r, n = g32.shape
    h = r // 2
    th = _row_tile(h)
    nt = h // th

    def body(c_ref, g_ref, s_ref, o32_ref, o16_ref):
        s = g_ref[...] + s_ref[...].astype(F32)
        o32_ref[...] = s
        o16_ref[...] = s.astype(BF16)

    blk = (None, th, n)
    return pl.pallas_call(
        body, name=name,
        grid_spec=pltpu.PrefetchScalarGridSpec(
            num_scalar_prefetch=1, grid=(N_CHIPS, nt),
            in_specs=[pl.BlockSpec(blk, lambda k, t, c_ref: (k, c_ref[0] * nt + t, 0)),
                      pl.BlockSpec(blk, lambda k, t, c_ref: (k, t, 0))],
            out_specs=[pl.BlockSpec(blk, lambda k, t, c_ref: (k, t, 0))] * 2),
        out_shape=[jax.ShapeDtypeStruct((N_CHIPS, h, n), F32), jax.ShapeDtypeStruct((N_CHIPS, h, n), BF16)],
        compiler_params=_params("arbitrary", "arbitrary"))(_scalar(c), g32, from_sib)


def _add_chips(name, p32, from_chips, me_chip):
    _, h, n = p32.shape
    th = _row_tile(h)

    def body(m_ref, p_ref, a_ref, b_ref, c_ref, o_ref):
        o_ref[...] = p_ref[...] + a_ref[...].astype(F32) + b_ref[...].astype(F32) + c_ref[...].astype(F32)

    blk = (None, th, n)
    return pl.pallas_call(
        body, name=name,
        grid_spec=pltpu.PrefetchScalarGridSpec(
            num_scalar_prefetch=1, grid=(h // th,),
            in_specs=[pl.BlockSpec(blk, lambda t, m_ref: (m_ref[0], t, 0))]
            + [pl.BlockSpec(blk, lambda t, m_ref, j=j: (j, t, 0)) for j in range(3)],
            out_specs=pl.BlockSpec((th, n), lambda t, m_ref: (t, 0))),
        out_shape=jax.ShapeDtypeStruct((h, n), F32),
        compiler_params=_params("arbitrary"))(_scalar(me_chip), p32, from_chips, from_chips, from_chips)


def _adamw_vals(w, g, m, v):
    m = ADAM_B1 * m + (1.0 - ADAM_B1) * g
    v = ADAM_B2 * v + (1.0 - ADAM_B2) * (g * g)
    m_hat = m / (1.0 - ADAM_B1 ** ADAM_STEP)
    v_hat = v / (1.0 - ADAM_B2 ** ADAM_STEP)
    delta = -ADAM_LR * (m_hat / (jnp.sqrt(v_hat) + ADAM_EPS) + ADAM_WD * w)
    return delta, m, v


def _adamw_halves(name, w, m, v, g_mine, g_sib, c):
    r, n = w.shape
    h = r // 2
    th = _row_tile(h, 8)
    nt = h // th

    def body(c_ref, w_ref, m_ref, v_ref, a_ref, b_ref, g_ref, d_ref, nm_ref, nv_ref):
        mine = (pl.program_id(0) // nt) == c_ref[0]
        g = jnp.where(mine, a_ref[...], b_ref[...])
        d, nm, nv = _adamw_vals(w_ref[...], g, m_ref[...], v_ref[...])
        g_ref[...] = g
        d_ref[...] = d
        nm_ref[...] = nm
        nv_ref[...] = nv

    full = pl.BlockSpec((th, n), lambda t, c_ref: (t, 0))
    part = pl.BlockSpec((th, n), lambda t, c_ref: (t % nt, 0))
    return pl.pallas_call(
        body, name=name,
        grid_spec=pltpu.PrefetchScalarGridSpec(
            num_scalar_prefetch=1, grid=(2 * nt,), in_specs=[full, full, full, part, part], out_specs=[full] * 4),
        out_shape=[jax.ShapeDtypeStruct((r, n), F32)] * 4,
        compiler_params=_params("arbitrary"))(_scalar(c), w, m, v, g_mine, g_sib)


def _adamw(name, w, g, m, v):
    R, W = w.shape

    def fn(accs, tv, cv):
        return list(_adamw_vals(*tv))

    return _tile_call(name, fn, R, W, _pick(R, 256), W, tiles=[(w, 0), (g, 0), (m, 0), (v, 0)], outs=[F32] * 3)


SMALL_LAYOUT = (("rel_bias", 2, 256), ("lb_param", 8, 1024), ("norm_ffn1", 8, 1024), ("norm_mix", 8, 1024),
                ("attn_sinks", 1, 8), ("rec_norm", 1, 128), ("norm_ffn2", 8, 1024), ("norm_ple", 8, 1024),
                ("norm_final", 8, 1024), ("loss", 8, 1024))


def _pack_small(vals):
    rows = []
    for name, nrows, n in SMALL_LAYOUT:
        flat = vals[name].reshape(-1)
        flat = jnp.pad(flat, (0, nrows * 128 - n))
        rows.append(flat.reshape(nrows, 128))
    packed = jnp.concatenate(rows, axis=0)
    return jnp.pad(packed, ((0, SMALL_ROWS - packed.shape[0]), (0, 0)))


def _unpack_small(packed, shapes):
    out, r = {}, 0
    for name, nrows, n in SMALL_LAYOUT:
        out[name] = packed[r:r + nrows].reshape(-1)[:n].reshape(shapes[name])
        r += nrows
    return out


def _natural(name, s):
    if name in COL_SHARDED:
        return s.transpose(1, 0, 2).reshape(s.shape[1], -1)
    return s.reshape(-1, s.shape[2])


def kernel(x, p, rel_bias, lb_param, norm_ffn1, w_ffn1_in, w_ffn1_out, norm_mix, w_in, attn_sinks, rec_norm, w_att_proj, w_rec_proj, w_out, norm_ffn2, w_ffn2_in, w_ffn2_out, norm_ple, w_ple_gate, w_ple_proj, norm_final, loss_target, m_rel_bias, m_lb_param, m_norm_ffn1, m_w_ffn1_in, m_w_ffn1_out, m_norm_mix, m_w_in, m_attn_sinks, m_rec_norm, m_w_att_proj, m_w_rec_proj, m_w_out, m_norm_ffn2, m_w_ffn2_in, m_w_ffn2_out, m_norm_ple, m_w_ple_gate, m_w_ple_proj, m_norm_final, v_rel_bias, v_lb_param, v_norm_ffn1, v_w_ffn1_in, v_w_ffn1_out, v_norm_mix, v_w_in, v_attn_sinks, v_rec_norm, v_w_att_proj, v_w_rec_proj, v_w_out, v_norm_ffn2, v_w_ffn2_in, v_w_ffn2_out, v_norm_ple, v_w_ple_gate, v_w_ple_proj, v_norm_final):
    args = dict(locals())
    wsh = {n: args[n] for n in WEIGHTS}
    B, S = x.shape[0], x.shape[1]
    T = B * S
    cx, cy, cc = _place()
    me_chip = 2 * cx + cy

    mine16 = {n: wsh[n][0].astype(BF16) for n in BIG}
    loss_p, dx, grads, part, from_chips = _local_step(
        x.reshape(T, D_MODEL), p.reshape(T, PLE_DIM), loss_target.reshape(T, D_MODEL),
        {n: wsh[n] for n in SMALL}, mine16, cc, me_chip, B, S)

    s_mine = [_add_chips("rs_add_chips_" + n, part[n][0], from_chips[n], me_chip) for n in BIG]
    s_sib = _join_halves("rs_join", s_mine)

    small_vals = {
        "rel_bias": grads["rel_bias"].T,
        "lb_param": jnp.concatenate([_colsum("dlb_sum", grads["lb_param"]),
                                     -_colsum("dlb_sum2", grads["lb_param"])], axis=0) / 8.0,
        "attn_sinks": grads["attn_sinks"][:, 0],
        "rec_norm": _colsum("drn_sum", grads["rec_norm"]).reshape(REC_HEADS, REC_DIM).sum(axis=0),
        "loss": _colsum("loss_sum", loss_p),
    }
    for n in ("norm_ffn1", "norm_mix", "norm_ffn2", "norm_ple", "norm_final"):
        small_vals[n] = _colsum(n + "_sum", grads[n])
    red = _allreduce_small(_pack_small(small_vals))
    small_shapes = {n: wsh[n].shape for n in SMALL}
    small_shapes["loss"] = (D_MODEL,)
    small = _unpack_small(red, small_shapes)
    loss = 0.5 * jnp.sum(small["loss"]) / D_MODEL

    out_g, out_d, out_m, out_v = {}, {}, {}, {}
    for n, gm, gs in zip(BIG, s_mine, s_sib):
        res = _adamw_halves("adamw_" + n, wsh[n][0], args["m_" + n][0], args["v_" + n][0], gm, gs, cc)
        out_g[n], out_d[n], out_m[n], out_v[n] = (t[None] for t in res)
    sw = _pack_small({**{n: wsh[n] for n in SMALL}, "loss": jnp.zeros((D_MODEL,), F32)})
    sm = _pack_small({**{n: args["m_" + n] for n in SMALL}, "loss": jnp.zeros((D_MODEL,), F32)})
    sv = _pack_small({**{n: args["v_" + n] for n in SMALL}, "loss": jnp.ones((D_MODEL,), F32)})
    sd, snm, snv = _adamw("adamw_small", sw, red, sm, sv)
    ud, um, uv = (_unpack_small(t, small_shapes) for t in (sd, snm, snv))
    for n in SMALL:
        out_g[n], out_d[n], out_m[n], out_v[n] = small[n], ud[n], um[n], uv[n]

    return (loss, dx.reshape(B, S, D_MODEL), *[out_g[n] for n in WEIGHTS], *[out_d[n] for n in WEIGHTS],
            *[out_m[n] for n in WEIGHTS], *[out_v[n] for n in WEIGHTS])
```

```python
import numpy as np
import jax
import jax.numpy as jnp
from jax import lax
from jax.experimental import pallas as pl
from jax.experimental.pallas import tpu as pltpu

F32 = jnp.float32
BF16 = jnp.bfloat16
MESH = pl.DeviceIdType.MESH

D_MODEL = 1024
D_FF = 2816
FF_SHARD = 2 * D_FF // 4
HEAD_DIM = 64
N_Q_HEADS = 8
ATT_BLOCK = 128
N_BUCKETS = 32
MAX_DISTANCE = 128
REC_HEADS = 4
REC_DIM = 128
PLE_DIM = 256
EPS = 1e-6
IN_W = 4864
COL_AQ, COL_AK, COL_AV = 0, 4, 5
COL_RQ, COL_RF, COL_RI = 0, 4, 8
COL_RG, COL_GA, COL_GB = 0, 4, 12
ATT_W, REC_W, GATE_W = 768, 1536, 2560

CHUNK = 64
SUB = 8
N_SUB = CHUNK // SUB
HGRN_PAIR = 2

ADAM_LR, ADAM_B1, ADAM_B2, ADAM_EPS, ADAM_WD, ADAM_STEP = 0.001, 0.9, 0.999, 1e-08, 0.01, 10

V7X_VMEM_LIMIT = 56 * 1024 * 1024
N_CHIPS = 4
N_DEV = 8

BIG = ("w_ffn1_in", "w_ffn1_out", "w_in", "w_att_proj", "w_rec_proj", "w_out",
       "w_ffn2_in", "w_ffn2_out", "w_ple_gate", "w_ple_proj")
COL_SHARDED = ("w_ffn1_in", "w_in", "w_att_proj", "w_rec_proj", "w_ffn2_in", "w_ple_proj")
WEIGHTS = ("rel_bias", "lb_param", "norm_ffn1", "w_ffn1_in", "w_ffn1_out", "norm_mix", "w_in", "attn_sinks",
           "rec_norm", "w_att_proj", "w_rec_proj", "w_out", "norm_ffn2", "w_ffn2_in", "w_ffn2_out", "norm_ple",
           "w_ple_gate", "w_ple_proj", "norm_final")
SMALL = tuple(n for n in WEIGHTS if n not in BIG)
SMALL_ROWS = 64


def _params(*sem):
    return pltpu.CompilerParams(dimension_semantics=sem, vmem_limit_bytes=V7X_VMEM_LIMIT)


def _pick(n, cap, mult=8):
    if n <= cap:
        return n
    for t in range(cap - cap % mult, 0, -mult):
        if n % t == 0:
            return t
    raise ValueError((n, cap, mult))


def _dot(a, b):
    return jnp.dot(a, b, preferred_element_type=F32)


def _dot_nt(a, b):
    return lax.dot_general(a, b, (((1,), (1,)), ((), ())), preferred_element_type=F32)


def _dot_tn(a, b):
    return lax.dot_general(a, b, (((0,), (0,)), ((), ())), preferred_element_type=F32)


def _split3(x):
    hi = x.astype(BF16)
    r = x - hi.astype(F32)
    mid = r.astype(BF16)
    lo = (r - mid.astype(F32)).astype(BF16)
    return hi, mid, lo


def _split2(x):
    hi = x.astype(BF16)
    return hi, (x - hi.astype(F32)).astype(BF16)


def _sel_left(sel_bf16, x):
    hi, mid, lo = _split3(x)
    return _dot(sel_bf16, hi) + _dot(sel_bf16, mid) + _dot(sel_bf16, lo)


def _sel_right(x, sel_bf16):
    hi, mid, lo = _split3(x)
    return _dot(hi, sel_bf16) + _dot(mid, sel_bf16) + _dot(lo, sel_bf16)


def _sigmoid(x):
    return 0.5 * jnp.tanh(0.5 * x) + 0.5


def _group8(x):
    r, w = x.shape
    return x.reshape(r // 8, 8, w).sum(axis=0)


class _Comm:
    def __init__(self, ins, out_shapes, n_sems, start, finish):
        self.ins, self.out_shapes, self.n_sems, self.start, self.finish = ins, out_shapes, n_sems, start, finish


ANY = pl.BlockSpec(memory_space=pl.ANY)


def _comm_parts(comm):
    if comm is None:
        return [], [], [], []
    sems = [pltpu.SemaphoreType.DMA((comm.n_sems,)), pltpu.SemaphoreType.DMA((comm.n_sems,))]
    return list(comm.ins), [ANY] * len(comm.ins), list(comm.out_shapes), sems


def _comm_run(comm, grid, refs, n_in, n_out):
    if comm is None:
        return (lambda: None), (lambda: None)
    nci, nco = len(comm.ins), len(comm.out_shapes)
    cin = refs[n_in:n_in + nci]
    cout = refs[n_in + nci + n_out:n_in + nci + n_out + nco]
    send_sems, recv_sems = refs[-2], refs[-1]
    ids = [pl.program_id(d) for d in range(len(grid))]
    is_first = ids[0] == 0
    is_last = ids[0] == grid[0] - 1
    for d in range(1, len(grid)):
        is_first = is_first & (ids[d] == 0)
        is_last = is_last & (ids[d] == grid[d] - 1)

    def first():
        @pl.when(is_first)
        def _():
            comm.start(cin, cout, send_sems, recv_sems)

    def last():
        @pl.when(is_last)
        def _():
            comm.finish(cin, cout, send_sems, recv_sems)

    return first, last


def _call(name, fn, grid, ins, outs, pairs=(), comm=None, j_outer=False):
    in_pair = {i for p in pairs for i in p[:2]}
    n_in, n_out = len(ins), len(outs)
    c_arrays, c_in_specs, c_out_shapes, c_sems = _comm_parts(comm)

    def body(*refs):
        first, last = _comm_run(comm, grid, refs, n_in, n_out)
        first()
        accs = []
        for ia, ib, kind in pairs:
            a, b = refs[ia][...].astype(BF16), refs[ib][...].astype(BF16)
            accs.append(_dot(a, b) if kind == "nn" else _dot_nt(a, b))
        vals = [refs[i][...] for i in range(n_in) if i not in in_pair]
        res = fn(accs, vals)
        out_refs = refs[n_in + len(c_arrays):n_in + len(c_arrays) + n_out]
        assert len(res) == len(out_refs), (name, len(res), len(out_refs))
        for o_ref, val in zip(out_refs, res):
            o_ref[...] = val.astype(o_ref.dtype)
        last()

    if j_outer:
        grid = (grid[1], grid[0])
        swap = lambda im: (lambda j, i: im(i, j))
        ins = [(a, blk, swap(im)) for a, blk, im in ins]
        outs = [(shp, dt, blk, swap(im)) for shp, dt, blk, im in outs]

    return pl.pallas_call(
        body, name=name, grid=grid,
        in_specs=[pl.BlockSpec(blk, im) for _, blk, im in ins] + c_in_specs,
        out_specs=[pl.BlockSpec(blk, im) for _, _, blk, im in outs] + [ANY] * len(c_out_shapes),
        out_shape=[jax.ShapeDtypeStruct(shp, dt) for shp, dt, _, _ in outs] + c_out_shapes,
        scratch_shapes=c_sems,
        compiler_params=_params(*(["arbitrary"] * len(grid))))(*[a for a, _, _ in ins], *c_arrays)


def _tile_call(name, fn, M, N, tm, tn, *, pairs=(), tiles=(), consts=(), outs=(), parts=0, comm=None,
               j_outer=False):
    gi, gj = M // tm, N // tn
    assert gi * tm == M and gj * tn == N, (name, M, N, tm, tn)
    ins, prs = [], []
    for a, a_col, b, kind in pairs:
        K = b.shape[0] if kind == "nn" else b.shape[1]
        ins.append((a, (tm, K), lambda i, j, c=a_col: (i, c)))
        if kind == "nn":
            ins.append((b, (K, tn), lambda i, j: (0, j)))
        else:
            ins.append((b, (tn, K), lambda i, j: (j, 0)))
        prs.append((len(ins) - 2, len(ins) - 1, kind))
    for arr, off in tiles:
        ins.append((arr, (tm, tn), lambda i, j, o=off: (i, j + o)))
    for arr in consts:
        ins.append((arr, arr.shape, lambda i, j: (0, 0)))
    out_l = [((M, N), dt, (tm, tn), lambda i, j: (i, j)) for dt in outs]
    out_l += [((gi * 8, N), F32, (8, tn), lambda i, j: (i, j))] * parts
    nt = len(tiles)

    def wrapped(accs, vals):
        return fn(accs, vals[:nt], vals[nt:])

    return _call(name, wrapped, (gi, gj), ins, out_l, prs, comm=comm, j_outer=j_outer)


def _mm_tn(name, grid, a_in, b_in, outs):
    nk = grid[2]
    tm = [d for d in a_in[1] if d is not None][1]
    tn = [d for d in b_in[1] if d is not None][1]

    def body(a_ref, b_ref, *rest):
        out_refs, acc_ref = rest[:-1], rest[-1]
        k = pl.program_id(2)

        @pl.when(k == 0)
        def _():
            acc_ref[...] = jnp.zeros_like(acc_ref)

        acc_ref[...] += _dot_tn(a_ref[...].astype(BF16), b_ref[...].astype(BF16))

        @pl.when(k == nk - 1)
        def _():
            for o_ref in out_refs:
                o_ref[...] = acc_ref[...].astype(o_ref.dtype)

    return pl.pallas_call(
        body, name=name, grid=grid,
        in_specs=[pl.BlockSpec(a_in[1], a_in[2]), pl.BlockSpec(b_in[1], b_in[2])],
        out_specs=[pl.BlockSpec(blk, im) for _, _, blk, im in outs],
        out_shape=[jax.ShapeDtypeStruct(shp, dt) for shp, dt, _, _ in outs],
        scratch_shapes=[pltpu.VMEM((tm, tn), F32)],
        compiler_params=_params("arbitrary", "arbitrary", "arbitrary"))(a_in[0], b_in[0])


def _grad_pair(shape, block, imap):
    return [(shape, F32, block, imap), (shape, BF16, block, imap)]


def _mm_tn_rows(name, a, b, tk=2048):
    T, a_w = a.shape
    b_w = b.shape[1]
    tm = _pick(a_w, 1408, 128)
    tk = _pick(T, tk if a_w > 1024 else 2 * tk, 128)
    g32, g16 = _mm_tn(name, (a_w // tm, 1, T // tk),
                      (a, (tk, tm), lambda i, j, k: (k, i)), (b, (tk, b_w), lambda i, j, k: (k, 0)),
                      _grad_pair((a_w, b_w), (tm, b_w), lambda i, j, k: (i, 0)))
    shp = (N_CHIPS, a_w // N_CHIPS, b_w)
    return g32.reshape(shp), g16.reshape(shp)


def _mm_tn_cols(name, a, b, tk=4096):
    T, a_w = a.shape
    b_w = b.shape[1]
    tk = _pick(T, tk, 128)
    g32, g16 = _mm_tn(name, (1, 1, T // tk),
                      (a, (tk, a_w), lambda i, j, k: (k, 0)), (b, (tk, b_w), lambda i, j, k: (k, 0)),
                      _grad_pair((a_w, b_w), (a_w, b_w), lambda i, j, k: (0, 0)))
    to_sh = lambda t: t.reshape(a_w, N_CHIPS, b_w // N_CHIPS).transpose(1, 0, 2)
    return to_sh(g32), to_sh(g16)


def _colsum(name, x):
    def body(x_ref, o_ref):
        o_ref[...] = jnp.sum(x_ref[...], axis=0, keepdims=True)
    return pl.pallas_call(body, name=name, out_shape=jax.ShapeDtypeStruct((1, x.shape[1]), F32))(x)


def _rms_hat(h):
    return h * lax.rsqrt(jnp.mean(h * h, axis=-1, keepdims=True) + EPS)


def _rms_bwd_vals(dn, h, g):
    r = lax.rsqrt(jnp.mean(h * h, axis=-1, keepdims=True) + EPS)
    nh = h * r
    gd = dn * g
    dh = r * (gd - nh * jnp.mean(gd * nh, axis=-1, keepdims=True))
    return dh, _group8(dn * nh)


def _rms_fwd(name, h, g, tm=512, comm=None):
    T = h.shape[0]

    def fn(accs, tv, cv):
        return [_rms_hat(tv[0]) * cv[0]]

    return _tile_call(name, fn, T, D_MODEL, _pick(T, tm), D_MODEL, tiles=[(h, 0)], consts=[g], outs=[BF16],
                      comm=comm)


def _ffn_fwd(tag, h, g, w_in, w_out, g_next, n=None, comm_norm=None, w_in_of=None, comm_in=None, comm_out=None,
             w_out_of=None):
    T = h.shape[0]
    if n is None:
        n, *got_norm = _rms_fwd(tag + "_norm", h, g, comm=comm_norm)
        if w_in_of is not None:
            w_in = w_in_of(got_norm)
    tm = _pick(T, 1024)
    wblk = (None, D_MODEL, FF_SHARD)

    def act(accs, vals):
        gate, up = accs
        return [gate, up, gate * _sigmoid(gate) * up]

    tile = lambda: ((T, D_FF), BF16, (tm, FF_SHARD), lambda i, j: (i, j))
    gate, up, a, *got_in = _call(
        tag + "_in", act, (T // tm, 2),
        [(n, (tm, D_MODEL), lambda i, j: (i, 0)),
         (w_in, wblk, lambda i, j: (j, 0, 0)), (w_in, wblk, lambda i, j: (j + 2, 0, 0))],
        [tile(), tile(), tile()], pairs=[(0, 1, "nn"), (0, 2, "nn")], comm=comm_in, j_outer=True)

    def res(accs, tv, cv):
        h_new = tv[0] + 0.5 * accs[0]
        return [h_new, _rms_hat(h_new) * cv[0]]

    if w_out_of is not None:
        w_out = w_out_of(got_in)
    h_new, n_next, *got_out = _tile_call(
        tag + "_out", res, T, D_MODEL, _pick(T, 512), D_MODEL, pairs=[(a, 0, w_out, "nn")], tiles=[(h, 0)],
        consts=[g_next], outs=[F32, BF16], comm=comm_out)
    return h_new, n_next, (n, gate, up, a), got_out


def _ffn_bwd(tag, dh_out, df, h, g, w_in, w_out, saved, comm=None, comm_last=None):
    T = h.shape[0]
    n, gate, up, a = saved
    tm = _pick(T, 512)

    def dact(accs, vals):
        da = accs[0]
        gt, u = vals[0].astype(F32), vals[1].astype(F32)
        sg = _sigmoid(gt)
        silu = gt * sg
        return [jnp.stack([(da * u * (sg + silu * (1.0 - sg))).astype(BF16), (da * silu).astype(BF16)])]

    dz, *got = _call(
        tag + "_dact", dact, (T // tm, 2),
        [(df, (tm, D_MODEL), lambda i, j: (i, 0)), (w_out, (FF_SHARD, D_MODEL), lambda i, j: (j, 0)),
         (gate, (tm, FF_SHARD), lambda i, j: (i, j)), (up, (tm, FF_SHARD), lambda i, j: (i, j))],
        [((2, T, D_FF), BF16, (2, tm, FF_SHARD), lambda i, j: (0, i, j))], pairs=[(0, 1, "nt")], comm=comm,
        j_outer=True)
    dw_out = _mm_tn_rows(tag + "_dwout", a, df)
    tk = _pick(T, 2048, 128)
    dw_in = _mm_tn(tag + "_dwin", (1, N_CHIPS, T // tk),
                   (n, (tk, D_MODEL), lambda i, j, k: (k, 0)),
                   (dz, (None, tk, FF_SHARD), lambda i, j, k: (j // 2, k, j % 2)),
                   _grad_pair((N_CHIPS, D_MODEL, FF_SHARD), (None, D_MODEL, FF_SHARD), lambda i, j, k: (j, 0, 0)))

    def dnorm(accs, vals):
        dn = accs[0] + accs[1] + accs[2] + accs[3]
        dh, dg = _rms_bwd_vals(dn, vals[0], vals[2])
        dh = vals[1] + dh
        return [dh, dh, dg]

    tm2 = _pick(T, 512)
    ins = [(dz, (None, tm2, FF_SHARD), lambda i, j, s=s: (s // 2, i, s % 2)) for s in range(N_CHIPS)]
    ins += [(w_in, (None, D_MODEL, FF_SHARD), lambda i, j, s=s: (s, 0, 0)) for s in range(N_CHIPS)]
    ins += [(h, (tm2, D_MODEL), lambda i, j: (i, 0)), (dh_out, (tm2, D_MODEL), lambda i, j: (i, 0)),
            (g, g.shape, lambda i, j: (0, 0))]
    dh, dh16, dg, *got_last = _call(
        tag + "_dnorm", dnorm, (T // tm2, 1), ins,
        [((T, D_MODEL), F32, (tm2, D_MODEL), lambda i, j: (i, 0)),
         ((T, D_MODEL), BF16, (tm2, D_MODEL), lambda i, j: (i, 0)),
         ((T // tm2 * 8, D_MODEL), F32, (8, D_MODEL), lambda i, j: (i, 0))],
        pairs=[(s, N_CHIPS + s, "nt") for s in range(N_CHIPS)],
        comm=None if comm_last is None else comm_last(dw_in, dw_out))
    return dh, dh16, dg, dw_in, dw_out, got, got_last


def _t5_onehot():
    qi = np.arange(ATT_BLOCK)[:, None] + ATT_BLOCK
    kj = np.arange(2 * ATT_BLOCK)[None, :]
    nn = np.maximum(qi - kj, 0)
    max_exact = N_BUCKETS // 2
    large = max_exact + (np.log(np.maximum(nn, 1) / max_exact) / np.log(MAX_DISTANCE / max_exact)
                         * (N_BUCKETS - max_exact)).astype(np.int32)
    large = np.minimum(large, N_BUCKETS - 1)
    bucket = np.where(nn < max_exact, nn, large).astype(np.int32).reshape(-1)
    return (bucket[None, :] == np.arange(N_BUCKETS)[:, None]).astype(np.float32)


def _small_mm(name, a, b, sel):
    def body(a_ref, b_ref, o_ref):
        if sel == "right":
            o_ref[...] = _sel_right(a_ref[...], b_ref[...])
        else:
            o_ref[...] = _sel_left(a_ref[...], b_ref[...])
    return pl.pallas_call(body, name=name, out_shape=jax.ShapeDtypeStruct((a.shape[0], b.shape[1]), F32),
                          compiler_params=pltpu.CompilerParams(vmem_limit_bytes=V7X_VMEM_LIMIT))(a, b)


def _dup_heads(t):
    a, b = t[:, :HEAD_DIM], t[:, HEAD_DIM:]
    return jnp.concatenate([a, a, b, b], axis=1)


def _kv_layouts(proj):
    return (_dup_heads(proj[:, COL_AK * 128:(COL_AK + 1) * 128]),
            _dup_heads(proj[:, COL_AV * 128:(COL_AV + 1) * 128]))


def _swa_masks():
    row = lax.broadcasted_iota(jnp.int32, (ATT_BLOCK, 2 * ATT_BLOCK), 0)
    col = lax.broadcasted_iota(jnp.int32, (ATT_BLOCK, 2 * ATT_BLOCK), 1)
    dist = ATT_BLOCK + row - col
    return (dist >= 0) & (dist < ATT_BLOCK), col


GROUP = 4


def _stack_group(blk, lo_q):
    zero = jnp.zeros_like(blk[:, :128])
    rows = []
    for pair in range(GROUP // 2):
        pb = blk[:, 128 * pair:128 * (pair + 1)]
        rows += [jnp.where(lo_q, pb, zero), jnp.where(lo_q, zero, pb)]
    return jnp.concatenate(rows, axis=0)


def _unstack_group(st, lo_q):
    pairs = [jnp.where(lo_q, st[256 * pair:256 * pair + 128], st[256 * pair + 128:256 * (pair + 1)])
             for pair in range(GROUP // 2)]
    return jnp.concatenate(pairs, axis=1)


def _swa_probs(s, bias_h, sink, valid):
    s = jnp.where(valid, s * (HEAD_DIM ** -0.5) + bias_h, -jnp.inf)
    m = jnp.maximum(jnp.max(s, axis=-1, keepdims=True), sink)
    e = jnp.exp(s - m)
    es = jnp.exp(sink - m)
    den = jnp.sum(e, axis=-1, keepdims=True) + es
    return e / den, es / den


def _swa_fwd(proj, kk2, vv2, bias, sinks, B, S):
    T = B * S
    nb = S // ATT_BLOCK

    def body(q_ref, k_ref, v_ref, bias_ref, sink_ref, o_ref, kpad, vpad):
        zeros = jnp.zeros((ATT_BLOCK, 256), BF16)
        kpad[pl.ds(0, ATT_BLOCK), :] = zeros
        vpad[pl.ds(0, ATT_BLOCK), :] = zeros
        kpad[pl.ds(ATT_BLOCK, S), :] = k_ref[...]
        vpad[pl.ds(ATT_BLOCK, S), :] = v_ref[...]
        valid0, col = _swa_masks()
        lo_q = lax.broadcasted_iota(jnp.int32, (1, 128), 1) < HEAD_DIM

        def blk(n, carry):
            r0 = pl.multiple_of(n * ATT_BLOCK, ATT_BLOCK)
            rows = pl.ds(r0, ATT_BLOCK)
            valid = valid0 & ((n > 0) | (col >= ATT_BLOCK))
            for g in range(N_Q_HEADS // GROUP):
                lanes = pl.ds(128 * g, 128)
                kg = kpad[pl.ds(r0, 2 * ATT_BLOCK), lanes]
                vg = vpad[pl.ds(r0, 2 * ATT_BLOCK), lanes]
                qm = _stack_group(q_ref[rows, pl.ds(256 * g, 256)].astype(BF16), lo_q)
                s = _dot_nt(qm, kg)
                ps = []
                for i in range(GROUP):
                    h = GROUP * g + i
                    p, _ = _swa_probs(s[ATT_BLOCK * i:ATT_BLOCK * (i + 1)], bias_ref[h], sink_ref[h], valid)
                    ps.append(p.astype(BF16))
                o = _dot(jnp.concatenate(ps, axis=0), vg)
                o_ref[rows, pl.ds(256 * g, 256)] = _unstack_group(o, lo_q).astype(o_ref.dtype)
            return carry

        if nb % 2 == 0:
            lax.fori_loop(0, nb // 2, lambda i, c: blk(2 * i + 1, blk(2 * i, c)), 0)
        else:
            lax.fori_loop(0, nb, blk, 0)

    return pl.pallas_call(
        body, name="swa_fwd", grid=(B,),
        in_specs=[pl.BlockSpec((S, 512), lambda b: (b, 0)),
                  pl.BlockSpec((S, 256), lambda b: (b, 0)),
                  pl.BlockSpec((S, 256), lambda b: (b, 0)),
                  pl.BlockSpec((N_Q_HEADS, ATT_BLOCK, 2 * ATT_BLOCK), lambda b: (0, 0, 0)),
                  pl.BlockSpec(memory_space=pltpu.SMEM)],
        out_specs=pl.BlockSpec((S, 512), lambda b: (b, 0)),
        out_shape=jax.ShapeDtypeStruct((T, 512), BF16),
        scratch_shapes=[pltpu.VMEM((S + ATT_BLOCK, 256), BF16), pltpu.VMEM((S + ATT_BLOCK, 256), BF16)],
        compiler_params=_params("arbitrary"))(proj, kk2, vv2, bias, sinks)


def _swa_bwd(proj, kk2, vv2, bias, sinks, datt, B, S):
    T = B * S
    nb = S // ATT_BLOCK

    def body(q_ref, k_ref, v_ref, bias_ref, sink_ref, do_ref, dq_ref, dk_ref, dv_ref, dbias_ref, dsink_ref,
             kpad, vpad, dkpad, dvpad):
        b = pl.program_id(0)

        @pl.when(b == 0)
        def _():
            dbias_ref[...] = jnp.zeros_like(dbias_ref)
            dsink_ref[...] = jnp.zeros_like(dsink_ref)

        zeros = jnp.zeros((ATT_BLOCK, 256), BF16)
        kpad[pl.ds(0, ATT_BLOCK), :] = zeros
        vpad[pl.ds(0, ATT_BLOCK), :] = zeros
        kpad[pl.ds(ATT_BLOCK, S), :] = k_ref[...]
        vpad[pl.ds(ATT_BLOCK, S), :] = v_ref[...]
        dkpad[...] = jnp.zeros_like(dkpad)
        dvpad[...] = jnp.zeros_like(dvpad)
        valid0, col = _swa_masks()
        lo_q = lax.broadcasted_iota(jnp.int32, (1, 128), 1) < HEAD_DIM
        scale = HEAD_DIM ** -0.5

        def blk(n, carry):
            r0 = pl.multiple_of(n * ATT_BLOCK, ATT_BLOCK)
            rows = pl.ds(r0, ATT_BLOCK)
            band = pl.ds(r0, 2 * ATT_BLOCK)
            valid = valid0 & ((n > 0) | (col >= ATT_BLOCK))
            for g in range(N_Q_HEADS // GROUP):
                lanes = pl.ds(128 * g, 128)
                kg = kpad[band, lanes]
                vg = vpad[band, lanes]
                qm = _stack_group(q_ref[rows, pl.ds(256 * g, 256)].astype(BF16), lo_q)
                dom = _stack_group(do_ref[rows, pl.ds(256 * g, 256)], lo_q)
                s = _dot_nt(qm, kg)
                dp = _dot_nt(dom, vg)
                pst, dst = [], []
                for i in range(GROUP):
                    h = GROUP * g + i
                    sl = slice(ATT_BLOCK * i, ATT_BLOCK * (i + 1))
                    p, ps = _swa_probs(s[sl], bias_ref[h], sink_ref[h], valid)
                    delta = jnp.sum(p * dp[sl], axis=-1, keepdims=True)
                    ds = p * (dp[sl] - delta)
                    dbias_ref[h] += ds
                    dsink_ref[pl.ds(h, 1), :] += -jnp.sum(jnp.broadcast_to(ps * delta, (ATT_BLOCK, 128)),
                                                          axis=0, keepdims=True)
                    pst.append(p.astype(BF16))
                    dst.append((ds * scale).astype(BF16))
                pst, dst = jnp.concatenate(pst, axis=0), jnp.concatenate(dst, axis=0)
                dq_ref[rows, pl.ds(256 * g, 256)] = _unstack_group(_dot(dst, kg), lo_q).astype(dq_ref.dtype)
                dkpad[band, lanes] += _dot_tn(dst, qm)
                dvpad[band, lanes] += _dot_tn(pst, dom)
            return carry

        if nb % 2 == 0:
            lax.fori_loop(0, nb // 2, lambda i, c: blk(2 * i + 1, blk(2 * i, c)), 0)
        else:
            lax.fori_loop(0, nb, blk, 0)
        lo_out = lax.broadcasted_iota(jnp.int32, (1, 128), 1) < HEAD_DIM

        def fold(pad_ref):
            halves = []
            for g in range(N_Q_HEADS // GROUP):
                t = pad_ref[pl.ds(ATT_BLOCK, S), pl.ds(128 * g, 128)]
                halves.append(t + pltpu.roll(t, HEAD_DIM, 1))
            return jnp.where(lo_out, halves[0], halves[1])

        dk_ref[...] = fold(dkpad).astype(dk_ref.dtype)
        dv_ref[...] = fold(dvpad).astype(dv_ref.dtype)

    return pl.pallas_call(
        body, name="swa_bwd", grid=(B,),
        in_specs=[pl.BlockSpec((S, 512), lambda b: (b, 0)),
                  pl.BlockSpec((S, 256), lambda b: (b, 0)),
                  pl.BlockSpec((S, 256), lambda b: (b, 0)),
                  pl.BlockSpec((N_Q_HEADS, ATT_BLOCK, 2 * ATT_BLOCK), lambda b: (0, 0, 0)),
                  pl.BlockSpec(memory_space=pltpu.SMEM),
                  pl.BlockSpec((S, 512), lambda b: (b, 0))],
        out_specs=[pl.BlockSpec((S, 512), lambda b: (b, 0)),
                   pl.BlockSpec((S, 128), lambda b: (b, 0)),
                   pl.BlockSpec((S, 128), lambda b: (b, 0)),
                   pl.BlockSpec((N_Q_HEADS, ATT_BLOCK, 2 * ATT_BLOCK), lambda b: (0, 0, 0)),
                   pl.BlockSpec((N_Q_HEADS, 128), lambda b: (0, 0))],
        out_shape=[jax.ShapeDtypeStruct((T, 512), BF16),
                   jax.ShapeDtypeStruct((T, 128), BF16),
                   jax.ShapeDtypeStruct((T, 128), BF16),
                   jax.ShapeDtypeStruct((N_Q_HEADS, ATT_BLOCK, 2 * ATT_BLOCK), F32),
                   jax.ShapeDtypeStruct((N_Q_HEADS, 128), F32)],
        scratch_shapes=[pltpu.VMEM((S + ATT_BLOCK, 256), BF16), pltpu.VMEM((S + ATT_BLOCK, 256), BF16),
                        pltpu.VMEM((S + ATT_BLOCK, 256), F32), pltpu.VMEM((S + ATT_BLOCK, 256), F32)],
        compiler_params=_params("arbitrary"))(proj, kk2, vv2, bias, sinks, datt)


def _hgrn_gates(z, lb):
    sg = _sigmoid(z)
    f = lb + (1.0 - lb) * sg
    return sg, f, jnp.log(f), 1.0 - f


def _hgrn_consts():
    r = lax.broadcasted_iota(jnp.int32, (CHUNK, CHUNK), 0)
    c = lax.broadcasted_iota(jnp.int32, (CHUNK, CHUNK), 1)
    tril = (r >= c).astype(BF16)
    triu = (r <= c).astype(BF16)
    causal = r >= c
    below = (r // SUB) > (c // SUB)
    inside = ((r // SUB) == (c // SUB)) & causal
    return tril, triu, causal, below, inside, c


def _block_rows(ref, lanes, s):
    rows = []
    for i in range(N_SUB):
        if SUB * i + s < 0:
            rows.append(jnp.zeros((SUB, REC_DIM), F32))
        else:
            rows.append(jnp.broadcast_to(ref[pl.ds(SUB * i + s, 1), lanes], (SUB, REC_DIM)))
    return jnp.concatenate(rows, axis=0)


def _hgrn_offdiag(q, k, bcum, b_ref, lanes):
    eq = jnp.exp(jnp.minimum(bcum - _block_rows(b_ref, lanes, -1), 0.0))
    qe = q * eq
    zero = jnp.zeros((SUB, REC_DIM), F32)
    q_rows, k_cols, eks = [jnp.zeros((SUB, (N_SUB - 1) * REC_DIM), F32)], [], []
    for i in range(1, N_SUB):
        q_rows.append(jnp.concatenate([zero] * (i - 1) + [qe[SUB * i:SUB * (i + 1), :]] + [zero] * (N_SUB - 1 - i),
                                      axis=1))
        p = b_ref[pl.ds(SUB * i - 1, 1), lanes]
        pad = jnp.zeros((CHUNK - SUB * i, REC_DIM), F32)
        ek = jnp.concatenate([jnp.exp(p - b_ref[pl.ds(0, SUB * i), lanes]), pad], axis=0)
        k_cols.append(k * ek)
        eks.append(ek)
    return jnp.concatenate(q_rows, axis=0), jnp.concatenate(k_cols, axis=1), eq, eks


def _hgrn_fwd(proj, lb_param, B, S, comm=None):
    T = B * S
    nc = S // CHUNK
    fwd_unroll = 4 if nc % 4 == 0 else 2
    c_arrays, c_in_specs, c_out_shapes, c_sems = _comm_parts(comm)
    nci, nco = len(c_arrays), len(c_out_shapes)

    def body(*refs):
        q_ref, z_ref, v_ref, lb_ref = refs[:4]
        o_ref, st_ref = refs[4 + nci:6 + nci]
        k_slots, b_slots = refs[6 + nci + nco:8 + nci + nco]
        comm_first, comm_last = _comm_run(comm, (B, REC_HEADS // HGRN_PAIR), refs, 4, 2)
        comm_first()
        tril, _, _, below, inside, col = _hgrn_consts()
        col_s = col & (SUB - 1)

        def chunk(ci, hts, slot):
            k_s, b_s = k_slots.at[slot], b_slots.at[slot]
            r0 = pl.multiple_of(ci * CHUNK, CHUNK)
            lb = _sigmoid(lb_ref[0:1, :] - lb_ref[1:2, :])
            _, _, g_all, k_all = _hgrn_gates(z_ref[pl.ds(r0, CHUNK), :], lb)
            b_all = _sel_left(tril, g_all)
            k_s[...] = k_all
            b_s[...] = b_all
            new = []
            for e, ht in enumerate(hts):
                lanes = pl.ds(REC_DIM * e, REC_DIM)
                cols = slice(REC_DIM * e, REC_DIM * (e + 1))
                q = q_ref[pl.ds(r0, CHUNK), lanes]
                v = v_ref[pl.ds(r0, CHUNK), lanes]
                k, bcum = k_all[:, cols], b_all[:, cols]
                st_ref[e * nc + ci] = ht
                qst, kst, _, _ = _hgrn_offdiag(q, k, bcum, b_s, lanes)
                d = jnp.zeros((CHUNK, CHUNK), F32)
                for s in range(SUB):
                    w = jnp.exp(jnp.minimum(bcum - _block_rows(b_s, lanes, s), 0.0))
                    colv = jnp.sum(q * _block_rows(k_s, lanes, s) * w, axis=-1, keepdims=True)
                    d = jnp.where(col_s == s, colv, d)
                a = jnp.where(below, _dot_nt(qst.astype(BF16), kst.astype(BF16)), 0.0) + jnp.where(inside, d, 0.0)
                vb = v.astype(BF16)
                qb = (q * jnp.exp(bcum)).astype(BF16)
                o_ref[pl.ds(r0, CHUNK), lanes] = _dot(a.astype(BF16), vb) + _dot_nt(qb, ht.astype(BF16))
                b_last = b_s[pl.ds(CHUNK - 1, 1), lanes]
                kb = (k * jnp.exp(b_last - bcum)).astype(BF16)
                new.append(ht * jnp.exp(b_last) + _dot_tn(vb, kb))
            return tuple(new)

        def trip(i, hts):
            for u in range(fwd_unroll):
                hts = chunk(fwd_unroll * i + u, hts, u)
            return hts

        lax.fori_loop(0, nc // fwd_unroll, trip, tuple(jnp.zeros((REC_DIM, REC_DIM), F32) for _ in range(HGRN_PAIR)))
        comm_last()

    hp, wd = REC_HEADS // HGRN_PAIR, HGRN_PAIR * REC_DIM
    cq, cf, ci_ = (c * REC_DIM // wd for c in (COL_RQ, COL_RF, COL_RI))
    return pl.pallas_call(
        body, name="hgrn_fwd", grid=(B, hp),
        in_specs=[pl.BlockSpec((S, wd), lambda b, h: (b, cq + h)),
                  pl.BlockSpec((S, wd), lambda b, h: (b, cf + h)),
                  pl.BlockSpec((S, wd), lambda b, h: (b, ci_ + h)),
                  pl.BlockSpec((2, wd), lambda b, h: (0, h))] + c_in_specs,
        out_specs=[pl.BlockSpec((S, wd), lambda b, h: (b, h)),
                   pl.BlockSpec((HGRN_PAIR * nc, REC_DIM, REC_DIM), lambda b, h: (b * hp + h, 0, 0))] + [ANY] * nco,
        out_shape=[jax.ShapeDtypeStruct((T, 512), F32),
                   jax.ShapeDtypeStruct((B * REC_HEADS * nc, REC_DIM, REC_DIM), F32)] + c_out_shapes,
        scratch_shapes=[pltpu.VMEM((fwd_unroll, CHUNK, wd), F32), pltpu.VMEM((fwd_unroll, CHUNK, wd), F32)] + c_sems,
        compiler_params=_params("arbitrary", "arbitrary"))(proj, proj, proj, lb_param, *c_arrays)


def _hgrn_bwd(proj, lb_param, states, do, B, S, comm=None):
    T = B * S
    nc = S // CHUNK
    bwd_unroll = 4 if nc % 4 == 0 else 2
    c_arrays, c_in_specs, c_out_shapes, c_sems = _comm_parts(comm)
    nci, nco = len(c_arrays), len(c_out_shapes)

    def body(*refs):
        q_ref, z_ref, v_ref, lb_ref, st_ref, do_ref = refs[:6]
        dq_ref, dz_ref, dv_ref, dlb_ref = refs[6 + nci:10 + nci]
        slots = refs[10 + nci + nco:14 + nci + nco]
        comm_first, comm_last = _comm_run(comm, (B, REC_HEADS // HGRN_PAIR), refs, 6, 4)
        comm_first()
        tril, triu, causal, below, inside, col = _hgrn_consts()
        col_s = col & (SUB - 1)
        last_row = lax.broadcasted_iota(jnp.int32, (CHUNK, 1), 0) == CHUNK - 1
        rc = lax.broadcasted_iota(jnp.int32, (CHUNK, SUB * REC_DIM), 0)
        lc = lax.broadcasted_iota(jnp.int32, (CHUNK, SUB * REC_DIM), 1)
        spread = ((rc & (SUB - 1)) == (lc // REC_DIM)).astype(BF16)
        rr = lax.broadcasted_iota(jnp.int32, (CHUNK, SUB * CHUNK), 0)
        cc = lax.broadcasted_iota(jnp.int32, (CHUNK, SUB * CHUNK), 1)
        gather = (((rr // SUB) == ((cc & (CHUNK - 1)) // SUB)) & ((rr & (SUB - 1)) == (cc // CHUNK))).astype(BF16)

        heads = range(HGRN_PAIR)
        cols = [slice(REC_DIM * e, REC_DIM * (e + 1)) for e in heads]
        lanes = [pl.ds(REC_DIM * e, REC_DIM) for e in heads]
        lane_cat = lambda vals: jnp.concatenate(vals, axis=1)
        row_cat = lambda vals: jnp.concatenate(vals, axis=0)

        def chunk(it, carry, slot):
            k_s, b_s, pc_hi, pc_lo = (r.at[slot] for r in slots)
            dhts, dlb = carry
            ci = nc - 1 - it
            r0 = pl.multiple_of(ci * CHUNK, CHUNK)
            rows = pl.ds(r0, CHUNK)
            lb = _sigmoid(lb_ref[0:1, :] - lb_ref[1:2, :])
            sg, f, g_all, k_all = _hgrn_gates(z_ref[rows, :], lb)
            b_all = _sel_left(tril, g_all)
            k_s[...] = k_all
            b_s[...] = b_all
            q_all = q_ref[rows, :]
            das, hd = [], []
            for e in heads:
                vb, dob = v_ref[rows, lanes[e]].astype(BF16), do_ref[rows, lanes[e]].astype(BF16)
                da = jnp.where(causal, _dot_nt(dob, vb), 0.0)
                das.append(jnp.where(inside, da, 0.0))
                hd.append((vb, dob, da))
            da_hi, da_lo = _split2(row_cat(das))
            da_in = _dot(da_hi, spread) + _dot(da_lo, spread)
            ds, dqs = [], []
            for e in heads:
                q, bcum = q_all[:, cols[e]], b_all[:, cols[e]]
                d = jnp.zeros((CHUNK, CHUNK), F32)
                dq = jnp.zeros((CHUNK, REC_DIM), F32)
                for s in range(SUB):
                    w = jnp.exp(jnp.minimum(bcum - _block_rows(b_s, lanes[e], s), 0.0))
                    ks = _block_rows(k_s, lanes[e], s)
                    qw = q * w
                    d = jnp.where(col_s == s, jnp.sum(qw * ks, axis=-1, keepdims=True), d)
                    da_s = da_in[CHUNK * e:CHUNK * (e + 1), REC_DIM * s:REC_DIM * (s + 1)]
                    dq = dq + da_s * ks * w
                    hi, lo = _split2(da_s * qw)
                    pc_hi[pl.ds(CHUNK * s, CHUNK), lanes[e]] = hi
                    pc_lo[pl.ds(CHUNK * s, CHUNK), lanes[e]] = lo
                ds.append(d)
                dqs.append(dq)
            dk_in = _dot(gather, pc_hi[...]) + _dot(gather, pc_lo[...])
            dq_out, dk_out, dv_out, db_out, new_dhts = [], [], [], [], []
            for e in heads:
                q, k, bcum = q_all[:, cols[e]], k_all[:, cols[e]], b_all[:, cols[e]]
                vb, dob, da = hd[e]
                dht, ht = dhts[e], st_ref[e * nc + ci]
                qst, kst, eq, eks = _hgrn_offdiag(q, k, bcum, b_s, lanes[e])
                qst_b, kst_b = qst.astype(BF16), kst.astype(BF16)
                a = jnp.where(below, _dot_nt(qst_b, kst_b), 0.0) + jnp.where(inside, ds[e], 0.0)
                da_off = jnp.where(below, da, 0.0).astype(BF16)
                dqst = _dot(da_off, kst_b)
                dkst = _dot_tn(da_off, qst_b)
                dk = dk_in[:, cols[e]]
                dq_rows = [jnp.zeros((SUB, REC_DIM), F32)]
                for i in range(1, N_SUB):
                    dq_rows.append(dqst[SUB * i:SUB * (i + 1), REC_DIM * (i - 1):REC_DIM * i])
                    dk = dk + dkst[:, REC_DIM * (i - 1):REC_DIM * i] * eks[i - 1]
                dq = dqs[e] + row_cat(dq_rows) * eq
                eb = jnp.exp(bcum)
                b_last = b_s[pl.ds(CHUNK - 1, 1), lanes[e]]
                el = jnp.exp(b_last)
                ekb = jnp.exp(b_last - bcum)
                qb = (q * eb).astype(BF16)
                kb = k * ekb
                dhb = dht.astype(BF16)
                dv_out.append(_dot_tn(a.astype(BF16), dob) + _dot_nt(kb.astype(BF16), dhb))
                dqb = _dot(dob, ht.astype(BF16))
                dkb = _dot(vb, dhb)
                new_dhts.append(dht * el + _dot_tn(dob, qb))
                dq = dq + eb * dqb
                dk = dk + ekb * dkb
                edge = jnp.sum(kb * dkb, axis=0, keepdims=True) + el * jnp.sum(ht * dht, axis=0, keepdims=True)
                db_out.append(q * dq - k * dk + jnp.where(last_row, edge, 0.0))
                dq_out.append(dq)
                dk_out.append(dk)
            dk_all = lane_cat(dk_out)
            db_hi, db_lo = _split2(lane_cat(db_out))
            dg = _dot(triu, db_hi) + _dot(triu, db_lo)
            df = dg / f - dk_all
            dz_ref[rows, :] = (df * (1.0 - lb) * sg * (1.0 - sg)).astype(dz_ref.dtype)
            dq_ref[rows, :] = lane_cat(dq_out).astype(dq_ref.dtype)
            dv_ref[rows, :] = lane_cat(dv_out).astype(dv_ref.dtype)
            return tuple(new_dhts), dlb + jnp.sum(df * (1.0 - sg), axis=0, keepdims=True)

        zero = (tuple(jnp.zeros((REC_DIM, REC_DIM), F32) for _ in heads), jnp.zeros((1, HGRN_PAIR * REC_DIM), F32))
        def trip(i, carry):
            for u in range(bwd_unroll):
                carry = chunk(bwd_unroll * i + u, carry, u)
            return carry

        _, dlb = lax.fori_loop(0, nc // bwd_unroll, trip, zero)
        lb = _sigmoid(lb_ref[0:1, :] - lb_ref[1:2, :])
        dlb_ref[...] = jnp.broadcast_to(dlb * lb * (1.0 - lb), (8, HGRN_PAIR * REC_DIM))
        comm_last()

    hp, wd = REC_HEADS // HGRN_PAIR, HGRN_PAIR * REC_DIM
    cq, cf, ci_ = (c * REC_DIM // wd for c in (COL_RQ, COL_RF, COL_RI))
    return pl.pallas_call(
        body, name="hgrn_bwd", grid=(B, hp),
        in_specs=[pl.BlockSpec((S, wd), lambda b, h: (b, cq + h)),
                  pl.BlockSpec((S, wd), lambda b, h: (b, cf + h)),
                  pl.BlockSpec((S, wd), lambda b, h: (b, ci_ + h)),
                  pl.BlockSpec((2, wd), lambda b, h: (0, h)),
                  pl.BlockSpec((HGRN_PAIR * nc, REC_DIM, REC_DIM), lambda b, h: (b * hp + h, 0, 0)),
                  pl.BlockSpec((S, wd), lambda b, h: (b, h))] + c_in_specs,
        out_specs=[pl.BlockSpec((S, wd), lambda b, h: (b, h))] * 3
        + [pl.BlockSpec((8, wd), lambda b, h: (b, h))] + [ANY] * nco,
        out_shape=[jax.ShapeDtypeStruct((T, 512), BF16)] * 3 + [jax.ShapeDtypeStruct((B * 8, 512), F32)]
        + c_out_shapes,
        scratch_shapes=[pltpu.VMEM((bwd_unroll, CHUNK, wd), F32)] * 2
        + [pltpu.VMEM((bwd_unroll, SUB * CHUNK, wd), BF16)] * 2 + c_sems,
        compiler_params=_params("arbitrary", "arbitrary"))(proj, proj, proj, lb_param, states, do, *c_arrays)


def _rec_gate_fwd(rec, proj, rec_norm):
    T = rec.shape[0]

    def fn(accs, tv, cv):
        return [_rms_hat(tv[0]) * cv[0] * _sigmoid(tv[1].astype(F32))]

    return _tile_call("rec_gate", fn, T, 512, _pick(T, 1024), REC_DIM, tiles=[(rec, 0), (proj, COL_RG)],
                      consts=[rec_norm], outs=[BF16])[0]


def _rec_gate_bwd(dyb, w_rec_proj, rec, proj, rec_norm):
    T = rec.shape[0]

    def fn(accs, tv, cv):
        d, r, rg = accs[0], tv[0], tv[1].astype(F32)
        sg = _sigmoid(rg)
        rn = _rms_hat(r) * cv[0]
        dh, dg = _rms_bwd_vals(d * sg, r, cv[0])
        return [dh, d * rn * sg * (1.0 - sg), dg]

    return _tile_call("rec_gate_bwd", fn, T, 512, _pick(T, 1024), REC_DIM, pairs=[(dyb, 0, w_rec_proj, "nt")],
                      tiles=[(rec, 0), (proj, COL_RG)], consts=[rec_norm], outs=[F32, BF16], parts=1)


def _mix_out_fwd(att, recn, proj, w_att_proj, w_rec_proj, w_out, h1, g_next):
    T = att.shape[0]
    tn = 256

    def merge(accs, tv, cv):
        ya, yb = accs
        return [ya, yb, _sigmoid(tv[0].astype(F32)) * ya + _sigmoid(tv[1].astype(F32)) * yb]

    ya, yb, merged = _tile_call(
        "merge", merge, T, D_MODEL, _pick(T, 1024), tn,
        pairs=[(att, 0, w_att_proj, "nn"), (recn, 0, w_rec_proj, "nn")],
        tiles=[(proj, COL_GA * 128 // tn), (proj, COL_GB * 128 // tn)], outs=[BF16] * 3)

    def res(accs, tv, cv):
        h2 = tv[0] + accs[0]
        return [h2, _rms_hat(h2) * cv[0]]

    h2, n2 = _tile_call("mix_out", res, T, D_MODEL, _pick(T, 512), D_MODEL, pairs=[(merged, 0, w_out, "nn")],
                        tiles=[(h1, 0)], consts=[g_next], outs=[F32, BF16])
    return h2, n2, (ya, yb, merged)


GATHER_FIRST = ("w_ffn1_in",)
GATHER_MIX = ("w_ffn1_out", "w_in")
GATHER_PROJ = ("w_att_proj", "w_rec_proj", "w_out")
GATHER_LAST = ("w_ffn2_in", "w_ffn2_out", "w_ple_gate", "w_ple_proj")
SCATTER_LATE = ("w_ple_gate", "w_ple_proj", "w_ffn2_in", "w_ffn2_out")
SCATTER_MIX = ("w_out", "w_att_proj", "w_rec_proj", "w_in")
SCATTER_LAST = ("w_ffn1_in", "w_ffn1_out")


def _local_step(x, p, tgt, w, mine16, cc, me_chip, B, S):
    T = B * S
    w = dict(w)
    g_ffn1, g_mix, g_ffn2, g_ple = w["norm_ffn1"], w["norm_mix"], w["norm_ffn2"], w["norm_ple"]
    g_fin = w["norm_final"].reshape(1, D_MODEL)
    grads, part, from_chips = {}, {}, {}

    def gather(names):
        return _gather_comm([mine16[n] for n in names])

    def place(names, got):
        for n, g in zip(names, got):
            full = lax.dynamic_update_index_in_dim(g, mine16[n], me_chip, 0)
            w[n] = full if n in ("w_ffn1_in", "w_ffn2_in") else _natural(n, full)

    def swap(names):
        return _swap_comm([grads[n][1] for n in names])

    def after_swap(names, from_sib):
        for n, fs in zip(names, from_sib):
            part[n] = _add_sibling("rs_add_sib_" + n, grads[n][0], fs, cc)
        return _scatter_comm([part[n][1] for n in names])

    def scattered(names, got):
        for n, g in zip(names, got):
            from_chips[n] = g

    def ffn1_in_weight(got):
        place(GATHER_FIRST, got)
        return w["w_ffn1_in"]

    def ffn1_out_weight(got):
        place(GATHER_MIX, got)
        return w["w_ffn1_out"]

    h1, u, sv1, got_proj = _ffn_fwd("ffn1", x, g_ffn1, None, None, g_mix, comm_norm=gather(GATHER_FIRST),
                                    w_in_of=ffn1_in_weight, comm_in=gather(GATHER_MIX),
                                    comm_out=gather(GATHER_PROJ), w_out_of=ffn1_out_weight)
    place(GATHER_PROJ, got_proj)

    def ident(accs, tv, cv):
        return [accs[0]]

    w_in = w["w_in"]
    p_att = _tile_call("in_proj_att", ident, T, ATT_W, _pick(T, 1024), ATT_W,
                       pairs=[(u, 0, w_in[:, :ATT_W], "nn")], outs=[BF16])[0]
    p_rec = _tile_call("in_proj_rec", ident, T, REC_W, _pick(T, 1024), REC_W // 2,
                       pairs=[(u, 0, w_in[:, ATT_W:ATT_W + REC_W], "nn")], outs=[F32], j_outer=True)[0]
    p_gate = _tile_call("in_proj_gate", ident, T, GATE_W, _pick(T, 1024), GATE_W // 2,
                        pairs=[(u, 0, w_in[:, ATT_W + REC_W:], "nn")], outs=[BF16], j_outer=True)[0]
    onehot = jnp.asarray(_t5_onehot())
    bias = _small_mm("t5_bias", w["rel_bias"].T, onehot.astype(BF16), "right")
    bias = bias.reshape(N_Q_HEADS, ATT_BLOCK, 2 * ATT_BLOCK)
    sinks = w["attn_sinks"].reshape(N_Q_HEADS)
    kk2, vv2 = _kv_layouts(p_att)
    att = _swa_fwd(p_att, kk2, vv2, bias, sinks, B, S)
    rec, states, *got_last = _hgrn_fwd(p_rec, w["lb_param"], B, S, comm=gather(GATHER_LAST))
    place(GATHER_LAST, got_last)
    recn = _rec_gate_fwd(rec, p_gate, w["rec_norm"])
    h2, n2, (ya, yb, merged) = _mix_out_fwd(att, recn, p_gate, w["w_att_proj"], w["w_rec_proj"], w["w_out"], h1,
                                            g_ffn2)
    h3, n3, sv2, _ = _ffn_fwd("ffn2", h2, g_ffn2, w["w_ffn2_in"], w["w_ffn2_out"], g_ple, n=n2)

    def ple(accs, tv, cv):
        gate = _sigmoid(accs[0])
        return [gate, accs[1], tv[0] + gate * accs[1]]

    gate_p, pp, h4 = _tile_call(
        "ple", ple, T, D_MODEL, _pick(T, 512), D_MODEL,
        pairs=[(n3, 0, w["w_ple_gate"], "nn"), (p, 0, w["w_ple_proj"], "nn")], tiles=[(h3, 0)],
        outs=[BF16, BF16, F32])

    def head(accs, tv, cv):
        h, t, gt, ppv = tv[0], tv[1], tv[2].astype(F32), tv[3].astype(F32)
        err = _rms_hat(h) * cv[0] - t
        dh, dg = _rms_bwd_vals(err * (1.0 / D_MODEL), h, cv[0])
        return [dh, dh * ppv * gt * (1.0 - gt), dh * gt, _group8(err * err), dg]

    dh4, dzg, dpp, loss_p, dg_fin = _tile_call(
        "loss_head", head, T, D_MODEL, _pick(T, 256), D_MODEL,
        tiles=[(h4, 0), (tgt, 0), (gate_p, 0), (pp, 0)], consts=[g_fin], outs=[F32, BF16, BF16], parts=2)
    grads["norm_final"] = dg_fin

    grads["w_ple_gate"] = _mm_tn_rows("ple_dwg", n3, dzg)
    grads["w_ple_proj"] = _mm_tn_cols("ple_dwp", p, dpp)

    def dnorm(accs, tv, cv):
        dh, dg = _rms_bwd_vals(accs[0], tv[0], cv[0])
        dh = tv[1] + dh
        return [dh, 0.5 * dh, dg]

    dh3, df3, grads["norm_ple"] = _tile_call(
        "ple_dnorm", dnorm, T, D_MODEL, _pick(T, 512), D_MODEL, pairs=[(dzg, 0, w["w_ple_gate"], "nt")],
        tiles=[(h3, 0), (dh4, 0)], consts=[g_ple], outs=[F32, BF16], parts=1)

    def swap_late(dw_in, dw_out):
        grads["w_ffn2_in"], grads["w_ffn2_out"] = dw_in, dw_out
        return swap(SCATTER_LATE)

    dh2, dh2b, grads["norm_ffn2"], _, _, _, from_sib = _ffn_bwd(
        "ffn2b", dh3, df3, h2, g_ffn2, w["w_ffn2_in"], w["w_ffn2_out"], sv2, comm_last=swap_late)
    scatter_late = after_swap(SCATTER_LATE, from_sib)

    grads["w_out"] = _mm_tn_rows("mix_dwout", merged, dh2b)
    tn = 256

    def dmerge(accs, tv, cv):
        dm = accs[0]
        sa, sb = _sigmoid(tv[0].astype(F32)), _sigmoid(tv[1].astype(F32))
        yav, ybv = tv[2].astype(F32), tv[3].astype(F32)
        return [dm * sa, dm * sb, dm * yav * sa * (1.0 - sa), dm * ybv * sb * (1.0 - sb)]

    dya, dyb, dga, dgb = _tile_call(
        "mix_dmerge", dmerge, T, D_MODEL, _pick(T, 1024), tn, pairs=[(dh2b, 0, w["w_out"], "nt")],
        tiles=[(p_gate, COL_GA * 128 // tn), (p_gate, COL_GB * 128 // tn), (ya, 0), (yb, 0)], outs=[BF16] * 4)
    grads["w_att_proj"] = _mm_tn_cols("mix_dwatt", att, dya)
    grads["w_rec_proj"] = _mm_tn_cols("mix_dwrec", recn, dyb)

    datt = _tile_call("mix_datt", ident, T, 512, _pick(T, 1024), 512, pairs=[(dya, 0, w["w_att_proj"], "nt")],
                      outs=[BF16])[0]
    drec, drg, grads["rec_norm"] = _rec_gate_bwd(dyb, w["w_rec_proj"], rec, p_gate, w["rec_norm"])

    drq, drf, dri, dlb, *got = _hgrn_bwd(p_rec, w["lb_param"], states, drec, B, S, comm=scatter_late)
    scattered(SCATTER_LATE, got)
    grads["lb_param"] = dlb
    daq, dak, dav, dbias, dsink = _swa_bwd(p_att, kk2, vv2, bias, sinks, datt, B, S)
    grads["attn_sinks"] = dsink
    grads["rel_bias"] = _small_mm("t5_dbias", dbias.reshape(N_Q_HEADS, -1), onehot.T.astype(BF16), "right")
    dproj = jnp.concatenate([daq, dak, dav, drq, drf, dri, drg, dga, dgb], axis=1)
    tk = _pick(T, 2048, 128)
    w_in_shard = IN_W // N_CHIPS
    half_d = D_MODEL // 2
    gw32, gw16 = _mm_tn("mix_dwin", (2, 2, T // tk),
                        (u, (tk, half_d), lambda i, j, k: (k, i)), (dproj, (tk, IN_W // 2), lambda i, j, k: (k, j)),
                        _grad_pair((D_MODEL, IN_W), (half_d, IN_W // 2), lambda i, j, k: (i, j)))
    to_sh = lambda t: t.reshape(D_MODEL, N_CHIPS, w_in_shard).transpose(1, 0, 2)
    grads["w_in"] = (to_sh(gw32), to_sh(gw16))

    def dnorm_mix(accs, tv, cv):
        dh, dg = _rms_bwd_vals(accs[0], tv[0], cv[0])
        dh = tv[1] + dh
        return [dh, 0.5 * dh, dg]

    dh1, df1, grads["norm_mix"], *from_sib = _tile_call(
        "mix_dnorm", dnorm_mix, T, D_MODEL, _pick(T, 512), D_MODEL, pairs=[(dproj, 0, w["w_in"], "nt")],
        tiles=[(h1, 0), (dh2, 0)], consts=[g_mix], outs=[F32, BF16], parts=1, comm=swap(SCATTER_MIX))
    scatter_mix = after_swap(SCATTER_MIX, from_sib)

    def scatter_last(dw_in, dw_out):
        grads["w_ffn1_in"], grads["w_ffn1_out"] = dw_in, dw_out
        return after_swap(SCATTER_LAST, _run_comm("rs_sibling_last", swap(SCATTER_LAST)))

    dx, _, grads["norm_ffn1"], _, _, got, got_last = _ffn_bwd(
        "ffn1b", dh1, df1, x, g_ffn1, w["w_ffn1_in"], w["w_ffn1_out"], sv1, comm=scatter_mix, comm_last=scatter_last)
    scattered(SCATTER_MIX, got)
    scattered(SCATTER_LAST, got_last)
    return loss_p, dx, grads, part, from_chips


def _place():
    x, y, c = lax.axis_index("x"), lax.axis_index("y"), lax.axis_index("c")
    return x, y, c


def _other_chips(x, y):
    return [(1 - x, y, 2 * (1 - x) + y), (x, 1 - y, 2 * x + 1 - y), (1 - x, 1 - y, 2 * (1 - x) + 1 - y)]


def _half_rows(ref_3d, chip, h, rows):
    return ref_3d.at[chip, pl.ds(h * rows, rows), :]


def _gather_comm(ws):
    nw = len(ws)

    def parts(w_refs, out_refs, send_sems, recv_sems):
        x, y, c = _place()
        me = 2 * x + y
        chips = _other_chips(x, y)

        def copy(i, k, chip, h, to, src=None):
            half = ws[i].shape[0] // 2
            dst = _half_rows(out_refs[i], chip, h, half)
            return pltpu.make_async_remote_copy(
                src_ref=dst if src is None else src, dst_ref=dst,
                send_sem=send_sems.at[6 * i + k], recv_sem=recv_sems.at[6 * i + k], device_id=to, device_id_type=MESH)

        def first():
            out = []
            for i in range(nw):
                half = ws[i].shape[0] // 2
                out += [copy(i, j, me, c, (cx, cy, c), src=w_refs[i].at[pl.ds(c * half, half), :])
                        for j, (cx, cy, _) in enumerate(chips)]
            return out

        return copy, first, chips, c, (x, y, 1 - c)

    def start(*refs):
        _, first, _, _, _ = parts(*refs)
        for cp in first():
            cp.start()

    def finish(*refs):
        copy, first, chips, c, sibling = parts(*refs)
        passed = []
        for i in range(nw):
            for j, (cx, cy, ci) in enumerate(chips):
                copy(i, j, ci, c, (cx, cy, c)).wait_recv()
                fw = copy(i, 3 + j, ci, c, sibling)
                fw.start()
                passed.append(fw)
        for i in range(nw):
            for j, (_, _, ci) in enumerate(chips):
                copy(i, 3 + j, ci, 1 - c, sibling).wait_recv()
        for cp in first() + passed:
            cp.wait_send()

    return _Comm(list(ws), [jax.ShapeDtypeStruct((N_CHIPS,) + w.shape, w.dtype) for w in ws], 6 * nw, start, finish)


def _scatter_comm(ps):
    nw = len(ps)

    def copies(p_refs, out_refs, send_sems, recv_sems):
        x, y, c = _place()
        cps = []
        for i in range(nw):
            for j, (cx, cy, ci) in enumerate(_other_chips(x, y)):
                cps.append(pltpu.make_async_remote_copy(
                    src_ref=p_refs[i].at[ci], dst_ref=out_refs[i].at[j], send_sem=send_sems.at[3 * i + j],
                    recv_sem=recv_sems.at[3 * i + j], device_id=(cx, cy, c), device_id_type=MESH))
        return cps

    def start(*refs):
        for cp in copies(*refs):
            cp.start()

    def finish(*refs):
        for cp in copies(*refs):
            cp.wait()

    return _Comm(list(ps), [jax.ShapeDtypeStruct((3,) + p.shape[1:], p.dtype) for p in ps], 3 * nw, start, finish)


def _swap_comm(gs):
    nw = len(gs)

    def copies(g_refs, out_refs, send_sems, recv_sems):
        x, y, c = _place()
        cps = []
        for i in range(nw):
            half = gs[i].shape[1] // 2
            cps.append(pltpu.make_async_remote_copy(
                src_ref=g_refs[i].at[:, pl.ds((1 - c) * half, half), :], dst_ref=out_refs[i],
                send_sem=send_sems.at[i], recv_sem=recv_sems.at[i], device_id=(x, y, 1 - c), device_id_type=MESH))
        return cps

    def start(*refs):
        for cp in copies(*refs):
            cp.start()

    def finish(*refs):
        for cp in copies(*refs):
            cp.wait()

    return _Comm(list(gs), [jax.ShapeDtypeStruct((N_CHIPS, g.shape[1] // 2, g.shape[2]), g.dtype) for g in gs],
                 nw, start, finish)


def _run_comm(name, comm):
    nci, nco = len(comm.ins), len(comm.out_shapes)

    def body(*refs):
        cin, cout, send_sems, recv_sems = refs[:nci], refs[nci:nci + nco], refs[-2], refs[-1]
        comm.start(cin, cout, send_sems, recv_sems)
        comm.finish(cin, cout, send_sems, recv_sems)

    return pl.pallas_call(
        body, name=name, in_specs=[ANY] * nci, out_specs=[ANY] * nco, out_shape=list(comm.out_shapes),
        scratch_shapes=[pltpu.SemaphoreType.DMA((comm.n_sems,)), pltpu.SemaphoreType.DMA((comm.n_sems,))],
    )(*comm.ins)


def _join_halves(name, ss):
    nw = len(ss)

    def body(*refs):
        s_refs, out_refs, send_sems, recv_sems = refs[:nw], refs[nw:2 * nw], refs[2 * nw], refs[2 * nw + 1]
        x, y, c = _place()
        cps = [pltpu.make_async_remote_copy(
            src_ref=s_refs[i], dst_ref=out_refs[i], send_sem=send_sems.at[i], recv_sem=recv_sems.at[i],
            device_id=(x, y, 1 - c), device_id_type=MESH) for i in range(nw)]
        for cp in cps:
            cp.start()
        for cp in cps:
            cp.wait()

    return pl.pallas_call(
        body, name=name, in_specs=[ANY] * nw, out_specs=[ANY] * nw,
        out_shape=[jax.ShapeDtypeStruct(s.shape, s.dtype) for s in ss],
        scratch_shapes=[pltpu.SemaphoreType.DMA((nw,)), pltpu.SemaphoreType.DMA((nw,))],
    )(*ss)


def _allreduce_small(sp):
    def body(s_ref, out_ref, slots, send_sems, recv_sems):
        x, y, c = _place()
        me = 4 * x + 2 * y + c
        slots[me] = s_ref[...]
        cps = []
        for r in range(1, N_DEV):
            px, py, pc = x ^ (r >> 2), y ^ ((r >> 1) & 1), c ^ (r & 1)
            cps.append(pltpu.make_async_remote_copy(
                src_ref=s_ref, dst_ref=slots.at[me], send_sem=send_sems.at[r - 1], recv_sem=recv_sems.at[r - 1],
                device_id=(px, py, pc), device_id_type=MESH))
        for cp in cps:
            cp.start()
        for r in range(1, N_DEV):
            px, py, pc = x ^ (r >> 2), y ^ ((r >> 1) & 1), c ^ (r & 1)
            pltpu.make_async_remote_copy(
                src_ref=s_ref, dst_ref=slots.at[4 * px + 2 * py + pc], send_sem=send_sems.at[r - 1],
                recv_sem=recv_sems.at[r - 1], device_id=(px, py, pc), device_id_type=MESH).wait_recv()
        for cp in cps:
            cp.wait_send()
        acc = slots[0]
        for d in range(1, N_DEV):
            acc = acc + slots[d]
        out_ref[...] = acc

    return pl.pallas_call(
        body, name="allreduce_small",
        in_specs=[pl.BlockSpec(memory_space=pltpu.VMEM)], out_specs=pl.BlockSpec(memory_space=pltpu.VMEM),
        out_shape=jax.ShapeDtypeStruct(sp.shape, F32),
        scratch_shapes=[pltpu.VMEM((N_DEV,) + sp.shape, F32), pltpu.SemaphoreType.DMA((N_DEV - 1,)),
                        pltpu.SemaphoreType.DMA((N_DEV - 1,))],
    )(sp)


def _scalar(v):
    return jnp.reshape(v, (1,)).astype(jnp.int32)


def _row_tile(h, dtype_mult=16):
    return _pick(h, 256, dtype_mult)


def _add_sibling(name, g32, from_sib, c):
    _, r, n = g32.shape
    h = r // 2
    th = _row_tile(h)
    nt = h // th

    def body(c_ref, g_ref, s_ref, o32_ref, o16_ref):
        s = g_ref[...] + s_ref[...].astype(F32)
        o32_ref[...] = s
        o16_ref[...] = s.astype(BF16)

    blk = (None, th, n)
    return pl.pallas_call(
        body, name=name,
        grid_spec=pltpu.PrefetchScalarGridSpec(
            num_scalar_prefetch=1, grid=(N_CHIPS, nt),
            in_specs=[pl.BlockSpec(blk, lambda k, t, c_ref: (k, c_ref[0] * nt + t, 0)),
                      pl.BlockSpec(blk, lambda k, t, c_ref: (k, t, 0))],
            out_specs=[pl.BlockSpec(blk, lambda k, t, c_ref: (k, t, 0))] * 2),
        out_shape=[jax.ShapeDtypeStruct((N_CHIPS, h, n), F32), jax.ShapeDtypeStruct((N_CHIPS, h, n), BF16)],
        compiler_params=_params("arbitrary", "arbitrary"))(_scalar(c), g32, from_sib)


def _add_chips(name, p32, from_chips, me_chip):
    _, h, n = p32.shape
    th = _row_tile(h)

    def body(m_ref, p_ref, a_ref, b_ref, c_ref, o_ref):
        o_ref[...] = p_ref[...] + a_ref[...].astype(F32) + b_ref[...].astype(F32) + c_ref[...].astype(F32)

    blk = (None, th, n)
    return pl.pallas_call(
        body, name=name,
        grid_spec=pltpu.PrefetchScalarGridSpec(
            num_scalar_prefetch=1, grid=(h // th,),
            in_specs=[pl.BlockSpec(blk, lambda t, m_ref: (m_ref[0], t, 0))]
            + [pl.BlockSpec(blk, lambda t, m_ref, j=j: (j, t, 0)) for j in range(3)],
            out_specs=pl.BlockSpec((th, n), lambda t, m_ref: (t, 0))),
        out_shape=jax.ShapeDtypeStruct((h, n), F32),
        compiler_params=_params("arbitrary"))(_scalar(me_chip), p32, from_chips, from_chips, from_chips)


def _adamw_vals(w, g, m, v):
    m = ADAM_B1 * m + (1.0 - ADAM_B1) * g
    v = ADAM_B2 * v + (1.0 - ADAM_B2) * (g * g)
    m_hat = m / (1.0 - ADAM_B1 ** ADAM_STEP)
    v_hat = v / (1.0 - ADAM_B2 ** ADAM_STEP)
    delta = -ADAM_LR * (m_hat / (jnp.sqrt(v_hat) + ADAM_EPS) + ADAM_WD * w)
    return delta, m, v


def _adamw_halves(name, w, m, v, g_mine, g_sib, c):
    r, n = w.shape
    h = r // 2
    th = _row_tile(h, 8)
    nt = h // th

    def body(c_ref, w_ref, m_ref, v_ref, a_ref, b_ref, g_ref, d_ref, nm_ref, nv_ref):
        mine = (pl.program_id(0) // nt) == c_ref[0]
        g = jnp.where(mine, a_ref[...], b_ref[...])
        d, nm, nv = _adamw_vals(w_ref[...], g, m_ref[...], v_ref[...])
        g_ref[...] = g
        d_ref[...] = d
        nm_ref[...] = nm
        nv_ref[...] = nv

    full = pl.BlockSpec((th, n), lambda t, c_ref: (t, 0))
    part = pl.BlockSpec((th, n), lambda t, c_ref: (t % nt, 0))
    return pl.pallas_call(
        body, name=name,
        grid_spec=pltpu.PrefetchScalarGridSpec(
            num_scalar_prefetch=1, grid=(2 * nt,), in_specs=[full, full, full, part, part], out_specs=[full] * 4),
        out_shape=[jax.ShapeDtypeStruct((r, n), F32)] * 4,
        compiler_params=_params("arbitrary"))(_scalar(c), w, m, v, g_mine, g_sib)


def _adamw(name, w, g, m, v):
    R, W = w.shape

    def fn(accs, tv, cv):
        return list(_adamw_vals(*tv))

    return _tile_call(name, fn, R, W, _pick(R, 256), W, tiles=[(w, 0), (g, 0), (m, 0), (v, 0)], outs=[F32] * 3)


SMALL_LAYOUT = (("rel_bias", 2, 256), ("lb_param", 8, 1024), ("norm_ffn1", 8, 1024), ("norm_mix", 8, 1024),
                ("attn_sinks", 1, 8), ("rec_norm", 1, 128), ("norm_ffn2", 8, 1024), ("norm_ple", 8, 1024),
                ("norm_final", 8, 1024), ("loss", 8, 1024))


def _pack_small(vals):
    rows = []
    for name, nrows, n in SMALL_LAYOUT:
        flat = vals[name].reshape(-1)
        flat = jnp.pad(flat, (0, nrows * 128 - n))
        rows.append(flat.reshape(nrows, 128))
    packed = jnp.concatenate(rows, axis=0)
    return jnp.pad(packed, ((0, SMALL_ROWS - packed.shape[0]), (0, 0)))


def _unpack_small(packed, shapes):
    out, r = {}, 0
    for name, nrows, n in SMALL_LAYOUT:
        out[name] = packed[r:r + nrows].reshape(-1)[:n].reshape(shapes[name])
        r += nrows
    return out


def _natural(name, s):
    if name in COL_SHARDED:
        return s.transpose(1, 0, 2).reshape(s.shape[1], -1)
    return s.reshape(-1, s.shape[2])


def kernel(x, p, rel_bias, lb_param, norm_ffn1, w_ffn1_in, w_ffn1_out, norm_mix, w_in, attn_sinks, rec_norm, w_att_proj, w_rec_proj, w_out, norm_ffn2, w_ffn2_in, w_ffn2_out, norm_ple, w_ple_gate, w_ple_proj, norm_final, loss_target, m_rel_bias, m_lb_param, m_norm_ffn1, m_w_ffn1_in, m_w_ffn1_out, m_norm_mix, m_w_in, m_attn_sinks, m_rec_norm, m_w_att_proj, m_w_rec_proj, m_w_out, m_norm_ffn2, m_w_ffn2_in, m_w_ffn2_out, m_norm_ple, m_w_ple_gate, m_w_ple_proj, m_norm_final, v_rel_bias, v_lb_param, v_norm_ffn1, v_w_ffn1_in, v_w_ffn1_out, v_norm_mix, v_w_in, v_attn_sinks, v_rec_norm, v_w_att_proj, v_w_rec_proj, v_w_out, v_norm_ffn2, v_w_ffn2_in, v_w_ffn2_out, v_norm_ple, v_w_ple_gate, v_w_ple_proj, v_norm_final):
    args = dict(locals())
    wsh = {n: args[n] for n in WEIGHTS}
    B, S = x.shape[0], x.shape[1]
    T = B * S
    cx, cy, cc = _place()
    me_chip = 2 * cx + cy

    mine16 = {n: wsh[n][0].astype(BF16) for n in BIG}
    loss_p, dx, grads, part, from_chips = _local_step(
        x.reshape(T, D_MODEL), p.reshape(T, PLE_DIM), loss_target.reshape(T, D_MODEL),
        {n: wsh[n] for n in SMALL}, mine16, cc, me_chip, B, S)

    s_mine = [_add_chips("rs_add_chips_" + n, part[n][0], from_chips[n], me_chip) for n in BIG]
    s_sib = _join_halves("rs_join", s_mine)

    small_vals = {
        "rel_bias": grads["rel_bias"].T,
        "lb_param": jnp.concatenate([_colsum("dlb_sum", grads["lb_param"]),
                                     -_colsum("dlb_sum2", grads["lb_param"])], axis=0) / 8.0,
        "attn_sinks": grads["attn_sinks"][:, 0],
        "rec_norm": _colsum("drn_sum", grads["rec_norm"]).reshape(REC_HEADS, REC_DIM).sum(axis=0),
        "loss": _colsum("loss_sum", loss_p),
    }
    for n in ("norm_ffn1", "norm_mix", "norm_ffn2", "norm_ple", "norm_final"):
        small_vals[n] = _colsum(n + "_sum", grads[n])
    red = _allreduce_small(_pack_small(small_vals))
    small_shapes = {n: wsh[n].shape for n in SMALL}
    small_shapes["loss"] = (D_MODEL,)
    small = _unpack_small(red, small_shapes)
    loss = 0.5 * jnp.sum(small["loss"]) / D_MODEL

    out_g, out_d, out_m, out_v = {}, {}, {}, {}
    for n, gm, gs in zip(BIG, s_mine, s_sib):
        res = _adamw_halves("adamw_" + n, wsh[n][0], args["m_" + n][0], args["v_" + n][0], gm, gs, cc)
        out_g[n], out_d[n], out_m[n], out_v[n] = (t[None] for t in res)
    sw = _pack_small({**{n: wsh[n] for n in SMALL}, "loss": jnp.zeros((D_MODEL,), F32)})
    sm = _pack_small({**{n: args["m_" + n] for n in SMALL}, "loss": jnp.zeros((D_MODEL,), F32)})
    sv = _pack_small({**{n: args["v_" + n] for n in SMALL}, "loss": jnp.ones((D_MODEL,), F32)})
    sd, snm, snv = _adamw("adamw_small", sw, red, sm, sv)
    ud, um, uv = (_unpack_small(t, small_shapes) for t in (sd, snm, snv))
    for n in SMALL:
        out_g[n], out_d[n], out_m[n], out_v[n] = small[n], ud[n], um[n], uv[n]

    return (loss, dx.reshape(B, S, D_MODEL), *[out_g[n] for n in WEIGHTS], *[out_d[n] for n in WEIGHTS],
            *[out_m[n] for n in WEIGHTS], *[out_v[n] for n in WEIGHTS])
```

```python
import numpy as np
import jax
import jax.numpy as jnp
from jax import lax
from jax.experimental import pallas as pl
from jax.experimental.pallas import tpu as pltpu

F32 = jnp.float32
BF16 = jnp.bfloat16
MESH = pl.DeviceIdType.MESH

D_MODEL = 1024
D_FF = 2816
FF_SHARD = 2 * D_FF // 4
HEAD_DIM = 64
N_Q_HEADS = 8
ATT_BLOCK = 128
N_BUCKETS = 32
MAX_DISTANCE = 128
REC_HEADS = 4
REC_DIM = 128
PLE_DIM = 256
EPS = 1e-6
IN_W = 4864
COL_AQ, COL_AK, COL_AV, COL_RQ, COL_RF, COL_RI, COL_RG, COL_GA, COL_GB = 0, 4, 5, 6, 10, 14, 18, 22, 30

CHUNK = 64
SUB = 8
N_SUB = CHUNK // SUB
HGRN_PAIR = 2

ADAM_LR, ADAM_B1, ADAM_B2, ADAM_EPS, ADAM_WD, ADAM_STEP = 0.001, 0.9, 0.999, 1e-08, 0.01, 10

V7X_VMEM_LIMIT = 56 * 1024 * 1024
N_CHIPS = 4
N_DEV = 8
LOCAL_PARTS = 4

BIG = ("w_ffn1_in", "w_ffn1_out", "w_in", "w_att_proj", "w_rec_proj", "w_out",
       "w_ffn2_in", "w_ffn2_out", "w_ple_gate", "w_ple_proj")
COL_SHARDED = ("w_ffn1_in", "w_in", "w_att_proj", "w_rec_proj", "w_ffn2_in", "w_ple_proj")
WEIGHTS = ("rel_bias", "lb_param", "norm_ffn1", "w_ffn1_in", "w_ffn1_out", "norm_mix", "w_in", "attn_sinks",
           "rec_norm", "w_att_proj", "w_rec_proj", "w_out", "norm_ffn2", "w_ffn2_in", "w_ffn2_out", "norm_ple",
           "w_ple_gate", "w_ple_proj", "norm_final")
SMALL = tuple(n for n in WEIGHTS if n not in BIG)
SMALL_ROWS = 64


def _params(*sem):
    return pltpu.CompilerParams(dimension_semantics=sem, vmem_limit_bytes=V7X_VMEM_LIMIT)


def _pick(n, cap, mult=8):
    if n <= cap:
        return n
    for t in range(cap - cap % mult, 0, -mult):
        if n % t == 0:
            return t
    raise ValueError((n, cap, mult))


def _dot(a, b):
    return jnp.dot(a, b, preferred_element_type=F32)


def _dot_nt(a, b):
    return lax.dot_general(a, b, (((1,), (1,)), ((), ())), preferred_element_type=F32)


def _dot_tn(a, b):
    return lax.dot_general(a, b, (((0,), (0,)), ((), ())), preferred_element_type=F32)


def _split3(x):
    hi = x.astype(BF16)
    r = x - hi.astype(F32)
    mid = r.astype(BF16)
    lo = (r - mid.astype(F32)).astype(BF16)
    return hi, mid, lo


def _split2(x):
    hi = x.astype(BF16)
    return hi, (x - hi.astype(F32)).astype(BF16)


def _sel_left(sel_bf16, x):
    hi, mid, lo = _split3(x)
    return _dot(sel_bf16, hi) + _dot(sel_bf16, mid) + _dot(sel_bf16, lo)


def _sel_right(x, sel_bf16):
    hi, mid, lo = _split3(x)
    return _dot(hi, sel_bf16) + _dot(mid, sel_bf16) + _dot(lo, sel_bf16)


def _sigmoid(x):
    return 0.5 * jnp.tanh(0.5 * x) + 0.5


def _group8(x):
    r, w = x.shape
    return x.reshape(r // 8, 8, w).sum(axis=0)


class _Comm:
    def __init__(self, ins, out_shapes, n_sems, start, finish):
        self.ins, self.out_shapes, self.n_sems, self.start, self.finish = ins, out_shapes, n_sems, start, finish


ANY = pl.BlockSpec(memory_space=pl.ANY)


def _comm_parts(comm):
    if comm is None:
        return [], [], [], []
    sems = [pltpu.SemaphoreType.DMA((comm.n_sems,)), pltpu.SemaphoreType.DMA((comm.n_sems,))]
    return list(comm.ins), [ANY] * len(comm.ins), list(comm.out_shapes), sems


def _comm_run(comm, grid, refs, n_in, n_out):
    if comm is None:
        return (lambda: None), (lambda: None)
    nci, nco = len(comm.ins), len(comm.out_shapes)
    cin = refs[n_in:n_in + nci]
    cout = refs[n_in + nci + n_out:n_in + nci + n_out + nco]
    send_sems, recv_sems = refs[-2], refs[-1]
    ids = [pl.program_id(d) for d in range(len(grid))]
    is_first = ids[0] == 0
    is_last = ids[0] == grid[0] - 1
    for d in range(1, len(grid)):
        is_first = is_first & (ids[d] == 0)
        is_last = is_last & (ids[d] == grid[d] - 1)

    def first():
        @pl.when(is_first)
        def _():
            comm.start(cin, cout, send_sems, recv_sems)

    def last():
        @pl.when(is_last)
        def _():
            comm.finish(cin, cout, send_sems, recv_sems)

    return first, last


def _call(name, fn, grid, ins, outs, pairs=(), comm=None, j_outer=False):
    in_pair = {i for p in pairs for i in p[:2]}
    n_in, n_out = len(ins), len(outs)
    c_arrays, c_in_specs, c_out_shapes, c_sems = _comm_parts(comm)

    def body(*refs):
        first, last = _comm_run(comm, grid, refs, n_in, n_out)
        first()
        accs = []
        for ia, ib, kind in pairs:
            a, b = refs[ia][...].astype(BF16), refs[ib][...].astype(BF16)
            accs.append(_dot(a, b) if kind == "nn" else _dot_nt(a, b))
        vals = [refs[i][...] for i in range(n_in) if i not in in_pair]
        res = fn(accs, vals)
        out_refs = refs[n_in + len(c_arrays):n_in + len(c_arrays) + n_out]
        assert len(res) == len(out_refs), (name, len(res), len(out_refs))
        for o_ref, val in zip(out_refs, res):
            o_ref[...] = val.astype(o_ref.dtype)
        last()

    if j_outer:
        grid = (grid[1], grid[0])
        swap = lambda im: (lambda j, i: im(i, j))
        ins = [(a, blk, swap(im)) for a, blk, im in ins]
        outs = [(shp, dt, blk, swap(im)) for shp, dt, blk, im in outs]

    return pl.pallas_call(
        body, name=name, grid=grid,
        in_specs=[pl.BlockSpec(blk, im) for _, blk, im in ins] + c_in_specs,
        out_specs=[pl.BlockSpec(blk, im) for _, _, blk, im in outs] + [ANY] * len(c_out_shapes),
        out_shape=[jax.ShapeDtypeStruct(shp, dt) for shp, dt, _, _ in outs] + c_out_shapes,
        scratch_shapes=c_sems,
        compiler_params=_params(*(["arbitrary"] * len(grid))))(*[a for a, _, _ in ins], *c_arrays)


def _tile_call(name, fn, M, N, tm, tn, *, pairs=(), tiles=(), consts=(), outs=(), parts=0, comm=None,
               j_outer=False):
    gi, gj = M // tm, N // tn
    assert gi * tm == M and gj * tn == N, (name, M, N, tm, tn)
    ins, prs = [], []
    for a, a_col, b, kind in pairs:
        K = b.shape[0] if kind == "nn" else b.shape[1]
        ins.append((a, (tm, K), lambda i, j, c=a_col: (i, c)))
        if kind == "nn":
            ins.append((b, (K, tn), lambda i, j: (0, j)))
        else:
            ins.append((b, (tn, K), lambda i, j: (j, 0)))
        prs.append((len(ins) - 2, len(ins) - 1, kind))
    for arr, off in tiles:
        ins.append((arr, (tm, tn), lambda i, j, o=off: (i, j + o)))
    for arr in consts:
        ins.append((arr, arr.shape, lambda i, j: (0, 0)))
    out_l = [((M, N), dt, (tm, tn), lambda i, j: (i, j)) for dt in outs]
    out_l += [((gi * 8, N), F32, (8, tn), lambda i, j: (i, j))] * parts
    nt = len(tiles)

    def wrapped(accs, vals):
        return fn(accs, vals[:nt], vals[nt:])

    return _call(name, wrapped, (gi, gj), ins, out_l, prs, comm=comm, j_outer=j_outer)


def _mm_tn(name, grid, a_in, b_in, outs):
    nk = grid[2]
    tm = [d for d in a_in[1] if d is not None][1]
    tn = [d for d in b_in[1] if d is not None][1]

    def body(a_ref, b_ref, *rest):
        out_refs, acc_ref = rest[:-1], rest[-1]
        k = pl.program_id(2)

        @pl.when(k == 0)
        def _():
            acc_ref[...] = jnp.zeros_like(acc_ref)

        acc_ref[...] += _dot_tn(a_ref[...].astype(BF16), b_ref[...].astype(BF16))

        @pl.when(k == nk - 1)
        def _():
            for o_ref in out_refs:
                o_ref[...] = acc_ref[...].astype(o_ref.dtype)

    return pl.pallas_call(
        body, name=name, grid=grid,
        in_specs=[pl.BlockSpec(a_in[1], a_in[2]), pl.BlockSpec(b_in[1], b_in[2])],
        out_specs=[pl.BlockSpec(blk, im) for _, _, blk, im in outs],
        out_shape=[jax.ShapeDtypeStruct(shp, dt) for shp, dt, _, _ in outs],
        scratch_shapes=[pltpu.VMEM((tm, tn), F32)],
        compiler_params=_params("arbitrary", "arbitrary", "arbitrary"))(a_in[0], b_in[0])


def _grad_pair(shape, block, imap):
    return [(shape, F32, block, imap), (shape, BF16, block, imap)]


def _mm_tn_rows(name, a, b, tk=2048):
    T, a_w = a.shape
    b_w = b.shape[1]
    tm = _pick(a_w, 1408, 128)
    tk = _pick(T, tk, 128)
    g32, g16 = _mm_tn(name, (a_w // tm, 1, T // tk),
                      (a, (tk, tm), lambda i, j, k: (k, i)), (b, (tk, b_w), lambda i, j, k: (k, 0)),
                      _grad_pair((a_w, b_w), (tm, b_w), lambda i, j, k: (i, 0)))
    shp = (N_CHIPS, a_w // N_CHIPS, b_w)
    return g32.reshape(shp), g16.reshape(shp)


def _mm_tn_cols(name, a, b, tk=4096):
    T, a_w = a.shape
    b_w = b.shape[1]
    tk = _pick(T, tk, 128)
    g32, g16 = _mm_tn(name, (1, 1, T // tk),
                      (a, (tk, a_w), lambda i, j, k: (k, 0)), (b, (tk, b_w), lambda i, j, k: (k, 0)),
                      _grad_pair((a_w, b_w), (a_w, b_w), lambda i, j, k: (0, 0)))
    to_sh = lambda t: t.reshape(a_w, N_CHIPS, b_w // N_CHIPS).transpose(1, 0, 2)
    return to_sh(g32), to_sh(g16)


def _colsum(name, x):
    def body(x_ref, o_ref):
        o_ref[...] = jnp.sum(x_ref[...], axis=0, keepdims=True)
    return pl.pallas_call(body, name=name, out_shape=jax.ShapeDtypeStruct((1, x.shape[1]), F32))(x)


def _rms_hat(h):
    return h * lax.rsqrt(jnp.mean(h * h, axis=-1, keepdims=True) + EPS)


def _rms_bwd_vals(dn, h, g):
    r = lax.rsqrt(jnp.mean(h * h, axis=-1, keepdims=True) + EPS)
    nh = h * r
    gd = dn * g
    dh = r * (gd - nh * jnp.mean(gd * nh, axis=-1, keepdims=True))
    return dh, _group8(dn * nh)


def _rms_fwd(name, h, g, tm=512, comm=None):
    T = h.shape[0]

    def fn(accs, tv, cv):
        return [_rms_hat(tv[0]) * cv[0]]

    return _tile_call(name, fn, T, D_MODEL, _pick(T, tm), D_MODEL, tiles=[(h, 0)], consts=[g], outs=[BF16],
                      comm=comm)


def _ffn_fwd(tag, h, g, w_in, w_out, g_next, n=None, comm_norm=None, w_in_of=None, comm_in=None, comm_out=None,
             w_out_of=None):
    T = h.shape[0]
    if n is None:
        n, *got_norm = _rms_fwd(tag + "_norm", h, g, comm=comm_norm)
        if w_in_of is not None:
            w_in = w_in_of(got_norm)
    tm = _pick(T, 1024)
    wblk = (None, D_MODEL, FF_SHARD)

    def act(accs, vals):
        gate, up = accs
        return [gate, up, gate * _sigmoid(gate) * up]

    tile = lambda: ((T, D_FF), BF16, (tm, FF_SHARD), lambda i, j: (i, j))
    gate, up, a, *got_in = _call(
        tag + "_in", act, (T // tm, 2),
        [(n, (tm, D_MODEL), lambda i, j: (i, 0)),
         (w_in, wblk, lambda i, j: (j, 0, 0)), (w_in, wblk, lambda i, j: (j + 2, 0, 0))],
        [tile(), tile(), tile()], pairs=[(0, 1, "nn"), (0, 2, "nn")], comm=comm_in, j_outer=True)

    def res(accs, tv, cv):
        h_new = tv[0] + 0.5 * accs[0]
        return [h_new, _rms_hat(h_new) * cv[0]]

    if w_out_of is not None:
        w_out = w_out_of(got_in)
    h_new, n_next, *got_out = _tile_call(
        tag + "_out", res, T, D_MODEL, _pick(T, 512), D_MODEL, pairs=[(a, 0, w_out, "nn")], tiles=[(h, 0)],
        consts=[g_next], outs=[F32, BF16], comm=comm_out)
    return h_new, n_next, (n, gate, up, a), got_out


def _ffn_bwd(tag, dh_out, df, h, g, w_in, w_out, saved, comm=None, comm_last=None):
    T = h.shape[0]
    n, gate, up, a = saved
    tm = _pick(T, 512)

    def dact(accs, vals):
        da = accs[0]
        gt, u = vals[0].astype(F32), vals[1].astype(F32)
        sg = _sigmoid(gt)
        silu = gt * sg
        return [jnp.stack([(da * u * (sg + silu * (1.0 - sg))).astype(BF16), (da * silu).astype(BF16)])]

    dz, *got = _call(
        tag + "_dact", dact, (T // tm, 2),
        [(df, (tm, D_MODEL), lambda i, j: (i, 0)), (w_out, (FF_SHARD, D_MODEL), lambda i, j: (j, 0)),
         (gate, (tm, FF_SHARD), lambda i, j: (i, j)), (up, (tm, FF_SHARD), lambda i, j: (i, j))],
        [((2, T, D_FF), BF16, (2, tm, FF_SHARD), lambda i, j: (0, i, j))], pairs=[(0, 1, "nt")], comm=comm,
        j_outer=True)
    dw_out = _mm_tn_rows(tag + "_dwout", a, df)
    tk = _pick(T, 2048, 128)
    dw_in = _mm_tn(tag + "_dwin", (1, N_CHIPS, T // tk),
                   (n, (tk, D_MODEL), lambda i, j, k: (k, 0)),
                   (dz, (None, tk, FF_SHARD), lambda i, j, k: (j // 2, k, j % 2)),
                   _grad_pair((N_CHIPS, D_MODEL, FF_SHARD), (None, D_MODEL, FF_SHARD), lambda i, j, k: (j, 0, 0)))

    def dnorm(accs, vals):
        dn = accs[0] + accs[1] + accs[2] + accs[3]
        dh, dg = _rms_bwd_vals(dn, vals[0], vals[2])
        dh = vals[1] + dh
        return [dh, dh, dg]

    tm2 = _pick(T, 512)
    ins = [(dz, (None, tm2, FF_SHARD), lambda i, j, s=s: (s // 2, i, s % 2)) for s in range(N_CHIPS)]
    ins += [(w_in, (None, D_MODEL, FF_SHARD), lambda i, j, s=s: (s, 0, 0)) for s in range(N_CHIPS)]
    ins += [(h, (tm2, D_MODEL), lambda i, j: (i, 0)), (dh_out, (tm2, D_MODEL), lambda i, j: (i, 0)),
            (g, g.shape, lambda i, j: (0, 0))]
    dh, dh16, dg, *got_last = _call(
        tag + "_dnorm", dnorm, (T // tm2, 1), ins,
        [((T, D_MODEL), F32, (tm2, D_MODEL), lambda i, j: (i, 0)),
         ((T, D_MODEL), BF16, (tm2, D_MODEL), lambda i, j: (i, 0)),
         ((T // tm2 * 8, D_MODEL), F32, (8, D_MODEL), lambda i, j: (i, 0))],
        pairs=[(s, N_CHIPS + s, "nt") for s in range(N_CHIPS)],
        comm=None if comm_last is None else comm_last(dw_in, dw_out))
    return dh, dh16, dg, dw_in, dw_out, got, got_last


def _t5_onehot():
    qi = np.arange(ATT_BLOCK)[:, None] + ATT_BLOCK
    kj = np.arange(2 * ATT_BLOCK)[None, :]
    nn = np.maximum(qi - kj, 0)
    max_exact = N_BUCKETS // 2
    large = max_exact + (np.log(np.maximum(nn, 1) / max_exact) / np.log(MAX_DISTANCE / max_exact)
                         * (N_BUCKETS - max_exact)).astype(np.int32)
    large = np.minimum(large, N_BUCKETS - 1)
    bucket = np.where(nn < max_exact, nn, large).astype(np.int32).reshape(-1)
    return (bucket[None, :] == np.arange(N_BUCKETS)[:, None]).astype(np.float32)


def _small_mm(name, a, b, sel):
    def body(a_ref, b_ref, o_ref):
        if sel == "right":
            o_ref[...] = _sel_right(a_ref[...], b_ref[...])
        else:
            o_ref[...] = _sel_left(a_ref[...], b_ref[...])
    return pl.pallas_call(body, name=name, out_shape=jax.ShapeDtypeStruct((a.shape[0], b.shape[1]), F32),
                          compiler_params=pltpu.CompilerParams(vmem_limit_bytes=V7X_VMEM_LIMIT))(a, b)


def _dup_heads(t):
    a, b = t[:, :HEAD_DIM], t[:, HEAD_DIM:]
    return jnp.concatenate([a, a, b, b], axis=1)


def _kv_layouts(proj):
    T = proj.shape[0]

    def fn(accs, tv, cv):
        return [tv[0], tv[1]]

    k, v = _tile_call("kv_cast", fn, T, 128, _pick(T, 1024), 128, tiles=[(proj, COL_AK), (proj, COL_AV)],
                      outs=[BF16, BF16])
    return _dup_heads(k), _dup_heads(v)


def _swa_masks():
    row = lax.broadcasted_iota(jnp.int32, (ATT_BLOCK, 2 * ATT_BLOCK), 0)
    col = lax.broadcasted_iota(jnp.int32, (ATT_BLOCK, 2 * ATT_BLOCK), 1)
    dist = ATT_BLOCK + row - col
    return (dist >= 0) & (dist < ATT_BLOCK), col


GROUP = 4


def _stack_group(blk, lo_q):
    zero = jnp.zeros_like(blk[:, :128])
    rows = []
    for pair in range(GROUP // 2):
        pb = blk[:, 128 * pair:128 * (pair + 1)]
        rows += [jnp.where(lo_q, pb, zero), jnp.where(lo_q, zero, pb)]
    return jnp.concatenate(rows, axis=0)


def _unstack_group(st, lo_q):
    pairs = [jnp.where(lo_q, st[256 * pair:256 * pair + 128], st[256 * pair + 128:256 * (pair + 1)])
             for pair in range(GROUP // 2)]
    return jnp.concatenate(pairs, axis=1)


def _swa_probs(s, bias_h, sink, valid):
    s = jnp.where(valid, s * (HEAD_DIM ** -0.5) + bias_h, -jnp.inf)
    m = jnp.maximum(jnp.max(s, axis=-1, keepdims=True), sink)
    e = jnp.exp(s - m)
    es = jnp.exp(sink - m)
    den = jnp.sum(e, axis=-1, keepdims=True) + es
    return e / den, es / den


def _swa_fwd(proj, kk2, vv2, bias, sinks, B, S):
    T = B * S
    nb = S // ATT_BLOCK

    def body(q_ref, k_ref, v_ref, bias_ref, sink_ref, o_ref, kpad, vpad):
        zeros = jnp.zeros((ATT_BLOCK, 256), BF16)
        kpad[pl.ds(0, ATT_BLOCK), :] = zeros
        vpad[pl.ds(0, ATT_BLOCK), :] = zeros
        kpad[pl.ds(ATT_BLOCK, S), :] = k_ref[...]
        vpad[pl.ds(ATT_BLOCK, S), :] = v_ref[...]
        valid0, col = _swa_masks()
        lo_q = lax.broadcasted_iota(jnp.int32, (1, 128), 1) < HEAD_DIM

        def blk(n, carry):
            r0 = pl.multiple_of(n * ATT_BLOCK, ATT_BLOCK)
            rows = pl.ds(r0, ATT_BLOCK)
            valid = valid0 & ((n > 0) | (col >= ATT_BLOCK))
            for g in range(N_Q_HEADS // GROUP):
                lanes = pl.ds(128 * g, 128)
                kg = kpad[pl.ds(r0, 2 * ATT_BLOCK), lanes]
                vg = vpad[pl.ds(r0, 2 * ATT_BLOCK), lanes]
                qm = _stack_group(q_ref[rows, pl.ds(256 * g, 256)].astype(BF16), lo_q)
                s = _dot_nt(qm, kg)
                ps = []
                for i in range(GROUP):
                    h = GROUP * g + i
                    p, _ = _swa_probs(s[ATT_BLOCK * i:ATT_BLOCK * (i + 1)], bias_ref[h], sink_ref[h], valid)
                    ps.append(p.astype(BF16))
                o = _dot(jnp.concatenate(ps, axis=0), vg)
                o_ref[rows, pl.ds(256 * g, 256)] = _unstack_group(o, lo_q).astype(o_ref.dtype)
            return carry

        if nb % 2 == 0:
            lax.fori_loop(0, nb // 2, lambda i, c: blk(2 * i + 1, blk(2 * i, c)), 0)
        else:
            lax.fori_loop(0, nb, blk, 0)

    return pl.pallas_call(
        body, name="swa_fwd", grid=(B,),
        in_specs=[pl.BlockSpec((S, 512), lambda b: (b, 0)),
                  pl.BlockSpec((S, 256), lambda b: (b, 0)),
                  pl.BlockSpec((S, 256), lambda b: (b, 0)),
                  pl.BlockSpec((N_Q_HEADS, ATT_BLOCK, 2 * ATT_BLOCK), lambda b: (0, 0, 0)),
                  pl.BlockSpec(memory_space=pltpu.SMEM)],
        out_specs=pl.BlockSpec((S, 512), lambda b: (b, 0)),
        out_shape=jax.ShapeDtypeStruct((T, 512), BF16),
        scratch_shapes=[pltpu.VMEM((S + ATT_BLOCK, 256), BF16), pltpu.VMEM((S + ATT_BLOCK, 256), BF16)],
        compiler_params=_params("arbitrary"))(proj, kk2, vv2, bias, sinks)


def _swa_bwd(proj, kk2, vv2, bias, sinks, datt, B, S):
    T = B * S
    nb = S // ATT_BLOCK

    def body(q_ref, k_ref, v_ref, bias_ref, sink_ref, do_ref, dq_ref, dk_ref, dv_ref, dbias_ref, dsink_ref,
             kpad, vpad, dkpad, dvpad):
        b = pl.program_id(0)

        @pl.when(b == 0)
        def _():
            dbias_ref[...] = jnp.zeros_like(dbias_ref)
            dsink_ref[...] = jnp.zeros_like(dsink_ref)

        zeros = jnp.zeros((ATT_BLOCK, 256), BF16)
        kpad[pl.ds(0, ATT_BLOCK), :] = zeros
        vpad[pl.ds(0, ATT_BLOCK), :] = zeros
        kpad[pl.ds(ATT_BLOCK, S), :] = k_ref[...]
        vpad[pl.ds(ATT_BLOCK, S), :] = v_ref[...]
        dkpad[...] = jnp.zeros_like(dkpad)
        dvpad[...] = jnp.zeros_like(dvpad)
        valid0, col = _swa_masks()
        lo_q = lax.broadcasted_iota(jnp.int32, (1, 128), 1) < HEAD_DIM
        scale = HEAD_DIM ** -0.5

        def blk(n, carry):
            r0 = pl.multiple_of(n * ATT_BLOCK, ATT_BLOCK)
            rows = pl.ds(r0, ATT_BLOCK)
            band = pl.ds(r0, 2 * ATT_BLOCK)
            valid = valid0 & ((n > 0) | (col >= ATT_BLOCK))
            for g in range(N_Q_HEADS // GROUP):
                lanes = pl.ds(128 * g, 128)
                kg = kpad[band, lanes]
                vg = vpad[band, lanes]
                qm = _stack_group(q_ref[rows, pl.ds(256 * g, 256)].astype(BF16), lo_q)
                dom = _stack_group(do_ref[rows, pl.ds(256 * g, 256)], lo_q)
                s = _dot_nt(qm, kg)
                dp = _dot_nt(dom, vg)
                pst, dst = [], []
                for i in range(GROUP):
                    h = GROUP * g + i
                    sl = slice(ATT_BLOCK * i, ATT_BLOCK * (i + 1))
                    p, ps = _swa_probs(s[sl], bias_ref[h], sink_ref[h], valid)
                    delta = jnp.sum(p * dp[sl], axis=-1, keepdims=True)
                    ds = p * (dp[sl] - delta)
                    dbias_ref[h] += ds
                    dsink_ref[pl.ds(h, 1), :] += -jnp.sum(jnp.broadcast_to(ps * delta, (ATT_BLOCK, 128)),
                                                          axis=0, keepdims=True)
                    pst.append(p.astype(BF16))
                    dst.append((ds * scale).astype(BF16))
                pst, dst = jnp.concatenate(pst, axis=0), jnp.concatenate(dst, axis=0)
                dq_ref[rows, pl.ds(256 * g, 256)] = _unstack_group(_dot(dst, kg), lo_q).astype(dq_ref.dtype)
                dkpad[band, lanes] += _dot_tn(dst, qm)
                dvpad[band, lanes] += _dot_tn(pst, dom)
            return carry

        if nb % 2 == 0:
            lax.fori_loop(0, nb // 2, lambda i, c: blk(2 * i + 1, blk(2 * i, c)), 0)
        else:
            lax.fori_loop(0, nb, blk, 0)
        lo_out = lax.broadcasted_iota(jnp.int32, (1, 128), 1) < HEAD_DIM

        def fold(pad_ref):
            halves = []
            for g in range(N_Q_HEADS // GROUP):
                t = pad_ref[pl.ds(ATT_BLOCK, S), pl.ds(128 * g, 128)]
                halves.append(t + pltpu.roll(t, HEAD_DIM, 1))
            return jnp.where(lo_out, halves[0], halves[1])

        dk_ref[...] = fold(dkpad).astype(dk_ref.dtype)
        dv_ref[...] = fold(dvpad).astype(dv_ref.dtype)

    return pl.pallas_call(
        body, name="swa_bwd", grid=(B,),
        in_specs=[pl.BlockSpec((S, 512), lambda b: (b, 0)),
                  pl.BlockSpec((S, 256), lambda b: (b, 0)),
                  pl.BlockSpec((S, 256), lambda b: (b, 0)),
                  pl.BlockSpec((N_Q_HEADS, ATT_BLOCK, 2 * ATT_BLOCK), lambda b: (0, 0, 0)),
                  pl.BlockSpec(memory_space=pltpu.SMEM),
                  pl.BlockSpec((S, 512), lambda b: (b, 0))],
        out_specs=[pl.BlockSpec((S, 512), lambda b: (b, 0)),
                   pl.BlockSpec((S, 128), lambda b: (b, 0)),
                   pl.BlockSpec((S, 128), lambda b: (b, 0)),
                   pl.BlockSpec((N_Q_HEADS, ATT_BLOCK, 2 * ATT_BLOCK), lambda b: (0, 0, 0)),
                   pl.BlockSpec((N_Q_HEADS, 128), lambda b: (0, 0))],
        out_shape=[jax.ShapeDtypeStruct((T, 512), BF16),
                   jax.ShapeDtypeStruct((T, 128), BF16),
                   jax.ShapeDtypeStruct((T, 128), BF16),
                   jax.ShapeDtypeStruct((N_Q_HEADS, ATT_BLOCK, 2 * ATT_BLOCK), F32),
                   jax.ShapeDtypeStruct((N_Q_HEADS, 128), F32)],
        scratch_shapes=[pltpu.VMEM((S + ATT_BLOCK, 256), BF16), pltpu.VMEM((S + ATT_BLOCK, 256), BF16),
                        pltpu.VMEM((S + ATT_BLOCK, 256), F32), pltpu.VMEM((S + ATT_BLOCK, 256), F32)],
        compiler_params=_params("arbitrary"))(proj, kk2, vv2, bias, sinks, datt)


def _hgrn_gates(z, lb):
    sg = _sigmoid(z)
    f = lb + (1.0 - lb) * sg
    return sg, f, jnp.log(f), 1.0 - f


def _hgrn_consts():
    r = lax.broadcasted_iota(jnp.int32, (CHUNK, CHUNK), 0)
    c = lax.broadcasted_iota(jnp.int32, (CHUNK, CHUNK), 1)
    tril = (r >= c).astype(BF16)
    triu = (r <= c).astype(BF16)
    causal = r >= c
    below = (r // SUB) > (c // SUB)
    inside = ((r // SUB) == (c // SUB)) & causal
    return tril, triu, causal, below, inside, c


def _block_rows(ref, lanes, s):
    rows = []
    for i in range(N_SUB):
        if SUB * i + s < 0:
            rows.append(jnp.zeros((SUB, REC_DIM), F32))
        else:
            rows.append(jnp.broadcast_to(ref[pl.ds(SUB * i + s, 1), lanes], (SUB, REC_DIM)))
    return jnp.concatenate(rows, axis=0)


def _hgrn_offdiag(q, k, bcum, b_ref, lanes):
    eq = jnp.exp(jnp.minimum(bcum - _block_rows(b_ref, lanes, -1), 0.0))
    qe = q * eq
    zero = jnp.zeros((SUB, REC_DIM), F32)
    q_rows, k_cols, eks = [jnp.zeros((SUB, (N_SUB - 1) * REC_DIM), F32)], [], []
    for i in range(1, N_SUB):
        q_rows.append(jnp.concatenate([zero] * (i - 1) + [qe[SUB * i:SUB * (i + 1), :]] + [zero] * (N_SUB - 1 - i),
                                      axis=1))
        p = b_ref[pl.ds(SUB * i - 1, 1), lanes]
        pad = jnp.zeros((CHUNK - SUB * i, REC_DIM), F32)
        ek = jnp.concatenate([jnp.exp(p - b_ref[pl.ds(0, SUB * i), lanes]), pad], axis=0)
        k_cols.append(k * ek)
        eks.append(ek)
    return jnp.concatenate(q_rows, axis=0), jnp.concatenate(k_cols, axis=1), eq, eks


def _hgrn_fwd(proj, lb_param, B, S, comm=None):
    T = B * S
    nc = S // CHUNK
    fwd_unroll = 4 if nc % 4 == 0 else 2
    c_arrays, c_in_specs, c_out_shapes, c_sems = _comm_parts(comm)
    nci, nco = len(c_arrays), len(c_out_shapes)

    def body(*refs):
        q_ref, z_ref, v_ref, lb_ref = refs[:4]
        o_ref, st_ref = refs[4 + nci:6 + nci]
        k_slots, b_slots = refs[6 + nci + nco:8 + nci + nco]
        comm_first, comm_last = _comm_run(comm, (B, REC_HEADS // HGRN_PAIR), refs, 4, 2)
        comm_first()
        tril, _, _, below, inside, col = _hgrn_consts()
        col_s = col & (SUB - 1)

        def chunk(ci, hts, slot):
            k_s, b_s = k_slots.at[slot], b_slots.at[slot]
            r0 = pl.multiple_of(ci * CHUNK, CHUNK)
            lb = _sigmoid(lb_ref[0:1, :] - lb_ref[1:2, :])
            _, _, g_all, k_all = _hgrn_gates(z_ref[pl.ds(r0, CHUNK), :], lb)
            b_all = _sel_left(tril, g_all)
            k_s[...] = k_all
            b_s[...] = b_all
            new = []
            for e, ht in enumerate(hts):
                lanes = pl.ds(REC_DIM * e, REC_DIM)
                cols = slice(REC_DIM * e, REC_DIM * (e + 1))
                q = q_ref[pl.ds(r0, CHUNK), lanes]
                v = v_ref[pl.ds(r0, CHUNK), lanes]
                k, bcum = k_all[:, cols], b_all[:, cols]
                st_ref[e * nc + ci] = ht
                qst, kst, _, _ = _hgrn_offdiag(q, k, bcum, b_s, lanes)
                d = jnp.zeros((CHUNK, CHUNK), F32)
                for s in range(SUB):
                    w = jnp.exp(jnp.minimum(bcum - _block_rows(b_s, lanes, s), 0.0))
                    colv = jnp.sum(q * _block_rows(k_s, lanes, s) * w, axis=-1, keepdims=True)
                    d = jnp.where(col_s == s, colv, d)
                a = jnp.where(below, _dot_nt(qst.astype(BF16), kst.astype(BF16)), 0.0) + jnp.where(inside, d, 0.0)
                vb = v.astype(BF16)
                qb = (q * jnp.exp(bcum)).astype(BF16)
                o_ref[pl.ds(r0, CHUNK), lanes] = _dot(a.astype(BF16), vb) + _dot_nt(qb, ht.astype(BF16))
                b_last = b_s[pl.ds(CHUNK - 1, 1), lanes]
                kb = (k * jnp.exp(b_last - bcum)).astype(BF16)
                new.append(ht * jnp.exp(b_last) + _dot_tn(vb, kb))
            return tuple(new)

        def trip(i, hts):
            for u in range(fwd_unroll):
                hts = chunk(fwd_unroll * i + u, hts, u)
            return hts

        lax.fori_loop(0, nc // fwd_unroll, trip, tuple(jnp.zeros((REC_DIM, REC_DIM), F32) for _ in range(HGRN_PAIR)))
        comm_last()

    hp, wd = REC_HEADS // HGRN_PAIR, HGRN_PAIR * REC_DIM
    cq, cf, ci_ = (c * REC_DIM // wd for c in (COL_RQ, COL_RF, COL_RI))
    return pl.pallas_call(
        body, name="hgrn_fwd", grid=(B, hp),
        in_specs=[pl.BlockSpec((S, wd), lambda b, h: (b, cq + h)),
                  pl.BlockSpec((S, wd), lambda b, h: (b, cf + h)),
                  pl.BlockSpec((S, wd), lambda b, h: (b, ci_ + h)),
                  pl.BlockSpec((2, wd), lambda b, h: (0, h))] + c_in_specs,
        out_specs=[pl.BlockSpec((S, wd), lambda b, h: (b, h)),
                   pl.BlockSpec((HGRN_PAIR * nc, REC_DIM, REC_DIM), lambda b, h: (b * hp + h, 0, 0))] + [ANY] * nco,
        out_shape=[jax.ShapeDtypeStruct((T, 512), F32),
                   jax.ShapeDtypeStruct((B * REC_HEADS * nc, REC_DIM, REC_DIM), F32)] + c_out_shapes,
        scratch_shapes=[pltpu.VMEM((fwd_unroll, CHUNK, wd), F32), pltpu.VMEM((fwd_unroll, CHUNK, wd), F32)] + c_sems,
        compiler_params=_params("arbitrary", "arbitrary"))(proj, proj, proj, lb_param, *c_arrays)


def _hgrn_bwd(proj, lb_param, states, do, B, S, comm=None):
    T = B * S
    nc = S // CHUNK
    bwd_unroll = 4 if nc % 4 == 0 else 2
    c_arrays, c_in_specs, c_out_shapes, c_sems = _comm_parts(comm)
    nci, nco = len(c_arrays), len(c_out_shapes)

    def body(*refs):
        q_ref, z_ref, v_ref, lb_ref, st_ref, do_ref = refs[:6]
        dq_ref, dz_ref, dv_ref, dlb_ref = refs[6 + nci:10 + nci]
        slots = refs[10 + nci + nco:14 + nci + nco]
        comm_first, comm_last = _comm_run(comm, (B, REC_HEADS // HGRN_PAIR), refs, 6, 4)
        comm_first()
        tril, triu, causal, below, inside, col = _hgrn_consts()
        col_s = col & (SUB - 1)
        last_row = lax.broadcasted_iota(jnp.int32, (CHUNK, 1), 0) == CHUNK - 1
        rc = lax.broadcasted_iota(jnp.int32, (CHUNK, SUB * REC_DIM), 0)
        lc = lax.broadcasted_iota(jnp.int32, (CHUNK, SUB * REC_DIM), 1)
        spread = ((rc & (SUB - 1)) == (lc // REC_DIM)).astype(BF16)
        rr = lax.broadcasted_iota(jnp.int32, (CHUNK, SUB * CHUNK), 0)
        cc = lax.broadcasted_iota(jnp.int32, (CHUNK, SUB * CHUNK), 1)
        gather = (((rr // SUB) == ((cc & (CHUNK - 1)) // SUB)) & ((rr & (SUB - 1)) == (cc // CHUNK))).astype(BF16)

        heads = range(HGRN_PAIR)
        cols = [slice(REC_DIM * e, REC_DIM * (e + 1)) for e in heads]
        lanes = [pl.ds(REC_DIM * e, REC_DIM) for e in heads]
        lane_cat = lambda vals: jnp.concatenate(vals, axis=1)
        row_cat = lambda vals: jnp.concatenate(vals, axis=0)

        def chunk(it, carry, slot):
            k_s, b_s, pc_hi, pc_lo = (r.at[slot] for r in slots)
            dhts, dlb = carry
            ci = nc - 1 - it
            r0 = pl.multiple_of(ci * CHUNK, CHUNK)
            rows = pl.ds(r0, CHUNK)
            lb = _sigmoid(lb_ref[0:1, :] - lb_ref[1:2, :])
            sg, f, g_all, k_all = _hgrn_gates(z_ref[rows, :], lb)
            b_all = _sel_left(tril, g_all)
            k_s[...] = k_all
            b_s[...] = b_all
            q_all = q_ref[rows, :]
            das, hd = [], []
            for e in heads:
                vb, dob = v_ref[rows, lanes[e]].astype(BF16), do_ref[rows, lanes[e]].astype(BF16)
                da = jnp.where(causal, _dot_nt(dob, vb), 0.0)
                das.append(jnp.where(inside, da, 0.0))
                hd.append((vb, dob, da))
            da_hi, da_lo = _split2(row_cat(das))
            da_in = _dot(da_hi, spread) + _dot(da_lo, spread)
            ds, dqs = [], []
            for e in heads:
                q, bcum = q_all[:, cols[e]], b_all[:, cols[e]]
                d = jnp.zeros((CHUNK, CHUNK), F32)
                dq = jnp.zeros((CHUNK, REC_DIM), F32)
                for s in range(SUB):
                    w = jnp.exp(jnp.minimum(bcum - _block_rows(b_s, lanes[e], s), 0.0))
                    ks = _block_rows(k_s, lanes[e], s)
                    qw = q * w
                    d = jnp.where(col_s == s, jnp.sum(qw * ks, axis=-1, keepdims=True), d)
                    da_s = da_in[CHUNK * e:CHUNK * (e + 1), REC_DIM * s:REC_DIM * (s + 1)]
                    dq = dq + da_s * ks * w
                    hi, lo = _split2(da_s * qw)
                    pc_hi[pl.ds(CHUNK * s, CHUNK), lanes[e]] = hi
                    pc_lo[pl.ds(CHUNK * s, CHUNK), lanes[e]] = lo
                ds.append(d)
                dqs.append(dq)
            dk_in = _dot(gather, pc_hi[...]) + _dot(gather, pc_lo[...])
            dq_out, dk_out, dv_out, db_out, new_dhts = [], [], [], [], []
            for e in heads:
                q, k, bcum = q_all[:, cols[e]], k_all[:, cols[e]], b_all[:, cols[e]]
                vb, dob, da = hd[e]
                dht, ht = dhts[e], st_ref[e * nc + ci]
                qst, kst, eq, eks = _hgrn_offdiag(q, k, bcum, b_s, lanes[e])
                qst_b, kst_b = qst.astype(BF16), kst.astype(BF16)
                a = jnp.where(below, _dot_nt(qst_b, kst_b), 0.0) + jnp.where(inside, ds[e], 0.0)
                da_off = jnp.where(below, da, 0.0).astype(BF16)
                dqst = _dot(da_off, kst_b)
                dkst = _dot_tn(da_off, qst_b)
                dk = dk_in[:, cols[e]]
                dq_rows = [jnp.zeros((SUB, REC_DIM), F32)]
                for i in range(1, N_SUB):
                    dq_rows.append(dqst[SUB * i:SUB * (i + 1), REC_DIM * (i - 1):REC_DIM * i])
                    dk = dk + dkst[:, REC_DIM * (i - 1):REC_DIM * i] * eks[i - 1]
                dq = dqs[e] + row_cat(dq_rows) * eq
                eb = jnp.exp(bcum)
                b_last = b_s[pl.ds(CHUNK - 1, 1), lanes[e]]
                el = jnp.exp(b_last)
                ekb = jnp.exp(b_last - bcum)
                qb = (q * eb).astype(BF16)
                kb = k * ekb
                dhb = dht.astype(BF16)
                dv_out.append(_dot_tn(a.astype(BF16), dob) + _dot_nt(kb.astype(BF16), dhb))
                dqb = _dot(dob, ht.astype(BF16))
                dkb = _dot(vb, dhb)
                new_dhts.append(dht * el + _dot_tn(dob, qb))
                dq = dq + eb * dqb
                dk = dk + ekb * dkb
                edge = jnp.sum(kb * dkb, axis=0, keepdims=True) + el * jnp.sum(ht * dht, axis=0, keepdims=True)
                db_out.append(q * dq - k * dk + jnp.where(last_row, edge, 0.0))
                dq_out.append(dq)
                dk_out.append(dk)
            dk_all = lane_cat(dk_out)
            db_hi, db_lo = _split2(lane_cat(db_out))
            dg = _dot(triu, db_hi) + _dot(triu, db_lo)
            df = dg / f - dk_all
            dz_ref[rows, :] = (df * (1.0 - lb) * sg * (1.0 - sg)).astype(dz_ref.dtype)
            dq_ref[rows, :] = lane_cat(dq_out).astype(dq_ref.dtype)
            dv_ref[rows, :] = lane_cat(dv_out).astype(dv_ref.dtype)
            return tuple(new_dhts), dlb + jnp.sum(df * (1.0 - sg), axis=0, keepdims=True)

        zero = (tuple(jnp.zeros((REC_DIM, REC_DIM), F32) for _ in heads), jnp.zeros((1, HGRN_PAIR * REC_DIM), F32))
        def trip(i, carry):
            for u in range(bwd_unroll):
                carry = chunk(bwd_unroll * i + u, carry, u)
            return carry

        _, dlb = lax.fori_loop(0, nc // bwd_unroll, trip, zero)
        lb = _sigmoid(lb_ref[0:1, :] - lb_ref[1:2, :])
        dlb_ref[...] = jnp.broadcast_to(dlb * lb * (1.0 - lb), (8, HGRN_PAIR * REC_DIM))
        comm_last()

    hp, wd = REC_HEADS // HGRN_PAIR, HGRN_PAIR * REC_DIM
    cq, cf, ci_ = (c * REC_DIM // wd for c in (COL_RQ, COL_RF, COL_RI))
    return pl.pallas_call(
        body, name="hgrn_bwd", grid=(B, hp),
        in_specs=[pl.BlockSpec((S, wd), lambda b, h: (b, cq + h)),
                  pl.BlockSpec((S, wd), lambda b, h: (b, cf + h)),
                  pl.BlockSpec((S, wd), lambda b, h: (b, ci_ + h)),
                  pl.BlockSpec((2, wd), lambda b, h: (0, h)),
                  pl.BlockSpec((HGRN_PAIR * nc, REC_DIM, REC_DIM), lambda b, h: (b * hp + h, 0, 0)),
                  pl.BlockSpec((S, wd), lambda b, h: (b, h))] + c_in_specs,
        out_specs=[pl.BlockSpec((S, wd), lambda b, h: (b, h))] * 3
        + [pl.BlockSpec((8, wd), lambda b, h: (b, h))] + [ANY] * nco,
        out_shape=[jax.ShapeDtypeStruct((T, 512), BF16)] * 3 + [jax.ShapeDtypeStruct((B * 8, 512), F32)]
        + c_out_shapes,
        scratch_shapes=[pltpu.VMEM((bwd_unroll, CHUNK, wd), F32)] * 2
        + [pltpu.VMEM((bwd_unroll, SUB * CHUNK, wd), BF16)] * 2 + c_sems,
        compiler_params=_params("arbitrary", "arbitrary"))(proj, proj, proj, lb_param, states, do, *c_arrays)


def _rec_gate_fwd(rec, proj, rec_norm):
    T = rec.shape[0]

    def fn(accs, tv, cv):
        return [_rms_hat(tv[0]) * cv[0] * _sigmoid(tv[1])]

    return _tile_call("rec_gate", fn, T, 512, _pick(T, 1024), REC_DIM, tiles=[(rec, 0), (proj, COL_RG)],
                      consts=[rec_norm], outs=[BF16])[0]


def _rec_gate_bwd(dyb, w_rec_proj, rec, proj, rec_norm):
    T = rec.shape[0]

    def fn(accs, tv, cv):
        d, r, rg = accs[0], tv[0], tv[1]
        sg = _sigmoid(rg)
        rn = _rms_hat(r) * cv[0]
        dh, dg = _rms_bwd_vals(d * sg, r, cv[0])
        return [dh, d * rn * sg * (1.0 - sg), dg]

    return _tile_call("rec_gate_bwd", fn, T, 512, _pick(T, 1024), REC_DIM, pairs=[(dyb, 0, w_rec_proj, "nt")],
                      tiles=[(rec, 0), (proj, COL_RG)], consts=[rec_norm], outs=[F32, BF16], parts=1)


def _mix_out_fwd(att, recn, proj, w_att_proj, w_rec_proj, w_out, h1, g_next):
    T = att.shape[0]
    tn = 256

    def merge(accs, tv, cv):
        ya, yb = accs
        return [ya, yb, _sigmoid(tv[0]) * ya + _sigmoid(tv[1]) * yb]

    ya, yb, merged = _tile_call(
        "merge", merge, T, D_MODEL, _pick(T, 1024), tn,
        pairs=[(att, 0, w_att_proj, "nn"), (recn, 0, w_rec_proj, "nn")],
        tiles=[(proj, COL_GA * 128 // tn), (proj, COL_GB * 128 // tn)], outs=[BF16] * 3)

    def res(accs, tv, cv):
        h2 = tv[0] + accs[0]
        return [h2, _rms_hat(h2) * cv[0]]

    h2, n2 = _tile_call("mix_out", res, T, D_MODEL, _pick(T, 512), D_MODEL, pairs=[(merged, 0, w_out, "nn")],
                        tiles=[(h1, 0)], consts=[g_next], outs=[F32, BF16])
    return h2, n2, (ya, yb, merged)


GATHER_FIRST = ("w_ffn1_in",)
GATHER_MIX = ("w_ffn1_out", "w_in")
GATHER_PROJ = ("w_att_proj", "w_rec_proj", "w_out")
GATHER_LAST = ("w_ffn2_in", "w_ffn2_out", "w_ple_gate", "w_ple_proj")
SCATTER_LATE = ("w_ple_gate", "w_ple_proj", "w_ffn2_in", "w_ffn2_out")
SCATTER_MIX = ("w_out", "w_att_proj", "w_rec_proj", "w_in")
SCATTER_LAST = ("w_ffn1_in", "w_ffn1_out")


def _local_step(x, p, tgt, w, mine16, cc, me_chip, B, S):
    T = B * S
    w = dict(w)
    g_ffn1, g_mix, g_ffn2, g_ple = w["norm_ffn1"], w["norm_mix"], w["norm_ffn2"], w["norm_ple"]
    g_fin = w["norm_final"].reshape(1, D_MODEL)
    grads, part, from_chips = {}, {}, {}

    def gather(names):
        return _gather_comm([mine16[n] for n in names])

    def place(names, got):
        for n, g in zip(names, got):
            w[n] = g if n in ("w_ffn1_in", "w_ffn2_in") else _natural(n, g)

    def swap(names):
        return _swap_comm([grads[n][1] for n in names])

    def after_swap(names, from_sib):
        for n, fs in zip(names, from_sib):
            part[n] = _add_sibling("rs_add_sib_" + n, grads[n][0], fs, cc)
        return _scatter_comm([part[n][1] for n in names])

    def scattered(names, got):
        for n, g in zip(names, got):
            from_chips[n] = g

    def ffn1_in_weight(got):
        place(GATHER_FIRST, got)
        return w["w_ffn1_in"]

    def ffn1_out_weight(got):
        place(GATHER_MIX, got)
        return w["w_ffn1_out"]

    h1, u, sv1, got_proj = _ffn_fwd("ffn1", x, g_ffn1, None, None, g_mix, comm_norm=gather(GATHER_FIRST),
                                    w_in_of=ffn1_in_weight, comm_in=gather(GATHER_MIX),
                                    comm_out=gather(GATHER_PROJ), w_out_of=ffn1_out_weight)
    place(GATHER_PROJ, got_proj)

    def ident(accs, tv, cv):
        return [accs[0]]

    proj = _tile_call("in_proj", ident, T, IN_W, _pick(T, 512), IN_W // 2, pairs=[(u, 0, w["w_in"], "nn")],
                      outs=[F32], j_outer=True)[0]
    onehot = jnp.asarray(_t5_onehot())
    bias = _small_mm("t5_bias", w["rel_bias"].T, onehot.astype(BF16), "right")
    bias = bias.reshape(N_Q_HEADS, ATT_BLOCK, 2 * ATT_BLOCK)
    sinks = w["attn_sinks"].reshape(N_Q_HEADS)
    kk2, vv2 = _kv_layouts(proj)
    att = _swa_fwd(proj, kk2, vv2, bias, sinks, B, S)
    rec, states, *got_last = _hgrn_fwd(proj, w["lb_param"], B, S, comm=gather(GATHER_LAST))
    place(GATHER_LAST, got_last)
    recn = _rec_gate_fwd(rec, proj, w["rec_norm"])
    h2, n2, (ya, yb, merged) = _mix_out_fwd(att, recn, proj, w["w_att_proj"], w["w_rec_proj"], w["w_out"], h1,
                                            g_ffn2)
    h3, n3, sv2, _ = _ffn_fwd("ffn2", h2, g_ffn2, w["w_ffn2_in"], w["w_ffn2_out"], g_ple, n=n2)

    def ple(accs, tv, cv):
        gate = _sigmoid(accs[0])
        return [gate, accs[1], tv[0] + gate * accs[1]]

    gate_p, pp, h4 = _tile_call(
        "ple", ple, T, D_MODEL, _pick(T, 512), D_MODEL,
        pairs=[(n3, 0, w["w_ple_gate"], "nn"), (p, 0, w["w_ple_proj"], "nn")], tiles=[(h3, 0)],
        outs=[BF16, BF16, F32])

    def head(accs, tv, cv):
        h, t, gt, ppv = tv[0], tv[1], tv[2].astype(F32), tv[3].astype(F32)
        err = _rms_hat(h) * cv[0] - t
        dh, dg = _rms_bwd_vals(err * (1.0 / D_MODEL), h, cv[0])
        return [dh, dh * ppv * gt * (1.0 - gt), dh * gt, _group8(err * err), dg]

    dh4, dzg, dpp, loss_p, dg_fin = _tile_call(
        "loss_head", head, T, D_MODEL, _pick(T, 256), D_MODEL,
        tiles=[(h4, 0), (tgt, 0), (gate_p, 0), (pp, 0)], consts=[g_fin], outs=[F32, BF16, BF16], parts=2)
    grads["norm_final"] = dg_fin

    grads["w_ple_gate"] = _mm_tn_rows("ple_dwg", n3, dzg)
    grads["w_ple_proj"] = _mm_tn_cols("ple_dwp", p, dpp)

    def dnorm(accs, tv, cv):
        dh, dg = _rms_bwd_vals(accs[0], tv[0], cv[0])
        dh = tv[1] + dh
        return [dh, 0.5 * dh, dg]

    dh3, df3, grads["norm_ple"] = _tile_call(
        "ple_dnorm", dnorm, T, D_MODEL, _pick(T, 512), D_MODEL, pairs=[(dzg, 0, w["w_ple_gate"], "nt")],
        tiles=[(h3, 0), (dh4, 0)], consts=[g_ple], outs=[F32, BF16], parts=1)

    def swap_late(dw_in, dw_out):
        grads["w_ffn2_in"], grads["w_ffn2_out"] = dw_in, dw_out
        return swap(SCATTER_LATE)

    dh2, dh2b, grads["norm_ffn2"], _, _, _, from_sib = _ffn_bwd(
        "ffn2b", dh3, df3, h2, g_ffn2, w["w_ffn2_in"], w["w_ffn2_out"], sv2, comm_last=swap_late)
    scatter_late = after_swap(SCATTER_LATE, from_sib)

    grads["w_out"] = _mm_tn_rows("mix_dwout", merged, dh2b)
    tn = 256

    def dmerge(accs, tv, cv):
        dm = accs[0]
        sa, sb = _sigmoid(tv[0]), _sigmoid(tv[1])
        yav, ybv = tv[2].astype(F32), tv[3].astype(F32)
        return [dm * sa, dm * sb, dm * yav * sa * (1.0 - sa), dm * ybv * sb * (1.0 - sb)]

    dya, dyb, dga, dgb = _tile_call(
        "mix_dmerge", dmerge, T, D_MODEL, _pick(T, 1024), tn, pairs=[(dh2b, 0, w["w_out"], "nt")],
        tiles=[(proj, COL_GA * 128 // tn), (proj, COL_GB * 128 // tn), (ya, 0), (yb, 0)], outs=[BF16] * 4)
    grads["w_att_proj"] = _mm_tn_cols("mix_dwatt", att, dya)
    grads["w_rec_proj"] = _mm_tn_cols("mix_dwrec", recn, dyb)

    datt = _tile_call("mix_datt", ident, T, 512, _pick(T, 1024), 512, pairs=[(dya, 0, w["w_att_proj"], "nt")],
                      outs=[BF16])[0]
    drec, drg, grads["rec_norm"] = _rec_gate_bwd(dyb, w["w_rec_proj"], rec, proj, w["rec_norm"])

    drq, drf, dri, dlb, *got = _hgrn_bwd(proj, w["lb_param"], states, drec, B, S, comm=scatter_late)
    scattered(SCATTER_LATE, got)
    grads["lb_param"] = dlb
    daq, dak, dav, dbias, dsink = _swa_bwd(proj, kk2, vv2, bias, sinks, datt, B, S)
    grads["attn_sinks"] = dsink
    grads["rel_bias"] = _small_mm("t5_dbias", dbias.reshape(N_Q_HEADS, -1), onehot.T.astype(BF16), "right")
    dproj = jnp.concatenate([daq, dak, dav, drq, drf, dri, drg, dga, dgb], axis=1)
    tk = _pick(T, 2048, 128)
    w_in_shard = IN_W // N_CHIPS
    half_d = D_MODEL // 2
    gw32, gw16 = _mm_tn("mix_dwin", (2, 2, T // tk),
                        (u, (tk, half_d), lambda i, j, k: (k, i)), (dproj, (tk, IN_W // 2), lambda i, j, k: (k, j)),
                        _grad_pair((D_MODEL, IN_W), (half_d, IN_W // 2), lambda i, j, k: (i, j)))
    to_sh = lambda t: t.reshape(D_MODEL, N_CHIPS, w_in_shard).transpose(1, 0, 2)
    grads["w_in"] = (to_sh(gw32), to_sh(gw16))

    def dnorm_mix(accs, tv, cv):
        dh, dg = _rms_bwd_vals(accs[0], tv[0], cv[0])
        dh = tv[1] + dh
        return [dh, 0.5 * dh, dg]

    dh1, df1, grads["norm_mix"], *from_sib = _tile_call(
        "mix_dnorm", dnorm_mix, T, D_MODEL, _pick(T, 512), D_MODEL, pairs=[(dproj, 0, w["w_in"], "nt")],
        tiles=[(h1, 0), (dh2, 0)], consts=[g_mix], outs=[F32, BF16], parts=1, comm=swap(SCATTER_MIX))
    scatter_mix = after_swap(SCATTER_MIX, from_sib)

    def scatter_last(dw_in, dw_out):
        grads["w_ffn1_in"], grads["w_ffn1_out"] = dw_in, dw_out
        return after_swap(SCATTER_LAST, _run_comm("rs_sibling_last", swap(SCATTER_LAST)))

    dx, _, grads["norm_ffn1"], _, _, got, got_last = _ffn_bwd(
        "ffn1b", dh1, df1, x, g_ffn1, w["w_ffn1_in"], w["w_ffn1_out"], sv1, comm=scatter_mix, comm_last=scatter_last)
    scattered(SCATTER_MIX, got)
    scattered(SCATTER_LAST, got_last)
    return loss_p, dx, grads, part, from_chips


def _place():
    x, y, c = lax.axis_index("x"), lax.axis_index("y"), lax.axis_index("c")
    return x, y, c


def _other_chips(x, y):
    return [(1 - x, y, 2 * (1 - x) + y), (x, 1 - y, 2 * x + 1 - y), (1 - x, 1 - y, 2 * (1 - x) + 1 - y)]


def _half_rows(ref_3d, chip, h, rows):
    return ref_3d.at[chip, pl.ds(h * rows, rows), :]


def _gather_comm(ws):
    nw = len(ws)

    def parts(w_refs, out_refs, send_sems, recv_sems):
        x, y, c = _place()
        me = 2 * x + y
        chips = _other_chips(x, y)

        def local():
            out = []
            for i in range(nw):
                rows = ws[i].shape[0] // LOCAL_PARTS
                for q in range(LOCAL_PARTS):
                    sl = pl.ds(q * rows, rows)
                    out.append(pltpu.make_async_copy(w_refs[i].at[sl, :], out_refs[i].at[me, sl, :],
                                                     send_sems.at[6 * nw + LOCAL_PARTS * i + q]))
            return out

        def copy(i, k, chip, h, to, src=None):
            half = ws[i].shape[0] // 2
            dst = _half_rows(out_refs[i], chip, h, half)
            return pltpu.make_async_remote_copy(
                src_ref=dst if src is None else src, dst_ref=dst,
                send_sem=send_sems.at[6 * i + k], recv_sem=recv_sems.at[6 * i + k], device_id=to, device_id_type=MESH)

        def first():
            out = []
            for i in range(nw):
                half = ws[i].shape[0] // 2
                out += [copy(i, j, me, c, (cx, cy, c), src=w_refs[i].at[pl.ds(c * half, half), :])
                        for j, (cx, cy, _) in enumerate(chips)]
            return out

        return copy, first, chips, c, (x, y, 1 - c), local

    def start(*refs):
        _, first, _, _, _, local = parts(*refs)
        for cp in first() + local():
            cp.start()

    def finish(*refs):
        copy, first, chips, c, sibling, local = parts(*refs)
        passed = []
        for i in range(nw):
            for j, (cx, cy, ci) in enumerate(chips):
                copy(i, j, ci, c, (cx, cy, c)).wait_recv()
                fw = copy(i, 3 + j, ci, c, sibling)
                fw.start()
                passed.append(fw)
        for i in range(nw):
            for j, (_, _, ci) in enumerate(chips):
                copy(i, 3 + j, ci, 1 - c, sibling).wait_recv()
        for cp in first() + passed:
            cp.wait_send()
        for cp in local():
            cp.wait()

    return _Comm(list(ws), [jax.ShapeDtypeStruct((N_CHIPS,) + w.shape, w.dtype) for w in ws],
                 (6 + LOCAL_PARTS) * nw, start, finish)


def _scatter_comm(ps):
    nw = len(ps)

    def copies(p_refs, out_refs, send_sems, recv_sems):
        x, y, c = _place()
        cps = []
        for i in range(nw):
            for j, (cx, cy, ci) in enumerate(_other_chips(x, y)):
                cps.append(pltpu.make_async_remote_copy(
                    src_ref=p_refs[i].at[ci], dst_ref=out_refs[i].at[j], send_sem=send_sems.at[3 * i + j],
                    recv_sem=recv_sems.at[3 * i + j], device_id=(cx, cy, c), device_id_type=MESH))
        return cps

    def start(*refs):
        for cp in copies(*refs):
            cp.start()

    def finish(*refs):
        for cp in copies(*refs):
            cp.wait()

    return _Comm(list(ps), [jax.ShapeDtypeStruct((3,) + p.shape[1:], p.dtype) for p in ps], 3 * nw, start, finish)


def _swap_comm(gs):
    nw = len(gs)

    def copies(g_refs, out_refs, send_sems, recv_sems):
        x, y, c = _place()
        cps = []
        for i in range(nw):
            half = gs[i].shape[1] // 2
            cps.append(pltpu.make_async_remote_copy(
                src_ref=g_refs[i].at[:, pl.ds((1 - c) * half, half), :], dst_ref=out_refs[i],
                send_sem=send_sems.at[i], recv_sem=recv_sems.at[i], device_id=(x, y, 1 - c), device_id_type=MESH))
        return cps

    def start(*refs):
        for cp in copies(*refs):
            cp.start()

    def finish(*refs):
        for cp in copies(*refs):
            cp.wait()

    return _Comm(list(gs), [jax.ShapeDtypeStruct((N_CHIPS, g.shape[1] // 2, g.shape[2]), g.dtype) for g in gs],
                 nw, start, finish)


def _run_comm(name, comm):
    nci, nco = len(comm.ins), len(comm.out_shapes)

    def body(*refs):
        cin, cout, send_sems, recv_sems = refs[:nci], refs[nci:nci + nco], refs[-2], refs[-1]
        comm.start(cin, cout, send_sems, recv_sems)
        comm.finish(cin, cout, send_sems, recv_sems)

    return pl.pallas_call(
        body, name=name, in_specs=[ANY] * nci, out_specs=[ANY] * nco, out_shape=list(comm.out_shapes),
        scratch_shapes=[pltpu.SemaphoreType.DMA((comm.n_sems,)), pltpu.SemaphoreType.DMA((comm.n_sems,))],
    )(*comm.ins)


def _join_halves(name, ss):
    nw = len(ss)

    def body(*refs):
        s_refs, out_refs, send_sems, recv_sems = refs[:nw], refs[nw:2 * nw], refs[2 * nw], refs[2 * nw + 1]
        x, y, c = _place()
        cps = [pltpu.make_async_remote_copy(
            src_ref=s_refs[i], dst_ref=out_refs[i], send_sem=send_sems.at[i], recv_sem=recv_sems.at[i],
            device_id=(x, y, 1 - c), device_id_type=MESH) for i in range(nw)]
        for cp in cps:
            cp.start()
        for cp in cps:
            cp.wait()

    return pl.pallas_call(
        body, name=name, in_specs=[ANY] * nw, out_specs=[ANY] * nw,
        out_shape=[jax.ShapeDtypeStruct(s.shape, s.dtype) for s in ss],
        scratch_shapes=[pltpu.SemaphoreType.DMA((nw,)), pltpu.SemaphoreType.DMA((nw,))],
    )(*ss)


def _allreduce_small(sp):
    def body(s_ref, out_ref, slots, send_sems, recv_sems):
        x, y, c = _place()
        me = 4 * x + 2 * y + c
        slots[me] = s_ref[...]
        cps = []
        for r in range(1, N_DEV):
            px, py, pc = x ^ (r >> 2), y ^ ((r >> 1) & 1), c ^ (r & 1)
            cps.append(pltpu.make_async_remote_copy(
                src_ref=s_ref, dst_ref=slots.at[me], send_sem=send_sems.at[r - 1], recv_sem=recv_sems.at[r - 1],
                device_id=(px, py, pc), device_id_type=MESH))
        for cp in cps:
            cp.start()
        for r in range(1, N_DEV):
            px, py, pc = x ^ (r >> 2), y ^ ((r >> 1) & 1), c ^ (r & 1)
            pltpu.make_async_remote_copy(
                src_ref=s_ref, dst_ref=slots.at[4 * px + 2 * py + pc], send_sem=send_sems.at[r - 1],
                recv_sem=recv_sems.at[r - 1], device_id=(px, py, pc), device_id_type=MESH).wait_recv()
        for cp in cps:
            cp.wait_send()
        acc = slots[0]
        for d in range(1, N_DEV):
            acc = acc + slots[d]
        out_ref[...] = acc

    return pl.pallas_call(
        body, name="allreduce_small",
        in_specs=[pl.BlockSpec(memory_space=pltpu.VMEM)], out_specs=pl.BlockSpec(memory_space=pltpu.VMEM),
        out_shape=jax.ShapeDtypeStruct(sp.shape, F32),
        scratch_shapes=[pltpu.VMEM((N_DEV,) + sp.shape, F32), pltpu.SemaphoreType.DMA((N_DEV - 1,)),
                        pltpu.SemaphoreType.DMA((N_DEV - 1,))],
    )(sp)


def _scalar(v):
    return jnp.reshape(v, (1,)).astype(jnp.int32)


def _row_tile(h, dtype_mult=16):
    return _pick(h, 256, dtype_mult)


def _add_sibling(name, g32, from_sib, c):
    _, r, n = g32.shape
    h = r // 2
    th = _row_tile(h)
    nt = h // th

    def body(c_ref, g_ref, s_ref, o32_ref, o16_ref):
        s = g_ref[...] + s_ref[...].astype(F32)
        o32_ref[...] = s
        o16_ref[...] = s.astype(BF16)

    blk = (None, th, n)
    return pl.pallas_call(
        body, name=name,
        grid_spec=pltpu.PrefetchScalarGridSpec(
            num_scalar_prefetch=1, grid=(N_CHIPS, nt),
            in_specs=[pl.BlockSpec(blk, lambda k, t, c_ref: (k, c_ref[0] * nt + t, 0)),
                      pl.BlockSpec(blk, lambda k, t, c_ref: (k, t, 0))],
            out_specs=[pl.BlockSpec(blk, lambda k, t, c_ref: (k, t, 0))] * 2),
        out_shape=[jax.ShapeDtypeStruct((N_CHIPS, h, n), F32), jax.ShapeDtypeStruct((N_CHIPS, h, n), BF16)],
        compiler_params=_params("arbitrary", "arbitrary"))(_scalar(c), g32, from_sib)


def _add_chips(name, p32, from_chips, me_chip):
    _, h, n = p32.shape
    th = _row_tile(h)

    def body(m_ref, p_ref, a_ref, b_ref, c_ref, o_ref):
        o_ref[...] = p_ref[...] + a_ref[...].astype(F32) + b_ref[...].astype(F32) + c_ref[...].astype(F32)

    blk = (None, th, n)
    return pl.pallas_call(
        body, name=name,
        grid_spec=pltpu.PrefetchScalarGridSpec(
            num_scalar_prefetch=1, grid=(h // th,),
            in_specs=[pl.BlockSpec(blk, lambda t, m_ref: (m_ref[0], t, 0))]
            + [pl.BlockSpec(blk, lambda t, m_ref, j=j: (j, t, 0)) for j in range(3)],
            out_specs=pl.BlockSpec((th, n), lambda t, m_ref: (t, 0))),
        out_shape=jax.ShapeDtypeStruct((h, n), F32),
        compiler_params=_params("arbitrary"))(_scalar(me_chip), p32, from_chips, from_chips, from_chips)


def _adamw_vals(w, g, m, v):
    m = ADAM_B1 * m + (1.0 - ADAM_B1) * g
    v = ADAM_B2 * v + (1.0 - ADAM_B2) * (g * g)
    m_hat = m / (1.0 - ADAM_B1 ** ADAM_STEP)
    v_hat = v / (1.0 - ADAM_B2 ** ADAM_STEP)
    delta = -ADAM_LR * (m_hat / (jnp.sqrt(v_hat) + ADAM_EPS) + ADAM_WD * w)
    return delta, m, v


def _adamw_halves(name, w, m, v, g_mine, g_sib, c):
    r, n = w.shape
    h = r // 2
    th = _row_tile(h, 8)
    nt = h // th

    def body(c_ref, w_ref, m_ref, v_ref, a_ref, b_ref, g_ref, d_ref, nm_ref, nv_ref):
        mine = (pl.program_id(0) // nt) == c_ref[0]
        g = jnp.where(mine, a_ref[...], b_ref[...])
        d, nm, nv = _adamw_vals(w_ref[...], g, m_ref[...], v_ref[...])
        g_ref[...] = g
        d_ref[...] = d
        nm_ref[...] = nm
        nv_ref[...] = nv

    full = pl.BlockSpec((th, n), lambda t, c_ref: (t, 0))
    part = pl.BlockSpec((th, n), lambda t, c_ref: (t % nt, 0))
    return pl.pallas_call(
        body, name=name,
        grid_spec=pltpu.PrefetchScalarGridSpec(
            num_scalar_prefetch=1, grid=(2 * nt,), in_specs=[full, full, full, part, part], out_specs=[full] * 4),
        out_shape=[jax.ShapeDtypeStruct((r, n), F32)] * 4,
        compiler_params=_params("arbitrary"))(_scalar(c), w, m, v, g_mine, g_sib)


def _adamw(name, w, g, m, v):
    R, W = w.shape

    def fn(accs, tv, cv):
        return list(_adamw_vals(*tv))

    return _tile_call(name, fn, R, W, _pick(R, 256), W, tiles=[(w, 0), (g, 0), (m, 0), (v, 0)], outs=[F32] * 3)


SMALL_LAYOUT = (("rel_bias", 2, 256), ("lb_param", 8, 1024), ("norm_ffn1", 8, 1024), ("norm_mix", 8, 1024),
                ("attn_sinks", 1, 8), ("rec_norm", 1, 128), ("norm_ffn2", 8, 1024), ("norm_ple", 8, 1024),
                ("norm_final", 8, 1024), ("loss", 8, 1024))


def _pack_small(vals):
    rows = []
    for name, nrows, n in SMALL_LAYOUT:
        flat = vals[name].reshape(-1)
        flat = jnp.pad(flat, (0, nrows * 128 - n))
        rows.append(flat.reshape(nrows, 128))
    packed = jnp.concatenate(rows, axis=0)
    return jnp.pad(packed, ((0, SMALL_ROWS - packed.shape[0]), (0, 0)))


def _unpack_small(packed, shapes):
    out, r = {}, 0
    for name, nrows, n in SMALL_LAYOUT:
        out[name] = packed[r:r + nrows].reshape(-1)[:n].reshape(shapes[name])
        r += nrows
    return out


def _natural(name, s):
    if name in COL_SHARDED:
        return s.transpose(1, 0, 2).reshape(s.shape[1], -1)
    return s.reshape(-1, s.shape[2])


def kernel(x, p, rel_bias, lb_param, norm_ffn1, w_ffn1_in, w_ffn1_out, norm_mix, w_in, attn_sinks, rec_norm, w_att_proj, w_rec_proj, w_out, norm_ffn2, w_ffn2_in, w_ffn2_out, norm_ple, w_ple_gate, w_ple_proj, norm_final, loss_target, m_rel_bias, m_lb_param, m_norm_ffn1, m_w_ffn1_in, m_w_ffn1_out, m_norm_mix, m_w_in, m_attn_sinks, m_rec_norm, m_w_att_proj, m_w_rec_proj, m_w_out, m_norm_ffn2, m_w_ffn2_in, m_w_ffn2_out, m_norm_ple, m_w_ple_gate, m_w_ple_proj, m_norm_final, v_rel_bias, v_lb_param, v_norm_ffn1, v_w_ffn1_in, v_w_ffn1_out, v_norm_mix, v_w_in, v_attn_sinks, v_rec_norm, v_w_att_proj, v_w_rec_proj, v_w_out, v_norm_ffn2, v_w_ffn2_in, v_w_ffn2_out, v_norm_ple, v_w_ple_gate, v_w_ple_proj, v_norm_final):
    args = dict(locals())
    wsh = {n: args[n] for n in WEIGHTS}
    B, S = x.shape[0], x.shape[1]
    T = B * S
    cx, cy, cc = _place()
    me_chip = 2 * cx + cy

    mine16 = {n: wsh[n][0].astype(BF16) for n in BIG}
    loss_p, dx, grads, part, from_chips = _local_step(
        x.reshape(T, D_MODEL), p.reshape(T, PLE_DIM), loss_target.reshape(T, D_MODEL),
        {n: wsh[n] for n in SMALL}, mine16, cc, me_chip, B, S)

    s_mine = [_add_chips("rs_add_chips_" + n, part[n][0], from_chips[n], me_chip) for n in BIG]
    s_sib = _join_halves("rs_join", s_mine)

    small_vals = {
        "rel_bias": grads["rel_bias"].T,
        "lb_param": jnp.concatenate([_colsum("dlb_sum", grads["lb_param"]),
                                     -_colsum("dlb_sum2", grads["lb_param"])], axis=0) / 8.0,
        "attn_sinks": grads["attn_sinks"][:, 0],
        "rec_norm": _colsum("drn_sum", grads["rec_norm"]).reshape(REC_HEADS, REC_DIM).sum(axis=0),
        "loss": _colsum("loss_sum", loss_p),
    }
    for n in ("norm_ffn1", "norm_mix", "norm_ffn2", "norm_ple", "norm_final"):
        small_vals[n] = _colsum(n + "_sum", grads[n])
    red = _allreduce_small(_pack_small(small_vals))
    small_shapes = {n: wsh[n].shape for n in SMALL}
    small_shapes["loss"] = (D_MODEL,)
    small = _unpack_small(red, small_shapes)
    loss = 0.5 * jnp.sum(small["loss"]) / D_MODEL

    out_g, out_d, out_m, out_v = {}, {}, {}, {}
    for n, gm, gs in zip(BIG, s_mine, s_sib):
        res = _adamw_halves("adamw_" + n, wsh[n][0], args["m_" + n][0], args["v_" + n][0], gm, gs, cc)
        out_g[n], out_d[n], out_m[n], out_v[n] = (t[None] for t in res)
    sw = _pack_small({**{n: wsh[n] for n in SMALL}, "loss": jnp.zeros((D_MODEL,), F32)})
    sm = _pack_small({**{n: args["m_" + n] for n in SMALL}, "loss": jnp.zeros((D_MODEL,), F32)})
    sv = _pack_small({**{n: args["v_" + n] for n in SMALL}, "loss": jnp.ones((D_MODEL,), F32)})
    sd, snm, snv = _adamw("adamw_small", sw, red, sm, sv)
    ud, um, uv = (_unpack_small(t, small_shapes) for t in (sd, snm, snv))
    for n in SMALL:
        out_g[n], out_d[n], out_m[n], out_v[n] = small[n], ud[n], um[n], uv[n]

    return (loss, dx.reshape(B, S, D_MODEL), *[out_g[n] for n in WEIGHTS], *[out_d[n] for n in WEIGHTS],
            *[out_m[n] for n in WEIGHTS], *[out_v[n] for n in WEIGHTS])
```

```python
import numpy as np
import jax
import jax.numpy as jnp
from jax import lax
from jax.experimental import pallas as pl
from jax.experimental.pallas import tpu as pltpu

F32 = jnp.float32
BF16 = jnp.bfloat16
MESH = pl.DeviceIdType.MESH

D_MODEL = 1024
D_FF = 2816
FF_SHARD = 2 * D_FF // 4
HEAD_DIM = 64
N_Q_HEADS = 8
ATT_BLOCK = 128
N_BUCKETS = 32
MAX_DISTANCE = 128
REC_HEADS = 4
REC_DIM = 128
PLE_DIM = 256
EPS = 1e-6
IN_W = 4864
COL_AQ, COL_AK, COL_AV, COL_RQ, COL_RF, COL_RI, COL_RG, COL_GA, COL_GB = 0, 4, 5, 6, 10, 14, 18, 22, 30

CHUNK = 64
SUB = 8
N_SUB = CHUNK // SUB
HGRN_PAIR = 2

ADAM_LR, ADAM_B1, ADAM_B2, ADAM_EPS, ADAM_WD, ADAM_STEP = 0.001, 0.9, 0.999, 1e-08, 0.01, 10

V7X_VMEM_LIMIT = 56 * 1024 * 1024
N_CHIPS = 4
N_DEV = 8

BIG = ("w_ffn1_in", "w_ffn1_out", "w_in", "w_att_proj", "w_rec_proj", "w_out",
       "w_ffn2_in", "w_ffn2_out", "w_ple_gate", "w_ple_proj")
COL_SHARDED = ("w_ffn1_in", "w_in", "w_att_proj", "w_rec_proj", "w_ffn2_in", "w_ple_proj")
WEIGHTS = ("rel_bias", "lb_param", "norm_ffn1", "w_ffn1_in", "w_ffn1_out", "norm_mix", "w_in", "attn_sinks",
           "rec_norm", "w_att_proj", "w_rec_proj", "w_out", "norm_ffn2", "w_ffn2_in", "w_ffn2_out", "norm_ple",
           "w_ple_gate", "w_ple_proj", "norm_final")
SMALL = tuple(n for n in WEIGHTS if n not in BIG)
SMALL_ROWS = 64


def _params(*sem):
    return pltpu.CompilerParams(dimension_semantics=sem, vmem_limit_bytes=V7X_VMEM_LIMIT)


def _pick(n, cap, mult=8):
    if n <= cap:
        return n
    for t in range(cap - cap % mult, 0, -mult):
        if n % t == 0:
            return t
    raise ValueError((n, cap, mult))


def _dot(a, b):
    return jnp.dot(a, b, preferred_element_type=F32)


def _dot_nt(a, b):
    return lax.dot_general(a, b, (((1,), (1,)), ((), ())), preferred_element_type=F32)


def _dot_tn(a, b):
    return lax.dot_general(a, b, (((0,), (0,)), ((), ())), preferred_element_type=F32)


def _split3(x):
    hi = x.astype(BF16)
    r = x - hi.astype(F32)
    mid = r.astype(BF16)
    lo = (r - mid.astype(F32)).astype(BF16)
    return hi, mid, lo


def _split2(x):
    hi = x.astype(BF16)
    return hi, (x - hi.astype(F32)).astype(BF16)


def _sel_left(sel_bf16, x):
    hi, mid, lo = _split3(x)
    return _dot(sel_bf16, hi) + _dot(sel_bf16, mid) + _dot(sel_bf16, lo)


def _sel_right(x, sel_bf16):
    hi, mid, lo = _split3(x)
    return _dot(hi, sel_bf16) + _dot(mid, sel_bf16) + _dot(lo, sel_bf16)


def _sigmoid(x):
    return 0.5 * jnp.tanh(0.5 * x) + 0.5


def _group8(x):
    r, w = x.shape
    return x.reshape(r // 8, 8, w).sum(axis=0)


class _Comm:
    def __init__(self, ins, out_shapes, n_sems, start, finish):
        self.ins, self.out_shapes, self.n_sems, self.start, self.finish = ins, out_shapes, n_sems, start, finish


ANY = pl.BlockSpec(memory_space=pl.ANY)


def _comm_parts(comm):
    if comm is None:
        return [], [], [], []
    sems = [pltpu.SemaphoreType.DMA((comm.n_sems,)), pltpu.SemaphoreType.DMA((comm.n_sems,))]
    return list(comm.ins), [ANY] * len(comm.ins), list(comm.out_shapes), sems


def _comm_run(comm, grid, refs, n_in, n_out):
    if comm is None:
        return (lambda: None), (lambda: None)
    nci, nco = len(comm.ins), len(comm.out_shapes)
    cin = refs[n_in:n_in + nci]
    cout = refs[n_in + nci + n_out:n_in + nci + n_out + nco]
    send_sems, recv_sems = refs[-2], refs[-1]
    ids = [pl.program_id(d) for d in range(len(grid))]
    is_first = ids[0] == 0
    is_last = ids[0] == grid[0] - 1
    for d in range(1, len(grid)):
        is_first = is_first & (ids[d] == 0)
        is_last = is_last & (ids[d] == grid[d] - 1)

    def first():
        @pl.when(is_first)
        def _():
            comm.start(cin, cout, send_sems, recv_sems)

    def last():
        @pl.when(is_last)
        def _():
            comm.finish(cin, cout, send_sems, recv_sems)

    return first, last


def _call(name, fn, grid, ins, outs, pairs=(), comm=None, j_outer=False):
    in_pair = {i for p in pairs for i in p[:2]}
    n_in, n_out = len(ins), len(outs)
    c_arrays, c_in_specs, c_out_shapes, c_sems = _comm_parts(comm)

    def body(*refs):
        first, last = _comm_run(comm, grid, refs, n_in, n_out)
        first()
        accs = []
        for ia, ib, kind in pairs:
            a, b = refs[ia][...].astype(BF16), refs[ib][...].astype(BF16)
            accs.append(_dot(a, b) if kind == "nn" else _dot_nt(a, b))
        vals = [refs[i][...] for i in range(n_in) if i not in in_pair]
        res = fn(accs, vals)
        out_refs = refs[n_in + len(c_arrays):n_in + len(c_arrays) + n_out]
        assert len(res) == len(out_refs), (name, len(res), len(out_refs))
        for o_ref, val in zip(out_refs, res):
            o_ref[...] = val.astype(o_ref.dtype)
        last()

    if j_outer:
        grid = (grid[1], grid[0])
        swap = lambda im: (lambda j, i: im(i, j))
        ins = [(a, blk, swap(im)) for a, blk, im in ins]
        outs = [(shp, dt, blk, swap(im)) for shp, dt, blk, im in outs]

    return pl.pallas_call(
        body, name=name, grid=grid,
        in_specs=[pl.BlockSpec(blk, im) for _, blk, im in ins] + c_in_specs,
        out_specs=[pl.BlockSpec(blk, im) for _, _, blk, im in outs] + [ANY] * len(c_out_shapes),
        out_shape=[jax.ShapeDtypeStruct(shp, dt) for shp, dt, _, _ in outs] + c_out_shapes,
        scratch_shapes=c_sems,
        compiler_params=_params(*(["arbitrary"] * len(grid))))(*[a for a, _, _ in ins], *c_arrays)


def _tile_call(name, fn, M, N, tm, tn, *, pairs=(), tiles=(), consts=(), outs=(), parts=0, comm=None,
               j_outer=False):
    gi, gj = M // tm, N // tn
    assert gi * tm == M and gj * tn == N, (name, M, N, tm, tn)
    ins, prs = [], []
    for a, a_col, b, kind in pairs:
        K = b.shape[0] if kind == "nn" else b.shape[1]
        ins.append((a, (tm, K), lambda i, j, c=a_col: (i, c)))
        if kind == "nn":
            ins.append((b, (K, tn), lambda i, j: (0, j)))
        else:
            ins.append((b, (tn, K), lambda i, j: (j, 0)))
        prs.append((len(ins) - 2, len(ins) - 1, kind))
    for arr, off in tiles:
        ins.append((arr, (tm, tn), lambda i, j, o=off: (i, j + o)))
    for arr in consts:
        ins.append((arr, arr.shape, lambda i, j: (0, 0)))
    out_l = [((M, N), dt, (tm, tn), lambda i, j: (i, j)) for dt in outs]
    out_l += [((gi * 8, N), F32, (8, tn), lambda i, j: (i, j))] * parts
    nt = len(tiles)

    def wrapped(accs, vals):
        return fn(accs, vals[:nt], vals[nt:])

    return _call(name, wrapped, (gi, gj), ins, out_l, prs, comm=comm, j_outer=j_outer)


def _mm_tn(name, grid, a_in, b_in, outs):
    nk = grid[2]
    tm = [d for d in a_in[1] if d is not None][1]
    tn = [d for d in b_in[1] if d is not None][1]

    def body(a_ref, b_ref, *rest):
        out_refs, acc_ref = rest[:-1], rest[-1]
        k = pl.program_id(2)

        @pl.when(k == 0)
        def _():
            acc_ref[...] = jnp.zeros_like(acc_ref)

        acc_ref[...] += _dot_tn(a_ref[...].astype(BF16), b_ref[...].astype(BF16))

        @pl.when(k == nk - 1)
        def _():
            for o_ref in out_refs:
                o_ref[...] = acc_ref[...].astype(o_ref.dtype)

    return pl.pallas_call(
        body, name=name, grid=grid,
        in_specs=[pl.BlockSpec(a_in[1], a_in[2]), pl.BlockSpec(b_in[1], b_in[2])],
        out_specs=[pl.BlockSpec(blk, im) for _, _, blk, im in outs],
        out_shape=[jax.ShapeDtypeStruct(shp, dt) for shp, dt, _, _ in outs],
        scratch_shapes=[pltpu.VMEM((tm, tn), F32)],
        compiler_params=_params("arbitrary", "arbitrary", "arbitrary"))(a_in[0], b_in[0])


def _grad_pair(shape, block, imap):
    return [(shape, F32, block, imap), (shape, BF16, block, imap)]


def _mm_tn_rows(name, a, b, tk=2048):
    T, a_w = a.shape
    b_w = b.shape[1]
    tm = _pick(a_w, 1408, 128)
    tk = _pick(T, tk, 128)
    g32, g16 = _mm_tn(name, (a_w // tm, 1, T // tk),
                      (a, (tk, tm), lambda i, j, k: (k, i)), (b, (tk, b_w), lambda i, j, k: (k, 0)),
                      _grad_pair((a_w, b_w), (tm, b_w), lambda i, j, k: (i, 0)))
    shp = (N_CHIPS, a_w // N_CHIPS, b_w)
    return g32.reshape(shp), g16.reshape(shp)


def _mm_tn_cols(name, a, b, tk=4096):
    T, a_w = a.shape
    b_w = b.shape[1]
    tk = _pick(T, tk, 128)
    g32, g16 = _mm_tn(name, (1, 1, T // tk),
                      (a, (tk, a_w), lambda i, j, k: (k, 0)), (b, (tk, b_w), lambda i, j, k: (k, 0)),
                      _grad_pair((a_w, b_w), (a_w, b_w), lambda i, j, k: (0, 0)))
    to_sh = lambda t: t.reshape(a_w, N_CHIPS, b_w // N_CHIPS).transpose(1, 0, 2)
    return to_sh(g32), to_sh(g16)


def _colsum(name, x):
    def body(x_ref, o_ref):
        o_ref[...] = jnp.sum(x_ref[...], axis=0, keepdims=True)
    return pl.pallas_call(body, name=name, out_shape=jax.ShapeDtypeStruct((1, x.shape[1]), F32))(x)


def _rms_hat(h):
    return h * lax.rsqrt(jnp.mean(h * h, axis=-1, keepdims=True) + EPS)


def _rms_bwd_vals(dn, h, g):
    r = lax.rsqrt(jnp.mean(h * h, axis=-1, keepdims=True) + EPS)
    nh = h * r
    gd = dn * g
    dh = r * (gd - nh * jnp.mean(gd * nh, axis=-1, keepdims=True))
    return dh, _group8(dn * nh)


def _rms_fwd(name, h, g, tm=512, comm=None):
    T = h.shape[0]

    def fn(accs, tv, cv):
        return [_rms_hat(tv[0]) * cv[0]]

    return _tile_call(name, fn, T, D_MODEL, _pick(T, tm), D_MODEL, tiles=[(h, 0)], consts=[g], outs=[BF16],
                      comm=comm)


def _ffn_fwd(tag, h, g, w_in, w_out, g_next, n=None, comm_norm=None, w_in_of=None, comm_in=None, comm_out=None,
             w_out_of=None):
    T = h.shape[0]
    if n is None:
        n, *got_norm = _rms_fwd(tag + "_norm", h, g, comm=comm_norm)
        if w_in_of is not None:
            w_in = w_in_of(got_norm)
    tm = _pick(T, 1024)
    wblk = (None, D_MODEL, FF_SHARD)

    def act(accs, vals):
        gate, up = accs
        return [gate, up, gate * _sigmoid(gate) * up]

    tile = lambda: ((T, D_FF), BF16, (tm, FF_SHARD), lambda i, j: (i, j))
    gate, up, a, *got_in = _call(
        tag + "_in", act, (T // tm, 2),
        [(n, (tm, D_MODEL), lambda i, j: (i, 0)),
         (w_in, wblk, lambda i, j: (j, 0, 0)), (w_in, wblk, lambda i, j: (j + 2, 0, 0))],
        [tile(), tile(), tile()], pairs=[(0, 1, "nn"), (0, 2, "nn")], comm=comm_in, j_outer=True)

    def res(accs, tv, cv):
        h_new = tv[0] + 0.5 * accs[0]
        return [h_new, _rms_hat(h_new) * cv[0]]

    if w_out_of is not None:
        w_out = w_out_of(got_in)
    h_new, n_next, *got_out = _tile_call(
        tag + "_out", res, T, D_MODEL, _pick(T, 512), D_MODEL, pairs=[(a, 0, w_out, "nn")], tiles=[(h, 0)],
        consts=[g_next], outs=[F32, BF16], comm=comm_out)
    return h_new, n_next, (n, gate, up, a), got_out


def _ffn_bwd(tag, dh_out, df, h, g, w_in, w_out, saved, comm=None, comm_last=None):
    T = h.shape[0]
    n, gate, up, a = saved
    tm = _pick(T, 512)

    def dact(accs, vals):
        da = accs[0]
        gt, u = vals[0].astype(F32), vals[1].astype(F32)
        sg = _sigmoid(gt)
        silu = gt * sg
        return [jnp.stack([(da * u * (sg + silu * (1.0 - sg))).astype(BF16), (da * silu).astype(BF16)])]

    dz, *got = _call(
        tag + "_dact", dact, (T // tm, 2),
        [(df, (tm, D_MODEL), lambda i, j: (i, 0)), (w_out, (FF_SHARD, D_MODEL), lambda i, j: (j, 0)),
         (gate, (tm, FF_SHARD), lambda i, j: (i, j)), (up, (tm, FF_SHARD), lambda i, j: (i, j))],
        [((2, T, D_FF), BF16, (2, tm, FF_SHARD), lambda i, j: (0, i, j))], pairs=[(0, 1, "nt")], comm=comm,
        j_outer=True)
    dw_out = _mm_tn_rows(tag + "_dwout", a, df)
    tk = _pick(T, 2048, 128)
    dw_in = _mm_tn(tag + "_dwin", (1, N_CHIPS, T // tk),
                   (n, (tk, D_MODEL), lambda i, j, k: (k, 0)),
                   (dz, (None, tk, FF_SHARD), lambda i, j, k: (j // 2, k, j % 2)),
                   _grad_pair((N_CHIPS, D_MODEL, FF_SHARD), (None, D_MODEL, FF_SHARD), lambda i, j, k: (j, 0, 0)))

    def dnorm(accs, vals):
        dn = accs[0] + accs[1] + accs[2] + accs[3]
        dh, dg = _rms_bwd_vals(dn, vals[0], vals[2])
        dh = vals[1] + dh
        return [dh, dh, dg]

    tm2 = _pick(T, 512)
    ins = [(dz, (None, tm2, FF_SHARD), lambda i, j, s=s: (s // 2, i, s % 2)) for s in range(N_CHIPS)]
    ins += [(w_in, (None, D_MODEL, FF_SHARD), lambda i, j, s=s: (s, 0, 0)) for s in range(N_CHIPS)]
    ins += [(h, (tm2, D_MODEL), lambda i, j: (i, 0)), (dh_out, (tm2, D_MODEL), lambda i, j: (i, 0)),
            (g, g.shape, lambda i, j: (0, 0))]
    dh, dh16, dg, *got_last = _call(
        tag + "_dnorm", dnorm, (T // tm2, 1), ins,
        [((T, D_MODEL), F32, (tm2, D_MODEL), lambda i, j: (i, 0)),
         ((T, D_MODEL), BF16, (tm2, D_MODEL), lambda i, j: (i, 0)),
         ((T // tm2 * 8, D_MODEL), F32, (8, D_MODEL), lambda i, j: (i, 0))],
        pairs=[(s, N_CHIPS + s, "nt") for s in range(N_CHIPS)],
        comm=None if comm_last is None else comm_last(dw_in, dw_out))
    return dh, dh16, dg, dw_in, dw_out, got, got_last


def _t5_onehot():
    qi = np.arange(ATT_BLOCK)[:, None] + ATT_BLOCK
    kj = np.arange(2 * ATT_BLOCK)[None, :]
    nn = np.maximum(qi - kj, 0)
    max_exact = N_BUCKETS // 2
    large = max_exact + (np.log(np.maximum(nn, 1) / max_exact) / np.log(MAX_DISTANCE / max_exact)
                         * (N_BUCKETS - max_exact)).astype(np.int32)
    large = np.minimum(large, N_BUCKETS - 1)
    bucket = np.where(nn < max_exact, nn, large).astype(np.int32).reshape(-1)
    return (bucket[None, :] == np.arange(N_BUCKETS)[:, None]).astype(np.float32)


def _small_mm(name, a, b, sel):
    def body(a_ref, b_ref, o_ref):
        if sel == "right":
            o_ref[...] = _sel_right(a_ref[...], b_ref[...])
        else:
            o_ref[...] = _sel_left(a_ref[...], b_ref[...])
    return pl.pallas_call(body, name=name, out_shape=jax.ShapeDtypeStruct((a.shape[0], b.shape[1]), F32),
                          compiler_params=pltpu.CompilerParams(vmem_limit_bytes=V7X_VMEM_LIMIT))(a, b)


def _dup_heads(t):
    a, b = t[:, :HEAD_DIM], t[:, HEAD_DIM:]
    return jnp.concatenate([a, a, b, b], axis=1)


def _kv_layouts(proj):
    T = proj.shape[0]

    def fn(accs, tv, cv):
        return [tv[0], tv[1]]

    k, v = _tile_call("kv_cast", fn, T, 128, _pick(T, 1024), 128, tiles=[(proj, COL_AK), (proj, COL_AV)],
                      outs=[BF16, BF16])
    return _dup_heads(k), _dup_heads(v)


def _swa_masks():
    row = lax.broadcasted_iota(jnp.int32, (ATT_BLOCK, 2 * ATT_BLOCK), 0)
    col = lax.broadcasted_iota(jnp.int32, (ATT_BLOCK, 2 * ATT_BLOCK), 1)
    dist = ATT_BLOCK + row - col
    return (dist >= 0) & (dist < ATT_BLOCK), col


GROUP = 4


def _stack_group(blk, lo_q):
    zero = jnp.zeros_like(blk[:, :128])
    rows = []
    for pair in range(GROUP // 2):
        pb = blk[:, 128 * pair:128 * (pair + 1)]
        rows += [jnp.where(lo_q, pb, zero), jnp.where(lo_q, zero, pb)]
    return jnp.concatenate(rows, axis=0)


def _unstack_group(st, lo_q):
    pairs = [jnp.where(lo_q, st[256 * pair:256 * pair + 128], st[256 * pair + 128:256 * (pair + 1)])
             for pair in range(GROUP // 2)]
    return jnp.concatenate(pairs, axis=1)


def _swa_probs(s, bias_h, sink, valid):
    s = jnp.where(valid, s * (HEAD_DIM ** -0.5) + bias_h, -jnp.inf)
    m = jnp.maximum(jnp.max(s, axis=-1, keepdims=True), sink)
    e = jnp.exp(s - m)
    es = jnp.exp(sink - m)
    den = jnp.sum(e, axis=-1, keepdims=True) + es
    return e / den, es / den


def _swa_fwd(proj, kk2, vv2, bias, sinks, B, S):
    T = B * S
    nb = S // ATT_BLOCK

    def body(q_ref, k_ref, v_ref, bias_ref, sink_ref, o_ref, kpad, vpad):
        zeros = jnp.zeros((ATT_BLOCK, 256), BF16)
        kpad[pl.ds(0, ATT_BLOCK), :] = zeros
        vpad[pl.ds(0, ATT_BLOCK), :] = zeros
        kpad[pl.ds(ATT_BLOCK, S), :] = k_ref[...]
        vpad[pl.ds(ATT_BLOCK, S), :] = v_ref[...]
        valid0, col = _swa_masks()
        lo_q = lax.broadcasted_iota(jnp.int32, (1, 128), 1) < HEAD_DIM

        def blk(n, carry):
            r0 = pl.multiple_of(n * ATT_BLOCK, ATT_BLOCK)
            rows = pl.ds(r0, ATT_BLOCK)
            valid = valid0 & ((n > 0) | (col >= ATT_BLOCK))
            for g in range(N_Q_HEADS // GROUP):
                lanes = pl.ds(128 * g, 128)
                kg = kpad[pl.ds(r0, 2 * ATT_BLOCK), lanes]
                vg = vpad[pl.ds(r0, 2 * ATT_BLOCK), lanes]
                qm = _stack_group(q_ref[rows, pl.ds(256 * g, 256)].astype(BF16), lo_q)
                s = _dot_nt(qm, kg)
                ps = []
                for i in range(GROUP):
                    h = GROUP * g + i
                    p, _ = _swa_probs(s[ATT_BLOCK * i:ATT_BLOCK * (i + 1)], bias_ref[h], sink_ref[h], valid)
                    ps.append(p.astype(BF16))
                o = _dot(jnp.concatenate(ps, axis=0), vg)
                o_ref[rows, pl.ds(256 * g, 256)] = _unstack_group(o, lo_q).astype(o_ref.dtype)
            return carry

        if nb % 2 == 0:
            lax.fori_loop(0, nb // 2, lambda i, c: blk(2 * i + 1, blk(2 * i, c)), 0)
        else:
            lax.fori_loop(0, nb, blk, 0)

    return pl.pallas_call(
        body, name="swa_fwd", grid=(B,),
        in_specs=[pl.BlockSpec((S, 512), lambda b: (b, 0)),
                  pl.BlockSpec((S, 256), lambda b: (b, 0)),
                  pl.BlockSpec((S, 256), lambda b: (b, 0)),
                  pl.BlockSpec((N_Q_HEADS, ATT_BLOCK, 2 * ATT_BLOCK), lambda b: (0, 0, 0)),
                  pl.BlockSpec(memory_space=pltpu.SMEM)],
        out_specs=pl.BlockSpec((S, 512), lambda b: (b, 0)),
        out_shape=jax.ShapeDtypeStruct((T, 512), BF16),
        scratch_shapes=[pltpu.VMEM((S + ATT_BLOCK, 256), BF16), pltpu.VMEM((S + ATT_BLOCK, 256), BF16)],
        compiler_params=_params("arbitrary"))(proj, kk2, vv2, bias, sinks)


def _swa_bwd(proj, kk2, vv2, bias, sinks, datt, B, S):
    T = B * S
    nb = S // ATT_BLOCK

    def body(q_ref, k_ref, v_ref, bias_ref, sink_ref, do_ref, dq_ref, dk_ref, dv_ref, dbias_ref, dsink_ref,
             kpad, vpad, dkpad, dvpad):
        b = pl.program_id(0)

        @pl.when(b == 0)
        def _():
            dbias_ref[...] = jnp.zeros_like(dbias_ref)
            dsink_ref[...] = jnp.zeros_like(dsink_ref)

        zeros = jnp.zeros((ATT_BLOCK, 256), BF16)
        kpad[pl.ds(0, ATT_BLOCK), :] = zeros
        vpad[pl.ds(0, ATT_BLOCK), :] = zeros
        kpad[pl.ds(ATT_BLOCK, S), :] = k_ref[...]
        vpad[pl.ds(ATT_BLOCK, S), :] = v_ref[...]
        dkpad[...] = jnp.zeros_like(dkpad)
        dvpad[...] = jnp.zeros_like(dvpad)
        valid0, col = _swa_masks()
        lo_q = lax.broadcasted_iota(jnp.int32, (1, 128), 1) < HEAD_DIM
        scale = HEAD_DIM ** -0.5

        def blk(n, carry):
            r0 = pl.multiple_of(n * ATT_BLOCK, ATT_BLOCK)
            rows = pl.ds(r0, ATT_BLOCK)
            band = pl.ds(r0, 2 * ATT_BLOCK)
            valid = valid0 & ((n > 0) | (col >= ATT_BLOCK))
            for g in range(N_Q_HEADS // GROUP):
                lanes = pl.ds(128 * g, 128)
                kg = kpad[band, lanes]
                vg = vpad[band, lanes]
                qm = _stack_group(q_ref[rows, pl.ds(256 * g, 256)].astype(BF16), lo_q)
                dom = _stack_group(do_ref[rows, pl.ds(256 * g, 256)], lo_q)
                s = _dot_nt(qm, kg)
                dp = _dot_nt(dom, vg)
                pst, dst = [], []
                for i in range(GROUP):
                    h = GROUP * g + i
                    sl = slice(ATT_BLOCK * i, ATT_BLOCK * (i + 1))
                    p, ps = _swa_probs(s[sl], bias_ref[h], sink_ref[h], valid)
                    delta = jnp.sum(p * dp[sl], axis=-1, keepdims=True)
                    ds = p * (dp[sl] - delta)
                    dbias_ref[h] += ds
                    dsink_ref[pl.ds(h, 1), :] += -jnp.sum(jnp.broadcast_to(ps * delta, (ATT_BLOCK, 128)),
                                                          axis=0, keepdims=True)
                    pst.append(p.astype(BF16))
                    dst.append((ds * scale).astype(BF16))
                pst, dst = jnp.concatenate(pst, axis=0), jnp.concatenate(dst, axis=0)
                dq_ref[rows, pl.ds(256 * g, 256)] = _unstack_group(_dot(dst, kg), lo_q).astype(dq_ref.dtype)
                dkpad[band, lanes] += _dot_tn(dst, qm)
                dvpad[band, lanes] += _dot_tn(pst, dom)
            return carry

        if nb % 2 == 0:
            lax.fori_loop(0, nb // 2, lambda i, c: blk(2 * i + 1, blk(2 * i, c)), 0)
        else:
            lax.fori_loop(0, nb, blk, 0)
        lo_out = lax.broadcasted_iota(jnp.int32, (1, 128), 1) < HEAD_DIM

        def fold(pad_ref):
            halves = []
            for g in range(N_Q_HEADS // GROUP):
                t = pad_ref[pl.ds(ATT_BLOCK, S), pl.ds(128 * g, 128)]
                halves.append(t + pltpu.roll(t, HEAD_DIM, 1))
            return jnp.where(lo_out, halves[0], halves[1])

        dk_ref[...] = fold(dkpad).astype(dk_ref.dtype)
        dv_ref[...] = fold(dvpad).astype(dv_ref.dtype)

    return pl.pallas_call(
        body, name="swa_bwd", grid=(B,),
        in_specs=[pl.BlockSpec((S, 512), lambda b: (b, 0)),
                  pl.BlockSpec((S, 256), lambda b: (b, 0)),
                  pl.BlockSpec((S, 256), lambda b: (b, 0)),
                  pl.BlockSpec((N_Q_HEADS, ATT_BLOCK, 2 * ATT_BLOCK), lambda b: (0, 0, 0)),
                  pl.BlockSpec(memory_space=pltpu.SMEM),
                  pl.BlockSpec((S, 512), lambda b: (b, 0))],
        out_specs=[pl.BlockSpec((S, 512), lambda b: (b, 0)),
                   pl.BlockSpec((S, 128), lambda b: (b, 0)),
                   pl.BlockSpec((S, 128), lambda b: (b, 0)),
                   pl.BlockSpec((N_Q_HEADS, ATT_BLOCK, 2 * ATT_BLOCK), lambda b: (0, 0, 0)),
                   pl.BlockSpec((N_Q_HEADS, 128), lambda b: (0, 0))],
        out_shape=[jax.ShapeDtypeStruct((T, 512), BF16),
                   jax.ShapeDtypeStruct((T, 128), BF16),
                   jax.ShapeDtypeStruct((T, 128), BF16),
                   jax.ShapeDtypeStruct((N_Q_HEADS, ATT_BLOCK, 2 * ATT_BLOCK), F32),
                   jax.ShapeDtypeStruct((N_Q_HEADS, 128), F32)],
        scratch_shapes=[pltpu.VMEM((S + ATT_BLOCK, 256), BF16), pltpu.VMEM((S + ATT_BLOCK, 256), BF16),
                        pltpu.VMEM((S + ATT_BLOCK, 256), F32), pltpu.VMEM((S + ATT_BLOCK, 256), F32)],
        compiler_params=_params("arbitrary"))(proj, kk2, vv2, bias, sinks, datt)


def _hgrn_gates(z, lb):
    sg = _sigmoid(z)
    f = lb + (1.0 - lb) * sg
    return sg, f, jnp.log(f), 1.0 - f


def _hgrn_consts():
    r = lax.broadcasted_iota(jnp.int32, (CHUNK, CHUNK), 0)
    c = lax.broadcasted_iota(jnp.int32, (CHUNK, CHUNK), 1)
    tril = (r >= c).astype(BF16)
    triu = (r <= c).astype(BF16)
    causal = r >= c
    below = (r // SUB) > (c // SUB)
    inside = ((r // SUB) == (c // SUB)) & causal
    return tril, triu, causal, below, inside, c


def _block_rows(ref, lanes, s):
    rows = []
    for i in range(N_SUB):
        if SUB * i + s < 0:
            rows.append(jnp.zeros((SUB, REC_DIM), F32))
        else:
            rows.append(jnp.broadcast_to(ref[pl.ds(SUB * i + s, 1), lanes], (SUB, REC_DIM)))
    return jnp.concatenate(rows, axis=0)


def _hgrn_offdiag(q, k, bcum, b_ref, lanes):
    eq = jnp.exp(jnp.minimum(bcum - _block_rows(b_ref, lanes, -1), 0.0))
    qe = q * eq
    zero = jnp.zeros((SUB, REC_DIM), F32)
    q_rows, k_cols, eks = [jnp.zeros((SUB, (N_SUB - 1) * REC_DIM), F32)], [], []
    for i in range(1, N_SUB):
        q_rows.append(jnp.concatenate([zero] * (i - 1) + [qe[SUB * i:SUB * (i + 1), :]] + [zero] * (N_SUB - 1 - i),
                                      axis=1))
        p = b_ref[pl.ds(SUB * i - 1, 1), lanes]
        pad = jnp.zeros((CHUNK - SUB * i, REC_DIM), F32)
        ek = jnp.concatenate([jnp.exp(p - b_ref[pl.ds(0, SUB * i), lanes]), pad], axis=0)
        k_cols.append(k * ek)
        eks.append(ek)
    return jnp.concatenate(q_rows, axis=0), jnp.concatenate(k_cols, axis=1), eq, eks


def _hgrn_fwd(proj, lb_param, B, S, comm=None):
    T = B * S
    nc = S // CHUNK
    fwd_unroll = 4 if nc % 4 == 0 else 2
    c_arrays, c_in_specs, c_out_shapes, c_sems = _comm_parts(comm)
    nci, nco = len(c_arrays), len(c_out_shapes)

    def body(*refs):
        q_ref, z_ref, v_ref, lb_ref = refs[:4]
        o_ref, st_ref = refs[4 + nci:6 + nci]
        k_slots, b_slots = refs[6 + nci + nco:8 + nci + nco]
        comm_first, comm_last = _comm_run(comm, (B, REC_HEADS // HGRN_PAIR), refs, 4, 2)
        comm_first()
        tril, _, _, below, inside, col = _hgrn_consts()
        col_s = col & (SUB - 1)

        def chunk(ci, hts, slot):
            k_s, b_s = k_slots.at[slot], b_slots.at[slot]
            r0 = pl.multiple_of(ci * CHUNK, CHUNK)
            lb = _sigmoid(lb_ref[0:1, :] - lb_ref[1:2, :])
            _, _, g_all, k_all = _hgrn_gates(z_ref[pl.ds(r0, CHUNK), :], lb)
            b_all = _sel_left(tril, g_all)
            k_s[...] = k_all
            b_s[...] = b_all
            new = []
            for e, ht in enumerate(hts):
                lanes = pl.ds(REC_DIM * e, REC_DIM)
                cols = slice(REC_DIM * e, REC_DIM * (e + 1))
                q = q_ref[pl.ds(r0, CHUNK), lanes]
                v = v_ref[pl.ds(r0, CHUNK), lanes]
                k, bcum = k_all[:, cols], b_all[:, cols]
                st_ref[e * nc + ci] = ht
                qst, kst, _, _ = _hgrn_offdiag(q, k, bcum, b_s, lanes)
                d = jnp.zeros((CHUNK, CHUNK), F32)
                for s in range(SUB):
                    w = jnp.exp(jnp.minimum(bcum - _block_rows(b_s, lanes, s), 0.0))
                    colv = jnp.sum(q * _block_rows(k_s, lanes, s) * w, axis=-1, keepdims=True)
                    d = jnp.where(col_s == s, colv, d)
                a = jnp.where(below, _dot_nt(qst.astype(BF16), kst.astype(BF16)), 0.0) + jnp.where(inside, d, 0.0)
                vb = v.astype(BF16)
                qb = (q * jnp.exp(bcum)).astype(BF16)
                o_ref[pl.ds(r0, CHUNK), lanes] = _dot(a.astype(BF16), vb) + _dot_nt(qb, ht.astype(BF16))
                b_last = b_s[pl.ds(CHUNK - 1, 1), lanes]
                kb = (k * jnp.exp(b_last - bcum)).astype(BF16)
                new.append(ht * jnp.exp(b_last) + _dot_tn(vb, kb))
            return tuple(new)

        def trip(i, hts):
            for u in range(fwd_unroll):
                hts = chunk(fwd_unroll * i + u, hts, u)
            return hts

        lax.fori_loop(0, nc // fwd_unroll, trip, tuple(jnp.zeros((REC_DIM, REC_DIM), F32) for _ in range(HGRN_PAIR)))
        comm_last()

    hp, wd = REC_HEADS // HGRN_PAIR, HGRN_PAIR * REC_DIM
    cq, cf, ci_ = (c * REC_DIM // wd for c in (COL_RQ, COL_RF, COL_RI))
    return pl.pallas_call(
        body, name="hgrn_fwd", grid=(B, hp),
        in_specs=[pl.BlockSpec((S, wd), lambda b, h: (b, cq + h)),
                  pl.BlockSpec((S, wd), lambda b, h: (b, cf + h)),
                  pl.BlockSpec((S, wd), lambda b, h: (b, ci_ + h)),
                  pl.BlockSpec((2, wd), lambda b, h: (0, h))] + c_in_specs,
        out_specs=[pl.BlockSpec((S, wd), lambda b, h: (b, h)),
                   pl.BlockSpec((HGRN_PAIR * nc, REC_DIM, REC_DIM), lambda b, h: (b * hp + h, 0, 0))] + [ANY] * nco,
        out_shape=[jax.ShapeDtypeStruct((T, 512), F32),
                   jax.ShapeDtypeStruct((B * REC_HEADS * nc, REC_DIM, REC_DIM), F32)] + c_out_shapes,
        scratch_shapes=[pltpu.VMEM((fwd_unroll, CHUNK, wd), F32), pltpu.VMEM((fwd_unroll, CHUNK, wd), F32)] + c_sems,
        compiler_params=_params("arbitrary", "arbitrary"))(proj, proj, proj, lb_param, *c_arrays)


def _hgrn_bwd(proj, lb_param, states, do, B, S, comm=None):
    T = B * S
    nc = S // CHUNK
    bwd_unroll = 4 if nc % 4 == 0 else 2
    c_arrays, c_in_specs, c_out_shapes, c_sems = _comm_parts(comm)
    nci, nco = len(c_arrays), len(c_out_shapes)

    def body(*refs):
        q_ref, z_ref, v_ref, lb_ref, st_ref, do_ref = refs[:6]
        dq_ref, dz_ref, dv_ref, dlb_ref = refs[6 + nci:10 + nci]
        slots = refs[10 + nci + nco:14 + nci + nco]
        comm_first, comm_last = _comm_run(comm, (B, REC_HEADS // HGRN_PAIR), refs, 6, 4)
        comm_first()
        tril, triu, causal, below, inside, col = _hgrn_consts()
        col_s = col & (SUB - 1)
        last_row = lax.broadcasted_iota(jnp.int32, (CHUNK, 1), 0) == CHUNK - 1
        rc = lax.broadcasted_iota(jnp.int32, (CHUNK, SUB * REC_DIM), 0)
        lc = lax.broadcasted_iota(jnp.int32, (CHUNK, SUB * REC_DIM), 1)
        spread = ((rc & (SUB - 1)) == (lc // REC_DIM)).astype(BF16)
        rr = lax.broadcasted_iota(jnp.int32, (CHUNK, SUB * CHUNK), 0)
        cc = lax.broadcasted_iota(jnp.int32, (CHUNK, SUB * CHUNK), 1)
        gather = (((rr // SUB) == ((cc & (CHUNK - 1)) // SUB)) & ((rr & (SUB - 1)) == (cc // CHUNK))).astype(BF16)

        heads = range(HGRN_PAIR)
        cols = [slice(REC_DIM * e, REC_DIM * (e + 1)) for e in heads]
        lanes = [pl.ds(REC_DIM * e, REC_DIM) for e in heads]
        lane_cat = lambda vals: jnp.concatenate(vals, axis=1)
        row_cat = lambda vals: jnp.concatenate(vals, axis=0)

        def chunk(it, carry, slot):
            k_s, b_s, pc_hi, pc_lo = (r.at[slot] for r in slots)
            dhts, dlb = carry
            ci = nc - 1 - it
            r0 = pl.multiple_of(ci * CHUNK, CHUNK)
            rows = pl.ds(r0, CHUNK)
            lb = _sigmoid(lb_ref[0:1, :] - lb_ref[1:2, :])
            sg, f, g_all, k_all = _hgrn_gates(z_ref[rows, :], lb)
            b_all = _sel_left(tril, g_all)
            k_s[...] = k_all
            b_s[...] = b_all
            q_all = q_ref[rows, :]
            das, hd = [], []
            for e in heads:
                vb, dob = v_ref[rows, lanes[e]].astype(BF16), do_ref[rows, lanes[e]].astype(BF16)
                da = jnp.where(causal, _dot_nt(dob, vb), 0.0)
                das.append(jnp.where(inside, da, 0.0))
                hd.append((vb, dob, da))
            da_hi, da_lo = _split2(row_cat(das))
            da_in = _dot(da_hi, spread) + _dot(da_lo, spread)
            ds, dqs = [], []
            for e in heads:
                q, bcum = q_all[:, cols[e]], b_all[:, cols[e]]
                d = jnp.zeros((CHUNK, CHUNK), F32)
                dq = jnp.zeros((CHUNK, REC_DIM), F32)
                for s in range(SUB):
                    w = jnp.exp(jnp.minimum(bcum - _block_rows(b_s, lanes[e], s), 0.0))
                    ks = _block_rows(k_s, lanes[e], s)
                    qw = q * w
                    d = jnp.where(col_s == s, jnp.sum(qw * ks, axis=-1, keepdims=True), d)
                    da_s = da_in[CHUNK * e:CHUNK * (e + 1), REC_DIM * s:REC_DIM * (s + 1)]
                    dq = dq + da_s * ks * w
                    hi, lo = _split2(da_s * qw)
                    pc_hi[pl.ds(CHUNK * s, CHUNK), lanes[e]] = hi
                    pc_lo[pl.ds(CHUNK * s, CHUNK), lanes[e]] = lo
                ds.append(d)
                dqs.append(dq)
            dk_in = _dot(gather, pc_hi[...]) + _dot(gather, pc_lo[...])
            dq_out, dk_out, dv_out, db_out, new_dhts = [], [], [], [], []
            for e in heads:
                q, k, bcum = q_all[:, cols[e]], k_all[:, cols[e]], b_all[:, cols[e]]
                vb, dob, da = hd[e]
                dht, ht = dhts[e], st_ref[e * nc + ci]
                qst, kst, eq, eks = _hgrn_offdiag(q, k, bcum, b_s, lanes[e])
                qst_b, kst_b = qst.astype(BF16), kst.astype(BF16)
                a = jnp.where(below, _dot_nt(qst_b, kst_b), 0.0) + jnp.where(inside, ds[e], 0.0)
                da_off = jnp.where(below, da, 0.0).astype(BF16)
                dqst = _dot(da_off, kst_b)
                dkst = _dot_tn(da_off, qst_b)
                dk = dk_in[:, cols[e]]
                dq_rows = [jnp.zeros((SUB, REC_DIM), F32)]
                for i in range(1, N_SUB):
                    dq_rows.append(dqst[SUB * i:SUB * (i + 1), REC_DIM * (i - 1):REC_DIM * i])
                    dk = dk + dkst[:, REC_DIM * (i - 1):REC_DIM * i] * eks[i - 1]
                dq = dqs[e] + row_cat(dq_rows) * eq
                eb = jnp.exp(bcum)
                b_last = b_s[pl.ds(CHUNK - 1, 1), lanes[e]]
                el = jnp.exp(b_last)
                ekb = jnp.exp(b_last - bcum)
                qb = (q * eb).astype(BF16)
                kb = k * ekb
                dhb = dht.astype(BF16)
                dv_out.append(_dot_tn(a.astype(BF16), dob) + _dot_nt(kb.astype(BF16), dhb))
                dqb = _dot(dob, ht.astype(BF16))
                dkb = _dot(vb, dhb)
                new_dhts.append(dht * el + _dot_tn(dob, qb))
                dq = dq + eb * dqb
                dk = dk + ekb * dkb
                edge = jnp.sum(kb * dkb, axis=0, keepdims=True) + el * jnp.sum(ht * dht, axis=0, keepdims=True)
                db_out.append(q * dq - k * dk + jnp.where(last_row, edge, 0.0))
                dq_out.append(dq)
                dk_out.append(dk)
            dk_all = lane_cat(dk_out)
            db_hi, db_lo = _split2(lane_cat(db_out))
            dg = _dot(triu, db_hi) + _dot(triu, db_lo)
            df = dg / f - dk_all
            dz_ref[rows, :] = (df * (1.0 - lb) * sg * (1.0 - sg)).astype(dz_ref.dtype)
            dq_ref[rows, :] = lane_cat(dq_out).astype(dq_ref.dtype)
            dv_ref[rows, :] = lane_cat(dv_out).astype(dv_ref.dtype)
            return tuple(new_dhts), dlb + jnp.sum(df * (1.0 - sg), axis=0, keepdims=True)

        zero = (tuple(jnp.zeros((REC_DIM, REC_DIM), F32) for _ in heads), jnp.zeros((1, HGRN_PAIR * REC_DIM), F32))
        def trip(i, carry):
            for u in range(bwd_unroll):
                carry = chunk(bwd_unroll * i + u, carry, u)
            return carry

        _, dlb = lax.fori_loop(0, nc // bwd_unroll, trip, zero)
        lb = _sigmoid(lb_ref[0:1, :] - lb_ref[1:2, :])
        dlb_ref[...] = jnp.broadcast_to(dlb * lb * (1.0 - lb), (8, HGRN_PAIR * REC_DIM))
        comm_last()

    hp, wd = REC_HEADS // HGRN_PAIR, HGRN_PAIR * REC_DIM
    cq, cf, ci_ = (c * REC_DIM // wd for c in (COL_RQ, COL_RF, COL_RI))
    return pl.pallas_call(
        body, name="hgrn_bwd", grid=(B, hp),
        in_specs=[pl.BlockSpec((S, wd), lambda b, h: (b, cq + h)),
                  pl.BlockSpec((S, wd), lambda b, h: (b, cf + h)),
                  pl.BlockSpec((S, wd), lambda b, h: (b, ci_ + h)),
                  pl.BlockSpec((2, wd), lambda b, h: (0, h)),
                  pl.BlockSpec((HGRN_PAIR * nc, REC_DIM, REC_DIM), lambda b, h: (b * hp + h, 0, 0)),
                  pl.BlockSpec((S, wd), lambda b, h: (b, h))] + c_in_specs,
        out_specs=[pl.BlockSpec((S, wd), lambda b, h: (b, h))] * 3
        + [pl.BlockSpec((8, wd), lambda b, h: (b, h))] + [ANY] * nco,
        out_shape=[jax.ShapeDtypeStruct((T, 512), BF16)] * 3 + [jax.ShapeDtypeStruct((B * 8, 512), F32)]
        + c_out_shapes,
        scratch_shapes=[pltpu.VMEM((bwd_unroll, CHUNK, wd), F32)] * 2
        + [pltpu.VMEM((bwd_unroll, SUB * CHUNK, wd), BF16)] * 2 + c_sems,
        compiler_params=_params("arbitrary", "arbitrary"))(proj, proj, proj, lb_param, states, do, *c_arrays)


def _rec_gate_fwd(rec, proj, rec_norm):
    T = rec.shape[0]

    def fn(accs, tv, cv):
        return [_rms_hat(tv[0]) * cv[0] * _sigmoid(tv[1])]

    return _tile_call("rec_gate", fn, T, 512, _pick(T, 1024), REC_DIM, tiles=[(rec, 0), (proj, COL_RG)],
                      consts=[rec_norm], outs=[BF16])[0]


def _rec_gate_bwd(dyb, w_rec_proj, rec, proj, rec_norm):
    T = rec.shape[0]

    def fn(accs, tv, cv):
        d, r, rg = accs[0], tv[0], tv[1]
        sg = _sigmoid(rg)
        rn = _rms_hat(r) * cv[0]
        dh, dg = _rms_bwd_vals(d * sg, r, cv[0])
        return [dh, d * rn * sg * (1.0 - sg), dg]

    return _tile_call("rec_gate_bwd", fn, T, 512, _pick(T, 1024), REC_DIM, pairs=[(dyb, 0, w_rec_proj, "nt")],
                      tiles=[(rec, 0), (proj, COL_RG)], consts=[rec_norm], outs=[F32, BF16], parts=1)


def _mix_out_fwd(att, recn, proj, w_att_proj, w_rec_proj, w_out, h1, g_next):
    T = att.shape[0]
    tn = 256

    def merge(accs, tv, cv):
        ya, yb = accs
        return [ya, yb, _sigmoid(tv[0]) * ya + _sigmoid(tv[1]) * yb]

    ya, yb, merged = _tile_call(
        "merge", merge, T, D_MODEL, _pick(T, 1024), tn,
        pairs=[(att, 0, w_att_proj, "nn"), (recn, 0, w_rec_proj, "nn")],
        tiles=[(proj, COL_GA * 128 // tn), (proj, COL_GB * 128 // tn)], outs=[BF16] * 3)

    def res(accs, tv, cv):
        h2 = tv[0] + accs[0]
        return [h2, _rms_hat(h2) * cv[0]]

    h2, n2 = _tile_call("mix_out", res, T, D_MODEL, _pick(T, 512), D_MODEL, pairs=[(merged, 0, w_out, "nn")],
                        tiles=[(h1, 0)], consts=[g_next], outs=[F32, BF16])
    return h2, n2, (ya, yb, merged)


GATHER_FIRST = ("w_ffn1_in",)
GATHER_MIX = ("w_ffn1_out", "w_in")
GATHER_PROJ = ("w_att_proj", "w_rec_proj", "w_out")
GATHER_TAIL = ("w_ffn2_out", "w_ple_gate", "w_ple_proj")
GATHER_LAST = ("w_ffn2_in",)
SCATTER_LATE = ("w_ple_gate", "w_ple_proj", "w_ffn2_in", "w_ffn2_out")
SCATTER_MIX = ("w_out", "w_att_proj", "w_rec_proj", "w_in")
SCATTER_LAST = ("w_ffn1_in", "w_ffn1_out")


def _local_step(x, p, tgt, w, mine16, cc, me_chip, B, S):
    T = B * S
    w = dict(w)
    g_ffn1, g_mix, g_ffn2, g_ple = w["norm_ffn1"], w["norm_mix"], w["norm_ffn2"], w["norm_ple"]
    g_fin = w["norm_final"].reshape(1, D_MODEL)
    grads, part, from_chips = {}, {}, {}

    def gather(names):
        return _gather_comm([mine16[n] for n in names])

    def place(names, got):
        for n, g in zip(names, got):
            full = lax.dynamic_update_index_in_dim(g, mine16[n], me_chip, 0)
            w[n] = full if n in ("w_ffn1_in", "w_ffn2_in") else _natural(n, full)

    def swap(names):
        return _swap_comm([grads[n][1] for n in names])

    def after_swap(names, from_sib):
        for n, fs in zip(names, from_sib):
            part[n] = _add_sibling("rs_add_sib_" + n, grads[n][0], fs, cc)
        return _scatter_comm([part[n][1] for n in names])

    def scattered(names, got):
        for n, g in zip(names, got):
            from_chips[n] = g

    def ffn1_in_weight(got):
        place(GATHER_FIRST, got)
        return w["w_ffn1_in"]

    def ffn1_out_weight(got):
        place(GATHER_MIX, got)
        return w["w_ffn1_out"]

    h1, u, sv1, got_proj = _ffn_fwd("ffn1", x, g_ffn1, None, None, g_mix, comm_norm=gather(GATHER_FIRST),
                                    w_in_of=ffn1_in_weight, comm_in=gather(GATHER_MIX),
                                    comm_out=gather(GATHER_PROJ), w_out_of=ffn1_out_weight)
    place(GATHER_PROJ, got_proj)

    def ident(accs, tv, cv):
        return [accs[0]]

    proj, *got_tail = _tile_call("in_proj", ident, T, IN_W, _pick(T, 512), IN_W // 2,
                                 pairs=[(u, 0, w["w_in"], "nn")], outs=[F32], j_outer=True, comm=gather(GATHER_TAIL))
    place(GATHER_TAIL, got_tail)
    onehot = jnp.asarray(_t5_onehot())
    bias = _small_mm("t5_bias", w["rel_bias"].T, onehot.astype(BF16), "right")
    bias = bias.reshape(N_Q_HEADS, ATT_BLOCK, 2 * ATT_BLOCK)
    sinks = w["attn_sinks"].reshape(N_Q_HEADS)
    kk2, vv2 = _kv_layouts(proj)
    att = _swa_fwd(proj, kk2, vv2, bias, sinks, B, S)
    rec, states, *got_last = _hgrn_fwd(proj, w["lb_param"], B, S, comm=gather(GATHER_LAST))
    place(GATHER_LAST, got_last)
    recn = _rec_gate_fwd(rec, proj, w["rec_norm"])
    h2, n2, (ya, yb, merged) = _mix_out_fwd(att, recn, proj, w["w_att_proj"], w["w_rec_proj"], w["w_out"], h1,
                                            g_ffn2)
    h3, n3, sv2, _ = _ffn_fwd("ffn2", h2, g_ffn2, w["w_ffn2_in"], w["w_ffn2_out"], g_ple, n=n2)

    def ple(accs, tv, cv):
        gate = _sigmoid(accs[0])
        return [gate, accs[1], tv[0] + gate * accs[1]]

    gate_p, pp, h4 = _tile_call(
        "ple", ple, T, D_MODEL, _pick(T, 512), D_MODEL,
        pairs=[(n3, 0, w["w_ple_gate"], "nn"), (p, 0, w["w_ple_proj"], "nn")], tiles=[(h3, 0)],
        outs=[BF16, BF16, F32])

    def head(accs, tv, cv):
        h, t, gt, ppv = tv[0], tv[1], tv[2].astype(F32), tv[3].astype(F32)
        err = _rms_hat(h) * cv[0] - t
        dh, dg = _rms_bwd_vals(err * (1.0 / D_MODEL), h, cv[0])
        return [dh, dh * ppv * gt * (1.0 - gt), dh * gt, _group8(err * err), dg]

    dh4, dzg, dpp, loss_p, dg_fin = _tile_call(
        "loss_head", head, T, D_MODEL, _pick(T, 256), D_MODEL,
        tiles=[(h4, 0), (tgt, 0), (gate_p, 0), (pp, 0)], consts=[g_fin], outs=[F32, BF16, BF16], parts=2)
    grads["norm_final"] = dg_fin

    grads["w_ple_gate"] = _mm_tn_rows("ple_dwg", n3, dzg)
    grads["w_ple_proj"] = _mm_tn_cols("ple_dwp", p, dpp)

    def dnorm(accs, tv, cv):
        dh, dg = _rms_bwd_vals(accs[0], tv[0], cv[0])
        dh = tv[1] + dh
        return [dh, 0.5 * dh, dg]

    dh3, df3, grads["norm_ple"] = _tile_call(
        "ple_dnorm", dnorm, T, D_MODEL, _pick(T, 512), D_MODEL, pairs=[(dzg, 0, w["w_ple_gate"], "nt")],
        tiles=[(h3, 0), (dh4, 0)], consts=[g_ple], outs=[F32, BF16], parts=1)

    def swap_late(dw_in, dw_out):
        grads["w_ffn2_in"], grads["w_ffn2_out"] = dw_in, dw_out
        return swap(SCATTER_LATE)

    dh2, dh2b, grads["norm_ffn2"], _, _, _, from_sib = _ffn_bwd(
        "ffn2b", dh3, df3, h2, g_ffn2, w["w_ffn2_in"], w["w_ffn2_out"], sv2, comm_last=swap_late)
    scatter_late = after_swap(SCATTER_LATE, from_sib)

    grads["w_out"] = _mm_tn_rows("mix_dwout", merged, dh2b)
    tn = 256

    def dmerge(accs, tv, cv):
        dm = accs[0]
        sa, sb = _sigmoid(tv[0]), _sigmoid(tv[1])
        yav, ybv = tv[2].astype(F32), tv[3].astype(F32)
        return [dm * sa, dm * sb, dm * yav * sa * (1.0 - sa), dm * ybv * sb * (1.0 - sb)]

    dya, dyb, dga, dgb = _tile_call(
        "mix_dmerge", dmerge, T, D_MODEL, _pick(T, 1024), tn, pairs=[(dh2b, 0, w["w_out"], "nt")],
        tiles=[(proj, COL_GA * 128 // tn), (proj, COL_GB * 128 // tn), (ya, 0), (yb, 0)], outs=[BF16] * 4)
    grads["w_att_proj"] = _mm_tn_cols("mix_dwatt", att, dya)
    grads["w_rec_proj"] = _mm_tn_cols("mix_dwrec", recn, dyb)

    datt = _tile_call("mix_datt", ident, T, 512, _pick(T, 1024), 512, pairs=[(dya, 0, w["w_att_proj"], "nt")],
                      outs=[BF16])[0]
    drec, drg, grads["rec_norm"] = _rec_gate_bwd(dyb, w["w_rec_proj"], rec, proj, w["rec_norm"])

    drq, drf, dri, dlb, *got = _hgrn_bwd(proj, w["lb_param"], states, drec, B, S, comm=scatter_late)
    scattered(SCATTER_LATE, got)
    grads["lb_param"] = dlb
    daq, dak, dav, dbias, dsink = _swa_bwd(proj, kk2, vv2, bias, sinks, datt, B, S)
    grads["attn_sinks"] = dsink
    grads["rel_bias"] = _small_mm("t5_dbias", dbias.reshape(N_Q_HEADS, -1), onehot.T.astype(BF16), "right")
    dproj = jnp.concatenate([daq, dak, dav, drq, drf, dri, drg, dga, dgb], axis=1)
    tk = _pick(T, 2048, 128)
    w_in_shard = IN_W // N_CHIPS
    half_d = D_MODEL // 2
    gw32, gw16 = _mm_tn("mix_dwin", (2, 2, T // tk),
                        (u, (tk, half_d), lambda i, j, k: (k, i)), (dproj, (tk, IN_W // 2), lambda i, j, k: (k, j)),
                        _grad_pair((D_MODEL, IN_W), (half_d, IN_W // 2), lambda i, j, k: (i, j)))
    to_sh = lambda t: t.reshape(D_MODEL, N_CHIPS, w_in_shard).transpose(1, 0, 2)
    grads["w_in"] = (to_sh(gw32), to_sh(gw16))

    def dnorm_mix(accs, tv, cv):
        dh, dg = _rms_bwd_vals(accs[0], tv[0], cv[0])
        dh = tv[1] + dh
        return [dh, 0.5 * dh, dg]

    dh1, df1, grads["norm_mix"], *from_sib = _tile_call(
        "mix_dnorm", dnorm_mix, T, D_MODEL, _pick(T, 512), D_MODEL, pairs=[(dproj, 0, w["w_in"], "nt")],
        tiles=[(h1, 0), (dh2, 0)], consts=[g_mix], outs=[F32, BF16], parts=1, comm=swap(SCATTER_MIX))
    scatter_mix = after_swap(SCATTER_MIX, from_sib)

    def scatter_last(dw_in, dw_out):
        grads["w_ffn1_in"], grads["w_ffn1_out"] = dw_in, dw_out
        return after_swap(SCATTER_LAST, _run_comm("rs_sibling_last", swap(SCATTER_LAST)))

    dx, _, grads["norm_ffn1"], _, _, got, got_last = _ffn_bwd(
        "ffn1b", dh1, df1, x, g_ffn1, w["w_ffn1_in"], w["w_ffn1_out"], sv1, comm=scatter_mix, comm_last=scatter_last)
    scattered(SCATTER_MIX, got)
    scattered(SCATTER_LAST, got_last)
    return loss_p, dx, grads, part, from_chips


def _place():
    x, y, c = lax.axis_index("x"), lax.axis_index("y"), lax.axis_index("c")
    return x, y, c


def _other_chips(x, y):
    return [(1 - x, y, 2 * (1 - x) + y), (x, 1 - y, 2 * x + 1 - y), (1 - x, 1 - y, 2 * (1 - x) + 1 - y)]


def _half_rows(ref_3d, chip, h, rows):
    return ref_3d.at[chip, pl.ds(h * rows, rows), :]


def _gather_comm(ws):
    nw = len(ws)

    def parts(w_refs, out_refs, send_sems, recv_sems):
        x, y, c = _place()
        me = 2 * x + y
        chips = _other_chips(x, y)

        def copy(i, k, chip, h, to, src=None):
            half = ws[i].shape[0] // 2
            dst = _half_rows(out_refs[i], chip, h, half)
            return pltpu.make_async_remote_copy(
                src_ref=dst if src is None else src, dst_ref=dst,
                send_sem=send_sems.at[6 * i + k], recv_sem=recv_sems.at[6 * i + k], device_id=to, device_id_type=MESH)

        def first():
            out = []
            for i in range(nw):
                half = ws[i].shape[0] // 2
                out += [copy(i, j, me, c, (cx, cy, c), src=w_refs[i].at[pl.ds(c * half, half), :])
                        for j, (cx, cy, _) in enumerate(chips)]
            return out

        return copy, first, chips, c, (x, y, 1 - c)

    def start(*refs):
        _, first, _, _, _ = parts(*refs)
        for cp in first():
            cp.start()

    def finish(*refs):
        copy, first, chips, c, sibling = parts(*refs)
        passed = []
        for i in range(nw):
            for j, (cx, cy, ci) in enumerate(chips):
                copy(i, j, ci, c, (cx, cy, c)).wait_recv()
                fw = copy(i, 3 + j, ci, c, sibling)
                fw.start()
                passed.append(fw)
        for i in range(nw):
            for j, (_, _, ci) in enumerate(chips):
                copy(i, 3 + j, ci, 1 - c, sibling).wait_recv()
        for cp in first() + passed:
            cp.wait_send()

    return _Comm(list(ws), [jax.ShapeDtypeStruct((N_CHIPS,) + w.shape, w.dtype) for w in ws], 6 * nw, start, finish)


def _scatter_comm(ps):
    nw = len(ps)

    def copies(p_refs, out_refs, send_sems, recv_sems):
        x, y, c = _place()
        cps = []
        for i in range(nw):
            for j, (cx, cy, ci) in enumerate(_other_chips(x, y)):
                cps.append(pltpu.make_async_remote_copy(
                    src_ref=p_refs[i].at[ci], dst_ref=out_refs[i].at[j], send_sem=send_sems.at[3 * i + j],
                    recv_sem=recv_sems.at[3 * i + j], device_id=(cx, cy, c), device_id_type=MESH))
        return cps

    def start(*refs):
        for cp in copies(*refs):
            cp.start()

    def finish(*refs):
        for cp in copies(*refs):
            cp.wait()

    return _Comm(list(ps), [jax.ShapeDtypeStruct((3,) + p.shape[1:], p.dtype) for p in ps], 3 * nw, start, finish)


def _swap_comm(gs):
    nw = len(gs)

    def copies(g_refs, out_refs, send_sems, recv_sems):
        x, y, c = _place()
        cps = []
        for i in range(nw):
            half = gs[i].shape[1] // 2
            cps.append(pltpu.make_async_remote_copy(
                src_ref=g_refs[i].at[:, pl.ds((1 - c) * half, half), :], dst_ref=out_refs[i],
                send_sem=send_sems.at[i], recv_sem=recv_sems.at[i], device_id=(x, y, 1 - c), device_id_type=MESH))
        return cps

    def start(*refs):
        for cp in copies(*refs):
            cp.start()

    def finish(*refs):
        for cp in copies(*refs):
            cp.wait()

    return _Comm(list(gs), [jax.ShapeDtypeStruct((N_CHIPS, g.shape[1] // 2, g.shape[2]), g.dtype) for g in gs],
                 nw, start, finish)


def _run_comm(name, comm):
    nci, nco = len(comm.ins), len(comm.out_shapes)

    def body(*refs):
        cin, cout, send_sems, recv_sems = refs[:nci], refs[nci:nci + nco], refs[-2], refs[-1]
        comm.start(cin, cout, send_sems, recv_sems)
        comm.finish(cin, cout, send_sems, recv_sems)

    return pl.pallas_call(
        body, name=name, in_specs=[ANY] * nci, out_specs=[ANY] * nco, out_shape=list(comm.out_shapes),
        scratch_shapes=[pltpu.SemaphoreType.DMA((comm.n_sems,)), pltpu.SemaphoreType.DMA((comm.n_sems,))],
    )(*comm.ins)


def _join_halves(name, ss):
    nw = len(ss)

    def body(*refs):
        s_refs, out_refs, send_sems, recv_sems = refs[:nw], refs[nw:2 * nw], refs[2 * nw], refs[2 * nw + 1]
        x, y, c = _place()
        cps = [pltpu.make_async_remote_copy(
            src_ref=s_refs[i], dst_ref=out_refs[i], send_sem=send_sems.at[i], recv_sem=recv_sems.at[i],
            device_id=(x, y, 1 - c), device_id_type=MESH) for i in range(nw)]
        for cp in cps:
            cp.start()
        for cp in cps:
            cp.wait()

    return pl.pallas_call(
        body, name=name, in_specs=[ANY] * nw, out_specs=[ANY] * nw,
        out_shape=[jax.ShapeDtypeStruct(s.shape, s.dtype) for s in ss],
        scratch_shapes=[pltpu.SemaphoreType.DMA((nw,)), pltpu.SemaphoreType.DMA((nw,))],
    )(*ss)


def _allreduce_small(sp):
    def body(s_ref, out_ref, slots, send_sems, recv_sems):
        x, y, c = _place()
        me = 4 * x + 2 * y + c
        slots[me] = s_ref[...]
        cps = []
        for r in range(1, N_DEV):
            px, py, pc = x ^ (r >> 2), y ^ ((r >> 1) & 1), c ^ (r & 1)
            cps.append(pltpu.make_async_remote_copy(
                src_ref=s_ref, dst_ref=slots.at[me], send_sem=send_sems.at[r - 1], recv_sem=recv_sems.at[r - 1],
                device_id=(px, py, pc), device_id_type=MESH))
        for cp in cps:
            cp.start()
        for r in range(1, N_DEV):
            px, py, pc = x ^ (r >> 2), y ^ ((r >> 1) & 1), c ^ (r & 1)
            pltpu.make_async_remote_copy(
                src_ref=s_ref, dst_ref=slots.at[4 * px + 2 * py + pc], send_sem=send_sems.at[r - 1],
                recv_sem=recv_sems.at[r - 1], device_id=(px, py, pc), device_id_type=MESH).wait_recv()
        for cp in cps:
            cp.wait_send()
        acc = slots[0]
        for d in range(1, N_DEV):
            acc = acc + slots[d]
        out_ref[...] = acc

    return pl.pallas_call(
        body, name="allreduce_small",
        in_specs=[pl.BlockSpec(memory_space=pltpu.VMEM)], out_specs=pl.BlockSpec(memory_space=pltpu.VMEM),
        out_shape=jax.ShapeDtypeStruct(sp.shape, F32),
        scratch_shapes=[pltpu.VMEM((N_DEV,) + sp.shape, F32), pltpu.SemaphoreType.DMA((N_DEV - 1,)),
                        pltpu.SemaphoreType.DMA((N_DEV - 1,))],
    )(sp)


def _scalar(v):
    return jnp.reshape(v, (1,)).astype(jnp.int32)


def _row_tile(h, dtype_mult=16):
    return _pick(h, 256, dtype_mult)


def _add_sibling(name, g32, from_sib, c):
    _, r, n = g32.shape
    h = r // 2
    th = _row_tile(h)
    nt = h // th

    def body(c_ref, g_ref, s_ref, o32_ref, o16_ref):
        s = g_ref[...] + s_ref[...].astype(F32)
        o32_ref[...] = s
        o16_ref[...] = s.astype(BF16)

    blk = (None, th, n)
    return pl.pallas_call(
        body, name=name,
        grid_spec=pltpu.PrefetchScalarGridSpec(
            num_scalar_prefetch=1, grid=(N_CHIPS, nt),
            in_specs=[pl.BlockSpec(blk, lambda k, t, c_ref: (k, c_ref[0] * nt + t, 0)),
                      pl.BlockSpec(blk, lambda k, t, c_ref: (k, t, 0))],
            out_specs=[pl.BlockSpec(blk, lambda k, t, c_ref: (k, t, 0))] * 2),
        out_shape=[jax.ShapeDtypeStruct((N_CHIPS, h, n), F32), jax.ShapeDtypeStruct((N_CHIPS, h, n), BF16)],
        compiler_params=_params("arbitrary", "arbitrary"))(_scalar(c), g32, from_sib)


def _add_chips(name, p32, from_chips, me_chip):
    _, h, n = p32.shape
    th = _row_tile(h)

    def body(m_ref, p_ref, a_ref, b_ref, c_ref, o_ref):
        o_ref[...] = p_ref[...] + a_ref[...].astype(F32) + b_ref[...].astype(F32) + c_ref[...].astype(F32)

    blk = (None, th, n)
    return pl.pallas_call(
        body, name=name,
        grid_spec=pltpu.PrefetchScalarGridSpec(
            num_scalar_prefetch=1, grid=(h // th,),
            in_specs=[pl.BlockSpec(blk, lambda t, m_ref: (m_ref[0], t, 0))]
            + [pl.BlockSpec(blk, lambda t, m_ref, j=j: (j, t, 0)) for j in range(3)],
            out_specs=pl.BlockSpec((th, n), lambda t, m_ref: (t, 0))),
        out_shape=jax.ShapeDtypeStruct((h, n), F32),
        compiler_params=_params("arbitrary"))(_scalar(me_chip), p32, from_chips, from_chips, from_chips)


def _adamw_vals(w, g, m, v):
    m = ADAM_B1 * m + (1.0 - ADAM_B1) * g
    v = ADAM_B2 * v + (1.0 - ADAM_B2) * (g * g)
    m_hat = m / (1.0 - ADAM_B1 ** ADAM_STEP)
    v_hat = v / (1.0 - ADAM_B2 ** ADAM_STEP)
    delta = -ADAM_LR * (m_hat / (jnp.sqrt(v_hat) + ADAM_EPS) + ADAM_WD * w)
    return delta, m, v


def _adamw_halves(name, w, m, v, g_mine, g_sib, c):
    r, n = w.shape
    h = r // 2
    th = _row_tile(h, 8)
    nt = h // th

    def body(c_ref, w_ref, m_ref, v_ref, a_ref, b_ref, g_ref, d_ref, nm_ref, nv_ref):
        mine = (pl.program_id(0) // nt) == c_ref[0]
        g = jnp.where(mine, a_ref[...], b_ref[...])
        d, nm, nv = _adamw_vals(w_ref[...], g, m_ref[...], v_ref[...])
        g_ref[...] = g
        d_ref[...] = d
        nm_ref[...] = nm
        nv_ref[...] = nv

    full = pl.BlockSpec((th, n), lambda t, c_ref: (t, 0))
    part = pl.BlockSpec((th, n), lambda t, c_ref: (t % nt, 0))
    return pl.pallas_call(
        body, name=name,
        grid_spec=pltpu.PrefetchScalarGridSpec(
            num_scalar_prefetch=1, grid=(2 * nt,), in_specs=[full, full, full, part, part], out_specs=[full] * 4),
        out_shape=[jax.ShapeDtypeStruct((r, n), F32)] * 4,
        compiler_params=_params("arbitrary"))(_scalar(c), w, m, v, g_mine, g_sib)


def _adamw(name, w, g, m, v):
    R, W = w.shape

    def fn(accs, tv, cv):
        return list(_adamw_vals(*tv))

    return _tile_call(name, fn, R, W, _pick(R, 256), W, tiles=[(w, 0), (g, 0), (m, 0), (v, 0)], outs=[F32] * 3)


SMALL_LAYOUT = (("rel_bias", 2, 256), ("lb_param", 8, 1024), ("norm_ffn1", 8, 1024), ("norm_mix", 8, 1024),
                ("attn_sinks", 1, 8), ("rec_norm", 1, 128), ("norm_ffn2", 8, 1024), ("norm_ple", 8, 1024),
                ("norm_final", 8, 1024), ("loss", 8, 1024))


def _pack_small(vals):
    rows = []
    for name, nrows, n in SMALL_LAYOUT:
        flat = vals[name].reshape(-1)
        flat = jnp.pad(flat, (0, nrows * 128 - n))
        rows.append(flat.reshape(nrows, 128))
    packed = jnp.concatenate(rows, axis=0)
    return jnp.pad(packed, ((0, SMALL_ROWS - packed.shape[0]), (0, 0)))


def _unpack_small(packed, shapes):
    out, r = {}, 0
    for name, nrows, n in SMALL_LAYOUT:
        out[name] = packed[r:r + nrows].reshape(-1)[:n].reshape(shapes[name])
        r += nrows
    return out


def _natural(name, s):
    if name in COL_SHARDED:
        return s.transpose(1, 0, 2).reshape(s.shape[1], -1)
    return s.reshape(-1, s.shape[2])


def kernel(x, p, rel_bias, lb_param, norm_ffn1, w_ffn1_in, w_ffn1_out, norm_mix, w_in, attn_sinks, rec_norm, w_att_proj, w_rec_proj, w_out, norm_ffn2, w_ffn2_in, w_ffn2_out, norm_ple, w_ple_gate, w_ple_proj, norm_final, loss_target, m_rel_bias, m_lb_param, m_norm_ffn1, m_w_ffn1_in, m_w_ffn1_out, m_norm_mix, m_w_in, m_attn_sinks, m_rec_norm, m_w_att_proj, m_w_rec_proj, m_w_out, m_norm_ffn2, m_w_ffn2_in, m_w_ffn2_out, m_norm_ple, m_w_ple_gate, m_w_ple_proj, m_norm_final, v_rel_bias, v_lb_param, v_norm_ffn1, v_w_ffn1_in, v_w_ffn1_out, v_norm_mix, v_w_in, v_attn_sinks, v_rec_norm, v_w_att_proj, v_w_rec_proj, v_w_out, v_norm_ffn2, v_w_ffn2_in, v_w_ffn2_out, v_norm_ple, v_w_ple_gate, v_w_ple_proj, v_norm_final):
    args = dict(locals())
    wsh = {n: args[n] for n in WEIGHTS}
    B, S = x.shape[0], x.shape[1]
    T = B * S
    cx, cy, cc = _place()
    me_chip = 2 * cx + cy

    mine16 = {n: wsh[n][0].astype(BF16) for n in BIG}
    loss_p, dx, grads, part, from_chips = _local_step(
        x.reshape(T, D_MODEL), p.reshape(T, PLE_DIM), loss_target.reshape(T, D_MODEL),
        {n: wsh[n] for n in SMALL}, mine16, cc, me_chip, B, S)

    s_mine = [_add_chips("rs_add_chips_" + n, part[n][0], from_chips[n], me_chip) for n in BIG]
    s_sib = _join_halves("rs_join", s_mine)

    small_vals = {
        "rel_bias": grads["rel_bias"].T,
        "lb_param": jnp.concatenate([_colsum("dlb_sum", grads["lb_param"]),
                                     -_colsum("dlb_sum2", grads["lb_param"])], axis=0) / 8.0,
        "attn_sinks": grads["attn_sinks"][:, 0],
        "rec_norm": _colsum("drn_sum", grads["rec_norm"]).reshape(REC_HEADS, REC_DIM).sum(axis=0),
        "loss": _colsum("loss_sum", loss_p),
    }
    for n in ("norm_ffn1", "norm_mix", "norm_ffn2", "norm_ple", "norm_final"):
        small_vals[n] = _colsum(n + "_sum", grads[n])
    red = _allreduce_small(_pack_small(small_vals))
    small_shapes = {n: wsh[n].shape for n in SMALL}
    small_shapes["loss"] = (D_MODEL,)
    small = _unpack_small(red, small_shapes)
    loss = 0.5 * jnp.sum(small["loss"]) / D_MODEL

    out_g, out_d, out_m, out_v = {}, {}, {}, {}
    for n, gm, gs in zip(BIG, s_mine, s_sib):
        res = _adamw_halves("adamw_" + n, wsh[n][0], args["m_" + n][0], args["v_" + n][0], gm, gs, cc)
        out_g[n], out_d[n], out_m[n], out_v[n] = (t[None] for t in res)
    sw = _pack_small({**{n: wsh[n] for n in SMALL}, "loss": jnp.zeros((D_MODEL,), F32)})
    sm = _pack_small({**{n: args["m_" + n] for n in SMALL}, "loss": jnp.zeros((D_MODEL,), F32)})
    sv = _pack_small({**{n: args["v_" + n] for n in SMALL}, "loss": jnp.ones((D_MODEL,), F32)})
    sd, snm, snv = _adamw("adamw_small", sw, red, sm, sv)
    ud, um, uv = (_unpack_small(t, small_shapes) for t in (sd, snm, snv))
    for n in SMALL:
        out_g[n], out_d[n], out_m[n], out_v[n] = small[n], ud[n], um[n], uv[n]

    return (loss, dx.reshape(B, S, D_MODEL), *[out_g[n] for n in WEIGHTS], *[out_d[n] for n in WEIGHTS],
            *[out_m[n] for n in WEIGHTS], *[out_v[n] for n in WEIGHTS])
```

```python
import numpy as np
import jax
import jax.numpy as jnp
from jax import lax
from jax.experimental import pallas as pl
from jax.experimental.pallas import tpu as pltpu

F32 = jnp.float32
BF16 = jnp.bfloat16
MESH = pl.DeviceIdType.MESH

D_MODEL = 1024
D_FF = 2816
FF_SHARD = 2 * D_FF // 4
HEAD_DIM = 64
N_Q_HEADS = 8
ATT_BLOCK = 128
N_BUCKETS = 32
MAX_DISTANCE = 128
REC_HEADS = 4
REC_DIM = 128
PLE_DIM = 256
EPS = 1e-6
IN_W = 4864
COL_AQ, COL_AK, COL_AV, COL_RQ, COL_RF, COL_RI, COL_RG, COL_GA, COL_GB = 0, 4, 5, 6, 10, 14, 18, 22, 30

CHUNK = 64
SUB = 8
N_SUB = CHUNK // SUB
HGRN_PAIR = 2

ADAM_LR, ADAM_B1, ADAM_B2, ADAM_EPS, ADAM_WD, ADAM_STEP = 0.001, 0.9, 0.999, 1e-08, 0.01, 10

V7X_VMEM_LIMIT = 56 * 1024 * 1024
N_CHIPS = 4
N_DEV = 8

BIG = ("w_ffn1_in", "w_ffn1_out", "w_in", "w_att_proj", "w_rec_proj", "w_out",
       "w_ffn2_in", "w_ffn2_out", "w_ple_gate", "w_ple_proj")
COL_SHARDED = ("w_ffn1_in", "w_in", "w_att_proj", "w_rec_proj", "w_ffn2_in", "w_ple_proj")
WEIGHTS = ("rel_bias", "lb_param", "norm_ffn1", "w_ffn1_in", "w_ffn1_out", "norm_mix", "w_in", "attn_sinks",
           "rec_norm", "w_att_proj", "w_rec_proj", "w_out", "norm_ffn2", "w_ffn2_in", "w_ffn2_out", "norm_ple",
           "w_ple_gate", "w_ple_proj", "norm_final")
SMALL = tuple(n for n in WEIGHTS if n not in BIG)
SMALL_ROWS = 64


def _params(*sem):
    return pltpu.CompilerParams(dimension_semantics=sem, vmem_limit_bytes=V7X_VMEM_LIMIT)


def _pick(n, cap, mult=8):
    if n <= cap:
        return n
    for t in range(cap - cap % mult, 0, -mult):
        if n % t == 0:
            return t
    raise ValueError((n, cap, mult))


def _dot(a, b):
    return jnp.dot(a, b, preferred_element_type=F32)


def _dot_nt(a, b):
    return lax.dot_general(a, b, (((1,), (1,)), ((), ())), preferred_element_type=F32)


def _dot_tn(a, b):
    return lax.dot_general(a, b, (((0,), (0,)), ((), ())), preferred_element_type=F32)


def _split3(x):
    hi = x.astype(BF16)
    r = x - hi.astype(F32)
    mid = r.astype(BF16)
    lo = (r - mid.astype(F32)).astype(BF16)
    return hi, mid, lo


def _split2(x):
    hi = x.astype(BF16)
    return hi, (x - hi.astype(F32)).astype(BF16)


def _sel_left(sel_bf16, x):
    hi, mid, lo = _split3(x)
    return _dot(sel_bf16, hi) + _dot(sel_bf16, mid) + _dot(sel_bf16, lo)


def _sel_right(x, sel_bf16):
    hi, mid, lo = _split3(x)
    return _dot(hi, sel_bf16) + _dot(mid, sel_bf16) + _dot(lo, sel_bf16)


def _sigmoid(x):
    return 0.5 * jnp.tanh(0.5 * x) + 0.5


def _group8(x):
    r, w = x.shape
    return x.reshape(r // 8, 8, w).sum(axis=0)


class _Comm:
    def __init__(self, ins, out_shapes, n_sems, start, finish):
        self.ins, self.out_shapes, self.n_sems, self.start, self.finish = ins, out_shapes, n_sems, start, finish


ANY = pl.BlockSpec(memory_space=pl.ANY)


def _comm_parts(comm):
    if comm is None:
        return [], [], [], []
    sems = [pltpu.SemaphoreType.DMA((comm.n_sems,)), pltpu.SemaphoreType.DMA((comm.n_sems,))]
    return list(comm.ins), [ANY] * len(comm.ins), list(comm.out_shapes), sems


def _comm_run(comm, grid, refs, n_in, n_out):
    if comm is None:
        return (lambda: None), (lambda: None)
    nci, nco = len(comm.ins), len(comm.out_shapes)
    cin = refs[n_in:n_in + nci]
    cout = refs[n_in + nci + n_out:n_in + nci + n_out + nco]
    send_sems, recv_sems = refs[-2], refs[-1]
    ids = [pl.program_id(d) for d in range(len(grid))]
    is_first = ids[0] == 0
    is_last = ids[0] == grid[0] - 1
    for d in range(1, len(grid)):
        is_first = is_first & (ids[d] == 0)
        is_last = is_last & (ids[d] == grid[d] - 1)

    def first():
        @pl.when(is_first)
        def _():
            comm.start(cin, cout, send_sems, recv_sems)

    def last():
        @pl.when(is_last)
        def _():
            comm.finish(cin, cout, send_sems, recv_sems)

    return first, last


def _call(name, fn, grid, ins, outs, pairs=(), comm=None, j_outer=False):
    in_pair = {i for p in pairs for i in p[:2]}
    n_in, n_out = len(ins), len(outs)
    c_arrays, c_in_specs, c_out_shapes, c_sems = _comm_parts(comm)

    def body(*refs):
        first, last = _comm_run(comm, grid, refs, n_in, n_out)
        first()
        accs = []
        for ia, ib, kind in pairs:
            a, b = refs[ia][...].astype(BF16), refs[ib][...].astype(BF16)
            accs.append(_dot(a, b) if kind == "nn" else _dot_nt(a, b))
        vals = [refs[i][...] for i in range(n_in) if i not in in_pair]
        res = fn(accs, vals)
        out_refs = refs[n_in + len(c_arrays):n_in + len(c_arrays) + n_out]
        assert len(res) == len(out_refs), (name, len(res), len(out_refs))
        for o_ref, val in zip(out_refs, res):
            o_ref[...] = val.astype(o_ref.dtype)
        last()

    if j_outer:
        grid = (grid[1], grid[0])
        swap = lambda im: (lambda j, i: im(i, j))
        ins = [(a, blk, swap(im)) for a, blk, im in ins]
        outs = [(shp, dt, blk, swap(im)) for shp, dt, blk, im in outs]

    return pl.pallas_call(
        body, name=name, grid=grid,
        in_specs=[pl.BlockSpec(blk, im) for _, blk, im in ins] + c_in_specs,
        out_specs=[pl.BlockSpec(blk, im) for _, _, blk, im in outs] + [ANY] * len(c_out_shapes),
        out_shape=[jax.ShapeDtypeStruct(shp, dt) for shp, dt, _, _ in outs] + c_out_shapes,
        scratch_shapes=c_sems,
        compiler_params=_params(*(["arbitrary"] * len(grid))))(*[a for a, _, _ in ins], *c_arrays)


def _tile_call(name, fn, M, N, tm, tn, *, pairs=(), tiles=(), consts=(), outs=(), parts=0, comm=None,
               j_outer=False):
    gi, gj = M // tm, N // tn
    assert gi * tm == M and gj * tn == N, (name, M, N, tm, tn)
    ins, prs = [], []
    for a, a_col, b, kind in pairs:
        K = b.shape[0] if kind == "nn" else b.shape[1]
        ins.append((a, (tm, K), lambda i, j, c=a_col: (i, c)))
        if kind == "nn":
            ins.append((b, (K, tn), lambda i, j: (0, j)))
        else:
            ins.append((b, (tn, K), lambda i, j: (j, 0)))
        prs.append((len(ins) - 2, len(ins) - 1, kind))
    for arr, off in tiles:
        ins.append((arr, (tm, tn), lambda i, j, o=off: (i, j + o)))
    for arr in consts:
        ins.append((arr, arr.shape, lambda i, j: (0, 0)))
    out_l = [((M, N), dt, (tm, tn), lambda i, j: (i, j)) for dt in outs]
    out_l += [((gi * 8, N), F32, (8, tn), lambda i, j: (i, j))] * parts
    nt = len(tiles)

    def wrapped(accs, vals):
        return fn(accs, vals[:nt], vals[nt:])

    return _call(name, wrapped, (gi, gj), ins, out_l, prs, comm=comm, j_outer=j_outer)


def _mm_tn(name, grid, a_in, b_in, outs):
    nk = grid[2]
    tm = [d for d in a_in[1] if d is not None][1]
    tn = [d for d in b_in[1] if d is not None][1]

    def body(a_ref, b_ref, *rest):
        out_refs, acc_ref = rest[:-1], rest[-1]
        k = pl.program_id(2)

        @pl.when(k == 0)
        def _():
            acc_ref[...] = jnp.zeros_like(acc_ref)

        acc_ref[...] += _dot_tn(a_ref[...].astype(BF16), b_ref[...].astype(BF16))

        @pl.when(k == nk - 1)
        def _():
            for o_ref in out_refs:
                if len(o_ref.shape) == 3:
                    n = o_ref.shape[2]
                    for s in range(o_ref.shape[0]):
                        o_ref[s] = acc_ref[:, n * s:n * (s + 1)].astype(o_ref.dtype)
                else:
                    o_ref[...] = acc_ref[...].astype(o_ref.dtype)

    return pl.pallas_call(
        body, name=name, grid=grid,
        in_specs=[pl.BlockSpec(a_in[1], a_in[2]), pl.BlockSpec(b_in[1], b_in[2])],
        out_specs=[pl.BlockSpec(blk, im) for _, _, blk, im in outs],
        out_shape=[jax.ShapeDtypeStruct(shp, dt) for shp, dt, _, _ in outs],
        scratch_shapes=[pltpu.VMEM((tm, tn), F32)],
        compiler_params=_params("arbitrary", "arbitrary", "arbitrary"))(a_in[0], b_in[0])


def _grad_pair(shape, block, imap):
    return [(shape, F32, block, imap), (shape, BF16, block, imap)]


def _mm_tn_rows(name, a, b, tk=2048):
    T, a_w = a.shape
    b_w = b.shape[1]
    tm = _pick(a_w, 1408, 128)
    tk = _pick(T, tk, 128)
    g32, g16 = _mm_tn(name, (a_w // tm, 1, T // tk),
                      (a, (tk, tm), lambda i, j, k: (k, i)), (b, (tk, b_w), lambda i, j, k: (k, 0)),
                      _grad_pair((a_w, b_w), (tm, b_w), lambda i, j, k: (i, 0)))
    shp = (N_CHIPS, a_w // N_CHIPS, b_w)
    return g32.reshape(shp), g16.reshape(shp)


def _mm_tn_cols(name, a, b, tk=4096):
    T, a_w = a.shape
    b_w = b.shape[1]
    tk = _pick(T, tk, 128)
    shp = (N_CHIPS, a_w, b_w // N_CHIPS)
    return _mm_tn(name, (1, 1, T // tk),
                  (a, (tk, a_w), lambda i, j, k: (k, 0)), (b, (tk, b_w), lambda i, j, k: (k, 0)),
                  _grad_pair(shp, shp, lambda i, j, k: (0, 0, 0)))


def _colsum(name, x):
    def body(x_ref, o_ref):
        o_ref[...] = jnp.sum(x_ref[...], axis=0, keepdims=True)
    return pl.pallas_call(body, name=name, out_shape=jax.ShapeDtypeStruct((1, x.shape[1]), F32))(x)


def _rms_hat(h):
    return h * lax.rsqrt(jnp.mean(h * h, axis=-1, keepdims=True) + EPS)


def _rms_bwd_vals(dn, h, g):
    r = lax.rsqrt(jnp.mean(h * h, axis=-1, keepdims=True) + EPS)
    nh = h * r
    gd = dn * g
    dh = r * (gd - nh * jnp.mean(gd * nh, axis=-1, keepdims=True))
    return dh, _group8(dn * nh)


def _rms_fwd(name, h, g, tm=512, comm=None):
    T = h.shape[0]

    def fn(accs, tv, cv):
        return [_rms_hat(tv[0]) * cv[0]]

    return _tile_call(name, fn, T, D_MODEL, _pick(T, tm), D_MODEL, tiles=[(h, 0)], consts=[g], outs=[BF16],
                      comm=comm)


def _ffn_fwd(tag, h, g, w_in, w_out, g_next, n=None, comm_norm=None, w_in_of=None, comm_in=None, comm_out=None,
             w_out_of=None):
    T = h.shape[0]
    if n is None:
        n, *got_norm = _rms_fwd(tag + "_norm", h, g, comm=comm_norm)
        if w_in_of is not None:
            w_in = w_in_of(got_norm)
    tm = _pick(T, 1024)
    wblk = (None, D_MODEL, FF_SHARD)

    def act(accs, vals):
        gate, up = accs
        return [gate, up, gate * _sigmoid(gate) * up]

    tile = lambda: ((T, D_FF), BF16, (tm, FF_SHARD), lambda i, j: (i, j))
    gate, up, a, *got_in = _call(
        tag + "_in", act, (T // tm, 2),
        [(n, (tm, D_MODEL), lambda i, j: (i, 0)),
         (w_in, wblk, lambda i, j: (j, 0, 0)), (w_in, wblk, lambda i, j: (j + 2, 0, 0))],
        [tile(), tile(), tile()], pairs=[(0, 1, "nn"), (0, 2, "nn")], comm=comm_in, j_outer=True)

    def res(accs, tv, cv):
        h_new = tv[0] + 0.5 * accs[0]
        return [h_new, _rms_hat(h_new) * cv[0]]

    if w_out_of is not None:
        w_out = w_out_of(got_in)
    h_new, n_next, *got_out = _tile_call(
        tag + "_out", res, T, D_MODEL, _pick(T, 512), D_MODEL, pairs=[(a, 0, w_out, "nn")], tiles=[(h, 0)],
        consts=[g_next], outs=[F32, BF16], comm=comm_out)
    return h_new, n_next, (n, gate, up, a), got_out


def _ffn_bwd(tag, dh_out, df, h, g, w_in, w_out, saved, comm=None, comm_last=None):
    T = h.shape[0]
    n, gate, up, a = saved
    tm = _pick(T, 512)

    def dact(accs, vals):
        da = accs[0]
        gt, u = vals[0].astype(F32), vals[1].astype(F32)
        sg = _sigmoid(gt)
        silu = gt * sg
        return [jnp.stack([(da * u * (sg + silu * (1.0 - sg))).astype(BF16), (da * silu).astype(BF16)])]

    dz, *got = _call(
        tag + "_dact", dact, (T // tm, 2),
        [(df, (tm, D_MODEL), lambda i, j: (i, 0)), (w_out, (FF_SHARD, D_MODEL), lambda i, j: (j, 0)),
         (gate, (tm, FF_SHARD), lambda i, j: (i, j)), (up, (tm, FF_SHARD), lambda i, j: (i, j))],
        [((2, T, D_FF), BF16, (2, tm, FF_SHARD), lambda i, j: (0, i, j))], pairs=[(0, 1, "nt")], comm=comm,
        j_outer=True)
    dw_out = _mm_tn_rows(tag + "_dwout", a, df)
    tk = _pick(T, 2048, 128)
    dw_in = _mm_tn(tag + "_dwin", (1, N_CHIPS, T // tk),
                   (n, (tk, D_MODEL), lambda i, j, k: (k, 0)),
                   (dz, (None, tk, FF_SHARD), lambda i, j, k: (j // 2, k, j % 2)),
                   _grad_pair((N_CHIPS, D_MODEL, FF_SHARD), (None, D_MODEL, FF_SHARD), lambda i, j, k: (j, 0, 0)))

    def dnorm(accs, vals):
        dn = accs[0] + accs[1] + accs[2] + accs[3]
        dh, dg = _rms_bwd_vals(dn, vals[0], vals[2])
        dh = vals[1] + dh
        return [dh, dh, dg]

    tm2 = _pick(T, 512)
    ins = [(dz, (None, tm2, FF_SHARD), lambda i, j, s=s: (s // 2, i, s % 2)) for s in range(N_CHIPS)]
    ins += [(w_in, (None, D_MODEL, FF_SHARD), lambda i, j, s=s: (s, 0, 0)) for s in range(N_CHIPS)]
    ins += [(h, (tm2, D_MODEL), lambda i, j: (i, 0)), (dh_out, (tm2, D_MODEL), lambda i, j: (i, 0)),
            (g, g.shape, lambda i, j: (0, 0))]
    dh, dh16, dg, *got_last = _call(
        tag + "_dnorm", dnorm, (T // tm2, 1), ins,
        [((T, D_MODEL), F32, (tm2, D_MODEL), lambda i, j: (i, 0)),
         ((T, D_MODEL), BF16, (tm2, D_MODEL), lambda i, j: (i, 0)),
         ((T // tm2 * 8, D_MODEL), F32, (8, D_MODEL), lambda i, j: (i, 0))],
        pairs=[(s, N_CHIPS + s, "nt") for s in range(N_CHIPS)],
        comm=None if comm_last is None else comm_last(dw_in, dw_out))
    return dh, dh16, dg, dw_in, dw_out, got, got_last


def _t5_onehot():
    qi = np.arange(ATT_BLOCK)[:, None] + ATT_BLOCK
    kj = np.arange(2 * ATT_BLOCK)[None, :]
    nn = np.maximum(qi - kj, 0)
    max_exact = N_BUCKETS // 2
    large = max_exact + (np.log(np.maximum(nn, 1) / max_exact) / np.log(MAX_DISTANCE / max_exact)
                         * (N_BUCKETS - max_exact)).astype(np.int32)
    large = np.minimum(large, N_BUCKETS - 1)
    bucket = np.where(nn < max_exact, nn, large).astype(np.int32).reshape(-1)
    return (bucket[None, :] == np.arange(N_BUCKETS)[:, None]).astype(np.float32)


def _small_mm(name, a, b, sel):
    def body(a_ref, b_ref, o_ref):
        if sel == "right":
            o_ref[...] = _sel_right(a_ref[...], b_ref[...])
        else:
            o_ref[...] = _sel_left(a_ref[...], b_ref[...])
    return pl.pallas_call(body, name=name, out_shape=jax.ShapeDtypeStruct((a.shape[0], b.shape[1]), F32),
                          compiler_params=pltpu.CompilerParams(vmem_limit_bytes=V7X_VMEM_LIMIT))(a, b)


def _dup_heads(t):
    a, b = t[:, :HEAD_DIM], t[:, HEAD_DIM:]
    return jnp.concatenate([a, a, b, b], axis=1)


def _kv_layouts(proj):
    T = proj.shape[0]

    def fn(accs, tv, cv):
        return [tv[0], tv[1]]

    k, v = _tile_call("kv_cast", fn, T, 128, _pick(T, 1024), 128, tiles=[(proj, COL_AK), (proj, COL_AV)],
                      outs=[BF16, BF16])
    return _dup_heads(k), _dup_heads(v)


def _swa_masks():
    row = lax.broadcasted_iota(jnp.int32, (ATT_BLOCK, 2 * ATT_BLOCK), 0)
    col = lax.broadcasted_iota(jnp.int32, (ATT_BLOCK, 2 * ATT_BLOCK), 1)
    dist = ATT_BLOCK + row - col
    return (dist >= 0) & (dist < ATT_BLOCK), col


GROUP = 4


def _stack_group(blk, lo_q):
    zero = jnp.zeros_like(blk[:, :128])
    rows = []
    for pair in range(GROUP // 2):
        pb = blk[:, 128 * pair:128 * (pair + 1)]
        rows += [jnp.where(lo_q, pb, zero), jnp.where(lo_q, zero, pb)]
    return jnp.concatenate(rows, axis=0)


def _unstack_group(st, lo_q):
    pairs = [jnp.where(lo_q, st[256 * pair:256 * pair + 128], st[256 * pair + 128:256 * (pair + 1)])
             for pair in range(GROUP // 2)]
    return jnp.concatenate(pairs, axis=1)


def _swa_probs(s, bias_h, sink, valid):
    s = jnp.where(valid, s * (HEAD_DIM ** -0.5) + bias_h, -jnp.inf)
    m = jnp.maximum(jnp.max(s, axis=-1, keepdims=True), sink)
    e = jnp.exp(s - m)
    es = jnp.exp(sink - m)
    den = jnp.sum(e, axis=-1, keepdims=True) + es
    return e / den, es / den


def _swa_fwd(proj, kk2, vv2, bias, sinks, B, S):
    T = B * S
    nb = S // ATT_BLOCK

    def body(q_ref, k_ref, v_ref, bias_ref, sink_ref, o_ref, kpad, vpad):
        zeros = jnp.zeros((ATT_BLOCK, 256), BF16)
        kpad[pl.ds(0, ATT_BLOCK), :] = zeros
        vpad[pl.ds(0, ATT_BLOCK), :] = zeros
        kpad[pl.ds(ATT_BLOCK, S), :] = k_ref[...]
        vpad[pl.ds(ATT_BLOCK, S), :] = v_ref[...]
        valid0, col = _swa_masks()
        lo_q = lax.broadcasted_iota(jnp.int32, (1, 128), 1) < HEAD_DIM

        def blk(n, carry):
            r0 = pl.multiple_of(n * ATT_BLOCK, ATT_BLOCK)
            rows = pl.ds(r0, ATT_BLOCK)
            valid = valid0 & ((n > 0) | (col >= ATT_BLOCK))
            for g in range(N_Q_HEADS // GROUP):
                lanes = pl.ds(128 * g, 128)
                kg = kpad[pl.ds(r0, 2 * ATT_BLOCK), lanes]
                vg = vpad[pl.ds(r0, 2 * ATT_BLOCK), lanes]
                qm = _stack_group(q_ref[rows, pl.ds(256 * g, 256)].astype(BF16), lo_q)
                s = _dot_nt(qm, kg)
                ps = []
                for i in range(GROUP):
                    h = GROUP * g + i
                    p, _ = _swa_probs(s[ATT_BLOCK * i:ATT_BLOCK * (i + 1)], bias_ref[h], sink_ref[h], valid)
                    ps.append(p.astype(BF16))
                o = _dot(jnp.concatenate(ps, axis=0), vg)
                o_ref[rows, pl.ds(256 * g, 256)] = _unstack_group(o, lo_q).astype(o_ref.dtype)
            return carry

        if nb % 2 == 0:
            lax.fori_loop(0, nb // 2, lambda i, c: blk(2 * i + 1, blk(2 * i, c)), 0)
        else:
            lax.fori_loop(0, nb, blk, 0)

    return pl.pallas_call(
        body, name="swa_fwd", grid=(B,),
        in_specs=[pl.BlockSpec((S, 512), lambda b: (b, 0)),
                  pl.BlockSpec((S, 256), lambda b: (b, 0)),
                  pl.BlockSpec((S, 256), lambda b: (b, 0)),
                  pl.BlockSpec((N_Q_HEADS, ATT_BLOCK, 2 * ATT_BLOCK), lambda b: (0, 0, 0)),
                  pl.BlockSpec(memory_space=pltpu.SMEM)],
        out_specs=pl.BlockSpec((S, 512), lambda b: (b, 0)),
        out_shape=jax.ShapeDtypeStruct((T, 512), BF16),
        scratch_shapes=[pltpu.VMEM((S + ATT_BLOCK, 256), BF16), pltpu.VMEM((S + ATT_BLOCK, 256), BF16)],
        compiler_params=_params("arbitrary"))(proj, kk2, vv2, bias, sinks)


def _swa_bwd(proj, kk2, vv2, bias, sinks, datt, B, S):
    T = B * S
    nb = S // ATT_BLOCK

    def body(q_ref, k_ref, v_ref, bias_ref, sink_ref, do_ref, dq_ref, dk_ref, dv_ref, dbias_ref, dsink_ref,
             kpad, vpad, dkpad, dvpad):
        b = pl.program_id(0)

        @pl.when(b == 0)
        def _():
            dbias_ref[...] = jnp.zeros_like(dbias_ref)
            dsink_ref[...] = jnp.zeros_like(dsink_ref)

        zeros = jnp.zeros((ATT_BLOCK, 256), BF16)
        kpad[pl.ds(0, ATT_BLOCK), :] = zeros
        vpad[pl.ds(0, ATT_BLOCK), :] = zeros
        kpad[pl.ds(ATT_BLOCK, S), :] = k_ref[...]
        vpad[pl.ds(ATT_BLOCK, S), :] = v_ref[...]
        dkpad[...] = jnp.zeros_like(dkpad)
        dvpad[...] = jnp.zeros_like(dvpad)
        valid0, col = _swa_masks()
        lo_q = lax.broadcasted_iota(jnp.int32, (1, 128), 1) < HEAD_DIM
        scale = HEAD_DIM ** -0.5

        def blk(n, carry):
            r0 = pl.multiple_of(n * ATT_BLOCK, ATT_BLOCK)
            rows = pl.ds(r0, ATT_BLOCK)
            band = pl.ds(r0, 2 * ATT_BLOCK)
            valid = valid0 & ((n > 0) | (col >= ATT_BLOCK))
            for g in range(N_Q_HEADS // GROUP):
                lanes = pl.ds(128 * g, 128)
                kg = kpad[band, lanes]
                vg = vpad[band, lanes]
                qm = _stack_group(q_ref[rows, pl.ds(256 * g, 256)].astype(BF16), lo_q)
                dom = _stack_group(do_ref[rows, pl.ds(256 * g, 256)], lo_q)
                s = _dot_nt(qm, kg)
                dp = _dot_nt(dom, vg)
                pst, dst = [], []
                for i in range(GROUP):
                    h = GROUP * g + i
                    sl = slice(ATT_BLOCK * i, ATT_BLOCK * (i + 1))
                    p, ps = _swa_probs(s[sl], bias_ref[h], sink_ref[h], valid)
                    delta = jnp.sum(p * dp[sl], axis=-1, keepdims=True)
                    ds = p * (dp[sl] - delta)
                    dbias_ref[h] += ds
                    dsink_ref[pl.ds(h, 1), :] += -jnp.sum(jnp.broadcast_to(ps * delta, (ATT_BLOCK, 128)),
                                                          axis=0, keepdims=True)
                    pst.append(p.astype(BF16))
                    dst.append((ds * scale).astype(BF16))
                pst, dst = jnp.concatenate(pst, axis=0), jnp.concatenate(dst, axis=0)
                dq_ref[rows, pl.ds(256 * g, 256)] = _unstack_group(_dot(dst, kg), lo_q).astype(dq_ref.dtype)
                dkpad[band, lanes] += _dot_tn(dst, qm)
                dvpad[band, lanes] += _dot_tn(pst, dom)
            return carry

        if nb % 2 == 0:
            lax.fori_loop(0, nb // 2, lambda i, c: blk(2 * i + 1, blk(2 * i, c)), 0)
        else:
            lax.fori_loop(0, nb, blk, 0)
        lo_out = lax.broadcasted_iota(jnp.int32, (1, 128), 1) < HEAD_DIM

        def fold(pad_ref):
            halves = []
            for g in range(N_Q_HEADS // GROUP):
                t = pad_ref[pl.ds(ATT_BLOCK, S), pl.ds(128 * g, 128)]
                halves.append(t + pltpu.roll(t, HEAD_DIM, 1))
            return jnp.where(lo_out, halves[0], halves[1])

        dk_ref[...] = fold(dkpad).astype(dk_ref.dtype)
        dv_ref[...] = fold(dvpad).astype(dv_ref.dtype)

    return pl.pallas_call(
        body, name="swa_bwd", grid=(B,),
        in_specs=[pl.BlockSpec((S, 512), lambda b: (b, 0)),
                  pl.BlockSpec((S, 256), lambda b: (b, 0)),
                  pl.BlockSpec((S, 256), lambda b: (b, 0)),
                  pl.BlockSpec((N_Q_HEADS, ATT_BLOCK, 2 * ATT_BLOCK), lambda b: (0, 0, 0)),
                  pl.BlockSpec(memory_space=pltpu.SMEM),
                  pl.BlockSpec((S, 512), lambda b: (b, 0))],
        out_specs=[pl.BlockSpec((S, 512), lambda b: (b, 0)),
                   pl.BlockSpec((S, 128), lambda b: (b, 0)),
                   pl.BlockSpec((S, 128), lambda b: (b, 0)),
                   pl.BlockSpec((N_Q_HEADS, ATT_BLOCK, 2 * ATT_BLOCK), lambda b: (0, 0, 0)),
                   pl.BlockSpec((N_Q_HEADS, 128), lambda b: (0, 0))],
        out_shape=[jax.ShapeDtypeStruct((T, 512), BF16),
                   jax.ShapeDtypeStruct((T, 128), BF16),
                   jax.ShapeDtypeStruct((T, 128), BF16),
                   jax.ShapeDtypeStruct((N_Q_HEADS, ATT_BLOCK, 2 * ATT_BLOCK), F32),
                   jax.ShapeDtypeStruct((N_Q_HEADS, 128), F32)],
        scratch_shapes=[pltpu.VMEM((S + ATT_BLOCK, 256), BF16), pltpu.VMEM((S + ATT_BLOCK, 256), BF16),
                        pltpu.VMEM((S + ATT_BLOCK, 256), F32), pltpu.VMEM((S + ATT_BLOCK, 256), F32)],
        compiler_params=_params("arbitrary"))(proj, kk2, vv2, bias, sinks, datt)


def _hgrn_gates(z, lb):
    sg = _sigmoid(z)
    f = lb + (1.0 - lb) * sg
    return sg, f, jnp.log(f), 1.0 - f


def _hgrn_consts():
    r = lax.broadcasted_iota(jnp.int32, (CHUNK, CHUNK), 0)
    c = lax.broadcasted_iota(jnp.int32, (CHUNK, CHUNK), 1)
    tril = (r >= c).astype(BF16)
    triu = (r <= c).astype(BF16)
    causal = r >= c
    below = (r // SUB) > (c // SUB)
    inside = ((r // SUB) == (c // SUB)) & causal
    return tril, triu, causal, below, inside, c


def _block_rows(ref, lanes, s):
    rows = []
    for i in range(N_SUB):
        if SUB * i + s < 0:
            rows.append(jnp.zeros((SUB, REC_DIM), F32))
        else:
            rows.append(jnp.broadcast_to(ref[pl.ds(SUB * i + s, 1), lanes], (SUB, REC_DIM)))
    return jnp.concatenate(rows, axis=0)


def _hgrn_offdiag(q, k, bcum, b_ref, lanes):
    eq = jnp.exp(jnp.minimum(bcum - _block_rows(b_ref, lanes, -1), 0.0))
    qe = q * eq
    zero = jnp.zeros((SUB, REC_DIM), F32)
    q_rows, k_cols, eks = [jnp.zeros((SUB, (N_SUB - 1) * REC_DIM), F32)], [], []
    for i in range(1, N_SUB):
        q_rows.append(jnp.concatenate([zero] * (i - 1) + [qe[SUB * i:SUB * (i + 1), :]] + [zero] * (N_SUB - 1 - i),
                                      axis=1))
        p = b_ref[pl.ds(SUB * i - 1, 1), lanes]
        pad = jnp.zeros((CHUNK - SUB * i, REC_DIM), F32)
        ek = jnp.concatenate([jnp.exp(p - b_ref[pl.ds(0, SUB * i), lanes]), pad], axis=0)
        k_cols.append(k * ek)
        eks.append(ek)
    return jnp.concatenate(q_rows, axis=0), jnp.concatenate(k_cols, axis=1), eq, eks


def _hgrn_fwd(proj, lb_param, B, S, comm=None):
    T = B * S
    nc = S // CHUNK
    fwd_unroll = 4 if nc % 4 == 0 else 2
    c_arrays, c_in_specs, c_out_shapes, c_sems = _comm_parts(comm)
    nci, nco = len(c_arrays), len(c_out_shapes)

    def body(*refs):
        q_ref, z_ref, v_ref, lb_ref = refs[:4]
        o_ref, st_ref = refs[4 + nci:6 + nci]
        k_slots, b_slots = refs[6 + nci + nco:8 + nci + nco]
        comm_first, comm_last = _comm_run(comm, (B, REC_HEADS // HGRN_PAIR), refs, 4, 2)
        comm_first()
        tril, _, _, below, inside, col = _hgrn_consts()
        col_s = col & (SUB - 1)

        def chunk(ci, hts, slot):
            k_s, b_s = k_slots.at[slot], b_slots.at[slot]
            r0 = pl.multiple_of(ci * CHUNK, CHUNK)
            lb = _sigmoid(lb_ref[0:1, :] - lb_ref[1:2, :])
            _, _, g_all, k_all = _hgrn_gates(z_ref[pl.ds(r0, CHUNK), :], lb)
            b_all = _sel_left(tril, g_all)
            k_s[...] = k_all
            b_s[...] = b_all
            new = []
            for e, ht in enumerate(hts):
                lanes = pl.ds(REC_DIM * e, REC_DIM)
                cols = slice(REC_DIM * e, REC_DIM * (e + 1))
                q = q_ref[pl.ds(r0, CHUNK), lanes]
                v = v_ref[pl.ds(r0, CHUNK), lanes]
                k, bcum = k_all[:, cols], b_all[:, cols]
                st_ref[e * nc + ci] = ht
                qst, kst, _, _ = _hgrn_offdiag(q, k, bcum, b_s, lanes)
                d = jnp.zeros((CHUNK, CHUNK), F32)
                for s in range(SUB):
                    w = jnp.exp(jnp.minimum(bcum - _block_rows(b_s, lanes, s), 0.0))
                    colv = jnp.sum(q * _block_rows(k_s, lanes, s) * w, axis=-1, keepdims=True)
                    d = jnp.where(col_s == s, colv, d)
                a = jnp.where(below, _dot_nt(qst.astype(BF16), kst.astype(BF16)), 0.0) + jnp.where(inside, d, 0.0)
                vb = v.astype(BF16)
                qb = (q * jnp.exp(bcum)).astype(BF16)
                o_ref[pl.ds(r0, CHUNK), lanes] = _dot(a.astype(BF16), vb) + _dot_nt(qb, ht.astype(BF16))
                b_last = b_s[pl.ds(CHUNK - 1, 1), lanes]
                kb = (k * jnp.exp(b_last - bcum)).astype(BF16)
                new.append(ht * jnp.exp(b_last) + _dot_tn(vb, kb))
            return tuple(new)

        def trip(i, hts):
            for u in range(fwd_unroll):
                hts = chunk(fwd_unroll * i + u, hts, u)
            return hts

        lax.fori_loop(0, nc // fwd_unroll, trip, tuple(jnp.zeros((REC_DIM, REC_DIM), F32) for _ in range(HGRN_PAIR)))
        comm_last()

    hp, wd = REC_HEADS // HGRN_PAIR, HGRN_PAIR * REC_DIM
    cq, cf, ci_ = (c * REC_DIM // wd for c in (COL_RQ, COL_RF, COL_RI))
    return pl.pallas_call(
        body, name="hgrn_fwd", grid=(B, hp),
        in_specs=[pl.BlockSpec((S, wd), lambda b, h: (b, cq + h)),
                  pl.BlockSpec((S, wd), lambda b, h: (b, cf + h)),
                  pl.BlockSpec((S, wd), lambda b, h: (b, ci_ + h)),
                  pl.BlockSpec((2, wd), lambda b, h: (0, h))] + c_in_specs,
        out_specs=[pl.BlockSpec((S, wd), lambda b, h: (b, h)),
                   pl.BlockSpec((HGRN_PAIR * nc, REC_DIM, REC_DIM), lambda b, h: (b * hp + h, 0, 0))] + [ANY] * nco,
        out_shape=[jax.ShapeDtypeStruct((T, 512), F32),
                   jax.ShapeDtypeStruct((B * REC_HEADS * nc, REC_DIM, REC_DIM), F32)] + c_out_shapes,
        scratch_shapes=[pltpu.VMEM((fwd_unroll, CHUNK, wd), F32), pltpu.VMEM((fwd_unroll, CHUNK, wd), F32)] + c_sems,
        compiler_params=_params("arbitrary", "arbitrary"))(proj, proj, proj, lb_param, *c_arrays)


def _hgrn_bwd(proj, lb_param, states, do, B, S, comm=None):
    T = B * S
    nc = S // CHUNK
    bwd_unroll = 4 if nc % 4 == 0 else 2
    c_arrays, c_in_specs, c_out_shapes, c_sems = _comm_parts(comm)
    nci, nco = len(c_arrays), len(c_out_shapes)

    def body(*refs):
        q_ref, z_ref, v_ref, lb_ref, st_ref, do_ref = refs[:6]
        dq_ref, dz_ref, dv_ref, dlb_ref = refs[6 + nci:10 + nci]
        slots = refs[10 + nci + nco:14 + nci + nco]
        comm_first, comm_last = _comm_run(comm, (B, REC_HEADS // HGRN_PAIR), refs, 6, 4)
        comm_first()
        tril, triu, causal, below, inside, col = _hgrn_consts()
        col_s = col & (SUB - 1)
        last_row = lax.broadcasted_iota(jnp.int32, (CHUNK, 1), 0) == CHUNK - 1
        rc = lax.broadcasted_iota(jnp.int32, (CHUNK, SUB * REC_DIM), 0)
        lc = lax.broadcasted_iota(jnp.int32, (CHUNK, SUB * REC_DIM), 1)
        spread = ((rc & (SUB - 1)) == (lc // REC_DIM)).astype(BF16)
        rr = lax.broadcasted_iota(jnp.int32, (CHUNK, SUB * CHUNK), 0)
        cc = lax.broadcasted_iota(jnp.int32, (CHUNK, SUB * CHUNK), 1)
        gather = (((rr // SUB) == ((cc & (CHUNK - 1)) // SUB)) & ((rr & (SUB - 1)) == (cc // CHUNK))).astype(BF16)

        heads = range(HGRN_PAIR)
        cols = [slice(REC_DIM * e, REC_DIM * (e + 1)) for e in heads]
        lanes = [pl.ds(REC_DIM * e, REC_DIM) for e in heads]
        lane_cat = lambda vals: jnp.concatenate(vals, axis=1)
        row_cat = lambda vals: jnp.concatenate(vals, axis=0)

        def chunk(it, carry, slot):
            k_s, b_s, pc_hi, pc_lo = (r.at[slot] for r in slots)
            dhts, dlb = carry
            ci = nc - 1 - it
            r0 = pl.multiple_of(ci * CHUNK, CHUNK)
            rows = pl.ds(r0, CHUNK)
            lb = _sigmoid(lb_ref[0:1, :] - lb_ref[1:2, :])
            sg, f, g_all, k_all = _hgrn_gates(z_ref[rows, :], lb)
            b_all = _sel_left(tril, g_all)
            k_s[...] = k_all
            b_s[...] = b_all
            q_all = q_ref[rows, :]
            das, hd = [], []
            for e in heads:
                vb, dob = v_ref[rows, lanes[e]].astype(BF16), do_ref[rows, lanes[e]].astype(BF16)
                da = jnp.where(causal, _dot_nt(dob, vb), 0.0)
                das.append(jnp.where(inside, da, 0.0))
                hd.append((vb, dob, da))
            da_hi, da_lo = _split2(row_cat(das))
            da_in = _dot(da_hi, spread) + _dot(da_lo, spread)
            ds, dqs = [], []
            for e in heads:
                q, bcum = q_all[:, cols[e]], b_all[:, cols[e]]
                d = jnp.zeros((CHUNK, CHUNK), F32)
                dq = jnp.zeros((CHUNK, REC_DIM), F32)
                for s in range(SUB):
                    w = jnp.exp(jnp.minimum(bcum - _block_rows(b_s, lanes[e], s), 0.0))
                    ks = _block_rows(k_s, lanes[e], s)
                    qw = q * w
                    d = jnp.where(col_s == s, jnp.sum(qw * ks, axis=-1, keepdims=True), d)
                    da_s = da_in[CHUNK * e:CHUNK * (e + 1), REC_DIM * s:REC_DIM * (s + 1)]
                    dq = dq + da_s * ks * w
                    hi, lo = _split2(da_s * qw)
                    pc_hi[pl.ds(CHUNK * s, CHUNK), lanes[e]] = hi
                    pc_lo[pl.ds(CHUNK * s, CHUNK), lanes[e]] = lo
                ds.append(d)
                dqs.append(dq)
            dk_in = _dot(gather, pc_hi[...]) + _dot(gather, pc_lo[...])
            dq_out, dk_out, dv_out, db_out, new_dhts = [], [], [], [], []
            for e in heads:
                q, k, bcum = q_all[:, cols[e]], k_all[:, cols[e]], b_all[:, cols[e]]
                vb, dob, da = hd[e]
                dht, ht = dhts[e], st_ref[e * nc + ci]
                qst, kst, eq, eks = _hgrn_offdiag(q, k, bcum, b_s, lanes[e])
                qst_b, kst_b = qst.astype(BF16), kst.astype(BF16)
                a = jnp.where(below, _dot_nt(qst_b, kst_b), 0.0) + jnp.where(inside, ds[e], 0.0)
                da_off = jnp.where(below, da, 0.0).astype(BF16)
                dqst = _dot(da_off, kst_b)
                dkst = _dot_tn(da_off, qst_b)
                dk = dk_in[:, cols[e]]
                dq_rows = [jnp.zeros((SUB, REC_DIM), F32)]
                for i in range(1, N_SUB):
                    dq_rows.append(dqst[SUB * i:SUB * (i + 1), REC_DIM * (i - 1):REC_DIM * i])
                    dk = dk + dkst[:, REC_DIM * (i - 1):REC_DIM * i] * eks[i - 1]
                dq = dqs[e] + row_cat(dq_rows) * eq
                eb = jnp.exp(bcum)
                b_last = b_s[pl.ds(CHUNK - 1, 1), lanes[e]]
                el = jnp.exp(b_last)
                ekb = jnp.exp(b_last - bcum)
                qb = (q * eb).astype(BF16)
                kb = k * ekb
                dhb = dht.astype(BF16)
                dv_out.append(_dot_tn(a.astype(BF16), dob) + _dot_nt(kb.astype(BF16), dhb))
                dqb = _dot(dob, ht.astype(BF16))
                dkb = _dot(vb, dhb)
                new_dhts.append(dht * el + _dot_tn(dob, qb))
                dq = dq + eb * dqb
                dk = dk + ekb * dkb
                edge = jnp.sum(kb * dkb, axis=0, keepdims=True) + el * jnp.sum(ht * dht, axis=0, keepdims=True)
                db_out.append(q * dq - k * dk + jnp.where(last_row, edge, 0.0))
                dq_out.append(dq)
                dk_out.append(dk)
            dk_all = lane_cat(dk_out)
            db_hi, db_lo = _split2(lane_cat(db_out))
            dg = _dot(triu, db_hi) + _dot(triu, db_lo)
            df = dg / f - dk_all
            dz_ref[rows, :] = (df * (1.0 - lb) * sg * (1.0 - sg)).astype(dz_ref.dtype)
            dq_ref[rows, :] = lane_cat(dq_out).astype(dq_ref.dtype)
            dv_ref[rows, :] = lane_cat(dv_out).astype(dv_ref.dtype)
            return tuple(new_dhts), dlb + jnp.sum(df * (1.0 - sg), axis=0, keepdims=True)

        zero = (tuple(jnp.zeros((REC_DIM, REC_DIM), F32) for _ in heads), jnp.zeros((1, HGRN_PAIR * REC_DIM), F32))
        def trip(i, carry):
            for u in range(bwd_unroll):
                carry = chunk(bwd_unroll * i + u, carry, u)
            return carry

        _, dlb = lax.fori_loop(0, nc // bwd_unroll, trip, zero)
        lb = _sigmoid(lb_ref[0:1, :] - lb_ref[1:2, :])
        dlb_ref[...] = jnp.broadcast_to(dlb * lb * (1.0 - lb), (8, HGRN_PAIR * REC_DIM))
        comm_last()

    hp, wd = REC_HEADS // HGRN_PAIR, HGRN_PAIR * REC_DIM
    cq, cf, ci_ = (c * REC_DIM // wd for c in (COL_RQ, COL_RF, COL_RI))
    return pl.pallas_call(
        body, name="hgrn_bwd", grid=(B, hp),
        in_specs=[pl.BlockSpec((S, wd), lambda b, h: (b, cq + h)),
                  pl.BlockSpec((S, wd), lambda b, h: (b, cf + h)),
                  pl.BlockSpec((S, wd), lambda b, h: (b, ci_ + h)),
                  pl.BlockSpec((2, wd), lambda b, h: (0, h)),
                  pl.BlockSpec((HGRN_PAIR * nc, REC_DIM, REC_DIM), lambda b, h: (b * hp + h, 0, 0)),
                  pl.BlockSpec((S, wd), lambda b, h: (b, h))] + c_in_specs,
        out_specs=[pl.BlockSpec((S, wd), lambda b, h: (b, h))] * 3
        + [pl.BlockSpec((8, wd), lambda b, h: (b, h))] + [ANY] * nco,
        out_shape=[jax.ShapeDtypeStruct((T, 512), BF16)] * 3 + [jax.ShapeDtypeStruct((B * 8, 512), F32)]
        + c_out_shapes,
        scratch_shapes=[pltpu.VMEM((bwd_unroll, CHUNK, wd), F32)] * 2
        + [pltpu.VMEM((bwd_unroll, SUB * CHUNK, wd), BF16)] * 2 + c_sems,
        compiler_params=_params("arbitrary", "arbitrary"))(proj, proj, proj, lb_param, states, do, *c_arrays)


def _rec_gate_fwd(rec, proj, rec_norm):
    T = rec.shape[0]

    def fn(accs, tv, cv):
        return [_rms_hat(tv[0]) * cv[0] * _sigmoid(tv[1])]

    return _tile_call("rec_gate", fn, T, 512, _pick(T, 1024), REC_DIM, tiles=[(rec, 0), (proj, COL_RG)],
                      consts=[rec_norm], outs=[BF16])[0]


def _rec_gate_bwd(dyb, w_rec_proj, rec, proj, rec_norm):
    T = rec.shape[0]

    def fn(accs, tv, cv):
        d, r, rg = accs[0], tv[0], tv[1]
        sg = _sigmoid(rg)
        rn = _rms_hat(r) * cv[0]
        dh, dg = _rms_bwd_vals(d * sg, r, cv[0])
        return [dh, d * rn * sg * (1.0 - sg), dg]

    return _tile_call("rec_gate_bwd", fn, T, 512, _pick(T, 1024), REC_DIM, pairs=[(dyb, 0, w_rec_proj, "nt")],
                      tiles=[(rec, 0), (proj, COL_RG)], consts=[rec_norm], outs=[F32, BF16], parts=1)


def _mix_out_fwd(att, recn, proj, w_att_proj, w_rec_proj, w_out, h1, g_next):
    T = att.shape[0]
    tn = 256

    def merge(accs, tv, cv):
        ya, yb = accs
        return [ya, yb, _sigmoid(tv[0]) * ya + _sigmoid(tv[1]) * yb]

    ya, yb, merged = _tile_call(
        "merge", merge, T, D_MODEL, _pick(T, 1024), tn,
        pairs=[(att, 0, w_att_proj, "nn"), (recn, 0, w_rec_proj, "nn")],
        tiles=[(proj, COL_GA * 128 // tn), (proj, COL_GB * 128 // tn)], outs=[BF16] * 3)

    def res(accs, tv, cv):
        h2 = tv[0] + accs[0]
        return [h2, _rms_hat(h2) * cv[0]]

    h2, n2 = _tile_call("mix_out", res, T, D_MODEL, _pick(T, 512), D_MODEL, pairs=[(merged, 0, w_out, "nn")],
                        tiles=[(h1, 0)], consts=[g_next], outs=[F32, BF16])
    return h2, n2, (ya, yb, merged)


GATHER_FIRST = ("w_ffn1_in",)
GATHER_MIX = ("w_ffn1_out", "w_in")
GATHER_PROJ = ("w_att_proj", "w_rec_proj", "w_out")
GATHER_TAIL = ("w_ffn2_out", "w_ple_gate", "w_ple_proj")
GATHER_LAST = ("w_ffn2_in",)
SCATTER_LATE = ("w_ple_gate", "w_ple_proj", "w_ffn2_in", "w_ffn2_out")
SCATTER_MIX = ("w_out", "w_att_proj", "w_rec_proj", "w_in")
SCATTER_LAST = ("w_ffn1_in", "w_ffn1_out")


def _local_step(x, p, tgt, w, mine16, cc, me_chip, B, S):
    T = B * S
    w = dict(w)
    g_ffn1, g_mix, g_ffn2, g_ple = w["norm_ffn1"], w["norm_mix"], w["norm_ffn2"], w["norm_ple"]
    g_fin = w["norm_final"].reshape(1, D_MODEL)
    grads, part, from_chips = {}, {}, {}

    def gather(names):
        return _gather_comm([mine16[n] for n in names])

    def place(names, got):
        for n, g in zip(names, got):
            full = lax.dynamic_update_index_in_dim(g, mine16[n], me_chip, 0)
            w[n] = full if n in ("w_ffn1_in", "w_ffn2_in") else _natural(n, full)

    def swap(names):
        return _swap_comm([grads[n][1] for n in names])

    def after_swap(names, from_sib):
        for n, fs in zip(names, from_sib):
            part[n] = _add_sibling("rs_add_sib_" + n, grads[n][0], fs, cc)
        return _scatter_comm([part[n][1] for n in names])

    def scattered(names, got):
        for n, g in zip(names, got):
            from_chips[n] = g

    def ffn1_in_weight(got):
        place(GATHER_FIRST, got)
        return w["w_ffn1_in"]

    def ffn1_out_weight(got):
        place(GATHER_MIX, got)
        return w["w_ffn1_out"]

    h1, u, sv1, got_proj = _ffn_fwd("ffn1", x, g_ffn1, None, None, g_mix, comm_norm=gather(GATHER_FIRST),
                                    w_in_of=ffn1_in_weight, comm_in=gather(GATHER_MIX),
                                    comm_out=gather(GATHER_PROJ), w_out_of=ffn1_out_weight)
    place(GATHER_PROJ, got_proj)

    def ident(accs, tv, cv):
        return [accs[0]]

    proj, *got_tail = _tile_call("in_proj", ident, T, IN_W, _pick(T, 512), IN_W // 2,
                                 pairs=[(u, 0, w["w_in"], "nn")], outs=[F32], j_outer=True, comm=gather(GATHER_TAIL))
    place(GATHER_TAIL, got_tail)
    onehot = jnp.asarray(_t5_onehot())
    bias = _small_mm("t5_bias", w["rel_bias"].T, onehot.astype(BF16), "right")
    bias = bias.reshape(N_Q_HEADS, ATT_BLOCK, 2 * ATT_BLOCK)
    sinks = w["attn_sinks"].reshape(N_Q_HEADS)
    kk2, vv2 = _kv_layouts(proj)
    att = _swa_fwd(proj, kk2, vv2, bias, sinks, B, S)
    rec, states, *got_last = _hgrn_fwd(proj, w["lb_param"], B, S, comm=gather(GATHER_LAST))
    place(GATHER_LAST, got_last)
    recn = _rec_gate_fwd(rec, proj, w["rec_norm"])
    h2, n2, (ya, yb, merged) = _mix_out_fwd(att, recn, proj, w["w_att_proj"], w["w_rec_proj"], w["w_out"], h1,
                                            g_ffn2)
    h3, n3, sv2, _ = _ffn_fwd("ffn2", h2, g_ffn2, w["w_ffn2_in"], w["w_ffn2_out"], g_ple, n=n2)

    def ple(accs, tv, cv):
        gate = _sigmoid(accs[0])
        return [gate, accs[1], tv[0] + gate * accs[1]]

    gate_p, pp, h4 = _tile_call(
        "ple", ple, T, D_MODEL, _pick(T, 512), D_MODEL,
        pairs=[(n3, 0, w["w_ple_gate"], "nn"), (p, 0, w["w_ple_proj"], "nn")], tiles=[(h3, 0)],
        outs=[BF16, BF16, F32])

    def head(accs, tv, cv):
        h, t, gt, ppv = tv[0], tv[1], tv[2].astype(F32), tv[3].astype(F32)
        err = _rms_hat(h) * cv[0] - t
        dh, dg = _rms_bwd_vals(err * (1.0 / D_MODEL), h, cv[0])
        return [dh, dh * ppv * gt * (1.0 - gt), dh * gt, _group8(err * err), dg]

    dh4, dzg, dpp, loss_p, dg_fin = _tile_call(
        "loss_head", head, T, D_MODEL, _pick(T, 256), D_MODEL,
        tiles=[(h4, 0), (tgt, 0), (gate_p, 0), (pp, 0)], consts=[g_fin], outs=[F32, BF16, BF16], parts=2)
    grads["norm_final"] = dg_fin

    grads["w_ple_gate"] = _mm_tn_rows("ple_dwg", n3, dzg)
    grads["w_ple_proj"] = _mm_tn_cols("ple_dwp", p, dpp)

    def dnorm(accs, tv, cv):
        dh, dg = _rms_bwd_vals(accs[0], tv[0], cv[0])
        dh = tv[1] + dh
        return [dh, 0.5 * dh, dg]

    dh3, df3, grads["norm_ple"] = _tile_call(
        "ple_dnorm", dnorm, T, D_MODEL, _pick(T, 512), D_MODEL, pairs=[(dzg, 0, w["w_ple_gate"], "nt")],
        tiles=[(h3, 0), (dh4, 0)], consts=[g_ple], outs=[F32, BF16], parts=1)

    def swap_late(dw_in, dw_out):
        grads["w_ffn2_in"], grads["w_ffn2_out"] = dw_in, dw_out
        return swap(SCATTER_LATE)

    dh2, dh2b, grads["norm_ffn2"], _, _, _, from_sib = _ffn_bwd(
        "ffn2b", dh3, df3, h2, g_ffn2, w["w_ffn2_in"], w["w_ffn2_out"], sv2, comm_last=swap_late)
    scatter_late = after_swap(SCATTER_LATE, from_sib)

    grads["w_out"] = _mm_tn_rows("mix_dwout", merged, dh2b)
    tn = 256

    def dmerge(accs, tv, cv):
        dm = accs[0]
        sa, sb = _sigmoid(tv[0]), _sigmoid(tv[1])
        yav, ybv = tv[2].astype(F32), tv[3].astype(F32)
        return [dm * sa, dm * sb, dm * yav * sa * (1.0 - sa), dm * ybv * sb * (1.0 - sb)]

    dya, dyb, dga, dgb = _tile_call(
        "mix_dmerge", dmerge, T, D_MODEL, _pick(T, 1024), tn, pairs=[(dh2b, 0, w["w_out"], "nt")],
        tiles=[(proj, COL_GA * 128 // tn), (proj, COL_GB * 128 // tn), (ya, 0), (yb, 0)], outs=[BF16] * 4)
    grads["w_att_proj"] = _mm_tn_cols("mix_dwatt", att, dya)
    grads["w_rec_proj"] = _mm_tn_cols("mix_dwrec", recn, dyb)

    datt = _tile_call("mix_datt", ident, T, 512, _pick(T, 1024), 512, pairs=[(dya, 0, w["w_att_proj"], "nt")],
                      outs=[BF16])[0]
    drec, drg, grads["rec_norm"] = _rec_gate_bwd(dyb, w["w_rec_proj"], rec, proj, w["rec_norm"])

    drq, drf, dri, dlb, *got = _hgrn_bwd(proj, w["lb_param"], states, drec, B, S, comm=scatter_late)
    scattered(SCATTER_LATE, got)
    grads["lb_param"] = dlb
    daq, dak, dav, dbias, dsink = _swa_bwd(proj, kk2, vv2, bias, sinks, datt, B, S)
    grads["attn_sinks"] = dsink
    grads["rel_bias"] = _small_mm("t5_dbias", dbias.reshape(N_Q_HEADS, -1), onehot.T.astype(BF16), "right")
    dproj = jnp.concatenate([daq, dak, dav, drq, drf, dri, drg, dga, dgb], axis=1)
    tk = _pick(T, 2048, 128)
    w_in_shard = IN_W // N_CHIPS
    half_d = D_MODEL // 2
    gw32, gw16 = _mm_tn("mix_dwin", (2, 2, T // tk),
                        (u, (tk, half_d), lambda i, j, k: (k, i)), (dproj, (tk, IN_W // 2), lambda i, j, k: (k, j)),
                        _grad_pair((D_MODEL, IN_W), (half_d, IN_W // 2), lambda i, j, k: (i, j)))
    to_sh = lambda t: t.reshape(D_MODEL, N_CHIPS, w_in_shard).transpose(1, 0, 2)
    grads["w_in"] = (to_sh(gw32), to_sh(gw16))

    def dnorm_mix(accs, tv, cv):
        dh, dg = _rms_bwd_vals(accs[0], tv[0], cv[0])
        dh = tv[1] + dh
        return [dh, 0.5 * dh, dg]

    dh1, df1, grads["norm_mix"], *from_sib = _tile_call(
        "mix_dnorm", dnorm_mix, T, D_MODEL, _pick(T, 512), D_MODEL, pairs=[(dproj, 0, w["w_in"], "nt")],
        tiles=[(h1, 0), (dh2, 0)], consts=[g_mix], outs=[F32, BF16], parts=1, comm=swap(SCATTER_MIX))
    scatter_mix = after_swap(SCATTER_MIX, from_sib)

    def scatter_last(dw_in, dw_out):
        grads["w_ffn1_in"], grads["w_ffn1_out"] = dw_in, dw_out
        return after_swap(SCATTER_LAST, _run_comm("rs_sibling_last", swap(SCATTER_LAST)))

    dx, _, grads["norm_ffn1"], _, _, got, got_last = _ffn_bwd(
        "ffn1b", dh1, df1, x, g_ffn1, w["w_ffn1_in"], w["w_ffn1_out"], sv1, comm=scatter_mix, comm_last=scatter_last)
    scattered(SCATTER_MIX, got)
    scattered(SCATTER_LAST, got_last)
    return loss_p, dx, grads, part, from_chips


def _place():
    x, y, c = lax.axis_index("x"), lax.axis_index("y"), lax.axis_index("c")
    return x, y, c


def _other_chips(x, y):
    return [(1 - x, y, 2 * (1 - x) + y), (x, 1 - y, 2 * x + 1 - y), (1 - x, 1 - y, 2 * (1 - x) + 1 - y)]


def _half_rows(ref_3d, chip, h, rows):
    return ref_3d.at[chip, pl.ds(h * rows, rows), :]


def _gather_comm(ws):
    nw = len(ws)

    def parts(w_refs, out_refs, send_sems, recv_sems):
        x, y, c = _place()
        me = 2 * x + y
        chips = _other_chips(x, y)

        def copy(i, k, chip, h, to, src=None):
            half = ws[i].shape[0] // 2
            dst = _half_rows(out_refs[i], chip, h, half)
            return pltpu.make_async_remote_copy(
                src_ref=dst if src is None else src, dst_ref=dst,
                send_sem=send_sems.at[6 * i + k], recv_sem=recv_sems.at[6 * i + k], device_id=to, device_id_type=MESH)

        def first():
            out = []
            for i in range(nw):
                half = ws[i].shape[0] // 2
                out += [copy(i, j, me, c, (cx, cy, c), src=w_refs[i].at[pl.ds(c * half, half), :])
                        for j, (cx, cy, _) in enumerate(chips)]
            return out

        return copy, first, chips, c, (x, y, 1 - c)

    def start(*refs):
        _, first, _, _, _ = parts(*refs)
        for cp in first():
            cp.start()

    def finish(*refs):
        copy, first, chips, c, sibling = parts(*refs)
        passed = []
        for i in range(nw):
            for j, (cx, cy, ci) in enumerate(chips):
                copy(i, j, ci, c, (cx, cy, c)).wait_recv()
                fw = copy(i, 3 + j, ci, c, sibling)
                fw.start()
                passed.append(fw)
        for i in range(nw):
            for j, (_, _, ci) in enumerate(chips):
                copy(i, 3 + j, ci, 1 - c, sibling).wait_recv()
        for cp in first() + passed:
            cp.wait_send()

    return _Comm(list(ws), [jax.ShapeDtypeStruct((N_CHIPS,) + w.shape, w.dtype) for w in ws], 6 * nw, start, finish)


def _scatter_comm(ps):
    nw = len(ps)

    def copies(p_refs, out_refs, send_sems, recv_sems):
        x, y, c = _place()
        cps = []
        for i in range(nw):
            for j, (cx, cy, ci) in enumerate(_other_chips(x, y)):
                cps.append(pltpu.make_async_remote_copy(
                    src_ref=p_refs[i].at[ci], dst_ref=out_refs[i].at[j], send_sem=send_sems.at[3 * i + j],
                    recv_sem=recv_sems.at[3 * i + j], device_id=(cx, cy, c), device_id_type=MESH))
        return cps

    def start(*refs):
        for cp in copies(*refs):
            cp.start()

    def finish(*refs):
        for cp in copies(*refs):
            cp.wait()

    return _Comm(list(ps), [jax.ShapeDtypeStruct((3,) + p.shape[1:], p.dtype) for p in ps], 3 * nw, start, finish)


def _swap_comm(gs):
    nw = len(gs)

    def copies(g_refs, out_refs, send_sems, recv_sems):
        x, y, c = _place()
        cps = []
        for i in range(nw):
            half = gs[i].shape[1] // 2
            cps.append(pltpu.make_async_remote_copy(
                src_ref=g_refs[i].at[:, pl.ds((1 - c) * half, half), :], dst_ref=out_refs[i],
                send_sem=send_sems.at[i], recv_sem=recv_sems.at[i], device_id=(x, y, 1 - c), device_id_type=MESH))
        return cps

    def start(*refs):
        for cp in copies(*refs):
            cp.start()

    def finish(*refs):
        for cp in copies(*refs):
            cp.wait()

    return _Comm(list(gs), [jax.ShapeDtypeStruct((N_CHIPS, g.shape[1] // 2, g.shape[2]), g.dtype) for g in gs],
                 nw, start, finish)


def _run_comm(name, comm):
    nci, nco = len(comm.ins), len(comm.out_shapes)

    def body(*refs):
        cin, cout, send_sems, recv_sems = refs[:nci], refs[nci:nci + nco], refs[-2], refs[-1]
        comm.start(cin, cout, send_sems, recv_sems)
        comm.finish(cin, cout, send_sems, recv_sems)

    return pl.pallas_call(
        body, name=name, in_specs=[ANY] * nci, out_specs=[ANY] * nco, out_shape=list(comm.out_shapes),
        scratch_shapes=[pltpu.SemaphoreType.DMA((comm.n_sems,)), pltpu.SemaphoreType.DMA((comm.n_sems,))],
    )(*comm.ins)


def _join_halves(name, ss):
    nw = len(ss)

    def body(*refs):
        s_refs, out_refs, send_sems, recv_sems = refs[:nw], refs[nw:2 * nw], refs[2 * nw], refs[2 * nw + 1]
        x, y, c = _place()
        cps = [pltpu.make_async_remote_copy(
            src_ref=s_refs[i], dst_ref=out_refs[i], send_sem=send_sems.at[i], recv_sem=recv_sems.at[i],
            device_id=(x, y, 1 - c), device_id_type=MESH) for i in range(nw)]
        for cp in cps:
            cp.start()
        for cp in cps:
            cp.wait()

    return pl.pallas_call(
        body, name=name, in_specs=[ANY] * nw, out_specs=[ANY] * nw,
        out_shape=[jax.ShapeDtypeStruct(s.shape, s.dtype) for s in ss],
        scratch_shapes=[pltpu.SemaphoreType.DMA((nw,)), pltpu.SemaphoreType.DMA((nw,))],
    )(*ss)


def _allreduce_small(sp):
    def body(s_ref, out_ref, slots, send_sems, recv_sems):
        x, y, c = _place()
        me = 4 * x + 2 * y + c
        slots[me] = s_ref[...]
        cps = []
        for r in range(1, N_DEV):
            px, py, pc = x ^ (r >> 2), y ^ ((r >> 1) & 1), c ^ (r & 1)
            cps.append(pltpu.make_async_remote_copy(
                src_ref=s_ref, dst_ref=slots.at[me], send_sem=send_sems.at[r - 1], recv_sem=recv_sems.at[r - 1],
                device_id=(px, py, pc), device_id_type=MESH))
        for cp in cps:
            cp.start()
        for r in range(1, N_DEV):
            px, py, pc = x ^ (r >> 2), y ^ ((r >> 1) & 1), c ^ (r & 1)
            pltpu.make_async_remote_copy(
                src_ref=s_ref, dst_ref=slots.at[4 * px + 2 * py + pc], send_sem=send_sems.at[r - 1],
                recv_sem=recv_sems.at[r - 1], device_id=(px, py, pc), device_id_type=MESH).wait_recv()
        for cp in cps:
            cp.wait_send()
        acc = slots[0]
        for d in range(1, N_DEV):
            acc = acc + slots[d]
        out_ref[...] = acc

    return pl.pallas_call(
        body, name="allreduce_small",
        in_specs=[pl.BlockSpec(memory_space=pltpu.VMEM)], out_specs=pl.BlockSpec(memory_space=pltpu.VMEM),
        out_shape=jax.ShapeDtypeStruct(sp.shape, F32),
        scratch_shapes=[pltpu.VMEM((N_DEV,) + sp.shape, F32), pltpu.SemaphoreType.DMA((N_DEV - 1,)),
                        pltpu.SemaphoreType.DMA((N_DEV - 1,))],
    )(sp)


def _scalar(v):
    return jnp.reshape(v, (1,)).astype(jnp.int32)


def _row_tile(h, dtype_mult=16):
    return _pick(h, 256, dtype_mult)


def _add_sibling(name, g32, from_sib, c):
    _, r, n = g32.shape
    h = r // 2
    th = _row_tile(h)
    nt = h // th

    def body(c_ref, g_ref, s_ref, o32_ref, o16_ref):
        s = g_ref[...] + s_ref[...].astype(F32)
        o32_ref[...] = s
        o16_ref[...] = s.astype(BF16)

    blk = (None, th, n)
    return pl.pallas_call(
        body, name=name,
        grid_spec=pltpu.PrefetchScalarGridSpec(
            num_scalar_prefetch=1, grid=(N_CHIPS, nt),
            in_specs=[pl.BlockSpec(blk, lambda k, t, c_ref: (k, c_ref[0] * nt + t, 0)),
                      pl.BlockSpec(blk, lambda k, t, c_ref: (k, t, 0))],
            out_specs=[pl.BlockSpec(blk, lambda k, t, c_ref: (k, t, 0))] * 2),
        out_shape=[jax.ShapeDtypeStruct((N_CHIPS, h, n), F32), jax.ShapeDtypeStruct((N_CHIPS, h, n), BF16)],
        compiler_params=_params("arbitrary", "arbitrary"))(_scalar(c), g32, from_sib)


def _add_chips(name, p32, from_chips, me_chip):
    _, h, n = p32.shape
    th = _row_tile(h)

    def body(m_ref, p_ref, a_ref, b_ref, c_ref, o_ref):
        o_ref[...] = p_ref[...] + a_ref[...].astype(F32) + b_ref[...].astype(F32) + c_ref[...].astype(F32)

    blk = (None, th, n)
    return pl.pallas_call(
        body, name=name,
        grid_spec=pltpu.PrefetchScalarGridSpec(
            num_scalar_prefetch=1, grid=(h // th,),
            in_specs=[pl.BlockSpec(blk, lambda t, m_ref: (m_ref[0], t, 0))]
            + [pl.BlockSpec(blk, lambda t, m_ref, j=j: (j, t, 0)) for j in range(3)],
            out_specs=pl.BlockSpec((th, n), lambda t, m_ref: (t, 0))),
        out_shape=jax.ShapeDtypeStruct((h, n), F32),
        compiler_params=_params("arbitrary"))(_scalar(me_chip), p32, from_chips, from_chips, from_chips)


def _adamw_vals(w, g, m, v):
    m = ADAM_B1 * m + (1.0 - ADAM_B1) * g
    v = ADAM_B2 * v + (1.0 - ADAM_B2) * (g * g)
    m_hat = m / (1.0 - ADAM_B1 ** ADAM_STEP)
    v_hat = v / (1.0 - ADAM_B2 ** ADAM_STEP)
    delta = -ADAM_LR * (m_hat / (jnp.sqrt(v_hat) + ADAM_EPS) + ADAM_WD * w)
    return delta, m, v


def _adamw_halves(name, w, m, v, g_mine, g_sib, c):
    r, n = w.shape
    h = r // 2
    th = _row_tile(h, 8)
    nt = h // th

    def body(c_ref, w_ref, m_ref, v_ref, a_ref, b_ref, g_ref, d_ref, nm_ref, nv_ref):
        mine = (pl.program_id(0) // nt) == c_ref[0]
        g = jnp.where(mine, a_ref[...], b_ref[...])
        d, nm, nv = _adamw_vals(w_ref[...], g, m_ref[...], v_ref[...])
        g_ref[...] = g
        d_ref[...] = d
        nm_ref[...] = nm
        nv_ref[...] = nv

    full = pl.BlockSpec((th, n), lambda t, c_ref: (t, 0))
    part = pl.BlockSpec((th, n), lambda t, c_ref: (t % nt, 0))
    return pl.pallas_call(
        body, name=name,
        grid_spec=pltpu.PrefetchScalarGridSpec(
            num_scalar_prefetch=1, grid=(2 * nt,), in_specs=[full, full, full, part, part], out_specs=[full] * 4),
        out_shape=[jax.ShapeDtypeStruct((r, n), F32)] * 4,
        compiler_params=_params("arbitrary"))(_scalar(c), w, m, v, g_mine, g_sib)


def _adamw(name, w, g, m, v):
    R, W = w.shape

    def fn(accs, tv, cv):
        return list(_adamw_vals(*tv))

    return _tile_call(name, fn, R, W, _pick(R, 256), W, tiles=[(w, 0), (g, 0), (m, 0), (v, 0)], outs=[F32] * 3)


SMALL_LAYOUT = (("rel_bias", 2, 256), ("lb_param", 8, 1024), ("norm_ffn1", 8, 1024), ("norm_mix", 8, 1024),
                ("attn_sinks", 1, 8), ("rec_norm", 1, 128), ("norm_ffn2", 8, 1024), ("norm_ple", 8, 1024),
                ("norm_final", 8, 1024), ("loss", 8, 1024))


def _pack_small(vals):
    rows = []
    for name, nrows, n in SMALL_LAYOUT:
        flat = vals[name].reshape(-1)
        flat = jnp.pad(flat, (0, nrows * 128 - n))
        rows.append(flat.reshape(nrows, 128))
    packed = jnp.concatenate(rows, axis=0)
    return jnp.pad(packed, ((0, SMALL_ROWS - packed.shape[0]), (0, 0)))


def _unpack_small(packed, shapes):
    out, r = {}, 0
    for name, nrows, n in SMALL_LAYOUT:
        out[name] = packed[r:r + nrows].reshape(-1)[:n].reshape(shapes[name])
        r += nrows
    return out


def _natural(name, s):
    if name in COL_SHARDED:
        return s.transpose(1, 0, 2).reshape(s.shape[1], -1)
    return s.reshape(-1, s.shape[2])


def kernel(x, p, rel_bias, lb_param, norm_ffn1, w_ffn1_in, w_ffn1_out, norm_mix, w_in, attn_sinks, rec_norm, w_att_proj, w_rec_proj, w_out, norm_ffn2, w_ffn2_in, w_ffn2_out, norm_ple, w_ple_gate, w_ple_proj, norm_final, loss_target, m_rel_bias, m_lb_param, m_norm_ffn1, m_w_ffn1_in, m_w_ffn1_out, m_norm_mix, m_w_in, m_attn_sinks, m_rec_norm, m_w_att_proj, m_w_rec_proj, m_w_out, m_norm_ffn2, m_w_ffn2_in, m_w_ffn2_out, m_norm_ple, m_w_ple_gate, m_w_ple_proj, m_norm_final, v_rel_bias, v_lb_param, v_norm_ffn1, v_w_ffn1_in, v_w_ffn1_out, v_norm_mix, v_w_in, v_attn_sinks, v_rec_norm, v_w_att_proj, v_w_rec_proj, v_w_out, v_norm_ffn2, v_w_ffn2_in, v_w_ffn2_out, v_norm_ple, v_w_ple_gate, v_w_ple_proj, v_norm_final):
    args = dict(locals())
    wsh = {n: args[n] for n in WEIGHTS}
    B, S = x.shape[0], x.shape[1]
    T = B * S
    cx, cy, cc = _place()
    me_chip = 2 * cx + cy

    mine16 = {n: wsh[n][0].astype(BF16) for n in BIG}
    loss_p, dx, grads, part, from_chips = _local_step(
        x.reshape(T, D_MODEL), p.reshape(T, PLE_DIM), loss_target.reshape(T, D_MODEL),
        {n: wsh[n] for n in SMALL}, mine16, cc, me_chip, B, S)

    s_mine = [_add_chips("rs_add_chips_" + n, part[n][0], from_chips[n], me_chip) for n in BIG]
    s_sib = _join_halves("rs_join", s_mine)

    small_vals = {
        "rel_bias": grads["rel_bias"].T,
        "lb_param": jnp.concatenate([_colsum("dlb_sum", grads["lb_param"]),
                                     -_colsum("dlb_sum2", grads["lb_param"])], axis=0) / 8.0,
        "attn_sinks": grads["attn_sinks"][:, 0],
        "rec_norm": _colsum("drn_sum", grads["rec_norm"]).reshape(REC_HEADS, REC_DIM).sum(axis=0),
        "loss": _colsum("loss_sum", loss_p),
    }
    for n in ("norm_ffn1", "norm_mix", "norm_ffn2", "norm_ple", "norm_final"):
        small_vals[n] = _colsum(n + "_sum", grads[n])
    red = _allreduce_small(_pack_small(small_vals))
    small_shapes = {n: wsh[n].shape for n in SMALL}
    small_shapes["loss"] = (D_MODEL,)
    small = _unpack_small(red, small_shapes)
    loss = 0.5 * jnp.sum(small["loss"]) / D_MODEL

    out_g, out_d, out_m, out_v = {}, {}, {}, {}
    for n, gm, gs in zip(BIG, s_mine, s_sib):
        res = _adamw_halves("adamw_" + n, wsh[n][0], args["m_" + n][0], args["v_" + n][0], gm, gs, cc)
        out_g[n], out_d[n], out_m[n], out_v[n] = (t[None] for t in res)
    sw = _pack_small({**{n: wsh[n] for n in SMALL}, "loss": jnp.zeros((D_MODEL,), F32)})
    sm = _pack_small({**{n: args["m_" + n] for n in SMALL}, "loss": jnp.zeros((D_MODEL,), F32)})
    sv = _pack_small({**{n: args["v_" + n] for n in SMALL}, "loss": jnp.ones((D_MODEL,), F32)})
    sd, snm, snv = _adamw("adamw_small", sw, red, sm, sv)
    ud, um, uv = (_unpack_small(t, small_shapes) for t in (sd, snm, snv))
    for n in SMALL:
        out_g[n], out_d[n], out_m[n], out_v[n] = small[n], ud[n], um[n], uv[n]

    return (loss, dx.reshape(B, S, D_MODEL), *[out_g[n] for n in WEIGHTS], *[out_d[n] for n in WEIGHTS],
            *[out_m[n] for n in WEIGHTS], *[out_v[n] for n in WEIGHTS])
```

```python
import numpy as np
import jax
import jax.numpy as jnp
from jax import lax
from jax.experimental import pallas as pl
from jax.experimental.pallas import tpu as pltpu

F32 = jnp.float32
BF16 = jnp.bfloat16
MESH = pl.DeviceIdType.MESH

D_MODEL = 1024
D_FF = 2816
FF_SHARD = 2 * D_FF // 4
HEAD_DIM = 64
N_Q_HEADS = 8
ATT_BLOCK = 128
N_BUCKETS = 32
MAX_DISTANCE = 128
REC_HEADS = 4
REC_DIM = 128
PLE_DIM = 256
EPS = 1e-6
IN_W = 4864
COL_AQ, COL_AK, COL_AV, COL_RQ, COL_RF, COL_RI, COL_RG, COL_GA, COL_GB = 0, 4, 5, 6, 10, 14, 18, 22, 30

CHUNK = 64
SUB = 8
N_SUB = CHUNK // SUB
HGRN_PAIR = 2

ADAM_LR, ADAM_B1, ADAM_B2, ADAM_EPS, ADAM_WD, ADAM_STEP = 0.001, 0.9, 0.999, 1e-08, 0.01, 10

V7X_VMEM_LIMIT = 56 * 1024 * 1024
N_CHIPS = 4
N_DEV = 8

BIG = ("w_ffn1_in", "w_ffn1_out", "w_in", "w_att_proj", "w_rec_proj", "w_out",
       "w_ffn2_in", "w_ffn2_out", "w_ple_gate", "w_ple_proj")
COL_SHARDED = ("w_ffn1_in", "w_in", "w_att_proj", "w_rec_proj", "w_ffn2_in", "w_ple_proj")
WEIGHTS = ("rel_bias", "lb_param", "norm_ffn1", "w_ffn1_in", "w_ffn1_out", "norm_mix", "w_in", "attn_sinks",
           "rec_norm", "w_att_proj", "w_rec_proj", "w_out", "norm_ffn2", "w_ffn2_in", "w_ffn2_out", "norm_ple",
           "w_ple_gate", "w_ple_proj", "norm_final")
SMALL = tuple(n for n in WEIGHTS if n not in BIG)
SMALL_ROWS = 64


def _params(*sem):
    return pltpu.CompilerParams(dimension_semantics=sem, vmem_limit_bytes=V7X_VMEM_LIMIT)


def _pick(n, cap, mult=8):
    if n <= cap:
        return n
    for t in range(cap - cap % mult, 0, -mult):
        if n % t == 0:
            return t
    raise ValueError((n, cap, mult))


def _dot(a, b):
    return jnp.dot(a, b, preferred_element_type=F32)


def _dot_nt(a, b):
    return lax.dot_general(a, b, (((1,), (1,)), ((), ())), preferred_element_type=F32)


def _dot_tn(a, b):
    return lax.dot_general(a, b, (((0,), (0,)), ((), ())), preferred_element_type=F32)


def _split3(x):
    hi = x.astype(BF16)
    r = x - hi.astype(F32)
    mid = r.astype(BF16)
    lo = (r - mid.astype(F32)).astype(BF16)
    return hi, mid, lo


def _split2(x):
    hi = x.astype(BF16)
    return hi, (x - hi.astype(F32)).astype(BF16)


def _sel_left(sel_bf16, x):
    hi, mid, lo = _split3(x)
    return _dot(sel_bf16, hi) + _dot(sel_bf16, mid) + _dot(sel_bf16, lo)


def _sel_right(x, sel_bf16):
    hi, mid, lo = _split3(x)
    return _dot(hi, sel_bf16) + _dot(mid, sel_bf16) + _dot(lo, sel_bf16)


def _sigmoid(x):
    return 0.5 * jnp.tanh(0.5 * x) + 0.5


def _group8(x):
    r, w = x.shape
    return x.reshape(r // 8, 8, w).sum(axis=0)


class _Comm:
    def __init__(self, ins, out_shapes, n_sems, start, finish):
        self.ins, self.out_shapes, self.n_sems, self.start, self.finish = ins, out_shapes, n_sems, start, finish


ANY = pl.BlockSpec(memory_space=pl.ANY)


def _comm_parts(comm):
    if comm is None:
        return [], [], [], []
    sems = [pltpu.SemaphoreType.DMA((comm.n_sems,)), pltpu.SemaphoreType.DMA((comm.n_sems,))]
    return list(comm.ins), [ANY] * len(comm.ins), list(comm.out_shapes), sems


def _comm_run(comm, grid, refs, n_in, n_out):
    if comm is None:
        return (lambda: None), (lambda: None)
    nci, nco = len(comm.ins), len(comm.out_shapes)
    cin = refs[n_in:n_in + nci]
    cout = refs[n_in + nci + n_out:n_in + nci + n_out + nco]
    send_sems, recv_sems = refs[-2], refs[-1]
    ids = [pl.program_id(d) for d in range(len(grid))]
    is_first = ids[0] == 0
    is_last = ids[0] == grid[0] - 1
    for d in range(1, len(grid)):
        is_first = is_first & (ids[d] == 0)
        is_last = is_last & (ids[d] == grid[d] - 1)

    def first():
        @pl.when(is_first)
        def _():
            comm.start(cin, cout, send_sems, recv_sems)

    def last():
        @pl.when(is_last)
        def _():
            comm.finish(cin, cout, send_sems, recv_sems)

    return first, last


def _call(name, fn, grid, ins, outs, pairs=(), comm=None, j_outer=False):
    in_pair = {i for p in pairs for i in p[:2]}
    n_in, n_out = len(ins), len(outs)
    c_arrays, c_in_specs, c_out_shapes, c_sems = _comm_parts(comm)

    def body(*refs):
        first, last = _comm_run(comm, grid, refs, n_in, n_out)
        first()
        accs = []
        for ia, ib, kind in pairs:
            a, b = refs[ia][...].astype(BF16), refs[ib][...].astype(BF16)
            accs.append(_dot(a, b) if kind == "nn" else _dot_nt(a, b))
        vals = [refs[i][...] for i in range(n_in) if i not in in_pair]
        res = fn(accs, vals)
        out_refs = refs[n_in + len(c_arrays):n_in + len(c_arrays) + n_out]
        assert len(res) == len(out_refs), (name, len(res), len(out_refs))
        for o_ref, val in zip(out_refs, res):
            o_ref[...] = val.astype(o_ref.dtype)
        last()

    if j_outer:
        grid = (grid[1], grid[0])
        swap = lambda im: (lambda j, i: im(i, j))
        ins = [(a, blk, swap(im)) for a, blk, im in ins]
        outs = [(shp, dt, blk, swap(im)) for shp, dt, blk, im in outs]

    return pl.pallas_call(
        body, name=name, grid=grid,
        in_specs=[pl.BlockSpec(blk, im) for _, blk, im in ins] + c_in_specs,
        out_specs=[pl.BlockSpec(blk, im) for _, _, blk, im in outs] + [ANY] * len(c_out_shapes),
        out_shape=[jax.ShapeDtypeStruct(shp, dt) for shp, dt, _, _ in outs] + c_out_shapes,
        scratch_shapes=c_sems,
        compiler_params=_params(*(["arbitrary"] * len(grid))))(*[a for a, _, _ in ins], *c_arrays)


def _tile_call(name, fn, M, N, tm, tn, *, pairs=(), tiles=(), consts=(), outs=(), parts=0, comm=None,
               j_outer=False):
    gi, gj = M // tm, N // tn
    assert gi * tm == M and gj * tn == N, (name, M, N, tm, tn)
    ins, prs = [], []
    for a, a_col, b, kind in pairs:
        K = b.shape[0] if kind == "nn" else b.shape[1]
        ins.append((a, (tm, K), lambda i, j, c=a_col: (i, c)))
        if kind == "nn":
            ins.append((b, (K, tn), lambda i, j: (0, j)))
        else:
            ins.append((b, (tn, K), lambda i, j: (j, 0)))
        prs.append((len(ins) - 2, len(ins) - 1, kind))
    for arr, off in tiles:
        ins.append((arr, (tm, tn), lambda i, j, o=off: (i, j + o)))
    for arr in consts:
        ins.append((arr, arr.shape, lambda i, j: (0, 0)))
    out_l = [((M, N), dt, (tm, tn), lambda i, j: (i, j)) for dt in outs]
    out_l += [((gi * 8, N), F32, (8, tn), lambda i, j: (i, j))] * parts
    nt = len(tiles)

    def wrapped(accs, vals):
        return fn(accs, vals[:nt], vals[nt:])

    return _call(name, wrapped, (gi, gj), ins, out_l, prs, comm=comm, j_outer=j_outer)


def _mm_tn(name, grid, a_in, b_in, outs):
    nk = grid[2]
    tm = [d for d in a_in[1] if d is not None][1]
    tn = [d for d in b_in[1] if d is not None][1]

    def body(a_ref, b_ref, *rest):
        out_refs, acc_ref = rest[:-1], rest[-1]
        k = pl.program_id(2)

        @pl.when(k == 0)
        def _():
            acc_ref[...] = jnp.zeros_like(acc_ref)

        acc_ref[...] += _dot_tn(a_ref[...].astype(BF16), b_ref[...].astype(BF16))

        @pl.when(k == nk - 1)
        def _():
            for o_ref in out_refs:
                if len(o_ref.shape) == 3:
                    n = o_ref.shape[2]
                    for s in range(o_ref.shape[0]):
                        o_ref[s] = acc_ref[:, n * s:n * (s + 1)].astype(o_ref.dtype)
                else:
                    o_ref[...] = acc_ref[...].astype(o_ref.dtype)

    return pl.pallas_call(
        body, name=name, grid=grid,
        in_specs=[pl.BlockSpec(a_in[1], a_in[2]), pl.BlockSpec(b_in[1], b_in[2])],
        out_specs=[pl.BlockSpec(blk, im) for _, _, blk, im in outs],
        out_shape=[jax.ShapeDtypeStruct(shp, dt) for shp, dt, _, _ in outs],
        scratch_shapes=[pltpu.VMEM((tm, tn), F32)],
        compiler_params=_params("arbitrary", "arbitrary", "arbitrary"))(a_in[0], b_in[0])


def _grad_pair(shape, block, imap):
    return [(shape, F32, block, imap), (shape, BF16, block, imap)]


def _mm_tn_rows(name, a, b, tk=2048):
    T, a_w = a.shape
    b_w = b.shape[1]
    tm = _pick(a_w, 1408, 128)
    tk = _pick(T, tk, 128)
    g32, g16 = _mm_tn(name, (a_w // tm, 1, T // tk),
                      (a, (tk, tm), lambda i, j, k: (k, i)), (b, (tk, b_w), lambda i, j, k: (k, 0)),
                      _grad_pair((a_w, b_w), (tm, b_w), lambda i, j, k: (i, 0)))
    shp = (N_CHIPS, a_w // N_CHIPS, b_w)
    return g32.reshape(shp), g16.reshape(shp)


def _mm_tn_cols(name, a, b, tk=4096):
    T, a_w = a.shape
    b_w = b.shape[1]
    tk = _pick(T, tk, 128)
    shp = (N_CHIPS, a_w, b_w // N_CHIPS)
    return _mm_tn(name, (1, 1, T // tk),
                  (a, (tk, a_w), lambda i, j, k: (k, 0)), (b, (tk, b_w), lambda i, j, k: (k, 0)),
                  _grad_pair(shp, shp, lambda i, j, k: (0, 0, 0)))


def _colsum(name, x):
    def body(x_ref, o_ref):
        o_ref[...] = jnp.sum(x_ref[...], axis=0, keepdims=True)
    return pl.pallas_call(body, name=name, out_shape=jax.ShapeDtypeStruct((1, x.shape[1]), F32))(x)


def _rms_hat(h):
    return h * lax.rsqrt(jnp.mean(h * h, axis=-1, keepdims=True) + EPS)


def _rms_bwd_vals(dn, h, g):
    r = lax.rsqrt(jnp.mean(h * h, axis=-1, keepdims=True) + EPS)
    nh = h * r
    gd = dn * g
    dh = r * (gd - nh * jnp.mean(gd * nh, axis=-1, keepdims=True))
    return dh, _group8(dn * nh)


def _rms_fwd(name, h, g, tm=512, comm=None):
    T = h.shape[0]

    def fn(accs, tv, cv):
        return [_rms_hat(tv[0]) * cv[0]]

    return _tile_call(name, fn, T, D_MODEL, _pick(T, tm), D_MODEL, tiles=[(h, 0)], consts=[g], outs=[BF16],
                      comm=comm)


def _ffn_fwd(tag, h, g, w_in, w_out, g_next, n=None, comm_norm=None, w_in_of=None, comm_in=None, comm_out=None,
             w_out_of=None):
    T = h.shape[0]
    if n is None:
        n, *got_norm = _rms_fwd(tag + "_norm", h, g, comm=comm_norm)
        if w_in_of is not None:
            w_in = w_in_of(got_norm)
    tm = _pick(T, 1024)
    wblk = (None, D_MODEL, FF_SHARD)

    def act(accs, vals):
        gate, up = accs
        return [gate, up, gate * _sigmoid(gate) * up]

    tile = lambda: ((T, D_FF), BF16, (tm, FF_SHARD), lambda i, j: (i, j))
    gate, up, a, *got_in = _call(
        tag + "_in", act, (T // tm, 2),
        [(n, (tm, D_MODEL), lambda i, j: (i, 0)),
         (w_in, wblk, lambda i, j: (j, 0, 0)), (w_in, wblk, lambda i, j: (j + 2, 0, 0))],
        [tile(), tile(), tile()], pairs=[(0, 1, "nn"), (0, 2, "nn")], comm=comm_in, j_outer=True)

    def res(accs, tv, cv):
        h_new = tv[0] + 0.5 * accs[0]
        return [h_new, _rms_hat(h_new) * cv[0]]

    if w_out_of is not None:
        w_out = w_out_of(got_in)
    h_new, n_next, *got_out = _tile_call(
        tag + "_out", res, T, D_MODEL, _pick(T, 512), D_MODEL, pairs=[(a, 0, w_out, "nn")], tiles=[(h, 0)],
        consts=[g_next], outs=[F32, BF16], comm=comm_out)
    return h_new, n_next, (n, gate, up, a), got_out


def _ffn_bwd(tag, dh_out, df, h, g, w_in, w_out, saved, comm=None, comm_last=None):
    T = h.shape[0]
    n, gate, up, a = saved
    tm = _pick(T, 512)

    def dact(accs, vals):
        da = accs[0]
        gt, u = vals[0].astype(F32), vals[1].astype(F32)
        sg = _sigmoid(gt)
        silu = gt * sg
        return [jnp.stack([(da * u * (sg + silu * (1.0 - sg))).astype(BF16), (da * silu).astype(BF16)])]

    dz, *got = _call(
        tag + "_dact", dact, (T // tm, 2),
        [(df, (tm, D_MODEL), lambda i, j: (i, 0)), (w_out, (FF_SHARD, D_MODEL), lambda i, j: (j, 0)),
         (gate, (tm, FF_SHARD), lambda i, j: (i, j)), (up, (tm, FF_SHARD), lambda i, j: (i, j))],
        [((2, T, D_FF), BF16, (2, tm, FF_SHARD), lambda i, j: (0, i, j))], pairs=[(0, 1, "nt")], comm=comm,
        j_outer=True)
    dw_out = _mm_tn_rows(tag + "_dwout", a, df)
    tk = _pick(T, 2048, 128)
    dw_in = _mm_tn(tag + "_dwin", (1, N_CHIPS, T // tk),
                   (n, (tk, D_MODEL), lambda i, j, k: (k, 0)),
                   (dz, (None, tk, FF_SHARD), lambda i, j, k: (j // 2, k, j % 2)),
                   _grad_pair((N_CHIPS, D_MODEL, FF_SHARD), (None, D_MODEL, FF_SHARD), lambda i, j, k: (j, 0, 0)))

    def dnorm(accs, vals):
        dn = accs[0] + accs[1] + accs[2] + accs[3]
        dh, dg = _rms_bwd_vals(dn, vals[0], vals[2])
        dh = vals[1] + dh
        return [dh, dh, dg]

    tm2 = _pick(T, 512)
    ins = [(dz, (None, tm2, FF_SHARD), lambda i, j, s=s: (s // 2, i, s % 2)) for s in range(N_CHIPS)]
    ins += [(w_in, (None, D_MODEL, FF_SHARD), lambda i, j, s=s: (s, 0, 0)) for s in range(N_CHIPS)]
    ins += [(h, (tm2, D_MODEL), lambda i, j: (i, 0)), (dh_out, (tm2, D_MODEL), lambda i, j: (i, 0)),
            (g, g.shape, lambda i, j: (0, 0))]
    dh, dh16, dg, *got_last = _call(
        tag + "_dnorm", dnorm, (T // tm2, 1), ins,
        [((T, D_MODEL), F32, (tm2, D_MODEL), lambda i, j: (i, 0)),
         ((T, D_MODEL), BF16, (tm2, D_MODEL), lambda i, j: (i, 0)),
         ((T // tm2 * 8, D_MODEL), F32, (8, D_MODEL), lambda i, j: (i, 0))],
        pairs=[(s, N_CHIPS + s, "nt") for s in range(N_CHIPS)],
        comm=None if comm_last is None else comm_last(dw_in, dw_out))
    return dh, dh16, dg, dw_in, dw_out, got, got_last


def _t5_onehot():
    qi = np.arange(ATT_BLOCK)[:, None] + ATT_BLOCK
    kj = np.arange(2 * ATT_BLOCK)[None, :]
    nn = np.maximum(qi - kj, 0)
    max_exact = N_BUCKETS // 2
    large = max_exact + (np.log(np.maximum(nn, 1) / max_exact) / np.log(MAX_DISTANCE / max_exact)
                         * (N_BUCKETS - max_exact)).astype(np.int32)
    large = np.minimum(large, N_BUCKETS - 1)
    bucket = np.where(nn < max_exact, nn, large).astype(np.int32).reshape(-1)
    return (bucket[None, :] == np.arange(N_BUCKETS)[:, None]).astype(np.float32)


def _small_mm(name, a, b, sel):
    def body(a_ref, b_ref, o_ref):
        if sel == "right":
            o_ref[...] = _sel_right(a_ref[...], b_ref[...])
        else:
            o_ref[...] = _sel_left(a_ref[...], b_ref[...])
    return pl.pallas_call(body, name=name, out_shape=jax.ShapeDtypeStruct((a.shape[0], b.shape[1]), F32),
                          compiler_params=pltpu.CompilerParams(vmem_limit_bytes=V7X_VMEM_LIMIT))(a, b)


def _dup_heads(t):
    a, b = t[:, :HEAD_DIM], t[:, HEAD_DIM:]
    return jnp.concatenate([a, a, b, b], axis=1)


def _kv_layouts(proj):
    T = proj.shape[0]

    def fn(accs, tv, cv):
        return [tv[0], tv[1]]

    k, v = _tile_call("kv_cast", fn, T, 128, _pick(T, 1024), 128, tiles=[(proj, COL_AK), (proj, COL_AV)],
                      outs=[BF16, BF16])
    return _dup_heads(k), _dup_heads(v)


def _swa_masks():
    row = lax.broadcasted_iota(jnp.int32, (ATT_BLOCK, 2 * ATT_BLOCK), 0)
    col = lax.broadcasted_iota(jnp.int32, (ATT_BLOCK, 2 * ATT_BLOCK), 1)
    dist = ATT_BLOCK + row - col
    return (dist >= 0) & (dist < ATT_BLOCK), col


GROUP = 4


def _stack_group(blk, lo_q):
    zero = jnp.zeros_like(blk[:, :128])
    rows = []
    for pair in range(GROUP // 2):
        pb = blk[:, 128 * pair:128 * (pair + 1)]
        rows += [jnp.where(lo_q, pb, zero), jnp.where(lo_q, zero, pb)]
    return jnp.concatenate(rows, axis=0)


def _unstack_group(st, lo_q):
    pairs = [jnp.where(lo_q, st[256 * pair:256 * pair + 128], st[256 * pair + 128:256 * (pair + 1)])
             for pair in range(GROUP // 2)]
    return jnp.concatenate(pairs, axis=1)


def _swa_probs(s, bias_h, sink, valid):
    s = jnp.where(valid, s * (HEAD_DIM ** -0.5) + bias_h, -jnp.inf)
    m = jnp.maximum(jnp.max(s, axis=-1, keepdims=True), sink)
    e = jnp.exp(s - m)
    es = jnp.exp(sink - m)
    den = jnp.sum(e, axis=-1, keepdims=True) + es
    return e / den, es / den


def _swa_fwd(proj, kk2, vv2, bias, sinks, B, S):
    T = B * S
    nb = S // ATT_BLOCK

    def body(q_ref, k_ref, v_ref, bias_ref, sink_ref, o_ref, kpad, vpad):
        zeros = jnp.zeros((ATT_BLOCK, 256), BF16)
        kpad[pl.ds(0, ATT_BLOCK), :] = zeros
        vpad[pl.ds(0, ATT_BLOCK), :] = zeros
        kpad[pl.ds(ATT_BLOCK, S), :] = k_ref[...]
        vpad[pl.ds(ATT_BLOCK, S), :] = v_ref[...]
        valid0, col = _swa_masks()
        lo_q = lax.broadcasted_iota(jnp.int32, (1, 128), 1) < HEAD_DIM

        def blk(n, carry):
            r0 = pl.multiple_of(n * ATT_BLOCK, ATT_BLOCK)
            rows = pl.ds(r0, ATT_BLOCK)
            valid = valid0 & ((n > 0) | (col >= ATT_BLOCK))
            for g in range(N_Q_HEADS // GROUP):
                lanes = pl.ds(128 * g, 128)
                kg = kpad[pl.ds(r0, 2 * ATT_BLOCK), lanes]
                vg = vpad[pl.ds(r0, 2 * ATT_BLOCK), lanes]
                qm = _stack_group(q_ref[rows, pl.ds(256 * g, 256)].astype(BF16), lo_q)
                s = _dot_nt(qm, kg)
                ps = []
                for i in range(GROUP):
                    h = GROUP * g + i
                    p, _ = _swa_probs(s[ATT_BLOCK * i:ATT_BLOCK * (i + 1)], bias_ref[h], sink_ref[h], valid)
                    ps.append(p.astype(BF16))
                o = _dot(jnp.concatenate(ps, axis=0), vg)
                o_ref[rows, pl.ds(256 * g, 256)] = _unstack_group(o, lo_q).astype(o_ref.dtype)
            return carry

        if nb % 2 == 0:
            lax.fori_loop(0, nb // 2, lambda i, c: blk(2 * i + 1, blk(2 * i, c)), 0)
        else:
            lax.fori_loop(0, nb, blk, 0)

    return pl.pallas_call(
        body, name="swa_fwd", grid=(B,),
        in_specs=[pl.BlockSpec((S, 512), lambda b: (b, 0)),
                  pl.BlockSpec((S, 256), lambda b: (b, 0)),
                  pl.BlockSpec((S, 256), lambda b: (b, 0)),
                  pl.BlockSpec((N_Q_HEADS, ATT_BLOCK, 2 * ATT_BLOCK), lambda b: (0, 0, 0)),
                  pl.BlockSpec(memory_space=pltpu.SMEM)],
        out_specs=pl.BlockSpec((S, 512), lambda b: (b, 0)),
        out_shape=jax.ShapeDtypeStruct((T, 512), BF16),
        scratch_shapes=[pltpu.VMEM((S + ATT_BLOCK, 256), BF16), pltpu.VMEM((S + ATT_BLOCK, 256), BF16)],
        compiler_params=_params("arbitrary"))(proj, kk2, vv2, bias, sinks)


def _swa_bwd(proj, kk2, vv2, bias, sinks, datt, B, S):
    T = B * S
    nb = S // ATT_BLOCK

    def body(q_ref, k_ref, v_ref, bias_ref, sink_ref, do_ref, dq_ref, dk_ref, dv_ref, dbias_ref, dsink_ref,
             kpad, vpad, dkpad, dvpad):
        b = pl.program_id(0)

        @pl.when(b == 0)
        def _():
            dbias_ref[...] = jnp.zeros_like(dbias_ref)
            dsink_ref[...] = jnp.zeros_like(dsink_ref)

        zeros = jnp.zeros((ATT_BLOCK, 256), BF16)
        kpad[pl.ds(0, ATT_BLOCK), :] = zeros
        vpad[pl.ds(0, ATT_BLOCK), :] = zeros
        kpad[pl.ds(ATT_BLOCK, S), :] = k_ref[...]
        vpad[pl.ds(ATT_BLOCK, S), :] = v_ref[...]
        dkpad[...] = jnp.zeros_like(dkpad)
        dvpad[...] = jnp.zeros_like(dvpad)
        valid0, col = _swa_masks()
        lo_q = lax.broadcasted_iota(jnp.int32, (1, 128), 1) < HEAD_DIM
        scale = HEAD_DIM ** -0.5

        def blk(n, carry):
            r0 = pl.multiple_of(n * ATT_BLOCK, ATT_BLOCK)
            rows = pl.ds(r0, ATT_BLOCK)
            band = pl.ds(r0, 2 * ATT_BLOCK)
            valid = valid0 & ((n > 0) | (col >= ATT_BLOCK))
            for g in range(N_Q_HEADS // GROUP):
                lanes = pl.ds(128 * g, 128)
                kg = kpad[band, lanes]
                vg = vpad[band, lanes]
                qm = _stack_group(q_ref[rows, pl.ds(256 * g, 256)].astype(BF16), lo_q)
                dom = _stack_group(do_ref[rows, pl.ds(256 * g, 256)], lo_q)
                s = _dot_nt(qm, kg)
                dp = _dot_nt(dom, vg)
                pst, dst = [], []
                for i in range(GROUP):
                    h = GROUP * g + i
                    sl = slice(ATT_BLOCK * i, ATT_BLOCK * (i + 1))
                    p, ps = _swa_probs(s[sl], bias_ref[h], sink_ref[h], valid)
                    delta = jnp.sum(p * dp[sl], axis=-1, keepdims=True)
                    ds = p * (dp[sl] - delta)
                    dbias_ref[h] += ds
                    dsink_ref[pl.ds(h, 1), :] += -jnp.sum(jnp.broadcast_to(ps * delta, (ATT_BLOCK, 128)),
                                                          axis=0, keepdims=True)
                    pst.append(p.astype(BF16))
                    dst.append((ds * scale).astype(BF16))
                pst, dst = jnp.concatenate(pst, axis=0), jnp.concatenate(dst, axis=0)
                dq_ref[rows, pl.ds(256 * g, 256)] = _unstack_group(_dot(dst, kg), lo_q).astype(dq_ref.dtype)
                dkpad[band, lanes] += _dot_tn(dst, qm)
                dvpad[band, lanes] += _dot_tn(pst, dom)
            return carry

        if nb % 2 == 0:
            lax.fori_loop(0, nb // 2, lambda i, c: blk(2 * i + 1, blk(2 * i, c)), 0)
        else:
            lax.fori_loop(0, nb, blk, 0)
        lo_out = lax.broadcasted_iota(jnp.int32, (1, 128), 1) < HEAD_DIM

        def fold(pad_ref):
            halves = []
            for g in range(N_Q_HEADS // GROUP):
                t = pad_ref[pl.ds(ATT_BLOCK, S), pl.ds(128 * g, 128)]
                halves.append(t + pltpu.roll(t, HEAD_DIM, 1))
            return jnp.where(lo_out, halves[0], halves[1])

        dk_ref[...] = fold(dkpad).astype(dk_ref.dtype)
        dv_ref[...] = fold(dvpad).astype(dv_ref.dtype)

    return pl.pallas_call(
        body, name="swa_bwd", grid=(B,),
        in_specs=[pl.BlockSpec((S, 512), lambda b: (b, 0)),
                  pl.BlockSpec((S, 256), lambda b: (b, 0)),
                  pl.BlockSpec((S, 256), lambda b: (b, 0)),
                  pl.BlockSpec((N_Q_HEADS, ATT_BLOCK, 2 * ATT_BLOCK), lambda b: (0, 0, 0)),
                  pl.BlockSpec(memory_space=pltpu.SMEM),
                  pl.BlockSpec((S, 512), lambda b: (b, 0))],
        out_specs=[pl.BlockSpec((S, 512), lambda b: (b, 0)),
                   pl.BlockSpec((S, 128), lambda b: (b, 0)),
                   pl.BlockSpec((S, 128), lambda b: (b, 0)),
                   pl.BlockSpec((N_Q_HEADS, ATT_BLOCK, 2 * ATT_BLOCK), lambda b: (0, 0, 0)),
                   pl.BlockSpec((N_Q_HEADS, 128), lambda b: (0, 0))],
        out_shape=[jax.ShapeDtypeStruct((T, 512), BF16),
                   jax.ShapeDtypeStruct((T, 128), BF16),
                   jax.ShapeDtypeStruct((T, 128), BF16),
                   jax.ShapeDtypeStruct((N_Q_HEADS, ATT_BLOCK, 2 * ATT_BLOCK), F32),
                   jax.ShapeDtypeStruct((N_Q_HEADS, 128), F32)],
        scratch_shapes=[pltpu.VMEM((S + ATT_BLOCK, 256), BF16), pltpu.VMEM((S + ATT_BLOCK, 256), BF16),
                        pltpu.VMEM((S + ATT_BLOCK, 256), F32), pltpu.VMEM((S + ATT_BLOCK, 256), F32)],
        compiler_params=_params("arbitrary"))(proj, kk2, vv2, bias, sinks, datt)


def _hgrn_gates(z, lb):
    sg = _sigmoid(z)
    f = lb + (1.0 - lb) * sg
    return sg, f, jnp.log(f), 1.0 - f


def _hgrn_consts():
    r = lax.broadcasted_iota(jnp.int32, (CHUNK, CHUNK), 0)
    c = lax.broadcasted_iota(jnp.int32, (CHUNK, CHUNK), 1)
    tril = (r >= c).astype(BF16)
    triu = (r <= c).astype(BF16)
    causal = r >= c
    below = (r // SUB) > (c // SUB)
    inside = ((r // SUB) == (c // SUB)) & causal
    return tril, triu, causal, below, inside, c


def _block_rows(ref, lanes, s):
    rows = []
    for i in range(N_SUB):
        if SUB * i + s < 0:
            rows.append(jnp.zeros((SUB, REC_DIM), F32))
        else:
            rows.append(jnp.broadcast_to(ref[pl.ds(SUB * i + s, 1), lanes], (SUB, REC_DIM)))
    return jnp.concatenate(rows, axis=0)


def _hgrn_offdiag(q, k, bcum, b_ref, lanes):
    eq = jnp.exp(jnp.minimum(bcum - _block_rows(b_ref, lanes, -1), 0.0))
    qe = q * eq
    zero = jnp.zeros((SUB, REC_DIM), F32)
    q_rows, k_cols, eks = [jnp.zeros((SUB, (N_SUB - 1) * REC_DIM), F32)], [], []
    for i in range(1, N_SUB):
        q_rows.append(jnp.concatenate([zero] * (i - 1) + [qe[SUB * i:SUB * (i + 1), :]] + [zero] * (N_SUB - 1 - i),
                                      axis=1))
        p = b_ref[pl.ds(SUB * i - 1, 1), lanes]
        pad = jnp.zeros((CHUNK - SUB * i, REC_DIM), F32)
        ek = jnp.concatenate([jnp.exp(p - b_ref[pl.ds(0, SUB * i), lanes]), pad], axis=0)
        k_cols.append(k * ek)
        eks.append(ek)
    return jnp.concatenate(q_rows, axis=0), jnp.concatenate(k_cols, axis=1), eq, eks


def _hgrn_fwd(proj, lb_param, B, S, comm=None):
    T = B * S
    nc = S // CHUNK
    fwd_unroll = 4 if nc % 4 == 0 else 2
    c_arrays, c_in_specs, c_out_shapes, c_sems = _comm_parts(comm)
    nci, nco = len(c_arrays), len(c_out_shapes)

    def body(*refs):
        q_ref, z_ref, v_ref, lb_ref = refs[:4]
        o_ref, st_ref = refs[4 + nci:6 + nci]
        k_slots, b_slots = refs[6 + nci + nco:8 + nci + nco]
        comm_first, comm_last = _comm_run(comm, (B, REC_HEADS // HGRN_PAIR), refs, 4, 2)
        comm_first()
        tril, _, _, below, inside, col = _hgrn_consts()
        col_s = col & (SUB - 1)

        def chunk(ci, hts, slot):
            k_s, b_s = k_slots.at[slot], b_slots.at[slot]
            r0 = pl.multiple_of(ci * CHUNK, CHUNK)
            lb = _sigmoid(lb_ref[0:1, :] - lb_ref[1:2, :])
            _, _, g_all, k_all = _hgrn_gates(z_ref[pl.ds(r0, CHUNK), :], lb)
            b_all = _sel_left(tril, g_all)
            k_s[...] = k_all
            b_s[...] = b_all
            new = []
            for e, ht in enumerate(hts):
                lanes = pl.ds(REC_DIM * e, REC_DIM)
                cols = slice(REC_DIM * e, REC_DIM * (e + 1))
                q = q_ref[pl.ds(r0, CHUNK), lanes]
                v = v_ref[pl.ds(r0, CHUNK), lanes]
                k, bcum = k_all[:, cols], b_all[:, cols]
                st_ref[e * nc + ci] = ht
                qst, kst, _, _ = _hgrn_offdiag(q, k, bcum, b_s, lanes)
                d = jnp.zeros((CHUNK, CHUNK), F32)
                for s in range(SUB):
                    w = jnp.exp(jnp.minimum(bcum - _block_rows(b_s, lanes, s), 0.0))
                    colv = jnp.sum(q * _block_rows(k_s, lanes, s) * w, axis=-1, keepdims=True)
                    d = jnp.where(col_s == s, colv, d)
                a = jnp.where(below, _dot_nt(qst.astype(BF16), kst.astype(BF16)), 0.0) + jnp.where(inside, d, 0.0)
                vb = v.astype(BF16)
                qb = (q * jnp.exp(bcum)).astype(BF16)
                o_ref[pl.ds(r0, CHUNK), lanes] = _dot(a.astype(BF16), vb) + _dot_nt(qb, ht.astype(BF16))
                b_last = b_s[pl.ds(CHUNK - 1, 1), lanes]
                kb = (k * jnp.exp(b_last - bcum)).astype(BF16)
                new.append(ht * jnp.exp(b_last) + _dot_tn(vb, kb))
            return tuple(new)

        def trip(i, hts):
            for u in range(fwd_unroll):
                hts = chunk(fwd_unroll * i + u, hts, u)
            return hts

        lax.fori_loop(0, nc // fwd_unroll, trip, tuple(jnp.zeros((REC_DIM, REC_DIM), F32) for _ in range(HGRN_PAIR)))
        comm_last()

    hp, wd = REC_HEADS // HGRN_PAIR, HGRN_PAIR * REC_DIM
    cq, cf, ci_ = (c * REC_DIM // wd for c in (COL_RQ, COL_RF, COL_RI))
    return pl.pallas_call(
        body, name="hgrn_fwd", grid=(B, hp),
        in_specs=[pl.BlockSpec((S, wd), lambda b, h: (b, cq + h)),
                  pl.BlockSpec((S, wd), lambda b, h: (b, cf + h)),
                  pl.BlockSpec((S, wd), lambda b, h: (b, ci_ + h)),
                  pl.BlockSpec((2, wd), lambda b, h: (0, h))] + c_in_specs,
        out_specs=[pl.BlockSpec((S, wd), lambda b, h: (b, h)),
                   pl.BlockSpec((HGRN_PAIR * nc, REC_DIM, REC_DIM), lambda b, h: (b * hp + h, 0, 0))] + [ANY] * nco,
        out_shape=[jax.ShapeDtypeStruct((T, 512), F32),
                   jax.ShapeDtypeStruct((B * REC_HEADS * nc, REC_DIM, REC_DIM), F32)] + c_out_shapes,
        scratch_shapes=[pltpu.VMEM((fwd_unroll, CHUNK, wd), F32), pltpu.VMEM((fwd_unroll, CHUNK, wd), F32)] + c_sems,
        compiler_params=_params("arbitrary", "arbitrary"))(proj, proj, proj, lb_param, *c_arrays)


def _hgrn_bwd(proj, lb_param, states, do, B, S, comm=None):
    T = B * S
    nc = S // CHUNK
    bwd_unroll = 4 if nc % 4 == 0 else 2
    c_arrays, c_in_specs, c_out_shapes, c_sems = _comm_parts(comm)
    nci, nco = len(c_arrays), len(c_out_shapes)

    def body(*refs):
        q_ref, z_ref, v_ref, lb_ref, st_ref, do_ref = refs[:6]
        dq_ref, dz_ref, dv_ref, dlb_ref = refs[6 + nci:10 + nci]
        slots = refs[10 + nci + nco:14 + nci + nco]
        comm_first, comm_last = _comm_run(comm, (B, REC_HEADS // HGRN_PAIR), refs, 6, 4)
        comm_first()
        tril, triu, causal, below, inside, col = _hgrn_consts()
        col_s = col & (SUB - 1)
        last_row = lax.broadcasted_iota(jnp.int32, (CHUNK, 1), 0) == CHUNK - 1
        rc = lax.broadcasted_iota(jnp.int32, (CHUNK, SUB * REC_DIM), 0)
        lc = lax.broadcasted_iota(jnp.int32, (CHUNK, SUB * REC_DIM), 1)
        spread = ((rc & (SUB - 1)) == (lc // REC_DIM)).astype(BF16)
        rr = lax.broadcasted_iota(jnp.int32, (CHUNK, SUB * CHUNK), 0)
        cc = lax.broadcasted_iota(jnp.int32, (CHUNK, SUB * CHUNK), 1)
        gather = (((rr // SUB) == ((cc & (CHUNK - 1)) // SUB)) & ((rr & (SUB - 1)) == (cc // CHUNK))).astype(BF16)

        heads = range(HGRN_PAIR)
        cols = [slice(REC_DIM * e, REC_DIM * (e + 1)) for e in heads]
        lanes = [pl.ds(REC_DIM * e, REC_DIM) for e in heads]
        lane_cat = lambda vals: jnp.concatenate(vals, axis=1)
        row_cat = lambda vals: jnp.concatenate(vals, axis=0)

        def chunk(it, carry, slot):
            k_s, b_s, pc_hi, pc_lo = (r.at[slot] for r in slots)
            dhts, dlb = carry
            ci = nc - 1 - it
            r0 = pl.multiple_of(ci * CHUNK, CHUNK)
            rows = pl.ds(r0, CHUNK)
            lb = _sigmoid(lb_ref[0:1, :] - lb_ref[1:2, :])
            sg, f, g_all, k_all = _hgrn_gates(z_ref[rows, :], lb)
            b_all = _sel_left(tril, g_all)
            k_s[...] = k_all
            b_s[...] = b_all
            q_all = q_ref[rows, :]
            das, hd = [], []
            for e in heads:
                vb, dob = v_ref[rows, lanes[e]].astype(BF16), do_ref[rows, lanes[e]].astype(BF16)
                da = jnp.where(causal, _dot_nt(dob, vb), 0.0)
                das.append(jnp.where(inside, da, 0.0))
                hd.append((vb, dob, da))
            da_hi, da_lo = _split2(row_cat(das))
            da_in = _dot(da_hi, spread) + _dot(da_lo, spread)
            ds, dqs = [], []
            for e in heads:
                q, bcum = q_all[:, cols[e]], b_all[:, cols[e]]
                d = jnp.zeros((CHUNK, CHUNK), F32)
                dq = jnp.zeros((CHUNK, REC_DIM), F32)
                for s in range(SUB):
                    w = jnp.exp(jnp.minimum(bcum - _block_rows(b_s, lanes[e], s), 0.0))
                    ks = _block_rows(k_s, lanes[e], s)
                    qw = q * w
                    d = jnp.where(col_s == s, jnp.sum(qw * ks, axis=-1, keepdims=True), d)
                    da_s = da_in[CHUNK * e:CHUNK * (e + 1), REC_DIM * s:REC_DIM * (s + 1)]
                    dq = dq + da_s * ks * w
                    hi, lo = _split2(da_s * qw)
                    pc_hi[pl.ds(CHUNK * s, CHUNK), lanes[e]] = hi
                    pc_lo[pl.ds(CHUNK * s, CHUNK), lanes[e]] = lo
                ds.append(d)
                dqs.append(dq)
            dk_in = _dot(gather, pc_hi[...]) + _dot(gather, pc_lo[...])
            dq_out, dk_out, dv_out, db_out, new_dhts = [], [], [], [], []
            for e in heads:
                q, k, bcum = q_all[:, cols[e]], k_all[:, cols[e]], b_all[:, cols[e]]
                vb, dob, da = hd[e]
                dht, ht = dhts[e], st_ref[e * nc + ci]
                qst, kst, eq, eks = _hgrn_offdiag(q, k, bcum, b_s, lanes[e])
                qst_b, kst_b = qst.astype(BF16), kst.astype(BF16)
                a = jnp.where(below, _dot_nt(qst_b, kst_b), 0.0) + jnp.where(inside, ds[e], 0.0)
                da_off = jnp.where(below, da, 0.0).astype(BF16)
                dqst = _dot(da_off, kst_b)
                dkst = _dot_tn(da_off, qst_b)
                dk = dk_in[:, cols[e]]
                dq_rows = [jnp.zeros((SUB, REC_DIM), F32)]
                for i in range(1, N_SUB):
                    dq_rows.append(dqst[SUB * i:SUB * (i + 1), REC_DIM * (i - 1):REC_DIM * i])
                    dk = dk + dkst[:, REC_DIM * (i - 1):REC_DIM * i] * eks[i - 1]
                dq = dqs[e] + row_cat(dq_rows) * eq
                eb = jnp.exp(bcum)
                b_last = b_s[pl.ds(CHUNK - 1, 1), lanes[e]]
                el = jnp.exp(b_last)
                ekb = jnp.exp(b_last - bcum)
                qb = (q * eb).astype(BF16)
                kb = k * ekb
                dhb = dht.astype(BF16)
                dv_out.append(_dot_tn(a.astype(BF16), dob) + _dot_nt(kb.astype(BF16), dhb))
                dqb = _dot(dob, ht.astype(BF16))
                dkb = _dot(vb, dhb)
                new_dhts.append(dht * el + _dot_tn(dob, qb))
                dq = dq + eb * dqb
                dk = dk + ekb * dkb
                edge = jnp.sum(kb * dkb, axis=0, keepdims=True) + el * jnp.sum(ht * dht, axis=0, keepdims=True)
                db_out.append(q * dq - k * dk + jnp.where(last_row, edge, 0.0))
                dq_out.append(dq)
                dk_out.append(dk)
            dk_all = lane_cat(dk_out)
            db_hi, db_lo = _split2(lane_cat(db_out))
            dg = _dot(triu, db_hi) + _dot(triu, db_lo)
            df = dg / f - dk_all
            dz_ref[rows, :] = (df * (1.0 - lb) * sg * (1.0 - sg)).astype(dz_ref.dtype)
            dq_ref[rows, :] = lane_cat(dq_out).astype(dq_ref.dtype)
            dv_ref[rows, :] = lane_cat(dv_out).astype(dv_ref.dtype)
            return tuple(new_dhts), dlb + jnp.sum(df * (1.0 - sg), axis=0, keepdims=True)

        zero = (tuple(jnp.zeros((REC_DIM, REC_DIM), F32) for _ in heads), jnp.zeros((1, HGRN_PAIR * REC_DIM), F32))
        def trip(i, carry):
            for u in range(bwd_unroll):
                carry = chunk(bwd_unroll * i + u, carry, u)
            return carry

        _, dlb = lax.fori_loop(0, nc // bwd_unroll, trip, zero)
        lb = _sigmoid(lb_ref[0:1, :] - lb_ref[1:2, :])
        dlb_ref[...] = jnp.broadcast_to(dlb * lb * (1.0 - lb), (8, HGRN_PAIR * REC_DIM))
        comm_last()

    hp, wd = REC_HEADS // HGRN_PAIR, HGRN_PAIR * REC_DIM
    cq, cf, ci_ = (c * REC_DIM // wd for c in (COL_RQ, COL_RF, COL_RI))
    return pl.pallas_call(
        body, name="hgrn_bwd", grid=(B, hp),
        in_specs=[pl.BlockSpec((S, wd), lambda b, h: (b, cq + h)),
                  pl.BlockSpec((S, wd), lambda b, h: (b, cf + h)),
                  pl.BlockSpec((S, wd), lambda b, h: (b, ci_ + h)),
                  pl.BlockSpec((2, wd), lambda b, h: (0, h)),
                  pl.BlockSpec((HGRN_PAIR * nc, REC_DIM, REC_DIM), lambda b, h: (b * hp + h, 0, 0)),
                  pl.BlockSpec((S, wd), lambda b, h: (b, h))] + c_in_specs,
        out_specs=[pl.BlockSpec((S, wd), lambda b, h: (b, h))] * 3
        + [pl.BlockSpec((8, wd), lambda b, h: (b, h))] + [ANY] * nco,
        out_shape=[jax.ShapeDtypeStruct((T, 512), BF16)] * 3 + [jax.ShapeDtypeStruct((B * 8, 512), F32)]
        + c_out_shapes,
        scratch_shapes=[pltpu.VMEM((bwd_unroll, CHUNK, wd), F32)] * 2
        + [pltpu.VMEM((bwd_unroll, SUB * CHUNK, wd), BF16)] * 2 + c_sems,
        compiler_params=_params("arbitrary", "arbitrary"))(proj, proj, proj, lb_param, states, do, *c_arrays)


def _rec_gate_fwd(rec, proj, rec_norm):
    T = rec.shape[0]

    def fn(accs, tv, cv):
        return [_rms_hat(tv[0]) * cv[0] * _sigmoid(tv[1])]

    return _tile_call("rec_gate", fn, T, 512, _pick(T, 1024), REC_DIM, tiles=[(rec, 0), (proj, COL_RG)],
                      consts=[rec_norm], outs=[BF16])[0]


def _rec_gate_bwd(dyb, w_rec_proj, rec, proj, rec_norm):
    T = rec.shape[0]

    def fn(accs, tv, cv):
        d, r, rg = accs[0], tv[0], tv[1]
        sg = _sigmoid(rg)
        rn = _rms_hat(r) * cv[0]
        dh, dg = _rms_bwd_vals(d * sg, r, cv[0])
        return [dh, d * rn * sg * (1.0 - sg), dg]

    return _tile_call("rec_gate_bwd", fn, T, 512, _pick(T, 1024), REC_DIM, pairs=[(dyb, 0, w_rec_proj, "nt")],
                      tiles=[(rec, 0), (proj, COL_RG)], consts=[rec_norm], outs=[F32, BF16], parts=1)


def _mix_out_fwd(att, recn, proj, w_att_proj, w_rec_proj, w_out, h1, g_next):
    T = att.shape[0]
    tn = 256

    def merge(accs, tv, cv):
        ya, yb = accs
        return [ya, yb, _sigmoid(tv[0]) * ya + _sigmoid(tv[1]) * yb]

    ya, yb, merged = _tile_call(
        "merge", merge, T, D_MODEL, _pick(T, 1024), tn,
        pairs=[(att, 0, w_att_proj, "nn"), (recn, 0, w_rec_proj, "nn")],
        tiles=[(proj, COL_GA * 128 // tn), (proj, COL_GB * 128 // tn)], outs=[BF16] * 3)

    def res(accs, tv, cv):
        h2 = tv[0] + accs[0]
        return [h2, _rms_hat(h2) * cv[0]]

    h2, n2 = _tile_call("mix_out", res, T, D_MODEL, _pick(T, 512), D_MODEL, pairs=[(merged, 0, w_out, "nn")],
                        tiles=[(h1, 0)], consts=[g_next], outs=[F32, BF16])
    return h2, n2, (ya, yb, merged)


GATHER_FIRST = ("w_ffn1_in",)
GATHER_MIX = ("w_ffn1_out", "w_in")
GATHER_PROJ = ("w_att_proj", "w_rec_proj", "w_out")
GATHER_TAIL = ("w_ffn2_out", "w_ple_gate", "w_ple_proj")
GATHER_LAST = ("w_ffn2_in",)
SCATTER_LATE = ("w_ple_gate", "w_ple_proj", "w_ffn2_in", "w_ffn2_out")
SCATTER_MIX = ("w_out", "w_att_proj", "w_rec_proj", "w_in")
SCATTER_LAST = ("w_ffn1_in", "w_ffn1_out")


def _local_step(x, p, tgt, w, mine16, cc, me_chip, B, S):
    T = B * S
    w = dict(w)
    g_ffn1, g_mix, g_ffn2, g_ple = w["norm_ffn1"], w["norm_mix"], w["norm_ffn2"], w["norm_ple"]
    g_fin = w["norm_final"].reshape(1, D_MODEL)
    grads, part, from_chips = {}, {}, {}

    def gather(names):
        return _gather_comm([mine16[n] for n in names])

    def place(names, got):
        for n, g in zip(names, got):
            full = lax.dynamic_update_index_in_dim(g, mine16[n], me_chip, 0)
            w[n] = full if n in ("w_ffn1_in", "w_ffn2_in") else _natural(n, full)

    def swap(names):
        return _swap_comm([grads[n][1] for n in names])

    def after_swap(names, from_sib):
        for n, fs in zip(names, from_sib):
            part[n] = _add_sibling("rs_add_sib_" + n, grads[n][0], fs, cc)
        return _scatter_comm([part[n][1] for n in names])

    def scattered(names, got):
        for n, g in zip(names, got):
            from_chips[n] = g

    def ffn1_in_weight(got):
        place(GATHER_FIRST, got)
        return w["w_ffn1_in"]

    def ffn1_out_weight(got):
        place(GATHER_MIX, got)
        return w["w_ffn1_out"]

    h1, u, sv1, got_proj = _ffn_fwd("ffn1", x, g_ffn1, None, None, g_mix, comm_norm=gather(GATHER_FIRST),
                                    w_in_of=ffn1_in_weight, comm_in=gather(GATHER_MIX),
                                    comm_out=gather(GATHER_PROJ), w_out_of=ffn1_out_weight)
    place(GATHER_PROJ, got_proj)

    def ident(accs, tv, cv):
        return [accs[0]]

    proj, *got_tail = _tile_call("in_proj", ident, T, IN_W, _pick(T, 512), IN_W // 2,
                                 pairs=[(u, 0, w["w_in"], "nn")], outs=[F32], j_outer=True, comm=gather(GATHER_TAIL))
    place(GATHER_TAIL, got_tail)
    onehot = jnp.asarray(_t5_onehot())
    bias = _small_mm("t5_bias", w["rel_bias"].T, onehot.astype(BF16), "right")
    bias = bias.reshape(N_Q_HEADS, ATT_BLOCK, 2 * ATT_BLOCK)
    sinks = w["attn_sinks"].reshape(N_Q_HEADS)
    kk2, vv2 = _kv_layouts(proj)
    att = _swa_fwd(proj, kk2, vv2, bias, sinks, B, S)
    rec, states, *got_last = _hgrn_fwd(proj, w["lb_param"], B, S, comm=gather(GATHER_LAST))
    place(GATHER_LAST, got_last)
    recn = _rec_gate_fwd(rec, proj, w["rec_norm"])
    h2, n2, (ya, yb, merged) = _mix_out_fwd(att, recn, proj, w["w_att_proj"], w["w_rec_proj"], w["w_out"], h1,
                                            g_ffn2)
    h3, n3, sv2, _ = _ffn_fwd("ffn2", h2, g_ffn2, w["w_ffn2_in"], w["w_ffn2_out"], g_ple, n=n2)

    def ple(accs, tv, cv):
        gate = _sigmoid(accs[0])
        return [gate, accs[1], tv[0] + gate * accs[1]]

    gate_p, pp, h4 = _tile_call(
        "ple", ple, T, D_MODEL, _pick(T, 512), D_MODEL,
        pairs=[(n3, 0, w["w_ple_gate"], "nn"), (p, 0, w["w_ple_proj"], "nn")], tiles=[(h3, 0)],
        outs=[BF16, BF16, F32])

    def head(accs, tv, cv):
        h, t, gt, ppv = tv[0], tv[1], tv[2].astype(F32), tv[3].astype(F32)
        err = _rms_hat(h) * cv[0] - t
        dh, dg = _rms_bwd_vals(err * (1.0 / D_MODEL), h, cv[0])
        return [dh, dh * ppv * gt * (1.0 - gt), dh * gt, _group8(err * err), dg]

    dh4, dzg, dpp, loss_p, dg_fin = _tile_call(
        "loss_head", head, T, D_MODEL, _pick(T, 256), D_MODEL,
        tiles=[(h4, 0), (tgt, 0), (gate_p, 0), (pp, 0)], consts=[g_fin], outs=[F32, BF16, BF16], parts=2)
    grads["norm_final"] = dg_fin

    grads["w_ple_gate"] = _mm_tn_rows("ple_dwg", n3, dzg)
    grads["w_ple_proj"] = _mm_tn_cols("ple_dwp", p, dpp)

    def dnorm(accs, tv, cv):
        dh, dg = _rms_bwd_vals(accs[0], tv[0], cv[0])
        dh = tv[1] + dh
        return [dh, 0.5 * dh, dg]

    dh3, df3, grads["norm_ple"] = _tile_call(
        "ple_dnorm", dnorm, T, D_MODEL, _pick(T, 512), D_MODEL, pairs=[(dzg, 0, w["w_ple_gate"], "nt")],
        tiles=[(h3, 0), (dh4, 0)], consts=[g_ple], outs=[F32, BF16], parts=1)

    def swap_late(dw_in, dw_out):
        grads["w_ffn2_in"], grads["w_ffn2_out"] = dw_in, dw_out
        return swap(SCATTER_LATE)

    dh2, dh2b, grads["norm_ffn2"], _, _, _, from_sib = _ffn_bwd(
        "ffn2b", dh3, df3, h2, g_ffn2, w["w_ffn2_in"], w["w_ffn2_out"], sv2, comm_last=swap_late)
    scatter_late = after_swap(SCATTER_LATE, from_sib)

    grads["w_out"] = _mm_tn_rows("mix_dwout", merged, dh2b)
    tn = 256

    def dmerge(accs, tv, cv):
        dm = accs[0]
        sa, sb = _sigmoid(tv[0]), _sigmoid(tv[1])
        yav, ybv = tv[2].astype(F32), tv[3].astype(F32)
        return [dm * sa, dm * sb, dm * yav * sa * (1.0 - sa), dm * ybv * sb * (1.0 - sb)]

    dya, dyb, dga, dgb = _tile_call(
        "mix_dmerge", dmerge, T, D_MODEL, _pick(T, 1024), tn, pairs=[(dh2b, 0, w["w_out"], "nt")],
        tiles=[(proj, COL_GA * 128 // tn), (proj, COL_GB * 128 // tn), (ya, 0), (yb, 0)], outs=[BF16] * 4)
    grads["w_att_proj"] = _mm_tn_cols("mix_dwatt", att, dya)
    grads["w_rec_proj"] = _mm_tn_cols("mix_dwrec", recn, dyb)

    datt = _tile_call("mix_datt", ident, T, 512, _pick(T, 1024), 512, pairs=[(dya, 0, w["w_att_proj"], "nt")],
                      outs=[BF16])[0]
    drec, drg, grads["rec_norm"] = _rec_gate_bwd(dyb, w["w_rec_proj"], rec, proj, w["rec_norm"])

    drq, drf, dri, dlb, *got = _hgrn_bwd(proj, w["lb_param"], states, drec, B, S, comm=scatter_late)
    scattered(SCATTER_LATE, got)
    grads["lb_param"] = dlb
    daq, dak, dav, dbias, dsink = _swa_bwd(proj, kk2, vv2, bias, sinks, datt, B, S)
    grads["attn_sinks"] = dsink
    grads["rel_bias"] = _small_mm("t5_dbias", dbias.reshape(N_Q_HEADS, -1), onehot.T.astype(BF16), "right")
    dproj = jnp.concatenate([daq, dak, dav, drq, drf, dri, drg, dga, dgb], axis=1)
    tk = _pick(T, 2048, 128)
    w_in_shard = IN_W // N_CHIPS
    half_d = D_MODEL // 2
    gw32, gw16 = _mm_tn("mix_dwin", (2, 2, T // tk),
                        (u, (tk, half_d), lambda i, j, k: (k, i)), (dproj, (tk, IN_W // 2), lambda i, j, k: (k, j)),
                        _grad_pair((D_MODEL, IN_W), (half_d, IN_W // 2), lambda i, j, k: (i, j)))
    to_sh = lambda t: t.reshape(D_MODEL, N_CHIPS, w_in_shard).transpose(1, 0, 2)
    grads["w_in"] = (to_sh(gw32), to_sh(gw16))

    def dnorm_mix(accs, tv, cv):
        dh, dg = _rms_bwd_vals(accs[0], tv[0], cv[0])
        dh = tv[1] + dh
        return [dh, 0.5 * dh, dg]

    dh1, df1, grads["norm_mix"], *from_sib = _tile_call(
        "mix_dnorm", dnorm_mix, T, D_MODEL, _pick(T, 512), D_MODEL, pairs=[(dproj, 0, w["w_in"], "nt")],
        tiles=[(h1, 0), (dh2, 0)], consts=[g_mix], outs=[F32, BF16], parts=1, comm=swap(SCATTER_MIX))
    scatter_mix = after_swap(SCATTER_MIX, from_sib)

    def scatter_last(dw_in, dw_out):
        grads["w_ffn1_in"], grads["w_ffn1_out"] = dw_in, dw_out
        return after_swap(SCATTER_LAST, _run_comm("rs_sibling_last", swap(SCATTER_LAST)))

    dx, _, grads["norm_ffn1"], _, _, got, got_last = _ffn_bwd(
        "ffn1b", dh1, df1, x, g_ffn1, w["w_ffn1_in"], w["w_ffn1_out"], sv1, comm=scatter_mix, comm_last=scatter_last)
    scattered(SCATTER_MIX, got)
    scattered(SCATTER_LAST, got_last)
    return loss_p, dx, grads, part, from_chips


def _place():
    x, y, c = lax.axis_index("x"), lax.axis_index("y"), lax.axis_index("c")
    return x, y, c


def _other_chips(x, y):
    return [(1 - x, y, 2 * (1 - x) + y), (x, 1 - y, 2 * x + 1 - y), (1 - x, 1 - y, 2 * (1 - x) + 1 - y)]


def _half_rows(ref_3d, chip, h, rows):
    return ref_3d.at[chip, pl.ds(h * rows, rows), :]


def _gather_comm(ws):
    nw = len(ws)

    def parts(w_refs, out_refs, send_sems, recv_sems):
        x, y, c = _place()
        me = 2 * x + y
        chips = _other_chips(x, y)

        def copy(i, k, chip, h, to, src=None):
            half = ws[i].shape[0] // 2
            dst = _half_rows(out_refs[i], chip, h, half)
            return pltpu.make_async_remote_copy(
                src_ref=dst if src is None else src, dst_ref=dst,
                send_sem=send_sems.at[6 * i + k], recv_sem=recv_sems.at[6 * i + k], device_id=to, device_id_type=MESH)

        def first():
            out = []
            for i in range(nw):
                half = ws[i].shape[0] // 2
                out += [copy(i, j, me, c, (cx, cy, c), src=w_refs[i].at[pl.ds(c * half, half), :])
                        for j, (cx, cy, _) in enumerate(chips)]
            return out

        return copy, first, chips, c, (x, y, 1 - c)

    def start(*refs):
        _, first, _, _, _ = parts(*refs)
        for cp in first():
            cp.start()

    def finish(*refs):
        copy, first, chips, c, sibling = parts(*refs)
        passed = []
        for i in range(nw):
            for j, (cx, cy, ci) in enumerate(chips):
                copy(i, j, ci, c, (cx, cy, c)).wait_recv()
                fw = copy(i, 3 + j, ci, c, sibling)
                fw.start()
                passed.append(fw)
        for i in range(nw):
            for j, (_, _, ci) in enumerate(chips):
                copy(i, 3 + j, ci, 1 - c, sibling).wait_recv()
        for cp in first() + passed:
            cp.wait_send()

    return _Comm(list(ws), [jax.ShapeDtypeStruct((N_CHIPS,) + w.shape, w.dtype) for w in ws], 6 * nw, start, finish)


def _scatter_comm(ps):
    nw = len(ps)

    def copies(p_refs, out_refs, send_sems, recv_sems):
        x, y, c = _place()
        cps = []
        for i in range(nw):
            for j, (cx, cy, ci) in enumerate(_other_chips(x, y)):
                cps.append(pltpu.make_async_remote_copy(
                    src_ref=p_refs[i].at[ci], dst_ref=out_refs[i].at[j], send_sem=send_sems.at[3 * i + j],
                    recv_sem=recv_sems.at[3 * i + j], device_id=(cx, cy, c), device_id_type=MESH))
        return cps

    def start(*refs):
        for cp in copies(*refs):
            cp.start()

    def finish(*refs):
        for cp in copies(*refs):
            cp.wait()

    return _Comm(list(ps), [jax.ShapeDtypeStruct((3,) + p.shape[1:], p.dtype) for p in ps], 3 * nw, start, finish)


def _swap_comm(gs):
    nw = len(gs)

    def copies(g_refs, out_refs, send_sems, recv_sems):
        x, y, c = _place()
        cps = []
        for i in range(nw):
            half = gs[i].shape[1] // 2
            cps.append(pltpu.make_async_remote_copy(
                src_ref=g_refs[i].at[:, pl.ds((1 - c) * half, half), :], dst_ref=out_refs[i],
                send_sem=send_sems.at[i], recv_sem=recv_sems.at[i], device_id=(x, y, 1 - c), device_id_type=MESH))
        return cps

    def start(*refs):
        for cp in copies(*refs):
            cp.start()

    def finish(*refs):
        for cp in copies(*refs):
            cp.wait()

    return _Comm(list(gs), [jax.ShapeDtypeStruct((N_CHIPS, g.shape[1] // 2, g.shape[2]), g.dtype) for g in gs],
                 nw, start, finish)


def _run_comm(name, comm):
    nci, nco = len(comm.ins), len(comm.out_shapes)

    def body(*refs):
        cin, cout, send_sems, recv_sems = refs[:nci], refs[nci:nci + nco], refs[-2], refs[-1]
        comm.start(cin, cout, send_sems, recv_sems)
        comm.finish(cin, cout, send_sems, recv_sems)

    return pl.pallas_call(
        body, name=name, in_specs=[ANY] * nci, out_specs=[ANY] * nco, out_shape=list(comm.out_shapes),
        scratch_shapes=[pltpu.SemaphoreType.DMA((comm.n_sems,)), pltpu.SemaphoreType.DMA((comm.n_sems,))],
    )(*comm.ins)


def _join_halves(name, ss):
    nw = len(ss)

    def body(*refs):
        s_refs, out_refs, send_sems, recv_sems = refs[:nw], refs[nw:2 * nw], refs[2 * nw], refs[2 * nw + 1]
        x, y, c = _place()
        cps = [pltpu.make_async_remote_copy(
            src_ref=s_refs[i], dst_ref=out_refs[i], send_sem=send_sems.at[i], recv_sem=recv_sems.at[i],
            device_id=(x, y, 1 - c), device_id_type=MESH) for i in range(nw)]
        for cp in cps:
            cp.start()
        for cp in cps:
            cp.wait()

    return pl.pallas_call(
        body, name=name, in_specs=[ANY] * nw, out_specs=[ANY] * nw,
        out_shape=[jax.ShapeDtypeStruct(s.shape, s.dtype) for s in ss],
        scratch_shapes=[pltpu.SemaphoreType.DMA((nw,)), pltpu.SemaphoreType.DMA((nw,))],
    )(*ss)


def _allreduce_small(sp):
    def body(s_ref, out_ref, slots, send_sems, recv_sems):
        x, y, c = _place()
        me = 4 * x + 2 * y + c
        slots[me] = s_ref[...]
        cps = []
        for r in range(1, N_DEV):
            px, py, pc = x ^ (r >> 2), y ^ ((r >> 1) & 1), c ^ (r & 1)
            cps.append(pltpu.make_async_remote_copy(
                src_ref=s_ref, dst_ref=slots.at[me], send_sem=send_sems.at[r - 1], recv_sem=recv_sems.at[r - 1],
                device_id=(px, py, pc), device_id_type=MESH))
        for cp in cps:
            cp.start()
        for r in range(1, N_DEV):
            px, py, pc = x ^ (r >> 2), y ^ ((r >> 1) & 1), c ^ (r & 1)
            pltpu.make_async_remote_copy(
                src_ref=s_ref, dst_ref=slots.at[4 * px + 2 * py + pc], send_sem=send_sems.at[r - 1],
                recv_sem=recv_sems.at[r - 1], device_id=(px, py, pc), device_id_type=MESH).wait_recv()
        for cp in cps:
            cp.wait_send()
        acc = slots[0]
        for d in range(1, N_DEV):
            acc = acc + slots[d]
        out_ref[...] = acc

    return pl.pallas_call(
        body, name="allreduce_small",
        in_specs=[pl.BlockSpec(memory_space=pltpu.VMEM)], out_specs=pl.BlockSpec(memory_space=pltpu.VMEM),
        out_shape=jax.ShapeDtypeStruct(sp.shape, F32),
        scratch_shapes=[pltpu.VMEM((N_DEV,) + sp.shape, F32), pltpu.SemaphoreType.DMA((N_DEV - 1,)),
                        pltpu.SemaphoreType.DMA((N_DEV - 1,))],
    )(sp)


def _scalar(v):
    return jnp.reshape(v, (1,)).astype(jnp.int32)


def _row_tile(h, dtype_mult=16):
    return _pick(h, 256, dtype_mult)


def _add_sibling(name, g32, from_sib, c):
    _, r, n = g32.shape
    h = r // 2
    th = _row_tile(h)
    nt = h // th

    def body(c_ref, g_ref, s_ref, o32_ref, o16_ref):
        s = g_ref[...] + s_ref[...].astype(F32)
        o32_ref[...] = s
        o16_ref[...] = s.astype(BF16)

    blk = (None, th, n)
    return pl.pallas_call(
        body, name=name,
        grid_spec=pltpu.PrefetchScalarGridSpec(
            num_scalar_prefetch=1, grid=(N_CHIPS, nt),
            in_specs=[pl.BlockSpec(blk, lambda k, t, c_ref: (k, c_ref[0] * nt + t, 0)),
                      pl.BlockSpec(blk, lambda k, t, c_ref: (k, t, 0))],
            out_specs=[pl.BlockSpec(blk, lambda k, t, c_ref: (k, t, 0))] * 2),
        out_shape=[jax.ShapeDtypeStruct((N_CHIPS, h, n), F32), jax.ShapeDtypeStruct((N_CHIPS, h, n), BF16)],
        compiler_params=_params("arbitrary", "arbitrary"))(_scalar(c), g32, from_sib)


def _add_chips(name, p32, from_chips, me_chip):
    _, h, n = p32.shape
    th = _row_tile(h)

    def body(m_ref, p_ref, a_ref, b_ref, c_ref, o_ref):
        o_ref[...] = p_ref[...] + a_ref[...].astype(F32) + b_ref[...].astype(F32) + c_ref[...].astype(F32)

    blk = (None, th, n)
    return pl.pallas_call(
        body, name=name,
        grid_spec=pltpu.PrefetchScalarGridSpec(
            num_scalar_prefetch=1, grid=(h // th,),
            in_specs=[pl.BlockSpec(blk, lambda t, m_ref: (m_ref[0], t, 0))]
            + [pl.BlockSpec(blk, lambda t, m_ref, j=j: (j, t, 0)) for j in range(3)],
            out_specs=pl.BlockSpec((th, n), lambda t, m_ref: (t, 0))),
        out_shape=jax.ShapeDtypeStruct((h, n), F32),
        compiler_params=_params("arbitrary"))(_scalar(me_chip), p32, from_chips, from_chips, from_chips)


def _adamw_vals(w, g, m, v):
    m = ADAM_B1 * m + (1.0 - ADAM_B1) * g
    v = ADAM_B2 * v + (1.0 - ADAM_B2) * (g * g)
    m_hat = m / (1.0 - ADAM_B1 ** ADAM_STEP)
    v_hat = v / (1.0 - ADAM_B2 ** ADAM_STEP)
    delta = -ADAM_LR * (m_hat / (jnp.sqrt(v_hat) + ADAM_EPS) + ADAM_WD * w)
    return delta, m, v


def _adamw_halves(name, w, m, v, g_mine, g_sib, c):
    r, n = w.shape
    h = r // 2
    th = _row_tile(h, 8)
    nt = h // th

    def body(c_ref, w_ref, m_ref, v_ref, a_ref, b_ref, g_ref, d_ref, nm_ref, nv_ref):
        mine = (pl.program_id(0) // nt) == c_ref[0]
        g = jnp.where(mine, a_ref[...], b_ref[...])
        d, nm, nv = _adamw_vals(w_ref[...], g, m_ref[...], v_ref[...])
        g_ref[...] = g
        d_ref[...] = d
        nm_ref[...] = nm
        nv_ref[...] = nv

    full = pl.BlockSpec((th, n), lambda t, c_ref: (t, 0))
    own_half = pl.BlockSpec((th, n), lambda t, c_ref: (jnp.where(t // nt == c_ref[0], t % nt, 0), 0))
    sib_half = pl.BlockSpec((th, n), lambda t, c_ref: (jnp.where(t // nt == c_ref[0], 0, t % nt), 0))
    return pl.pallas_call(
        body, name=name,
        grid_spec=pltpu.PrefetchScalarGridSpec(
            num_scalar_prefetch=1, grid=(2 * nt,), in_specs=[full, full, full, own_half, sib_half],
            out_specs=[full] * 4),
        out_shape=[jax.ShapeDtypeStruct((r, n), F32)] * 4,
        compiler_params=_params("arbitrary"))(_scalar(c), w, m, v, g_mine, g_sib)


def _adamw(name, w, g, m, v):
    R, W = w.shape

    def fn(accs, tv, cv):
        return list(_adamw_vals(*tv))

    return _tile_call(name, fn, R, W, _pick(R, 256), W, tiles=[(w, 0), (g, 0), (m, 0), (v, 0)], outs=[F32] * 3)


SMALL_LAYOUT = (("rel_bias", 2, 256), ("lb_param", 8, 1024), ("norm_ffn1", 8, 1024), ("norm_mix", 8, 1024),
                ("attn_sinks", 1, 8), ("rec_norm", 1, 128), ("norm_ffn2", 8, 1024), ("norm_ple", 8, 1024),
                ("norm_final", 8, 1024), ("loss", 8, 1024))


def _pack_small(vals):
    rows = []
    for name, nrows, n in SMALL_LAYOUT:
        flat = vals[name].reshape(-1)
        flat = jnp.pad(flat, (0, nrows * 128 - n))
        rows.append(flat.reshape(nrows, 128))
    packed = jnp.concatenate(rows, axis=0)
    return jnp.pad(packed, ((0, SMALL_ROWS - packed.shape[0]), (0, 0)))


def _unpack_small(packed, shapes):
    out, r = {}, 0
    for name, nrows, n in SMALL_LAYOUT:
        out[name] = packed[r:r + nrows].reshape(-1)[:n].reshape(shapes[name])
        r += nrows
    return out


def _natural(name, s):
    if name in COL_SHARDED:
        return s.transpose(1, 0, 2).reshape(s.shape[1], -1)
    return s.reshape(-1, s.shape[2])


def kernel(x, p, rel_bias, lb_param, norm_ffn1, w_ffn1_in, w_ffn1_out, norm_mix, w_in, attn_sinks, rec_norm, w_att_proj, w_rec_proj, w_out, norm_ffn2, w_ffn2_in, w_ffn2_out, norm_ple, w_ple_gate, w_ple_proj, norm_final, loss_target, m_rel_bias, m_lb_param, m_norm_ffn1, m_w_ffn1_in, m_w_ffn1_out, m_norm_mix, m_w_in, m_attn_sinks, m_rec_norm, m_w_att_proj, m_w_rec_proj, m_w_out, m_norm_ffn2, m_w_ffn2_in, m_w_ffn2_out, m_norm_ple, m_w_ple_gate, m_w_ple_proj, m_norm_final, v_rel_bias, v_lb_param, v_norm_ffn1, v_w_ffn1_in, v_w_ffn1_out, v_norm_mix, v_w_in, v_attn_sinks, v_rec_norm, v_w_att_proj, v_w_rec_proj, v_w_out, v_norm_ffn2, v_w_ffn2_in, v_w_ffn2_out, v_norm_ple, v_w_ple_gate, v_w_ple_proj, v_norm_final):
    args = dict(locals())
    wsh = {n: args[n] for n in WEIGHTS}
    B, S = x.shape[0], x.shape[1]
    T = B * S
    cx, cy, cc = _place()
    me_chip = 2 * cx + cy

    mine16 = {n: wsh[n][0].astype(BF16) for n in BIG}
    loss_p, dx, grads, part, from_chips = _local_step(
        x.reshape(T, D_MODEL), p.reshape(T, PLE_DIM), loss_target.reshape(T, D_MODEL),
        {n: wsh[n] for n in SMALL}, mine16, cc, me_chip, B, S)

    s_mine = [_add_chips("rs_add_chips_" + n, part[n][0], from_chips[n], me_chip) for n in BIG]
    s_sib = _join_halves("rs_join", s_mine)

    small_vals = {
        "rel_bias": grads["rel_bias"].T,
        "lb_param": jnp.concatenate([_colsum("dlb_sum", grads["lb_param"]),
                                     -_colsum("dlb_sum2", grads["lb_param"])], axis=0) / 8.0,
        "attn_sinks": grads["attn_sinks"][:, 0],
        "rec_norm": _colsum("drn_sum", grads["rec_norm"]).reshape(REC_HEADS, REC_DIM).sum(axis=0),
        "loss": _colsum("loss_sum", loss_p),
    }
    for n in ("norm_ffn1", "norm_mix", "norm_ffn2", "norm_ple", "norm_final"):
        small_vals[n] = _colsum(n + "_sum", grads[n])
    red = _allreduce_small(_pack_small(small_vals))
    small_shapes = {n: wsh[n].shape for n in SMALL}
    small_shapes["loss"] = (D_MODEL,)
    small = _unpack_small(red, small_shapes)
    loss = 0.5 * jnp.sum(small["loss"]) / D_MODEL

    out_g, out_d, out_m, out_v = {}, {}, {}, {}
    for n, gm, gs in zip(BIG, s_mine, s_sib):
        res = _adamw_halves("adamw_" + n, wsh[n][0], args["m_" + n][0], args["v_" + n][0], gm, gs, cc)
        out_g[n], out_d[n], out_m[n], out_v[n] = (t[None] for t in res)
    sw = _pack_small({**{n: wsh[n] for n in SMALL}, "loss": jnp.zeros((D_MODEL,), F32)})
    sm = _pack_small({**{n: args["m_" + n] for n in SMALL}, "loss": jnp.zeros((D_MODEL,), F32)})
    sv = _pack_small({**{n: args["v_" + n] for n in SMALL}, "loss": jnp.ones((D_MODEL,), F32)})
    sd, snm, snv = _adamw("adamw_small", sw, red, sm, sv)
    ud, um, uv = (_unpack_small(t, small_shapes) for t in (sd, snm, snv))
    for n in SMALL:
        out_g[n], out_d[n], out_m[n], out_v[n] = small[n], ud[n], um[n], uv[n]

    return (loss, dx.reshape(B, S, D_MODEL), *[out_g[n] for n in WEIGHTS], *[out_d[n] for n in WEIGHTS],
            *[out_m[n] for n in WEIGHTS], *[out_v[n] for n in WEIGHTS])
```

```python
import numpy as np
import jax
import jax.numpy as jnp
from jax import lax
from jax.experimental import pallas as pl
from jax.experimental.pallas import tpu as pltpu

F32 = jnp.float32
BF16 = jnp.bfloat16
MESH = pl.DeviceIdType.MESH

D_MODEL = 1024
D_FF = 2816
FF_SHARD = 2 * D_FF // 4
HEAD_DIM = 64
N_Q_HEADS = 8
ATT_BLOCK = 128
N_BUCKETS = 32
MAX_DISTANCE = 128
REC_HEADS = 4
REC_DIM = 128
PLE_DIM = 256
EPS = 1e-6
IN_W = 4864
COL_AQ, COL_AK, COL_AV, COL_RQ, COL_RF, COL_RI, COL_RG, COL_GA, COL_GB = 0, 4, 5, 6, 10, 14, 18, 22, 30

CHUNK = 64
SUB = 8
N_SUB = CHUNK // SUB
HGRN_PAIR = 2

ADAM_LR, ADAM_B1, ADAM_B2, ADAM_EPS, ADAM_WD, ADAM_STEP = 0.001, 0.9, 0.999, 1e-08, 0.01, 10

V7X_VMEM_LIMIT = 56 * 1024 * 1024
N_CHIPS = 4
N_DEV = 8

BIG = ("w_ffn1_in", "w_ffn1_out", "w_in", "w_att_proj", "w_rec_proj", "w_out",
       "w_ffn2_in", "w_ffn2_out", "w_ple_gate", "w_ple_proj")
COL_SHARDED = ("w_ffn1_in", "w_in", "w_att_proj", "w_rec_proj", "w_ffn2_in", "w_ple_proj")
WEIGHTS = ("rel_bias", "lb_param", "norm_ffn1", "w_ffn1_in", "w_ffn1_out", "norm_mix", "w_in", "attn_sinks",
           "rec_norm", "w_att_proj", "w_rec_proj", "w_out", "norm_ffn2", "w_ffn2_in", "w_ffn2_out", "norm_ple",
           "w_ple_gate", "w_ple_proj", "norm_final")
SMALL = tuple(n for n in WEIGHTS if n not in BIG)
SMALL_ROWS = 64


def _params(*sem):
    return pltpu.CompilerParams(dimension_semantics=sem, vmem_limit_bytes=V7X_VMEM_LIMIT)


def _pick(n, cap, mult=8):
    if n <= cap:
        return n
    for t in range(cap - cap % mult, 0, -mult):
        if n % t == 0:
            return t
    raise ValueError((n, cap, mult))


def _dot(a, b):
    return jnp.dot(a, b, preferred_element_type=F32)


def _dot_nt(a, b):
    return lax.dot_general(a, b, (((1,), (1,)), ((), ())), preferred_element_type=F32)


def _dot_tn(a, b):
    return lax.dot_general(a, b, (((0,), (0,)), ((), ())), preferred_element_type=F32)


def _split3(x):
    hi = x.astype(BF16)
    r = x - hi.astype(F32)
    mid = r.astype(BF16)
    lo = (r - mid.astype(F32)).astype(BF16)
    return hi, mid, lo


def _split2(x):
    hi = x.astype(BF16)
    return hi, (x - hi.astype(F32)).astype(BF16)


def _sel_left(sel_bf16, x):
    hi, mid, lo = _split3(x)
    return _dot(sel_bf16, hi) + _dot(sel_bf16, mid) + _dot(sel_bf16, lo)


def _sel_right(x, sel_bf16):
    hi, mid, lo = _split3(x)
    return _dot(hi, sel_bf16) + _dot(mid, sel_bf16) + _dot(lo, sel_bf16)


def _sigmoid(x):
    return 0.5 * jnp.tanh(0.5 * x) + 0.5


def _group8(x):
    r, w = x.shape
    return x.reshape(r // 8, 8, w).sum(axis=0)


class _Comm:
    def __init__(self, ins, out_shapes, n_sems, start, finish):
        self.ins, self.out_shapes, self.n_sems, self.start, self.finish = ins, out_shapes, n_sems, start, finish


ANY = pl.BlockSpec(memory_space=pl.ANY)


def _comm_parts(comm):
    if comm is None:
        return [], [], [], []
    sems = [pltpu.SemaphoreType.DMA((comm.n_sems,)), pltpu.SemaphoreType.DMA((comm.n_sems,))]
    return list(comm.ins), [ANY] * len(comm.ins), list(comm.out_shapes), sems


def _comm_run(comm, grid, refs, n_in, n_out):
    if comm is None:
        return (lambda: None), (lambda: None)
    nci, nco = len(comm.ins), len(comm.out_shapes)
    cin = refs[n_in:n_in + nci]
    cout = refs[n_in + nci + n_out:n_in + nci + n_out + nco]
    send_sems, recv_sems = refs[-2], refs[-1]
    ids = [pl.program_id(d) for d in range(len(grid))]
    is_first = ids[0] == 0
    is_last = ids[0] == grid[0] - 1
    for d in range(1, len(grid)):
        is_first = is_first & (ids[d] == 0)
        is_last = is_last & (ids[d] == grid[d] - 1)

    def first():
        @pl.when(is_first)
        def _():
            comm.start(cin, cout, send_sems, recv_sems)

    def last():
        @pl.when(is_last)
        def _():
            comm.finish(cin, cout, send_sems, recv_sems)

    return first, last


def _call(name, fn, grid, ins, outs, pairs=(), comm=None, j_outer=False):
    in_pair = {i for p in pairs for i in p[:2]}
    n_in, n_out = len(ins), len(outs)
    c_arrays, c_in_specs, c_out_shapes, c_sems = _comm_parts(comm)

    def body(*refs):
        first, last = _comm_run(comm, grid, refs, n_in, n_out)
        first()
        accs = []
        for ia, ib, kind in pairs:
            a, b = refs[ia][...].astype(BF16), refs[ib][...].astype(BF16)
            accs.append(_dot(a, b) if kind == "nn" else _dot_nt(a, b))
        vals = [refs[i][...] for i in range(n_in) if i not in in_pair]
        res = fn(accs, vals)
        out_refs = refs[n_in + len(c_arrays):n_in + len(c_arrays) + n_out]
        assert len(res) == len(out_refs), (name, len(res), len(out_refs))
        for o_ref, val in zip(out_refs, res):
            o_ref[...] = val.astype(o_ref.dtype)
        last()

    if j_outer:
        grid = (grid[1], grid[0])
        swap = lambda im: (lambda j, i: im(i, j))
        ins = [(a, blk, swap(im)) for a, blk, im in ins]
        outs = [(shp, dt, blk, swap(im)) for shp, dt, blk, im in outs]

    return pl.pallas_call(
        body, name=name, grid=grid,
        in_specs=[pl.BlockSpec(blk, im) for _, blk, im in ins] + c_in_specs,
        out_specs=[pl.BlockSpec(blk, im) for _, _, blk, im in outs] + [ANY] * len(c_out_shapes),
        out_shape=[jax.ShapeDtypeStruct(shp, dt) for shp, dt, _, _ in outs] + c_out_shapes,
        scratch_shapes=c_sems,
        compiler_params=_params(*(["arbitrary"] * len(grid))))(*[a for a, _, _ in ins], *c_arrays)


def _tile_call(name, fn, M, N, tm, tn, *, pairs=(), tiles=(), consts=(), outs=(), parts=0, comm=None,
               j_outer=False):
    gi, gj = M // tm, N // tn
    assert gi * tm == M and gj * tn == N, (name, M, N, tm, tn)
    ins, prs = [], []
    for a, a_col, b, kind in pairs:
        K = b.shape[0] if kind == "nn" else b.shape[1]
        ins.append((a, (tm, K), lambda i, j, c=a_col: (i, c)))
        if kind == "nn":
            ins.append((b, (K, tn), lambda i, j: (0, j)))
        else:
            ins.append((b, (tn, K), lambda i, j: (j, 0)))
        prs.append((len(ins) - 2, len(ins) - 1, kind))
    for arr, off in tiles:
        ins.append((arr, (tm, tn), lambda i, j, o=off: (i, j + o)))
    for arr in consts:
        ins.append((arr, arr.shape, lambda i, j: (0, 0)))
    out_l = [((M, N), dt, (tm, tn), lambda i, j: (i, j)) for dt in outs]
    out_l += [((gi * 8, N), F32, (8, tn), lambda i, j: (i, j))] * parts
    nt = len(tiles)

    def wrapped(accs, vals):
        return fn(accs, vals[:nt], vals[nt:])

    return _call(name, wrapped, (gi, gj), ins, out_l, prs, comm=comm, j_outer=j_outer)


def _mm_tn(name, grid, a_in, b_in, outs):
    nk = grid[2]
    tm = [d for d in a_in[1] if d is not None][1]
    tn = [d for d in b_in[1] if d is not None][1]

    def body(a_ref, b_ref, *rest):
        out_refs, acc_ref = rest[:-1], rest[-1]
        k = pl.program_id(2)

        @pl.when(k == 0)
        def _():
            acc_ref[...] = jnp.zeros_like(acc_ref)

        acc_ref[...] += _dot_tn(a_ref[...].astype(BF16), b_ref[...].astype(BF16))

        @pl.when(k == nk - 1)
        def _():
            for o_ref in out_refs:
                if len(o_ref.shape) == 3:
                    n = o_ref.shape[2]
                    for s in range(o_ref.shape[0]):
                        o_ref[s] = acc_ref[:, n * s:n * (s + 1)].astype(o_ref.dtype)
                else:
                    o_ref[...] = acc_ref[...].astype(o_ref.dtype)

    return pl.pallas_call(
        body, name=name, grid=grid,
        in_specs=[pl.BlockSpec(a_in[1], a_in[2]), pl.BlockSpec(b_in[1], b_in[2])],
        out_specs=[pl.BlockSpec(blk, im) for _, _, blk, im in outs],
        out_shape=[jax.ShapeDtypeStruct(shp, dt) for shp, dt, _, _ in outs],
        scratch_shapes=[pltpu.VMEM((tm, tn), F32)],
        compiler_params=_params("arbitrary", "arbitrary", "arbitrary"))(a_in[0], b_in[0])


def _grad_pair(shape, block, imap):
    return [(shape, F32, block, imap), (shape, BF16, block, imap)]


def _mm_tn_rows(name, a, b, tk=2048):
    T, a_w = a.shape
    b_w = b.shape[1]
    tm = _pick(a_w, 1408, 128)
    tk = _pick(T, tk, 128)
    g32, g16 = _mm_tn(name, (a_w // tm, 1, T // tk),
                      (a, (tk, tm), lambda i, j, k: (k, i)), (b, (tk, b_w), lambda i, j, k: (k, 0)),
                      _grad_pair((a_w, b_w), (tm, b_w), lambda i, j, k: (i, 0)))
    shp = (N_CHIPS, a_w // N_CHIPS, b_w)
    return g32.reshape(shp), g16.reshape(shp)


def _mm_tn_cols(name, a, b, tk=4096):
    T, a_w = a.shape
    b_w = b.shape[1]
    tk = _pick(T, tk, 128)
    shp = (N_CHIPS, a_w, b_w // N_CHIPS)
    return _mm_tn(name, (1, 1, T // tk),
                  (a, (tk, a_w), lambda i, j, k: (k, 0)), (b, (tk, b_w), lambda i, j, k: (k, 0)),
                  _grad_pair(shp, shp, lambda i, j, k: (0, 0, 0)))


def _colsum(name, x):
    def body(x_ref, o_ref):
        o_ref[...] = jnp.sum(x_ref[...], axis=0, keepdims=True)
    return pl.pallas_call(body, name=name, out_shape=jax.ShapeDtypeStruct((1, x.shape[1]), F32))(x)


def _rms_hat(h):
    return h * lax.rsqrt(jnp.mean(h * h, axis=-1, keepdims=True) + EPS)


def _rms_bwd_vals(dn, h, g):
    r = lax.rsqrt(jnp.mean(h * h, axis=-1, keepdims=True) + EPS)
    nh = h * r
    gd = dn * g
    dh = r * (gd - nh * jnp.mean(gd * nh, axis=-1, keepdims=True))
    return dh, _group8(dn * nh)


def _rms_fwd(name, h, g, tm=512, comm=None):
    T = h.shape[0]

    def fn(accs, tv, cv):
        return [_rms_hat(tv[0]) * cv[0]]

    return _tile_call(name, fn, T, D_MODEL, _pick(T, tm), D_MODEL, tiles=[(h, 0)], consts=[g], outs=[BF16],
                      comm=comm)


def _ffn_fwd(tag, h, g, w_in, w_out, g_next, n=None, comm_norm=None, w_in_of=None, comm_in=None, comm_out=None,
             w_out_of=None):
    T = h.shape[0]
    if n is None:
        n, *got_norm = _rms_fwd(tag + "_norm", h, g, comm=comm_norm)
        if w_in_of is not None:
            w_in = w_in_of(got_norm)
    tm = _pick(T, 1024)
    wblk = (None, D_MODEL, FF_SHARD)

    def act(accs, vals):
        gate, up = accs
        return [gate, up, gate * _sigmoid(gate) * up]

    tile = lambda: ((T, D_FF), BF16, (tm, FF_SHARD), lambda i, j: (i, j))
    gate, up, a, *got_in = _call(
        tag + "_in", act, (T // tm, 2),
        [(n, (tm, D_MODEL), lambda i, j: (i, 0)),
         (w_in, wblk, lambda i, j: (j, 0, 0)), (w_in, wblk, lambda i, j: (j + 2, 0, 0))],
        [tile(), tile(), tile()], pairs=[(0, 1, "nn"), (0, 2, "nn")], comm=comm_in, j_outer=True)

    def res(accs, tv, cv):
        h_new = tv[0] + 0.5 * accs[0]
        return [h_new, _rms_hat(h_new) * cv[0]]

    if w_out_of is not None:
        w_out = w_out_of(got_in)
    h_new, n_next, *got_out = _tile_call(
        tag + "_out", res, T, D_MODEL, _pick(T, 512), D_MODEL, pairs=[(a, 0, w_out, "nn")], tiles=[(h, 0)],
        consts=[g_next], outs=[F32, BF16], comm=comm_out)
    return h_new, n_next, (n, gate, up, a), got_out


def _ffn_bwd(tag, dh_out, df, h, g, w_in, w_out, saved, comm=None, comm_last=None):
    T = h.shape[0]
    n, gate, up, a = saved
    tm = _pick(T, 512)

    def dact(accs, vals):
        da = accs[0]
        gt, u = vals[0].astype(F32), vals[1].astype(F32)
        sg = _sigmoid(gt)
        silu = gt * sg
        return [jnp.stack([(da * u * (sg + silu * (1.0 - sg))).astype(BF16), (da * silu).astype(BF16)])]

    dz, *got = _call(
        tag + "_dact", dact, (T // tm, 2),
        [(df, (tm, D_MODEL), lambda i, j: (i, 0)), (w_out, (FF_SHARD, D_MODEL), lambda i, j: (j, 0)),
         (gate, (tm, FF_SHARD), lambda i, j: (i, j)), (up, (tm, FF_SHARD), lambda i, j: (i, j))],
        [((2, T, D_FF), BF16, (2, tm, FF_SHARD), lambda i, j: (0, i, j))], pairs=[(0, 1, "nt")], comm=comm,
        j_outer=True)
    dw_out = _mm_tn_rows(tag + "_dwout", a, df)
    tk = _pick(T, 2048, 128)
    dw_in = _mm_tn(tag + "_dwin", (1, N_CHIPS, T // tk),
                   (n, (tk, D_MODEL), lambda i, j, k: (k, 0)),
                   (dz, (None, tk, FF_SHARD), lambda i, j, k: (j // 2, k, j % 2)),
                   _grad_pair((N_CHIPS, D_MODEL, FF_SHARD), (None, D_MODEL, FF_SHARD), lambda i, j, k: (j, 0, 0)))

    def dnorm(accs, vals):
        dn = accs[0] + accs[1] + accs[2] + accs[3]
        dh, dg = _rms_bwd_vals(dn, vals[0], vals[2])
        dh = vals[1] + dh
        return [dh, dh, dg]

    tm2 = _pick(T, 512)
    ins = [(dz, (None, tm2, FF_SHARD), lambda i, j, s=s: (s // 2, i, s % 2)) for s in range(N_CHIPS)]
    ins += [(w_in, (None, D_MODEL, FF_SHARD), lambda i, j, s=s: (s, 0, 0)) for s in range(N_CHIPS)]
    ins += [(h, (tm2, D_MODEL), lambda i, j: (i, 0)), (dh_out, (tm2, D_MODEL), lambda i, j: (i, 0)),
            (g, g.shape, lambda i, j: (0, 0))]
    dh, dh16, dg, *got_last = _call(
        tag + "_dnorm", dnorm, (T // tm2, 1), ins,
        [((T, D_MODEL), F32, (tm2, D_MODEL), lambda i, j: (i, 0)),
         ((T, D_MODEL), BF16, (tm2, D_MODEL), lambda i, j: (i, 0)),
         ((T // tm2 * 8, D_MODEL), F32, (8, D_MODEL), lambda i, j: (i, 0))],
        pairs=[(s, N_CHIPS + s, "nt") for s in range(N_CHIPS)],
        comm=None if comm_last is None else comm_last(dw_in, dw_out))
    return dh, dh16, dg, dw_in, dw_out, got, got_last


def _t5_onehot():
    qi = np.arange(ATT_BLOCK)[:, None] + ATT_BLOCK
    kj = np.arange(2 * ATT_BLOCK)[None, :]
    nn = np.maximum(qi - kj, 0)
    max_exact = N_BUCKETS // 2
    large = max_exact + (np.log(np.maximum(nn, 1) / max_exact) / np.log(MAX_DISTANCE / max_exact)
                         * (N_BUCKETS - max_exact)).astype(np.int32)
    large = np.minimum(large, N_BUCKETS - 1)
    bucket = np.where(nn < max_exact, nn, large).astype(np.int32).reshape(-1)
    return (bucket[None, :] == np.arange(N_BUCKETS)[:, None]).astype(np.float32)


def _small_mm(name, a, b, sel):
    def body(a_ref, b_ref, o_ref):
        if sel == "right":
            o_ref[...] = _sel_right(a_ref[...], b_ref[...])
        else:
            o_ref[...] = _sel_left(a_ref[...], b_ref[...])
    return pl.pallas_call(body, name=name, out_shape=jax.ShapeDtypeStruct((a.shape[0], b.shape[1]), F32),
                          compiler_params=pltpu.CompilerParams(vmem_limit_bytes=V7X_VMEM_LIMIT))(a, b)


def _dup_heads(t):
    a, b = t[:, :HEAD_DIM], t[:, HEAD_DIM:]
    return jnp.concatenate([a, a, b, b], axis=1)


def _kv_layouts(proj):
    T = proj.shape[0]

    def fn(accs, tv, cv):
        return [tv[0], tv[1]]

    k, v = _tile_call("kv_cast", fn, T, 128, _pick(T, 1024), 128, tiles=[(proj, COL_AK), (proj, COL_AV)],
                      outs=[BF16, BF16])
    return _dup_heads(k), _dup_heads(v)


def _swa_masks():
    row = lax.broadcasted_iota(jnp.int32, (ATT_BLOCK, 2 * ATT_BLOCK), 0)
    col = lax.broadcasted_iota(jnp.int32, (ATT_BLOCK, 2 * ATT_BLOCK), 1)
    dist = ATT_BLOCK + row - col
    return (dist >= 0) & (dist < ATT_BLOCK), col


GROUP = 4


def _stack_group(blk, lo_q):
    zero = jnp.zeros_like(blk[:, :128])
    rows = []
    for pair in range(GROUP // 2):
        pb = blk[:, 128 * pair:128 * (pair + 1)]
        rows += [jnp.where(lo_q, pb, zero), jnp.where(lo_q, zero, pb)]
    return jnp.concatenate(rows, axis=0)


def _unstack_group(st, lo_q):
    pairs = [jnp.where(lo_q, st[256 * pair:256 * pair + 128], st[256 * pair + 128:256 * (pair + 1)])
             for pair in range(GROUP // 2)]
    return jnp.concatenate(pairs, axis=1)


def _swa_probs(s, bias_h, sink, valid):
    s = jnp.where(valid, s * (HEAD_DIM ** -0.5) + bias_h, -jnp.inf)
    m = jnp.maximum(jnp.max(s, axis=-1, keepdims=True), sink)
    e = jnp.exp(s - m)
    es = jnp.exp(sink - m)
    den = jnp.sum(e, axis=-1, keepdims=True) + es
    return e / den, es / den


def _swa_fwd(proj, kk2, vv2, bias, sinks, B, S):
    T = B * S
    nb = S // ATT_BLOCK

    def body(q_ref, k_ref, v_ref, bias_ref, sink_ref, o_ref, kpad, vpad):
        zeros = jnp.zeros((ATT_BLOCK, 256), BF16)
        kpad[pl.ds(0, ATT_BLOCK), :] = zeros
        vpad[pl.ds(0, ATT_BLOCK), :] = zeros
        kpad[pl.ds(ATT_BLOCK, S), :] = k_ref[...]
        vpad[pl.ds(ATT_BLOCK, S), :] = v_ref[...]
        valid0, col = _swa_masks()
        lo_q = lax.broadcasted_iota(jnp.int32, (1, 128), 1) < HEAD_DIM

        def blk(n, carry):
            r0 = pl.multiple_of(n * ATT_BLOCK, ATT_BLOCK)
            rows = pl.ds(r0, ATT_BLOCK)
            valid = valid0 & ((n > 0) | (col >= ATT_BLOCK))
            for g in range(N_Q_HEADS // GROUP):
                lanes = pl.ds(128 * g, 128)
                kg = kpad[pl.ds(r0, 2 * ATT_BLOCK), lanes]
                vg = vpad[pl.ds(r0, 2 * ATT_BLOCK), lanes]
                qm = _stack_group(q_ref[rows, pl.ds(256 * g, 256)].astype(BF16), lo_q)
                s = _dot_nt(qm, kg)
                ps = []
                for i in range(GROUP):
                    h = GROUP * g + i
                    p, _ = _swa_probs(s[ATT_BLOCK * i:ATT_BLOCK * (i + 1)], bias_ref[h], sink_ref[h], valid)
                    ps.append(p.astype(BF16))
                o = _dot(jnp.concatenate(ps, axis=0), vg)
                o_ref[rows, pl.ds(256 * g, 256)] = _unstack_group(o, lo_q).astype(o_ref.dtype)
            return carry

        if nb % 2 == 0:
            lax.fori_loop(0, nb // 2, lambda i, c: blk(2 * i + 1, blk(2 * i, c)), 0)
        else:
            lax.fori_loop(0, nb, blk, 0)

    return pl.pallas_call(
        body, name="swa_fwd", grid=(B,),
        in_specs=[pl.BlockSpec((S, 512), lambda b: (b, 0)),
                  pl.BlockSpec((S, 256), lambda b: (b, 0)),
                  pl.BlockSpec((S, 256), lambda b: (b, 0)),
                  pl.BlockSpec((N_Q_HEADS, ATT_BLOCK, 2 * ATT_BLOCK), lambda b: (0, 0, 0)),
                  pl.BlockSpec(memory_space=pltpu.SMEM)],
        out_specs=pl.BlockSpec((S, 512), lambda b: (b, 0)),
        out_shape=jax.ShapeDtypeStruct((T, 512), BF16),
        scratch_shapes=[pltpu.VMEM((S + ATT_BLOCK, 256), BF16), pltpu.VMEM((S + ATT_BLOCK, 256), BF16)],
        compiler_params=_params("arbitrary"))(proj, kk2, vv2, bias, sinks)


def _swa_bwd(proj, kk2, vv2, bias, sinks, datt, B, S):
    T = B * S
    nb = S // ATT_BLOCK

    def body(q_ref, k_ref, v_ref, bias_ref, sink_ref, do_ref, dq_ref, dk_ref, dv_ref, dbias_ref, dsink_ref,
             kpad, vpad, dkpad, dvpad):
        b = pl.program_id(0)

        @pl.when(b == 0)
        def _():
            dbias_ref[...] = jnp.zeros_like(dbias_ref)
            dsink_ref[...] = jnp.zeros_like(dsink_ref)

        zeros = jnp.zeros((ATT_BLOCK, 256), BF16)
        kpad[pl.ds(0, ATT_BLOCK), :] = zeros
        vpad[pl.ds(0, ATT_BLOCK), :] = zeros
        kpad[pl.ds(ATT_BLOCK, S), :] = k_ref[...]
        vpad[pl.ds(ATT_BLOCK, S), :] = v_ref[...]
        dkpad[...] = jnp.zeros_like(dkpad)
        dvpad[...] = jnp.zeros_like(dvpad)
        valid0, col = _swa_masks()
        lo_q = lax.broadcasted_iota(jnp.int32, (1, 128), 1) < HEAD_DIM
        scale = HEAD_DIM ** -0.5

        def blk(n, carry):
            r0 = pl.multiple_of(n * ATT_BLOCK, ATT_BLOCK)
            rows = pl.ds(r0, ATT_BLOCK)
            band = pl.ds(r0, 2 * ATT_BLOCK)
            valid = valid0 & ((n > 0) | (col >= ATT_BLOCK))
            for g in range(N_Q_HEADS // GROUP):
                lanes = pl.ds(128 * g, 128)
                kg = kpad[band, lanes]
                vg = vpad[band, lanes]
                qm = _stack_group(q_ref[rows, pl.ds(256 * g, 256)].astype(BF16), lo_q)
                dom = _stack_group(do_ref[rows, pl.ds(256 * g, 256)], lo_q)
                s = _dot_nt(qm, kg)
                dp = _dot_nt(dom, vg)
                pst, dst = [], []
                for i in range(GROUP):
                    h = GROUP * g + i
                    sl = slice(ATT_BLOCK * i, ATT_BLOCK * (i + 1))
                    p, ps = _swa_probs(s[sl], bias_ref[h], sink_ref[h], valid)
                    delta = jnp.sum(p * dp[sl], axis=-1, keepdims=True)
                    ds = p * (dp[sl] - delta)
                    dbias_ref[h] += ds
                    dsink_ref[pl.ds(h, 1), :] += -jnp.sum(jnp.broadcast_to(ps * delta, (ATT_BLOCK, 128)),
                                                          axis=0, keepdims=True)
                    pst.append(p.astype(BF16))
                    dst.append((ds * scale).astype(BF16))
                pst, dst = jnp.concatenate(pst, axis=0), jnp.concatenate(dst, axis=0)
                dq_ref[rows, pl.ds(256 * g, 256)] = _unstack_group(_dot(dst, kg), lo_q).astype(dq_ref.dtype)
                dkpad[band, lanes] += _dot_tn(dst, qm)
                dvpad[band, lanes] += _dot_tn(pst, dom)
            return carry

        if nb % 2 == 0:
            lax.fori_loop(0, nb // 2, lambda i, c: blk(2 * i + 1, blk(2 * i, c)), 0)
        else:
            lax.fori_loop(0, nb, blk, 0)
        lo_out = lax.broadcasted_iota(jnp.int32, (1, 128), 1) < HEAD_DIM

        def fold(pad_ref):
            halves = []
            for g in range(N_Q_HEADS // GROUP):
                t = pad_ref[pl.ds(ATT_BLOCK, S), pl.ds(128 * g, 128)]
                halves.append(t + pltpu.roll(t, HEAD_DIM, 1))
            return jnp.where(lo_out, halves[0], halves[1])

        dk_ref[...] = fold(dkpad).astype(dk_ref.dtype)
        dv_ref[...] = fold(dvpad).astype(dv_ref.dtype)

    return pl.pallas_call(
        body, name="swa_bwd", grid=(B,),
        in_specs=[pl.BlockSpec((S, 512), lambda b: (b, 0)),
                  pl.BlockSpec((S, 256), lambda b: (b, 0)),
                  pl.BlockSpec((S, 256), lambda b: (b, 0)),
                  pl.BlockSpec((N_Q_HEADS, ATT_BLOCK, 2 * ATT_BLOCK), lambda b: (0, 0, 0)),
                  pl.BlockSpec(memory_space=pltpu.SMEM),
                  pl.BlockSpec((S, 512), lambda b: (b, 0))],
        out_specs=[pl.BlockSpec((S, 512), lambda b: (b, 0)),
                   pl.BlockSpec((S, 128), lambda b: (b, 0)),
                   pl.BlockSpec((S, 128), lambda b: (b, 0)),
                   pl.BlockSpec((N_Q_HEADS, ATT_BLOCK, 2 * ATT_BLOCK), lambda b: (0, 0, 0)),
                   pl.BlockSpec((N_Q_HEADS, 128), lambda b: (0, 0))],
        out_shape=[jax.ShapeDtypeStruct((T, 512), BF16),
                   jax.ShapeDtypeStruct((T, 128), BF16),
                   jax.ShapeDtypeStruct((T, 128), BF16),
                   jax.ShapeDtypeStruct((N_Q_HEADS, ATT_BLOCK, 2 * ATT_BLOCK), F32),
                   jax.ShapeDtypeStruct((N_Q_HEADS, 128), F32)],
        scratch_shapes=[pltpu.VMEM((S + ATT_BLOCK, 256), BF16), pltpu.VMEM((S + ATT_BLOCK, 256), BF16),
                        pltpu.VMEM((S + ATT_BLOCK, 256), F32), pltpu.VMEM((S + ATT_BLOCK, 256), F32)],
        compiler_params=_params("arbitrary"))(proj, kk2, vv2, bias, sinks, datt)


def _hgrn_gates(z, lb):
    sg = _sigmoid(z)
    f = lb + (1.0 - lb) * sg
    return sg, f, jnp.log(f), 1.0 - f


def _hgrn_consts():
    r = lax.broadcasted_iota(jnp.int32, (CHUNK, CHUNK), 0)
    c = lax.broadcasted_iota(jnp.int32, (CHUNK, CHUNK), 1)
    tril = (r >= c).astype(BF16)
    triu = (r <= c).astype(BF16)
    causal = r >= c
    below = (r // SUB) > (c // SUB)
    inside = ((r // SUB) == (c // SUB)) & causal
    return tril, triu, causal, below, inside, c


def _block_rows(ref, lanes, s):
    rows = []
    for i in range(N_SUB):
        if SUB * i + s < 0:
            rows.append(jnp.zeros((SUB, REC_DIM), F32))
        else:
            rows.append(jnp.broadcast_to(ref[pl.ds(SUB * i + s, 1), lanes], (SUB, REC_DIM)))
    return jnp.concatenate(rows, axis=0)


def _hgrn_offdiag(q, k, bcum, b_ref, lanes):
    eq = jnp.exp(jnp.minimum(bcum - _block_rows(b_ref, lanes, -1), 0.0))
    qe = q * eq
    zero = jnp.zeros((SUB, REC_DIM), F32)
    q_rows, k_cols, eks = [jnp.zeros((SUB, (N_SUB - 1) * REC_DIM), F32)], [], []
    for i in range(1, N_SUB):
        q_rows.append(jnp.concatenate([zero] * (i - 1) + [qe[SUB * i:SUB * (i + 1), :]] + [zero] * (N_SUB - 1 - i),
                                      axis=1))
        p = b_ref[pl.ds(SUB * i - 1, 1), lanes]
        pad = jnp.zeros((CHUNK - SUB * i, REC_DIM), F32)
        ek = jnp.concatenate([jnp.exp(p - b_ref[pl.ds(0, SUB * i), lanes]), pad], axis=0)
        k_cols.append(k * ek)
        eks.append(ek)
    return jnp.concatenate(q_rows, axis=0), jnp.concatenate(k_cols, axis=1), eq, eks


def _hgrn_fwd(proj, lb_param, B, S, comm=None):
    T = B * S
    nc = S // CHUNK
    fwd_unroll = 4 if nc % 4 == 0 else 2
    c_arrays, c_in_specs, c_out_shapes, c_sems = _comm_parts(comm)
    nci, nco = len(c_arrays), len(c_out_shapes)

    def body(*refs):
        q_ref, z_ref, v_ref, lb_ref = refs[:4]
        o_ref, st_ref = refs[4 + nci:6 + nci]
        k_slots, b_slots = refs[6 + nci + nco:8 + nci + nco]
        comm_first, comm_last = _comm_run(comm, (B, REC_HEADS // HGRN_PAIR), refs, 4, 2)
        comm_first()
        tril, _, _, below, inside, col = _hgrn_consts()
        col_s = col & (SUB - 1)

        def chunk(ci, hts, slot):
            k_s, b_s = k_slots.at[slot], b_slots.at[slot]
            r0 = pl.multiple_of(ci * CHUNK, CHUNK)
            lb = _sigmoid(lb_ref[0:1, :] - lb_ref[1:2, :])
            _, _, g_all, k_all = _hgrn_gates(z_ref[pl.ds(r0, CHUNK), :], lb)
            b_all = _sel_left(tril, g_all)
            k_s[...] = k_all
            b_s[...] = b_all
            new = []
            for e, ht in enumerate(hts):
                lanes = pl.ds(REC_DIM * e, REC_DIM)
                cols = slice(REC_DIM * e, REC_DIM * (e + 1))
                q = q_ref[pl.ds(r0, CHUNK), lanes]
                v = v_ref[pl.ds(r0, CHUNK), lanes]
                k, bcum = k_all[:, cols], b_all[:, cols]
                st_ref[e * nc + ci] = ht
                qst, kst, _, _ = _hgrn_offdiag(q, k, bcum, b_s, lanes)
                d = jnp.zeros((CHUNK, CHUNK), F32)
                for s in range(SUB):
                    w = jnp.exp(jnp.minimum(bcum - _block_rows(b_s, lanes, s), 0.0))
                    colv = jnp.sum(q * _block_rows(k_s, lanes, s) * w, axis=-1, keepdims=True)
                    d = jnp.where(col_s == s, colv, d)
                a = jnp.where(below, _dot_nt(qst.astype(BF16), kst.astype(BF16)), 0.0) + jnp.where(inside, d, 0.0)
                vb = v.astype(BF16)
                qb = (q * jnp.exp(bcum)).astype(BF16)
                o_ref[pl.ds(r0, CHUNK), lanes] = _dot(a.astype(BF16), vb) + _dot_nt(qb, ht.astype(BF16))
                b_last = b_s[pl.ds(CHUNK - 1, 1), lanes]
                kb = (k * jnp.exp(b_last - bcum)).astype(BF16)
                new.append(ht * jnp.exp(b_last) + _dot_tn(vb, kb))
            return tuple(new)

        def trip(i, hts):
            for u in range(fwd_unroll):
                hts = chunk(fwd_unroll * i + u, hts, u)
            return hts

        lax.fori_loop(0, nc // fwd_unroll, trip, tuple(jnp.zeros((REC_DIM, REC_DIM), F32) for _ in range(HGRN_PAIR)))
        comm_last()

    hp, wd = REC_HEADS // HGRN_PAIR, HGRN_PAIR * REC_DIM
    cq, cf, ci_ = (c * REC_DIM // wd for c in (COL_RQ, COL_RF, COL_RI))
    return pl.pallas_call(
        body, name="hgrn_fwd", grid=(B, hp),
        in_specs=[pl.BlockSpec((S, wd), lambda b, h: (b, cq + h)),
                  pl.BlockSpec((S, wd), lambda b, h: (b, cf + h)),
                  pl.BlockSpec((S, wd), lambda b, h: (b, ci_ + h)),
                  pl.BlockSpec((2, wd), lambda b, h: (0, h))] + c_in_specs,
        out_specs=[pl.BlockSpec((S, wd), lambda b, h: (b, h)),
                   pl.BlockSpec((HGRN_PAIR * nc, REC_DIM, REC_DIM), lambda b, h: (b * hp + h, 0, 0))] + [ANY] * nco,
        out_shape=[jax.ShapeDtypeStruct((T, 512), F32),
                   jax.ShapeDtypeStruct((B * REC_HEADS * nc, REC_DIM, REC_DIM), F32)] + c_out_shapes,
        scratch_shapes=[pltpu.VMEM((fwd_unroll, CHUNK, wd), F32), pltpu.VMEM((fwd_unroll, CHUNK, wd), F32)] + c_sems,
        compiler_params=_params("arbitrary", "arbitrary"))(proj, proj, proj, lb_param, *c_arrays)


def _hgrn_bwd(proj, lb_param, states, do, B, S, comm=None):
    T = B * S
    nc = S // CHUNK
    bwd_unroll = 4 if nc % 4 == 0 else 2
    c_arrays, c_in_specs, c_out_shapes, c_sems = _comm_parts(comm)
    nci, nco = len(c_arrays), len(c_out_shapes)

    def body(*refs):
        q_ref, z_ref, v_ref, lb_ref, st_ref, do_ref = refs[:6]
        dq_ref, dz_ref, dv_ref, dlb_ref = refs[6 + nci:10 + nci]
        slots = refs[10 + nci + nco:14 + nci + nco]
        comm_first, comm_last = _comm_run(comm, (B, REC_HEADS // HGRN_PAIR), refs, 6, 4)
        comm_first()
        tril, triu, causal, below, inside, col = _hgrn_consts()
        col_s = col & (SUB - 1)
        last_row = lax.broadcasted_iota(jnp.int32, (CHUNK, 1), 0) == CHUNK - 1
        rc = lax.broadcasted_iota(jnp.int32, (CHUNK, SUB * REC_DIM), 0)
        lc = lax.broadcasted_iota(jnp.int32, (CHUNK, SUB * REC_DIM), 1)
        spread = ((rc & (SUB - 1)) == (lc // REC_DIM)).astype(BF16)
        rr = lax.broadcasted_iota(jnp.int32, (CHUNK, SUB * CHUNK), 0)
        cc = lax.broadcasted_iota(jnp.int32, (CHUNK, SUB * CHUNK), 1)
        gather = (((rr // SUB) == ((cc & (CHUNK - 1)) // SUB)) & ((rr & (SUB - 1)) == (cc // CHUNK))).astype(BF16)

        heads = range(HGRN_PAIR)
        cols = [slice(REC_DIM * e, REC_DIM * (e + 1)) for e in heads]
        lanes = [pl.ds(REC_DIM * e, REC_DIM) for e in heads]
        lane_cat = lambda vals: jnp.concatenate(vals, axis=1)
        row_cat = lambda vals: jnp.concatenate(vals, axis=0)

        def chunk(it, carry, slot):
            k_s, b_s, pc_hi, pc_lo = (r.at[slot] for r in slots)
            dhts, dlb = carry
            ci = nc - 1 - it
            r0 = pl.multiple_of(ci * CHUNK, CHUNK)
            rows = pl.ds(r0, CHUNK)
            lb = _sigmoid(lb_ref[0:1, :] - lb_ref[1:2, :])
            sg, f, g_all, k_all = _hgrn_gates(z_ref[rows, :], lb)
            b_all = _sel_left(tril, g_all)
            k_s[...] = k_all
            b_s[...] = b_all
            q_all = q_ref[rows, :]
            das, hd = [], []
            for e in heads:
                vb, dob = v_ref[rows, lanes[e]].astype(BF16), do_ref[rows, lanes[e]].astype(BF16)
                da = jnp.where(causal, _dot_nt(dob, vb), 0.0)
                das.append(jnp.where(inside, da, 0.0))
                hd.append((vb, dob, da))
            da_hi, da_lo = _split2(row_cat(das))
            da_in = _dot(da_hi, spread) + _dot(da_lo, spread)
            ds, dqs = [], []
            for e in heads:
                q, bcum = q_all[:, cols[e]], b_all[:, cols[e]]
                d = jnp.zeros((CHUNK, CHUNK), F32)
                dq = jnp.zeros((CHUNK, REC_DIM), F32)
                for s in range(SUB):
                    w = jnp.exp(jnp.minimum(bcum - _block_rows(b_s, lanes[e], s), 0.0))
                    ks = _block_rows(k_s, lanes[e], s)
                    qw = q * w
                    d = jnp.where(col_s == s, jnp.sum(qw * ks, axis=-1, keepdims=True), d)
                    da_s = da_in[CHUNK * e:CHUNK * (e + 1), REC_DIM * s:REC_DIM * (s + 1)]
                    dq = dq + da_s * ks * w
                    hi, lo = _split2(da_s * qw)
                    pc_hi[pl.ds(CHUNK * s, CHUNK), lanes[e]] = hi
                    pc_lo[pl.ds(CHUNK * s, CHUNK), lanes[e]] = lo
                ds.append(d)
                dqs.append(dq)
            dk_in = _dot(gather, pc_hi[...]) + _dot(gather, pc_lo[...])
            dq_out, dk_out, dv_out, db_out, new_dhts = [], [], [], [], []
            for e in heads:
                q, k, bcum = q_all[:, cols[e]], k_all[:, cols[e]], b_all[:, cols[e]]
                vb, dob, da = hd[e]
                dht, ht = dhts[e], st_ref[e * nc + ci]
                qst, kst, eq, eks = _hgrn_offdiag(q, k, bcum, b_s, lanes[e])
                qst_b, kst_b = qst.astype(BF16), kst.astype(BF16)
                a = jnp.where(below, _dot_nt(qst_b, kst_b), 0.0) + jnp.where(inside, ds[e], 0.0)
                da_off = jnp.where(below, da, 0.0).astype(BF16)
                dqst = _dot(da_off, kst_b)
                dkst = _dot_tn(da_off, qst_b)
                dk = dk_in[:, cols[e]]
                dq_rows = [jnp.zeros((SUB, REC_DIM), F32)]
                for i in range(1, N_SUB):
                    dq_rows.append(dqst[SUB * i:SUB * (i + 1), REC_DIM * (i - 1):REC_DIM * i])
                    dk = dk + dkst[:, REC_DIM * (i - 1):REC_DIM * i] * eks[i - 1]
                dq = dqs[e] + row_cat(dq_rows) * eq
                eb = jnp.exp(bcum)
                b_last = b_s[pl.ds(CHUNK - 1, 1), lanes[e]]
                el = jnp.exp(b_last)
                ekb = jnp.exp(b_last - bcum)
                qb = (q * eb).astype(BF16)
                kb = k * ekb
                dhb = dht.astype(BF16)
                dv_out.append(_dot_tn(a.astype(BF16), dob) + _dot_nt(kb.astype(BF16), dhb))
                dqb = _dot(dob, ht.astype(BF16))
                dkb = _dot(vb, dhb)
                new_dhts.append(dht * el + _dot_tn(dob, qb))
                dq = dq + eb * dqb
                dk = dk + ekb * dkb
                edge = jnp.sum(kb * dkb, axis=0, keepdims=True) + el * jnp.sum(ht * dht, axis=0, keepdims=True)
                db_out.append(q * dq - k * dk + jnp.where(last_row, edge, 0.0))
                dq_out.append(dq)
                dk_out.append(dk)
            dk_all = lane_cat(dk_out)
            db_hi, db_lo = _split2(lane_cat(db_out))
            dg = _dot(triu, db_hi) + _dot(triu, db_lo)
            df = dg / f - dk_all
            dz_ref[rows, :] = (df * (1.0 - lb) * sg * (1.0 - sg)).astype(dz_ref.dtype)
            dq_ref[rows, :] = lane_cat(dq_out).astype(dq_ref.dtype)
            dv_ref[rows, :] = lane_cat(dv_out).astype(dv_ref.dtype)
            return tuple(new_dhts), dlb + jnp.sum(df * (1.0 - sg), axis=0, keepdims=True)

        zero = (tuple(jnp.zeros((REC_DIM, REC_DIM), F32) for _ in heads), jnp.zeros((1, HGRN_PAIR * REC_DIM), F32))
        def trip(i, carry):
            for u in range(bwd_unroll):
                carry = chunk(bwd_unroll * i + u, carry, u)
            return carry

        _, dlb = lax.fori_loop(0, nc // bwd_unroll, trip, zero)
        lb = _sigmoid(lb_ref[0:1, :] - lb_ref[1:2, :])
        dlb_ref[...] = jnp.broadcast_to(dlb * lb * (1.0 - lb), (8, HGRN_PAIR * REC_DIM))
        comm_last()

    hp, wd = REC_HEADS // HGRN_PAIR, HGRN_PAIR * REC_DIM
    cq, cf, ci_ = (c * REC_DIM // wd for c in (COL_RQ, COL_RF, COL_RI))
    return pl.pallas_call(
        body, name="hgrn_bwd", grid=(B, hp),
        in_specs=[pl.BlockSpec((S, wd), lambda b, h: (b, cq + h)),
                  pl.BlockSpec((S, wd), lambda b, h: (b, cf + h)),
                  pl.BlockSpec((S, wd), lambda b, h: (b, ci_ + h)),
                  pl.BlockSpec((2, wd), lambda b, h: (0, h)),
                  pl.BlockSpec((HGRN_PAIR * nc, REC_DIM, REC_DIM), lambda b, h: (b * hp + h, 0, 0)),
                  pl.BlockSpec((S, wd), lambda b, h: (b, h))] + c_in_specs,
        out_specs=[pl.BlockSpec((S, wd), lambda b, h: (b, h))] * 3
        + [pl.BlockSpec((8, wd), lambda b, h: (b, h))] + [ANY] * nco,
        out_shape=[jax.ShapeDtypeStruct((T, 512), BF16)] * 3 + [jax.ShapeDtypeStruct((B * 8, 512), F32)]
        + c_out_shapes,
        scratch_shapes=[pltpu.VMEM((bwd_unroll, CHUNK, wd), F32)] * 2
        + [pltpu.VMEM((bwd_unroll, SUB * CHUNK, wd), BF16)] * 2 + c_sems,
        compiler_params=_params("arbitrary", "arbitrary"))(proj, proj, proj, lb_param, states, do, *c_arrays)


def _rec_gate_fwd(rec, proj, rec_norm):
    T = rec.shape[0]

    def fn(accs, tv, cv):
        return [_rms_hat(tv[0]) * cv[0] * _sigmoid(tv[1])]

    return _tile_call("rec_gate", fn, T, 512, _pick(T, 1024), REC_DIM, tiles=[(rec, 0), (proj, COL_RG)],
                      consts=[rec_norm], outs=[BF16])[0]


def _rec_gate_bwd(dyb, w_rec_proj, rec, proj, rec_norm):
    T = rec.shape[0]

    def fn(accs, tv, cv):
        d, r, rg = accs[0], tv[0], tv[1]
        sg = _sigmoid(rg)
        rn = _rms_hat(r) * cv[0]
        dh, dg = _rms_bwd_vals(d * sg, r, cv[0])
        return [dh, d * rn * sg * (1.0 - sg), dg]

    return _tile_call("rec_gate_bwd", fn, T, 512, _pick(T, 1024), REC_DIM, pairs=[(dyb, 0, w_rec_proj, "nt")],
                      tiles=[(rec, 0), (proj, COL_RG)], consts=[rec_norm], outs=[F32, BF16], parts=1)


def _mix_out_fwd(att, recn, proj, w_att_proj, w_rec_proj, w_out, h1, g_next):
    T = att.shape[0]
    tn = 256

    def merge(accs, tv, cv):
        ya, yb = accs
        return [ya, yb, _sigmoid(tv[0]) * ya + _sigmoid(tv[1]) * yb]

    ya, yb, merged = _tile_call(
        "merge", merge, T, D_MODEL, _pick(T, 1024), tn,
        pairs=[(att, 0, w_att_proj, "nn"), (recn, 0, w_rec_proj, "nn")],
        tiles=[(proj, COL_GA * 128 // tn), (proj, COL_GB * 128 // tn)], outs=[BF16] * 3)

    def res(accs, tv, cv):
        h2 = tv[0] + accs[0]
        return [h2, _rms_hat(h2) * cv[0]]

    h2, n2 = _tile_call("mix_out", res, T, D_MODEL, _pick(T, 512), D_MODEL, pairs=[(merged, 0, w_out, "nn")],
                        tiles=[(h1, 0)], consts=[g_next], outs=[F32, BF16])
    return h2, n2, (ya, yb, merged)


GATHER_FIRST = ("w_ffn1_in",)
GATHER_MIX = ("w_ffn1_out", "w_in")
GATHER_PROJ = ("w_att_proj", "w_rec_proj", "w_out")
GATHER_TAIL = ("w_ffn2_out", "w_ple_gate", "w_ple_proj")
GATHER_LAST = ("w_ffn2_in",)
SCATTER_LATE = ("w_ple_gate", "w_ple_proj", "w_ffn2_in", "w_ffn2_out")
SCATTER_MIX = ("w_out", "w_att_proj", "w_rec_proj", "w_in")
SCATTER_LAST = ("w_ffn1_in", "w_ffn1_out")


def _local_step(x, p, tgt, w, mine16, cc, me_chip, B, S):
    T = B * S
    w = dict(w)
    g_ffn1, g_mix, g_ffn2, g_ple = w["norm_ffn1"], w["norm_mix"], w["norm_ffn2"], w["norm_ple"]
    g_fin = w["norm_final"].reshape(1, D_MODEL)
    grads, part, from_chips = {}, {}, {}

    def gather(names):
        return _gather_comm([mine16[n] for n in names])

    def place(names, got):
        for n, g in zip(names, got):
            full = lax.dynamic_update_index_in_dim(g, mine16[n], me_chip, 0)
            w[n] = full if n in ("w_ffn1_in", "w_ffn2_in") else _natural(n, full)

    def swap(names):
        return _swap_comm([grads[n][1] for n in names])

    def after_swap(names, from_sib):
        for n, fs in zip(names, from_sib):
            part[n] = _add_sibling("rs_add_sib_" + n, grads[n][0], fs, cc)
        return _scatter_comm([part[n][1] for n in names])

    def scattered(names, got):
        for n, g in zip(names, got):
            from_chips[n] = g

    def ffn1_in_weight(got):
        place(GATHER_FIRST, got)
        return w["w_ffn1_in"]

    def ffn1_out_weight(got):
        place(GATHER_MIX, got)
        return w["w_ffn1_out"]

    h1, u, sv1, got_proj = _ffn_fwd("ffn1", x, g_ffn1, None, None, g_mix, comm_norm=gather(GATHER_FIRST),
                                    w_in_of=ffn1_in_weight, comm_in=gather(GATHER_MIX),
                                    comm_out=gather(GATHER_PROJ), w_out_of=ffn1_out_weight)
    place(GATHER_PROJ, got_proj)

    def ident(accs, tv, cv):
        return [accs[0]]

    proj, *got_tail = _tile_call("in_proj", ident, T, IN_W, _pick(T, 512), IN_W // 2,
                                 pairs=[(u, 0, w["w_in"], "nn")], outs=[F32], j_outer=True, comm=gather(GATHER_TAIL))
    place(GATHER_TAIL, got_tail)
    onehot = jnp.asarray(_t5_onehot())
    bias = _small_mm("t5_bias", w["rel_bias"].T, onehot.astype(BF16), "right")
    bias = bias.reshape(N_Q_HEADS, ATT_BLOCK, 2 * ATT_BLOCK)
    sinks = w["attn_sinks"].reshape(N_Q_HEADS)
    kk2, vv2 = _kv_layouts(proj)
    att = _swa_fwd(proj, kk2, vv2, bias, sinks, B, S)
    rec, states, *got_last = _hgrn_fwd(proj, w["lb_param"], B, S, comm=gather(GATHER_LAST))
    place(GATHER_LAST, got_last)
    recn = _rec_gate_fwd(rec, proj, w["rec_norm"])
    h2, n2, (ya, yb, merged) = _mix_out_fwd(att, recn, proj, w["w_att_proj"], w["w_rec_proj"], w["w_out"], h1,
                                            g_ffn2)
    h3, n3, sv2, _ = _ffn_fwd("ffn2", h2, g_ffn2, w["w_ffn2_in"], w["w_ffn2_out"], g_ple, n=n2)

    def ple(accs, tv, cv):
        gate = _sigmoid(accs[0])
        return [gate, accs[1], tv[0] + gate * accs[1]]

    gate_p, pp, h4 = _tile_call(
        "ple", ple, T, D_MODEL, _pick(T, 512), D_MODEL,
        pairs=[(n3, 0, w["w_ple_gate"], "nn"), (p, 0, w["w_ple_proj"], "nn")], tiles=[(h3, 0)],
        outs=[BF16, BF16, F32])

    def head(accs, tv, cv):
        h, t, gt, ppv = tv[0], tv[1], tv[2].astype(F32), tv[3].astype(F32)
        err = _rms_hat(h) * cv[0] - t
        dh, dg = _rms_bwd_vals(err * (1.0 / D_MODEL), h, cv[0])
        return [dh, dh * ppv * gt * (1.0 - gt), dh * gt, _group8(err * err), dg]

    dh4, dzg, dpp, loss_p, dg_fin = _tile_call(
        "loss_head", head, T, D_MODEL, _pick(T, 512), D_MODEL,
        tiles=[(h4, 0), (tgt, 0), (gate_p, 0), (pp, 0)], consts=[g_fin], outs=[F32, BF16, BF16], parts=2)
    grads["norm_final"] = dg_fin

    grads["w_ple_gate"] = _mm_tn_rows("ple_dwg", n3, dzg)
    grads["w_ple_proj"] = _mm_tn_cols("ple_dwp", p, dpp)

    def dnorm(accs, tv, cv):
        dh, dg = _rms_bwd_vals(accs[0], tv[0], cv[0])
        dh = tv[1] + dh
        return [dh, 0.5 * dh, dg]

    dh3, df3, grads["norm_ple"] = _tile_call(
        "ple_dnorm", dnorm, T, D_MODEL, _pick(T, 512), D_MODEL, pairs=[(dzg, 0, w["w_ple_gate"], "nt")],
        tiles=[(h3, 0), (dh4, 0)], consts=[g_ple], outs=[F32, BF16], parts=1)

    def swap_late(dw_in, dw_out):
        grads["w_ffn2_in"], grads["w_ffn2_out"] = dw_in, dw_out
        return swap(SCATTER_LATE)

    dh2, dh2b, grads["norm_ffn2"], _, _, _, from_sib = _ffn_bwd(
        "ffn2b", dh3, df3, h2, g_ffn2, w["w_ffn2_in"], w["w_ffn2_out"], sv2, comm_last=swap_late)
    scatter_late = after_swap(SCATTER_LATE, from_sib)

    grads["w_out"] = _mm_tn_rows("mix_dwout", merged, dh2b)
    tn = 256

    def dmerge(accs, tv, cv):
        dm = accs[0]
        sa, sb = _sigmoid(tv[0]), _sigmoid(tv[1])
        yav, ybv = tv[2].astype(F32), tv[3].astype(F32)
        return [dm * sa, dm * sb, dm * yav * sa * (1.0 - sa), dm * ybv * sb * (1.0 - sb)]

    dya, dyb, dga, dgb = _tile_call(
        "mix_dmerge", dmerge, T, D_MODEL, _pick(T, 1024), tn, pairs=[(dh2b, 0, w["w_out"], "nt")],
        tiles=[(proj, COL_GA * 128 // tn), (proj, COL_GB * 128 // tn), (ya, 0), (yb, 0)], outs=[BF16] * 4)
    grads["w_att_proj"] = _mm_tn_cols("mix_dwatt", att, dya)
    grads["w_rec_proj"] = _mm_tn_cols("mix_dwrec", recn, dyb)

    datt = _tile_call("mix_datt", ident, T, 512, _pick(T, 1024), 512, pairs=[(dya, 0, w["w_att_proj"], "nt")],
                      outs=[BF16])[0]
    drec, drg, grads["rec_norm"] = _rec_gate_bwd(dyb, w["w_rec_proj"], rec, proj, w["rec_norm"])

    drq, drf, dri, dlb, *got = _hgrn_bwd(proj, w["lb_param"], states, drec, B, S, comm=scatter_late)
    scattered(SCATTER_LATE, got)
    grads["lb_param"] = dlb
    daq, dak, dav, dbias, dsink = _swa_bwd(proj, kk2, vv2, bias, sinks, datt, B, S)
    grads["attn_sinks"] = dsink
    grads["rel_bias"] = _small_mm("t5_dbias", dbias.reshape(N_Q_HEADS, -1), onehot.T.astype(BF16), "right")
    dproj = jnp.concatenate([daq, dak, dav, drq, drf, dri, drg, dga, dgb], axis=1)
    tk = _pick(T, 2048, 128)
    w_in_shard = IN_W // N_CHIPS
    half_d = D_MODEL // 2
    gw32, gw16 = _mm_tn("mix_dwin", (2, 2, T // tk),
                        (u, (tk, half_d), lambda i, j, k: (k, i)), (dproj, (tk, IN_W // 2), lambda i, j, k: (k, j)),
                        _grad_pair((D_MODEL, IN_W), (half_d, IN_W // 2), lambda i, j, k: (i, j)))
    to_sh = lambda t: t.reshape(D_MODEL, N_CHIPS, w_in_shard).transpose(1, 0, 2)
    grads["w_in"] = (to_sh(gw32), to_sh(gw16))

    def dnorm_mix(accs, tv, cv):
        dh, dg = _rms_bwd_vals(accs[0], tv[0], cv[0])
        dh = tv[1] + dh
        return [dh, 0.5 * dh, dg]

    dh1, df1, grads["norm_mix"], *from_sib = _tile_call(
        "mix_dnorm", dnorm_mix, T, D_MODEL, _pick(T, 512), D_MODEL, pairs=[(dproj, 0, w["w_in"], "nt")],
        tiles=[(h1, 0), (dh2, 0)], consts=[g_mix], outs=[F32, BF16], parts=1, comm=swap(SCATTER_MIX))
    scatter_mix = after_swap(SCATTER_MIX, from_sib)

    def scatter_last(dw_in, dw_out):
        grads["w_ffn1_in"], grads["w_ffn1_out"] = dw_in, dw_out
        return after_swap(SCATTER_LAST, _run_comm("rs_sibling_last", swap(SCATTER_LAST)))

    dx, _, grads["norm_ffn1"], _, _, got, got_last = _ffn_bwd(
        "ffn1b", dh1, df1, x, g_ffn1, w["w_ffn1_in"], w["w_ffn1_out"], sv1, comm=scatter_mix, comm_last=scatter_last)
    scattered(SCATTER_MIX, got)
    scattered(SCATTER_LAST, got_last)
    return loss_p, dx, grads, part, from_chips


def _place():
    x, y, c = lax.axis_index("x"), lax.axis_index("y"), lax.axis_index("c")
    return x, y, c


def _other_chips(x, y):
    return [(1 - x, y, 2 * (1 - x) + y), (x, 1 - y, 2 * x + 1 - y), (1 - x, 1 - y, 2 * (1 - x) + 1 - y)]


def _half_rows(ref_3d, chip, h, rows):
    return ref_3d.at[chip, pl.ds(h * rows, rows), :]


def _gather_comm(ws):
    nw = len(ws)

    def parts(w_refs, out_refs, send_sems, recv_sems):
        x, y, c = _place()
        me = 2 * x + y
        chips = _other_chips(x, y)

        def copy(i, k, chip, h, to, src=None):
            half = ws[i].shape[0] // 2
            dst = _half_rows(out_refs[i], chip, h, half)
            return pltpu.make_async_remote_copy(
                src_ref=dst if src is None else src, dst_ref=dst,
                send_sem=send_sems.at[6 * i + k], recv_sem=recv_sems.at[6 * i + k], device_id=to, device_id_type=MESH)

        def first():
            out = []
            for i in range(nw):
                half = ws[i].shape[0] // 2
                out += [copy(i, j, me, c, (cx, cy, c), src=w_refs[i].at[pl.ds(c * half, half), :])
                        for j, (cx, cy, _) in enumerate(chips)]
            return out

        return copy, first, chips, c, (x, y, 1 - c)

    def start(*refs):
        _, first, _, _, _ = parts(*refs)
        for cp in first():
            cp.start()

    def finish(*refs):
        copy, first, chips, c, sibling = parts(*refs)
        passed = []
        for i in range(nw):
            for j, (cx, cy, ci) in enumerate(chips):
                copy(i, j, ci, c, (cx, cy, c)).wait_recv()
                fw = copy(i, 3 + j, ci, c, sibling)
                fw.start()
                passed.append(fw)
        for i in range(nw):
            for j, (_, _, ci) in enumerate(chips):
                copy(i, 3 + j, ci, 1 - c, sibling).wait_recv()
        for cp in first() + passed:
            cp.wait_send()

    return _Comm(list(ws), [jax.ShapeDtypeStruct((N_CHIPS,) + w.shape, w.dtype) for w in ws], 6 * nw, start, finish)


def _scatter_comm(ps):
    nw = len(ps)

    def copies(p_refs, out_refs, send_sems, recv_sems):
        x, y, c = _place()
        cps = []
        for i in range(nw):
            for j, (cx, cy, ci) in enumerate(_other_chips(x, y)):
                cps.append(pltpu.make_async_remote_copy(
                    src_ref=p_refs[i].at[ci], dst_ref=out_refs[i].at[j], send_sem=send_sems.at[3 * i + j],
                    recv_sem=recv_sems.at[3 * i + j], device_id=(cx, cy, c), device_id_type=MESH))
        return cps

    def start(*refs):
        for cp in copies(*refs):
            cp.start()

    def finish(*refs):
        for cp in copies(*refs):
            cp.wait()

    return _Comm(list(ps), [jax.ShapeDtypeStruct((3,) + p.shape[1:], p.dtype) for p in ps], 3 * nw, start, finish)


def _swap_comm(gs):
    nw = len(gs)

    def copies(g_refs, out_refs, send_sems, recv_sems):
        x, y, c = _place()
        cps = []
        for i in range(nw):
            half = gs[i].shape[1] // 2
            cps.append(pltpu.make_async_remote_copy(
                src_ref=g_refs[i].at[:, pl.ds((1 - c) * half, half), :], dst_ref=out_refs[i],
                send_sem=send_sems.at[i], recv_sem=recv_sems.at[i], device_id=(x, y, 1 - c), device_id_type=MESH))
        return cps

    def start(*refs):
        for cp in copies(*refs):
            cp.start()

    def finish(*refs):
        for cp in copies(*refs):
            cp.wait()

    return _Comm(list(gs), [jax.ShapeDtypeStruct((N_CHIPS, g.shape[1] // 2, g.shape[2]), g.dtype) for g in gs],
                 nw, start, finish)


def _run_comm(name, comm):
    nci, nco = len(comm.ins), len(comm.out_shapes)

    def body(*refs):
        cin, cout, send_sems, recv_sems = refs[:nci], refs[nci:nci + nco], refs[-2], refs[-1]
        comm.start(cin, cout, send_sems, recv_sems)
        comm.finish(cin, cout, send_sems, recv_sems)

    return pl.pallas_call(
        body, name=name, in_specs=[ANY] * nci, out_specs=[ANY] * nco, out_shape=list(comm.out_shapes),
        scratch_shapes=[pltpu.SemaphoreType.DMA((comm.n_sems,)), pltpu.SemaphoreType.DMA((comm.n_sems,))],
    )(*comm.ins)


def _join_halves(name, ss):
    nw = len(ss)

    def body(*refs):
        s_refs, out_refs, send_sems, recv_sems = refs[:nw], refs[nw:2 * nw], refs[2 * nw], refs[2 * nw + 1]
        x, y, c = _place()
        cps = [pltpu.make_async_remote_copy(
            src_ref=s_refs[i], dst_ref=out_refs[i], send_sem=send_sems.at[i], recv_sem=recv_sems.at[i],
            device_id=(x, y, 1 - c), device_id_type=MESH) for i in range(nw)]
        for cp in cps:
            cp.start()
        for cp in cps:
            cp.wait()

    return pl.pallas_call(
        body, name=name, in_specs=[ANY] * nw, out_specs=[ANY] * nw,
        out_shape=[jax.ShapeDtypeStruct(s.shape, s.dtype) for s in ss],
        scratch_shapes=[pltpu.SemaphoreType.DMA((nw,)), pltpu.SemaphoreType.DMA((nw,))],
    )(*ss)


def _allreduce_small(sp):
    def body(s_ref, out_ref, slots, send_sems, recv_sems):
        x, y, c = _place()
        me = 4 * x + 2 * y + c
        slots[me] = s_ref[...]
        cps = []
        for r in range(1, N_DEV):
            px, py, pc = x ^ (r >> 2), y ^ ((r >> 1) & 1), c ^ (r & 1)
            cps.append(pltpu.make_async_remote_copy(
                src_ref=s_ref, dst_ref=slots.at[me], send_sem=send_sems.at[r - 1], recv_sem=recv_sems.at[r - 1],
                device_id=(px, py, pc), device_id_type=MESH))
        for cp in cps:
            cp.start()
        for r in range(1, N_DEV):
            px, py, pc = x ^ (r >> 2), y ^ ((r >> 1) & 1), c ^ (r & 1)
            pltpu.make_async_remote_copy(
                src_ref=s_ref, dst_ref=slots.at[4 * px + 2 * py + pc], send_sem=send_sems.at[r - 1],
                recv_sem=recv_sems.at[r - 1], device_id=(px, py, pc), device_id_type=MESH).wait_recv()
        for cp in cps:
            cp.wait_send()
        acc = slots[0]
        for d in range(1, N_DEV):
            acc = acc + slots[d]
        out_ref[...] = acc

    return pl.pallas_call(
        body, name="allreduce_small",
        in_specs=[pl.BlockSpec(memory_space=pltpu.VMEM)], out_specs=pl.BlockSpec(memory_space=pltpu.VMEM),
        out_shape=jax.ShapeDtypeStruct(sp.shape, F32),
        scratch_shapes=[pltpu.VMEM((N_DEV,) + sp.shape, F32), pltpu.SemaphoreType.DMA((N_DEV - 1,)),
                        pltpu.SemaphoreType.DMA((N_DEV - 1,))],
    )(sp)


def _scalar(v):
    return jnp.reshape(v, (1,)).astype(jnp.int32)


def _row_tile(h, dtype_mult=16):
    return _pick(h, 256, dtype_mult)


def _add_sibling(name, g32, from_sib, c):
    _, r, n = g32.shape
    h = r // 2
    th = _row_tile(h)
    nt = h // th

    def body(c_ref, g_ref, s_ref, o32_ref, o16_ref):
        s = g_ref[...] + s_ref[...].astype(F32)
        o32_ref[...] = s
        o16_ref[...] = s.astype(BF16)

    blk = (None, th, n)
    return pl.pallas_call(
        body, name=name,
        grid_spec=pltpu.PrefetchScalarGridSpec(
            num_scalar_prefetch=1, grid=(N_CHIPS, nt),
            in_specs=[pl.BlockSpec(blk, lambda k, t, c_ref: (k, c_ref[0] * nt + t, 0)),
                      pl.BlockSpec(blk, lambda k, t, c_ref: (k, t, 0))],
            out_specs=[pl.BlockSpec(blk, lambda k, t, c_ref: (k, t, 0))] * 2),
        out_shape=[jax.ShapeDtypeStruct((N_CHIPS, h, n), F32), jax.ShapeDtypeStruct((N_CHIPS, h, n), BF16)],
        compiler_params=_params("arbitrary", "arbitrary"))(_scalar(c), g32, from_sib)


def _add_chips(name, p32, from_chips, me_chip):
    _, h, n = p32.shape
    th = _row_tile(h)

    def body(m_ref, p_ref, a_ref, b_ref, c_ref, o_ref):
        o_ref[...] = p_ref[...] + a_ref[...].astype(F32) + b_ref[...].astype(F32) + c_ref[...].astype(F32)

    blk = (None, th, n)
    return pl.pallas_call(
        body, name=name,
        grid_spec=pltpu.PrefetchScalarGridSpec(
            num_scalar_prefetch=1, grid=(h // th,),
            in_specs=[pl.BlockSpec(blk, lambda t, m_ref: (m_ref[0], t, 0))]
            + [pl.BlockSpec(blk, lambda t, m_ref, j=j: (j, t, 0)) for j in range(3)],
            out_specs=pl.BlockSpec((th, n), lambda t, m_ref: (t, 0))),
        out_shape=jax.ShapeDtypeStruct((h, n), F32),
        compiler_params=_params("arbitrary"))(_scalar(me_chip), p32, from_chips, from_chips, from_chips)


def _adamw_vals(w, g, m, v):
    m = ADAM_B1 * m + (1.0 - ADAM_B1) * g
    v = ADAM_B2 * v + (1.0 - ADAM_B2) * (g * g)
    m_hat = m / (1.0 - ADAM_B1 ** ADAM_STEP)
    v_hat = v / (1.0 - ADAM_B2 ** ADAM_STEP)
    delta = -ADAM_LR * (m_hat / (jnp.sqrt(v_hat) + ADAM_EPS) + ADAM_WD * w)
    return delta, m, v


def _adamw_halves(name, w, m, v, g_mine, g_sib, c):
    r, n = w.shape
    h = r // 2
    th = _row_tile(h, 8)
    nt = h // th

    def body(c_ref, w_ref, m_ref, v_ref, a_ref, b_ref, g_ref, d_ref, nm_ref, nv_ref):
        mine = (pl.program_id(0) // nt) == c_ref[0]
        g = jnp.where(mine, a_ref[...], b_ref[...])
        d, nm, nv = _adamw_vals(w_ref[...], g, m_ref[...], v_ref[...])
        g_ref[...] = g
        d_ref[...] = d
        nm_ref[...] = nm
        nv_ref[...] = nv

    full = pl.BlockSpec((th, n), lambda t, c_ref: (t, 0))
    own_half = pl.BlockSpec((th, n), lambda t, c_ref: (jnp.where(t // nt == c_ref[0], t % nt, 0), 0))
    sib_half = pl.BlockSpec((th, n), lambda t, c_ref: (jnp.where(t // nt == c_ref[0], 0, t % nt), 0))
    return pl.pallas_call(
        body, name=name,
        grid_spec=pltpu.PrefetchScalarGridSpec(
            num_scalar_prefetch=1, grid=(2 * nt,), in_specs=[full, full, full, own_half, sib_half],
            out_specs=[full] * 4),
        out_shape=[jax.ShapeDtypeStruct((r, n), F32)] * 4,
        compiler_params=_params("arbitrary"))(_scalar(c), w, m, v, g_mine, g_sib)


def _adamw(name, w, g, m, v):
    R, W = w.shape

    def fn(accs, tv, cv):
        return list(_adamw_vals(*tv))

    return _tile_call(name, fn, R, W, _pick(R, 256), W, tiles=[(w, 0), (g, 0), (m, 0), (v, 0)], outs=[F32] * 3)


SMALL_LAYOUT = (("rel_bias", 2, 256), ("lb_param", 8, 1024), ("norm_ffn1", 8, 1024), ("norm_mix", 8, 1024),
                ("attn_sinks", 1, 8), ("rec_norm", 1, 128), ("norm_ffn2", 8, 1024), ("norm_ple", 8, 1024),
                ("norm_final", 8, 1024), ("loss", 8, 1024))


def _pack_small(vals):
    rows = []
    for name, nrows, n in SMALL_LAYOUT:
        flat = vals[name].reshape(-1)
        flat = jnp.pad(flat, (0, nrows * 128 - n))
        rows.append(flat.reshape(nrows, 128))
    packed = jnp.concatenate(rows, axis=0)
    return jnp.pad(packed, ((0, SMALL_ROWS - packed.shape[0]), (0, 0)))


def _unpack_small(packed, shapes):
    out, r = {}, 0
    for name, nrows, n in SMALL_LAYOUT:
        out[name] = packed[r:r + nrows].reshape(-1)[:n].reshape(shapes[name])
        r += nrows
    return out


def _natural(name, s):
    if name in COL_SHARDED:
        return s.transpose(1, 0, 2).reshape(s.shape[1], -1)
    return s.reshape(-1, s.shape[2])


def kernel(x, p, rel_bias, lb_param, norm_ffn1, w_ffn1_in, w_ffn1_out, norm_mix, w_in, attn_sinks, rec_norm, w_att_proj, w_rec_proj, w_out, norm_ffn2, w_ffn2_in, w_ffn2_out, norm_ple, w_ple_gate, w_ple_proj, norm_final, loss_target, m_rel_bias, m_lb_param, m_norm_ffn1, m_w_ffn1_in, m_w_ffn1_out, m_norm_mix, m_w_in, m_attn_sinks, m_rec_norm, m_w_att_proj, m_w_rec_proj, m_w_out, m_norm_ffn2, m_w_ffn2_in, m_w_ffn2_out, m_norm_ple, m_w_ple_gate, m_w_ple_proj, m_norm_final, v_rel_bias, v_lb_param, v_norm_ffn1, v_w_ffn1_in, v_w_ffn1_out, v_norm_mix, v_w_in, v_attn_sinks, v_rec_norm, v_w_att_proj, v_w_rec_proj, v_w_out, v_norm_ffn2, v_w_ffn2_in, v_w_ffn2_out, v_norm_ple, v_w_ple_gate, v_w_ple_proj, v_norm_final):
    args = dict(locals())
    wsh = {n: args[n] for n in WEIGHTS}
    B, S = x.shape[0], x.shape[1]
    T = B * S
    cx, cy, cc = _place()
    me_chip = 2 * cx + cy

    mine16 = {n: wsh[n][0].astype(BF16) for n in BIG}
    loss_p, dx, grads, part, from_chips = _local_step(
        x.reshape(T, D_MODEL), p.reshape(T, PLE_DIM), loss_target.reshape(T, D_MODEL),
        {n: wsh[n] for n in SMALL}, mine16, cc, me_chip, B, S)

    s_mine = [_add_chips("rs_add_chips_" + n, part[n][0], from_chips[n], me_chip) for n in BIG]
    s_sib = _join_halves("rs_join", s_mine)

    small_vals = {
        "rel_bias": grads["rel_bias"].T,
        "lb_param": jnp.concatenate([_colsum("dlb_sum", grads["lb_param"]),
                                     -_colsum("dlb_sum2", grads["lb_param"])], axis=0) / 8.0,
        "attn_sinks": grads["attn_sinks"][:, 0],
        "rec_norm": _colsum("drn_sum", grads["rec_norm"]).reshape(REC_HEADS, REC_DIM).sum(axis=0),
        "loss": _colsum("loss_sum", loss_p),
    }
    for n in ("norm_ffn1", "norm_mix", "norm_ffn2", "norm_ple", "norm_final"):
        small_vals[n] = _colsum(n + "_sum", grads[n])
    red = _allreduce_small(_pack_small(small_vals))
    small_shapes = {n: wsh[n].shape for n in SMALL}
    small_shapes["loss"] = (D_MODEL,)
    small = _unpack_small(red, small_shapes)
    loss = 0.5 * jnp.sum(small["loss"]) / D_MODEL

    out_g, out_d, out_m, out_v = {}, {}, {}, {}
    for n, gm, gs in zip(BIG, s_mine, s_sib):
        res = _adamw_halves("adamw_" + n, wsh[n][0], args["m_" + n][0], args["v_" + n][0], gm, gs, cc)
        out_g[n], out_d[n], out_m[n], out_v[n] = (t[None] for t in res)
    sw = _pack_small({**{n: wsh[n] for n in SMALL}, "loss": jnp.zeros((D_MODEL,), F32)})
    sm = _pack_small({**{n: args["m_" + n] for n in SMALL}, "loss": jnp.zeros((D_MODEL,), F32)})
    sv = _pack_small({**{n: args["v_" + n] for n in SMALL}, "loss": jnp.ones((D_MODEL,), F32)})
    sd, snm, snv = _adamw("adamw_small", sw, red, sm, sv)
    ud, um, uv = (_unpack_small(t, small_shapes) for t in (sd, snm, snv))
    for n in SMALL:
        out_g[n], out_d[n], out_m[n], out_v[n] = small[n], ud[n], um[n], uv[n]

    return (loss, dx.reshape(B, S, D_MODEL), *[out_g[n] for n in WEIGHTS], *[out_d[n] for n in WEIGHTS],
            *[out_m[n] for n in WEIGHTS], *[out_v[n] for n in WEIGHTS])
```

```python
import numpy as np
import jax
import jax.numpy as jnp
from jax import lax
from jax.experimental import pallas as pl
from jax.experimental.pallas import tpu as pltpu

F32 = jnp.float32
BF16 = jnp.bfloat16
MESH = pl.DeviceIdType.MESH

D_MODEL = 1024
D_FF = 2816
FF_SHARD = 2 * D_FF // 4
HEAD_DIM = 64
N_Q_HEADS = 8
ATT_BLOCK = 128
N_BUCKETS = 32
MAX_DISTANCE = 128
REC_HEADS = 4
REC_DIM = 128
PLE_DIM = 256
EPS = 1e-6
IN_W = 4864
COL_AQ, COL_AK, COL_AV, COL_RQ, COL_RF, COL_RI, COL_RG, COL_GA, COL_GB = 0, 4, 5, 6, 10, 14, 18, 22, 30

CHUNK = 64
SUB = 8
N_SUB = CHUNK // SUB
HGRN_PAIR = 2

ADAM_LR, ADAM_B1, ADAM_B2, ADAM_EPS, ADAM_WD, ADAM_STEP = 0.001, 0.9, 0.999, 1e-08, 0.01, 10

V7X_VMEM_LIMIT = 56 * 1024 * 1024
N_CHIPS = 4
N_DEV = 8

BIG = ("w_ffn1_in", "w_ffn1_out", "w_in", "w_att_proj", "w_rec_proj", "w_out",
       "w_ffn2_in", "w_ffn2_out", "w_ple_gate", "w_ple_proj")
COL_SHARDED = ("w_ffn1_in", "w_in", "w_att_proj", "w_rec_proj", "w_ffn2_in", "w_ple_proj")
WEIGHTS = ("rel_bias", "lb_param", "norm_ffn1", "w_ffn1_in", "w_ffn1_out", "norm_mix", "w_in", "attn_sinks",
           "rec_norm", "w_att_proj", "w_rec_proj", "w_out", "norm_ffn2", "w_ffn2_in", "w_ffn2_out", "norm_ple",
           "w_ple_gate", "w_ple_proj", "norm_final")
SMALL = tuple(n for n in WEIGHTS if n not in BIG)
SMALL_ROWS = 64


def _params(*sem):
    return pltpu.CompilerParams(dimension_semantics=sem, vmem_limit_bytes=V7X_VMEM_LIMIT)


def _pick(n, cap, mult=8):
    if n <= cap:
        return n
    for t in range(cap - cap % mult, 0, -mult):
        if n % t == 0:
            return t
    raise ValueError((n, cap, mult))


def _dot(a, b):
    return jnp.dot(a, b, preferred_element_type=F32)


def _dot_nt(a, b):
    return lax.dot_general(a, b, (((1,), (1,)), ((), ())), preferred_element_type=F32)


def _dot_tn(a, b):
    return lax.dot_general(a, b, (((0,), (0,)), ((), ())), preferred_element_type=F32)


def _split3(x):
    hi = x.astype(BF16)
    r = x - hi.astype(F32)
    mid = r.astype(BF16)
    lo = (r - mid.astype(F32)).astype(BF16)
    return hi, mid, lo


def _split2(x):
    hi = x.astype(BF16)
    return hi, (x - hi.astype(F32)).astype(BF16)


def _sel_left(sel_bf16, x):
    hi, mid, lo = _split3(x)
    return _dot(sel_bf16, hi) + _dot(sel_bf16, mid) + _dot(sel_bf16, lo)


def _sel_right(x, sel_bf16):
    hi, mid, lo = _split3(x)
    return _dot(hi, sel_bf16) + _dot(mid, sel_bf16) + _dot(lo, sel_bf16)


def _sigmoid(x):
    return 0.5 * jnp.tanh(0.5 * x) + 0.5


def _group8(x):
    r, w = x.shape
    return x.reshape(r // 8, 8, w).sum(axis=0)


class _Comm:
    def __init__(self, ins, out_shapes, n_sems, start, finish):
        self.ins, self.out_shapes, self.n_sems, self.start, self.finish = ins, out_shapes, n_sems, start, finish


ANY = pl.BlockSpec(memory_space=pl.ANY)


def _comm_parts(comm):
    if comm is None:
        return [], [], [], []
    sems = [pltpu.SemaphoreType.DMA((comm.n_sems,)), pltpu.SemaphoreType.DMA((comm.n_sems,))]
    return list(comm.ins), [ANY] * len(comm.ins), list(comm.out_shapes), sems


def _comm_run(comm, grid, refs, n_in, n_out):
    if comm is None:
        return (lambda: None), (lambda: None)
    nci, nco = len(comm.ins), len(comm.out_shapes)
    cin = refs[n_in:n_in + nci]
    cout = refs[n_in + nci + n_out:n_in + nci + n_out + nco]
    send_sems, recv_sems = refs[-2], refs[-1]
    ids = [pl.program_id(d) for d in range(len(grid))]
    is_first = ids[0] == 0
    is_last = ids[0] == grid[0] - 1
    for d in range(1, len(grid)):
        is_first = is_first & (ids[d] == 0)
        is_last = is_last & (ids[d] == grid[d] - 1)

    def first():
        @pl.when(is_first)
        def _():
            comm.start(cin, cout, send_sems, recv_sems)

    def last():
        @pl.when(is_last)
        def _():
            comm.finish(cin, cout, send_sems, recv_sems)

    return first, last


def _call(name, fn, grid, ins, outs, pairs=(), comm=None, j_outer=False):
    in_pair = {i for p in pairs for i in p[:2]}
    n_in, n_out = len(ins), len(outs)
    c_arrays, c_in_specs, c_out_shapes, c_sems = _comm_parts(comm)

    def body(*refs):
        first, last = _comm_run(comm, grid, refs, n_in, n_out)
        first()
        accs = []
        for ia, ib, kind in pairs:
            a, b = refs[ia][...].astype(BF16), refs[ib][...].astype(BF16)
            accs.append(_dot(a, b) if kind == "nn" else _dot_nt(a, b))
        vals = [refs[i][...] for i in range(n_in) if i not in in_pair]
        res = fn(accs, vals)
        out_refs = refs[n_in + len(c_arrays):n_in + len(c_arrays) + n_out]
        assert len(res) == len(out_refs), (name, len(res), len(out_refs))
        for o_ref, val in zip(out_refs, res):
            o_ref[...] = val.astype(o_ref.dtype)
        last()

    if j_outer:
        grid = (grid[1], grid[0])
        swap = lambda im: (lambda j, i: im(i, j))
        ins = [(a, blk, swap(im)) for a, blk, im in ins]
        outs = [(shp, dt, blk, swap(im)) for shp, dt, blk, im in outs]

    return pl.pallas_call(
        body, name=name, grid=grid,
        in_specs=[pl.BlockSpec(blk, im) for _, blk, im in ins] + c_in_specs,
        out_specs=[pl.BlockSpec(blk, im) for _, _, blk, im in outs] + [ANY] * len(c_out_shapes),
        out_shape=[jax.ShapeDtypeStruct(shp, dt) for shp, dt, _, _ in outs] + c_out_shapes,
        scratch_shapes=c_sems,
        compiler_params=_params(*(["arbitrary"] * len(grid))))(*[a for a, _, _ in ins], *c_arrays)


def _tile_call(name, fn, M, N, tm, tn, *, pairs=(), tiles=(), consts=(), outs=(), parts=0, comm=None,
               j_outer=False):
    gi, gj = M // tm, N // tn
    assert gi * tm == M and gj * tn == N, (name, M, N, tm, tn)
    ins, prs = [], []
    for a, a_col, b, kind in pairs:
        K = b.shape[0] if kind == "nn" else b.shape[1]
        ins.append((a, (tm, K), lambda i, j, c=a_col: (i, c)))
        if kind == "nn":
            ins.append((b, (K, tn), lambda i, j: (0, j)))
        else:
            ins.append((b, (tn, K), lambda i, j: (j, 0)))
        prs.append((len(ins) - 2, len(ins) - 1, kind))
    for arr, off in tiles:
        ins.append((arr, (tm, tn), lambda i, j, o=off: (i, j + o)))
    for arr in consts:
        ins.append((arr, arr.shape, lambda i, j: (0, 0)))
    out_l = [((M, N), dt, (tm, tn), lambda i, j: (i, j)) for dt in outs]
    out_l += [((gi * 8, N), F32, (8, tn), lambda i, j: (i, j))] * parts
    nt = len(tiles)

    def wrapped(accs, vals):
        return fn(accs, vals[:nt], vals[nt:])

    return _call(name, wrapped, (gi, gj), ins, out_l, prs, comm=comm, j_outer=j_outer)


def _mm_tn(name, grid, a_in, b_in, outs):
    nk = grid[2]
    tm = [d for d in a_in[1] if d is not None][1]
    tn = [d for d in b_in[1] if d is not None][1]

    def body(a_ref, b_ref, *rest):
        out_refs, acc_ref = rest[:-1], rest[-1]
        k = pl.program_id(2)

        @pl.when(k == 0)
        def _():
            acc_ref[...] = jnp.zeros_like(acc_ref)

        acc_ref[...] += _dot_tn(a_ref[...].astype(BF16), b_ref[...].astype(BF16))

        @pl.when(k == nk - 1)
        def _():
            for o_ref in out_refs:
                if len(o_ref.shape) == 3:
                    n = o_ref.shape[2]
                    for s in range(o_ref.shape[0]):
                        o_ref[s] = acc_ref[:, n * s:n * (s + 1)].astype(o_ref.dtype)
                else:
                    o_ref[...] = acc_ref[...].astype(o_ref.dtype)

    return pl.pallas_call(
        body, name=name, grid=grid,
        in_specs=[pl.BlockSpec(a_in[1], a_in[2]), pl.BlockSpec(b_in[1], b_in[2])],
        out_specs=[pl.BlockSpec(blk, im) for _, _, blk, im in outs],
        out_shape=[jax.ShapeDtypeStruct(shp, dt) for shp, dt, _, _ in outs],
        scratch_shapes=[pltpu.VMEM((tm, tn), F32)],
        compiler_params=_params("arbitrary", "arbitrary", "arbitrary"))(a_in[0], b_in[0])


def _grad_pair(shape, block, imap):
    return [(shape, F32, block, imap), (shape, BF16, block, imap)]


def _mm_tn_rows(name, a, b, tk=2048):
    T, a_w = a.shape
    b_w = b.shape[1]
    tm = _pick(a_w, 1408, 128)
    tk = _pick(T, tk, 128)
    g32, g16 = _mm_tn(name, (a_w // tm, 1, T // tk),
                      (a, (tk, tm), lambda i, j, k: (k, i)), (b, (tk, b_w), lambda i, j, k: (k, 0)),
                      _grad_pair((a_w, b_w), (tm, b_w), lambda i, j, k: (i, 0)))
    shp = (N_CHIPS, a_w // N_CHIPS, b_w)
    return g32.reshape(shp), g16.reshape(shp)


def _mm_tn_cols(name, a, b, tk=4096):
    T, a_w = a.shape
    b_w = b.shape[1]
    tk = _pick(T, tk, 128)
    shp = (N_CHIPS, a_w, b_w // N_CHIPS)
    return _mm_tn(name, (1, 1, T // tk),
                  (a, (tk, a_w), lambda i, j, k: (k, 0)), (b, (tk, b_w), lambda i, j, k: (k, 0)),
                  _grad_pair(shp, shp, lambda i, j, k: (0, 0, 0)))


def _colsum(name, x):
    def body(x_ref, o_ref):
        o_ref[...] = jnp.sum(x_ref[...], axis=0, keepdims=True)
    return pl.pallas_call(body, name=name, out_shape=jax.ShapeDtypeStruct((1, x.shape[1]), F32))(x)


def _rms_hat(h):
    return h * lax.rsqrt(jnp.mean(h * h, axis=-1, keepdims=True) + EPS)


def _rms_bwd_vals(dn, h, g):
    r = lax.rsqrt(jnp.mean(h * h, axis=-1, keepdims=True) + EPS)
    nh = h * r
    gd = dn * g
    dh = r * (gd - nh * jnp.mean(gd * nh, axis=-1, keepdims=True))
    return dh, _group8(dn * nh)


def _rms_fwd(name, h, g, tm=512, comm=None):
    T = h.shape[0]

    def fn(accs, tv, cv):
        return [_rms_hat(tv[0]) * cv[0]]

    return _tile_call(name, fn, T, D_MODEL, _pick(T, tm), D_MODEL, tiles=[(h, 0)], consts=[g], outs=[BF16],
                      comm=comm)


def _ffn_fwd(tag, h, g, w_in, w_out, g_next, n=None, comm_norm=None, w_in_of=None, comm_in=None, comm_out=None,
             w_out_of=None):
    T = h.shape[0]
    if n is None:
        n, *got_norm = _rms_fwd(tag + "_norm", h, g, comm=comm_norm)
        if w_in_of is not None:
            w_in = w_in_of(got_norm)
    tm = _pick(T, 1024)
    wblk = (None, D_MODEL, FF_SHARD)

    def act(accs, vals):
        gate, up = accs
        return [gate, up, gate * _sigmoid(gate) * up]

    tile = lambda: ((T, D_FF), BF16, (tm, FF_SHARD), lambda i, j: (i, j))
    gate, up, a, *got_in = _call(
        tag + "_in", act, (T // tm, 2),
        [(n, (tm, D_MODEL), lambda i, j: (i, 0)),
         (w_in, wblk, lambda i, j: (j, 0, 0)), (w_in, wblk, lambda i, j: (j + 2, 0, 0))],
        [tile(), tile(), tile()], pairs=[(0, 1, "nn"), (0, 2, "nn")], comm=comm_in, j_outer=True)

    def res(accs, tv, cv):
        h_new = tv[0] + 0.5 * accs[0]
        return [h_new, _rms_hat(h_new) * cv[0]]

    if w_out_of is not None:
        w_out = w_out_of(got_in)
    h_new, n_next, *got_out = _tile_call(
        tag + "_out", res, T, D_MODEL, _pick(T, 512), D_MODEL, pairs=[(a, 0, w_out, "nn")], tiles=[(h, 0)],
        consts=[g_next], outs=[F32, BF16], comm=comm_out)
    return h_new, n_next, (n, gate, up, a), got_out


def _ffn_bwd(tag, dh_out, df, h, g, w_in, w_out, saved, comm=None, comm_last=None):
    T = h.shape[0]
    n, gate, up, a = saved
    tm = _pick(T, 512)

    def dact(accs, vals):
        da = accs[0]
        gt, u = vals[0].astype(F32), vals[1].astype(F32)
        sg = _sigmoid(gt)
        silu = gt * sg
        return [jnp.stack([(da * u * (sg + silu * (1.0 - sg))).astype(BF16), (da * silu).astype(BF16)])]

    dz, *got = _call(
        tag + "_dact", dact, (T // tm, 2),
        [(df, (tm, D_MODEL), lambda i, j: (i, 0)), (w_out, (FF_SHARD, D_MODEL), lambda i, j: (j, 0)),
         (gate, (tm, FF_SHARD), lambda i, j: (i, j)), (up, (tm, FF_SHARD), lambda i, j: (i, j))],
        [((2, T, D_FF), BF16, (2, tm, FF_SHARD), lambda i, j: (0, i, j))], pairs=[(0, 1, "nt")], comm=comm,
        j_outer=True)
    dw_out = _mm_tn_rows(tag + "_dwout", a, df)
    tk = _pick(T, 2048, 128)
    dw_in = _mm_tn(tag + "_dwin", (1, N_CHIPS, T // tk),
                   (n, (tk, D_MODEL), lambda i, j, k: (k, 0)),
                   (dz, (None, tk, FF_SHARD), lambda i, j, k: (j // 2, k, j % 2)),
                   _grad_pair((N_CHIPS, D_MODEL, FF_SHARD), (None, D_MODEL, FF_SHARD), lambda i, j, k: (j, 0, 0)))

    def dnorm(accs, vals):
        dn = accs[0] + accs[1] + accs[2] + accs[3]
        dh, dg = _rms_bwd_vals(dn, vals[0], vals[2])
        dh = vals[1] + dh
        return [dh, dh, dg]

    tm2 = _pick(T, 512)
    ins = [(dz, (None, tm2, FF_SHARD), lambda i, j, s=s: (s // 2, i, s % 2)) for s in range(N_CHIPS)]
    ins += [(w_in, (None, D_MODEL, FF_SHARD), lambda i, j, s=s: (s, 0, 0)) for s in range(N_CHIPS)]
    ins += [(h, (tm2, D_MODEL), lambda i, j: (i, 0)), (dh_out, (tm2, D_MODEL), lambda i, j: (i, 0)),
            (g, g.shape, lambda i, j: (0, 0))]
    dh, dh16, dg, *got_last = _call(
        tag + "_dnorm", dnorm, (T // tm2, 1), ins,
        [((T, D_MODEL), F32, (tm2, D_MODEL), lambda i, j: (i, 0)),
         ((T, D_MODEL), BF16, (tm2, D_MODEL), lambda i, j: (i, 0)),
         ((T // tm2 * 8, D_MODEL), F32, (8, D_MODEL), lambda i, j: (i, 0))],
        pairs=[(s, N_CHIPS + s, "nt") for s in range(N_CHIPS)],
        comm=None if comm_last is None else comm_last(dw_in, dw_out))
    return dh, dh16, dg, dw_in, dw_out, got, got_last


def _t5_onehot():
    qi = np.arange(ATT_BLOCK)[:, None] + ATT_BLOCK
    kj = np.arange(2 * ATT_BLOCK)[None, :]
    nn = np.maximum(qi - kj, 0)
    max_exact = N_BUCKETS // 2
    large = max_exact + (np.log(np.maximum(nn, 1) / max_exact) / np.log(MAX_DISTANCE / max_exact)
                         * (N_BUCKETS - max_exact)).astype(np.int32)
    large = np.minimum(large, N_BUCKETS - 1)
    bucket = np.where(nn < max_exact, nn, large).astype(np.int32).reshape(-1)
    return (bucket[None, :] == np.arange(N_BUCKETS)[:, None]).astype(np.float32)


def _small_mm(name, a, b, sel):
    def body(a_ref, b_ref, o_ref):
        if sel == "right":
            o_ref[...] = _sel_right(a_ref[...], b_ref[...])
        else:
            o_ref[...] = _sel_left(a_ref[...], b_ref[...])
    return pl.pallas_call(body, name=name, out_shape=jax.ShapeDtypeStruct((a.shape[0], b.shape[1]), F32),
                          compiler_params=pltpu.CompilerParams(vmem_limit_bytes=V7X_VMEM_LIMIT))(a, b)


def _dup_heads(t):
    a, b = t[:, :HEAD_DIM], t[:, HEAD_DIM:]
    return jnp.concatenate([a, a, b, b], axis=1)


def _kv_layouts(proj):
    T = proj.shape[0]

    def fn(accs, tv, cv):
        return [tv[0], tv[1]]

    k, v = _tile_call("kv_cast", fn, T, 128, _pick(T, 1024), 128, tiles=[(proj, COL_AK), (proj, COL_AV)],
                      outs=[BF16, BF16])
    return _dup_heads(k), _dup_heads(v)


def _swa_masks():
    row = lax.broadcasted_iota(jnp.int32, (ATT_BLOCK, 2 * ATT_BLOCK), 0)
    col = lax.broadcasted_iota(jnp.int32, (ATT_BLOCK, 2 * ATT_BLOCK), 1)
    dist = ATT_BLOCK + row - col
    return (dist >= 0) & (dist < ATT_BLOCK), col


GROUP = 4


def _stack_group(blk, lo_q):
    zero = jnp.zeros_like(blk[:, :128])
    rows = []
    for pair in range(GROUP // 2):
        pb = blk[:, 128 * pair:128 * (pair + 1)]
        rows += [jnp.where(lo_q, pb, zero), jnp.where(lo_q, zero, pb)]
    return jnp.concatenate(rows, axis=0)


def _unstack_group(st, lo_q):
    pairs = [jnp.where(lo_q, st[256 * pair:256 * pair + 128], st[256 * pair + 128:256 * (pair + 1)])
             for pair in range(GROUP // 2)]
    return jnp.concatenate(pairs, axis=1)


def _swa_probs(s, bias_h, sink, valid):
    s = jnp.where(valid, s * (HEAD_DIM ** -0.5) + bias_h, -jnp.inf)
    m = jnp.maximum(jnp.max(s, axis=-1, keepdims=True), sink)
    e = jnp.exp(s - m)
    es = jnp.exp(sink - m)
    den = jnp.sum(e, axis=-1, keepdims=True) + es
    return e / den, es / den


def _swa_fwd(proj, kk2, vv2, bias, sinks, B, S):
    T = B * S
    nb = S // ATT_BLOCK

    def body(q_ref, k_ref, v_ref, bias_ref, sink_ref, o_ref, kpad, vpad):
        zeros = jnp.zeros((ATT_BLOCK, 256), BF16)
        kpad[pl.ds(0, ATT_BLOCK), :] = zeros
        vpad[pl.ds(0, ATT_BLOCK), :] = zeros
        kpad[pl.ds(ATT_BLOCK, S), :] = k_ref[...]
        vpad[pl.ds(ATT_BLOCK, S), :] = v_ref[...]
        valid0, col = _swa_masks()
        lo_q = lax.broadcasted_iota(jnp.int32, (1, 128), 1) < HEAD_DIM

        def blk(n, carry):
            r0 = pl.multiple_of(n * ATT_BLOCK, ATT_BLOCK)
            rows = pl.ds(r0, ATT_BLOCK)
            valid = valid0 & ((n > 0) | (col >= ATT_BLOCK))
            for g in range(N_Q_HEADS // GROUP):
                lanes = pl.ds(128 * g, 128)
                kg = kpad[pl.ds(r0, 2 * ATT_BLOCK), lanes]
                vg = vpad[pl.ds(r0, 2 * ATT_BLOCK), lanes]
                qm = _stack_group(q_ref[rows, pl.ds(256 * g, 256)].astype(BF16), lo_q)
                s = _dot_nt(qm, kg)
                ps = []
                for i in range(GROUP):
                    h = GROUP * g + i
                    p, _ = _swa_probs(s[ATT_BLOCK * i:ATT_BLOCK * (i + 1)], bias_ref[h], sink_ref[h], valid)
                    ps.append(p.astype(BF16))
                o = _dot(jnp.concatenate(ps, axis=0), vg)
                o_ref[rows, pl.ds(256 * g, 256)] = _unstack_group(o, lo_q).astype(o_ref.dtype)
            return carry

        if nb % 2 == 0:
            lax.fori_loop(0, nb // 2, lambda i, c: blk(2 * i + 1, blk(2 * i, c)), 0)
        else:
            lax.fori_loop(0, nb, blk, 0)

    return pl.pallas_call(
        body, name="swa_fwd", grid=(B,),
        in_specs=[pl.BlockSpec((S, 512), lambda b: (b, 0)),
                  pl.BlockSpec((S, 256), lambda b: (b, 0)),
                  pl.BlockSpec((S, 256), lambda b: (b, 0)),
                  pl.BlockSpec((N_Q_HEADS, ATT_BLOCK, 2 * ATT_BLOCK), lambda b: (0, 0, 0)),
                  pl.BlockSpec(memory_space=pltpu.SMEM)],
        out_specs=pl.BlockSpec((S, 512), lambda b: (b, 0)),
        out_shape=jax.ShapeDtypeStruct((T, 512), BF16),
        scratch_shapes=[pltpu.VMEM((S + ATT_BLOCK, 256), BF16), pltpu.VMEM((S + ATT_BLOCK, 256), BF16)],
        compiler_params=_params("arbitrary"))(proj, kk2, vv2, bias, sinks)


def _swa_bwd(proj, kk2, vv2, bias, sinks, datt, B, S):
    T = B * S
    nb = S // ATT_BLOCK

    def body(q_ref, k_ref, v_ref, bias_ref, sink_ref, do_ref, dq_ref, dk_ref, dv_ref, dbias_ref, dsink_ref,
             kpad, vpad, dkpad, dvpad):
        b = pl.program_id(0)

        @pl.when(b == 0)
        def _():
            dbias_ref[...] = jnp.zeros_like(dbias_ref)
            dsink_ref[...] = jnp.zeros_like(dsink_ref)

        zeros = jnp.zeros((ATT_BLOCK, 256), BF16)
        kpad[pl.ds(0, ATT_BLOCK), :] = zeros
        vpad[pl.ds(0, ATT_BLOCK), :] = zeros
        kpad[pl.ds(ATT_BLOCK, S), :] = k_ref[...]
        vpad[pl.ds(ATT_BLOCK, S), :] = v_ref[...]
        dkpad[...] = jnp.zeros_like(dkpad)
        dvpad[...] = jnp.zeros_like(dvpad)
        valid0, col = _swa_masks()
        lo_q = lax.broadcasted_iota(jnp.int32, (1, 128), 1) < HEAD_DIM
        scale = HEAD_DIM ** -0.5

        def blk(n, carry):
            r0 = pl.multiple_of(n * ATT_BLOCK, ATT_BLOCK)
            rows = pl.ds(r0, ATT_BLOCK)
            band = pl.ds(r0, 2 * ATT_BLOCK)
            valid = valid0 & ((n > 0) | (col >= ATT_BLOCK))
            for g in range(N_Q_HEADS // GROUP):
                lanes = pl.ds(128 * g, 128)
                kg = kpad[band, lanes]
                vg = vpad[band, lanes]
                qm = _stack_group(q_ref[rows, pl.ds(256 * g, 256)].astype(BF16), lo_q)
                dom = _stack_group(do_ref[rows, pl.ds(256 * g, 256)], lo_q)
                s = _dot_nt(qm, kg)
                dp = _dot_nt(dom, vg)
                pst, dst = [], []
                for i in range(GROUP):
                    h = GROUP * g + i
                    sl = slice(ATT_BLOCK * i, ATT_BLOCK * (i + 1))
                    p, ps = _swa_probs(s[sl], bias_ref[h], sink_ref[h], valid)
                    delta = jnp.sum(p * dp[sl], axis=-1, keepdims=True)
                    ds = p * (dp[sl] - delta)
                    dbias_ref[h] += ds
                    dsink_ref[pl.ds(h, 1), :] += -jnp.sum(jnp.broadcast_to(ps * delta, (ATT_BLOCK, 128)),
                                                          axis=0, keepdims=True)
                    pst.append(p.astype(BF16))
                    dst.append((ds * scale).astype(BF16))
                pst, dst = jnp.concatenate(pst, axis=0), jnp.concatenate(dst, axis=0)
                dq_ref[rows, pl.ds(256 * g, 256)] = _unstack_group(_dot(dst, kg), lo_q).astype(dq_ref.dtype)
                dkpad[band, lanes] += _dot_tn(dst, qm)
                dvpad[band, lanes] += _dot_tn(pst, dom)
            return carry

        if nb % 2 == 0:
            lax.fori_loop(0, nb // 2, lambda i, c: blk(2 * i + 1, blk(2 * i, c)), 0)
        else:
            lax.fori_loop(0, nb, blk, 0)
        lo_out = lax.broadcasted_iota(jnp.int32, (1, 128), 1) < HEAD_DIM

        def fold(pad_ref):
            halves = []
            for g in range(N_Q_HEADS // GROUP):
                t = pad_ref[pl.ds(ATT_BLOCK, S), pl.ds(128 * g, 128)]
                halves.append(t + pltpu.roll(t, HEAD_DIM, 1))
            return jnp.where(lo_out, halves[0], halves[1])

        dk_ref[...] = fold(dkpad).astype(dk_ref.dtype)
        dv_ref[...] = fold(dvpad).astype(dv_ref.dtype)

    return pl.pallas_call(
        body, name="swa_bwd", grid=(B,),
        in_specs=[pl.BlockSpec((S, 512), lambda b: (b, 0)),
                  pl.BlockSpec((S, 256), lambda b: (b, 0)),
                  pl.BlockSpec((S, 256), lambda b: (b, 0)),
                  pl.BlockSpec((N_Q_HEADS, ATT_BLOCK, 2 * ATT_BLOCK), lambda b: (0, 0, 0)),
                  pl.BlockSpec(memory_space=pltpu.SMEM),
                  pl.BlockSpec((S, 512), lambda b: (b, 0))],
        out_specs=[pl.BlockSpec((S, 512), lambda b: (b, 0)),
                   pl.BlockSpec((S, 128), lambda b: (b, 0)),
                   pl.BlockSpec((S, 128), lambda b: (b, 0)),
                   pl.BlockSpec((N_Q_HEADS, ATT_BLOCK, 2 * ATT_BLOCK), lambda b: (0, 0, 0)),
                   pl.BlockSpec((N_Q_HEADS, 128), lambda b: (0, 0))],
        out_shape=[jax.ShapeDtypeStruct((T, 512), BF16),
                   jax.ShapeDtypeStruct((T, 128), BF16),
                   jax.ShapeDtypeStruct((T, 128), BF16),
                   jax.ShapeDtypeStruct((N_Q_HEADS, ATT_BLOCK, 2 * ATT_BLOCK), F32),
                   jax.ShapeDtypeStruct((N_Q_HEADS, 128), F32)],
        scratch_shapes=[pltpu.VMEM((S + ATT_BLOCK, 256), BF16), pltpu.VMEM((S + ATT_BLOCK, 256), BF16),
                        pltpu.VMEM((S + ATT_BLOCK, 256), F32), pltpu.VMEM((S + ATT_BLOCK, 256), F32)],
        compiler_params=_params("arbitrary"))(proj, kk2, vv2, bias, sinks, datt)


def _hgrn_gates(z, lb):
    sg = _sigmoid(z)
    f = lb + (1.0 - lb) * sg
    return sg, f, jnp.log(f), 1.0 - f


def _hgrn_consts():
    r = lax.broadcasted_iota(jnp.int32, (CHUNK, CHUNK), 0)
    c = lax.broadcasted_iota(jnp.int32, (CHUNK, CHUNK), 1)
    tril = (r >= c).astype(BF16)
    triu = (r <= c).astype(BF16)
    causal = r >= c
    below = (r // SUB) > (c // SUB)
    inside = ((r // SUB) == (c // SUB)) & causal
    return tril, triu, causal, below, inside, c


def _block_rows(ref, lanes, s):
    rows = []
    for i in range(N_SUB):
        if SUB * i + s < 0:
            rows.append(jnp.zeros((SUB, REC_DIM), F32))
        else:
            rows.append(jnp.broadcast_to(ref[pl.ds(SUB * i + s, 1), lanes], (SUB, REC_DIM)))
    return jnp.concatenate(rows, axis=0)


def _hgrn_offdiag(q, k, bcum, b_ref, lanes):
    eq = jnp.exp(jnp.minimum(bcum - _block_rows(b_ref, lanes, -1), 0.0))
    qe = q * eq
    zero = jnp.zeros((SUB, REC_DIM), F32)
    q_rows, k_cols, eks = [jnp.zeros((SUB, (N_SUB - 1) * REC_DIM), F32)], [], []
    for i in range(1, N_SUB):
        q_rows.append(jnp.concatenate([zero] * (i - 1) + [qe[SUB * i:SUB * (i + 1), :]] + [zero] * (N_SUB - 1 - i),
                                      axis=1))
        p = b_ref[pl.ds(SUB * i - 1, 1), lanes]
        pad = jnp.zeros((CHUNK - SUB * i, REC_DIM), F32)
        ek = jnp.concatenate([jnp.exp(p - b_ref[pl.ds(0, SUB * i), lanes]), pad], axis=0)
        k_cols.append(k * ek)
        eks.append(ek)
    return jnp.concatenate(q_rows, axis=0), jnp.concatenate(k_cols, axis=1), eq, eks


def _hgrn_fwd(proj, lb_param, B, S, comm=None):
    T = B * S
    nc = S // CHUNK
    fwd_unroll = 4 if nc % 4 == 0 else 2
    c_arrays, c_in_specs, c_out_shapes, c_sems = _comm_parts(comm)
    nci, nco = len(c_arrays), len(c_out_shapes)

    def body(*refs):
        q_ref, z_ref, v_ref, lb_ref = refs[:4]
        o_ref, st_ref = refs[4 + nci:6 + nci]
        k_slots, b_slots = refs[6 + nci + nco:8 + nci + nco]
        comm_first, comm_last = _comm_run(comm, (B, REC_HEADS // HGRN_PAIR), refs, 4, 2)
        comm_first()
        tril, _, _, below, inside, col = _hgrn_consts()
        col_s = col & (SUB - 1)

        def chunk(ci, hts, slot):
            k_s, b_s = k_slots.at[slot], b_slots.at[slot]
            r0 = pl.multiple_of(ci * CHUNK, CHUNK)
            lb = _sigmoid(lb_ref[0:1, :] - lb_ref[1:2, :])
            _, _, g_all, k_all = _hgrn_gates(z_ref[pl.ds(r0, CHUNK), :], lb)
            b_all = _sel_left(tril, g_all)
            k_s[...] = k_all
            b_s[...] = b_all
            new = []
            for e, ht in enumerate(hts):
                lanes = pl.ds(REC_DIM * e, REC_DIM)
                cols = slice(REC_DIM * e, REC_DIM * (e + 1))
                q = q_ref[pl.ds(r0, CHUNK), lanes]
                v = v_ref[pl.ds(r0, CHUNK), lanes]
                k, bcum = k_all[:, cols], b_all[:, cols]
                st_ref[e * nc + ci] = ht
                qst, kst, _, _ = _hgrn_offdiag(q, k, bcum, b_s, lanes)
                d = jnp.zeros((CHUNK, CHUNK), F32)
                for s in range(SUB):
                    w = jnp.exp(jnp.minimum(bcum - _block_rows(b_s, lanes, s), 0.0))
                    colv = jnp.sum(q * _block_rows(k_s, lanes, s) * w, axis=-1, keepdims=True)
                    d = jnp.where(col_s == s, colv, d)
                a = jnp.where(below, _dot_nt(qst.astype(BF16), kst.astype(BF16)), 0.0) + jnp.where(inside, d, 0.0)
                vb = v.astype(BF16)
                qb = (q * jnp.exp(bcum)).astype(BF16)
                o_ref[pl.ds(r0, CHUNK), lanes] = _dot(a.astype(BF16), vb) + _dot_nt(qb, ht.astype(BF16))
                b_last = b_s[pl.ds(CHUNK - 1, 1), lanes]
                kb = (k * jnp.exp(b_last - bcum)).astype(BF16)
                new.append(ht * jnp.exp(b_last) + _dot_tn(vb, kb))
            return tuple(new)

        def trip(i, hts):
            for u in range(fwd_unroll):
                hts = chunk(fwd_unroll * i + u, hts, u)
            return hts

        lax.fori_loop(0, nc // fwd_unroll, trip, tuple(jnp.zeros((REC_DIM, REC_DIM), F32) for _ in range(HGRN_PAIR)))
        comm_last()

    hp, wd = REC_HEADS // HGRN_PAIR, HGRN_PAIR * REC_DIM
    cq, cf, ci_ = (c * REC_DIM // wd for c in (COL_RQ, COL_RF, COL_RI))
    return pl.pallas_call(
        body, name="hgrn_fwd", grid=(B, hp),
        in_specs=[pl.BlockSpec((S, wd), lambda b, h: (b, cq + h)),
                  pl.BlockSpec((S, wd), lambda b, h: (b, cf + h)),
                  pl.BlockSpec((S, wd), lambda b, h: (b, ci_ + h)),
                  pl.BlockSpec((2, wd), lambda b, h: (0, h))] + c_in_specs,
        out_specs=[pl.BlockSpec((S, wd), lambda b, h: (b, h)),
                   pl.BlockSpec((HGRN_PAIR * nc, REC_DIM, REC_DIM), lambda b, h: (b * hp + h, 0, 0))] + [ANY] * nco,
        out_shape=[jax.ShapeDtypeStruct((T, 512), F32),
                   jax.ShapeDtypeStruct((B * REC_HEADS * nc, REC_DIM, REC_DIM), F32)] + c_out_shapes,
        scratch_shapes=[pltpu.VMEM((fwd_unroll, CHUNK, wd), F32), pltpu.VMEM((fwd_unroll, CHUNK, wd), F32)] + c_sems,
        compiler_params=_params("arbitrary", "arbitrary"))(proj, proj, proj, lb_param, *c_arrays)


def _hgrn_bwd(proj, lb_param, states, do, B, S, comm=None):
    T = B * S
    nc = S // CHUNK
    bwd_unroll = 4 if nc % 4 == 0 else 2
    c_arrays, c_in_specs, c_out_shapes, c_sems = _comm_parts(comm)
    nci, nco = len(c_arrays), len(c_out_shapes)

    def body(*refs):
        q_ref, z_ref, v_ref, lb_ref, st_ref, do_ref = refs[:6]
        dq_ref, dz_ref, dv_ref, dlb_ref = refs[6 + nci:10 + nci]
        slots = refs[10 + nci + nco:14 + nci + nco]
        comm_first, comm_last = _comm_run(comm, (B, REC_HEADS // HGRN_PAIR), refs, 6, 4)
        comm_first()
        tril, triu, causal, below, inside, col = _hgrn_consts()
        col_s = col & (SUB - 1)
        last_row = lax.broadcasted_iota(jnp.int32, (CHUNK, 1), 0) == CHUNK - 1
        rc = lax.broadcasted_iota(jnp.int32, (CHUNK, SUB * REC_DIM), 0)
        lc = lax.broadcasted_iota(jnp.int32, (CHUNK, SUB * REC_DIM), 1)
        spread = ((rc & (SUB - 1)) == (lc // REC_DIM)).astype(BF16)
        rr = lax.broadcasted_iota(jnp.int32, (CHUNK, SUB * CHUNK), 0)
        cc = lax.broadcasted_iota(jnp.int32, (CHUNK, SUB * CHUNK), 1)
        gather = (((rr // SUB) == ((cc & (CHUNK - 1)) // SUB)) & ((rr & (SUB - 1)) == (cc // CHUNK))).astype(BF16)

        heads = range(HGRN_PAIR)
        cols = [slice(REC_DIM * e, REC_DIM * (e + 1)) for e in heads]
        lanes = [pl.ds(REC_DIM * e, REC_DIM) for e in heads]
        lane_cat = lambda vals: jnp.concatenate(vals, axis=1)
        row_cat = lambda vals: jnp.concatenate(vals, axis=0)

        def chunk(it, carry, slot):
            k_s, b_s, pc_hi, pc_lo = (r.at[slot] for r in slots)
            dhts, dlb = carry
            ci = nc - 1 - it
            r0 = pl.multiple_of(ci * CHUNK, CHUNK)
            rows = pl.ds(r0, CHUNK)
            lb = _sigmoid(lb_ref[0:1, :] - lb_ref[1:2, :])
            sg, f, g_all, k_all = _hgrn_gates(z_ref[rows, :], lb)
            b_all = _sel_left(tril, g_all)
            k_s[...] = k_all
            b_s[...] = b_all
            q_all = q_ref[rows, :]
            das, hd = [], []
            for e in heads:
                vb, dob = v_ref[rows, lanes[e]].astype(BF16), do_ref[rows, lanes[e]].astype(BF16)
                da = jnp.where(causal, _dot_nt(dob, vb), 0.0)
                das.append(jnp.where(inside, da, 0.0))
                hd.append((vb, dob, da))
            da_hi, da_lo = _split2(row_cat(das))
            da_in = _dot(da_hi, spread) + _dot(da_lo, spread)
            ds, dqs = [], []
            for e in heads:
                q, bcum = q_all[:, cols[e]], b_all[:, cols[e]]
                d = jnp.zeros((CHUNK, CHUNK), F32)
                dq = jnp.zeros((CHUNK, REC_DIM), F32)
                for s in range(SUB):
                    w = jnp.exp(jnp.minimum(bcum - _block_rows(b_s, lanes[e], s), 0.0))
                    ks = _block_rows(k_s, lanes[e], s)
                    qw = q * w
                    d = jnp.where(col_s == s, jnp.sum(qw * ks, axis=-1, keepdims=True), d)
                    da_s = da_in[CHUNK * e:CHUNK * (e + 1), REC_DIM * s:REC_DIM * (s + 1)]
                    dq = dq + da_s * ks * w
                    hi, lo = _split2(da_s * qw)
                    pc_hi[pl.ds(CHUNK * s, CHUNK), lanes[e]] = hi
                    pc_lo[pl.ds(CHUNK * s, CHUNK), lanes[e]] = lo
                ds.append(d)
                dqs.append(dq)
            dk_in = _dot(gather, pc_hi[...]) + _dot(gather, pc_lo[...])
            dq_out, dk_out, dv_out, db_out, new_dhts = [], [], [], [], []
            for e in heads:
                q, k, bcum = q_all[:, cols[e]], k_all[:, cols[e]], b_all[:, cols[e]]
                vb, dob, da = hd[e]
                dht, ht = dhts[e], st_ref[e * nc + ci]
                qst, kst, eq, eks = _hgrn_offdiag(q, k, bcum, b_s, lanes[e])
                qst_b, kst_b = qst.astype(BF16), kst.astype(BF16)
                a = jnp.where(below, _dot_nt(qst_b, kst_b), 0.0) + jnp.where(inside, ds[e], 0.0)
                da_off = jnp.where(below, da, 0.0).astype(BF16)
                dqst = _dot(da_off, kst_b)
                dkst = _dot_tn(da_off, qst_b)
                dk = dk_in[:, cols[e]]
                dq_rows = [jnp.zeros((SUB, REC_DIM), F32)]
                for i in range(1, N_SUB):
                    dq_rows.append(dqst[SUB * i:SUB * (i + 1), REC_DIM * (i - 1):REC_DIM * i])
                    dk = dk + dkst[:, REC_DIM * (i - 1):REC_DIM * i] * eks[i - 1]
                dq = dqs[e] + row_cat(dq_rows) * eq
                eb = jnp.exp(bcum)
                b_last = b_s[pl.ds(CHUNK - 1, 1), lanes[e]]
                el = jnp.exp(b_last)
                ekb = jnp.exp(b_last - bcum)
                qb = (q * eb).astype(BF16)
                kb = k * ekb
                dhb = dht.astype(BF16)
                dv_out.append(_dot_tn(a.astype(BF16), dob) + _dot_nt(kb.astype(BF16), dhb))
                dqb = _dot(dob, ht.astype(BF16))
                dkb = _dot(vb, dhb)
                new_dhts.append(dht * el + _dot_tn(dob, qb))
                dq = dq + eb * dqb
                dk = dk + ekb * dkb
                edge = jnp.sum(kb * dkb, axis=0, keepdims=True) + el * jnp.sum(ht * dht, axis=0, keepdims=True)
                db_out.append(q * dq - k * dk + jnp.where(last_row, edge, 0.0))
                dq_out.append(dq)
                dk_out.append(dk)
            dk_all = lane_cat(dk_out)
            db_hi, db_lo = _split2(lane_cat(db_out))
            dg = _dot(triu, db_hi) + _dot(triu, db_lo)
            df = dg / f - dk_all
            dz_ref[rows, :] = (df * (1.0 - lb) * sg * (1.0 - sg)).astype(dz_ref.dtype)
            dq_ref[rows, :] = lane_cat(dq_out).astype(dq_ref.dtype)
            dv_ref[rows, :] = lane_cat(dv_out).astype(dv_ref.dtype)
            return tuple(new_dhts), dlb + jnp.sum(df * (1.0 - sg), axis=0, keepdims=True)

        zero = (tuple(jnp.zeros((REC_DIM, REC_DIM), F32) for _ in heads), jnp.zeros((1, HGRN_PAIR * REC_DIM), F32))
        def trip(i, carry):
            for u in range(bwd_unroll):
                carry = chunk(bwd_unroll * i + u, carry, u)
            return carry

        _, dlb = lax.fori_loop(0, nc // bwd_unroll, trip, zero)
        lb = _sigmoid(lb_ref[0:1, :] - lb_ref[1:2, :])
        dlb_ref[...] = jnp.broadcast_to(dlb * lb * (1.0 - lb), (8, HGRN_PAIR * REC_DIM))
        comm_last()

    hp, wd = REC_HEADS // HGRN_PAIR, HGRN_PAIR * REC_DIM
    cq, cf, ci_ = (c * REC_DIM // wd for c in (COL_RQ, COL_RF, COL_RI))
    return pl.pallas_call(
        body, name="hgrn_bwd", grid=(B, hp),
        in_specs=[pl.BlockSpec((S, wd), lambda b, h: (b, cq + h)),
                  pl.BlockSpec((S, wd), lambda b, h: (b, cf + h)),
                  pl.BlockSpec((S, wd), lambda b, h: (b, ci_ + h)),
                  pl.BlockSpec((2, wd), lambda b, h: (0, h)),
                  pl.BlockSpec((HGRN_PAIR * nc, REC_DIM, REC_DIM), lambda b, h: (b * hp + h, 0, 0)),
                  pl.BlockSpec((S, wd), lambda b, h: (b, h))] + c_in_specs,
        out_specs=[pl.BlockSpec((S, wd), lambda b, h: (b, h))] * 3
        + [pl.BlockSpec((8, wd), lambda b, h: (b, h))] + [ANY] * nco,
        out_shape=[jax.ShapeDtypeStruct((T, 512), BF16)] * 3 + [jax.ShapeDtypeStruct((B * 8, 512), F32)]
        + c_out_shapes,
        scratch_shapes=[pltpu.VMEM((bwd_unroll, CHUNK, wd), F32)] * 2
        + [pltpu.VMEM((bwd_unroll, SUB * CHUNK, wd), BF16)] * 2 + c_sems,
        compiler_params=_params("arbitrary", "arbitrary"))(proj, proj, proj, lb_param, states, do, *c_arrays)


def _rec_gate_fwd(rec, proj, rec_norm):
    T = rec.shape[0]

    def fn(accs, tv, cv):
        return [_rms_hat(tv[0]) * cv[0] * _sigmoid(tv[1])]

    return _tile_call("rec_gate", fn, T, 512, _pick(T, 1024), REC_DIM, tiles=[(rec, 0), (proj, COL_RG)],
                      consts=[rec_norm], outs=[BF16])[0]


def _rec_gate_bwd(dyb, w_rec_proj, rec, proj, rec_norm):
    T = rec.shape[0]

    def fn(accs, tv, cv):
        d, r, rg = accs[0], tv[0], tv[1]
        sg = _sigmoid(rg)
        rn = _rms_hat(r) * cv[0]
        dh, dg = _rms_bwd_vals(d * sg, r, cv[0])
        return [dh, d * rn * sg * (1.0 - sg), dg]

    return _tile_call("rec_gate_bwd", fn, T, 512, _pick(T, 1024), REC_DIM, pairs=[(dyb, 0, w_rec_proj, "nt")],
                      tiles=[(rec, 0), (proj, COL_RG)], consts=[rec_norm], outs=[F32, BF16], parts=1)


def _mix_out_fwd(att, recn, proj, w_att_proj, w_rec_proj, w_out, h1, g_next):
    T = att.shape[0]
    tn = 256

    def merge(accs, tv, cv):
        ya, yb = accs
        return [ya, yb, _sigmoid(tv[0]) * ya + _sigmoid(tv[1]) * yb]

    ya, yb, merged = _tile_call(
        "merge", merge, T, D_MODEL, _pick(T, 1024), tn,
        pairs=[(att, 0, w_att_proj, "nn"), (recn, 0, w_rec_proj, "nn")],
        tiles=[(proj, COL_GA * 128 // tn), (proj, COL_GB * 128 // tn)], outs=[BF16] * 3)

    def res(accs, tv, cv):
        h2 = tv[0] + accs[0]
        return [h2, _rms_hat(h2) * cv[0]]

    h2, n2 = _tile_call("mix_out", res, T, D_MODEL, _pick(T, 1024), D_MODEL, pairs=[(merged, 0, w_out, "nn")],
                        tiles=[(h1, 0)], consts=[g_next], outs=[F32, BF16])
    return h2, n2, (ya, yb, merged)


GATHER_FIRST = ("w_ffn1_in",)
GATHER_MIX = ("w_ffn1_out", "w_in")
GATHER_PROJ = ("w_att_proj", "w_rec_proj", "w_out")
GATHER_TAIL = ("w_ffn2_out", "w_ple_gate", "w_ple_proj")
GATHER_LAST = ("w_ffn2_in",)
SCATTER_LATE = ("w_ple_gate", "w_ple_proj", "w_ffn2_in", "w_ffn2_out")
SCATTER_MIX = ("w_out", "w_att_proj", "w_rec_proj", "w_in")
SCATTER_LAST = ("w_ffn1_in", "w_ffn1_out")


def _local_step(x, p, tgt, w, mine16, cc, me_chip, B, S):
    T = B * S
    w = dict(w)
    g_ffn1, g_mix, g_ffn2, g_ple = w["norm_ffn1"], w["norm_mix"], w["norm_ffn2"], w["norm_ple"]
    g_fin = w["norm_final"].reshape(1, D_MODEL)
    grads, part, from_chips = {}, {}, {}

    def gather(names):
        return _gather_comm([mine16[n] for n in names])

    def place(names, got):
        for n, g in zip(names, got):
            full = lax.dynamic_update_index_in_dim(g, mine16[n], me_chip, 0)
            w[n] = full if n in ("w_ffn1_in", "w_ffn2_in") else _natural(n, full)

    def swap(names):
        return _swap_comm([grads[n][1] for n in names])

    def after_swap(names, from_sib):
        for n, fs in zip(names, from_sib):
            part[n] = _add_sibling("rs_add_sib_" + n, grads[n][0], fs, cc)
        return _scatter_comm([part[n][1] for n in names])

    def scattered(names, got):
        for n, g in zip(names, got):
            from_chips[n] = g

    def ffn1_in_weight(got):
        place(GATHER_FIRST, got)
        return w["w_ffn1_in"]

    def ffn1_out_weight(got):
        place(GATHER_MIX, got)
        return w["w_ffn1_out"]

    h1, u, sv1, got_proj = _ffn_fwd("ffn1", x, g_ffn1, None, None, g_mix, comm_norm=gather(GATHER_FIRST),
                                    w_in_of=ffn1_in_weight, comm_in=gather(GATHER_MIX),
                                    comm_out=gather(GATHER_PROJ), w_out_of=ffn1_out_weight)
    place(GATHER_PROJ, got_proj)

    def ident(accs, tv, cv):
        return [accs[0]]

    proj, *got_tail = _tile_call("in_proj", ident, T, IN_W, _pick(T, 512), IN_W // 2,
                                 pairs=[(u, 0, w["w_in"], "nn")], outs=[F32], j_outer=True, comm=gather(GATHER_TAIL))
    place(GATHER_TAIL, got_tail)
    onehot = jnp.asarray(_t5_onehot())
    bias = _small_mm("t5_bias", w["rel_bias"].T, onehot.astype(BF16), "right")
    bias = bias.reshape(N_Q_HEADS, ATT_BLOCK, 2 * ATT_BLOCK)
    sinks = w["attn_sinks"].reshape(N_Q_HEADS)
    kk2, vv2 = _kv_layouts(proj)
    att = _swa_fwd(proj, kk2, vv2, bias, sinks, B, S)
    rec, states, *got_last = _hgrn_fwd(proj, w["lb_param"], B, S, comm=gather(GATHER_LAST))
    place(GATHER_LAST, got_last)
    recn = _rec_gate_fwd(rec, proj, w["rec_norm"])
    h2, n2, (ya, yb, merged) = _mix_out_fwd(att, recn, proj, w["w_att_proj"], w["w_rec_proj"], w["w_out"], h1,
                                            g_ffn2)
    h3, n3, sv2, _ = _ffn_fwd("ffn2", h2, g_ffn2, w["w_ffn2_in"], w["w_ffn2_out"], g_ple, n=n2)

    def ple(accs, tv, cv):
        gate = _sigmoid(accs[0])
        return [gate, accs[1], tv[0] + gate * accs[1]]

    gate_p, pp, h4 = _tile_call(
        "ple", ple, T, D_MODEL, _pick(T, 1024), D_MODEL,
        pairs=[(n3, 0, w["w_ple_gate"], "nn"), (p, 0, w["w_ple_proj"], "nn")], tiles=[(h3, 0)],
        outs=[BF16, BF16, F32])

    def head(accs, tv, cv):
        h, t, gt, ppv = tv[0], tv[1], tv[2].astype(F32), tv[3].astype(F32)
        err = _rms_hat(h) * cv[0] - t
        dh, dg = _rms_bwd_vals(err * (1.0 / D_MODEL), h, cv[0])
        return [dh, dh * ppv * gt * (1.0 - gt), dh * gt, _group8(err * err), dg]

    dh4, dzg, dpp, loss_p, dg_fin = _tile_call(
        "loss_head", head, T, D_MODEL, _pick(T, 512), D_MODEL,
        tiles=[(h4, 0), (tgt, 0), (gate_p, 0), (pp, 0)], consts=[g_fin], outs=[F32, BF16, BF16], parts=2)
    grads["norm_final"] = dg_fin

    grads["w_ple_gate"] = _mm_tn_rows("ple_dwg", n3, dzg)
    grads["w_ple_proj"] = _mm_tn_cols("ple_dwp", p, dpp)

    def dnorm(accs, tv, cv):
        dh, dg = _rms_bwd_vals(accs[0], tv[0], cv[0])
        dh = tv[1] + dh
        return [dh, 0.5 * dh, dg]

    dh3, df3, grads["norm_ple"] = _tile_call(
        "ple_dnorm", dnorm, T, D_MODEL, _pick(T, 512), D_MODEL, pairs=[(dzg, 0, w["w_ple_gate"], "nt")],
        tiles=[(h3, 0), (dh4, 0)], consts=[g_ple], outs=[F32, BF16], parts=1)

    def swap_late(dw_in, dw_out):
        grads["w_ffn2_in"], grads["w_ffn2_out"] = dw_in, dw_out
        return swap(SCATTER_LATE)

    dh2, dh2b, grads["norm_ffn2"], _, _, _, from_sib = _ffn_bwd(
        "ffn2b", dh3, df3, h2, g_ffn2, w["w_ffn2_in"], w["w_ffn2_out"], sv2, comm_last=swap_late)
    scatter_late = after_swap(SCATTER_LATE, from_sib)

    grads["w_out"] = _mm_tn_rows("mix_dwout", merged, dh2b)
    tn = 256

    def dmerge(accs, tv, cv):
        dm = accs[0]
        sa, sb = _sigmoid(tv[0]), _sigmoid(tv[1])
        yav, ybv = tv[2].astype(F32), tv[3].astype(F32)
        return [dm * sa, dm * sb, dm * yav * sa * (1.0 - sa), dm * ybv * sb * (1.0 - sb)]

    dya, dyb, dga, dgb = _tile_call(
        "mix_dmerge", dmerge, T, D_MODEL, _pick(T, 1024), tn, pairs=[(dh2b, 0, w["w_out"], "nt")],
        tiles=[(proj, COL_GA * 128 // tn), (proj, COL_GB * 128 // tn), (ya, 0), (yb, 0)], outs=[BF16] * 4)
    grads["w_att_proj"] = _mm_tn_cols("mix_dwatt", att, dya)
    grads["w_rec_proj"] = _mm_tn_cols("mix_dwrec", recn, dyb)

    datt = _tile_call("mix_datt", ident, T, 512, _pick(T, 1024), 512, pairs=[(dya, 0, w["w_att_proj"], "nt")],
                      outs=[BF16])[0]
    drec, drg, grads["rec_norm"] = _rec_gate_bwd(dyb, w["w_rec_proj"], rec, proj, w["rec_norm"])

    drq, drf, dri, dlb, *got = _hgrn_bwd(proj, w["lb_param"], states, drec, B, S, comm=scatter_late)
    scattered(SCATTER_LATE, got)
    grads["lb_param"] = dlb
    daq, dak, dav, dbias, dsink = _swa_bwd(proj, kk2, vv2, bias, sinks, datt, B, S)
    grads["attn_sinks"] = dsink
    grads["rel_bias"] = _small_mm("t5_dbias", dbias.reshape(N_Q_HEADS, -1), onehot.T.astype(BF16), "right")
    dproj = jnp.concatenate([daq, dak, dav, drq, drf, dri, drg, dga, dgb], axis=1)
    tk = _pick(T, 2048, 128)
    w_in_shard = IN_W // N_CHIPS
    half_d = D_MODEL // 2
    gw32, gw16 = _mm_tn("mix_dwin", (2, 2, T // tk),
                        (u, (tk, half_d), lambda i, j, k: (k, i)), (dproj, (tk, IN_W // 2), lambda i, j, k: (k, j)),
                        _grad_pair((D_MODEL, IN_W), (half_d, IN_W // 2), lambda i, j, k: (i, j)))
    to_sh = lambda t: t.reshape(D_MODEL, N_CHIPS, w_in_shard).transpose(1, 0, 2)
    grads["w_in"] = (to_sh(gw32), to_sh(gw16))

    def dnorm_mix(accs, tv, cv):
        dh, dg = _rms_bwd_vals(accs[0], tv[0], cv[0])
        dh = tv[1] + dh
        return [dh, 0.5 * dh, dg]

    dh1, df1, grads["norm_mix"], *from_sib = _tile_call(
        "mix_dnorm", dnorm_mix, T, D_MODEL, _pick(T, 512), D_MODEL, pairs=[(dproj, 0, w["w_in"], "nt")],
        tiles=[(h1, 0), (dh2, 0)], consts=[g_mix], outs=[F32, BF16], parts=1, comm=swap(SCATTER_MIX))
    scatter_mix = after_swap(SCATTER_MIX, from_sib)

    def scatter_last(dw_in, dw_out):
        grads["w_ffn1_in"], grads["w_ffn1_out"] = dw_in, dw_out
        return after_swap(SCATTER_LAST, _run_comm("rs_sibling_last", swap(SCATTER_LAST)))

    dx, _, grads["norm_ffn1"], _, _, got, got_last = _ffn_bwd(
        "ffn1b", dh1, df1, x, g_ffn1, w["w_ffn1_in"], w["w_ffn1_out"], sv1, comm=scatter_mix, comm_last=scatter_last)
    scattered(SCATTER_MIX, got)
    scattered(SCATTER_LAST, got_last)
    return loss_p, dx, grads, part, from_chips


def _place():
    x, y, c = lax.axis_index("x"), lax.axis_index("y"), lax.axis_index("c")
    return x, y, c


def _other_chips(x, y):
    return [(1 - x, y, 2 * (1 - x) + y), (x, 1 - y, 2 * x + 1 - y), (1 - x, 1 - y, 2 * (1 - x) + 1 - y)]


def _half_rows(ref_3d, chip, h, rows):
    return ref_3d.at[chip, pl.ds(h * rows, rows), :]


def _gather_comm(ws):
    nw = len(ws)

    def parts(w_refs, out_refs, send_sems, recv_sems):
        x, y, c = _place()
        me = 2 * x + y
        chips = _other_chips(x, y)

        def copy(i, k, chip, h, to, src=None):
            half = ws[i].shape[0] // 2
            dst = _half_rows(out_refs[i], chip, h, half)
            return pltpu.make_async_remote_copy(
                src_ref=dst if src is None else src, dst_ref=dst,
                send_sem=send_sems.at[6 * i + k], recv_sem=recv_sems.at[6 * i + k], device_id=to, device_id_type=MESH)

        def first():
            out = []
            for i in range(nw):
                half = ws[i].shape[0] // 2
                out += [copy(i, j, me, c, (cx, cy, c), src=w_refs[i].at[pl.ds(c * half, half), :])
                        for j, (cx, cy, _) in enumerate(chips)]
            return out

        return copy, first, chips, c, (x, y, 1 - c)

    def start(*refs):
        _, first, _, _, _ = parts(*refs)
        for cp in first():
            cp.start()

    def finish(*refs):
        copy, first, chips, c, sibling = parts(*refs)
        passed = []
        for i in range(nw):
            for j, (cx, cy, ci) in enumerate(chips):
                copy(i, j, ci, c, (cx, cy, c)).wait_recv()
                fw = copy(i, 3 + j, ci, c, sibling)
                fw.start()
                passed.append(fw)
        for i in range(nw):
            for j, (_, _, ci) in enumerate(chips):
                copy(i, 3 + j, ci, 1 - c, sibling).wait_recv()
        for cp in first() + passed:
            cp.wait_send()

    return _Comm(list(ws), [jax.ShapeDtypeStruct((N_CHIPS,) + w.shape, w.dtype) for w in ws], 6 * nw, start, finish)


def _scatter_comm(ps):
    nw = len(ps)

    def copies(p_refs, out_refs, send_sems, recv_sems):
        x, y, c = _place()
        cps = []
        for i in range(nw):
            for j, (cx, cy, ci) in enumerate(_other_chips(x, y)):
                cps.append(pltpu.make_async_remote_copy(
                    src_ref=p_refs[i].at[ci], dst_ref=out_refs[i].at[j], send_sem=send_sems.at[3 * i + j],
                    recv_sem=recv_sems.at[3 * i + j], device_id=(cx, cy, c), device_id_type=MESH))
        return cps

    def start(*refs):
        for cp in copies(*refs):
            cp.start()

    def finish(*refs):
        for cp in copies(*refs):
            cp.wait()

    return _Comm(list(ps), [jax.ShapeDtypeStruct((3,) + p.shape[1:], p.dtype) for p in ps], 3 * nw, start, finish)


def _swap_comm(gs):
    nw = len(gs)

    def copies(g_refs, out_refs, send_sems, recv_sems):
        x, y, c = _place()
        cps = []
        for i in range(nw):
            half = gs[i].shape[1] // 2
            cps.append(pltpu.make_async_remote_copy(
                src_ref=g_refs[i].at[:, pl.ds((1 - c) * half, half), :], dst_ref=out_refs[i],
                send_sem=send_sems.at[i], recv_sem=recv_sems.at[i], device_id=(x, y, 1 - c), device_id_type=MESH))
        return cps

    def start(*refs):
        for cp in copies(*refs):
            cp.start()

    def finish(*refs):
        for cp in copies(*refs):
            cp.wait()

    return _Comm(list(gs), [jax.ShapeDtypeStruct((N_CHIPS, g.shape[1] // 2, g.shape[2]), g.dtype) for g in gs],
                 nw, start, finish)


def _run_comm(name, comm):
    nci, nco = len(comm.ins), len(comm.out_shapes)

    def body(*refs):
        cin, cout, send_sems, recv_sems = refs[:nci], refs[nci:nci + nco], refs[-2], refs[-1]
        comm.start(cin, cout, send_sems, recv_sems)
        comm.finish(cin, cout, send_sems, recv_sems)

    return pl.pallas_call(
        body, name=name, in_specs=[ANY] * nci, out_specs=[ANY] * nco, out_shape=list(comm.out_shapes),
        scratch_shapes=[pltpu.SemaphoreType.DMA((comm.n_sems,)), pltpu.SemaphoreType.DMA((comm.n_sems,))],
    )(*comm.ins)


def _join_halves(name, ss):
    nw = len(ss)

    def body(*refs):
        s_refs, out_refs, send_sems, recv_sems = refs[:nw], refs[nw:2 * nw], refs[2 * nw], refs[2 * nw + 1]
        x, y, c = _place()
        cps = [pltpu.make_async_remote_copy(
            src_ref=s_refs[i], dst_ref=out_refs[i], send_sem=send_sems.at[i], recv_sem=recv_sems.at[i],
            device_id=(x, y, 1 - c), device_id_type=MESH) for i in range(nw)]
        for cp in cps:
            cp.start()
        for cp in cps:
            cp.wait()

    return pl.pallas_call(
        body, name=name, in_specs=[ANY] * nw, out_specs=[ANY] * nw,
        out_shape=[jax.ShapeDtypeStruct(s.shape, s.dtype) for s in ss],
        scratch_shapes=[pltpu.SemaphoreType.DMA((nw,)), pltpu.SemaphoreType.DMA((nw,))],
    )(*ss)


def _allreduce_small(sp):
    def body(s_ref, out_ref, slots, send_sems, recv_sems):
        x, y, c = _place()
        me = 4 * x + 2 * y + c
        slots[me] = s_ref[...]
        cps = []
        for r in range(1, N_DEV):
            px, py, pc = x ^ (r >> 2), y ^ ((r >> 1) & 1), c ^ (r & 1)
            cps.append(pltpu.make_async_remote_copy(
                src_ref=s_ref, dst_ref=slots.at[me], send_sem=send_sems.at[r - 1], recv_sem=recv_sems.at[r - 1],
                device_id=(px, py, pc), device_id_type=MESH))
        for cp in cps:
            cp.start()
        for r in range(1, N_DEV):
            px, py, pc = x ^ (r >> 2), y ^ ((r >> 1) & 1), c ^ (r & 1)
            pltpu.make_async_remote_copy(
                src_ref=s_ref, dst_ref=slots.at[4 * px + 2 * py + pc], send_sem=send_sems.at[r - 1],
                recv_sem=recv_sems.at[r - 1], device_id=(px, py, pc), device_id_type=MESH).wait_recv()
        for cp in cps:
            cp.wait_send()
        acc = slots[0]
        for d in range(1, N_DEV):
            acc = acc + slots[d]
        out_ref[...] = acc

    return pl.pallas_call(
        body, name="allreduce_small",
        in_specs=[pl.BlockSpec(memory_space=pltpu.VMEM)], out_specs=pl.BlockSpec(memory_space=pltpu.VMEM),
        out_shape=jax.ShapeDtypeStruct(sp.shape, F32),
        scratch_shapes=[pltpu.VMEM((N_DEV,) + sp.shape, F32), pltpu.SemaphoreType.DMA((N_DEV - 1,)),
                        pltpu.SemaphoreType.DMA((N_DEV - 1,))],
    )(sp)


def _scalar(v):
    return jnp.reshape(v, (1,)).astype(jnp.int32)


def _row_tile(h, dtype_mult=16):
    return _pick(h, 256, dtype_mult)


def _add_sibling(name, g32, from_sib, c):
    _, r, n = g32.shape
    h = r // 2
    th = _row_tile(h)
    nt = h // th

    def body(c_ref, g_ref, s_ref, o32_ref, o16_ref):
        s = g_ref[...] + s_ref[...].astype(F32)
        o32_ref[...] = s
        o16_ref[...] = s.astype(BF16)

    blk = (None, th, n)
    return pl.pallas_call(
        body, name=name,
        grid_spec=pltpu.PrefetchScalarGridSpec(
            num_scalar_prefetch=1, grid=(N_CHIPS, nt),
            in_specs=[pl.BlockSpec(blk, lambda k, t, c_ref: (k, c_ref[0] * nt + t, 0)),
                      pl.BlockSpec(blk, lambda k, t, c_ref: (k, t, 0))],
            out_specs=[pl.BlockSpec(blk, lambda k, t, c_ref: (k, t, 0))] * 2),
        out_shape=[jax.ShapeDtypeStruct((N_CHIPS, h, n), F32), jax.ShapeDtypeStruct((N_CHIPS, h, n), BF16)],
        compiler_params=_params("arbitrary", "arbitrary"))(_scalar(c), g32, from_sib)


def _add_chips(name, p32, from_chips, me_chip):
    _, h, n = p32.shape
    th = _row_tile(h)

    def body(m_ref, p_ref, a_ref, b_ref, c_ref, o_ref):
        o_ref[...] = p_ref[...] + a_ref[...].astype(F32) + b_ref[...].astype(F32) + c_ref[...].astype(F32)

    blk = (None, th, n)
    return pl.pallas_call(
        body, name=name,
        grid_spec=pltpu.PrefetchScalarGridSpec(
            num_scalar_prefetch=1, grid=(h // th,),
            in_specs=[pl.BlockSpec(blk, lambda t, m_ref: (m_ref[0], t, 0))]
            + [pl.BlockSpec(blk, lambda t, m_ref, j=j: (j, t, 0)) for j in range(3)],
            out_specs=pl.BlockSpec((th, n), lambda t, m_ref: (t, 0))),
        out_shape=jax.ShapeDtypeStruct((h, n), F32),
        compiler_params=_params("arbitrary"))(_scalar(me_chip), p32, from_chips, from_chips, from_chips)


def _adamw_vals(w, g, m, v):
    m = ADAM_B1 * m + (1.0 - ADAM_B1) * g
    v = ADAM_B2 * v + (1.0 - ADAM_B2) * (g * g)
    m_hat = m / (1.0 - ADAM_B1 ** ADAM_STEP)
    v_hat = v / (1.0 - ADAM_B2 ** ADAM_STEP)
    delta = -ADAM_LR * (m_hat / (jnp.sqrt(v_hat) + ADAM_EPS) + ADAM_WD * w)
    return delta, m, v


def _adamw_halves(name, w, m, v, g_mine, g_sib, c):
    r, n = w.shape
    h = r // 2
    th = _row_tile(h, 8)
    nt = h // th

    def body(c_ref, w_ref, m_ref, v_ref, a_ref, b_ref, g_ref, d_ref, nm_ref, nv_ref):
        mine = (pl.program_id(0) // nt) == c_ref[0]
        g = jnp.where(mine, a_ref[...], b_ref[...])
        d, nm, nv = _adamw_vals(w_ref[...], g, m_ref[...], v_ref[...])
        g_ref[...] = g
        d_ref[...] = d
        nm_ref[...] = nm
        nv_ref[...] = nv

    full = pl.BlockSpec((th, n), lambda t, c_ref: (t, 0))
    own_half = pl.BlockSpec((th, n), lambda t, c_ref: (jnp.where(t // nt == c_ref[0], t % nt, 0), 0))
    sib_half = pl.BlockSpec((th, n), lambda t, c_ref: (jnp.where(t // nt == c_ref[0], 0, t % nt), 0))
    return pl.pallas_call(
        body, name=name,
        grid_spec=pltpu.PrefetchScalarGridSpec(
            num_scalar_prefetch=1, grid=(2 * nt,), in_specs=[full, full, full, own_half, sib_half],
            out_specs=[full] * 4),
        out_shape=[jax.ShapeDtypeStruct((r, n), F32)] * 4,
        compiler_params=_params("arbitrary"))(_scalar(c), w, m, v, g_mine, g_sib)


def _adamw(name, w, g, m, v):
    R, W = w.shape

    def fn(accs, tv, cv):
        return list(_adamw_vals(*tv))

    return _tile_call(name, fn, R, W, _pick(R, 256), W, tiles=[(w, 0), (g, 0), (m, 0), (v, 0)], outs=[F32] * 3)


SMALL_LAYOUT = (("rel_bias", 2, 256), ("lb_param", 8, 1024), ("norm_ffn1", 8, 1024), ("norm_mix", 8, 1024),
                ("attn_sinks", 1, 8), ("rec_norm", 1, 128), ("norm_ffn2", 8, 1024), ("norm_ple", 8, 1024),
                ("norm_final", 8, 1024), ("loss", 8, 1024))


def _pack_small(vals):
    rows = []
    for name, nrows, n in SMALL_LAYOUT:
        flat = vals[name].reshape(-1)
        flat = jnp.pad(flat, (0, nrows * 128 - n))
        rows.append(flat.reshape(nrows, 128))
    packed = jnp.concatenate(rows, axis=0)
    return jnp.pad(packed, ((0, SMALL_ROWS - packed.shape[0]), (0, 0)))


def _unpack_small(packed, shapes):
    out, r = {}, 0
    for name, nrows, n in SMALL_LAYOUT:
        out[name] = packed[r:r + nrows].reshape(-1)[:n].reshape(shapes[name])
        r += nrows
    return out


def _natural(name, s):
    if name in COL_SHARDED:
        return s.transpose(1, 0, 2).reshape(s.shape[1], -1)
    return s.reshape(-1, s.shape[2])


def kernel(x, p, rel_bias, lb_param, norm_ffn1, w_ffn1_in, w_ffn1_out, norm_mix, w_in, attn_sinks, rec_norm, w_att_proj, w_rec_proj, w_out, norm_ffn2, w_ffn2_in, w_ffn2_out, norm_ple, w_ple_gate, w_ple_proj, norm_final, loss_target, m_rel_bias, m_lb_param, m_norm_ffn1, m_w_ffn1_in, m_w_ffn1_out, m_norm_mix, m_w_in, m_attn_sinks, m_rec_norm, m_w_att_proj, m_w_rec_proj, m_w_out, m_norm_ffn2, m_w_ffn2_in, m_w_ffn2_out, m_norm_ple, m_w_ple_gate, m_w_ple_proj, m_norm_final, v_rel_bias, v_lb_param, v_norm_ffn1, v_w_ffn1_in, v_w_ffn1_out, v_norm_mix, v_w_in, v_attn_sinks, v_rec_norm, v_w_att_proj, v_w_rec_proj, v_w_out, v_norm_ffn2, v_w_ffn2_in, v_w_ffn2_out, v_norm_ple, v_w_ple_gate, v_w_ple_proj, v_norm_final):
    args = dict(locals())
    wsh = {n: args[n] for n in WEIGHTS}
    B, S = x.shape[0], x.shape[1]
    T = B * S
    cx, cy, cc = _place()
    me_chip = 2 * cx + cy

    mine16 = {n: wsh[n][0].astype(BF16) for n in BIG}
    loss_p, dx, grads, part, from_chips = _local_step(
        x.reshape(T, D_MODEL), p.reshape(T, PLE_DIM), loss_target.reshape(T, D_MODEL),
        {n: wsh[n] for n in SMALL}, mine16, cc, me_chip, B, S)

    s_mine = [_add_chips("rs_add_chips_" + n, part[n][0], from_chips[n], me_chip) for n in BIG]
    s_sib = _join_halves("rs_join", s_mine)

    small_vals = {
        "rel_bias": grads["rel_bias"].T,
        "lb_param": jnp.concatenate([_colsum("dlb_sum", grads["lb_param"]),
                                     -_colsum("dlb_sum2", grads["lb_param"])], axis=0) / 8.0,
        "attn_sinks": grads["attn_sinks"][:, 0],
        "rec_norm": _colsum("drn_sum", grads["rec_norm"]).reshape(REC_HEADS, REC_DIM).sum(axis=0),
        "loss": _colsum("loss_sum", loss_p),
    }
    for n in ("norm_ffn1", "norm_mix", "norm_ffn2", "norm_ple", "norm_final"):
        small_vals[n] = _colsum(n + "_sum", grads[n])
    red = _allreduce_small(_pack_small(small_vals))
    small_shapes = {n: wsh[n].shape for n in SMALL}
    small_shapes["loss"] = (D_MODEL,)
    small = _unpack_small(red, small_shapes)
    loss = 0.5 * jnp.sum(small["loss"]) / D_MODEL

    out_g, out_d, out_m, out_v = {}, {}, {}, {}
    for n, gm, gs in zip(BIG, s_mine, s_sib):
        res = _adamw_halves("adamw_" + n, wsh[n][0], args["m_" + n][0], args["v_" + n][0], gm, gs, cc)
        out_g[n], out_d[n], out_m[n], out_v[n] = (t[None] for t in res)
    sw = _pack_small({**{n: wsh[n] for n in SMALL}, "loss": jnp.zeros((D_MODEL,), F32)})
    sm = _pack_small({**{n: args["m_" + n] for n in SMALL}, "loss": jnp.zeros((D_MODEL,), F32)})
    sv = _pack_small({**{n: args["v_" + n] for n in SMALL}, "loss": jnp.ones((D_MODEL,), F32)})
    sd, snm, snv = _adamw("adamw_small", sw, red, sm, sv)
    ud, um, uv = (_unpack_small(t, small_shapes) for t in (sd, snm, snv))
    for n in SMALL:
        out_g[n], out_d[n], out_m[n], out_v[n] = small[n], ud[n], um[n], uv[n]

    return (loss, dx.reshape(B, S, D_MODEL), *[out_g[n] for n in WEIGHTS], *[out_d[n] for n in WEIGHTS],
            *[out_m[n] for n in WEIGHTS], *[out_v[n] for n in WEIGHTS])
```

```python
import numpy as np
import jax
import jax.numpy as jnp
from jax import lax
from jax.experimental import pallas as pl
from jax.experimental.pallas import tpu as pltpu

F32 = jnp.float32
BF16 = jnp.bfloat16
MESH = pl.DeviceIdType.MESH

D_MODEL = 1024
D_FF = 2816
FF_SHARD = 2 * D_FF // 4
HEAD_DIM = 64
N_Q_HEADS = 8
ATT_BLOCK = 128
N_BUCKETS = 32
MAX_DISTANCE = 128
REC_HEADS = 4
REC_DIM = 128
PLE_DIM = 256
EPS = 1e-6
IN_W = 4864
COL_AQ, COL_AK, COL_AV, COL_RQ, COL_RF, COL_RI, COL_RG, COL_GA, COL_GB = 0, 4, 5, 6, 10, 14, 18, 22, 30

CHUNK = 64
SUB = 8
N_SUB = CHUNK // SUB
HGRN_PAIR = 2

ADAM_LR, ADAM_B1, ADAM_B2, ADAM_EPS, ADAM_WD, ADAM_STEP = 0.001, 0.9, 0.999, 1e-08, 0.01, 10

V7X_VMEM_LIMIT = 56 * 1024 * 1024
N_CHIPS = 4
N_DEV = 8

BIG = ("w_ffn1_in", "w_ffn1_out", "w_in", "w_att_proj", "w_rec_proj", "w_out",
       "w_ffn2_in", "w_ffn2_out", "w_ple_gate", "w_ple_proj")
COL_SHARDED = ("w_ffn1_in", "w_in", "w_att_proj", "w_rec_proj", "w_ffn2_in", "w_ple_proj")
WEIGHTS = ("rel_bias", "lb_param", "norm_ffn1", "w_ffn1_in", "w_ffn1_out", "norm_mix", "w_in", "attn_sinks",
           "rec_norm", "w_att_proj", "w_rec_proj", "w_out", "norm_ffn2", "w_ffn2_in", "w_ffn2_out", "norm_ple",
           "w_ple_gate", "w_ple_proj", "norm_final")
SMALL = tuple(n for n in WEIGHTS if n not in BIG)
SMALL_ROWS = 64


def _params(*sem):
    return pltpu.CompilerParams(dimension_semantics=sem, vmem_limit_bytes=V7X_VMEM_LIMIT)


def _pick(n, cap, mult=8):
    if n <= cap:
        return n
    for t in range(cap - cap % mult, 0, -mult):
        if n % t == 0:
            return t
    raise ValueError((n, cap, mult))


def _dot(a, b):
    return jnp.dot(a, b, preferred_element_type=F32)


def _dot_nt(a, b):
    return lax.dot_general(a, b, (((1,), (1,)), ((), ())), preferred_element_type=F32)


def _dot_tn(a, b):
    return lax.dot_general(a, b, (((0,), (0,)), ((), ())), preferred_element_type=F32)


def _split3(x):
    hi = x.astype(BF16)
    r = x - hi.astype(F32)
    mid = r.astype(BF16)
    lo = (r - mid.astype(F32)).astype(BF16)
    return hi, mid, lo


def _split2(x):
    hi = x.astype(BF16)
    return hi, (x - hi.astype(F32)).astype(BF16)


def _sel_left(sel_bf16, x):
    hi, mid, lo = _split3(x)
    return _dot(sel_bf16, hi) + _dot(sel_bf16, mid) + _dot(sel_bf16, lo)


def _sel_right(x, sel_bf16):
    hi, mid, lo = _split3(x)
    return _dot(hi, sel_bf16) + _dot(mid, sel_bf16) + _dot(lo, sel_bf16)


def _sigmoid(x):
    return 0.5 * jnp.tanh(0.5 * x) + 0.5


def _group8(x):
    r, w = x.shape
    return x.reshape(r // 8, 8, w).sum(axis=0)


class _Comm:
    def __init__(self, ins, out_shapes, n_sems, start, finish):
        self.ins, self.out_shapes, self.n_sems, self.start, self.finish = ins, out_shapes, n_sems, start, finish


ANY = pl.BlockSpec(memory_space=pl.ANY)


def _comm_parts(comm):
    if comm is None:
        return [], [], [], []
    sems = [pltpu.SemaphoreType.DMA((comm.n_sems,)), pltpu.SemaphoreType.DMA((comm.n_sems,))]
    return list(comm.ins), [ANY] * len(comm.ins), list(comm.out_shapes), sems


def _comm_run(comm, grid, refs, n_in, n_out):
    if comm is None:
        return (lambda: None), (lambda: None)
    nci, nco = len(comm.ins), len(comm.out_shapes)
    cin = refs[n_in:n_in + nci]
    cout = refs[n_in + nci + n_out:n_in + nci + n_out + nco]
    send_sems, recv_sems = refs[-2], refs[-1]
    ids = [pl.program_id(d) for d in range(len(grid))]
    is_first = ids[0] == 0
    is_last = ids[0] == grid[0] - 1
    for d in range(1, len(grid)):
        is_first = is_first & (ids[d] == 0)
        is_last = is_last & (ids[d] == grid[d] - 1)

    def first():
        @pl.when(is_first)
        def _():
            comm.start(cin, cout, send_sems, recv_sems)

    def last():
        @pl.when(is_last)
        def _():
            comm.finish(cin, cout, send_sems, recv_sems)

    return first, last


def _call(name, fn, grid, ins, outs, pairs=(), comm=None, j_outer=False):
    in_pair = {i for p in pairs for i in p[:2]}
    n_in, n_out = len(ins), len(outs)
    c_arrays, c_in_specs, c_out_shapes, c_sems = _comm_parts(comm)

    def body(*refs):
        first, last = _comm_run(comm, grid, refs, n_in, n_out)
        first()
        accs = []
        for ia, ib, kind in pairs:
            a, b = refs[ia][...].astype(BF16), refs[ib][...].astype(BF16)
            accs.append(_dot(a, b) if kind == "nn" else _dot_nt(a, b))
        vals = [refs[i][...] for i in range(n_in) if i not in in_pair]
        res = fn(accs, vals)
        out_refs = refs[n_in + len(c_arrays):n_in + len(c_arrays) + n_out]
        assert len(res) == len(out_refs), (name, len(res), len(out_refs))
        for o_ref, val in zip(out_refs, res):
            o_ref[...] = val.astype(o_ref.dtype)
        last()

    if j_outer:
        grid = (grid[1], grid[0])
        swap = lambda im: (lambda j, i: im(i, j))
        ins = [(a, blk, swap(im)) for a, blk, im in ins]
        outs = [(shp, dt, blk, swap(im)) for shp, dt, blk, im in outs]

    return pl.pallas_call(
        body, name=name, grid=grid,
        in_specs=[pl.BlockSpec(blk, im) for _, blk, im in ins] + c_in_specs,
        out_specs=[pl.BlockSpec(blk, im) for _, _, blk, im in outs] + [ANY] * len(c_out_shapes),
        out_shape=[jax.ShapeDtypeStruct(shp, dt) for shp, dt, _, _ in outs] + c_out_shapes,
        scratch_shapes=c_sems,
        compiler_params=_params(*(["arbitrary"] * len(grid))))(*[a for a, _, _ in ins], *c_arrays)


def _tile_call(name, fn, M, N, tm, tn, *, pairs=(), tiles=(), consts=(), outs=(), parts=0, comm=None,
               j_outer=False):
    gi, gj = M // tm, N // tn
    assert gi * tm == M and gj * tn == N, (name, M, N, tm, tn)
    ins, prs = [], []
    for a, a_col, b, kind in pairs:
        K = b.shape[0] if kind == "nn" else b.shape[1]
        ins.append((a, (tm, K), lambda i, j, c=a_col: (i, c)))
        if kind == "nn":
            ins.append((b, (K, tn), lambda i, j: (0, j)))
        else:
            ins.append((b, (tn, K), lambda i, j: (j, 0)))
        prs.append((len(ins) - 2, len(ins) - 1, kind))
    for arr, off in tiles:
        ins.append((arr, (tm, tn), lambda i, j, o=off: (i, j + o)))
    for arr in consts:
        ins.append((arr, arr.shape, lambda i, j: (0, 0)))
    out_l = [((M, N), dt, (tm, tn), lambda i, j: (i, j)) for dt in outs]
    out_l += [((gi * 8, N), F32, (8, tn), lambda i, j: (i, j))] * parts
    nt = len(tiles)

    def wrapped(accs, vals):
        return fn(accs, vals[:nt], vals[nt:])

    return _call(name, wrapped, (gi, gj), ins, out_l, prs, comm=comm, j_outer=j_outer)


def _mm_tn(name, grid, a_in, b_in, outs):
    nk = grid[2]
    tm = [d for d in a_in[1] if d is not None][1]
    tn = [d for d in b_in[1] if d is not None][1]

    def body(a_ref, b_ref, *rest):
        out_refs, acc_ref = rest[:-1], rest[-1]
        k = pl.program_id(2)

        @pl.when(k == 0)
        def _():
            acc_ref[...] = jnp.zeros_like(acc_ref)

        acc_ref[...] += _dot_tn(a_ref[...].astype(BF16), b_ref[...].astype(BF16))

        @pl.when(k == nk - 1)
        def _():
            for o_ref in out_refs:
                if len(o_ref.shape) == 3:
                    n = o_ref.shape[2]
                    for s in range(o_ref.shape[0]):
                        o_ref[s] = acc_ref[:, n * s:n * (s + 1)].astype(o_ref.dtype)
                else:
                    o_ref[...] = acc_ref[...].astype(o_ref.dtype)

    return pl.pallas_call(
        body, name=name, grid=grid,
        in_specs=[pl.BlockSpec(a_in[1], a_in[2]), pl.BlockSpec(b_in[1], b_in[2])],
        out_specs=[pl.BlockSpec(blk, im) for _, _, blk, im in outs],
        out_shape=[jax.ShapeDtypeStruct(shp, dt) for shp, dt, _, _ in outs],
        scratch_shapes=[pltpu.VMEM((tm, tn), F32)],
        compiler_params=_params("arbitrary", "arbitrary", "arbitrary"))(a_in[0], b_in[0])


def _grad_pair(shape, block, imap):
    return [(shape, F32, block, imap), (shape, BF16, block, imap)]


def _mm_tn_rows(name, a, b, tk=2048):
    T, a_w = a.shape
    b_w = b.shape[1]
    tm = _pick(a_w, 1408, 128)
    tk = _pick(T, tk, 128)
    g32, g16 = _mm_tn(name, (a_w // tm, 1, T // tk),
                      (a, (tk, tm), lambda i, j, k: (k, i)), (b, (tk, b_w), lambda i, j, k: (k, 0)),
                      _grad_pair((a_w, b_w), (tm, b_w), lambda i, j, k: (i, 0)))
    shp = (N_CHIPS, a_w // N_CHIPS, b_w)
    return g32.reshape(shp), g16.reshape(shp)


def _mm_tn_cols(name, a, b, tk=4096):
    T, a_w = a.shape
    b_w = b.shape[1]
    tk = _pick(T, tk, 128)
    shp = (N_CHIPS, a_w, b_w // N_CHIPS)
    return _mm_tn(name, (1, 1, T // tk),
                  (a, (tk, a_w), lambda i, j, k: (k, 0)), (b, (tk, b_w), lambda i, j, k: (k, 0)),
                  _grad_pair(shp, shp, lambda i, j, k: (0, 0, 0)))


def _colsum(name, x):
    def body(x_ref, o_ref):
        o_ref[...] = jnp.sum(x_ref[...], axis=0, keepdims=True)
    return pl.pallas_call(body, name=name, out_shape=jax.ShapeDtypeStruct((1, x.shape[1]), F32))(x)


def _rms_hat(h):
    return h * lax.rsqrt(jnp.mean(h * h, axis=-1, keepdims=True) + EPS)


def _rms_bwd_vals(dn, h, g):
    r = lax.rsqrt(jnp.mean(h * h, axis=-1, keepdims=True) + EPS)
    nh = h * r
    gd = dn * g
    dh = r * (gd - nh * jnp.mean(gd * nh, axis=-1, keepdims=True))
    return dh, _group8(dn * nh)


def _rms_fwd(name, h, g, tm=512, comm=None):
    T = h.shape[0]

    def fn(accs, tv, cv):
        return [_rms_hat(tv[0]) * cv[0]]

    return _tile_call(name, fn, T, D_MODEL, _pick(T, tm), D_MODEL, tiles=[(h, 0)], consts=[g], outs=[BF16],
                      comm=comm)


def _ffn_fwd(tag, h, g, w_in, w_out, g_next, n=None, comm_norm=None, w_in_of=None, comm_in=None, comm_out=None,
             w_out_of=None):
    T = h.shape[0]
    if n is None:
        n, *got_norm = _rms_fwd(tag + "_norm", h, g, comm=comm_norm)
        if w_in_of is not None:
            w_in = w_in_of(got_norm)
    tm = _pick(T, 1024)
    wblk = (None, D_MODEL, FF_SHARD)

    def act(accs, vals):
        gate, up = accs
        return [gate, up, gate * _sigmoid(gate) * up]

    tile = lambda: ((T, D_FF), BF16, (tm, FF_SHARD), lambda i, j: (i, j))
    gate, up, a, *got_in = _call(
        tag + "_in", act, (T // tm, 2),
        [(n, (tm, D_MODEL), lambda i, j: (i, 0)),
         (w_in, wblk, lambda i, j: (j, 0, 0)), (w_in, wblk, lambda i, j: (j + 2, 0, 0))],
        [tile(), tile(), tile()], pairs=[(0, 1, "nn"), (0, 2, "nn")], comm=comm_in, j_outer=True)

    def res(accs, tv, cv):
        h_new = tv[0] + 0.5 * accs[0]
        return [h_new, _rms_hat(h_new) * cv[0]]

    if w_out_of is not None:
        w_out = w_out_of(got_in)
    h_new, n_next, *got_out = _tile_call(
        tag + "_out", res, T, D_MODEL, _pick(T, 512), D_MODEL, pairs=[(a, 0, w_out, "nn")], tiles=[(h, 0)],
        consts=[g_next], outs=[F32, BF16], comm=comm_out)
    return h_new, n_next, (n, gate, up, a), got_out


def _ffn_bwd(tag, dh_out, df, h, g, w_in, w_out, saved, comm=None, comm_last=None):
    T = h.shape[0]
    n, gate, up, a = saved
    tm = _pick(T, 512)

    def dact(accs, vals):
        da = accs[0]
        gt, u = vals[0].astype(F32), vals[1].astype(F32)
        sg = _sigmoid(gt)
        silu = gt * sg
        return [jnp.stack([(da * u * (sg + silu * (1.0 - sg))).astype(BF16), (da * silu).astype(BF16)])]

    dz, *got = _call(
        tag + "_dact", dact, (T // tm, 2),
        [(df, (tm, D_MODEL), lambda i, j: (i, 0)), (w_out, (FF_SHARD, D_MODEL), lambda i, j: (j, 0)),
         (gate, (tm, FF_SHARD), lambda i, j: (i, j)), (up, (tm, FF_SHARD), lambda i, j: (i, j))],
        [((2, T, D_FF), BF16, (2, tm, FF_SHARD), lambda i, j: (0, i, j))], pairs=[(0, 1, "nt")], comm=comm,
        j_outer=True)
    dw_out = _mm_tn_rows(tag + "_dwout", a, df)
    tk = _pick(T, 2048, 128)
    dw_in = _mm_tn(tag + "_dwin", (1, N_CHIPS, T // tk),
                   (n, (tk, D_MODEL), lambda i, j, k: (k, 0)),
                   (dz, (None, tk, FF_SHARD), lambda i, j, k: (j // 2, k, j % 2)),
                   _grad_pair((N_CHIPS, D_MODEL, FF_SHARD), (None, D_MODEL, FF_SHARD), lambda i, j, k: (j, 0, 0)))

    def dnorm(accs, vals):
        dn = accs[0] + accs[1] + accs[2] + accs[3]
        dh, dg = _rms_bwd_vals(dn, vals[0], vals[2])
        dh = vals[1] + dh
        return [dh, dh, dg]

    tm2 = _pick(T, 512)
    ins = [(dz, (None, tm2, FF_SHARD), lambda i, j, s=s: (s // 2, i, s % 2)) for s in range(N_CHIPS)]
    ins += [(w_in, (None, D_MODEL, FF_SHARD), lambda i, j, s=s: (s, 0, 0)) for s in range(N_CHIPS)]
    ins += [(h, (tm2, D_MODEL), lambda i, j: (i, 0)), (dh_out, (tm2, D_MODEL), lambda i, j: (i, 0)),
            (g, g.shape, lambda i, j: (0, 0))]
    dh, dh16, dg, *got_last = _call(
        tag + "_dnorm", dnorm, (T // tm2, 1), ins,
        [((T, D_MODEL), F32, (tm2, D_MODEL), lambda i, j: (i, 0)),
         ((T, D_MODEL), BF16, (tm2, D_MODEL), lambda i, j: (i, 0)),
         ((T // tm2 * 8, D_MODEL), F32, (8, D_MODEL), lambda i, j: (i, 0))],
        pairs=[(s, N_CHIPS + s, "nt") for s in range(N_CHIPS)],
        comm=None if comm_last is None else comm_last(dw_in, dw_out))
    return dh, dh16, dg, dw_in, dw_out, got, got_last


def _t5_onehot():
    qi = np.arange(ATT_BLOCK)[:, None] + ATT_BLOCK
    kj = np.arange(2 * ATT_BLOCK)[None, :]
    nn = np.maximum(qi - kj, 0)
    max_exact = N_BUCKETS // 2
    large = max_exact + (np.log(np.maximum(nn, 1) / max_exact) / np.log(MAX_DISTANCE / max_exact)
                         * (N_BUCKETS - max_exact)).astype(np.int32)
    large = np.minimum(large, N_BUCKETS - 1)
    bucket = np.where(nn < max_exact, nn, large).astype(np.int32).reshape(-1)
    return (bucket[None, :] == np.arange(N_BUCKETS)[:, None]).astype(np.float32)


def _small_mm(name, a, b, sel):
    def body(a_ref, b_ref, o_ref):
        if sel == "right":
            o_ref[...] = _sel_right(a_ref[...], b_ref[...])
        else:
            o_ref[...] = _sel_left(a_ref[...], b_ref[...])
    return pl.pallas_call(body, name=name, out_shape=jax.ShapeDtypeStruct((a.shape[0], b.shape[1]), F32),
                          compiler_params=pltpu.CompilerParams(vmem_limit_bytes=V7X_VMEM_LIMIT))(a, b)


def _dup_heads(t):
    a, b = t[:, :HEAD_DIM], t[:, HEAD_DIM:]
    return jnp.concatenate([a, a, b, b], axis=1)


def _kv_layouts(proj):
    T = proj.shape[0]

    def fn(accs, tv, cv):
        return [tv[0], tv[1]]

    k, v = _tile_call("kv_cast", fn, T, 128, _pick(T, 1024), 128, tiles=[(proj, COL_AK), (proj, COL_AV)],
                      outs=[BF16, BF16])
    return _dup_heads(k), _dup_heads(v)


def _swa_masks():
    row = lax.broadcasted_iota(jnp.int32, (ATT_BLOCK, 2 * ATT_BLOCK), 0)
    col = lax.broadcasted_iota(jnp.int32, (ATT_BLOCK, 2 * ATT_BLOCK), 1)
    dist = ATT_BLOCK + row - col
    return (dist >= 0) & (dist < ATT_BLOCK), col


GROUP = 4


def _stack_group(blk, lo_q):
    zero = jnp.zeros_like(blk[:, :128])
    rows = []
    for pair in range(GROUP // 2):
        pb = blk[:, 128 * pair:128 * (pair + 1)]
        rows += [jnp.where(lo_q, pb, zero), jnp.where(lo_q, zero, pb)]
    return jnp.concatenate(rows, axis=0)


def _unstack_group(st, lo_q):
    pairs = [jnp.where(lo_q, st[256 * pair:256 * pair + 128], st[256 * pair + 128:256 * (pair + 1)])
             for pair in range(GROUP // 2)]
    return jnp.concatenate(pairs, axis=1)


def _swa_probs(s, bias_h, sink, valid):
    s = jnp.where(valid, s * (HEAD_DIM ** -0.5) + bias_h, -jnp.inf)
    m = jnp.maximum(jnp.max(s, axis=-1, keepdims=True), sink)
    e = jnp.exp(s - m)
    es = jnp.exp(sink - m)
    den = jnp.sum(e, axis=-1, keepdims=True) + es
    return e / den, es / den


def _swa_fwd(proj, kk2, vv2, bias, sinks, B, S):
    T = B * S
    nb = S // ATT_BLOCK

    def body(q_ref, k_ref, v_ref, bias_ref, sink_ref, o_ref, kpad, vpad):
        zeros = jnp.zeros((ATT_BLOCK, 256), BF16)
        kpad[pl.ds(0, ATT_BLOCK), :] = zeros
        vpad[pl.ds(0, ATT_BLOCK), :] = zeros
        kpad[pl.ds(ATT_BLOCK, S), :] = k_ref[...]
        vpad[pl.ds(ATT_BLOCK, S), :] = v_ref[...]
        valid0, col = _swa_masks()
        lo_q = lax.broadcasted_iota(jnp.int32, (1, 128), 1) < HEAD_DIM

        def blk(n, carry):
            r0 = pl.multiple_of(n * ATT_BLOCK, ATT_BLOCK)
            rows = pl.ds(r0, ATT_BLOCK)
            valid = valid0 & ((n > 0) | (col >= ATT_BLOCK))
            for g in range(N_Q_HEADS // GROUP):
                lanes = pl.ds(128 * g, 128)
                kg = kpad[pl.ds(r0, 2 * ATT_BLOCK), lanes]
                vg = vpad[pl.ds(r0, 2 * ATT_BLOCK), lanes]
                qm = _stack_group(q_ref[rows, pl.ds(256 * g, 256)].astype(BF16), lo_q)
                s = _dot_nt(qm, kg)
                ps = []
                for i in range(GROUP):
                    h = GROUP * g + i
                    p, _ = _swa_probs(s[ATT_BLOCK * i:ATT_BLOCK * (i + 1)], bias_ref[h], sink_ref[h], valid)
                    ps.append(p.astype(BF16))
                o = _dot(jnp.concatenate(ps, axis=0), vg)
                o_ref[rows, pl.ds(256 * g, 256)] = _unstack_group(o, lo_q).astype(o_ref.dtype)
            return carry

        if nb % 2 == 0:
            lax.fori_loop(0, nb // 2, lambda i, c: blk(2 * i + 1, blk(2 * i, c)), 0)
        else:
            lax.fori_loop(0, nb, blk, 0)

    return pl.pallas_call(
        body, name="swa_fwd", grid=(B,),
        in_specs=[pl.BlockSpec((S, 512), lambda b: (b, 0)),
                  pl.BlockSpec((S, 256), lambda b: (b, 0)),
                  pl.BlockSpec((S, 256), lambda b: (b, 0)),
                  pl.BlockSpec((N_Q_HEADS, ATT_BLOCK, 2 * ATT_BLOCK), lambda b: (0, 0, 0)),
                  pl.BlockSpec(memory_space=pltpu.SMEM)],
        out_specs=pl.BlockSpec((S, 512), lambda b: (b, 0)),
        out_shape=jax.ShapeDtypeStruct((T, 512), BF16),
        scratch_shapes=[pltpu.VMEM((S + ATT_BLOCK, 256), BF16), pltpu.VMEM((S + ATT_BLOCK, 256), BF16)],
        compiler_params=_params("arbitrary"))(proj, kk2, vv2, bias, sinks)


def _swa_bwd(proj, kk2, vv2, bias, sinks, datt, B, S):
    T = B * S
    nb = S // ATT_BLOCK

    def body(q_ref, k_ref, v_ref, bias_ref, sink_ref, do_ref, dq_ref, dk_ref, dv_ref, dbias_ref, dsink_ref,
             kpad, vpad, dkpad, dvpad):
        b = pl.program_id(0)

        @pl.when(b == 0)
        def _():
            dbias_ref[...] = jnp.zeros_like(dbias_ref)
            dsink_ref[...] = jnp.zeros_like(dsink_ref)

        zeros = jnp.zeros((ATT_BLOCK, 256), BF16)
        kpad[pl.ds(0, ATT_BLOCK), :] = zeros
        vpad[pl.ds(0, ATT_BLOCK), :] = zeros
        kpad[pl.ds(ATT_BLOCK, S), :] = k_ref[...]
        vpad[pl.ds(ATT_BLOCK, S), :] = v_ref[...]
        dkpad[...] = jnp.zeros_like(dkpad)
        dvpad[...] = jnp.zeros_like(dvpad)
        valid0, col = _swa_masks()
        lo_q = lax.broadcasted_iota(jnp.int32, (1, 128), 1) < HEAD_DIM
        scale = HEAD_DIM ** -0.5

        def blk(n, carry):
            r0 = pl.multiple_of(n * ATT_BLOCK, ATT_BLOCK)
            rows = pl.ds(r0, ATT_BLOCK)
            band = pl.ds(r0, 2 * ATT_BLOCK)
            valid = valid0 & ((n > 0) | (col >= ATT_BLOCK))
            for g in range(N_Q_HEADS // GROUP):
                lanes = pl.ds(128 * g, 128)
                kg = kpad[band, lanes]
                vg = vpad[band, lanes]
                qm = _stack_group(q_ref[rows, pl.ds(256 * g, 256)].astype(BF16), lo_q)
                dom = _stack_group(do_ref[rows, pl.ds(256 * g, 256)], lo_q)
                s = _dot_nt(qm, kg)
                dp = _dot_nt(dom, vg)
                pst, dst = [], []
                for i in range(GROUP):
                    h = GROUP * g + i
                    sl = slice(ATT_BLOCK * i, ATT_BLOCK * (i + 1))
                    p, ps = _swa_probs(s[sl], bias_ref[h], sink_ref[h], valid)
                    delta = jnp.sum(p * dp[sl], axis=-1, keepdims=True)
                    ds = p * (dp[sl] - delta)
                    dbias_ref[h] += ds
                    dsink_ref[pl.ds(h, 1), :] += -jnp.sum(jnp.broadcast_to(ps * delta, (ATT_BLOCK, 128)),
                                                          axis=0, keepdims=True)
                    pst.append(p.astype(BF16))
                    dst.append((ds * scale).astype(BF16))
                pst, dst = jnp.concatenate(pst, axis=0), jnp.concatenate(dst, axis=0)
                dq_ref[rows, pl.ds(256 * g, 256)] = _unstack_group(_dot(dst, kg), lo_q).astype(dq_ref.dtype)
                dkpad[band, lanes] += _dot_tn(dst, qm)
                dvpad[band, lanes] += _dot_tn(pst, dom)
            return carry

        if nb % 2 == 0:
            lax.fori_loop(0, nb // 2, lambda i, c: blk(2 * i + 1, blk(2 * i, c)), 0)
        else:
            lax.fori_loop(0, nb, blk, 0)
        lo_out = lax.broadcasted_iota(jnp.int32, (1, 128), 1) < HEAD_DIM

        def fold(pad_ref):
            halves = []
            for g in range(N_Q_HEADS // GROUP):
                t = pad_ref[pl.ds(ATT_BLOCK, S), pl.ds(128 * g, 128)]
                halves.append(t + pltpu.roll(t, HEAD_DIM, 1))
            return jnp.where(lo_out, halves[0], halves[1])

        dk_ref[...] = fold(dkpad).astype(dk_ref.dtype)
        dv_ref[...] = fold(dvpad).astype(dv_ref.dtype)

    return pl.pallas_call(
        body, name="swa_bwd", grid=(B,),
        in_specs=[pl.BlockSpec((S, 512), lambda b: (b, 0)),
                  pl.BlockSpec((S, 256), lambda b: (b, 0)),
                  pl.BlockSpec((S, 256), lambda b: (b, 0)),
                  pl.BlockSpec((N_Q_HEADS, ATT_BLOCK, 2 * ATT_BLOCK), lambda b: (0, 0, 0)),
                  pl.BlockSpec(memory_space=pltpu.SMEM),
                  pl.BlockSpec((S, 512), lambda b: (b, 0))],
        out_specs=[pl.BlockSpec((S, 512), lambda b: (b, 0)),
                   pl.BlockSpec((S, 128), lambda b: (b, 0)),
                   pl.BlockSpec((S, 128), lambda b: (b, 0)),
                   pl.BlockSpec((N_Q_HEADS, ATT_BLOCK, 2 * ATT_BLOCK), lambda b: (0, 0, 0)),
                   pl.BlockSpec((N_Q_HEADS, 128), lambda b: (0, 0))],
        out_shape=[jax.ShapeDtypeStruct((T, 512), BF16),
                   jax.ShapeDtypeStruct((T, 128), BF16),
                   jax.ShapeDtypeStruct((T, 128), BF16),
                   jax.ShapeDtypeStruct((N_Q_HEADS, ATT_BLOCK, 2 * ATT_BLOCK), F32),
                   jax.ShapeDtypeStruct((N_Q_HEADS, 128), F32)],
        scratch_shapes=[pltpu.VMEM((S + ATT_BLOCK, 256), BF16), pltpu.VMEM((S + ATT_BLOCK, 256), BF16),
                        pltpu.VMEM((S + ATT_BLOCK, 256), F32), pltpu.VMEM((S + ATT_BLOCK, 256), F32)],
        compiler_params=_params("arbitrary"))(proj, kk2, vv2, bias, sinks, datt)


def _hgrn_gates(z, lb):
    sg = _sigmoid(z)
    f = lb + (1.0 - lb) * sg
    return sg, f, jnp.log(f), 1.0 - f


def _hgrn_consts():
    r = lax.broadcasted_iota(jnp.int32, (CHUNK, CHUNK), 0)
    c = lax.broadcasted_iota(jnp.int32, (CHUNK, CHUNK), 1)
    tril = (r >= c).astype(BF16)
    triu = (r <= c).astype(BF16)
    causal = r >= c
    below = (r // SUB) > (c // SUB)
    inside = ((r // SUB) == (c // SUB)) & causal
    return tril, triu, causal, below, inside, c


def _block_rows(ref, lanes, s):
    rows = []
    for i in range(N_SUB):
        if SUB * i + s < 0:
            rows.append(jnp.zeros((SUB, REC_DIM), F32))
        else:
            rows.append(jnp.broadcast_to(ref[pl.ds(SUB * i + s, 1), lanes], (SUB, REC_DIM)))
    return jnp.concatenate(rows, axis=0)


def _hgrn_offdiag(q, k, bcum, b_ref, lanes):
    eq = jnp.exp(jnp.minimum(bcum - _block_rows(b_ref, lanes, -1), 0.0))
    qe = q * eq
    zero = jnp.zeros((SUB, REC_DIM), F32)
    q_rows, k_cols, eks = [jnp.zeros((SUB, (N_SUB - 1) * REC_DIM), F32)], [], []
    for i in range(1, N_SUB):
        q_rows.append(jnp.concatenate([zero] * (i - 1) + [qe[SUB * i:SUB * (i + 1), :]] + [zero] * (N_SUB - 1 - i),
                                      axis=1))
        p = b_ref[pl.ds(SUB * i - 1, 1), lanes]
        pad = jnp.zeros((CHUNK - SUB * i, REC_DIM), F32)
        ek = jnp.concatenate([jnp.exp(p - b_ref[pl.ds(0, SUB * i), lanes]), pad], axis=0)
        k_cols.append(k * ek)
        eks.append(ek)
    return jnp.concatenate(q_rows, axis=0), jnp.concatenate(k_cols, axis=1), eq, eks


def _hgrn_fwd(proj, lb_param, B, S, comm=None):
    T = B * S
    nc = S // CHUNK
    fwd_unroll = 4 if nc % 4 == 0 else 2
    c_arrays, c_in_specs, c_out_shapes, c_sems = _comm_parts(comm)
    nci, nco = len(c_arrays), len(c_out_shapes)

    def body(*refs):
        q_ref, z_ref, v_ref, lb_ref = refs[:4]
        o_ref, st_ref = refs[4 + nci:6 + nci]
        k_slots, b_slots = refs[6 + nci + nco:8 + nci + nco]
        comm_first, comm_last = _comm_run(comm, (B, REC_HEADS // HGRN_PAIR), refs, 4, 2)
        comm_first()
        tril, _, _, below, inside, col = _hgrn_consts()
        col_s = col & (SUB - 1)

        def chunk(ci, hts, slot):
            k_s, b_s = k_slots.at[slot], b_slots.at[slot]
            r0 = pl.multiple_of(ci * CHUNK, CHUNK)
            lb = _sigmoid(lb_ref[0:1, :] - lb_ref[1:2, :])
            _, _, g_all, k_all = _hgrn_gates(z_ref[pl.ds(r0, CHUNK), :], lb)
            b_all = _sel_left(tril, g_all)
            k_s[...] = k_all
            b_s[...] = b_all
            new = []
            for e, ht in enumerate(hts):
                lanes = pl.ds(REC_DIM * e, REC_DIM)
                cols = slice(REC_DIM * e, REC_DIM * (e + 1))
                q = q_ref[pl.ds(r0, CHUNK), lanes]
                v = v_ref[pl.ds(r0, CHUNK), lanes]
                k, bcum = k_all[:, cols], b_all[:, cols]
                st_ref[e * nc + ci] = ht
                qst, kst, _, _ = _hgrn_offdiag(q, k, bcum, b_s, lanes)
                d = jnp.zeros((CHUNK, CHUNK), F32)
                for s in range(SUB):
                    w = jnp.exp(jnp.minimum(bcum - _block_rows(b_s, lanes, s), 0.0))
                    colv = jnp.sum(q * _block_rows(k_s, lanes, s) * w, axis=-1, keepdims=True)
                    d = jnp.where(col_s == s, colv, d)
                a = jnp.where(below, _dot_nt(qst.astype(BF16), kst.astype(BF16)), 0.0) + jnp.where(inside, d, 0.0)
                vb = v.astype(BF16)
                qb = (q * jnp.exp(bcum)).astype(BF16)
                o_ref[pl.ds(r0, CHUNK), lanes] = _dot(a.astype(BF16), vb) + _dot_nt(qb, ht.astype(BF16))
                b_last = b_s[pl.ds(CHUNK - 1, 1), lanes]
                kb = (k * jnp.exp(b_last - bcum)).astype(BF16)
                new.append(ht * jnp.exp(b_last) + _dot_tn(vb, kb))
            return tuple(new)

        def trip(i, hts):
            for u in range(fwd_unroll):
                hts = chunk(fwd_unroll * i + u, hts, u)
            return hts

        lax.fori_loop(0, nc // fwd_unroll, trip, tuple(jnp.zeros((REC_DIM, REC_DIM), F32) for _ in range(HGRN_PAIR)))
        comm_last()

    hp, wd = REC_HEADS // HGRN_PAIR, HGRN_PAIR * REC_DIM
    cq, cf, ci_ = (c * REC_DIM // wd for c in (COL_RQ, COL_RF, COL_RI))
    return pl.pallas_call(
        body, name="hgrn_fwd", grid=(B, hp),
        in_specs=[pl.BlockSpec((S, wd), lambda b, h: (b, cq + h)),
                  pl.BlockSpec((S, wd), lambda b, h: (b, cf + h)),
                  pl.BlockSpec((S, wd), lambda b, h: (b, ci_ + h)),
                  pl.BlockSpec((2, wd), lambda b, h: (0, h))] + c_in_specs,
        out_specs=[pl.BlockSpec((S, wd), lambda b, h: (b, h)),
                   pl.BlockSpec((HGRN_PAIR * nc, REC_DIM, REC_DIM), lambda b, h: (b * hp + h, 0, 0))] + [ANY] * nco,
        out_shape=[jax.ShapeDtypeStruct((T, 512), F32),
                   jax.ShapeDtypeStruct((B * REC_HEADS * nc, REC_DIM, REC_DIM), F32)] + c_out_shapes,
        scratch_shapes=[pltpu.VMEM((fwd_unroll, CHUNK, wd), F32), pltpu.VMEM((fwd_unroll, CHUNK, wd), F32)] + c_sems,
        compiler_params=_params("arbitrary", "arbitrary"))(proj, proj, proj, lb_param, *c_arrays)


def _hgrn_bwd(proj, lb_param, states, do, B, S, comm=None):
    T = B * S
    nc = S // CHUNK
    bwd_unroll = 4 if nc % 4 == 0 else 2
    c_arrays, c_in_specs, c_out_shapes, c_sems = _comm_parts(comm)
    nci, nco = len(c_arrays), len(c_out_shapes)

    def body(*refs):
        q_ref, z_ref, v_ref, lb_ref, st_ref, do_ref = refs[:6]
        dq_ref, dz_ref, dv_ref, dlb_ref = refs[6 + nci:10 + nci]
        slots = refs[10 + nci + nco:14 + nci + nco]
        comm_first, comm_last = _comm_run(comm, (B, REC_HEADS // HGRN_PAIR), refs, 6, 4)
        comm_first()
        tril, triu, causal, below, inside, col = _hgrn_consts()
        col_s = col & (SUB - 1)
        last_row = lax.broadcasted_iota(jnp.int32, (CHUNK, 1), 0) == CHUNK - 1
        rc = lax.broadcasted_iota(jnp.int32, (CHUNK, SUB * REC_DIM), 0)
        lc = lax.broadcasted_iota(jnp.int32, (CHUNK, SUB * REC_DIM), 1)
        spread = ((rc & (SUB - 1)) == (lc // REC_DIM)).astype(BF16)
        rr = lax.broadcasted_iota(jnp.int32, (CHUNK, SUB * CHUNK), 0)
        cc = lax.broadcasted_iota(jnp.int32, (CHUNK, SUB * CHUNK), 1)
        gather = (((rr // SUB) == ((cc & (CHUNK - 1)) // SUB)) & ((rr & (SUB - 1)) == (cc // CHUNK))).astype(BF16)

        heads = range(HGRN_PAIR)
        cols = [slice(REC_DIM * e, REC_DIM * (e + 1)) for e in heads]
        lanes = [pl.ds(REC_DIM * e, REC_DIM) for e in heads]
        lane_cat = lambda vals: jnp.concatenate(vals, axis=1)
        row_cat = lambda vals: jnp.concatenate(vals, axis=0)

        def chunk(it, carry, slot):
            k_s, b_s, pc_hi, pc_lo = (r.at[slot] for r in slots)
            dhts, dlb = carry
            ci = nc - 1 - it
            r0 = pl.multiple_of(ci * CHUNK, CHUNK)
            rows = pl.ds(r0, CHUNK)
            lb = _sigmoid(lb_ref[0:1, :] - lb_ref[1:2, :])
            sg, f, g_all, k_all = _hgrn_gates(z_ref[rows, :], lb)
            b_all = _sel_left(tril, g_all)
            k_s[...] = k_all
            b_s[...] = b_all
            q_all = q_ref[rows, :]
            das, hd = [], []
            for e in heads:
                vb, dob = v_ref[rows, lanes[e]].astype(BF16), do_ref[rows, lanes[e]].astype(BF16)
                da = jnp.where(causal, _dot_nt(dob, vb), 0.0)
                das.append(jnp.where(inside, da, 0.0))
                hd.append((vb, dob, da))
            da_hi, da_lo = _split2(row_cat(das))
            da_in = _dot(da_hi, spread) + _dot(da_lo, spread)
            ds, dqs = [], []
            for e in heads:
                q, bcum = q_all[:, cols[e]], b_all[:, cols[e]]
                d = jnp.zeros((CHUNK, CHUNK), F32)
                dq = jnp.zeros((CHUNK, REC_DIM), F32)
                for s in range(SUB):
                    w = jnp.exp(jnp.minimum(bcum - _block_rows(b_s, lanes[e], s), 0.0))
                    ks = _block_rows(k_s, lanes[e], s)
                    qw = q * w
                    d = jnp.where(col_s == s, jnp.sum(qw * ks, axis=-1, keepdims=True), d)
                    da_s = da_in[CHUNK * e:CHUNK * (e + 1), REC_DIM * s:REC_DIM * (s + 1)]
                    dq = dq + da_s * ks * w
                    hi, lo = _split2(da_s * qw)
                    pc_hi[pl.ds(CHUNK * s, CHUNK), lanes[e]] = hi
                    pc_lo[pl.ds(CHUNK * s, CHUNK), lanes[e]] = lo
                ds.append(d)
                dqs.append(dq)
            dk_in = _dot(gather, pc_hi[...]) + _dot(gather, pc_lo[...])
            dq_out, dk_out, dv_out, db_out, new_dhts = [], [], [], [], []
            for e in heads:
                q, k, bcum = q_all[:, cols[e]], k_all[:, cols[e]], b_all[:, cols[e]]
                vb, dob, da = hd[e]
                dht, ht = dhts[e], st_ref[e * nc + ci]
                qst, kst, eq, eks = _hgrn_offdiag(q, k, bcum, b_s, lanes[e])
                qst_b, kst_b = qst.astype(BF16), kst.astype(BF16)
                a = jnp.where(below, _dot_nt(qst_b, kst_b), 0.0) + jnp.where(inside, ds[e], 0.0)
                da_off = jnp.where(below, da, 0.0).astype(BF16)
                dqst = _dot(da_off, kst_b)
                dkst = _dot_tn(da_off, qst_b)
                dk = dk_in[:, cols[e]]
                dq_rows = [jnp.zeros((SUB, REC_DIM), F32)]
                for i in range(1, N_SUB):
                    dq_rows.append(dqst[SUB * i:SUB * (i + 1), REC_DIM * (i - 1):REC_DIM * i])
                    dk = dk + dkst[:, REC_DIM * (i - 1):REC_DIM * i] * eks[i - 1]
                dq = dqs[e] + row_cat(dq_rows) * eq
                eb = jnp.exp(bcum)
                b_last = b_s[pl.ds(CHUNK - 1, 1), lanes[e]]
                el = jnp.exp(b_last)
                ekb = jnp.exp(b_last - bcum)
                qb = (q * eb).astype(BF16)
                kb = k * ekb
                dhb = dht.astype(BF16)
                dv_out.append(_dot_tn(a.astype(BF16), dob) + _dot_nt(kb.astype(BF16), dhb))
                dqb = _dot(dob, ht.astype(BF16))
                dkb = _dot(vb, dhb)
                new_dhts.append(dht * el + _dot_tn(dob, qb))
                dq = dq + eb * dqb
                dk = dk + ekb * dkb
                edge = jnp.sum(kb * dkb, axis=0, keepdims=True) + el * jnp.sum(ht * dht, axis=0, keepdims=True)
                db_out.append(q * dq - k * dk + jnp.where(last_row, edge, 0.0))
                dq_out.append(dq)
                dk_out.append(dk)
            dk_all = lane_cat(dk_out)
            db_hi, db_lo = _split2(lane_cat(db_out))
            dg = _dot(triu, db_hi) + _dot(triu, db_lo)
            df = dg / f - dk_all
            dz_ref[rows, :] = (df * (1.0 - lb) * sg * (1.0 - sg)).astype(dz_ref.dtype)
            dq_ref[rows, :] = lane_cat(dq_out).astype(dq_ref.dtype)
            dv_ref[rows, :] = lane_cat(dv_out).astype(dv_ref.dtype)
            return tuple(new_dhts), dlb + jnp.sum(df * (1.0 - sg), axis=0, keepdims=True)

        zero = (tuple(jnp.zeros((REC_DIM, REC_DIM), F32) for _ in heads), jnp.zeros((1, HGRN_PAIR * REC_DIM), F32))
        def trip(i, carry):
            for u in range(bwd_unroll):
                carry = chunk(bwd_unroll * i + u, carry, u)
            return carry

        _, dlb = lax.fori_loop(0, nc // bwd_unroll, trip, zero)
        lb = _sigmoid(lb_ref[0:1, :] - lb_ref[1:2, :])
        dlb_ref[...] = jnp.broadcast_to(dlb * lb * (1.0 - lb), (8, HGRN_PAIR * REC_DIM))
        comm_last()

    hp, wd = REC_HEADS // HGRN_PAIR, HGRN_PAIR * REC_DIM
    cq, cf, ci_ = (c * REC_DIM // wd for c in (COL_RQ, COL_RF, COL_RI))
    return pl.pallas_call(
        body, name="hgrn_bwd", grid=(B, hp),
        in_specs=[pl.BlockSpec((S, wd), lambda b, h: (b, cq + h)),
                  pl.BlockSpec((S, wd), lambda b, h: (b, cf + h)),
                  pl.BlockSpec((S, wd), lambda b, h: (b, ci_ + h)),
                  pl.BlockSpec((2, wd), lambda b, h: (0, h)),
                  pl.BlockSpec((HGRN_PAIR * nc, REC_DIM, REC_DIM), lambda b, h: (b * hp + h, 0, 0)),
                  pl.BlockSpec((S, wd), lambda b, h: (b, h))] + c_in_specs,
        out_specs=[pl.BlockSpec((S, wd), lambda b, h: (b, h))] * 3
        + [pl.BlockSpec((8, wd), lambda b, h: (b, h))] + [ANY] * nco,
        out_shape=[jax.ShapeDtypeStruct((T, 512), BF16)] * 3 + [jax.ShapeDtypeStruct((B * 8, 512), F32)]
        + c_out_shapes,
        scratch_shapes=[pltpu.VMEM((bwd_unroll, CHUNK, wd), F32)] * 2
        + [pltpu.VMEM((bwd_unroll, SUB * CHUNK, wd), BF16)] * 2 + c_sems,
        compiler_params=_params("arbitrary", "arbitrary"))(proj, proj, proj, lb_param, states, do, *c_arrays)


def _rec_gate_fwd(rec, proj, rec_norm):
    T = rec.shape[0]

    def fn(accs, tv, cv):
        return [_rms_hat(tv[0]) * cv[0] * _sigmoid(tv[1])]

    return _tile_call("rec_gate", fn, T, 512, _pick(T, 1024), REC_DIM, tiles=[(rec, 0), (proj, COL_RG)],
                      consts=[rec_norm], outs=[BF16])[0]


def _rec_gate_bwd(dyb, w_rec_proj, rec, proj, rec_norm):
    T = rec.shape[0]

    def fn(accs, tv, cv):
        d, r, rg = accs[0], tv[0], tv[1]
        sg = _sigmoid(rg)
        rn = _rms_hat(r) * cv[0]
        dh, dg = _rms_bwd_vals(d * sg, r, cv[0])
        return [dh, d * rn * sg * (1.0 - sg), dg]

    return _tile_call("rec_gate_bwd", fn, T, 512, _pick(T, 1024), REC_DIM, pairs=[(dyb, 0, w_rec_proj, "nt")],
                      tiles=[(rec, 0), (proj, COL_RG)], consts=[rec_norm], outs=[F32, BF16], parts=1)


def _mix_out_fwd(att, recn, proj, w_att_proj, w_rec_proj, w_out, h1, g_next):
    T = att.shape[0]
    tn = 256

    def merge(accs, tv, cv):
        ya, yb = accs
        return [ya, yb, _sigmoid(tv[0]) * ya + _sigmoid(tv[1]) * yb]

    ya, yb, merged = _tile_call(
        "merge", merge, T, D_MODEL, _pick(T, 1024), tn,
        pairs=[(att, 0, w_att_proj, "nn"), (recn, 0, w_rec_proj, "nn")],
        tiles=[(proj, COL_GA * 128 // tn), (proj, COL_GB * 128 // tn)], outs=[BF16] * 3)

    def res(accs, tv, cv):
        h2 = tv[0] + accs[0]
        return [h2, _rms_hat(h2) * cv[0]]

    h2, n2 = _tile_call("mix_out", res, T, D_MODEL, _pick(T, 1024), D_MODEL, pairs=[(merged, 0, w_out, "nn")],
                        tiles=[(h1, 0)], consts=[g_next], outs=[F32, BF16])
    return h2, n2, (ya, yb, merged)


GATHER_FIRST = ("w_ffn1_in",)
GATHER_MIX = ("w_ffn1_out", "w_in")
GATHER_PROJ = ("w_att_proj", "w_rec_proj", "w_out")
GATHER_TAIL = ("w_ffn2_out", "w_ple_gate", "w_ple_proj")
GATHER_LAST = ("w_ffn2_in",)
SCATTER_LATE = ("w_ple_gate", "w_ple_proj", "w_ffn2_in", "w_ffn2_out")
SCATTER_MIX = ("w_out", "w_att_proj", "w_rec_proj", "w_in")
SCATTER_LAST = ("w_ffn1_in", "w_ffn1_out")


def _local_step(x, p, tgt, w, mine16, cc, me_chip, B, S):
    T = B * S
    w = dict(w)
    g_ffn1, g_mix, g_ffn2, g_ple = w["norm_ffn1"], w["norm_mix"], w["norm_ffn2"], w["norm_ple"]
    g_fin = w["norm_final"].reshape(1, D_MODEL)
    grads, part, from_chips = {}, {}, {}

    def gather(names):
        return _gather_comm([mine16[n] for n in names])

    def place(names, got):
        for n, g in zip(names, got):
            full = lax.dynamic_update_index_in_dim(g, mine16[n], me_chip, 0)
            w[n] = full if n in ("w_ffn1_in", "w_ffn2_in") else _natural(n, full)

    def swap(names):
        return _swap_comm([grads[n][1] for n in names])

    def after_swap(names, from_sib):
        for n, fs in zip(names, from_sib):
            part[n] = _add_sibling("rs_add_sib_" + n, grads[n][0], fs, cc)
        return _scatter_comm([part[n][1] for n in names])

    def scattered(names, got):
        for n, g in zip(names, got):
            from_chips[n] = g

    def ffn1_in_weight(got):
        place(GATHER_FIRST, got)
        return w["w_ffn1_in"]

    def ffn1_out_weight(got):
        place(GATHER_MIX, got)
        return w["w_ffn1_out"]

    h1, u, sv1, got_proj = _ffn_fwd("ffn1", x, g_ffn1, None, None, g_mix, comm_norm=gather(GATHER_FIRST),
                                    w_in_of=ffn1_in_weight, comm_in=gather(GATHER_MIX),
                                    comm_out=gather(GATHER_PROJ), w_out_of=ffn1_out_weight)
    place(GATHER_PROJ, got_proj)

    def ident(accs, tv, cv):
        return [accs[0]]

    proj, *got_tail = _tile_call("in_proj", ident, T, IN_W, _pick(T, 1024), IN_W // 2,
                                 pairs=[(u, 0, w["w_in"], "nn")], outs=[F32], j_outer=True, comm=gather(GATHER_TAIL))
    place(GATHER_TAIL, got_tail)
    onehot = jnp.asarray(_t5_onehot())
    bias = _small_mm("t5_bias", w["rel_bias"].T, onehot.astype(BF16), "right")
    bias = bias.reshape(N_Q_HEADS, ATT_BLOCK, 2 * ATT_BLOCK)
    sinks = w["attn_sinks"].reshape(N_Q_HEADS)
    kk2, vv2 = _kv_layouts(proj)
    att = _swa_fwd(proj, kk2, vv2, bias, sinks, B, S)
    rec, states, *got_last = _hgrn_fwd(proj, w["lb_param"], B, S, comm=gather(GATHER_LAST))
    place(GATHER_LAST, got_last)
    recn = _rec_gate_fwd(rec, proj, w["rec_norm"])
    h2, n2, (ya, yb, merged) = _mix_out_fwd(att, recn, proj, w["w_att_proj"], w["w_rec_proj"], w["w_out"], h1,
                                            g_ffn2)
    h3, n3, sv2, _ = _ffn_fwd("ffn2", h2, g_ffn2, w["w_ffn2_in"], w["w_ffn2_out"], g_ple, n=n2)

    def ple(accs, tv, cv):
        gate = _sigmoid(accs[0])
        return [gate, accs[1], tv[0] + gate * accs[1]]

    gate_p, pp, h4 = _tile_call(
        "ple", ple, T, D_MODEL, _pick(T, 1024), D_MODEL,
        pairs=[(n3, 0, w["w_ple_gate"], "nn"), (p, 0, w["w_ple_proj"], "nn")], tiles=[(h3, 0)],
        outs=[BF16, BF16, F32])

    def head(accs, tv, cv):
        h, t, gt, ppv = tv[0], tv[1], tv[2].astype(F32), tv[3].astype(F32)
        err = _rms_hat(h) * cv[0] - t
        dh, dg = _rms_bwd_vals(err * (1.0 / D_MODEL), h, cv[0])
        return [dh, dh * ppv * gt * (1.0 - gt), dh * gt, _group8(err * err), dg]

    dh4, dzg, dpp, loss_p, dg_fin = _tile_call(
        "loss_head", head, T, D_MODEL, _pick(T, 512), D_MODEL,
        tiles=[(h4, 0), (tgt, 0), (gate_p, 0), (pp, 0)], consts=[g_fin], outs=[F32, BF16, BF16], parts=2)
    grads["norm_final"] = dg_fin

    grads["w_ple_gate"] = _mm_tn_rows("ple_dwg", n3, dzg)
    grads["w_ple_proj"] = _mm_tn_cols("ple_dwp", p, dpp)

    def dnorm(accs, tv, cv):
        dh, dg = _rms_bwd_vals(accs[0], tv[0], cv[0])
        dh = tv[1] + dh
        return [dh, 0.5 * dh, dg]

    dh3, df3, grads["norm_ple"] = _tile_call(
        "ple_dnorm", dnorm, T, D_MODEL, _pick(T, 512), D_MODEL, pairs=[(dzg, 0, w["w_ple_gate"], "nt")],
        tiles=[(h3, 0), (dh4, 0)], consts=[g_ple], outs=[F32, BF16], parts=1)

    def swap_late(dw_in, dw_out):
        grads["w_ffn2_in"], grads["w_ffn2_out"] = dw_in, dw_out
        return swap(SCATTER_LATE)

    dh2, dh2b, grads["norm_ffn2"], _, _, _, from_sib = _ffn_bwd(
        "ffn2b", dh3, df3, h2, g_ffn2, w["w_ffn2_in"], w["w_ffn2_out"], sv2, comm_last=swap_late)
    scatter_late = after_swap(SCATTER_LATE, from_sib)

    grads["w_out"] = _mm_tn_rows("mix_dwout", merged, dh2b)
    tn = 256

    def dmerge(accs, tv, cv):
        dm = accs[0]
        sa, sb = _sigmoid(tv[0]), _sigmoid(tv[1])
        yav, ybv = tv[2].astype(F32), tv[3].astype(F32)
        return [dm * sa, dm * sb, dm * yav * sa * (1.0 - sa), dm * ybv * sb * (1.0 - sb)]

    dya, dyb, dga, dgb = _tile_call(
        "mix_dmerge", dmerge, T, D_MODEL, _pick(T, 1024), tn, pairs=[(dh2b, 0, w["w_out"], "nt")],
        tiles=[(proj, COL_GA * 128 // tn), (proj, COL_GB * 128 // tn), (ya, 0), (yb, 0)], outs=[BF16] * 4)
    grads["w_att_proj"] = _mm_tn_cols("mix_dwatt", att, dya)
    grads["w_rec_proj"] = _mm_tn_cols("mix_dwrec", recn, dyb)

    datt = _tile_call("mix_datt", ident, T, 512, _pick(T, 1024), 512, pairs=[(dya, 0, w["w_att_proj"], "nt")],
                      outs=[BF16])[0]
    drec, drg, grads["rec_norm"] = _rec_gate_bwd(dyb, w["w_rec_proj"], rec, proj, w["rec_norm"])

    drq, drf, dri, dlb, *got = _hgrn_bwd(proj, w["lb_param"], states, drec, B, S, comm=scatter_late)
    scattered(SCATTER_LATE, got)
    grads["lb_param"] = dlb
    daq, dak, dav, dbias, dsink = _swa_bwd(proj, kk2, vv2, bias, sinks, datt, B, S)
    grads["attn_sinks"] = dsink
    grads["rel_bias"] = _small_mm("t5_dbias", dbias.reshape(N_Q_HEADS, -1), onehot.T.astype(BF16), "right")
    dproj = jnp.concatenate([daq, dak, dav, drq, drf, dri, drg, dga, dgb], axis=1)
    tk = _pick(T, 2048, 128)
    w_in_shard = IN_W // N_CHIPS
    half_d = D_MODEL // 2
    gw32, gw16 = _mm_tn("mix_dwin", (2, 2, T // tk),
                        (u, (tk, half_d), lambda i, j, k: (k, i)), (dproj, (tk, IN_W // 2), lambda i, j, k: (k, j)),
                        _grad_pair((D_MODEL, IN_W), (half_d, IN_W // 2), lambda i, j, k: (i, j)))
    to_sh = lambda t: t.reshape(D_MODEL, N_CHIPS, w_in_shard).transpose(1, 0, 2)
    grads["w_in"] = (to_sh(gw32), to_sh(gw16))

    def dnorm_mix(accs, tv, cv):
        dh, dg = _rms_bwd_vals(accs[0], tv[0], cv[0])
        dh = tv[1] + dh
        return [dh, 0.5 * dh, dg]

    dh1, df1, grads["norm_mix"], *from_sib = _tile_call(
        "mix_dnorm", dnorm_mix, T, D_MODEL, _pick(T, 512), D_MODEL, pairs=[(dproj, 0, w["w_in"], "nt")],
        tiles=[(h1, 0), (dh2, 0)], consts=[g_mix], outs=[F32, BF16], parts=1, comm=swap(SCATTER_MIX))
    scatter_mix = after_swap(SCATTER_MIX, from_sib)

    def scatter_last(dw_in, dw_out):
        grads["w_ffn1_in"], grads["w_ffn1_out"] = dw_in, dw_out
        return after_swap(SCATTER_LAST, _run_comm("rs_sibling_last", swap(SCATTER_LAST)))

    dx, _, grads["norm_ffn1"], _, _, got, got_last = _ffn_bwd(
        "ffn1b", dh1, df1, x, g_ffn1, w["w_ffn1_in"], w["w_ffn1_out"], sv1, comm=scatter_mix, comm_last=scatter_last)
    scattered(SCATTER_MIX, got)
    scattered(SCATTER_LAST, got_last)
    return loss_p, dx, grads, part, from_chips


def _place():
    x, y, c = lax.axis_index("x"), lax.axis_index("y"), lax.axis_index("c")
    return x, y, c


def _other_chips(x, y):
    return [(1 - x, y, 2 * (1 - x) + y), (x, 1 - y, 2 * x + 1 - y), (1 - x, 1 - y, 2 * (1 - x) + 1 - y)]


def _half_rows(ref_3d, chip, h, rows):
    return ref_3d.at[chip, pl.ds(h * rows, rows), :]


def _gather_comm(ws):
    nw = len(ws)

    def parts(w_refs, out_refs, send_sems, recv_sems):
        x, y, c = _place()
        me = 2 * x + y
        chips = _other_chips(x, y)

        def copy(i, k, chip, h, to, src=None):
            half = ws[i].shape[0] // 2
            dst = _half_rows(out_refs[i], chip, h, half)
            return pltpu.make_async_remote_copy(
                src_ref=dst if src is None else src, dst_ref=dst,
                send_sem=send_sems.at[6 * i + k], recv_sem=recv_sems.at[6 * i + k], device_id=to, device_id_type=MESH)

        def first():
            out = []
            for i in range(nw):
                half = ws[i].shape[0] // 2
                out += [copy(i, j, me, c, (cx, cy, c), src=w_refs[i].at[pl.ds(c * half, half), :])
                        for j, (cx, cy, _) in enumerate(chips)]
            return out

        return copy, first, chips, c, (x, y, 1 - c)

    def start(*refs):
        _, first, _, _, _ = parts(*refs)
        for cp in first():
            cp.start()

    def finish(*refs):
        copy, first, chips, c, sibling = parts(*refs)
        passed = []
        for i in range(nw):
            for j, (cx, cy, ci) in enumerate(chips):
                copy(i, j, ci, c, (cx, cy, c)).wait_recv()
                fw = copy(i, 3 + j, ci, c, sibling)
                fw.start()
                passed.append(fw)
        for i in range(nw):
            for j, (_, _, ci) in enumerate(chips):
                copy(i, 3 + j, ci, 1 - c, sibling).wait_recv()
        for cp in first() + passed:
            cp.wait_send()

    return _Comm(list(ws), [jax.ShapeDtypeStruct((N_CHIPS,) + w.shape, w.dtype) for w in ws], 6 * nw, start, finish)


def _scatter_comm(ps):
    nw = len(ps)

    def copies(p_refs, out_refs, send_sems, recv_sems):
        x, y, c = _place()
        cps = []
        for i in range(nw):
            for j, (cx, cy, ci) in enumerate(_other_chips(x, y)):
                cps.append(pltpu.make_async_remote_copy(
                    src_ref=p_refs[i].at[ci], dst_ref=out_refs[i].at[j], send_sem=send_sems.at[3 * i + j],
                    recv_sem=recv_sems.at[3 * i + j], device_id=(cx, cy, c), device_id_type=MESH))
        return cps

    def start(*refs):
        for cp in copies(*refs):
            cp.start()

    def finish(*refs):
        for cp in copies(*refs):
            cp.wait()

    return _Comm(list(ps), [jax.ShapeDtypeStruct((3,) + p.shape[1:], p.dtype) for p in ps], 3 * nw, start, finish)


def _swap_comm(gs):
    nw = len(gs)

    def copies(g_refs, out_refs, send_sems, recv_sems):
        x, y, c = _place()
        cps = []
        for i in range(nw):
            half = gs[i].shape[1] // 2
            cps.append(pltpu.make_async_remote_copy(
                src_ref=g_refs[i].at[:, pl.ds((1 - c) * half, half), :], dst_ref=out_refs[i],
                send_sem=send_sems.at[i], recv_sem=recv_sems.at[i], device_id=(x, y, 1 - c), device_id_type=MESH))
        return cps

    def start(*refs):
        for cp in copies(*refs):
            cp.start()

    def finish(*refs):
        for cp in copies(*refs):
            cp.wait()

    return _Comm(list(gs), [jax.ShapeDtypeStruct((N_CHIPS, g.shape[1] // 2, g.shape[2]), g.dtype) for g in gs],
                 nw, start, finish)


def _run_comm(name, comm):
    nci, nco = len(comm.ins), len(comm.out_shapes)

    def body(*refs):
        cin, cout, send_sems, recv_sems = refs[:nci], refs[nci:nci + nco], refs[-2], refs[-1]
        comm.start(cin, cout, send_sems, recv_sems)
        comm.finish(cin, cout, send_sems, recv_sems)

    return pl.pallas_call(
        body, name=name, in_specs=[ANY] * nci, out_specs=[ANY] * nco, out_shape=list(comm.out_shapes),
        scratch_shapes=[pltpu.SemaphoreType.DMA((comm.n_sems,)), pltpu.SemaphoreType.DMA((comm.n_sems,))],
    )(*comm.ins)


def _join_halves(name, ss):
    nw = len(ss)

    def body(*refs):
        s_refs, out_refs, send_sems, recv_sems = refs[:nw], refs[nw:2 * nw], refs[2 * nw], refs[2 * nw + 1]
        x, y, c = _place()
        cps = [pltpu.make_async_remote_copy(
            src_ref=s_refs[i], dst_ref=out_refs[i], send_sem=send_sems.at[i], recv_sem=recv_sems.at[i],
            device_id=(x, y, 1 - c), device_id_type=MESH) for i in range(nw)]
        for cp in cps:
            cp.start()
        for cp in cps:
            cp.wait()

    return pl.pallas_call(
        body, name=name, in_specs=[ANY] * nw, out_specs=[ANY] * nw,
        out_shape=[jax.ShapeDtypeStruct(s.shape, s.dtype) for s in ss],
        scratch_shapes=[pltpu.SemaphoreType.DMA((nw,)), pltpu.SemaphoreType.DMA((nw,))],
    )(*ss)


def _allreduce_small(sp):
    def body(s_ref, out_ref, slots, send_sems, recv_sems):
        x, y, c = _place()
        me = 4 * x + 2 * y + c
        slots[me] = s_ref[...]
        cps = []
        for r in range(1, N_DEV):
            px, py, pc = x ^ (r >> 2), y ^ ((r >> 1) & 1), c ^ (r & 1)
            cps.append(pltpu.make_async_remote_copy(
                src_ref=s_ref, dst_ref=slots.at[me], send_sem=send_sems.at[r - 1], recv_sem=recv_sems.at[r - 1],
                device_id=(px, py, pc), device_id_type=MESH))
        for cp in cps:
            cp.start()
        for r in range(1, N_DEV):
            px, py, pc = x ^ (r >> 2), y ^ ((r >> 1) & 1), c ^ (r & 1)
            pltpu.make_async_remote_copy(
                src_ref=s_ref, dst_ref=slots.at[4 * px + 2 * py + pc], send_sem=send_sems.at[r - 1],
                recv_sem=recv_sems.at[r - 1], device_id=(px, py, pc), device_id_type=MESH).wait_recv()
        for cp in cps:
            cp.wait_send()
        acc = slots[0]
        for d in range(1, N_DEV):
            acc = acc + slots[d]
        out_ref[...] = acc

    return pl.pallas_call(
        body, name="allreduce_small",
        in_specs=[pl.BlockSpec(memory_space=pltpu.VMEM)], out_specs=pl.BlockSpec(memory_space=pltpu.VMEM),
        out_shape=jax.ShapeDtypeStruct(sp.shape, F32),
        scratch_shapes=[pltpu.VMEM((N_DEV,) + sp.shape, F32), pltpu.SemaphoreType.DMA((N_DEV - 1,)),
                        pltpu.SemaphoreType.DMA((N_DEV - 1,))],
    )(sp)


def _scalar(v):
    return jnp.reshape(v, (1,)).astype(jnp.int32)


def _row_tile(h, dtype_mult=16):
    return _pick(h, 256, dtype_mult)


def _add_sibling(name, g32, from_sib, c):
    _, r, n = g32.shape
    h = r // 2
    th = _row_tile(h)
    nt = h // th

    def body(c_ref, g_ref, s_ref, o32_ref, o16_ref):
        s = g_ref[...] + s_ref[...].astype(F32)
        o32_ref[...] = s
        o16_ref[...] = s.astype(BF16)

    blk = (None, th, n)
    return pl.pallas_call(
        body, name=name,
        grid_spec=pltpu.PrefetchScalarGridSpec(
            num_scalar_prefetch=1, grid=(N_CHIPS, nt),
            in_specs=[pl.BlockSpec(blk, lambda k, t, c_ref: (k, c_ref[0] * nt + t, 0)),
                      pl.BlockSpec(blk, lambda k, t, c_ref: (k, t, 0))],
            out_specs=[pl.BlockSpec(blk, lambda k, t, c_ref: (k, t, 0))] * 2),
        out_shape=[jax.ShapeDtypeStruct((N_CHIPS, h, n), F32), jax.ShapeDtypeStruct((N_CHIPS, h, n), BF16)],
        compiler_params=_params("arbitrary", "arbitrary"))(_scalar(c), g32, from_sib)


def _add_chips(name, p32, from_chips, me_chip):
    _, h, n = p32.shape
    th = _row_tile(h)

    def body(m_ref, p_ref, a_ref, b_ref, c_ref, o_ref):
        o_ref[...] = p_ref[...] + a_ref[...].astype(F32) + b_ref[...].astype(F32) + c_ref[...].astype(F32)

    blk = (None, th, n)
    return pl.pallas_call(
        body, name=name,
        grid_spec=pltpu.PrefetchScalarGridSpec(
            num_scalar_prefetch=1, grid=(h // th,),
            in_specs=[pl.BlockSpec(blk, lambda t, m_ref: (m_ref[0], t, 0))]
            + [pl.BlockSpec(blk, lambda t, m_ref, j=j: (j, t, 0)) for j in range(3)],
            out_specs=pl.BlockSpec((th, n), lambda t, m_ref: (t, 0))),
        out_shape=jax.ShapeDtypeStruct((h, n), F32),
        compiler_params=_params("arbitrary"))(_scalar(me_chip), p32, from_chips, from_chips, from_chips)


def _adamw_vals(w, g, m, v):
    m = ADAM_B1 * m + (1.0 - ADAM_B1) * g
    v = ADAM_B2 * v + (1.0 - ADAM_B2) * (g * g)
    m_hat = m / (1.0 - ADAM_B1 ** ADAM_STEP)
    v_hat = v / (1.0 - ADAM_B2 ** ADAM_STEP)
    delta = -ADAM_LR * (m_hat / (jnp.sqrt(v_hat) + ADAM_EPS) + ADAM_WD * w)
    return delta, m, v


def _adamw_halves(name, w, m, v, g_mine, g_sib, c):
    r, n = w.shape
    h = r // 2
    th = _row_tile(h, 8)
    nt = h // th

    def body(c_ref, w_ref, m_ref, v_ref, a_ref, b_ref, g_ref, d_ref, nm_ref, nv_ref):
        mine = (pl.program_id(0) // nt) == c_ref[0]
        g = jnp.where(mine, a_ref[...], b_ref[...])
        d, nm, nv = _adamw_vals(w_ref[...], g, m_ref[...], v_ref[...])
        g_ref[...] = g
        d_ref[...] = d
        nm_ref[...] = nm
        nv_ref[...] = nv

    full = pl.BlockSpec((th, n), lambda t, c_ref: (t, 0))
    own_half = pl.BlockSpec((th, n), lambda t, c_ref: (jnp.where(t // nt == c_ref[0], t % nt, 0), 0))
    sib_half = pl.BlockSpec((th, n), lambda t, c_ref: (jnp.where(t // nt == c_ref[0], 0, t % nt), 0))
    return pl.pallas_call(
        body, name=name,
        grid_spec=pltpu.PrefetchScalarGridSpec(
            num_scalar_prefetch=1, grid=(2 * nt,), in_specs=[full, full, full, own_half, sib_half],
            out_specs=[full] * 4),
        out_shape=[jax.ShapeDtypeStruct((r, n), F32)] * 4,
        compiler_params=_params("arbitrary"))(_scalar(c), w, m, v, g_mine, g_sib)


def _adamw(name, w, g, m, v):
    R, W = w.shape

    def fn(accs, tv, cv):
        return list(_adamw_vals(*tv))

    return _tile_call(name, fn, R, W, _pick(R, 256), W, tiles=[(w, 0), (g, 0), (m, 0), (v, 0)], outs=[F32] * 3)


SMALL_LAYOUT = (("rel_bias", 2, 256), ("lb_param", 8, 1024), ("norm_ffn1", 8, 1024), ("norm_mix", 8, 1024),
                ("attn_sinks", 1, 8), ("rec_norm", 1, 128), ("norm_ffn2", 8, 1024), ("norm_ple", 8, 1024),
                ("norm_final", 8, 1024), ("loss", 8, 1024))


def _pack_small(vals):
    rows = []
    for name, nrows, n in SMALL_LAYOUT:
        flat = vals[name].reshape(-1)
        flat = jnp.pad(flat, (0, nrows * 128 - n))
        rows.append(flat.reshape(nrows, 128))
    packed = jnp.concatenate(rows, axis=0)
    return jnp.pad(packed, ((0, SMALL_ROWS - packed.shape[0]), (0, 0)))


def _unpack_small(packed, shapes):
    out, r = {}, 0
    for name, nrows, n in SMALL_LAYOUT:
        out[name] = packed[r:r + nrows].reshape(-1)[:n].reshape(shapes[name])
        r += nrows
    return out


def _natural(name, s):
    if name in COL_SHARDED:
        return s.transpose(1, 0, 2).reshape(s.shape[1], -1)
    return s.reshape(-1, s.shape[2])


def kernel(x, p, rel_bias, lb_param, norm_ffn1, w_ffn1_in, w_ffn1_out, norm_mix, w_in, attn_sinks, rec_norm, w_att_proj, w_rec_proj, w_out, norm_ffn2, w_ffn2_in, w_ffn2_out, norm_ple, w_ple_gate, w_ple_proj, norm_final, loss_target, m_rel_bias, m_lb_param, m_norm_ffn1, m_w_ffn1_in, m_w_ffn1_out, m_norm_mix, m_w_in, m_attn_sinks, m_rec_norm, m_w_att_proj, m_w_rec_proj, m_w_out, m_norm_ffn2, m_w_ffn2_in, m_w_ffn2_out, m_norm_ple, m_w_ple_gate, m_w_ple_proj, m_norm_final, v_rel_bias, v_lb_param, v_norm_ffn1, v_w_ffn1_in, v_w_ffn1_out, v_norm_mix, v_w_in, v_attn_sinks, v_rec_norm, v_w_att_proj, v_w_rec_proj, v_w_out, v_norm_ffn2, v_w_ffn2_in, v_w_ffn2_out, v_norm_ple, v_w_ple_gate, v_w_ple_proj, v_norm_final):
    args = dict(locals())
    wsh = {n: args[n] for n in WEIGHTS}
    B, S = x.shape[0], x.shape[1]
    T = B * S
    cx, cy, cc = _place()
    me_chip = 2 * cx + cy

    mine16 = {n: wsh[n][0].astype(BF16) for n in BIG}
    loss_p, dx, grads, part, from_chips = _local_step(
        x.reshape(T, D_MODEL), p.reshape(T, PLE_DIM), loss_target.reshape(T, D_MODEL),
        {n: wsh[n] for n in SMALL}, mine16, cc, me_chip, B, S)

    s_mine = [_add_chips("rs_add_chips_" + n, part[n][0], from_chips[n], me_chip) for n in BIG]
    s_sib = _join_halves("rs_join", s_mine)

    small_vals = {
        "rel_bias": grads["rel_bias"].T,
        "lb_param": jnp.concatenate([_colsum("dlb_sum", grads["lb_param"]),
                                     -_colsum("dlb_sum2", grads["lb_param"])], axis=0) / 8.0,
        "attn_sinks": grads["attn_sinks"][:, 0],
        "rec_norm": _colsum("drn_sum", grads["rec_norm"]).reshape(REC_HEADS, REC_DIM).sum(axis=0),
        "loss": _colsum("loss_sum", loss_p),
    }
    for n in ("norm_ffn1", "norm_mix", "norm_ffn2", "norm_ple", "norm_final"):
        small_vals[n] = _colsum(n + "_sum", grads[n])
    red = _allreduce_small(_pack_small(small_vals))
    small_shapes = {n: wsh[n].shape for n in SMALL}
    small_shapes["loss"] = (D_MODEL,)
    small = _unpack_small(red, small_shapes)
    loss = 0.5 * jnp.sum(small["loss"]) / D_MODEL

    out_g, out_d, out_m, out_v = {}, {}, {}, {}
    for n, gm, gs in zip(BIG, s_mine, s_sib):
        res = _adamw_halves("adamw_" + n, wsh[n][0], args["m_" + n][0], args["v_" + n][0], gm, gs, cc)
        out_g[n], out_d[n], out_m[n], out_v[n] = (t[None] for t in res)
    sw = _pack_small({**{n: wsh[n] for n in SMALL}, "loss": jnp.zeros((D_MODEL,), F32)})
    sm = _pack_small({**{n: args["m_" + n] for n in SMALL}, "loss": jnp.zeros((D_MODEL,), F32)})
    sv = _pack_small({**{n: args["v_" + n] for n in SMALL}, "loss": jnp.ones((D_MODEL,), F32)})
    sd, snm, snv = _adamw("adamw_small", sw, red, sm, sv)
    ud, um, uv = (_unpack_small(t, small_shapes) for t in (sd, snm, snv))
    for n in SMALL:
        out_g[n], out_d[n], out_m[n], out_v[n] = small[n], ud[n], um[n], uv[n]

    return (loss, dx.reshape(B, S, D_MODEL), *[out_g[n] for n in WEIGHTS], *[out_d[n] for n in WEIGHTS],
            *[out_m[n] for n in WEIGHTS], *[out_v[n] for n in WEIGHTS])
```

```python
import numpy as np
import jax
import jax.numpy as jnp
from jax import lax
from jax.experimental import pallas as pl
from jax.experimental.pallas import tpu as pltpu

F32 = jnp.float32
BF16 = jnp.bfloat16
MESH = pl.DeviceIdType.MESH

D_MODEL = 1024
D_FF = 2816
FF_SHARD = 2 * D_FF // 4
HEAD_DIM = 64
N_Q_HEADS = 8
ATT_BLOCK = 128
N_BUCKETS = 32
MAX_DISTANCE = 128
REC_HEADS = 4
REC_DIM = 128
PLE_DIM = 256
EPS = 1e-6
IN_W = 4864
COL_AQ, COL_AK, COL_AV, COL_RQ, COL_RF, COL_RI, COL_RG, COL_GA, COL_GB = 0, 4, 5, 6, 10, 14, 18, 22, 30

CHUNK = 64
SUB = 8
N_SUB = CHUNK // SUB
HGRN_PAIR = 2

ADAM_LR, ADAM_B1, ADAM_B2, ADAM_EPS, ADAM_WD, ADAM_STEP = 0.001, 0.9, 0.999, 1e-08, 0.01, 10

V7X_VMEM_LIMIT = 56 * 1024 * 1024
N_CHIPS = 4
N_DEV = 8

BIG = ("w_ffn1_in", "w_ffn1_out", "w_in", "w_att_proj", "w_rec_proj", "w_out",
       "w_ffn2_in", "w_ffn2_out", "w_ple_gate", "w_ple_proj")
COL_SHARDED = ("w_ffn1_in", "w_in", "w_att_proj", "w_rec_proj", "w_ffn2_in", "w_ple_proj")
WEIGHTS = ("rel_bias", "lb_param", "norm_ffn1", "w_ffn1_in", "w_ffn1_out", "norm_mix", "w_in", "attn_sinks",
           "rec_norm", "w_att_proj", "w_rec_proj", "w_out", "norm_ffn2", "w_ffn2_in", "w_ffn2_out", "norm_ple",
           "w_ple_gate", "w_ple_proj", "norm_final")
SMALL = tuple(n for n in WEIGHTS if n not in BIG)
SMALL_ROWS = 64


def _params(*sem):
    return pltpu.CompilerParams(dimension_semantics=sem, vmem_limit_bytes=V7X_VMEM_LIMIT)


def _pick(n, cap, mult=8):
    if n <= cap:
        return n
    for t in range(cap - cap % mult, 0, -mult):
        if n % t == 0:
            return t
    raise ValueError((n, cap, mult))


def _dot(a, b):
    return jnp.dot(a, b, preferred_element_type=F32)


def _dot_nt(a, b):
    return lax.dot_general(a, b, (((1,), (1,)), ((), ())), preferred_element_type=F32)


def _dot_tn(a, b):
    return lax.dot_general(a, b, (((0,), (0,)), ((), ())), preferred_element_type=F32)


def _split3(x):
    hi = x.astype(BF16)
    r = x - hi.astype(F32)
    mid = r.astype(BF16)
    lo = (r - mid.astype(F32)).astype(BF16)
    return hi, mid, lo


def _split2(x):
    hi = x.astype(BF16)
    return hi, (x - hi.astype(F32)).astype(BF16)


def _sel_left(sel_bf16, x):
    hi, mid, lo = _split3(x)
    return _dot(sel_bf16, hi) + _dot(sel_bf16, mid) + _dot(sel_bf16, lo)


def _sel_right(x, sel_bf16):
    hi, mid, lo = _split3(x)
    return _dot(hi, sel_bf16) + _dot(mid, sel_bf16) + _dot(lo, sel_bf16)


def _sigmoid(x):
    return 0.5 * jnp.tanh(0.5 * x) + 0.5


def _group8(x):
    r, w = x.shape
    return x.reshape(r // 8, 8, w).sum(axis=0)


class _Comm:
    def __init__(self, ins, out_shapes, n_sems, start, finish):
        self.ins, self.out_shapes, self.n_sems, self.start, self.finish = ins, out_shapes, n_sems, start, finish


ANY = pl.BlockSpec(memory_space=pl.ANY)


def _comm_parts(comm):
    if comm is None:
        return [], [], [], []
    sems = [pltpu.SemaphoreType.DMA((comm.n_sems,)), pltpu.SemaphoreType.DMA((comm.n_sems,))]
    return list(comm.ins), [ANY] * len(comm.ins), list(comm.out_shapes), sems


def _comm_run(comm, grid, refs, n_in, n_out):
    if comm is None:
        return (lambda: None), (lambda: None)
    nci, nco = len(comm.ins), len(comm.out_shapes)
    cin = refs[n_in:n_in + nci]
    cout = refs[n_in + nci + n_out:n_in + nci + n_out + nco]
    send_sems, recv_sems = refs[-2], refs[-1]
    ids = [pl.program_id(d) for d in range(len(grid))]
    is_first = ids[0] == 0
    is_last = ids[0] == grid[0] - 1
    for d in range(1, len(grid)):
        is_first = is_first & (ids[d] == 0)
        is_last = is_last & (ids[d] == grid[d] - 1)

    def first():
        @pl.when(is_first)
        def _():
            comm.start(cin, cout, send_sems, recv_sems)

    def last():
        @pl.when(is_last)
        def _():
            comm.finish(cin, cout, send_sems, recv_sems)

    return first, last


def _call(name, fn, grid, ins, outs, pairs=(), comm=None, j_outer=False):
    in_pair = {i for p in pairs for i in p[:2]}
    n_in, n_out = len(ins), len(outs)
    c_arrays, c_in_specs, c_out_shapes, c_sems = _comm_parts(comm)

    def body(*refs):
        first, last = _comm_run(comm, grid, refs, n_in, n_out)
        first()
        accs = []
        for ia, ib, kind in pairs:
            a, b = refs[ia][...].astype(BF16), refs[ib][...].astype(BF16)
            accs.append(_dot(a, b) if kind == "nn" else _dot_nt(a, b))
        vals = [refs[i][...] for i in range(n_in) if i not in in_pair]
        res = fn(accs, vals)
        out_refs = refs[n_in + len(c_arrays):n_in + len(c_arrays) + n_out]
        assert len(res) == len(out_refs), (name, len(res), len(out_refs))
        for o_ref, val in zip(out_refs, res):
            o_ref[...] = val.astype(o_ref.dtype)
        last()

    if j_outer:
        grid = (grid[1], grid[0])
        swap = lambda im: (lambda j, i: im(i, j))
        ins = [(a, blk, swap(im)) for a, blk, im in ins]
        outs = [(shp, dt, blk, swap(im)) for shp, dt, blk, im in outs]

    return pl.pallas_call(
        body, name=name, grid=grid,
        in_specs=[pl.BlockSpec(blk, im) for _, blk, im in ins] + c_in_specs,
        out_specs=[pl.BlockSpec(blk, im) for _, _, blk, im in outs] + [ANY] * len(c_out_shapes),
        out_shape=[jax.ShapeDtypeStruct(shp, dt) for shp, dt, _, _ in outs] + c_out_shapes,
        scratch_shapes=c_sems,
        compiler_params=_params(*(["arbitrary"] * len(grid))))(*[a for a, _, _ in ins], *c_arrays)


def _tile_call(name, fn, M, N, tm, tn, *, pairs=(), tiles=(), consts=(), outs=(), parts=0, comm=None,
               j_outer=False):
    gi, gj = M // tm, N // tn
    assert gi * tm == M and gj * tn == N, (name, M, N, tm, tn)
    ins, prs = [], []
    for a, a_col, b, kind in pairs:
        K = b.shape[0] if kind == "nn" else b.shape[1]
        ins.append((a, (tm, K), lambda i, j, c=a_col: (i, c)))
        if kind == "nn":
            ins.append((b, (K, tn), lambda i, j: (0, j)))
        else:
            ins.append((b, (tn, K), lambda i, j: (j, 0)))
        prs.append((len(ins) - 2, len(ins) - 1, kind))
    for arr, off in tiles:
        ins.append((arr, (tm, tn), lambda i, j, o=off: (i, j + o)))
    for arr in consts:
        ins.append((arr, arr.shape, lambda i, j: (0, 0)))
    out_l = [((M, N), dt, (tm, tn), lambda i, j: (i, j)) for dt in outs]
    out_l += [((gi * 8, N), F32, (8, tn), lambda i, j: (i, j))] * parts
    nt = len(tiles)

    def wrapped(accs, vals):
        return fn(accs, vals[:nt], vals[nt:])

    return _call(name, wrapped, (gi, gj), ins, out_l, prs, comm=comm, j_outer=j_outer)


def _mm_tn(name, grid, a_in, b_in, outs):
    nk = grid[2]
    tm = [d for d in a_in[1] if d is not None][1]
    tn = [d for d in b_in[1] if d is not None][1]
    direct = outs[0][1] == F32 and tuple(d for d in outs[0][2] if d is not None) == (tm, tn)

    def body(a_ref, b_ref, *rest):
        out_refs = rest[:-1] if not direct else rest
        acc_ref = rest[0] if direct else rest[-1]
        k = pl.program_id(2)

        @pl.when(k == 0)
        def _():
            acc_ref[...] = jnp.zeros_like(acc_ref)

        acc_ref[...] += _dot_tn(a_ref[...].astype(BF16), b_ref[...].astype(BF16))

        @pl.when(k == nk - 1)
        def _():
            for o_ref in out_refs[1:] if direct else out_refs:
                if len(o_ref.shape) == 3:
                    n = o_ref.shape[2]
                    for s in range(o_ref.shape[0]):
                        o_ref[s] = acc_ref[:, n * s:n * (s + 1)].astype(o_ref.dtype)
                else:
                    o_ref[...] = acc_ref[...].astype(o_ref.dtype)

    return pl.pallas_call(
        body, name=name, grid=grid,
        in_specs=[pl.BlockSpec(a_in[1], a_in[2]), pl.BlockSpec(b_in[1], b_in[2])],
        out_specs=[pl.BlockSpec(blk, im) for _, _, blk, im in outs],
        out_shape=[jax.ShapeDtypeStruct(shp, dt) for shp, dt, _, _ in outs],
        scratch_shapes=[] if direct else [pltpu.VMEM((tm, tn), F32)],
        compiler_params=_params("arbitrary", "arbitrary", "arbitrary"))(a_in[0], b_in[0])


def _grad_pair(shape, block, imap):
    return [(shape, F32, block, imap), (shape, BF16, block, imap)]


def _mm_tn_rows(name, a, b, tk=2048):
    T, a_w = a.shape
    b_w = b.shape[1]
    tm = _pick(a_w, 1408, 128)
    tk = _pick(T, tk, 128)
    g32, g16 = _mm_tn(name, (a_w // tm, 1, T // tk),
                      (a, (tk, tm), lambda i, j, k: (k, i)), (b, (tk, b_w), lambda i, j, k: (k, 0)),
                      _grad_pair((a_w, b_w), (tm, b_w), lambda i, j, k: (i, 0)))
    shp = (N_CHIPS, a_w // N_CHIPS, b_w)
    return g32.reshape(shp), g16.reshape(shp)


def _mm_tn_cols(name, a, b, tk=4096):
    T, a_w = a.shape
    b_w = b.shape[1]
    tk = _pick(T, tk, 128)
    shp = (N_CHIPS, a_w, b_w // N_CHIPS)
    return _mm_tn(name, (1, 1, T // tk),
                  (a, (tk, a_w), lambda i, j, k: (k, 0)), (b, (tk, b_w), lambda i, j, k: (k, 0)),
                  _grad_pair(shp, shp, lambda i, j, k: (0, 0, 0)))


def _colsum(name, x):
    def body(x_ref, o_ref):
        o_ref[...] = jnp.sum(x_ref[...], axis=0, keepdims=True)
    return pl.pallas_call(body, name=name, out_shape=jax.ShapeDtypeStruct((1, x.shape[1]), F32))(x)


def _rms_hat(h):
    return h * lax.rsqrt(jnp.mean(h * h, axis=-1, keepdims=True) + EPS)


def _rms_bwd_vals(dn, h, g):
    r = lax.rsqrt(jnp.mean(h * h, axis=-1, keepdims=True) + EPS)
    nh = h * r
    gd = dn * g
    dh = r * (gd - nh * jnp.mean(gd * nh, axis=-1, keepdims=True))
    return dh, _group8(dn * nh)


def _rms_fwd(name, h, g, tm=512, comm=None):
    T = h.shape[0]

    def fn(accs, tv, cv):
        return [_rms_hat(tv[0]) * cv[0]]

    return _tile_call(name, fn, T, D_MODEL, _pick(T, tm), D_MODEL, tiles=[(h, 0)], consts=[g], outs=[BF16],
                      comm=comm)


def _ffn_fwd(tag, h, g, w_in, w_out, g_next, n=None, comm_norm=None, w_in_of=None, comm_in=None, comm_out=None,
             w_out_of=None):
    T = h.shape[0]
    if n is None:
        n, *got_norm = _rms_fwd(tag + "_norm", h, g, comm=comm_norm)
        if w_in_of is not None:
            w_in = w_in_of(got_norm)
    tm = _pick(T, 1024)
    wblk = (None, D_MODEL, FF_SHARD)

    def act(accs, vals):
        gate, up = accs
        return [gate, up, gate * _sigmoid(gate) * up]

    tile = lambda: ((T, D_FF), BF16, (tm, FF_SHARD), lambda i, j: (i, j))
    gate, up, a, *got_in = _call(
        tag + "_in", act, (T // tm, 2),
        [(n, (tm, D_MODEL), lambda i, j: (i, 0)),
         (w_in, wblk, lambda i, j: (j, 0, 0)), (w_in, wblk, lambda i, j: (j + 2, 0, 0))],
        [tile(), tile(), tile()], pairs=[(0, 1, "nn"), (0, 2, "nn")], comm=comm_in, j_outer=True)

    def res(accs, tv, cv):
        h_new = tv[0] + 0.5 * accs[0]
        return [h_new, _rms_hat(h_new) * cv[0]]

    if w_out_of is not None:
        w_out = w_out_of(got_in)
    h_new, n_next, *got_out = _tile_call(
        tag + "_out", res, T, D_MODEL, _pick(T, 512), D_MODEL, pairs=[(a, 0, w_out, "nn")], tiles=[(h, 0)],
        consts=[g_next], outs=[F32, BF16], comm=comm_out)
    return h_new, n_next, (n, gate, up, a), got_out


def _ffn_bwd(tag, dh_out, df, h, g, w_in, w_out, saved, comm=None, comm_last=None):
    T = h.shape[0]
    n, gate, up, a = saved
    tm = _pick(T, 512)

    def dact(accs, vals):
        da = accs[0]
        gt, u = vals[0].astype(F32), vals[1].astype(F32)
        sg = _sigmoid(gt)
        silu = gt * sg
        return [jnp.stack([(da * u * (sg + silu * (1.0 - sg))).astype(BF16), (da * silu).astype(BF16)])]

    dz, *got = _call(
        tag + "_dact", dact, (T // tm, 2),
        [(df, (tm, D_MODEL), lambda i, j: (i, 0)), (w_out, (FF_SHARD, D_MODEL), lambda i, j: (j, 0)),
         (gate, (tm, FF_SHARD), lambda i, j: (i, j)), (up, (tm, FF_SHARD), lambda i, j: (i, j))],
        [((2, T, D_FF), BF16, (2, tm, FF_SHARD), lambda i, j: (0, i, j))], pairs=[(0, 1, "nt")], comm=comm,
        j_outer=True)
    dw_out = _mm_tn_rows(tag + "_dwout", a, df)
    tk = _pick(T, 2048, 128)
    dw_in = _mm_tn(tag + "_dwin", (1, N_CHIPS, T // tk),
                   (n, (tk, D_MODEL), lambda i, j, k: (k, 0)),
                   (dz, (None, tk, FF_SHARD), lambda i, j, k: (j // 2, k, j % 2)),
                   _grad_pair((N_CHIPS, D_MODEL, FF_SHARD), (None, D_MODEL, FF_SHARD), lambda i, j, k: (j, 0, 0)))

    def dnorm(accs, vals):
        dn = accs[0] + accs[1] + accs[2] + accs[3]
        dh, dg = _rms_bwd_vals(dn, vals[0], vals[2])
        dh = vals[1] + dh
        return [dh, dh, dg]

    tm2 = _pick(T, 512)
    ins = [(dz, (None, tm2, FF_SHARD), lambda i, j, s=s: (s // 2, i, s % 2)) for s in range(N_CHIPS)]
    ins += [(w_in, (None, D_MODEL, FF_SHARD), lambda i, j, s=s: (s, 0, 0)) for s in range(N_CHIPS)]
    ins += [(h, (tm2, D_MODEL), lambda i, j: (i, 0)), (dh_out, (tm2, D_MODEL), lambda i, j: (i, 0)),
            (g, g.shape, lambda i, j: (0, 0))]
    dh, dh16, dg, *got_last = _call(
        tag + "_dnorm", dnorm, (T // tm2, 1), ins,
        [((T, D_MODEL), F32, (tm2, D_MODEL), lambda i, j: (i, 0)),
         ((T, D_MODEL), BF16, (tm2, D_MODEL), lambda i, j: (i, 0)),
         ((T // tm2 * 8, D_MODEL), F32, (8, D_MODEL), lambda i, j: (i, 0))],
        pairs=[(s, N_CHIPS + s, "nt") for s in range(N_CHIPS)],
        comm=None if comm_last is None else comm_last(dw_in, dw_out))
    return dh, dh16, dg, dw_in, dw_out, got, got_last


def _t5_onehot():
    qi = np.arange(ATT_BLOCK)[:, None] + ATT_BLOCK
    kj = np.arange(2 * ATT_BLOCK)[None, :]
    nn = np.maximum(qi - kj, 0)
    max_exact = N_BUCKETS // 2
    large = max_exact + (np.log(np.maximum(nn, 1) / max_exact) / np.log(MAX_DISTANCE / max_exact)
                         * (N_BUCKETS - max_exact)).astype(np.int32)
    large = np.minimum(large, N_BUCKETS - 1)
    bucket = np.where(nn < max_exact, nn, large).astype(np.int32).reshape(-1)
    return (bucket[None, :] == np.arange(N_BUCKETS)[:, None]).astype(np.float32)


def _small_mm(name, a, b, sel):
    def body(a_ref, b_ref, o_ref):
        if sel == "right":
            o_ref[...] = _sel_right(a_ref[...], b_ref[...])
        else:
            o_ref[...] = _sel_left(a_ref[...], b_ref[...])
    return pl.pallas_call(body, name=name, out_shape=jax.ShapeDtypeStruct((a.shape[0], b.shape[1]), F32),
                          compiler_params=pltpu.CompilerParams(vmem_limit_bytes=V7X_VMEM_LIMIT))(a, b)


def _dup_heads(t):
    a, b = t[:, :HEAD_DIM], t[:, HEAD_DIM:]
    return jnp.concatenate([a, a, b, b], axis=1)


def _kv_layouts(proj):
    T = proj.shape[0]

    def fn(accs, tv, cv):
        return [tv[0], tv[1]]

    k, v = _tile_call("kv_cast", fn, T, 128, _pick(T, 1024), 128, tiles=[(proj, COL_AK), (proj, COL_AV)],
                      outs=[BF16, BF16])
    return _dup_heads(k), _dup_heads(v)


def _swa_masks():
    row = lax.broadcasted_iota(jnp.int32, (ATT_BLOCK, 2 * ATT_BLOCK), 0)
    col = lax.broadcasted_iota(jnp.int32, (ATT_BLOCK, 2 * ATT_BLOCK), 1)
    dist = ATT_BLOCK + row - col
    return (dist >= 0) & (dist < ATT_BLOCK), col


GROUP = 4


def _stack_group(blk, lo_q):
    zero = jnp.zeros_like(blk[:, :128])
    rows = []
    for pair in range(GROUP // 2):
        pb = blk[:, 128 * pair:128 * (pair + 1)]
        rows += [jnp.where(lo_q, pb, zero), jnp.where(lo_q, zero, pb)]
    return jnp.concatenate(rows, axis=0)


def _unstack_group(st, lo_q):
    pairs = [jnp.where(lo_q, st[256 * pair:256 * pair + 128], st[256 * pair + 128:256 * (pair + 1)])
             for pair in range(GROUP // 2)]
    return jnp.concatenate(pairs, axis=1)


def _swa_probs(s, bias_h, sink, valid):
    s = jnp.where(valid, s * (HEAD_DIM ** -0.5) + bias_h, -jnp.inf)
    m = jnp.maximum(jnp.max(s, axis=-1, keepdims=True), sink)
    e = jnp.exp(s - m)
    es = jnp.exp(sink - m)
    den = jnp.sum(e, axis=-1, keepdims=True) + es
    return e / den, es / den


def _swa_fwd(proj, kk2, vv2, bias, sinks, B, S):
    T = B * S
    nb = S // ATT_BLOCK

    def body(q_ref, k_ref, v_ref, bias_ref, sink_ref, o_ref, kpad, vpad):
        zeros = jnp.zeros((ATT_BLOCK, 256), BF16)
        kpad[pl.ds(0, ATT_BLOCK), :] = zeros
        vpad[pl.ds(0, ATT_BLOCK), :] = zeros
        kpad[pl.ds(ATT_BLOCK, S), :] = k_ref[...]
        vpad[pl.ds(ATT_BLOCK, S), :] = v_ref[...]
        valid0, col = _swa_masks()
        lo_q = lax.broadcasted_iota(jnp.int32, (1, 128), 1) < HEAD_DIM

        def blk(n, carry):
            r0 = pl.multiple_of(n * ATT_BLOCK, ATT_BLOCK)
            rows = pl.ds(r0, ATT_BLOCK)
            valid = valid0 & ((n > 0) | (col >= ATT_BLOCK))
            for g in range(N_Q_HEADS // GROUP):
                lanes = pl.ds(128 * g, 128)
                kg = kpad[pl.ds(r0, 2 * ATT_BLOCK), lanes]
                vg = vpad[pl.ds(r0, 2 * ATT_BLOCK), lanes]
                qm = _stack_group(q_ref[rows, pl.ds(256 * g, 256)].astype(BF16), lo_q)
                s = _dot_nt(qm, kg)
                ps = []
                for i in range(GROUP):
                    h = GROUP * g + i
                    p, _ = _swa_probs(s[ATT_BLOCK * i:ATT_BLOCK * (i + 1)], bias_ref[h], sink_ref[h], valid)
                    ps.append(p.astype(BF16))
                o = _dot(jnp.concatenate(ps, axis=0), vg)
                o_ref[rows, pl.ds(256 * g, 256)] = _unstack_group(o, lo_q).astype(o_ref.dtype)
            return carry

        if nb % 2 == 0:
            lax.fori_loop(0, nb // 2, lambda i, c: blk(2 * i + 1, blk(2 * i, c)), 0)
        else:
            lax.fori_loop(0, nb, blk, 0)

    return pl.pallas_call(
        body, name="swa_fwd", grid=(B,),
        in_specs=[pl.BlockSpec((S, 512), lambda b: (b, 0)),
                  pl.BlockSpec((S, 256), lambda b: (b, 0)),
                  pl.BlockSpec((S, 256), lambda b: (b, 0)),
                  pl.BlockSpec((N_Q_HEADS, ATT_BLOCK, 2 * ATT_BLOCK), lambda b: (0, 0, 0)),
                  pl.BlockSpec(memory_space=pltpu.SMEM)],
        out_specs=pl.BlockSpec((S, 512), lambda b: (b, 0)),
        out_shape=jax.ShapeDtypeStruct((T, 512), BF16),
        scratch_shapes=[pltpu.VMEM((S + ATT_BLOCK, 256), BF16), pltpu.VMEM((S + ATT_BLOCK, 256), BF16)],
        compiler_params=_params("arbitrary"))(proj, kk2, vv2, bias, sinks)


def _swa_bwd(proj, kk2, vv2, bias, sinks, datt, B, S):
    T = B * S
    nb = S // ATT_BLOCK

    def body(q_ref, k_ref, v_ref, bias_ref, sink_ref, do_ref, dq_ref, dk_ref, dv_ref, dbias_ref, dsink_ref,
             kpad, vpad, dkpad, dvpad):
        b = pl.program_id(0)

        @pl.when(b == 0)
        def _():
            dbias_ref[...] = jnp.zeros_like(dbias_ref)
            dsink_ref[...] = jnp.zeros_like(dsink_ref)

        zeros = jnp.zeros((ATT_BLOCK, 256), BF16)
        kpad[pl.ds(0, ATT_BLOCK), :] = zeros
        vpad[pl.ds(0, ATT_BLOCK), :] = zeros
        kpad[pl.ds(ATT_BLOCK, S), :] = k_ref[...]
        vpad[pl.ds(ATT_BLOCK, S), :] = v_ref[...]
        dkpad[...] = jnp.zeros_like(dkpad)
        dvpad[...] = jnp.zeros_like(dvpad)
        valid0, col = _swa_masks()
        lo_q = lax.broadcasted_iota(jnp.int32, (1, 128), 1) < HEAD_DIM
        scale = HEAD_DIM ** -0.5

        def blk(n, carry):
            r0 = pl.multiple_of(n * ATT_BLOCK, ATT_BLOCK)
            rows = pl.ds(r0, ATT_BLOCK)
            band = pl.ds(r0, 2 * ATT_BLOCK)
            valid = valid0 & ((n > 0) | (col >= ATT_BLOCK))
            for g in range(N_Q_HEADS // GROUP):
                lanes = pl.ds(128 * g, 128)
                kg = kpad[band, lanes]
                vg = vpad[band, lanes]
                qm = _stack_group(q_ref[rows, pl.ds(256 * g, 256)].astype(BF16), lo_q)
                dom = _stack_group(do_ref[rows, pl.ds(256 * g, 256)], lo_q)
                s = _dot_nt(qm, kg)
                dp = _dot_nt(dom, vg)
                pst, dst = [], []
                for i in range(GROUP):
                    h = GROUP * g + i
                    sl = slice(ATT_BLOCK * i, ATT_BLOCK * (i + 1))
                    p, ps = _swa_probs(s[sl], bias_ref[h], sink_ref[h], valid)
                    delta = jnp.sum(p * dp[sl], axis=-1, keepdims=True)
                    ds = p * (dp[sl] - delta)
                    dbias_ref[h] += ds
                    dsink_ref[pl.ds(h, 1), :] += -jnp.sum(jnp.broadcast_to(ps * delta, (ATT_BLOCK, 128)),
                                                          axis=0, keepdims=True)
                    pst.append(p.astype(BF16))
                    dst.append((ds * scale).astype(BF16))
                pst, dst = jnp.concatenate(pst, axis=0), jnp.concatenate(dst, axis=0)
                dq_ref[rows, pl.ds(256 * g, 256)] = _unstack_group(_dot(dst, kg), lo_q).astype(dq_ref.dtype)
                dkpad[band, lanes] += _dot_tn(dst, qm)
                dvpad[band, lanes] += _dot_tn(pst, dom)
            return carry

        if nb % 2 == 0:
            lax.fori_loop(0, nb // 2, lambda i, c: blk(2 * i + 1, blk(2 * i, c)), 0)
        else:
            lax.fori_loop(0, nb, blk, 0)
        lo_out = lax.broadcasted_iota(jnp.int32, (1, 128), 1) < HEAD_DIM

        def fold(pad_ref):
            halves = []
            for g in range(N_Q_HEADS // GROUP):
                t = pad_ref[pl.ds(ATT_BLOCK, S), pl.ds(128 * g, 128)]
                halves.append(t + pltpu.roll(t, HEAD_DIM, 1))
            return jnp.where(lo_out, halves[0], halves[1])

        dk_ref[...] = fold(dkpad).astype(dk_ref.dtype)
        dv_ref[...] = fold(dvpad).astype(dv_ref.dtype)

    return pl.pallas_call(
        body, name="swa_bwd", grid=(B,),
        in_specs=[pl.BlockSpec((S, 512), lambda b: (b, 0)),
                  pl.BlockSpec((S, 256), lambda b: (b, 0)),
                  pl.BlockSpec((S, 256), lambda b: (b, 0)),
                  pl.BlockSpec((N_Q_HEADS, ATT_BLOCK, 2 * ATT_BLOCK), lambda b: (0, 0, 0)),
                  pl.BlockSpec(memory_space=pltpu.SMEM),
                  pl.BlockSpec((S, 512), lambda b: (b, 0))],
        out_specs=[pl.BlockSpec((S, 512), lambda b: (b, 0)),
                   pl.BlockSpec((S, 128), lambda b: (b, 0)),
                   pl.BlockSpec((S, 128), lambda b: (b, 0)),
                   pl.BlockSpec((N_Q_HEADS, ATT_BLOCK, 2 * ATT_BLOCK), lambda b: (0, 0, 0)),
                   pl.BlockSpec((N_Q_HEADS, 128), lambda b: (0, 0))],
        out_shape=[jax.ShapeDtypeStruct((T, 512), BF16),
                   jax.ShapeDtypeStruct((T, 128), BF16),
                   jax.ShapeDtypeStruct((T, 128), BF16),
                   jax.ShapeDtypeStruct((N_Q_HEADS, ATT_BLOCK, 2 * ATT_BLOCK), F32),
                   jax.ShapeDtypeStruct((N_Q_HEADS, 128), F32)],
        scratch_shapes=[pltpu.VMEM((S + ATT_BLOCK, 256), BF16), pltpu.VMEM((S + ATT_BLOCK, 256), BF16),
                        pltpu.VMEM((S + ATT_BLOCK, 256), F32), pltpu.VMEM((S + ATT_BLOCK, 256), F32)],
        compiler_params=_params("arbitrary"))(proj, kk2, vv2, bias, sinks, datt)


def _hgrn_gates(z, lb):
    sg = _sigmoid(z)
    f = lb + (1.0 - lb) * sg
    return sg, f, jnp.log(f), 1.0 - f


def _hgrn_consts():
    r = lax.broadcasted_iota(jnp.int32, (CHUNK, CHUNK), 0)
    c = lax.broadcasted_iota(jnp.int32, (CHUNK, CHUNK), 1)
    tril = (r >= c).astype(BF16)
    triu = (r <= c).astype(BF16)
    causal = r >= c
    below = (r // SUB) > (c // SUB)
    inside = ((r // SUB) == (c // SUB)) & causal
    return tril, triu, causal, below, inside, c


def _block_rows(ref, lanes, s):
    rows = []
    for i in range(N_SUB):
        if SUB * i + s < 0:
            rows.append(jnp.zeros((SUB, REC_DIM), F32))
        else:
            rows.append(jnp.broadcast_to(ref[pl.ds(SUB * i + s, 1), lanes], (SUB, REC_DIM)))
    return jnp.concatenate(rows, axis=0)


def _hgrn_offdiag(q, k, bcum, b_ref, lanes):
    eq = jnp.exp(jnp.minimum(bcum - _block_rows(b_ref, lanes, -1), 0.0))
    qe = q * eq
    zero = jnp.zeros((SUB, REC_DIM), F32)
    q_rows, k_cols, eks = [jnp.zeros((SUB, (N_SUB - 1) * REC_DIM), F32)], [], []
    for i in range(1, N_SUB):
        q_rows.append(jnp.concatenate([zero] * (i - 1) + [qe[SUB * i:SUB * (i + 1), :]] + [zero] * (N_SUB - 1 - i),
                                      axis=1))
        p = b_ref[pl.ds(SUB * i - 1, 1), lanes]
        pad = jnp.zeros((CHUNK - SUB * i, REC_DIM), F32)
        ek = jnp.concatenate([jnp.exp(p - b_ref[pl.ds(0, SUB * i), lanes]), pad], axis=0)
        k_cols.append(k * ek)
        eks.append(ek)
    return jnp.concatenate(q_rows, axis=0), jnp.concatenate(k_cols, axis=1), eq, eks


def _hgrn_fwd(proj, lb_param, B, S, comm=None):
    T = B * S
    nc = S // CHUNK
    fwd_unroll = 4 if nc % 4 == 0 else 2
    c_arrays, c_in_specs, c_out_shapes, c_sems = _comm_parts(comm)
    nci, nco = len(c_arrays), len(c_out_shapes)

    def body(*refs):
        q_ref, z_ref, v_ref, lb_ref = refs[:4]
        o_ref, st_ref = refs[4 + nci:6 + nci]
        k_slots, b_slots = refs[6 + nci + nco:8 + nci + nco]
        comm_first, comm_last = _comm_run(comm, (B, REC_HEADS // HGRN_PAIR), refs, 4, 2)
        comm_first()
        tril, _, _, below, inside, col = _hgrn_consts()
        col_s = col & (SUB - 1)

        def chunk(ci, hts, slot):
            k_s, b_s = k_slots.at[slot], b_slots.at[slot]
            r0 = pl.multiple_of(ci * CHUNK, CHUNK)
            lb = _sigmoid(lb_ref[0:1, :] - lb_ref[1:2, :])
            _, _, g_all, k_all = _hgrn_gates(z_ref[pl.ds(r0, CHUNK), :], lb)
            b_all = _sel_left(tril, g_all)
            k_s[...] = k_all
            b_s[...] = b_all
            new = []
            for e, ht in enumerate(hts):
                lanes = pl.ds(REC_DIM * e, REC_DIM)
                cols = slice(REC_DIM * e, REC_DIM * (e + 1))
                q = q_ref[pl.ds(r0, CHUNK), lanes]
                v = v_ref[pl.ds(r0, CHUNK), lanes]
                k, bcum = k_all[:, cols], b_all[:, cols]
                st_ref[e * nc + ci] = ht
                qst, kst, _, _ = _hgrn_offdiag(q, k, bcum, b_s, lanes)
                d = jnp.zeros((CHUNK, CHUNK), F32)
                for s in range(SUB):
                    w = jnp.exp(jnp.minimum(bcum - _block_rows(b_s, lanes, s), 0.0))
                    colv = jnp.sum(q * _block_rows(k_s, lanes, s) * w, axis=-1, keepdims=True)
                    d = jnp.where(col_s == s, colv, d)
                a = jnp.where(below, _dot_nt(qst.astype(BF16), kst.astype(BF16)), 0.0) + jnp.where(inside, d, 0.0)
                vb = v.astype(BF16)
                qb = (q * jnp.exp(bcum)).astype(BF16)
                o_ref[pl.ds(r0, CHUNK), lanes] = _dot(a.astype(BF16), vb) + _dot_nt(qb, ht.astype(BF16))
                b_last = b_s[pl.ds(CHUNK - 1, 1), lanes]
                kb = (k * jnp.exp(b_last - bcum)).astype(BF16)
                new.append(ht * jnp.exp(b_last) + _dot_tn(vb, kb))
            return tuple(new)

        def trip(i, hts):
            for u in range(fwd_unroll):
                hts = chunk(fwd_unroll * i + u, hts, u)
            return hts

        lax.fori_loop(0, nc // fwd_unroll, trip, tuple(jnp.zeros((REC_DIM, REC_DIM), F32) for _ in range(HGRN_PAIR)))
        comm_last()

    hp, wd = REC_HEADS // HGRN_PAIR, HGRN_PAIR * REC_DIM
    cq, cf, ci_ = (c * REC_DIM // wd for c in (COL_RQ, COL_RF, COL_RI))
    return pl.pallas_call(
        body, name="hgrn_fwd", grid=(B, hp),
        in_specs=[pl.BlockSpec((S, wd), lambda b, h: (b, cq + h)),
                  pl.BlockSpec((S, wd), lambda b, h: (b, cf + h)),
                  pl.BlockSpec((S, wd), lambda b, h: (b, ci_ + h)),
                  pl.BlockSpec((2, wd), lambda b, h: (0, h))] + c_in_specs,
        out_specs=[pl.BlockSpec((S, wd), lambda b, h: (b, h)),
                   pl.BlockSpec((HGRN_PAIR * nc, REC_DIM, REC_DIM), lambda b, h: (b * hp + h, 0, 0))] + [ANY] * nco,
        out_shape=[jax.ShapeDtypeStruct((T, 512), F32),
                   jax.ShapeDtypeStruct((B * REC_HEADS * nc, REC_DIM, REC_DIM), F32)] + c_out_shapes,
        scratch_shapes=[pltpu.VMEM((fwd_unroll, CHUNK, wd), F32), pltpu.VMEM((fwd_unroll, CHUNK, wd), F32)] + c_sems,
        compiler_params=_params("arbitrary", "arbitrary"))(proj, proj, proj, lb_param, *c_arrays)


def _hgrn_bwd(proj, lb_param, states, do, B, S, comm=None):
    T = B * S
    nc = S // CHUNK
    bwd_unroll = 4 if nc % 4 == 0 else 2
    c_arrays, c_in_specs, c_out_shapes, c_sems = _comm_parts(comm)
    nci, nco = len(c_arrays), len(c_out_shapes)

    def body(*refs):
        q_ref, z_ref, v_ref, lb_ref, st_ref, do_ref = refs[:6]
        dq_ref, dz_ref, dv_ref, dlb_ref = refs[6 + nci:10 + nci]
        slots = refs[10 + nci + nco:14 + nci + nco]
        comm_first, comm_last = _comm_run(comm, (B, REC_HEADS // HGRN_PAIR), refs, 6, 4)
        comm_first()
        tril, triu, causal, below, inside, col = _hgrn_consts()
        col_s = col & (SUB - 1)
        last_row = lax.broadcasted_iota(jnp.int32, (CHUNK, 1), 0) == CHUNK - 1
        rc = lax.broadcasted_iota(jnp.int32, (CHUNK, SUB * REC_DIM), 0)
        lc = lax.broadcasted_iota(jnp.int32, (CHUNK, SUB * REC_DIM), 1)
        spread = ((rc & (SUB - 1)) == (lc // REC_DIM)).astype(BF16)
        rr = lax.broadcasted_iota(jnp.int32, (CHUNK, SUB * CHUNK), 0)
        cc = lax.broadcasted_iota(jnp.int32, (CHUNK, SUB * CHUNK), 1)
        gather = (((rr // SUB) == ((cc & (CHUNK - 1)) // SUB)) & ((rr & (SUB - 1)) == (cc // CHUNK))).astype(BF16)

        heads = range(HGRN_PAIR)
        cols = [slice(REC_DIM * e, REC_DIM * (e + 1)) for e in heads]
        lanes = [pl.ds(REC_DIM * e, REC_DIM) for e in heads]
        lane_cat = lambda vals: jnp.concatenate(vals, axis=1)
        row_cat = lambda vals: jnp.concatenate(vals, axis=0)

        def chunk(it, carry, slot):
            k_s, b_s, pc_hi, pc_lo = (r.at[slot] for r in slots)
            dhts, dlb = carry
            ci = nc - 1 - it
            r0 = pl.multiple_of(ci * CHUNK, CHUNK)
            rows = pl.ds(r0, CHUNK)
            lb = _sigmoid(lb_ref[0:1, :] - lb_ref[1:2, :])
            sg, f, g_all, k_all = _hgrn_gates(z_ref[rows, :], lb)
            b_all = _sel_left(tril, g_all)
            k_s[...] = k_all
            b_s[...] = b_all
            q_all = q_ref[rows, :]
            das, hd = [], []
            for e in heads:
                vb, dob = v_ref[rows, lanes[e]].astype(BF16), do_ref[rows, lanes[e]].astype(BF16)
                da = jnp.where(causal, _dot_nt(dob, vb), 0.0)
                das.append(jnp.where(inside, da, 0.0))
                hd.append((vb, dob, da))
            da_hi, da_lo = _split2(row_cat(das))
            da_in = _dot(da_hi, spread) + _dot(da_lo, spread)
            ds, dqs = [], []
            for e in heads:
                q, bcum = q_all[:, cols[e]], b_all[:, cols[e]]
                d = jnp.zeros((CHUNK, CHUNK), F32)
                dq = jnp.zeros((CHUNK, REC_DIM), F32)
                for s in range(SUB):
                    w = jnp.exp(jnp.minimum(bcum - _block_rows(b_s, lanes[e], s), 0.0))
                    ks = _block_rows(k_s, lanes[e], s)
                    qw = q * w
                    d = jnp.where(col_s == s, jnp.sum(qw * ks, axis=-1, keepdims=True), d)
                    da_s = da_in[CHUNK * e:CHUNK * (e + 1), REC_DIM * s:REC_DIM * (s + 1)]
                    dq = dq + da_s * ks * w
                    hi, lo = _split2(da_s * qw)
                    pc_hi[pl.ds(CHUNK * s, CHUNK), lanes[e]] = hi
                    pc_lo[pl.ds(CHUNK * s, CHUNK), lanes[e]] = lo
                ds.append(d)
                dqs.append(dq)
            dk_in = _dot(gather, pc_hi[...]) + _dot(gather, pc_lo[...])
            dq_out, dk_out, dv_out, db_out, new_dhts = [], [], [], [], []
            for e in heads:
                q, k, bcum = q_all[:, cols[e]], k_all[:, cols[e]], b_all[:, cols[e]]
                vb, dob, da = hd[e]
                dht, ht = dhts[e], st_ref[e * nc + ci]
                qst, kst, eq, eks = _hgrn_offdiag(q, k, bcum, b_s, lanes[e])
                qst_b, kst_b = qst.astype(BF16), kst.astype(BF16)
                a = jnp.where(below, _dot_nt(qst_b, kst_b), 0.0) + jnp.where(inside, ds[e], 0.0)
                da_off = jnp.where(below, da, 0.0).astype(BF16)
                dqst = _dot(da_off, kst_b)
                dkst = _dot_tn(da_off, qst_b)
                dk = dk_in[:, cols[e]]
                dq_rows = [jnp.zeros((SUB, REC_DIM), F32)]
                for i in range(1, N_SUB):
                    dq_rows.append(dqst[SUB * i:SUB * (i + 1), REC_DIM * (i - 1):REC_DIM * i])
                    dk = dk + dkst[:, REC_DIM * (i - 1):REC_DIM * i] * eks[i - 1]
                dq = dqs[e] + row_cat(dq_rows) * eq
                eb = jnp.exp(bcum)
                b_last = b_s[pl.ds(CHUNK - 1, 1), lanes[e]]
                el = jnp.exp(b_last)
                ekb = jnp.exp(b_last - bcum)
                qb = (q * eb).astype(BF16)
                kb = k * ekb
                dhb = dht.astype(BF16)
                dv_out.append(_dot_tn(a.astype(BF16), dob) + _dot_nt(kb.astype(BF16), dhb))
                dqb = _dot(dob, ht.astype(BF16))
                dkb = _dot(vb, dhb)
                new_dhts.append(dht * el + _dot_tn(dob, qb))
                dq = dq + eb * dqb
                dk = dk + ekb * dkb
                edge = jnp.sum(kb * dkb, axis=0, keepdims=True) + el * jnp.sum(ht * dht, axis=0, keepdims=True)
                db_out.append(q * dq - k * dk + jnp.where(last_row, edge, 0.0))
                dq_out.append(dq)
                dk_out.append(dk)
            dk_all = lane_cat(dk_out)
            db_hi, db_lo = _split2(lane_cat(db_out))
            dg = _dot(triu, db_hi) + _dot(triu, db_lo)
            df = dg / f - dk_all
            dz_ref[rows, :] = (df * (1.0 - lb) * sg * (1.0 - sg)).astype(dz_ref.dtype)
            dq_ref[rows, :] = lane_cat(dq_out).astype(dq_ref.dtype)
            dv_ref[rows, :] = lane_cat(dv_out).astype(dv_ref.dtype)
            return tuple(new_dhts), dlb + jnp.sum(df * (1.0 - sg), axis=0, keepdims=True)

        zero = (tuple(jnp.zeros((REC_DIM, REC_DIM), F32) for _ in heads), jnp.zeros((1, HGRN_PAIR * REC_DIM), F32))
        def trip(i, carry):
            for u in range(bwd_unroll):
                carry = chunk(bwd_unroll * i + u, carry, u)
            return carry

        _, dlb = lax.fori_loop(0, nc // bwd_unroll, trip, zero)
        lb = _sigmoid(lb_ref[0:1, :] - lb_ref[1:2, :])
        dlb_ref[...] = jnp.broadcast_to(dlb * lb * (1.0 - lb), (8, HGRN_PAIR * REC_DIM))
        comm_last()

    hp, wd = REC_HEADS // HGRN_PAIR, HGRN_PAIR * REC_DIM
    cq, cf, ci_ = (c * REC_DIM // wd for c in (COL_RQ, COL_RF, COL_RI))
    return pl.pallas_call(
        body, name="hgrn_bwd", grid=(B, hp),
        in_specs=[pl.BlockSpec((S, wd), lambda b, h: (b, cq + h)),
                  pl.BlockSpec((S, wd), lambda b, h: (b, cf + h)),
                  pl.BlockSpec((S, wd), lambda b, h: (b, ci_ + h)),
                  pl.BlockSpec((2, wd), lambda b, h: (0, h)),
                  pl.BlockSpec((HGRN_PAIR * nc, REC_DIM, REC_DIM), lambda b, h: (b * hp + h, 0, 0)),
                  pl.BlockSpec((S, wd), lambda b, h: (b, h))] + c_in_specs,
        out_specs=[pl.BlockSpec((S, wd), lambda b, h: (b, h))] * 3
        + [pl.BlockSpec((8, wd), lambda b, h: (b, h))] + [ANY] * nco,
        out_shape=[jax.ShapeDtypeStruct((T, 512), BF16)] * 3 + [jax.ShapeDtypeStruct((B * 8, 512), F32)]
        + c_out_shapes,
        scratch_shapes=[pltpu.VMEM((bwd_unroll, CHUNK, wd), F32)] * 2
        + [pltpu.VMEM((bwd_unroll, SUB * CHUNK, wd), BF16)] * 2 + c_sems,
        compiler_params=_params("arbitrary", "arbitrary"))(proj, proj, proj, lb_param, states, do, *c_arrays)


def _rec_gate_fwd(rec, proj, rec_norm):
    T = rec.shape[0]

    def fn(accs, tv, cv):
        return [_rms_hat(tv[0]) * cv[0] * _sigmoid(tv[1])]

    return _tile_call("rec_gate", fn, T, 512, _pick(T, 1024), REC_DIM, tiles=[(rec, 0), (proj, COL_RG)],
                      consts=[rec_norm], outs=[BF16])[0]


def _rec_gate_bwd(dyb, w_rec_proj, rec, proj, rec_norm):
    T = rec.shape[0]

    def fn(accs, tv, cv):
        d, r, rg = accs[0], tv[0], tv[1]
        sg = _sigmoid(rg)
        rn = _rms_hat(r) * cv[0]
        dh, dg = _rms_bwd_vals(d * sg, r, cv[0])
        return [dh, d * rn * sg * (1.0 - sg), dg]

    return _tile_call("rec_gate_bwd", fn, T, 512, _pick(T, 1024), REC_DIM, pairs=[(dyb, 0, w_rec_proj, "nt")],
                      tiles=[(rec, 0), (proj, COL_RG)], consts=[rec_norm], outs=[F32, BF16], parts=1)


def _mix_out_fwd(att, recn, proj, w_att_proj, w_rec_proj, w_out, h1, g_next):
    T = att.shape[0]
    tn = 256

    def merge(accs, tv, cv):
        ya, yb = accs
        return [ya, yb, _sigmoid(tv[0]) * ya + _sigmoid(tv[1]) * yb]

    ya, yb, merged = _tile_call(
        "merge", merge, T, D_MODEL, _pick(T, 1024), tn,
        pairs=[(att, 0, w_att_proj, "nn"), (recn, 0, w_rec_proj, "nn")],
        tiles=[(proj, COL_GA * 128 // tn), (proj, COL_GB * 128 // tn)], outs=[BF16] * 3)

    def res(accs, tv, cv):
        h2 = tv[0] + accs[0]
        return [h2, _rms_hat(h2) * cv[0]]

    h2, n2 = _tile_call("mix_out", res, T, D_MODEL, _pick(T, 1024), D_MODEL, pairs=[(merged, 0, w_out, "nn")],
                        tiles=[(h1, 0)], consts=[g_next], outs=[F32, BF16])
    return h2, n2, (ya, yb, merged)


GATHER_FIRST = ("w_ffn1_in",)
GATHER_MIX = ("w_ffn1_out", "w_in")
GATHER_PROJ = ("w_att_proj", "w_rec_proj", "w_out")
GATHER_TAIL = ("w_ffn2_out", "w_ple_gate", "w_ple_proj")
GATHER_LAST = ("w_ffn2_in",)
SCATTER_LATE = ("w_ple_gate", "w_ple_proj", "w_ffn2_in", "w_ffn2_out")
SCATTER_MIX = ("w_out", "w_att_proj", "w_rec_proj", "w_in")
SCATTER_LAST = ("w_ffn1_in", "w_ffn1_out")


def _local_step(x, p, tgt, w, mine16, cc, me_chip, B, S):
    T = B * S
    w = dict(w)
    g_ffn1, g_mix, g_ffn2, g_ple = w["norm_ffn1"], w["norm_mix"], w["norm_ffn2"], w["norm_ple"]
    g_fin = w["norm_final"].reshape(1, D_MODEL)
    grads, part, from_chips = {}, {}, {}

    def gather(names):
        return _gather_comm([mine16[n] for n in names])

    def place(names, got):
        for n, g in zip(names, got):
            full = lax.dynamic_update_index_in_dim(g, mine16[n], me_chip, 0)
            w[n] = full if n in ("w_ffn1_in", "w_ffn2_in") else _natural(n, full)

    def swap(names):
        return _swap_comm([grads[n][1] for n in names])

    def after_swap(names, from_sib):
        for n, fs in zip(names, from_sib):
            part[n] = _add_sibling("rs_add_sib_" + n, grads[n][0], fs, cc)
        return _scatter_comm([part[n][1] for n in names])

    def scattered(names, got):
        for n, g in zip(names, got):
            from_chips[n] = g

    def ffn1_in_weight(got):
        place(GATHER_FIRST, got)
        return w["w_ffn1_in"]

    def ffn1_out_weight(got):
        place(GATHER_MIX, got)
        return w["w_ffn1_out"]

    h1, u, sv1, got_proj = _ffn_fwd("ffn1", x, g_ffn1, None, None, g_mix, comm_norm=gather(GATHER_FIRST),
                                    w_in_of=ffn1_in_weight, comm_in=gather(GATHER_MIX),
                                    comm_out=gather(GATHER_PROJ), w_out_of=ffn1_out_weight)
    place(GATHER_PROJ, got_proj)

    def ident(accs, tv, cv):
        return [accs[0]]

    proj, *got_tail = _tile_call("in_proj", ident, T, IN_W, _pick(T, 1024), IN_W // 2,
                                 pairs=[(u, 0, w["w_in"], "nn")], outs=[F32], j_outer=True, comm=gather(GATHER_TAIL))
    place(GATHER_TAIL, got_tail)
    onehot = jnp.asarray(_t5_onehot())
    bias = _small_mm("t5_bias", w["rel_bias"].T, onehot.astype(BF16), "right")
    bias = bias.reshape(N_Q_HEADS, ATT_BLOCK, 2 * ATT_BLOCK)
    sinks = w["attn_sinks"].reshape(N_Q_HEADS)
    kk2, vv2 = _kv_layouts(proj)
    att = _swa_fwd(proj, kk2, vv2, bias, sinks, B, S)
    rec, states, *got_last = _hgrn_fwd(proj, w["lb_param"], B, S, comm=gather(GATHER_LAST))
    place(GATHER_LAST, got_last)
    recn = _rec_gate_fwd(rec, proj, w["rec_norm"])
    h2, n2, (ya, yb, merged) = _mix_out_fwd(att, recn, proj, w["w_att_proj"], w["w_rec_proj"], w["w_out"], h1,
                                            g_ffn2)
    h3, n3, sv2, _ = _ffn_fwd("ffn2", h2, g_ffn2, w["w_ffn2_in"], w["w_ffn2_out"], g_ple, n=n2)

    def ple(accs, tv, cv):
        gate = _sigmoid(accs[0])
        return [gate, accs[1], tv[0] + gate * accs[1]]

    gate_p, pp, h4 = _tile_call(
        "ple", ple, T, D_MODEL, _pick(T, 1024), D_MODEL,
        pairs=[(n3, 0, w["w_ple_gate"], "nn"), (p, 0, w["w_ple_proj"], "nn")], tiles=[(h3, 0)],
        outs=[BF16, BF16, F32])

    def head(accs, tv, cv):
        h, t, gt, ppv = tv[0], tv[1], tv[2].astype(F32), tv[3].astype(F32)
        err = _rms_hat(h) * cv[0] - t
        dh, dg = _rms_bwd_vals(err * (1.0 / D_MODEL), h, cv[0])
        return [dh, dh * ppv * gt * (1.0 - gt), dh * gt, _group8(err * err), dg]

    dh4, dzg, dpp, loss_p, dg_fin = _tile_call(
        "loss_head", head, T, D_MODEL, _pick(T, 512), D_MODEL,
        tiles=[(h4, 0), (tgt, 0), (gate_p, 0), (pp, 0)], consts=[g_fin], outs=[F32, BF16, BF16], parts=2)
    grads["norm_final"] = dg_fin

    grads["w_ple_gate"] = _mm_tn_rows("ple_dwg", n3, dzg)
    grads["w_ple_proj"] = _mm_tn_cols("ple_dwp", p, dpp)

    def dnorm(accs, tv, cv):
        dh, dg = _rms_bwd_vals(accs[0], tv[0], cv[0])
        dh = tv[1] + dh
        return [dh, 0.5 * dh, dg]

    dh3, df3, grads["norm_ple"] = _tile_call(
        "ple_dnorm", dnorm, T, D_MODEL, _pick(T, 512), D_MODEL, pairs=[(dzg, 0, w["w_ple_gate"], "nt")],
        tiles=[(h3, 0), (dh4, 0)], consts=[g_ple], outs=[F32, BF16], parts=1)

    def swap_late(dw_in, dw_out):
        grads["w_ffn2_in"], grads["w_ffn2_out"] = dw_in, dw_out
        return swap(SCATTER_LATE)

    dh2, dh2b, grads["norm_ffn2"], _, _, _, from_sib = _ffn_bwd(
        "ffn2b", dh3, df3, h2, g_ffn2, w["w_ffn2_in"], w["w_ffn2_out"], sv2, comm_last=swap_late)
    scatter_late = after_swap(SCATTER_LATE, from_sib)

    grads["w_out"] = _mm_tn_rows("mix_dwout", merged, dh2b)
    tn = 256

    def dmerge(accs, tv, cv):
        dm = accs[0]
        sa, sb = _sigmoid(tv[0]), _sigmoid(tv[1])
        yav, ybv = tv[2].astype(F32), tv[3].astype(F32)
        return [dm * sa, dm * sb, dm * yav * sa * (1.0 - sa), dm * ybv * sb * (1.0 - sb)]

    dya, dyb, dga, dgb = _tile_call(
        "mix_dmerge", dmerge, T, D_MODEL, _pick(T, 1024), tn, pairs=[(dh2b, 0, w["w_out"], "nt")],
        tiles=[(proj, COL_GA * 128 // tn), (proj, COL_GB * 128 // tn), (ya, 0), (yb, 0)], outs=[BF16] * 4)
    grads["w_att_proj"] = _mm_tn_cols("mix_dwatt", att, dya)
    grads["w_rec_proj"] = _mm_tn_cols("mix_dwrec", recn, dyb)

    datt = _tile_call("mix_datt", ident, T, 512, _pick(T, 1024), 512, pairs=[(dya, 0, w["w_att_proj"], "nt")],
                      outs=[BF16])[0]
    drec, drg, grads["rec_norm"] = _rec_gate_bwd(dyb, w["w_rec_proj"], rec, proj, w["rec_norm"])

    drq, drf, dri, dlb, *got = _hgrn_bwd(proj, w["lb_param"], states, drec, B, S, comm=scatter_late)
    scattered(SCATTER_LATE, got)
    grads["lb_param"] = dlb
    daq, dak, dav, dbias, dsink = _swa_bwd(proj, kk2, vv2, bias, sinks, datt, B, S)
    grads["attn_sinks"] = dsink
    grads["rel_bias"] = _small_mm("t5_dbias", dbias.reshape(N_Q_HEADS, -1), onehot.T.astype(BF16), "right")
    dproj = jnp.concatenate([daq, dak, dav, drq, drf, dri, drg, dga, dgb], axis=1)
    tk = _pick(T, 2048, 128)
    w_in_shard = IN_W // N_CHIPS
    half_d = D_MODEL // 2
    gw32, gw16 = _mm_tn("mix_dwin", (2, 2, T // tk),
                        (u, (tk, half_d), lambda i, j, k: (k, i)), (dproj, (tk, IN_W // 2), lambda i, j, k: (k, j)),
                        _grad_pair((D_MODEL, IN_W), (half_d, IN_W // 2), lambda i, j, k: (i, j)))
    to_sh = lambda t: t.reshape(D_MODEL, N_CHIPS, w_in_shard).transpose(1, 0, 2)
    grads["w_in"] = (to_sh(gw32), to_sh(gw16))

    def dnorm_mix(accs, tv, cv):
        dh, dg = _rms_bwd_vals(accs[0], tv[0], cv[0])
        dh = tv[1] + dh
        return [dh, 0.5 * dh, dg]

    dh1, df1, grads["norm_mix"], *from_sib = _tile_call(
        "mix_dnorm", dnorm_mix, T, D_MODEL, _pick(T, 512), D_MODEL, pairs=[(dproj, 0, w["w_in"], "nt")],
        tiles=[(h1, 0), (dh2, 0)], consts=[g_mix], outs=[F32, BF16], parts=1, comm=swap(SCATTER_MIX))
    scatter_mix = after_swap(SCATTER_MIX, from_sib)

    def scatter_last(dw_in, dw_out):
        grads["w_ffn1_in"], grads["w_ffn1_out"] = dw_in, dw_out
        return after_swap(SCATTER_LAST, _run_comm("rs_sibling_last", swap(SCATTER_LAST)))

    dx, _, grads["norm_ffn1"], _, _, got, got_last = _ffn_bwd(
        "ffn1b", dh1, df1, x, g_ffn1, w["w_ffn1_in"], w["w_ffn1_out"], sv1, comm=scatter_mix, comm_last=scatter_last)
    scattered(SCATTER_MIX, got)
    scattered(SCATTER_LAST, got_last)
    return loss_p, dx, grads, part, from_chips


def _place():
    x, y, c = lax.axis_index("x"), lax.axis_index("y"), lax.axis_index("c")
    return x, y, c


def _other_chips(x, y):
    return [(1 - x, y, 2 * (1 - x) + y), (x, 1 - y, 2 * x + 1 - y), (1 - x, 1 - y, 2 * (1 - x) + 1 - y)]


def _half_rows(ref_3d, chip, h, rows):
    return ref_3d.at[chip, pl.ds(h * rows, rows), :]


def _gather_comm(ws):
    nw = len(ws)

    def parts(w_refs, out_refs, send_sems, recv_sems):
        x, y, c = _place()
        me = 2 * x + y
        chips = _other_chips(x, y)

        def copy(i, k, chip, h, to, src=None):
            half = ws[i].shape[0] // 2
            dst = _half_rows(out_refs[i], chip, h, half)
            return pltpu.make_async_remote_copy(
                src_ref=dst if src is None else src, dst_ref=dst,
                send_sem=send_sems.at[6 * i + k], recv_sem=recv_sems.at[6 * i + k], device_id=to, device_id_type=MESH)

        def first():
            out = []
            for i in range(nw):
                half = ws[i].shape[0] // 2
                out += [copy(i, j, me, c, (cx, cy, c), src=w_refs[i].at[pl.ds(c * half, half), :])
                        for j, (cx, cy, _) in enumerate(chips)]
            return out

        return copy, first, chips, c, (x, y, 1 - c)

    def start(*refs):
        _, first, _, _, _ = parts(*refs)
        for cp in first():
            cp.start()

    def finish(*refs):
        copy, first, chips, c, sibling = parts(*refs)
        passed = []
        for i in range(nw):
            for j, (cx, cy, ci) in enumerate(chips):
                copy(i, j, ci, c, (cx, cy, c)).wait_recv()
                fw = copy(i, 3 + j, ci, c, sibling)
                fw.start()
                passed.append(fw)
        for i in range(nw):
            for j, (_, _, ci) in enumerate(chips):
                copy(i, 3 + j, ci, 1 - c, sibling).wait_recv()
        for cp in first() + passed:
            cp.wait_send()

    return _Comm(list(ws), [jax.ShapeDtypeStruct((N_CHIPS,) + w.shape, w.dtype) for w in ws], 6 * nw, start, finish)


def _scatter_comm(ps):
    nw = len(ps)

    def copies(p_refs, out_refs, send_sems, recv_sems):
        x, y, c = _place()
        cps = []
        for i in range(nw):
            for j, (cx, cy, ci) in enumerate(_other_chips(x, y)):
                cps.append(pltpu.make_async_remote_copy(
                    src_ref=p_refs[i].at[ci], dst_ref=out_refs[i].at[j], send_sem=send_sems.at[3 * i + j],
                    recv_sem=recv_sems.at[3 * i + j], device_id=(cx, cy, c), device_id_type=MESH))
        return cps

    def start(*refs):
        for cp in copies(*refs):
            cp.start()

    def finish(*refs):
        for cp in copies(*refs):
            cp.wait()

    return _Comm(list(ps), [jax.ShapeDtypeStruct((3,) + p.shape[1:], p.dtype) for p in ps], 3 * nw, start, finish)


def _swap_comm(gs):
    nw = len(gs)

    def copies(g_refs, out_refs, send_sems, recv_sems):
        x, y, c = _place()
        cps = []
        for i in range(nw):
            half = gs[i].shape[1] // 2
            cps.append(pltpu.make_async_remote_copy(
                src_ref=g_refs[i].at[:, pl.ds((1 - c) * half, half), :], dst_ref=out_refs[i],
                send_sem=send_sems.at[i], recv_sem=recv_sems.at[i], device_id=(x, y, 1 - c), device_id_type=MESH))
        return cps

    def start(*refs):
        for cp in copies(*refs):
            cp.start()

    def finish(*refs):
        for cp in copies(*refs):
            cp.wait()

    return _Comm(list(gs), [jax.ShapeDtypeStruct((N_CHIPS, g.shape[1] // 2, g.shape[2]), g.dtype) for g in gs],
                 nw, start, finish)


def _run_comm(name, comm):
    nci, nco = len(comm.ins), len(comm.out_shapes)

    def body(*refs):
        cin, cout, send_sems, recv_sems = refs[:nci], refs[nci:nci + nco], refs[-2], refs[-1]
        comm.start(cin, cout, send_sems, recv_sems)
        comm.finish(cin, cout, send_sems, recv_sems)

    return pl.pallas_call(
        body, name=name, in_specs=[ANY] * nci, out_specs=[ANY] * nco, out_shape=list(comm.out_shapes),
        scratch_shapes=[pltpu.SemaphoreType.DMA((comm.n_sems,)), pltpu.SemaphoreType.DMA((comm.n_sems,))],
    )(*comm.ins)


def _join_halves(name, ss):
    nw = len(ss)

    def body(*refs):
        s_refs, out_refs, send_sems, recv_sems = refs[:nw], refs[nw:2 * nw], refs[2 * nw], refs[2 * nw + 1]
        x, y, c = _place()
        cps = [pltpu.make_async_remote_copy(
            src_ref=s_refs[i], dst_ref=out_refs[i], send_sem=send_sems.at[i], recv_sem=recv_sems.at[i],
            device_id=(x, y, 1 - c), device_id_type=MESH) for i in range(nw)]
        for cp in cps:
            cp.start()
        for cp in cps:
            cp.wait()

    return pl.pallas_call(
        body, name=name, in_specs=[ANY] * nw, out_specs=[ANY] * nw,
        out_shape=[jax.ShapeDtypeStruct(s.shape, s.dtype) for s in ss],
        scratch_shapes=[pltpu.SemaphoreType.DMA((nw,)), pltpu.SemaphoreType.DMA((nw,))],
    )(*ss)


def _allreduce_small(sp):
    def body(s_ref, out_ref, slots, send_sems, recv_sems):
        x, y, c = _place()
        me = 4 * x + 2 * y + c
        slots[me] = s_ref[...]
        cps = []
        for r in range(1, N_DEV):
            px, py, pc = x ^ (r >> 2), y ^ ((r >> 1) & 1), c ^ (r & 1)
            cps.append(pltpu.make_async_remote_copy(
                src_ref=s_ref, dst_ref=slots.at[me], send_sem=send_sems.at[r - 1], recv_sem=recv_sems.at[r - 1],
                device_id=(px, py, pc), device_id_type=MESH))
        for cp in cps:
            cp.start()
        for r in range(1, N_DEV):
            px, py, pc = x ^ (r >> 2), y ^ ((r >> 1) & 1), c ^ (r & 1)
            pltpu.make_async_remote_copy(
                src_ref=s_ref, dst_ref=slots.at[4 * px + 2 * py + pc], send_sem=send_sems.at[r - 1],
                recv_sem=recv_sems.at[r - 1], device_id=(px, py, pc), device_id_type=MESH).wait_recv()
        for cp in cps:
            cp.wait_send()
        acc = slots[0]
        for d in range(1, N_DEV):
            acc = acc + slots[d]
        out_ref[...] = acc

    return pl.pallas_call(
        body, name="allreduce_small",
        in_specs=[pl.BlockSpec(memory_space=pltpu.VMEM)], out_specs=pl.BlockSpec(memory_space=pltpu.VMEM),
        out_shape=jax.ShapeDtypeStruct(sp.shape, F32),
        scratch_shapes=[pltpu.VMEM((N_DEV,) + sp.shape, F32), pltpu.SemaphoreType.DMA((N_DEV - 1,)),
                        pltpu.SemaphoreType.DMA((N_DEV - 1,))],
    )(sp)


def _scalar(v):
    return jnp.reshape(v, (1,)).astype(jnp.int32)


def _row_tile(h, dtype_mult=16):
    return _pick(h, 256, dtype_mult)


def _add_sibling(name, g32, from_sib, c):
    _, r, n = g32.shape
    h = r // 2
    th = _row_tile(h)
    nt = h // th

    def body(c_ref, g_ref, s_ref, o32_ref, o16_ref):
        s = g_ref[...] + s_ref[...].astype(F32)
        o32_ref[...] = s
        o16_ref[...] = s.astype(BF16)

    blk = (None, th, n)
    return pl.pallas_call(
        body, name=name,
        grid_spec=pltpu.PrefetchScalarGridSpec(
            num_scalar_prefetch=1, grid=(N_CHIPS, nt),
            in_specs=[pl.BlockSpec(blk, lambda k, t, c_ref: (k, c_ref[0] * nt + t, 0)),
                      pl.BlockSpec(blk, lambda k, t, c_ref: (k, t, 0))],
            out_specs=[pl.BlockSpec(blk, lambda k, t, c_ref: (k, t, 0))] * 2),
        out_shape=[jax.ShapeDtypeStruct((N_CHIPS, h, n), F32), jax.ShapeDtypeStruct((N_CHIPS, h, n), BF16)],
        compiler_params=_params("arbitrary", "arbitrary"))(_scalar(c), g32, from_sib)


def _add_chips(name, p32, from_chips, me_chip):
    _, h, n = p32.shape
    th = _row_tile(h)

    def body(m_ref, p_ref, a_ref, b_ref, c_ref, o_ref):
        o_ref[...] = p_ref[...] + a_ref[...].astype(F32) + b_ref[...].astype(F32) + c_ref[...].astype(F32)

    blk = (None, th, n)
    return pl.pallas_call(
        body, name=name,
        grid_spec=pltpu.PrefetchScalarGridSpec(
            num_scalar_prefetch=1, grid=(h // th,),
            in_specs=[pl.BlockSpec(blk, lambda t, m_ref: (m_ref[0], t, 0))]
            + [pl.BlockSpec(blk, lambda t, m_ref, j=j: (j, t, 0)) for j in range(3)],
            out_specs=pl.BlockSpec((th, n), lambda t, m_ref: (t, 0))),
        out_shape=jax.ShapeDtypeStruct((h, n), F32),
        compiler_params=_params("arbitrary"))(_scalar(me_chip), p32, from_chips, from_chips, from_chips)


def _adamw_vals(w, g, m, v):
    m = ADAM_B1 * m + (1.0 - ADAM_B1) * g
    v = ADAM_B2 * v + (1.0 - ADAM_B2) * (g * g)
    m_hat = m / (1.0 - ADAM_B1 ** ADAM_STEP)
    v_hat = v / (1.0 - ADAM_B2 ** ADAM_STEP)
    delta = -ADAM_LR * (m_hat / (jnp.sqrt(v_hat) + ADAM_EPS) + ADAM_WD * w)
    return delta, m, v


def _adamw_halves(name, w, m, v, g_mine, g_sib, c):
    r, n = w.shape
    h = r // 2
    th = _row_tile(h, 8)
    nt = h // th

    def body(c_ref, w_ref, m_ref, v_ref, a_ref, b_ref, g_ref, d_ref, nm_ref, nv_ref):
        mine = (pl.program_id(0) // nt) == c_ref[0]
        g = jnp.where(mine, a_ref[...], b_ref[...])
        d, nm, nv = _adamw_vals(w_ref[...], g, m_ref[...], v_ref[...])
        g_ref[...] = g
        d_ref[...] = d
        nm_ref[...] = nm
        nv_ref[...] = nv

    full = pl.BlockSpec((th, n), lambda t, c_ref: (t, 0))
    own_half = pl.BlockSpec((th, n), lambda t, c_ref: (jnp.where(t // nt == c_ref[0], t % nt, 0), 0))
    sib_half = pl.BlockSpec((th, n), lambda t, c_ref: (jnp.where(t // nt == c_ref[0], 0, t % nt), 0))
    return pl.pallas_call(
        body, name=name,
        grid_spec=pltpu.PrefetchScalarGridSpec(
            num_scalar_prefetch=1, grid=(2 * nt,), in_specs=[full, full, full, own_half, sib_half],
            out_specs=[full] * 4),
        out_shape=[jax.ShapeDtypeStruct((r, n), F32)] * 4,
        compiler_params=_params("arbitrary"))(_scalar(c), w, m, v, g_mine, g_sib)


def _adamw(name, w, g, m, v):
    R, W = w.shape

    def fn(accs, tv, cv):
        return list(_adamw_vals(*tv))

    return _tile_call(name, fn, R, W, _pick(R, 256), W, tiles=[(w, 0), (g, 0), (m, 0), (v, 0)], outs=[F32] * 3)


SMALL_LAYOUT = (("rel_bias", 2, 256), ("lb_param", 8, 1024), ("norm_ffn1", 8, 1024), ("norm_mix", 8, 1024),
                ("attn_sinks", 1, 8), ("rec_norm", 1, 128), ("norm_ffn2", 8, 1024), ("norm_ple", 8, 1024),
                ("norm_final", 8, 1024), ("loss", 8, 1024))


def _pack_small(vals):
    rows = []
    for name, nrows, n in SMALL_LAYOUT:
        flat = vals[name].reshape(-1)
        flat = jnp.pad(flat, (0, nrows * 128 - n))
        rows.append(flat.reshape(nrows, 128))
    packed = jnp.concatenate(rows, axis=0)
    return jnp.pad(packed, ((0, SMALL_ROWS - packed.shape[0]), (0, 0)))


def _unpack_small(packed, shapes):
    out, r = {}, 0
    for name, nrows, n in SMALL_LAYOUT:
        out[name] = packed[r:r + nrows].reshape(-1)[:n].reshape(shapes[name])
        r += nrows
    return out


def _natural(name, s):
    if name in COL_SHARDED:
        return s.transpose(1, 0, 2).reshape(s.shape[1], -1)
    return s.reshape(-1, s.shape[2])


def kernel(x, p, rel_bias, lb_param, norm_ffn1, w_ffn1_in, w_ffn1_out, norm_mix, w_in, attn_sinks, rec_norm, w_att_proj, w_rec_proj, w_out, norm_ffn2, w_ffn2_in, w_ffn2_out, norm_ple, w_ple_gate, w_ple_proj, norm_final, loss_target, m_rel_bias, m_lb_param, m_norm_ffn1, m_w_ffn1_in, m_w_ffn1_out, m_norm_mix, m_w_in, m_attn_sinks, m_rec_norm, m_w_att_proj, m_w_rec_proj, m_w_out, m_norm_ffn2, m_w_ffn2_in, m_w_ffn2_out, m_norm_ple, m_w_ple_gate, m_w_ple_proj, m_norm_final, v_rel_bias, v_lb_param, v_norm_ffn1, v_w_ffn1_in, v_w_ffn1_out, v_norm_mix, v_w_in, v_attn_sinks, v_rec_norm, v_w_att_proj, v_w_rec_proj, v_w_out, v_norm_ffn2, v_w_ffn2_in, v_w_ffn2_out, v_norm_ple, v_w_ple_gate, v_w_ple_proj, v_norm_final):
    args = dict(locals())
    wsh = {n: args[n] for n in WEIGHTS}
    B, S = x.shape[0], x.shape[1]
    T = B * S
    cx, cy, cc = _place()
    me_chip = 2 * cx + cy

    mine16 = {n: wsh[n][0].astype(BF16) for n in BIG}
    loss_p, dx, grads, part, from_chips = _local_step(
        x.reshape(T, D_MODEL), p.reshape(T, PLE_DIM), loss_target.reshape(T, D_MODEL),
        {n: wsh[n] for n in SMALL}, mine16, cc, me_chip, B, S)

    s_mine = [_add_chips("rs_add_chips_" + n, part[n][0], from_chips[n], me_chip) for n in BIG]
    s_sib = _join_halves("rs_join", s_mine)

    small_vals = {
        "rel_bias": grads["rel_bias"].T,
        "lb_param": jnp.concatenate([_colsum("dlb_sum", grads["lb_param"]),
                                     -_colsum("dlb_sum2", grads["lb_param"])], axis=0) / 8.0,
        "attn_sinks": grads["attn_sinks"][:, 0],
        "rec_norm": _colsum("drn_sum", grads["rec_norm"]).reshape(REC_HEADS, REC_DIM).sum(axis=0),
        "loss": _colsum("loss_sum", loss_p),
    }
    for n in ("norm_ffn1", "norm_mix", "norm_ffn2", "norm_ple", "norm_final"):
        small_vals[n] = _colsum(n + "_sum", grads[n])
    red = _allreduce_small(_pack_small(small_vals))
    small_shapes = {n: wsh[n].shape for n in SMALL}
    small_shapes["loss"] = (D_MODEL,)
    small = _unpack_small(red, small_shapes)
    loss = 0.5 * jnp.sum(small["loss"]) / D_MODEL

    out_g, out_d, out_m, out_v = {}, {}, {}, {}
    for n, gm, gs in zip(BIG, s_mine, s_sib):
        res = _adamw_halves("adamw_" + n, wsh[n][0], args["m_" + n][0], args["v_" + n][0], gm, gs, cc)
        out_g[n], out_d[n], out_m[n], out_v[n] = (t[None] for t in res)
    sw = _pack_small({**{n: wsh[n] for n in SMALL}, "loss": jnp.zeros((D_MODEL,), F32)})
    sm = _pack_small({**{n: args["m_" + n] for n in SMALL}, "loss": jnp.zeros((D_MODEL,), F32)})
    sv = _pack_small({**{n: args["v_" + n] for n in SMALL}, "loss": jnp.ones((D_MODEL,), F32)})
    sd, snm, snv = _adamw("adamw_small", sw, red, sm, sv)
    ud, um, uv = (_unpack_small(t, small_shapes) for t in (sd, snm, snv))
    for n in SMALL:
        out_g[n], out_d[n], out_m[n], out_v[n] = small[n], ud[n], um[n], uv[n]

    return (loss, dx.reshape(B, S, D_MODEL), *[out_g[n] for n in WEIGHTS], *[out_d[n] for n in WEIGHTS],
            *[out_m[n] for n in WEIGHTS], *[out_v[n] for n in WEIGHTS])
```
